```python
import math
import jax, jax.numpy as jnp
from jax import lax
import numpy as np

D_MODEL = 1024
BATCH = 16
SEQ = 2048
DEPTH = 1

MIX_WIDTH = D_MODEL
SB_WIDTH = D_MODEL // 2
SB_HEADS = 8
SB_HEAD_DIM = SB_WIDTH // SB_HEADS
POOL_WIDTH = MIX_WIDTH - SB_WIDTH
POOL_WINDOWS = (2, 4, 8, 16)
POOL_GROUPS = len(POOL_WINDOWS)
POOL_GROUP_DIM = POOL_WIDTH // POOL_GROUPS
IN_WIDTH = 3 * SB_WIDTH + POOL_WIDTH
D_FF = -(-8 * D_MODEL // (3 * 256)) * 256
Q_BLOCK = 128
N_MOD = 6
EPS = 1e-6

kernel_name = "hybrid_stickbreak_pool_block"


def rmsnorm(x, g):
    xf = x.astype(jnp.float32)
    y = xf * lax.rsqrt(jnp.mean(xf * xf, axis=-1, keepdims=True) + EPS)
    return (y * g.astype(jnp.float32)).astype(x.dtype)


def stick_breaking_attention(q, k, v):
    S = q.shape[2]
    inv_sqrt = 1.0 / math.sqrt(q.shape[-1])
    outs = []
    for i in range(S // Q_BLOCK):
        L = (i + 1) * Q_BLOCK
        qb = q[:, :, i * Q_BLOCK:L].astype(jnp.float32)
        kb = k[:, :, :L].astype(jnp.float32)
        vb = v[:, :, :L].astype(jnp.float32)
        z = jnp.einsum('bhqd,bhkd->bhqk', qb, kb) * inv_sqrt
        t_idx = i * Q_BLOCK + jnp.arange(Q_BLOCK)[:, None]
        s_idx = jnp.arange(L)[None, :]
        mask = s_idx < t_idx
        log1m = jnp.where(mask, -jax.nn.softplus(z), 0.0)
        after = lax.cumsum(log1m, axis=3, reverse=True) - log1m
        logw = jax.nn.log_sigmoid(z) + after
        w = jnp.where(mask, jnp.exp(jnp.where(mask, logw, 0.0)), 0.0)
        outs.append(jnp.einsum('bhqk,bhkd->bhqd', w, vb))
    return jnp.concatenate(outs, axis=2).astype(v.dtype)


def pooling_mixer(u, w_pool, pool_scale):
    B, S, P = u.shape
    uf = u.astype(jnp.float32)
    cs = jnp.concatenate([jnp.zeros((B, 1, P), jnp.float32), jnp.cumsum(uf, axis=1)], axis=1)
    t = jnp.arange(S)
    parts = []
    for g, win in enumerate(POOL_WINDOWS):
        sl = slice(g * POOL_GROUP_DIM, (g + 1) * POOL_GROUP_DIM)
        lo = jnp.maximum(t + 1 - win, 0)
        cnt = (t + 1 - lo).astype(jnp.float32)
        csg = cs[..., sl]
        mean = (csg[:, 1:] - csg[:, lo]) / cnt[None, :, None]
        parts.append(mean - uf[..., sl])
    pooled = jnp.stack(parts, axis=2)
    y = jnp.einsum('bsgc,gcd->bsgd', pooled, w_pool.astype(jnp.float32)).reshape(B, S, P)
    return (y * pool_scale.astype(jnp.float32)).astype(u.dtype)


def _fwd_setup_inputs(seed: int = 0) -> dict:
    key = jax.random.key(seed)
    ks = jax.random.split(key, 16)
    f32 = jnp.float32

    def nrm(k, shape, fan_in):
        return jax.random.normal(k, shape, f32) * fan_in ** -0.5

    def gain(k):
        return 1.0 + 0.05 * jax.random.normal(k, (DEPTH, D_MODEL), f32)

    return {
        "x": jax.random.normal(ks[0], (BATCH, SEQ, D_MODEL), f32),
        "c": jax.random.normal(ks[1], (BATCH, D_MODEL), f32),
        "w_cond": nrm(ks[2], (DEPTH, D_MODEL, N_MOD * D_MODEL), D_MODEL),
        "b_cond": 0.01 * jax.random.normal(ks[3], (DEPTH, N_MOD * D_MODEL), f32),
        "g_mix_pre": gain(ks[4]),
        "g_mix_post": gain(ks[5]),
        "w_in": nrm(ks[6], (DEPTH, D_MODEL, IN_WIDTH), D_MODEL),
        "w_pool": nrm(ks[7], (DEPTH, POOL_GROUPS, POOL_GROUP_DIM, POOL_GROUP_DIM), POOL_GROUP_DIM),
        "pool_scale": 1.0 + 0.1 * jax.random.normal(ks[8], (DEPTH, POOL_WIDTH), f32),
        "w_out": nrm(ks[9], (DEPTH, MIX_WIDTH, D_MODEL), MIX_WIDTH),
        "g_ffn_pre": gain(ks[10]),
        "g_ffn_post": gain(ks[11]),
        "w_gate": nrm(ks[12], (DEPTH, D_MODEL, D_FF), D_MODEL),
        "w_up": nrm(ks[13], (DEPTH, D_MODEL, D_FF), D_MODEL),
        "w_down": nrm(ks[14], (DEPTH, D_FF, D_MODEL), D_FF),
    }


def _fwd_reference(x, c, w_cond, b_cond, g_mix_pre, g_mix_post, w_in, w_pool, pool_scale,
              w_out, g_ffn_pre, g_ffn_post, w_gate, w_up, w_down):
    B, S, D = x.shape
    for l in range(DEPTH):
        mod = jax.nn.silu(c) @ w_cond[l] + b_cond[l]
        shift_m, scale_m, gate_m, shift_f, scale_f, gate_f = [
            m[:, None, :] for m in jnp.split(mod, N_MOD, axis=-1)]

        h = rmsnorm(x, g_mix_pre[l]) * (1.0 + scale_m) + shift_m
        proj = h @ w_in[l]
        q, k, v, u = jnp.split(proj, [SB_WIDTH, 2 * SB_WIDTH, 3 * SB_WIDTH], axis=-1)
        to_heads = lambda a: a.reshape(B, S, SB_HEADS, SB_HEAD_DIM).transpose(0, 2, 1, 3)
        attn = stick_breaking_attention(to_heads(q), to_heads(k), to_heads(v))
        attn = attn.transpose(0, 2, 1, 3).reshape(B, S, SB_WIDTH)
        pool = pooling_mixer(u, w_pool[l], pool_scale[l])
        mix = jnp.concatenate([attn, pool], axis=-1) @ w_out[l]
        x = x + gate_m * rmsnorm(mix, g_mix_post[l])

        h = rmsnorm(x, g_ffn_pre[l]) * (1.0 + scale_f) + shift_f
        f = (jax.nn.silu(h @ w_gate[l]) * (h @ w_up[l])) @ w_down[l]
        x = x + gate_f * rmsnorm(f, g_ffn_post[l])
    return x


import jax as _jax
import jax.numpy as _jnp

TWIN_FORMAT = 'train_step'
FWD_PARAMS = ['x', 'c', 'w_cond', 'b_cond', 'g_mix_pre', 'g_mix_post', 'w_in', 'w_pool', 'pool_scale', 'w_out', 'g_ffn_pre', 'g_ffn_post', 'w_gate', 'w_up', 'w_down']
TWIN_WEIGHTS = ['w_cond', 'b_cond', 'g_mix_pre', 'g_mix_post', 'w_in', 'w_pool', 'pool_scale', 'w_out', 'g_ffn_pre', 'g_ffn_post', 'w_gate', 'w_up', 'w_down']
TWIN_DIFF_INPUT = 'x'
TWIN_INPUTS = ['x', 'c', 'w_cond', 'b_cond', 'g_mix_pre', 'g_mix_post', 'w_in', 'w_pool', 'pool_scale', 'w_out', 'g_ffn_pre', 'g_ffn_post', 'w_gate', 'w_up', 'w_down', 'loss_target', 'm_w_cond', 'm_b_cond', 'm_g_mix_pre', 'm_g_mix_post', 'm_w_in', 'm_w_pool', 'm_pool_scale', 'm_w_out', 'm_g_ffn_pre', 'm_g_ffn_post', 'm_w_gate', 'm_w_up', 'm_w_down', 'v_w_cond', 'v_b_cond', 'v_g_mix_pre', 'v_g_mix_post', 'v_w_in', 'v_w_pool', 'v_pool_scale', 'v_w_out', 'v_g_ffn_pre', 'v_g_ffn_post', 'v_w_gate', 'v_w_up', 'v_w_down']
TWIN_OUTPUTS = ['loss', 'grad_x', 'grad_w_cond', 'grad_b_cond', 'grad_g_mix_pre', 'grad_g_mix_post', 'grad_w_in', 'grad_w_pool', 'grad_pool_scale', 'grad_w_out', 'grad_g_ffn_pre', 'grad_g_ffn_post', 'grad_w_gate', 'grad_w_up', 'grad_w_down', 'delta_w_cond', 'delta_b_cond', 'delta_g_mix_pre', 'delta_g_mix_post', 'delta_w_in', 'delta_w_pool', 'delta_pool_scale', 'delta_w_out', 'delta_g_ffn_pre', 'delta_g_ffn_post', 'delta_w_gate', 'delta_w_up', 'delta_w_down', 'new_m_w_cond', 'new_m_b_cond', 'new_m_g_mix_pre', 'new_m_g_mix_post', 'new_m_w_in', 'new_m_w_pool', 'new_m_pool_scale', 'new_m_w_out', 'new_m_g_ffn_pre', 'new_m_g_ffn_post', 'new_m_w_gate', 'new_m_w_up', 'new_m_w_down', 'new_v_w_cond', 'new_v_b_cond', 'new_v_g_mix_pre', 'new_v_g_mix_post', 'new_v_w_in', 'new_v_w_pool', 'new_v_pool_scale', 'new_v_w_out', 'new_v_g_ffn_pre', 'new_v_g_ffn_post', 'new_v_w_gate', 'new_v_w_up', 'new_v_w_down']
TWIN_LEAF_KINDS = {'loss': 'loss', 'grad_x': 'grad_x', 'grad_w_cond': 'grad_w', 'grad_b_cond': 'grad_w', 'grad_g_mix_pre': 'grad_w', 'grad_g_mix_post': 'grad_w', 'grad_w_in': 'grad_w', 'grad_w_pool': 'grad_w', 'grad_pool_scale': 'grad_w', 'grad_w_out': 'grad_w', 'grad_g_ffn_pre': 'grad_w', 'grad_g_ffn_post': 'grad_w', 'grad_w_gate': 'grad_w', 'grad_w_up': 'grad_w', 'grad_w_down': 'grad_w', 'delta_w_cond': 'delta_w', 'delta_b_cond': 'delta_w', 'delta_g_mix_pre': 'delta_w', 'delta_g_mix_post': 'delta_w', 'delta_w_in': 'delta_w', 'delta_w_pool': 'delta_w', 'delta_pool_scale': 'delta_w', 'delta_w_out': 'delta_w', 'delta_g_ffn_pre': 'delta_w', 'delta_g_ffn_post': 'delta_w', 'delta_w_gate': 'delta_w', 'delta_w_up': 'delta_w', 'delta_w_down': 'delta_w', 'new_m_w_cond': 'new_m', 'new_m_b_cond': 'new_m', 'new_m_g_mix_pre': 'new_m', 'new_m_g_mix_post': 'new_m', 'new_m_w_in': 'new_m', 'new_m_w_pool': 'new_m', 'new_m_pool_scale': 'new_m', 'new_m_w_out': 'new_m', 'new_m_g_ffn_pre': 'new_m', 'new_m_g_ffn_post': 'new_m', 'new_m_w_gate': 'new_m', 'new_m_w_up': 'new_m', 'new_m_w_down': 'new_m', 'new_v_w_cond': 'new_v', 'new_v_b_cond': 'new_v', 'new_v_g_mix_pre': 'new_v', 'new_v_g_mix_post': 'new_v', 'new_v_w_in': 'new_v', 'new_v_w_pool': 'new_v', 'new_v_pool_scale': 'new_v', 'new_v_w_out': 'new_v', 'new_v_g_ffn_pre': 'new_v', 'new_v_g_ffn_post': 'new_v', 'new_v_w_gate': 'new_v', 'new_v_w_up': 'new_v', 'new_v_w_down': 'new_v'}


def _forward(args):
    return _fwd_reference(*[args[k] for k in FWD_PARAMS])


def _output_shape():
    out = _jax.eval_shape(lambda: _forward(_fwd_setup_inputs(0)))
    return out.shape, out.dtype

N_MICROBATCH = 1
ADAM_LR = 0.001
ADAM_B1 = 0.9
ADAM_B2 = 0.999
ADAM_EPS = 1e-08
ADAM_WD = 0.01
ADAM_STEP = 10
PER_EXAMPLE_BATCH_AXIS = {'x': 0, 'c': 0, 'loss_target': 0}
SHARED_INPUTS = []
_WEIGHT_DTYPES = {'w_cond': _jnp.float32, 'b_cond': _jnp.float32, 'g_mix_pre': _jnp.float32, 'g_mix_post': _jnp.float32, 'w_in': _jnp.float32, 'w_pool': _jnp.float32, 'pool_scale': _jnp.float32, 'w_out': _jnp.float32, 'g_ffn_pre': _jnp.float32, 'g_ffn_post': _jnp.float32, 'w_gate': _jnp.float32, 'w_up': _jnp.float32, 'w_down': _jnp.float32}
MOMENT_SCALE = {'w_cond': 3.777439e+00, 'b_cond': 7.078180e+00, 'g_mix_pre': 3.235633e-01, 'g_mix_post': 1.547640e+01, 'w_in': 1.110121e+00, 'w_pool': 4.001338e-01, 'pool_scale': 4.827422e-01, 'w_out': 1.620795e+00, 'g_ffn_pre': 5.341113e-01, 'g_ffn_post': 1.539685e+01, 'w_gate': 3.843960e-01, 'w_up': 5.430322e-01, 'w_down': 9.299515e-01}


def _to_microbatches(a, axis):
    t = _jnp.moveaxis(a, axis, 0)
    t = t.reshape((N_MICROBATCH, t.shape[0] // N_MICROBATCH) + t.shape[1:])
    return _jnp.moveaxis(t, 1, axis + 1)


def setup_inputs(seed: int = 0) -> dict:
    inp = _fwd_setup_inputs(seed)
    key = _jax.random.fold_in(_jax.random.key(seed), 7919)
    shape, _ = _output_shape()
    out = dict(inp)
    out["loss_target"] = _jax.random.normal(_jax.random.fold_in(key, 0), shape, _jnp.float32)
    for i, name in enumerate(TWIN_WEIGHTS):
        w = inp[name].astype(_jnp.float32)
        if MOMENT_SCALE is None:
            s = _jnp.sqrt(_jnp.mean(_jnp.square(w)) + 1e-30)
        else:
            s = MOMENT_SCALE[name]
        km, kv = _jax.random.split(_jax.random.fold_in(key, i + 1))
        out[name] = w
        out["m_" + name] = s * _jax.random.normal(km, w.shape, _jnp.float32)
        out["v_" + name] = (s * s) * _jax.random.uniform(kv, w.shape, _jnp.float32, 0.5, 1.5)
    if N_MICROBATCH > 1:
        for name, axis in PER_EXAMPLE_BATCH_AXIS.items():
            out[name] = _to_microbatches(out[name], axis)
    return {'x': out['x'], 'c': out['c'], 'w_cond': out['w_cond'], 'b_cond': out['b_cond'], 'g_mix_pre': out['g_mix_pre'], 'g_mix_post': out['g_mix_post'], 'w_in': out['w_in'], 'w_pool': out['w_pool'], 'pool_scale': out['pool_scale'], 'w_out': out['w_out'], 'g_ffn_pre': out['g_ffn_pre'], 'g_ffn_post': out['g_ffn_post'], 'w_gate': out['w_gate'], 'w_up': out['w_up'], 'w_down': out['w_down'], 'loss_target': out['loss_target'], 'm_w_cond': out['m_w_cond'], 'm_b_cond': out['m_b_cond'], 'm_g_mix_pre': out['m_g_mix_pre'], 'm_g_mix_post': out['m_g_mix_post'], 'm_w_in': out['m_w_in'], 'm_w_pool': out['m_w_pool'], 'm_pool_scale': out['m_pool_scale'], 'm_w_out': out['m_w_out'], 'm_g_ffn_pre': out['m_g_ffn_pre'], 'm_g_ffn_post': out['m_g_ffn_post'], 'm_w_gate': out['m_w_gate'], 'm_w_up': out['m_w_up'], 'm_w_down': out['m_w_down'], 'v_w_cond': out['v_w_cond'], 'v_b_cond': out['v_b_cond'], 'v_g_mix_pre': out['v_g_mix_pre'], 'v_g_mix_post': out['v_g_mix_post'], 'v_w_in': out['v_w_in'], 'v_w_pool': out['v_w_pool'], 'v_pool_scale': out['v_pool_scale'], 'v_w_out': out['v_w_out'], 'v_g_ffn_pre': out['v_g_ffn_pre'], 'v_g_ffn_post': out['v_g_ffn_post'], 'v_w_gate': out['v_w_gate'], 'v_w_up': out['v_w_up'], 'v_w_down': out['v_w_down']}


def _loss(weights, diff, rest, loss_target):
    with _jax.named_scope("forward"):
        args = {**rest, TWIN_DIFF_INPUT: diff, **{k: w.astype(_WEIGHT_DTYPES[k]) for k, w in weights.items()}}
        y = _forward(args)
    with _jax.named_scope("loss_head"):
        err = _jnp.square(y.astype(_jnp.float32) - loss_target)
        return 0.5 * _jnp.sum(_jnp.mean(err, axis=-1)) if err.ndim else 0.5 * err


def _adamw(w, g, m, v):
    m = ADAM_B1 * m + (1.0 - ADAM_B1) * g
    v = ADAM_B2 * v + (1.0 - ADAM_B2) * _jnp.square(g)
    m_hat = m / (1.0 - ADAM_B1 ** ADAM_STEP)
    v_hat = v / (1.0 - ADAM_B2 ** ADAM_STEP)
    delta = -ADAM_LR * (m_hat / (_jnp.sqrt(v_hat) + ADAM_EPS) + ADAM_WD * w)
    return delta, m, v


def reference(x, c, w_cond, b_cond, g_mix_pre, g_mix_post, w_in, w_pool, pool_scale, w_out, g_ffn_pre, g_ffn_post, w_gate, w_up, w_down, loss_target, m_w_cond, m_b_cond, m_g_mix_pre, m_g_mix_post, m_w_in, m_w_pool, m_pool_scale, m_w_out, m_g_ffn_pre, m_g_ffn_post, m_w_gate, m_w_up, m_w_down, v_w_cond, v_b_cond, v_g_mix_pre, v_g_mix_post, v_w_in, v_w_pool, v_pool_scale, v_w_out, v_g_ffn_pre, v_g_ffn_post, v_w_gate, v_w_up, v_w_down):
    given = dict(x=x, c=c, w_cond=w_cond, b_cond=b_cond, g_mix_pre=g_mix_pre, g_mix_post=g_mix_post, w_in=w_in, w_pool=w_pool, pool_scale=pool_scale, w_out=w_out, g_ffn_pre=g_ffn_pre, g_ffn_post=g_ffn_post, w_gate=w_gate, w_up=w_up, w_down=w_down, loss_target=loss_target, m_w_cond=m_w_cond, m_b_cond=m_b_cond, m_g_mix_pre=m_g_mix_pre, m_g_mix_post=m_g_mix_post, m_w_in=m_w_in, m_w_pool=m_w_pool, m_pool_scale=m_pool_scale, m_w_out=m_w_out, m_g_ffn_pre=m_g_ffn_pre, m_g_ffn_post=m_g_ffn_post, m_w_gate=m_w_gate, m_w_up=m_w_up, m_w_down=m_w_down, v_w_cond=v_w_cond, v_b_cond=v_b_cond, v_g_mix_pre=v_g_mix_pre, v_g_mix_post=v_g_mix_post, v_w_in=v_w_in, v_w_pool=v_w_pool, v_pool_scale=v_pool_scale, v_w_out=v_w_out, v_g_ffn_pre=v_g_ffn_pre, v_g_ffn_post=v_g_ffn_post, v_w_gate=v_w_gate, v_w_up=v_w_up, v_w_down=v_w_down)
    weights = {n: given[n] for n in TWIN_WEIGHTS}
    shared = {n: given[n] for n in SHARED_INPUTS}
    per_example = {n: given[n] for n in ['x', 'c']}
    grad_fn = _jax.value_and_grad(_loss, argnums=(0, 1))

    def one_microbatch(ex, loss_target):
        ex = dict(ex)
        diff = ex.pop(TWIN_DIFF_INPUT)
        return grad_fn(weights, diff, {**shared, **ex}, loss_target)

    if N_MICROBATCH == 1:
        loss, (grad_w, grad_x) = one_microbatch(per_example, given["loss_target"])
    else:
        def body(carry, xs):
            loss_sum, grad_sum = carry
            l_k, (gw_k, gx_k) = one_microbatch(xs[0], xs[1])
            with _jax.named_scope("update"):
                return (loss_sum + l_k, _jax.tree.map(_jnp.add, grad_sum, gw_k)), gx_k

        init = (_jnp.zeros((), _jnp.float32), _jax.tree.map(_jnp.zeros_like, weights))
        (loss, grad_w), grad_x = _jax.lax.scan(body, init, (per_example, given["loss_target"]))
    with _jax.named_scope("update"):
        delta_w, new_m, new_v = {}, {}, {}
        for n in TWIN_WEIGHTS:
            delta_w[n], new_m[n], new_v[n] = _adamw(weights[n], grad_w[n], given["m_" + n], given["v_" + n])
    return (loss, grad_x, *[grad_w[n] for n in TWIN_WEIGHTS], *[delta_w[n] for n in TWIN_WEIGHTS],
            *[new_m[n] for n in TWIN_WEIGHTS], *[new_v[n] for n in TWIN_WEIGHTS])
```

```python
import functools

import jax
import jax.numpy as jnp
from jax import lax
from jax.experimental import pallas as pl
from jax.experimental.pallas import tpu as pltpu

F32 = jnp.float32
BF16 = jnp.bfloat16
MESH = pl.DeviceIdType.MESH

EPS = 1e-6
HEAD_DIM = 64
HEADS_PER_BLOCK = 2
LANES = 128
QK_SCALE = 0.125
POOL_WINDOWS = (2, 4, 8, 16)
POOL_GROUP = 128
HALO = 16
N_MOD = 6
MOD_ROWS = 8
N_CHIPS = 4
N_DEV = 8
VMEM_LIMIT = 56 * 1024 * 1024

ADAM_LR = 0.001
ADAM_B1 = 0.9
ADAM_B2 = 0.999
ADAM_EPS = 1e-08
ADAM_WD = 0.01
ADAM_STEP = 10

TOKEN_TILE = 512
ATTN_TILE = 256


def _dot(a, b):
    return jnp.dot(a, b, preferred_element_type=F32)


def _dot_nt(a, b):
    return lax.dot_general(a, b, (((1,), (1,)), ((), ())), preferred_element_type=F32)


def _dot_tn(a, b):
    return lax.dot_general(a, b, (((0,), (0,)), ((), ())), preferred_element_type=F32)


def _split(v):
    hi = v.astype(BF16)
    lo = (v - hi.astype(F32)).astype(BF16)
    return hi, lo


def _rms(v):
    return lax.rsqrt(jnp.mean(v * v, axis=-1, keepdims=True) + EPS)


def _norm_bwd(dn, n, r):
    return r * (dn - n * jnp.mean(dn * n, axis=-1, keepdims=True))


def _sigmoid(v):
    return 1.0 / (1.0 + jnp.exp(-v))


def _colsum(v):
    return jnp.sum(v, axis=0, keepdims=True)


def _params(sem=None):
    return pltpu.CompilerParams(dimension_semantics=sem, vmem_limit_bytes=VMEM_LIMIT)


def _position():
    return lax.axis_index("x"), lax.axis_index("y"), lax.axis_index("c")


def _prenorm_proj(x, mod, g_pre, w_in, seq, tm):
    t_all, d = x.shape
    nt = seq // tm
    p = w_in.shape[2]

    def body(x_ref, mod_ref, g_ref, w_ref, h_ref, q_ref, k_ref, v_ref, u_ref):
        xf = x_ref[...]
        n = xf * _rms(xf)
        h = (n * g_ref[...]) * (1.0 + mod_ref[0, 1:2, :]) + mod_ref[0, 0:1, :]
        hb = h.astype(BF16)
        h_ref[...] = hb
        q_ref[...] = (_dot(hb, w_ref[0]) * QK_SCALE).astype(BF16)
        k_ref[...] = _dot(hb, w_ref[1]).astype(BF16)
        v_ref[...] = _dot(hb, w_ref[2]).astype(BF16)
        u_ref[...] = _dot(hb, w_ref[3])

    tok = lambda i: (i, 0)
    return pl.pallas_call(
        body, name="prenorm_proj", grid=(t_all // tm,),
        in_specs=[pl.BlockSpec((tm, d), tok),
                  pl.BlockSpec((1, MOD_ROWS, d), lambda i: (i // nt, 0, 0)),
                  pl.BlockSpec((1, d), lambda i: (0, 0)),
                  pl.BlockSpec((N_CHIPS, d, p), lambda i: (0, 0, 0))],
        out_specs=[pl.BlockSpec((tm, d), tok)] + [pl.BlockSpec((tm, p), tok)] * 4,
        out_shape=[jax.ShapeDtypeStruct((t_all, d), BF16)] + [jax.ShapeDtypeStruct((t_all, p), BF16)] * 3
        + [jax.ShapeDtypeStruct((t_all, p), F32)],
        compiler_params=_params(("arbitrary",)),
    )(x, mod, g_pre, w_in)


def _softplus(z):
    return jnp.maximum(z, 0.0) + jnp.log(1.0 + jnp.exp(-jnp.abs(z)))


def _attn_fwd(q, k, v, seq, tq):
    t_all, w = q.shape
    nb, nq, tk = t_all // seq, seq // tq, tq

    def body(q_ref, k_ref, v_ref, o_ref, l_ref):
        i = pl.program_id(2)
        lane = lax.broadcasted_iota(jnp.int32, (1, LANES), 1)
        row = lax.broadcasted_iota(jnp.int32, (tq, tk), 0)
        col = lax.broadcasted_iota(jnp.int32, (tq, tk), 1)
        after_mat = (row > col).astype(BF16)
        causal = col < row
        q2 = q_ref[...]
        accs, runs = [], []
        for h in range(HEADS_PER_BLOCK):
            head = (lane >= HEAD_DIM) if h else (lane < HEAD_DIM)
            qh = jnp.where(head, q2, jnp.zeros_like(q2))

            def step(j, carry, diagonal, qh=qh):
                acc, run = carry
                off = pl.multiple_of(j * tk, tk)
                kj = k_ref[pl.ds(off, tk), :]
                vj = v_ref[pl.ds(off, tk), :]
                z = _dot_nt(qh, kj)
                sp = _softplus(z)
                l1 = -sp
                if diagonal:
                    l1 = jnp.where(causal, l1, 0.0)
                hi, lo = _split(l1)
                after = _dot(hi, after_mat) + _dot(lo, after_mat)
                wgt = jnp.exp((z - sp) + after + run)
                if diagonal:
                    wgt = jnp.where(causal, wgt, 0.0)
                acc = acc + _dot(wgt.astype(BF16), vj)
                run = run + jnp.sum(l1, axis=-1, keepdims=True)
                return acc, run

            carry = (jnp.zeros((tq, LANES), F32), jnp.zeros((tq, 1), F32))
            carry = step(i, carry, True)
            carry = lax.fori_loop(0, i, lambda jj, cr: step(i - 1 - jj, cr, False), carry)
            accs.append(carry[0])
            runs.append(carry[1])
        first = lane < HEAD_DIM
        o_ref[...] = jnp.where(first, accs[0], accs[1]).astype(BF16)
        l_ref[...] = jnp.where(first, runs[0], runs[1])

    qmap = lambda b, hp, i: (b * nq + i, hp)
    kmap = lambda b, hp, i: (b, hp)
    return pl.pallas_call(
        body, name="attn_fwd", grid=(nb, w // LANES, nq),
        in_specs=[pl.BlockSpec((tq, LANES), qmap), pl.BlockSpec((seq, LANES), kmap), pl.BlockSpec((seq, LANES), kmap)],
        out_specs=[pl.BlockSpec((tq, LANES), qmap), pl.BlockSpec((tq, LANES), qmap)],
        out_shape=[jax.ShapeDtypeStruct((t_all, w), BF16), jax.ShapeDtypeStruct((t_all, w), F32)],
        compiler_params=_params(("arbitrary", "arbitrary", "arbitrary")),
    )(q, k, v)


def _window_sums(ext, rows, offset, forward):
    r = lax.broadcasted_iota(jnp.int32, (rows, rows + HALO), 0)
    e = lax.broadcasted_iota(jnp.int32, (rows, rows + HALO), 1)
    hi, lo = _split(ext)
    out = []
    for g, win in enumerate(POOL_WINDOWS):
        if forward:
            band = (e >= r) & (e < r + win)
        else:
            band = (e <= r + offset) & (e > r + offset - win)
        bm = band.astype(BF16)
        cols = slice(g * POOL_GROUP, (g + 1) * POOL_GROUP)
        out.append(_dot(bm, hi[:, cols]) + _dot(bm, lo[:, cols]))
    return out


def _window_counts(pos):
    return [jnp.minimum(pos + 1, win).astype(F32) for win in POOL_WINDOWS]


def _mixer_post(u, o, x, mod, g_post, g_fpre, w_pool, pool_scale, w_out, seq, tm):
    t_all, d = x.shape
    nt = seq // tm
    p = u.shape[1]

    def body(u_ref, halo_ref, o_ref, x_ref, mod_ref, gp_ref, gf_ref, wp_ref, ps_ref, wo_ref,
             pooled_ref, mixin_ref, mix_ref, x1_ref, h2_ref):
        it = pl.program_id(0) % nt
        uf = u_ref[...]
        halo = jnp.where(it == 0, 0.0, halo_ref[...])
        ext = jnp.concatenate([halo, uf], axis=0)
        pos = it * tm + lax.broadcasted_iota(jnp.int32, (tm, 1), 0)
        sums = _window_sums(ext, tm, HALO, False)
        cnts = _window_counts(pos)
        pools = []
        for g in range(len(POOL_WINDOWS)):
            cols = slice(g * POOL_GROUP, (g + 1) * POOL_GROUP)
            pooled = (sums[g] / cnts[g] - uf[:, cols]).astype(BF16)
            pooled_ref[:, cols] = pooled
            yg = _dot(pooled, wp_ref[g].astype(BF16))
            pools.append((yg * ps_ref[:, cols]).astype(BF16))
        mixin = jnp.concatenate([o_ref[...]] + pools, axis=1)
        mixin_ref[...] = mixin
        mix = _dot(mixin, wo_ref[...])
        mix_ref[...] = mix
        n2 = mix * _rms(mix)
        x1 = x_ref[...] + mod_ref[0, 2:3, :] * (n2 * gp_ref[...])
        x1_ref[...] = x1
        n3 = x1 * _rms(x1)
        h2 = (n3 * gf_ref[...]) * (1.0 + mod_ref[0, 4:5, :]) + mod_ref[0, 3:4, :]
        h2_ref[...] = h2.astype(BF16)

    tok = lambda i: (i, 0)
    const2 = lambda i: (0, 0)
    hb = tm // HALO
    return pl.pallas_call(
        body, name="mixer_post", grid=(t_all // tm,),
        in_specs=[pl.BlockSpec((tm, p), tok),
                  pl.BlockSpec((HALO, p), lambda i: (jnp.maximum(i * hb - 1, 0), 0)),
                  pl.BlockSpec((tm, p), tok),
                  pl.BlockSpec((tm, d), tok),
                  pl.BlockSpec((1, MOD_ROWS, d), lambda i: (i // nt, 0, 0)),
                  pl.BlockSpec((1, d), const2), pl.BlockSpec((1, d), const2),
                  pl.BlockSpec(w_pool.shape, lambda i: (0, 0, 0)),
                  pl.BlockSpec((1, p), const2),
                  pl.BlockSpec((d, d), const2)],
        out_specs=[pl.BlockSpec((tm, p), tok), pl.BlockSpec((tm, d), tok), pl.BlockSpec((tm, d), tok),
                   pl.BlockSpec((tm, d), tok), pl.BlockSpec((tm, d), tok)],
        out_shape=[jax.ShapeDtypeStruct((t_all, p), BF16), jax.ShapeDtypeStruct((t_all, d), BF16),
                   jax.ShapeDtypeStruct((t_all, d), F32), jax.ShapeDtypeStruct((t_all, d), F32),
                   jax.ShapeDtypeStruct((t_all, d), BF16)],
        compiler_params=_params(("arbitrary",)),
    )(u, u, o, x, mod, g_post, g_fpre, w_pool, pool_scale, w_out)


def _ffn_fwd(h2, w_g, w_u, w_d, x1, tgt, mod, g_post, seq, tm):
    t_all, d = x1.shape
    nt = seq // tm
    nk, _, ff = w_g.shape

    def body(h_ref, wg_ref, wu_ref, wd_ref, x1_ref, t_ref, mod_ref, g_ref,
             a_ref, b_ref, fin_ref, dy_ref, df_ref, loss_ref, accb_ref, accg_ref, facc):
        i, k = pl.program_id(0), pl.program_id(1)
        hb = h_ref[...]
        a = _dot(hb, wg_ref[0])
        b = _dot(hb, wu_ref[0])
        a_ref[0] = a.astype(BF16)
        b_ref[0] = b.astype(BF16)
        fin = ((a * _sigmoid(a)) * b).astype(BF16)
        fin_ref[0] = fin
        part = _dot(fin, wd_ref[0])

        @pl.when(k == 0)
        def _():
            facc[...] = part

        @pl.when(k > 0)
        def _():
            facc[...] += part

        @pl.when(k == nk - 1)
        def _():
            f = facc[...]
            r4 = _rms(f)
            n4 = f * r4
            gate = mod_ref[0, 5:6, :]
            g = g_ref[...]
            err = (x1_ref[...] + gate * (n4 * g)) - t_ref[...]
            dy = err * (1.0 / d)
            dy_ref[...] = dy

            @pl.when(i == 0)
            def _():
                loss_ref[...] = jnp.zeros_like(loss_ref)
                accg_ref[...] = jnp.zeros_like(accg_ref)

            @pl.when(i % nt == 0)
            def _():
                accb_ref[...] = jnp.zeros_like(accb_ref)

            loss_ref[...] += (0.5 / d) * jnp.sum(err * err)
            accb_ref[0, 0:1, :] += _colsum(dy * (n4 * g))
            accg_ref[0:1, :] += _colsum((dy * gate) * n4)
            dn4 = (dy * gate) * g
            df_ref[...] = _norm_bwd(dn4, n4, r4).astype(BF16)

    tok = lambda i, k: (i, 0)
    ktok = lambda i, k: (k, i, 0)
    kw = lambda i, k: (k, 0, 0)
    const2 = lambda i, k: (0, 0)
    return pl.pallas_call(
        body, name="ffn_fwd", grid=(t_all // tm, nk),
        in_specs=[pl.BlockSpec((tm, d), tok),
                  pl.BlockSpec((1, d, ff), kw), pl.BlockSpec((1, d, ff), kw), pl.BlockSpec((1, ff, d), kw),
                  pl.BlockSpec((tm, d), tok), pl.BlockSpec((tm, d), tok),
                  pl.BlockSpec((1, MOD_ROWS, d), lambda i, k: (i // nt, 0, 0)),
                  pl.BlockSpec((1, d), const2)],
        out_specs=[pl.BlockSpec((1, tm, ff), ktok)] * 3
        + [pl.BlockSpec((tm, d), tok), pl.BlockSpec((tm, d), tok),
           pl.BlockSpec((8, LANES), const2),
           pl.BlockSpec((1, 8, d), lambda i, k: (i // nt, 0, 0)),
           pl.BlockSpec((8, d), const2)],
        out_shape=[jax.ShapeDtypeStruct((nk, t_all, ff), BF16)] * 3
        + [jax.ShapeDtypeStruct((t_all, d), F32), jax.ShapeDtypeStruct((t_all, d), BF16),
           jax.ShapeDtypeStruct((8, LANES), F32),
           jax.ShapeDtypeStruct((t_all // seq, 8, d), F32),
           jax.ShapeDtypeStruct((8, d), F32)],
        scratch_shapes=[pltpu.VMEM((tm, d), F32)],
        compiler_params=_params(("arbitrary", "arbitrary")),
    )(h2, w_g, w_u, w_d, x1, tgt, mod, g_post)


def _ffn_bwd(df, a, b, w_d, w_g, w_u, x1, dy, mix, mod, g_fpre, g_mpost, seq, tm):
    t_all, d = x1.shape
    nt = seq // tm
    nk, _, ff = w_g.shape

    def body(df_ref, a_ref, b_ref, wd_ref, wg_ref, wu_ref, x1_ref, dy_ref, mix_ref, mod_ref, gf_ref, gm_ref,
             da_ref, db_ref, dx1_ref, dmix_ref, accb_ref, accg_ref, hacc):
        i, k = pl.program_id(0), pl.program_id(1)
        dfin = _dot_nt(df_ref[...], wd_ref[0])
        af = a_ref[0].astype(F32)
        bf = b_ref[0].astype(F32)
        sig = _sigmoid(af)
        da = ((dfin * bf) * (sig * (1.0 + af * (1.0 - sig)))).astype(BF16)
        db = (dfin * (af * sig)).astype(BF16)
        da_ref[0] = da
        db_ref[0] = db
        part = _dot_nt(da, wg_ref[0]) + _dot_nt(db, wu_ref[0])

        @pl.when(k == 0)
        def _():
            hacc[...] = part

        @pl.when(k > 0)
        def _():
            hacc[...] += part

        @pl.when(k == nk - 1)
        def _():
            @pl.when(i == 0)
            def _():
                accg_ref[...] = jnp.zeros_like(accg_ref)

            @pl.when(i % nt == 0)
            def _():
                accb_ref[...] = jnp.zeros_like(accb_ref)

            dh2 = hacc[...]
            x1 = x1_ref[...]
            r3 = _rms(x1)
            n3 = x1 * r3
            g3 = gf_ref[...]
            scale1 = 1.0 + mod_ref[0, 4:5, :]
            accb_ref[0, 0:1, :] += _colsum(dh2)
            accb_ref[0, 1:2, :] += _colsum(dh2 * (n3 * g3))
            accg_ref[0:1, :] += _colsum((dh2 * scale1) * n3)
            dx1 = dy_ref[...] + _norm_bwd((dh2 * scale1) * g3, n3, r3)
            dx1_ref[...] = dx1
            mix = mix_ref[...]
            r2 = _rms(mix)
            n2 = mix * r2
            g2 = gm_ref[...]
            gate = mod_ref[0, 2:3, :]
            accb_ref[0, 2:3, :] += _colsum(dx1 * (n2 * g2))
            accg_ref[1:2, :] += _colsum((dx1 * gate) * n2)
            dmix_ref[...] = _norm_bwd((dx1 * gate) * g2, n2, r2).astype(BF16)

    tok = lambda i, k: (i, 0)
    ktok = lambda i, k: (k, i, 0)
    kw = lambda i, k: (k, 0, 0)
    const2 = lambda i, k: (0, 0)
    return pl.pallas_call(
        body, name="ffn_bwd", grid=(t_all // tm, nk),
        in_specs=[pl.BlockSpec((tm, d), tok),
                  pl.BlockSpec((1, tm, ff), ktok), pl.BlockSpec((1, tm, ff), ktok),
                  pl.BlockSpec((1, ff, d), kw), pl.BlockSpec((1, d, ff), kw), pl.BlockSpec((1, d, ff), kw),
                  pl.BlockSpec((tm, d), tok), pl.BlockSpec((tm, d), tok), pl.BlockSpec((tm, d), tok),
                  pl.BlockSpec((1, MOD_ROWS, d), lambda i, k: (i // nt, 0, 0)),
                  pl.BlockSpec((1, d), const2), pl.BlockSpec((1, d), const2)],
        out_specs=[pl.BlockSpec((1, tm, ff), ktok)] * 2
        + [pl.BlockSpec((tm, d), tok), pl.BlockSpec((tm, d), tok),
           pl.BlockSpec((1, 8, d), lambda i, k: (i // nt, 0, 0)),
           pl.BlockSpec((8, d), const2)],
        out_shape=[jax.ShapeDtypeStruct((nk, t_all, ff), BF16)] * 2
        + [jax.ShapeDtypeStruct((t_all, d), F32), jax.ShapeDtypeStruct((t_all, d), BF16),
           jax.ShapeDtypeStruct((t_all // seq, 8, d), F32),
           jax.ShapeDtypeStruct((8, d), F32)],
        scratch_shapes=[pltpu.VMEM((tm, d), F32)],
        compiler_params=_params(("arbitrary", "arbitrary")),
    )(df, a, b, w_d, w_g, w_u, x1, dy, mix, mod, g_fpre, g_mpost)


def _mixer_bwd(dmix, w_out, pooled, w_pool, pool_scale, seq, tm):
    t_all, d = dmix.shape
    p = pooled.shape[1]
    ng = len(POOL_WINDOWS)

    def body(dm_ref, wo_ref, pooled_ref, wp_ref, ps_ref, do_ref, dpd_ref, dps_ref, dwp_ref):
        i = pl.program_id(0)

        @pl.when(i == 0)
        def _():
            dps_ref[...] = jnp.zeros_like(dps_ref)
            dwp_ref[...] = jnp.zeros_like(dwp_ref)

        dmixin = _dot_nt(dm_ref[...], wo_ref[...])
        do_ref[...] = dmixin[:, :p].astype(BF16)
        for g in range(ng):
            cols = slice(g * POOL_GROUP, (g + 1) * POOL_GROUP)
            dpool = dmixin[:, p + g * POOL_GROUP:p + (g + 1) * POOL_GROUP]
            pooled = pooled_ref[:, cols]
            wpg = wp_ref[g].astype(BF16)
            yg = _dot(pooled, wpg)
            dps_ref[0:1, cols] += _colsum(dpool * yg)
            dyg = (dpool * ps_ref[:, cols]).astype(BF16)
            dwp_ref[g] += _dot_tn(pooled, dyg)
            dpd_ref[:, cols] = _dot_nt(dyg, wpg)

    tok = lambda i: (i, 0)
    const2 = lambda i: (0, 0)
    const3 = lambda i: (0, 0, 0)
    return pl.pallas_call(
        body, name="mixer_bwd", grid=(t_all // tm,),
        in_specs=[pl.BlockSpec((tm, d), tok), pl.BlockSpec((d, d), const2), pl.BlockSpec((tm, p), tok),
                  pl.BlockSpec(w_pool.shape, const3), pl.BlockSpec((1, p), const2)],
        out_specs=[pl.BlockSpec((tm, p), tok), pl.BlockSpec((tm, p), tok),
                   pl.BlockSpec((8, p), const2), pl.BlockSpec(w_pool.shape, const3)],
        out_shape=[jax.ShapeDtypeStruct((t_all, p), BF16), jax.ShapeDtypeStruct((t_all, p), F32),
                   jax.ShapeDtypeStruct((8, p), F32), jax.ShapeDtypeStruct(w_pool.shape, F32)],
        compiler_params=_params(("arbitrary",)),
    )(dmix, w_out, pooled, w_pool, pool_scale)


def _attn_bwd(q, k, v, do, ltot, seq, tq):
    t_all, w = q.shape
    nb, nq, tk = t_all // seq, seq // tq, tq

    def body(q_ref, k_ref, v_ref, do_ref, l_ref, dq_ref, dk_ref, dv_ref, dk_acc, dv_acc):
        i = pl.program_id(2)

        @pl.when(i == 0)
        def _():
            dk_acc[...] = jnp.zeros_like(dk_acc)
            dv_acc[...] = jnp.zeros_like(dv_acc)

        lane = lax.broadcasted_iota(jnp.int32, (1, LANES), 1)
        row = lax.broadcasted_iota(jnp.int32, (tq, tk), 0)
        col = lax.broadcasted_iota(jnp.int32, (tq, tk), 1)
        upto_mat = (row <= col).astype(BF16)
        before_mat = (row < col).astype(BF16)
        causal = col < row
        q2 = q_ref[...]
        do2 = do_ref[...]
        l2 = l_ref[...]
        dqs = []
        for h in range(HEADS_PER_BLOCK):
            head = (lane >= HEAD_DIM) if h else (lane < HEAD_DIM)
            qh = jnp.where(head, q2, jnp.zeros_like(q2))
            doh = jnp.where(head, do2, jnp.zeros_like(do2))
            lh = l2[:, h * HEAD_DIM:h * HEAD_DIM + 1]

            def step(j, carry, diagonal, qh=qh, doh=doh, lh=lh):
                dq, pre_l, pre_g = carry
                off = pl.multiple_of(j * tk, tk)
                kj = k_ref[pl.ds(off, tk), :]
                vj = v_ref[pl.ds(off, tk), :]
                z = _dot_nt(qh, kj)
                sp = _softplus(z)
                l1 = -sp
                if diagonal:
                    l1 = jnp.where(causal, l1, 0.0)
                hi, lo = _split(l1)
                upto = pre_l + _dot(hi, upto_mat) + _dot(lo, upto_mat)
                logsig = z - sp
                wgt = jnp.exp(logsig + (lh - upto))
                if diagonal:
                    wgt = jnp.where(causal, wgt, 0.0)
                g = wgt * _dot_nt(doh, vj)
                ghi, glo = _split(g)
                before = pre_g + _dot(ghi, before_mat) + _dot(glo, before_mat)
                sig = jnp.exp(logsig)
                dz = g * (1.0 - sig) - sig * before
                if diagonal:
                    dz = jnp.where(causal, dz, 0.0)
                dzb = dz.astype(BF16)
                dq = dq + _dot(dzb, kj)
                dk_acc[pl.ds(off, tk), :] += _dot_tn(dzb, qh)
                dv_acc[pl.ds(off, tk), :] += _dot_tn(wgt.astype(BF16), doh)
                pre_l = pre_l + jnp.sum(l1, axis=-1, keepdims=True)
                pre_g = pre_g + jnp.sum(g, axis=-1, keepdims=True)
                return dq, pre_l, pre_g

            carry = (jnp.zeros((tq, LANES), F32), jnp.zeros((tq, 1), F32), jnp.zeros((tq, 1), F32))
            carry = lax.fori_loop(0, i, lambda j, cr: step(j, cr, False), carry)
            carry = step(i, carry, True)
            dqs.append(carry[0])
        dq_ref[...] = (jnp.where(lane < HEAD_DIM, dqs[0], dqs[1]) * QK_SCALE).astype(BF16)

        @pl.when(i == nq - 1)
        def _():
            dk_ref[...] = dk_acc[...].astype(BF16)
            dv_ref[...] = dv_acc[...].astype(BF16)

    qmap = lambda b, hp, i: (b * nq + i, hp)
    kmap = lambda b, hp, i: (b, hp)
    return pl.pallas_call(
        body, name="attn_bwd", grid=(nb, w // LANES, nq),
        in_specs=[pl.BlockSpec((tq, LANES), qmap), pl.BlockSpec((seq, LANES), kmap), pl.BlockSpec((seq, LANES), kmap),
                  pl.BlockSpec((tq, LANES), qmap), pl.BlockSpec((tq, LANES), qmap)],
        out_specs=[pl.BlockSpec((tq, LANES), qmap), pl.BlockSpec((seq, LANES), kmap), pl.BlockSpec((seq, LANES), kmap)],
        out_shape=[jax.ShapeDtypeStruct((t_all, w), BF16)] * 3,
        scratch_shapes=[pltpu.VMEM((seq, LANES), F32), pltpu.VMEM((seq, LANES), F32)],
        compiler_params=_params(("arbitrary", "arbitrary", "arbitrary")),
    )(q, k, v, do, ltot)


def _inproj_bwd(dq, dk, dv, dpd, x, dx1, mod, g_pre, w_in, seq, tm):
    t_all, d = x.shape
    nt = seq // tm
    p = dq.shape[1]

    def body(dq_ref, dk_ref, dv_ref, dpd_ref, halo_ref, x_ref, dx1_ref, mod_ref, g_ref, w_ref,
             gx_ref, du_ref, accb_ref, accg_ref):
        i = pl.program_id(0)
        it = i % nt

        @pl.when(i == 0)
        def _():
            accg_ref[...] = jnp.zeros_like(accg_ref)

        @pl.when(it == 0)
        def _():
            accb_ref[...] = jnp.zeros_like(accb_ref)

        dpd = dpd_ref[...]
        pos = it * tm + lax.broadcasted_iota(jnp.int32, (tm, 1), 0)
        cnts = _window_counts(pos)
        halo = jnp.where(it == nt - 1, 0.0, halo_ref[...])
        scaled = []
        halos = []
        for g, win in enumerate(POOL_WINDOWS):
            cols = slice(g * POOL_GROUP, (g + 1) * POOL_GROUP)
            scaled.append(dpd[:, cols] / cnts[g])
            halos.append(halo[:, cols] / float(win))
        ext = jnp.concatenate([jnp.concatenate(scaled, axis=1), jnp.concatenate(halos, axis=1)], axis=0)
        sums = _window_sums(ext, tm, 0, True)
        du = (jnp.concatenate(sums, axis=1) - dpd).astype(BF16)
        du_ref[...] = du
        dh1 = (_dot_nt(dq_ref[...], w_ref[0]) + _dot_nt(dk_ref[...], w_ref[1])
               + _dot_nt(dv_ref[...], w_ref[2]) + _dot_nt(du, w_ref[3]))
        xf = x_ref[...]
        r1 = _rms(xf)
        n1 = xf * r1
        g1 = g_ref[...]
        scale1 = 1.0 + mod_ref[0, 1:2, :]
        accb_ref[0, 0:1, :] += _colsum(dh1)
        accb_ref[0, 1:2, :] += _colsum(dh1 * (n1 * g1))
        accg_ref[0:1, :] += _colsum((dh1 * scale1) * n1)
        gx_ref[...] = dx1_ref[...] + _norm_bwd((dh1 * scale1) * g1, n1, r1)

    tok = lambda i: (i, 0)
    const2 = lambda i: (0, 0)
    hb = tm // HALO
    last = t_all // HALO - 1
    return pl.pallas_call(
        body, name="inproj_bwd", grid=(t_all // tm,),
        in_specs=[pl.BlockSpec((tm, p), tok), pl.BlockSpec((tm, p), tok), pl.BlockSpec((tm, p), tok),
                  pl.BlockSpec((tm, p), tok),
                  pl.BlockSpec((HALO, p), lambda i: (jnp.minimum((i + 1) * hb, last), 0)),
                  pl.BlockSpec((tm, d), tok), pl.BlockSpec((tm, d), tok),
                  pl.BlockSpec((1, MOD_ROWS, d), lambda i: (i // nt, 0, 0)),
                  pl.BlockSpec((1, d), const2),
                  pl.BlockSpec((N_CHIPS, d, p), lambda i: (0, 0, 0))],
        out_specs=[pl.BlockSpec((tm, d), tok), pl.BlockSpec((tm, p), tok),
                   pl.BlockSpec((1, 8, d), lambda i: (i // nt, 0, 0)),
                   pl.BlockSpec((8, d), const2)],
        out_shape=[jax.ShapeDtypeStruct((t_all, d), F32), jax.ShapeDtypeStruct((t_all, p), BF16),
                   jax.ShapeDtypeStruct((t_all // seq, 8, d), F32),
                   jax.ShapeDtypeStruct((8, d), F32)],
        compiler_params=_params(("arbitrary",)),
    )(dq, dk, dv, dpd, dpd, x, dx1, mod, g_pre, w_in)


def _tn_matmul(x, ys, nk, bt, name):
    t_all = x.shape[-2]
    m = x.shape[-1]
    ny = len(ys)

    def spec(arr):
        if arr.ndim == 3:
            return pl.BlockSpec((1, bt, arr.shape[-1]), lambda k, t: (k, t, 0))
        return pl.BlockSpec((bt, arr.shape[-1]), lambda k, t: (t, 0))

    def tile(ref):
        return ref[0] if len(ref.shape) == 3 else ref[...]

    def body(*refs):
        x_ref, y_refs, o_refs = refs[0], refs[1:1 + ny], refs[1 + ny:]
        t = pl.program_id(1)
        xt = tile(x_ref)
        for y_ref, o_ref in zip(y_refs, o_refs):
            part = _dot_tn(xt, tile(y_ref))

            @pl.when(t == 0)
            def _(o_ref=o_ref, part=part):
                o_ref[0] = part

            @pl.when(t > 0)
            def _(o_ref=o_ref, part=part):
                o_ref[0] += part

    return pl.pallas_call(
        body, name=name, grid=(nk, t_all // bt),
        in_specs=[spec(x)] + [spec(y) for y in ys],
        out_specs=[pl.BlockSpec((1, m, y.shape[-1]), lambda k, t: (k, 0, 0)) for y in ys],
        out_shape=[jax.ShapeDtypeStruct((nk, m, y.shape[-1]), F32) for y in ys],
        compiler_params=_params(("arbitrary", "arbitrary")),
    )(x, *ys)


def _cond_fwd(c_all, w_q, b_q, bn):
    nrow, d = c_all.shape
    ncol = w_q.shape[1]

    def body(c_ref, w_ref, b_ref, sc_ref, mod_ref):
        cf = c_ref[...]
        sc = cf * _sigmoid(cf)
        sc_ref[...] = sc
        shi, slo = _split(sc)
        whi, wlo = _split(w_ref[...])
        mod_ref[...] = (_dot(shi, whi) + _dot(shi, wlo) + _dot(slo, whi)) + b_ref[...]

    return pl.pallas_call(
        body, name="cond_fwd", grid=(ncol // bn,),
        in_specs=[pl.BlockSpec((nrow, d), lambda n: (0, 0)), pl.BlockSpec((d, bn), lambda n: (0, n)),
                  pl.BlockSpec((1, bn), lambda n: (0, n))],
        out_specs=[pl.BlockSpec((nrow, d), lambda n: (0, 0)), pl.BlockSpec((nrow, bn), lambda n: (0, n))],
        out_shape=[jax.ShapeDtypeStruct((nrow, d), F32), jax.ShapeDtypeStruct((nrow, ncol), F32)],
        compiler_params=_params(("arbitrary",)),
    )(c_all, w_q, b_q)


def _cond_bwd(sc_all, dmod_q, bn):
    nrow, d = sc_all.shape
    ncol = dmod_q.shape[1]

    def body(sc_ref, dm_ref, gw_ref):
        shi, slo = _split(sc_ref[...])
        dhi, dlo = _split(dm_ref[...])
        gw_ref[...] = _dot_tn(shi, dhi) + _dot_tn(shi, dlo) + _dot_tn(slo, dhi)

    return pl.pallas_call(
        body, name="cond_bwd", grid=(ncol // bn,),
        in_specs=[pl.BlockSpec((nrow, d), lambda n: (0, 0)), pl.BlockSpec((nrow, bn), lambda n: (0, n))],
        out_specs=pl.BlockSpec((d, bn), lambda n: (0, n)),
        out_shape=jax.ShapeDtypeStruct((d, ncol), F32),
        compiler_params=_params(("arbitrary",)),
    )(sc_all, dmod_q)


def _row_block(rows, cols, budget=1 << 18):
    best = None
    for br in range(8, rows + 1, 8):
        if rows % br == 0 and br * cols <= budget:
            best = br
    return best if best is not None else rows


def _adamw(w, g, m, v, name):
    rows, cols = w.shape
    br = _row_block(rows, cols)
    c1 = 1.0 - ADAM_B1 ** ADAM_STEP
    c2 = 1.0 - ADAM_B2 ** ADAM_STEP

    def body(w_ref, g_ref, m_ref, v_ref, d_ref, nm_ref, nv_ref):
        gf = g_ref[...]
        m2 = ADAM_B1 * m_ref[...] + (1.0 - ADAM_B1) * gf
        v2 = ADAM_B2 * v_ref[...] + (1.0 - ADAM_B2) * (gf * gf)
        nm_ref[...] = m2
        nv_ref[...] = v2
        d_ref[...] = -ADAM_LR * ((m2 / c1) / (jnp.sqrt(v2 / c2) + ADAM_EPS) + ADAM_WD * w_ref[...])

    blk = pl.BlockSpec((br, cols), lambda i: (i, 0))
    return pl.pallas_call(
        body, name=name, grid=(rows // br,),
        in_specs=[blk] * 4, out_specs=[blk] * 3,
        out_shape=[jax.ShapeDtypeStruct((rows, cols), F32)] * 3,
        compiler_params=_params(("arbitrary",)),
    )(w, g, m, v)


def _all_gather(x_shard, name):
    m_per, n = x_shard.shape

    def body(x_ref, out_ref, send_sems, recv_sems, local_sem):
        x, y, c = _position()
        me, sibling = (x, y, c), (x, y, 1 - c)
        chips = [(1 - x, y), (x, 1 - y), (1 - x, 1 - y)]

        def rows(px, py, pc):
            return out_ref.at[pl.ds((4 * px + 2 * py + pc) * m_per, m_per), :]

        def copy(k, block, to, src=None):
            return pltpu.make_async_remote_copy(
                src_ref=rows(*block) if src is None else src, dst_ref=rows(*block),
                send_sem=send_sems.at[k], recv_sem=recv_sems.at[k], device_id=to, device_id_type=MESH)

        mine = pltpu.make_async_copy(x_ref, rows(*me), local_sem)
        mine.start()
        first = [copy(0, me, sibling, src=x_ref)]
        first += [copy(1 + j, me, (*chip, c), src=x_ref) for j, chip in enumerate(chips)]
        for cp in first:
            cp.start()
        passed = [copy(4 + j, (*chip, c), sibling) for j, chip in enumerate(chips)]
        for j, chip in enumerate(chips):
            copy(1 + j, (*chip, c), me).wait_recv()
            passed[j].start()
        copy(0, sibling, me).wait_recv()
        for j, chip in enumerate(chips):
            copy(4 + j, (*chip, 1 - c), me).wait_recv()
        for cp in first + passed:
            cp.wait_send()
        mine.wait()

    return pl.pallas_call(
        body, name=name,
        out_shape=jax.ShapeDtypeStruct((N_DEV * m_per, n), x_shard.dtype),
        in_specs=[pl.BlockSpec(memory_space=pltpu.VMEM)],
        out_specs=pl.BlockSpec(memory_space=pltpu.VMEM),
        scratch_shapes=[pltpu.SemaphoreType.DMA((7,)), pltpu.SemaphoreType.DMA((7,)), pltpu.SemaphoreType.DMA],
        compiler_params=pltpu.CompilerParams(vmem_limit_bytes=VMEM_LIMIT),
    )(x_shard)


_ANY = pl.BlockSpec(memory_space=pl.ANY)


def _gather_weights(quarters):
    n = len(quarters)
    shapes = [q.shape for q in quarters]

    def body(*refs):
        w_refs, g_refs = refs[:n], refs[n:2 * n]
        send_sems, recv_sems, local_sems = refs[2 * n:]
        x, y, c = _position()
        sibling = (x, y, 1 - c)
        chips = [(1 - x, y), (x, 1 - y), (1 - x, 1 - y)]
        mine = 2 * x + y

        def half(a, which):
            hr = shapes[a][0] // 2
            return pl.ds(which * hr, hr)

        def over_ici(a, p, slot, src=None):
            dst = g_refs[a].at[slot, half(a, c), :]
            return pltpu.make_async_remote_copy(
                src_ref=dst if src is None else src, dst_ref=dst,
                send_sem=send_sems.at[6 * a + p], recv_sem=recv_sems.at[6 * a + p],
                device_id=(*chips[p], c), device_id_type=MESH)

        def over_d2d(a, p, slot, which):
            ref = g_refs[a].at[slot, half(a, which), :]
            return pltpu.make_async_remote_copy(
                src_ref=ref, dst_ref=ref,
                send_sem=send_sems.at[6 * a + 3 + p], recv_sem=recv_sems.at[6 * a + 3 + p],
                device_id=sibling, device_id_type=MESH)

        local = [pltpu.make_async_copy(w_refs[a], g_refs[a].at[mine], local_sems.at[a]) for a in range(n)]
        for cp in local:
            cp.start()
        sends = []
        for a in range(n):
            for p in range(3):
                cp = over_ici(a, p, mine, src=w_refs[a].at[half(a, c), :])
                cp.start()
                sends.append(cp)
        for a in range(n):
            for p, (cx, cy) in enumerate(chips):
                slot = 2 * cx + cy
                over_ici(a, p, slot).wait_recv()
                cp = over_d2d(a, p, slot, c)
                cp.start()
                sends.append(cp)
        for a in range(n):
            for p, (cx, cy) in enumerate(chips):
                over_d2d(a, p, 2 * cx + cy, 1 - c).wait_recv()
        for cp in sends:
            cp.wait_send()
        for cp in local:
            cp.wait()

    return pl.pallas_call(
        body, name="gather_weights",
        out_shape=[jax.ShapeDtypeStruct((N_CHIPS,) + s, BF16) for s in shapes],
        in_specs=[_ANY] * n, out_specs=[_ANY] * n,
        scratch_shapes=[pltpu.SemaphoreType.DMA((6 * n,)), pltpu.SemaphoreType.DMA((6 * n,)),
                        pltpu.SemaphoreType.DMA((n,))],
    )(*quarters)


def _sibling_exchange(grads):
    n = len(grads)
    shapes = [g.shape for g in grads]

    def body(*refs):
        g_refs, x_refs = refs[:n], refs[n:2 * n]
        send_sems, recv_sems = refs[2 * n:]
        x, y, c = _position()
        copies = []
        for a in range(n):
            hr = shapes[a][1] // 2
            cp = pltpu.make_async_remote_copy(
                src_ref=g_refs[a].at[:, pl.ds((1 - c) * hr, hr), :], dst_ref=x_refs[a],
                send_sem=send_sems.at[a], recv_sem=recv_sems.at[a],
                device_id=(x, y, 1 - c), device_id_type=MESH)
            cp.start()
            copies.append(cp)
        for cp in copies:
            cp.wait()

    return pl.pallas_call(
        body, name="grad_sibling_exchange",
        out_shape=[jax.ShapeDtypeStruct((s[0], s[1] // 2, s[2]), F32) for s in shapes],
        in_specs=[_ANY] * n, out_specs=[_ANY] * n,
        scratch_shapes=[pltpu.SemaphoreType.DMA((n,)), pltpu.SemaphoreType.DMA((n,))],
    )(*grads)


def _chip_sums(core, grads, theirs):
    n = len(grads)

    def body(core_ref, *refs):
        g_refs, t_refs, o_refs = refs[:n], refs[n:2 * n], refs[2 * n:]
        for g_ref, t_ref, o_ref in zip(g_refs, t_refs, o_refs):
            o_ref[...] = (g_ref[...] + t_ref[...]).astype(BF16)

    in_specs = [pl.BlockSpec((1, g.shape[1] // 2, g.shape[2]), lambda k, core_ref: (k, core_ref[0], 0)) for g in grads]
    in_specs += [pl.BlockSpec((1,) + t.shape[1:], lambda k, core_ref: (k, 0, 0)) for t in theirs]
    return pl.pallas_call(
        body, name="grad_chip_sums",
        grid_spec=pltpu.PrefetchScalarGridSpec(
            num_scalar_prefetch=1, grid=(N_CHIPS,), in_specs=in_specs,
            out_specs=[pl.BlockSpec((1,) + t.shape[1:], lambda k, core_ref: (k, 0, 0)) for t in theirs]),
        out_shape=[jax.ShapeDtypeStruct(t.shape, BF16) for t in theirs],
        compiler_params=_params(("arbitrary",)),
    )(core, *grads, *theirs)


def _chip_exchange(sums):
    n = len(sums)

    def body(*refs):
        s_refs, y_refs = refs[:n], refs[n:2 * n]
        send_sems, recv_sems, local_sems = refs[2 * n:]
        x, y, c = _position()
        chips = [(1 - x, y), (x, 1 - y), (1 - x, 1 - y)]
        mine = 2 * x + y
        local = [pltpu.make_async_copy(s_refs[a].at[mine], y_refs[a].at[mine], local_sems.at[a]) for a in range(n)]
        for cp in local:
            cp.start()
        copies = []
        for a in range(n):
            for p, (cx, cy) in enumerate(chips):
                cp = pltpu.make_async_remote_copy(
                    src_ref=s_refs[a].at[2 * cx + cy], dst_ref=y_refs[a].at[mine],
                    send_sem=send_sems.at[3 * a + p], recv_sem=recv_sems.at[3 * a + p],
                    device_id=(cx, cy, c), device_id_type=MESH)
                cp.start()
                copies.append((cp, a, 2 * cx + cy, p))
        for cp, a, slot, p in copies:
            cp.wait_send()
            pltpu.make_async_remote_copy(
                src_ref=s_refs[a].at[slot], dst_ref=y_refs[a].at[slot],
                send_sem=send_sems.at[3 * a + p], recv_sem=recv_sems.at[3 * a + p],
                device_id=(x, y, c), device_id_type=MESH).wait_recv()
        for cp in local:
            cp.wait()

    return pl.pallas_call(
        body, name="grad_chip_exchange",
        out_shape=[jax.ShapeDtypeStruct(s.shape, BF16) for s in sums],
        in_specs=[_ANY] * n, out_specs=[_ANY] * n,
        scratch_shapes=[pltpu.SemaphoreType.DMA((3 * n,)), pltpu.SemaphoreType.DMA((3 * n,)),
                        pltpu.SemaphoreType.DMA((n,))],
    )(*sums)


def _total_sums(parts):
    n = len(parts)

    def body(*refs):
        for y_ref, o_ref in zip(refs[:n], refs[n:]):
            o_ref[...] = ((y_ref[0].astype(F32) + y_ref[1].astype(F32)) + y_ref[2].astype(F32)) + y_ref[3].astype(F32)

    rows = 2

    def specs(pt):
        hr, cols = pt.shape[1], pt.shape[2]
        step = hr // rows
        return (pl.BlockSpec((N_CHIPS, step, cols), lambda r: (0, r, 0)), pl.BlockSpec((step, cols), lambda r: (r, 0)))

    return pl.pallas_call(
        body, name="grad_total_sums", grid=(rows,),
        in_specs=[specs(pt)[0] for pt in parts], out_specs=[specs(pt)[1] for pt in parts],
        out_shape=[jax.ShapeDtypeStruct(pt.shape[1:], F32) for pt in parts],
        compiler_params=_params(("arbitrary",)),
    )(*parts)


def _sibling_share(halves):
    n = len(halves)

    def body(*refs):
        t_refs, f_refs = refs[:n], refs[n:2 * n]
        send_sems, recv_sems, local_sems = refs[2 * n:]
        x, y, c = _position()
        copies, local = [], []
        for a in range(n):
            hr = halves[a].shape[0]
            dst = f_refs[a].at[pl.ds(c * hr, hr), :]
            lc = pltpu.make_async_copy(t_refs[a], dst, local_sems.at[a])
            lc.start()
            local.append(lc)
            cp = pltpu.make_async_remote_copy(
                src_ref=t_refs[a], dst_ref=dst, send_sem=send_sems.at[a], recv_sem=recv_sems.at[a],
                device_id=(x, y, 1 - c), device_id_type=MESH)
            cp.start()
            copies.append(cp)
        for a, cp in enumerate(copies):
            hr = halves[a].shape[0]
            cp.wait_send()
            other = f_refs[a].at[pl.ds((1 - c) * hr, hr), :]
            pltpu.make_async_remote_copy(
                src_ref=other, dst_ref=other, send_sem=send_sems.at[a], recv_sem=recv_sems.at[a],
                device_id=(x, y, c), device_id_type=MESH).wait_recv()
        for lc in local:
            lc.wait()

    return pl.pallas_call(
        body, name="grad_sibling_share",
        out_shape=[jax.ShapeDtypeStruct((2 * h.shape[0], h.shape[1]), F32) for h in halves],
        in_specs=[_ANY] * n, out_specs=[_ANY] * n,
        scratch_shapes=[pltpu.SemaphoreType.DMA((n,)), pltpu.SemaphoreType.DMA((n,)), pltpu.SemaphoreType.DMA((n,))],
    )(*halves)


def _group_sum(stacked, nrow, name):
    total, n = stacked.shape
    groups = total // nrow

    def body(g_ref, o_ref):
        acc = g_ref[0:nrow, :]
        for grp in range(1, groups):
            acc = acc + g_ref[grp * nrow:(grp + 1) * nrow, :]
        o_ref[...] = acc

    return pl.pallas_call(
        body, name=name,
        out_shape=jax.ShapeDtypeStruct((nrow, n), F32),
        compiler_params=pltpu.CompilerParams(vmem_limit_bytes=VMEM_LIMIT),
    )(stacked)


def _local_step(xt, tgt, mod, gains, w_pool, pool_scale, weights, seq):
    g_mpre, g_mpost, g_fpre, g_fpost = gains
    w_in, w_out, w_g, w_u, w_d = weights
    d = xt.shape[1]
    tm, tq = min(TOKEN_TILE, seq), min(ATTN_TILE, seq)
    w_out2 = w_out.reshape(d, d)

    h1, q, k, v, u = _prenorm_proj(xt, mod, g_mpre, w_in, seq, tm)
    o, ltot = _attn_fwd(q, k, v, seq, tq)
    pooled, mixin, mix, x1, h2 = _mixer_post(u, o, xt, mod, g_mpost, g_fpre, w_pool, pool_scale, w_out2, seq, tm)
    a, b, fin, dy, df, loss_blk, accb4, accg4 = _ffn_fwd(h2, w_g, w_u, w_d, x1, tgt, mod, g_fpost, seq, tm)
    da, db, dx1, dmix, accb5, accg5 = _ffn_bwd(df, a, b, w_d, w_g, w_u, x1, dy, mix, mod, g_fpre, g_mpost, seq, tm)
    do, dpd, dps, dwp = _mixer_bwd(dmix, w_out2, pooled, w_pool, pool_scale, seq, tm)
    dq, dk, dv = _attn_bwd(q, k, v, do, ltot, seq, tq)
    gx, du, accb8, accg8 = _inproj_bwd(dq, dk, dv, dpd, xt, dx1, mod, g_mpre, w_in, seq, tm)

    g_in = jnp.concatenate(_tn_matmul(h1, [dq, dk, dv, du], 1, tm, "grad_w_in"), axis=0)
    g_out = _tn_matmul(mixin, [dmix], 1, tm, "grad_w_out")[0].reshape(w_out.shape)
    g_g, g_u = _tn_matmul(h2, [da, db], w_g.shape[0], tm, "grad_w_gate_up")
    (g_d,) = _tn_matmul(fin, [df], w_d.shape[0], tm, "grad_w_down")

    dmod = jnp.stack([accb8[:, 0], accb8[:, 1], accb5[:, 2], accb5[:, 0], accb5[:, 1], accb4[:, 0]], axis=1)
    dgain = jnp.stack([accg8[0], accg5[1], accg5[0], accg4[0]], axis=0)
    return loss_blk, gx, [g_in, g_out, g_g, g_u, g_d], dmod, dgain, dps[0:1], dwp


def kernel(x, c, w_cond, b_cond, g_mix_pre, g_mix_post, w_in, w_pool, pool_scale, w_out, g_ffn_pre, g_ffn_post, w_gate, w_up, w_down, loss_target, m_w_cond, m_b_cond, m_g_mix_pre, m_g_mix_post, m_w_in, m_w_pool, m_pool_scale, m_w_out, m_g_ffn_pre, m_g_ffn_post, m_w_gate, m_w_up, m_w_down, v_w_cond, v_b_cond, v_g_mix_pre, v_g_mix_post, v_w_in, v_w_pool, v_pool_scale, v_w_out, v_g_ffn_pre, v_g_ffn_post, v_w_gate, v_w_up, v_w_down):
    xi, yi, ci = _position()
    chip = 2 * xi + yi
    dev = 4 * xi + 2 * yi + ci
    nb, seq, d = x.shape
    t_all = nb * seq
    xt = x.reshape(t_all, d)
    tgt = loss_target.reshape(t_all, d)
    ncol = w_cond.shape[2]
    pw = pool_scale.shape[1]

    c_pad = jnp.concatenate([c, jnp.zeros((8 - nb, d), F32)], axis=0)
    c_all = _all_gather(c_pad, "gather_c").reshape(N_DEV, 8, d)[:, :nb].reshape(N_DEV * nb, d)
    b_q = lax.dynamic_slice(b_cond, (0, chip * ncol), (1, ncol))
    sc_all, mod_q = _cond_fwd(c_all, w_cond[0], b_q, 512)
    mod_parts = _all_gather(mod_q, "gather_mod").reshape(N_DEV, N_DEV * nb, ncol)
    mod_rows = lax.dynamic_slice(mod_parts, (0, dev * nb, 0), (N_DEV, nb, ncol))[0::2]
    mod = jnp.transpose(mod_rows, (1, 0, 2)).reshape(nb, N_MOD, d)
    mod = jnp.concatenate([mod, jnp.zeros((nb, MOD_ROWS - N_MOD, d), F32)], axis=1)

    quarters = [w[0].astype(BF16) for w in (w_in, w_out, w_gate, w_up, w_down)]
    weights = _gather_weights(quarters)

    gains = (g_mix_pre, g_mix_post, g_ffn_pre, g_ffn_post)
    loss_blk, gx, grads, dmod, dgain, dps, dwp = _local_step(xt, tgt, mod, gains, w_pool[0], pool_scale, weights, seq)
    loss = lax.psum(loss_blk[0, 0], ("x", "y", "c"))

    theirs = _sibling_exchange(grads)
    sums = _chip_sums(jnp.reshape(ci, (1,)).astype(jnp.int32), grads, theirs)
    parts = _chip_exchange(sums)
    halves = _total_sums(parts)
    g_big = _sibling_share(halves)

    wp_rows = dwp.size // d
    pad_rows = 24 - (2 * N_MOD + 4 + 1)
    payload = jnp.concatenate([
        dmod.reshape(nb * N_MOD, d), dgain,
        jnp.concatenate([dps, jnp.zeros((1, d - pw), F32)], axis=1),
        jnp.zeros((pad_rows, d), F32), dwp.reshape(wp_rows, d)], axis=0)
    prow = payload.shape[0]
    gathered = _all_gather(payload, "gather_small")
    summed = _group_sum(gathered, prow, "small_device_sum")
    dmod_all = gathered.reshape(N_DEV, prow, d)[:, :nb * N_MOD].reshape(N_DEV * nb, N_MOD * d)
    g_b_cond = _group_sum(dmod_all, 1, "grad_b_cond")
    dmod_q = lax.dynamic_slice(dmod_all, (0, chip * ncol), (N_DEV * nb, ncol))
    g_w_cond = _cond_bwd(sc_all, dmod_q, 512)
    first_gain = 2 * N_MOD
    g_gains = [summed[first_gain + r:first_gain + r + 1] for r in range(4)]
    g_pool_scale = summed[first_gain + 4:first_gain + 5, :pw]
    g_w_pool = summed[24:24 + wp_rows].reshape(w_pool.shape[1] * w_pool.shape[2], w_pool.shape[3])

    flat_pool = lambda t: t.reshape(g_w_pool.shape)
    plan = [
        ("w_cond", w_cond[0], g_w_cond, m_w_cond[0], v_w_cond[0], w_cond.shape),
        ("b_cond", b_cond, g_b_cond, m_b_cond, v_b_cond, b_cond.shape),
        ("g_mix_pre", g_mix_pre, g_gains[0], m_g_mix_pre, v_g_mix_pre, g_mix_pre.shape),
        ("g_mix_post", g_mix_post, g_gains[1], m_g_mix_post, v_g_mix_post, g_mix_post.shape),
        ("w_in", w_in[0], g_big[0], m_w_in[0], v_w_in[0], w_in.shape),
        ("w_pool", flat_pool(w_pool), g_w_pool, flat_pool(m_w_pool), flat_pool(v_w_pool), w_pool.shape),
        ("pool_scale", pool_scale, g_pool_scale, m_pool_scale, v_pool_scale, pool_scale.shape),
        ("w_out", w_out[0], g_big[1], m_w_out[0], v_w_out[0], w_out.shape),
        ("g_ffn_pre", g_ffn_pre, g_gains[2], m_g_ffn_pre, v_g_ffn_pre, g_ffn_pre.shape),
        ("g_ffn_post", g_ffn_post, g_gains[3], m_g_ffn_post, v_g_ffn_post, g_ffn_post.shape),
        ("w_gate", w_gate[0], g_big[2], m_w_gate[0], v_w_gate[0], w_gate.shape),
        ("w_up", w_up[0], g_big[3], m_w_up[0], v_w_up[0], w_up.shape),
        ("w_down", w_down[0], g_big[4], m_w_down[0], v_w_down[0], w_down.shape),
    ]
    out_g, out_d, out_m, out_v = [], [], [], []
    for name, w2, g2, m2, v2, shape in plan:
        delta, new_m, new_v = _adamw(w2, g2, m2, v2, "adamw_" + name)
        out_g.append(g2.reshape(shape))
        out_d.append(delta.reshape(shape))
        out_m.append(new_m.reshape(shape))
        out_v.append(new_v.reshape(shape))
    return (loss, gx.reshape(x.shape), *out_g, *out_d, *out_m, *out_v)
```

```python
import functools

import jax
import jax.numpy as jnp
from jax import lax
from jax.experimental import pallas as pl
from jax.experimental.pallas import tpu as pltpu

F32 = jnp.float32
BF16 = jnp.bfloat16
MESH = pl.DeviceIdType.MESH

EPS = 1e-6
HEAD_DIM = 64
HEADS_PER_BLOCK = 2
LANES = 128
NEG_QK_SCALE = -0.125
POOL_WINDOWS = (2, 4, 8, 16)
POOL_GROUP = 128
HALO = 16
N_MOD = 6
MOD_ROWS = 8
N_CHIPS = 4
N_DEV = 8
VMEM_LIMIT = 56 * 1024 * 1024

ADAM_LR = 0.001
ADAM_B1 = 0.9
ADAM_B2 = 0.999
ADAM_EPS = 1e-08
ADAM_WD = 0.01
ADAM_STEP = 10

TOKEN_TILE = 512
ATTN_TILE = 512
ATTN_KEY_TILE = 256
ATTN_ROW_CHUNK = 32


def _dot(a, b):
    return jnp.dot(a, b, preferred_element_type=F32)


def _dot_nt(a, b):
    return lax.dot_general(a, b, (((1,), (1,)), ((), ())), preferred_element_type=F32)


def _dot_tn(a, b):
    return lax.dot_general(a, b, (((0,), (0,)), ((), ())), preferred_element_type=F32)


def _split(v):
    hi = v.astype(BF16)
    lo = (v - hi.astype(F32)).astype(BF16)
    return hi, lo


def _rms(v):
    return lax.rsqrt(jnp.mean(v * v, axis=-1, keepdims=True) + EPS)


def _norm_bwd(dn, n, r):
    return r * (dn - n * jnp.mean(dn * n, axis=-1, keepdims=True))


def _sigmoid(v):
    return 1.0 / (1.0 + jnp.exp(-v))


def _colsum(v):
    return jnp.sum(v, axis=0, keepdims=True)


def _params(sem=None):
    return pltpu.CompilerParams(dimension_semantics=sem, vmem_limit_bytes=VMEM_LIMIT)


def _position():
    return lax.axis_index("x"), lax.axis_index("y"), lax.axis_index("c")


def _prenorm_proj(x, mod, g_pre, w_in, seq, tm):
    t_all, d = x.shape
    nt = seq // tm
    p = w_in.shape[2]

    def body(x_ref, mod_ref, g_ref, w_ref, h_ref, q_ref, k_ref, v_ref, u_ref, kt_ref, vt_ref):
        xf = x_ref[...]
        n = xf * _rms(xf)
        h = (n * g_ref[...]) * (1.0 + mod_ref[0, 1:2, :]) + mod_ref[0, 0:1, :]
        hb = h.astype(BF16)
        h_ref[...] = hb
        q_ref[...] = (_dot(hb, w_ref[0]) * NEG_QK_SCALE).astype(BF16)
        kf = _dot(hb, w_ref[1])
        vf = _dot(hb, w_ref[2])
        k_ref[...] = kf.astype(BF16)
        v_ref[...] = vf.astype(BF16)
        kt_ref[...] = kf.T.astype(BF16)
        vt_ref[...] = vf.T.astype(BF16)
        u_ref[...] = _dot(hb, w_ref[3])

    tok = lambda i: (i, 0)
    tok_t = lambda i: (0, i)
    return pl.pallas_call(
        body, name="prenorm_proj", grid=(t_all // tm,),
        in_specs=[pl.BlockSpec((tm, d), tok),
                  pl.BlockSpec((1, MOD_ROWS, d), lambda i: (i // nt, 0, 0)),
                  pl.BlockSpec((1, d), lambda i: (0, 0)),
                  pl.BlockSpec((N_CHIPS, d, p), lambda i: (0, 0, 0))],
        out_specs=[pl.BlockSpec((tm, d), tok)] + [pl.BlockSpec((tm, p), tok)] * 4 + [pl.BlockSpec((p, tm), tok_t)] * 2,
        out_shape=[jax.ShapeDtypeStruct((t_all, d), BF16)] + [jax.ShapeDtypeStruct((t_all, p), BF16)] * 3
        + [jax.ShapeDtypeStruct((t_all, p), F32)] + [jax.ShapeDtypeStruct((p, t_all), BF16)] * 2,
        compiler_params=_params(("arbitrary",)),
    )(x, mod, g_pre, w_in)


def _tri_matrix(tk, kind):
    j = lax.broadcasted_iota(jnp.int32, (2 * tk, tk), 0) % tk
    s = lax.broadcasted_iota(jnp.int32, (2 * tk, tk), 1)
    return {"after": j > s, "upto": j <= s, "before": j < s}[kind].astype(BF16)


def _row_sums(v):
    return jnp.broadcast_to(jnp.sum(v, axis=-1, keepdims=True), (v.shape[0], LANES))


def _across(v, n):
    return jnp.concatenate([v] * (n // LANES), axis=1)


def _all_masked(c, diag, rc, tk):
    return diag is not None and diag * tk >= (c + 1) * rc - 1


def _some_masked(c, diag, rc, tk):
    return diag is not None and diag * tk + tk - 1 >= c * rc


def _attn_fwd(qn, k, vt, seq, tq, tk):
    t_all, w = qn.shape
    nb, nq, ndiag = t_all // seq, seq // tq, tq // tk
    assert ndiag % 2 == 0, "two key blocks per loop trip"
    rc = ATTN_ROW_CHUNK
    heads = range(HEADS_PER_BLOCK)

    def body(q_ref, k_ref, vt_ref, tri_ref, o_ref, l_ref,
             z_buf, ls_buf, hl_buf, aft_buf, w_buf, tot_buf, acc_t, run_buf):
        i = pl.program_id(2)
        nblk = (i + 1) * ndiag
        lane = lax.broadcasted_iota(jnp.int32, (1, LANES), 1)
        sub = lax.broadcasted_iota(jnp.int32, (LANES, 1), 0)
        row = lax.broadcasted_iota(jnp.int32, (rc, tk), 0)
        col = lax.broadcasted_iota(jnp.int32, (rc, tk), 1)
        first = lane < HEAD_DIM
        q2 = q_ref[...]
        qs = [jnp.where(first, q2, jnp.zeros_like(q2)), jnp.where(first, jnp.zeros_like(q2), q2)]
        acc_t[...] = jnp.zeros_like(acc_t)
        run_buf[...] = jnp.zeros_like(run_buf)
        w_buf[1] = jnp.zeros((HEADS_PER_BLOCK, tq, tk), BF16)

        def causal(c, diag):
            return (col + diag * tk) < (row + c * rc)

        def scores(blk, slot):
            kj = k_ref[pl.ds(pl.multiple_of(blk * tk, tk), tk), :]
            for h in heads:
                z_buf[slot, h] = _dot_nt(qs[h], kj)

        def values(blk, slot):
            vtj = vt_ref[:, pl.ds(pl.multiple_of(blk * tk, tk), tk)]
            zero = jnp.zeros_like(vtj)
            acc_t[...] += (_dot_nt(jnp.where(sub < HEAD_DIM, vtj, zero), w_buf[slot, 0])
                           + _dot_nt(jnp.where(sub < HEAD_DIM, zero, vtj), w_buf[slot, 1]))

        def softplus_stage(h, slot, diag):
            for c in range(tq // rc):
                rows = slice(c * rc, (c + 1) * rc)
                if _all_masked(c, diag, rc, tk):
                    hl_buf[h, rows, :] = jnp.zeros((rc, 2 * tk), BF16)
                    tot_buf[h, rows, :] = jnp.zeros((rc, LANES), F32)
                    continue
                nz = z_buf[slot, h, rows, :]
                l1 = jnp.minimum(nz, 0.0) - jnp.log(1.0 + jnp.exp(-jnp.abs(nz)))
                if _some_masked(c, diag, rc, tk):
                    l1 = jnp.where(causal(c, diag), l1, 0.0)
                hi, lo = _split(l1)
                hl_buf[h, rows, 0:tk] = hi
                hl_buf[h, rows, tk:2 * tk] = lo
                ls_buf[h, rows, :] = l1 - nz
                tot_buf[h, rows, :] = _row_sums(l1)

        def weights_stage(h, slot, diag):
            for c in range(tq // rc):
                rows = slice(c * rc, (c + 1) * rc)
                if _all_masked(c, diag, rc, tk):
                    w_buf[slot, h, rows, :] = jnp.zeros((rc, tk), BF16)
                    continue
                wgt = jnp.exp((ls_buf[h, rows, :] + aft_buf[h, rows, :]) + _across(run_buf[h, rows, :], tk))
                if _some_masked(c, diag, rc, tk):
                    wgt = jnp.where(causal(c, diag), wgt, 0.0)
                w_buf[slot, h, rows, :] = wgt.astype(BF16)
                run_buf[h, rows, :] += tot_buf[h, rows, :]

        def position(blk, slot, diag):
            scores(jnp.maximum(blk - 1, 0), 1 - slot)
            for h in heads:
                softplus_stage(h, slot, diag)
                aft_buf[h] = _dot(hl_buf[h], tri_ref[...])
            values(jnp.minimum(blk + 1, nblk - 1), 1 - slot)
            for h in heads:
                weights_stage(h, slot, diag)

        scores(nblk - 1, 0)
        for p in range(ndiag):
            position(nblk - 1 - p, p % 2, ndiag - 1 - p)

        def trip(jj, carry):
            for u in range(2):
                position(i * ndiag - 1 - 2 * jj - u, u, None)
            return carry

        lax.fori_loop(0, (i * ndiag) // 2, trip, 0)
        values(0, 1)
        o_ref[...] = acc_t[...].T.astype(BF16)
        l_ref[...] = jnp.where(first, run_buf[0], run_buf[1])

    qmap = lambda b, hp, i: (b * nq + i, hp)
    nh = HEADS_PER_BLOCK
    return pl.pallas_call(
        body, name="attn_fwd", grid=(nb, w // LANES, nq),
        in_specs=[pl.BlockSpec((tq, LANES), qmap), pl.BlockSpec((seq, LANES), lambda b, hp, i: (b, hp)),
                  pl.BlockSpec((LANES, seq), lambda b, hp, i: (hp, b)),
                  pl.BlockSpec((2 * tk, tk), lambda b, hp, i: (0, 0))],
        out_specs=[pl.BlockSpec((tq, LANES), qmap), pl.BlockSpec((tq, LANES), qmap)],
        out_shape=[jax.ShapeDtypeStruct((t_all, w), BF16), jax.ShapeDtypeStruct((t_all, w), F32)],
        scratch_shapes=[pltpu.VMEM((2, nh, tq, tk), F32), pltpu.VMEM((nh, tq, tk), F32),
                        pltpu.VMEM((nh, tq, 2 * tk), BF16), pltpu.VMEM((nh, tq, tk), F32),
                        pltpu.VMEM((2, nh, tq, tk), BF16), pltpu.VMEM((nh, tq, LANES), F32),
                        pltpu.VMEM((LANES, tq), F32), pltpu.VMEM((nh, tq, LANES), F32)],
        compiler_params=_params(("arbitrary", "arbitrary", "arbitrary")),
    )(qn, k, vt, _tri_matrix(tk, "after"))


def _window_sums(ext, rows, offset, forward):
    r = lax.broadcasted_iota(jnp.int32, (rows, rows + HALO), 0)
    e = lax.broadcasted_iota(jnp.int32, (rows, rows + HALO), 1)
    hi, lo = _split(ext)
    out = []
    for g, win in enumerate(POOL_WINDOWS):
        if forward:
            band = (e >= r) & (e < r + win)
        else:
            band = (e <= r + offset) & (e > r + offset - win)
        bm = band.astype(BF16)
        cols = slice(g * POOL_GROUP, (g + 1) * POOL_GROUP)
        out.append(_dot(bm, hi[:, cols]) + _dot(bm, lo[:, cols]))
    return out


def _window_counts(pos):
    return [jnp.minimum(pos + 1, win).astype(F32) for win in POOL_WINDOWS]


def _mixer_post(u, o, x, mod, g_post, g_fpre, w_pool, pool_scale, w_out, seq, tm):
    t_all, d = x.shape
    nt = seq // tm
    p = u.shape[1]

    def body(u_ref, halo_ref, o_ref, x_ref, mod_ref, gp_ref, gf_ref, wp_ref, ps_ref, wo_ref,
             pooled_ref, mixin_ref, mix_ref, x1_ref, h2_ref):
        it = pl.program_id(0) % nt
        uf = u_ref[...]
        halo = jnp.where(it == 0, 0.0, halo_ref[...])
        ext = jnp.concatenate([halo, uf], axis=0)
        pos = it * tm + lax.broadcasted_iota(jnp.int32, (tm, 1), 0)
        sums = _window_sums(ext, tm, HALO, False)
        cnts = _window_counts(pos)
        pools = []
        for g in range(len(POOL_WINDOWS)):
            cols = slice(g * POOL_GROUP, (g + 1) * POOL_GROUP)
            pooled = (sums[g] / cnts[g] - uf[:, cols]).astype(BF16)
            pooled_ref[:, cols] = pooled
            yg = _dot(pooled, wp_ref[g].astype(BF16))
            pools.append((yg * ps_ref[:, cols]).astype(BF16))
        mixin = jnp.concatenate([o_ref[...]] + pools, axis=1)
        mixin_ref[...] = mixin
        mix = _dot(mixin, wo_ref[...])
        mix_ref[...] = mix
        n2 = mix * _rms(mix)
        x1 = x_ref[...] + mod_ref[0, 2:3, :] * (n2 * gp_ref[...])
        x1_ref[...] = x1
        n3 = x1 * _rms(x1)
        h2 = (n3 * gf_ref[...]) * (1.0 + mod_ref[0, 4:5, :]) + mod_ref[0, 3:4, :]
        h2_ref[...] = h2.astype(BF16)

    tok = lambda i: (i, 0)
    const2 = lambda i: (0, 0)
    hb = tm // HALO
    return pl.pallas_call(
        body, name="mixer_post", grid=(t_all // tm,),
        in_specs=[pl.BlockSpec((tm, p), tok),
                  pl.BlockSpec((HALO, p), lambda i: (jnp.maximum(i * hb - 1, 0), 0)),
                  pl.BlockSpec((tm, p), tok),
                  pl.BlockSpec((tm, d), tok),
                  pl.BlockSpec((1, MOD_ROWS, d), lambda i: (i // nt, 0, 0)),
                  pl.BlockSpec((1, d), const2), pl.BlockSpec((1, d), const2),
                  pl.BlockSpec(w_pool.shape, lambda i: (0, 0, 0)),
                  pl.BlockSpec((1, p), const2),
                  pl.BlockSpec((d, d), const2)],
        out_specs=[pl.BlockSpec((tm, p), tok), pl.BlockSpec((tm, d), tok), pl.BlockSpec((tm, d), tok),
                   pl.BlockSpec((tm, d), tok), pl.BlockSpec((tm, d), tok)],
        out_shape=[jax.ShapeDtypeStruct((t_all, p), BF16), jax.ShapeDtypeStruct((t_all, d), BF16),
                   jax.ShapeDtypeStruct((t_all, d), F32), jax.ShapeDtypeStruct((t_all, d), F32),
                   jax.ShapeDtypeStruct((t_all, d), BF16)],
        compiler_params=_params(("arbitrary",)),
    )(u, u, o, x, mod, g_post, g_fpre, w_pool, pool_scale, w_out)


def _ffn_fwd(h2, w_g, w_u, w_d, x1, tgt, mod, g_post, seq, tm):
    t_all, d = x1.shape
    nt = seq // tm
    nk, _, ff = w_g.shape

    def body(h_ref, wg_ref, wu_ref, wd_ref, x1_ref, t_ref, mod_ref, g_ref,
             a_ref, b_ref, fin_ref, dy_ref, df_ref, loss_ref, accb_ref, accg_ref, facc):
        i, k = pl.program_id(0), pl.program_id(1)
        hb = h_ref[...]
        a = _dot(hb, wg_ref[0])
        b = _dot(hb, wu_ref[0])
        a_ref[0] = a.astype(BF16)
        b_ref[0] = b.astype(BF16)
        fin = ((a * _sigmoid(a)) * b).astype(BF16)
        fin_ref[0] = fin
        part = _dot(fin, wd_ref[0])

        @pl.when(k == 0)
        def _():
            facc[...] = part

        @pl.when(k > 0)
        def _():
            facc[...] += part

        @pl.when(k == nk - 1)
        def _():
            f = facc[...]
            r4 = _rms(f)
            n4 = f * r4
            gate = mod_ref[0, 5:6, :]
            g = g_ref[...]
            err = (x1_ref[...] + gate * (n4 * g)) - t_ref[...]
            dy = err * (1.0 / d)
            dy_ref[...] = dy

            @pl.when(i == 0)
            def _():
                loss_ref[...] = jnp.zeros_like(loss_ref)
                accg_ref[...] = jnp.zeros_like(accg_ref)

            @pl.when(i % nt == 0)
            def _():
                accb_ref[...] = jnp.zeros_like(accb_ref)

            loss_ref[...] += (0.5 / d) * jnp.sum(err * err)
            accb_ref[0, 0:1, :] += _colsum(dy * (n4 * g))
            accg_ref[0:1, :] += _colsum((dy * gate) * n4)
            dn4 = (dy * gate) * g
            df_ref[...] = _norm_bwd(dn4, n4, r4).astype(BF16)

    tok = lambda i, k: (i, 0)
    ktok = lambda i, k: (k, i, 0)
    kw = lambda i, k: (k, 0, 0)
    const2 = lambda i, k: (0, 0)
    return pl.pallas_call(
        body, name="ffn_fwd", grid=(t_all // tm, nk),
        in_specs=[pl.BlockSpec((tm, d), tok),
                  pl.BlockSpec((1, d, ff), kw), pl.BlockSpec((1, d, ff), kw), pl.BlockSpec((1, ff, d), kw),
                  pl.BlockSpec((tm, d), tok), pl.BlockSpec((tm, d), tok),
                  pl.BlockSpec((1, MOD_ROWS, d), lambda i, k: (i // nt, 0, 0)),
                  pl.BlockSpec((1, d), const2)],
        out_specs=[pl.BlockSpec((1, tm, ff), ktok)] * 3
        + [pl.BlockSpec((tm, d), tok), pl.BlockSpec((tm, d), tok),
           pl.BlockSpec((8, LANES), const2),
           pl.BlockSpec((1, 8, d), lambda i, k: (i // nt, 0, 0)),
           pl.BlockSpec((8, d), const2)],
        out_shape=[jax.ShapeDtypeStruct((nk, t_all, ff), BF16)] * 3
        + [jax.ShapeDtypeStruct((t_all, d), F32), jax.ShapeDtypeStruct((t_all, d), BF16),
           jax.ShapeDtypeStruct((8, LANES), F32),
           jax.ShapeDtypeStruct((t_all // seq, 8, d), F32),
           jax.ShapeDtypeStruct((8, d), F32)],
        scratch_shapes=[pltpu.VMEM((tm, d), F32)],
        compiler_params=_params(("arbitrary", "arbitrary")),
    )(h2, w_g, w_u, w_d, x1, tgt, mod, g_post)


def _ffn_bwd(df, a, b, w_d, w_g, w_u, x1, dy, mix, mod, g_fpre, g_mpost, seq, tm):
    t_all, d = x1.shape
    nt = seq // tm
    nk, _, ff = w_g.shape

    def body(df_ref, a_ref, b_ref, wd_ref, wg_ref, wu_ref, x1_ref, dy_ref, mix_ref, mod_ref, gf_ref, gm_ref,
             da_ref, db_ref, dx1_ref, dmix_ref, accb_ref, accg_ref, hacc):
        i, k = pl.program_id(0), pl.program_id(1)
        dfin = _dot_nt(df_ref[...], wd_ref[0])
        af = a_ref[0].astype(F32)
        bf = b_ref[0].astype(F32)
        sig = _sigmoid(af)
        da = ((dfin * bf) * (sig * (1.0 + af * (1.0 - sig)))).astype(BF16)
        db = (dfin * (af * sig)).astype(BF16)
        da_ref[0] = da
        db_ref[0] = db
        part = _dot_nt(da, wg_ref[0]) + _dot_nt(db, wu_ref[0])

        @pl.when(k == 0)
        def _():
            hacc[...] = part

        @pl.when(k > 0)
        def _():
            hacc[...] += part

        @pl.when(k == nk - 1)
        def _():
            @pl.when(i == 0)
            def _():
                accg_ref[...] = jnp.zeros_like(accg_ref)

            @pl.when(i % nt == 0)
            def _():
                accb_ref[...] = jnp.zeros_like(accb_ref)

            dh2 = hacc[...]
            x1 = x1_ref[...]
            r3 = _rms(x1)
            n3 = x1 * r3
            g3 = gf_ref[...]
            scale1 = 1.0 + mod_ref[0, 4:5, :]
            accb_ref[0, 0:1, :] += _colsum(dh2)
            accb_ref[0, 1:2, :] += _colsum(dh2 * (n3 * g3))
            accg_ref[0:1, :] += _colsum((dh2 * scale1) * n3)
            dx1 = dy_ref[...] + _norm_bwd((dh2 * scale1) * g3, n3, r3)
            dx1_ref[...] = dx1
            mix = mix_ref[...]
            r2 = _rms(mix)
            n2 = mix * r2
            g2 = gm_ref[...]
            gate = mod_ref[0, 2:3, :]
            accb_ref[0, 2:3, :] += _colsum(dx1 * (n2 * g2))
            accg_ref[1:2, :] += _colsum((dx1 * gate) * n2)
            dmix_ref[...] = _norm_bwd((dx1 * gate) * g2, n2, r2).astype(BF16)

    tok = lambda i, k: (i, 0)
    ktok = lambda i, k: (k, i, 0)
    kw = lambda i, k: (k, 0, 0)
    const2 = lambda i, k: (0, 0)
    return pl.pallas_call(
        body, name="ffn_bwd", grid=(t_all // tm, nk),
        in_specs=[pl.BlockSpec((tm, d), tok),
                  pl.BlockSpec((1, tm, ff), ktok), pl.BlockSpec((1, tm, ff), ktok),
                  pl.BlockSpec((1, ff, d), kw), pl.BlockSpec((1, d, ff), kw), pl.BlockSpec((1, d, ff), kw),
                  pl.BlockSpec((tm, d), tok), pl.BlockSpec((tm, d), tok), pl.BlockSpec((tm, d), tok),
                  pl.BlockSpec((1, MOD_ROWS, d), lambda i, k: (i // nt, 0, 0)),
                  pl.BlockSpec((1, d), const2), pl.BlockSpec((1, d), const2)],
        out_specs=[pl.BlockSpec((1, tm, ff), ktok)] * 2
        + [pl.BlockSpec((tm, d), tok), pl.BlockSpec((tm, d), tok),
           pl.BlockSpec((1, 8, d), lambda i, k: (i // nt, 0, 0)),
           pl.BlockSpec((8, d), const2)],
        out_shape=[jax.ShapeDtypeStruct((nk, t_all, ff), BF16)] * 2
        + [jax.ShapeDtypeStruct((t_all, d), F32), jax.ShapeDtypeStruct((t_all, d), BF16),
           jax.ShapeDtypeStruct((t_all // seq, 8, d), F32),
           jax.ShapeDtypeStruct((8, d), F32)],
        scratch_shapes=[pltpu.VMEM((tm, d), F32)],
        compiler_params=_params(("arbitrary", "arbitrary")),
    )(df, a, b, w_d, w_g, w_u, x1, dy, mix, mod, g_fpre, g_mpost)


def _mixer_bwd(dmix, w_out, pooled, w_pool, pool_scale, seq, tm):
    t_all, d = dmix.shape
    p = pooled.shape[1]
    ng = len(POOL_WINDOWS)

    def body(dm_ref, wo_ref, pooled_ref, wp_ref, ps_ref, do_ref, dpd_ref, dps_ref, dwp_ref):
        i = pl.program_id(0)

        @pl.when(i == 0)
        def _():
            dps_ref[...] = jnp.zeros_like(dps_ref)
            dwp_ref[...] = jnp.zeros_like(dwp_ref)

        dmixin = _dot_nt(dm_ref[...], wo_ref[...])
        do_ref[...] = dmixin[:, :p].astype(BF16)
        for g in range(ng):
            cols = slice(g * POOL_GROUP, (g + 1) * POOL_GROUP)
            dpool = dmixin[:, p + g * POOL_GROUP:p + (g + 1) * POOL_GROUP]
            pooled = pooled_ref[:, cols]
            wpg = wp_ref[g].astype(BF16)
            yg = _dot(pooled, wpg)
            dps_ref[0:1, cols] += _colsum(dpool * yg)
            dyg = (dpool * ps_ref[:, cols]).astype(BF16)
            dwp_ref[g] += _dot_tn(pooled, dyg)
            dpd_ref[:, cols] = _dot_nt(dyg, wpg)

    tok = lambda i: (i, 0)
    const2 = lambda i: (0, 0)
    const3 = lambda i: (0, 0, 0)
    return pl.pallas_call(
        body, name="mixer_bwd", grid=(t_all // tm,),
        in_specs=[pl.BlockSpec((tm, d), tok), pl.BlockSpec((d, d), const2), pl.BlockSpec((tm, p), tok),
                  pl.BlockSpec(w_pool.shape, const3), pl.BlockSpec((1, p), const2)],
        out_specs=[pl.BlockSpec((tm, p), tok), pl.BlockSpec((tm, p), tok),
                   pl.BlockSpec((8, p), const2), pl.BlockSpec(w_pool.shape, const3)],
        out_shape=[jax.ShapeDtypeStruct((t_all, p), BF16), jax.ShapeDtypeStruct((t_all, p), F32),
                   jax.ShapeDtypeStruct((8, p), F32), jax.ShapeDtypeStruct(w_pool.shape, F32)],
        compiler_params=_params(("arbitrary",)),
    )(dmix, w_out, pooled, w_pool, pool_scale)


def _attn_bwd(qn, k, kt, v, do, ltot, seq, tq, tk):
    t_all, w = qn.shape
    nb, nq, ndiag, nkb = t_all // seq, seq // tq, tq // tk, seq // tk
    assert ndiag % 2 == 0, "two key blocks per loop trip"
    rc = ATTN_ROW_CHUNK
    nh = HEADS_PER_BLOCK
    heads = range(nh)

    def body(q_ref, k_ref, kt_ref, v_ref, do_ref, l_ref, up_ref, bf_ref, dq_ref, dk_ref, dv_ref,
             z_buf, dw_buf, ls_buf, hl_buf, upto_buf, g_buf, gb_buf, before_buf, w_buf, dz_buf,
             totl_buf, totg_buf, rem_buf, preg_buf, qnt_buf, dot_buf, dq_t, dk_t, dv_t):
        i = pl.program_id(2)
        nblk = (i + 1) * ndiag

        @pl.when(i == 0)
        def _():
            dk_t[...] = jnp.zeros_like(dk_t)
            dv_t[...] = jnp.zeros_like(dv_t)

        lane = lax.broadcasted_iota(jnp.int32, (1, LANES), 1)
        sub = lax.broadcasted_iota(jnp.int32, (LANES, 1), 0)
        row = lax.broadcasted_iota(jnp.int32, (rc, tk), 0)
        col = lax.broadcasted_iota(jnp.int32, (rc, tk), 1)
        first = lane < HEAD_DIM
        upper = sub < HEAD_DIM
        q2 = q_ref[...]
        do2 = do_ref[...]
        l2 = l_ref[...]
        qs = [jnp.where(first, q2, jnp.zeros_like(q2)), jnp.where(first, jnp.zeros_like(q2), q2)]
        dos = [jnp.where(first, do2, jnp.zeros_like(do2)), jnp.where(first, jnp.zeros_like(do2), do2)]
        for src, dst in ((q2, qnt_buf), (do2, dot_buf)):
            t = src.astype(F32).T
            dst[:, 0:tq] = jnp.where(upper, t, 0.0).astype(BF16)
            dst[:, tq:2 * tq] = jnp.where(upper, 0.0, t).astype(BF16)
        for h in heads:
            rem_buf[h] = jnp.where(first if h == 0 else ~first, l2, pltpu.roll(l2, HEAD_DIM, 1))
        preg_buf[...] = jnp.zeros_like(preg_buf)
        dq_t[...] = jnp.zeros_like(dq_t)
        w_buf[1] = jnp.zeros((nh * tq, tk), BF16)
        dz_buf[1] = jnp.zeros((nh * tq, tk), BF16)

        def causal(c, diag):
            return (col + diag * tk) < (row + c * rc)

        def scores(blk, slot):
            off = pl.multiple_of(blk * tk, tk)
            kj = k_ref[pl.ds(off, tk), :]
            vj = v_ref[pl.ds(off, tk), :]
            for h in heads:
                z_buf[slot, h] = _dot_nt(qs[h], kj)
                dw_buf[slot, h] = _dot_nt(dos[h], vj)

        def gradients(blk, slot):
            off = pl.multiple_of(blk * tk, tk)
            ktj = kt_ref[:, pl.ds(off, tk)]
            zero = jnp.zeros_like(ktj)
            dq_t[...] += (_dot_nt(jnp.where(upper, ktj, zero), dz_buf[slot, 0:tq, :])
                          + _dot_nt(jnp.where(upper, zero, ktj), dz_buf[slot, tq:2 * tq, :]))
            dk_t[blk] += _dot(qnt_buf[...], dz_buf[slot])
            dv_t[blk] += _dot(dot_buf[...], w_buf[slot])

        def softplus_stage(h, slot, diag):
            for c in range(tq // rc):
                rows = slice(c * rc, (c + 1) * rc)
                if _all_masked(c, diag, rc, tk):
                    hl_buf[h, rows, :] = jnp.zeros((rc, 2 * tk), BF16)
                    continue
                nz = z_buf[slot, h, rows, :]
                l1 = jnp.minimum(nz, 0.0) - jnp.log(1.0 + jnp.exp(-jnp.abs(nz)))
                if _some_masked(c, diag, rc, tk):
                    l1 = jnp.where(causal(c, diag), l1, 0.0)
                hi, lo = _split(l1)
                hl_buf[h, rows, 0:tk] = hi
                hl_buf[h, rows, tk:2 * tk] = lo
                ls_buf[h, rows, :] = l1 - nz
                totl_buf[h, rows, :] = _row_sums(l1)

        def weights_stage(h, slot, diag):
            for c in range(tq // rc):
                rows = slice(c * rc, (c + 1) * rc)
                stacked = slice(h * tq + c * rc, h * tq + (c + 1) * rc)
                if _all_masked(c, diag, rc, tk):
                    w_buf[slot, stacked, :] = jnp.zeros((rc, tk), BF16)
                    gb_buf[h, rows, :] = jnp.zeros((rc, tk), BF16)
                    continue
                wgt = jnp.exp(ls_buf[h, rows, :] + (_across(rem_buf[h, rows, :], tk) - upto_buf[h, rows, :]))
                if _some_masked(c, diag, rc, tk):
                    wgt = jnp.where(causal(c, diag), wgt, 0.0)
                w_buf[slot, stacked, :] = wgt.astype(BF16)
                g = wgt * dw_buf[slot, h, rows, :]
                g_buf[h, rows, :] = g
                gb_buf[h, rows, :] = g.astype(BF16)
                totg_buf[h, rows, :] = _row_sums(g)
                rem_buf[h, rows, :] -= totl_buf[h, rows, :]

        def dscore_stage(h, slot, diag):
            for c in range(tq // rc):
                rows = slice(c * rc, (c + 1) * rc)
                stacked = slice(h * tq + c * rc, h * tq + (c + 1) * rc)
                if _all_masked(c, diag, rc, tk):
                    dz_buf[slot, stacked, :] = jnp.zeros((rc, tk), BF16)
                    continue
                sig = jnp.exp(ls_buf[h, rows, :])
                g = g_buf[h, rows, :]
                dnz = sig * (before_buf[h, rows, :] + _across(preg_buf[h, rows, :], tk)) - g * (1.0 - sig)
                if _some_masked(c, diag, rc, tk):
                    dnz = jnp.where(causal(c, diag), dnz, 0.0)
                dz_buf[slot, stacked, :] = dnz.astype(BF16)
                preg_buf[h, rows, :] += totg_buf[h, rows, :]

        def position(blk, slot, diag, prefetch):
            if prefetch:
                scores(blk + 1, 1 - slot)
            for h in heads:
                softplus_stage(h, slot, diag)
                upto_buf[h] = _dot(hl_buf[h], up_ref[...])
            gradients(jnp.maximum(blk - 1, 0), 1 - slot)
            for h in heads:
                weights_stage(h, slot, diag)
                before_buf[h] = _dot(gb_buf[h], bf_ref[...])
            for h in heads:
                dscore_stage(h, slot, diag)

        scores(0, 0)

        def trip(jj, carry):
            for u in range(2):
                position(2 * jj + u, u, None, True)
            return carry

        lax.fori_loop(0, (i * ndiag) // 2, trip, 0)
        for d in range(ndiag):
            position(i * ndiag + d, d % 2, d, d < ndiag - 1)
        gradients(nblk - 1, 1)
        dq_ref[...] = (dq_t[...].T * NEG_QK_SCALE).astype(BF16)

        @pl.when(i == nq - 1)
        def _():
            for blk in range(nkb):
                dk_ref[blk * tk:(blk + 1) * tk, :] = dk_t[blk].T.astype(BF16)
                dv_ref[blk * tk:(blk + 1) * tk, :] = dv_t[blk].T.astype(BF16)

    qmap = lambda b, hp, i: (b * nq + i, hp)
    kmap = lambda b, hp, i: (b, hp)
    const = lambda b, hp, i: (0, 0)
    return pl.pallas_call(
        body, name="attn_bwd", grid=(nb, w // LANES, nq),
        in_specs=[pl.BlockSpec((tq, LANES), qmap), pl.BlockSpec((seq, LANES), kmap),
                  pl.BlockSpec((LANES, seq), lambda b, hp, i: (hp, b)), pl.BlockSpec((seq, LANES), kmap),
                  pl.BlockSpec((tq, LANES), qmap), pl.BlockSpec((tq, LANES), qmap),
                  pl.BlockSpec((2 * tk, tk), const), pl.BlockSpec((tk, tk), const)],
        out_specs=[pl.BlockSpec((tq, LANES), qmap), pl.BlockSpec((seq, LANES), kmap), pl.BlockSpec((seq, LANES), kmap)],
        out_shape=[jax.ShapeDtypeStruct((t_all, w), BF16)] * 3,
        scratch_shapes=[pltpu.VMEM((2, nh, tq, tk), F32), pltpu.VMEM((2, nh, tq, tk), F32),
                        pltpu.VMEM((nh, tq, tk), F32), pltpu.VMEM((nh, tq, 2 * tk), BF16),
                        pltpu.VMEM((nh, tq, tk), F32), pltpu.VMEM((nh, tq, tk), F32),
                        pltpu.VMEM((nh, tq, tk), BF16), pltpu.VMEM((nh, tq, tk), F32),
                        pltpu.VMEM((2, nh * tq, tk), BF16), pltpu.VMEM((2, nh * tq, tk), BF16),
                        pltpu.VMEM((nh, tq, LANES), F32), pltpu.VMEM((nh, tq, LANES), F32),
                        pltpu.VMEM((nh, tq, LANES), F32), pltpu.VMEM((nh, tq, LANES), F32),
                        pltpu.VMEM((LANES, nh * tq), BF16), pltpu.VMEM((LANES, nh * tq), BF16),
                        pltpu.VMEM((LANES, tq), F32), pltpu.VMEM((nkb, LANES, tk), F32),
                        pltpu.VMEM((nkb, LANES, tk), F32)],
        compiler_params=_params(("arbitrary", "arbitrary", "arbitrary")),
    )(qn, k, kt, v, do, ltot, _tri_matrix(tk, "upto"), _tri_matrix(tk, "before")[:tk])


def _inproj_bwd(dq, dk, dv, dpd, x, dx1, mod, g_pre, w_in, seq, tm):
    t_all, d = x.shape
    nt = seq // tm
    p = dq.shape[1]

    def body(dq_ref, dk_ref, dv_ref, dpd_ref, halo_ref, x_ref, dx1_ref, mod_ref, g_ref, w_ref,
             gx_ref, du_ref, accb_ref, accg_ref):
        i = pl.program_id(0)
        it = i % nt

        @pl.when(i == 0)
        def _():
            accg_ref[...] = jnp.zeros_like(accg_ref)

        @pl.when(it == 0)
        def _():
            accb_ref[...] = jnp.zeros_like(accb_ref)

        dpd = dpd_ref[...]
        pos = it * tm + lax.broadcasted_iota(jnp.int32, (tm, 1), 0)
        cnts = _window_counts(pos)
        halo = jnp.where(it == nt - 1, 0.0, halo_ref[...])
        scaled = []
        halos = []
        for g, win in enumerate(POOL_WINDOWS):
            cols = slice(g * POOL_GROUP, (g + 1) * POOL_GROUP)
            scaled.append(dpd[:, cols] / cnts[g])
            halos.append(halo[:, cols] / float(win))
        ext = jnp.concatenate([jnp.concatenate(scaled, axis=1), jnp.concatenate(halos, axis=1)], axis=0)
        sums = _window_sums(ext, tm, 0, True)
        du = (jnp.concatenate(sums, axis=1) - dpd).astype(BF16)
        du_ref[...] = du
        dh1 = (_dot_nt(dq_ref[...], w_ref[0]) + _dot_nt(dk_ref[...], w_ref[1])
               + _dot_nt(dv_ref[...], w_ref[2]) + _dot_nt(du, w_ref[3]))
        xf = x_ref[...]
        r1 = _rms(xf)
        n1 = xf * r1
        g1 = g_ref[...]
        scale1 = 1.0 + mod_ref[0, 1:2, :]
        accb_ref[0, 0:1, :] += _colsum(dh1)
        accb_ref[0, 1:2, :] += _colsum(dh1 * (n1 * g1))
        accg_ref[0:1, :] += _colsum((dh1 * scale1) * n1)
        gx_ref[...] = dx1_ref[...] + _norm_bwd((dh1 * scale1) * g1, n1, r1)

    tok = lambda i: (i, 0)
    const2 = lambda i: (0, 0)
    hb = tm // HALO
    last = t_all // HALO - 1
    return pl.pallas_call(
        body, name="inproj_bwd", grid=(t_all // tm,),
        in_specs=[pl.BlockSpec((tm, p), tok), pl.BlockSpec((tm, p), tok), pl.BlockSpec((tm, p), tok),
                  pl.BlockSpec((tm, p), tok),
                  pl.BlockSpec((HALO, p), lambda i: (jnp.minimum((i + 1) * hb, last), 0)),
                  pl.BlockSpec((tm, d), tok), pl.BlockSpec((tm, d), tok),
                  pl.BlockSpec((1, MOD_ROWS, d), lambda i: (i // nt, 0, 0)),
                  pl.BlockSpec((1, d), const2),
                  pl.BlockSpec((N_CHIPS, d, p), lambda i: (0, 0, 0))],
        out_specs=[pl.BlockSpec((tm, d), tok), pl.BlockSpec((tm, p), tok),
                   pl.BlockSpec((1, 8, d), lambda i: (i // nt, 0, 0)),
                   pl.BlockSpec((8, d), const2)],
        out_shape=[jax.ShapeDtypeStruct((t_all, d), F32), jax.ShapeDtypeStruct((t_all, p), BF16),
                   jax.ShapeDtypeStruct((t_all // seq, 8, d), F32),
                   jax.ShapeDtypeStruct((8, d), F32)],
        compiler_params=_params(("arbitrary",)),
    )(dq, dk, dv, dpd, dpd, x, dx1, mod, g_pre, w_in)


def _tn_matmul(x, ys, nk, bt, name):
    t_all = x.shape[-2]
    m = x.shape[-1]
    ny = len(ys)

    def spec(arr):
        if arr.ndim == 3:
            return pl.BlockSpec((1, bt, arr.shape[-1]), lambda k, t: (k, t, 0))
        return pl.BlockSpec((bt, arr.shape[-1]), lambda k, t: (t, 0))

    def tile(ref):
        return ref[0] if len(ref.shape) == 3 else ref[...]

    def body(*refs):
        x_ref, y_refs, o_refs = refs[0], refs[1:1 + ny], refs[1 + ny:]
        t = pl.program_id(1)
        xt = tile(x_ref)
        for y_ref, o_ref in zip(y_refs, o_refs):
            part = _dot_tn(xt, tile(y_ref))

            @pl.when(t == 0)
            def _(o_ref=o_ref, part=part):
                o_ref[0] = part

            @pl.when(t > 0)
            def _(o_ref=o_ref, part=part):
                o_ref[0] += part

    return pl.pallas_call(
        body, name=name, grid=(nk, t_all // bt),
        in_specs=[spec(x)] + [spec(y) for y in ys],
        out_specs=[pl.BlockSpec((1, m, y.shape[-1]), lambda k, t: (k, 0, 0)) for y in ys],
        out_shape=[jax.ShapeDtypeStruct((nk, m, y.shape[-1]), F32) for y in ys],
        compiler_params=_params(("arbitrary", "arbitrary")),
    )(x, *ys)


def _cond_fwd(c_all, w_q, b_q, bn):
    nrow, d = c_all.shape
    ncol = w_q.shape[1]

    def body(c_ref, w_ref, b_ref, sc_ref, mod_ref):
        cf = c_ref[...]
        sc = cf * _sigmoid(cf)
        sc_ref[...] = sc
        shi, slo = _split(sc)
        whi, wlo = _split(w_ref[...])
        mod_ref[...] = (_dot(shi, whi) + _dot(shi, wlo) + _dot(slo, whi)) + b_ref[...]

    return pl.pallas_call(
        body, name="cond_fwd", grid=(ncol // bn,),
        in_specs=[pl.BlockSpec((nrow, d), lambda n: (0, 0)), pl.BlockSpec((d, bn), lambda n: (0, n)),
                  pl.BlockSpec((1, bn), lambda n: (0, n))],
        out_specs=[pl.BlockSpec((nrow, d), lambda n: (0, 0)), pl.BlockSpec((nrow, bn), lambda n: (0, n))],
        out_shape=[jax.ShapeDtypeStruct((nrow, d), F32), jax.ShapeDtypeStruct((nrow, ncol), F32)],
        compiler_params=_params(("arbitrary",)),
    )(c_all, w_q, b_q)


def _cond_bwd(sc_all, dmod_q, bn):
    nrow, d = sc_all.shape
    ncol = dmod_q.shape[1]

    def body(sc_ref, dm_ref, gw_ref):
        shi, slo = _split(sc_ref[...])
        dhi, dlo = _split(dm_ref[...])
        gw_ref[...] = _dot_tn(shi, dhi) + _dot_tn(shi, dlo) + _dot_tn(slo, dhi)

    return pl.pallas_call(
        body, name="cond_bwd", grid=(ncol // bn,),
        in_specs=[pl.BlockSpec((nrow, d), lambda n: (0, 0)), pl.BlockSpec((nrow, bn), lambda n: (0, n))],
        out_specs=pl.BlockSpec((d, bn), lambda n: (0, n)),
        out_shape=jax.ShapeDtypeStruct((d, ncol), F32),
        compiler_params=_params(("arbitrary",)),
    )(sc_all, dmod_q)


def _row_block(rows, cols, budget=1 << 18):
    best = None
    for br in range(8, rows + 1, 8):
        if rows % br == 0 and br * cols <= budget:
            best = br
    return best if best is not None else rows


def _adamw(w, g, m, v, name):
    rows, cols = w.shape
    br = _row_block(rows, cols)
    c1 = 1.0 - ADAM_B1 ** ADAM_STEP
    c2 = 1.0 - ADAM_B2 ** ADAM_STEP

    def body(w_ref, g_ref, m_ref, v_ref, d_ref, nm_ref, nv_ref):
        gf = g_ref[...]
        m2 = ADAM_B1 * m_ref[...] + (1.0 - ADAM_B1) * gf
        v2 = ADAM_B2 * v_ref[...] + (1.0 - ADAM_B2) * (gf * gf)
        nm_ref[...] = m2
        nv_ref[...] = v2
        d_ref[...] = -ADAM_LR * ((m2 / c1) / (jnp.sqrt(v2 / c2) + ADAM_EPS) + ADAM_WD * w_ref[...])

    blk = pl.BlockSpec((br, cols), lambda i: (i, 0))
    return pl.pallas_call(
        body, name=name, grid=(rows // br,),
        in_specs=[blk] * 4, out_specs=[blk] * 3,
        out_shape=[jax.ShapeDtypeStruct((rows, cols), F32)] * 3,
        compiler_params=_params(("arbitrary",)),
    )(w, g, m, v)


def _all_gather(x_shard, name):
    m_per, n = x_shard.shape

    def body(x_ref, out_ref, send_sems, recv_sems, local_sem):
        x, y, c = _position()
        me, sibling = (x, y, c), (x, y, 1 - c)
        chips = [(1 - x, y), (x, 1 - y), (1 - x, 1 - y)]

        def rows(px, py, pc):
            return out_ref.at[pl.ds((4 * px + 2 * py + pc) * m_per, m_per), :]

        def copy(k, block, to, src=None):
            return pltpu.make_async_remote_copy(
                src_ref=rows(*block) if src is None else src, dst_ref=rows(*block),
                send_sem=send_sems.at[k], recv_sem=recv_sems.at[k], device_id=to, device_id_type=MESH)

        mine = pltpu.make_async_copy(x_ref, rows(*me), local_sem)
        mine.start()
        first = [copy(0, me, sibling, src=x_ref)]
        first += [copy(1 + j, me, (*chip, c), src=x_ref) for j, chip in enumerate(chips)]
        for cp in first:
            cp.start()
        passed = [copy(4 + j, (*chip, c), sibling) for j, chip in enumerate(chips)]
        for j, chip in enumerate(chips):
            copy(1 + j, (*chip, c), me).wait_recv()
            passed[j].start()
        copy(0, sibling, me).wait_recv()
        for j, chip in enumerate(chips):
            copy(4 + j, (*chip, 1 - c), me).wait_recv()
        for cp in first + passed:
            cp.wait_send()
        mine.wait()

    return pl.pallas_call(
        body, name=name,
        out_shape=jax.ShapeDtypeStruct((N_DEV * m_per, n), x_shard.dtype),
        in_specs=[pl.BlockSpec(memory_space=pltpu.VMEM)],
        out_specs=pl.BlockSpec(memory_space=pltpu.VMEM),
        scratch_shapes=[pltpu.SemaphoreType.DMA((7,)), pltpu.SemaphoreType.DMA((7,)), pltpu.SemaphoreType.DMA],
        compiler_params=pltpu.CompilerParams(vmem_limit_bytes=VMEM_LIMIT),
    )(x_shard)


_ANY = pl.BlockSpec(memory_space=pl.ANY)


def _gather_weights(quarters):
    n = len(quarters)
    shapes = [q.shape for q in quarters]

    def body(*refs):
        w_refs, g_refs = refs[:n], refs[n:2 * n]
        send_sems, recv_sems, local_sems = refs[2 * n:]
        x, y, c = _position()
        sibling = (x, y, 1 - c)
        chips = [(1 - x, y), (x, 1 - y), (1 - x, 1 - y)]
        mine = 2 * x + y

        def half(a, which):
            hr = shapes[a][0] // 2
            return pl.ds(which * hr, hr)

        def over_ici(a, p, slot, src=None):
            dst = g_refs[a].at[slot, half(a, c), :]
            return pltpu.make_async_remote_copy(
                src_ref=dst if src is None else src, dst_ref=dst,
                send_sem=send_sems.at[6 * a + p], recv_sem=recv_sems.at[6 * a + p],
                device_id=(*chips[p], c), device_id_type=MESH)

        def over_d2d(a, p, slot, which):
            ref = g_refs[a].at[slot, half(a, which), :]
            return pltpu.make_async_remote_copy(
                src_ref=ref, dst_ref=ref,
                send_sem=send_sems.at[6 * a + 3 + p], recv_sem=recv_sems.at[6 * a + 3 + p],
                device_id=sibling, device_id_type=MESH)

        local = [pltpu.make_async_copy(w_refs[a], g_refs[a].at[mine], local_sems.at[a]) for a in range(n)]
        for cp in local:
            cp.start()
        sends = []
        for a in range(n):
            for p in range(3):
                cp = over_ici(a, p, mine, src=w_refs[a].at[half(a, c), :])
                cp.start()
                sends.append(cp)
        for a in range(n):
            for p, (cx, cy) in enumerate(chips):
                slot = 2 * cx + cy
                over_ici(a, p, slot).wait_recv()
                cp = over_d2d(a, p, slot, c)
                cp.start()
                sends.append(cp)
        for a in range(n):
            for p, (cx, cy) in enumerate(chips):
                over_d2d(a, p, 2 * cx + cy, 1 - c).wait_recv()
        for cp in sends:
            cp.wait_send()
        for cp in local:
            cp.wait()

    return pl.pallas_call(
        body, name="gather_weights",
        out_shape=[jax.ShapeDtypeStruct((N_CHIPS,) + s, BF16) for s in shapes],
        in_specs=[_ANY] * n, out_specs=[_ANY] * n,
        scratch_shapes=[pltpu.SemaphoreType.DMA((6 * n,)), pltpu.SemaphoreType.DMA((6 * n,)),
                        pltpu.SemaphoreType.DMA((n,))],
    )(*quarters)


def _sibling_exchange(grads):
    n = len(grads)
    shapes = [g.shape for g in grads]

    def body(*refs):
        g_refs, x_refs = refs[:n], refs[n:2 * n]
        send_sems, recv_sems = refs[2 * n:]
        x, y, c = _position()
        copies = []
        for a in range(n):
            hr = shapes[a][1] // 2
            cp = pltpu.make_async_remote_copy(
                src_ref=g_refs[a].at[:, pl.ds((1 - c) * hr, hr), :], dst_ref=x_refs[a],
                send_sem=send_sems.at[a], recv_sem=recv_sems.at[a],
                device_id=(x, y, 1 - c), device_id_type=MESH)
            cp.start()
            copies.append(cp)
        for cp in copies:
            cp.wait()

    return pl.pallas_call(
        body, name="grad_sibling_exchange",
        out_shape=[jax.ShapeDtypeStruct((s[0], s[1] // 2, s[2]), F32) for s in shapes],
        in_specs=[_ANY] * n, out_specs=[_ANY] * n,
        scratch_shapes=[pltpu.SemaphoreType.DMA((n,)), pltpu.SemaphoreType.DMA((n,))],
    )(*grads)


def _chip_sums(core, grads, theirs):
    n = len(grads)

    def body(core_ref, *refs):
        g_refs, t_refs, o_refs = refs[:n], refs[n:2 * n], refs[2 * n:]
        for g_ref, t_ref, o_ref in zip(g_refs, t_refs, o_refs):
            o_ref[...] = (g_ref[...] + t_ref[...]).astype(BF16)

    in_specs = [pl.BlockSpec((1, g.shape[1] // 2, g.shape[2]), lambda k, core_ref: (k, core_ref[0], 0)) for g in grads]
    in_specs += [pl.BlockSpec((1,) + t.shape[1:], lambda k, core_ref: (k, 0, 0)) for t in theirs]
    return pl.pallas_call(
        body, name="grad_chip_sums",
        grid_spec=pltpu.PrefetchScalarGridSpec(
            num_scalar_prefetch=1, grid=(N_CHIPS,), in_specs=in_specs,
            out_specs=[pl.BlockSpec((1,) + t.shape[1:], lambda k, core_ref: (k, 0, 0)) for t in theirs]),
        out_shape=[jax.ShapeDtypeStruct(t.shape, BF16) for t in theirs],
        compiler_params=_params(("arbitrary",)),
    )(core, *grads, *theirs)


def _chip_exchange(sums):
    n = len(sums)

    def body(*refs):
        s_refs, y_refs = refs[:n], refs[n:2 * n]
        send_sems, recv_sems, local_sems = refs[2 * n:]
        x, y, c = _position()
        chips = [(1 - x, y), (x, 1 - y), (1 - x, 1 - y)]
        mine = 2 * x + y
        local = [pltpu.make_async_copy(s_refs[a].at[mine], y_refs[a].at[mine], local_sems.at[a]) for a in range(n)]
        for cp in local:
            cp.start()
        copies = []
        for a in range(n):
            for p, (cx, cy) in enumerate(chips):
                cp = pltpu.make_async_remote_copy(
                    src_ref=s_refs[a].at[2 * cx + cy], dst_ref=y_refs[a].at[mine],
                    send_sem=send_sems.at[3 * a + p], recv_sem=recv_sems.at[3 * a + p],
                    device_id=(cx, cy, c), device_id_type=MESH)
                cp.start()
                copies.append((cp, a, 2 * cx + cy, p))
        for cp, a, slot, p in copies:
            cp.wait_send()
            pltpu.make_async_remote_copy(
                src_ref=s_refs[a].at[slot], dst_ref=y_refs[a].at[slot],
                send_sem=send_sems.at[3 * a + p], recv_sem=recv_sems.at[3 * a + p],
                device_id=(x, y, c), device_id_type=MESH).wait_recv()
        for cp in local:
            cp.wait()

    return pl.pallas_call(
        body, name="grad_chip_exchange",
        out_shape=[jax.ShapeDtypeStruct(s.shape, BF16) for s in sums],
        in_specs=[_ANY] * n, out_specs=[_ANY] * n,
        scratch_shapes=[pltpu.SemaphoreType.DMA((3 * n,)), pltpu.SemaphoreType.DMA((3 * n,)),
                        pltpu.SemaphoreType.DMA((n,))],
    )(*sums)


def _total_sums(parts):
    n = len(parts)

    def body(*refs):
        for y_ref, o_ref in zip(refs[:n], refs[n:]):
            o_ref[...] = ((y_ref[0].astype(F32) + y_ref[1].astype(F32)) + y_ref[2].astype(F32)) + y_ref[3].astype(F32)

    rows = 2

    def specs(pt):
        hr, cols = pt.shape[1], pt.shape[2]
        step = hr // rows
        return (pl.BlockSpec((N_CHIPS, step, cols), lambda r: (0, r, 0)), pl.BlockSpec((step, cols), lambda r: (r, 0)))

    return pl.pallas_call(
        body, name="grad_total_sums", grid=(rows,),
        in_specs=[specs(pt)[0] for pt in parts], out_specs=[specs(pt)[1] for pt in parts],
        out_shape=[jax.ShapeDtypeStruct(pt.shape[1:], F32) for pt in parts],
        compiler_params=_params(("arbitrary",)),
    )(*parts)


def _sibling_share(halves):
    n = len(halves)

    def body(*refs):
        t_refs, f_refs = refs[:n], refs[n:2 * n]
        send_sems, recv_sems, local_sems = refs[2 * n:]
        x, y, c = _position()
        copies, local = [], []
        for a in range(n):
            hr = halves[a].shape[0]
            dst = f_refs[a].at[pl.ds(c * hr, hr), :]
            lc = pltpu.make_async_copy(t_refs[a], dst, local_sems.at[a])
            lc.start()
            local.append(lc)
            cp = pltpu.make_async_remote_copy(
                src_ref=t_refs[a], dst_ref=dst, send_sem=send_sems.at[a], recv_sem=recv_sems.at[a],
                device_id=(x, y, 1 - c), device_id_type=MESH)
            cp.start()
            copies.append(cp)
        for a, cp in enumerate(copies):
            hr = halves[a].shape[0]
            cp.wait_send()
            other = f_refs[a].at[pl.ds((1 - c) * hr, hr), :]
            pltpu.make_async_remote_copy(
                src_ref=other, dst_ref=other, send_sem=send_sems.at[a], recv_sem=recv_sems.at[a],
                device_id=(x, y, c), device_id_type=MESH).wait_recv()
        for lc in local:
            lc.wait()

    return pl.pallas_call(
        body, name="grad_sibling_share",
        out_shape=[jax.ShapeDtypeStruct((2 * h.shape[0], h.shape[1]), F32) for h in halves],
        in_specs=[_ANY] * n, out_specs=[_ANY] * n,
        scratch_shapes=[pltpu.SemaphoreType.DMA((n,)), pltpu.SemaphoreType.DMA((n,)), pltpu.SemaphoreType.DMA((n,))],
    )(*halves)


def _group_sum(stacked, nrow, name):
    total, n = stacked.shape
    groups = total // nrow

    def body(g_ref, o_ref):
        acc = g_ref[0:nrow, :]
        for grp in range(1, groups):
            acc = acc + g_ref[grp * nrow:(grp + 1) * nrow, :]
        o_ref[...] = acc

    return pl.pallas_call(
        body, name=name,
        out_shape=jax.ShapeDtypeStruct((nrow, n), F32),
        compiler_params=pltpu.CompilerParams(vmem_limit_bytes=VMEM_LIMIT),
    )(stacked)


def _local_step(xt, tgt, mod, gains, w_pool, pool_scale, weights, seq):
    g_mpre, g_mpost, g_fpre, g_fpost = gains
    w_in, w_out, w_g, w_u, w_d = weights
    d = xt.shape[1]
    tm, tq = min(TOKEN_TILE, seq), min(ATTN_TILE, seq)
    w_out2 = w_out.reshape(d, d)

    h1, qn, k, v, u, kt, vt = _prenorm_proj(xt, mod, g_mpre, w_in, seq, tm)
    tk = min(ATTN_KEY_TILE, tq // 2)
    o, ltot = _attn_fwd(qn, k, vt, seq, tq, tk)
    pooled, mixin, mix, x1, h2 = _mixer_post(u, o, xt, mod, g_mpost, g_fpre, w_pool, pool_scale, w_out2, seq, tm)
    a, b, fin, dy, df, loss_blk, accb4, accg4 = _ffn_fwd(h2, w_g, w_u, w_d, x1, tgt, mod, g_fpost, seq, tm)
    da, db, dx1, dmix, accb5, accg5 = _ffn_bwd(df, a, b, w_d, w_g, w_u, x1, dy, mix, mod, g_fpre, g_mpost, seq, tm)
    do, dpd, dps, dwp = _mixer_bwd(dmix, w_out2, pooled, w_pool, pool_scale, seq, tm)
    dq, dk, dv = _attn_bwd(qn, k, kt, v, do, ltot, seq, tq, tk)
    gx, du, accb8, accg8 = _inproj_bwd(dq, dk, dv, dpd, xt, dx1, mod, g_mpre, w_in, seq, tm)

    g_in = jnp.concatenate(_tn_matmul(h1, [dq, dk, dv, du], 1, tm, "grad_w_in"), axis=0)
    g_out = _tn_matmul(mixin, [dmix], 1, tm, "grad_w_out")[0].reshape(w_out.shape)
    g_g, g_u = _tn_matmul(h2, [da, db], w_g.shape[0], tm, "grad_w_gate_up")
    (g_d,) = _tn_matmul(fin, [df], w_d.shape[0], tm, "grad_w_down")

    dmod = jnp.stack([accb8[:, 0], accb8[:, 1], accb5[:, 2], accb5[:, 0], accb5[:, 1], accb4[:, 0]], axis=1)
    dgain = jnp.stack([accg8[0], accg5[1], accg5[0], accg4[0]], axis=0)
    return loss_blk, gx, [g_in, g_out, g_g, g_u, g_d], dmod, dgain, dps[0:1], dwp


def kernel(x, c, w_cond, b_cond, g_mix_pre, g_mix_post, w_in, w_pool, pool_scale, w_out, g_ffn_pre, g_ffn_post, w_gate, w_up, w_down, loss_target, m_w_cond, m_b_cond, m_g_mix_pre, m_g_mix_post, m_w_in, m_w_pool, m_pool_scale, m_w_out, m_g_ffn_pre, m_g_ffn_post, m_w_gate, m_w_up, m_w_down, v_w_cond, v_b_cond, v_g_mix_pre, v_g_mix_post, v_w_in, v_w_pool, v_pool_scale, v_w_out, v_g_ffn_pre, v_g_ffn_post, v_w_gate, v_w_up, v_w_down):
    xi, yi, ci = _position()
    chip = 2 * xi + yi
    dev = 4 * xi + 2 * yi + ci
    nb, seq, d = x.shape
    t_all = nb * seq
    xt = x.reshape(t_all, d)
    tgt = loss_target.reshape(t_all, d)
    ncol = w_cond.shape[2]
    pw = pool_scale.shape[1]

    c_pad = jnp.concatenate([c, jnp.zeros((8 - nb, d), F32)], axis=0)
    c_all = _all_gather(c_pad, "gather_c").reshape(N_DEV, 8, d)[:, :nb].reshape(N_DEV * nb, d)
    b_q = lax.dynamic_slice(b_cond, (0, chip * ncol), (1, ncol))
    sc_all, mod_q = _cond_fwd(c_all, w_cond[0], b_q, 512)
    mod_parts = _all_gather(mod_q, "gather_mod").reshape(N_DEV, N_DEV * nb, ncol)
    mod_rows = lax.dynamic_slice(mod_parts, (0, dev * nb, 0), (N_DEV, nb, ncol))[0::2]
    mod = jnp.transpose(mod_rows, (1, 0, 2)).reshape(nb, N_MOD, d)
    mod = jnp.concatenate([mod, jnp.zeros((nb, MOD_ROWS - N_MOD, d), F32)], axis=1)

    quarters = [w[0].astype(BF16) for w in (w_in, w_out, w_gate, w_up, w_down)]
    weights = _gather_weights(quarters)

    gains = (g_mix_pre, g_mix_post, g_ffn_pre, g_ffn_post)
    loss_blk, gx, grads, dmod, dgain, dps, dwp = _local_step(xt, tgt, mod, gains, w_pool[0], pool_scale, weights, seq)
    loss = lax.psum(loss_blk[0, 0], ("x", "y", "c"))

    theirs = _sibling_exchange(grads)
    sums = _chip_sums(jnp.reshape(ci, (1,)).astype(jnp.int32), grads, theirs)
    parts = _chip_exchange(sums)
    halves = _total_sums(parts)
    g_big = _sibling_share(halves)

    wp_rows = dwp.size // d
    pad_rows = 24 - (2 * N_MOD + 4 + 1)
    payload = jnp.concatenate([
        dmod.reshape(nb * N_MOD, d), dgain,
        jnp.concatenate([dps, jnp.zeros((1, d - pw), F32)], axis=1),
        jnp.zeros((pad_rows, d), F32), dwp.reshape(wp_rows, d)], axis=0)
    prow = payload.shape[0]
    gathered = _all_gather(payload, "gather_small")
    summed = _group_sum(gathered, prow, "small_device_sum")
    dmod_all = gathered.reshape(N_DEV, prow, d)[:, :nb * N_MOD].reshape(N_DEV * nb, N_MOD * d)
    g_b_cond = _group_sum(dmod_all, 1, "grad_b_cond")
    dmod_q = lax.dynamic_slice(dmod_all, (0, chip * ncol), (N_DEV * nb, ncol))
    g_w_cond = _cond_bwd(sc_all, dmod_q, 512)
    first_gain = 2 * N_MOD
    g_gains = [summed[first_gain + r:first_gain + r + 1] for r in range(4)]
    g_pool_scale = summed[first_gain + 4:first_gain + 5, :pw]
    g_w_pool = summed[24:24 + wp_rows].reshape(w_pool.shape[1] * w_pool.shape[2], w_pool.shape[3])

    flat_pool = lambda t: t.reshape(g_w_pool.shape)
    plan = [
        ("w_cond", w_cond[0], g_w_cond, m_w_cond[0], v_w_cond[0], w_cond.shape),
        ("b_cond", b_cond, g_b_cond, m_b_cond, v_b_cond, b_cond.shape),
        ("g_mix_pre", g_mix_pre, g_gains[0], m_g_mix_pre, v_g_mix_pre, g_mix_pre.shape),
        ("g_mix_post", g_mix_post, g_gains[1], m_g_mix_post, v_g_mix_post, g_mix_post.shape),
        ("w_in", w_in[0], g_big[0], m_w_in[0], v_w_in[0], w_in.shape),
        ("w_pool", flat_pool(w_pool), g_w_pool, flat_pool(m_w_pool), flat_pool(v_w_pool), w_pool.shape),
        ("pool_scale", pool_scale, g_pool_scale, m_pool_scale, v_pool_scale, pool_scale.shape),
        ("w_out", w_out[0], g_big[1], m_w_out[0], v_w_out[0], w_out.shape),
        ("g_ffn_pre", g_ffn_pre, g_gains[2], m_g_ffn_pre, v_g_ffn_pre, g_ffn_pre.shape),
        ("g_ffn_post", g_ffn_post, g_gains[3], m_g_ffn_post, v_g_ffn_post, g_ffn_post.shape),
        ("w_gate", w_gate[0], g_big[2], m_w_gate[0], v_w_gate[0], w_gate.shape),
        ("w_up", w_up[0], g_big[3], m_w_up[0], v_w_up[0], w_up.shape),
        ("w_down", w_down[0], g_big[4], m_w_down[0], v_w_down[0], w_down.shape),
    ]
    out_g, out_d, out_m, out_v = [], [], [], []
    for name, w2, g2, m2, v2, shape in plan:
        delta, new_m, new_v = _adamw(w2, g2, m2, v2, "adamw_" + name)
        out_g.append(g2.reshape(shape))
        out_d.append(delta.reshape(shape))
        out_m.append(new_m.reshape(shape))
        out_v.append(new_v.reshape(shape))
    return (loss, gx.reshape(x.shape), *out_g, *out_d, *out_m, *out_v)
```

```python
import functools

import jax
import jax.numpy as jnp
from jax import lax
from jax.experimental import pallas as pl
from jax.experimental.pallas import tpu as pltpu

F32 = jnp.float32
BF16 = jnp.bfloat16
MESH = pl.DeviceIdType.MESH

EPS = 1e-6
HEAD_DIM = 64
HEADS_PER_BLOCK = 2
LANES = 128
NEG_QK_SCALE = -0.125
POOL_WINDOWS = (2, 4, 8, 16)
POOL_GROUP = 128
HALO = 16
N_MOD = 6
MOD_ROWS = 8
N_CHIPS = 4
N_DEV = 8
VMEM_LIMIT = 56 * 1024 * 1024

ADAM_LR = 0.001
ADAM_B1 = 0.9
ADAM_B2 = 0.999
ADAM_EPS = 1e-08
ADAM_WD = 0.01
ADAM_STEP = 10

TOKEN_TILE = 512
ATTN_TILE = 512
ATTN_KEY_TILE = 256
ATTN_ROW_CHUNK = 32


def _dot(a, b):
    return jnp.dot(a, b, preferred_element_type=F32)


def _dot_nt(a, b):
    return lax.dot_general(a, b, (((1,), (1,)), ((), ())), preferred_element_type=F32)


def _dot_tn(a, b):
    return lax.dot_general(a, b, (((0,), (0,)), ((), ())), preferred_element_type=F32)


def _split(v):
    hi = v.astype(BF16)
    lo = (v - hi.astype(F32)).astype(BF16)
    return hi, lo


def _rms(v):
    return lax.rsqrt(jnp.mean(v * v, axis=-1, keepdims=True) + EPS)


def _norm_bwd(dn, n, r):
    return r * (dn - n * jnp.mean(dn * n, axis=-1, keepdims=True))


def _sigmoid(v):
    return 1.0 / (1.0 + jnp.exp(-v))


def _colsum(v):
    return jnp.sum(v, axis=0, keepdims=True)


def _params(sem=None):
    return pltpu.CompilerParams(dimension_semantics=sem, vmem_limit_bytes=VMEM_LIMIT)


def _position():
    return lax.axis_index("x"), lax.axis_index("y"), lax.axis_index("c")


def _prenorm_proj(x, mod, g_pre, w_in, seq, tm):
    t_all, d = x.shape
    nt = seq // tm
    p = w_in.shape[2]

    def body(x_ref, mod_ref, g_ref, w_ref, h_ref, q_ref, k_ref, v_ref, u_ref, kt_ref, vt_ref):
        xf = x_ref[...]
        n = xf * _rms(xf)
        h = (n * g_ref[...]) * (1.0 + mod_ref[0, 1:2, :]) + mod_ref[0, 0:1, :]
        hb = h.astype(BF16)
        h_ref[...] = hb
        q_ref[...] = (_dot(hb, w_ref[0]) * NEG_QK_SCALE).astype(BF16)
        kf = _dot(hb, w_ref[1])
        vf = _dot(hb, w_ref[2])
        k_ref[...] = kf.astype(BF16)
        v_ref[...] = vf.astype(BF16)
        kt_ref[...] = kf.T.astype(BF16)
        vt_ref[...] = vf.T.astype(BF16)
        u_ref[...] = _dot(hb, w_ref[3])

    tok = lambda i: (i, 0)
    tok_t = lambda i: (0, i)
    return pl.pallas_call(
        body, name="prenorm_proj", grid=(t_all // tm,),
        in_specs=[pl.BlockSpec((tm, d), tok),
                  pl.BlockSpec((1, MOD_ROWS, d), lambda i: (i // nt, 0, 0)),
                  pl.BlockSpec((1, d), lambda i: (0, 0)),
                  pl.BlockSpec((N_CHIPS, d, p), lambda i: (0, 0, 0))],
        out_specs=[pl.BlockSpec((tm, d), tok)] + [pl.BlockSpec((tm, p), tok)] * 4 + [pl.BlockSpec((p, tm), tok_t)] * 2,
        out_shape=[jax.ShapeDtypeStruct((t_all, d), BF16)] + [jax.ShapeDtypeStruct((t_all, p), BF16)] * 3
        + [jax.ShapeDtypeStruct((t_all, p), F32)] + [jax.ShapeDtypeStruct((p, t_all), BF16)] * 2,
        compiler_params=_params(("arbitrary",)),
    )(x, mod, g_pre, w_in)


def _tri_matrix(tk, kind):
    j = lax.broadcasted_iota(jnp.int32, (2 * tk, tk), 0) % tk
    s = lax.broadcasted_iota(jnp.int32, (2 * tk, tk), 1)
    return {"after": j > s, "upto": j <= s, "before": j < s}[kind].astype(BF16)


def _row_sums(v):
    return jnp.broadcast_to(jnp.sum(v, axis=-1, keepdims=True), (v.shape[0], LANES))


def _across(v, n):
    return jnp.concatenate([v] * (n // LANES), axis=1)


def _all_masked(c, diag, rc, tk):
    return diag is not None and diag * tk >= (c + 1) * rc - 1


def _some_masked(c, diag, rc, tk):
    return diag is not None and diag * tk + tk - 1 >= c * rc


def _attn_fwd(qn, k, vt, seq, tq, tk):
    t_all, w = qn.shape
    nb, nq, ndiag = t_all // seq, seq // tq, tq // tk
    assert ndiag % 2 == 0, "two key blocks per loop trip"
    rc = ATTN_ROW_CHUNK
    heads = range(HEADS_PER_BLOCK)

    def body(q_ref, k_ref, vt_ref, tri_ref, o_ref, l_ref,
             z_buf, ls_buf, hl_buf, aft_buf, w_buf, tot_buf, acc_t, run_buf):
        i = pl.program_id(2)
        nblk = (i + 1) * ndiag
        lane = lax.broadcasted_iota(jnp.int32, (1, LANES), 1)
        sub = lax.broadcasted_iota(jnp.int32, (LANES, 1), 0)
        row = lax.broadcasted_iota(jnp.int32, (rc, tk), 0)
        col = lax.broadcasted_iota(jnp.int32, (rc, tk), 1)
        first = lane < HEAD_DIM
        q2 = q_ref[...]
        qs = [jnp.where(first, q2, jnp.zeros_like(q2)), jnp.where(first, jnp.zeros_like(q2), q2)]
        acc_t[...] = jnp.zeros_like(acc_t)
        run_buf[...] = jnp.zeros_like(run_buf)
        w_buf[1] = jnp.zeros((HEADS_PER_BLOCK, tq, tk), BF16)

        def causal(c, diag):
            return (col + diag * tk) < (row + c * rc)

        def scores(blk, slot):
            kj = k_ref[pl.ds(pl.multiple_of(blk * tk, tk), tk), :]
            for h in heads:
                z_buf[slot, h] = _dot_nt(qs[h], kj)

        def values(blk, slot):
            vtj = vt_ref[:, pl.ds(pl.multiple_of(blk * tk, tk), tk)]
            zero = jnp.zeros_like(vtj)
            acc_t[...] += (_dot_nt(jnp.where(sub < HEAD_DIM, vtj, zero), w_buf[slot, 0])
                           + _dot_nt(jnp.where(sub < HEAD_DIM, zero, vtj), w_buf[slot, 1]))

        def softplus_stage(h, slot, diag):
            for c in range(tq // rc):
                rows = slice(c * rc, (c + 1) * rc)
                if _all_masked(c, diag, rc, tk):
                    hl_buf[h, rows, :] = jnp.zeros((rc, 2 * tk), BF16)
                    tot_buf[h, rows, :] = jnp.zeros((rc, LANES), F32)
                    continue
                nz = z_buf[slot, h, rows, :]
                l1 = jnp.minimum(nz, 0.0) - jnp.log(1.0 + jnp.exp(-jnp.abs(nz)))
                if _some_masked(c, diag, rc, tk):
                    l1 = jnp.where(causal(c, diag), l1, 0.0)
                hi, lo = _split(l1)
                hl_buf[h, rows, 0:tk] = hi
                hl_buf[h, rows, tk:2 * tk] = lo
                ls_buf[h, rows, :] = l1 - nz
                tot_buf[h, rows, :] = _row_sums(l1)

        def weights_stage(h, slot, diag):
            for c in range(tq // rc):
                rows = slice(c * rc, (c + 1) * rc)
                if _all_masked(c, diag, rc, tk):
                    w_buf[slot, h, rows, :] = jnp.zeros((rc, tk), BF16)
                    continue
                wgt = jnp.exp((ls_buf[h, rows, :] + aft_buf[h, rows, :]) + _across(run_buf[h, rows, :], tk))
                if _some_masked(c, diag, rc, tk):
                    wgt = jnp.where(causal(c, diag), wgt, 0.0)
                w_buf[slot, h, rows, :] = wgt.astype(BF16)
                run_buf[h, rows, :] += tot_buf[h, rows, :]

        def position(blk, slot, diag):
            scores(jnp.maximum(blk - 1, 0), 1 - slot)
            for h in heads:
                softplus_stage(h, slot, diag)
                aft_buf[h] = _dot(hl_buf[h], tri_ref[...])
            values(jnp.minimum(blk + 1, nblk - 1), 1 - slot)
            for h in heads:
                weights_stage(h, slot, diag)

        scores(nblk - 1, 0)
        for p in range(ndiag):
            position(nblk - 1 - p, p % 2, ndiag - 1 - p)

        def trip(jj, carry):
            for u in range(2):
                position(i * ndiag - 1 - 2 * jj - u, u, None)
            return carry

        lax.fori_loop(0, (i * ndiag) // 2, trip, 0)
        values(0, 1)
        o_ref[...] = acc_t[...].T.astype(BF16)
        l_ref[...] = jnp.where(first, run_buf[0], run_buf[1])

    qmap = lambda b, hp, i: (b * nq + i, hp)
    nh = HEADS_PER_BLOCK
    return pl.pallas_call(
        body, name="attn_fwd", grid=(nb, w // LANES, nq),
        in_specs=[pl.BlockSpec((tq, LANES), qmap), pl.BlockSpec((seq, LANES), lambda b, hp, i: (b, hp)),
                  pl.BlockSpec((LANES, seq), lambda b, hp, i: (hp, b)),
                  pl.BlockSpec((2 * tk, tk), lambda b, hp, i: (0, 0))],
        out_specs=[pl.BlockSpec((tq, LANES), qmap), pl.BlockSpec((tq, LANES), qmap)],
        out_shape=[jax.ShapeDtypeStruct((t_all, w), BF16), jax.ShapeDtypeStruct((t_all, w), F32)],
        scratch_shapes=[pltpu.VMEM((2, nh, tq, tk), F32), pltpu.VMEM((nh, tq, tk), F32),
                        pltpu.VMEM((nh, tq, 2 * tk), BF16), pltpu.VMEM((nh, tq, tk), F32),
                        pltpu.VMEM((2, nh, tq, tk), BF16), pltpu.VMEM((nh, tq, LANES), F32),
                        pltpu.VMEM((LANES, tq), F32), pltpu.VMEM((nh, tq, LANES), F32)],
        compiler_params=_params(("arbitrary", "arbitrary", "arbitrary")),
    )(qn, k, vt, _tri_matrix(tk, "after"))


def _window_sums(ext, rows, offset, forward):
    r = lax.broadcasted_iota(jnp.int32, (rows, rows + HALO), 0)
    e = lax.broadcasted_iota(jnp.int32, (rows, rows + HALO), 1)
    hi, lo = _split(ext)
    out = []
    for g, win in enumerate(POOL_WINDOWS):
        if forward:
            band = (e >= r) & (e < r + win)
        else:
            band = (e <= r + offset) & (e > r + offset - win)
        bm = band.astype(BF16)
        cols = slice(g * POOL_GROUP, (g + 1) * POOL_GROUP)
        out.append(_dot(bm, hi[:, cols]) + _dot(bm, lo[:, cols]))
    return out


def _window_counts(pos):
    return [jnp.minimum(pos + 1, win).astype(F32) for win in POOL_WINDOWS]


def _mixer_post(u, o, x, mod, g_post, g_fpre, w_pool, pool_scale, w_out, seq, tm):
    t_all, d = x.shape
    nt = seq // tm
    p = u.shape[1]

    def body(u_ref, halo_ref, o_ref, x_ref, mod_ref, gp_ref, gf_ref, wp_ref, ps_ref, wo_ref,
             pooled_ref, mixin_ref, mix_ref, x1_ref, h2_ref):
        it = pl.program_id(0) % nt
        uf = u_ref[...]
        halo = jnp.where(it == 0, 0.0, halo_ref[...])
        ext = jnp.concatenate([halo, uf], axis=0)
        pos = it * tm + lax.broadcasted_iota(jnp.int32, (tm, 1), 0)
        sums = _window_sums(ext, tm, HALO, False)
        cnts = _window_counts(pos)
        pools = []
        for g in range(len(POOL_WINDOWS)):
            cols = slice(g * POOL_GROUP, (g + 1) * POOL_GROUP)
            pooled = (sums[g] / cnts[g] - uf[:, cols]).astype(BF16)
            pooled_ref[:, cols] = pooled
            yg = _dot(pooled, wp_ref[g].astype(BF16))
            pools.append((yg * ps_ref[:, cols]).astype(BF16))
        mixin = jnp.concatenate([o_ref[...]] + pools, axis=1)
        mixin_ref[...] = mixin
        mix = _dot(mixin, wo_ref[...])
        mix_ref[...] = mix
        n2 = mix * _rms(mix)
        x1 = x_ref[...] + mod_ref[0, 2:3, :] * (n2 * gp_ref[...])
        x1_ref[...] = x1
        n3 = x1 * _rms(x1)
        h2 = (n3 * gf_ref[...]) * (1.0 + mod_ref[0, 4:5, :]) + mod_ref[0, 3:4, :]
        h2_ref[...] = h2.astype(BF16)

    tok = lambda i: (i, 0)
    const2 = lambda i: (0, 0)
    hb = tm // HALO
    return pl.pallas_call(
        body, name="mixer_post", grid=(t_all // tm,),
        in_specs=[pl.BlockSpec((tm, p), tok),
                  pl.BlockSpec((HALO, p), lambda i: (jnp.maximum(i * hb - 1, 0), 0)),
                  pl.BlockSpec((tm, p), tok),
                  pl.BlockSpec((tm, d), tok),
                  pl.BlockSpec((1, MOD_ROWS, d), lambda i: (i // nt, 0, 0)),
                  pl.BlockSpec((1, d), const2), pl.BlockSpec((1, d), const2),
                  pl.BlockSpec(w_pool.shape, lambda i: (0, 0, 0)),
                  pl.BlockSpec((1, p), const2),
                  pl.BlockSpec((d, d), const2)],
        out_specs=[pl.BlockSpec((tm, p), tok), pl.BlockSpec((tm, d), tok), pl.BlockSpec((tm, d), tok),
                   pl.BlockSpec((tm, d), tok), pl.BlockSpec((tm, d), tok)],
        out_shape=[jax.ShapeDtypeStruct((t_all, p), BF16), jax.ShapeDtypeStruct((t_all, d), BF16),
                   jax.ShapeDtypeStruct((t_all, d), F32), jax.ShapeDtypeStruct((t_all, d), F32),
                   jax.ShapeDtypeStruct((t_all, d), BF16)],
        compiler_params=_params(("arbitrary",)),
    )(u, u, o, x, mod, g_post, g_fpre, w_pool, pool_scale, w_out)


def _ffn_fwd(h2, w_g, w_u, w_d, x1, tgt, mod, g_post, seq, tm):
    t_all, d = x1.shape
    nt = seq // tm
    nk, _, ff = w_g.shape

    def body(h_ref, wg_ref, wu_ref, wd_ref, x1_ref, t_ref, mod_ref, g_ref,
             a_ref, b_ref, fin_ref, dy_ref, df_ref, loss_ref, accb_ref, accg_ref, facc):
        i, k = pl.program_id(0), pl.program_id(1)
        hb = h_ref[...]
        a = _dot(hb, wg_ref[0])
        b = _dot(hb, wu_ref[0])
        a_ref[0] = a.astype(BF16)
        b_ref[0] = b.astype(BF16)
        fin = ((a * _sigmoid(a)) * b).astype(BF16)
        fin_ref[0] = fin
        part = _dot(fin, wd_ref[0])

        @pl.when(k == 0)
        def _():
            facc[...] = part

        @pl.when(k > 0)
        def _():
            facc[...] += part

        @pl.when(k == nk - 1)
        def _():
            f = facc[...]
            r4 = _rms(f)
            n4 = f * r4
            gate = mod_ref[0, 5:6, :]
            g = g_ref[...]
            err = (x1_ref[...] + gate * (n4 * g)) - t_ref[...]
            dy = err * (1.0 / d)
            dy_ref[...] = dy

            @pl.when(i == 0)
            def _():
                loss_ref[...] = jnp.zeros_like(loss_ref)
                accg_ref[...] = jnp.zeros_like(accg_ref)

            @pl.when(i % nt == 0)
            def _():
                accb_ref[...] = jnp.zeros_like(accb_ref)

            loss_ref[...] += (0.5 / d) * jnp.sum(err * err)
            accb_ref[0, 0:1, :] += _colsum(dy * (n4 * g))
            accg_ref[0:1, :] += _colsum((dy * gate) * n4)
            dn4 = (dy * gate) * g
            df_ref[...] = _norm_bwd(dn4, n4, r4).astype(BF16)

    tok = lambda i, k: (i, 0)
    ktok = lambda i, k: (k, i, 0)
    kw = lambda i, k: (k, 0, 0)
    const2 = lambda i, k: (0, 0)
    return pl.pallas_call(
        body, name="ffn_fwd", grid=(t_all // tm, nk),
        in_specs=[pl.BlockSpec((tm, d), tok),
                  pl.BlockSpec((1, d, ff), kw), pl.BlockSpec((1, d, ff), kw), pl.BlockSpec((1, ff, d), kw),
                  pl.BlockSpec((tm, d), tok), pl.BlockSpec((tm, d), tok),
                  pl.BlockSpec((1, MOD_ROWS, d), lambda i, k: (i // nt, 0, 0)),
                  pl.BlockSpec((1, d), const2)],
        out_specs=[pl.BlockSpec((1, tm, ff), ktok)] * 3
        + [pl.BlockSpec((tm, d), tok), pl.BlockSpec((tm, d), tok),
           pl.BlockSpec((8, LANES), const2),
           pl.BlockSpec((1, 8, d), lambda i, k: (i // nt, 0, 0)),
           pl.BlockSpec((8, d), const2)],
        out_shape=[jax.ShapeDtypeStruct((nk, t_all, ff), BF16)] * 3
        + [jax.ShapeDtypeStruct((t_all, d), F32), jax.ShapeDtypeStruct((t_all, d), BF16),
           jax.ShapeDtypeStruct((8, LANES), F32),
           jax.ShapeDtypeStruct((t_all // seq, 8, d), F32),
           jax.ShapeDtypeStruct((8, d), F32)],
        scratch_shapes=[pltpu.VMEM((tm, d), F32)],
        compiler_params=_params(("arbitrary", "arbitrary")),
    )(h2, w_g, w_u, w_d, x1, tgt, mod, g_post)


def _ffn_bwd(df, a, b, w_d, w_g, w_u, x1, dy, mix, mod, g_fpre, g_mpost, seq, tm):
    t_all, d = x1.shape
    nt = seq // tm
    nk, _, ff = w_g.shape

    def body(df_ref, a_ref, b_ref, wd_ref, wg_ref, wu_ref, x1_ref, dy_ref, mix_ref, mod_ref, gf_ref, gm_ref,
             da_ref, db_ref, dx1_ref, dmix_ref, accb_ref, accg_ref, hacc):
        i, k = pl.program_id(0), pl.program_id(1)
        dfin = _dot_nt(df_ref[...], wd_ref[0])
        af = a_ref[0].astype(F32)
        bf = b_ref[0].astype(F32)
        sig = _sigmoid(af)
        da = ((dfin * bf) * (sig * (1.0 + af * (1.0 - sig)))).astype(BF16)
        db = (dfin * (af * sig)).astype(BF16)
        da_ref[0] = da
        db_ref[0] = db
        part = _dot_nt(da, wg_ref[0]) + _dot_nt(db, wu_ref[0])

        @pl.when(k == 0)
        def _():
            hacc[...] = part

        @pl.when(k > 0)
        def _():
            hacc[...] += part

        @pl.when(k == nk - 1)
        def _():
            @pl.when(i == 0)
            def _():
                accg_ref[...] = jnp.zeros_like(accg_ref)

            @pl.when(i % nt == 0)
            def _():
                accb_ref[...] = jnp.zeros_like(accb_ref)

            dh2 = hacc[...]
            x1 = x1_ref[...]
            r3 = _rms(x1)
            n3 = x1 * r3
            g3 = gf_ref[...]
            scale1 = 1.0 + mod_ref[0, 4:5, :]
            accb_ref[0, 0:1, :] += _colsum(dh2)
            accb_ref[0, 1:2, :] += _colsum(dh2 * (n3 * g3))
            accg_ref[0:1, :] += _colsum((dh2 * scale1) * n3)
            dx1 = dy_ref[...] + _norm_bwd((dh2 * scale1) * g3, n3, r3)
            dx1_ref[...] = dx1
            mix = mix_ref[...]
            r2 = _rms(mix)
            n2 = mix * r2
            g2 = gm_ref[...]
            gate = mod_ref[0, 2:3, :]
            accb_ref[0, 2:3, :] += _colsum(dx1 * (n2 * g2))
            accg_ref[1:2, :] += _colsum((dx1 * gate) * n2)
            dmix_ref[...] = _norm_bwd((dx1 * gate) * g2, n2, r2).astype(BF16)

    tok = lambda i, k: (i, 0)
    ktok = lambda i, k: (k, i, 0)
    kw = lambda i, k: (k, 0, 0)
    const2 = lambda i, k: (0, 0)
    return pl.pallas_call(
        body, name="ffn_bwd", grid=(t_all // tm, nk),
        in_specs=[pl.BlockSpec((tm, d), tok),
                  pl.BlockSpec((1, tm, ff), ktok), pl.BlockSpec((1, tm, ff), ktok),
                  pl.BlockSpec((1, ff, d), kw), pl.BlockSpec((1, d, ff), kw), pl.BlockSpec((1, d, ff), kw),
                  pl.BlockSpec((tm, d), tok), pl.BlockSpec((tm, d), tok), pl.BlockSpec((tm, d), tok),
                  pl.BlockSpec((1, MOD_ROWS, d), lambda i, k: (i // nt, 0, 0)),
                  pl.BlockSpec((1, d), const2), pl.BlockSpec((1, d), const2)],
        out_specs=[pl.BlockSpec((1, tm, ff), ktok)] * 2
        + [pl.BlockSpec((tm, d), tok), pl.BlockSpec((tm, d), tok),
           pl.BlockSpec((1, 8, d), lambda i, k: (i // nt, 0, 0)),
           pl.BlockSpec((8, d), const2)],
        out_shape=[jax.ShapeDtypeStruct((nk, t_all, ff), BF16)] * 2
        + [jax.ShapeDtypeStruct((t_all, d), F32), jax.ShapeDtypeStruct((t_all, d), BF16),
           jax.ShapeDtypeStruct((t_all // seq, 8, d), F32),
           jax.ShapeDtypeStruct((8, d), F32)],
        scratch_shapes=[pltpu.VMEM((tm, d), F32)],
        compiler_params=_params(("arbitrary", "arbitrary")),
    )(df, a, b, w_d, w_g, w_u, x1, dy, mix, mod, g_fpre, g_mpost)


def _mixer_bwd(dmix, w_out, pooled, w_pool, pool_scale, seq, tm):
    t_all, d = dmix.shape
    p = pooled.shape[1]
    ng = len(POOL_WINDOWS)

    def body(dm_ref, wo_ref, pooled_ref, wp_ref, ps_ref, do_ref, dpd_ref, dps_ref, dwp_ref):
        i = pl.program_id(0)

        @pl.when(i == 0)
        def _():
            dps_ref[...] = jnp.zeros_like(dps_ref)
            dwp_ref[...] = jnp.zeros_like(dwp_ref)

        dmixin = _dot_nt(dm_ref[...], wo_ref[...])
        do_ref[...] = dmixin[:, :p].astype(BF16)
        for g in range(ng):
            cols = slice(g * POOL_GROUP, (g + 1) * POOL_GROUP)
            dpool = dmixin[:, p + g * POOL_GROUP:p + (g + 1) * POOL_GROUP]
            pooled = pooled_ref[:, cols]
            wpg = wp_ref[g].astype(BF16)
            yg = _dot(pooled, wpg)
            dps_ref[0:1, cols] += _colsum(dpool * yg)
            dyg = (dpool * ps_ref[:, cols]).astype(BF16)
            dwp_ref[g] += _dot_tn(pooled, dyg)
            dpd_ref[:, cols] = _dot_nt(dyg, wpg)

    tok = lambda i: (i, 0)
    const2 = lambda i: (0, 0)
    const3 = lambda i: (0, 0, 0)
    return pl.pallas_call(
        body, name="mixer_bwd", grid=(t_all // tm,),
        in_specs=[pl.BlockSpec((tm, d), tok), pl.BlockSpec((d, d), const2), pl.BlockSpec((tm, p), tok),
                  pl.BlockSpec(w_pool.shape, const3), pl.BlockSpec((1, p), const2)],
        out_specs=[pl.BlockSpec((tm, p), tok), pl.BlockSpec((tm, p), tok),
                   pl.BlockSpec((8, p), const2), pl.BlockSpec(w_pool.shape, const3)],
        out_shape=[jax.ShapeDtypeStruct((t_all, p), BF16), jax.ShapeDtypeStruct((t_all, p), F32),
                   jax.ShapeDtypeStruct((8, p), F32), jax.ShapeDtypeStruct(w_pool.shape, F32)],
        compiler_params=_params(("arbitrary",)),
    )(dmix, w_out, pooled, w_pool, pool_scale)


def _attn_bwd(qn, k, kt, v, do, ltot, seq, tq, tk):
    t_all, w = qn.shape
    nb, nq, ndiag, nkb = t_all // seq, seq // tq, tq // tk, seq // tk
    assert ndiag % 2 == 0, "two key blocks per loop trip"
    rc = ATTN_ROW_CHUNK
    nh = HEADS_PER_BLOCK
    heads = range(nh)

    def body(q_ref, k_ref, kt_ref, v_ref, do_ref, l_ref, up_ref, bf_ref, dq_ref, dk_ref, dv_ref,
             z_buf, dw_buf, ls_buf, hl_buf, upto_buf, g_buf, gb_buf, before_buf, w_buf, dz_buf,
             totl_buf, totg_buf, rem_buf, preg_buf, qnt_buf, dot_buf, dq_t, dk_t, dv_t):
        i = pl.program_id(2)
        nblk = (i + 1) * ndiag

        @pl.when(i == 0)
        def _():
            dk_t[...] = jnp.zeros_like(dk_t)
            dv_t[...] = jnp.zeros_like(dv_t)

        lane = lax.broadcasted_iota(jnp.int32, (1, LANES), 1)
        sub = lax.broadcasted_iota(jnp.int32, (LANES, 1), 0)
        row = lax.broadcasted_iota(jnp.int32, (rc, tk), 0)
        col = lax.broadcasted_iota(jnp.int32, (rc, tk), 1)
        first = lane < HEAD_DIM
        upper = sub < HEAD_DIM
        q2 = q_ref[...]
        do2 = do_ref[...]
        l2 = l_ref[...]
        qs = [jnp.where(first, q2, jnp.zeros_like(q2)), jnp.where(first, jnp.zeros_like(q2), q2)]
        dos = [jnp.where(first, do2, jnp.zeros_like(do2)), jnp.where(first, jnp.zeros_like(do2), do2)]
        for src, dst in ((q2, qnt_buf), (do2, dot_buf)):
            t = src.astype(F32).T
            dst[:, 0:tq] = jnp.where(upper, t, 0.0).astype(BF16)
            dst[:, tq:2 * tq] = jnp.where(upper, 0.0, t).astype(BF16)
        for h in heads:
            rem_buf[h] = jnp.where(first if h == 0 else ~first, l2, pltpu.roll(l2, HEAD_DIM, 1))
        preg_buf[...] = jnp.zeros_like(preg_buf)
        dq_t[...] = jnp.zeros_like(dq_t)
        w_buf[1] = jnp.zeros((nh * tq, tk), BF16)
        dz_buf[1] = jnp.zeros((nh * tq, tk), BF16)

        def causal(c, diag):
            return (col + diag * tk) < (row + c * rc)

        def scores(blk, slot):
            off = pl.multiple_of(blk * tk, tk)
            kj = k_ref[pl.ds(off, tk), :]
            vj = v_ref[pl.ds(off, tk), :]
            for h in heads:
                z_buf[slot, h] = _dot_nt(qs[h], kj)
                dw_buf[slot, h] = _dot_nt(dos[h], vj)

        def gradients(blk, slot):
            off = pl.multiple_of(blk * tk, tk)
            ktj = kt_ref[:, pl.ds(off, tk)]
            zero = jnp.zeros_like(ktj)
            dq_t[...] += (_dot_nt(jnp.where(upper, ktj, zero), dz_buf[slot, 0:tq, :])
                          + _dot_nt(jnp.where(upper, zero, ktj), dz_buf[slot, tq:2 * tq, :]))
            dk_t[blk] += _dot(qnt_buf[...], dz_buf[slot])
            dv_t[blk] += _dot(dot_buf[...], w_buf[slot])

        def softplus_stage(h, slot, diag):
            for c in range(tq // rc):
                rows = slice(c * rc, (c + 1) * rc)
                if _all_masked(c, diag, rc, tk):
                    hl_buf[h, rows, :] = jnp.zeros((rc, 2 * tk), BF16)
                    continue
                nz = z_buf[slot, h, rows, :]
                l1 = jnp.minimum(nz, 0.0) - jnp.log(1.0 + jnp.exp(-jnp.abs(nz)))
                if _some_masked(c, diag, rc, tk):
                    l1 = jnp.where(causal(c, diag), l1, 0.0)
                hi, lo = _split(l1)
                hl_buf[h, rows, 0:tk] = hi
                hl_buf[h, rows, tk:2 * tk] = lo
                ls_buf[h, rows, :] = l1 - nz
                totl_buf[h, rows, :] = _row_sums(l1)

        def weights_stage(h, slot, diag):
            for c in range(tq // rc):
                rows = slice(c * rc, (c + 1) * rc)
                stacked = slice(h * tq + c * rc, h * tq + (c + 1) * rc)
                if _all_masked(c, diag, rc, tk):
                    w_buf[slot, stacked, :] = jnp.zeros((rc, tk), BF16)
                    gb_buf[h, rows, :] = jnp.zeros((rc, tk), BF16)
                    continue
                wgt = jnp.exp(ls_buf[h, rows, :] + (_across(rem_buf[h, rows, :], tk) - upto_buf[h, rows, :]))
                if _some_masked(c, diag, rc, tk):
                    wgt = jnp.where(causal(c, diag), wgt, 0.0)
                w_buf[slot, stacked, :] = wgt.astype(BF16)
                g = wgt * dw_buf[slot, h, rows, :]
                g_buf[h, rows, :] = g
                gb_buf[h, rows, :] = g.astype(BF16)
                totg_buf[h, rows, :] = _row_sums(g)
                rem_buf[h, rows, :] -= totl_buf[h, rows, :]

        def dscore_stage(h, slot, diag):
            for c in range(tq // rc):
                rows = slice(c * rc, (c + 1) * rc)
                stacked = slice(h * tq + c * rc, h * tq + (c + 1) * rc)
                if _all_masked(c, diag, rc, tk):
                    dz_buf[slot, stacked, :] = jnp.zeros((rc, tk), BF16)
                    continue
                sig = jnp.exp(ls_buf[h, rows, :])
                g = g_buf[h, rows, :]
                dnz = sig * (before_buf[h, rows, :] + _across(preg_buf[h, rows, :], tk)) - g * (1.0 - sig)
                if _some_masked(c, diag, rc, tk):
                    dnz = jnp.where(causal(c, diag), dnz, 0.0)
                dz_buf[slot, stacked, :] = dnz.astype(BF16)
                preg_buf[h, rows, :] += totg_buf[h, rows, :]

        def position(blk, slot, diag, prefetch):
            if prefetch:
                scores(blk + 1, 1 - slot)
            for h in heads:
                softplus_stage(h, slot, diag)
                upto_buf[h] = _dot(hl_buf[h], up_ref[...])
            gradients(jnp.maximum(blk - 1, 0), 1 - slot)
            for h in heads:
                weights_stage(h, slot, diag)
                before_buf[h] = _dot(gb_buf[h], bf_ref[...])
            for h in heads:
                dscore_stage(h, slot, diag)

        scores(0, 0)

        def trip(jj, carry):
            for u in range(2):
                position(2 * jj + u, u, None, True)
            return carry

        lax.fori_loop(0, (i * ndiag) // 2, trip, 0)
        for d in range(ndiag):
            position(i * ndiag + d, d % 2, d, d < ndiag - 1)
        gradients(nblk - 1, 1)
        dq_ref[...] = (dq_t[...].T * NEG_QK_SCALE).astype(BF16)

        @pl.when(i == nq - 1)
        def _():
            for blk in range(nkb):
                dk_ref[blk * tk:(blk + 1) * tk, :] = dk_t[blk].T.astype(BF16)
                dv_ref[blk * tk:(blk + 1) * tk, :] = dv_t[blk].T.astype(BF16)

    qmap = lambda b, hp, i: (b * nq + i, hp)
    kmap = lambda b, hp, i: (b, hp)
    const = lambda b, hp, i: (0, 0)
    return pl.pallas_call(
        body, name="attn_bwd", grid=(nb, w // LANES, nq),
        in_specs=[pl.BlockSpec((tq, LANES), qmap), pl.BlockSpec((seq, LANES), kmap),
                  pl.BlockSpec((LANES, seq), lambda b, hp, i: (hp, b)), pl.BlockSpec((seq, LANES), kmap),
                  pl.BlockSpec((tq, LANES), qmap), pl.BlockSpec((tq, LANES), qmap),
                  pl.BlockSpec((2 * tk, tk), const), pl.BlockSpec((tk, tk), const)],
        out_specs=[pl.BlockSpec((tq, LANES), qmap), pl.BlockSpec((seq, LANES), kmap), pl.BlockSpec((seq, LANES), kmap)],
        out_shape=[jax.ShapeDtypeStruct((t_all, w), BF16)] * 3,
        scratch_shapes=[pltpu.VMEM((2, nh, tq, tk), F32), pltpu.VMEM((2, nh, tq, tk), F32),
                        pltpu.VMEM((nh, tq, tk), F32), pltpu.VMEM((nh, tq, 2 * tk), BF16),
                        pltpu.VMEM((nh, tq, tk), F32), pltpu.VMEM((nh, tq, tk), F32),
                        pltpu.VMEM((nh, tq, tk), BF16), pltpu.VMEM((nh, tq, tk), F32),
                        pltpu.VMEM((2, nh * tq, tk), BF16), pltpu.VMEM((2, nh * tq, tk), BF16),
                        pltpu.VMEM((nh, tq, LANES), F32), pltpu.VMEM((nh, tq, LANES), F32),
                        pltpu.VMEM((nh, tq, LANES), F32), pltpu.VMEM((nh, tq, LANES), F32),
                        pltpu.VMEM((LANES, nh * tq), BF16), pltpu.VMEM((LANES, nh * tq), BF16),
                        pltpu.VMEM((LANES, tq), F32), pltpu.VMEM((nkb, LANES, tk), F32),
                        pltpu.VMEM((nkb, LANES, tk), F32)],
        compiler_params=_params(("arbitrary", "arbitrary", "arbitrary")),
    )(qn, k, kt, v, do, ltot, _tri_matrix(tk, "upto"), _tri_matrix(tk, "before")[:tk])


def _inproj_bwd(dq, dk, dv, dpd, x, dx1, mod, g_pre, w_in, seq, tm):
    t_all, d = x.shape
    nt = seq // tm
    p = dq.shape[1]

    def body(dq_ref, dk_ref, dv_ref, dpd_ref, halo_ref, x_ref, dx1_ref, mod_ref, g_ref, w_ref,
             gx_ref, du_ref, accb_ref, accg_ref):
        i = pl.program_id(0)
        it = i % nt

        @pl.when(i == 0)
        def _():
            accg_ref[...] = jnp.zeros_like(accg_ref)

        @pl.when(it == 0)
        def _():
            accb_ref[...] = jnp.zeros_like(accb_ref)

        dpd = dpd_ref[...]
        pos = it * tm + lax.broadcasted_iota(jnp.int32, (tm, 1), 0)
        cnts = _window_counts(pos)
        halo = jnp.where(it == nt - 1, 0.0, halo_ref[...])
        scaled = []
        halos = []
        for g, win in enumerate(POOL_WINDOWS):
            cols = slice(g * POOL_GROUP, (g + 1) * POOL_GROUP)
            scaled.append(dpd[:, cols] / cnts[g])
            halos.append(halo[:, cols] / float(win))
        ext = jnp.concatenate([jnp.concatenate(scaled, axis=1), jnp.concatenate(halos, axis=1)], axis=0)
        sums = _window_sums(ext, tm, 0, True)
        du = (jnp.concatenate(sums, axis=1) - dpd).astype(BF16)
        du_ref[...] = du
        dh1 = (_dot_nt(dq_ref[...], w_ref[0]) + _dot_nt(dk_ref[...], w_ref[1])
               + _dot_nt(dv_ref[...], w_ref[2]) + _dot_nt(du, w_ref[3]))
        xf = x_ref[...]
        r1 = _rms(xf)
        n1 = xf * r1
        g1 = g_ref[...]
        scale1 = 1.0 + mod_ref[0, 1:2, :]
        accb_ref[0, 0:1, :] += _colsum(dh1)
        accb_ref[0, 1:2, :] += _colsum(dh1 * (n1 * g1))
        accg_ref[0:1, :] += _colsum((dh1 * scale1) * n1)
        gx_ref[...] = dx1_ref[...] + _norm_bwd((dh1 * scale1) * g1, n1, r1)

    tok = lambda i: (i, 0)
    const2 = lambda i: (0, 0)
    hb = tm // HALO
    last = t_all // HALO - 1
    return pl.pallas_call(
        body, name="inproj_bwd", grid=(t_all // tm,),
        in_specs=[pl.BlockSpec((tm, p), tok), pl.BlockSpec((tm, p), tok), pl.BlockSpec((tm, p), tok),
                  pl.BlockSpec((tm, p), tok),
                  pl.BlockSpec((HALO, p), lambda i: (jnp.minimum((i + 1) * hb, last), 0)),
                  pl.BlockSpec((tm, d), tok), pl.BlockSpec((tm, d), tok),
                  pl.BlockSpec((1, MOD_ROWS, d), lambda i: (i // nt, 0, 0)),
                  pl.BlockSpec((1, d), const2),
                  pl.BlockSpec((N_CHIPS, d, p), lambda i: (0, 0, 0))],
        out_specs=[pl.BlockSpec((tm, d), tok), pl.BlockSpec((tm, p), tok),
                   pl.BlockSpec((1, 8, d), lambda i: (i // nt, 0, 0)),
                   pl.BlockSpec((8, d), const2)],
        out_shape=[jax.ShapeDtypeStruct((t_all, d), F32), jax.ShapeDtypeStruct((t_all, p), BF16),
                   jax.ShapeDtypeStruct((t_all // seq, 8, d), F32),
                   jax.ShapeDtypeStruct((8, d), F32)],
        compiler_params=_params(("arbitrary",)),
    )(dq, dk, dv, dpd, dpd, x, dx1, mod, g_pre, w_in)


def _tn_matmul(x, ys, nk, bt, name):
    t_all = x.shape[-2]
    m = x.shape[-1]
    ny = len(ys)

    def spec(arr):
        if arr.ndim == 3:
            return pl.BlockSpec((1, bt, arr.shape[-1]), lambda k, t: (k, t, 0))
        return pl.BlockSpec((bt, arr.shape[-1]), lambda k, t: (t, 0))

    def tile(ref):
        return ref[0] if len(ref.shape) == 3 else ref[...]

    def body(*refs):
        x_ref, y_refs, o_refs = refs[0], refs[1:1 + ny], refs[1 + ny:]
        t = pl.program_id(1)
        xt = tile(x_ref)
        for y_ref, o_ref in zip(y_refs, o_refs):
            part = _dot_tn(xt, tile(y_ref))

            @pl.when(t == 0)
            def _(o_ref=o_ref, part=part):
                o_ref[0] = part

            @pl.when(t > 0)
            def _(o_ref=o_ref, part=part):
                o_ref[0] += part

    return pl.pallas_call(
        body, name=name, grid=(nk, t_all // bt),
        in_specs=[spec(x)] + [spec(y) for y in ys],
        out_specs=[pl.BlockSpec((1, m, y.shape[-1]), lambda k, t: (k, 0, 0)) for y in ys],
        out_shape=[jax.ShapeDtypeStruct((nk, m, y.shape[-1]), F32) for y in ys],
        compiler_params=_params(("arbitrary", "arbitrary")),
    )(x, *ys)


def _cond_fwd(c_all, w_q, b_q, bn):
    nrow, d = c_all.shape
    ncol = w_q.shape[1]

    def body(c_ref, w_ref, b_ref, sc_ref, mod_ref):
        cf = c_ref[...]
        sc = cf * _sigmoid(cf)
        sc_ref[...] = sc
        shi, slo = _split(sc)
        whi, wlo = _split(w_ref[...])
        mod_ref[...] = (_dot(shi, whi) + _dot(shi, wlo) + _dot(slo, whi)) + b_ref[...]

    return pl.pallas_call(
        body, name="cond_fwd", grid=(ncol // bn,),
        in_specs=[pl.BlockSpec((nrow, d), lambda n: (0, 0)), pl.BlockSpec((d, bn), lambda n: (0, n)),
                  pl.BlockSpec((1, bn), lambda n: (0, n))],
        out_specs=[pl.BlockSpec((nrow, d), lambda n: (0, 0)), pl.BlockSpec((nrow, bn), lambda n: (0, n))],
        out_shape=[jax.ShapeDtypeStruct((nrow, d), F32), jax.ShapeDtypeStruct((nrow, ncol), F32)],
        compiler_params=_params(("arbitrary",)),
    )(c_all, w_q, b_q)


def _cond_bwd(sc_all, dmod_q, bn):
    nrow, d = sc_all.shape
    ncol = dmod_q.shape[1]

    def body(sc_ref, dm_ref, gw_ref):
        shi, slo = _split(sc_ref[...])
        dhi, dlo = _split(dm_ref[...])
        gw_ref[...] = _dot_tn(shi, dhi) + _dot_tn(shi, dlo) + _dot_tn(slo, dhi)

    return pl.pallas_call(
        body, name="cond_bwd", grid=(ncol // bn,),
        in_specs=[pl.BlockSpec((nrow, d), lambda n: (0, 0)), pl.BlockSpec((nrow, bn), lambda n: (0, n))],
        out_specs=pl.BlockSpec((d, bn), lambda n: (0, n)),
        out_shape=jax.ShapeDtypeStruct((d, ncol), F32),
        compiler_params=_params(("arbitrary",)),
    )(sc_all, dmod_q)


def _row_block(rows, cols, budget=1 << 18):
    best = None
    for br in range(8, rows + 1, 8):
        if rows % br == 0 and br * cols <= budget:
            best = br
    return best if best is not None else rows


def _adamw(w, g, m, v, name):
    rows, cols = w.shape
    br = _row_block(rows, cols)
    c1 = 1.0 - ADAM_B1 ** ADAM_STEP
    c2 = 1.0 - ADAM_B2 ** ADAM_STEP

    def body(w_ref, g_ref, m_ref, v_ref, d_ref, nm_ref, nv_ref):
        gf = g_ref[...]
        m2 = ADAM_B1 * m_ref[...] + (1.0 - ADAM_B1) * gf
        v2 = ADAM_B2 * v_ref[...] + (1.0 - ADAM_B2) * (gf * gf)
        nm_ref[...] = m2
        nv_ref[...] = v2
        d_ref[...] = -ADAM_LR * ((m2 / c1) / (jnp.sqrt(v2 / c2) + ADAM_EPS) + ADAM_WD * w_ref[...])

    blk = pl.BlockSpec((br, cols), lambda i: (i, 0))
    return pl.pallas_call(
        body, name=name, grid=(rows // br,),
        in_specs=[blk] * 4, out_specs=[blk] * 3,
        out_shape=[jax.ShapeDtypeStruct((rows, cols), F32)] * 3,
        compiler_params=_params(("arbitrary",)),
    )(w, g, m, v)


def _all_gather(x_shard, name):
    m_per, n = x_shard.shape

    def body(x_ref, out_ref, send_sems, recv_sems, local_sem):
        x, y, c = _position()
        me, sibling = (x, y, c), (x, y, 1 - c)
        chips = [(1 - x, y), (x, 1 - y), (1 - x, 1 - y)]

        def rows(px, py, pc):
            return out_ref.at[pl.ds((4 * px + 2 * py + pc) * m_per, m_per), :]

        def copy(k, block, to, src=None):
            return pltpu.make_async_remote_copy(
                src_ref=rows(*block) if src is None else src, dst_ref=rows(*block),
                send_sem=send_sems.at[k], recv_sem=recv_sems.at[k], device_id=to, device_id_type=MESH)

        mine = pltpu.make_async_copy(x_ref, rows(*me), local_sem)
        mine.start()
        first = [copy(0, me, sibling, src=x_ref)]
        first += [copy(1 + j, me, (*chip, c), src=x_ref) for j, chip in enumerate(chips)]
        for cp in first:
            cp.start()
        passed = [copy(4 + j, (*chip, c), sibling) for j, chip in enumerate(chips)]
        for j, chip in enumerate(chips):
            copy(1 + j, (*chip, c), me).wait_recv()
            passed[j].start()
        copy(0, sibling, me).wait_recv()
        for j, chip in enumerate(chips):
            copy(4 + j, (*chip, 1 - c), me).wait_recv()
        for cp in first + passed:
            cp.wait_send()
        mine.wait()

    return pl.pallas_call(
        body, name=name,
        out_shape=jax.ShapeDtypeStruct((N_DEV * m_per, n), x_shard.dtype),
        in_specs=[pl.BlockSpec(memory_space=pltpu.VMEM)],
        out_specs=pl.BlockSpec(memory_space=pltpu.VMEM),
        scratch_shapes=[pltpu.SemaphoreType.DMA((7,)), pltpu.SemaphoreType.DMA((7,)), pltpu.SemaphoreType.DMA],
        compiler_params=pltpu.CompilerParams(vmem_limit_bytes=VMEM_LIMIT),
    )(x_shard)


_ANY = pl.BlockSpec(memory_space=pl.ANY)


def _place_quarters(place, quarters):
    steps = 2

    def body(place_ref, *refs):
        n = len(refs) // 2
        for w_ref, o_ref in zip(refs[:n], refs[n:]):
            o_ref[0] = w_ref[...].astype(BF16)

    return pl.pallas_call(
        body, name="place_quarters",
        grid_spec=pltpu.PrefetchScalarGridSpec(
            num_scalar_prefetch=1, grid=(steps,),
            in_specs=[pl.BlockSpec((q.shape[0] // steps, q.shape[1]), lambda r, place_ref: (r, 0)) for q in quarters],
            out_specs=[pl.BlockSpec((1, q.shape[0] // steps, q.shape[1]), lambda r, place_ref: (place_ref[0], r, 0))
                       for q in quarters]),
        out_shape=[jax.ShapeDtypeStruct((N_CHIPS,) + q.shape, BF16) for q in quarters],
        compiler_params=_params(("arbitrary",)),
    )(place, *quarters)


def _gather_weights(placed):
    n = len(placed)
    shapes = [b.shape[1:] for b in placed]

    def body(*refs):
        g_refs = refs[n:2 * n]
        send_sems, recv_sems = refs[2 * n:]
        x, y, c = _position()
        sibling = (x, y, 1 - c)
        chips = [(1 - x, y), (x, 1 - y), (1 - x, 1 - y)]
        mine = 2 * x + y

        def half(a, which):
            hr = shapes[a][0] // 2
            return pl.ds(which * hr, hr)

        def over_ici(a, p, slot):
            ref = g_refs[a].at[slot, half(a, c), :]
            return pltpu.make_async_remote_copy(
                src_ref=ref, dst_ref=ref,
                send_sem=send_sems.at[6 * a + p], recv_sem=recv_sems.at[6 * a + p],
                device_id=(*chips[p], c), device_id_type=MESH)

        def over_d2d(a, p, slot, which):
            ref = g_refs[a].at[slot, half(a, which), :]
            return pltpu.make_async_remote_copy(
                src_ref=ref, dst_ref=ref,
                send_sem=send_sems.at[6 * a + 3 + p], recv_sem=recv_sems.at[6 * a + 3 + p],
                device_id=sibling, device_id_type=MESH)

        sends = []
        for a in range(n):
            for p in range(3):
                cp = over_ici(a, p, mine)
                cp.start()
                sends.append(cp)
        for a in range(n):
            for p, (cx, cy) in enumerate(chips):
                slot = 2 * cx + cy
                over_ici(a, p, slot).wait_recv()
                cp = over_d2d(a, p, slot, c)
                cp.start()
                sends.append(cp)
        for a in range(n):
            for p, (cx, cy) in enumerate(chips):
                over_d2d(a, p, 2 * cx + cy, 1 - c).wait_recv()
        for cp in sends:
            cp.wait_send()

    return pl.pallas_call(
        body, name="gather_weights",
        out_shape=[jax.ShapeDtypeStruct(b.shape, BF16) for b in placed],
        in_specs=[_ANY] * n, out_specs=[_ANY] * n,
        input_output_aliases={a: a for a in range(n)},
        scratch_shapes=[pltpu.SemaphoreType.DMA((6 * n,)), pltpu.SemaphoreType.DMA((6 * n,))],
    )(*placed)


def _sibling_exchange(grads):
    n = len(grads)
    shapes = [g.shape for g in grads]

    def body(*refs):
        g_refs, x_refs = refs[:n], refs[n:2 * n]
        send_sems, recv_sems = refs[2 * n:]
        x, y, c = _position()
        copies = []
        for a in range(n):
            hr = shapes[a][1] // 2
            cp = pltpu.make_async_remote_copy(
                src_ref=g_refs[a].at[:, pl.ds((1 - c) * hr, hr), :], dst_ref=x_refs[a],
                send_sem=send_sems.at[a], recv_sem=recv_sems.at[a],
                device_id=(x, y, 1 - c), device_id_type=MESH)
            cp.start()
            copies.append(cp)
        for cp in copies:
            cp.wait()

    return pl.pallas_call(
        body, name="grad_sibling_exchange",
        out_shape=[jax.ShapeDtypeStruct((s[0], s[1] // 2, s[2]), F32) for s in shapes],
        in_specs=[_ANY] * n, out_specs=[_ANY] * n,
        scratch_shapes=[pltpu.SemaphoreType.DMA((n,)), pltpu.SemaphoreType.DMA((n,))],
    )(*grads)


def _chip_sums(core, grads, theirs):
    n = len(grads)

    def body(core_ref, *refs):
        g_refs, t_refs, o_refs = refs[:n], refs[n:2 * n], refs[2 * n:]
        for g_ref, t_ref, o_ref in zip(g_refs, t_refs, o_refs):
            o_ref[...] = (g_ref[...] + t_ref[...]).astype(BF16)

    in_specs = [pl.BlockSpec((1, g.shape[1] // 2, g.shape[2]), lambda k, core_ref: (k, core_ref[0], 0)) for g in grads]
    in_specs += [pl.BlockSpec((1,) + t.shape[1:], lambda k, core_ref: (k, 0, 0)) for t in theirs]
    return pl.pallas_call(
        body, name="grad_chip_sums",
        grid_spec=pltpu.PrefetchScalarGridSpec(
            num_scalar_prefetch=1, grid=(N_CHIPS,), in_specs=in_specs,
            out_specs=[pl.BlockSpec((1,) + t.shape[1:], lambda k, core_ref: (k, 0, 0)) for t in theirs]),
        out_shape=[jax.ShapeDtypeStruct(t.shape, BF16) for t in theirs],
        compiler_params=_params(("arbitrary",)),
    )(core, *grads, *theirs)


def _chip_exchange(sums):
    n = len(sums)

    def body(*refs):
        s_refs, y_refs = refs[:n], refs[n:2 * n]
        send_sems, recv_sems = refs[2 * n:]
        x, y, c = _position()
        chips = [(1 - x, y), (x, 1 - y), (1 - x, 1 - y)]
        copies = []
        for a in range(n):
            for p, (cx, cy) in enumerate(chips):
                cp = pltpu.make_async_remote_copy(
                    src_ref=s_refs[a].at[2 * cx + cy], dst_ref=y_refs[a].at[p],
                    send_sem=send_sems.at[3 * a + p], recv_sem=recv_sems.at[3 * a + p],
                    device_id=(cx, cy, c), device_id_type=MESH)
                cp.start()
                copies.append(cp)
        for cp in copies:
            cp.wait()

    return pl.pallas_call(
        body, name="grad_chip_exchange",
        out_shape=[jax.ShapeDtypeStruct((3,) + s.shape[1:], BF16) for s in sums],
        in_specs=[_ANY] * n, out_specs=[_ANY] * n,
        scratch_shapes=[pltpu.SemaphoreType.DMA((3 * n,)), pltpu.SemaphoreType.DMA((3 * n,))],
    )(*sums)


def _total_sums(place, sums, parts):
    n = len(parts)
    steps = 2

    def body(place_ref, *refs):
        for s_ref, y_ref, o_ref in zip(refs[:n], refs[n:2 * n], refs[2 * n:]):
            o_ref[0] = ((s_ref[0].astype(F32) + y_ref[0].astype(F32)) + y_ref[1].astype(F32)) + y_ref[2].astype(F32)

    def step_rows(pt):
        return pt.shape[1] // steps

    in_specs = [pl.BlockSpec((1, step_rows(s), s.shape[2]), lambda r, place_ref: (place_ref[0], r, 0)) for s in sums]
    in_specs += [pl.BlockSpec((3, step_rows(pt), pt.shape[2]), lambda r, place_ref: (0, r, 0)) for pt in parts]
    return pl.pallas_call(
        body, name="grad_total_sums",
        grid_spec=pltpu.PrefetchScalarGridSpec(
            num_scalar_prefetch=1, grid=(steps,), in_specs=in_specs,
            out_specs=[pl.BlockSpec((1, step_rows(pt), pt.shape[2]), lambda r, place_ref: (place_ref[1], r, 0))
                       for pt in parts]),
        out_shape=[jax.ShapeDtypeStruct((2,) + pt.shape[1:], F32) for pt in parts],
        compiler_params=_params(("arbitrary",)),
    )(place, *sums, *parts)


def _sibling_share(halves):
    n = len(halves)

    def body(*refs):
        f_refs = refs[n:2 * n]
        send_sems, recv_sems = refs[2 * n:]
        x, y, c = _position()
        copies = []
        for a in range(n):
            cp = pltpu.make_async_remote_copy(
                src_ref=f_refs[a].at[c], dst_ref=f_refs[a].at[c], send_sem=send_sems.at[a], recv_sem=recv_sems.at[a],
                device_id=(x, y, 1 - c), device_id_type=MESH)
            cp.start()
            copies.append(cp)
        for a, cp in enumerate(copies):
            cp.wait_send()
            pltpu.make_async_remote_copy(
                src_ref=f_refs[a].at[1 - c], dst_ref=f_refs[a].at[1 - c], send_sem=send_sems.at[a],
                recv_sem=recv_sems.at[a], device_id=(x, y, c), device_id_type=MESH).wait_recv()

    return pl.pallas_call(
        body, name="grad_sibling_share",
        out_shape=[jax.ShapeDtypeStruct(h.shape, F32) for h in halves],
        in_specs=[_ANY] * n, out_specs=[_ANY] * n,
        input_output_aliases={a: a for a in range(n)},
        scratch_shapes=[pltpu.SemaphoreType.DMA((n,)), pltpu.SemaphoreType.DMA((n,))],
    )(*halves)


def _group_sum(stacked, nrow, name):
    total, n = stacked.shape
    groups = total // nrow

    def body(g_ref, o_ref):
        acc = g_ref[0:nrow, :]
        for grp in range(1, groups):
            acc = acc + g_ref[grp * nrow:(grp + 1) * nrow, :]
        o_ref[...] = acc

    return pl.pallas_call(
        body, name=name,
        out_shape=jax.ShapeDtypeStruct((nrow, n), F32),
        compiler_params=pltpu.CompilerParams(vmem_limit_bytes=VMEM_LIMIT),
    )(stacked)


def _local_step(xt, tgt, mod, gains, w_pool, pool_scale, weights, seq):
    g_mpre, g_mpost, g_fpre, g_fpost = gains
    w_in, w_out, w_g, w_u, w_d = weights
    d = xt.shape[1]
    tm, tq = min(TOKEN_TILE, seq), min(ATTN_TILE, seq)
    w_out2 = w_out.reshape(d, d)

    h1, qn, k, v, u, kt, vt = _prenorm_proj(xt, mod, g_mpre, w_in, seq, tm)
    tk = min(ATTN_KEY_TILE, tq // 2)
    o, ltot = _attn_fwd(qn, k, vt, seq, tq, tk)
    pooled, mixin, mix, x1, h2 = _mixer_post(u, o, xt, mod, g_mpost, g_fpre, w_pool, pool_scale, w_out2, seq, tm)
    a, b, fin, dy, df, loss_blk, accb4, accg4 = _ffn_fwd(h2, w_g, w_u, w_d, x1, tgt, mod, g_fpost, seq, tm)
    da, db, dx1, dmix, accb5, accg5 = _ffn_bwd(df, a, b, w_d, w_g, w_u, x1, dy, mix, mod, g_fpre, g_mpost, seq, tm)
    do, dpd, dps, dwp = _mixer_bwd(dmix, w_out2, pooled, w_pool, pool_scale, seq, tm)
    dq, dk, dv = _attn_bwd(qn, k, kt, v, do, ltot, seq, tq, tk)
    gx, du, accb8, accg8 = _inproj_bwd(dq, dk, dv, dpd, xt, dx1, mod, g_mpre, w_in, seq, tm)

    g_in = jnp.concatenate(_tn_matmul(h1, [dq, dk, dv, du], 1, tm, "grad_w_in"), axis=0)
    g_out = _tn_matmul(mixin, [dmix], 1, tm, "grad_w_out")[0].reshape(w_out.shape)
    g_g, g_u = _tn_matmul(h2, [da, db], w_g.shape[0], tm, "grad_w_gate_up")
    (g_d,) = _tn_matmul(fin, [df], w_d.shape[0], tm, "grad_w_down")

    dmod = jnp.stack([accb8[:, 0], accb8[:, 1], accb5[:, 2], accb5[:, 0], accb5[:, 1], accb4[:, 0]], axis=1)
    dgain = jnp.stack([accg8[0], accg5[1], accg5[0], accg4[0]], axis=0)
    return loss_blk, gx, [g_in, g_out, g_g, g_u, g_d], dmod, dgain, dps[0:1], dwp


def kernel(x, c, w_cond, b_cond, g_mix_pre, g_mix_post, w_in, w_pool, pool_scale, w_out, g_ffn_pre, g_ffn_post, w_gate, w_up, w_down, loss_target, m_w_cond, m_b_cond, m_g_mix_pre, m_g_mix_post, m_w_in, m_w_pool, m_pool_scale, m_w_out, m_g_ffn_pre, m_g_ffn_post, m_w_gate, m_w_up, m_w_down, v_w_cond, v_b_cond, v_g_mix_pre, v_g_mix_post, v_w_in, v_w_pool, v_pool_scale, v_w_out, v_g_ffn_pre, v_g_ffn_post, v_w_gate, v_w_up, v_w_down):
    xi, yi, ci = _position()
    chip = 2 * xi + yi
    dev = 4 * xi + 2 * yi + ci
    nb, seq, d = x.shape
    t_all = nb * seq
    xt = x.reshape(t_all, d)
    tgt = loss_target.reshape(t_all, d)
    ncol = w_cond.shape[2]
    pw = pool_scale.shape[1]

    c_pad = jnp.concatenate([c, jnp.zeros((8 - nb, d), F32)], axis=0)
    c_all = _all_gather(c_pad, "gather_c").reshape(N_DEV, 8, d)[:, :nb].reshape(N_DEV * nb, d)
    b_q = lax.dynamic_slice(b_cond, (0, chip * ncol), (1, ncol))
    sc_all, mod_q = _cond_fwd(c_all, w_cond[0], b_q, 512)
    mod_parts = _all_gather(mod_q, "gather_mod").reshape(N_DEV, N_DEV * nb, ncol)
    mod_rows = lax.dynamic_slice(mod_parts, (0, dev * nb, 0), (N_DEV, nb, ncol))[0::2]
    mod = jnp.transpose(mod_rows, (1, 0, 2)).reshape(nb, N_MOD, d)
    mod = jnp.concatenate([mod, jnp.zeros((nb, MOD_ROWS - N_MOD, d), F32)], axis=1)

    place = jnp.stack([chip, ci]).astype(jnp.int32)
    weights = _gather_weights(_place_quarters(place, [w[0] for w in (w_in, w_out, w_gate, w_up, w_down)]))

    gains = (g_mix_pre, g_mix_post, g_ffn_pre, g_ffn_post)
    loss_blk, gx, grads, dmod, dgain, dps, dwp = _local_step(xt, tgt, mod, gains, w_pool[0], pool_scale, weights, seq)
    loss = lax.psum(loss_blk[0, 0], ("x", "y", "c"))

    theirs = _sibling_exchange(grads)
    sums = _chip_sums(place[1:], grads, theirs)
    parts = _chip_exchange(sums)
    halves = _total_sums(place, sums, parts)
    g_big = [g.reshape(2 * g.shape[1], g.shape[2]) for g in _sibling_share(halves)]

    wp_rows = dwp.size // d
    pad_rows = 24 - (2 * N_MOD + 4 + 1)
    payload = jnp.concatenate([
        dmod.reshape(nb * N_MOD, d), dgain,
        jnp.concatenate([dps, jnp.zeros((1, d - pw), F32)], axis=1),
        jnp.zeros((pad_rows, d), F32), dwp.reshape(wp_rows, d)], axis=0)
    prow = payload.shape[0]
    gathered = _all_gather(payload, "gather_small")
    summed = _group_sum(gathered, prow, "small_device_sum")
    dmod_all = gathered.reshape(N_DEV, prow, d)[:, :nb * N_MOD].reshape(N_DEV * nb, N_MOD * d)
    g_b_cond = _group_sum(dmod_all, 1, "grad_b_cond")
    dmod_q = lax.dynamic_slice(dmod_all, (0, chip * ncol), (N_DEV * nb, ncol))
    g_w_cond = _cond_bwd(sc_all, dmod_q, 512)
    first_gain = 2 * N_MOD
    g_gains = [summed[first_gain + r:first_gain + r + 1] for r in range(4)]
    g_pool_scale = summed[first_gain + 4:first_gain + 5, :pw]
    g_w_pool = summed[24:24 + wp_rows].reshape(w_pool.shape[1] * w_pool.shape[2], w_pool.shape[3])

    flat_pool = lambda t: t.reshape(g_w_pool.shape)
    plan = [
        ("w_cond", w_cond[0], g_w_cond, m_w_cond[0], v_w_cond[0], w_cond.shape),
        ("b_cond", b_cond, g_b_cond, m_b_cond, v_b_cond, b_cond.shape),
        ("g_mix_pre", g_mix_pre, g_gains[0], m_g_mix_pre, v_g_mix_pre, g_mix_pre.shape),
        ("g_mix_post", g_mix_post, g_gains[1], m_g_mix_post, v_g_mix_post, g_mix_post.shape),
        ("w_in", w_in[0], g_big[0], m_w_in[0], v_w_in[0], w_in.shape),
        ("w_pool", flat_pool(w_pool), g_w_pool, flat_pool(m_w_pool), flat_pool(v_w_pool), w_pool.shape),
        ("pool_scale", pool_scale, g_pool_scale, m_pool_scale, v_pool_scale, pool_scale.shape),
        ("w_out", w_out[0], g_big[1], m_w_out[0], v_w_out[0], w_out.shape),
        ("g_ffn_pre", g_ffn_pre, g_gains[2], m_g_ffn_pre, v_g_ffn_pre, g_ffn_pre.shape),
        ("g_ffn_post", g_ffn_post, g_gains[3], m_g_ffn_post, v_g_ffn_post, g_ffn_post.shape),
        ("w_gate", w_gate[0], g_big[2], m_w_gate[0], v_w_gate[0], w_gate.shape),
        ("w_up", w_up[0], g_big[3], m_w_up[0], v_w_up[0], w_up.shape),
        ("w_down", w_down[0], g_big[4], m_w_down[0], v_w_down[0], w_down.shape),
    ]
    out_g, out_d, out_m, out_v = [], [], [], []
    for name, w2, g2, m2, v2, shape in plan:
        delta, new_m, new_v = _adamw(w2, g2, m2, v2, "adamw_" + name)
        out_g.append(g2.reshape(shape))
        out_d.append(delta.reshape(shape))
        out_m.append(new_m.reshape(shape))
        out_v.append(new_v.reshape(shape))
    return (loss, gx.reshape(x.shape), *out_g, *out_d, *out_m, *out_v)
```

```python
import functools

import jax
import jax.numpy as jnp
from jax import lax
from jax.experimental import pallas as pl
from jax.experimental.pallas import tpu as pltpu

F32 = jnp.float32
BF16 = jnp.bfloat16
MESH = pl.DeviceIdType.MESH

EPS = 1e-6
HEAD_DIM = 64
HEADS_PER_BLOCK = 2
LANES = 128
NEG_QK_SCALE = -0.125
POOL_WINDOWS = (2, 4, 8, 16)
POOL_GROUP = 128
HALO = 16
N_MOD = 6
MOD_ROWS = 8
N_CHIPS = 4
N_DEV = 8
VMEM_LIMIT = 56 * 1024 * 1024

ADAM_LR = 0.001
ADAM_B1 = 0.9
ADAM_B2 = 0.999
ADAM_EPS = 1e-08
ADAM_WD = 0.01
ADAM_STEP = 10

TOKEN_TILE = 512
ATTN_TILE = 512
ATTN_KEY_TILE = 256
ATTN_ROW_CHUNK = 32


def _dot(a, b):
    return jnp.dot(a, b, preferred_element_type=F32)


def _dot_nt(a, b):
    return lax.dot_general(a, b, (((1,), (1,)), ((), ())), preferred_element_type=F32)


def _dot_tn(a, b):
    return lax.dot_general(a, b, (((0,), (0,)), ((), ())), preferred_element_type=F32)


def _split(v):
    hi = v.astype(BF16)
    lo = (v - hi.astype(F32)).astype(BF16)
    return hi, lo


def _rms(v):
    return lax.rsqrt(jnp.mean(v * v, axis=-1, keepdims=True) + EPS)


def _norm_bwd(dn, n, r):
    return r * (dn - n * jnp.mean(dn * n, axis=-1, keepdims=True))


def _sigmoid(v):
    return 1.0 / (1.0 + jnp.exp(-v))


def _colsum(v):
    return jnp.sum(v, axis=0, keepdims=True)


def _params(sem=None):
    return pltpu.CompilerParams(dimension_semantics=sem, vmem_limit_bytes=VMEM_LIMIT)


def _position():
    return lax.axis_index("x"), lax.axis_index("y"), lax.axis_index("c")


def _prenorm_proj(x, mod, g_pre, w_in, seq, tm):
    t_all, d = x.shape
    nt = seq // tm
    p = w_in.shape[2]

    def body(x_ref, mod_ref, g_ref, w_ref, h_ref, q_ref, k_ref, v_ref, u_ref, kt_ref, vt_ref):
        xf = x_ref[...]
        n = xf * _rms(xf)
        h = (n * g_ref[...]) * (1.0 + mod_ref[0, 1:2, :]) + mod_ref[0, 0:1, :]
        hb = h.astype(BF16)
        h_ref[...] = hb
        q_ref[...] = (_dot(hb, w_ref[0]) * NEG_QK_SCALE).astype(BF16)
        kf = _dot(hb, w_ref[1])
        vf = _dot(hb, w_ref[2])
        k_ref[...] = kf.astype(BF16)
        v_ref[...] = vf.astype(BF16)
        kt_ref[...] = kf.T.astype(BF16)
        vt_ref[...] = vf.T.astype(BF16)
        u_ref[...] = _dot(hb, w_ref[3])

    tok = lambda i: (i, 0)
    tok_t = lambda i: (0, i)
    return pl.pallas_call(
        body, name="prenorm_proj", grid=(t_all // tm,),
        in_specs=[pl.BlockSpec((tm, d), tok),
                  pl.BlockSpec((1, MOD_ROWS, d), lambda i: (i // nt, 0, 0)),
                  pl.BlockSpec((1, d), lambda i: (0, 0)),
                  pl.BlockSpec((N_CHIPS, d, p), lambda i: (0, 0, 0))],
        out_specs=[pl.BlockSpec((tm, d), tok)] + [pl.BlockSpec((tm, p), tok)] * 4 + [pl.BlockSpec((p, tm), tok_t)] * 2,
        out_shape=[jax.ShapeDtypeStruct((t_all, d), BF16)] + [jax.ShapeDtypeStruct((t_all, p), BF16)] * 3
        + [jax.ShapeDtypeStruct((t_all, p), F32)] + [jax.ShapeDtypeStruct((p, t_all), BF16)] * 2,
        compiler_params=_params(("arbitrary",)),
    )(x, mod, g_pre, w_in)


def _tri_matrix(tk, kind):
    j = lax.broadcasted_iota(jnp.int32, (2 * tk, tk), 0) % tk
    s = lax.broadcasted_iota(jnp.int32, (2 * tk, tk), 1)
    return {"after": j > s, "upto": j <= s, "before": j < s}[kind].astype(BF16)


def _row_sums(v):
    return jnp.broadcast_to(jnp.sum(v, axis=-1, keepdims=True), (v.shape[0], LANES))


def _across(v, n):
    return jnp.concatenate([v] * (n // LANES), axis=1)


def _all_masked(c, diag, rc, tk):
    return diag is not None and diag * tk >= (c + 1) * rc - 1


def _some_masked(c, diag, rc, tk):
    return diag is not None and diag * tk + tk - 1 >= c * rc


def _attn_fwd(qn, k, vt, seq, tq, tk):
    t_all, w = qn.shape
    nb, nq, ndiag = t_all // seq, seq // tq, tq // tk
    assert ndiag % 2 == 0, "two key blocks per loop trip"
    rc = ATTN_ROW_CHUNK
    heads = range(HEADS_PER_BLOCK)

    def body(q_ref, k_ref, vt_ref, tri_ref, o_ref, l_ref,
             z_buf, ls_buf, hl_buf, aft_buf, w_buf, tot_buf, acc_t, run_buf):
        i = pl.program_id(2)
        nblk = (i + 1) * ndiag
        lane = lax.broadcasted_iota(jnp.int32, (1, LANES), 1)
        sub = lax.broadcasted_iota(jnp.int32, (LANES, 1), 0)
        row = lax.broadcasted_iota(jnp.int32, (rc, tk), 0)
        col = lax.broadcasted_iota(jnp.int32, (rc, tk), 1)
        first = lane < HEAD_DIM
        q2 = q_ref[...]
        qs = [jnp.where(first, q2, jnp.zeros_like(q2)), jnp.where(first, jnp.zeros_like(q2), q2)]
        acc_t[...] = jnp.zeros_like(acc_t)
        run_buf[...] = jnp.zeros_like(run_buf)
        w_buf[1] = jnp.zeros((HEADS_PER_BLOCK, tq, tk), BF16)

        def causal(c, diag):
            return (col + diag * tk) < (row + c * rc)

        def scores(blk, slot):
            kj = k_ref[pl.ds(pl.multiple_of(blk * tk, tk), tk), :]
            for h in heads:
                z_buf[slot, h] = _dot_nt(qs[h], kj)

        def values(blk, slot):
            vtj = vt_ref[:, pl.ds(pl.multiple_of(blk * tk, tk), tk)]
            zero = jnp.zeros_like(vtj)
            acc_t[...] += (_dot_nt(jnp.where(sub < HEAD_DIM, vtj, zero), w_buf[slot, 0])
                           + _dot_nt(jnp.where(sub < HEAD_DIM, zero, vtj), w_buf[slot, 1]))

        def softplus_stage(h, slot, diag):
            for c in range(tq // rc):
                rows = slice(c * rc, (c + 1) * rc)
                if _all_masked(c, diag, rc, tk):
                    hl_buf[h, rows, :] = jnp.zeros((rc, 2 * tk), BF16)
                    tot_buf[h, rows, :] = jnp.zeros((rc, LANES), F32)
                    continue
                nz = z_buf[slot, h, rows, :]
                l1 = jnp.minimum(nz, 0.0) - jnp.log(1.0 + jnp.exp(-jnp.abs(nz)))
                if _some_masked(c, diag, rc, tk):
                    l1 = jnp.where(causal(c, diag), l1, 0.0)
                hi, lo = _split(l1)
                hl_buf[h, rows, 0:tk] = hi
                hl_buf[h, rows, tk:2 * tk] = lo
                ls_buf[h, rows, :] = l1 - nz
                tot_buf[h, rows, :] = _row_sums(l1)

        def weights_stage(h, slot, diag):
            for c in range(tq // rc):
                rows = slice(c * rc, (c + 1) * rc)
                if _all_masked(c, diag, rc, tk):
                    w_buf[slot, h, rows, :] = jnp.zeros((rc, tk), BF16)
                    continue
                wgt = jnp.exp((ls_buf[h, rows, :] + aft_buf[h, rows, :]) + _across(run_buf[h, rows, :], tk))
                if _some_masked(c, diag, rc, tk):
                    wgt = jnp.where(causal(c, diag), wgt, 0.0)
                w_buf[slot, h, rows, :] = wgt.astype(BF16)
                run_buf[h, rows, :] += tot_buf[h, rows, :]

        def position(blk, slot, diag):
            scores(jnp.maximum(blk - 1, 0), 1 - slot)
            for h in heads:
                softplus_stage(h, slot, diag)
                aft_buf[h] = _dot(hl_buf[h], tri_ref[...])
            values(jnp.minimum(blk + 1, nblk - 1), 1 - slot)
            for h in heads:
                weights_stage(h, slot, diag)

        scores(nblk - 1, 0)
        for p in range(ndiag):
            position(nblk - 1 - p, p % 2, ndiag - 1 - p)

        def trip(jj, carry):
            for u in range(2):
                position(i * ndiag - 1 - 2 * jj - u, u, None)
            return carry

        lax.fori_loop(0, (i * ndiag) // 2, trip, 0)
        values(0, 1)
        o_ref[...] = acc_t[...].T.astype(BF16)
        l_ref[...] = jnp.where(first, run_buf[0], run_buf[1])

    qmap = lambda b, hp, i: (b * nq + i, hp)
    nh = HEADS_PER_BLOCK
    return pl.pallas_call(
        body, name="attn_fwd", grid=(nb, w // LANES, nq),
        in_specs=[pl.BlockSpec((tq, LANES), qmap), pl.BlockSpec((seq, LANES), lambda b, hp, i: (b, hp)),
                  pl.BlockSpec((LANES, seq), lambda b, hp, i: (hp, b)),
                  pl.BlockSpec((2 * tk, tk), lambda b, hp, i: (0, 0))],
        out_specs=[pl.BlockSpec((tq, LANES), qmap), pl.BlockSpec((tq, LANES), qmap)],
        out_shape=[jax.ShapeDtypeStruct((t_all, w), BF16), jax.ShapeDtypeStruct((t_all, w), F32)],
        scratch_shapes=[pltpu.VMEM((2, nh, tq, tk), F32), pltpu.VMEM((nh, tq, tk), F32),
                        pltpu.VMEM((nh, tq, 2 * tk), BF16), pltpu.VMEM((nh, tq, tk), F32),
                        pltpu.VMEM((2, nh, tq, tk), BF16), pltpu.VMEM((nh, tq, LANES), F32),
                        pltpu.VMEM((LANES, tq), F32), pltpu.VMEM((nh, tq, LANES), F32)],
        compiler_params=_params(("arbitrary", "arbitrary", "arbitrary")),
    )(qn, k, vt, _tri_matrix(tk, "after"))


def _window_sums(ext, rows, offset, forward):
    r = lax.broadcasted_iota(jnp.int32, (rows, rows + HALO), 0)
    e = lax.broadcasted_iota(jnp.int32, (rows, rows + HALO), 1)
    hi, lo = _split(ext)
    out = []
    for g, win in enumerate(POOL_WINDOWS):
        if forward:
            band = (e >= r) & (e < r + win)
        else:
            band = (e <= r + offset) & (e > r + offset - win)
        bm = band.astype(BF16)
        cols = slice(g * POOL_GROUP, (g + 1) * POOL_GROUP)
        out.append(_dot(bm, hi[:, cols]) + _dot(bm, lo[:, cols]))
    return out


def _window_counts(pos):
    return [jnp.minimum(pos + 1, win).astype(F32) for win in POOL_WINDOWS]


def _mixer_post(u, o, x, mod, g_post, g_fpre, w_pool, pool_scale, w_out, seq, tm):
    t_all, d = x.shape
    nt = seq // tm
    p = u.shape[1]

    def body(u_ref, halo_ref, o_ref, x_ref, mod_ref, gp_ref, gf_ref, wp_ref, ps_ref, wo_ref,
             pooled_ref, mixin_ref, mix_ref, x1_ref, h2_ref):
        it = pl.program_id(0) % nt
        uf = u_ref[...]
        halo = jnp.where(it == 0, 0.0, halo_ref[...])
        ext = jnp.concatenate([halo, uf], axis=0)
        pos = it * tm + lax.broadcasted_iota(jnp.int32, (tm, 1), 0)
        sums = _window_sums(ext, tm, HALO, False)
        cnts = _window_counts(pos)
        pools = []
        for g in range(len(POOL_WINDOWS)):
            cols = slice(g * POOL_GROUP, (g + 1) * POOL_GROUP)
            pooled = (sums[g] / cnts[g] - uf[:, cols]).astype(BF16)
            pooled_ref[:, cols] = pooled
            yg = _dot(pooled, wp_ref[g].astype(BF16))
            pools.append((yg * ps_ref[:, cols]).astype(BF16))
        mixin = jnp.concatenate([o_ref[...]] + pools, axis=1)
        mixin_ref[...] = mixin
        mix = _dot(mixin, wo_ref[...])
        mix_ref[...] = mix
        n2 = mix * _rms(mix)
        x1 = x_ref[...] + mod_ref[0, 2:3, :] * (n2 * gp_ref[...])
        x1_ref[...] = x1
        n3 = x1 * _rms(x1)
        h2 = (n3 * gf_ref[...]) * (1.0 + mod_ref[0, 4:5, :]) + mod_ref[0, 3:4, :]
        h2_ref[...] = h2.astype(BF16)

    tok = lambda i: (i, 0)
    const2 = lambda i: (0, 0)
    hb = tm // HALO
    return pl.pallas_call(
        body, name="mixer_post", grid=(t_all // tm,),
        in_specs=[pl.BlockSpec((tm, p), tok),
                  pl.BlockSpec((HALO, p), lambda i: (jnp.maximum(i * hb - 1, 0), 0)),
                  pl.BlockSpec((tm, p), tok),
                  pl.BlockSpec((tm, d), tok),
                  pl.BlockSpec((1, MOD_ROWS, d), lambda i: (i // nt, 0, 0)),
                  pl.BlockSpec((1, d), const2), pl.BlockSpec((1, d), const2),
                  pl.BlockSpec(w_pool.shape, lambda i: (0, 0, 0)),
                  pl.BlockSpec((1, p), const2),
                  pl.BlockSpec((d, d), const2)],
        out_specs=[pl.BlockSpec((tm, p), tok), pl.BlockSpec((tm, d), tok), pl.BlockSpec((tm, d), tok),
                   pl.BlockSpec((tm, d), tok), pl.BlockSpec((tm, d), tok)],
        out_shape=[jax.ShapeDtypeStruct((t_all, p), BF16), jax.ShapeDtypeStruct((t_all, d), BF16),
                   jax.ShapeDtypeStruct((t_all, d), F32), jax.ShapeDtypeStruct((t_all, d), F32),
                   jax.ShapeDtypeStruct((t_all, d), BF16)],
        compiler_params=_params(("arbitrary",)),
    )(u, u, o, x, mod, g_post, g_fpre, w_pool, pool_scale, w_out)


def _ffn_fwd(h2, w_g, w_u, w_d, x1, tgt, mod, g_post, seq, tm):
    t_all, d = x1.shape
    nt = seq // tm
    nk, _, ff = w_g.shape

    def body(h_ref, wg_ref, wu_ref, wd_ref, x1_ref, t_ref, mod_ref, g_ref,
             a_ref, b_ref, fin_ref, dy_ref, df_ref, loss_ref, accb_ref, accg_ref, facc):
        i, k = pl.program_id(0), pl.program_id(1)
        hb = h_ref[...]
        a = _dot(hb, wg_ref[0])
        b = _dot(hb, wu_ref[0])
        a_ref[0] = a.astype(BF16)
        b_ref[0] = b.astype(BF16)
        fin = ((a * _sigmoid(a)) * b).astype(BF16)
        fin_ref[0] = fin
        part = _dot(fin, wd_ref[0])

        @pl.when(k == 0)
        def _():
            facc[...] = part

        @pl.when(k > 0)
        def _():
            facc[...] += part

        @pl.when(k == nk - 1)
        def _():
            f = facc[...]
            r4 = _rms(f)
            n4 = f * r4
            gate = mod_ref[0, 5:6, :]
            g = g_ref[...]
            err = (x1_ref[...] + gate * (n4 * g)) - t_ref[...]
            dy = err * (1.0 / d)
            dy_ref[...] = dy

            @pl.when(i == 0)
            def _():
                loss_ref[...] = jnp.zeros_like(loss_ref)
                accg_ref[...] = jnp.zeros_like(accg_ref)

            @pl.when(i % nt == 0)
            def _():
                accb_ref[...] = jnp.zeros_like(accb_ref)

            loss_ref[...] += (0.5 / d) * jnp.sum(err * err)
            accb_ref[0, 0:1, :] += _colsum(dy * (n4 * g))
            accg_ref[0:1, :] += _colsum((dy * gate) * n4)
            dn4 = (dy * gate) * g
            df_ref[...] = _norm_bwd(dn4, n4, r4).astype(BF16)

    tok = lambda i, k: (i, 0)
    ktok = lambda i, k: (k, i, 0)
    kw = lambda i, k: (k, 0, 0)
    const2 = lambda i, k: (0, 0)
    return pl.pallas_call(
        body, name="ffn_fwd", grid=(t_all // tm, nk),
        in_specs=[pl.BlockSpec((tm, d), tok),
                  pl.BlockSpec((1, d, ff), kw), pl.BlockSpec((1, d, ff), kw), pl.BlockSpec((1, ff, d), kw),
                  pl.BlockSpec((tm, d), tok), pl.BlockSpec((tm, d), tok),
                  pl.BlockSpec((1, MOD_ROWS, d), lambda i, k: (i // nt, 0, 0)),
                  pl.BlockSpec((1, d), const2)],
        out_specs=[pl.BlockSpec((1, tm, ff), ktok)] * 3
        + [pl.BlockSpec((tm, d), tok), pl.BlockSpec((tm, d), tok),
           pl.BlockSpec((8, LANES), const2),
           pl.BlockSpec((1, 8, d), lambda i, k: (i // nt, 0, 0)),
           pl.BlockSpec((8, d), const2)],
        out_shape=[jax.ShapeDtypeStruct((nk, t_all, ff), BF16)] * 3
        + [jax.ShapeDtypeStruct((t_all, d), F32), jax.ShapeDtypeStruct((t_all, d), BF16),
           jax.ShapeDtypeStruct((8, LANES), F32),
           jax.ShapeDtypeStruct((t_all // seq, 8, d), F32),
           jax.ShapeDtypeStruct((8, d), F32)],
        scratch_shapes=[pltpu.VMEM((tm, d), F32)],
        compiler_params=_params(("arbitrary", "arbitrary")),
    )(h2, w_g, w_u, w_d, x1, tgt, mod, g_post)


def _ffn_bwd(df, a, b, w_d, w_g, w_u, x1, dy, mix, mod, g_fpre, g_mpost, seq, tm):
    t_all, d = x1.shape
    nt = seq // tm
    nk, _, ff = w_g.shape

    def body(df_ref, a_ref, b_ref, wd_ref, wg_ref, wu_ref, x1_ref, dy_ref, mix_ref, mod_ref, gf_ref, gm_ref,
             da_ref, db_ref, dx1_ref, dmix_ref, accb_ref, accg_ref, hacc):
        i, k = pl.program_id(0), pl.program_id(1)
        dfin = _dot_nt(df_ref[...], wd_ref[0])
        af = a_ref[0].astype(F32)
        bf = b_ref[0].astype(F32)
        sig = _sigmoid(af)
        da = ((dfin * bf) * (sig * (1.0 + af * (1.0 - sig)))).astype(BF16)
        db = (dfin * (af * sig)).astype(BF16)
        da_ref[0] = da
        db_ref[0] = db
        part = _dot_nt(da, wg_ref[0]) + _dot_nt(db, wu_ref[0])

        @pl.when(k == 0)
        def _():
            hacc[...] = part

        @pl.when(k > 0)
        def _():
            hacc[...] += part

        @pl.when(k == nk - 1)
        def _():
            @pl.when(i == 0)
            def _():
                accg_ref[...] = jnp.zeros_like(accg_ref)

            @pl.when(i % nt == 0)
            def _():
                accb_ref[...] = jnp.zeros_like(accb_ref)

            dh2 = hacc[...]
            x1 = x1_ref[...]
            r3 = _rms(x1)
            n3 = x1 * r3
            g3 = gf_ref[...]
            scale1 = 1.0 + mod_ref[0, 4:5, :]
            accb_ref[0, 0:1, :] += _colsum(dh2)
            accb_ref[0, 1:2, :] += _colsum(dh2 * (n3 * g3))
            accg_ref[0:1, :] += _colsum((dh2 * scale1) * n3)
            dx1 = dy_ref[...] + _norm_bwd((dh2 * scale1) * g3, n3, r3)
            dx1_ref[...] = dx1
            mix = mix_ref[...]
            r2 = _rms(mix)
            n2 = mix * r2
            g2 = gm_ref[...]
            gate = mod_ref[0, 2:3, :]
            accb_ref[0, 2:3, :] += _colsum(dx1 * (n2 * g2))
            accg_ref[1:2, :] += _colsum((dx1 * gate) * n2)
            dmix_ref[...] = _norm_bwd((dx1 * gate) * g2, n2, r2).astype(BF16)

    tok = lambda i, k: (i, 0)
    ktok = lambda i, k: (k, i, 0)
    kw = lambda i, k: (k, 0, 0)
    const2 = lambda i, k: (0, 0)
    return pl.pallas_call(
        body, name="ffn_bwd", grid=(t_all // tm, nk),
        in_specs=[pl.BlockSpec((tm, d), tok),
                  pl.BlockSpec((1, tm, ff), ktok), pl.BlockSpec((1, tm, ff), ktok),
                  pl.BlockSpec((1, ff, d), kw), pl.BlockSpec((1, d, ff), kw), pl.BlockSpec((1, d, ff), kw),
                  pl.BlockSpec((tm, d), tok), pl.BlockSpec((tm, d), tok), pl.BlockSpec((tm, d), tok),
                  pl.BlockSpec((1, MOD_ROWS, d), lambda i, k: (i // nt, 0, 0)),
                  pl.BlockSpec((1, d), const2), pl.BlockSpec((1, d), const2)],
        out_specs=[pl.BlockSpec((1, tm, ff), ktok)] * 2
        + [pl.BlockSpec((tm, d), tok), pl.BlockSpec((tm, d), tok),
           pl.BlockSpec((1, 8, d), lambda i, k: (i // nt, 0, 0)),
           pl.BlockSpec((8, d), const2)],
        out_shape=[jax.ShapeDtypeStruct((nk, t_all, ff), BF16)] * 2
        + [jax.ShapeDtypeStruct((t_all, d), F32), jax.ShapeDtypeStruct((t_all, d), BF16),
           jax.ShapeDtypeStruct((t_all // seq, 8, d), F32),
           jax.ShapeDtypeStruct((8, d), F32)],
        scratch_shapes=[pltpu.VMEM((tm, d), F32)],
        compiler_params=_params(("arbitrary", "arbitrary")),
    )(df, a, b, w_d, w_g, w_u, x1, dy, mix, mod, g_fpre, g_mpost)


def _mixer_bwd(dmix, w_out, pooled, w_pool, pool_scale, seq, tm):
    t_all, d = dmix.shape
    p = pooled.shape[1]
    ng = len(POOL_WINDOWS)

    def body(dm_ref, wo_ref, pooled_ref, wp_ref, ps_ref, do_ref, dpd_ref, dps_ref, dwp_ref):
        i = pl.program_id(0)

        @pl.when(i == 0)
        def _():
            dps_ref[...] = jnp.zeros_like(dps_ref)
            dwp_ref[...] = jnp.zeros_like(dwp_ref)

        dmixin = _dot_nt(dm_ref[...], wo_ref[...])
        do_ref[...] = dmixin[:, :p].astype(BF16)
        for g in range(ng):
            cols = slice(g * POOL_GROUP, (g + 1) * POOL_GROUP)
            dpool = dmixin[:, p + g * POOL_GROUP:p + (g + 1) * POOL_GROUP]
            pooled = pooled_ref[:, cols]
            wpg = wp_ref[g].astype(BF16)
            yg = _dot(pooled, wpg)
            dps_ref[0:1, cols] += _colsum(dpool * yg)
            dyg = (dpool * ps_ref[:, cols]).astype(BF16)
            dwp_ref[g] += _dot_tn(pooled, dyg)
            dpd_ref[:, cols] = _dot_nt(dyg, wpg)

    tok = lambda i: (i, 0)
    const2 = lambda i: (0, 0)
    const3 = lambda i: (0, 0, 0)
    return pl.pallas_call(
        body, name="mixer_bwd", grid=(t_all // tm,),
        in_specs=[pl.BlockSpec((tm, d), tok), pl.BlockSpec((d, d), const2), pl.BlockSpec((tm, p), tok),
                  pl.BlockSpec(w_pool.shape, const3), pl.BlockSpec((1, p), const2)],
        out_specs=[pl.BlockSpec((tm, p), tok), pl.BlockSpec((tm, p), tok),
                   pl.BlockSpec((8, p), const2), pl.BlockSpec(w_pool.shape, const3)],
        out_shape=[jax.ShapeDtypeStruct((t_all, p), BF16), jax.ShapeDtypeStruct((t_all, p), F32),
                   jax.ShapeDtypeStruct((8, p), F32), jax.ShapeDtypeStruct(w_pool.shape, F32)],
        compiler_params=_params(("arbitrary",)),
    )(dmix, w_out, pooled, w_pool, pool_scale)


def _attn_bwd(qn, k, kt, v, do, ltot, seq, tq, tk):
    t_all, w = qn.shape
    nb, nq, ndiag, nkb = t_all // seq, seq // tq, tq // tk, seq // tk
    assert ndiag % 2 == 0, "two key blocks per loop trip"
    rc = ATTN_ROW_CHUNK
    nh = HEADS_PER_BLOCK
    heads = range(nh)

    def body(q_ref, k_ref, kt_ref, v_ref, do_ref, l_ref, up_ref, bf_ref, dq_ref, dk_ref, dv_ref,
             z_buf, dw_buf, ls_buf, hl_buf, upto_buf, g_buf, gb_buf, before_buf, w_buf, dz_buf,
             totl_buf, totg_buf, rem_buf, preg_buf, qnt_buf, dot_buf, dq_t, dk_t, dv_t):
        i = pl.program_id(2)
        nblk = (i + 1) * ndiag

        @pl.when(i == 0)
        def _():
            dk_t[...] = jnp.zeros_like(dk_t)
            dv_t[...] = jnp.zeros_like(dv_t)

        lane = lax.broadcasted_iota(jnp.int32, (1, LANES), 1)
        sub = lax.broadcasted_iota(jnp.int32, (LANES, 1), 0)
        row = lax.broadcasted_iota(jnp.int32, (rc, tk), 0)
        col = lax.broadcasted_iota(jnp.int32, (rc, tk), 1)
        first = lane < HEAD_DIM
        upper = sub < HEAD_DIM
        q2 = q_ref[...]
        do2 = do_ref[...]
        l2 = l_ref[...]
        qs = [jnp.where(first, q2, jnp.zeros_like(q2)), jnp.where(first, jnp.zeros_like(q2), q2)]
        dos = [jnp.where(first, do2, jnp.zeros_like(do2)), jnp.where(first, jnp.zeros_like(do2), do2)]
        for src, dst in ((q2, qnt_buf), (do2, dot_buf)):
            t = src.astype(F32).T
            dst[:, 0:tq] = jnp.where(upper, t, 0.0).astype(BF16)
            dst[:, tq:2 * tq] = jnp.where(upper, 0.0, t).astype(BF16)
        for h in heads:
            rem_buf[h] = jnp.where(first if h == 0 else ~first, l2, pltpu.roll(l2, HEAD_DIM, 1))
        preg_buf[...] = jnp.zeros_like(preg_buf)
        dq_t[...] = jnp.zeros_like(dq_t)
        w_buf[1] = jnp.zeros((nh * tq, tk), BF16)
        dz_buf[1] = jnp.zeros((nh * tq, tk), BF16)

        def causal(c, diag):
            return (col + diag * tk) < (row + c * rc)

        def scores(blk, slot):
            off = pl.multiple_of(blk * tk, tk)
            kj = k_ref[pl.ds(off, tk), :]
            vj = v_ref[pl.ds(off, tk), :]
            for h in heads:
                z_buf[slot, h] = _dot_nt(qs[h], kj)
                dw_buf[slot, h] = _dot_nt(dos[h], vj)

        def gradients(blk, slot):
            off = pl.multiple_of(blk * tk, tk)
            ktj = kt_ref[:, pl.ds(off, tk)]
            zero = jnp.zeros_like(ktj)
            dq_t[...] += (_dot_nt(jnp.where(upper, ktj, zero), dz_buf[slot, 0:tq, :])
                          + _dot_nt(jnp.where(upper, zero, ktj), dz_buf[slot, tq:2 * tq, :]))
            dk_t[blk] += _dot(qnt_buf[...], dz_buf[slot])
            dv_t[blk] += _dot(dot_buf[...], w_buf[slot])

        def softplus_stage(h, slot, diag):
            for c in range(tq // rc):
                rows = slice(c * rc, (c + 1) * rc)
                if _all_masked(c, diag, rc, tk):
                    hl_buf[h, rows, :] = jnp.zeros((rc, 2 * tk), BF16)
                    continue
                nz = z_buf[slot, h, rows, :]
                l1 = jnp.minimum(nz, 0.0) - jnp.log(1.0 + jnp.exp(-jnp.abs(nz)))
                if _some_masked(c, diag, rc, tk):
                    l1 = jnp.where(causal(c, diag), l1, 0.0)
                hi, lo = _split(l1)
                hl_buf[h, rows, 0:tk] = hi
                hl_buf[h, rows, tk:2 * tk] = lo
                ls_buf[h, rows, :] = l1 - nz
                totl_buf[h, rows, :] = _row_sums(l1)

        def weights_stage(h, slot, diag):
            for c in range(tq // rc):
                rows = slice(c * rc, (c + 1) * rc)
                stacked = slice(h * tq + c * rc, h * tq + (c + 1) * rc)
                if _all_masked(c, diag, rc, tk):
                    w_buf[slot, stacked, :] = jnp.zeros((rc, tk), BF16)
                    gb_buf[h, rows, :] = jnp.zeros((rc, tk), BF16)
                    continue
                wgt = jnp.exp(ls_buf[h, rows, :] + (_across(rem_buf[h, rows, :], tk) - upto_buf[h, rows, :]))
                if _some_masked(c, diag, rc, tk):
                    wgt = jnp.where(causal(c, diag), wgt, 0.0)
                w_buf[slot, stacked, :] = wgt.astype(BF16)
                g = wgt * dw_buf[slot, h, rows, :]
                g_buf[h, rows, :] = g
                gb_buf[h, rows, :] = g.astype(BF16)
                totg_buf[h, rows, :] = _row_sums(g)
                rem_buf[h, rows, :] -= totl_buf[h, rows, :]

        def dscore_stage(h, slot, diag):
            for c in range(tq // rc):
                rows = slice(c * rc, (c + 1) * rc)
                stacked = slice(h * tq + c * rc, h * tq + (c + 1) * rc)
                if _all_masked(c, diag, rc, tk):
                    dz_buf[slot, stacked, :] = jnp.zeros((rc, tk), BF16)
                    continue
                sig = jnp.exp(ls_buf[h, rows, :])
                g = g_buf[h, rows, :]
                dnz = sig * (before_buf[h, rows, :] + _across(preg_buf[h, rows, :], tk)) - g * (1.0 - sig)
                if _some_masked(c, diag, rc, tk):
                    dnz = jnp.where(causal(c, diag), dnz, 0.0)
                dz_buf[slot, stacked, :] = dnz.astype(BF16)
                preg_buf[h, rows, :] += totg_buf[h, rows, :]

        def position(blk, slot, diag, prefetch):
            if prefetch:
                scores(blk + 1, 1 - slot)
            for h in heads:
                softplus_stage(h, slot, diag)
                upto_buf[h] = _dot(hl_buf[h], up_ref[...])
            gradients(jnp.maximum(blk - 1, 0), 1 - slot)
            for h in heads:
                weights_stage(h, slot, diag)
                before_buf[h] = _dot(gb_buf[h], bf_ref[...])
            for h in heads:
                dscore_stage(h, slot, diag)

        scores(0, 0)

        def trip(jj, carry):
            for u in range(2):
                position(2 * jj + u, u, None, True)
            return carry

        lax.fori_loop(0, (i * ndiag) // 2, trip, 0)
        for d in range(ndiag):
            position(i * ndiag + d, d % 2, d, d < ndiag - 1)
        gradients(nblk - 1, 1)
        dq_ref[...] = (dq_t[...].T * NEG_QK_SCALE).astype(BF16)

        @pl.when(i == nq - 1)
        def _():
            for blk in range(nkb):
                dk_ref[blk * tk:(blk + 1) * tk, :] = dk_t[blk].T.astype(BF16)
                dv_ref[blk * tk:(blk + 1) * tk, :] = dv_t[blk].T.astype(BF16)

    qmap = lambda b, hp, i: (b * nq + i, hp)
    kmap = lambda b, hp, i: (b, hp)
    const = lambda b, hp, i: (0, 0)
    return pl.pallas_call(
        body, name="attn_bwd", grid=(nb, w // LANES, nq),
        in_specs=[pl.BlockSpec((tq, LANES), qmap), pl.BlockSpec((seq, LANES), kmap),
                  pl.BlockSpec((LANES, seq), lambda b, hp, i: (hp, b)), pl.BlockSpec((seq, LANES), kmap),
                  pl.BlockSpec((tq, LANES), qmap), pl.BlockSpec((tq, LANES), qmap),
                  pl.BlockSpec((2 * tk, tk), const), pl.BlockSpec((tk, tk), const)],
        out_specs=[pl.BlockSpec((tq, LANES), qmap), pl.BlockSpec((seq, LANES), kmap), pl.BlockSpec((seq, LANES), kmap)],
        out_shape=[jax.ShapeDtypeStruct((t_all, w), BF16)] * 3,
        scratch_shapes=[pltpu.VMEM((2, nh, tq, tk), F32), pltpu.VMEM((2, nh, tq, tk), F32),
                        pltpu.VMEM((nh, tq, tk), F32), pltpu.VMEM((nh, tq, 2 * tk), BF16),
                        pltpu.VMEM((nh, tq, tk), F32), pltpu.VMEM((nh, tq, tk), F32),
                        pltpu.VMEM((nh, tq, tk), BF16), pltpu.VMEM((nh, tq, tk), F32),
                        pltpu.VMEM((2, nh * tq, tk), BF16), pltpu.VMEM((2, nh * tq, tk), BF16),
                        pltpu.VMEM((nh, tq, LANES), F32), pltpu.VMEM((nh, tq, LANES), F32),
                        pltpu.VMEM((nh, tq, LANES), F32), pltpu.VMEM((nh, tq, LANES), F32),
                        pltpu.VMEM((LANES, nh * tq), BF16), pltpu.VMEM((LANES, nh * tq), BF16),
                        pltpu.VMEM((LANES, tq), F32), pltpu.VMEM((nkb, LANES, tk), F32),
                        pltpu.VMEM((nkb, LANES, tk), F32)],
        compiler_params=_params(("arbitrary", "arbitrary", "arbitrary")),
    )(qn, k, kt, v, do, ltot, _tri_matrix(tk, "upto"), _tri_matrix(tk, "before")[:tk])


def _inproj_bwd(dq, dk, dv, dpd, x, dx1, mod, g_pre, w_in, seq, tm):
    t_all, d = x.shape
    nt = seq // tm
    p = dq.shape[1]

    def body(dq_ref, dk_ref, dv_ref, dpd_ref, halo_ref, x_ref, dx1_ref, mod_ref, g_ref, w_ref,
             gx_ref, du_ref, accb_ref, accg_ref):
        i = pl.program_id(0)
        it = i % nt

        @pl.when(i == 0)
        def _():
            accg_ref[...] = jnp.zeros_like(accg_ref)

        @pl.when(it == 0)
        def _():
            accb_ref[...] = jnp.zeros_like(accb_ref)

        dpd = dpd_ref[...]
        pos = it * tm + lax.broadcasted_iota(jnp.int32, (tm, 1), 0)
        cnts = _window_counts(pos)
        halo = jnp.where(it == nt - 1, 0.0, halo_ref[...])
        scaled = []
        halos = []
        for g, win in enumerate(POOL_WINDOWS):
            cols = slice(g * POOL_GROUP, (g + 1) * POOL_GROUP)
            scaled.append(dpd[:, cols] / cnts[g])
            halos.append(halo[:, cols] / float(win))
        ext = jnp.concatenate([jnp.concatenate(scaled, axis=1), jnp.concatenate(halos, axis=1)], axis=0)
        sums = _window_sums(ext, tm, 0, True)
        du = (jnp.concatenate(sums, axis=1) - dpd).astype(BF16)
        du_ref[...] = du
        dh1 = (_dot_nt(dq_ref[...], w_ref[0]) + _dot_nt(dk_ref[...], w_ref[1])
               + _dot_nt(dv_ref[...], w_ref[2]) + _dot_nt(du, w_ref[3]))
        xf = x_ref[...]
        r1 = _rms(xf)
        n1 = xf * r1
        g1 = g_ref[...]
        scale1 = 1.0 + mod_ref[0, 1:2, :]
        accb_ref[0, 0:1, :] += _colsum(dh1)
        accb_ref[0, 1:2, :] += _colsum(dh1 * (n1 * g1))
        accg_ref[0:1, :] += _colsum((dh1 * scale1) * n1)
        gx_ref[...] = dx1_ref[...] + _norm_bwd((dh1 * scale1) * g1, n1, r1)

    tok = lambda i: (i, 0)
    const2 = lambda i: (0, 0)
    hb = tm // HALO
    last = t_all // HALO - 1
    return pl.pallas_call(
        body, name="inproj_bwd", grid=(t_all // tm,),
        in_specs=[pl.BlockSpec((tm, p), tok), pl.BlockSpec((tm, p), tok), pl.BlockSpec((tm, p), tok),
                  pl.BlockSpec((tm, p), tok),
                  pl.BlockSpec((HALO, p), lambda i: (jnp.minimum((i + 1) * hb, last), 0)),
                  pl.BlockSpec((tm, d), tok), pl.BlockSpec((tm, d), tok),
                  pl.BlockSpec((1, MOD_ROWS, d), lambda i: (i // nt, 0, 0)),
                  pl.BlockSpec((1, d), const2),
                  pl.BlockSpec((N_CHIPS, d, p), lambda i: (0, 0, 0))],
        out_specs=[pl.BlockSpec((tm, d), tok), pl.BlockSpec((tm, p), tok),
                   pl.BlockSpec((1, 8, d), lambda i: (i // nt, 0, 0)),
                   pl.BlockSpec((8, d), const2)],
        out_shape=[jax.ShapeDtypeStruct((t_all, d), F32), jax.ShapeDtypeStruct((t_all, p), BF16),
                   jax.ShapeDtypeStruct((t_all // seq, 8, d), F32),
                   jax.ShapeDtypeStruct((8, d), F32)],
        compiler_params=_params(("arbitrary",)),
    )(dq, dk, dv, dpd, dpd, x, dx1, mod, g_pre, w_in)


def _tn_matmul(x, ys, nk, bt, name):
    t_all = x.shape[-2]
    m = x.shape[-1]
    ny = len(ys)

    def spec(arr):
        if arr.ndim == 3:
            return pl.BlockSpec((1, bt, arr.shape[-1]), lambda k, t: (k, t, 0))
        return pl.BlockSpec((bt, arr.shape[-1]), lambda k, t: (t, 0))

    def tile(ref):
        return ref[0] if len(ref.shape) == 3 else ref[...]

    def body(*refs):
        x_ref, y_refs, o_refs = refs[0], refs[1:1 + ny], refs[1 + ny:]
        t = pl.program_id(1)
        xt = tile(x_ref)
        for y_ref, o_ref in zip(y_refs, o_refs):
            part = _dot_tn(xt, tile(y_ref))

            @pl.when(t == 0)
            def _(o_ref=o_ref, part=part):
                o_ref[0] = part

            @pl.when(t > 0)
            def _(o_ref=o_ref, part=part):
                o_ref[0] += part

    return pl.pallas_call(
        body, name=name, grid=(nk, t_all // bt),
        in_specs=[spec(x)] + [spec(y) for y in ys],
        out_specs=[pl.BlockSpec((1, m, y.shape[-1]), lambda k, t: (k, 0, 0)) for y in ys],
        out_shape=[jax.ShapeDtypeStruct((nk, m, y.shape[-1]), F32) for y in ys],
        compiler_params=_params(("arbitrary", "arbitrary")),
    )(x, *ys)


def _cond_fwd(c_all, w_q, b_q, bn):
    nrow, d = c_all.shape
    ncol = w_q.shape[1]

    def body(c_ref, w_ref, b_ref, sc_ref, mod_ref):
        cf = c_ref[...]
        sc = cf * _sigmoid(cf)
        sc_ref[...] = sc
        shi, slo = _split(sc)
        whi, wlo = _split(w_ref[...])
        mod_ref[...] = (_dot(shi, whi) + _dot(shi, wlo) + _dot(slo, whi)) + b_ref[...]

    return pl.pallas_call(
        body, name="cond_fwd", grid=(ncol // bn,),
        in_specs=[pl.BlockSpec((nrow, d), lambda n: (0, 0)), pl.BlockSpec((d, bn), lambda n: (0, n)),
                  pl.BlockSpec((1, bn), lambda n: (0, n))],
        out_specs=[pl.BlockSpec((nrow, d), lambda n: (0, 0)), pl.BlockSpec((nrow, bn), lambda n: (0, n))],
        out_shape=[jax.ShapeDtypeStruct((nrow, d), F32), jax.ShapeDtypeStruct((nrow, ncol), F32)],
        compiler_params=_params(("arbitrary",)),
    )(c_all, w_q, b_q)


def _cond_bwd(sc_all, dmod_q, bn):
    nrow, d = sc_all.shape
    ncol = dmod_q.shape[1]

    def body(sc_ref, dm_ref, gw_ref):
        shi, slo = _split(sc_ref[...])
        dhi, dlo = _split(dm_ref[...])
        gw_ref[...] = _dot_tn(shi, dhi) + _dot_tn(shi, dlo) + _dot_tn(slo, dhi)

    return pl.pallas_call(
        body, name="cond_bwd", grid=(ncol // bn,),
        in_specs=[pl.BlockSpec((nrow, d), lambda n: (0, 0)), pl.BlockSpec((nrow, bn), lambda n: (0, n))],
        out_specs=pl.BlockSpec((d, bn), lambda n: (0, n)),
        out_shape=jax.ShapeDtypeStruct((d, ncol), F32),
        compiler_params=_params(("arbitrary",)),
    )(sc_all, dmod_q)


def _row_block(rows, cols, budget=1 << 18):
    best = None
    for br in range(8, rows + 1, 8):
        if rows % br == 0 and br * cols <= budget:
            best = br
    return best if best is not None else rows


def _adamw(w, g, m, v, name):
    rows, cols = w.shape
    br = _row_block(rows, cols)
    c1 = 1.0 - ADAM_B1 ** ADAM_STEP
    c2 = 1.0 - ADAM_B2 ** ADAM_STEP

    def body(w_ref, g_ref, m_ref, v_ref, d_ref, nm_ref, nv_ref):
        gf = g_ref[...]
        m2 = ADAM_B1 * m_ref[...] + (1.0 - ADAM_B1) * gf
        v2 = ADAM_B2 * v_ref[...] + (1.0 - ADAM_B2) * (gf * gf)
        nm_ref[...] = m2
        nv_ref[...] = v2
        d_ref[...] = -ADAM_LR * ((m2 / c1) / (jnp.sqrt(v2 / c2) + ADAM_EPS) + ADAM_WD * w_ref[...])

    blk = pl.BlockSpec((br, cols), lambda i: (i, 0))
    return pl.pallas_call(
        body, name=name, grid=(rows // br,),
        in_specs=[blk] * 4, out_specs=[blk] * 3,
        out_shape=[jax.ShapeDtypeStruct((rows, cols), F32)] * 3,
        compiler_params=_params(("arbitrary",)),
    )(w, g, m, v)


def _all_gather(x_shard, name):
    m_per, n = x_shard.shape

    def body(x_ref, out_ref, send_sems, recv_sems, local_sem):
        x, y, c = _position()
        me, sibling = (x, y, c), (x, y, 1 - c)
        chips = [(1 - x, y), (x, 1 - y), (1 - x, 1 - y)]

        def rows(px, py, pc):
            return out_ref.at[pl.ds((4 * px + 2 * py + pc) * m_per, m_per), :]

        def copy(k, block, to, src=None):
            return pltpu.make_async_remote_copy(
                src_ref=rows(*block) if src is None else src, dst_ref=rows(*block),
                send_sem=send_sems.at[k], recv_sem=recv_sems.at[k], device_id=to, device_id_type=MESH)

        mine = pltpu.make_async_copy(x_ref, rows(*me), local_sem)
        mine.start()
        first = [copy(0, me, sibling, src=x_ref)]
        first += [copy(1 + j, me, (*chip, c), src=x_ref) for j, chip in enumerate(chips)]
        for cp in first:
            cp.start()
        passed = [copy(4 + j, (*chip, c), sibling) for j, chip in enumerate(chips)]
        for j, chip in enumerate(chips):
            copy(1 + j, (*chip, c), me).wait_recv()
            passed[j].start()
        copy(0, sibling, me).wait_recv()
        for j, chip in enumerate(chips):
            copy(4 + j, (*chip, 1 - c), me).wait_recv()
        for cp in first + passed:
            cp.wait_send()
        mine.wait()

    return pl.pallas_call(
        body, name=name,
        out_shape=jax.ShapeDtypeStruct((N_DEV * m_per, n), x_shard.dtype),
        in_specs=[pl.BlockSpec(memory_space=pltpu.VMEM)],
        out_specs=pl.BlockSpec(memory_space=pltpu.VMEM),
        scratch_shapes=[pltpu.SemaphoreType.DMA((7,)), pltpu.SemaphoreType.DMA((7,)), pltpu.SemaphoreType.DMA],
        compiler_params=pltpu.CompilerParams(vmem_limit_bytes=VMEM_LIMIT),
    )(x_shard)


_ANY = pl.BlockSpec(memory_space=pl.ANY)


def _place_quarters(place, quarters):
    steps = 2

    def body(place_ref, *refs):
        n = len(refs) // 2
        for w_ref, o_ref in zip(refs[:n], refs[n:]):
            o_ref[0] = w_ref[...].astype(BF16)

    return pl.pallas_call(
        body, name="place_quarters",
        grid_spec=pltpu.PrefetchScalarGridSpec(
            num_scalar_prefetch=1, grid=(steps,),
            in_specs=[pl.BlockSpec((q.shape[0] // steps, q.shape[1]), lambda r, place_ref: (r, 0)) for q in quarters],
            out_specs=[pl.BlockSpec((1, q.shape[0] // steps, q.shape[1]), lambda r, place_ref: (place_ref[0], r, 0))
                       for q in quarters]),
        out_shape=[jax.ShapeDtypeStruct((N_CHIPS,) + q.shape, BF16) for q in quarters],
        compiler_params=_params(("arbitrary",)),
    )(place, *quarters)


def _gather_weights(placed):
    n = len(placed)
    shapes = [b.shape[1:] for b in placed]

    def body(*refs):
        g_refs = refs[n:2 * n]
        send_sems, recv_sems = refs[2 * n:]
        x, y, c = _position()
        sibling = (x, y, 1 - c)
        chips = [(1 - x, y), (x, 1 - y), (1 - x, 1 - y)]
        mine = 2 * x + y

        def half(a, which):
            hr = shapes[a][0] // 2
            return pl.ds(which * hr, hr)

        def over_ici(a, p, slot):
            ref = g_refs[a].at[slot, half(a, c), :]
            return pltpu.make_async_remote_copy(
                src_ref=ref, dst_ref=ref,
                send_sem=send_sems.at[6 * a + p], recv_sem=recv_sems.at[6 * a + p],
                device_id=(*chips[p], c), device_id_type=MESH)

        def over_d2d(a, p, slot, which):
            ref = g_refs[a].at[slot, half(a, which), :]
            return pltpu.make_async_remote_copy(
                src_ref=ref, dst_ref=ref,
                send_sem=send_sems.at[6 * a + 3 + p], recv_sem=recv_sems.at[6 * a + 3 + p],
                device_id=sibling, device_id_type=MESH)

        sends = []
        for a in range(n):
            for p in range(3):
                cp = over_ici(a, p, mine)
                cp.start()
                sends.append(cp)
        for a in range(n):
            for p, (cx, cy) in enumerate(chips):
                slot = 2 * cx + cy
                over_ici(a, p, slot).wait_recv()
                cp = over_d2d(a, p, slot, c)
                cp.start()
                sends.append(cp)
        for a in range(n):
            for p, (cx, cy) in enumerate(chips):
                over_d2d(a, p, 2 * cx + cy, 1 - c).wait_recv()
        for cp in sends:
            cp.wait_send()

    return pl.pallas_call(
        body, name="gather_weights",
        out_shape=[jax.ShapeDtypeStruct(b.shape, BF16) for b in placed],
        in_specs=[_ANY] * n, out_specs=[_ANY] * n,
        input_output_aliases={a: a for a in range(n)},
        scratch_shapes=[pltpu.SemaphoreType.DMA((6 * n,)), pltpu.SemaphoreType.DMA((6 * n,))],
    )(*placed)


_HBM = pl.BlockSpec(memory_space=pltpu.HBM)
_SEM = pl.BlockSpec(memory_space=pltpu.SEMAPHORE)
_EFFECT = pltpu.SideEffectType.DATAFLOW_SIDE_EFFECTING


def _quarter_halves(shapes, a, which):
    hr = shapes[a][0] // 2
    return pl.ds(which * hr, hr)


def _gather_start(placed):
    n = len(placed)
    shapes = [b.shape[1:] for b in placed]

    def body(*refs):
        g_refs = refs[:n]
        send_sems, recv_sems = refs[n], refs[n + 1]
        token = refs[2 * n + 2]
        x, y, c = _position()
        chips = [(1 - x, y), (x, 1 - y), (1 - x, 1 - y)]
        mine = 2 * x + y
        for a in range(n):
            ref = g_refs[a].at[mine, _quarter_halves(shapes, a, c), :]
            for p in range(3):
                pltpu.make_async_remote_copy(
                    src_ref=ref, dst_ref=ref, send_sem=send_sems.at[3 * a + p], recv_sem=recv_sems.at[3 * a + p],
                    device_id=(*chips[p], c), device_id_type=MESH).start()
        token[...] = jnp.zeros_like(token)

    out = pl.pallas_call(
        body, name="gather_start",
        out_shape=(pltpu.SemaphoreType.DMA((3 * n,)), pltpu.SemaphoreType.DMA((3 * n,)),
                   *[pltpu.HBM(b.shape, b.dtype) for b in placed], jax.ShapeDtypeStruct((8, LANES), F32)),
        in_specs=[_HBM] * n,
        out_specs=(_SEM, _SEM, *[_HBM] * n, pl.BlockSpec(memory_space=pltpu.VMEM)),
        input_output_aliases={a: 2 + a for a in range(n)},
        compiler_params=pltpu.CompilerParams(has_side_effects=_EFFECT),
    )(*[pltpu.with_memory_space_constraint(b, pltpu.HBM) for b in placed])
    return out[0], out[1], list(out[2:2 + n]), out[2 + n]


def _gather_wait(send_sems, recv_sems, thru, after):
    n = len(thru)
    shapes = [b.shape[1:] for b in thru]

    def body(*refs):
        g_refs = refs[:n]
        send_sems, recv_sems = refs[n], refs[n + 1]
        x, y, c = _position()
        chips = [(1 - x, y), (x, 1 - y), (1 - x, 1 - y)]
        mine = 2 * x + y
        for a in range(n):
            rows = _quarter_halves(shapes, a, c)
            for p, (cx, cy) in enumerate(chips):
                copy = pltpu.make_async_remote_copy(
                    src_ref=g_refs[a].at[mine, rows, :], dst_ref=g_refs[a].at[2 * cx + cy, rows, :],
                    send_sem=send_sems.at[3 * a + p], recv_sem=recv_sems.at[3 * a + p],
                    device_id=(cx, cy, c), device_id_type=MESH)
                copy.wait_send()
                copy.wait_recv()

    return pl.pallas_call(
        body, name="gather_wait",
        out_shape=[pltpu.HBM(b.shape, b.dtype) for b in thru],
        in_specs=[_HBM] * n + [_SEM, _SEM, _ANY], out_specs=[_HBM] * n,
        input_output_aliases={a: a for a in range(n)},
        compiler_params=pltpu.CompilerParams(has_side_effects=_EFFECT),
    )(*thru, send_sems, recv_sems, after)


def _gather_forward(bufs):
    n = len(bufs)
    shapes = [b.shape[1:] for b in bufs]

    def body(*refs):
        g_refs = refs[n:2 * n]
        send_sems, recv_sems = refs[2 * n:]
        x, y, c = _position()
        chips = [(1 - x, y), (x, 1 - y), (1 - x, 1 - y)]

        def over_d2d(a, p, which):
            cx, cy = chips[p]
            ref = g_refs[a].at[2 * cx + cy, _quarter_halves(shapes, a, which), :]
            return pltpu.make_async_remote_copy(
                src_ref=ref, dst_ref=ref, send_sem=send_sems.at[3 * a + p], recv_sem=recv_sems.at[3 * a + p],
                device_id=(x, y, 1 - c), device_id_type=MESH)

        sends = [over_d2d(a, p, c) for a in range(n) for p in range(3)]
        for cp in sends:
            cp.start()
        for a in range(n):
            for p in range(3):
                over_d2d(a, p, 1 - c).wait_recv()
        for cp in sends:
            cp.wait_send()

    return pl.pallas_call(
        body, name="gather_forward",
        out_shape=[jax.ShapeDtypeStruct(b.shape, BF16) for b in bufs],
        in_specs=[_ANY] * n, out_specs=[_ANY] * n,
        input_output_aliases={a: a for a in range(n)},
        scratch_shapes=[pltpu.SemaphoreType.DMA((3 * n,)), pltpu.SemaphoreType.DMA((3 * n,))],
    )(*bufs)


def _sibling_exchange(grads):
    n = len(grads)
    shapes = [g.shape for g in grads]

    def body(*refs):
        g_refs, x_refs = refs[:n], refs[n:2 * n]
        send_sems, recv_sems = refs[2 * n:]
        x, y, c = _position()
        copies = []
        for a in range(n):
            hr = shapes[a][1] // 2
            cp = pltpu.make_async_remote_copy(
                src_ref=g_refs[a].at[:, pl.ds((1 - c) * hr, hr), :], dst_ref=x_refs[a],
                send_sem=send_sems.at[a], recv_sem=recv_sems.at[a],
                device_id=(x, y, 1 - c), device_id_type=MESH)
            cp.start()
            copies.append(cp)
        for cp in copies:
            cp.wait()

    return pl.pallas_call(
        body, name="grad_sibling_exchange",
        out_shape=[jax.ShapeDtypeStruct((s[0], s[1] // 2, s[2]), F32) for s in shapes],
        in_specs=[_ANY] * n, out_specs=[_ANY] * n,
        scratch_shapes=[pltpu.SemaphoreType.DMA((n,)), pltpu.SemaphoreType.DMA((n,))],
    )(*grads)


def _chip_sums(core, grads, theirs):
    n = len(grads)

    def body(core_ref, *refs):
        g_refs, t_refs, o_refs = refs[:n], refs[n:2 * n], refs[2 * n:]
        for g_ref, t_ref, o_ref in zip(g_refs, t_refs, o_refs):
            o_ref[...] = (g_ref[...] + t_ref[...]).astype(BF16)

    in_specs = [pl.BlockSpec((1, g.shape[1] // 2, g.shape[2]), lambda k, core_ref: (k, core_ref[0], 0)) for g in grads]
    in_specs += [pl.BlockSpec((1,) + t.shape[1:], lambda k, core_ref: (k, 0, 0)) for t in theirs]
    return pl.pallas_call(
        body, name="grad_chip_sums",
        grid_spec=pltpu.PrefetchScalarGridSpec(
            num_scalar_prefetch=1, grid=(N_CHIPS,), in_specs=in_specs,
            out_specs=[pl.BlockSpec((1,) + t.shape[1:], lambda k, core_ref: (k, 0, 0)) for t in theirs]),
        out_shape=[jax.ShapeDtypeStruct(t.shape, BF16) for t in theirs],
        compiler_params=_params(("arbitrary",)),
    )(core, *grads, *theirs)


def _chip_exchange(sums):
    n = len(sums)

    def body(*refs):
        s_refs, y_refs = refs[:n], refs[n:2 * n]
        send_sems, recv_sems = refs[2 * n:]
        x, y, c = _position()
        chips = [(1 - x, y), (x, 1 - y), (1 - x, 1 - y)]
        copies = []
        for a in range(n):
            for p, (cx, cy) in enumerate(chips):
                cp = pltpu.make_async_remote_copy(
                    src_ref=s_refs[a].at[2 * cx + cy], dst_ref=y_refs[a].at[p],
                    send_sem=send_sems.at[3 * a + p], recv_sem=recv_sems.at[3 * a + p],
                    device_id=(cx, cy, c), device_id_type=MESH)
                cp.start()
                copies.append(cp)
        for cp in copies:
            cp.wait()

    return pl.pallas_call(
        body, name="grad_chip_exchange",
        out_shape=[jax.ShapeDtypeStruct((3,) + s.shape[1:], BF16) for s in sums],
        in_specs=[_ANY] * n, out_specs=[_ANY] * n,
        scratch_shapes=[pltpu.SemaphoreType.DMA((3 * n,)), pltpu.SemaphoreType.DMA((3 * n,))],
    )(*sums)


def _total_sums(place, sums, parts):
    n = len(parts)
    steps = 2

    def body(place_ref, *refs):
        for s_ref, y_ref, o_ref in zip(refs[:n], refs[n:2 * n], refs[2 * n:]):
            o_ref[0] = ((s_ref[0].astype(F32) + y_ref[0].astype(F32)) + y_ref[1].astype(F32)) + y_ref[2].astype(F32)

    def step_rows(pt):
        return pt.shape[1] // steps

    in_specs = [pl.BlockSpec((1, step_rows(s), s.shape[2]), lambda r, place_ref: (place_ref[0], r, 0)) for s in sums]
    in_specs += [pl.BlockSpec((3, step_rows(pt), pt.shape[2]), lambda r, place_ref: (0, r, 0)) for pt in parts]
    return pl.pallas_call(
        body, name="grad_total_sums",
        grid_spec=pltpu.PrefetchScalarGridSpec(
            num_scalar_prefetch=1, grid=(steps,), in_specs=in_specs,
            out_specs=[pl.BlockSpec((1, step_rows(pt), pt.shape[2]), lambda r, place_ref: (place_ref[1], r, 0))
                       for pt in parts]),
        out_shape=[jax.ShapeDtypeStruct((2,) + pt.shape[1:], F32) for pt in parts],
        compiler_params=_params(("arbitrary",)),
    )(place, *sums, *parts)


def _sibling_share(halves):
    n = len(halves)

    def body(*refs):
        f_refs = refs[n:2 * n]
        send_sems, recv_sems = refs[2 * n:]
        x, y, c = _position()
        copies = []
        for a in range(n):
            cp = pltpu.make_async_remote_copy(
                src_ref=f_refs[a].at[c], dst_ref=f_refs[a].at[c], send_sem=send_sems.at[a], recv_sem=recv_sems.at[a],
                device_id=(x, y, 1 - c), device_id_type=MESH)
            cp.start()
            copies.append(cp)
        for a, cp in enumerate(copies):
            cp.wait_send()
            pltpu.make_async_remote_copy(
                src_ref=f_refs[a].at[1 - c], dst_ref=f_refs[a].at[1 - c], send_sem=send_sems.at[a],
                recv_sem=recv_sems.at[a], device_id=(x, y, c), device_id_type=MESH).wait_recv()

    return pl.pallas_call(
        body, name="grad_sibling_share",
        out_shape=[jax.ShapeDtypeStruct(h.shape, F32) for h in halves],
        in_specs=[_ANY] * n, out_specs=[_ANY] * n,
        input_output_aliases={a: a for a in range(n)},
        scratch_shapes=[pltpu.SemaphoreType.DMA((n,)), pltpu.SemaphoreType.DMA((n,))],
    )(*halves)


def _group_sum(stacked, nrow, name):
    total, n = stacked.shape
    groups = total // nrow

    def body(g_ref, o_ref):
        acc = g_ref[0:nrow, :]
        for grp in range(1, groups):
            acc = acc + g_ref[grp * nrow:(grp + 1) * nrow, :]
        o_ref[...] = acc

    return pl.pallas_call(
        body, name=name,
        out_shape=jax.ShapeDtypeStruct((nrow, n), F32),
        compiler_params=pltpu.CompilerParams(vmem_limit_bytes=VMEM_LIMIT),
    )(stacked)


def _local_step(xt, tgt, mod, gains, w_pool, pool_scale, w_in, later_weights, seq):
    g_mpre, g_mpost, g_fpre, g_fpost = gains
    d = xt.shape[1]
    tm, tq = min(TOKEN_TILE, seq), min(ATTN_TILE, seq)

    h1, qn, k, v, u, kt, vt = _prenorm_proj(xt, mod, g_mpre, w_in, seq, tm)
    tk = min(ATTN_KEY_TILE, tq // 2)
    o, ltot = _attn_fwd(qn, k, vt, seq, tq, tk)
    w_out, w_g, w_u, w_d = later_weights(o)
    w_out2 = w_out.reshape(d, d)
    pooled, mixin, mix, x1, h2 =_mixer_post(u, o, xt, mod, g_mpost, g_fpre, w_pool, pool_scale, w_out2, seq, tm)
    a, b, fin, dy, df, loss_blk, accb4, accg4 = _ffn_fwd(h2, w_g, w_u, w_d, x1, tgt, mod, g_fpost, seq, tm)
    da, db, dx1, dmix, accb5, accg5 = _ffn_bwd(df, a, b, w_d, w_g, w_u, x1, dy, mix, mod, g_fpre, g_mpost, seq, tm)
    do, dpd, dps, dwp = _mixer_bwd(dmix, w_out2, pooled, w_pool, pool_scale, seq, tm)
    dq, dk, dv = _attn_bwd(qn, k, kt, v, do, ltot, seq, tq, tk)
    gx, du, accb8, accg8 = _inproj_bwd(dq, dk, dv, dpd, xt, dx1, mod, g_mpre, w_in, seq, tm)

    g_in = jnp.concatenate(_tn_matmul(h1, [dq, dk, dv, du], 1, tm, "grad_w_in"), axis=0)
    g_out = _tn_matmul(mixin, [dmix], 1, tm, "grad_w_out")[0].reshape(w_out.shape)
    g_g, g_u = _tn_matmul(h2, [da, db], w_g.shape[0], tm, "grad_w_gate_up")
    (g_d,) = _tn_matmul(fin, [df], w_d.shape[0], tm, "grad_w_down")

    dmod = jnp.stack([accb8[:, 0], accb8[:, 1], accb5[:, 2], accb5[:, 0], accb5[:, 1], accb4[:, 0]], axis=1)
    dgain = jnp.stack([accg8[0], accg5[1], accg5[0], accg4[0]], axis=0)
    return loss_blk, gx, [g_in, g_out, g_g, g_u, g_d], dmod, dgain, dps[0:1], dwp


def kernel(x, c, w_cond, b_cond, g_mix_pre, g_mix_post, w_in, w_pool, pool_scale, w_out, g_ffn_pre, g_ffn_post, w_gate, w_up, w_down, loss_target, m_w_cond, m_b_cond, m_g_mix_pre, m_g_mix_post, m_w_in, m_w_pool, m_pool_scale, m_w_out, m_g_ffn_pre, m_g_ffn_post, m_w_gate, m_w_up, m_w_down, v_w_cond, v_b_cond, v_g_mix_pre, v_g_mix_post, v_w_in, v_w_pool, v_pool_scale, v_w_out, v_g_ffn_pre, v_g_ffn_post, v_w_gate, v_w_up, v_w_down):
    xi, yi, ci = _position()
    chip = 2 * xi + yi
    dev = 4 * xi + 2 * yi + ci
    nb, seq, d = x.shape
    t_all = nb * seq
    xt = x.reshape(t_all, d)
    tgt = loss_target.reshape(t_all, d)
    ncol = w_cond.shape[2]
    pw = pool_scale.shape[1]

    c_pad = jnp.concatenate([c, jnp.zeros((8 - nb, d), F32)], axis=0)
    c_all = _all_gather(c_pad, "gather_c").reshape(N_DEV, 8, d)[:, :nb].reshape(N_DEV * nb, d)
    b_q = lax.dynamic_slice(b_cond, (0, chip * ncol), (1, ncol))
    sc_all, mod_q = _cond_fwd(c_all, w_cond[0], b_q, 512)
    mod_parts = _all_gather(mod_q, "gather_mod").reshape(N_DEV, N_DEV * nb, ncol)
    mod_rows = lax.dynamic_slice(mod_parts, (0, dev * nb, 0), (N_DEV, nb, ncol))[0::2]
    mod = jnp.transpose(mod_rows, (1, 0, 2)).reshape(nb, N_MOD, d)
    mod = jnp.concatenate([mod, jnp.zeros((nb, MOD_ROWS - N_MOD, d), F32)], axis=1)

    place = jnp.stack([chip, ci]).astype(jnp.int32)
    placed = _place_quarters(place, [w[0] for w in (w_in, w_out, w_gate, w_up, w_down)])
    (w_in_all,) = _gather_weights(placed[:1])
    send_sems, recv_sems, in_flight, token = _gather_start(placed[1:])
    mod = mod + token[0:1, 0:1]

    def later_weights(after):
        return _gather_forward(_gather_wait(send_sems, recv_sems, in_flight, after))

    gains = (g_mix_pre, g_mix_post, g_ffn_pre, g_ffn_post)
    loss_blk, gx, grads, dmod, dgain, dps, dwp = _local_step(
        xt, tgt, mod, gains, w_pool[0], pool_scale, w_in_all, later_weights, seq)
    loss = lax.psum(loss_blk[0, 0], ("x", "y", "c"))

    theirs = _sibling_exchange(grads)
    sums = _chip_sums(place[1:], grads, theirs)
    parts = _chip_exchange(sums)
    halves = _total_sums(place, sums, parts)
    g_big = [g.reshape(2 * g.shape[1], g.shape[2]) for g in _sibling_share(halves)]

    wp_rows = dwp.size // d
    pad_rows = 24 - (2 * N_MOD + 4 + 1)
    payload = jnp.concatenate([
        dmod.reshape(nb * N_MOD, d), dgain,
        jnp.concatenate([dps, jnp.zeros((1, d - pw), F32)], axis=1),
        jnp.zeros((pad_rows, d), F32), dwp.reshape(wp_rows, d)], axis=0)
    prow = payload.shape[0]
    gathered = _all_gather(payload, "gather_small")
    summed = _group_sum(gathered, prow, "small_device_sum")
    dmod_all = gathered.reshape(N_DEV, prow, d)[:, :nb * N_MOD].reshape(N_DEV * nb, N_MOD * d)
    g_b_cond = _group_sum(dmod_all, 1, "grad_b_cond")
    dmod_q = lax.dynamic_slice(dmod_all, (0, chip * ncol), (N_DEV * nb, ncol))
    g_w_cond = _cond_bwd(sc_all, dmod_q, 512)
    first_gain = 2 * N_MOD
    g_gains = [summed[first_gain + r:first_gain + r + 1] for r in range(4)]
    g_pool_scale = summed[first_gain + 4:first_gain + 5, :pw]
    g_w_pool = summed[24:24 + wp_rows].reshape(w_pool.shape[1] * w_pool.shape[2], w_pool.shape[3])

    flat_pool = lambda t: t.reshape(g_w_pool.shape)
    plan = [
        ("w_cond", w_cond[0], g_w_cond, m_w_cond[0], v_w_cond[0], w_cond.shape),
        ("b_cond", b_cond, g_b_cond, m_b_cond, v_b_cond, b_cond.shape),
        ("g_mix_pre", g_mix_pre, g_gains[0], m_g_mix_pre, v_g_mix_pre, g_mix_pre.shape),
        ("g_mix_post", g_mix_post, g_gains[1], m_g_mix_post, v_g_mix_post, g_mix_post.shape),
        ("w_in", w_in[0], g_big[0], m_w_in[0], v_w_in[0], w_in.shape),
        ("w_pool", flat_pool(w_pool), g_w_pool, flat_pool(m_w_pool), flat_pool(v_w_pool), w_pool.shape),
        ("pool_scale", pool_scale, g_pool_scale, m_pool_scale, v_pool_scale, pool_scale.shape),
        ("w_out", w_out[0], g_big[1], m_w_out[0], v_w_out[0], w_out.shape),
        ("g_ffn_pre", g_ffn_pre, g_gains[2], m_g_ffn_pre, v_g_ffn_pre, g_ffn_pre.shape),
        ("g_ffn_post", g_ffn_post, g_gains[3], m_g_ffn_post, v_g_ffn_post, g_ffn_post.shape),
        ("w_gate", w_gate[0], g_big[2], m_w_gate[0], v_w_gate[0], w_gate.shape),
        ("w_up", w_up[0], g_big[3], m_w_up[0], v_w_up[0], w_up.shape),
        ("w_down", w_down[0], g_big[4], m_w_down[0], v_w_down[0], w_down.shape),
    ]
    out_g, out_d, out_m, out_v = [], [], [], []
    for name, w2, g2, m2, v2, shape in plan:
        delta, new_m, new_v = _adamw(w2, g2, m2, v2, "adamw_" + name)
        out_g.append(g2.reshape(shape))
        out_d.append(delta.reshape(shape))
        out_m.append(new_m.reshape(shape))
        out_v.append(new_v.reshape(shape))
    return (loss, gx.reshape(x.shape), *out_g, *out_d, *out_m, *out_v)
```

```python
import functools

import jax
import jax.numpy as jnp
from jax import lax
from jax.experimental import pallas as pl
from jax.experimental.pallas import tpu as pltpu

F32 = jnp.float32
BF16 = jnp.bfloat16
MESH = pl.DeviceIdType.MESH

EPS = 1e-6
HEAD_DIM = 64
HEADS_PER_BLOCK = 2
LANES = 128
NEG_QK_SCALE = -0.125
POOL_WINDOWS = (2, 4, 8, 16)
POOL_GROUP = 128
HALO = 16
N_MOD = 6
MOD_ROWS = 8
N_CHIPS = 4
N_DEV = 8
VMEM_LIMIT = 56 * 1024 * 1024

ADAM_LR = 0.001
ADAM_B1 = 0.9
ADAM_B2 = 0.999
ADAM_EPS = 1e-08
ADAM_WD = 0.01
ADAM_STEP = 10

TOKEN_TILE = 512
ATTN_TILE = 512
ATTN_KEY_TILE = 256
ATTN_ROW_CHUNK = 32


def _dot(a, b):
    return jnp.dot(a, b, preferred_element_type=F32)


def _dot_nt(a, b):
    return lax.dot_general(a, b, (((1,), (1,)), ((), ())), preferred_element_type=F32)


def _dot_tn(a, b):
    return lax.dot_general(a, b, (((0,), (0,)), ((), ())), preferred_element_type=F32)


def _split(v):
    hi = v.astype(BF16)
    lo = (v - hi.astype(F32)).astype(BF16)
    return hi, lo


def _rms(v):
    return lax.rsqrt(jnp.mean(v * v, axis=-1, keepdims=True) + EPS)


def _norm_bwd(dn, n, r):
    return r * (dn - n * jnp.mean(dn * n, axis=-1, keepdims=True))


def _sigmoid(v):
    return 1.0 / (1.0 + jnp.exp(-v))


def _colsum(v):
    return jnp.sum(v, axis=0, keepdims=True)


def _params(sem=None):
    return pltpu.CompilerParams(dimension_semantics=sem, vmem_limit_bytes=VMEM_LIMIT)


def _position():
    return lax.axis_index("x"), lax.axis_index("y"), lax.axis_index("c")


def _prenorm_proj(x, mod, g_pre, w_in, seq, tm):
    t_all, d = x.shape
    nt = seq // tm
    p = w_in.shape[2]

    def body(x_ref, mod_ref, g_ref, w_ref, h_ref, q_ref, k_ref, v_ref, u_ref, kt_ref, vt_ref):
        xf = x_ref[...]
        n = xf * _rms(xf)
        h = (n * g_ref[...]) * (1.0 + mod_ref[0, 1:2, :]) + mod_ref[0, 0:1, :]
        hb = h.astype(BF16)
        h_ref[...] = hb
        q_ref[...] = (_dot(hb, w_ref[0]) * NEG_QK_SCALE).astype(BF16)
        kf = _dot(hb, w_ref[1])
        vf = _dot(hb, w_ref[2])
        k_ref[...] = kf.astype(BF16)
        v_ref[...] = vf.astype(BF16)
        kt_ref[...] = kf.T.astype(BF16)
        vt_ref[...] = vf.T.astype(BF16)
        u_ref[...] = _dot(hb, w_ref[3])

    tok = lambda i: (i, 0)
    tok_t = lambda i: (0, i)
    return pl.pallas_call(
        body, name="prenorm_proj", grid=(t_all // tm,),
        in_specs=[pl.BlockSpec((tm, d), tok),
                  pl.BlockSpec((1, MOD_ROWS, d), lambda i: (i // nt, 0, 0)),
                  pl.BlockSpec((1, d), lambda i: (0, 0)),
                  pl.BlockSpec((N_CHIPS, d, p), lambda i: (0, 0, 0))],
        out_specs=[pl.BlockSpec((tm, d), tok)] + [pl.BlockSpec((tm, p), tok)] * 4 + [pl.BlockSpec((p, tm), tok_t)] * 2,
        out_shape=[jax.ShapeDtypeStruct((t_all, d), BF16)] + [jax.ShapeDtypeStruct((t_all, p), BF16)] * 3
        + [jax.ShapeDtypeStruct((t_all, p), F32)] + [jax.ShapeDtypeStruct((p, t_all), BF16)] * 2,
        compiler_params=_params(("arbitrary",)),
    )(x, mod, g_pre, w_in)


def _tri_matrix(tk, kind):
    j = lax.broadcasted_iota(jnp.int32, (2 * tk, tk), 0) % tk
    s = lax.broadcasted_iota(jnp.int32, (2 * tk, tk), 1)
    return {"after": j > s, "upto": j <= s, "before": j < s}[kind].astype(BF16)


def _row_sums(v):
    return jnp.broadcast_to(jnp.sum(v, axis=-1, keepdims=True), (v.shape[0], LANES))


def _across(v, n):
    return jnp.concatenate([v] * (n // LANES), axis=1)


def _all_masked(c, diag, rc, tk):
    return diag is not None and diag * tk >= (c + 1) * rc - 1


def _some_masked(c, diag, rc, tk):
    return diag is not None and diag * tk + tk - 1 >= c * rc


def _attn_fwd(qn, k, vt, seq, tq, tk):
    t_all, w = qn.shape
    nb, nq, ndiag = t_all // seq, seq // tq, tq // tk
    assert ndiag % 2 == 0, "two key blocks per loop trip"
    rc = ATTN_ROW_CHUNK
    heads = range(HEADS_PER_BLOCK)

    def body(q_ref, k_ref, vt_ref, tri_ref, o_ref, l_ref,
             z_buf, ls_buf, hl_buf, aft_buf, w_buf, tot_buf, acc_t, run_buf):
        i = pl.program_id(2)
        nblk = (i + 1) * ndiag
        lane = lax.broadcasted_iota(jnp.int32, (1, LANES), 1)
        sub = lax.broadcasted_iota(jnp.int32, (LANES, 1), 0)
        row = lax.broadcasted_iota(jnp.int32, (rc, tk), 0)
        col = lax.broadcasted_iota(jnp.int32, (rc, tk), 1)
        first = lane < HEAD_DIM
        q2 = q_ref[...]
        qs = [jnp.where(first, q2, jnp.zeros_like(q2)), jnp.where(first, jnp.zeros_like(q2), q2)]
        acc_t[...] = jnp.zeros_like(acc_t)
        run_buf[...] = jnp.zeros_like(run_buf)
        w_buf[1] = jnp.zeros((HEADS_PER_BLOCK, tq, tk), BF16)

        def causal(c, diag):
            return (col + diag * tk) < (row + c * rc)

        def scores(blk, slot):
            kj = k_ref[pl.ds(pl.multiple_of(blk * tk, tk), tk), :]
            for h in heads:
                z_buf[slot, h] = _dot_nt(qs[h], kj)

        def values(blk, slot):
            vtj = vt_ref[:, pl.ds(pl.multiple_of(blk * tk, tk), tk)]
            zero = jnp.zeros_like(vtj)
            acc_t[...] += (_dot_nt(jnp.where(sub < HEAD_DIM, vtj, zero), w_buf[slot, 0])
                           + _dot_nt(jnp.where(sub < HEAD_DIM, zero, vtj), w_buf[slot, 1]))

        def softplus_stage(h, slot, diag):
            for c in range(tq // rc):
                rows = slice(c * rc, (c + 1) * rc)
                if _all_masked(c, diag, rc, tk):
                    hl_buf[h, rows, :] = jnp.zeros((rc, 2 * tk), BF16)
                    tot_buf[h, rows, :] = jnp.zeros((rc, LANES), F32)
                    continue
                nz = z_buf[slot, h, rows, :]
                l1 = jnp.minimum(nz, 0.0) - jnp.log(1.0 + jnp.exp(-jnp.abs(nz)))
                if _some_masked(c, diag, rc, tk):
                    l1 = jnp.where(causal(c, diag), l1, 0.0)
                hi, lo = _split(l1)
                hl_buf[h, rows, 0:tk] = hi
                hl_buf[h, rows, tk:2 * tk] = lo
                ls_buf[h, rows, :] = l1 - nz
                tot_buf[h, rows, :] = _row_sums(l1)

        def weights_stage(h, slot, diag):
            for c in range(tq // rc):
                rows = slice(c * rc, (c + 1) * rc)
                if _all_masked(c, diag, rc, tk):
                    w_buf[slot, h, rows, :] = jnp.zeros((rc, tk), BF16)
                    continue
                wgt = jnp.exp((ls_buf[h, rows, :] + aft_buf[h, rows, :]) + _across(run_buf[h, rows, :], tk))
                if _some_masked(c, diag, rc, tk):
                    wgt = jnp.where(causal(c, diag), wgt, 0.0)
                w_buf[slot, h, rows, :] = wgt.astype(BF16)
                run_buf[h, rows, :] += tot_buf[h, rows, :]

        def position(blk, slot, diag):
            scores(jnp.maximum(blk - 1, 0), 1 - slot)
            for h in heads:
                softplus_stage(h, slot, diag)
                aft_buf[h] = _dot(hl_buf[h], tri_ref[...])
            values(jnp.minimum(blk + 1, nblk - 1), 1 - slot)
            for h in heads:
                weights_stage(h, slot, diag)

        scores(nblk - 1, 0)
        for p in range(ndiag):
            position(nblk - 1 - p, p % 2, ndiag - 1 - p)

        def trip(jj, carry):
            for u in range(2):
                position(i * ndiag - 1 - 2 * jj - u, u, None)
            return carry

        lax.fori_loop(0, (i * ndiag) // 2, trip, 0)
        values(0, 1)
        o_ref[...] = acc_t[...].T.astype(BF16)
        l_ref[...] = jnp.where(first, run_buf[0], run_buf[1])

    qmap = lambda b, hp, i: (b * nq + i, hp)
    nh = HEADS_PER_BLOCK
    return pl.pallas_call(
        body, name="attn_fwd", grid=(nb, w // LANES, nq),
        in_specs=[pl.BlockSpec((tq, LANES), qmap), pl.BlockSpec((seq, LANES), lambda b, hp, i: (b, hp)),
                  pl.BlockSpec((LANES, seq), lambda b, hp, i: (hp, b)),
                  pl.BlockSpec((2 * tk, tk), lambda b, hp, i: (0, 0))],
        out_specs=[pl.BlockSpec((tq, LANES), qmap), pl.BlockSpec((tq, LANES), qmap)],
        out_shape=[jax.ShapeDtypeStruct((t_all, w), BF16), jax.ShapeDtypeStruct((t_all, w), F32)],
        scratch_shapes=[pltpu.VMEM((2, nh, tq, tk), F32), pltpu.VMEM((nh, tq, tk), F32),
                        pltpu.VMEM((nh, tq, 2 * tk), BF16), pltpu.VMEM((nh, tq, tk), F32),
                        pltpu.VMEM((2, nh, tq, tk), BF16), pltpu.VMEM((nh, tq, LANES), F32),
                        pltpu.VMEM((LANES, tq), F32), pltpu.VMEM((nh, tq, LANES), F32)],
        compiler_params=_params(("arbitrary", "arbitrary", "arbitrary")),
    )(qn, k, vt, _tri_matrix(tk, "after"))


def _window_sums(ext, rows, offset, forward):
    r = lax.broadcasted_iota(jnp.int32, (rows, rows + HALO), 0)
    e = lax.broadcasted_iota(jnp.int32, (rows, rows + HALO), 1)
    hi, lo = _split(ext)
    out = []
    for g, win in enumerate(POOL_WINDOWS):
        if forward:
            band = (e >= r) & (e < r + win)
        else:
            band = (e <= r + offset) & (e > r + offset - win)
        bm = band.astype(BF16)
        cols = slice(g * POOL_GROUP, (g + 1) * POOL_GROUP)
        out.append(_dot(bm, hi[:, cols]) + _dot(bm, lo[:, cols]))
    return out


def _window_counts(pos):
    return [jnp.minimum(pos + 1, win).astype(F32) for win in POOL_WINDOWS]


def _mixer_post(u, o, x, mod, g_post, g_fpre, w_pool, pool_scale, w_out, seq, tm):
    t_all, d = x.shape
    nt = seq // tm
    p = u.shape[1]

    def body(u_ref, halo_ref, o_ref, x_ref, mod_ref, gp_ref, gf_ref, wp_ref, ps_ref, wo_ref,
             pooled_ref, mixin_ref, mix_ref, x1_ref, h2_ref):
        it = pl.program_id(0) % nt
        uf = u_ref[...]
        halo = jnp.where(it == 0, 0.0, halo_ref[...])
        ext = jnp.concatenate([halo, uf], axis=0)
        pos = it * tm + lax.broadcasted_iota(jnp.int32, (tm, 1), 0)
        sums = _window_sums(ext, tm, HALO, False)
        cnts = _window_counts(pos)
        pools = []
        for g in range(len(POOL_WINDOWS)):
            cols = slice(g * POOL_GROUP, (g + 1) * POOL_GROUP)
            pooled = (sums[g] / cnts[g] - uf[:, cols]).astype(BF16)
            pooled_ref[:, cols] = pooled
            yg = _dot(pooled, wp_ref[g].astype(BF16))
            pools.append((yg * ps_ref[:, cols]).astype(BF16))
        mixin = jnp.concatenate([o_ref[...]] + pools, axis=1)
        mixin_ref[...] = mixin
        mix = _dot(mixin, wo_ref[...])
        mix_ref[...] = mix
        n2 = mix * _rms(mix)
        x1 = x_ref[...] + mod_ref[0, 2:3, :] * (n2 * gp_ref[...])
        x1_ref[...] = x1
        n3 = x1 * _rms(x1)
        h2 = (n3 * gf_ref[...]) * (1.0 + mod_ref[0, 4:5, :]) + mod_ref[0, 3:4, :]
        h2_ref[...] = h2.astype(BF16)

    tok = lambda i: (i, 0)
    const2 = lambda i: (0, 0)
    hb = tm // HALO
    return pl.pallas_call(
        body, name="mixer_post", grid=(t_all // tm,),
        in_specs=[pl.BlockSpec((tm, p), tok),
                  pl.BlockSpec((HALO, p), lambda i: (jnp.maximum(i * hb - 1, 0), 0)),
                  pl.BlockSpec((tm, p), tok),
                  pl.BlockSpec((tm, d), tok),
                  pl.BlockSpec((1, MOD_ROWS, d), lambda i: (i // nt, 0, 0)),
                  pl.BlockSpec((1, d), const2), pl.BlockSpec((1, d), const2),
                  pl.BlockSpec(w_pool.shape, lambda i: (0, 0, 0)),
                  pl.BlockSpec((1, p), const2),
                  pl.BlockSpec((d, d), const2)],
        out_specs=[pl.BlockSpec((tm, p), tok), pl.BlockSpec((tm, d), tok), pl.BlockSpec((tm, d), tok),
                   pl.BlockSpec((tm, d), tok), pl.BlockSpec((tm, d), tok)],
        out_shape=[jax.ShapeDtypeStruct((t_all, p), BF16), jax.ShapeDtypeStruct((t_all, d), BF16),
                   jax.ShapeDtypeStruct((t_all, d), F32), jax.ShapeDtypeStruct((t_all, d), F32),
                   jax.ShapeDtypeStruct((t_all, d), BF16)],
        compiler_params=_params(("arbitrary",)),
    )(u, u, o, x, mod, g_post, g_fpre, w_pool, pool_scale, w_out)


def _ffn_fwd(h2, w_g, w_u, w_d, x1, tgt, mod, g_post, seq, tm):
    t_all, d = x1.shape
    nt = seq // tm
    nk, _, ff = w_g.shape

    def body(h_ref, wg_ref, wu_ref, wd_ref, x1_ref, t_ref, mod_ref, g_ref,
             a_ref, b_ref, fin_ref, dy_ref, df_ref, loss_ref, accb_ref, accg_ref, facc):
        i, k = pl.program_id(0), pl.program_id(1)
        hb = h_ref[...]
        a = _dot(hb, wg_ref[0])
        b = _dot(hb, wu_ref[0])
        a_ref[0] = a.astype(BF16)
        b_ref[0] = b.astype(BF16)
        fin = ((a * _sigmoid(a)) * b).astype(BF16)
        fin_ref[0] = fin
        part = _dot(fin, wd_ref[0])

        @pl.when(k == 0)
        def _():
            facc[...] = part

        @pl.when(k > 0)
        def _():
            facc[...] += part

        @pl.when(k == nk - 1)
        def _():
            f = facc[...]
            r4 = _rms(f)
            n4 = f * r4
            gate = mod_ref[0, 5:6, :]
            g = g_ref[...]
            err = (x1_ref[...] + gate * (n4 * g)) - t_ref[...]
            dy = err * (1.0 / d)
            dy_ref[...] = dy

            @pl.when(i == 0)
            def _():
                loss_ref[...] = jnp.zeros_like(loss_ref)
                accg_ref[...] = jnp.zeros_like(accg_ref)

            @pl.when(i % nt == 0)
            def _():
                accb_ref[...] = jnp.zeros_like(accb_ref)

            loss_ref[...] += (0.5 / d) * jnp.sum(err * err)
            accb_ref[0, 0:1, :] += _colsum(dy * (n4 * g))
            accg_ref[0:1, :] += _colsum((dy * gate) * n4)
            dn4 = (dy * gate) * g
            df_ref[...] = _norm_bwd(dn4, n4, r4).astype(BF16)

    tok = lambda i, k: (i, 0)
    ktok = lambda i, k: (k, i, 0)
    kw = lambda i, k: (k, 0, 0)
    const2 = lambda i, k: (0, 0)
    return pl.pallas_call(
        body, name="ffn_fwd", grid=(t_all // tm, nk),
        in_specs=[pl.BlockSpec((tm, d), tok),
                  pl.BlockSpec((1, d, ff), kw), pl.BlockSpec((1, d, ff), kw), pl.BlockSpec((1, ff, d), kw),
                  pl.BlockSpec((tm, d), tok), pl.BlockSpec((tm, d), tok),
                  pl.BlockSpec((1, MOD_ROWS, d), lambda i, k: (i // nt, 0, 0)),
                  pl.BlockSpec((1, d), const2)],
        out_specs=[pl.BlockSpec((1, tm, ff), ktok)] * 3
        + [pl.BlockSpec((tm, d), tok), pl.BlockSpec((tm, d), tok),
           pl.BlockSpec((8, LANES), const2),
           pl.BlockSpec((1, 8, d), lambda i, k: (i // nt, 0, 0)),
           pl.BlockSpec((8, d), const2)],
        out_shape=[jax.ShapeDtypeStruct((nk, t_all, ff), BF16)] * 3
        + [jax.ShapeDtypeStruct((t_all, d), F32), jax.ShapeDtypeStruct((t_all, d), BF16),
           jax.ShapeDtypeStruct((8, LANES), F32),
           jax.ShapeDtypeStruct((t_all // seq, 8, d), F32),
           jax.ShapeDtypeStruct((8, d), F32)],
        scratch_shapes=[pltpu.VMEM((tm, d), F32)],
        compiler_params=_params(("arbitrary", "arbitrary")),
    )(h2, w_g, w_u, w_d, x1, tgt, mod, g_post)


def _ffn_bwd(df, a, b, w_d, w_g, w_u, x1, dy, mix, mod, g_fpre, g_mpost, seq, tm):
    t_all, d = x1.shape
    nt = seq // tm
    nk, _, ff = w_g.shape

    def body(df_ref, a_ref, b_ref, wd_ref, wg_ref, wu_ref, x1_ref, dy_ref, mix_ref, mod_ref, gf_ref, gm_ref,
             da_ref, db_ref, dx1_ref, dmix_ref, accb_ref, accg_ref, hacc):
        i, k = pl.program_id(0), pl.program_id(1)
        dfin = _dot_nt(df_ref[...], wd_ref[0])
        af = a_ref[0].astype(F32)
        bf = b_ref[0].astype(F32)
        sig = _sigmoid(af)
        da = ((dfin * bf) * (sig * (1.0 + af * (1.0 - sig)))).astype(BF16)
        db = (dfin * (af * sig)).astype(BF16)
        da_ref[0] = da
        db_ref[0] = db
        part = _dot_nt(da, wg_ref[0]) + _dot_nt(db, wu_ref[0])

        @pl.when(k == 0)
        def _():
            hacc[...] = part

        @pl.when(k > 0)
        def _():
            hacc[...] += part

        @pl.when(k == nk - 1)
        def _():
            @pl.when(i == 0)
            def _():
                accg_ref[...] = jnp.zeros_like(accg_ref)

            @pl.when(i % nt == 0)
            def _():
                accb_ref[...] = jnp.zeros_like(accb_ref)

            dh2 = hacc[...]
            x1 = x1_ref[...]
            r3 = _rms(x1)
            n3 = x1 * r3
            g3 = gf_ref[...]
            scale1 = 1.0 + mod_ref[0, 4:5, :]
            accb_ref[0, 0:1, :] += _colsum(dh2)
            accb_ref[0, 1:2, :] += _colsum(dh2 * (n3 * g3))
            accg_ref[0:1, :] += _colsum((dh2 * scale1) * n3)
            dx1 = dy_ref[...] + _norm_bwd((dh2 * scale1) * g3, n3, r3)
            dx1_ref[...] = dx1
            mix = mix_ref[...]
            r2 = _rms(mix)
            n2 = mix * r2
            g2 = gm_ref[...]
            gate = mod_ref[0, 2:3, :]
            accb_ref[0, 2:3, :] += _colsum(dx1 * (n2 * g2))
            accg_ref[1:2, :] += _colsum((dx1 * gate) * n2)
            dmix_ref[...] = _norm_bwd((dx1 * gate) * g2, n2, r2).astype(BF16)

    tok = lambda i, k: (i, 0)
    ktok = lambda i, k: (k, i, 0)
    kw = lambda i, k: (k, 0, 0)
    const2 = lambda i, k: (0, 0)
    return pl.pallas_call(
        body, name="ffn_bwd", grid=(t_all // tm, nk),
        in_specs=[pl.BlockSpec((tm, d), tok),
                  pl.BlockSpec((1, tm, ff), ktok), pl.BlockSpec((1, tm, ff), ktok),
                  pl.BlockSpec((1, ff, d), kw), pl.BlockSpec((1, d, ff), kw), pl.BlockSpec((1, d, ff), kw),
                  pl.BlockSpec((tm, d), tok), pl.BlockSpec((tm, d), tok), pl.BlockSpec((tm, d), tok),
                  pl.BlockSpec((1, MOD_ROWS, d), lambda i, k: (i // nt, 0, 0)),
                  pl.BlockSpec((1, d), const2), pl.BlockSpec((1, d), const2)],
        out_specs=[pl.BlockSpec((1, tm, ff), ktok)] * 2
        + [pl.BlockSpec((tm, d), tok), pl.BlockSpec((tm, d), tok),
           pl.BlockSpec((1, 8, d), lambda i, k: (i // nt, 0, 0)),
           pl.BlockSpec((8, d), const2)],
        out_shape=[jax.ShapeDtypeStruct((nk, t_all, ff), BF16)] * 2
        + [jax.ShapeDtypeStruct((t_all, d), F32), jax.ShapeDtypeStruct((t_all, d), BF16),
           jax.ShapeDtypeStruct((t_all // seq, 8, d), F32),
           jax.ShapeDtypeStruct((8, d), F32)],
        scratch_shapes=[pltpu.VMEM((tm, d), F32)],
        compiler_params=_params(("arbitrary", "arbitrary")),
    )(df, a, b, w_d, w_g, w_u, x1, dy, mix, mod, g_fpre, g_mpost)


def _mixer_bwd(dmix, w_out, pooled, w_pool, pool_scale, seq, tm):
    t_all, d = dmix.shape
    p = pooled.shape[1]
    ng = len(POOL_WINDOWS)

    def body(dm_ref, wo_ref, pooled_ref, wp_ref, ps_ref, do_ref, dpd_ref, dps_ref, dwp_ref):
        i = pl.program_id(0)

        @pl.when(i == 0)
        def _():
            dps_ref[...] = jnp.zeros_like(dps_ref)
            dwp_ref[...] = jnp.zeros_like(dwp_ref)

        dmixin = _dot_nt(dm_ref[...], wo_ref[...])
        do_ref[...] = dmixin[:, :p].astype(BF16)
        for g in range(ng):
            cols = slice(g * POOL_GROUP, (g + 1) * POOL_GROUP)
            dpool = dmixin[:, p + g * POOL_GROUP:p + (g + 1) * POOL_GROUP]
            pooled = pooled_ref[:, cols]
            wpg = wp_ref[g].astype(BF16)
            yg = _dot(pooled, wpg)
            dps_ref[0:1, cols] += _colsum(dpool * yg)
            dyg = (dpool * ps_ref[:, cols]).astype(BF16)
            dwp_ref[g] += _dot_tn(pooled, dyg)
            dpd_ref[:, cols] = _dot_nt(dyg, wpg)

    tok = lambda i: (i, 0)
    const2 = lambda i: (0, 0)
    const3 = lambda i: (0, 0, 0)
    return pl.pallas_call(
        body, name="mixer_bwd", grid=(t_all // tm,),
        in_specs=[pl.BlockSpec((tm, d), tok), pl.BlockSpec((d, d), const2), pl.BlockSpec((tm, p), tok),
                  pl.BlockSpec(w_pool.shape, const3), pl.BlockSpec((1, p), const2)],
        out_specs=[pl.BlockSpec((tm, p), tok), pl.BlockSpec((tm, p), tok),
                   pl.BlockSpec((8, p), const2), pl.BlockSpec(w_pool.shape, const3)],
        out_shape=[jax.ShapeDtypeStruct((t_all, p), BF16), jax.ShapeDtypeStruct((t_all, p), F32),
                   jax.ShapeDtypeStruct((8, p), F32), jax.ShapeDtypeStruct(w_pool.shape, F32)],
        compiler_params=_params(("arbitrary",)),
    )(dmix, w_out, pooled, w_pool, pool_scale)


def _attn_bwd(qn, k, kt, v, do, ltot, seq, tq, tk):
    t_all, w = qn.shape
    nb, nq, ndiag, nkb = t_all // seq, seq // tq, tq // tk, seq // tk
    assert ndiag % 2 == 0, "two key blocks per loop trip"
    rc = ATTN_ROW_CHUNK
    nh = HEADS_PER_BLOCK
    heads = range(nh)

    def body(q_ref, k_ref, kt_ref, v_ref, do_ref, l_ref, up_ref, bf_ref, dq_ref, dk_ref, dv_ref,
             z_buf, dw_buf, ls_buf, hl_buf, upto_buf, g_buf, gb_buf, before_buf, w_buf, dz_buf,
             totl_buf, totg_buf, rem_buf, preg_buf, qnt_buf, dot_buf, dq_t, dk_t, dv_t):
        i = pl.program_id(2)
        nblk = (i + 1) * ndiag

        @pl.when(i == 0)
        def _():
            dk_t[...] = jnp.zeros_like(dk_t)
            dv_t[...] = jnp.zeros_like(dv_t)

        lane = lax.broadcasted_iota(jnp.int32, (1, LANES), 1)
        sub = lax.broadcasted_iota(jnp.int32, (LANES, 1), 0)
        row = lax.broadcasted_iota(jnp.int32, (rc, tk), 0)
        col = lax.broadcasted_iota(jnp.int32, (rc, tk), 1)
        first = lane < HEAD_DIM
        upper = sub < HEAD_DIM
        q2 = q_ref[...]
        do2 = do_ref[...]
        l2 = l_ref[...]
        qs = [jnp.where(first, q2, jnp.zeros_like(q2)), jnp.where(first, jnp.zeros_like(q2), q2)]
        dos = [jnp.where(first, do2, jnp.zeros_like(do2)), jnp.where(first, jnp.zeros_like(do2), do2)]
        for src, dst in ((q2, qnt_buf), (do2, dot_buf)):
            t = src.astype(F32).T
            dst[:, 0:tq] = jnp.where(upper, t, 0.0).astype(BF16)
            dst[:, tq:2 * tq] = jnp.where(upper, 0.0, t).astype(BF16)
        for h in heads:
            rem_buf[h] = jnp.where(first if h == 0 else ~first, l2, pltpu.roll(l2, HEAD_DIM, 1))
        preg_buf[...] = jnp.zeros_like(preg_buf)
        dq_t[...] = jnp.zeros_like(dq_t)
        w_buf[1] = jnp.zeros((nh * tq, tk), BF16)
        dz_buf[1] = jnp.zeros((nh * tq, tk), BF16)

        def causal(c, diag):
            return (col + diag * tk) < (row + c * rc)

        def scores(blk, slot):
            off = pl.multiple_of(blk * tk, tk)
            kj = k_ref[pl.ds(off, tk), :]
            vj = v_ref[pl.ds(off, tk), :]
            for h in heads:
                z_buf[slot, h] = _dot_nt(qs[h], kj)
                dw_buf[slot, h] = _dot_nt(dos[h], vj)

        def gradients(blk, slot):
            off = pl.multiple_of(blk * tk, tk)
            ktj = kt_ref[:, pl.ds(off, tk)]
            zero = jnp.zeros_like(ktj)
            dq_t[...] += (_dot_nt(jnp.where(upper, ktj, zero), dz_buf[slot, 0:tq, :])
                          + _dot_nt(jnp.where(upper, zero, ktj), dz_buf[slot, tq:2 * tq, :]))
            dk_t[blk] += _dot(qnt_buf[...], dz_buf[slot])
            dv_t[blk] += _dot(dot_buf[...], w_buf[slot])

        def softplus_stage(h, slot, diag):
            for c in range(tq // rc):
                rows = slice(c * rc, (c + 1) * rc)
                if _all_masked(c, diag, rc, tk):
                    hl_buf[h, rows, :] = jnp.zeros((rc, 2 * tk), BF16)
                    continue
                nz = z_buf[slot, h, rows, :]
                l1 = jnp.minimum(nz, 0.0) - jnp.log(1.0 + jnp.exp(-jnp.abs(nz)))
                if _some_masked(c, diag, rc, tk):
                    l1 = jnp.where(causal(c, diag), l1, 0.0)
                hi, lo = _split(l1)
                hl_buf[h, rows, 0:tk] = hi
                hl_buf[h, rows, tk:2 * tk] = lo
                ls_buf[h, rows, :] = l1 - nz
                totl_buf[h, rows, :] = _row_sums(l1)

        def weights_stage(h, slot, diag):
            for c in range(tq // rc):
                rows = slice(c * rc, (c + 1) * rc)
                stacked = slice(h * tq + c * rc, h * tq + (c + 1) * rc)
                if _all_masked(c, diag, rc, tk):
                    w_buf[slot, stacked, :] = jnp.zeros((rc, tk), BF16)
                    gb_buf[h, rows, :] = jnp.zeros((rc, tk), BF16)
                    continue
                wgt = jnp.exp(ls_buf[h, rows, :] + (_across(rem_buf[h, rows, :], tk) - upto_buf[h, rows, :]))
                if _some_masked(c, diag, rc, tk):
                    wgt = jnp.where(causal(c, diag), wgt, 0.0)
                w_buf[slot, stacked, :] = wgt.astype(BF16)
                g = wgt * dw_buf[slot, h, rows, :]
                g_buf[h, rows, :] = g
                gb_buf[h, rows, :] = g.astype(BF16)
                totg_buf[h, rows, :] = _row_sums(g)
                rem_buf[h, rows, :] -= totl_buf[h, rows, :]

        def dscore_stage(h, slot, diag):
            for c in range(tq // rc):
                rows = slice(c * rc, (c + 1) * rc)
                stacked = slice(h * tq + c * rc, h * tq + (c + 1) * rc)
                if _all_masked(c, diag, rc, tk):
                    dz_buf[slot, stacked, :] = jnp.zeros((rc, tk), BF16)
                    continue
                sig = jnp.exp(ls_buf[h, rows, :])
                g = g_buf[h, rows, :]
                dnz = sig * (before_buf[h, rows, :] + _across(preg_buf[h, rows, :], tk)) - g * (1.0 - sig)
                if _some_masked(c, diag, rc, tk):
                    dnz = jnp.where(causal(c, diag), dnz, 0.0)
                dz_buf[slot, stacked, :] = dnz.astype(BF16)
                preg_buf[h, rows, :] += totg_buf[h, rows, :]

        def position(blk, slot, diag, prefetch):
            if prefetch:
                scores(blk + 1, 1 - slot)
            for h in heads:
                softplus_stage(h, slot, diag)
                upto_buf[h] = _dot(hl_buf[h], up_ref[...])
            gradients(jnp.maximum(blk - 1, 0), 1 - slot)
            for h in heads:
                weights_stage(h, slot, diag)
                before_buf[h] = _dot(gb_buf[h], bf_ref[...])
            for h in heads:
                dscore_stage(h, slot, diag)

        scores(0, 0)

        def trip(jj, carry):
            for u in range(2):
                position(2 * jj + u, u, None, True)
            return carry

        lax.fori_loop(0, (i * ndiag) // 2, trip, 0)
        for d in range(ndiag):
            position(i * ndiag + d, d % 2, d, d < ndiag - 1)
        gradients(nblk - 1, 1)
        dq_ref[...] = (dq_t[...].T * NEG_QK_SCALE).astype(BF16)

        @pl.when(i == nq - 1)
        def _():
            for blk in range(nkb):
                dk_ref[blk * tk:(blk + 1) * tk, :] = dk_t[blk].T.astype(BF16)
                dv_ref[blk * tk:(blk + 1) * tk, :] = dv_t[blk].T.astype(BF16)

    qmap = lambda b, hp, i: (b * nq + i, hp)
    kmap = lambda b, hp, i: (b, hp)
    const = lambda b, hp, i: (0, 0)
    return pl.pallas_call(
        body, name="attn_bwd", grid=(nb, w // LANES, nq),
        in_specs=[pl.BlockSpec((tq, LANES), qmap), pl.BlockSpec((seq, LANES), kmap),
                  pl.BlockSpec((LANES, seq), lambda b, hp, i: (hp, b)), pl.BlockSpec((seq, LANES), kmap),
                  pl.BlockSpec((tq, LANES), qmap), pl.BlockSpec((tq, LANES), qmap),
                  pl.BlockSpec((2 * tk, tk), const), pl.BlockSpec((tk, tk), const)],
        out_specs=[pl.BlockSpec((tq, LANES), qmap), pl.BlockSpec((seq, LANES), kmap), pl.BlockSpec((seq, LANES), kmap)],
        out_shape=[jax.ShapeDtypeStruct((t_all, w), BF16)] * 3,
        scratch_shapes=[pltpu.VMEM((2, nh, tq, tk), F32), pltpu.VMEM((2, nh, tq, tk), F32),
                        pltpu.VMEM((nh, tq, tk), F32), pltpu.VMEM((nh, tq, 2 * tk), BF16),
                        pltpu.VMEM((nh, tq, tk), F32), pltpu.VMEM((nh, tq, tk), F32),
                        pltpu.VMEM((nh, tq, tk), BF16), pltpu.VMEM((nh, tq, tk), F32),
                        pltpu.VMEM((2, nh * tq, tk), BF16), pltpu.VMEM((2, nh * tq, tk), BF16),
                        pltpu.VMEM((nh, tq, LANES), F32), pltpu.VMEM((nh, tq, LANES), F32),
                        pltpu.VMEM((nh, tq, LANES), F32), pltpu.VMEM((nh, tq, LANES), F32),
                        pltpu.VMEM((LANES, nh * tq), BF16), pltpu.VMEM((LANES, nh * tq), BF16),
                        pltpu.VMEM((LANES, tq), F32), pltpu.VMEM((nkb, LANES, tk), F32),
                        pltpu.VMEM((nkb, LANES, tk), F32)],
        compiler_params=_params(("arbitrary", "arbitrary", "arbitrary")),
    )(qn, k, kt, v, do, ltot, _tri_matrix(tk, "upto"), _tri_matrix(tk, "before")[:tk])


def _inproj_bwd(dq, dk, dv, dpd, x, dx1, mod, g_pre, w_in, seq, tm):
    t_all, d = x.shape
    nt = seq // tm
    p = dq.shape[1]

    def body(dq_ref, dk_ref, dv_ref, dpd_ref, halo_ref, x_ref, dx1_ref, mod_ref, g_ref, w_ref,
             gx_ref, du_ref, accb_ref, accg_ref):
        i = pl.program_id(0)
        it = i % nt

        @pl.when(i == 0)
        def _():
            accg_ref[...] = jnp.zeros_like(accg_ref)

        @pl.when(it == 0)
        def _():
            accb_ref[...] = jnp.zeros_like(accb_ref)

        dpd = dpd_ref[...]
        pos = it * tm + lax.broadcasted_iota(jnp.int32, (tm, 1), 0)
        cnts = _window_counts(pos)
        halo = jnp.where(it == nt - 1, 0.0, halo_ref[...])
        scaled = []
        halos = []
        for g, win in enumerate(POOL_WINDOWS):
            cols = slice(g * POOL_GROUP, (g + 1) * POOL_GROUP)
            scaled.append(dpd[:, cols] / cnts[g])
            halos.append(halo[:, cols] / float(win))
        ext = jnp.concatenate([jnp.concatenate(scaled, axis=1), jnp.concatenate(halos, axis=1)], axis=0)
        sums = _window_sums(ext, tm, 0, True)
        du = (jnp.concatenate(sums, axis=1) - dpd).astype(BF16)
        du_ref[...] = du
        dh1 = (_dot_nt(dq_ref[...], w_ref[0]) + _dot_nt(dk_ref[...], w_ref[1])
               + _dot_nt(dv_ref[...], w_ref[2]) + _dot_nt(du, w_ref[3]))
        xf = x_ref[...]
        r1 = _rms(xf)
        n1 = xf * r1
        g1 = g_ref[...]
        scale1 = 1.0 + mod_ref[0, 1:2, :]
        accb_ref[0, 0:1, :] += _colsum(dh1)
        accb_ref[0, 1:2, :] += _colsum(dh1 * (n1 * g1))
        accg_ref[0:1, :] += _colsum((dh1 * scale1) * n1)
        gx_ref[...] = dx1_ref[...] + _norm_bwd((dh1 * scale1) * g1, n1, r1)

    tok = lambda i: (i, 0)
    const2 = lambda i: (0, 0)
    hb = tm // HALO
    last = t_all // HALO - 1
    return pl.pallas_call(
        body, name="inproj_bwd", grid=(t_all // tm,),
        in_specs=[pl.BlockSpec((tm, p), tok), pl.BlockSpec((tm, p), tok), pl.BlockSpec((tm, p), tok),
                  pl.BlockSpec((tm, p), tok),
                  pl.BlockSpec((HALO, p), lambda i: (jnp.minimum((i + 1) * hb, last), 0)),
                  pl.BlockSpec((tm, d), tok), pl.BlockSpec((tm, d), tok),
                  pl.BlockSpec((1, MOD_ROWS, d), lambda i: (i // nt, 0, 0)),
                  pl.BlockSpec((1, d), const2),
                  pl.BlockSpec((N_CHIPS, d, p), lambda i: (0, 0, 0))],
        out_specs=[pl.BlockSpec((tm, d), tok), pl.BlockSpec((tm, p), tok),
                   pl.BlockSpec((1, 8, d), lambda i: (i // nt, 0, 0)),
                   pl.BlockSpec((8, d), const2)],
        out_shape=[jax.ShapeDtypeStruct((t_all, d), F32), jax.ShapeDtypeStruct((t_all, p), BF16),
                   jax.ShapeDtypeStruct((t_all // seq, 8, d), F32),
                   jax.ShapeDtypeStruct((8, d), F32)],
        compiler_params=_params(("arbitrary",)),
    )(dq, dk, dv, dpd, dpd, x, dx1, mod, g_pre, w_in)


def _tn_matmul(x, ys, nk, bt, name):
    t_all = x.shape[-2]
    m = x.shape[-1]
    ny = len(ys)

    def spec(arr):
        if arr.ndim == 3:
            return pl.BlockSpec((1, bt, arr.shape[-1]), lambda k, t: (k, t, 0))
        return pl.BlockSpec((bt, arr.shape[-1]), lambda k, t: (t, 0))

    def tile(ref):
        return ref[0] if len(ref.shape) == 3 else ref[...]

    def body(*refs):
        x_ref, y_refs, o_refs = refs[0], refs[1:1 + ny], refs[1 + ny:]
        t = pl.program_id(1)
        xt = tile(x_ref)
        for y_ref, o_ref in zip(y_refs, o_refs):
            part = _dot_tn(xt, tile(y_ref))

            @pl.when(t == 0)
            def _(o_ref=o_ref, part=part):
                o_ref[0] = part

            @pl.when(t > 0)
            def _(o_ref=o_ref, part=part):
                o_ref[0] += part

    return pl.pallas_call(
        body, name=name, grid=(nk, t_all // bt),
        in_specs=[spec(x)] + [spec(y) for y in ys],
        out_specs=[pl.BlockSpec((1, m, y.shape[-1]), lambda k, t: (k, 0, 0)) for y in ys],
        out_shape=[jax.ShapeDtypeStruct((nk, m, y.shape[-1]), F32) for y in ys],
        compiler_params=_params(("arbitrary", "arbitrary")),
    )(x, *ys)


def _cond_fwd(c_all, w_q, b_q, bn):
    nrow, d = c_all.shape
    ncol = w_q.shape[1]

    def body(c_ref, w_ref, b_ref, sc_ref, mod_ref):
        cf = c_ref[...]
        sc = cf * _sigmoid(cf)
        sc_ref[...] = sc
        shi, slo = _split(sc)
        whi, wlo = _split(w_ref[...])
        mod_ref[...] = (_dot(shi, whi) + _dot(shi, wlo) + _dot(slo, whi)) + b_ref[...]

    return pl.pallas_call(
        body, name="cond_fwd", grid=(ncol // bn,),
        in_specs=[pl.BlockSpec((nrow, d), lambda n: (0, 0)), pl.BlockSpec((d, bn), lambda n: (0, n)),
                  pl.BlockSpec((1, bn), lambda n: (0, n))],
        out_specs=[pl.BlockSpec((nrow, d), lambda n: (0, 0)), pl.BlockSpec((nrow, bn), lambda n: (0, n))],
        out_shape=[jax.ShapeDtypeStruct((nrow, d), F32), jax.ShapeDtypeStruct((nrow, ncol), F32)],
        compiler_params=_params(("arbitrary",)),
    )(c_all, w_q, b_q)


def _cond_bwd(sc_all, dmod_q, bn):
    nrow, d = sc_all.shape
    ncol = dmod_q.shape[1]

    def body(sc_ref, dm_ref, gw_ref):
        shi, slo = _split(sc_ref[...])
        dhi, dlo = _split(dm_ref[...])
        gw_ref[...] = _dot_tn(shi, dhi) + _dot_tn(shi, dlo) + _dot_tn(slo, dhi)

    return pl.pallas_call(
        body, name="cond_bwd", grid=(ncol // bn,),
        in_specs=[pl.BlockSpec((nrow, d), lambda n: (0, 0)), pl.BlockSpec((nrow, bn), lambda n: (0, n))],
        out_specs=pl.BlockSpec((d, bn), lambda n: (0, n)),
        out_shape=jax.ShapeDtypeStruct((d, ncol), F32),
        compiler_params=_params(("arbitrary",)),
    )(sc_all, dmod_q)


def _row_block(rows, cols, budget=1 << 18):
    best = None
    for br in range(8, rows + 1, 8):
        if rows % br == 0 and br * cols <= budget:
            best = br
    return best if best is not None else rows


def _adamw(w, g, m, v, name):
    rows, cols = w.shape
    br = _row_block(rows, cols)
    c1 = 1.0 - ADAM_B1 ** ADAM_STEP
    c2 = 1.0 - ADAM_B2 ** ADAM_STEP

    def body(w_ref, g_ref, m_ref, v_ref, d_ref, nm_ref, nv_ref):
        gf = g_ref[...]
        m2 = ADAM_B1 * m_ref[...] + (1.0 - ADAM_B1) * gf
        v2 = ADAM_B2 * v_ref[...] + (1.0 - ADAM_B2) * (gf * gf)
        nm_ref[...] = m2
        nv_ref[...] = v2
        d_ref[...] = -ADAM_LR * ((m2 / c1) / (jnp.sqrt(v2 / c2) + ADAM_EPS) + ADAM_WD * w_ref[...])

    blk = pl.BlockSpec((br, cols), lambda i: (i, 0))
    return pl.pallas_call(
        body, name=name, grid=(rows // br,),
        in_specs=[blk] * 4, out_specs=[blk] * 3,
        out_shape=[jax.ShapeDtypeStruct((rows, cols), F32)] * 3,
        compiler_params=_params(("arbitrary",)),
    )(w, g, m, v)


def _all_gather(x_shard, name):
    m_per, n = x_shard.shape

    def body(x_ref, out_ref, send_sems, recv_sems, local_sem):
        x, y, c = _position()
        me, sibling = (x, y, c), (x, y, 1 - c)
        chips = [(1 - x, y), (x, 1 - y), (1 - x, 1 - y)]

        def rows(px, py, pc):
            return out_ref.at[pl.ds((4 * px + 2 * py + pc) * m_per, m_per), :]

        def copy(k, block, to, src=None):
            return pltpu.make_async_remote_copy(
                src_ref=rows(*block) if src is None else src, dst_ref=rows(*block),
                send_sem=send_sems.at[k], recv_sem=recv_sems.at[k], device_id=to, device_id_type=MESH)

        mine = pltpu.make_async_copy(x_ref, rows(*me), local_sem)
        mine.start()
        first = [copy(0, me, sibling, src=x_ref)]
        first += [copy(1 + j, me, (*chip, c), src=x_ref) for j, chip in enumerate(chips)]
        for cp in first:
            cp.start()
        passed = [copy(4 + j, (*chip, c), sibling) for j, chip in enumerate(chips)]
        for j, chip in enumerate(chips):
            copy(1 + j, (*chip, c), me).wait_recv()
            passed[j].start()
        copy(0, sibling, me).wait_recv()
        for j, chip in enumerate(chips):
            copy(4 + j, (*chip, 1 - c), me).wait_recv()
        for cp in first + passed:
            cp.wait_send()
        mine.wait()

    return pl.pallas_call(
        body, name=name,
        out_shape=jax.ShapeDtypeStruct((N_DEV * m_per, n), x_shard.dtype),
        in_specs=[pl.BlockSpec(memory_space=pltpu.VMEM)],
        out_specs=pl.BlockSpec(memory_space=pltpu.VMEM),
        scratch_shapes=[pltpu.SemaphoreType.DMA((7,)), pltpu.SemaphoreType.DMA((7,)), pltpu.SemaphoreType.DMA],
        compiler_params=pltpu.CompilerParams(vmem_limit_bytes=VMEM_LIMIT),
    )(x_shard)


_ANY = pl.BlockSpec(memory_space=pl.ANY)


def _place_quarters(place, quarters):
    steps = 2

    def body(place_ref, *refs):
        n = len(refs) // 2
        for w_ref, o_ref in zip(refs[:n], refs[n:]):
            o_ref[0] = w_ref[...].astype(BF16)

    return pl.pallas_call(
        body, name="place_quarters",
        grid_spec=pltpu.PrefetchScalarGridSpec(
            num_scalar_prefetch=1, grid=(steps,),
            in_specs=[pl.BlockSpec((q.shape[0] // steps, q.shape[1]), lambda r, place_ref: (r, 0)) for q in quarters],
            out_specs=[pl.BlockSpec((1, q.shape[0] // steps, q.shape[1]), lambda r, place_ref: (place_ref[0], r, 0))
                       for q in quarters]),
        out_shape=[jax.ShapeDtypeStruct((N_CHIPS,) + q.shape, BF16) for q in quarters],
        compiler_params=_params(("arbitrary",)),
    )(place, *quarters)


def _gather_weights(placed):
    n = len(placed)
    shapes = [b.shape[1:] for b in placed]

    def body(*refs):
        g_refs = refs[n:2 * n]
        send_sems, recv_sems = refs[2 * n:]
        x, y, c = _position()
        sibling = (x, y, 1 - c)
        chips = [(1 - x, y), (x, 1 - y), (1 - x, 1 - y)]
        mine = 2 * x + y

        def half(a, which):
            hr = shapes[a][0] // 2
            return pl.ds(which * hr, hr)

        def over_ici(a, p, slot):
            ref = g_refs[a].at[slot, half(a, c), :]
            return pltpu.make_async_remote_copy(
                src_ref=ref, dst_ref=ref,
                send_sem=send_sems.at[6 * a + p], recv_sem=recv_sems.at[6 * a + p],
                device_id=(*chips[p], c), device_id_type=MESH)

        def over_d2d(a, p, slot, which):
            ref = g_refs[a].at[slot, half(a, which), :]
            return pltpu.make_async_remote_copy(
                src_ref=ref, dst_ref=ref,
                send_sem=send_sems.at[6 * a + 3 + p], recv_sem=recv_sems.at[6 * a + 3 + p],
                device_id=sibling, device_id_type=MESH)

        sends = []
        for a in range(n):
            for p in range(3):
                cp = over_ici(a, p, mine)
                cp.start()
                sends.append(cp)
        for a in range(n):
            for p, (cx, cy) in enumerate(chips):
                slot = 2 * cx + cy
                over_ici(a, p, slot).wait_recv()
                cp = over_d2d(a, p, slot, c)
                cp.start()
                sends.append(cp)
        for a in range(n):
            for p, (cx, cy) in enumerate(chips):
                over_d2d(a, p, 2 * cx + cy, 1 - c).wait_recv()
        for cp in sends:
            cp.wait_send()

    return pl.pallas_call(
        body, name="gather_weights",
        out_shape=[jax.ShapeDtypeStruct(b.shape, BF16) for b in placed],
        in_specs=[_ANY] * n, out_specs=[_ANY] * n,
        input_output_aliases={a: a for a in range(n)},
        scratch_shapes=[pltpu.SemaphoreType.DMA((6 * n,)), pltpu.SemaphoreType.DMA((6 * n,))],
    )(*placed)


_HBM = pl.BlockSpec(memory_space=pltpu.HBM)
_SEM = pl.BlockSpec(memory_space=pltpu.SEMAPHORE)
_EFFECT = pltpu.SideEffectType.DATAFLOW_SIDE_EFFECTING


def _quarter_halves(shapes, a, which):
    hr = shapes[a][0] // 2
    return pl.ds(which * hr, hr)


def _gather_start(placed, after):
    n = len(placed)
    shapes = [b.shape[1:] for b in placed]

    def body(*refs):
        g_refs = refs[:n]
        send_sems, recv_sems = refs[n + 1], refs[n + 2]
        token = refs[2 * n + 3]
        x, y, c = _position()
        chips = [(1 - x, y), (x, 1 - y), (1 - x, 1 - y)]
        mine = 2 * x + y
        for a in range(n):
            ref = g_refs[a].at[mine, _quarter_halves(shapes, a, c), :]
            for p in range(3):
                pltpu.make_async_remote_copy(
                    src_ref=ref, dst_ref=ref, send_sem=send_sems.at[3 * a + p], recv_sem=recv_sems.at[3 * a + p],
                    device_id=(*chips[p], c), device_id_type=MESH).start()
        token[...] = jnp.zeros_like(token)

    out = pl.pallas_call(
        body, name="gather_start",
        out_shape=(pltpu.SemaphoreType.DMA((3 * n,)), pltpu.SemaphoreType.DMA((3 * n,)),
                   *[pltpu.HBM(b.shape, b.dtype) for b in placed], jax.ShapeDtypeStruct((8, LANES), F32)),
        in_specs=[_HBM] * n + [_ANY],
        out_specs=(_SEM, _SEM, *[_HBM] * n, pl.BlockSpec(memory_space=pltpu.VMEM)),
        input_output_aliases={a: 2 + a for a in range(n)},
        compiler_params=pltpu.CompilerParams(has_side_effects=_EFFECT),
    )(*[pltpu.with_memory_space_constraint(b, pltpu.HBM) for b in placed], after)
    return out[0], out[1], list(out[2:2 + n]), out[2 + n]


def _gather_wait(send_sems, recv_sems, thru, after):
    n = len(thru)
    shapes = [b.shape[1:] for b in thru]

    def body(*refs):
        g_refs = refs[:n]
        send_sems, recv_sems = refs[n], refs[n + 1]
        x, y, c = _position()
        chips = [(1 - x, y), (x, 1 - y), (1 - x, 1 - y)]
        mine = 2 * x + y
        for a in range(n):
            rows = _quarter_halves(shapes, a, c)
            for p, (cx, cy) in enumerate(chips):
                copy = pltpu.make_async_remote_copy(
                    src_ref=g_refs[a].at[mine, rows, :], dst_ref=g_refs[a].at[2 * cx + cy, rows, :],
                    send_sem=send_sems.at[3 * a + p], recv_sem=recv_sems.at[3 * a + p],
                    device_id=(cx, cy, c), device_id_type=MESH)
                copy.wait_send()
                copy.wait_recv()

    return pl.pallas_call(
        body, name="gather_wait",
        out_shape=[pltpu.HBM(b.shape, b.dtype) for b in thru],
        in_specs=[_HBM] * n + [_SEM, _SEM, _ANY], out_specs=[_HBM] * n,
        input_output_aliases={a: a for a in range(n)},
        compiler_params=pltpu.CompilerParams(has_side_effects=_EFFECT),
    )(*thru, send_sems, recv_sems, after)


def _gather_forward(bufs):
    n = len(bufs)
    shapes = [b.shape[1:] for b in bufs]

    def body(*refs):
        g_refs = refs[n:2 * n]
        send_sems, recv_sems = refs[2 * n:]
        x, y, c = _position()
        chips = [(1 - x, y), (x, 1 - y), (1 - x, 1 - y)]

        def over_d2d(a, p, which):
            cx, cy = chips[p]
            ref = g_refs[a].at[2 * cx + cy, _quarter_halves(shapes, a, which), :]
            return pltpu.make_async_remote_copy(
                src_ref=ref, dst_ref=ref, send_sem=send_sems.at[3 * a + p], recv_sem=recv_sems.at[3 * a + p],
                device_id=(x, y, 1 - c), device_id_type=MESH)

        sends = [over_d2d(a, p, c) for a in range(n) for p in range(3)]
        for cp in sends:
            cp.start()
        for a in range(n):
            for p in range(3):
                over_d2d(a, p, 1 - c).wait_recv()
        for cp in sends:
            cp.wait_send()

    return pl.pallas_call(
        body, name="gather_forward",
        out_shape=[jax.ShapeDtypeStruct(b.shape, BF16) for b in bufs],
        in_specs=[_ANY] * n, out_specs=[_ANY] * n,
        input_output_aliases={a: a for a in range(n)},
        scratch_shapes=[pltpu.SemaphoreType.DMA((3 * n,)), pltpu.SemaphoreType.DMA((3 * n,))],
    )(*bufs)


def _sibling_exchange(grads, tag):
    n = len(grads)
    shapes = [g.shape for g in grads]

    def body(*refs):
        g_refs, x_refs = refs[:n], refs[n:2 * n]
        send_sems, recv_sems = refs[2 * n:]
        x, y, c = _position()
        copies = []
        for a in range(n):
            hr = shapes[a][1] // 2
            cp = pltpu.make_async_remote_copy(
                src_ref=g_refs[a].at[:, pl.ds((1 - c) * hr, hr), :], dst_ref=x_refs[a],
                send_sem=send_sems.at[a], recv_sem=recv_sems.at[a],
                device_id=(x, y, 1 - c), device_id_type=MESH)
            cp.start()
            copies.append(cp)
        for cp in copies:
            cp.wait()

    return pl.pallas_call(
        body, name="grad_sibling_exchange_" + tag,
        out_shape=[jax.ShapeDtypeStruct((s[0], s[1] // 2, s[2]), F32) for s in shapes],
        in_specs=[_ANY] * n, out_specs=[_ANY] * n,
        scratch_shapes=[pltpu.SemaphoreType.DMA((n,)), pltpu.SemaphoreType.DMA((n,))],
    )(*grads)


def _chip_sums(core, grads, theirs, tag):
    n = len(grads)

    def body(core_ref, *refs):
        g_refs, t_refs, o_refs = refs[:n], refs[n:2 * n], refs[2 * n:]
        for g_ref, t_ref, o_ref in zip(g_refs, t_refs, o_refs):
            o_ref[...] = (g_ref[...] + t_ref[...]).astype(BF16)

    in_specs = [pl.BlockSpec((1, g.shape[1] // 2, g.shape[2]), lambda k, core_ref: (k, core_ref[0], 0)) for g in grads]
    in_specs += [pl.BlockSpec((1,) + t.shape[1:], lambda k, core_ref: (k, 0, 0)) for t in theirs]
    return pl.pallas_call(
        body, name="grad_chip_sums_" + tag,
        grid_spec=pltpu.PrefetchScalarGridSpec(
            num_scalar_prefetch=1, grid=(N_CHIPS,), in_specs=in_specs,
            out_specs=[pl.BlockSpec((1,) + t.shape[1:], lambda k, core_ref: (k, 0, 0)) for t in theirs]),
        out_shape=[jax.ShapeDtypeStruct(t.shape, BF16) for t in theirs],
        compiler_params=_params(("arbitrary",)),
    )(core, *grads, *theirs)


def _chip_exchange(sums):
    n = len(sums)

    def body(*refs):
        s_refs, y_refs = refs[:n], refs[n:2 * n]
        send_sems, recv_sems = refs[2 * n:]
        x, y, c = _position()
        chips = [(1 - x, y), (x, 1 - y), (1 - x, 1 - y)]
        copies = []
        for a in range(n):
            for p, (cx, cy) in enumerate(chips):
                cp = pltpu.make_async_remote_copy(
                    src_ref=s_refs[a].at[2 * cx + cy], dst_ref=y_refs[a].at[p],
                    send_sem=send_sems.at[3 * a + p], recv_sem=recv_sems.at[3 * a + p],
                    device_id=(cx, cy, c), device_id_type=MESH)
                cp.start()
                copies.append(cp)
        for cp in copies:
            cp.wait()

    return pl.pallas_call(
        body, name="grad_chip_exchange",
        out_shape=[jax.ShapeDtypeStruct((3,) + s.shape[1:], BF16) for s in sums],
        in_specs=[_ANY] * n, out_specs=[_ANY] * n,
        scratch_shapes=[pltpu.SemaphoreType.DMA((3 * n,)), pltpu.SemaphoreType.DMA((3 * n,))],
    )(*sums)


def _chip_exchange_start(sums):
    n = len(sums)
    lands = [lax.empty((3,) + s.shape[1:], BF16) for s in sums]

    def body(*refs):
        s_refs, y_refs = refs[:n], refs[n:2 * n]
        send_sems, recv_sems = refs[2 * n], refs[2 * n + 1]
        token = refs[4 * n + 2]
        x, y, c = _position()
        chips = [(1 - x, y), (x, 1 - y), (1 - x, 1 - y)]
        for a in range(n):
            for p, (cx, cy) in enumerate(chips):
                pltpu.make_async_remote_copy(
                    src_ref=s_refs[a].at[2 * cx + cy], dst_ref=y_refs[a].at[p],
                    send_sem=send_sems.at[3 * a + p], recv_sem=recv_sems.at[3 * a + p],
                    device_id=(cx, cy, c), device_id_type=MESH).start()
        token[...] = jnp.zeros_like(token)

    both = list(sums) + lands
    out = pl.pallas_call(
        body, name="grad_chip_exchange_start",
        out_shape=(pltpu.SemaphoreType.DMA((3 * n,)), pltpu.SemaphoreType.DMA((3 * n,)),
                   *[pltpu.HBM(b.shape, b.dtype) for b in both], jax.ShapeDtypeStruct((8, LANES), F32)),
        in_specs=[_HBM] * (2 * n),
        out_specs=(_SEM, _SEM, *[_HBM] * (2 * n), pl.BlockSpec(memory_space=pltpu.VMEM)),
        input_output_aliases={a: 2 + a for a in range(2 * n)},
        compiler_params=pltpu.CompilerParams(has_side_effects=_EFFECT),
    )(*[pltpu.with_memory_space_constraint(b, pltpu.HBM) for b in both])
    return out[0], out[1], list(out[2:2 + n]), list(out[2 + n:2 + 2 * n]), out[2 + 2 * n]


def _chip_exchange_wait(send_sems, recv_sems, sums, lands, after):
    n = len(sums)

    def body(*refs):
        s_refs, y_refs = refs[:n], refs[n:2 * n]
        send_sems, recv_sems = refs[2 * n], refs[2 * n + 1]
        x, y, c = _position()
        chips = [(1 - x, y), (x, 1 - y), (1 - x, 1 - y)]
        for a in range(n):
            for p, (cx, cy) in enumerate(chips):
                copy = pltpu.make_async_remote_copy(
                    src_ref=s_refs[a].at[2 * cx + cy], dst_ref=y_refs[a].at[p],
                    send_sem=send_sems.at[3 * a + p], recv_sem=recv_sems.at[3 * a + p],
                    device_id=(cx, cy, c), device_id_type=MESH)
                copy.wait_send()
                copy.wait_recv()

    both = list(sums) + list(lands)
    out = pl.pallas_call(
        body, name="grad_chip_exchange_wait",
        out_shape=[pltpu.HBM(b.shape, b.dtype) for b in both],
        in_specs=[_HBM] * (2 * n) + [_SEM, _SEM, _ANY], out_specs=[_HBM] * (2 * n),
        input_output_aliases={a: a for a in range(2 * n)},
        compiler_params=pltpu.CompilerParams(has_side_effects=_EFFECT),
    )(*both, send_sems, recv_sems, after)
    return list(out[:n]), list(out[n:])


def _total_sums(place, sums, parts):
    n = len(parts)
    steps = 2

    def body(place_ref, *refs):
        for s_ref, y_ref, o_ref in zip(refs[:n], refs[n:2 * n], refs[2 * n:]):
            o_ref[0] = ((s_ref[0].astype(F32) + y_ref[0].astype(F32)) + y_ref[1].astype(F32)) + y_ref[2].astype(F32)

    def step_rows(pt):
        return pt.shape[1] // steps

    in_specs = [pl.BlockSpec((1, step_rows(s), s.shape[2]), lambda r, place_ref: (place_ref[0], r, 0)) for s in sums]
    in_specs += [pl.BlockSpec((3, step_rows(pt), pt.shape[2]), lambda r, place_ref: (0, r, 0)) for pt in parts]
    return pl.pallas_call(
        body, name="grad_total_sums",
        grid_spec=pltpu.PrefetchScalarGridSpec(
            num_scalar_prefetch=1, grid=(steps,), in_specs=in_specs,
            out_specs=[pl.BlockSpec((1, step_rows(pt), pt.shape[2]), lambda r, place_ref: (place_ref[1], r, 0))
                       for pt in parts]),
        out_shape=[jax.ShapeDtypeStruct((2,) + pt.shape[1:], F32) for pt in parts],
        compiler_params=_params(("arbitrary",)),
    )(place, *sums, *parts)


def _sibling_share(halves):
    n = len(halves)

    def body(*refs):
        f_refs = refs[n:2 * n]
        send_sems, recv_sems = refs[2 * n:]
        x, y, c = _position()
        copies = []
        for a in range(n):
            cp = pltpu.make_async_remote_copy(
                src_ref=f_refs[a].at[c], dst_ref=f_refs[a].at[c], send_sem=send_sems.at[a], recv_sem=recv_sems.at[a],
                device_id=(x, y, 1 - c), device_id_type=MESH)
            cp.start()
            copies.append(cp)
        for a, cp in enumerate(copies):
            cp.wait_send()
            pltpu.make_async_remote_copy(
                src_ref=f_refs[a].at[1 - c], dst_ref=f_refs[a].at[1 - c], send_sem=send_sems.at[a],
                recv_sem=recv_sems.at[a], device_id=(x, y, c), device_id_type=MESH).wait_recv()

    return pl.pallas_call(
        body, name="grad_sibling_share",
        out_shape=[jax.ShapeDtypeStruct(h.shape, F32) for h in halves],
        in_specs=[_ANY] * n, out_specs=[_ANY] * n,
        input_output_aliases={a: a for a in range(n)},
        scratch_shapes=[pltpu.SemaphoreType.DMA((n,)), pltpu.SemaphoreType.DMA((n,))],
    )(*halves)


def _group_sum(stacked, nrow, name):
    total, n = stacked.shape
    groups = total // nrow

    def body(g_ref, o_ref):
        acc = g_ref[0:nrow, :]
        for grp in range(1, groups):
            acc = acc + g_ref[grp * nrow:(grp + 1) * nrow, :]
        o_ref[...] = acc

    return pl.pallas_call(
        body, name=name,
        out_shape=jax.ShapeDtypeStruct((nrow, n), F32),
        compiler_params=pltpu.CompilerParams(vmem_limit_bytes=VMEM_LIMIT),
    )(stacked)


def _local_step(xt, tgt, mod, gains, w_pool, pool_scale, w_in, later_weights, on_ffn_grads, seq):
    g_mpre, g_mpost, g_fpre, g_fpost = gains
    d = xt.shape[1]
    tm, tq = min(TOKEN_TILE, seq), min(ATTN_TILE, seq)

    h1, qn, k, v, u, kt, vt = _prenorm_proj(xt, mod, g_mpre, w_in, seq, tm)
    tk = min(ATTN_KEY_TILE, tq // 2)
    o, ltot = _attn_fwd(qn, k, vt, seq, tq, tk)
    w_out, w_g, w_u, w_d = later_weights(o)
    w_out2 = w_out.reshape(d, d)
    pooled, mixin, mix, x1, h2 =_mixer_post(u, o, xt, mod, g_mpost, g_fpre, w_pool, pool_scale, w_out2, seq, tm)
    a, b, fin, dy, df, loss_blk, accb4, accg4 = _ffn_fwd(h2, w_g, w_u, w_d, x1, tgt, mod, g_fpost, seq, tm)
    da, db, dx1, dmix, accb5, accg5 = _ffn_bwd(df, a, b, w_d, w_g, w_u, x1, dy, mix, mod, g_fpre, g_mpost, seq, tm)
    g_g, g_u = _tn_matmul(h2, [da, db], w_g.shape[0], tm, "grad_w_gate_up")
    (g_d,) = _tn_matmul(fin, [df], w_d.shape[0], tm, "grad_w_down")
    token = on_ffn_grads([g_g, g_u, g_d])
    do, dpd, dps, dwp = _mixer_bwd(dmix, w_out2, pooled, w_pool, pool_scale + token, seq, tm)
    dq, dk, dv = _attn_bwd(qn, k, kt, v, do, ltot, seq, tq, tk)
    gx, du, accb8, accg8 = _inproj_bwd(dq, dk, dv, dpd, xt, dx1, mod, g_mpre, w_in, seq, tm)

    g_in = jnp.concatenate(_tn_matmul(h1, [dq, dk, dv, du], 1, tm, "grad_w_in"), axis=0)
    g_out = _tn_matmul(mixin, [dmix], 1, tm, "grad_w_out")[0].reshape(w_out.shape)

    dmod = jnp.stack([accb8[:, 0], accb8[:, 1], accb5[:, 2], accb5[:, 0], accb5[:, 1], accb4[:, 0]], axis=1)
    dgain = jnp.stack([accg8[0], accg5[1], accg5[0], accg4[0]], axis=0)
    return loss_blk, gx, [g_in, g_out, g_g, g_u, g_d], dmod, dgain, dps[0:1], dwp


def kernel(x, c, w_cond, b_cond, g_mix_pre, g_mix_post, w_in, w_pool, pool_scale, w_out, g_ffn_pre, g_ffn_post, w_gate, w_up, w_down, loss_target, m_w_cond, m_b_cond, m_g_mix_pre, m_g_mix_post, m_w_in, m_w_pool, m_pool_scale, m_w_out, m_g_ffn_pre, m_g_ffn_post, m_w_gate, m_w_up, m_w_down, v_w_cond, v_b_cond, v_g_mix_pre, v_g_mix_post, v_w_in, v_w_pool, v_pool_scale, v_w_out, v_g_ffn_pre, v_g_ffn_post, v_w_gate, v_w_up, v_w_down):
    xi, yi, ci = _position()
    chip = 2 * xi + yi
    dev = 4 * xi + 2 * yi + ci
    nb, seq, d = x.shape
    t_all = nb * seq
    xt = x.reshape(t_all, d)
    tgt = loss_target.reshape(t_all, d)
    ncol = w_cond.shape[2]
    pw = pool_scale.shape[1]

    c_pad = jnp.concatenate([c, jnp.zeros((8 - nb, d), F32)], axis=0)
    c_all = _all_gather(c_pad, "gather_c").reshape(N_DEV, 8, d)[:, :nb].reshape(N_DEV * nb, d)
    b_q = lax.dynamic_slice(b_cond, (0, chip * ncol), (1, ncol))
    sc_all, mod_q = _cond_fwd(c_all, w_cond[0], b_q, 512)
    mod_parts = _all_gather(mod_q, "gather_mod").reshape(N_DEV, N_DEV * nb, ncol)
    mod_rows = lax.dynamic_slice(mod_parts, (0, dev * nb, 0), (N_DEV, nb, ncol))[0::2]
    mod = jnp.transpose(mod_rows, (1, 0, 2)).reshape(nb, N_MOD, d)
    mod = jnp.concatenate([mod, jnp.zeros((nb, MOD_ROWS - N_MOD, d), F32)], axis=1)

    place = jnp.stack([chip, ci]).astype(jnp.int32)
    placed = _place_quarters(place, [w[0] for w in (w_in, w_out, w_gate, w_up, w_down)])
    (w_in_all,) = _gather_weights(placed[:1])
    send_sems, recv_sems, in_flight, token = _gather_start(placed[1:], mod)
    mod = mod + token[0:1, 0:1]

    def later_weights(after):
        return _gather_forward(_gather_wait(send_sems, recv_sems, in_flight, after))

    ffn_split = []

    def on_ffn_grads(ffn_grads):
        theirs = _sibling_exchange(ffn_grads, "ffn")
        ffn_split.extend(_chip_exchange_start(_chip_sums(place[1:], ffn_grads, theirs, "ffn")))
        return ffn_split[4][0:1, 0:1]

    gains = (g_mix_pre, g_mix_post, g_ffn_pre, g_ffn_post)
    loss_blk, gx, grads, dmod, dgain, dps, dwp = _local_step(
        xt, tgt, mod, gains, w_pool[0], pool_scale, w_in_all, later_weights, on_ffn_grads, seq)
    loss = lax.psum(loss_blk[0, 0], ("x", "y", "c"))

    sums_ffn, parts_ffn = _chip_exchange_wait(*ffn_split[:4], gx)
    theirs = _sibling_exchange(grads[:2], "mix")
    sums_mix = _chip_sums(place[1:], grads[:2], theirs, "mix")
    parts_mix = _chip_exchange(sums_mix)
    halves = _total_sums(place, list(sums_mix) + list(sums_ffn), list(parts_mix) + list(parts_ffn))
    g_big = [g.reshape(2 * g.shape[1], g.shape[2]) for g in _sibling_share(halves)]

    wp_rows = dwp.size // d
    pad_rows = 24 - (2 * N_MOD + 4 + 1)
    payload = jnp.concatenate([
        dmod.reshape(nb * N_MOD, d), dgain,
        jnp.concatenate([dps, jnp.zeros((1, d - pw), F32)], axis=1),
        jnp.zeros((pad_rows, d), F32), dwp.reshape(wp_rows, d)], axis=0)
    prow = payload.shape[0]
    gathered = _all_gather(payload, "gather_small")
    summed = _group_sum(gathered, prow, "small_device_sum")
    dmod_all = gathered.reshape(N_DEV, prow, d)[:, :nb * N_MOD].reshape(N_DEV * nb, N_MOD * d)
    g_b_cond = _group_sum(dmod_all, 1, "grad_b_cond")
    dmod_q = lax.dynamic_slice(dmod_all, (0, chip * ncol), (N_DEV * nb, ncol))
    g_w_cond = _cond_bwd(sc_all, dmod_q, 512)
    first_gain = 2 * N_MOD
    g_gains = [summed[first_gain + r:first_gain + r + 1] for r in range(4)]
    g_pool_scale = summed[first_gain + 4:first_gain + 5, :pw]
    g_w_pool = summed[24:24 + wp_rows].reshape(w_pool.shape[1] * w_pool.shape[2], w_pool.shape[3])

    flat_pool = lambda t: t.reshape(g_w_pool.shape)
    plan = [
        ("w_cond", w_cond[0], g_w_cond, m_w_cond[0], v_w_cond[0], w_cond.shape),
        ("b_cond", b_cond, g_b_cond, m_b_cond, v_b_cond, b_cond.shape),
        ("g_mix_pre", g_mix_pre, g_gains[0], m_g_mix_pre, v_g_mix_pre, g_mix_pre.shape),
        ("g_mix_post", g_mix_post, g_gains[1], m_g_mix_post, v_g_mix_post, g_mix_post.shape),
        ("w_in", w_in[0], g_big[0], m_w_in[0], v_w_in[0], w_in.shape),
        ("w_pool", flat_pool(w_pool), g_w_pool, flat_pool(m_w_pool), flat_pool(v_w_pool), w_pool.shape),
        ("pool_scale", pool_scale, g_pool_scale, m_pool_scale, v_pool_scale, pool_scale.shape),
        ("w_out", w_out[0], g_big[1], m_w_out[0], v_w_out[0], w_out.shape),
        ("g_ffn_pre", g_ffn_pre, g_gains[2], m_g_ffn_pre, v_g_ffn_pre, g_ffn_pre.shape),
        ("g_ffn_post", g_ffn_post, g_gains[3], m_g_ffn_post, v_g_ffn_post, g_ffn_post.shape),
        ("w_gate", w_gate[0], g_big[2], m_w_gate[0], v_w_gate[0], w_gate.shape),
        ("w_up", w_up[0], g_big[3], m_w_up[0], v_w_up[0], w_up.shape),
        ("w_down", w_down[0], g_big[4], m_w_down[0], v_w_down[0], w_down.shape),
    ]
    out_g, out_d, out_m, out_v = [], [], [], []
    for name, w2, g2, m2, v2, shape in plan:
        delta, new_m, new_v = _adamw(w2, g2, m2, v2, "adamw_" + name)
        out_g.append(g2.reshape(shape))
        out_d.append(delta.reshape(shape))
        out_m.append(new_m.reshape(shape))
        out_v.append(new_v.reshape(shape))
    return (loss, gx.reshape(x.shape), *out_g, *out_d, *out_m, *out_v)
```

```python
import functools

import jax
import jax.numpy as jnp
from jax import lax
from jax.experimental import pallas as pl
from jax.experimental.pallas import tpu as pltpu

F32 = jnp.float32
BF16 = jnp.bfloat16
MESH = pl.DeviceIdType.MESH

EPS = 1e-6
HEAD_DIM = 64
HEADS_PER_BLOCK = 2
LANES = 128
NEG_QK_SCALE = -0.125
POOL_WINDOWS = (2, 4, 8, 16)
POOL_GROUP = 128
HALO = 16
N_MOD = 6
MOD_ROWS = 8
N_CHIPS = 4
N_DEV = 8
VMEM_LIMIT = 56 * 1024 * 1024

ADAM_LR = 0.001
ADAM_B1 = 0.9
ADAM_B2 = 0.999
ADAM_EPS = 1e-08
ADAM_WD = 0.01
ADAM_STEP = 10

TOKEN_TILE = 512
ATTN_TILE = 512
ATTN_KEY_TILE = 256
ATTN_ROW_CHUNK = 32


def _dot(a, b):
    return jnp.dot(a, b, preferred_element_type=F32)


def _dot_nt(a, b):
    return lax.dot_general(a, b, (((1,), (1,)), ((), ())), preferred_element_type=F32)


def _dot_tn(a, b):
    return lax.dot_general(a, b, (((0,), (0,)), ((), ())), preferred_element_type=F32)


def _split(v):
    hi = v.astype(BF16)
    lo = (v - hi.astype(F32)).astype(BF16)
    return hi, lo


def _rms(v):
    return lax.rsqrt(jnp.mean(v * v, axis=-1, keepdims=True) + EPS)


def _norm_bwd(dn, n, r):
    return r * (dn - n * jnp.mean(dn * n, axis=-1, keepdims=True))


def _sigmoid(v):
    return 1.0 / (1.0 + jnp.exp(-v))


def _colsum(v):
    return jnp.sum(v, axis=0, keepdims=True)


def _params(sem=None):
    return pltpu.CompilerParams(dimension_semantics=sem, vmem_limit_bytes=VMEM_LIMIT)


def _position():
    return lax.axis_index("x"), lax.axis_index("y"), lax.axis_index("c")


def _prenorm_proj(x, mod, g_pre, w_in, seq, tm):
    t_all, d = x.shape
    nt = seq // tm
    p = w_in.shape[2]

    def body(x_ref, mod_ref, g_ref, w_ref, h_ref, q_ref, k_ref, v_ref, u_ref, kt_ref, vt_ref):
        xf = x_ref[...]
        n = xf * _rms(xf)
        h = (n * g_ref[...]) * (1.0 + mod_ref[0, 1:2, :]) + mod_ref[0, 0:1, :]
        hb = h.astype(BF16)
        h_ref[...] = hb
        q_ref[...] = (_dot(hb, w_ref[0]) * NEG_QK_SCALE).astype(BF16)
        kf = _dot(hb, w_ref[1])
        vf = _dot(hb, w_ref[2])
        k_ref[...] = kf.astype(BF16)
        v_ref[...] = vf.astype(BF16)
        kt_ref[...] = kf.T.astype(BF16)
        vt_ref[...] = vf.T.astype(BF16)
        u_ref[...] = _dot(hb, w_ref[3])

    tok = lambda i: (i, 0)
    tok_t = lambda i: (0, i)
    return pl.pallas_call(
        body, name="prenorm_proj", grid=(t_all // tm,),
        in_specs=[pl.BlockSpec((tm, d), tok),
                  pl.BlockSpec((1, MOD_ROWS, d), lambda i: (i // nt, 0, 0)),
                  pl.BlockSpec((1, d), lambda i: (0, 0)),
                  pl.BlockSpec((N_CHIPS, d, p), lambda i: (0, 0, 0))],
        out_specs=[pl.BlockSpec((tm, d), tok)] + [pl.BlockSpec((tm, p), tok)] * 4 + [pl.BlockSpec((p, tm), tok_t)] * 2,
        out_shape=[jax.ShapeDtypeStruct((t_all, d), BF16)] + [jax.ShapeDtypeStruct((t_all, p), BF16)] * 3
        + [jax.ShapeDtypeStruct((t_all, p), F32)] + [jax.ShapeDtypeStruct((p, t_all), BF16)] * 2,
        compiler_params=_params(("arbitrary",)),
    )(x, mod, g_pre, w_in)


def _tri_matrix(tk, kind):
    j = lax.broadcasted_iota(jnp.int32, (2 * tk, tk), 0) % tk
    s = lax.broadcasted_iota(jnp.int32, (2 * tk, tk), 1)
    return {"after": j > s, "upto": j <= s, "before": j < s}[kind].astype(BF16)


def _row_sums(v):
    return jnp.broadcast_to(jnp.sum(v, axis=-1, keepdims=True), (v.shape[0], LANES))


def _across(v, n):
    return jnp.concatenate([v] * (n // LANES), axis=1)


def _all_masked(c, diag, rc, tk):
    return diag is not None and diag * tk >= (c + 1) * rc - 1


def _some_masked(c, diag, rc, tk):
    return diag is not None and diag * tk + tk - 1 >= c * rc


def _attn_fwd(qn, k, vt, seq, tq, tk):
    t_all, w = qn.shape
    nb, nq, ndiag = t_all // seq, seq // tq, tq // tk
    assert ndiag % 2 == 0, "two key blocks per loop trip"
    rc = ATTN_ROW_CHUNK
    heads = range(HEADS_PER_BLOCK)

    def body(q_ref, k_ref, vt_ref, tri_ref, o_ref, l_ref,
             z_buf, ls_buf, hl_buf, aft_buf, w_buf, tot_buf, acc_t, run_buf):
        i = pl.program_id(2)
        nblk = (i + 1) * ndiag
        lane = lax.broadcasted_iota(jnp.int32, (1, LANES), 1)
        sub = lax.broadcasted_iota(jnp.int32, (LANES, 1), 0)
        row = lax.broadcasted_iota(jnp.int32, (rc, tk), 0)
        col = lax.broadcasted_iota(jnp.int32, (rc, tk), 1)
        first = lane < HEAD_DIM
        q2 = q_ref[...]
        qs = [jnp.where(first, q2, jnp.zeros_like(q2)), jnp.where(first, jnp.zeros_like(q2), q2)]
        acc_t[...] = jnp.zeros_like(acc_t)
        run_buf[...] = jnp.zeros_like(run_buf)
        w_buf[1] = jnp.zeros((HEADS_PER_BLOCK, tq, tk), BF16)

        def causal(c, diag):
            return (col + diag * tk) < (row + c * rc)

        def scores(blk, slot):
            kj = k_ref[pl.ds(pl.multiple_of(blk * tk, tk), tk), :]
            for h in heads:
                z_buf[slot, h] = _dot_nt(qs[h], kj)

        def values(blk, slot):
            vtj = vt_ref[:, pl.ds(pl.multiple_of(blk * tk, tk), tk)]
            zero = jnp.zeros_like(vtj)
            acc_t[...] += (_dot_nt(jnp.where(sub < HEAD_DIM, vtj, zero), w_buf[slot, 0])
                           + _dot_nt(jnp.where(sub < HEAD_DIM, zero, vtj), w_buf[slot, 1]))

        def softplus_stage(h, slot, diag):
            for c in range(tq // rc):
                rows = slice(c * rc, (c + 1) * rc)
                if _all_masked(c, diag, rc, tk):
                    hl_buf[h, rows, :] = jnp.zeros((rc, 2 * tk), BF16)
                    tot_buf[h, rows, :] = jnp.zeros((rc, LANES), F32)
                    continue
                nz = z_buf[slot, h, rows, :]
                l1 = jnp.minimum(nz, 0.0) - jnp.log(1.0 + jnp.exp(-jnp.abs(nz)))
                if _some_masked(c, diag, rc, tk):
                    l1 = jnp.where(causal(c, diag), l1, 0.0)
                hi, lo = _split(l1)
                hl_buf[h, rows, 0:tk] = hi
                hl_buf[h, rows, tk:2 * tk] = lo
                ls_buf[h, rows, :] = l1 - nz
                tot_buf[h, rows, :] = _row_sums(l1)

        def weights_stage(h, slot, diag):
            for c in range(tq // rc):
                rows = slice(c * rc, (c + 1) * rc)
                if _all_masked(c, diag, rc, tk):
                    w_buf[slot, h, rows, :] = jnp.zeros((rc, tk), BF16)
                    continue
                wgt = jnp.exp((ls_buf[h, rows, :] + aft_buf[h, rows, :]) + _across(run_buf[h, rows, :], tk))
                if _some_masked(c, diag, rc, tk):
                    wgt = jnp.where(causal(c, diag), wgt, 0.0)
                w_buf[slot, h, rows, :] = wgt.astype(BF16)
                run_buf[h, rows, :] += tot_buf[h, rows, :]

        def position(blk, slot, diag):
            scores(jnp.maximum(blk - 1, 0), 1 - slot)
            for h in heads:
                softplus_stage(h, slot, diag)
                aft_buf[h] = _dot(hl_buf[h], tri_ref[...])
            values(jnp.minimum(blk + 1, nblk - 1), 1 - slot)
            for h in heads:
                weights_stage(h, slot, diag)

        scores(nblk - 1, 0)
        for p in range(ndiag):
            position(nblk - 1 - p, p % 2, ndiag - 1 - p)

        def trip(jj, carry):
            for u in range(2):
                position(i * ndiag - 1 - 2 * jj - u, u, None)
            return carry

        lax.fori_loop(0, (i * ndiag) // 2, trip, 0)
        values(0, 1)
        o_ref[...] = acc_t[...].T.astype(BF16)
        l_ref[...] = jnp.where(first, run_buf[0], run_buf[1])

    qmap = lambda b, hp, i: (b * nq + i, hp)
    nh = HEADS_PER_BLOCK
    return pl.pallas_call(
        body, name="attn_fwd", grid=(nb, w // LANES, nq),
        in_specs=[pl.BlockSpec((tq, LANES), qmap), pl.BlockSpec((seq, LANES), lambda b, hp, i: (b, hp)),
                  pl.BlockSpec((LANES, seq), lambda b, hp, i: (hp, b)),
                  pl.BlockSpec((2 * tk, tk), lambda b, hp, i: (0, 0))],
        out_specs=[pl.BlockSpec((tq, LANES), qmap), pl.BlockSpec((tq, LANES), qmap)],
        out_shape=[jax.ShapeDtypeStruct((t_all, w), BF16), jax.ShapeDtypeStruct((t_all, w), F32)],
        scratch_shapes=[pltpu.VMEM((2, nh, tq, tk), F32), pltpu.VMEM((nh, tq, tk), F32),
                        pltpu.VMEM((nh, tq, 2 * tk), BF16), pltpu.VMEM((nh, tq, tk), F32),
                        pltpu.VMEM((2, nh, tq, tk), BF16), pltpu.VMEM((nh, tq, LANES), F32),
                        pltpu.VMEM((LANES, tq), F32), pltpu.VMEM((nh, tq, LANES), F32)],
        compiler_params=_params(("arbitrary", "arbitrary", "arbitrary")),
    )(qn, k, vt, _tri_matrix(tk, "after"))


def _window_sums(ext, rows, offset, forward):
    r = lax.broadcasted_iota(jnp.int32, (rows, rows + HALO), 0)
    e = lax.broadcasted_iota(jnp.int32, (rows, rows + HALO), 1)
    hi, lo = _split(ext)
    out = []
    for g, win in enumerate(POOL_WINDOWS):
        if forward:
            band = (e >= r) & (e < r + win)
        else:
            band = (e <= r + offset) & (e > r + offset - win)
        bm = band.astype(BF16)
        cols = slice(g * POOL_GROUP, (g + 1) * POOL_GROUP)
        out.append(_dot(bm, hi[:, cols]) + _dot(bm, lo[:, cols]))
    return out


def _window_counts(pos):
    return [jnp.minimum(pos + 1, win).astype(F32) for win in POOL_WINDOWS]


def _mixer_post(u, o, x, mod, g_post, g_fpre, w_pool, pool_scale, w_out, seq, tm):
    t_all, d = x.shape
    nt = seq // tm
    p = u.shape[1]

    def body(u_ref, halo_ref, o_ref, x_ref, mod_ref, gp_ref, gf_ref, wp_ref, ps_ref, wo_ref,
             pooled_ref, mixin_ref, mix_ref, x1_ref, h2_ref):
        it = pl.program_id(0) % nt
        uf = u_ref[...]
        halo = jnp.where(it == 0, 0.0, halo_ref[...])
        ext = jnp.concatenate([halo, uf], axis=0)
        pos = it * tm + lax.broadcasted_iota(jnp.int32, (tm, 1), 0)
        sums = _window_sums(ext, tm, HALO, False)
        cnts = _window_counts(pos)
        pools = []
        for g in range(len(POOL_WINDOWS)):
            cols = slice(g * POOL_GROUP, (g + 1) * POOL_GROUP)
            pooled = (sums[g] / cnts[g] - uf[:, cols]).astype(BF16)
            pooled_ref[:, cols] = pooled
            yg = _dot(pooled, wp_ref[g].astype(BF16))
            pools.append((yg * ps_ref[:, cols]).astype(BF16))
        mixin = jnp.concatenate([o_ref[...]] + pools, axis=1)
        mixin_ref[...] = mixin
        mix = _dot(mixin, wo_ref[...])
        mix_ref[...] = mix
        n2 = mix * _rms(mix)
        x1 = x_ref[...] + mod_ref[0, 2:3, :] * (n2 * gp_ref[...])
        x1_ref[...] = x1
        n3 = x1 * _rms(x1)
        h2 = (n3 * gf_ref[...]) * (1.0 + mod_ref[0, 4:5, :]) + mod_ref[0, 3:4, :]
        h2_ref[...] = h2.astype(BF16)

    tok = lambda i: (i, 0)
    const2 = lambda i: (0, 0)
    hb = tm // HALO
    return pl.pallas_call(
        body, name="mixer_post", grid=(t_all // tm,),
        in_specs=[pl.BlockSpec((tm, p), tok),
                  pl.BlockSpec((HALO, p), lambda i: (jnp.maximum(i * hb - 1, 0), 0)),
                  pl.BlockSpec((tm, p), tok),
                  pl.BlockSpec((tm, d), tok),
                  pl.BlockSpec((1, MOD_ROWS, d), lambda i: (i // nt, 0, 0)),
                  pl.BlockSpec((1, d), const2), pl.BlockSpec((1, d), const2),
                  pl.BlockSpec(w_pool.shape, lambda i: (0, 0, 0)),
                  pl.BlockSpec((1, p), const2),
                  pl.BlockSpec((d, d), const2)],
        out_specs=[pl.BlockSpec((tm, p), tok), pl.BlockSpec((tm, d), tok), pl.BlockSpec((tm, d), tok),
                   pl.BlockSpec((tm, d), tok), pl.BlockSpec((tm, d), tok)],
        out_shape=[jax.ShapeDtypeStruct((t_all, p), BF16), jax.ShapeDtypeStruct((t_all, d), BF16),
                   jax.ShapeDtypeStruct((t_all, d), F32), jax.ShapeDtypeStruct((t_all, d), F32),
                   jax.ShapeDtypeStruct((t_all, d), BF16)],
        compiler_params=_params(("arbitrary",)),
    )(u, u, o, x, mod, g_post, g_fpre, w_pool, pool_scale, w_out)


def _ffn_fwd(h2, w_g, w_u, w_d, x1, tgt, mod, g_post, seq, tm):
    t_all, d = x1.shape
    nt = seq // tm
    nk, _, ff = w_g.shape

    def body(h_ref, wg_ref, wu_ref, wd_ref, x1_ref, t_ref, mod_ref, g_ref,
             a_ref, b_ref, fin_ref, dy_ref, df_ref, loss_ref, accb_ref, accg_ref, facc):
        i, k = pl.program_id(0), pl.program_id(1)
        hb = h_ref[...]
        a = _dot(hb, wg_ref[0])
        b = _dot(hb, wu_ref[0])
        a_ref[0] = a.astype(BF16)
        b_ref[0] = b.astype(BF16)
        fin = ((a * _sigmoid(a)) * b).astype(BF16)
        fin_ref[0] = fin
        part = _dot(fin, wd_ref[0])

        @pl.when(k == 0)
        def _():
            facc[...] = part

        @pl.when(k > 0)
        def _():
            facc[...] += part

        @pl.when(k == nk - 1)
        def _():
            f = facc[...]
            r4 = _rms(f)
            n4 = f * r4
            gate = mod_ref[0, 5:6, :]
            g = g_ref[...]
            err = (x1_ref[...] + gate * (n4 * g)) - t_ref[...]
            dy = err * (1.0 / d)
            dy_ref[...] = dy

            @pl.when(i == 0)
            def _():
                loss_ref[...] = jnp.zeros_like(loss_ref)
                accg_ref[...] = jnp.zeros_like(accg_ref)

            @pl.when(i % nt == 0)
            def _():
                accb_ref[...] = jnp.zeros_like(accb_ref)

            loss_ref[...] += (0.5 / d) * jnp.sum(err * err)
            accb_ref[0, 0:1, :] += _colsum(dy * (n4 * g))
            accg_ref[0:1, :] += _colsum((dy * gate) * n4)
            dn4 = (dy * gate) * g
            df_ref[...] = _norm_bwd(dn4, n4, r4).astype(BF16)

    tok = lambda i, k: (i, 0)
    ktok = lambda i, k: (k, i, 0)
    kw = lambda i, k: (k, 0, 0)
    const2 = lambda i, k: (0, 0)
    return pl.pallas_call(
        body, name="ffn_fwd", grid=(t_all // tm, nk),
        in_specs=[pl.BlockSpec((tm, d), tok),
                  pl.BlockSpec((1, d, ff), kw), pl.BlockSpec((1, d, ff), kw), pl.BlockSpec((1, ff, d), kw),
                  pl.BlockSpec((tm, d), tok), pl.BlockSpec((tm, d), tok),
                  pl.BlockSpec((1, MOD_ROWS, d), lambda i, k: (i // nt, 0, 0)),
                  pl.BlockSpec((1, d), const2)],
        out_specs=[pl.BlockSpec((1, tm, ff), ktok)] * 3
        + [pl.BlockSpec((tm, d), tok), pl.BlockSpec((tm, d), tok),
           pl.BlockSpec((8, LANES), const2),
           pl.BlockSpec((1, 8, d), lambda i, k: (i // nt, 0, 0)),
           pl.BlockSpec((8, d), const2)],
        out_shape=[jax.ShapeDtypeStruct((nk, t_all, ff), BF16)] * 3
        + [jax.ShapeDtypeStruct((t_all, d), F32), jax.ShapeDtypeStruct((t_all, d), BF16),
           jax.ShapeDtypeStruct((8, LANES), F32),
           jax.ShapeDtypeStruct((t_all // seq, 8, d), F32),
           jax.ShapeDtypeStruct((8, d), F32)],
        scratch_shapes=[pltpu.VMEM((tm, d), F32)],
        compiler_params=_params(("arbitrary", "arbitrary")),
    )(h2, w_g, w_u, w_d, x1, tgt, mod, g_post)


def _ffn_bwd(df, a, b, w_d, w_g, w_u, x1, dy, mix, mod, g_fpre, g_mpost, seq, tm):
    t_all, d = x1.shape
    nt = seq // tm
    nk, _, ff = w_g.shape

    def body(df_ref, a_ref, b_ref, wd_ref, wg_ref, wu_ref, x1_ref, dy_ref, mix_ref, mod_ref, gf_ref, gm_ref,
             da_ref, db_ref, dx1_ref, dmix_ref, accb_ref, accg_ref, hacc):
        i, k = pl.program_id(0), pl.program_id(1)
        dfin = _dot_nt(df_ref[...], wd_ref[0])
        af = a_ref[0].astype(F32)
        bf = b_ref[0].astype(F32)
        sig = _sigmoid(af)
        da = ((dfin * bf) * (sig * (1.0 + af * (1.0 - sig)))).astype(BF16)
        db = (dfin * (af * sig)).astype(BF16)
        da_ref[0] = da
        db_ref[0] = db
        part = _dot_nt(da, wg_ref[0]) + _dot_nt(db, wu_ref[0])

        @pl.when(k == 0)
        def _():
            hacc[...] = part

        @pl.when(k > 0)
        def _():
            hacc[...] += part

        @pl.when(k == nk - 1)
        def _():
            @pl.when(i == 0)
            def _():
                accg_ref[...] = jnp.zeros_like(accg_ref)

            @pl.when(i % nt == 0)
            def _():
                accb_ref[...] = jnp.zeros_like(accb_ref)

            dh2 = hacc[...]
            x1 = x1_ref[...]
            r3 = _rms(x1)
            n3 = x1 * r3
            g3 = gf_ref[...]
            scale1 = 1.0 + mod_ref[0, 4:5, :]
            accb_ref[0, 0:1, :] += _colsum(dh2)
            accb_ref[0, 1:2, :] += _colsum(dh2 * (n3 * g3))
            accg_ref[0:1, :] += _colsum((dh2 * scale1) * n3)
            dx1 = dy_ref[...] + _norm_bwd((dh2 * scale1) * g3, n3, r3)
            dx1_ref[...] = dx1
            mix = mix_ref[...]
            r2 = _rms(mix)
            n2 = mix * r2
            g2 = gm_ref[...]
            gate = mod_ref[0, 2:3, :]
            accb_ref[0, 2:3, :] += _colsum(dx1 * (n2 * g2))
            accg_ref[1:2, :] += _colsum((dx1 * gate) * n2)
            dmix_ref[...] = _norm_bwd((dx1 * gate) * g2, n2, r2).astype(BF16)

    tok = lambda i, k: (i, 0)
    ktok = lambda i, k: (k, i, 0)
    kw = lambda i, k: (k, 0, 0)
    const2 = lambda i, k: (0, 0)
    return pl.pallas_call(
        body, name="ffn_bwd", grid=(t_all // tm, nk),
        in_specs=[pl.BlockSpec((tm, d), tok),
                  pl.BlockSpec((1, tm, ff), ktok), pl.BlockSpec((1, tm, ff), ktok),
                  pl.BlockSpec((1, ff, d), kw), pl.BlockSpec((1, d, ff), kw), pl.BlockSpec((1, d, ff), kw),
                  pl.BlockSpec((tm, d), tok), pl.BlockSpec((tm, d), tok), pl.BlockSpec((tm, d), tok),
                  pl.BlockSpec((1, MOD_ROWS, d), lambda i, k: (i // nt, 0, 0)),
                  pl.BlockSpec((1, d), const2), pl.BlockSpec((1, d), const2)],
        out_specs=[pl.BlockSpec((1, tm, ff), ktok)] * 2
        + [pl.BlockSpec((tm, d), tok), pl.BlockSpec((tm, d), tok),
           pl.BlockSpec((1, 8, d), lambda i, k: (i // nt, 0, 0)),
           pl.BlockSpec((8, d), const2)],
        out_shape=[jax.ShapeDtypeStruct((nk, t_all, ff), BF16)] * 2
        + [jax.ShapeDtypeStruct((t_all, d), F32), jax.ShapeDtypeStruct((t_all, d), BF16),
           jax.ShapeDtypeStruct((t_all // seq, 8, d), F32),
           jax.ShapeDtypeStruct((8, d), F32)],
        scratch_shapes=[pltpu.VMEM((tm, d), F32)],
        compiler_params=_params(("arbitrary", "arbitrary")),
    )(df, a, b, w_d, w_g, w_u, x1, dy, mix, mod, g_fpre, g_mpost)


def _mixer_bwd(dmix, w_out, pooled, w_pool, pool_scale, seq, tm):
    t_all, d = dmix.shape
    p = pooled.shape[1]
    ng = len(POOL_WINDOWS)

    def body(dm_ref, wo_ref, pooled_ref, wp_ref, ps_ref, do_ref, dpd_ref, dps_ref, dwp_ref):
        i = pl.program_id(0)

        @pl.when(i == 0)
        def _():
            dps_ref[...] = jnp.zeros_like(dps_ref)
            dwp_ref[...] = jnp.zeros_like(dwp_ref)

        dmixin = _dot_nt(dm_ref[...], wo_ref[...])
        do_ref[...] = dmixin[:, :p].astype(BF16)
        for g in range(ng):
            cols = slice(g * POOL_GROUP, (g + 1) * POOL_GROUP)
            dpool = dmixin[:, p + g * POOL_GROUP:p + (g + 1) * POOL_GROUP]
            pooled = pooled_ref[:, cols]
            wpg = wp_ref[g].astype(BF16)
            yg = _dot(pooled, wpg)
            dps_ref[0:1, cols] += _colsum(dpool * yg)
            dyg = (dpool * ps_ref[:, cols]).astype(BF16)
            dwp_ref[g] += _dot_tn(pooled, dyg)
            dpd_ref[:, cols] = _dot_nt(dyg, wpg)

    tok = lambda i: (i, 0)
    const2 = lambda i: (0, 0)
    const3 = lambda i: (0, 0, 0)
    return pl.pallas_call(
        body, name="mixer_bwd", grid=(t_all // tm,),
        in_specs=[pl.BlockSpec((tm, d), tok), pl.BlockSpec((d, d), const2), pl.BlockSpec((tm, p), tok),
                  pl.BlockSpec(w_pool.shape, const3), pl.BlockSpec((1, p), const2)],
        out_specs=[pl.BlockSpec((tm, p), tok), pl.BlockSpec((tm, p), tok),
                   pl.BlockSpec((8, p), const2), pl.BlockSpec(w_pool.shape, const3)],
        out_shape=[jax.ShapeDtypeStruct((t_all, p), BF16), jax.ShapeDtypeStruct((t_all, p), F32),
                   jax.ShapeDtypeStruct((8, p), F32), jax.ShapeDtypeStruct(w_pool.shape, F32)],
        compiler_params=_params(("arbitrary",)),
    )(dmix, w_out, pooled, w_pool, pool_scale)


def _attn_bwd(qn, k, kt, v, do, ltot, seq, tq, tk):
    t_all, w = qn.shape
    nb, nq, ndiag, nkb = t_all // seq, seq // tq, tq // tk, seq // tk
    assert ndiag % 2 == 0, "two key blocks per loop trip"
    rc = ATTN_ROW_CHUNK
    nh = HEADS_PER_BLOCK
    heads = range(nh)

    def body(q_ref, k_ref, kt_ref, v_ref, do_ref, l_ref, up_ref, bf_ref, dq_ref, dk_ref, dv_ref,
             z_buf, dw_buf, ls_buf, hl_buf, upto_buf, g_buf, gb_buf, before_buf, w_buf, dz_buf,
             totl_buf, totg_buf, rem_buf, preg_buf, qnt_buf, dot_buf, dq_t, dk_t, dv_t):
        i = pl.program_id(2)
        nblk = (i + 1) * ndiag

        @pl.when(i == 0)
        def _():
            dk_t[...] = jnp.zeros_like(dk_t)
            dv_t[...] = jnp.zeros_like(dv_t)

        lane = lax.broadcasted_iota(jnp.int32, (1, LANES), 1)
        sub = lax.broadcasted_iota(jnp.int32, (LANES, 1), 0)
        row = lax.broadcasted_iota(jnp.int32, (rc, tk), 0)
        col = lax.broadcasted_iota(jnp.int32, (rc, tk), 1)
        first = lane < HEAD_DIM
        upper = sub < HEAD_DIM
        q2 = q_ref[...]
        do2 = do_ref[...]
        l2 = l_ref[...]
        qs = [jnp.where(first, q2, jnp.zeros_like(q2)), jnp.where(first, jnp.zeros_like(q2), q2)]
        dos = [jnp.where(first, do2, jnp.zeros_like(do2)), jnp.where(first, jnp.zeros_like(do2), do2)]
        for src, dst in ((q2, qnt_buf), (do2, dot_buf)):
            t = src.astype(F32).T
            dst[:, 0:tq] = jnp.where(upper, t, 0.0).astype(BF16)
            dst[:, tq:2 * tq] = jnp.where(upper, 0.0, t).astype(BF16)
        for h in heads:
            rem_buf[h] = jnp.where(first if h == 0 else ~first, l2, pltpu.roll(l2, HEAD_DIM, 1))
        preg_buf[...] = jnp.zeros_like(preg_buf)
        dq_t[...] = jnp.zeros_like(dq_t)
        w_buf[1] = jnp.zeros((nh * tq, tk), BF16)
        dz_buf[1] = jnp.zeros((nh * tq, tk), BF16)

        def causal(c, diag):
            return (col + diag * tk) < (row + c * rc)

        def scores(blk, slot):
            off = pl.multiple_of(blk * tk, tk)
            kj = k_ref[pl.ds(off, tk), :]
            vj = v_ref[pl.ds(off, tk), :]
            for h in heads:
                z_buf[slot, h] = _dot_nt(qs[h], kj)
                dw_buf[slot, h] = _dot_nt(dos[h], vj)

        def gradients(blk, slot):
            off = pl.multiple_of(blk * tk, tk)
            ktj = kt_ref[:, pl.ds(off, tk)]
            zero = jnp.zeros_like(ktj)
            dq_t[...] += (_dot_nt(jnp.where(upper, ktj, zero), dz_buf[slot, 0:tq, :])
                          + _dot_nt(jnp.where(upper, zero, ktj), dz_buf[slot, tq:2 * tq, :]))
            dk_t[blk] += _dot(qnt_buf[...], dz_buf[slot])
            dv_t[blk] += _dot(dot_buf[...], w_buf[slot])

        def softplus_stage(h, slot, diag):
            for c in range(tq // rc):
                rows = slice(c * rc, (c + 1) * rc)
                if _all_masked(c, diag, rc, tk):
                    hl_buf[h, rows, :] = jnp.zeros((rc, 2 * tk), BF16)
                    continue
                nz = z_buf[slot, h, rows, :]
                l1 = jnp.minimum(nz, 0.0) - jnp.log(1.0 + jnp.exp(-jnp.abs(nz)))
                if _some_masked(c, diag, rc, tk):
                    l1 = jnp.where(causal(c, diag), l1, 0.0)
                hi, lo = _split(l1)
                hl_buf[h, rows, 0:tk] = hi
                hl_buf[h, rows, tk:2 * tk] = lo
                ls_buf[h, rows, :] = l1 - nz
                totl_buf[h, rows, :] = _row_sums(l1)

        def weights_stage(h, slot, diag):
            for c in range(tq // rc):
                rows = slice(c * rc, (c + 1) * rc)
                stacked = slice(h * tq + c * rc, h * tq + (c + 1) * rc)
                if _all_masked(c, diag, rc, tk):
                    w_buf[slot, stacked, :] = jnp.zeros((rc, tk), BF16)
                    gb_buf[h, rows, :] = jnp.zeros((rc, tk), BF16)
                    continue
                wgt = jnp.exp(ls_buf[h, rows, :] + (_across(rem_buf[h, rows, :], tk) - upto_buf[h, rows, :]))
                if _some_masked(c, diag, rc, tk):
                    wgt = jnp.where(causal(c, diag), wgt, 0.0)
                w_buf[slot, stacked, :] = wgt.astype(BF16)
                g = wgt * dw_buf[slot, h, rows, :]
                g_buf[h, rows, :] = g
                gb_buf[h, rows, :] = g.astype(BF16)
                totg_buf[h, rows, :] = _row_sums(g)
                rem_buf[h, rows, :] -= totl_buf[h, rows, :]

        def dscore_stage(h, slot, diag):
            for c in range(tq // rc):
                rows = slice(c * rc, (c + 1) * rc)
                stacked = slice(h * tq + c * rc, h * tq + (c + 1) * rc)
                if _all_masked(c, diag, rc, tk):
                    dz_buf[slot, stacked, :] = jnp.zeros((rc, tk), BF16)
                    continue
                sig = jnp.exp(ls_buf[h, rows, :])
                g = g_buf[h, rows, :]
                dnz = sig * (before_buf[h, rows, :] + _across(preg_buf[h, rows, :], tk)) - g * (1.0 - sig)
                if _some_masked(c, diag, rc, tk):
                    dnz = jnp.where(causal(c, diag), dnz, 0.0)
                dz_buf[slot, stacked, :] = dnz.astype(BF16)
                preg_buf[h, rows, :] += totg_buf[h, rows, :]

        def position(blk, slot, diag, prefetch):
            if prefetch:
                scores(blk + 1, 1 - slot)
            for h in heads:
                softplus_stage(h, slot, diag)
                upto_buf[h] = _dot(hl_buf[h], up_ref[...])
            gradients(jnp.maximum(blk - 1, 0), 1 - slot)
            for h in heads:
                weights_stage(h, slot, diag)
                before_buf[h] = _dot(gb_buf[h], bf_ref[...])
            for h in heads:
                dscore_stage(h, slot, diag)

        scores(0, 0)

        def trip(jj, carry):
            for u in range(2):
                position(2 * jj + u, u, None, True)
            return carry

        lax.fori_loop(0, (i * ndiag) // 2, trip, 0)
        for d in range(ndiag):
            position(i * ndiag + d, d % 2, d, d < ndiag - 1)
        gradients(nblk - 1, 1)
        dq_ref[...] = (dq_t[...].T * NEG_QK_SCALE).astype(BF16)

        @pl.when(i == nq - 1)
        def _():
            for blk in range(nkb):
                dk_ref[blk * tk:(blk + 1) * tk, :] = dk_t[blk].T.astype(BF16)
                dv_ref[blk * tk:(blk + 1) * tk, :] = dv_t[blk].T.astype(BF16)

    qmap = lambda b, hp, i: (b * nq + i, hp)
    kmap = lambda b, hp, i: (b, hp)
    const = lambda b, hp, i: (0, 0)
    return pl.pallas_call(
        body, name="attn_bwd", grid=(nb, w // LANES, nq),
        in_specs=[pl.BlockSpec((tq, LANES), qmap), pl.BlockSpec((seq, LANES), kmap),
                  pl.BlockSpec((LANES, seq), lambda b, hp, i: (hp, b)), pl.BlockSpec((seq, LANES), kmap),
                  pl.BlockSpec((tq, LANES), qmap), pl.BlockSpec((tq, LANES), qmap),
                  pl.BlockSpec((2 * tk, tk), const), pl.BlockSpec((tk, tk), const)],
        out_specs=[pl.BlockSpec((tq, LANES), qmap), pl.BlockSpec((seq, LANES), kmap), pl.BlockSpec((seq, LANES), kmap)],
        out_shape=[jax.ShapeDtypeStruct((t_all, w), BF16)] * 3,
        scratch_shapes=[pltpu.VMEM((2, nh, tq, tk), F32), pltpu.VMEM((2, nh, tq, tk), F32),
                        pltpu.VMEM((nh, tq, tk), F32), pltpu.VMEM((nh, tq, 2 * tk), BF16),
                        pltpu.VMEM((nh, tq, tk), F32), pltpu.VMEM((nh, tq, tk), F32),
                        pltpu.VMEM((nh, tq, tk), BF16), pltpu.VMEM((nh, tq, tk), F32),
                        pltpu.VMEM((2, nh * tq, tk), BF16), pltpu.VMEM((2, nh * tq, tk), BF16),
                        pltpu.VMEM((nh, tq, LANES), F32), pltpu.VMEM((nh, tq, LANES), F32),
                        pltpu.VMEM((nh, tq, LANES), F32), pltpu.VMEM((nh, tq, LANES), F32),
                        pltpu.VMEM((LANES, nh * tq), BF16), pltpu.VMEM((LANES, nh * tq), BF16),
                        pltpu.VMEM((LANES, tq), F32), pltpu.VMEM((nkb, LANES, tk), F32),
                        pltpu.VMEM((nkb, LANES, tk), F32)],
        compiler_params=_params(("arbitrary", "arbitrary", "arbitrary")),
    )(qn, k, kt, v, do, ltot, _tri_matrix(tk, "upto"), _tri_matrix(tk, "before")[:tk])


def _inproj_bwd(dq, dk, dv, dpd, x, dx1, mod, g_pre, w_in, seq, tm):
    t_all, d = x.shape
    nt = seq // tm
    p = dq.shape[1]

    def body(dq_ref, dk_ref, dv_ref, dpd_ref, halo_ref, x_ref, dx1_ref, mod_ref, g_ref, w_ref,
             gx_ref, du_ref, accb_ref, accg_ref):
        i = pl.program_id(0)
        it = i % nt

        @pl.when(i == 0)
        def _():
            accg_ref[...] = jnp.zeros_like(accg_ref)

        @pl.when(it == 0)
        def _():
            accb_ref[...] = jnp.zeros_like(accb_ref)

        dpd = dpd_ref[...]
        pos = it * tm + lax.broadcasted_iota(jnp.int32, (tm, 1), 0)
        cnts = _window_counts(pos)
        halo = jnp.where(it == nt - 1, 0.0, halo_ref[...])
        scaled = []
        halos = []
        for g, win in enumerate(POOL_WINDOWS):
            cols = slice(g * POOL_GROUP, (g + 1) * POOL_GROUP)
            scaled.append(dpd[:, cols] / cnts[g])
            halos.append(halo[:, cols] / float(win))
        ext = jnp.concatenate([jnp.concatenate(scaled, axis=1), jnp.concatenate(halos, axis=1)], axis=0)
        sums = _window_sums(ext, tm, 0, True)
        du = (jnp.concatenate(sums, axis=1) - dpd).astype(BF16)
        du_ref[...] = du
        dh1 = (_dot_nt(dq_ref[...], w_ref[0]) + _dot_nt(dk_ref[...], w_ref[1])
               + _dot_nt(dv_ref[...], w_ref[2]) + _dot_nt(du, w_ref[3]))
        xf = x_ref[...]
        r1 = _rms(xf)
        n1 = xf * r1
        g1 = g_ref[...]
        scale1 = 1.0 + mod_ref[0, 1:2, :]
        accb_ref[0, 0:1, :] += _colsum(dh1)
        accb_ref[0, 1:2, :] += _colsum(dh1 * (n1 * g1))
        accg_ref[0:1, :] += _colsum((dh1 * scale1) * n1)
        gx_ref[...] = dx1_ref[...] + _norm_bwd((dh1 * scale1) * g1, n1, r1)

    tok = lambda i: (i, 0)
    const2 = lambda i: (0, 0)
    hb = tm // HALO
    last = t_all // HALO - 1
    return pl.pallas_call(
        body, name="inproj_bwd", grid=(t_all // tm,),
        in_specs=[pl.BlockSpec((tm, p), tok), pl.BlockSpec((tm, p), tok), pl.BlockSpec((tm, p), tok),
                  pl.BlockSpec((tm, p), tok),
                  pl.BlockSpec((HALO, p), lambda i: (jnp.minimum((i + 1) * hb, last), 0)),
                  pl.BlockSpec((tm, d), tok), pl.BlockSpec((tm, d), tok),
                  pl.BlockSpec((1, MOD_ROWS, d), lambda i: (i // nt, 0, 0)),
                  pl.BlockSpec((1, d), const2),
                  pl.BlockSpec((N_CHIPS, d, p), lambda i: (0, 0, 0))],
        out_specs=[pl.BlockSpec((tm, d), tok), pl.BlockSpec((tm, p), tok),
                   pl.BlockSpec((1, 8, d), lambda i: (i // nt, 0, 0)),
                   pl.BlockSpec((8, d), const2)],
        out_shape=[jax.ShapeDtypeStruct((t_all, d), F32), jax.ShapeDtypeStruct((t_all, p), BF16),
                   jax.ShapeDtypeStruct((t_all // seq, 8, d), F32),
                   jax.ShapeDtypeStruct((8, d), F32)],
        compiler_params=_params(("arbitrary",)),
    )(dq, dk, dv, dpd, dpd, x, dx1, mod, g_pre, w_in)


def _tn_matmul(x, ys, nk, bt, name):
    t_all = x.shape[-2]
    m = x.shape[-1]
    ny = len(ys)

    def spec(arr):
        if arr.ndim == 3:
            return pl.BlockSpec((1, bt, arr.shape[-1]), lambda k, t: (k, t, 0))
        return pl.BlockSpec((bt, arr.shape[-1]), lambda k, t: (t, 0))

    def tile(ref):
        return ref[0] if len(ref.shape) == 3 else ref[...]

    def body(*refs):
        x_ref, y_refs, o_refs = refs[0], refs[1:1 + ny], refs[1 + ny:]
        t = pl.program_id(1)
        xt = tile(x_ref)
        for y_ref, o_ref in zip(y_refs, o_refs):
            part = _dot_tn(xt, tile(y_ref))

            @pl.when(t == 0)
            def _(o_ref=o_ref, part=part):
                o_ref[0] = part

            @pl.when(t > 0)
            def _(o_ref=o_ref, part=part):
                o_ref[0] += part

    return pl.pallas_call(
        body, name=name, grid=(nk, t_all // bt),
        in_specs=[spec(x)] + [spec(y) for y in ys],
        out_specs=[pl.BlockSpec((1, m, y.shape[-1]), lambda k, t: (k, 0, 0)) for y in ys],
        out_shape=[jax.ShapeDtypeStruct((nk, m, y.shape[-1]), F32) for y in ys],
        compiler_params=_params(("arbitrary", "arbitrary")),
    )(x, *ys)


def _cond_fwd(c_all, w_q, b_q, bn):
    nrow, d = c_all.shape
    ncol = w_q.shape[1]

    def body(c_ref, w_ref, b_ref, sc_ref, mod_ref):
        cf = c_ref[...]
        sc = cf * _sigmoid(cf)
        sc_ref[...] = sc
        shi, slo = _split(sc)
        whi, wlo = _split(w_ref[...])
        mod_ref[...] = (_dot(shi, whi) + _dot(shi, wlo) + _dot(slo, whi)) + b_ref[...]

    return pl.pallas_call(
        body, name="cond_fwd", grid=(ncol // bn,),
        in_specs=[pl.BlockSpec((nrow, d), lambda n: (0, 0)), pl.BlockSpec((d, bn), lambda n: (0, n)),
                  pl.BlockSpec((1, bn), lambda n: (0, n))],
        out_specs=[pl.BlockSpec((nrow, d), lambda n: (0, 0)), pl.BlockSpec((nrow, bn), lambda n: (0, n))],
        out_shape=[jax.ShapeDtypeStruct((nrow, d), F32), jax.ShapeDtypeStruct((nrow, ncol), F32)],
        compiler_params=_params(("arbitrary",)),
    )(c_all, w_q, b_q)


def _cond_bwd(sc_all, dmod_q, bn):
    nrow, d = sc_all.shape
    ncol = dmod_q.shape[1]

    def body(sc_ref, dm_ref, gw_ref):
        shi, slo = _split(sc_ref[...])
        dhi, dlo = _split(dm_ref[...])
        gw_ref[...] = _dot_tn(shi, dhi) + _dot_tn(shi, dlo) + _dot_tn(slo, dhi)

    return pl.pallas_call(
        body, name="cond_bwd", grid=(ncol // bn,),
        in_specs=[pl.BlockSpec((nrow, d), lambda n: (0, 0)), pl.BlockSpec((nrow, bn), lambda n: (0, n))],
        out_specs=pl.BlockSpec((d, bn), lambda n: (0, n)),
        out_shape=jax.ShapeDtypeStruct((d, ncol), F32),
        compiler_params=_params(("arbitrary",)),
    )(sc_all, dmod_q)


def _row_block(rows, cols, budget=1 << 18):
    best = None
    for br in range(8, rows + 1, 8):
        if rows % br == 0 and br * cols <= budget:
            best = br
    return best if best is not None else rows


def _adamw(w, g, m, v, name):
    rows, cols = w.shape
    br = _row_block(rows, cols)
    c1 = 1.0 - ADAM_B1 ** ADAM_STEP
    c2 = 1.0 - ADAM_B2 ** ADAM_STEP

    def body(w_ref, g_ref, m_ref, v_ref, d_ref, nm_ref, nv_ref):
        gf = g_ref[...]
        m2 = ADAM_B1 * m_ref[...] + (1.0 - ADAM_B1) * gf
        v2 = ADAM_B2 * v_ref[...] + (1.0 - ADAM_B2) * (gf * gf)
        nm_ref[...] = m2
        nv_ref[...] = v2
        d_ref[...] = -ADAM_LR * ((m2 / c1) / (jnp.sqrt(v2 / c2) + ADAM_EPS) + ADAM_WD * w_ref[...])

    blk = pl.BlockSpec((br, cols), lambda i: (i, 0))
    return pl.pallas_call(
        body, name=name, grid=(rows // br,),
        in_specs=[blk] * 4, out_specs=[blk] * 3,
        out_shape=[jax.ShapeDtypeStruct((rows, cols), F32)] * 3,
        compiler_params=_params(("arbitrary",)),
    )(w, g, m, v)


def _all_gather(x_shard, name):
    m_per, n = x_shard.shape

    def body(x_ref, out_ref, send_sems, recv_sems, local_sem):
        x, y, c = _position()
        me, sibling = (x, y, c), (x, y, 1 - c)
        chips = [(1 - x, y), (x, 1 - y), (1 - x, 1 - y)]

        def rows(px, py, pc):
            return out_ref.at[pl.ds((4 * px + 2 * py + pc) * m_per, m_per), :]

        def copy(k, block, to, src=None):
            return pltpu.make_async_remote_copy(
                src_ref=rows(*block) if src is None else src, dst_ref=rows(*block),
                send_sem=send_sems.at[k], recv_sem=recv_sems.at[k], device_id=to, device_id_type=MESH)

        mine = pltpu.make_async_copy(x_ref, rows(*me), local_sem)
        mine.start()
        first = [copy(0, me, sibling, src=x_ref)]
        first += [copy(1 + j, me, (*chip, c), src=x_ref) for j, chip in enumerate(chips)]
        for cp in first:
            cp.start()
        passed = [copy(4 + j, (*chip, c), sibling) for j, chip in enumerate(chips)]
        for j, chip in enumerate(chips):
            copy(1 + j, (*chip, c), me).wait_recv()
            passed[j].start()
        copy(0, sibling, me).wait_recv()
        for j, chip in enumerate(chips):
            copy(4 + j, (*chip, 1 - c), me).wait_recv()
        for cp in first + passed:
            cp.wait_send()
        mine.wait()

    return pl.pallas_call(
        body, name=name,
        out_shape=jax.ShapeDtypeStruct((N_DEV * m_per, n), x_shard.dtype),
        in_specs=[pl.BlockSpec(memory_space=pltpu.VMEM)],
        out_specs=pl.BlockSpec(memory_space=pltpu.VMEM),
        scratch_shapes=[pltpu.SemaphoreType.DMA((7,)), pltpu.SemaphoreType.DMA((7,)), pltpu.SemaphoreType.DMA],
        compiler_params=pltpu.CompilerParams(vmem_limit_bytes=VMEM_LIMIT),
    )(x_shard)


_ANY = pl.BlockSpec(memory_space=pl.ANY)


def _place_quarters(place, quarters):
    steps = 2

    def body(place_ref, *refs):
        n = len(refs) // 2
        for w_ref, o_ref in zip(refs[:n], refs[n:]):
            o_ref[0] = w_ref[...].astype(BF16)

    return pl.pallas_call(
        body, name="place_quarters",
        grid_spec=pltpu.PrefetchScalarGridSpec(
            num_scalar_prefetch=1, grid=(steps,),
            in_specs=[pl.BlockSpec((q.shape[0] // steps, q.shape[1]), lambda r, place_ref: (r, 0)) for q in quarters],
            out_specs=[pl.BlockSpec((1, q.shape[0] // steps, q.shape[1]), lambda r, place_ref: (place_ref[0], r, 0))
                       for q in quarters]),
        out_shape=[jax.ShapeDtypeStruct((N_CHIPS,) + q.shape, BF16) for q in quarters],
        compiler_params=_params(("arbitrary",)),
    )(place, *quarters)


def _gather_weights(placed):
    n = len(placed)
    shapes = [b.shape[1:] for b in placed]

    def body(*refs):
        g_refs = refs[n:2 * n]
        send_sems, recv_sems = refs[2 * n:]
        x, y, c = _position()
        sibling = (x, y, 1 - c)
        chips = [(1 - x, y), (x, 1 - y), (1 - x, 1 - y)]
        mine = 2 * x + y

        def half(a, which):
            hr = shapes[a][0] // 2
            return pl.ds(which * hr, hr)

        def over_ici(a, p, slot):
            ref = g_refs[a].at[slot, half(a, c), :]
            return pltpu.make_async_remote_copy(
                src_ref=ref, dst_ref=ref,
                send_sem=send_sems.at[6 * a + p], recv_sem=recv_sems.at[6 * a + p],
                device_id=(*chips[p], c), device_id_type=MESH)

        def over_d2d(a, p, slot, which):
            ref = g_refs[a].at[slot, half(a, which), :]
            return pltpu.make_async_remote_copy(
                src_ref=ref, dst_ref=ref,
                send_sem=send_sems.at[6 * a + 3 + p], recv_sem=recv_sems.at[6 * a + 3 + p],
                device_id=sibling, device_id_type=MESH)

        sends = []
        for a in range(n):
            for p in range(3):
                cp = over_ici(a, p, mine)
                cp.start()
                sends.append(cp)
        for a in range(n):
            for p, (cx, cy) in enumerate(chips):
                slot = 2 * cx + cy
                over_ici(a, p, slot).wait_recv()
                cp = over_d2d(a, p, slot, c)
                cp.start()
                sends.append(cp)
        for a in range(n):
            for p, (cx, cy) in enumerate(chips):
                over_d2d(a, p, 2 * cx + cy, 1 - c).wait_recv()
        for cp in sends:
            cp.wait_send()

    return pl.pallas_call(
        body, name="gather_weights",
        out_shape=[jax.ShapeDtypeStruct(b.shape, BF16) for b in placed],
        in_specs=[_ANY] * n, out_specs=[_ANY] * n,
        input_output_aliases={a: a for a in range(n)},
        scratch_shapes=[pltpu.SemaphoreType.DMA((6 * n,)), pltpu.SemaphoreType.DMA((6 * n,))],
    )(*placed)


_HBM = pl.BlockSpec(memory_space=pltpu.HBM)
_SEM = pl.BlockSpec(memory_space=pltpu.SEMAPHORE)
_EFFECT = pltpu.SideEffectType.DATAFLOW_SIDE_EFFECTING


def _quarter_halves(shapes, a, which):
    hr = shapes[a][0] // 2
    return pl.ds(which * hr, hr)


def _gather_start(placed, after):
    n = len(placed)
    m = len(after)
    shapes = [b.shape[1:] for b in placed]

    def body(*refs):
        g_refs = refs[:n]
        send_sems, recv_sems = refs[n + m], refs[n + m + 1]
        token = refs[2 * n + m + 2]
        x, y, c = _position()
        chips = [(1 - x, y), (x, 1 - y), (1 - x, 1 - y)]
        mine = 2 * x + y
        for a in range(n):
            ref = g_refs[a].at[mine, _quarter_halves(shapes, a, c), :]
            for p in range(3):
                pltpu.make_async_remote_copy(
                    src_ref=ref, dst_ref=ref, send_sem=send_sems.at[3 * a + p], recv_sem=recv_sems.at[3 * a + p],
                    device_id=(*chips[p], c), device_id_type=MESH).start()
        token[...] = jnp.zeros_like(token)

    out = pl.pallas_call(
        body, name="gather_start",
        out_shape=(pltpu.SemaphoreType.DMA((3 * n,)), pltpu.SemaphoreType.DMA((3 * n,)),
                   *[pltpu.HBM(b.shape, b.dtype) for b in placed], jax.ShapeDtypeStruct((8, LANES), F32)),
        in_specs=[_HBM] * n + [_ANY] * m,
        out_specs=(_SEM, _SEM, *[_HBM] * n, pl.BlockSpec(memory_space=pltpu.VMEM)),
        input_output_aliases={a: 2 + a for a in range(n)},
        compiler_params=pltpu.CompilerParams(has_side_effects=_EFFECT),
    )(*[pltpu.with_memory_space_constraint(b, pltpu.HBM) for b in placed], *after)
    return out[0], out[1], list(out[2:2 + n]), out[2 + n]


def _gather_wait(send_sems, recv_sems, thru, after):
    n = len(thru)
    shapes = [b.shape[1:] for b in thru]

    def body(*refs):
        g_refs = refs[:n]
        send_sems, recv_sems = refs[n], refs[n + 1]
        x, y, c = _position()
        chips = [(1 - x, y), (x, 1 - y), (1 - x, 1 - y)]
        mine = 2 * x + y
        for a in range(n):
            rows = _quarter_halves(shapes, a, c)
            for p, (cx, cy) in enumerate(chips):
                copy = pltpu.make_async_remote_copy(
                    src_ref=g_refs[a].at[mine, rows, :], dst_ref=g_refs[a].at[2 * cx + cy, rows, :],
                    send_sem=send_sems.at[3 * a + p], recv_sem=recv_sems.at[3 * a + p],
                    device_id=(cx, cy, c), device_id_type=MESH)
                copy.wait_send()
                copy.wait_recv()

    return pl.pallas_call(
        body, name="gather_wait",
        out_shape=[pltpu.HBM(b.shape, b.dtype) for b in thru],
        in_specs=[_HBM] * n + [_SEM, _SEM, _ANY], out_specs=[_HBM] * n,
        input_output_aliases={a: a for a in range(n)},
        compiler_params=pltpu.CompilerParams(has_side_effects=_EFFECT),
    )(*thru, send_sems, recv_sems, after)


def _gather_forward(bufs):
    n = len(bufs)
    shapes = [b.shape[1:] for b in bufs]

    def body(*refs):
        g_refs = refs[n:2 * n]
        send_sems, recv_sems = refs[2 * n:]
        x, y, c = _position()
        chips = [(1 - x, y), (x, 1 - y), (1 - x, 1 - y)]

        def over_d2d(a, p, which):
            cx, cy = chips[p]
            ref = g_refs[a].at[2 * cx + cy, _quarter_halves(shapes, a, which), :]
            return pltpu.make_async_remote_copy(
                src_ref=ref, dst_ref=ref, send_sem=send_sems.at[3 * a + p], recv_sem=recv_sems.at[3 * a + p],
                device_id=(x, y, 1 - c), device_id_type=MESH)

        sends = [over_d2d(a, p, c) for a in range(n) for p in range(3)]
        for cp in sends:
            cp.start()
        for a in range(n):
            for p in range(3):
                over_d2d(a, p, 1 - c).wait_recv()
        for cp in sends:
            cp.wait_send()

    return pl.pallas_call(
        body, name="gather_forward",
        out_shape=[jax.ShapeDtypeStruct(b.shape, BF16) for b in bufs],
        in_specs=[_ANY] * n, out_specs=[_ANY] * n,
        input_output_aliases={a: a for a in range(n)},
        scratch_shapes=[pltpu.SemaphoreType.DMA((3 * n,)), pltpu.SemaphoreType.DMA((3 * n,))],
    )(*bufs)


def _sibling_exchange(grads, tag):
    n = len(grads)
    shapes = [g.shape for g in grads]

    def body(*refs):
        g_refs, x_refs = refs[:n], refs[n:2 * n]
        send_sems, recv_sems = refs[2 * n:]
        x, y, c = _position()
        copies = []
        for a in range(n):
            hr = shapes[a][1] // 2
            cp = pltpu.make_async_remote_copy(
                src_ref=g_refs[a].at[:, pl.ds((1 - c) * hr, hr), :], dst_ref=x_refs[a],
                send_sem=send_sems.at[a], recv_sem=recv_sems.at[a],
                device_id=(x, y, 1 - c), device_id_type=MESH)
            cp.start()
            copies.append(cp)
        for cp in copies:
            cp.wait()

    return pl.pallas_call(
        body, name="grad_sibling_exchange_" + tag,
        out_shape=[jax.ShapeDtypeStruct((s[0], s[1] // 2, s[2]), F32) for s in shapes],
        in_specs=[_ANY] * n, out_specs=[_ANY] * n,
        scratch_shapes=[pltpu.SemaphoreType.DMA((n,)), pltpu.SemaphoreType.DMA((n,))],
    )(*grads)


def _chip_sums(core, grads, theirs, tag):
    n = len(grads)

    def body(core_ref, *refs):
        g_refs, t_refs, o_refs = refs[:n], refs[n:2 * n], refs[2 * n:]
        for g_ref, t_ref, o_ref in zip(g_refs, t_refs, o_refs):
            o_ref[...] = (g_ref[...] + t_ref[...]).astype(BF16)

    in_specs = [pl.BlockSpec((1, g.shape[1] // 2, g.shape[2]), lambda k, core_ref: (k, core_ref[0], 0)) for g in grads]
    in_specs += [pl.BlockSpec((1,) + t.shape[1:], lambda k, core_ref: (k, 0, 0)) for t in theirs]
    return pl.pallas_call(
        body, name="grad_chip_sums_" + tag,
        grid_spec=pltpu.PrefetchScalarGridSpec(
            num_scalar_prefetch=1, grid=(N_CHIPS,), in_specs=in_specs,
            out_specs=[pl.BlockSpec((1,) + t.shape[1:], lambda k, core_ref: (k, 0, 0)) for t in theirs]),
        out_shape=[jax.ShapeDtypeStruct(t.shape, BF16) for t in theirs],
        compiler_params=_params(("arbitrary",)),
    )(core, *grads, *theirs)


def _chip_exchange(sums):
    n = len(sums)

    def body(*refs):
        s_refs, y_refs = refs[:n], refs[n:2 * n]
        send_sems, recv_sems = refs[2 * n:]
        x, y, c = _position()
        chips = [(1 - x, y), (x, 1 - y), (1 - x, 1 - y)]
        copies = []
        for a in range(n):
            for p, (cx, cy) in enumerate(chips):
                cp = pltpu.make_async_remote_copy(
                    src_ref=s_refs[a].at[2 * cx + cy], dst_ref=y_refs[a].at[p],
                    send_sem=send_sems.at[3 * a + p], recv_sem=recv_sems.at[3 * a + p],
                    device_id=(cx, cy, c), device_id_type=MESH)
                cp.start()
                copies.append(cp)
        for cp in copies:
            cp.wait()

    return pl.pallas_call(
        body, name="grad_chip_exchange",
        out_shape=[jax.ShapeDtypeStruct((3,) + s.shape[1:], BF16) for s in sums],
        in_specs=[_ANY] * n, out_specs=[_ANY] * n,
        scratch_shapes=[pltpu.SemaphoreType.DMA((3 * n,)), pltpu.SemaphoreType.DMA((3 * n,))],
    )(*sums)


def _chip_exchange_start(sums):
    n = len(sums)
    lands = [lax.empty((3,) + s.shape[1:], BF16) for s in sums]

    def body(*refs):
        s_refs, y_refs = refs[:n], refs[n:2 * n]
        send_sems, recv_sems = refs[2 * n], refs[2 * n + 1]
        token = refs[4 * n + 2]
        x, y, c = _position()
        chips = [(1 - x, y), (x, 1 - y), (1 - x, 1 - y)]
        for a in range(n):
            for p, (cx, cy) in enumerate(chips):
                pltpu.make_async_remote_copy(
                    src_ref=s_refs[a].at[2 * cx + cy], dst_ref=y_refs[a].at[p],
                    send_sem=send_sems.at[3 * a + p], recv_sem=recv_sems.at[3 * a + p],
                    device_id=(cx, cy, c), device_id_type=MESH).start()
        token[...] = jnp.zeros_like(token)

    both = list(sums) + lands
    out = pl.pallas_call(
        body, name="grad_chip_exchange_start",
        out_shape=(pltpu.SemaphoreType.DMA((3 * n,)), pltpu.SemaphoreType.DMA((3 * n,)),
                   *[pltpu.HBM(b.shape, b.dtype) for b in both], jax.ShapeDtypeStruct((8, LANES), F32)),
        in_specs=[_HBM] * (2 * n),
        out_specs=(_SEM, _SEM, *[_HBM] * (2 * n), pl.BlockSpec(memory_space=pltpu.VMEM)),
        input_output_aliases={a: 2 + a for a in range(2 * n)},
        compiler_params=pltpu.CompilerParams(has_side_effects=_EFFECT),
    )(*[pltpu.with_memory_space_constraint(b, pltpu.HBM) for b in both])
    return out[0], out[1], list(out[2:2 + n]), list(out[2 + n:2 + 2 * n]), out[2 + 2 * n]


def _chip_exchange_wait(send_sems, recv_sems, sums, lands, after):
    n = len(sums)

    def body(*refs):
        s_refs, y_refs = refs[:n], refs[n:2 * n]
        send_sems, recv_sems = refs[2 * n], refs[2 * n + 1]
        x, y, c = _position()
        chips = [(1 - x, y), (x, 1 - y), (1 - x, 1 - y)]
        for a in range(n):
            for p, (cx, cy) in enumerate(chips):
                copy = pltpu.make_async_remote_copy(
                    src_ref=s_refs[a].at[2 * cx + cy], dst_ref=y_refs[a].at[p],
                    send_sem=send_sems.at[3 * a + p], recv_sem=recv_sems.at[3 * a + p],
                    device_id=(cx, cy, c), device_id_type=MESH)
                copy.wait_send()
                copy.wait_recv()

    both = list(sums) + list(lands)
    out = pl.pallas_call(
        body, name="grad_chip_exchange_wait",
        out_shape=[pltpu.HBM(b.shape, b.dtype) for b in both],
        in_specs=[_HBM] * (2 * n) + [_SEM, _SEM, _ANY], out_specs=[_HBM] * (2 * n),
        input_output_aliases={a: a for a in range(2 * n)},
        compiler_params=pltpu.CompilerParams(has_side_effects=_EFFECT),
    )(*both, send_sems, recv_sems, after)
    return list(out[:n]), list(out[n:])


def _total_sums(place, sums, parts):
    n = len(parts)
    steps = 2

    def body(place_ref, *refs):
        for s_ref, y_ref, o_ref in zip(refs[:n], refs[n:2 * n], refs[2 * n:]):
            o_ref[0] = ((s_ref[0].astype(F32) + y_ref[0].astype(F32)) + y_ref[1].astype(F32)) + y_ref[2].astype(F32)

    def step_rows(pt):
        return pt.shape[1] // steps

    in_specs = [pl.BlockSpec((1, step_rows(s), s.shape[2]), lambda r, place_ref: (place_ref[0], r, 0)) for s in sums]
    in_specs += [pl.BlockSpec((3, step_rows(pt), pt.shape[2]), lambda r, place_ref: (0, r, 0)) for pt in parts]
    return pl.pallas_call(
        body, name="grad_total_sums",
        grid_spec=pltpu.PrefetchScalarGridSpec(
            num_scalar_prefetch=1, grid=(steps,), in_specs=in_specs,
            out_specs=[pl.BlockSpec((1, step_rows(pt), pt.shape[2]), lambda r, place_ref: (place_ref[1], r, 0))
                       for pt in parts]),
        out_shape=[jax.ShapeDtypeStruct((2,) + pt.shape[1:], F32) for pt in parts],
        compiler_params=_params(("arbitrary",)),
    )(place, *sums, *parts)


def _sibling_share(halves):
    n = len(halves)

    def body(*refs):
        f_refs = refs[n:2 * n]
        send_sems, recv_sems = refs[2 * n:]
        x, y, c = _position()
        copies = []
        for a in range(n):
            cp = pltpu.make_async_remote_copy(
                src_ref=f_refs[a].at[c], dst_ref=f_refs[a].at[c], send_sem=send_sems.at[a], recv_sem=recv_sems.at[a],
                device_id=(x, y, 1 - c), device_id_type=MESH)
            cp.start()
            copies.append(cp)
        for a, cp in enumerate(copies):
            cp.wait_send()
            pltpu.make_async_remote_copy(
                src_ref=f_refs[a].at[1 - c], dst_ref=f_refs[a].at[1 - c], send_sem=send_sems.at[a],
                recv_sem=recv_sems.at[a], device_id=(x, y, c), device_id_type=MESH).wait_recv()

    return pl.pallas_call(
        body, name="grad_sibling_share",
        out_shape=[jax.ShapeDtypeStruct(h.shape, F32) for h in halves],
        in_specs=[_ANY] * n, out_specs=[_ANY] * n,
        input_output_aliases={a: a for a in range(n)},
        scratch_shapes=[pltpu.SemaphoreType.DMA((n,)), pltpu.SemaphoreType.DMA((n,))],
    )(*halves)


def _group_sum(stacked, nrow, name):
    total, n = stacked.shape
    groups = total // nrow

    def body(g_ref, o_ref):
        acc = g_ref[0:nrow, :]
        for grp in range(1, groups):
            acc = acc + g_ref[grp * nrow:(grp + 1) * nrow, :]
        o_ref[...] = acc

    return pl.pallas_call(
        body, name=name,
        out_shape=jax.ShapeDtypeStruct((nrow, n), F32),
        compiler_params=pltpu.CompilerParams(vmem_limit_bytes=VMEM_LIMIT),
    )(stacked)


def _local_step(xt, tgt, mod, gains, w_pool, pool_scale, w_in, later_weights, on_ffn_grads, seq):
    g_mpre, g_mpost, g_fpre, g_fpost = gains
    d = xt.shape[1]
    tm, tq = min(TOKEN_TILE, seq), min(ATTN_TILE, seq)

    h1, qn, k, v, u, kt, vt = _prenorm_proj(xt, mod, g_mpre, w_in, seq, tm)
    tk = min(ATTN_KEY_TILE, tq // 2)
    o, ltot = _attn_fwd(qn, k, vt, seq, tq, tk)
    w_out, w_g, w_u, w_d = later_weights(o)
    w_out2 = w_out.reshape(d, d)
    pooled, mixin, mix, x1, h2 =_mixer_post(u, o, xt, mod, g_mpost, g_fpre, w_pool, pool_scale, w_out2, seq, tm)
    a, b, fin, dy, df, loss_blk, accb4, accg4 = _ffn_fwd(h2, w_g, w_u, w_d, x1, tgt, mod, g_fpost, seq, tm)
    da, db, dx1, dmix, accb5, accg5 = _ffn_bwd(df, a, b, w_d, w_g, w_u, x1, dy, mix, mod, g_fpre, g_mpost, seq, tm)
    g_g, g_u = _tn_matmul(h2, [da, db], w_g.shape[0], tm, "grad_w_gate_up")
    (g_d,) = _tn_matmul(fin, [df], w_d.shape[0], tm, "grad_w_down")
    token = on_ffn_grads([g_g, g_u, g_d])
    do, dpd, dps, dwp = _mixer_bwd(dmix, w_out2, pooled, w_pool, pool_scale + token, seq, tm)
    dq, dk, dv = _attn_bwd(qn, k, kt, v, do, ltot, seq, tq, tk)
    gx, du, accb8, accg8 = _inproj_bwd(dq, dk, dv, dpd, xt, dx1, mod, g_mpre, w_in, seq, tm)

    g_in = jnp.concatenate(_tn_matmul(h1, [dq, dk, dv, du], 1, tm, "grad_w_in"), axis=0)
    g_out = _tn_matmul(mixin, [dmix], 1, tm, "grad_w_out")[0].reshape(w_out.shape)

    dmod = jnp.stack([accb8[:, 0], accb8[:, 1], accb5[:, 2], accb5[:, 0], accb5[:, 1], accb4[:, 0]], axis=1)
    dgain = jnp.stack([accg8[0], accg5[1], accg5[0], accg4[0]], axis=0)
    return loss_blk, gx, [g_in, g_out, g_g, g_u, g_d], dmod, dgain, dps[0:1], dwp


def kernel(x, c, w_cond, b_cond, g_mix_pre, g_mix_post, w_in, w_pool, pool_scale, w_out, g_ffn_pre, g_ffn_post, w_gate, w_up, w_down, loss_target, m_w_cond, m_b_cond, m_g_mix_pre, m_g_mix_post, m_w_in, m_w_pool, m_pool_scale, m_w_out, m_g_ffn_pre, m_g_ffn_post, m_w_gate, m_w_up, m_w_down, v_w_cond, v_b_cond, v_g_mix_pre, v_g_mix_post, v_w_in, v_w_pool, v_pool_scale, v_w_out, v_g_ffn_pre, v_g_ffn_post, v_w_gate, v_w_up, v_w_down):
    xi, yi, ci = _position()
    chip = 2 * xi + yi
    dev = 4 * xi + 2 * yi + ci
    nb, seq, d = x.shape
    t_all = nb * seq
    xt = x.reshape(t_all, d)
    tgt = loss_target.reshape(t_all, d)
    ncol = w_cond.shape[2]
    pw = pool_scale.shape[1]

    c_pad = jnp.concatenate([c, jnp.zeros((8 - nb, d), F32)], axis=0)
    c_all = _all_gather(c_pad, "gather_c").reshape(N_DEV, 8, d)[:, :nb].reshape(N_DEV * nb, d)
    b_q = lax.dynamic_slice(b_cond, (0, chip * ncol), (1, ncol))
    sc_all, mod_q = _cond_fwd(c_all, w_cond[0], b_q, 512)
    mod_parts = _all_gather(mod_q, "gather_mod").reshape(N_DEV, N_DEV * nb, ncol)
    mod_rows = lax.dynamic_slice(mod_parts, (0, dev * nb, 0), (N_DEV, nb, ncol))[0::2]
    mod = jnp.transpose(mod_rows, (1, 0, 2)).reshape(nb, N_MOD, d)
    mod = jnp.concatenate([mod, jnp.zeros((nb, MOD_ROWS - N_MOD, d), F32)], axis=1)

    place = jnp.stack([chip, ci]).astype(jnp.int32)
    placed = _place_quarters(place, [w[0] for w in (w_in, w_out, w_gate, w_up, w_down)])
    (w_in_all,) = _gather_weights(placed[:1])
    send_sems, recv_sems, in_flight, token = _gather_start(placed[1:], [mod, w_in_all])
    mod = mod + token[0:1, 0:1]

    def later_weights(after):
        return _gather_forward(_gather_wait(send_sems, recv_sems, in_flight, after))

    ffn_split = []

    def on_ffn_grads(ffn_grads):
        theirs = _sibling_exchange(ffn_grads, "ffn")
        ffn_split.extend(_chip_exchange_start(_chip_sums(place[1:], ffn_grads, theirs, "ffn")))
        return ffn_split[4][0:1, 0:1]

    gains = (g_mix_pre, g_mix_post, g_ffn_pre, g_ffn_post)
    loss_blk, gx, grads, dmod, dgain, dps, dwp = _local_step(
        xt, tgt, mod, gains, w_pool[0], pool_scale, w_in_all, later_weights, on_ffn_grads, seq)
    loss = lax.psum(loss_blk[0, 0], ("x", "y", "c"))

    sums_ffn, parts_ffn = _chip_exchange_wait(*ffn_split[:4], gx)
    theirs = _sibling_exchange(grads[:2], "mix")
    sums_mix = _chip_sums(place[1:], grads[:2], theirs, "mix")
    parts_mix = _chip_exchange(sums_mix)
    halves = _total_sums(place, list(sums_mix) + list(sums_ffn), list(parts_mix) + list(parts_ffn))
    g_big = [g.reshape(2 * g.shape[1], g.shape[2]) for g in _sibling_share(halves)]

    wp_rows = dwp.size // d
    pad_rows = 24 - (2 * N_MOD + 4 + 1)
    payload = jnp.concatenate([
        dmod.reshape(nb * N_MOD, d), dgain,
        jnp.concatenate([dps, jnp.zeros((1, d - pw), F32)], axis=1),
        jnp.zeros((pad_rows, d), F32), dwp.reshape(wp_rows, d)], axis=0)
    prow = payload.shape[0]
    gathered = _all_gather(payload, "gather_small")
    summed = _group_sum(gathered, prow, "small_device_sum")
    dmod_all = gathered.reshape(N_DEV, prow, d)[:, :nb * N_MOD].reshape(N_DEV * nb, N_MOD * d)
    g_b_cond = _group_sum(dmod_all, 1, "grad_b_cond")
    dmod_q = lax.dynamic_slice(dmod_all, (0, chip * ncol), (N_DEV * nb, ncol))
    g_w_cond = _cond_bwd(sc_all, dmod_q, 512)
    first_gain = 2 * N_MOD
    g_gains = [summed[first_gain + r:first_gain + r + 1] for r in range(4)]
    g_pool_scale = summed[first_gain + 4:first_gain + 5, :pw]
    g_w_pool = summed[24:24 + wp_rows].reshape(w_pool.shape[1] * w_pool.shape[2], w_pool.shape[3])

    flat_pool = lambda t: t.reshape(g_w_pool.shape)
    plan = [
        ("w_cond", w_cond[0], g_w_cond, m_w_cond[0], v_w_cond[0], w_cond.shape),
        ("b_cond", b_cond, g_b_cond, m_b_cond, v_b_cond, b_cond.shape),
        ("g_mix_pre", g_mix_pre, g_gains[0], m_g_mix_pre, v_g_mix_pre, g_mix_pre.shape),
        ("g_mix_post", g_mix_post, g_gains[1], m_g_mix_post, v_g_mix_post, g_mix_post.shape),
        ("w_in", w_in[0], g_big[0], m_w_in[0], v_w_in[0], w_in.shape),
        ("w_pool", flat_pool(w_pool), g_w_pool, flat_pool(m_w_pool), flat_pool(v_w_pool), w_pool.shape),
        ("pool_scale", pool_scale, g_pool_scale, m_pool_scale, v_pool_scale, pool_scale.shape),
        ("w_out", w_out[0], g_big[1], m_w_out[0], v_w_out[0], w_out.shape),
        ("g_ffn_pre", g_ffn_pre, g_gains[2], m_g_ffn_pre, v_g_ffn_pre, g_ffn_pre.shape),
        ("g_ffn_post", g_ffn_post, g_gains[3], m_g_ffn_post, v_g_ffn_post, g_ffn_post.shape),
        ("w_gate", w_gate[0], g_big[2], m_w_gate[0], v_w_gate[0], w_gate.shape),
        ("w_up", w_up[0], g_big[3], m_w_up[0], v_w_up[0], w_up.shape),
        ("w_down", w_down[0], g_big[4], m_w_down[0], v_w_down[0], w_down.shape),
    ]
    out_g, out_d, out_m, out_v = [], [], [], []
    for name, w2, g2, m2, v2, shape in plan:
        delta, new_m, new_v = _adamw(w2, g2, m2, v2, "adamw_" + name)
        out_g.append(g2.reshape(shape))
        out_d.append(delta.reshape(shape))
        out_m.append(new_m.reshape(shape))
        out_v.append(new_v.reshape(shape))
    return (loss, gx.reshape(x.shape), *out_g, *out_d, *out_m, *out_v)
```

```python
import functools

import jax
import jax.numpy as jnp
from jax import lax
from jax.experimental import pallas as pl
from jax.experimental.pallas import tpu as pltpu

F32 = jnp.float32
BF16 = jnp.bfloat16
MESH = pl.DeviceIdType.MESH

EPS = 1e-6
HEAD_DIM = 64
HEADS_PER_BLOCK = 2
LANES = 128
NEG_QK_SCALE = -0.125
POOL_WINDOWS = (2, 4, 8, 16)
POOL_GROUP = 128
HALO = 16
N_MOD = 6
MOD_ROWS = 8
N_CHIPS = 4
N_DEV = 8
VMEM_LIMIT = 56 * 1024 * 1024

ADAM_LR = 0.001
ADAM_B1 = 0.9
ADAM_B2 = 0.999
ADAM_EPS = 1e-08
ADAM_WD = 0.01
ADAM_STEP = 10

TOKEN_TILE = 512
GRAD_TOKEN_TILE = 2048
ATTN_TILE = 512
ATTN_KEY_TILE = 256
ATTN_ROW_CHUNK = 32


def _dot(a, b):
    return jnp.dot(a, b, preferred_element_type=F32)


def _dot_nt(a, b):
    return lax.dot_general(a, b, (((1,), (1,)), ((), ())), preferred_element_type=F32)


def _dot_tn(a, b):
    return lax.dot_general(a, b, (((0,), (0,)), ((), ())), preferred_element_type=F32)


def _split(v):
    hi = v.astype(BF16)
    lo = (v - hi.astype(F32)).astype(BF16)
    return hi, lo


def _rms(v):
    return lax.rsqrt(jnp.mean(v * v, axis=-1, keepdims=True) + EPS)


def _norm_bwd(dn, n, r):
    return r * (dn - n * jnp.mean(dn * n, axis=-1, keepdims=True))


def _sigmoid(v):
    return 0.5 * jnp.tanh(0.5 * v) + 0.5


def _colsum(v):
    return jnp.sum(v, axis=0, keepdims=True)


def _params(sem=None):
    return pltpu.CompilerParams(dimension_semantics=sem, vmem_limit_bytes=VMEM_LIMIT)


def _position():
    return lax.axis_index("x"), lax.axis_index("y"), lax.axis_index("c")


def _prenorm_proj(x, mod, g_pre, w_in, seq, tm):
    t_all, d = x.shape
    nt = seq // tm
    p = w_in.shape[2]

    def body(x_ref, mod_ref, g_ref, w_ref, h_ref, q_ref, k_ref, v_ref, u_ref, kt_ref, vt_ref):
        xf = x_ref[...]
        n = xf * _rms(xf)
        h = (n * g_ref[...]) * (1.0 + mod_ref[0, 1:2, :]) + mod_ref[0, 0:1, :]
        hb = h.astype(BF16)
        h_ref[...] = hb
        q_ref[...] = (_dot(hb, w_ref[0]) * NEG_QK_SCALE).astype(BF16)
        kf = _dot(hb, w_ref[1])
        vf = _dot(hb, w_ref[2])
        k_ref[...] = kf.astype(BF16)
        v_ref[...] = vf.astype(BF16)
        kt_ref[...] = kf.T.astype(BF16)
        vt_ref[...] = vf.T.astype(BF16)
        u_ref[...] = _dot(hb, w_ref[3])

    tok = lambda i: (i, 0)
    tok_t = lambda i: (0, i)
    return pl.pallas_call(
        body, name="prenorm_proj", grid=(t_all // tm,),
        in_specs=[pl.BlockSpec((tm, d), tok),
                  pl.BlockSpec((1, MOD_ROWS, d), lambda i: (i // nt, 0, 0)),
                  pl.BlockSpec((1, d), lambda i: (0, 0)),
                  pl.BlockSpec((N_CHIPS, d, p), lambda i: (0, 0, 0))],
        out_specs=[pl.BlockSpec((tm, d), tok)] + [pl.BlockSpec((tm, p), tok)] * 4 + [pl.BlockSpec((p, tm), tok_t)] * 2,
        out_shape=[jax.ShapeDtypeStruct((t_all, d), BF16)] + [jax.ShapeDtypeStruct((t_all, p), BF16)] * 3
        + [jax.ShapeDtypeStruct((t_all, p), F32)] + [jax.ShapeDtypeStruct((p, t_all), BF16)] * 2,
        compiler_params=_params(("arbitrary",)),
    )(x, mod, g_pre, w_in)


def _tri_matrix(tk, kind):
    j = lax.broadcasted_iota(jnp.int32, (2 * tk, tk), 0) % tk
    s = lax.broadcasted_iota(jnp.int32, (2 * tk, tk), 1)
    return {"after": j > s, "upto": j <= s, "before": j < s}[kind].astype(BF16)


def _row_sums(v):
    return jnp.broadcast_to(jnp.sum(v, axis=-1, keepdims=True), (v.shape[0], LANES))


def _across(v, n):
    return jnp.concatenate([v] * (n // LANES), axis=1)


def _all_masked(c, diag, rc, tk):
    return diag is not None and diag * tk >= (c + 1) * rc - 1


def _some_masked(c, diag, rc, tk):
    return diag is not None and diag * tk + tk - 1 >= c * rc


def _attn_fwd(qn, k, vt, seq, tq, tk):
    t_all, w = qn.shape
    nb, nq, ndiag = t_all // seq, seq // tq, tq // tk
    assert ndiag % 2 == 0, "two key blocks per loop trip"
    rc = ATTN_ROW_CHUNK
    heads = range(HEADS_PER_BLOCK)

    def body(q_ref, k_ref, vt_ref, tri_ref, o_ref, l_ref,
             z_buf, ls_buf, hl_buf, aft_buf, w_buf, tot_buf, acc_t, run_buf):
        i = pl.program_id(2)
        nblk = (i + 1) * ndiag
        lane = lax.broadcasted_iota(jnp.int32, (1, LANES), 1)
        sub = lax.broadcasted_iota(jnp.int32, (LANES, 1), 0)
        row = lax.broadcasted_iota(jnp.int32, (rc, tk), 0)
        col = lax.broadcasted_iota(jnp.int32, (rc, tk), 1)
        first = lane < HEAD_DIM
        q2 = q_ref[...]
        qs = [jnp.where(first, q2, jnp.zeros_like(q2)), jnp.where(first, jnp.zeros_like(q2), q2)]
        acc_t[...] = jnp.zeros_like(acc_t)
        run_buf[...] = jnp.zeros_like(run_buf)
        w_buf[1] = jnp.zeros((HEADS_PER_BLOCK, tq, tk), BF16)

        def causal(c, diag):
            return (col + diag * tk) < (row + c * rc)

        def scores(blk, slot):
            kj = k_ref[pl.ds(pl.multiple_of(blk * tk, tk), tk), :]
            for h in heads:
                z_buf[slot, h] = _dot_nt(qs[h], kj)

        def values(blk, slot):
            vtj = vt_ref[:, pl.ds(pl.multiple_of(blk * tk, tk), tk)]
            zero = jnp.zeros_like(vtj)
            acc_t[...] += (_dot_nt(jnp.where(sub < HEAD_DIM, vtj, zero), w_buf[slot, 0])
                           + _dot_nt(jnp.where(sub < HEAD_DIM, zero, vtj), w_buf[slot, 1]))

        def softplus_stage(h, slot, diag):
            for c in range(tq // rc):
                rows = slice(c * rc, (c + 1) * rc)
                if _all_masked(c, diag, rc, tk):
                    hl_buf[h, rows, :] = jnp.zeros((rc, 2 * tk), BF16)
                    tot_buf[h, rows, :] = jnp.zeros((rc, LANES), F32)
                    continue
                nz = z_buf[slot, h, rows, :]
                l1 = jnp.minimum(nz, 0.0) - jnp.log(1.0 + jnp.exp(-jnp.abs(nz)))
                if _some_masked(c, diag, rc, tk):
                    l1 = jnp.where(causal(c, diag), l1, 0.0)
                hi, lo = _split(l1)
                hl_buf[h, rows, 0:tk] = hi
                hl_buf[h, rows, tk:2 * tk] = lo
                ls_buf[h, rows, :] = l1 - nz
                tot_buf[h, rows, :] = _row_sums(l1)

        def weights_stage(h, slot, diag):
            for c in range(tq // rc):
                rows = slice(c * rc, (c + 1) * rc)
                if _all_masked(c, diag, rc, tk):
                    w_buf[slot, h, rows, :] = jnp.zeros((rc, tk), BF16)
                    continue
                wgt = jnp.exp((ls_buf[h, rows, :] + aft_buf[h, rows, :]) + _across(run_buf[h, rows, :], tk))
                if _some_masked(c, diag, rc, tk):
                    wgt = jnp.where(causal(c, diag), wgt, 0.0)
                w_buf[slot, h, rows, :] = wgt.astype(BF16)
                run_buf[h, rows, :] += tot_buf[h, rows, :]

        def position(blk, slot, diag):
            scores(jnp.maximum(blk - 1, 0), 1 - slot)
            for h in heads:
                softplus_stage(h, slot, diag)
                aft_buf[h] = _dot(hl_buf[h], tri_ref[...])
            values(jnp.minimum(blk + 1, nblk - 1), 1 - slot)
            for h in heads:
                weights_stage(h, slot, diag)

        scores(nblk - 1, 0)
        for p in range(ndiag):
            position(nblk - 1 - p, p % 2, ndiag - 1 - p)

        def trip(jj, carry):
            for u in range(2):
                position(i * ndiag - 1 - 2 * jj - u, u, None)
            return carry

        lax.fori_loop(0, (i * ndiag) // 2, trip, 0)
        values(0, 1)
        o_ref[...] = acc_t[...].T.astype(BF16)
        l_ref[...] = jnp.where(first, run_buf[0], run_buf[1])

    qmap = lambda b, hp, i: (b * nq + i, hp)
    nh = HEADS_PER_BLOCK
    return pl.pallas_call(
        body, name="attn_fwd", grid=(nb, w // LANES, nq),
        in_specs=[pl.BlockSpec((tq, LANES), qmap), pl.BlockSpec((seq, LANES), lambda b, hp, i: (b, hp)),
                  pl.BlockSpec((LANES, seq), lambda b, hp, i: (hp, b)),
                  pl.BlockSpec((2 * tk, tk), lambda b, hp, i: (0, 0))],
        out_specs=[pl.BlockSpec((tq, LANES), qmap), pl.BlockSpec((tq, LANES), qmap)],
        out_shape=[jax.ShapeDtypeStruct((t_all, w), BF16), jax.ShapeDtypeStruct((t_all, w), F32)],
        scratch_shapes=[pltpu.VMEM((2, nh, tq, tk), F32), pltpu.VMEM((nh, tq, tk), F32),
                        pltpu.VMEM((nh, tq, 2 * tk), BF16), pltpu.VMEM((nh, tq, tk), F32),
                        pltpu.VMEM((2, nh, tq, tk), BF16), pltpu.VMEM((nh, tq, LANES), F32),
                        pltpu.VMEM((LANES, tq), F32), pltpu.VMEM((nh, tq, LANES), F32)],
        compiler_params=_params(("arbitrary", "arbitrary", "arbitrary")),
    )(qn, k, vt, _tri_matrix(tk, "after"))


def _window_sums(ext, rows, offset, forward):
    r = lax.broadcasted_iota(jnp.int32, (rows, rows + HALO), 0)
    e = lax.broadcasted_iota(jnp.int32, (rows, rows + HALO), 1)
    hi, lo = _split(ext)
    out = []
    for g, win in enumerate(POOL_WINDOWS):
        if forward:
            band = (e >= r) & (e < r + win)
        else:
            band = (e <= r + offset) & (e > r + offset - win)
        bm = band.astype(BF16)
        cols = slice(g * POOL_GROUP, (g + 1) * POOL_GROUP)
        out.append(_dot(bm, hi[:, cols]) + _dot(bm, lo[:, cols]))
    return out


def _window_counts(pos):
    return [jnp.minimum(pos + 1, win).astype(F32) for win in POOL_WINDOWS]


def _mixer_post(u, o, x, mod, g_post, g_fpre, w_pool, pool_scale, w_out, seq, tm):
    t_all, d = x.shape
    nt = seq // tm
    p = u.shape[1]

    def body(u_ref, halo_ref, o_ref, x_ref, mod_ref, gp_ref, gf_ref, wp_ref, ps_ref, wo_ref,
             pooled_ref, mixin_ref, mix_ref, x1_ref, h2_ref):
        it = pl.program_id(0) % nt
        uf = u_ref[...]
        halo = jnp.where(it == 0, 0.0, halo_ref[...])
        ext = jnp.concatenate([halo, uf], axis=0)
        pos = it * tm + lax.broadcasted_iota(jnp.int32, (tm, 1), 0)
        sums = _window_sums(ext, tm, HALO, False)
        cnts = _window_counts(pos)
        pools = []
        for g in range(len(POOL_WINDOWS)):
            cols = slice(g * POOL_GROUP, (g + 1) * POOL_GROUP)
            pooled = (sums[g] / cnts[g] - uf[:, cols]).astype(BF16)
            pooled_ref[:, cols] = pooled
            yg = _dot(pooled, wp_ref[g].astype(BF16))
            pools.append((yg * ps_ref[:, cols]).astype(BF16))
        mixin = jnp.concatenate([o_ref[...]] + pools, axis=1)
        mixin_ref[...] = mixin
        mix = _dot(mixin, wo_ref[...])
        mix_ref[...] = mix
        n2 = mix * _rms(mix)
        x1 = x_ref[...] + mod_ref[0, 2:3, :] * (n2 * gp_ref[...])
        x1_ref[...] = x1
        n3 = x1 * _rms(x1)
        h2 = (n3 * gf_ref[...]) * (1.0 + mod_ref[0, 4:5, :]) + mod_ref[0, 3:4, :]
        h2_ref[...] = h2.astype(BF16)

    tok = lambda i: (i, 0)
    const2 = lambda i: (0, 0)
    hb = tm // HALO
    return pl.pallas_call(
        body, name="mixer_post", grid=(t_all // tm,),
        in_specs=[pl.BlockSpec((tm, p), tok),
                  pl.BlockSpec((HALO, p), lambda i: (jnp.maximum(i * hb - 1, 0), 0)),
                  pl.BlockSpec((tm, p), tok),
                  pl.BlockSpec((tm, d), tok),
                  pl.BlockSpec((1, MOD_ROWS, d), lambda i: (i // nt, 0, 0)),
                  pl.BlockSpec((1, d), const2), pl.BlockSpec((1, d), const2),
                  pl.BlockSpec(w_pool.shape, lambda i: (0, 0, 0)),
                  pl.BlockSpec((1, p), const2),
                  pl.BlockSpec((d, d), const2)],
        out_specs=[pl.BlockSpec((tm, p), tok), pl.BlockSpec((tm, d), tok), pl.BlockSpec((tm, d), tok),
                   pl.BlockSpec((tm, d), tok), pl.BlockSpec((tm, d), tok)],
        out_shape=[jax.ShapeDtypeStruct((t_all, p), BF16), jax.ShapeDtypeStruct((t_all, d), BF16),
                   jax.ShapeDtypeStruct((t_all, d), F32), jax.ShapeDtypeStruct((t_all, d), F32),
                   jax.ShapeDtypeStruct((t_all, d), BF16)],
        compiler_params=_params(("arbitrary",)),
    )(u, u, o, x, mod, g_post, g_fpre, w_pool, pool_scale, w_out)


def _ffn_fwd(h2, w_g, w_u, w_d, x1, tgt, mod, g_post, seq, tm):
    t_all, d = x1.shape
    nt = seq // tm
    nk, _, ff = w_g.shape

    def body(h_ref, wg_ref, wu_ref, wd_ref, x1_ref, t_ref, mod_ref, g_ref,
             a_ref, b_ref, fin_ref, dy_ref, df_ref, loss_ref, accb_ref, accg_ref, facc):
        i, k = pl.program_id(0), pl.program_id(1)
        hb = h_ref[...]
        a = _dot(hb, wg_ref[0])
        b = _dot(hb, wu_ref[0])
        a_ref[0] = a.astype(BF16)
        b_ref[0] = b.astype(BF16)
        fin = ((a * _sigmoid(a)) * b).astype(BF16)
        fin_ref[0] = fin
        part = _dot(fin, wd_ref[0])

        @pl.when(k == 0)
        def _():
            facc[...] = part

        @pl.when(k > 0)
        def _():
            facc[...] += part

        @pl.when(k == nk - 1)
        def _():
            f = facc[...]
            r4 = _rms(f)
            n4 = f * r4
            gate = mod_ref[0, 5:6, :]
            g = g_ref[...]
            err = (x1_ref[...] + gate * (n4 * g)) - t_ref[...]
            dy = err * (1.0 / d)
            dy_ref[...] = dy

            @pl.when(i == 0)
            def _():
                loss_ref[...] = jnp.zeros_like(loss_ref)
                accg_ref[...] = jnp.zeros_like(accg_ref)

            @pl.when(i % nt == 0)
            def _():
                accb_ref[...] = jnp.zeros_like(accb_ref)

            loss_ref[...] += (0.5 / d) * jnp.sum(err * err)
            accb_ref[0, 0:1, :] += _colsum(dy * (n4 * g))
            accg_ref[0:1, :] += _colsum((dy * gate) * n4)
            dn4 = (dy * gate) * g
            df_ref[...] = _norm_bwd(dn4, n4, r4).astype(BF16)

    tok = lambda i, k: (i, 0)
    ktok = lambda i, k: (k, i, 0)
    kw = lambda i, k: (k, 0, 0)
    const2 = lambda i, k: (0, 0)
    return pl.pallas_call(
        body, name="ffn_fwd", grid=(t_all // tm, nk),
        in_specs=[pl.BlockSpec((tm, d), tok),
                  pl.BlockSpec((1, d, ff), kw), pl.BlockSpec((1, d, ff), kw), pl.BlockSpec((1, ff, d), kw),
                  pl.BlockSpec((tm, d), tok), pl.BlockSpec((tm, d), tok),
                  pl.BlockSpec((1, MOD_ROWS, d), lambda i, k: (i // nt, 0, 0)),
                  pl.BlockSpec((1, d), const2)],
        out_specs=[pl.BlockSpec((1, tm, ff), ktok)] * 3
        + [pl.BlockSpec((tm, d), tok), pl.BlockSpec((tm, d), tok),
           pl.BlockSpec((8, LANES), const2),
           pl.BlockSpec((1, 8, d), lambda i, k: (i // nt, 0, 0)),
           pl.BlockSpec((8, d), const2)],
        out_shape=[jax.ShapeDtypeStruct((nk, t_all, ff), BF16)] * 3
        + [jax.ShapeDtypeStruct((t_all, d), F32), jax.ShapeDtypeStruct((t_all, d), BF16),
           jax.ShapeDtypeStruct((8, LANES), F32),
           jax.ShapeDtypeStruct((t_all // seq, 8, d), F32),
           jax.ShapeDtypeStruct((8, d), F32)],
        scratch_shapes=[pltpu.VMEM((tm, d), F32)],
        compiler_params=_params(("arbitrary", "arbitrary")),
    )(h2, w_g, w_u, w_d, x1, tgt, mod, g_post)


def _ffn_bwd(df, a, b, w_d, w_g, w_u, x1, dy, mix, mod, g_fpre, g_mpost, seq, tm):
    t_all, d = x1.shape
    nt = seq // tm
    nk, _, ff = w_g.shape

    def body(df_ref, a_ref, b_ref, wd_ref, wg_ref, wu_ref, x1_ref, dy_ref, mix_ref, mod_ref, gf_ref, gm_ref,
             da_ref, db_ref, dx1_ref, dmix_ref, accb_ref, accg_ref, hacc):
        i, k = pl.program_id(0), pl.program_id(1)
        dfin = _dot_nt(df_ref[...], wd_ref[0])
        af = a_ref[0].astype(F32)
        bf = b_ref[0].astype(F32)
        sig = _sigmoid(af)
        da = ((dfin * bf) * (sig * (1.0 + af * (1.0 - sig)))).astype(BF16)
        db = (dfin * (af * sig)).astype(BF16)
        da_ref[0] = da
        db_ref[0] = db
        part = _dot_nt(da, wg_ref[0]) + _dot_nt(db, wu_ref[0])

        @pl.when(k == 0)
        def _():
            hacc[...] = part

        @pl.when(k > 0)
        def _():
            hacc[...] += part

        @pl.when(k == nk - 1)
        def _():
            @pl.when(i == 0)
            def _():
                accg_ref[...] = jnp.zeros_like(accg_ref)

            @pl.when(i % nt == 0)
            def _():
                accb_ref[...] = jnp.zeros_like(accb_ref)

            dh2 = hacc[...]
            x1 = x1_ref[...]
            r3 = _rms(x1)
            n3 = x1 * r3
            g3 = gf_ref[...]
            scale1 = 1.0 + mod_ref[0, 4:5, :]
            accb_ref[0, 0:1, :] += _colsum(dh2)
            accb_ref[0, 1:2, :] += _colsum(dh2 * (n3 * g3))
            accg_ref[0:1, :] += _colsum((dh2 * scale1) * n3)
            dx1 = dy_ref[...] + _norm_bwd((dh2 * scale1) * g3, n3, r3)
            dx1_ref[...] = dx1
            mix = mix_ref[...]
            r2 = _rms(mix)
            n2 = mix * r2
            g2 = gm_ref[...]
            gate = mod_ref[0, 2:3, :]
            accb_ref[0, 2:3, :] += _colsum(dx1 * (n2 * g2))
            accg_ref[1:2, :] += _colsum((dx1 * gate) * n2)
            dmix_ref[...] = _norm_bwd((dx1 * gate) * g2, n2, r2).astype(BF16)

    tok = lambda i, k: (i, 0)
    ktok = lambda i, k: (k, i, 0)
    kw = lambda i, k: (k, 0, 0)
    const2 = lambda i, k: (0, 0)
    return pl.pallas_call(
        body, name="ffn_bwd", grid=(t_all // tm, nk),
        in_specs=[pl.BlockSpec((tm, d), tok),
                  pl.BlockSpec((1, tm, ff), ktok), pl.BlockSpec((1, tm, ff), ktok),
                  pl.BlockSpec((1, ff, d), kw), pl.BlockSpec((1, d, ff), kw), pl.BlockSpec((1, d, ff), kw),
                  pl.BlockSpec((tm, d), tok), pl.BlockSpec((tm, d), tok), pl.BlockSpec((tm, d), tok),
                  pl.BlockSpec((1, MOD_ROWS, d), lambda i, k: (i // nt, 0, 0)),
                  pl.BlockSpec((1, d), const2), pl.BlockSpec((1, d), const2)],
        out_specs=[pl.BlockSpec((1, tm, ff), ktok)] * 2
        + [pl.BlockSpec((tm, d), tok), pl.BlockSpec((tm, d), tok),
           pl.BlockSpec((1, 8, d), lambda i, k: (i // nt, 0, 0)),
           pl.BlockSpec((8, d), const2)],
        out_shape=[jax.ShapeDtypeStruct((nk, t_all, ff), BF16)] * 2
        + [jax.ShapeDtypeStruct((t_all, d), F32), jax.ShapeDtypeStruct((t_all, d), BF16),
           jax.ShapeDtypeStruct((t_all // seq, 8, d), F32),
           jax.ShapeDtypeStruct((8, d), F32)],
        scratch_shapes=[pltpu.VMEM((tm, d), F32)],
        compiler_params=_params(("arbitrary", "arbitrary")),
    )(df, a, b, w_d, w_g, w_u, x1, dy, mix, mod, g_fpre, g_mpost)


def _mixer_bwd(dmix, w_out, pooled, w_pool, pool_scale, seq, tm):
    t_all, d = dmix.shape
    p = pooled.shape[1]
    ng = len(POOL_WINDOWS)

    def body(dm_ref, wo_ref, pooled_ref, wp_ref, ps_ref, do_ref, dpd_ref, dps_ref, dwp_ref):
        i = pl.program_id(0)

        @pl.when(i == 0)
        def _():
            dps_ref[...] = jnp.zeros_like(dps_ref)
            dwp_ref[...] = jnp.zeros_like(dwp_ref)

        dmixin = _dot_nt(dm_ref[...], wo_ref[...])
        do_ref[...] = dmixin[:, :p].astype(BF16)
        for g in range(ng):
            cols = slice(g * POOL_GROUP, (g + 1) * POOL_GROUP)
            dpool = dmixin[:, p + g * POOL_GROUP:p + (g + 1) * POOL_GROUP]
            pooled = pooled_ref[:, cols]
            wpg = wp_ref[g].astype(BF16)
            yg = _dot(pooled, wpg)
            dps_ref[0:1, cols] += _colsum(dpool * yg)
            dyg = (dpool * ps_ref[:, cols]).astype(BF16)
            dwp_ref[g] += _dot_tn(pooled, dyg)
            dpd_ref[:, cols] = _dot_nt(dyg, wpg)

    tok = lambda i: (i, 0)
    const2 = lambda i: (0, 0)
    const3 = lambda i: (0, 0, 0)
    return pl.pallas_call(
        body, name="mixer_bwd", grid=(t_all // tm,),
        in_specs=[pl.BlockSpec((tm, d), tok), pl.BlockSpec((d, d), const2), pl.BlockSpec((tm, p), tok),
                  pl.BlockSpec(w_pool.shape, const3), pl.BlockSpec((1, p), const2)],
        out_specs=[pl.BlockSpec((tm, p), tok), pl.BlockSpec((tm, p), tok),
                   pl.BlockSpec((8, p), const2), pl.BlockSpec(w_pool.shape, const3)],
        out_shape=[jax.ShapeDtypeStruct((t_all, p), BF16), jax.ShapeDtypeStruct((t_all, p), F32),
                   jax.ShapeDtypeStruct((8, p), F32), jax.ShapeDtypeStruct(w_pool.shape, F32)],
        compiler_params=_params(("arbitrary",)),
    )(dmix, w_out, pooled, w_pool, pool_scale)


def _attn_bwd(qn, k, kt, v, do, ltot, seq, tq, tk):
    t_all, w = qn.shape
    nb, nq, ndiag, nkb = t_all // seq, seq // tq, tq // tk, seq // tk
    assert ndiag % 2 == 0, "two key blocks per loop trip"
    rc = ATTN_ROW_CHUNK
    nh = HEADS_PER_BLOCK
    heads = range(nh)

    def body(q_ref, k_ref, kt_ref, v_ref, do_ref, l_ref, up_ref, bf_ref, dq_ref, dk_ref, dv_ref,
             z_buf, dw_buf, ls_buf, hl_buf, upto_buf, g_buf, gb_buf, before_buf, w_buf, dz_buf,
             totl_buf, totg_buf, rem_buf, preg_buf, qnt_buf, dot_buf, dq_t, dk_t, dv_t):
        i = pl.program_id(2)
        nblk = (i + 1) * ndiag

        @pl.when(i == 0)
        def _():
            dk_t[...] = jnp.zeros_like(dk_t)
            dv_t[...] = jnp.zeros_like(dv_t)

        lane = lax.broadcasted_iota(jnp.int32, (1, LANES), 1)
        sub = lax.broadcasted_iota(jnp.int32, (LANES, 1), 0)
        row = lax.broadcasted_iota(jnp.int32, (rc, tk), 0)
        col = lax.broadcasted_iota(jnp.int32, (rc, tk), 1)
        first = lane < HEAD_DIM
        upper = sub < HEAD_DIM
        q2 = q_ref[...]
        do2 = do_ref[...]
        l2 = l_ref[...]
        qs = [jnp.where(first, q2, jnp.zeros_like(q2)), jnp.where(first, jnp.zeros_like(q2), q2)]
        dos = [jnp.where(first, do2, jnp.zeros_like(do2)), jnp.where(first, jnp.zeros_like(do2), do2)]
        for src, dst in ((q2, qnt_buf), (do2, dot_buf)):
            t = src.astype(F32).T
            dst[:, 0:tq] = jnp.where(upper, t, 0.0).astype(BF16)
            dst[:, tq:2 * tq] = jnp.where(upper, 0.0, t).astype(BF16)
        for h in heads:
            rem_buf[h] = jnp.where(first if h == 0 else ~first, l2, pltpu.roll(l2, HEAD_DIM, 1))
        preg_buf[...] = jnp.zeros_like(preg_buf)
        dq_t[...] = jnp.zeros_like(dq_t)
        w_buf[1] = jnp.zeros((nh * tq, tk), BF16)
        dz_buf[1] = jnp.zeros((nh * tq, tk), BF16)

        def causal(c, diag):
            return (col + diag * tk) < (row + c * rc)

        def scores(blk, slot):
            off = pl.multiple_of(blk * tk, tk)
            kj = k_ref[pl.ds(off, tk), :]
            vj = v_ref[pl.ds(off, tk), :]
            for h in heads:
                z_buf[slot, h] = _dot_nt(qs[h], kj)
                dw_buf[slot, h] = _dot_nt(dos[h], vj)

        def gradients(blk, slot):
            off = pl.multiple_of(blk * tk, tk)
            ktj = kt_ref[:, pl.ds(off, tk)]
            zero = jnp.zeros_like(ktj)
            dq_t[...] += (_dot_nt(jnp.where(upper, ktj, zero), dz_buf[slot, 0:tq, :])
                          + _dot_nt(jnp.where(upper, zero, ktj), dz_buf[slot, tq:2 * tq, :]))
            dk_t[blk] += _dot(qnt_buf[...], dz_buf[slot])
            dv_t[blk] += _dot(dot_buf[...], w_buf[slot])

        def softplus_stage(h, slot, diag):
            for c in range(tq // rc):
                rows = slice(c * rc, (c + 1) * rc)
                if _all_masked(c, diag, rc, tk):
                    hl_buf[h, rows, :] = jnp.zeros((rc, 2 * tk), BF16)
                    continue
                nz = z_buf[slot, h, rows, :]
                l1 = jnp.minimum(nz, 0.0) - jnp.log(1.0 + jnp.exp(-jnp.abs(nz)))
                if _some_masked(c, diag, rc, tk):
                    l1 = jnp.where(causal(c, diag), l1, 0.0)
                hi, lo = _split(l1)
                hl_buf[h, rows, 0:tk] = hi
                hl_buf[h, rows, tk:2 * tk] = lo
                ls_buf[h, rows, :] = l1 - nz
                totl_buf[h, rows, :] = _row_sums(l1)

        def weights_stage(h, slot, diag):
            for c in range(tq // rc):
                rows = slice(c * rc, (c + 1) * rc)
                stacked = slice(h * tq + c * rc, h * tq + (c + 1) * rc)
                if _all_masked(c, diag, rc, tk):
                    w_buf[slot, stacked, :] = jnp.zeros((rc, tk), BF16)
                    gb_buf[h, rows, :] = jnp.zeros((rc, tk), BF16)
                    continue
                wgt = jnp.exp(ls_buf[h, rows, :] + (_across(rem_buf[h, rows, :], tk) - upto_buf[h, rows, :]))
                if _some_masked(c, diag, rc, tk):
                    wgt = jnp.where(causal(c, diag), wgt, 0.0)
                w_buf[slot, stacked, :] = wgt.astype(BF16)
                g = wgt * dw_buf[slot, h, rows, :]
                g_buf[h, rows, :] = g
                gb_buf[h, rows, :] = g.astype(BF16)
                totg_buf[h, rows, :] = _row_sums(g)
                rem_buf[h, rows, :] -= totl_buf[h, rows, :]

        def dscore_stage(h, slot, diag):
            for c in range(tq // rc):
                rows = slice(c * rc, (c + 1) * rc)
                stacked = slice(h * tq + c * rc, h * tq + (c + 1) * rc)
                if _all_masked(c, diag, rc, tk):
                    dz_buf[slot, stacked, :] = jnp.zeros((rc, tk), BF16)
                    continue
                sig = jnp.exp(ls_buf[h, rows, :])
                g = g_buf[h, rows, :]
                dnz = sig * (before_buf[h, rows, :] + _across(preg_buf[h, rows, :], tk)) - g * (1.0 - sig)
                if _some_masked(c, diag, rc, tk):
                    dnz = jnp.where(causal(c, diag), dnz, 0.0)
                dz_buf[slot, stacked, :] = dnz.astype(BF16)
                preg_buf[h, rows, :] += totg_buf[h, rows, :]

        def position(blk, slot, diag, prefetch):
            if prefetch:
                scores(blk + 1, 1 - slot)
            for h in heads:
                softplus_stage(h, slot, diag)
                upto_buf[h] = _dot(hl_buf[h], up_ref[...])
            gradients(jnp.maximum(blk - 1, 0), 1 - slot)
            for h in heads:
                weights_stage(h, slot, diag)
                before_buf[h] = _dot(gb_buf[h], bf_ref[...])
            for h in heads:
                dscore_stage(h, slot, diag)

        scores(0, 0)

        def trip(jj, carry):
            for u in range(2):
                position(2 * jj + u, u, None, True)
            return carry

        lax.fori_loop(0, (i * ndiag) // 2, trip, 0)
        for d in range(ndiag):
            position(i * ndiag + d, d % 2, d, d < ndiag - 1)
        gradients(nblk - 1, 1)
        dq_ref[...] = (dq_t[...].T * NEG_QK_SCALE).astype(BF16)

        @pl.when(i == nq - 1)
        def _():
            for blk in range(nkb):
                dk_ref[blk * tk:(blk + 1) * tk, :] = dk_t[blk].T.astype(BF16)
                dv_ref[blk * tk:(blk + 1) * tk, :] = dv_t[blk].T.astype(BF16)

    qmap = lambda b, hp, i: (b * nq + i, hp)
    kmap = lambda b, hp, i: (b, hp)
    const = lambda b, hp, i: (0, 0)
    return pl.pallas_call(
        body, name="attn_bwd", grid=(nb, w // LANES, nq),
        in_specs=[pl.BlockSpec((tq, LANES), qmap), pl.BlockSpec((seq, LANES), kmap),
                  pl.BlockSpec((LANES, seq), lambda b, hp, i: (hp, b)), pl.BlockSpec((seq, LANES), kmap),
                  pl.BlockSpec((tq, LANES), qmap), pl.BlockSpec((tq, LANES), qmap),
                  pl.BlockSpec((2 * tk, tk), const), pl.BlockSpec((tk, tk), const)],
        out_specs=[pl.BlockSpec((tq, LANES), qmap), pl.BlockSpec((seq, LANES), kmap), pl.BlockSpec((seq, LANES), kmap)],
        out_shape=[jax.ShapeDtypeStruct((t_all, w), BF16)] * 3,
        scratch_shapes=[pltpu.VMEM((2, nh, tq, tk), F32), pltpu.VMEM((2, nh, tq, tk), F32),
                        pltpu.VMEM((nh, tq, tk), F32), pltpu.VMEM((nh, tq, 2 * tk), BF16),
                        pltpu.VMEM((nh, tq, tk), F32), pltpu.VMEM((nh, tq, tk), F32),
                        pltpu.VMEM((nh, tq, tk), BF16), pltpu.VMEM((nh, tq, tk), F32),
                        pltpu.VMEM((2, nh * tq, tk), BF16), pltpu.VMEM((2, nh * tq, tk), BF16),
                        pltpu.VMEM((nh, tq, LANES), F32), pltpu.VMEM((nh, tq, LANES), F32),
                        pltpu.VMEM((nh, tq, LANES), F32), pltpu.VMEM((nh, tq, LANES), F32),
                        pltpu.VMEM((LANES, nh * tq), BF16), pltpu.VMEM((LANES, nh * tq), BF16),
                        pltpu.VMEM((LANES, tq), F32), pltpu.VMEM((nkb, LANES, tk), F32),
                        pltpu.VMEM((nkb, LANES, tk), F32)],
        compiler_params=_params(("arbitrary", "arbitrary", "arbitrary")),
    )(qn, k, kt, v, do, ltot, _tri_matrix(tk, "upto"), _tri_matrix(tk, "before")[:tk])


def _inproj_bwd(dq, dk, dv, dpd, x, dx1, mod, g_pre, w_in, seq, tm):
    t_all, d = x.shape
    nt = seq // tm
    p = dq.shape[1]

    def body(dq_ref, dk_ref, dv_ref, dpd_ref, halo_ref, x_ref, dx1_ref, mod_ref, g_ref, w_ref,
             gx_ref, du_ref, accb_ref, accg_ref):
        i = pl.program_id(0)
        it = i % nt

        @pl.when(i == 0)
        def _():
            accg_ref[...] = jnp.zeros_like(accg_ref)

        @pl.when(it == 0)
        def _():
            accb_ref[...] = jnp.zeros_like(accb_ref)

        dpd = dpd_ref[...]
        pos = it * tm + lax.broadcasted_iota(jnp.int32, (tm, 1), 0)
        cnts = _window_counts(pos)
        halo = jnp.where(it == nt - 1, 0.0, halo_ref[...])
        scaled = []
        halos = []
        for g, win in enumerate(POOL_WINDOWS):
            cols = slice(g * POOL_GROUP, (g + 1) * POOL_GROUP)
            scaled.append(dpd[:, cols] / cnts[g])
            halos.append(halo[:, cols] / float(win))
        ext = jnp.concatenate([jnp.concatenate(scaled, axis=1), jnp.concatenate(halos, axis=1)], axis=0)
        sums = _window_sums(ext, tm, 0, True)
        du = (jnp.concatenate(sums, axis=1) - dpd).astype(BF16)
        du_ref[...] = du
        dh1 = (_dot_nt(dq_ref[...], w_ref[0]) + _dot_nt(dk_ref[...], w_ref[1])
               + _dot_nt(dv_ref[...], w_ref[2]) + _dot_nt(du, w_ref[3]))
        xf = x_ref[...]
        r1 = _rms(xf)
        n1 = xf * r1
        g1 = g_ref[...]
        scale1 = 1.0 + mod_ref[0, 1:2, :]
        accb_ref[0, 0:1, :] += _colsum(dh1)
        accb_ref[0, 1:2, :] += _colsum(dh1 * (n1 * g1))
        accg_ref[0:1, :] += _colsum((dh1 * scale1) * n1)
        gx_ref[...] = dx1_ref[...] + _norm_bwd((dh1 * scale1) * g1, n1, r1)

    tok = lambda i: (i, 0)
    const2 = lambda i: (0, 0)
    hb = tm // HALO
    last = t_all // HALO - 1
    return pl.pallas_call(
        body, name="inproj_bwd", grid=(t_all // tm,),
        in_specs=[pl.BlockSpec((tm, p), tok), pl.BlockSpec((tm, p), tok), pl.BlockSpec((tm, p), tok),
                  pl.BlockSpec((tm, p), tok),
                  pl.BlockSpec((HALO, p), lambda i: (jnp.minimum((i + 1) * hb, last), 0)),
                  pl.BlockSpec((tm, d), tok), pl.BlockSpec((tm, d), tok),
                  pl.BlockSpec((1, MOD_ROWS, d), lambda i: (i // nt, 0, 0)),
                  pl.BlockSpec((1, d), const2),
                  pl.BlockSpec((N_CHIPS, d, p), lambda i: (0, 0, 0))],
        out_specs=[pl.BlockSpec((tm, d), tok), pl.BlockSpec((tm, p), tok),
                   pl.BlockSpec((1, 8, d), lambda i: (i // nt, 0, 0)),
                   pl.BlockSpec((8, d), const2)],
        out_shape=[jax.ShapeDtypeStruct((t_all, d), F32), jax.ShapeDtypeStruct((t_all, p), BF16),
                   jax.ShapeDtypeStruct((t_all // seq, 8, d), F32),
                   jax.ShapeDtypeStruct((8, d), F32)],
        compiler_params=_params(("arbitrary",)),
    )(dq, dk, dv, dpd, dpd, x, dx1, mod, g_pre, w_in)


def _tn_matmul(x, ys, nk, bt, name):
    t_all = x.shape[-2]
    m = x.shape[-1]
    ny = len(ys)

    def spec(arr):
        if arr.ndim == 3:
            return pl.BlockSpec((1, bt, arr.shape[-1]), lambda k, t: (k, t, 0))
        return pl.BlockSpec((bt, arr.shape[-1]), lambda k, t: (t, 0))

    def tile(ref):
        return ref[0] if len(ref.shape) == 3 else ref[...]

    def body(*refs):
        x_ref, y_refs, o_refs = refs[0], refs[1:1 + ny], refs[1 + ny:]
        t = pl.program_id(1)
        xt = tile(x_ref)
        for y_ref, o_ref in zip(y_refs, o_refs):
            part = _dot_tn(xt, tile(y_ref))

            @pl.when(t == 0)
            def _(o_ref=o_ref, part=part):
                o_ref[0] = part

            @pl.when(t > 0)
            def _(o_ref=o_ref, part=part):
                o_ref[0] += part

    return pl.pallas_call(
        body, name=name, grid=(nk, t_all // bt),
        in_specs=[spec(x)] + [spec(y) for y in ys],
        out_specs=[pl.BlockSpec((1, m, y.shape[-1]), lambda k, t: (k, 0, 0)) for y in ys],
        out_shape=[jax.ShapeDtypeStruct((nk, m, y.shape[-1]), F32) for y in ys],
        compiler_params=_params(("arbitrary", "arbitrary")),
    )(x, *ys)


def _cond_fwd(c_all, w_q, b_q, bn):
    nrow, d = c_all.shape
    ncol = w_q.shape[1]

    def body(c_ref, w_ref, b_ref, sc_ref, mod_ref):
        cf = c_ref[...]
        sc = cf * _sigmoid(cf)
        sc_ref[...] = sc
        shi, slo = _split(sc)
        whi, wlo = _split(w_ref[...])
        mod_ref[...] = (_dot(shi, whi) + _dot(shi, wlo) + _dot(slo, whi)) + b_ref[...]

    return pl.pallas_call(
        body, name="cond_fwd", grid=(ncol // bn,),
        in_specs=[pl.BlockSpec((nrow, d), lambda n: (0, 0)), pl.BlockSpec((d, bn), lambda n: (0, n)),
                  pl.BlockSpec((1, bn), lambda n: (0, n))],
        out_specs=[pl.BlockSpec((nrow, d), lambda n: (0, 0)), pl.BlockSpec((nrow, bn), lambda n: (0, n))],
        out_shape=[jax.ShapeDtypeStruct((nrow, d), F32), jax.ShapeDtypeStruct((nrow, ncol), F32)],
        compiler_params=_params(("arbitrary",)),
    )(c_all, w_q, b_q)


def _cond_bwd(sc_all, dmod_q, bn):
    nrow, d = sc_all.shape
    ncol = dmod_q.shape[1]

    def body(sc_ref, dm_ref, gw_ref):
        shi, slo = _split(sc_ref[...])
        dhi, dlo = _split(dm_ref[...])
        gw_ref[...] = _dot_tn(shi, dhi) + _dot_tn(shi, dlo) + _dot_tn(slo, dhi)

    return pl.pallas_call(
        body, name="cond_bwd", grid=(ncol // bn,),
        in_specs=[pl.BlockSpec((nrow, d), lambda n: (0, 0)), pl.BlockSpec((nrow, bn), lambda n: (0, n))],
        out_specs=pl.BlockSpec((d, bn), lambda n: (0, n)),
        out_shape=jax.ShapeDtypeStruct((d, ncol), F32),
        compiler_params=_params(("arbitrary",)),
    )(sc_all, dmod_q)


def _row_block(rows, cols, budget=1 << 18):
    best = None
    for br in range(8, rows + 1, 8):
        if rows % br == 0 and br * cols <= budget:
            best = br
    return best if best is not None else rows


def _adamw(w, g, m, v, name):
    rows, cols = w.shape
    br = _row_block(rows, cols)
    c1 = 1.0 - ADAM_B1 ** ADAM_STEP
    c2 = 1.0 - ADAM_B2 ** ADAM_STEP

    def body(w_ref, g_ref, m_ref, v_ref, d_ref, nm_ref, nv_ref):
        gf = g_ref[...]
        m2 = ADAM_B1 * m_ref[...] + (1.0 - ADAM_B1) * gf
        v2 = ADAM_B2 * v_ref[...] + (1.0 - ADAM_B2) * (gf * gf)
        nm_ref[...] = m2
        nv_ref[...] = v2
        d_ref[...] = -ADAM_LR * ((m2 / c1) / (jnp.sqrt(v2 / c2) + ADAM_EPS) + ADAM_WD * w_ref[...])

    blk = pl.BlockSpec((br, cols), lambda i: (i, 0))
    return pl.pallas_call(
        body, name=name, grid=(rows // br,),
        in_specs=[blk] * 4, out_specs=[blk] * 3,
        out_shape=[jax.ShapeDtypeStruct((rows, cols), F32)] * 3,
        compiler_params=_params(("arbitrary",)),
    )(w, g, m, v)


def _all_gather(x_shard, name):
    m_per, n = x_shard.shape

    def body(x_ref, out_ref, send_sems, recv_sems, local_sem):
        x, y, c = _position()
        me, sibling = (x, y, c), (x, y, 1 - c)
        chips = [(1 - x, y), (x, 1 - y), (1 - x, 1 - y)]

        def rows(px, py, pc):
            return out_ref.at[pl.ds((4 * px + 2 * py + pc) * m_per, m_per), :]

        def copy(k, block, to, src=None):
            return pltpu.make_async_remote_copy(
                src_ref=rows(*block) if src is None else src, dst_ref=rows(*block),
                send_sem=send_sems.at[k], recv_sem=recv_sems.at[k], device_id=to, device_id_type=MESH)

        mine = pltpu.make_async_copy(x_ref, rows(*me), local_sem)
        mine.start()
        first = [copy(0, me, sibling, src=x_ref)]
        first += [copy(1 + j, me, (*chip, c), src=x_ref) for j, chip in enumerate(chips)]
        for cp in first:
            cp.start()
        passed = [copy(4 + j, (*chip, c), sibling) for j, chip in enumerate(chips)]
        for j, chip in enumerate(chips):
            copy(1 + j, (*chip, c), me).wait_recv()
            passed[j].start()
        copy(0, sibling, me).wait_recv()
        for j, chip in enumerate(chips):
            copy(4 + j, (*chip, 1 - c), me).wait_recv()
        for cp in first + passed:
            cp.wait_send()
        mine.wait()

    return pl.pallas_call(
        body, name=name,
        out_shape=jax.ShapeDtypeStruct((N_DEV * m_per, n), x_shard.dtype),
        in_specs=[pl.BlockSpec(memory_space=pltpu.VMEM)],
        out_specs=pl.BlockSpec(memory_space=pltpu.VMEM),
        scratch_shapes=[pltpu.SemaphoreType.DMA((7,)), pltpu.SemaphoreType.DMA((7,)), pltpu.SemaphoreType.DMA],
        compiler_params=pltpu.CompilerParams(vmem_limit_bytes=VMEM_LIMIT),
    )(x_shard)


_ANY = pl.BlockSpec(memory_space=pl.ANY)


def _place_quarters(place, quarters):
    steps = 2

    def body(place_ref, *refs):
        n = len(refs) // 2
        for w_ref, o_ref in zip(refs[:n], refs[n:]):
            o_ref[0] = w_ref[...].astype(BF16)

    return pl.pallas_call(
        body, name="place_quarters",
        grid_spec=pltpu.PrefetchScalarGridSpec(
            num_scalar_prefetch=1, grid=(steps,),
            in_specs=[pl.BlockSpec((q.shape[0] // steps, q.shape[1]), lambda r, place_ref: (r, 0)) for q in quarters],
            out_specs=[pl.BlockSpec((1, q.shape[0] // steps, q.shape[1]), lambda r, place_ref: (place_ref[0], r, 0))
                       for q in quarters]),
        out_shape=[jax.ShapeDtypeStruct((N_CHIPS,) + q.shape, BF16) for q in quarters],
        compiler_params=_params(("arbitrary",)),
    )(place, *quarters)


def _gather_weights(placed):
    n = len(placed)
    shapes = [b.shape[1:] for b in placed]

    def body(*refs):
        g_refs = refs[n:2 * n]
        send_sems, recv_sems = refs[2 * n:]
        x, y, c = _position()
        sibling = (x, y, 1 - c)
        chips = [(1 - x, y), (x, 1 - y), (1 - x, 1 - y)]
        mine = 2 * x + y

        def half(a, which):
            hr = shapes[a][0] // 2
            return pl.ds(which * hr, hr)

        def over_ici(a, p, slot):
            ref = g_refs[a].at[slot, half(a, c), :]
            return pltpu.make_async_remote_copy(
                src_ref=ref, dst_ref=ref,
                send_sem=send_sems.at[6 * a + p], recv_sem=recv_sems.at[6 * a + p],
                device_id=(*chips[p], c), device_id_type=MESH)

        def over_d2d(a, p, slot, which):
            ref = g_refs[a].at[slot, half(a, which), :]
            return pltpu.make_async_remote_copy(
                src_ref=ref, dst_ref=ref,
                send_sem=send_sems.at[6 * a + 3 + p], recv_sem=recv_sems.at[6 * a + 3 + p],
                device_id=sibling, device_id_type=MESH)

        sends = []
        for a in range(n):
            for p in range(3):
                cp = over_ici(a, p, mine)
                cp.start()
                sends.append(cp)
        for a in range(n):
            for p, (cx, cy) in enumerate(chips):
                slot = 2 * cx + cy
                over_ici(a, p, slot).wait_recv()
                cp = over_d2d(a, p, slot, c)
                cp.start()
                sends.append(cp)
        for a in range(n):
            for p, (cx, cy) in enumerate(chips):
                over_d2d(a, p, 2 * cx + cy, 1 - c).wait_recv()
        for cp in sends:
            cp.wait_send()

    return pl.pallas_call(
        body, name="gather_weights",
        out_shape=[jax.ShapeDtypeStruct(b.shape, BF16) for b in placed],
        in_specs=[_ANY] * n, out_specs=[_ANY] * n,
        input_output_aliases={a: a for a in range(n)},
        scratch_shapes=[pltpu.SemaphoreType.DMA((6 * n,)), pltpu.SemaphoreType.DMA((6 * n,))],
    )(*placed)


_HBM = pl.BlockSpec(memory_space=pltpu.HBM)
_SEM = pl.BlockSpec(memory_space=pltpu.SEMAPHORE)
_EFFECT = pltpu.SideEffectType.DATAFLOW_SIDE_EFFECTING


def _quarter_halves(shapes, a, which):
    hr = shapes[a][0] // 2
    return pl.ds(which * hr, hr)


def _gather_start(placed, after):
    n = len(placed)
    m = len(after)
    shapes = [b.shape[1:] for b in placed]

    def body(*refs):
        g_refs = refs[:n]
        send_sems, recv_sems = refs[n + m], refs[n + m + 1]
        token = refs[2 * n + m + 2]
        x, y, c = _position()
        chips = [(1 - x, y), (x, 1 - y), (1 - x, 1 - y)]
        mine = 2 * x + y
        for a in range(n):
            ref = g_refs[a].at[mine, _quarter_halves(shapes, a, c), :]
            for p in range(3):
                pltpu.make_async_remote_copy(
                    src_ref=ref, dst_ref=ref, send_sem=send_sems.at[3 * a + p], recv_sem=recv_sems.at[3 * a + p],
                    device_id=(*chips[p], c), device_id_type=MESH).start()
        token[...] = jnp.zeros_like(token)

    out = pl.pallas_call(
        body, name="gather_start",
        out_shape=(pltpu.SemaphoreType.DMA((3 * n,)), pltpu.SemaphoreType.DMA((3 * n,)),
                   *[pltpu.HBM(b.shape, b.dtype) for b in placed], jax.ShapeDtypeStruct((8, LANES), F32)),
        in_specs=[_HBM] * n + [_ANY] * m,
        out_specs=(_SEM, _SEM, *[_HBM] * n, pl.BlockSpec(memory_space=pltpu.VMEM)),
        input_output_aliases={a: 2 + a for a in range(n)},
        compiler_params=pltpu.CompilerParams(has_side_effects=_EFFECT),
    )(*[pltpu.with_memory_space_constraint(b, pltpu.HBM) for b in placed], *after)
    return out[0], out[1], list(out[2:2 + n]), out[2 + n]


def _gather_wait(send_sems, recv_sems, thru, after):
    n = len(thru)
    shapes = [b.shape[1:] for b in thru]

    def body(*refs):
        g_refs = refs[:n]
        send_sems, recv_sems = refs[n], refs[n + 1]
        x, y, c = _position()
        chips = [(1 - x, y), (x, 1 - y), (1 - x, 1 - y)]
        mine = 2 * x + y
        for a in range(n):
            rows = _quarter_halves(shapes, a, c)
            for p, (cx, cy) in enumerate(chips):
                copy = pltpu.make_async_remote_copy(
                    src_ref=g_refs[a].at[mine, rows, :], dst_ref=g_refs[a].at[2 * cx + cy, rows, :],
                    send_sem=send_sems.at[3 * a + p], recv_sem=recv_sems.at[3 * a + p],
                    device_id=(cx, cy, c), device_id_type=MESH)
                copy.wait_send()
                copy.wait_recv()

    return pl.pallas_call(
        body, name="gather_wait",
        out_shape=[pltpu.HBM(b.shape, b.dtype) for b in thru],
        in_specs=[_HBM] * n + [_SEM, _SEM, _ANY], out_specs=[_HBM] * n,
        input_output_aliases={a: a for a in range(n)},
        compiler_params=pltpu.CompilerParams(has_side_effects=_EFFECT),
    )(*thru, send_sems, recv_sems, after)


def _gather_forward(bufs):
    n = len(bufs)
    shapes = [b.shape[1:] for b in bufs]

    def body(*refs):
        g_refs = refs[n:2 * n]
        send_sems, recv_sems = refs[2 * n:]
        x, y, c = _position()
        chips = [(1 - x, y), (x, 1 - y), (1 - x, 1 - y)]

        def over_d2d(a, p, which):
            cx, cy = chips[p]
            ref = g_refs[a].at[2 * cx + cy, _quarter_halves(shapes, a, which), :]
            return pltpu.make_async_remote_copy(
                src_ref=ref, dst_ref=ref, send_sem=send_sems.at[3 * a + p], recv_sem=recv_sems.at[3 * a + p],
                device_id=(x, y, 1 - c), device_id_type=MESH)

        sends = [over_d2d(a, p, c) for a in range(n) for p in range(3)]
        for cp in sends:
            cp.start()
        for a in range(n):
            for p in range(3):
                over_d2d(a, p, 1 - c).wait_recv()
        for cp in sends:
            cp.wait_send()

    return pl.pallas_call(
        body, name="gather_forward",
        out_shape=[jax.ShapeDtypeStruct(b.shape, BF16) for b in bufs],
        in_specs=[_ANY] * n, out_specs=[_ANY] * n,
        input_output_aliases={a: a for a in range(n)},
        scratch_shapes=[pltpu.SemaphoreType.DMA((3 * n,)), pltpu.SemaphoreType.DMA((3 * n,))],
    )(*bufs)


def _sibling_exchange(grads, tag):
    n = len(grads)
    shapes = [g.shape for g in grads]

    def body(*refs):
        g_refs, x_refs = refs[:n], refs[n:2 * n]
        send_sems, recv_sems = refs[2 * n:]
        x, y, c = _position()
        copies = []
        for a in range(n):
            hr = shapes[a][1] // 2
            cp = pltpu.make_async_remote_copy(
                src_ref=g_refs[a].at[:, pl.ds((1 - c) * hr, hr), :], dst_ref=x_refs[a],
                send_sem=send_sems.at[a], recv_sem=recv_sems.at[a],
                device_id=(x, y, 1 - c), device_id_type=MESH)
            cp.start()
            copies.append(cp)
        for cp in copies:
            cp.wait()

    return pl.pallas_call(
        body, name="grad_sibling_exchange_" + tag,
        out_shape=[jax.ShapeDtypeStruct((s[0], s[1] // 2, s[2]), F32) for s in shapes],
        in_specs=[_ANY] * n, out_specs=[_ANY] * n,
        scratch_shapes=[pltpu.SemaphoreType.DMA((n,)), pltpu.SemaphoreType.DMA((n,))],
    )(*grads)


def _chip_sums(core, grads, theirs, tag):
    n = len(grads)

    def body(core_ref, *refs):
        g_refs, t_refs, o_refs = refs[:n], refs[n:2 * n], refs[2 * n:]
        for g_ref, t_ref, o_ref in zip(g_refs, t_refs, o_refs):
            o_ref[...] = (g_ref[...] + t_ref[...]).astype(BF16)

    in_specs = [pl.BlockSpec((1, g.shape[1] // 2, g.shape[2]), lambda k, core_ref: (k, core_ref[0], 0)) for g in grads]
    in_specs += [pl.BlockSpec((1,) + t.shape[1:], lambda k, core_ref: (k, 0, 0)) for t in theirs]
    return pl.pallas_call(
        body, name="grad_chip_sums_" + tag,
        grid_spec=pltpu.PrefetchScalarGridSpec(
            num_scalar_prefetch=1, grid=(N_CHIPS,), in_specs=in_specs,
            out_specs=[pl.BlockSpec((1,) + t.shape[1:], lambda k, core_ref: (k, 0, 0)) for t in theirs]),
        out_shape=[jax.ShapeDtypeStruct(t.shape, BF16) for t in theirs],
        compiler_params=_params(("arbitrary",)),
    )(core, *grads, *theirs)


def _chip_exchange(sums):
    n = len(sums)

    def body(*refs):
        s_refs, y_refs = refs[:n], refs[n:2 * n]
        send_sems, recv_sems = refs[2 * n:]
        x, y, c = _position()
        chips = [(1 - x, y), (x, 1 - y), (1 - x, 1 - y)]
        copies = []
        for a in range(n):
            for p, (cx, cy) in enumerate(chips):
                cp = pltpu.make_async_remote_copy(
                    src_ref=s_refs[a].at[2 * cx + cy], dst_ref=y_refs[a].at[p],
                    send_sem=send_sems.at[3 * a + p], recv_sem=recv_sems.at[3 * a + p],
                    device_id=(cx, cy, c), device_id_type=MESH)
                cp.start()
                copies.append(cp)
        for cp in copies:
            cp.wait()

    return pl.pallas_call(
        body, name="grad_chip_exchange",
        out_shape=[jax.ShapeDtypeStruct((3,) + s.shape[1:], BF16) for s in sums],
        in_specs=[_ANY] * n, out_specs=[_ANY] * n,
        scratch_shapes=[pltpu.SemaphoreType.DMA((3 * n,)), pltpu.SemaphoreType.DMA((3 * n,))],
    )(*sums)


def _chip_exchange_start(sums):
    n = len(sums)
    lands = [lax.empty((3,) + s.shape[1:], BF16) for s in sums]

    def body(*refs):
        s_refs, y_refs = refs[:n], refs[n:2 * n]
        send_sems, recv_sems = refs[2 * n], refs[2 * n + 1]
        token = refs[4 * n + 2]
        x, y, c = _position()
        chips = [(1 - x, y), (x, 1 - y), (1 - x, 1 - y)]
        for a in range(n):
            for p, (cx, cy) in enumerate(chips):
                pltpu.make_async_remote_copy(
                    src_ref=s_refs[a].at[2 * cx + cy], dst_ref=y_refs[a].at[p],
                    send_sem=send_sems.at[3 * a + p], recv_sem=recv_sems.at[3 * a + p],
                    device_id=(cx, cy, c), device_id_type=MESH).start()
        token[...] = jnp.zeros_like(token)

    both = list(sums) + lands
    out = pl.pallas_call(
        body, name="grad_chip_exchange_start",
        out_shape=(pltpu.SemaphoreType.DMA((3 * n,)), pltpu.SemaphoreType.DMA((3 * n,)),
                   *[pltpu.HBM(b.shape, b.dtype) for b in both], jax.ShapeDtypeStruct((8, LANES), F32)),
        in_specs=[_HBM] * (2 * n),
        out_specs=(_SEM, _SEM, *[_HBM] * (2 * n), pl.BlockSpec(memory_space=pltpu.VMEM)),
        input_output_aliases={a: 2 + a for a in range(2 * n)},
        compiler_params=pltpu.CompilerParams(has_side_effects=_EFFECT),
    )(*[pltpu.with_memory_space_constraint(b, pltpu.HBM) for b in both])
    return out[0], out[1], list(out[2:2 + n]), list(out[2 + n:2 + 2 * n]), out[2 + 2 * n]


def _chip_exchange_wait(send_sems, recv_sems, sums, lands, after):
    n = len(sums)

    def body(*refs):
        s_refs, y_refs = refs[:n], refs[n:2 * n]
        send_sems, recv_sems = refs[2 * n], refs[2 * n + 1]
        x, y, c = _position()
        chips = [(1 - x, y), (x, 1 - y), (1 - x, 1 - y)]
        for a in range(n):
            for p, (cx, cy) in enumerate(chips):
                copy = pltpu.make_async_remote_copy(
                    src_ref=s_refs[a].at[2 * cx + cy], dst_ref=y_refs[a].at[p],
                    send_sem=send_sems.at[3 * a + p], recv_sem=recv_sems.at[3 * a + p],
                    device_id=(cx, cy, c), device_id_type=MESH)
                copy.wait_send()
                copy.wait_recv()

    both = list(sums) + list(lands)
    out = pl.pallas_call(
        body, name="grad_chip_exchange_wait",
        out_shape=[pltpu.HBM(b.shape, b.dtype) for b in both],
        in_specs=[_HBM] * (2 * n) + [_SEM, _SEM, _ANY], out_specs=[_HBM] * (2 * n),
        input_output_aliases={a: a for a in range(2 * n)},
        compiler_params=pltpu.CompilerParams(has_side_effects=_EFFECT),
    )(*both, send_sems, recv_sems, after)
    return list(out[:n]), list(out[n:])


def _total_sums(place, sums, parts):
    n = len(parts)
    steps = 2

    def body(place_ref, *refs):
        for s_ref, y_ref, o_ref in zip(refs[:n], refs[n:2 * n], refs[2 * n:]):
            o_ref[0] = ((s_ref[0].astype(F32) + y_ref[0].astype(F32)) + y_ref[1].astype(F32)) + y_ref[2].astype(F32)

    def step_rows(pt):
        return pt.shape[1] // steps

    in_specs = [pl.BlockSpec((1, step_rows(s), s.shape[2]), lambda r, place_ref: (place_ref[0], r, 0)) for s in sums]
    in_specs += [pl.BlockSpec((3, step_rows(pt), pt.shape[2]), lambda r, place_ref: (0, r, 0)) for pt in parts]
    return pl.pallas_call(
        body, name="grad_total_sums",
        grid_spec=pltpu.PrefetchScalarGridSpec(
            num_scalar_prefetch=1, grid=(steps,), in_specs=in_specs,
            out_specs=[pl.BlockSpec((1, step_rows(pt), pt.shape[2]), lambda r, place_ref: (place_ref[1], r, 0))
                       for pt in parts]),
        out_shape=[jax.ShapeDtypeStruct((2,) + pt.shape[1:], F32) for pt in parts],
        compiler_params=_params(("arbitrary",)),
    )(place, *sums, *parts)


def _sibling_share(halves):
    n = len(halves)

    def body(*refs):
        f_refs = refs[n:2 * n]
        send_sems, recv_sems = refs[2 * n:]
        x, y, c = _position()
        copies = []
        for a in range(n):
            cp = pltpu.make_async_remote_copy(
                src_ref=f_refs[a].at[c], dst_ref=f_refs[a].at[c], send_sem=send_sems.at[a], recv_sem=recv_sems.at[a],
                device_id=(x, y, 1 - c), device_id_type=MESH)
            cp.start()
            copies.append(cp)
        for a, cp in enumerate(copies):
            cp.wait_send()
            pltpu.make_async_remote_copy(
                src_ref=f_refs[a].at[1 - c], dst_ref=f_refs[a].at[1 - c], send_sem=send_sems.at[a],
                recv_sem=recv_sems.at[a], device_id=(x, y, c), device_id_type=MESH).wait_recv()

    return pl.pallas_call(
        body, name="grad_sibling_share",
        out_shape=[jax.ShapeDtypeStruct(h.shape, F32) for h in halves],
        in_specs=[_ANY] * n, out_specs=[_ANY] * n,
        input_output_aliases={a: a for a in range(n)},
        scratch_shapes=[pltpu.SemaphoreType.DMA((n,)), pltpu.SemaphoreType.DMA((n,))],
    )(*halves)


def _group_sum(stacked, nrow, name):
    total, n = stacked.shape
    groups = total // nrow

    def body(g_ref, o_ref):
        acc = g_ref[0:nrow, :]
        for grp in range(1, groups):
            acc = acc + g_ref[grp * nrow:(grp + 1) * nrow, :]
        o_ref[...] = acc

    return pl.pallas_call(
        body, name=name,
        out_shape=jax.ShapeDtypeStruct((nrow, n), F32),
        compiler_params=pltpu.CompilerParams(vmem_limit_bytes=VMEM_LIMIT),
    )(stacked)


def _local_step(xt, tgt, mod, gains, w_pool, pool_scale, w_in, later_weights, on_ffn_grads, seq):
    g_mpre, g_mpost, g_fpre, g_fpost = gains
    d = xt.shape[1]
    tm, tq = min(TOKEN_TILE, seq), min(ATTN_TILE, seq)

    h1, qn, k, v, u, kt, vt = _prenorm_proj(xt, mod, g_mpre, w_in, seq, tm)
    tk = min(ATTN_KEY_TILE, tq // 2)
    o, ltot = _attn_fwd(qn, k, vt, seq, tq, tk)
    w_out, w_g, w_u, w_d = later_weights(o)
    w_out2 = w_out.reshape(d, d)
    pooled, mixin, mix, x1, h2 =_mixer_post(u, o, xt, mod, g_mpost, g_fpre, w_pool, pool_scale, w_out2, seq, tm)
    a, b, fin, dy, df, loss_blk, accb4, accg4 = _ffn_fwd(h2, w_g, w_u, w_d, x1, tgt, mod, g_fpost, seq, tm)
    da, db, dx1, dmix, accb5, accg5 = _ffn_bwd(df, a, b, w_d, w_g, w_u, x1, dy, mix, mod, g_fpre, g_mpost, seq, tm)
    bt = min(GRAD_TOKEN_TILE, xt.shape[0])
    g_g, g_u = _tn_matmul(h2, [da, db], w_g.shape[0], bt, "grad_w_gate_up")
    (g_d,) = _tn_matmul(fin, [df], w_d.shape[0], bt, "grad_w_down")
    token = on_ffn_grads([g_g, g_u, g_d])
    do, dpd, dps, dwp = _mixer_bwd(dmix, w_out2, pooled, w_pool, pool_scale + token, seq, tm)
    dq, dk, dv = _attn_bwd(qn, k, kt, v, do, ltot, seq, tq, tk)
    gx, du, accb8, accg8 = _inproj_bwd(dq, dk, dv, dpd, xt, dx1, mod, g_mpre, w_in, seq, tm)

    g_in = jnp.concatenate(_tn_matmul(h1, [dq, dk, dv, du], 1, bt, "grad_w_in"), axis=0)
    g_out = _tn_matmul(mixin, [dmix], 1, bt, "grad_w_out")[0].reshape(w_out.shape)

    dmod = jnp.stack([accb8[:, 0], accb8[:, 1], accb5[:, 2], accb5[:, 0], accb5[:, 1], accb4[:, 0]], axis=1)
    dgain = jnp.stack([accg8[0], accg5[1], accg5[0], accg4[0]], axis=0)
    return loss_blk, gx, [g_in, g_out, g_g, g_u, g_d], dmod, dgain, dps[0:1], dwp


def kernel(x, c, w_cond, b_cond, g_mix_pre, g_mix_post, w_in, w_pool, pool_scale, w_out, g_ffn_pre, g_ffn_post, w_gate, w_up, w_down, loss_target, m_w_cond, m_b_cond, m_g_mix_pre, m_g_mix_post, m_w_in, m_w_pool, m_pool_scale, m_w_out, m_g_ffn_pre, m_g_ffn_post, m_w_gate, m_w_up, m_w_down, v_w_cond, v_b_cond, v_g_mix_pre, v_g_mix_post, v_w_in, v_w_pool, v_pool_scale, v_w_out, v_g_ffn_pre, v_g_ffn_post, v_w_gate, v_w_up, v_w_down):
    xi, yi, ci = _position()
    chip = 2 * xi + yi
    dev = 4 * xi + 2 * yi + ci
    nb, seq, d = x.shape
    t_all = nb * seq
    xt = x.reshape(t_all, d)
    tgt = loss_target.reshape(t_all, d)
    ncol = w_cond.shape[2]
    pw = pool_scale.shape[1]

    c_pad = jnp.concatenate([c, jnp.zeros((8 - nb, d), F32)], axis=0)
    c_all = _all_gather(c_pad, "gather_c").reshape(N_DEV, 8, d)[:, :nb].reshape(N_DEV * nb, d)
    b_q = lax.dynamic_slice(b_cond, (0, chip * ncol), (1, ncol))
    sc_all, mod_q = _cond_fwd(c_all, w_cond[0], b_q, 512)
    mod_parts = _all_gather(mod_q, "gather_mod").reshape(N_DEV, N_DEV * nb, ncol)
    mod_rows = lax.dynamic_slice(mod_parts, (0, dev * nb, 0), (N_DEV, nb, ncol))[0::2]
    mod = jnp.transpose(mod_rows, (1, 0, 2)).reshape(nb, N_MOD, d)
    mod = jnp.concatenate([mod, jnp.zeros((nb, MOD_ROWS - N_MOD, d), F32)], axis=1)

    place = jnp.stack([chip, ci]).astype(jnp.int32)
    placed = _place_quarters(place, [w[0] for w in (w_in, w_out, w_gate, w_up, w_down)])
    (w_in_all,) = _gather_weights(placed[:1])
    send_sems, recv_sems, in_flight, token = _gather_start(placed[1:], [mod, w_in_all])
    mod = mod + token[0:1, 0:1]

    def later_weights(after):
        return _gather_forward(_gather_wait(send_sems, recv_sems, in_flight, after))

    ffn_split = []

    def on_ffn_grads(ffn_grads):
        theirs = _sibling_exchange(ffn_grads, "ffn")
        ffn_split.extend(_chip_exchange_start(_chip_sums(place[1:], ffn_grads, theirs, "ffn")))
        return ffn_split[4][0:1, 0:1]

    gains = (g_mix_pre, g_mix_post, g_ffn_pre, g_ffn_post)
    loss_blk, gx, grads, dmod, dgain, dps, dwp = _local_step(
        xt, tgt, mod, gains, w_pool[0], pool_scale, w_in_all, later_weights, on_ffn_grads, seq)
    loss = lax.psum(loss_blk[0, 0], ("x", "y", "c"))

    sums_ffn, parts_ffn = _chip_exchange_wait(*ffn_split[:4], gx)
    theirs = _sibling_exchange(grads[:2], "mix")
    sums_mix = _chip_sums(place[1:], grads[:2], theirs, "mix")
    parts_mix = _chip_exchange(sums_mix)
    halves = _total_sums(place, list(sums_mix) + list(sums_ffn), list(parts_mix) + list(parts_ffn))
    g_big = [g.reshape(2 * g.shape[1], g.shape[2]) for g in _sibling_share(halves)]

    wp_rows = dwp.size // d
    pad_rows = 24 - (2 * N_MOD + 4 + 1)
    payload = jnp.concatenate([
        dmod.reshape(nb * N_MOD, d), dgain,
        jnp.concatenate([dps, jnp.zeros((1, d - pw), F32)], axis=1),
        jnp.zeros((pad_rows, d), F32), dwp.reshape(wp_rows, d)], axis=0)
    prow = payload.shape[0]
    gathered = _all_gather(payload, "gather_small")
    summed = _group_sum(gathered, prow, "small_device_sum")
    dmod_all = gathered.reshape(N_DEV, prow, d)[:, :nb * N_MOD].reshape(N_DEV * nb, N_MOD * d)
    g_b_cond = _group_sum(dmod_all, 1, "grad_b_cond")
    dmod_q = lax.dynamic_slice(dmod_all, (0, chip * ncol), (N_DEV * nb, ncol))
    g_w_cond = _cond_bwd(sc_all, dmod_q, 512)
    first_gain = 2 * N_MOD
    g_gains = [summed[first_gain + r:first_gain + r + 1] for r in range(4)]
    g_pool_scale = summed[first_gain + 4:first_gain + 5, :pw]
    g_w_pool = summed[24:24 + wp_rows].reshape(w_pool.shape[1] * w_pool.shape[2], w_pool.shape[3])

    flat_pool = lambda t: t.reshape(g_w_pool.shape)
    plan = [
        ("w_cond", w_cond[0], g_w_cond, m_w_cond[0], v_w_cond[0], w_cond.shape),
        ("b_cond", b_cond, g_b_cond, m_b_cond, v_b_cond, b_cond.shape),
        ("g_mix_pre", g_mix_pre, g_gains[0], m_g_mix_pre, v_g_mix_pre, g_mix_pre.shape),
        ("g_mix_post", g_mix_post, g_gains[1], m_g_mix_post, v_g_mix_post, g_mix_post.shape),
        ("w_in", w_in[0], g_big[0], m_w_in[0], v_w_in[0], w_in.shape),
        ("w_pool", flat_pool(w_pool), g_w_pool, flat_pool(m_w_pool), flat_pool(v_w_pool), w_pool.shape),
        ("pool_scale", pool_scale, g_pool_scale, m_pool_scale, v_pool_scale, pool_scale.shape),
        ("w_out", w_out[0], g_big[1], m_w_out[0], v_w_out[0], w_out.shape),
        ("g_ffn_pre", g_ffn_pre, g_gains[2], m_g_ffn_pre, v_g_ffn_pre, g_ffn_pre.shape),
        ("g_ffn_post", g_ffn_post, g_gains[3], m_g_ffn_post, v_g_ffn_post, g_ffn_post.shape),
        ("w_gate", w_gate[0], g_big[2], m_w_gate[0], v_w_gate[0], w_gate.shape),
        ("w_up", w_up[0], g_big[3], m_w_up[0], v_w_up[0], w_up.shape),
        ("w_down", w_down[0], g_big[4], m_w_down[0], v_w_down[0], w_down.shape),
    ]
    out_g, out_d, out_m, out_v = [], [], [], []
    for name, w2, g2, m2, v2, shape in plan:
        delta, new_m, new_v = _adamw(w2, g2, m2, v2, "adamw_" + name)
        out_g.append(g2.reshape(shape))
        out_d.append(delta.reshape(shape))
        out_m.append(new_m.reshape(shape))
        out_v.append(new_v.reshape(shape))
    return (loss, gx.reshape(x.shape), *out_g, *out_d, *out_m, *out_v)
```

```python
import functools

import jax
import jax.numpy as jnp
from jax import lax
from jax.experimental import pallas as pl
from jax.experimental.pallas import tpu as pltpu

F32 = jnp.float32
BF16 = jnp.bfloat16
MESH = pl.DeviceIdType.MESH

EPS = 1e-6
HEAD_DIM = 64
HEADS_PER_BLOCK = 2
LANES = 128
NEG_QK_SCALE = -0.125
POOL_WINDOWS = (2, 4, 8, 16)
POOL_GROUP = 128
HALO = 16
N_MOD = 6
MOD_ROWS = 8
N_CHIPS = 4
N_DEV = 8
VMEM_LIMIT = 56 * 1024 * 1024

ADAM_LR = 0.001
ADAM_B1 = 0.9
ADAM_B2 = 0.999
ADAM_EPS = 1e-08
ADAM_WD = 0.01
ADAM_STEP = 10

TOKEN_TILE = 512
GRAD_TOKEN_TILE = 2048
ATTN_TILE = 512
ATTN_KEY_TILE = 256
ATTN_ROW_CHUNK = 32
LOG_SUM_PASSES = 2


def _dot(a, b):
    return jnp.dot(a, b, preferred_element_type=F32)


def _dot_nt(a, b):
    return lax.dot_general(a, b, (((1,), (1,)), ((), ())), preferred_element_type=F32)


def _dot_tn(a, b):
    return lax.dot_general(a, b, (((0,), (0,)), ((), ())), preferred_element_type=F32)


def _split(v):
    hi = v.astype(BF16)
    lo = (v - hi.astype(F32)).astype(BF16)
    return hi, lo


def _rms(v):
    return lax.rsqrt(jnp.mean(v * v, axis=-1, keepdims=True) + EPS)


def _norm_bwd(dn, n, r):
    return r * (dn - n * jnp.mean(dn * n, axis=-1, keepdims=True))


def _sigmoid(v):
    return 0.5 * jnp.tanh(0.5 * v) + 0.5


def _colsum(v):
    return jnp.sum(v, axis=0, keepdims=True)


def _params(sem=None):
    return pltpu.CompilerParams(dimension_semantics=sem, vmem_limit_bytes=VMEM_LIMIT)


def _position():
    return lax.axis_index("x"), lax.axis_index("y"), lax.axis_index("c")


def _prenorm_proj(x, mod, g_pre, w_in, seq, tm):
    t_all, d = x.shape
    nt = seq // tm
    p = w_in.shape[2]

    def body(x_ref, mod_ref, g_ref, w_ref, h_ref, q_ref, k_ref, v_ref, u_ref, kt_ref, vt_ref):
        xf = x_ref[...]
        n = xf * _rms(xf)
        h = (n * g_ref[...]) * (1.0 + mod_ref[0, 1:2, :]) + mod_ref[0, 0:1, :]
        hb = h.astype(BF16)
        h_ref[...] = hb
        q_ref[...] = (_dot(hb, w_ref[0]) * NEG_QK_SCALE).astype(BF16)
        kf = _dot(hb, w_ref[1])
        vf = _dot(hb, w_ref[2])
        k_ref[...] = kf.astype(BF16)
        v_ref[...] = vf.astype(BF16)
        kt_ref[...] = kf.T.astype(BF16)
        vt_ref[...] = vf.T.astype(BF16)
        u_ref[...] = _dot(hb, w_ref[3])

    tok = lambda i: (i, 0)
    tok_t = lambda i: (0, i)
    return pl.pallas_call(
        body, name="prenorm_proj", grid=(t_all // tm,),
        in_specs=[pl.BlockSpec((tm, d), tok),
                  pl.BlockSpec((1, MOD_ROWS, d), lambda i: (i // nt, 0, 0)),
                  pl.BlockSpec((1, d), lambda i: (0, 0)),
                  pl.BlockSpec((N_CHIPS, d, p), lambda i: (0, 0, 0))],
        out_specs=[pl.BlockSpec((tm, d), tok)] + [pl.BlockSpec((tm, p), tok)] * 4 + [pl.BlockSpec((p, tm), tok_t)] * 2,
        out_shape=[jax.ShapeDtypeStruct((t_all, d), BF16)] + [jax.ShapeDtypeStruct((t_all, p), BF16)] * 3
        + [jax.ShapeDtypeStruct((t_all, p), F32)] + [jax.ShapeDtypeStruct((p, t_all), BF16)] * 2,
        compiler_params=_params(("arbitrary",)),
    )(x, mod, g_pre, w_in)


def _tri_matrix(tk, kind):
    j = lax.broadcasted_iota(jnp.int32, (2 * tk, tk), 0) % tk
    s = lax.broadcasted_iota(jnp.int32, (2 * tk, tk), 1)
    return {"after": j > s, "upto": j <= s, "before": j < s}[kind].astype(BF16)


def _row_sums(v):
    return jnp.broadcast_to(jnp.sum(v, axis=-1, keepdims=True), (v.shape[0], LANES))


def _across(v, n):
    return jnp.concatenate([v] * (n // LANES), axis=1)


def _all_masked(c, diag, rc, tk):
    return diag is not None and diag * tk >= (c + 1) * rc - 1


def _some_masked(c, diag, rc, tk):
    return diag is not None and diag * tk + tk - 1 >= c * rc


def _attn_fwd(qn, k, vt, seq, tq, tk):
    t_all, w = qn.shape
    nb, nq, ndiag = t_all // seq, seq // tq, tq // tk
    assert ndiag % 2 == 0, "two key blocks per loop trip"
    rc = ATTN_ROW_CHUNK
    heads = range(HEADS_PER_BLOCK)

    def body(q_ref, k_ref, vt_ref, tri_ref, o_ref, l_ref,
             z_buf, ls_buf, hl_buf, aft_buf, w_buf, tot_buf, acc_t, run_buf):
        i = pl.program_id(2)
        nblk = (i + 1) * ndiag
        lane = lax.broadcasted_iota(jnp.int32, (1, LANES), 1)
        sub = lax.broadcasted_iota(jnp.int32, (LANES, 1), 0)
        row = lax.broadcasted_iota(jnp.int32, (rc, tk), 0)
        col = lax.broadcasted_iota(jnp.int32, (rc, tk), 1)
        first = lane < HEAD_DIM
        q2 = q_ref[...]
        qs = [jnp.where(first, q2, jnp.zeros_like(q2)), jnp.where(first, jnp.zeros_like(q2), q2)]
        acc_t[...] = jnp.zeros_like(acc_t)
        run_buf[...] = jnp.zeros_like(run_buf)
        w_buf[1] = jnp.zeros((HEADS_PER_BLOCK, tq, tk), BF16)

        def causal(c, diag):
            return (col + diag * tk) < (row + c * rc)

        def scores(blk, slot):
            kj = k_ref[pl.ds(pl.multiple_of(blk * tk, tk), tk), :]
            for h in heads:
                z_buf[slot, h] = _dot_nt(qs[h], kj)

        def values(blk, slot):
            vtj = vt_ref[:, pl.ds(pl.multiple_of(blk * tk, tk), tk)]
            zero = jnp.zeros_like(vtj)
            acc_t[...] += (_dot_nt(jnp.where(sub < HEAD_DIM, vtj, zero), w_buf[slot, 0])
                           + _dot_nt(jnp.where(sub < HEAD_DIM, zero, vtj), w_buf[slot, 1]))

        def softplus_stage(h, slot, diag):
            for c in range(tq // rc):
                rows = slice(c * rc, (c + 1) * rc)
                if _all_masked(c, diag, rc, tk):
                    hl_buf[h, rows, :] = jnp.zeros((rc, LOG_SUM_PASSES * tk), BF16)
                    tot_buf[h, rows, :] = jnp.zeros((rc, LANES), F32)
                    continue
                nz = z_buf[slot, h, rows, :]
                l1 = jnp.minimum(nz, 0.0) - jnp.log(1.0 + jnp.exp(-jnp.abs(nz)))
                if _some_masked(c, diag, rc, tk):
                    l1 = jnp.where(causal(c, diag), l1, 0.0)
                for s, part in enumerate(_split(l1)[:LOG_SUM_PASSES]):
                    hl_buf[h, rows, s * tk:(s + 1) * tk] = part
                ls_buf[h, rows, :] = l1 - nz
                tot_buf[h, rows, :] = _row_sums(l1)

        def weights_stage(h, slot, diag):
            for c in range(tq // rc):
                rows = slice(c * rc, (c + 1) * rc)
                if _all_masked(c, diag, rc, tk):
                    w_buf[slot, h, rows, :] = jnp.zeros((rc, tk), BF16)
                    continue
                wgt = jnp.exp((ls_buf[h, rows, :] + aft_buf[h, rows, :]) + _across(run_buf[h, rows, :], tk))
                if _some_masked(c, diag, rc, tk):
                    wgt = jnp.where(causal(c, diag), wgt, 0.0)
                w_buf[slot, h, rows, :] = wgt.astype(BF16)
                run_buf[h, rows, :] += tot_buf[h, rows, :]

        def position(blk, slot, diag):
            scores(jnp.maximum(blk - 1, 0), 1 - slot)
            for h in heads:
                softplus_stage(h, slot, diag)
                aft_buf[h] = _dot(hl_buf[h], tri_ref[...])
            values(jnp.minimum(blk + 1, nblk - 1), 1 - slot)
            for h in heads:
                weights_stage(h, slot, diag)

        scores(nblk - 1, 0)
        for p in range(ndiag):
            position(nblk - 1 - p, p % 2, ndiag - 1 - p)

        def trip(jj, carry):
            for u in range(2):
                position(i * ndiag - 1 - 2 * jj - u, u, None)
            return carry

        lax.fori_loop(0, (i * ndiag) // 2, trip, 0)
        values(0, 1)
        o_ref[...] = acc_t[...].T.astype(BF16)
        l_ref[...] = jnp.where(first, run_buf[0], run_buf[1])

    qmap = lambda b, hp, i: (b * nq + i, hp)
    nh = HEADS_PER_BLOCK
    return pl.pallas_call(
        body, name="attn_fwd", grid=(nb, w // LANES, nq),
        in_specs=[pl.BlockSpec((tq, LANES), qmap), pl.BlockSpec((seq, LANES), lambda b, hp, i: (b, hp)),
                  pl.BlockSpec((LANES, seq), lambda b, hp, i: (hp, b)),
                  pl.BlockSpec((LOG_SUM_PASSES * tk, tk), lambda b, hp, i: (0, 0))],
        out_specs=[pl.BlockSpec((tq, LANES), qmap), pl.BlockSpec((tq, LANES), qmap)],
        out_shape=[jax.ShapeDtypeStruct((t_all, w), BF16), jax.ShapeDtypeStruct((t_all, w), F32)],
        scratch_shapes=[pltpu.VMEM((2, nh, tq, tk), F32), pltpu.VMEM((nh, tq, tk), F32),
                        pltpu.VMEM((nh, tq, LOG_SUM_PASSES * tk), BF16), pltpu.VMEM((nh, tq, tk), F32),
                        pltpu.VMEM((2, nh, tq, tk), BF16), pltpu.VMEM((nh, tq, LANES), F32),
                        pltpu.VMEM((LANES, tq), F32), pltpu.VMEM((nh, tq, LANES), F32)],
        compiler_params=_params(("arbitrary", "arbitrary", "arbitrary")),
    )(qn, k, vt, _tri_matrix(tk, "after")[:LOG_SUM_PASSES * tk])


def _window_sums(ext, rows, offset, forward):
    r = lax.broadcasted_iota(jnp.int32, (rows, rows + HALO), 0)
    e = lax.broadcasted_iota(jnp.int32, (rows, rows + HALO), 1)
    hi, lo = _split(ext)
    out = []
    for g, win in enumerate(POOL_WINDOWS):
        if forward:
            band = (e >= r) & (e < r + win)
        else:
            band = (e <= r + offset) & (e > r + offset - win)
        bm = band.astype(BF16)
        cols = slice(g * POOL_GROUP, (g + 1) * POOL_GROUP)
        out.append(_dot(bm, hi[:, cols]) + _dot(bm, lo[:, cols]))
    return out


def _window_counts(pos):
    return [jnp.minimum(pos + 1, win).astype(F32) for win in POOL_WINDOWS]


def _mixer_post(u, o, x, mod, g_post, g_fpre, w_pool, pool_scale, w_out, seq, tm):
    t_all, d = x.shape
    nt = seq // tm
    p = u.shape[1]

    def body(u_ref, halo_ref, o_ref, x_ref, mod_ref, gp_ref, gf_ref, wp_ref, ps_ref, wo_ref,
             pooled_ref, mixin_ref, mix_ref, x1_ref, h2_ref):
        it = pl.program_id(0) % nt
        uf = u_ref[...]
        halo = jnp.where(it == 0, 0.0, halo_ref[...])
        ext = jnp.concatenate([halo, uf], axis=0)
        pos = it * tm + lax.broadcasted_iota(jnp.int32, (tm, 1), 0)
        sums = _window_sums(ext, tm, HALO, False)
        cnts = _window_counts(pos)
        pools = []
        for g in range(len(POOL_WINDOWS)):
            cols = slice(g * POOL_GROUP, (g + 1) * POOL_GROUP)
            pooled = (sums[g] / cnts[g] - uf[:, cols]).astype(BF16)
            pooled_ref[:, cols] = pooled
            yg = _dot(pooled, wp_ref[g].astype(BF16))
            pools.append((yg * ps_ref[:, cols]).astype(BF16))
        mixin = jnp.concatenate([o_ref[...]] + pools, axis=1)
        mixin_ref[...] = mixin
        mix = _dot(mixin, wo_ref[...])
        mix_ref[...] = mix
        n2 = mix * _rms(mix)
        x1 = x_ref[...] + mod_ref[0, 2:3, :] * (n2 * gp_ref[...])
        x1_ref[...] = x1
        n3 = x1 * _rms(x1)
        h2 = (n3 * gf_ref[...]) * (1.0 + mod_ref[0, 4:5, :]) + mod_ref[0, 3:4, :]
        h2_ref[...] = h2.astype(BF16)

    tok = lambda i: (i, 0)
    const2 = lambda i: (0, 0)
    hb = tm // HALO
    return pl.pallas_call(
        body, name="mixer_post", grid=(t_all // tm,),
        in_specs=[pl.BlockSpec((tm, p), tok),
                  pl.BlockSpec((HALO, p), lambda i: (jnp.maximum(i * hb - 1, 0), 0)),
                  pl.BlockSpec((tm, p), tok),
                  pl.BlockSpec((tm, d), tok),
                  pl.BlockSpec((1, MOD_ROWS, d), lambda i: (i // nt, 0, 0)),
                  pl.BlockSpec((1, d), const2), pl.BlockSpec((1, d), const2),
                  pl.BlockSpec(w_pool.shape, lambda i: (0, 0, 0)),
                  pl.BlockSpec((1, p), const2),
                  pl.BlockSpec((d, d), const2)],
        out_specs=[pl.BlockSpec((tm, p), tok), pl.BlockSpec((tm, d), tok), pl.BlockSpec((tm, d), tok),
                   pl.BlockSpec((tm, d), tok), pl.BlockSpec((tm, d), tok)],
        out_shape=[jax.ShapeDtypeStruct((t_all, p), BF16), jax.ShapeDtypeStruct((t_all, d), BF16),
                   jax.ShapeDtypeStruct((t_all, d), F32), jax.ShapeDtypeStruct((t_all, d), F32),
                   jax.ShapeDtypeStruct((t_all, d), BF16)],
        compiler_params=_params(("arbitrary",)),
    )(u, u, o, x, mod, g_post, g_fpre, w_pool, pool_scale, w_out)


def _ffn_fwd(h2, w_g, w_u, w_d, x1, tgt, mod, g_post, seq, tm):
    t_all, d = x1.shape
    nt = seq // tm
    nk, ff, _ = w_g.shape

    def body(h_ref, wg_ref, wu_ref, wd_ref, x1_ref, t_ref, mod_ref, g_ref,
             a_ref, b_ref, fin_ref, dy_ref, df_ref, loss_ref, accb_ref, accg_ref, facc):
        i, k = pl.program_id(0), pl.program_id(1)
        hb = h_ref[...]
        a = _dot_nt(hb, wg_ref[0])
        b = _dot_nt(hb, wu_ref[0])
        a_ref[0] = a.astype(BF16)
        b_ref[0] = b.astype(BF16)
        fin = ((a * _sigmoid(a)) * b).astype(BF16)
        fin_ref[0] = fin
        part = _dot(fin, wd_ref[0])

        @pl.when(k == 0)
        def _():
            facc[...] = part

        @pl.when(k > 0)
        def _():
            facc[...] += part

        @pl.when(k == nk - 1)
        def _():
            f = facc[...]
            r4 = _rms(f)
            n4 = f * r4
            gate = mod_ref[0, 5:6, :]
            g = g_ref[...]
            err = (x1_ref[...] + gate * (n4 * g)) - t_ref[...]
            dy = err * (1.0 / d)
            dy_ref[...] = dy

            @pl.when(i == 0)
            def _():
                loss_ref[...] = jnp.zeros_like(loss_ref)
                accg_ref[...] = jnp.zeros_like(accg_ref)

            @pl.when(i % nt == 0)
            def _():
                accb_ref[...] = jnp.zeros_like(accb_ref)

            loss_ref[...] += (0.5 / d) * jnp.sum(err * err)
            accb_ref[0, 0:1, :] += _colsum(dy * (n4 * g))
            accg_ref[0:1, :] += _colsum((dy * gate) * n4)
            dn4 = (dy * gate) * g
            df_ref[...] = _norm_bwd(dn4, n4, r4).astype(BF16)

    tok = lambda i, k: (i, 0)
    ktok = lambda i, k: (k, i, 0)
    kw = lambda i, k: (k, 0, 0)
    const2 = lambda i, k: (0, 0)
    return pl.pallas_call(
        body, name="ffn_fwd", grid=(t_all // tm, nk),
        in_specs=[pl.BlockSpec((tm, d), tok),
                  pl.BlockSpec((1, ff, d), kw), pl.BlockSpec((1, ff, d), kw), pl.BlockSpec((1, ff, d), kw),
                  pl.BlockSpec((tm, d), tok), pl.BlockSpec((tm, d), tok),
                  pl.BlockSpec((1, MOD_ROWS, d), lambda i, k: (i // nt, 0, 0)),
                  pl.BlockSpec((1, d), const2)],
        out_specs=[pl.BlockSpec((1, tm, ff), ktok)] * 3
        + [pl.BlockSpec((tm, d), tok), pl.BlockSpec((tm, d), tok),
           pl.BlockSpec((8, LANES), const2),
           pl.BlockSpec((1, 8, d), lambda i, k: (i // nt, 0, 0)),
           pl.BlockSpec((8, d), const2)],
        out_shape=[jax.ShapeDtypeStruct((nk, t_all, ff), BF16)] * 3
        + [jax.ShapeDtypeStruct((t_all, d), F32), jax.ShapeDtypeStruct((t_all, d), BF16),
           jax.ShapeDtypeStruct((8, LANES), F32),
           jax.ShapeDtypeStruct((t_all // seq, 8, d), F32),
           jax.ShapeDtypeStruct((8, d), F32)],
        scratch_shapes=[pltpu.VMEM((tm, d), F32)],
        compiler_params=_params(("arbitrary", "arbitrary")),
    )(h2, w_g, w_u, w_d, x1, tgt, mod, g_post)


def _ffn_bwd(df, a, b, w_d, w_g, w_u, x1, dy, mix, mod, g_fpre, g_mpost, seq, tm):
    t_all, d = x1.shape
    nt = seq // tm
    nk, ff, _ = w_g.shape

    def body(df_ref, a_ref, b_ref, wd_ref, wg_ref, wu_ref, x1_ref, dy_ref, mix_ref, mod_ref, gf_ref, gm_ref,
             da_ref, db_ref, dx1_ref, dmix_ref, accb_ref, accg_ref, hacc):
        i, k = pl.program_id(0), pl.program_id(1)
        dfin = _dot_nt(df_ref[...], wd_ref[0])
        af = a_ref[0].astype(F32)
        bf = b_ref[0].astype(F32)
        sig = _sigmoid(af)
        da = ((dfin * bf) * (sig * (1.0 + af * (1.0 - sig)))).astype(BF16)
        db = (dfin * (af * sig)).astype(BF16)
        da_ref[0] = da
        db_ref[0] = db
        part = _dot(da, wg_ref[0]) + _dot(db, wu_ref[0])

        @pl.when(k == 0)
        def _():
            hacc[...] = part

        @pl.when(k > 0)
        def _():
            hacc[...] += part

        @pl.when(k == nk - 1)
        def _():
            @pl.when(i == 0)
            def _():
                accg_ref[...] = jnp.zeros_like(accg_ref)

            @pl.when(i % nt == 0)
            def _():
                accb_ref[...] = jnp.zeros_like(accb_ref)

            dh2 = hacc[...]
            x1 = x1_ref[...]
            r3 = _rms(x1)
            n3 = x1 * r3
            g3 = gf_ref[...]
            scale1 = 1.0 + mod_ref[0, 4:5, :]
            accb_ref[0, 0:1, :] += _colsum(dh2)
            accb_ref[0, 1:2, :] += _colsum(dh2 * (n3 * g3))
            accg_ref[0:1, :] += _colsum((dh2 * scale1) * n3)
            dx1 = dy_ref[...] + _norm_bwd((dh2 * scale1) * g3, n3, r3)
            dx1_ref[...] = dx1
            mix = mix_ref[...]
            r2 = _rms(mix)
            n2 = mix * r2
            g2 = gm_ref[...]
            gate = mod_ref[0, 2:3, :]
            accb_ref[0, 2:3, :] += _colsum(dx1 * (n2 * g2))
            accg_ref[1:2, :] += _colsum((dx1 * gate) * n2)
            dmix_ref[...] = _norm_bwd((dx1 * gate) * g2, n2, r2).astype(BF16)

    tok = lambda i, k: (i, 0)
    ktok = lambda i, k: (k, i, 0)
    kw = lambda i, k: (k, 0, 0)
    const2 = lambda i, k: (0, 0)
    return pl.pallas_call(
        body, name="ffn_bwd", grid=(t_all // tm, nk),
        in_specs=[pl.BlockSpec((tm, d), tok),
                  pl.BlockSpec((1, tm, ff), ktok), pl.BlockSpec((1, tm, ff), ktok),
                  pl.BlockSpec((1, ff, d), kw), pl.BlockSpec((1, ff, d), kw), pl.BlockSpec((1, ff, d), kw),
                  pl.BlockSpec((tm, d), tok), pl.BlockSpec((tm, d), tok), pl.BlockSpec((tm, d), tok),
                  pl.BlockSpec((1, MOD_ROWS, d), lambda i, k: (i // nt, 0, 0)),
                  pl.BlockSpec((1, d), const2), pl.BlockSpec((1, d), const2)],
        out_specs=[pl.BlockSpec((1, tm, ff), ktok)] * 2
        + [pl.BlockSpec((tm, d), tok), pl.BlockSpec((tm, d), tok),
           pl.BlockSpec((1, 8, d), lambda i, k: (i // nt, 0, 0)),
           pl.BlockSpec((8, d), const2)],
        out_shape=[jax.ShapeDtypeStruct((nk, t_all, ff), BF16)] * 2
        + [jax.ShapeDtypeStruct((t_all, d), F32), jax.ShapeDtypeStruct((t_all, d), BF16),
           jax.ShapeDtypeStruct((t_all // seq, 8, d), F32),
           jax.ShapeDtypeStruct((8, d), F32)],
        scratch_shapes=[pltpu.VMEM((tm, d), F32)],
        compiler_params=_params(("arbitrary", "arbitrary")),
    )(df, a, b, w_d, w_g, w_u, x1, dy, mix, mod, g_fpre, g_mpost)


def _mixer_bwd(dmix, w_out, pooled, w_pool, pool_scale, seq, tm):
    t_all, d = dmix.shape
    p = pooled.shape[1]
    ng = len(POOL_WINDOWS)

    def body(dm_ref, wo_ref, pooled_ref, wp_ref, ps_ref, do_ref, dpd_ref, dps_ref, dwp_ref):
        i = pl.program_id(0)

        @pl.when(i == 0)
        def _():
            dps_ref[...] = jnp.zeros_like(dps_ref)
            dwp_ref[...] = jnp.zeros_like(dwp_ref)

        dmixin = _dot_nt(dm_ref[...], wo_ref[...])
        do_ref[...] = dmixin[:, :p].astype(BF16)
        for g in range(ng):
            cols = slice(g * POOL_GROUP, (g + 1) * POOL_GROUP)
            dpool = dmixin[:, p + g * POOL_GROUP:p + (g + 1) * POOL_GROUP]
            pooled = pooled_ref[:, cols]
            wpg = wp_ref[g].astype(BF16)
            yg = _dot(pooled, wpg)
            dps_ref[0:1, cols] += _colsum(dpool * yg)
            dyg = (dpool * ps_ref[:, cols]).astype(BF16)
            dwp_ref[g] += _dot_tn(pooled, dyg)
            dpd_ref[:, cols] = _dot_nt(dyg, wpg)

    tok = lambda i: (i, 0)
    const2 = lambda i: (0, 0)
    const3 = lambda i: (0, 0, 0)
    return pl.pallas_call(
        body, name="mixer_bwd", grid=(t_all // tm,),
        in_specs=[pl.BlockSpec((tm, d), tok), pl.BlockSpec((d, d), const2), pl.BlockSpec((tm, p), tok),
                  pl.BlockSpec(w_pool.shape, const3), pl.BlockSpec((1, p), const2)],
        out_specs=[pl.BlockSpec((tm, p), tok), pl.BlockSpec((tm, p), tok),
                   pl.BlockSpec((8, p), const2), pl.BlockSpec(w_pool.shape, const3)],
        out_shape=[jax.ShapeDtypeStruct((t_all, p), BF16), jax.ShapeDtypeStruct((t_all, p), F32),
                   jax.ShapeDtypeStruct((8, p), F32), jax.ShapeDtypeStruct(w_pool.shape, F32)],
        compiler_params=_params(("arbitrary",)),
    )(dmix, w_out, pooled, w_pool, pool_scale)


def _attn_bwd(qn, k, kt, v, do, ltot, seq, tq, tk):
    t_all, w = qn.shape
    nb, nq, ndiag, nkb = t_all // seq, seq // tq, tq // tk, seq // tk
    assert ndiag % 2 == 0, "two key blocks per loop trip"
    rc = ATTN_ROW_CHUNK
    nh = HEADS_PER_BLOCK
    heads = range(nh)

    def body(q_ref, k_ref, kt_ref, v_ref, do_ref, l_ref, up_ref, bf_ref, dq_ref, dk_ref, dv_ref,
             z_buf, dw_buf, ls_buf, hl_buf, upto_buf, g_buf, gb_buf, before_buf, w_buf, dz_buf,
             totl_buf, totg_buf, rem_buf, preg_buf, qnt_buf, dot_buf, dq_t, dk_t, dv_t):
        i = pl.program_id(2)
        nblk = (i + 1) * ndiag

        @pl.when(i == 0)
        def _():
            dk_t[...] = jnp.zeros_like(dk_t)
            dv_t[...] = jnp.zeros_like(dv_t)

        lane = lax.broadcasted_iota(jnp.int32, (1, LANES), 1)
        sub = lax.broadcasted_iota(jnp.int32, (LANES, 1), 0)
        row = lax.broadcasted_iota(jnp.int32, (rc, tk), 0)
        col = lax.broadcasted_iota(jnp.int32, (rc, tk), 1)
        first = lane < HEAD_DIM
        upper = sub < HEAD_DIM
        q2 = q_ref[...]
        do2 = do_ref[...]
        l2 = l_ref[...]
        qs = [jnp.where(first, q2, jnp.zeros_like(q2)), jnp.where(first, jnp.zeros_like(q2), q2)]
        dos = [jnp.where(first, do2, jnp.zeros_like(do2)), jnp.where(first, jnp.zeros_like(do2), do2)]
        for src, dst in ((q2, qnt_buf), (do2, dot_buf)):
            t = src.astype(F32).T
            dst[:, 0:tq] = jnp.where(upper, t, 0.0).astype(BF16)
            dst[:, tq:2 * tq] = jnp.where(upper, 0.0, t).astype(BF16)
        for h in heads:
            rem_buf[h] = jnp.where(first if h == 0 else ~first, l2, pltpu.roll(l2, HEAD_DIM, 1))
        preg_buf[...] = jnp.zeros_like(preg_buf)
        dq_t[...] = jnp.zeros_like(dq_t)
        w_buf[1] = jnp.zeros((nh * tq, tk), BF16)
        dz_buf[1] = jnp.zeros((nh * tq, tk), BF16)

        def causal(c, diag):
            return (col + diag * tk) < (row + c * rc)

        def scores(blk, slot):
            off = pl.multiple_of(blk * tk, tk)
            kj = k_ref[pl.ds(off, tk), :]
            vj = v_ref[pl.ds(off, tk), :]
            for h in heads:
                z_buf[slot, h] = _dot_nt(qs[h], kj)
                dw_buf[slot, h] = _dot_nt(dos[h], vj)

        def gradients(blk, slot):
            off = pl.multiple_of(blk * tk, tk)
            ktj = kt_ref[:, pl.ds(off, tk)]
            zero = jnp.zeros_like(ktj)
            dq_t[...] += (_dot_nt(jnp.where(upper, ktj, zero), dz_buf[slot, 0:tq, :])
                          + _dot_nt(jnp.where(upper, zero, ktj), dz_buf[slot, tq:2 * tq, :]))
            dk_t[blk] += _dot(qnt_buf[...], dz_buf[slot])
            dv_t[blk] += _dot(dot_buf[...], w_buf[slot])

        def softplus_stage(h, slot, diag):
            for c in range(tq // rc):
                rows = slice(c * rc, (c + 1) * rc)
                if _all_masked(c, diag, rc, tk):
                    hl_buf[h, rows, :] = jnp.zeros((rc, LOG_SUM_PASSES * tk), BF16)
                    continue
                nz = z_buf[slot, h, rows, :]
                l1 = jnp.minimum(nz, 0.0) - jnp.log(1.0 + jnp.exp(-jnp.abs(nz)))
                if _some_masked(c, diag, rc, tk):
                    l1 = jnp.where(causal(c, diag), l1, 0.0)
                for s, part in enumerate(_split(l1)[:LOG_SUM_PASSES]):
                    hl_buf[h, rows, s * tk:(s + 1) * tk] = part
                ls_buf[h, rows, :] = l1 - nz
                totl_buf[h, rows, :] = _row_sums(l1)

        def weights_stage(h, slot, diag):
            for c in range(tq // rc):
                rows = slice(c * rc, (c + 1) * rc)
                stacked = slice(h * tq + c * rc, h * tq + (c + 1) * rc)
                if _all_masked(c, diag, rc, tk):
                    w_buf[slot, stacked, :] = jnp.zeros((rc, tk), BF16)
                    gb_buf[h, rows, :] = jnp.zeros((rc, tk), BF16)
                    continue
                wgt = jnp.exp(ls_buf[h, rows, :] + (_across(rem_buf[h, rows, :], tk) - upto_buf[h, rows, :]))
                if _some_masked(c, diag, rc, tk):
                    wgt = jnp.where(causal(c, diag), wgt, 0.0)
                w_buf[slot, stacked, :] = wgt.astype(BF16)
                g = wgt * dw_buf[slot, h, rows, :]
                g_buf[h, rows, :] = g
                gb_buf[h, rows, :] = g.astype(BF16)
                totg_buf[h, rows, :] = _row_sums(g)
                rem_buf[h, rows, :] -= totl_buf[h, rows, :]

        def dscore_stage(h, slot, diag):
            for c in range(tq // rc):
                rows = slice(c * rc, (c + 1) * rc)
                stacked = slice(h * tq + c * rc, h * tq + (c + 1) * rc)
                if _all_masked(c, diag, rc, tk):
                    dz_buf[slot, stacked, :] = jnp.zeros((rc, tk), BF16)
                    continue
                sig = jnp.exp(ls_buf[h, rows, :])
                g = g_buf[h, rows, :]
                dnz = sig * (before_buf[h, rows, :] + _across(preg_buf[h, rows, :], tk)) - g * (1.0 - sig)
                if _some_masked(c, diag, rc, tk):
                    dnz = jnp.where(causal(c, diag), dnz, 0.0)
                dz_buf[slot, stacked, :] = dnz.astype(BF16)
                preg_buf[h, rows, :] += totg_buf[h, rows, :]

        def position(blk, slot, diag, prefetch):
            if prefetch:
                scores(blk + 1, 1 - slot)
            for h in heads:
                softplus_stage(h, slot, diag)
                upto_buf[h] = _dot(hl_buf[h], up_ref[...])
            gradients(jnp.maximum(blk - 1, 0), 1 - slot)
            for h in heads:
                weights_stage(h, slot, diag)
                before_buf[h] = _dot(gb_buf[h], bf_ref[...])
            for h in heads:
                dscore_stage(h, slot, diag)

        scores(0, 0)

        def trip(jj, carry):
            for u in range(2):
                position(2 * jj + u, u, None, True)
            return carry

        lax.fori_loop(0, (i * ndiag) // 2, trip, 0)
        for d in range(ndiag):
            position(i * ndiag + d, d % 2, d, d < ndiag - 1)
        gradients(nblk - 1, 1)
        dq_ref[...] = (dq_t[...].T * NEG_QK_SCALE).astype(BF16)

        @pl.when(i == nq - 1)
        def _():
            for blk in range(nkb):
                dk_ref[blk * tk:(blk + 1) * tk, :] = dk_t[blk].T.astype(BF16)
                dv_ref[blk * tk:(blk + 1) * tk, :] = dv_t[blk].T.astype(BF16)

    qmap = lambda b, hp, i: (b * nq + i, hp)
    kmap = lambda b, hp, i: (b, hp)
    const = lambda b, hp, i: (0, 0)
    return pl.pallas_call(
        body, name="attn_bwd", grid=(nb, w // LANES, nq),
        in_specs=[pl.BlockSpec((tq, LANES), qmap), pl.BlockSpec((seq, LANES), kmap),
                  pl.BlockSpec((LANES, seq), lambda b, hp, i: (hp, b)), pl.BlockSpec((seq, LANES), kmap),
                  pl.BlockSpec((tq, LANES), qmap), pl.BlockSpec((tq, LANES), qmap),
                  pl.BlockSpec((LOG_SUM_PASSES * tk, tk), const), pl.BlockSpec((tk, tk), const)],
        out_specs=[pl.BlockSpec((tq, LANES), qmap), pl.BlockSpec((seq, LANES), kmap), pl.BlockSpec((seq, LANES), kmap)],
        out_shape=[jax.ShapeDtypeStruct((t_all, w), BF16)] * 3,
        scratch_shapes=[pltpu.VMEM((2, nh, tq, tk), F32), pltpu.VMEM((2, nh, tq, tk), F32),
                        pltpu.VMEM((nh, tq, tk), F32), pltpu.VMEM((nh, tq, LOG_SUM_PASSES * tk), BF16),
                        pltpu.VMEM((nh, tq, tk), F32), pltpu.VMEM((nh, tq, tk), F32),
                        pltpu.VMEM((nh, tq, tk), BF16), pltpu.VMEM((nh, tq, tk), F32),
                        pltpu.VMEM((2, nh * tq, tk), BF16), pltpu.VMEM((2, nh * tq, tk), BF16),
                        pltpu.VMEM((nh, tq, LANES), F32), pltpu.VMEM((nh, tq, LANES), F32),
                        pltpu.VMEM((nh, tq, LANES), F32), pltpu.VMEM((nh, tq, LANES), F32),
                        pltpu.VMEM((LANES, nh * tq), BF16), pltpu.VMEM((LANES, nh * tq), BF16),
                        pltpu.VMEM((LANES, tq), F32), pltpu.VMEM((nkb, LANES, tk), F32),
                        pltpu.VMEM((nkb, LANES, tk), F32)],
        compiler_params=_params(("arbitrary", "arbitrary", "arbitrary")),
    )(qn, k, kt, v, do, ltot, _tri_matrix(tk, "upto")[:LOG_SUM_PASSES * tk], _tri_matrix(tk, "before")[:tk])


def _inproj_bwd(dq, dk, dv, dpd, x, dx1, mod, g_pre, w_in, seq, tm):
    t_all, d = x.shape
    nt = seq // tm
    p = dq.shape[1]

    def body(dq_ref, dk_ref, dv_ref, dpd_ref, halo_ref, x_ref, dx1_ref, mod_ref, g_ref, w_ref,
             gx_ref, du_ref, accb_ref, accg_ref):
        i = pl.program_id(0)
        it = i % nt

        @pl.when(i == 0)
        def _():
            accg_ref[...] = jnp.zeros_like(accg_ref)

        @pl.when(it == 0)
        def _():
            accb_ref[...] = jnp.zeros_like(accb_ref)

        dpd = dpd_ref[...]
        pos = it * tm + lax.broadcasted_iota(jnp.int32, (tm, 1), 0)
        cnts = _window_counts(pos)
        halo = jnp.where(it == nt - 1, 0.0, halo_ref[...])
        scaled = []
        halos = []
        for g, win in enumerate(POOL_WINDOWS):
            cols = slice(g * POOL_GROUP, (g + 1) * POOL_GROUP)
            scaled.append(dpd[:, cols] / cnts[g])
            halos.append(halo[:, cols] / float(win))
        ext = jnp.concatenate([jnp.concatenate(scaled, axis=1), jnp.concatenate(halos, axis=1)], axis=0)
        sums = _window_sums(ext, tm, 0, True)
        du = (jnp.concatenate(sums, axis=1) - dpd).astype(BF16)
        du_ref[...] = du
        dh1 = (_dot_nt(dq_ref[...], w_ref[0]) + _dot_nt(dk_ref[...], w_ref[1])
               + _dot_nt(dv_ref[...], w_ref[2]) + _dot_nt(du, w_ref[3]))
        xf = x_ref[...]
        r1 = _rms(xf)
        n1 = xf * r1
        g1 = g_ref[...]
        scale1 = 1.0 + mod_ref[0, 1:2, :]
        accb_ref[0, 0:1, :] += _colsum(dh1)
        accb_ref[0, 1:2, :] += _colsum(dh1 * (n1 * g1))
        accg_ref[0:1, :] += _colsum((dh1 * scale1) * n1)
        gx_ref[...] = dx1_ref[...] + _norm_bwd((dh1 * scale1) * g1, n1, r1)

    tok = lambda i: (i, 0)
    const2 = lambda i: (0, 0)
    hb = tm // HALO
    last = t_all // HALO - 1
    return pl.pallas_call(
        body, name="inproj_bwd", grid=(t_all // tm,),
        in_specs=[pl.BlockSpec((tm, p), tok), pl.BlockSpec((tm, p), tok), pl.BlockSpec((tm, p), tok),
                  pl.BlockSpec((tm, p), tok),
                  pl.BlockSpec((HALO, p), lambda i: (jnp.minimum((i + 1) * hb, last), 0)),
                  pl.BlockSpec((tm, d), tok), pl.BlockSpec((tm, d), tok),
                  pl.BlockSpec((1, MOD_ROWS, d), lambda i: (i // nt, 0, 0)),
                  pl.BlockSpec((1, d), const2),
                  pl.BlockSpec((N_CHIPS, d, p), lambda i: (0, 0, 0))],
        out_specs=[pl.BlockSpec((tm, d), tok), pl.BlockSpec((tm, p), tok),
                   pl.BlockSpec((1, 8, d), lambda i: (i // nt, 0, 0)),
                   pl.BlockSpec((8, d), const2)],
        out_shape=[jax.ShapeDtypeStruct((t_all, d), F32), jax.ShapeDtypeStruct((t_all, p), BF16),
                   jax.ShapeDtypeStruct((t_all // seq, 8, d), F32),
                   jax.ShapeDtypeStruct((8, d), F32)],
        compiler_params=_params(("arbitrary",)),
    )(dq, dk, dv, dpd, dpd, x, dx1, mod, g_pre, w_in)


def _tn_matmul(x, ys, nk, bt, name):
    t_all = x.shape[-2]
    m = x.shape[-1]
    ny = len(ys)

    def spec(arr):
        if arr.ndim == 3:
            return pl.BlockSpec((1, bt, arr.shape[-1]), lambda k, t: (k, t, 0))
        return pl.BlockSpec((bt, arr.shape[-1]), lambda k, t: (t, 0))

    def tile(ref):
        return ref[0] if len(ref.shape) == 3 else ref[...]

    def body(*refs):
        x_ref, y_refs, o_refs = refs[0], refs[1:1 + ny], refs[1 + ny:]
        t = pl.program_id(1)
        xt = tile(x_ref)
        for y_ref, o_ref in zip(y_refs, o_refs):
            part = _dot_tn(xt, tile(y_ref))

            @pl.when(t == 0)
            def _(o_ref=o_ref, part=part):
                o_ref[0] = part

            @pl.when(t > 0)
            def _(o_ref=o_ref, part=part):
                o_ref[0] += part

    return pl.pallas_call(
        body, name=name, grid=(nk, t_all // bt),
        in_specs=[spec(x)] + [spec(y) for y in ys],
        out_specs=[pl.BlockSpec((1, m, y.shape[-1]), lambda k, t: (k, 0, 0)) for y in ys],
        out_shape=[jax.ShapeDtypeStruct((nk, m, y.shape[-1]), F32) for y in ys],
        compiler_params=_params(("arbitrary", "arbitrary")),
    )(x, *ys)


def _cond_fwd(c_all, w_q, b_q, bn):
    nrow, d = c_all.shape
    ncol = w_q.shape[1]

    def body(c_ref, w_ref, b_ref, sc_ref, mod_ref):
        cf = c_ref[...]
        sc = cf * _sigmoid(cf)
        sc_ref[...] = sc
        shi, slo = _split(sc)
        whi, wlo = _split(w_ref[...])
        mod_ref[...] = (_dot(shi, whi) + _dot(shi, wlo) + _dot(slo, whi)) + b_ref[...]

    return pl.pallas_call(
        body, name="cond_fwd", grid=(ncol // bn,),
        in_specs=[pl.BlockSpec((nrow, d), lambda n: (0, 0)), pl.BlockSpec((d, bn), lambda n: (0, n)),
                  pl.BlockSpec((1, bn), lambda n: (0, n))],
        out_specs=[pl.BlockSpec((nrow, d), lambda n: (0, 0)), pl.BlockSpec((nrow, bn), lambda n: (0, n))],
        out_shape=[jax.ShapeDtypeStruct((nrow, d), F32), jax.ShapeDtypeStruct((nrow, ncol), F32)],
        compiler_params=_params(("arbitrary",)),
    )(c_all, w_q, b_q)


def _cond_bwd(sc_all, dmod_q, bn):
    nrow, d = sc_all.shape
    ncol = dmod_q.shape[1]

    def body(sc_ref, dm_ref, gw_ref):
        shi, slo = _split(sc_ref[...])
        dhi, dlo = _split(dm_ref[...])
        gw_ref[...] = _dot_tn(shi, dhi) + _dot_tn(shi, dlo) + _dot_tn(slo, dhi)

    return pl.pallas_call(
        body, name="cond_bwd", grid=(ncol // bn,),
        in_specs=[pl.BlockSpec((nrow, d), lambda n: (0, 0)), pl.BlockSpec((nrow, bn), lambda n: (0, n))],
        out_specs=pl.BlockSpec((d, bn), lambda n: (0, n)),
        out_shape=jax.ShapeDtypeStruct((d, ncol), F32),
        compiler_params=_params(("arbitrary",)),
    )(sc_all, dmod_q)


def _row_block(rows, cols, budget=1 << 18):
    best = None
    for br in range(8, rows + 1, 8):
        if rows % br == 0 and br * cols <= budget:
            best = br
    return best if best is not None else rows


def _adamw(w, g, m, v, name):
    rows, cols = w.shape
    br = _row_block(rows, cols)
    c1 = 1.0 - ADAM_B1 ** ADAM_STEP
    c2 = 1.0 - ADAM_B2 ** ADAM_STEP

    def body(w_ref, g_ref, m_ref, v_ref, d_ref, nm_ref, nv_ref):
        gf = g_ref[...]
        m2 = ADAM_B1 * m_ref[...] + (1.0 - ADAM_B1) * gf
        v2 = ADAM_B2 * v_ref[...] + (1.0 - ADAM_B2) * (gf * gf)
        nm_ref[...] = m2
        nv_ref[...] = v2
        d_ref[...] = -ADAM_LR * ((m2 / c1) / (jnp.sqrt(v2 / c2) + ADAM_EPS) + ADAM_WD * w_ref[...])

    blk = pl.BlockSpec((br, cols), lambda i: (i, 0))
    return pl.pallas_call(
        body, name=name, grid=(rows // br,),
        in_specs=[blk] * 4, out_specs=[blk] * 3,
        out_shape=[jax.ShapeDtypeStruct((rows, cols), F32)] * 3,
        compiler_params=_params(("arbitrary",)),
    )(w, g, m, v)


def _all_gather(x_shard, name):
    m_per, n = x_shard.shape

    def body(x_ref, out_ref, send_sems, recv_sems, local_sem):
        x, y, c = _position()
        me, sibling = (x, y, c), (x, y, 1 - c)
        chips = [(1 - x, y), (x, 1 - y), (1 - x, 1 - y)]

        def rows(px, py, pc):
            return out_ref.at[pl.ds((4 * px + 2 * py + pc) * m_per, m_per), :]

        def copy(k, block, to, src=None):
            return pltpu.make_async_remote_copy(
                src_ref=rows(*block) if src is None else src, dst_ref=rows(*block),
                send_sem=send_sems.at[k], recv_sem=recv_sems.at[k], device_id=to, device_id_type=MESH)

        mine = pltpu.make_async_copy(x_ref, rows(*me), local_sem)
        mine.start()
        first = [copy(0, me, sibling, src=x_ref)]
        first += [copy(1 + j, me, (*chip, c), src=x_ref) for j, chip in enumerate(chips)]
        for cp in first:
            cp.start()
        passed = [copy(4 + j, (*chip, c), sibling) for j, chip in enumerate(chips)]
        for j, chip in enumerate(chips):
            copy(1 + j, (*chip, c), me).wait_recv()
            passed[j].start()
        copy(0, sibling, me).wait_recv()
        for j, chip in enumerate(chips):
            copy(4 + j, (*chip, 1 - c), me).wait_recv()
        for cp in first + passed:
            cp.wait_send()
        mine.wait()

    return pl.pallas_call(
        body, name=name,
        out_shape=jax.ShapeDtypeStruct((N_DEV * m_per, n), x_shard.dtype),
        in_specs=[pl.BlockSpec(memory_space=pltpu.VMEM)],
        out_specs=pl.BlockSpec(memory_space=pltpu.VMEM),
        scratch_shapes=[pltpu.SemaphoreType.DMA((7,)), pltpu.SemaphoreType.DMA((7,)), pltpu.SemaphoreType.DMA],
        compiler_params=pltpu.CompilerParams(vmem_limit_bytes=VMEM_LIMIT),
    )(x_shard)


_ANY = pl.BlockSpec(memory_space=pl.ANY)


def _place_quarters(place, quarters):
    steps = 2

    def body(place_ref, *refs):
        n = len(refs) // 2
        for w_ref, o_ref in zip(refs[:n], refs[n:]):
            o_ref[0] = w_ref[...].astype(BF16)

    return pl.pallas_call(
        body, name="place_quarters",
        grid_spec=pltpu.PrefetchScalarGridSpec(
            num_scalar_prefetch=1, grid=(steps,),
            in_specs=[pl.BlockSpec((q.shape[0] // steps, q.shape[1]), lambda r, place_ref: (r, 0)) for q in quarters],
            out_specs=[pl.BlockSpec((1, q.shape[0] // steps, q.shape[1]), lambda r, place_ref: (place_ref[0], r, 0))
                       for q in quarters]),
        out_shape=[jax.ShapeDtypeStruct((N_CHIPS,) + q.shape, BF16) for q in quarters],
        compiler_params=_params(("arbitrary",)),
    )(place, *quarters)


def _gather_weights(placed):
    n = len(placed)
    shapes = [b.shape[1:] for b in placed]

    def body(*refs):
        g_refs = refs[n:2 * n]
        send_sems, recv_sems = refs[2 * n:]
        x, y, c = _position()
        sibling = (x, y, 1 - c)
        chips = [(1 - x, y), (x, 1 - y), (1 - x, 1 - y)]
        mine = 2 * x + y

        def half(a, which):
            hr = shapes[a][0] // 2
            return pl.ds(which * hr, hr)

        def over_ici(a, p, slot):
            ref = g_refs[a].at[slot, half(a, c), :]
            return pltpu.make_async_remote_copy(
                src_ref=ref, dst_ref=ref,
                send_sem=send_sems.at[6 * a + p], recv_sem=recv_sems.at[6 * a + p],
                device_id=(*chips[p], c), device_id_type=MESH)

        def over_d2d(a, p, slot, which):
            ref = g_refs[a].at[slot, half(a, which), :]
            return pltpu.make_async_remote_copy(
                src_ref=ref, dst_ref=ref,
                send_sem=send_sems.at[6 * a + 3 + p], recv_sem=recv_sems.at[6 * a + 3 + p],
                device_id=sibling, device_id_type=MESH)

        sends = []
        for a in range(n):
            for p in range(3):
                cp = over_ici(a, p, mine)
                cp.start()
                sends.append(cp)
        for a in range(n):
            for p, (cx, cy) in enumerate(chips):
                slot = 2 * cx + cy
                over_ici(a, p, slot).wait_recv()
                cp = over_d2d(a, p, slot, c)
                cp.start()
                sends.append(cp)
        for a in range(n):
            for p, (cx, cy) in enumerate(chips):
                over_d2d(a, p, 2 * cx + cy, 1 - c).wait_recv()
        for cp in sends:
            cp.wait_send()

    return pl.pallas_call(
        body, name="gather_weights",
        out_shape=[jax.ShapeDtypeStruct(b.shape, BF16) for b in placed],
        in_specs=[_ANY] * n, out_specs=[_ANY] * n,
        input_output_aliases={a: a for a in range(n)},
        scratch_shapes=[pltpu.SemaphoreType.DMA((6 * n,)), pltpu.SemaphoreType.DMA((6 * n,))],
    )(*placed)


_HBM = pl.BlockSpec(memory_space=pltpu.HBM)
_SEM = pl.BlockSpec(memory_space=pltpu.SEMAPHORE)
_EFFECT = pltpu.SideEffectType.DATAFLOW_SIDE_EFFECTING


def _quarter_halves(shapes, a, which):
    hr = shapes[a][0] // 2
    return pl.ds(which * hr, hr)


def _gather_start(placed, after):
    n = len(placed)
    m = len(after)
    shapes = [b.shape[1:] for b in placed]

    def body(*refs):
        g_refs = refs[:n]
        send_sems, recv_sems = refs[n + m], refs[n + m + 1]
        token = refs[2 * n + m + 2]
        x, y, c = _position()
        chips = [(1 - x, y), (x, 1 - y), (1 - x, 1 - y)]
        mine = 2 * x + y
        for a in range(n):
            ref = g_refs[a].at[mine, _quarter_halves(shapes, a, c), :]
            for p in range(3):
                pltpu.make_async_remote_copy(
                    src_ref=ref, dst_ref=ref, send_sem=send_sems.at[3 * a + p], recv_sem=recv_sems.at[3 * a + p],
                    device_id=(*chips[p], c), device_id_type=MESH).start()
        token[...] = jnp.zeros_like(token)

    out = pl.pallas_call(
        body, name="gather_start",
        out_shape=(pltpu.SemaphoreType.DMA((3 * n,)), pltpu.SemaphoreType.DMA((3 * n,)),
                   *[pltpu.HBM(b.shape, b.dtype) for b in placed], jax.ShapeDtypeStruct((8, LANES), F32)),
        in_specs=[_HBM] * n + [_ANY] * m,
        out_specs=(_SEM, _SEM, *[_HBM] * n, pl.BlockSpec(memory_space=pltpu.VMEM)),
        input_output_aliases={a: 2 + a for a in range(n)},
        compiler_params=pltpu.CompilerParams(has_side_effects=_EFFECT),
    )(*[pltpu.with_memory_space_constraint(b, pltpu.HBM) for b in placed], *after)
    return out[0], out[1], list(out[2:2 + n]), out[2 + n]


def _gather_wait(send_sems, recv_sems, thru, after):
    n = len(thru)
    shapes = [b.shape[1:] for b in thru]

    def body(*refs):
        g_refs = refs[:n]
        send_sems, recv_sems = refs[n], refs[n + 1]
        x, y, c = _position()
        chips = [(1 - x, y), (x, 1 - y), (1 - x, 1 - y)]
        mine = 2 * x + y
        for a in range(n):
            rows = _quarter_halves(shapes, a, c)
            for p, (cx, cy) in enumerate(chips):
                copy = pltpu.make_async_remote_copy(
                    src_ref=g_refs[a].at[mine, rows, :], dst_ref=g_refs[a].at[2 * cx + cy, rows, :],
                    send_sem=send_sems.at[3 * a + p], recv_sem=recv_sems.at[3 * a + p],
                    device_id=(cx, cy, c), device_id_type=MESH)
                copy.wait_send()
                copy.wait_recv()

    return pl.pallas_call(
        body, name="gather_wait",
        out_shape=[pltpu.HBM(b.shape, b.dtype) for b in thru],
        in_specs=[_HBM] * n + [_SEM, _SEM, _ANY], out_specs=[_HBM] * n,
        input_output_aliases={a: a for a in range(n)},
        compiler_params=pltpu.CompilerParams(has_side_effects=_EFFECT),
    )(*thru, send_sems, recv_sems, after)


def _gather_forward(bufs):
    n = len(bufs)
    shapes = [b.shape[1:] for b in bufs]

    def body(*refs):
        g_refs = refs[n:2 * n]
        send_sems, recv_sems = refs[2 * n:]
        x, y, c = _position()
        chips = [(1 - x, y), (x, 1 - y), (1 - x, 1 - y)]

        def over_d2d(a, p, which):
            cx, cy = chips[p]
            ref = g_refs[a].at[2 * cx + cy, _quarter_halves(shapes, a, which), :]
            return pltpu.make_async_remote_copy(
                src_ref=ref, dst_ref=ref, send_sem=send_sems.at[3 * a + p], recv_sem=recv_sems.at[3 * a + p],
                device_id=(x, y, 1 - c), device_id_type=MESH)

        sends = [over_d2d(a, p, c) for a in range(n) for p in range(3)]
        for cp in sends:
            cp.start()
        for a in range(n):
            for p in range(3):
                over_d2d(a, p, 1 - c).wait_recv()
        for cp in sends:
            cp.wait_send()

    return pl.pallas_call(
        body, name="gather_forward",
        out_shape=[jax.ShapeDtypeStruct(b.shape, BF16) for b in bufs],
        in_specs=[_ANY] * n, out_specs=[_ANY] * n,
        input_output_aliases={a: a for a in range(n)},
        scratch_shapes=[pltpu.SemaphoreType.DMA((3 * n,)), pltpu.SemaphoreType.DMA((3 * n,))],
    )(*bufs)


def _sibling_exchange(grads, tag):
    n = len(grads)
    shapes = [g.shape for g in grads]

    def body(*refs):
        g_refs, x_refs = refs[:n], refs[n:2 * n]
        send_sems, recv_sems = refs[2 * n:]
        x, y, c = _position()
        copies = []
        for a in range(n):
            hr = shapes[a][1] // 2
            cp = pltpu.make_async_remote_copy(
                src_ref=g_refs[a].at[:, pl.ds((1 - c) * hr, hr), :], dst_ref=x_refs[a],
                send_sem=send_sems.at[a], recv_sem=recv_sems.at[a],
                device_id=(x, y, 1 - c), device_id_type=MESH)
            cp.start()
            copies.append(cp)
        for cp in copies:
            cp.wait()

    return pl.pallas_call(
        body, name="grad_sibling_exchange_" + tag,
        out_shape=[jax.ShapeDtypeStruct((s[0], s[1] // 2, s[2]), F32) for s in shapes],
        in_specs=[_ANY] * n, out_specs=[_ANY] * n,
        scratch_shapes=[pltpu.SemaphoreType.DMA((n,)), pltpu.SemaphoreType.DMA((n,))],
    )(*grads)


def _chip_sums(core, grads, theirs, tag):
    n = len(grads)

    def body(core_ref, *refs):
        g_refs, t_refs, o_refs = refs[:n], refs[n:2 * n], refs[2 * n:]
        for g_ref, t_ref, o_ref in zip(g_refs, t_refs, o_refs):
            o_ref[...] = (g_ref[...] + t_ref[...]).astype(BF16)

    in_specs = [pl.BlockSpec((1, g.shape[1] // 2, g.shape[2]), lambda k, core_ref: (k, core_ref[0], 0)) for g in grads]
    in_specs += [pl.BlockSpec((1,) + t.shape[1:], lambda k, core_ref: (k, 0, 0)) for t in theirs]
    return pl.pallas_call(
        body, name="grad_chip_sums_" + tag,
        grid_spec=pltpu.PrefetchScalarGridSpec(
            num_scalar_prefetch=1, grid=(N_CHIPS,), in_specs=in_specs,
            out_specs=[pl.BlockSpec((1,) + t.shape[1:], lambda k, core_ref: (k, 0, 0)) for t in theirs]),
        out_shape=[jax.ShapeDtypeStruct(t.shape, BF16) for t in theirs],
        compiler_params=_params(("arbitrary",)),
    )(core, *grads, *theirs)


def _chip_exchange(sums):
    n = len(sums)

    def body(*refs):
        s_refs, y_refs = refs[:n], refs[n:2 * n]
        send_sems, recv_sems = refs[2 * n:]
        x, y, c = _position()
        chips = [(1 - x, y), (x, 1 - y), (1 - x, 1 - y)]
        copies = []
        for a in range(n):
            for p, (cx, cy) in enumerate(chips):
                cp = pltpu.make_async_remote_copy(
                    src_ref=s_refs[a].at[2 * cx + cy], dst_ref=y_refs[a].at[p],
                    send_sem=send_sems.at[3 * a + p], recv_sem=recv_sems.at[3 * a + p],
                    device_id=(cx, cy, c), device_id_type=MESH)
                cp.start()
                copies.append(cp)
        for cp in copies:
            cp.wait()

    return pl.pallas_call(
        body, name="grad_chip_exchange",
        out_shape=[jax.ShapeDtypeStruct((3,) + s.shape[1:], BF16) for s in sums],
        in_specs=[_ANY] * n, out_specs=[_ANY] * n,
        scratch_shapes=[pltpu.SemaphoreType.DMA((3 * n,)), pltpu.SemaphoreType.DMA((3 * n,))],
    )(*sums)


def _chip_exchange_start(sums):
    n = len(sums)
    lands = [lax.empty((3,) + s.shape[1:], BF16) for s in sums]

    def body(*refs):
        s_refs, y_refs = refs[:n], refs[n:2 * n]
        send_sems, recv_sems = refs[2 * n], refs[2 * n + 1]
        token = refs[4 * n + 2]
        x, y, c = _position()
        chips = [(1 - x, y), (x, 1 - y), (1 - x, 1 - y)]
        for a in range(n):
            for p, (cx, cy) in enumerate(chips):
                pltpu.make_async_remote_copy(
                    src_ref=s_refs[a].at[2 * cx + cy], dst_ref=y_refs[a].at[p],
                    send_sem=send_sems.at[3 * a + p], recv_sem=recv_sems.at[3 * a + p],
                    device_id=(cx, cy, c), device_id_type=MESH).start()
        token[...] = jnp.zeros_like(token)

    both = list(sums) + lands
    out = pl.pallas_call(
        body, name="grad_chip_exchange_start",
        out_shape=(pltpu.SemaphoreType.DMA((3 * n,)), pltpu.SemaphoreType.DMA((3 * n,)),
                   *[pltpu.HBM(b.shape, b.dtype) for b in both], jax.ShapeDtypeStruct((8, LANES), F32)),
        in_specs=[_HBM] * (2 * n),
        out_specs=(_SEM, _SEM, *[_HBM] * (2 * n), pl.BlockSpec(memory_space=pltpu.VMEM)),
        input_output_aliases={a: 2 + a for a in range(2 * n)},
        compiler_params=pltpu.CompilerParams(has_side_effects=_EFFECT),
    )(*[pltpu.with_memory_space_constraint(b, pltpu.HBM) for b in both])
    return out[0], out[1], list(out[2:2 + n]), list(out[2 + n:2 + 2 * n]), out[2 + 2 * n]


def _chip_exchange_wait(send_sems, recv_sems, sums, lands, after):
    n = len(sums)

    def body(*refs):
        s_refs, y_refs = refs[:n], refs[n:2 * n]
        send_sems, recv_sems = refs[2 * n], refs[2 * n + 1]
        x, y, c = _position()
        chips = [(1 - x, y), (x, 1 - y), (1 - x, 1 - y)]
        for a in range(n):
            for p, (cx, cy) in enumerate(chips):
                copy = pltpu.make_async_remote_copy(
                    src_ref=s_refs[a].at[2 * cx + cy], dst_ref=y_refs[a].at[p],
                    send_sem=send_sems.at[3 * a + p], recv_sem=recv_sems.at[3 * a + p],
                    device_id=(cx, cy, c), device_id_type=MESH)
                copy.wait_send()
                copy.wait_recv()

    both = list(sums) + list(lands)
    out = pl.pallas_call(
        body, name="grad_chip_exchange_wait",
        out_shape=[pltpu.HBM(b.shape, b.dtype) for b in both],
        in_specs=[_HBM] * (2 * n) + [_SEM, _SEM, _ANY], out_specs=[_HBM] * (2 * n),
        input_output_aliases={a: a for a in range(2 * n)},
        compiler_params=pltpu.CompilerParams(has_side_effects=_EFFECT),
    )(*both, send_sems, recv_sems, after)
    return list(out[:n]), list(out[n:])


def _total_sums(place, sums, parts):
    n = len(parts)
    steps = 2

    def body(place_ref, *refs):
        for s_ref, y_ref, o_ref in zip(refs[:n], refs[n:2 * n], refs[2 * n:]):
            o_ref[0] = ((s_ref[0].astype(F32) + y_ref[0].astype(F32)) + y_ref[1].astype(F32)) + y_ref[2].astype(F32)

    def step_rows(pt):
        return pt.shape[1] // steps

    in_specs = [pl.BlockSpec((1, step_rows(s), s.shape[2]), lambda r, place_ref: (place_ref[0], r, 0)) for s in sums]
    in_specs += [pl.BlockSpec((3, step_rows(pt), pt.shape[2]), lambda r, place_ref: (0, r, 0)) for pt in parts]
    return pl.pallas_call(
        body, name="grad_total_sums",
        grid_spec=pltpu.PrefetchScalarGridSpec(
            num_scalar_prefetch=1, grid=(steps,), in_specs=in_specs,
            out_specs=[pl.BlockSpec((1, step_rows(pt), pt.shape[2]), lambda r, place_ref: (place_ref[1], r, 0))
                       for pt in parts]),
        out_shape=[jax.ShapeDtypeStruct((2,) + pt.shape[1:], F32) for pt in parts],
        compiler_params=_params(("arbitrary",)),
    )(place, *sums, *parts)


def _sibling_share(halves):
    n = len(halves)

    def body(*refs):
        f_refs = refs[n:2 * n]
        send_sems, recv_sems = refs[2 * n:]
        x, y, c = _position()
        copies = []
        for a in range(n):
            cp = pltpu.make_async_remote_copy(
                src_ref=f_refs[a].at[c], dst_ref=f_refs[a].at[c], send_sem=send_sems.at[a], recv_sem=recv_sems.at[a],
                device_id=(x, y, 1 - c), device_id_type=MESH)
            cp.start()
            copies.append(cp)
        for a, cp in enumerate(copies):
            cp.wait_send()
            pltpu.make_async_remote_copy(
                src_ref=f_refs[a].at[1 - c], dst_ref=f_refs[a].at[1 - c], send_sem=send_sems.at[a],
                recv_sem=recv_sems.at[a], device_id=(x, y, c), device_id_type=MESH).wait_recv()

    return pl.pallas_call(
        body, name="grad_sibling_share",
        out_shape=[jax.ShapeDtypeStruct(h.shape, F32) for h in halves],
        in_specs=[_ANY] * n, out_specs=[_ANY] * n,
        input_output_aliases={a: a for a in range(n)},
        scratch_shapes=[pltpu.SemaphoreType.DMA((n,)), pltpu.SemaphoreType.DMA((n,))],
    )(*halves)


def _group_sum(stacked, nrow, name):
    total, n = stacked.shape
    groups = total // nrow

    def body(g_ref, o_ref):
        acc = g_ref[0:nrow, :]
        for grp in range(1, groups):
            acc = acc + g_ref[grp * nrow:(grp + 1) * nrow, :]
        o_ref[...] = acc

    return pl.pallas_call(
        body, name=name,
        out_shape=jax.ShapeDtypeStruct((nrow, n), F32),
        compiler_params=pltpu.CompilerParams(vmem_limit_bytes=VMEM_LIMIT),
    )(stacked)


def _local_step(xt, tgt, mod, gains, w_pool, pool_scale, w_in, later_weights, on_ffn_grads, seq):
    g_mpre, g_mpost, g_fpre, g_fpost = gains
    d = xt.shape[1]
    tm, tq = min(TOKEN_TILE, seq), min(ATTN_TILE, seq)

    h1, qn, k, v, u, kt, vt = _prenorm_proj(xt, mod, g_mpre, w_in, seq, tm)
    tk = min(ATTN_KEY_TILE, tq // 2)
    o, ltot = _attn_fwd(qn, k, vt, seq, tq, tk)
    w_out, w_g, w_u, w_d = later_weights(o)
    w_out2 = w_out.reshape(d, d)
    pooled, mixin, mix, x1, h2 =_mixer_post(u, o, xt, mod, g_mpost, g_fpre, w_pool, pool_scale, w_out2, seq, tm)
    a, b, fin, dy, df, loss_blk, accb4, accg4 = _ffn_fwd(h2, w_g, w_u, w_d, x1, tgt, mod, g_fpost, seq, tm)
    da, db, dx1, dmix, accb5, accg5 = _ffn_bwd(df, a, b, w_d, w_g, w_u, x1, dy, mix, mod, g_fpre, g_mpost, seq, tm)
    bt = min(GRAD_TOKEN_TILE, xt.shape[0])
    (g_g,) = _tn_matmul(da, [h2], w_g.shape[0], bt, "grad_w_gate")
    (g_u,) = _tn_matmul(db, [h2], w_u.shape[0], bt, "grad_w_up")
    (g_d,) = _tn_matmul(fin, [df], w_d.shape[0], bt, "grad_w_down")
    token = on_ffn_grads([g_g, g_u, g_d])
    do, dpd, dps, dwp = _mixer_bwd(dmix, w_out2, pooled, w_pool, pool_scale + token, seq, tm)
    dq, dk, dv = _attn_bwd(qn, k, kt, v, do, ltot, seq, tq, tk)
    gx, du, accb8, accg8 = _inproj_bwd(dq, dk, dv, dpd, xt, dx1, mod, g_mpre, w_in, seq, tm)

    g_in = jnp.concatenate(_tn_matmul(h1, [dq, dk, dv, du], 1, bt, "grad_w_in"), axis=0)
    g_out = _tn_matmul(mixin, [dmix], 1, bt, "grad_w_out")[0].reshape(w_out.shape)

    dmod = jnp.stack([accb8[:, 0], accb8[:, 1], accb5[:, 2], accb5[:, 0], accb5[:, 1], accb4[:, 0]], axis=1)
    dgain = jnp.stack([accg8[0], accg5[1], accg5[0], accg4[0]], axis=0)
    return loss_blk, gx, [g_in, g_out, g_g, g_u, g_d], dmod, dgain, dps[0:1], dwp


def kernel(x, c, w_cond, b_cond, g_mix_pre, g_mix_post, w_in, w_pool, pool_scale, w_out, g_ffn_pre, g_ffn_post, w_gate, w_up, w_down, loss_target, m_w_cond, m_b_cond, m_g_mix_pre, m_g_mix_post, m_w_in, m_w_pool, m_pool_scale, m_w_out, m_g_ffn_pre, m_g_ffn_post, m_w_gate, m_w_up, m_w_down, v_w_cond, v_b_cond, v_g_mix_pre, v_g_mix_post, v_w_in, v_w_pool, v_pool_scale, v_w_out, v_g_ffn_pre, v_g_ffn_post, v_w_gate, v_w_up, v_w_down):
    xi, yi, ci = _position()
    chip = 2 * xi + yi
    dev = 4 * xi + 2 * yi + ci
    nb, seq, d = x.shape
    t_all = nb * seq
    xt = x.reshape(t_all, d)
    tgt = loss_target.reshape(t_all, d)
    ncol = w_cond.shape[2]
    pw = pool_scale.shape[1]

    c_pad = jnp.concatenate([c, jnp.zeros((8 - nb, d), F32)], axis=0)
    c_all = _all_gather(c_pad, "gather_c").reshape(N_DEV, 8, d)[:, :nb].reshape(N_DEV * nb, d)
    b_q = lax.dynamic_slice(b_cond, (0, chip * ncol), (1, ncol))
    sc_all, mod_q = _cond_fwd(c_all, w_cond[0], b_q, 512)
    mod_parts = _all_gather(mod_q, "gather_mod").reshape(N_DEV, N_DEV * nb, ncol)
    mod_rows = lax.dynamic_slice(mod_parts, (0, dev * nb, 0), (N_DEV, nb, ncol))[0::2]
    mod = jnp.transpose(mod_rows, (1, 0, 2)).reshape(nb, N_MOD, d)
    mod = jnp.concatenate([mod, jnp.zeros((nb, MOD_ROWS - N_MOD, d), F32)], axis=1)

    place = jnp.stack([chip, ci]).astype(jnp.int32)
    turned = lambda t: jnp.swapaxes(t[0], 0, 1)
    placed = _place_quarters(place, [w_in[0], w_out[0], turned(w_gate), turned(w_up), w_down[0]])
    (w_in_all,) = _gather_weights(placed[:1])
    send_sems, recv_sems, in_flight, token = _gather_start(placed[1:], [mod, w_in_all])
    mod = mod + token[0:1, 0:1]

    def later_weights(after):
        return _gather_forward(_gather_wait(send_sems, recv_sems, in_flight, after))

    ffn_split = []

    def on_ffn_grads(ffn_grads):
        theirs = _sibling_exchange(ffn_grads, "ffn")
        ffn_split.extend(_chip_exchange_start(_chip_sums(place[1:], ffn_grads, theirs, "ffn")))
        return ffn_split[4][0:1, 0:1]

    gains = (g_mix_pre, g_mix_post, g_ffn_pre, g_ffn_post)
    loss_blk, gx, grads, dmod, dgain, dps, dwp = _local_step(
        xt, tgt, mod, gains, w_pool[0], pool_scale, w_in_all, later_weights, on_ffn_grads, seq)

    sums_ffn, parts_ffn = _chip_exchange_wait(*ffn_split[:4], gx)
    theirs = _sibling_exchange(grads[:2], "mix")
    sums_mix = _chip_sums(place[1:], grads[:2], theirs, "mix")
    parts_mix = _chip_exchange(sums_mix)
    halves = _total_sums(place, list(sums_mix) + list(sums_ffn), list(parts_mix) + list(parts_ffn))
    g_big = [g.reshape(2 * g.shape[1], g.shape[2]) for g in _sibling_share(halves)]

    wp_rows = dwp.size // d
    loss_row = 2 * N_MOD + 4 + 1
    pad_rows = 24 - (loss_row + 1)
    payload = jnp.concatenate([
        dmod.reshape(nb * N_MOD, d), dgain,
        jnp.concatenate([dps, jnp.zeros((1, d - pw), F32)], axis=1),
        jnp.concatenate([loss_blk[0:1], jnp.zeros((1, d - LANES), F32)], axis=1),
        jnp.zeros((pad_rows, d), F32), dwp.reshape(wp_rows, d)], axis=0)
    prow = payload.shape[0]
    gathered = _all_gather(payload, "gather_small")
    summed = _group_sum(gathered, prow, "small_device_sum")
    loss = summed[loss_row, 0]
    dmod_all = gathered.reshape(N_DEV, prow, d)[:, :nb * N_MOD].reshape(N_DEV * nb, N_MOD * d)
    g_b_cond = _group_sum(dmod_all, 1, "grad_b_cond")
    dmod_q = lax.dynamic_slice(dmod_all, (0, chip * ncol), (N_DEV * nb, ncol))
    g_w_cond = _cond_bwd(sc_all, dmod_q, 512)
    first_gain = 2 * N_MOD
    g_gains = [summed[first_gain + r:first_gain + r + 1] for r in range(4)]
    g_pool_scale = summed[first_gain + 4:first_gain + 5, :pw]
    g_w_pool = summed[24:24 + wp_rows].reshape(w_pool.shape[1] * w_pool.shape[2], w_pool.shape[3])

    flat_pool = lambda t: t.reshape(g_w_pool.shape)
    plan = [
        ("w_cond", w_cond[0], g_w_cond, m_w_cond[0], v_w_cond[0], w_cond.shape),
        ("b_cond", b_cond, g_b_cond, m_b_cond, v_b_cond, b_cond.shape),
        ("g_mix_pre", g_mix_pre, g_gains[0], m_g_mix_pre, v_g_mix_pre, g_mix_pre.shape),
        ("g_mix_post", g_mix_post, g_gains[1], m_g_mix_post, v_g_mix_post, g_mix_post.shape),
        ("w_in", w_in[0], g_big[0], m_w_in[0], v_w_in[0], w_in.shape),
        ("w_pool", flat_pool(w_pool), g_w_pool, flat_pool(m_w_pool), flat_pool(v_w_pool), w_pool.shape),
        ("pool_scale", pool_scale, g_pool_scale, m_pool_scale, v_pool_scale, pool_scale.shape),
        ("w_out", w_out[0], g_big[1], m_w_out[0], v_w_out[0], w_out.shape),
        ("g_ffn_pre", g_ffn_pre, g_gains[2], m_g_ffn_pre, v_g_ffn_pre, g_ffn_pre.shape),
        ("g_ffn_post", g_ffn_post, g_gains[3], m_g_ffn_post, v_g_ffn_post, g_ffn_post.shape),
        ("w_gate", turned(w_gate), g_big[2], turned(m_w_gate), turned(v_w_gate), None),
        ("w_up", turned(w_up), g_big[3], turned(m_w_up), turned(v_w_up), None),
        ("w_down", w_down[0], g_big[4], m_w_down[0], v_w_down[0], w_down.shape),
    ]
    out_g, out_d, out_m, out_v = [], [], [], []
    for name, w2, g2, m2, v2, shape in plan:
        delta, new_m, new_v = _adamw(w2, g2, m2, v2, "adamw_" + name)
        back = (lambda t: jnp.swapaxes(t, 0, 1)[None]) if shape is None else (lambda t, shape=shape: t.reshape(shape))
        out_g.append(back(g2))
        out_d.append(back(delta))
        out_m.append(back(new_m))
        out_v.append(back(new_v))
    return (loss, gx.reshape(x.shape), *out_g, *out_d, *out_m, *out_v)
```

```python
import functools

import jax
import jax.numpy as jnp
from jax import lax
from jax.experimental import pallas as pl
from jax.experimental.pallas import tpu as pltpu

F32 = jnp.float32
BF16 = jnp.bfloat16
MESH = pl.DeviceIdType.MESH

EPS = 1e-6
HEAD_DIM = 64
HEADS_PER_BLOCK = 2
LANES = 128
NEG_QK_SCALE = -0.125
POOL_WINDOWS = (2, 4, 8, 16)
POOL_GROUP = 128
HALO = 16
N_MOD = 6
MOD_ROWS = 8
N_CHIPS = 4
N_DEV = 8
VMEM_LIMIT = 56 * 1024 * 1024

ADAM_LR = 0.001
ADAM_B1 = 0.9
ADAM_B2 = 0.999
ADAM_EPS = 1e-08
ADAM_WD = 0.01
ADAM_STEP = 10

TOKEN_TILE = 512
GRAD_TOKEN_TILE = 2048
FFN_ROW_CHUNKS = 2
ATTN_TILE = 512
ATTN_KEY_TILE = 256
ATTN_ROW_CHUNK = 32
LOG_SUM_PASSES = 2


def _dot(a, b):
    return jnp.dot(a, b, preferred_element_type=F32)


def _dot_nt(a, b):
    return lax.dot_general(a, b, (((1,), (1,)), ((), ())), preferred_element_type=F32)


def _dot_tn(a, b):
    return lax.dot_general(a, b, (((0,), (0,)), ((), ())), preferred_element_type=F32)


def _split(v):
    hi = v.astype(BF16)
    lo = (v - hi.astype(F32)).astype(BF16)
    return hi, lo


def _rms(v):
    return lax.rsqrt(jnp.mean(v * v, axis=-1, keepdims=True) + EPS)


def _norm_bwd(dn, n, r):
    return r * (dn - n * jnp.mean(dn * n, axis=-1, keepdims=True))


def _sigmoid(v):
    return 0.5 * jnp.tanh(0.5 * v) + 0.5


def _colsum(v):
    return jnp.sum(v, axis=0, keepdims=True)


def _params(sem=None):
    return pltpu.CompilerParams(dimension_semantics=sem, vmem_limit_bytes=VMEM_LIMIT)


def _position():
    return lax.axis_index("x"), lax.axis_index("y"), lax.axis_index("c")


def _prenorm_proj(x, mod, g_pre, w_in, seq, tm):
    t_all, d = x.shape
    nt = seq // tm
    p = w_in.shape[2]

    def body(x_ref, mod_ref, g_ref, w_ref, h_ref, q_ref, k_ref, v_ref, u_ref, kt_ref, vt_ref):
        xf = x_ref[...]
        n = xf * _rms(xf)
        h = (n * g_ref[...]) * (1.0 + mod_ref[0, 1:2, :]) + mod_ref[0, 0:1, :]
        hb = h.astype(BF16)
        h_ref[...] = hb
        q_ref[...] = (_dot(hb, w_ref[0]) * NEG_QK_SCALE).astype(BF16)
        kf = _dot(hb, w_ref[1])
        vf = _dot(hb, w_ref[2])
        k_ref[...] = kf.astype(BF16)
        v_ref[...] = vf.astype(BF16)
        kt_ref[...] = kf.T.astype(BF16)
        vt_ref[...] = vf.T.astype(BF16)
        u_ref[...] = _dot(hb, w_ref[3])

    tok = lambda i: (i, 0)
    tok_t = lambda i: (0, i)
    return pl.pallas_call(
        body, name="prenorm_proj", grid=(t_all // tm,),
        in_specs=[pl.BlockSpec((tm, d), tok),
                  pl.BlockSpec((1, MOD_ROWS, d), lambda i: (i // nt, 0, 0)),
                  pl.BlockSpec((1, d), lambda i: (0, 0)),
                  pl.BlockSpec((N_CHIPS, d, p), lambda i: (0, 0, 0))],
        out_specs=[pl.BlockSpec((tm, d), tok)] + [pl.BlockSpec((tm, p), tok)] * 4 + [pl.BlockSpec((p, tm), tok_t)] * 2,
        out_shape=[jax.ShapeDtypeStruct((t_all, d), BF16)] + [jax.ShapeDtypeStruct((t_all, p), BF16)] * 3
        + [jax.ShapeDtypeStruct((t_all, p), F32)] + [jax.ShapeDtypeStruct((p, t_all), BF16)] * 2,
        compiler_params=_params(("arbitrary",)),
    )(x, mod, g_pre, w_in)


def _tri_matrix(tk, kind):
    j = lax.broadcasted_iota(jnp.int32, (2 * tk, tk), 0) % tk
    s = lax.broadcasted_iota(jnp.int32, (2 * tk, tk), 1)
    return {"after": j > s, "upto": j <= s, "before": j < s}[kind].astype(BF16)


def _row_sums(v):
    return jnp.broadcast_to(jnp.sum(v, axis=-1, keepdims=True), (v.shape[0], LANES))


def _across(v, n):
    return jnp.concatenate([v] * (n // LANES), axis=1)


def _all_masked(c, diag, rc, tk):
    return diag is not None and diag * tk >= (c + 1) * rc - 1


def _some_masked(c, diag, rc, tk):
    return diag is not None and diag * tk + tk - 1 >= c * rc


def _attn_fwd(qn, k, vt, seq, tq, tk):
    t_all, w = qn.shape
    nb, nq, ndiag = t_all // seq, seq // tq, tq // tk
    assert ndiag % 2 == 0, "two key blocks per loop trip"
    rc = ATTN_ROW_CHUNK
    heads = range(HEADS_PER_BLOCK)

    def body(q_ref, k_ref, vt_ref, tri_ref, o_ref, l_ref,
             z_buf, ls_buf, hl_buf, aft_buf, w_buf, tot_buf, acc_t, run_buf):
        i = pl.program_id(2)
        nblk = (i + 1) * ndiag
        lane = lax.broadcasted_iota(jnp.int32, (1, LANES), 1)
        sub = lax.broadcasted_iota(jnp.int32, (LANES, 1), 0)
        row = lax.broadcasted_iota(jnp.int32, (rc, tk), 0)
        col = lax.broadcasted_iota(jnp.int32, (rc, tk), 1)
        first = lane < HEAD_DIM
        q2 = q_ref[...]
        qs = [jnp.where(first, q2, jnp.zeros_like(q2)), jnp.where(first, jnp.zeros_like(q2), q2)]
        acc_t[...] = jnp.zeros_like(acc_t)
        run_buf[...] = jnp.zeros_like(run_buf)
        w_buf[1] = jnp.zeros((HEADS_PER_BLOCK, tq, tk), BF16)

        def causal(c, diag):
            return (col + diag * tk) < (row + c * rc)

        def scores(blk, slot):
            kj = k_ref[pl.ds(pl.multiple_of(blk * tk, tk), tk), :]
            for h in heads:
                z_buf[slot, h] = _dot_nt(qs[h], kj)

        def values(blk, slot):
            vtj = vt_ref[:, pl.ds(pl.multiple_of(blk * tk, tk), tk)]
            zero = jnp.zeros_like(vtj)
            acc_t[...] += (_dot_nt(jnp.where(sub < HEAD_DIM, vtj, zero), w_buf[slot, 0])
                           + _dot_nt(jnp.where(sub < HEAD_DIM, zero, vtj), w_buf[slot, 1]))

        def softplus_stage(h, slot, diag):
            for c in range(tq // rc):
                rows = slice(c * rc, (c + 1) * rc)
                if _all_masked(c, diag, rc, tk):
                    hl_buf[h, rows, :] = jnp.zeros((rc, LOG_SUM_PASSES * tk), BF16)
                    tot_buf[h, rows, :] = jnp.zeros((rc, LANES), F32)
                    continue
                nz = z_buf[slot, h, rows, :]
                l1 = jnp.minimum(nz, 0.0) - jnp.log(1.0 + jnp.exp(-jnp.abs(nz)))
                if _some_masked(c, diag, rc, tk):
                    l1 = jnp.where(causal(c, diag), l1, 0.0)
                for s, part in enumerate(_split(l1)[:LOG_SUM_PASSES]):
                    hl_buf[h, rows, s * tk:(s + 1) * tk] = part
                ls_buf[h, rows, :] = l1 - nz
                tot_buf[h, rows, :] = _row_sums(l1)

        def weights_stage(h, slot, diag):
            for c in range(tq // rc):
                rows = slice(c * rc, (c + 1) * rc)
                if _all_masked(c, diag, rc, tk):
                    w_buf[slot, h, rows, :] = jnp.zeros((rc, tk), BF16)
                    continue
                wgt = jnp.exp((ls_buf[h, rows, :] + aft_buf[h, rows, :]) + _across(run_buf[h, rows, :], tk))
                if _some_masked(c, diag, rc, tk):
                    wgt = jnp.where(causal(c, diag), wgt, 0.0)
                w_buf[slot, h, rows, :] = wgt.astype(BF16)
                run_buf[h, rows, :] += tot_buf[h, rows, :]

        def position(blk, slot, diag):
            scores(jnp.maximum(blk - 1, 0), 1 - slot)
            for h in heads:
                softplus_stage(h, slot, diag)
                aft_buf[h] = _dot(hl_buf[h], tri_ref[...])
            values(jnp.minimum(blk + 1, nblk - 1), 1 - slot)
            for h in heads:
                weights_stage(h, slot, diag)

        scores(nblk - 1, 0)
        for p in range(ndiag):
            position(nblk - 1 - p, p % 2, ndiag - 1 - p)

        def trip(jj, carry):
            for u in range(2):
                position(i * ndiag - 1 - 2 * jj - u, u, None)
            return carry

        lax.fori_loop(0, (i * ndiag) // 2, trip, 0)
        values(0, 1)
        o_ref[...] = acc_t[...].T.astype(BF16)
        l_ref[...] = jnp.where(first, run_buf[0], run_buf[1])

    qmap = lambda b, hp, i: (b * nq + i, hp)
    nh = HEADS_PER_BLOCK
    return pl.pallas_call(
        body, name="attn_fwd", grid=(nb, w // LANES, nq),
        in_specs=[pl.BlockSpec((tq, LANES), qmap), pl.BlockSpec((seq, LANES), lambda b, hp, i: (b, hp)),
                  pl.BlockSpec((LANES, seq), lambda b, hp, i: (hp, b)),
                  pl.BlockSpec((LOG_SUM_PASSES * tk, tk), lambda b, hp, i: (0, 0))],
        out_specs=[pl.BlockSpec((tq, LANES), qmap), pl.BlockSpec((tq, LANES), qmap)],
        out_shape=[jax.ShapeDtypeStruct((t_all, w), BF16), jax.ShapeDtypeStruct((t_all, w), F32)],
        scratch_shapes=[pltpu.VMEM((2, nh, tq, tk), F32), pltpu.VMEM((nh, tq, tk), F32),
                        pltpu.VMEM((nh, tq, LOG_SUM_PASSES * tk), BF16), pltpu.VMEM((nh, tq, tk), F32),
                        pltpu.VMEM((2, nh, tq, tk), BF16), pltpu.VMEM((nh, tq, LANES), F32),
                        pltpu.VMEM((LANES, tq), F32), pltpu.VMEM((nh, tq, LANES), F32)],
        compiler_params=_params(("arbitrary", "arbitrary", "arbitrary")),
    )(qn, k, vt, _tri_matrix(tk, "after")[:LOG_SUM_PASSES * tk])


def _window_sums(ext, rows, offset, forward):
    r = lax.broadcasted_iota(jnp.int32, (rows, rows + HALO), 0)
    e = lax.broadcasted_iota(jnp.int32, (rows, rows + HALO), 1)
    hi, lo = _split(ext)
    out = []
    for g, win in enumerate(POOL_WINDOWS):
        if forward:
            band = (e >= r) & (e < r + win)
        else:
            band = (e <= r + offset) & (e > r + offset - win)
        bm = band.astype(BF16)
        cols = slice(g * POOL_GROUP, (g + 1) * POOL_GROUP)
        out.append(_dot(bm, hi[:, cols]) + _dot(bm, lo[:, cols]))
    return out


def _window_counts(pos):
    return [jnp.minimum(pos + 1, win).astype(F32) for win in POOL_WINDOWS]


def _mixer_post(u, o, x, mod, g_post, g_fpre, w_pool, pool_scale, w_out, seq, tm):
    t_all, d = x.shape
    nt = seq // tm
    p = u.shape[1]

    def body(u_ref, halo_ref, o_ref, x_ref, mod_ref, gp_ref, gf_ref, wp_ref, ps_ref, wo_ref,
             pooled_ref, mixin_ref, mix_ref, x1_ref, h2_ref):
        it = pl.program_id(0) % nt
        uf = u_ref[...]
        halo = jnp.where(it == 0, 0.0, halo_ref[...])
        ext = jnp.concatenate([halo, uf], axis=0)
        pos = it * tm + lax.broadcasted_iota(jnp.int32, (tm, 1), 0)
        sums = _window_sums(ext, tm, HALO, False)
        cnts = _window_counts(pos)
        pools = []
        for g in range(len(POOL_WINDOWS)):
            cols = slice(g * POOL_GROUP, (g + 1) * POOL_GROUP)
            pooled = (sums[g] / cnts[g] - uf[:, cols]).astype(BF16)
            pooled_ref[:, cols] = pooled
            yg = _dot(pooled, wp_ref[g].astype(BF16))
            pools.append((yg * ps_ref[:, cols]).astype(BF16))
        mixin = jnp.concatenate([o_ref[...]] + pools, axis=1)
        mixin_ref[...] = mixin
        mix = _dot(mixin, wo_ref[...])
        mix_ref[...] = mix
        n2 = mix * _rms(mix)
        x1 = x_ref[...] + mod_ref[0, 2:3, :] * (n2 * gp_ref[...])
        x1_ref[...] = x1
        n3 = x1 * _rms(x1)
        h2 = (n3 * gf_ref[...]) * (1.0 + mod_ref[0, 4:5, :]) + mod_ref[0, 3:4, :]
        h2_ref[...] = h2.astype(BF16)

    tok = lambda i: (i, 0)
    const2 = lambda i: (0, 0)
    hb = tm // HALO
    return pl.pallas_call(
        body, name="mixer_post", grid=(t_all // tm,),
        in_specs=[pl.BlockSpec((tm, p), tok),
                  pl.BlockSpec((HALO, p), lambda i: (jnp.maximum(i * hb - 1, 0), 0)),
                  pl.BlockSpec((tm, p), tok),
                  pl.BlockSpec((tm, d), tok),
                  pl.BlockSpec((1, MOD_ROWS, d), lambda i: (i // nt, 0, 0)),
                  pl.BlockSpec((1, d), const2), pl.BlockSpec((1, d), const2),
                  pl.BlockSpec(w_pool.shape, lambda i: (0, 0, 0)),
                  pl.BlockSpec((1, p), const2),
                  pl.BlockSpec((d, d), const2)],
        out_specs=[pl.BlockSpec((tm, p), tok), pl.BlockSpec((tm, d), tok), pl.BlockSpec((tm, d), tok),
                   pl.BlockSpec((tm, d), tok), pl.BlockSpec((tm, d), tok)],
        out_shape=[jax.ShapeDtypeStruct((t_all, p), BF16), jax.ShapeDtypeStruct((t_all, d), BF16),
                   jax.ShapeDtypeStruct((t_all, d), F32), jax.ShapeDtypeStruct((t_all, d), F32),
                   jax.ShapeDtypeStruct((t_all, d), BF16)],
        compiler_params=_params(("arbitrary",)),
    )(u, u, o, x, mod, g_post, g_fpre, w_pool, pool_scale, w_out)


def _ffn_fwd(h2, w_g, w_u, w_d, x1, tgt, mod, g_post, seq, tm):
    t_all, d = x1.shape
    nt = seq // tm
    nk, ff, _ = w_g.shape

    def body(h_ref, wg_ref, wu_ref, wd_ref, x1_ref, t_ref, mod_ref, g_ref,
             a_ref, b_ref, fin_ref, dy_ref, df_ref, loss_ref, accb_ref, accg_ref, facc):
        i, k = pl.program_id(0), pl.program_id(1)

        @pl.when(k == 0)
        def _():
            facc[...] = jnp.zeros_like(facc)

        for c in range(FFN_ROW_CHUNKS):
            rows = slice(c * (tm // FFN_ROW_CHUNKS), (c + 1) * (tm // FFN_ROW_CHUNKS))
            hb = h_ref[rows, :]
            a = _dot_nt(hb, wg_ref[0])
            b = _dot_nt(hb, wu_ref[0])
            a_ref[0, rows, :] = a.astype(BF16)
            b_ref[0, rows, :] = b.astype(BF16)
            fin = ((a * _sigmoid(a)) * b).astype(BF16)
            fin_ref[0, rows, :] = fin
            facc[rows, :] += _dot(fin, wd_ref[0])

        @pl.when(k == nk - 1)
        def _():
            f = facc[...]
            r4 = _rms(f)
            n4 = f * r4
            gate = mod_ref[0, 5:6, :]
            g = g_ref[...]
            err = (x1_ref[...] + gate * (n4 * g)) - t_ref[...]
            dy = err * (1.0 / d)
            dy_ref[...] = dy

            @pl.when(i == 0)
            def _():
                loss_ref[...] = jnp.zeros_like(loss_ref)
                accg_ref[...] = jnp.zeros_like(accg_ref)

            @pl.when(i % nt == 0)
            def _():
                accb_ref[...] = jnp.zeros_like(accb_ref)

            loss_ref[...] += (0.5 / d) * jnp.sum(err * err)
            accb_ref[0, 0:1, :] += _colsum(dy * (n4 * g))
            accg_ref[0:1, :] += _colsum((dy * gate) * n4)
            dn4 = (dy * gate) * g
            df_ref[...] = _norm_bwd(dn4, n4, r4).astype(BF16)

    tok = lambda i, k: (i, 0)
    ktok = lambda i, k: (k, i, 0)
    kw = lambda i, k: (k, 0, 0)
    const2 = lambda i, k: (0, 0)
    return pl.pallas_call(
        body, name="ffn_fwd", grid=(t_all // tm, nk),
        in_specs=[pl.BlockSpec((tm, d), tok),
                  pl.BlockSpec((1, ff, d), kw), pl.BlockSpec((1, ff, d), kw), pl.BlockSpec((1, ff, d), kw),
                  pl.BlockSpec((tm, d), tok), pl.BlockSpec((tm, d), tok),
                  pl.BlockSpec((1, MOD_ROWS, d), lambda i, k: (i // nt, 0, 0)),
                  pl.BlockSpec((1, d), const2)],
        out_specs=[pl.BlockSpec((1, tm, ff), ktok)] * 3
        + [pl.BlockSpec((tm, d), tok), pl.BlockSpec((tm, d), tok),
           pl.BlockSpec((8, LANES), const2),
           pl.BlockSpec((1, 8, d), lambda i, k: (i // nt, 0, 0)),
           pl.BlockSpec((8, d), const2)],
        out_shape=[jax.ShapeDtypeStruct((nk, t_all, ff), BF16)] * 3
        + [jax.ShapeDtypeStruct((t_all, d), F32), jax.ShapeDtypeStruct((t_all, d), BF16),
           jax.ShapeDtypeStruct((8, LANES), F32),
           jax.ShapeDtypeStruct((t_all // seq, 8, d), F32),
           jax.ShapeDtypeStruct((8, d), F32)],
        scratch_shapes=[pltpu.VMEM((tm, d), F32)],
        compiler_params=_params(("arbitrary", "arbitrary")),
    )(h2, w_g, w_u, w_d, x1, tgt, mod, g_post)


def _ffn_bwd(df, a, b, w_d, w_g, w_u, x1, dy, mix, mod, g_fpre, g_mpost, seq, tm):
    t_all, d = x1.shape
    nt = seq // tm
    nk, ff, _ = w_g.shape

    def body(df_ref, a_ref, b_ref, wd_ref, wg_ref, wu_ref, x1_ref, dy_ref, mix_ref, mod_ref, gf_ref, gm_ref,
             da_ref, db_ref, dx1_ref, dmix_ref, accb_ref, accg_ref, hacc):
        i, k = pl.program_id(0), pl.program_id(1)

        @pl.when(k == 0)
        def _():
            hacc[...] = jnp.zeros_like(hacc)

        for c in range(FFN_ROW_CHUNKS):
            rows = slice(c * (tm // FFN_ROW_CHUNKS), (c + 1) * (tm // FFN_ROW_CHUNKS))
            dfin = _dot_nt(df_ref[rows, :], wd_ref[0])
            af = a_ref[0, rows, :].astype(F32)
            bf = b_ref[0, rows, :].astype(F32)
            sig = _sigmoid(af)
            da = ((dfin * bf) * (sig * (1.0 + af * (1.0 - sig)))).astype(BF16)
            db = (dfin * (af * sig)).astype(BF16)
            da_ref[0, rows, :] = da
            db_ref[0, rows, :] = db
            hacc[rows, :] += _dot(da, wg_ref[0]) + _dot(db, wu_ref[0])

        @pl.when(k == nk - 1)
        def _():
            @pl.when(i == 0)
            def _():
                accg_ref[...] = jnp.zeros_like(accg_ref)

            @pl.when(i % nt == 0)
            def _():
                accb_ref[...] = jnp.zeros_like(accb_ref)

            dh2 = hacc[...]
            x1 = x1_ref[...]
            r3 = _rms(x1)
            n3 = x1 * r3
            g3 = gf_ref[...]
            scale1 = 1.0 + mod_ref[0, 4:5, :]
            accb_ref[0, 0:1, :] += _colsum(dh2)
            accb_ref[0, 1:2, :] += _colsum(dh2 * (n3 * g3))
            accg_ref[0:1, :] += _colsum((dh2 * scale1) * n3)
            dx1 = dy_ref[...] + _norm_bwd((dh2 * scale1) * g3, n3, r3)
            dx1_ref[...] = dx1
            mix = mix_ref[...]
            r2 = _rms(mix)
            n2 = mix * r2
            g2 = gm_ref[...]
            gate = mod_ref[0, 2:3, :]
            accb_ref[0, 2:3, :] += _colsum(dx1 * (n2 * g2))
            accg_ref[1:2, :] += _colsum((dx1 * gate) * n2)
            dmix_ref[...] = _norm_bwd((dx1 * gate) * g2, n2, r2).astype(BF16)

    tok = lambda i, k: (i, 0)
    ktok = lambda i, k: (k, i, 0)
    kw = lambda i, k: (k, 0, 0)
    const2 = lambda i, k: (0, 0)
    return pl.pallas_call(
        body, name="ffn_bwd", grid=(t_all // tm, nk),
        in_specs=[pl.BlockSpec((tm, d), tok),
                  pl.BlockSpec((1, tm, ff), ktok), pl.BlockSpec((1, tm, ff), ktok),
                  pl.BlockSpec((1, ff, d), kw), pl.BlockSpec((1, ff, d), kw), pl.BlockSpec((1, ff, d), kw),
                  pl.BlockSpec((tm, d), tok), pl.BlockSpec((tm, d), tok), pl.BlockSpec((tm, d), tok),
                  pl.BlockSpec((1, MOD_ROWS, d), lambda i, k: (i // nt, 0, 0)),
                  pl.BlockSpec((1, d), const2), pl.BlockSpec((1, d), const2)],
        out_specs=[pl.BlockSpec((1, tm, ff), ktok)] * 2
        + [pl.BlockSpec((tm, d), tok), pl.BlockSpec((tm, d), tok),
           pl.BlockSpec((1, 8, d), lambda i, k: (i // nt, 0, 0)),
           pl.BlockSpec((8, d), const2)],
        out_shape=[jax.ShapeDtypeStruct((nk, t_all, ff), BF16)] * 2
        + [jax.ShapeDtypeStruct((t_all, d), F32), jax.ShapeDtypeStruct((t_all, d), BF16),
           jax.ShapeDtypeStruct((t_all // seq, 8, d), F32),
           jax.ShapeDtypeStruct((8, d), F32)],
        scratch_shapes=[pltpu.VMEM((tm, d), F32)],
        compiler_params=_params(("arbitrary", "arbitrary")),
    )(df, a, b, w_d, w_g, w_u, x1, dy, mix, mod, g_fpre, g_mpost)


def _mixer_bwd(dmix, w_out, pooled, w_pool, pool_scale, seq, tm):
    t_all, d = dmix.shape
    p = pooled.shape[1]
    ng = len(POOL_WINDOWS)

    def body(dm_ref, wo_ref, pooled_ref, wp_ref, ps_ref, do_ref, dpd_ref, dps_ref, dwp_ref):
        i = pl.program_id(0)

        @pl.when(i == 0)
        def _():
            dps_ref[...] = jnp.zeros_like(dps_ref)
            dwp_ref[...] = jnp.zeros_like(dwp_ref)

        dmixin = _dot_nt(dm_ref[...], wo_ref[...])
        do_ref[...] = dmixin[:, :p].astype(BF16)
        for g in range(ng):
            cols = slice(g * POOL_GROUP, (g + 1) * POOL_GROUP)
            dpool = dmixin[:, p + g * POOL_GROUP:p + (g + 1) * POOL_GROUP]
            pooled = pooled_ref[:, cols]
            wpg = wp_ref[g].astype(BF16)
            yg = _dot(pooled, wpg)
            dps_ref[0:1, cols] += _colsum(dpool * yg)
            dyg = (dpool * ps_ref[:, cols]).astype(BF16)
            dwp_ref[g] += _dot_tn(pooled, dyg)
            dpd_ref[:, cols] = _dot_nt(dyg, wpg)

    tok = lambda i: (i, 0)
    const2 = lambda i: (0, 0)
    const3 = lambda i: (0, 0, 0)
    return pl.pallas_call(
        body, name="mixer_bwd", grid=(t_all // tm,),
        in_specs=[pl.BlockSpec((tm, d), tok), pl.BlockSpec((d, d), const2), pl.BlockSpec((tm, p), tok),
                  pl.BlockSpec(w_pool.shape, const3), pl.BlockSpec((1, p), const2)],
        out_specs=[pl.BlockSpec((tm, p), tok), pl.BlockSpec((tm, p), tok),
                   pl.BlockSpec((8, p), const2), pl.BlockSpec(w_pool.shape, const3)],
        out_shape=[jax.ShapeDtypeStruct((t_all, p), BF16), jax.ShapeDtypeStruct((t_all, p), F32),
                   jax.ShapeDtypeStruct((8, p), F32), jax.ShapeDtypeStruct(w_pool.shape, F32)],
        compiler_params=_params(("arbitrary",)),
    )(dmix, w_out, pooled, w_pool, pool_scale)


def _attn_bwd(qn, k, kt, v, do, ltot, seq, tq, tk):
    t_all, w = qn.shape
    nb, nq, ndiag, nkb = t_all // seq, seq // tq, tq // tk, seq // tk
    assert ndiag % 2 == 0, "two key blocks per loop trip"
    rc = ATTN_ROW_CHUNK
    nh = HEADS_PER_BLOCK
    heads = range(nh)

    def body(q_ref, k_ref, kt_ref, v_ref, do_ref, l_ref, up_ref, bf_ref, dq_ref, dk_ref, dv_ref,
             z_buf, dw_buf, ls_buf, hl_buf, upto_buf, g_buf, gb_buf, before_buf, w_buf, dz_buf,
             totl_buf, totg_buf, rem_buf, preg_buf, qnt_buf, dot_buf, dq_t, dk_t, dv_t):
        i = pl.program_id(2)
        nblk = (i + 1) * ndiag

        @pl.when(i == 0)
        def _():
            dk_t[...] = jnp.zeros_like(dk_t)
            dv_t[...] = jnp.zeros_like(dv_t)

        lane = lax.broadcasted_iota(jnp.int32, (1, LANES), 1)
        sub = lax.broadcasted_iota(jnp.int32, (LANES, 1), 0)
        row = lax.broadcasted_iota(jnp.int32, (rc, tk), 0)
        col = lax.broadcasted_iota(jnp.int32, (rc, tk), 1)
        first = lane < HEAD_DIM
        upper = sub < HEAD_DIM
        q2 = q_ref[...]
        do2 = do_ref[...]
        l2 = l_ref[...]
        qs = [jnp.where(first, q2, jnp.zeros_like(q2)), jnp.where(first, jnp.zeros_like(q2), q2)]
        dos = [jnp.where(first, do2, jnp.zeros_like(do2)), jnp.where(first, jnp.zeros_like(do2), do2)]
        for src, dst in ((q2, qnt_buf), (do2, dot_buf)):
            t = src.astype(F32).T
            dst[:, 0:tq] = jnp.where(upper, t, 0.0).astype(BF16)
            dst[:, tq:2 * tq] = jnp.where(upper, 0.0, t).astype(BF16)
        for h in heads:
            rem_buf[h] = jnp.where(first if h == 0 else ~first, l2, pltpu.roll(l2, HEAD_DIM, 1))
        preg_buf[...] = jnp.zeros_like(preg_buf)
        dq_t[...] = jnp.zeros_like(dq_t)
        w_buf[1] = jnp.zeros((nh * tq, tk), BF16)
        dz_buf[1] = jnp.zeros((nh * tq, tk), BF16)

        def causal(c, diag):
            return (col + diag * tk) < (row + c * rc)

        def scores(blk, slot):
            off = pl.multiple_of(blk * tk, tk)
            kj = k_ref[pl.ds(off, tk), :]
            vj = v_ref[pl.ds(off, tk), :]
            for h in heads:
                z_buf[slot, h] = _dot_nt(qs[h], kj)
                dw_buf[slot, h] = _dot_nt(dos[h], vj)

        def gradients(blk, slot):
            off = pl.multiple_of(blk * tk, tk)
            ktj = kt_ref[:, pl.ds(off, tk)]
            zero = jnp.zeros_like(ktj)
            dq_t[...] += (_dot_nt(jnp.where(upper, ktj, zero), dz_buf[slot, 0:tq, :])
                          + _dot_nt(jnp.where(upper, zero, ktj), dz_buf[slot, tq:2 * tq, :]))
            dk_t[blk] += _dot(qnt_buf[...], dz_buf[slot])
            dv_t[blk] += _dot(dot_buf[...], w_buf[slot])

        def softplus_stage(h, slot, diag):
            for c in range(tq // rc):
                rows = slice(c * rc, (c + 1) * rc)
                if _all_masked(c, diag, rc, tk):
                    hl_buf[h, rows, :] = jnp.zeros((rc, LOG_SUM_PASSES * tk), BF16)
                    continue
                nz = z_buf[slot, h, rows, :]
                l1 = jnp.minimum(nz, 0.0) - jnp.log(1.0 + jnp.exp(-jnp.abs(nz)))
                if _some_masked(c, diag, rc, tk):
                    l1 = jnp.where(causal(c, diag), l1, 0.0)
                for s, part in enumerate(_split(l1)[:LOG_SUM_PASSES]):
                    hl_buf[h, rows, s * tk:(s + 1) * tk] = part
                ls_buf[h, rows, :] = l1 - nz
                totl_buf[h, rows, :] = _row_sums(l1)

        def weights_stage(h, slot, diag):
            for c in range(tq // rc):
                rows = slice(c * rc, (c + 1) * rc)
                stacked = slice(h * tq + c * rc, h * tq + (c + 1) * rc)
                if _all_masked(c, diag, rc, tk):
                    w_buf[slot, stacked, :] = jnp.zeros((rc, tk), BF16)
                    gb_buf[h, rows, :] = jnp.zeros((rc, tk), BF16)
                    continue
                wgt = jnp.exp(ls_buf[h, rows, :] + (_across(rem_buf[h, rows, :], tk) - upto_buf[h, rows, :]))
                if _some_masked(c, diag, rc, tk):
                    wgt = jnp.where(causal(c, diag), wgt, 0.0)
                w_buf[slot, stacked, :] = wgt.astype(BF16)
                g = wgt * dw_buf[slot, h, rows, :]
                g_buf[h, rows, :] = g
                gb_buf[h, rows, :] = g.astype(BF16)
                totg_buf[h, rows, :] = _row_sums(g)
                rem_buf[h, rows, :] -= totl_buf[h, rows, :]

        def dscore_stage(h, slot, diag):
            for c in range(tq // rc):
                rows = slice(c * rc, (c + 1) * rc)
                stacked = slice(h * tq + c * rc, h * tq + (c + 1) * rc)
                if _all_masked(c, diag, rc, tk):
                    dz_buf[slot, stacked, :] = jnp.zeros((rc, tk), BF16)
                    continue
                sig = jnp.exp(ls_buf[h, rows, :])
                g = g_buf[h, rows, :]
                dnz = sig * (before_buf[h, rows, :] + _across(preg_buf[h, rows, :], tk)) - g * (1.0 - sig)
                if _some_masked(c, diag, rc, tk):
                    dnz = jnp.where(causal(c, diag), dnz, 0.0)
                dz_buf[slot, stacked, :] = dnz.astype(BF16)
                preg_buf[h, rows, :] += totg_buf[h, rows, :]

        def position(blk, slot, diag, prefetch):
            if prefetch:
                scores(blk + 1, 1 - slot)
            for h in heads:
                softplus_stage(h, slot, diag)
                upto_buf[h] = _dot(hl_buf[h], up_ref[...])
            gradients(jnp.maximum(blk - 1, 0), 1 - slot)
            for h in heads:
                weights_stage(h, slot, diag)
                before_buf[h] = _dot(gb_buf[h], bf_ref[...])
            for h in heads:
                dscore_stage(h, slot, diag)

        scores(0, 0)

        def trip(jj, carry):
            for u in range(2):
                position(2 * jj + u, u, None, True)
            return carry

        lax.fori_loop(0, (i * ndiag) // 2, trip, 0)
        for d in range(ndiag):
            position(i * ndiag + d, d % 2, d, d < ndiag - 1)
        gradients(nblk - 1, 1)
        dq_ref[...] = (dq_t[...].T * NEG_QK_SCALE).astype(BF16)

        @pl.when(i == nq - 1)
        def _():
            for blk in range(nkb):
                dk_ref[blk * tk:(blk + 1) * tk, :] = dk_t[blk].T.astype(BF16)
                dv_ref[blk * tk:(blk + 1) * tk, :] = dv_t[blk].T.astype(BF16)

    qmap = lambda b, hp, i: (b * nq + i, hp)
    kmap = lambda b, hp, i: (b, hp)
    const = lambda b, hp, i: (0, 0)
    return pl.pallas_call(
        body, name="attn_bwd", grid=(nb, w // LANES, nq),
        in_specs=[pl.BlockSpec((tq, LANES), qmap), pl.BlockSpec((seq, LANES), kmap),
                  pl.BlockSpec((LANES, seq), lambda b, hp, i: (hp, b)), pl.BlockSpec((seq, LANES), kmap),
                  pl.BlockSpec((tq, LANES), qmap), pl.BlockSpec((tq, LANES), qmap),
                  pl.BlockSpec((LOG_SUM_PASSES * tk, tk), const), pl.BlockSpec((tk, tk), const)],
        out_specs=[pl.BlockSpec((tq, LANES), qmap), pl.BlockSpec((seq, LANES), kmap), pl.BlockSpec((seq, LANES), kmap)],
        out_shape=[jax.ShapeDtypeStruct((t_all, w), BF16)] * 3,
        scratch_shapes=[pltpu.VMEM((2, nh, tq, tk), F32), pltpu.VMEM((2, nh, tq, tk), F32),
                        pltpu.VMEM((nh, tq, tk), F32), pltpu.VMEM((nh, tq, LOG_SUM_PASSES * tk), BF16),
                        pltpu.VMEM((nh, tq, tk), F32), pltpu.VMEM((nh, tq, tk), F32),
                        pltpu.VMEM((nh, tq, tk), BF16), pltpu.VMEM((nh, tq, tk), F32),
                        pltpu.VMEM((2, nh * tq, tk), BF16), pltpu.VMEM((2, nh * tq, tk), BF16),
                        pltpu.VMEM((nh, tq, LANES), F32), pltpu.VMEM((nh, tq, LANES), F32),
                        pltpu.VMEM((nh, tq, LANES), F32), pltpu.VMEM((nh, tq, LANES), F32),
                        pltpu.VMEM((LANES, nh * tq), BF16), pltpu.VMEM((LANES, nh * tq), BF16),
                        pltpu.VMEM((LANES, tq), F32), pltpu.VMEM((nkb, LANES, tk), F32),
                        pltpu.VMEM((nkb, LANES, tk), F32)],
        compiler_params=_params(("arbitrary", "arbitrary", "arbitrary")),
    )(qn, k, kt, v, do, ltot, _tri_matrix(tk, "upto")[:LOG_SUM_PASSES * tk], _tri_matrix(tk, "before")[:tk])


def _inproj_bwd(dq, dk, dv, dpd, x, dx1, mod, g_pre, w_in, seq, tm):
    t_all, d = x.shape
    nt = seq // tm
    p = dq.shape[1]

    def body(dq_ref, dk_ref, dv_ref, dpd_ref, halo_ref, x_ref, dx1_ref, mod_ref, g_ref, w_ref,
             gx_ref, du_ref, accb_ref, accg_ref):
        i = pl.program_id(0)
        it = i % nt

        @pl.when(i == 0)
        def _():
            accg_ref[...] = jnp.zeros_like(accg_ref)

        @pl.when(it == 0)
        def _():
            accb_ref[...] = jnp.zeros_like(accb_ref)

        dpd = dpd_ref[...]
        pos = it * tm + lax.broadcasted_iota(jnp.int32, (tm, 1), 0)
        cnts = _window_counts(pos)
        halo = jnp.where(it == nt - 1, 0.0, halo_ref[...])
        scaled = []
        halos = []
        for g, win in enumerate(POOL_WINDOWS):
            cols = slice(g * POOL_GROUP, (g + 1) * POOL_GROUP)
            scaled.append(dpd[:, cols] / cnts[g])
            halos.append(halo[:, cols] / float(win))
        ext = jnp.concatenate([jnp.concatenate(scaled, axis=1), jnp.concatenate(halos, axis=1)], axis=0)
        sums = _window_sums(ext, tm, 0, True)
        du = (jnp.concatenate(sums, axis=1) - dpd).astype(BF16)
        du_ref[...] = du
        dh1 = (_dot_nt(dq_ref[...], w_ref[0]) + _dot_nt(dk_ref[...], w_ref[1])
               + _dot_nt(dv_ref[...], w_ref[2]) + _dot_nt(du, w_ref[3]))
        xf = x_ref[...]
        r1 = _rms(xf)
        n1 = xf * r1
        g1 = g_ref[...]
        scale1 = 1.0 + mod_ref[0, 1:2, :]
        accb_ref[0, 0:1, :] += _colsum(dh1)
        accb_ref[0, 1:2, :] += _colsum(dh1 * (n1 * g1))
        accg_ref[0:1, :] += _colsum((dh1 * scale1) * n1)
        gx_ref[...] = dx1_ref[...] + _norm_bwd((dh1 * scale1) * g1, n1, r1)

    tok = lambda i: (i, 0)
    const2 = lambda i: (0, 0)
    hb = tm // HALO
    last = t_all // HALO - 1
    return pl.pallas_call(
        body, name="inproj_bwd", grid=(t_all // tm,),
        in_specs=[pl.BlockSpec((tm, p), tok), pl.BlockSpec((tm, p), tok), pl.BlockSpec((tm, p), tok),
                  pl.BlockSpec((tm, p), tok),
                  pl.BlockSpec((HALO, p), lambda i: (jnp.minimum((i + 1) * hb, last), 0)),
                  pl.BlockSpec((tm, d), tok), pl.BlockSpec((tm, d), tok),
                  pl.BlockSpec((1, MOD_ROWS, d), lambda i: (i // nt, 0, 0)),
                  pl.BlockSpec((1, d), const2),
                  pl.BlockSpec((N_CHIPS, d, p), lambda i: (0, 0, 0))],
        out_specs=[pl.BlockSpec((tm, d), tok), pl.BlockSpec((tm, p), tok),
                   pl.BlockSpec((1, 8, d), lambda i: (i // nt, 0, 0)),
                   pl.BlockSpec((8, d), const2)],
        out_shape=[jax.ShapeDtypeStruct((t_all, d), F32), jax.ShapeDtypeStruct((t_all, p), BF16),
                   jax.ShapeDtypeStruct((t_all // seq, 8, d), F32),
                   jax.ShapeDtypeStruct((8, d), F32)],
        compiler_params=_params(("arbitrary",)),
    )(dq, dk, dv, dpd, dpd, x, dx1, mod, g_pre, w_in)


def _tn_matmul(x, ys, nk, bt, name):
    t_all = x.shape[-2]
    m = x.shape[-1]
    ny = len(ys)

    def spec(arr):
        if arr.ndim == 3:
            return pl.BlockSpec((1, bt, arr.shape[-1]), lambda k, t: (k, t, 0))
        return pl.BlockSpec((bt, arr.shape[-1]), lambda k, t: (t, 0))

    def tile(ref):
        return ref[0] if len(ref.shape) == 3 else ref[...]

    def body(*refs):
        x_ref, y_refs, o_refs = refs[0], refs[1:1 + ny], refs[1 + ny:]
        t = pl.program_id(1)
        xt = tile(x_ref)
        for y_ref, o_ref in zip(y_refs, o_refs):
            part = _dot_tn(xt, tile(y_ref))

            @pl.when(t == 0)
            def _(o_ref=o_ref, part=part):
                o_ref[0] = part

            @pl.when(t > 0)
            def _(o_ref=o_ref, part=part):
                o_ref[0] += part

    return pl.pallas_call(
        body, name=name, grid=(nk, t_all // bt),
        in_specs=[spec(x)] + [spec(y) for y in ys],
        out_specs=[pl.BlockSpec((1, m, y.shape[-1]), lambda k, t: (k, 0, 0)) for y in ys],
        out_shape=[jax.ShapeDtypeStruct((nk, m, y.shape[-1]), F32) for y in ys],
        compiler_params=_params(("arbitrary", "arbitrary")),
    )(x, *ys)


def _cond_fwd(c_all, w_q, b_q, bn):
    nrow, d = c_all.shape
    ncol = w_q.shape[1]

    def body(c_ref, w_ref, b_ref, sc_ref, mod_ref):
        cf = c_ref[...]
        sc = cf * _sigmoid(cf)
        sc_ref[...] = sc
        shi, slo = _split(sc)
        whi, wlo = _split(w_ref[...])
        mod_ref[...] = (_dot(shi, whi) + _dot(shi, wlo) + _dot(slo, whi)) + b_ref[...]

    return pl.pallas_call(
        body, name="cond_fwd", grid=(ncol // bn,),
        in_specs=[pl.BlockSpec((nrow, d), lambda n: (0, 0)), pl.BlockSpec((d, bn), lambda n: (0, n)),
                  pl.BlockSpec((1, bn), lambda n: (0, n))],
        out_specs=[pl.BlockSpec((nrow, d), lambda n: (0, 0)), pl.BlockSpec((nrow, bn), lambda n: (0, n))],
        out_shape=[jax.ShapeDtypeStruct((nrow, d), F32), jax.ShapeDtypeStruct((nrow, ncol), F32)],
        compiler_params=_params(("arbitrary",)),
    )(c_all, w_q, b_q)


def _cond_bwd(sc_all, dmod_q, bn):
    nrow, d = sc_all.shape
    ncol = dmod_q.shape[1]

    def body(sc_ref, dm_ref, gw_ref):
        shi, slo = _split(sc_ref[...])
        dhi, dlo = _split(dm_ref[...])
        gw_ref[...] = _dot_tn(shi, dhi) + _dot_tn(shi, dlo) + _dot_tn(slo, dhi)

    return pl.pallas_call(
        body, name="cond_bwd", grid=(ncol // bn,),
        in_specs=[pl.BlockSpec((nrow, d), lambda n: (0, 0)), pl.BlockSpec((nrow, bn), lambda n: (0, n))],
        out_specs=pl.BlockSpec((d, bn), lambda n: (0, n)),
        out_shape=jax.ShapeDtypeStruct((d, ncol), F32),
        compiler_params=_params(("arbitrary",)),
    )(sc_all, dmod_q)


def _row_block(rows, cols, budget=1 << 18):
    best = None
    for br in range(8, rows + 1, 8):
        if rows % br == 0 and br * cols <= budget:
            best = br
    return best if best is not None else rows


def _adamw(w, g, m, v, name):
    rows, cols = w.shape
    br = _row_block(rows, cols)
    c1 = 1.0 - ADAM_B1 ** ADAM_STEP
    c2 = 1.0 - ADAM_B2 ** ADAM_STEP

    def body(w_ref, g_ref, m_ref, v_ref, d_ref, nm_ref, nv_ref):
        gf = g_ref[...]
        m2 = ADAM_B1 * m_ref[...] + (1.0 - ADAM_B1) * gf
        v2 = ADAM_B2 * v_ref[...] + (1.0 - ADAM_B2) * (gf * gf)
        nm_ref[...] = m2
        nv_ref[...] = v2
        d_ref[...] = -ADAM_LR * ((m2 / c1) / (jnp.sqrt(v2 / c2) + ADAM_EPS) + ADAM_WD * w_ref[...])

    blk = pl.BlockSpec((br, cols), lambda i: (i, 0))
    return pl.pallas_call(
        body, name=name, grid=(rows // br,),
        in_specs=[blk] * 4, out_specs=[blk] * 3,
        out_shape=[jax.ShapeDtypeStruct((rows, cols), F32)] * 3,
        compiler_params=_params(("arbitrary",)),
    )(w, g, m, v)


def _all_gather(x_shard, name):
    m_per, n = x_shard.shape

    def body(x_ref, out_ref, send_sems, recv_sems, local_sem):
        x, y, c = _position()
        me, sibling = (x, y, c), (x, y, 1 - c)
        chips = [(1 - x, y), (x, 1 - y), (1 - x, 1 - y)]

        def rows(px, py, pc):
            return out_ref.at[pl.ds((4 * px + 2 * py + pc) * m_per, m_per), :]

        def copy(k, block, to, src=None):
            return pltpu.make_async_remote_copy(
                src_ref=rows(*block) if src is None else src, dst_ref=rows(*block),
                send_sem=send_sems.at[k], recv_sem=recv_sems.at[k], device_id=to, device_id_type=MESH)

        mine = pltpu.make_async_copy(x_ref, rows(*me), local_sem)
        mine.start()
        first = [copy(0, me, sibling, src=x_ref)]
        first += [copy(1 + j, me, (*chip, c), src=x_ref) for j, chip in enumerate(chips)]
        for cp in first:
            cp.start()
        passed = [copy(4 + j, (*chip, c), sibling) for j, chip in enumerate(chips)]
        for j, chip in enumerate(chips):
            copy(1 + j, (*chip, c), me).wait_recv()
            passed[j].start()
        copy(0, sibling, me).wait_recv()
        for j, chip in enumerate(chips):
            copy(4 + j, (*chip, 1 - c), me).wait_recv()
        for cp in first + passed:
            cp.wait_send()
        mine.wait()

    return pl.pallas_call(
        body, name=name,
        out_shape=jax.ShapeDtypeStruct((N_DEV * m_per, n), x_shard.dtype),
        in_specs=[pl.BlockSpec(memory_space=pltpu.VMEM)],
        out_specs=pl.BlockSpec(memory_space=pltpu.VMEM),
        scratch_shapes=[pltpu.SemaphoreType.DMA((7,)), pltpu.SemaphoreType.DMA((7,)), pltpu.SemaphoreType.DMA],
        compiler_params=pltpu.CompilerParams(vmem_limit_bytes=VMEM_LIMIT),
    )(x_shard)


_ANY = pl.BlockSpec(memory_space=pl.ANY)


def _place_quarters(place, quarters):
    steps = 2

    def body(place_ref, *refs):
        n = len(refs) // 2
        for w_ref, o_ref in zip(refs[:n], refs[n:]):
            o_ref[0] = w_ref[...].astype(BF16)

    return pl.pallas_call(
        body, name="place_quarters",
        grid_spec=pltpu.PrefetchScalarGridSpec(
            num_scalar_prefetch=1, grid=(steps,),
            in_specs=[pl.BlockSpec((q.shape[0] // steps, q.shape[1]), lambda r, place_ref: (r, 0)) for q in quarters],
            out_specs=[pl.BlockSpec((1, q.shape[0] // steps, q.shape[1]), lambda r, place_ref: (place_ref[0], r, 0))
                       for q in quarters]),
        out_shape=[jax.ShapeDtypeStruct((N_CHIPS,) + q.shape, BF16) for q in quarters],
        compiler_params=_params(("arbitrary",)),
    )(place, *quarters)


def _gather_weights(placed):
    n = len(placed)
    shapes = [b.shape[1:] for b in placed]

    def body(*refs):
        g_refs = refs[n:2 * n]
        send_sems, recv_sems = refs[2 * n:]
        x, y, c = _position()
        sibling = (x, y, 1 - c)
        chips = [(1 - x, y), (x, 1 - y), (1 - x, 1 - y)]
        mine = 2 * x + y

        def half(a, which):
            hr = shapes[a][0] // 2
            return pl.ds(which * hr, hr)

        def over_ici(a, p, slot):
            ref = g_refs[a].at[slot, half(a, c), :]
            return pltpu.make_async_remote_copy(
                src_ref=ref, dst_ref=ref,
                send_sem=send_sems.at[6 * a + p], recv_sem=recv_sems.at[6 * a + p],
                device_id=(*chips[p], c), device_id_type=MESH)

        def over_d2d(a, p, slot, which):
            ref = g_refs[a].at[slot, half(a, which), :]
            return pltpu.make_async_remote_copy(
                src_ref=ref, dst_ref=ref,
                send_sem=send_sems.at[6 * a + 3 + p], recv_sem=recv_sems.at[6 * a + 3 + p],
                device_id=sibling, device_id_type=MESH)

        sends = []
        for a in range(n):
            for p in range(3):
                cp = over_ici(a, p, mine)
                cp.start()
                sends.append(cp)
        for a in range(n):
            for p, (cx, cy) in enumerate(chips):
                slot = 2 * cx + cy
                over_ici(a, p, slot).wait_recv()
                cp = over_d2d(a, p, slot, c)
                cp.start()
                sends.append(cp)
        for a in range(n):
            for p, (cx, cy) in enumerate(chips):
                over_d2d(a, p, 2 * cx + cy, 1 - c).wait_recv()
        for cp in sends:
            cp.wait_send()

    return pl.pallas_call(
        body, name="gather_weights",
        out_shape=[jax.ShapeDtypeStruct(b.shape, BF16) for b in placed],
        in_specs=[_ANY] * n, out_specs=[_ANY] * n,
        input_output_aliases={a: a for a in range(n)},
        scratch_shapes=[pltpu.SemaphoreType.DMA((6 * n,)), pltpu.SemaphoreType.DMA((6 * n,))],
    )(*placed)


_HBM = pl.BlockSpec(memory_space=pltpu.HBM)
_SEM = pl.BlockSpec(memory_space=pltpu.SEMAPHORE)
_EFFECT = pltpu.SideEffectType.DATAFLOW_SIDE_EFFECTING


def _quarter_halves(shapes, a, which):
    hr = shapes[a][0] // 2
    return pl.ds(which * hr, hr)


def _gather_start(placed, after):
    n = len(placed)
    m = len(after)
    shapes = [b.shape[1:] for b in placed]

    def body(*refs):
        g_refs = refs[:n]
        send_sems, recv_sems = refs[n + m], refs[n + m + 1]
        token = refs[2 * n + m + 2]
        x, y, c = _position()
        chips = [(1 - x, y), (x, 1 - y), (1 - x, 1 - y)]
        mine = 2 * x + y
        for a in range(n):
            ref = g_refs[a].at[mine, _quarter_halves(shapes, a, c), :]
            for p in range(3):
                pltpu.make_async_remote_copy(
                    src_ref=ref, dst_ref=ref, send_sem=send_sems.at[3 * a + p], recv_sem=recv_sems.at[3 * a + p],
                    device_id=(*chips[p], c), device_id_type=MESH).start()
        token[...] = jnp.zeros_like(token)

    out = pl.pallas_call(
        body, name="gather_start",
        out_shape=(pltpu.SemaphoreType.DMA((3 * n,)), pltpu.SemaphoreType.DMA((3 * n,)),
                   *[pltpu.HBM(b.shape, b.dtype) for b in placed], jax.ShapeDtypeStruct((8, LANES), F32)),
        in_specs=[_HBM] * n + [_ANY] * m,
        out_specs=(_SEM, _SEM, *[_HBM] * n, pl.BlockSpec(memory_space=pltpu.VMEM)),
        input_output_aliases={a: 2 + a for a in range(n)},
        compiler_params=pltpu.CompilerParams(has_side_effects=_EFFECT),
    )(*[pltpu.with_memory_space_constraint(b, pltpu.HBM) for b in placed], *after)
    return out[0], out[1], list(out[2:2 + n]), out[2 + n]


def _gather_wait(send_sems, recv_sems, thru, after):
    n = len(thru)
    shapes = [b.shape[1:] for b in thru]

    def body(*refs):
        g_refs = refs[:n]
        send_sems, recv_sems = refs[n], refs[n + 1]
        x, y, c = _position()
        chips = [(1 - x, y), (x, 1 - y), (1 - x, 1 - y)]
        mine = 2 * x + y
        for a in range(n):
            rows = _quarter_halves(shapes, a, c)
            for p, (cx, cy) in enumerate(chips):
                copy = pltpu.make_async_remote_copy(
                    src_ref=g_refs[a].at[mine, rows, :], dst_ref=g_refs[a].at[2 * cx + cy, rows, :],
                    send_sem=send_sems.at[3 * a + p], recv_sem=recv_sems.at[3 * a + p],
                    device_id=(cx, cy, c), device_id_type=MESH)
                copy.wait_send()
                copy.wait_recv()

    return pl.pallas_call(
        body, name="gather_wait",
        out_shape=[pltpu.HBM(b.shape, b.dtype) for b in thru],
        in_specs=[_HBM] * n + [_SEM, _SEM, _ANY], out_specs=[_HBM] * n,
        input_output_aliases={a: a for a in range(n)},
        compiler_params=pltpu.CompilerParams(has_side_effects=_EFFECT),
    )(*thru, send_sems, recv_sems, after)


def _gather_forward(bufs):
    n = len(bufs)
    shapes = [b.shape[1:] for b in bufs]

    def body(*refs):
        g_refs = refs[n:2 * n]
        send_sems, recv_sems = refs[2 * n:]
        x, y, c = _position()
        chips = [(1 - x, y), (x, 1 - y), (1 - x, 1 - y)]

        def over_d2d(a, p, which):
            cx, cy = chips[p]
            ref = g_refs[a].at[2 * cx + cy, _quarter_halves(shapes, a, which), :]
            return pltpu.make_async_remote_copy(
                src_ref=ref, dst_ref=ref, send_sem=send_sems.at[3 * a + p], recv_sem=recv_sems.at[3 * a + p],
                device_id=(x, y, 1 - c), device_id_type=MESH)

        sends = [over_d2d(a, p, c) for a in range(n) for p in range(3)]
        for cp in sends:
            cp.start()
        for a in range(n):
            for p in range(3):
                over_d2d(a, p, 1 - c).wait_recv()
        for cp in sends:
            cp.wait_send()

    return pl.pallas_call(
        body, name="gather_forward",
        out_shape=[jax.ShapeDtypeStruct(b.shape, BF16) for b in bufs],
        in_specs=[_ANY] * n, out_specs=[_ANY] * n,
        input_output_aliases={a: a for a in range(n)},
        scratch_shapes=[pltpu.SemaphoreType.DMA((3 * n,)), pltpu.SemaphoreType.DMA((3 * n,))],
    )(*bufs)


def _sibling_exchange(grads, tag):
    n = len(grads)
    shapes = [g.shape for g in grads]

    def body(*refs):
        g_refs, x_refs = refs[:n], refs[n:2 * n]
        send_sems, recv_sems = refs[2 * n:]
        x, y, c = _position()
        copies = []
        for a in range(n):
            hr = shapes[a][1] // 2
            cp = pltpu.make_async_remote_copy(
                src_ref=g_refs[a].at[:, pl.ds((1 - c) * hr, hr), :], dst_ref=x_refs[a],
                send_sem=send_sems.at[a], recv_sem=recv_sems.at[a],
                device_id=(x, y, 1 - c), device_id_type=MESH)
            cp.start()
            copies.append(cp)
        for cp in copies:
            cp.wait()

    return pl.pallas_call(
        body, name="grad_sibling_exchange_" + tag,
        out_shape=[jax.ShapeDtypeStruct((s[0], s[1] // 2, s[2]), F32) for s in shapes],
        in_specs=[_ANY] * n, out_specs=[_ANY] * n,
        scratch_shapes=[pltpu.SemaphoreType.DMA((n,)), pltpu.SemaphoreType.DMA((n,))],
    )(*grads)


def _chip_sums(core, grads, theirs, tag):
    n = len(grads)

    def body(core_ref, *refs):
        g_refs, t_refs, o_refs = refs[:n], refs[n:2 * n], refs[2 * n:]
        for g_ref, t_ref, o_ref in zip(g_refs, t_refs, o_refs):
            o_ref[...] = (g_ref[...] + t_ref[...]).astype(BF16)

    in_specs = [pl.BlockSpec((1, g.shape[1] // 2, g.shape[2]), lambda k, core_ref: (k, core_ref[0], 0)) for g in grads]
    in_specs += [pl.BlockSpec((1,) + t.shape[1:], lambda k, core_ref: (k, 0, 0)) for t in theirs]
    return pl.pallas_call(
        body, name="grad_chip_sums_" + tag,
        grid_spec=pltpu.PrefetchScalarGridSpec(
            num_scalar_prefetch=1, grid=(N_CHIPS,), in_specs=in_specs,
            out_specs=[pl.BlockSpec((1,) + t.shape[1:], lambda k, core_ref: (k, 0, 0)) for t in theirs]),
        out_shape=[jax.ShapeDtypeStruct(t.shape, BF16) for t in theirs],
        compiler_params=_params(("arbitrary",)),
    )(core, *grads, *theirs)


def _chip_exchange_start(sums, after, tag):
    n = len(sums)
    m = len(after)
    lands = [lax.empty((3,) + s.shape[1:], BF16) for s in sums]

    def body(*refs):
        s_refs, y_refs = refs[:n], refs[n:2 * n]
        send_sems, recv_sems = refs[2 * n + m], refs[2 * n + m + 1]
        token = refs[4 * n + m + 2]
        x, y, c = _position()
        chips = [(1 - x, y), (x, 1 - y), (1 - x, 1 - y)]
        for a in range(n):
            for p, (cx, cy) in enumerate(chips):
                pltpu.make_async_remote_copy(
                    src_ref=s_refs[a].at[2 * cx + cy], dst_ref=y_refs[a].at[p],
                    send_sem=send_sems.at[3 * a + p], recv_sem=recv_sems.at[3 * a + p],
                    device_id=(cx, cy, c), device_id_type=MESH).start()
        token[...] = jnp.zeros_like(token)

    both = list(sums) + lands
    out = pl.pallas_call(
        body, name="grad_chip_exchange_start_" + tag,
        out_shape=(pltpu.SemaphoreType.DMA((3 * n,)), pltpu.SemaphoreType.DMA((3 * n,)),
                   *[pltpu.HBM(b.shape, b.dtype) for b in both], jax.ShapeDtypeStruct((8, LANES), F32)),
        in_specs=[_HBM] * (2 * n) + [_ANY] * m,
        out_specs=(_SEM, _SEM, *[_HBM] * (2 * n), pl.BlockSpec(memory_space=pltpu.VMEM)),
        input_output_aliases={a: 2 + a for a in range(2 * n)},
        compiler_params=pltpu.CompilerParams(has_side_effects=_EFFECT),
    )(*[pltpu.with_memory_space_constraint(b, pltpu.HBM) for b in both], *after)
    return out[0], out[1], list(out[2:2 + n]), list(out[2 + n:2 + 2 * n]), out[2 + 2 * n]


def _chip_exchange_wait(send_sems, recv_sems, sums, lands, after, tag):
    n = len(sums)

    def body(*refs):
        s_refs, y_refs = refs[:n], refs[n:2 * n]
        send_sems, recv_sems = refs[2 * n], refs[2 * n + 1]
        x, y, c = _position()
        chips = [(1 - x, y), (x, 1 - y), (1 - x, 1 - y)]
        for a in range(n):
            for p, (cx, cy) in enumerate(chips):
                copy = pltpu.make_async_remote_copy(
                    src_ref=s_refs[a].at[2 * cx + cy], dst_ref=y_refs[a].at[p],
                    send_sem=send_sems.at[3 * a + p], recv_sem=recv_sems.at[3 * a + p],
                    device_id=(cx, cy, c), device_id_type=MESH)
                copy.wait_send()
                copy.wait_recv()

    both = list(sums) + list(lands)
    out = pl.pallas_call(
        body, name="grad_chip_exchange_wait_" + tag,
        out_shape=[pltpu.HBM(b.shape, b.dtype) for b in both],
        in_specs=[_HBM] * (2 * n) + [_SEM, _SEM, _ANY], out_specs=[_HBM] * (2 * n),
        input_output_aliases={a: a for a in range(2 * n)},
        compiler_params=pltpu.CompilerParams(has_side_effects=_EFFECT),
    )(*both, send_sems, recv_sems, after)
    return list(out[:n]), list(out[n:])


def _total_sums(place, sums, parts, after, tag):
    n = len(parts)
    m = len(after)
    steps = 2

    def body(place_ref, *refs):
        for s_ref, y_ref, o_ref in zip(refs[:n], refs[n:2 * n], refs[2 * n + m:]):
            o_ref[0] = ((s_ref[0].astype(F32) + y_ref[0].astype(F32)) + y_ref[1].astype(F32)) + y_ref[2].astype(F32)

    def step_rows(pt):
        return pt.shape[1] // steps

    in_specs = [pl.BlockSpec((1, step_rows(s), s.shape[2]), lambda r, place_ref: (place_ref[0], r, 0)) for s in sums]
    in_specs += [pl.BlockSpec((3, step_rows(pt), pt.shape[2]), lambda r, place_ref: (0, r, 0)) for pt in parts]
    in_specs += [_ANY] * m
    return pl.pallas_call(
        body, name="grad_total_sums_" + tag,
        grid_spec=pltpu.PrefetchScalarGridSpec(
            num_scalar_prefetch=1, grid=(steps,), in_specs=in_specs,
            out_specs=[pl.BlockSpec((1, step_rows(pt), pt.shape[2]), lambda r, place_ref: (place_ref[1], r, 0))
                       for pt in parts]),
        out_shape=[jax.ShapeDtypeStruct((2,) + pt.shape[1:], F32) for pt in parts],
        compiler_params=_params(("arbitrary",)),
    )(place, *sums, *parts, *after)


def _sibling_share(halves, tag):
    n = len(halves)

    def body(*refs):
        f_refs = refs[n:2 * n]
        send_sems, recv_sems = refs[2 * n:]
        x, y, c = _position()
        copies = []
        for a in range(n):
            cp = pltpu.make_async_remote_copy(
                src_ref=f_refs[a].at[c], dst_ref=f_refs[a].at[c], send_sem=send_sems.at[a], recv_sem=recv_sems.at[a],
                device_id=(x, y, 1 - c), device_id_type=MESH)
            cp.start()
            copies.append(cp)
        for a, cp in enumerate(copies):
            cp.wait_send()
            pltpu.make_async_remote_copy(
                src_ref=f_refs[a].at[1 - c], dst_ref=f_refs[a].at[1 - c], send_sem=send_sems.at[a],
                recv_sem=recv_sems.at[a], device_id=(x, y, c), device_id_type=MESH).wait_recv()

    return pl.pallas_call(
        body, name="grad_sibling_share_" + tag,
        out_shape=[jax.ShapeDtypeStruct(h.shape, F32) for h in halves],
        in_specs=[_ANY] * n, out_specs=[_ANY] * n,
        input_output_aliases={a: a for a in range(n)},
        scratch_shapes=[pltpu.SemaphoreType.DMA((n,)), pltpu.SemaphoreType.DMA((n,))],
    )(*halves)


def _group_sum(stacked, nrow, name):
    total, n = stacked.shape
    groups = total // nrow

    def body(g_ref, o_ref):
        acc = g_ref[0:nrow, :]
        for grp in range(1, groups):
            acc = acc + g_ref[grp * nrow:(grp + 1) * nrow, :]
        o_ref[...] = acc

    return pl.pallas_call(
        body, name=name,
        out_shape=jax.ShapeDtypeStruct((nrow, n), F32),
        compiler_params=pltpu.CompilerParams(vmem_limit_bytes=VMEM_LIMIT),
    )(stacked)


def _local_step(xt, tgt, mod, gains, w_pool, pool_scale, w_in, later_weights, on_ffn_grads, seq):
    g_mpre, g_mpost, g_fpre, g_fpost = gains
    d = xt.shape[1]
    tm, tq = min(TOKEN_TILE, seq), min(ATTN_TILE, seq)

    h1, qn, k, v, u, kt, vt = _prenorm_proj(xt, mod, g_mpre, w_in, seq, tm)
    tk = min(ATTN_KEY_TILE, tq // 2)
    o, ltot = _attn_fwd(qn, k, vt, seq, tq, tk)
    w_out, w_g, w_u, w_d = later_weights(o)
    w_out2 = w_out.reshape(d, d)
    pooled, mixin, mix, x1, h2 =_mixer_post(u, o, xt, mod, g_mpost, g_fpre, w_pool, pool_scale, w_out2, seq, tm)
    a, b, fin, dy, df, loss_blk, accb4, accg4 = _ffn_fwd(h2, w_g, w_u, w_d, x1, tgt, mod, g_fpost, seq, tm)
    da, db, dx1, dmix, accb5, accg5 = _ffn_bwd(df, a, b, w_d, w_g, w_u, x1, dy, mix, mod, g_fpre, g_mpost, seq, tm)
    bt = min(GRAD_TOKEN_TILE, xt.shape[0])
    (g_g,) = _tn_matmul(da, [h2], w_g.shape[0], bt, "grad_w_gate")
    (g_u,) = _tn_matmul(db, [h2], w_u.shape[0], bt, "grad_w_up")
    (g_d,) = _tn_matmul(fin, [df], w_d.shape[0], bt, "grad_w_down")
    token = on_ffn_grads([g_g, g_u, g_d])
    do, dpd, dps, dwp = _mixer_bwd(dmix, w_out2, pooled, w_pool, pool_scale + token, seq, tm)
    dq, dk, dv = _attn_bwd(qn, k, kt, v, do, ltot, seq, tq, tk)
    gx, du, accb8, accg8 = _inproj_bwd(dq, dk, dv, dpd, xt, dx1, mod, g_mpre, w_in, seq, tm)

    g_in = jnp.concatenate(_tn_matmul(h1, [dq, dk, dv, du], 1, bt, "grad_w_in"), axis=0)
    g_out = _tn_matmul(mixin, [dmix], 1, bt, "grad_w_out")[0].reshape(w_out.shape)

    dmod = jnp.stack([accb8[:, 0], accb8[:, 1], accb5[:, 2], accb5[:, 0], accb5[:, 1], accb4[:, 0]], axis=1)
    dgain = jnp.stack([accg8[0], accg5[1], accg5[0], accg4[0]], axis=0)
    return loss_blk, gx, [g_in, g_out, g_g, g_u, g_d], dmod, dgain, dps[0:1], dwp


def kernel(x, c, w_cond, b_cond, g_mix_pre, g_mix_post, w_in, w_pool, pool_scale, w_out, g_ffn_pre, g_ffn_post, w_gate, w_up, w_down, loss_target, m_w_cond, m_b_cond, m_g_mix_pre, m_g_mix_post, m_w_in, m_w_pool, m_pool_scale, m_w_out, m_g_ffn_pre, m_g_ffn_post, m_w_gate, m_w_up, m_w_down, v_w_cond, v_b_cond, v_g_mix_pre, v_g_mix_post, v_w_in, v_w_pool, v_pool_scale, v_w_out, v_g_ffn_pre, v_g_ffn_post, v_w_gate, v_w_up, v_w_down):
    xi, yi, ci = _position()
    chip = 2 * xi + yi
    dev = 4 * xi + 2 * yi + ci
    nb, seq, d = x.shape
    t_all = nb * seq
    xt = x.reshape(t_all, d)
    tgt = loss_target.reshape(t_all, d)
    ncol = w_cond.shape[2]
    pw = pool_scale.shape[1]

    c_pad = jnp.concatenate([c, jnp.zeros((8 - nb, d), F32)], axis=0)
    c_all = _all_gather(c_pad, "gather_c").reshape(N_DEV, 8, d)[:, :nb].reshape(N_DEV * nb, d)
    b_q = lax.dynamic_slice(b_cond, (0, chip * ncol), (1, ncol))
    sc_all, mod_q = _cond_fwd(c_all, w_cond[0], b_q, 512)
    mod_parts = _all_gather(mod_q, "gather_mod").reshape(N_DEV, N_DEV * nb, ncol)
    mod_rows = lax.dynamic_slice(mod_parts, (0, dev * nb, 0), (N_DEV, nb, ncol))[0::2]
    mod = jnp.transpose(mod_rows, (1, 0, 2)).reshape(nb, N_MOD, d)
    mod = jnp.concatenate([mod, jnp.zeros((nb, MOD_ROWS - N_MOD, d), F32)], axis=1)

    place = jnp.stack([chip, ci]).astype(jnp.int32)
    turned = lambda t: jnp.swapaxes(t[0], 0, 1)
    placed = _place_quarters(place, [w_in[0], w_out[0], turned(w_gate), turned(w_up), w_down[0]])
    (w_in_all,) = _gather_weights(placed[:1])
    send_sems, recv_sems, in_flight, token = _gather_start(placed[1:], [mod, w_in_all])
    mod = mod + token[0:1, 0:1]

    def later_weights(after):
        return _gather_forward(_gather_wait(send_sems, recv_sems, in_flight, after))

    ffn_split = []

    def on_ffn_grads(ffn_grads):
        theirs = _sibling_exchange(ffn_grads, "ffn")
        ffn_split.extend(_chip_exchange_start(_chip_sums(place[1:], ffn_grads, theirs, "ffn"), [], "ffn"))
        return ffn_split[4][0:1, 0:1]

    gains = (g_mix_pre, g_mix_post, g_ffn_pre, g_ffn_post)
    loss_blk, gx, grads, dmod, dgain, dps, dwp = _local_step(
        xt, tgt, mod, gains, w_pool[0], pool_scale, w_in_all, later_weights, on_ffn_grads, seq)

    sums_ffn, parts_ffn = _chip_exchange_wait(*ffn_split[:4], gx, "ffn")

    wp_rows = dwp.size // d
    loss_row = 2 * N_MOD + 4 + 1
    pad_rows = 24 - (loss_row + 1)
    payload = jnp.concatenate([
        dmod.reshape(nb * N_MOD, d), dgain,
        jnp.concatenate([dps, jnp.zeros((1, d - pw), F32)], axis=1),
        jnp.concatenate([loss_blk[0:1], jnp.zeros((1, d - LANES), F32)], axis=1),
        jnp.zeros((pad_rows, d), F32), dwp.reshape(wp_rows, d)], axis=0)
    prow = payload.shape[0]
    gathered = _all_gather(payload, "gather_small")
    summed = _group_sum(gathered, prow, "small_device_sum")
    loss = summed[loss_row, 0]
    dmod_all = gathered.reshape(N_DEV, prow, d)[:, :nb * N_MOD].reshape(N_DEV * nb, N_MOD * d)
    g_b_cond = _group_sum(dmod_all, 1, "grad_b_cond")
    dmod_q = lax.dynamic_slice(dmod_all, (0, chip * ncol), (N_DEV * nb, ncol))
    g_w_cond = _cond_bwd(sc_all, dmod_q, 512)
    first_gain = 2 * N_MOD
    g_gains = [summed[first_gain + r:first_gain + r + 1] for r in range(4)]
    g_pool_scale = summed[first_gain + 4:first_gain + 5, :pw]
    g_w_pool = summed[24:24 + wp_rows].reshape(w_pool.shape[1] * w_pool.shape[2], w_pool.shape[3])

    theirs = _sibling_exchange(grads[:2], "mix")
    mix_split = _chip_exchange_start(_chip_sums(place[1:], grads[:2], theirs, "mix"), [gathered], "mix")
    unfold = lambda halves: [g.reshape(2 * g.shape[1], g.shape[2]) for g in halves]
    g_ffn = unfold(_sibling_share(_total_sums(place, sums_ffn, parts_ffn, [mix_split[4]], "ffn"), "ffn"))

    flat_pool = lambda t: t.reshape(g_w_pool.shape)
    results = {}

    def update(name, w2, g2, m2, v2, shape):
        delta, new_m, new_v = _adamw(w2, g2, m2, v2, "adamw_" + name)
        back = (lambda t: jnp.swapaxes(t, 0, 1)[None]) if shape is None else (lambda t: t.reshape(shape))
        results[name] = [back(t) for t in (g2, delta, new_m, new_v)]
        return delta

    done = [update("w_gate", turned(w_gate), g_ffn[0], turned(m_w_gate), turned(v_w_gate), None),
            update("w_up", turned(w_up), g_ffn[1], turned(m_w_up), turned(v_w_up), None),
            update("w_down", w_down[0], g_ffn[2], m_w_down[0], v_w_down[0], w_down.shape),
            update("w_cond", w_cond[0], g_w_cond, m_w_cond[0], v_w_cond[0], w_cond.shape)]
    update("b_cond", b_cond, g_b_cond, m_b_cond, v_b_cond, b_cond.shape)
    update("g_mix_pre", g_mix_pre, g_gains[0], m_g_mix_pre, v_g_mix_pre, g_mix_pre.shape)
    update("g_mix_post", g_mix_post, g_gains[1], m_g_mix_post, v_g_mix_post, g_mix_post.shape)
    update("w_pool", flat_pool(w_pool), g_w_pool, flat_pool(m_w_pool), flat_pool(v_w_pool), w_pool.shape)
    update("pool_scale", pool_scale, g_pool_scale, m_pool_scale, v_pool_scale, pool_scale.shape)
    update("g_ffn_pre", g_ffn_pre, g_gains[2], m_g_ffn_pre, v_g_ffn_pre, g_ffn_pre.shape)
    update("g_ffn_post", g_ffn_post, g_gains[3], m_g_ffn_post, v_g_ffn_post, g_ffn_post.shape)

    sums_mix, parts_mix = _chip_exchange_wait(*mix_split[:4], done[-1], "mix")
    g_mix = unfold(_sibling_share(_total_sums(place, sums_mix, parts_mix, done[:3], "mix"), "mix"))
    update("w_in", w_in[0], g_mix[0], m_w_in[0], v_w_in[0], w_in.shape)
    update("w_out", w_out[0], g_mix[1], m_w_out[0], v_w_out[0], w_out.shape)

    names = ("w_cond", "b_cond", "g_mix_pre", "g_mix_post", "w_in", "w_pool", "pool_scale", "w_out",
             "g_ffn_pre", "g_ffn_post", "w_gate", "w_up", "w_down")
    outs = [results[name][part] for part in range(4) for name in names]
    return (loss, gx.reshape(x.shape), *outs)
```

```python
import functools

import jax
import jax.numpy as jnp
from jax import lax
from jax.experimental import pallas as pl
from jax.experimental.pallas import tpu as pltpu

F32 = jnp.float32
BF16 = jnp.bfloat16
MESH = pl.DeviceIdType.MESH

EPS = 1e-6
HEAD_DIM = 64
HEADS_PER_BLOCK = 2
LANES = 128
NEG_QK_SCALE = -0.125
POOL_WINDOWS = (2, 4, 8, 16)
POOL_GROUP = 128
HALO = 16
N_MOD = 6
MOD_ROWS = 8
N_CHIPS = 4
N_DEV = 8
VMEM_LIMIT = 56 * 1024 * 1024

ADAM_LR = 0.001
ADAM_B1 = 0.9
ADAM_B2 = 0.999
ADAM_EPS = 1e-08
ADAM_WD = 0.01
ADAM_STEP = 10

TOKEN_TILE = 512
GRAD_TOKEN_TILE = 2048
FFN_ROW_CHUNKS = 2
ATTN_TILE = 512
ATTN_KEY_TILE = 256
ATTN_ROW_CHUNK = 32
LOG_SUM_PASSES = 2


def _dot(a, b):
    return jnp.dot(a, b, preferred_element_type=F32)


def _dot_nt(a, b):
    return lax.dot_general(a, b, (((1,), (1,)), ((), ())), preferred_element_type=F32)


def _dot_tn(a, b):
    return lax.dot_general(a, b, (((0,), (0,)), ((), ())), preferred_element_type=F32)


def _split(v):
    hi = v.astype(BF16)
    lo = (v - hi.astype(F32)).astype(BF16)
    return hi, lo


def _rms(v):
    return lax.rsqrt(jnp.mean(v * v, axis=-1, keepdims=True) + EPS)


def _norm_bwd(dn, n, r):
    return r * (dn - n * jnp.mean(dn * n, axis=-1, keepdims=True))


def _sigmoid(v):
    return 0.5 * jnp.tanh(0.5 * v) + 0.5


def _colsum(v):
    return jnp.sum(v, axis=0, keepdims=True)


def _params(sem=None):
    return pltpu.CompilerParams(dimension_semantics=sem, vmem_limit_bytes=VMEM_LIMIT)


def _position():
    return lax.axis_index("x"), lax.axis_index("y"), lax.axis_index("c")


def _prenorm_proj(x, mod, g_pre, w_in, seq, tm):
    t_all, d = x.shape
    nt = seq // tm
    p = w_in.shape[2]

    def body(x_ref, mod_ref, g_ref, w_ref, h_ref, q_ref, k_ref, v_ref, u_ref, kt_ref, vt_ref):
        xf = x_ref[...]
        n = xf * _rms(xf)
        h = (n * g_ref[...]) * (1.0 + mod_ref[0, 1:2, :]) + mod_ref[0, 0:1, :]
        hb = h.astype(BF16)
        h_ref[...] = hb
        q_ref[...] = (_dot(hb, w_ref[0]) * NEG_QK_SCALE).astype(BF16)
        kf = _dot(hb, w_ref[1])
        vf = _dot(hb, w_ref[2])
        k_ref[...] = kf.astype(BF16)
        v_ref[...] = vf.astype(BF16)
        kt_ref[...] = kf.T.astype(BF16)
        vt_ref[...] = vf.T.astype(BF16)
        u_ref[...] = _dot(hb, w_ref[3])

    tok = lambda i: (i, 0)
    tok_t = lambda i: (0, i)
    return pl.pallas_call(
        body, name="prenorm_proj", grid=(t_all // tm,),
        in_specs=[pl.BlockSpec((tm, d), tok),
                  pl.BlockSpec((1, MOD_ROWS, d), lambda i: (i // nt, 0, 0)),
                  pl.BlockSpec((1, d), lambda i: (0, 0)),
                  pl.BlockSpec((N_CHIPS, d, p), lambda i: (0, 0, 0))],
        out_specs=[pl.BlockSpec((tm, d), tok)] + [pl.BlockSpec((tm, p), tok)] * 4 + [pl.BlockSpec((p, tm), tok_t)] * 2,
        out_shape=[jax.ShapeDtypeStruct((t_all, d), BF16)] + [jax.ShapeDtypeStruct((t_all, p), BF16)] * 3
        + [jax.ShapeDtypeStruct((t_all, p), F32)] + [jax.ShapeDtypeStruct((p, t_all), BF16)] * 2,
        compiler_params=_params(("arbitrary",)),
    )(x, mod, g_pre, w_in)


def _tri_matrix(tk, kind):
    j = lax.broadcasted_iota(jnp.int32, (2 * tk, tk), 0) % tk
    s = lax.broadcasted_iota(jnp.int32, (2 * tk, tk), 1)
    return {"after": j > s, "upto": j <= s, "before": j < s}[kind].astype(BF16)


def _row_sums(v):
    return jnp.broadcast_to(jnp.sum(v, axis=-1, keepdims=True), (v.shape[0], LANES))


def _across(v, n):
    return jnp.concatenate([v] * (n // LANES), axis=1)


def _all_masked(c, diag, rc, tk):
    return diag is not None and diag * tk >= (c + 1) * rc - 1


def _some_masked(c, diag, rc, tk):
    return diag is not None and diag * tk + tk - 1 >= c * rc


def _attn_fwd(qn, k, vt, seq, tq, tk):
    t_all, w = qn.shape
    nb, nq, ndiag = t_all // seq, seq // tq, tq // tk
    assert ndiag % 2 == 0, "two key blocks per loop trip"
    rc = ATTN_ROW_CHUNK
    heads = range(HEADS_PER_BLOCK)

    def body(q_ref, k_ref, vt_ref, tri_ref, o_ref, l_ref,
             z_buf, ls_buf, hl_buf, aft_buf, w_buf, tot_buf, acc_t, run_buf):
        i = pl.program_id(2)
        nblk = (i + 1) * ndiag
        lane = lax.broadcasted_iota(jnp.int32, (1, LANES), 1)
        sub = lax.broadcasted_iota(jnp.int32, (LANES, 1), 0)
        row = lax.broadcasted_iota(jnp.int32, (rc, tk), 0)
        col = lax.broadcasted_iota(jnp.int32, (rc, tk), 1)
        first = lane < HEAD_DIM
        q2 = q_ref[...]
        qs = [jnp.where(first, q2, jnp.zeros_like(q2)), jnp.where(first, jnp.zeros_like(q2), q2)]
        acc_t[...] = jnp.zeros_like(acc_t)
        run_buf[...] = jnp.zeros_like(run_buf)
        w_buf[1] = jnp.zeros((HEADS_PER_BLOCK, tq, tk), BF16)

        def causal(c, diag):
            return (col + diag * tk) < (row + c * rc)

        def scores(blk, slot):
            kj = k_ref[pl.ds(pl.multiple_of(blk * tk, tk), tk), :]
            for h in heads:
                z_buf[slot, h] = _dot_nt(qs[h], kj)

        def values(blk, slot):
            vtj = vt_ref[:, pl.ds(pl.multiple_of(blk * tk, tk), tk)]
            zero = jnp.zeros_like(vtj)
            acc_t[...] += (_dot_nt(jnp.where(sub < HEAD_DIM, vtj, zero), w_buf[slot, 0])
                           + _dot_nt(jnp.where(sub < HEAD_DIM, zero, vtj), w_buf[slot, 1]))

        def softplus_stage(h, slot, diag):
            for c in range(tq // rc):
                rows = slice(c * rc, (c + 1) * rc)
                if _all_masked(c, diag, rc, tk):
                    hl_buf[h, rows, :] = jnp.zeros((rc, LOG_SUM_PASSES * tk), BF16)
                    tot_buf[h, rows, :] = jnp.zeros((rc, LANES), F32)
                    continue
                nz = z_buf[slot, h, rows, :]
                l1 = jnp.minimum(nz, 0.0) - jnp.log(1.0 + jnp.exp(-jnp.abs(nz)))
                if _some_masked(c, diag, rc, tk):
                    l1 = jnp.where(causal(c, diag), l1, 0.0)
                for s, part in enumerate(_split(l1)[:LOG_SUM_PASSES]):
                    hl_buf[h, rows, s * tk:(s + 1) * tk] = part
                ls_buf[h, rows, :] = l1 - nz
                tot_buf[h, rows, :] = _row_sums(l1)

        def weights_stage(h, slot, diag):
            for c in range(tq // rc):
                rows = slice(c * rc, (c + 1) * rc)
                if _all_masked(c, diag, rc, tk):
                    w_buf[slot, h, rows, :] = jnp.zeros((rc, tk), BF16)
                    continue
                wgt = jnp.exp((ls_buf[h, rows, :] + aft_buf[h, rows, :]) + _across(run_buf[h, rows, :], tk))
                if _some_masked(c, diag, rc, tk):
                    wgt = jnp.where(causal(c, diag), wgt, 0.0)
                w_buf[slot, h, rows, :] = wgt.astype(BF16)
                run_buf[h, rows, :] += tot_buf[h, rows, :]

        def position(blk, slot, diag):
            scores(jnp.maximum(blk - 1, 0), 1 - slot)
            for h in heads:
                softplus_stage(h, slot, diag)
                aft_buf[h] = _dot(hl_buf[h], tri_ref[...])
            values(jnp.minimum(blk + 1, nblk - 1), 1 - slot)
            for h in heads:
                weights_stage(h, slot, diag)

        scores(nblk - 1, 0)
        for p in range(ndiag):
            position(nblk - 1 - p, p % 2, ndiag - 1 - p)

        def trip(jj, carry):
            for u in range(2):
                position(i * ndiag - 1 - 2 * jj - u, u, None)
            return carry

        lax.fori_loop(0, (i * ndiag) // 2, trip, 0)
        values(0, 1)
        o_ref[...] = acc_t[...].T.astype(BF16)
        l_ref[...] = jnp.where(first, run_buf[0], run_buf[1])

    qmap = lambda b, hp, i: (b * nq + i, hp)
    nh = HEADS_PER_BLOCK
    return pl.pallas_call(
        body, name="attn_fwd", grid=(nb, w // LANES, nq),
        in_specs=[pl.BlockSpec((tq, LANES), qmap), pl.BlockSpec((seq, LANES), lambda b, hp, i: (b, hp)),
                  pl.BlockSpec((LANES, seq), lambda b, hp, i: (hp, b)),
                  pl.BlockSpec((LOG_SUM_PASSES * tk, tk), lambda b, hp, i: (0, 0))],
        out_specs=[pl.BlockSpec((tq, LANES), qmap), pl.BlockSpec((tq, LANES), qmap)],
        out_shape=[jax.ShapeDtypeStruct((t_all, w), BF16), jax.ShapeDtypeStruct((t_all, w), F32)],
        scratch_shapes=[pltpu.VMEM((2, nh, tq, tk), F32), pltpu.VMEM((nh, tq, tk), F32),
                        pltpu.VMEM((nh, tq, LOG_SUM_PASSES * tk), BF16), pltpu.VMEM((nh, tq, tk), F32),
                        pltpu.VMEM((2, nh, tq, tk), BF16), pltpu.VMEM((nh, tq, LANES), F32),
                        pltpu.VMEM((LANES, tq), F32), pltpu.VMEM((nh, tq, LANES), F32)],
        compiler_params=_params(("arbitrary", "arbitrary", "arbitrary")),
    )(qn, k, vt, _tri_matrix(tk, "after")[:LOG_SUM_PASSES * tk])


def _window_sums(ext, rows, offset, forward):
    r = lax.broadcasted_iota(jnp.int32, (rows, rows + HALO), 0)
    e = lax.broadcasted_iota(jnp.int32, (rows, rows + HALO), 1)
    hi, lo = _split(ext)
    out = []
    for g, win in enumerate(POOL_WINDOWS):
        if forward:
            band = (e >= r) & (e < r + win)
        else:
            band = (e <= r + offset) & (e > r + offset - win)
        bm = band.astype(BF16)
        cols = slice(g * POOL_GROUP, (g + 1) * POOL_GROUP)
        out.append(_dot(bm, hi[:, cols]) + _dot(bm, lo[:, cols]))
    return out


def _window_counts(pos):
    return [jnp.minimum(pos + 1, win).astype(F32) for win in POOL_WINDOWS]


def _mixer_post(u, o, x, mod, g_post, g_fpre, w_pool, pool_scale, w_out, seq, tm):
    t_all, d = x.shape
    nt = seq // tm
    p = u.shape[1]

    def body(u_ref, halo_ref, o_ref, x_ref, mod_ref, gp_ref, gf_ref, wp_ref, ps_ref, wo_ref,
             pooled_ref, mixin_ref, mix_ref, x1_ref, h2_ref):
        it = pl.program_id(0) % nt
        uf = u_ref[...]
        halo = jnp.where(it == 0, 0.0, halo_ref[...])
        ext = jnp.concatenate([halo, uf], axis=0)
        pos = it * tm + lax.broadcasted_iota(jnp.int32, (tm, 1), 0)
        sums = _window_sums(ext, tm, HALO, False)
        cnts = _window_counts(pos)
        pools = []
        for g in range(len(POOL_WINDOWS)):
            cols = slice(g * POOL_GROUP, (g + 1) * POOL_GROUP)
            pooled = (sums[g] / cnts[g] - uf[:, cols]).astype(BF16)
            pooled_ref[:, cols] = pooled
            yg = _dot(pooled, wp_ref[g].astype(BF16))
            pools.append((yg * ps_ref[:, cols]).astype(BF16))
        mixin = jnp.concatenate([o_ref[...]] + pools, axis=1)
        mixin_ref[...] = mixin
        mix = _dot(mixin, wo_ref[...])
        mix_ref[...] = mix
        n2 = mix * _rms(mix)
        x1 = x_ref[...] + mod_ref[0, 2:3, :] * (n2 * gp_ref[...])
        x1_ref[...] = x1
        n3 = x1 * _rms(x1)
        h2 = (n3 * gf_ref[...]) * (1.0 + mod_ref[0, 4:5, :]) + mod_ref[0, 3:4, :]
        h2_ref[...] = h2.astype(BF16)

    tok = lambda i: (i, 0)
    const2 = lambda i: (0, 0)
    hb = tm // HALO
    return pl.pallas_call(
        body, name="mixer_post", grid=(t_all // tm,),
        in_specs=[pl.BlockSpec((tm, p), tok),
                  pl.BlockSpec((HALO, p), lambda i: (jnp.maximum(i * hb - 1, 0), 0)),
                  pl.BlockSpec((tm, p), tok),
                  pl.BlockSpec((tm, d), tok),
                  pl.BlockSpec((1, MOD_ROWS, d), lambda i: (i // nt, 0, 0)),
                  pl.BlockSpec((1, d), const2), pl.BlockSpec((1, d), const2),
                  pl.BlockSpec(w_pool.shape, lambda i: (0, 0, 0)),
                  pl.BlockSpec((1, p), const2),
                  pl.BlockSpec((d, d), const2)],
        out_specs=[pl.BlockSpec((tm, p), tok), pl.BlockSpec((tm, d), tok), pl.BlockSpec((tm, d), tok),
                   pl.BlockSpec((tm, d), tok), pl.BlockSpec((tm, d), tok)],
        out_shape=[jax.ShapeDtypeStruct((t_all, p), BF16), jax.ShapeDtypeStruct((t_all, d), BF16),
                   jax.ShapeDtypeStruct((t_all, d), F32), jax.ShapeDtypeStruct((t_all, d), F32),
                   jax.ShapeDtypeStruct((t_all, d), BF16)],
        compiler_params=_params(("arbitrary",)),
    )(u, u, o, x, mod, g_post, g_fpre, w_pool, pool_scale, w_out)


def _ffn_fwd(h2, w_g, w_u, w_d, x1, tgt, mod, g_post, seq, tm):
    t_all, d = x1.shape
    nt = seq // tm
    nk, ff, _ = w_g.shape

    def body(h_ref, wg_ref, wu_ref, wd_ref, x1_ref, t_ref, mod_ref, g_ref,
             a_ref, b_ref, fin_ref, dy_ref, df_ref, loss_ref, accb_ref, accg_ref, facc):
        i, k = pl.program_id(0), pl.program_id(1)

        @pl.when(k == 0)
        def _():
            facc[...] = jnp.zeros_like(facc)

        for c in range(FFN_ROW_CHUNKS):
            rows = slice(c * (tm // FFN_ROW_CHUNKS), (c + 1) * (tm // FFN_ROW_CHUNKS))
            hb = h_ref[rows, :]
            a = _dot_nt(hb, wg_ref[0])
            b = _dot_nt(hb, wu_ref[0])
            a_ref[0, rows, :] = a.astype(BF16)
            b_ref[0, rows, :] = b.astype(BF16)
            fin = ((a * _sigmoid(a)) * b).astype(BF16)
            fin_ref[0, rows, :] = fin
            facc[rows, :] += _dot(fin, wd_ref[0])

        @pl.when(k == nk - 1)
        def _():
            f = facc[...]
            r4 = _rms(f)
            n4 = f * r4
            gate = mod_ref[0, 5:6, :]
            g = g_ref[...]
            err = (x1_ref[...] + gate * (n4 * g)) - t_ref[...]
            dy = err * (1.0 / d)
            dy_ref[...] = dy

            @pl.when(i == 0)
            def _():
                loss_ref[...] = jnp.zeros_like(loss_ref)
                accg_ref[...] = jnp.zeros_like(accg_ref)

            @pl.when(i % nt == 0)
            def _():
                accb_ref[...] = jnp.zeros_like(accb_ref)

            loss_ref[...] += (0.5 / d) * jnp.sum(err * err)
            accb_ref[0, 0:1, :] += _colsum(dy * (n4 * g))
            accg_ref[0:1, :] += _colsum((dy * gate) * n4)
            dn4 = (dy * gate) * g
            df_ref[...] = _norm_bwd(dn4, n4, r4).astype(BF16)

    tok = lambda i, k: (i, 0)
    ktok = lambda i, k: (k, i, 0)
    kw = lambda i, k: (k, 0, 0)
    const2 = lambda i, k: (0, 0)
    return pl.pallas_call(
        body, name="ffn_fwd", grid=(t_all // tm, nk),
        in_specs=[pl.BlockSpec((tm, d), tok),
                  pl.BlockSpec((1, ff, d), kw), pl.BlockSpec((1, ff, d), kw), pl.BlockSpec((1, ff, d), kw),
                  pl.BlockSpec((tm, d), tok), pl.BlockSpec((tm, d), tok),
                  pl.BlockSpec((1, MOD_ROWS, d), lambda i, k: (i // nt, 0, 0)),
                  pl.BlockSpec((1, d), const2)],
        out_specs=[pl.BlockSpec((1, tm, ff), ktok)] * 3
        + [pl.BlockSpec((tm, d), tok), pl.BlockSpec((tm, d), tok),
           pl.BlockSpec((8, LANES), const2),
           pl.BlockSpec((1, 8, d), lambda i, k: (i // nt, 0, 0)),
           pl.BlockSpec((8, d), const2)],
        out_shape=[jax.ShapeDtypeStruct((nk, t_all, ff), BF16)] * 3
        + [jax.ShapeDtypeStruct((t_all, d), F32), jax.ShapeDtypeStruct((t_all, d), BF16),
           jax.ShapeDtypeStruct((8, LANES), F32),
           jax.ShapeDtypeStruct((t_all // seq, 8, d), F32),
           jax.ShapeDtypeStruct((8, d), F32)],
        scratch_shapes=[pltpu.VMEM((tm, d), F32)],
        compiler_params=_params(("arbitrary", "arbitrary")),
    )(h2, w_g, w_u, w_d, x1, tgt, mod, g_post)


def _ffn_bwd(df, a, b, w_d, w_g, w_u, x1, dy, mix, mod, g_fpre, g_mpost, seq, tm):
    t_all, d = x1.shape
    nt = seq // tm
    nk, ff, _ = w_g.shape

    def body(df_ref, a_ref, b_ref, wd_ref, wg_ref, wu_ref, x1_ref, dy_ref, mix_ref, mod_ref, gf_ref, gm_ref,
             da_ref, db_ref, dx1_ref, dmix_ref, accb_ref, accg_ref, hacc):
        i, k = pl.program_id(0), pl.program_id(1)

        @pl.when(k == 0)
        def _():
            hacc[...] = jnp.zeros_like(hacc)

        for c in range(FFN_ROW_CHUNKS):
            rows = slice(c * (tm // FFN_ROW_CHUNKS), (c + 1) * (tm // FFN_ROW_CHUNKS))
            dfin = _dot_nt(df_ref[rows, :], wd_ref[0])
            af = a_ref[0, rows, :].astype(F32)
            bf = b_ref[0, rows, :].astype(F32)
            sig = _sigmoid(af)
            da = ((dfin * bf) * (sig * (1.0 + af * (1.0 - sig)))).astype(BF16)
            db = (dfin * (af * sig)).astype(BF16)
            da_ref[0, rows, :] = da
            db_ref[0, rows, :] = db
            hacc[rows, :] += _dot(da, wg_ref[0]) + _dot(db, wu_ref[0])

        @pl.when(k == nk - 1)
        def _():
            @pl.when(i == 0)
            def _():
                accg_ref[...] = jnp.zeros_like(accg_ref)

            @pl.when(i % nt == 0)
            def _():
                accb_ref[...] = jnp.zeros_like(accb_ref)

            dh2 = hacc[...]
            x1 = x1_ref[...]
            r3 = _rms(x1)
            n3 = x1 * r3
            g3 = gf_ref[...]
            scale1 = 1.0 + mod_ref[0, 4:5, :]
            accb_ref[0, 0:1, :] += _colsum(dh2)
            accb_ref[0, 1:2, :] += _colsum(dh2 * (n3 * g3))
            accg_ref[0:1, :] += _colsum((dh2 * scale1) * n3)
            dx1 = dy_ref[...] + _norm_bwd((dh2 * scale1) * g3, n3, r3)
            dx1_ref[...] = dx1
            mix = mix_ref[...]
            r2 = _rms(mix)
            n2 = mix * r2
            g2 = gm_ref[...]
            gate = mod_ref[0, 2:3, :]
            accb_ref[0, 2:3, :] += _colsum(dx1 * (n2 * g2))
            accg_ref[1:2, :] += _colsum((dx1 * gate) * n2)
            dmix_ref[...] = _norm_bwd((dx1 * gate) * g2, n2, r2).astype(BF16)

    tok = lambda i, k: (i, 0)
    ktok = lambda i, k: (k, i, 0)
    kw = lambda i, k: (k, 0, 0)
    const2 = lambda i, k: (0, 0)
    return pl.pallas_call(
        body, name="ffn_bwd", grid=(t_all // tm, nk),
        in_specs=[pl.BlockSpec((tm, d), tok),
                  pl.BlockSpec((1, tm, ff), ktok), pl.BlockSpec((1, tm, ff), ktok),
                  pl.BlockSpec((1, ff, d), kw), pl.BlockSpec((1, ff, d), kw), pl.BlockSpec((1, ff, d), kw),
                  pl.BlockSpec((tm, d), tok), pl.BlockSpec((tm, d), tok), pl.BlockSpec((tm, d), tok),
                  pl.BlockSpec((1, MOD_ROWS, d), lambda i, k: (i // nt, 0, 0)),
                  pl.BlockSpec((1, d), const2), pl.BlockSpec((1, d), const2)],
        out_specs=[pl.BlockSpec((1, tm, ff), ktok)] * 2
        + [pl.BlockSpec((tm, d), tok), pl.BlockSpec((tm, d), tok),
           pl.BlockSpec((1, 8, d), lambda i, k: (i // nt, 0, 0)),
           pl.BlockSpec((8, d), const2)],
        out_shape=[jax.ShapeDtypeStruct((nk, t_all, ff), BF16)] * 2
        + [jax.ShapeDtypeStruct((t_all, d), F32), jax.ShapeDtypeStruct((t_all, d), BF16),
           jax.ShapeDtypeStruct((t_all // seq, 8, d), F32),
           jax.ShapeDtypeStruct((8, d), F32)],
        scratch_shapes=[pltpu.VMEM((tm, d), F32)],
        compiler_params=_params(("arbitrary", "arbitrary")),
    )(df, a, b, w_d, w_g, w_u, x1, dy, mix, mod, g_fpre, g_mpost)


def _mixer_bwd(dmix, w_out, pooled, w_pool, pool_scale, seq, tm):
    t_all, d = dmix.shape
    p = pooled.shape[1]
    ng = len(POOL_WINDOWS)

    def body(dm_ref, wo_ref, pooled_ref, wp_ref, ps_ref, do_ref, dpd_ref, dps_ref, dwp_ref):
        i = pl.program_id(0)

        @pl.when(i == 0)
        def _():
            dps_ref[...] = jnp.zeros_like(dps_ref)
            dwp_ref[...] = jnp.zeros_like(dwp_ref)

        dmixin = _dot_nt(dm_ref[...], wo_ref[...])
        do_ref[...] = dmixin[:, :p].astype(BF16)
        for g in range(ng):
            cols = slice(g * POOL_GROUP, (g + 1) * POOL_GROUP)
            dpool = dmixin[:, p + g * POOL_GROUP:p + (g + 1) * POOL_GROUP]
            pooled = pooled_ref[:, cols]
            wpg = wp_ref[g].astype(BF16)
            yg = _dot(pooled, wpg)
            dps_ref[0:1, cols] += _colsum(dpool * yg)
            dyg = (dpool * ps_ref[:, cols]).astype(BF16)
            dwp_ref[g] += _dot_tn(pooled, dyg)
            dpd_ref[:, cols] = _dot_nt(dyg, wpg)

    tok = lambda i: (i, 0)
    const2 = lambda i: (0, 0)
    const3 = lambda i: (0, 0, 0)
    return pl.pallas_call(
        body, name="mixer_bwd", grid=(t_all // tm,),
        in_specs=[pl.BlockSpec((tm, d), tok), pl.BlockSpec((d, d), const2), pl.BlockSpec((tm, p), tok),
                  pl.BlockSpec(w_pool.shape, const3), pl.BlockSpec((1, p), const2)],
        out_specs=[pl.BlockSpec((tm, p), tok), pl.BlockSpec((tm, p), tok),
                   pl.BlockSpec((8, p), const2), pl.BlockSpec(w_pool.shape, const3)],
        out_shape=[jax.ShapeDtypeStruct((t_all, p), BF16), jax.ShapeDtypeStruct((t_all, p), F32),
                   jax.ShapeDtypeStruct((8, p), F32), jax.ShapeDtypeStruct(w_pool.shape, F32)],
        compiler_params=_params(("arbitrary",)),
    )(dmix, w_out, pooled, w_pool, pool_scale)


def _attn_bwd(qn, k, kt, v, do, ltot, seq, tq, tk):
    t_all, w = qn.shape
    nb, nq, ndiag, nkb = t_all // seq, seq // tq, tq // tk, seq // tk
    assert ndiag % 2 == 0, "two key blocks per loop trip"
    rc = ATTN_ROW_CHUNK
    nh = HEADS_PER_BLOCK
    heads = range(nh)

    def body(q_ref, k_ref, kt_ref, v_ref, do_ref, l_ref, up_ref, bf_ref, dq_ref, dk_ref, dv_ref,
             z_buf, dw_buf, ls_buf, hl_buf, upto_buf, g_buf, gb_buf, before_buf, w_buf, dz_buf,
             totl_buf, totg_buf, rem_buf, preg_buf, qnt_buf, dot_buf, dq_t, dk_t, dv_t):
        i = pl.program_id(2)
        nblk = (i + 1) * ndiag

        @pl.when(i == 0)
        def _():
            dk_t[...] = jnp.zeros_like(dk_t)
            dv_t[...] = jnp.zeros_like(dv_t)

        lane = lax.broadcasted_iota(jnp.int32, (1, LANES), 1)
        sub = lax.broadcasted_iota(jnp.int32, (LANES, 1), 0)
        row = lax.broadcasted_iota(jnp.int32, (rc, tk), 0)
        col = lax.broadcasted_iota(jnp.int32, (rc, tk), 1)
        first = lane < HEAD_DIM
        upper = sub < HEAD_DIM
        q2 = q_ref[...]
        do2 = do_ref[...]
        l2 = l_ref[...]
        qs = [jnp.where(first, q2, jnp.zeros_like(q2)), jnp.where(first, jnp.zeros_like(q2), q2)]
        dos = [jnp.where(first, do2, jnp.zeros_like(do2)), jnp.where(first, jnp.zeros_like(do2), do2)]
        for src, dst in ((q2, qnt_buf), (do2, dot_buf)):
            t = src.astype(F32).T
            dst[:, 0:tq] = jnp.where(upper, t, 0.0).astype(BF16)
            dst[:, tq:2 * tq] = jnp.where(upper, 0.0, t).astype(BF16)
        for h in heads:
            rem_buf[h] = jnp.where(first if h == 0 else ~first, l2, pltpu.roll(l2, HEAD_DIM, 1))
        preg_buf[...] = jnp.zeros_like(preg_buf)
        dq_t[...] = jnp.zeros_like(dq_t)
        w_buf[1] = jnp.zeros((nh * tq, tk), BF16)
        dz_buf[1] = jnp.zeros((nh * tq, tk), BF16)

        def causal(c, diag):
            return (col + diag * tk) < (row + c * rc)

        def scores(blk, slot):
            off = pl.multiple_of(blk * tk, tk)
            kj = k_ref[pl.ds(off, tk), :]
            vj = v_ref[pl.ds(off, tk), :]
            for h in heads:
                z_buf[slot, h] = _dot_nt(qs[h], kj)
                dw_buf[slot, h] = _dot_nt(dos[h], vj)

        def gradients(blk, slot):
            off = pl.multiple_of(blk * tk, tk)
            ktj = kt_ref[:, pl.ds(off, tk)]
            zero = jnp.zeros_like(ktj)
            dq_t[...] += (_dot_nt(jnp.where(upper, ktj, zero), dz_buf[slot, 0:tq, :])
                          + _dot_nt(jnp.where(upper, zero, ktj), dz_buf[slot, tq:2 * tq, :]))
            dk_t[blk] += _dot(qnt_buf[...], dz_buf[slot])
            dv_t[blk] += _dot(dot_buf[...], w_buf[slot])

        def softplus_stage(h, slot, diag):
            for c in range(tq // rc):
                rows = slice(c * rc, (c + 1) * rc)
                if _all_masked(c, diag, rc, tk):
                    hl_buf[h, rows, :] = jnp.zeros((rc, LOG_SUM_PASSES * tk), BF16)
                    continue
                nz = z_buf[slot, h, rows, :]
                l1 = jnp.minimum(nz, 0.0) - jnp.log(1.0 + jnp.exp(-jnp.abs(nz)))
                if _some_masked(c, diag, rc, tk):
                    l1 = jnp.where(causal(c, diag), l1, 0.0)
                for s, part in enumerate(_split(l1)[:LOG_SUM_PASSES]):
                    hl_buf[h, rows, s * tk:(s + 1) * tk] = part
                ls_buf[h, rows, :] = l1 - nz
                totl_buf[h, rows, :] = _row_sums(l1)

        def weights_stage(h, slot, diag):
            for c in range(tq // rc):
                rows = slice(c * rc, (c + 1) * rc)
                stacked = slice(h * tq + c * rc, h * tq + (c + 1) * rc)
                if _all_masked(c, diag, rc, tk):
                    w_buf[slot, stacked, :] = jnp.zeros((rc, tk), BF16)
                    gb_buf[h, rows, :] = jnp.zeros((rc, tk), BF16)
                    continue
                wgt = jnp.exp(ls_buf[h, rows, :] + (_across(rem_buf[h, rows, :], tk) - upto_buf[h, rows, :]))
                if _some_masked(c, diag, rc, tk):
                    wgt = jnp.where(causal(c, diag), wgt, 0.0)
                w_buf[slot, stacked, :] = wgt.astype(BF16)
                g = wgt * dw_buf[slot, h, rows, :]
                g_buf[h, rows, :] = g
                gb_buf[h, rows, :] = g.astype(BF16)
                totg_buf[h, rows, :] = _row_sums(g)
                rem_buf[h, rows, :] -= totl_buf[h, rows, :]

        def dscore_stage(h, slot, diag):
            for c in range(tq // rc):
                rows = slice(c * rc, (c + 1) * rc)
                stacked = slice(h * tq + c * rc, h * tq + (c + 1) * rc)
                if _all_masked(c, diag, rc, tk):
                    dz_buf[slot, stacked, :] = jnp.zeros((rc, tk), BF16)
                    continue
                sig = jnp.exp(ls_buf[h, rows, :])
                g = g_buf[h, rows, :]
                dnz = sig * (before_buf[h, rows, :] + _across(preg_buf[h, rows, :], tk)) - g * (1.0 - sig)
                if _some_masked(c, diag, rc, tk):
                    dnz = jnp.where(causal(c, diag), dnz, 0.0)
                dz_buf[slot, stacked, :] = dnz.astype(BF16)
                preg_buf[h, rows, :] += totg_buf[h, rows, :]

        def position(blk, slot, diag, prefetch):
            if prefetch:
                scores(blk + 1, 1 - slot)
            for h in heads:
                softplus_stage(h, slot, diag)
                upto_buf[h] = _dot(hl_buf[h], up_ref[...])
            gradients(jnp.maximum(blk - 1, 0), 1 - slot)
            for h in heads:
                weights_stage(h, slot, diag)
                before_buf[h] = _dot(gb_buf[h], bf_ref[...])
            for h in heads:
                dscore_stage(h, slot, diag)

        scores(0, 0)

        def trip(jj, carry):
            for u in range(2):
                position(2 * jj + u, u, None, True)
            return carry

        lax.fori_loop(0, (i * ndiag) // 2, trip, 0)
        for d in range(ndiag):
            position(i * ndiag + d, d % 2, d, d < ndiag - 1)
        gradients(nblk - 1, 1)
        dq_ref[...] = (dq_t[...].T * NEG_QK_SCALE).astype(BF16)

        @pl.when(i == nq - 1)
        def _():
            for blk in range(nkb):
                dk_ref[blk * tk:(blk + 1) * tk, :] = dk_t[blk].T.astype(BF16)
                dv_ref[blk * tk:(blk + 1) * tk, :] = dv_t[blk].T.astype(BF16)

    qmap = lambda b, hp, i: (b * nq + i, hp)
    kmap = lambda b, hp, i: (b, hp)
    const = lambda b, hp, i: (0, 0)
    return pl.pallas_call(
        body, name="attn_bwd", grid=(nb, w // LANES, nq),
        in_specs=[pl.BlockSpec((tq, LANES), qmap), pl.BlockSpec((seq, LANES), kmap),
                  pl.BlockSpec((LANES, seq), lambda b, hp, i: (hp, b)), pl.BlockSpec((seq, LANES), kmap),
                  pl.BlockSpec((tq, LANES), qmap), pl.BlockSpec((tq, LANES), qmap),
                  pl.BlockSpec((LOG_SUM_PASSES * tk, tk), const), pl.BlockSpec((tk, tk), const)],
        out_specs=[pl.BlockSpec((tq, LANES), qmap), pl.BlockSpec((seq, LANES), kmap), pl.BlockSpec((seq, LANES), kmap)],
        out_shape=[jax.ShapeDtypeStruct((t_all, w), BF16)] * 3,
        scratch_shapes=[pltpu.VMEM((2, nh, tq, tk), F32), pltpu.VMEM((2, nh, tq, tk), F32),
                        pltpu.VMEM((nh, tq, tk), F32), pltpu.VMEM((nh, tq, LOG_SUM_PASSES * tk), BF16),
                        pltpu.VMEM((nh, tq, tk), F32), pltpu.VMEM((nh, tq, tk), F32),
                        pltpu.VMEM((nh, tq, tk), BF16), pltpu.VMEM((nh, tq, tk), F32),
                        pltpu.VMEM((2, nh * tq, tk), BF16), pltpu.VMEM((2, nh * tq, tk), BF16),
                        pltpu.VMEM((nh, tq, LANES), F32), pltpu.VMEM((nh, tq, LANES), F32),
                        pltpu.VMEM((nh, tq, LANES), F32), pltpu.VMEM((nh, tq, LANES), F32),
                        pltpu.VMEM((LANES, nh * tq), BF16), pltpu.VMEM((LANES, nh * tq), BF16),
                        pltpu.VMEM((LANES, tq), F32), pltpu.VMEM((nkb, LANES, tk), F32),
                        pltpu.VMEM((nkb, LANES, tk), F32)],
        compiler_params=_params(("arbitrary", "arbitrary", "arbitrary")),
    )(qn, k, kt, v, do, ltot, _tri_matrix(tk, "upto")[:LOG_SUM_PASSES * tk], _tri_matrix(tk, "before")[:tk])


def _inproj_bwd(dq, dk, dv, dpd, x, dx1, mod, g_pre, w_in, seq, tm):
    t_all, d = x.shape
    nt = seq // tm
    p = dq.shape[1]

    def body(dq_ref, dk_ref, dv_ref, dpd_ref, halo_ref, x_ref, dx1_ref, mod_ref, g_ref, w_ref,
             gx_ref, du_ref, accb_ref, accg_ref):
        i = pl.program_id(0)
        it = i % nt

        @pl.when(i == 0)
        def _():
            accg_ref[...] = jnp.zeros_like(accg_ref)

        @pl.when(it == 0)
        def _():
            accb_ref[...] = jnp.zeros_like(accb_ref)

        dpd = dpd_ref[...]
        pos = it * tm + lax.broadcasted_iota(jnp.int32, (tm, 1), 0)
        cnts = _window_counts(pos)
        halo = jnp.where(it == nt - 1, 0.0, halo_ref[...])
        scaled = []
        halos = []
        for g, win in enumerate(POOL_WINDOWS):
            cols = slice(g * POOL_GROUP, (g + 1) * POOL_GROUP)
            scaled.append(dpd[:, cols] / cnts[g])
            halos.append(halo[:, cols] / float(win))
        ext = jnp.concatenate([jnp.concatenate(scaled, axis=1), jnp.concatenate(halos, axis=1)], axis=0)
        sums = _window_sums(ext, tm, 0, True)
        du = (jnp.concatenate(sums, axis=1) - dpd).astype(BF16)
        du_ref[...] = du
        dh1 = (_dot_nt(dq_ref[...], w_ref[0]) + _dot_nt(dk_ref[...], w_ref[1])
               + _dot_nt(dv_ref[...], w_ref[2]) + _dot_nt(du, w_ref[3]))
        xf = x_ref[...]
        r1 = _rms(xf)
        n1 = xf * r1
        g1 = g_ref[...]
        scale1 = 1.0 + mod_ref[0, 1:2, :]
        accb_ref[0, 0:1, :] += _colsum(dh1)
        accb_ref[0, 1:2, :] += _colsum(dh1 * (n1 * g1))
        accg_ref[0:1, :] += _colsum((dh1 * scale1) * n1)
        gx_ref[...] = dx1_ref[...] + _norm_bwd((dh1 * scale1) * g1, n1, r1)

    tok = lambda i: (i, 0)
    const2 = lambda i: (0, 0)
    hb = tm // HALO
    last = t_all // HALO - 1
    return pl.pallas_call(
        body, name="inproj_bwd", grid=(t_all // tm,),
        in_specs=[pl.BlockSpec((tm, p), tok), pl.BlockSpec((tm, p), tok), pl.BlockSpec((tm, p), tok),
                  pl.BlockSpec((tm, p), tok),
                  pl.BlockSpec((HALO, p), lambda i: (jnp.minimum((i + 1) * hb, last), 0)),
                  pl.BlockSpec((tm, d), tok), pl.BlockSpec((tm, d), tok),
                  pl.BlockSpec((1, MOD_ROWS, d), lambda i: (i // nt, 0, 0)),
                  pl.BlockSpec((1, d), const2),
                  pl.BlockSpec((N_CHIPS, d, p), lambda i: (0, 0, 0))],
        out_specs=[pl.BlockSpec((tm, d), tok), pl.BlockSpec((tm, p), tok),
                   pl.BlockSpec((1, 8, d), lambda i: (i // nt, 0, 0)),
                   pl.BlockSpec((8, d), const2)],
        out_shape=[jax.ShapeDtypeStruct((t_all, d), F32), jax.ShapeDtypeStruct((t_all, p), BF16),
                   jax.ShapeDtypeStruct((t_all // seq, 8, d), F32),
                   jax.ShapeDtypeStruct((8, d), F32)],
        compiler_params=_params(("arbitrary",)),
    )(dq, dk, dv, dpd, dpd, x, dx1, mod, g_pre, w_in)


def _tn_matmul(x, ys, nk, bt, name):
    t_all = x.shape[-2]
    m = x.shape[-1]
    ny = len(ys)
    nt = t_all // bt

    def spec(arr):
        if arr.ndim == 3:
            return pl.BlockSpec((1, bt, arr.shape[-1]), lambda k, t: (k, t, 0))
        return pl.BlockSpec((bt, arr.shape[-1]), lambda k, t: (t, 0))

    def tile(ref):
        return ref[0] if len(ref.shape) == 3 else ref[...]

    def body(*refs):
        x_ref, y_refs, o_refs, h_refs = refs[0], refs[1:1 + ny], refs[1 + ny:1 + 2 * ny], refs[1 + 2 * ny:]
        t = pl.program_id(1)
        xt = tile(x_ref)
        for y_ref, o_ref, h_ref in zip(y_refs, o_refs, h_refs):
            part = _dot_tn(xt, tile(y_ref))

            @pl.when(t == 0)
            def _(o_ref=o_ref, part=part):
                o_ref[0] = part

            @pl.when(t > 0)
            def _(o_ref=o_ref, part=part):
                o_ref[0] += part

            @pl.when(t == nt - 1)
            def _(o_ref=o_ref, h_ref=h_ref):
                h_ref[0] = o_ref[0].astype(BF16)

    out_specs = [pl.BlockSpec((1, m, y.shape[-1]), lambda k, t: (k, 0, 0)) for y in ys]
    out = pl.pallas_call(
        body, name=name, grid=(nk, nt),
        in_specs=[spec(x)] + [spec(y) for y in ys],
        out_specs=out_specs * 2,
        out_shape=[jax.ShapeDtypeStruct((nk, m, y.shape[-1]), dt) for dt in (F32, BF16) for y in ys],
        compiler_params=_params(("arbitrary", "arbitrary")),
    )(x, *ys)
    return out[:ny], out[ny:]


def _cond_fwd(c_all, w_q, b_q, bn):
    nrow, d = c_all.shape
    ncol = w_q.shape[1]

    def body(c_ref, w_ref, b_ref, sc_ref, mod_ref):
        cf = c_ref[...]
        sc = cf * _sigmoid(cf)
        sc_ref[...] = sc
        shi, slo = _split(sc)
        whi, wlo = _split(w_ref[...])
        mod_ref[...] = (_dot(shi, whi) + _dot(shi, wlo) + _dot(slo, whi)) + b_ref[...]

    return pl.pallas_call(
        body, name="cond_fwd", grid=(ncol // bn,),
        in_specs=[pl.BlockSpec((nrow, d), lambda n: (0, 0)), pl.BlockSpec((d, bn), lambda n: (0, n)),
                  pl.BlockSpec((1, bn), lambda n: (0, n))],
        out_specs=[pl.BlockSpec((nrow, d), lambda n: (0, 0)), pl.BlockSpec((nrow, bn), lambda n: (0, n))],
        out_shape=[jax.ShapeDtypeStruct((nrow, d), F32), jax.ShapeDtypeStruct((nrow, ncol), F32)],
        compiler_params=_params(("arbitrary",)),
    )(c_all, w_q, b_q)


def _cond_bwd(sc_all, dmod_q, bn):
    nrow, d = sc_all.shape
    ncol = dmod_q.shape[1]

    def body(sc_ref, dm_ref, gw_ref):
        shi, slo = _split(sc_ref[...])
        dhi, dlo = _split(dm_ref[...])
        gw_ref[...] = _dot_tn(shi, dhi) + _dot_tn(shi, dlo) + _dot_tn(slo, dhi)

    return pl.pallas_call(
        body, name="cond_bwd", grid=(ncol // bn,),
        in_specs=[pl.BlockSpec((nrow, d), lambda n: (0, 0)), pl.BlockSpec((nrow, bn), lambda n: (0, n))],
        out_specs=pl.BlockSpec((d, bn), lambda n: (0, n)),
        out_shape=jax.ShapeDtypeStruct((d, ncol), F32),
        compiler_params=_params(("arbitrary",)),
    )(sc_all, dmod_q)


def _row_block(rows, cols, budget=1 << 18):
    best = None
    for br in range(8, rows + 1, 8):
        if rows % br == 0 and br * cols <= budget:
            best = br
    return best if best is not None else rows


def _adamw(w, g, m, v, name):
    rows, cols = w.shape
    br = _row_block(rows, cols)
    c1 = 1.0 - ADAM_B1 ** ADAM_STEP
    c2 = 1.0 - ADAM_B2 ** ADAM_STEP

    def body(w_ref, g_ref, m_ref, v_ref, d_ref, nm_ref, nv_ref):
        gf = g_ref[...]
        m2 = ADAM_B1 * m_ref[...] + (1.0 - ADAM_B1) * gf
        v2 = ADAM_B2 * v_ref[...] + (1.0 - ADAM_B2) * (gf * gf)
        nm_ref[...] = m2
        nv_ref[...] = v2
        d_ref[...] = -ADAM_LR * ((m2 / c1) / (jnp.sqrt(v2 / c2) + ADAM_EPS) + ADAM_WD * w_ref[...])

    blk = pl.BlockSpec((br, cols), lambda i: (i, 0))
    return pl.pallas_call(
        body, name=name, grid=(rows // br,),
        in_specs=[blk] * 4, out_specs=[blk] * 3,
        out_shape=[jax.ShapeDtypeStruct((rows, cols), F32)] * 3,
        compiler_params=_params(("arbitrary",)),
    )(w, g, m, v)


def _all_gather(x_shard, name):
    m_per, n = x_shard.shape

    def body(x_ref, out_ref, send_sems, recv_sems, local_sem):
        x, y, c = _position()
        me, sibling = (x, y, c), (x, y, 1 - c)
        chips = [(1 - x, y), (x, 1 - y), (1 - x, 1 - y)]

        def rows(px, py, pc):
            return out_ref.at[pl.ds((4 * px + 2 * py + pc) * m_per, m_per), :]

        def copy(k, block, to, src=None):
            return pltpu.make_async_remote_copy(
                src_ref=rows(*block) if src is None else src, dst_ref=rows(*block),
                send_sem=send_sems.at[k], recv_sem=recv_sems.at[k], device_id=to, device_id_type=MESH)

        mine = pltpu.make_async_copy(x_ref, rows(*me), local_sem)
        mine.start()
        first = [copy(0, me, sibling, src=x_ref)]
        first += [copy(1 + j, me, (*chip, c), src=x_ref) for j, chip in enumerate(chips)]
        for cp in first:
            cp.start()
        passed = [copy(4 + j, (*chip, c), sibling) for j, chip in enumerate(chips)]
        for j, chip in enumerate(chips):
            copy(1 + j, (*chip, c), me).wait_recv()
            passed[j].start()
        copy(0, sibling, me).wait_recv()
        for j, chip in enumerate(chips):
            copy(4 + j, (*chip, 1 - c), me).wait_recv()
        for cp in first + passed:
            cp.wait_send()
        mine.wait()

    return pl.pallas_call(
        body, name=name,
        out_shape=jax.ShapeDtypeStruct((N_DEV * m_per, n), x_shard.dtype),
        in_specs=[pl.BlockSpec(memory_space=pltpu.VMEM)],
        out_specs=pl.BlockSpec(memory_space=pltpu.VMEM),
        scratch_shapes=[pltpu.SemaphoreType.DMA((7,)), pltpu.SemaphoreType.DMA((7,)), pltpu.SemaphoreType.DMA],
        compiler_params=pltpu.CompilerParams(vmem_limit_bytes=VMEM_LIMIT),
    )(x_shard)


_ANY = pl.BlockSpec(memory_space=pl.ANY)


def _place_quarters(place, quarters):
    steps = 2

    def body(place_ref, *refs):
        n = len(refs) // 2
        for w_ref, o_ref in zip(refs[:n], refs[n:]):
            o_ref[0] = w_ref[...].astype(BF16)

    return pl.pallas_call(
        body, name="place_quarters",
        grid_spec=pltpu.PrefetchScalarGridSpec(
            num_scalar_prefetch=1, grid=(steps,),
            in_specs=[pl.BlockSpec((q.shape[0] // steps, q.shape[1]), lambda r, place_ref: (r, 0)) for q in quarters],
            out_specs=[pl.BlockSpec((1, q.shape[0] // steps, q.shape[1]), lambda r, place_ref: (place_ref[0], r, 0))
                       for q in quarters]),
        out_shape=[jax.ShapeDtypeStruct((N_CHIPS,) + q.shape, BF16) for q in quarters],
        compiler_params=_params(("arbitrary",)),
    )(place, *quarters)


def _gather_weights(placed):
    n = len(placed)
    shapes = [b.shape[1:] for b in placed]

    def body(*refs):
        g_refs = refs[n:2 * n]
        send_sems, recv_sems = refs[2 * n:]
        x, y, c = _position()
        sibling = (x, y, 1 - c)
        chips = [(1 - x, y), (x, 1 - y), (1 - x, 1 - y)]
        mine = 2 * x + y

        def half(a, which):
            hr = shapes[a][0] // 2
            return pl.ds(which * hr, hr)

        def over_ici(a, p, slot):
            ref = g_refs[a].at[slot, half(a, c), :]
            return pltpu.make_async_remote_copy(
                src_ref=ref, dst_ref=ref,
                send_sem=send_sems.at[6 * a + p], recv_sem=recv_sems.at[6 * a + p],
                device_id=(*chips[p], c), device_id_type=MESH)

        def over_d2d(a, p, slot, which):
            ref = g_refs[a].at[slot, half(a, which), :]
            return pltpu.make_async_remote_copy(
                src_ref=ref, dst_ref=ref,
                send_sem=send_sems.at[6 * a + 3 + p], recv_sem=recv_sems.at[6 * a + 3 + p],
                device_id=sibling, device_id_type=MESH)

        sends = []
        for a in range(n):
            for p in range(3):
                cp = over_ici(a, p, mine)
                cp.start()
                sends.append(cp)
        for a in range(n):
            for p, (cx, cy) in enumerate(chips):
                slot = 2 * cx + cy
                over_ici(a, p, slot).wait_recv()
                cp = over_d2d(a, p, slot, c)
                cp.start()
                sends.append(cp)
        for a in range(n):
            for p, (cx, cy) in enumerate(chips):
                over_d2d(a, p, 2 * cx + cy, 1 - c).wait_recv()
        for cp in sends:
            cp.wait_send()

    return pl.pallas_call(
        body, name="gather_weights",
        out_shape=[jax.ShapeDtypeStruct(b.shape, BF16) for b in placed],
        in_specs=[_ANY] * n, out_specs=[_ANY] * n,
        input_output_aliases={a: a for a in range(n)},
        scratch_shapes=[pltpu.SemaphoreType.DMA((6 * n,)), pltpu.SemaphoreType.DMA((6 * n,))],
    )(*placed)


_HBM = pl.BlockSpec(memory_space=pltpu.HBM)
_SEM = pl.BlockSpec(memory_space=pltpu.SEMAPHORE)
_EFFECT = pltpu.SideEffectType.DATAFLOW_SIDE_EFFECTING


def _quarter_halves(shapes, a, which):
    hr = shapes[a][0] // 2
    return pl.ds(which * hr, hr)


def _gather_start(placed, after):
    n = len(placed)
    m = len(after)
    shapes = [b.shape[1:] for b in placed]

    def body(*refs):
        g_refs = refs[:n]
        send_sems, recv_sems = refs[n + m], refs[n + m + 1]
        token = refs[2 * n + m + 2]
        x, y, c = _position()
        chips = [(1 - x, y), (x, 1 - y), (1 - x, 1 - y)]
        mine = 2 * x + y
        for a in range(n):
            ref = g_refs[a].at[mine, _quarter_halves(shapes, a, c), :]
            for p in range(3):
                pltpu.make_async_remote_copy(
                    src_ref=ref, dst_ref=ref, send_sem=send_sems.at[3 * a + p], recv_sem=recv_sems.at[3 * a + p],
                    device_id=(*chips[p], c), device_id_type=MESH).start()
        token[...] = jnp.zeros_like(token)

    out = pl.pallas_call(
        body, name="gather_start",
        out_shape=(pltpu.SemaphoreType.DMA((3 * n,)), pltpu.SemaphoreType.DMA((3 * n,)),
                   *[pltpu.HBM(b.shape, b.dtype) for b in placed], jax.ShapeDtypeStruct((8, LANES), F32)),
        in_specs=[_HBM] * n + [_ANY] * m,
        out_specs=(_SEM, _SEM, *[_HBM] * n, pl.BlockSpec(memory_space=pltpu.VMEM)),
        input_output_aliases={a: 2 + a for a in range(n)},
        compiler_params=pltpu.CompilerParams(has_side_effects=_EFFECT),
    )(*[pltpu.with_memory_space_constraint(b, pltpu.HBM) for b in placed], *after)
    return out[0], out[1], list(out[2:2 + n]), out[2 + n]


def _gather_wait(send_sems, recv_sems, thru, after):
    n = len(thru)
    shapes = [b.shape[1:] for b in thru]

    def body(*refs):
        g_refs = refs[:n]
        send_sems, recv_sems = refs[n], refs[n + 1]
        x, y, c = _position()
        chips = [(1 - x, y), (x, 1 - y), (1 - x, 1 - y)]
        mine = 2 * x + y
        for a in range(n):
            rows = _quarter_halves(shapes, a, c)
            for p, (cx, cy) in enumerate(chips):
                copy = pltpu.make_async_remote_copy(
                    src_ref=g_refs[a].at[mine, rows, :], dst_ref=g_refs[a].at[2 * cx + cy, rows, :],
                    send_sem=send_sems.at[3 * a + p], recv_sem=recv_sems.at[3 * a + p],
                    device_id=(cx, cy, c), device_id_type=MESH)
                copy.wait_send()
                copy.wait_recv()

    return pl.pallas_call(
        body, name="gather_wait",
        out_shape=[pltpu.HBM(b.shape, b.dtype) for b in thru],
        in_specs=[_HBM] * n + [_SEM, _SEM, _ANY], out_specs=[_HBM] * n,
        input_output_aliases={a: a for a in range(n)},
        compiler_params=pltpu.CompilerParams(has_side_effects=_EFFECT),
    )(*thru, send_sems, recv_sems, after)


def _gather_forward(bufs):
    n = len(bufs)
    shapes = [b.shape[1:] for b in bufs]

    def body(*refs):
        g_refs = refs[n:2 * n]
        send_sems, recv_sems = refs[2 * n:]
        x, y, c = _position()
        chips = [(1 - x, y), (x, 1 - y), (1 - x, 1 - y)]

        def over_d2d(a, p, which):
            cx, cy = chips[p]
            ref = g_refs[a].at[2 * cx + cy, _quarter_halves(shapes, a, which), :]
            return pltpu.make_async_remote_copy(
                src_ref=ref, dst_ref=ref, send_sem=send_sems.at[3 * a + p], recv_sem=recv_sems.at[3 * a + p],
                device_id=(x, y, 1 - c), device_id_type=MESH)

        sends = [over_d2d(a, p, c) for a in range(n) for p in range(3)]
        for cp in sends:
            cp.start()
        for a in range(n):
            for p in range(3):
                over_d2d(a, p, 1 - c).wait_recv()
        for cp in sends:
            cp.wait_send()

    return pl.pallas_call(
        body, name="gather_forward",
        out_shape=[jax.ShapeDtypeStruct(b.shape, BF16) for b in bufs],
        in_specs=[_ANY] * n, out_specs=[_ANY] * n,
        input_output_aliases={a: a for a in range(n)},
        scratch_shapes=[pltpu.SemaphoreType.DMA((3 * n,)), pltpu.SemaphoreType.DMA((3 * n,))],
    )(*bufs)


def _sibling_exchange(grads, tag):
    n = len(grads)
    shapes = [g.shape for g in grads]

    def body(*refs):
        g_refs, x_refs = refs[:n], refs[n:2 * n]
        send_sems, recv_sems = refs[2 * n:]
        x, y, c = _position()
        copies = []
        for a in range(n):
            hr = shapes[a][1] // 2
            cp = pltpu.make_async_remote_copy(
                src_ref=g_refs[a].at[:, pl.ds((1 - c) * hr, hr), :], dst_ref=x_refs[a],
                send_sem=send_sems.at[a], recv_sem=recv_sems.at[a],
                device_id=(x, y, 1 - c), device_id_type=MESH)
            cp.start()
            copies.append(cp)
        for cp in copies:
            cp.wait()

    return pl.pallas_call(
        body, name="grad_sibling_exchange_" + tag,
        out_shape=[jax.ShapeDtypeStruct((g.shape[0], g.shape[1] // 2, g.shape[2]), g.dtype) for g in grads],
        in_specs=[_ANY] * n, out_specs=[_ANY] * n,
        scratch_shapes=[pltpu.SemaphoreType.DMA((n,)), pltpu.SemaphoreType.DMA((n,))],
    )(*grads)


def _chip_sums(core, grads, theirs, tag):
    n = len(grads)

    def body(core_ref, *refs):
        g_refs, t_refs, o_refs = refs[:n], refs[n:2 * n], refs[2 * n:]
        for g_ref, t_ref, o_ref in zip(g_refs, t_refs, o_refs):
            o_ref[...] = (g_ref[...] + t_ref[...].astype(F32)).astype(BF16)

    in_specs = [pl.BlockSpec((1, g.shape[1] // 2, g.shape[2]), lambda k, core_ref: (k, core_ref[0], 0)) for g in grads]
    in_specs += [pl.BlockSpec((1,) + t.shape[1:], lambda k, core_ref: (k, 0, 0)) for t in theirs]
    return pl.pallas_call(
        body, name="grad_chip_sums_" + tag,
        grid_spec=pltpu.PrefetchScalarGridSpec(
            num_scalar_prefetch=1, grid=(N_CHIPS,), in_specs=in_specs,
            out_specs=[pl.BlockSpec((1,) + t.shape[1:], lambda k, core_ref: (k, 0, 0)) for t in theirs]),
        out_shape=[jax.ShapeDtypeStruct(t.shape, BF16) for t in theirs],
        compiler_params=_params(("arbitrary",)),
    )(core, *grads, *theirs)


def _chip_exchange_start(sums, after, tag):
    n = len(sums)
    m = len(after)
    lands = [lax.empty((3,) + s.shape[1:], BF16) for s in sums]

    def body(*refs):
        s_refs, y_refs = refs[:n], refs[n:2 * n]
        send_sems, recv_sems = refs[2 * n + m], refs[2 * n + m + 1]
        token = refs[4 * n + m + 2]
        x, y, c = _position()
        chips = [(1 - x, y), (x, 1 - y), (1 - x, 1 - y)]
        for a in range(n):
            for p, (cx, cy) in enumerate(chips):
                pltpu.make_async_remote_copy(
                    src_ref=s_refs[a].at[2 * cx + cy], dst_ref=y_refs[a].at[p],
                    send_sem=send_sems.at[3 * a + p], recv_sem=recv_sems.at[3 * a + p],
                    device_id=(cx, cy, c), device_id_type=MESH).start()
        token[...] = jnp.zeros_like(token)

    both = list(sums) + lands
    out = pl.pallas_call(
        body, name="grad_chip_exchange_start_" + tag,
        out_shape=(pltpu.SemaphoreType.DMA((3 * n,)), pltpu.SemaphoreType.DMA((3 * n,)),
                   *[pltpu.HBM(b.shape, b.dtype) for b in both], jax.ShapeDtypeStruct((8, LANES), F32)),
        in_specs=[_HBM] * (2 * n) + [_ANY] * m,
        out_specs=(_SEM, _SEM, *[_HBM] * (2 * n), pl.BlockSpec(memory_space=pltpu.VMEM)),
        input_output_aliases={a: 2 + a for a in range(2 * n)},
        compiler_params=pltpu.CompilerParams(has_side_effects=_EFFECT),
    )(*[pltpu.with_memory_space_constraint(b, pltpu.HBM) for b in both], *after)
    return out[0], out[1], list(out[2:2 + n]), list(out[2 + n:2 + 2 * n]), out[2 + 2 * n]


def _chip_exchange_wait(send_sems, recv_sems, sums, lands, after, tag):
    n = len(sums)

    def body(*refs):
        s_refs, y_refs = refs[:n], refs[n:2 * n]
        send_sems, recv_sems = refs[2 * n], refs[2 * n + 1]
        x, y, c = _position()
        chips = [(1 - x, y), (x, 1 - y), (1 - x, 1 - y)]
        for a in range(n):
            for p, (cx, cy) in enumerate(chips):
                copy = pltpu.make_async_remote_copy(
                    src_ref=s_refs[a].at[2 * cx + cy], dst_ref=y_refs[a].at[p],
                    send_sem=send_sems.at[3 * a + p], recv_sem=recv_sems.at[3 * a + p],
                    device_id=(cx, cy, c), device_id_type=MESH)
                copy.wait_send()
                copy.wait_recv()

    both = list(sums) + list(lands)
    out = pl.pallas_call(
        body, name="grad_chip_exchange_wait_" + tag,
        out_shape=[pltpu.HBM(b.shape, b.dtype) for b in both],
        in_specs=[_HBM] * (2 * n) + [_SEM, _SEM, _ANY], out_specs=[_HBM] * (2 * n),
        input_output_aliases={a: a for a in range(2 * n)},
        compiler_params=pltpu.CompilerParams(has_side_effects=_EFFECT),
    )(*both, send_sems, recv_sems, after)
    return list(out[:n]), list(out[n:])


def _total_sums(place, sums, parts, after, tag):
    n = len(parts)
    m = len(after)
    steps = 2

    def body(place_ref, *refs):
        for s_ref, y_ref, o_ref in zip(refs[:n], refs[n:2 * n], refs[2 * n + m:]):
            o_ref[0] = ((s_ref[0].astype(F32) + y_ref[0].astype(F32)) + y_ref[1].astype(F32)) + y_ref[2].astype(F32)

    def step_rows(pt):
        return pt.shape[1] // steps

    in_specs = [pl.BlockSpec((1, step_rows(s), s.shape[2]), lambda r, place_ref: (place_ref[0], r, 0)) for s in sums]
    in_specs += [pl.BlockSpec((3, step_rows(pt), pt.shape[2]), lambda r, place_ref: (0, r, 0)) for pt in parts]
    in_specs += [_ANY] * m
    return pl.pallas_call(
        body, name="grad_total_sums_" + tag,
        grid_spec=pltpu.PrefetchScalarGridSpec(
            num_scalar_prefetch=1, grid=(steps,), in_specs=in_specs,
            out_specs=[pl.BlockSpec((1, step_rows(pt), pt.shape[2]), lambda r, place_ref: (place_ref[1], r, 0))
                       for pt in parts]),
        out_shape=[jax.ShapeDtypeStruct((2,) + pt.shape[1:], F32) for pt in parts],
        compiler_params=_params(("arbitrary",)),
    )(place, *sums, *parts, *after)


def _sibling_share(halves, tag):
    n = len(halves)

    def body(*refs):
        f_refs = refs[n:2 * n]
        send_sems, recv_sems = refs[2 * n:]
        x, y, c = _position()
        copies = []
        for a in range(n):
            cp = pltpu.make_async_remote_copy(
                src_ref=f_refs[a].at[c], dst_ref=f_refs[a].at[c], send_sem=send_sems.at[a], recv_sem=recv_sems.at[a],
                device_id=(x, y, 1 - c), device_id_type=MESH)
            cp.start()
            copies.append(cp)
        for a, cp in enumerate(copies):
            cp.wait_send()
            pltpu.make_async_remote_copy(
                src_ref=f_refs[a].at[1 - c], dst_ref=f_refs[a].at[1 - c], send_sem=send_sems.at[a],
                recv_sem=recv_sems.at[a], device_id=(x, y, c), device_id_type=MESH).wait_recv()

    return pl.pallas_call(
        body, name="grad_sibling_share_" + tag,
        out_shape=[jax.ShapeDtypeStruct(h.shape, F32) for h in halves],
        in_specs=[_ANY] * n, out_specs=[_ANY] * n,
        input_output_aliases={a: a for a in range(n)},
        scratch_shapes=[pltpu.SemaphoreType.DMA((n,)), pltpu.SemaphoreType.DMA((n,))],
    )(*halves)


def _group_sum(stacked, nrow, name):
    total, n = stacked.shape
    groups = total // nrow

    def body(g_ref, o_ref):
        acc = g_ref[0:nrow, :]
        for grp in range(1, groups):
            acc = acc + g_ref[grp * nrow:(grp + 1) * nrow, :]
        o_ref[...] = acc

    return pl.pallas_call(
        body, name=name,
        out_shape=jax.ShapeDtypeStruct((nrow, n), F32),
        compiler_params=pltpu.CompilerParams(vmem_limit_bytes=VMEM_LIMIT),
    )(stacked)


def _local_step(xt, tgt, mod, gains, w_pool, pool_scale, w_in, later_weights, on_ffn_grads, seq):
    g_mpre, g_mpost, g_fpre, g_fpost = gains
    d = xt.shape[1]
    tm, tq = min(TOKEN_TILE, seq), min(ATTN_TILE, seq)

    h1, qn, k, v, u, kt, vt = _prenorm_proj(xt, mod, g_mpre, w_in, seq, tm)
    tk = min(ATTN_KEY_TILE, tq // 2)
    o, ltot = _attn_fwd(qn, k, vt, seq, tq, tk)
    w_out, w_g, w_u, w_d = later_weights(o)
    w_out2 = w_out.reshape(d, d)
    pooled, mixin, mix, x1, h2 =_mixer_post(u, o, xt, mod, g_mpost, g_fpre, w_pool, pool_scale, w_out2, seq, tm)
    a, b, fin, dy, df, loss_blk, accb4, accg4 = _ffn_fwd(h2, w_g, w_u, w_d, x1, tgt, mod, g_fpost, seq, tm)
    da, db, dx1, dmix, accb5, accg5 = _ffn_bwd(df, a, b, w_d, w_g, w_u, x1, dy, mix, mod, g_fpre, g_mpost, seq, tm)
    bt = min(GRAD_TOKEN_TILE, xt.shape[0])
    bt_one = min(2 * GRAD_TOKEN_TILE, xt.shape[0])
    (g_g,), (g_g16,) = _tn_matmul(da, [h2], w_g.shape[0], bt_one, "grad_w_gate")
    (g_u,), (g_u16,) = _tn_matmul(db, [h2], w_u.shape[0], bt_one, "grad_w_up")
    (g_d,), (g_d16,) = _tn_matmul(fin, [df], w_d.shape[0], bt_one, "grad_w_down")
    token = on_ffn_grads([g_g, g_u, g_d], [g_g16, g_u16, g_d16])
    do, dpd, dps, dwp = _mixer_bwd(dmix, w_out2, pooled, w_pool, pool_scale + token, seq, tm)
    dq, dk, dv = _attn_bwd(qn, k, kt, v, do, ltot, seq, tq, tk)
    gx, du, accb8, accg8 = _inproj_bwd(dq, dk, dv, dpd, xt, dx1, mod, g_mpre, w_in, seq, tm)

    g_in, g_in16 = [jnp.concatenate(parts, axis=0) for parts in _tn_matmul(h1, [dq, dk, dv, du], 1, bt, "grad_w_in")]
    g_out, g_out16 = [parts[0].reshape(w_out.shape) for parts in _tn_matmul(mixin, [dmix], 1, bt_one, "grad_w_out")]

    dmod = jnp.stack([accb8[:, 0], accb8[:, 1], accb5[:, 2], accb5[:, 0], accb5[:, 1], accb4[:, 0]], axis=1)
    dgain = jnp.stack([accg8[0], accg5[1], accg5[0], accg4[0]], axis=0)
    grads = [g_in, g_out, g_g, g_u, g_d]
    grads16 = [g_in16, g_out16, g_g16, g_u16, g_d16]
    return loss_blk, gx, grads, grads16, dmod, dgain, dps[0:1], dwp


def kernel(x, c, w_cond, b_cond, g_mix_pre, g_mix_post, w_in, w_pool, pool_scale, w_out, g_ffn_pre, g_ffn_post, w_gate, w_up, w_down, loss_target, m_w_cond, m_b_cond, m_g_mix_pre, m_g_mix_post, m_w_in, m_w_pool, m_pool_scale, m_w_out, m_g_ffn_pre, m_g_ffn_post, m_w_gate, m_w_up, m_w_down, v_w_cond, v_b_cond, v_g_mix_pre, v_g_mix_post, v_w_in, v_w_pool, v_pool_scale, v_w_out, v_g_ffn_pre, v_g_ffn_post, v_w_gate, v_w_up, v_w_down):
    xi, yi, ci = _position()
    chip = 2 * xi + yi
    dev = 4 * xi + 2 * yi + ci
    nb, seq, d = x.shape
    t_all = nb * seq
    xt = x.reshape(t_all, d)
    tgt = loss_target.reshape(t_all, d)
    ncol = w_cond.shape[2]
    pw = pool_scale.shape[1]

    c_pad = jnp.concatenate([c, jnp.zeros((8 - nb, d), F32)], axis=0)
    c_all = _all_gather(c_pad, "gather_c").reshape(N_DEV, 8, d)[:, :nb].reshape(N_DEV * nb, d)
    b_q = lax.dynamic_slice(b_cond, (0, chip * ncol), (1, ncol))
    sc_all, mod_q = _cond_fwd(c_all, w_cond[0], b_q, 512)
    mod_parts = _all_gather(mod_q, "gather_mod").reshape(N_DEV, N_DEV * nb, ncol)
    mod_rows = lax.dynamic_slice(mod_parts, (0, dev * nb, 0), (N_DEV, nb, ncol))[0::2]
    mod = jnp.transpose(mod_rows, (1, 0, 2)).reshape(nb, N_MOD, d)
    mod = jnp.concatenate([mod, jnp.zeros((nb, MOD_ROWS - N_MOD, d), F32)], axis=1)

    place = jnp.stack([chip, ci]).astype(jnp.int32)
    turned = lambda t: jnp.swapaxes(t[0], 0, 1)
    placed = _place_quarters(place, [w_in[0], w_out[0], turned(w_gate), turned(w_up), w_down[0]])
    (w_in_all,) = _gather_weights(placed[:1])
    send_sems, recv_sems, in_flight, token = _gather_start(placed[1:], [mod, w_in_all])
    mod = mod + token[0:1, 0:1]

    def later_weights(after):
        return _gather_forward(_gather_wait(send_sems, recv_sems, in_flight, after))

    ffn_split = []

    def on_ffn_grads(ffn_grads, ffn_grads16):
        theirs = _sibling_exchange(ffn_grads16, "ffn")
        ffn_split.extend(_chip_exchange_start(_chip_sums(place[1:], ffn_grads, theirs, "ffn"), [], "ffn"))
        return ffn_split[4][0:1, 0:1]

    gains = (g_mix_pre, g_mix_post, g_ffn_pre, g_ffn_post)
    loss_blk, gx, grads, grads16, dmod, dgain, dps, dwp = _local_step(
        xt, tgt, mod, gains, w_pool[0], pool_scale, w_in_all, later_weights, on_ffn_grads, seq)

    sums_ffn, parts_ffn = _chip_exchange_wait(*ffn_split[:4], gx, "ffn")

    wp_rows = dwp.size // d
    loss_row = 2 * N_MOD + 4 + 1
    pad_rows = 24 - (loss_row + 1)
    payload = jnp.concatenate([
        dmod.reshape(nb * N_MOD, d), dgain,
        jnp.concatenate([dps, jnp.zeros((1, d - pw), F32)], axis=1),
        jnp.concatenate([loss_blk[0:1], jnp.zeros((1, d - LANES), F32)], axis=1),
        jnp.zeros((pad_rows, d), F32), dwp.reshape(wp_rows, d)], axis=0)
    prow = payload.shape[0]
    gathered = _all_gather(payload, "gather_small")
    summed = _group_sum(gathered, prow, "small_device_sum")
    loss = summed[loss_row, 0]
    dmod_all = gathered.reshape(N_DEV, prow, d)[:, :nb * N_MOD].reshape(N_DEV * nb, N_MOD * d)
    g_b_cond = _group_sum(dmod_all, 1, "grad_b_cond")
    dmod_q = lax.dynamic_slice(dmod_all, (0, chip * ncol), (N_DEV * nb, ncol))
    g_w_cond = _cond_bwd(sc_all, dmod_q, 512)
    first_gain = 2 * N_MOD
    g_gains = [summed[first_gain + r:first_gain + r + 1] for r in range(4)]
    g_pool_scale = summed[first_gain + 4:first_gain + 5, :pw]
    g_w_pool = summed[24:24 + wp_rows].reshape(w_pool.shape[1] * w_pool.shape[2], w_pool.shape[3])

    theirs = _sibling_exchange(grads16[:2], "mix")
    mix_split = _chip_exchange_start(_chip_sums(place[1:], grads[:2], theirs, "mix"), [gathered], "mix")
    unfold = lambda halves: [g.reshape(2 * g.shape[1], g.shape[2]) for g in halves]
    g_ffn = unfold(_sibling_share(_total_sums(place, sums_ffn, parts_ffn, [mix_split[4]], "ffn"), "ffn"))

    flat_pool = lambda t: t.reshape(g_w_pool.shape)
    results = {}

    def update(name, w2, g2, m2, v2, shape):
        delta, new_m, new_v = _adamw(w2, g2, m2, v2, "adamw_" + name)
        back = (lambda t: jnp.swapaxes(t, 0, 1)[None]) if shape is None else (lambda t: t.reshape(shape))
        results[name] = [back(t) for t in (g2, delta, new_m, new_v)]
        return delta

    done = [update("w_gate", turned(w_gate), g_ffn[0], turned(m_w_gate), turned(v_w_gate), None),
            update("w_up", turned(w_up), g_ffn[1], turned(m_w_up), turned(v_w_up), None),
            update("w_down", w_down[0], g_ffn[2], m_w_down[0], v_w_down[0], w_down.shape),
            update("w_cond", w_cond[0], g_w_cond, m_w_cond[0], v_w_cond[0], w_cond.shape)]
    update("b_cond", b_cond, g_b_cond, m_b_cond, v_b_cond, b_cond.shape)
    update("g_mix_pre", g_mix_pre, g_gains[0], m_g_mix_pre, v_g_mix_pre, g_mix_pre.shape)
    update("g_mix_post", g_mix_post, g_gains[1], m_g_mix_post, v_g_mix_post, g_mix_post.shape)
    update("w_pool", flat_pool(w_pool), g_w_pool, flat_pool(m_w_pool), flat_pool(v_w_pool), w_pool.shape)
    update("pool_scale", pool_scale, g_pool_scale, m_pool_scale, v_pool_scale, pool_scale.shape)
    update("g_ffn_pre", g_ffn_pre, g_gains[2], m_g_ffn_pre, v_g_ffn_pre, g_ffn_pre.shape)
    update("g_ffn_post", g_ffn_post, g_gains[3], m_g_ffn_post, v_g_ffn_post, g_ffn_post.shape)

    sums_mix, parts_mix = _chip_exchange_wait(*mix_split[:4], done[-1], "mix")
    g_mix = unfold(_sibling_share(_total_sums(place, sums_mix, parts_mix, done[:3], "mix"), "mix"))
    update("w_in", w_in[0], g_mix[0], m_w_in[0], v_w_in[0], w_in.shape)
    update("w_out", w_out[0], g_mix[1], m_w_out[0], v_w_out[0], w_out.shape)

    names = ("w_cond", "b_cond", "g_mix_pre", "g_mix_post", "w_in", "w_pool", "pool_scale", "w_out",
             "g_ffn_pre", "g_ffn_post", "w_gate", "w_up", "w_down")
    outs = [results[name][part] for part in range(4) for name in names]
    return (loss, gx.reshape(x.shape), *outs)
```

```python
import functools

import jax
import jax.numpy as jnp
from jax import lax
from jax.experimental import pallas as pl
from jax.experimental.pallas import tpu as pltpu

F32 = jnp.float32
BF16 = jnp.bfloat16
MESH = pl.DeviceIdType.MESH

EPS = 1e-6
HEAD_DIM = 64
HEADS_PER_BLOCK = 2
LANES = 128
NEG_QK_SCALE = -0.125
POOL_WINDOWS = (2, 4, 8, 16)
POOL_GROUP = 128
HALO = 16
N_MOD = 6
MOD_ROWS = 8
N_CHIPS = 4
N_DEV = 8
VMEM_LIMIT = 56 * 1024 * 1024

ADAM_LR = 0.001
ADAM_B1 = 0.9
ADAM_B2 = 0.999
ADAM_EPS = 1e-08
ADAM_WD = 0.01
ADAM_STEP = 10

TOKEN_TILE = 512
GRAD_TOKEN_TILE = 2048
FFN_ROW_CHUNKS = 2
ROW_CHUNKS = 2
ATTN_TILE = 512
ATTN_KEY_TILE = 256
ATTN_ROW_CHUNK = 32
LOG_SUM_PASSES = 1


def _dot(a, b):
    return jnp.dot(a, b, preferred_element_type=F32)


def _dot_nt(a, b):
    return lax.dot_general(a, b, (((1,), (1,)), ((), ())), preferred_element_type=F32)


def _dot_tn(a, b):
    return lax.dot_general(a, b, (((0,), (0,)), ((), ())), preferred_element_type=F32)


def _split(v):
    hi = v.astype(BF16)
    lo = (v - hi.astype(F32)).astype(BF16)
    return hi, lo


def _rms(v):
    return lax.rsqrt(jnp.mean(v * v, axis=-1, keepdims=True) + EPS)


def _norm_bwd(dn, n, r):
    return r * (dn - n * jnp.mean(dn * n, axis=-1, keepdims=True))


def _sigmoid(v):
    return 0.5 * jnp.tanh(0.5 * v) + 0.5


def _colsum(v):
    return jnp.sum(v, axis=0, keepdims=True)


def _params(sem=None):
    return pltpu.CompilerParams(dimension_semantics=sem, vmem_limit_bytes=VMEM_LIMIT)


def _position():
    return lax.axis_index("x"), lax.axis_index("y"), lax.axis_index("c")


def _prenorm_proj(x, mod, g_pre, w_in, seq, tm):
    t_all, d = x.shape
    nt = seq // tm
    p = w_in.shape[2]

    def body(x_ref, mod_ref, g_ref, w_ref, h_ref, q_ref, k_ref, v_ref, u_ref, kt_ref, vt_ref):
        for c in range(ROW_CHUNKS):
            rows = slice(c * (tm // ROW_CHUNKS), (c + 1) * (tm // ROW_CHUNKS))
            xf = x_ref[rows, :]
            n = xf * _rms(xf)
            h = (n * g_ref[...]) * (1.0 + mod_ref[0, 1:2, :]) + mod_ref[0, 0:1, :]
            hb = h.astype(BF16)
            h_ref[rows, :] = hb
            q_ref[rows, :] = (_dot(hb, w_ref[0]) * NEG_QK_SCALE).astype(BF16)
            kf = _dot(hb, w_ref[1])
            vf = _dot(hb, w_ref[2])
            k_ref[rows, :] = kf.astype(BF16)
            v_ref[rows, :] = vf.astype(BF16)
            kt_ref[:, rows] = kf.T.astype(BF16)
            vt_ref[:, rows] = vf.T.astype(BF16)
            u_ref[rows, :] = _dot(hb, w_ref[3])

    tok = lambda i: (i, 0)
    tok_t = lambda i: (0, i)
    return pl.pallas_call(
        body, name="prenorm_proj", grid=(t_all // tm,),
        in_specs=[pl.BlockSpec((tm, d), tok),
                  pl.BlockSpec((1, MOD_ROWS, d), lambda i: (i // nt, 0, 0)),
                  pl.BlockSpec((1, d), lambda i: (0, 0)),
                  pl.BlockSpec((N_CHIPS, d, p), lambda i: (0, 0, 0))],
        out_specs=[pl.BlockSpec((tm, d), tok)] + [pl.BlockSpec((tm, p), tok)] * 4 + [pl.BlockSpec((p, tm), tok_t)] * 2,
        out_shape=[jax.ShapeDtypeStruct((t_all, d), BF16)] + [jax.ShapeDtypeStruct((t_all, p), BF16)] * 3
        + [jax.ShapeDtypeStruct((t_all, p), F32)] + [jax.ShapeDtypeStruct((p, t_all), BF16)] * 2,
        compiler_params=_params(("arbitrary",)),
    )(x, mod, g_pre, w_in)


def _tri_matrix(tk, kind):
    j = lax.broadcasted_iota(jnp.int32, (2 * tk, tk), 0) % tk
    s = lax.broadcasted_iota(jnp.int32, (2 * tk, tk), 1)
    return {"after": j > s, "upto": j <= s, "before": j < s}[kind].astype(BF16)


def _row_sums(v):
    return jnp.broadcast_to(jnp.sum(v, axis=-1, keepdims=True), (v.shape[0], LANES))


def _across(v, n):
    return jnp.concatenate([v] * (n // LANES), axis=1)


def _all_masked(c, diag, rc, tk):
    return diag is not None and diag * tk >= (c + 1) * rc - 1


def _some_masked(c, diag, rc, tk):
    return diag is not None and diag * tk + tk - 1 >= c * rc


def _attn_fwd(qn, k, vt, seq, tq, tk):
    t_all, w = qn.shape
    nb, nq, ndiag = t_all // seq, seq // tq, tq // tk
    assert ndiag % 2 == 0, "two key blocks per loop trip"
    rc = ATTN_ROW_CHUNK
    heads = range(HEADS_PER_BLOCK)

    def body(q_ref, k_ref, vt_ref, tri_ref, o_ref, l_ref,
             z_buf, ls_buf, hl_buf, aft_buf, w_buf, tot_buf, acc_t, run_buf):
        i = pl.program_id(2)
        nblk = (i + 1) * ndiag
        lane = lax.broadcasted_iota(jnp.int32, (1, LANES), 1)
        sub = lax.broadcasted_iota(jnp.int32, (LANES, 1), 0)
        row = lax.broadcasted_iota(jnp.int32, (rc, tk), 0)
        col = lax.broadcasted_iota(jnp.int32, (rc, tk), 1)
        first = lane < HEAD_DIM
        q2 = q_ref[...]
        qs = [jnp.where(first, q2, jnp.zeros_like(q2)), jnp.where(first, jnp.zeros_like(q2), q2)]
        acc_t[...] = jnp.zeros_like(acc_t)
        run_buf[...] = jnp.zeros_like(run_buf)
        w_buf[1] = jnp.zeros((HEADS_PER_BLOCK, tq, tk), BF16)

        def causal(c, diag):
            return (col + diag * tk) < (row + c * rc)

        def scores(blk, slot):
            kj = k_ref[pl.ds(pl.multiple_of(blk * tk, tk), tk), :]
            for h in heads:
                z_buf[slot, h] = _dot_nt(qs[h], kj)

        def values(blk, slot):
            vtj = vt_ref[:, pl.ds(pl.multiple_of(blk * tk, tk), tk)]
            zero = jnp.zeros_like(vtj)
            acc_t[...] += (_dot_nt(jnp.where(sub < HEAD_DIM, vtj, zero), w_buf[slot, 0])
                           + _dot_nt(jnp.where(sub < HEAD_DIM, zero, vtj), w_buf[slot, 1]))

        def softplus_stage(h, slot, diag):
            for c in range(tq // rc):
                rows = slice(c * rc, (c + 1) * rc)
                if _all_masked(c, diag, rc, tk):
                    hl_buf[h, rows, :] = jnp.zeros((rc, LOG_SUM_PASSES * tk), BF16)
                    tot_buf[h, rows, :] = jnp.zeros((rc, LANES), F32)
                    continue
                nz = z_buf[slot, h, rows, :]
                l1 = jnp.minimum(nz, 0.0) - jnp.log(1.0 + jnp.exp(-jnp.abs(nz)))
                if _some_masked(c, diag, rc, tk):
                    l1 = jnp.where(causal(c, diag), l1, 0.0)
                for s, part in enumerate(_split(l1)[:LOG_SUM_PASSES]):
                    hl_buf[h, rows, s * tk:(s + 1) * tk] = part
                ls_buf[h, rows, :] = l1 - nz
                tot_buf[h, rows, :] = _row_sums(l1)

        def weights_stage(h, slot, diag):
            for c in range(tq // rc):
                rows = slice(c * rc, (c + 1) * rc)
                if _all_masked(c, diag, rc, tk):
                    w_buf[slot, h, rows, :] = jnp.zeros((rc, tk), BF16)
                    continue
                wgt = jnp.exp((ls_buf[h, rows, :] + aft_buf[h, rows, :]) + _across(run_buf[h, rows, :], tk))
                if _some_masked(c, diag, rc, tk):
                    wgt = jnp.where(causal(c, diag), wgt, 0.0)
                w_buf[slot, h, rows, :] = wgt.astype(BF16)
                run_buf[h, rows, :] += tot_buf[h, rows, :]

        def position(blk, slot, diag):
            scores(jnp.maximum(blk - 1, 0), 1 - slot)
            for h in heads:
                softplus_stage(h, slot, diag)
                aft_buf[h] = _dot(hl_buf[h], tri_ref[...])
            values(jnp.minimum(blk + 1, nblk - 1), 1 - slot)
            for h in heads:
                weights_stage(h, slot, diag)

        scores(nblk - 1, 0)
        for p in range(ndiag):
            position(nblk - 1 - p, p % 2, ndiag - 1 - p)

        def trip(jj, carry):
            for u in range(2):
                position(i * ndiag - 1 - 2 * jj - u, u, None)
            return carry

        lax.fori_loop(0, (i * ndiag) // 2, trip, 0)
        values(0, 1)
        o_ref[...] = acc_t[...].T.astype(BF16)
        l_ref[...] = jnp.where(first, run_buf[0], run_buf[1])

    qmap = lambda b, hp, i: (b * nq + i, hp)
    nh = HEADS_PER_BLOCK
    return pl.pallas_call(
        body, name="attn_fwd", grid=(nb, w // LANES, nq),
        in_specs=[pl.BlockSpec((tq, LANES), qmap), pl.BlockSpec((seq, LANES), lambda b, hp, i: (b, hp)),
                  pl.BlockSpec((LANES, seq), lambda b, hp, i: (hp, b)),
                  pl.BlockSpec((LOG_SUM_PASSES * tk, tk), lambda b, hp, i: (0, 0))],
        out_specs=[pl.BlockSpec((tq, LANES), qmap), pl.BlockSpec((tq, LANES), qmap)],
        out_shape=[jax.ShapeDtypeStruct((t_all, w), BF16), jax.ShapeDtypeStruct((t_all, w), F32)],
        scratch_shapes=[pltpu.VMEM((2, nh, tq, tk), F32), pltpu.VMEM((nh, tq, tk), F32),
                        pltpu.VMEM((nh, tq, LOG_SUM_PASSES * tk), BF16), pltpu.VMEM((nh, tq, tk), F32),
                        pltpu.VMEM((2, nh, tq, tk), BF16), pltpu.VMEM((nh, tq, LANES), F32),
                        pltpu.VMEM((LANES, tq), F32), pltpu.VMEM((nh, tq, LANES), F32)],
        compiler_params=_params(("arbitrary", "arbitrary", "arbitrary")),
    )(qn, k, vt, _tri_matrix(tk, "after")[:LOG_SUM_PASSES * tk])


def _window_sums(ext, rows, offset, forward):
    r = lax.broadcasted_iota(jnp.int32, (rows, rows + HALO), 0)
    e = lax.broadcasted_iota(jnp.int32, (rows, rows + HALO), 1)
    hi, lo = _split(ext)
    out = []
    for g, win in enumerate(POOL_WINDOWS):
        if forward:
            band = (e >= r) & (e < r + win)
        else:
            band = (e <= r + offset) & (e > r + offset - win)
        bm = band.astype(BF16)
        cols = slice(g * POOL_GROUP, (g + 1) * POOL_GROUP)
        out.append(_dot(bm, hi[:, cols]) + _dot(bm, lo[:, cols]))
    return out


def _window_counts(pos):
    return [jnp.minimum(pos + 1, win).astype(F32) for win in POOL_WINDOWS]


def _mixer_post(u, o, x, mod, g_post, g_fpre, w_pool, pool_scale, w_out, seq, tm):
    t_all, d = x.shape
    nt = seq // tm
    p = u.shape[1]

    def body(u_ref, halo_ref, o_ref, x_ref, mod_ref, gp_ref, gf_ref, wp_ref, ps_ref, wo_ref,
             pooled_ref, mixin_ref, mix_ref, x1_ref, h2_ref):
        it = pl.program_id(0) % nt
        uf = u_ref[...]
        halo = jnp.where(it == 0, 0.0, halo_ref[...])
        ext = jnp.concatenate([halo, uf], axis=0)
        pos = it * tm + lax.broadcasted_iota(jnp.int32, (tm, 1), 0)
        sums = _window_sums(ext, tm, HALO, False)
        cnts = _window_counts(pos)
        pools = []
        for g in range(len(POOL_WINDOWS)):
            cols = slice(g * POOL_GROUP, (g + 1) * POOL_GROUP)
            pooled = (sums[g] / cnts[g] - uf[:, cols]).astype(BF16)
            pooled_ref[:, cols] = pooled
            yg = _dot(pooled, wp_ref[g].astype(BF16))
            pools.append((yg * ps_ref[:, cols]).astype(BF16))
        mixin_ref[...] = jnp.concatenate([o_ref[...]] + pools, axis=1)
        for c in range(ROW_CHUNKS):
            rows = slice(c * (tm // ROW_CHUNKS), (c + 1) * (tm // ROW_CHUNKS))
            mix = _dot(mixin_ref[rows, :], wo_ref[...])
            mix_ref[rows, :] = mix
            n2 = mix * _rms(mix)
            x1 = x_ref[rows, :] + mod_ref[0, 2:3, :] * (n2 * gp_ref[...])
            x1_ref[rows, :] = x1
            n3 = x1 * _rms(x1)
            h2 = (n3 * gf_ref[...]) * (1.0 + mod_ref[0, 4:5, :]) + mod_ref[0, 3:4, :]
            h2_ref[rows, :] = h2.astype(BF16)

    tok = lambda i: (i, 0)
    const2 = lambda i: (0, 0)
    hb = tm // HALO
    return pl.pallas_call(
        body, name="mixer_post", grid=(t_all // tm,),
        in_specs=[pl.BlockSpec((tm, p), tok),
                  pl.BlockSpec((HALO, p), lambda i: (jnp.maximum(i * hb - 1, 0), 0)),
                  pl.BlockSpec((tm, p), tok),
                  pl.BlockSpec((tm, d), tok),
                  pl.BlockSpec((1, MOD_ROWS, d), lambda i: (i // nt, 0, 0)),
                  pl.BlockSpec((1, d), const2), pl.BlockSpec((1, d), const2),
                  pl.BlockSpec(w_pool.shape, lambda i: (0, 0, 0)),
                  pl.BlockSpec((1, p), const2),
                  pl.BlockSpec((d, d), const2)],
        out_specs=[pl.BlockSpec((tm, p), tok), pl.BlockSpec((tm, d), tok), pl.BlockSpec((tm, d), tok),
                   pl.BlockSpec((tm, d), tok), pl.BlockSpec((tm, d), tok)],
        out_shape=[jax.ShapeDtypeStruct((t_all, p), BF16), jax.ShapeDtypeStruct((t_all, d), BF16),
                   jax.ShapeDtypeStruct((t_all, d), F32), jax.ShapeDtypeStruct((t_all, d), F32),
                   jax.ShapeDtypeStruct((t_all, d), BF16)],
        compiler_params=_params(("arbitrary",)),
    )(u, u, o, x, mod, g_post, g_fpre, w_pool, pool_scale, w_out)


def _ffn_fwd(h2, w_g, w_u, w_d, x1, tgt, mod, g_post, seq, tm):
    t_all, d = x1.shape
    nt = seq // tm
    nk, ff, _ = w_g.shape

    def body(h_ref, wg_ref, wu_ref, wd_ref, x1_ref, t_ref, mod_ref, g_ref,
             a_ref, b_ref, fin_ref, dy_ref, df_ref, loss_ref, accb_ref, accg_ref, facc):
        i, k = pl.program_id(0), pl.program_id(1)

        @pl.when(k == 0)
        def _():
            facc[...] = jnp.zeros_like(facc)

        for c in range(FFN_ROW_CHUNKS):
            rows = slice(c * (tm // FFN_ROW_CHUNKS), (c + 1) * (tm // FFN_ROW_CHUNKS))
            hb = h_ref[rows, :]
            a = _dot_nt(hb, wg_ref[0])
            b = _dot_nt(hb, wu_ref[0])
            a_ref[0, rows, :] = a.astype(BF16)
            b_ref[0, rows, :] = b.astype(BF16)
            fin = ((a * _sigmoid(a)) * b).astype(BF16)
            fin_ref[0, rows, :] = fin
            facc[rows, :] += _dot(fin, wd_ref[0])

        @pl.when(k == nk - 1)
        def _():
            f = facc[...]
            r4 = _rms(f)
            n4 = f * r4
            gate = mod_ref[0, 5:6, :]
            g = g_ref[...]
            err = (x1_ref[...] + gate * (n4 * g)) - t_ref[...]
            dy = err * (1.0 / d)
            dy_ref[...] = dy

            @pl.when(i == 0)
            def _():
                loss_ref[...] = jnp.zeros_like(loss_ref)
                accg_ref[...] = jnp.zeros_like(accg_ref)

            @pl.when(i % nt == 0)
            def _():
                accb_ref[...] = jnp.zeros_like(accb_ref)

            loss_ref[...] += (0.5 / d) * jnp.sum(err * err)
            accb_ref[0, 0:1, :] += _colsum(dy * (n4 * g))
            accg_ref[0:1, :] += _colsum((dy * gate) * n4)
            dn4 = (dy * gate) * g
            df_ref[...] = _norm_bwd(dn4, n4, r4).astype(BF16)

    tok = lambda i, k: (i, 0)
    ktok = lambda i, k: (k, i, 0)
    kw = lambda i, k: (k, 0, 0)
    const2 = lambda i, k: (0, 0)
    return pl.pallas_call(
        body, name="ffn_fwd", grid=(t_all // tm, nk),
        in_specs=[pl.BlockSpec((tm, d), tok),
                  pl.BlockSpec((1, ff, d), kw), pl.BlockSpec((1, ff, d), kw), pl.BlockSpec((1, ff, d), kw),
                  pl.BlockSpec((tm, d), tok), pl.BlockSpec((tm, d), tok),
                  pl.BlockSpec((1, MOD_ROWS, d), lambda i, k: (i // nt, 0, 0)),
                  pl.BlockSpec((1, d), const2)],
        out_specs=[pl.BlockSpec((1, tm, ff), ktok)] * 3
        + [pl.BlockSpec((tm, d), tok), pl.BlockSpec((tm, d), tok),
           pl.BlockSpec((8, LANES), const2),
           pl.BlockSpec((1, 8, d), lambda i, k: (i // nt, 0, 0)),
           pl.BlockSpec((8, d), const2)],
        out_shape=[jax.ShapeDtypeStruct((nk, t_all, ff), BF16)] * 3
        + [jax.ShapeDtypeStruct((t_all, d), F32), jax.ShapeDtypeStruct((t_all, d), BF16),
           jax.ShapeDtypeStruct((8, LANES), F32),
           jax.ShapeDtypeStruct((t_all // seq, 8, d), F32),
           jax.ShapeDtypeStruct((8, d), F32)],
        scratch_shapes=[pltpu.VMEM((tm, d), F32)],
        compiler_params=_params(("arbitrary", "arbitrary")),
    )(h2, w_g, w_u, w_d, x1, tgt, mod, g_post)


def _ffn_bwd(df, a, b, w_d, w_g, w_u, x1, dy, mix, mod, g_fpre, g_mpost, seq, tm):
    t_all, d = x1.shape
    nt = seq // tm
    nk, ff, _ = w_g.shape

    def body(df_ref, a_ref, b_ref, wd_ref, wg_ref, wu_ref, x1_ref, dy_ref, mix_ref, mod_ref, gf_ref, gm_ref,
             da_ref, db_ref, dx1_ref, dmix_ref, accb_ref, accg_ref, hacc):
        i, k = pl.program_id(0), pl.program_id(1)

        @pl.when(k == 0)
        def _():
            hacc[...] = jnp.zeros_like(hacc)

        for c in range(FFN_ROW_CHUNKS):
            rows = slice(c * (tm // FFN_ROW_CHUNKS), (c + 1) * (tm // FFN_ROW_CHUNKS))
            dfin = _dot_nt(df_ref[rows, :], wd_ref[0])
            af = a_ref[0, rows, :].astype(F32)
            bf = b_ref[0, rows, :].astype(F32)
            sig = _sigmoid(af)
            da = ((dfin * bf) * (sig * (1.0 + af * (1.0 - sig)))).astype(BF16)
            db = (dfin * (af * sig)).astype(BF16)
            da_ref[0, rows, :] = da
            db_ref[0, rows, :] = db
            hacc[rows, :] += _dot(da, wg_ref[0]) + _dot(db, wu_ref[0])

        @pl.when(k == nk - 1)
        def _():
            @pl.when(i == 0)
            def _():
                accg_ref[...] = jnp.zeros_like(accg_ref)

            @pl.when(i % nt == 0)
            def _():
                accb_ref[...] = jnp.zeros_like(accb_ref)

            dh2 = hacc[...]
            x1 = x1_ref[...]
            r3 = _rms(x1)
            n3 = x1 * r3
            g3 = gf_ref[...]
            scale1 = 1.0 + mod_ref[0, 4:5, :]
            accb_ref[0, 0:1, :] += _colsum(dh2)
            accb_ref[0, 1:2, :] += _colsum(dh2 * (n3 * g3))
            accg_ref[0:1, :] += _colsum((dh2 * scale1) * n3)
            dx1 = dy_ref[...] + _norm_bwd((dh2 * scale1) * g3, n3, r3)
            dx1_ref[...] = dx1
            mix = mix_ref[...]
            r2 = _rms(mix)
            n2 = mix * r2
            g2 = gm_ref[...]
            gate = mod_ref[0, 2:3, :]
            accb_ref[0, 2:3, :] += _colsum(dx1 * (n2 * g2))
            accg_ref[1:2, :] += _colsum((dx1 * gate) * n2)
            dmix_ref[...] = _norm_bwd((dx1 * gate) * g2, n2, r2).astype(BF16)

    tok = lambda i, k: (i, 0)
    ktok = lambda i, k: (k, i, 0)
    kw = lambda i, k: (k, 0, 0)
    const2 = lambda i, k: (0, 0)
    return pl.pallas_call(
        body, name="ffn_bwd", grid=(t_all // tm, nk),
        in_specs=[pl.BlockSpec((tm, d), tok),
                  pl.BlockSpec((1, tm, ff), ktok), pl.BlockSpec((1, tm, ff), ktok),
                  pl.BlockSpec((1, ff, d), kw), pl.BlockSpec((1, ff, d), kw), pl.BlockSpec((1, ff, d), kw),
                  pl.BlockSpec((tm, d), tok), pl.BlockSpec((tm, d), tok), pl.BlockSpec((tm, d), tok),
                  pl.BlockSpec((1, MOD_ROWS, d), lambda i, k: (i // nt, 0, 0)),
                  pl.BlockSpec((1, d), const2), pl.BlockSpec((1, d), const2)],
        out_specs=[pl.BlockSpec((1, tm, ff), ktok)] * 2
        + [pl.BlockSpec((tm, d), tok), pl.BlockSpec((tm, d), tok),
           pl.BlockSpec((1, 8, d), lambda i, k: (i // nt, 0, 0)),
           pl.BlockSpec((8, d), const2)],
        out_shape=[jax.ShapeDtypeStruct((nk, t_all, ff), BF16)] * 2
        + [jax.ShapeDtypeStruct((t_all, d), F32), jax.ShapeDtypeStruct((t_all, d), BF16),
           jax.ShapeDtypeStruct((t_all // seq, 8, d), F32),
           jax.ShapeDtypeStruct((8, d), F32)],
        scratch_shapes=[pltpu.VMEM((tm, d), F32)],
        compiler_params=_params(("arbitrary", "arbitrary")),
    )(df, a, b, w_d, w_g, w_u, x1, dy, mix, mod, g_fpre, g_mpost)


def _mixer_bwd(dmix, w_out, pooled, w_pool, pool_scale, seq, tm):
    t_all, d = dmix.shape
    p = pooled.shape[1]
    ng = len(POOL_WINDOWS)

    def body(dm_ref, wo_ref, pooled_ref, wp_ref, ps_ref, do_ref, dpd_ref, dps_ref, dwp_ref):
        i = pl.program_id(0)

        @pl.when(i == 0)
        def _():
            dps_ref[...] = jnp.zeros_like(dps_ref)
            dwp_ref[...] = jnp.zeros_like(dwp_ref)

        dmixin = _dot_nt(dm_ref[...], wo_ref[...])
        do_ref[...] = dmixin[:, :p].astype(BF16)
        for g in range(ng):
            cols = slice(g * POOL_GROUP, (g + 1) * POOL_GROUP)
            dpool = dmixin[:, p + g * POOL_GROUP:p + (g + 1) * POOL_GROUP]
            pooled = pooled_ref[:, cols]
            wpg = wp_ref[g].astype(BF16)
            yg = _dot(pooled, wpg)
            dps_ref[0:1, cols] += _colsum(dpool * yg)
            dyg = (dpool * ps_ref[:, cols]).astype(BF16)
            dwp_ref[g] += _dot_tn(pooled, dyg)
            dpd_ref[:, cols] = _dot_nt(dyg, wpg)

    tok = lambda i: (i, 0)
    const2 = lambda i: (0, 0)
    const3 = lambda i: (0, 0, 0)
    return pl.pallas_call(
        body, name="mixer_bwd", grid=(t_all // tm,),
        in_specs=[pl.BlockSpec((tm, d), tok), pl.BlockSpec((d, d), const2), pl.BlockSpec((tm, p), tok),
                  pl.BlockSpec(w_pool.shape, const3), pl.BlockSpec((1, p), const2)],
        out_specs=[pl.BlockSpec((tm, p), tok), pl.BlockSpec((tm, p), tok),
                   pl.BlockSpec((8, p), const2), pl.BlockSpec(w_pool.shape, const3)],
        out_shape=[jax.ShapeDtypeStruct((t_all, p), BF16), jax.ShapeDtypeStruct((t_all, p), F32),
                   jax.ShapeDtypeStruct((8, p), F32), jax.ShapeDtypeStruct(w_pool.shape, F32)],
        compiler_params=_params(("arbitrary",)),
    )(dmix, w_out, pooled, w_pool, pool_scale)


def _attn_bwd(qn, k, kt, v, do, ltot, seq, tq, tk):
    t_all, w = qn.shape
    nb, nq, ndiag, nkb = t_all // seq, seq // tq, tq // tk, seq // tk
    assert ndiag % 2 == 0, "two key blocks per loop trip"
    rc = ATTN_ROW_CHUNK
    nh = HEADS_PER_BLOCK
    heads = range(nh)

    def body(q_ref, k_ref, kt_ref, v_ref, do_ref, l_ref, up_ref, bf_ref, dq_ref, dk_ref, dv_ref,
             z_buf, dw_buf, ls_buf, hl_buf, upto_buf, g_buf, gb_buf, before_buf, w_buf, dz_buf,
             totl_buf, totg_buf, rem_buf, preg_buf, qnt_buf, dot_buf, dq_t, dk_t, dv_t):
        i = pl.program_id(2)
        nblk = (i + 1) * ndiag

        @pl.when(i == 0)
        def _():
            dk_t[...] = jnp.zeros_like(dk_t)
            dv_t[...] = jnp.zeros_like(dv_t)

        lane = lax.broadcasted_iota(jnp.int32, (1, LANES), 1)
        sub = lax.broadcasted_iota(jnp.int32, (LANES, 1), 0)
        row = lax.broadcasted_iota(jnp.int32, (rc, tk), 0)
        col = lax.broadcasted_iota(jnp.int32, (rc, tk), 1)
        first = lane < HEAD_DIM
        upper = sub < HEAD_DIM
        q2 = q_ref[...]
        do2 = do_ref[...]
        l2 = l_ref[...]
        qs = [jnp.where(first, q2, jnp.zeros_like(q2)), jnp.where(first, jnp.zeros_like(q2), q2)]
        dos = [jnp.where(first, do2, jnp.zeros_like(do2)), jnp.where(first, jnp.zeros_like(do2), do2)]
        for src, dst in ((q2, qnt_buf), (do2, dot_buf)):
            t = src.astype(F32).T
            dst[:, 0:tq] = jnp.where(upper, t, 0.0).astype(BF16)
            dst[:, tq:2 * tq] = jnp.where(upper, 0.0, t).astype(BF16)
        for h in heads:
            rem_buf[h] = jnp.where(first if h == 0 else ~first, l2, pltpu.roll(l2, HEAD_DIM, 1))
        preg_buf[...] = jnp.zeros_like(preg_buf)
        dq_t[...] = jnp.zeros_like(dq_t)
        w_buf[1] = jnp.zeros((nh * tq, tk), BF16)
        dz_buf[1] = jnp.zeros((nh * tq, tk), BF16)

        def causal(c, diag):
            return (col + diag * tk) < (row + c * rc)

        def scores(blk, slot):
            off = pl.multiple_of(blk * tk, tk)
            kj = k_ref[pl.ds(off, tk), :]
            vj = v_ref[pl.ds(off, tk), :]
            for h in heads:
                z_buf[slot, h] = _dot_nt(qs[h], kj)
                dw_buf[slot, h] = _dot_nt(dos[h], vj)

        def gradients(blk, slot):
            off = pl.multiple_of(blk * tk, tk)
            ktj = kt_ref[:, pl.ds(off, tk)]
            zero = jnp.zeros_like(ktj)
            dq_t[...] += (_dot_nt(jnp.where(upper, ktj, zero), dz_buf[slot, 0:tq, :])
                          + _dot_nt(jnp.where(upper, zero, ktj), dz_buf[slot, tq:2 * tq, :]))
            dk_t[blk] += _dot(qnt_buf[...], dz_buf[slot])
            dv_t[blk] += _dot(dot_buf[...], w_buf[slot])

        def softplus_stage(h, slot, diag):
            for c in range(tq // rc):
                rows = slice(c * rc, (c + 1) * rc)
                if _all_masked(c, diag, rc, tk):
                    hl_buf[h, rows, :] = jnp.zeros((rc, LOG_SUM_PASSES * tk), BF16)
                    continue
                nz = z_buf[slot, h, rows, :]
                l1 = jnp.minimum(nz, 0.0) - jnp.log(1.0 + jnp.exp(-jnp.abs(nz)))
                if _some_masked(c, diag, rc, tk):
                    l1 = jnp.where(causal(c, diag), l1, 0.0)
                for s, part in enumerate(_split(l1)[:LOG_SUM_PASSES]):
                    hl_buf[h, rows, s * tk:(s + 1) * tk] = part
                ls_buf[h, rows, :] = l1 - nz
                totl_buf[h, rows, :] = _row_sums(l1)

        def weights_stage(h, slot, diag):
            for c in range(tq // rc):
                rows = slice(c * rc, (c + 1) * rc)
                stacked = slice(h * tq + c * rc, h * tq + (c + 1) * rc)
                if _all_masked(c, diag, rc, tk):
                    w_buf[slot, stacked, :] = jnp.zeros((rc, tk), BF16)
                    gb_buf[h, rows, :] = jnp.zeros((rc, tk), BF16)
                    continue
                wgt = jnp.exp(ls_buf[h, rows, :] + (_across(rem_buf[h, rows, :], tk) - upto_buf[h, rows, :]))
                if _some_masked(c, diag, rc, tk):
                    wgt = jnp.where(causal(c, diag), wgt, 0.0)
                w_buf[slot, stacked, :] = wgt.astype(BF16)
                g = wgt * dw_buf[slot, h, rows, :]
                g_buf[h, rows, :] = g
                gb_buf[h, rows, :] = g.astype(BF16)
                totg_buf[h, rows, :] = _row_sums(g)
                rem_buf[h, rows, :] -= totl_buf[h, rows, :]

        def dscore_stage(h, slot, diag):
            for c in range(tq // rc):
                rows = slice(c * rc, (c + 1) * rc)
                stacked = slice(h * tq + c * rc, h * tq + (c + 1) * rc)
                if _all_masked(c, diag, rc, tk):
                    dz_buf[slot, stacked, :] = jnp.zeros((rc, tk), BF16)
                    continue
                sig = jnp.exp(ls_buf[h, rows, :])
                g = g_buf[h, rows, :]
                dnz = sig * (before_buf[h, rows, :] + _across(preg_buf[h, rows, :], tk)) - g * (1.0 - sig)
                if _some_masked(c, diag, rc, tk):
                    dnz = jnp.where(causal(c, diag), dnz, 0.0)
                dz_buf[slot, stacked, :] = dnz.astype(BF16)
                preg_buf[h, rows, :] += totg_buf[h, rows, :]

        def position(blk, slot, diag, prefetch):
            if prefetch:
                scores(blk + 1, 1 - slot)
            for h in heads:
                softplus_stage(h, slot, diag)
                upto_buf[h] = _dot(hl_buf[h], up_ref[...])
            gradients(jnp.maximum(blk - 1, 0), 1 - slot)
            for h in heads:
                weights_stage(h, slot, diag)
                before_buf[h] = _dot(gb_buf[h], bf_ref[...])
            for h in heads:
                dscore_stage(h, slot, diag)

        scores(0, 0)

        def trip(jj, carry):
            for u in range(2):
                position(2 * jj + u, u, None, True)
            return carry

        lax.fori_loop(0, (i * ndiag) // 2, trip, 0)
        for d in range(ndiag):
            position(i * ndiag + d, d % 2, d, d < ndiag - 1)
        gradients(nblk - 1, 1)
        dq_ref[...] = (dq_t[...].T * NEG_QK_SCALE).astype(BF16)

        @pl.when(i == nq - 1)
        def _():
            for blk in range(nkb):
                dk_ref[blk * tk:(blk + 1) * tk, :] = dk_t[blk].T.astype(BF16)
                dv_ref[blk * tk:(blk + 1) * tk, :] = dv_t[blk].T.astype(BF16)

    qmap = lambda b, hp, i: (b * nq + i, hp)
    kmap = lambda b, hp, i: (b, hp)
    const = lambda b, hp, i: (0, 0)
    return pl.pallas_call(
        body, name="attn_bwd", grid=(nb, w // LANES, nq),
        in_specs=[pl.BlockSpec((tq, LANES), qmap), pl.BlockSpec((seq, LANES), kmap),
                  pl.BlockSpec((LANES, seq), lambda b, hp, i: (hp, b)), pl.BlockSpec((seq, LANES), kmap),
                  pl.BlockSpec((tq, LANES), qmap), pl.BlockSpec((tq, LANES), qmap),
                  pl.BlockSpec((LOG_SUM_PASSES * tk, tk), const), pl.BlockSpec((tk, tk), const)],
        out_specs=[pl.BlockSpec((tq, LANES), qmap), pl.BlockSpec((seq, LANES), kmap), pl.BlockSpec((seq, LANES), kmap)],
        out_shape=[jax.ShapeDtypeStruct((t_all, w), BF16)] * 3,
        scratch_shapes=[pltpu.VMEM((2, nh, tq, tk), F32), pltpu.VMEM((2, nh, tq, tk), F32),
                        pltpu.VMEM((nh, tq, tk), F32), pltpu.VMEM((nh, tq, LOG_SUM_PASSES * tk), BF16),
                        pltpu.VMEM((nh, tq, tk), F32), pltpu.VMEM((nh, tq, tk), F32),
                        pltpu.VMEM((nh, tq, tk), BF16), pltpu.VMEM((nh, tq, tk), F32),
                        pltpu.VMEM((2, nh * tq, tk), BF16), pltpu.VMEM((2, nh * tq, tk), BF16),
                        pltpu.VMEM((nh, tq, LANES), F32), pltpu.VMEM((nh, tq, LANES), F32),
                        pltpu.VMEM((nh, tq, LANES), F32), pltpu.VMEM((nh, tq, LANES), F32),
                        pltpu.VMEM((LANES, nh * tq), BF16), pltpu.VMEM((LANES, nh * tq), BF16),
                        pltpu.VMEM((LANES, tq), F32), pltpu.VMEM((nkb, LANES, tk), F32),
                        pltpu.VMEM((nkb, LANES, tk), F32)],
        compiler_params=_params(("arbitrary", "arbitrary", "arbitrary")),
    )(qn, k, kt, v, do, ltot, _tri_matrix(tk, "upto")[:LOG_SUM_PASSES * tk], _tri_matrix(tk, "before")[:tk])


def _inproj_bwd(dq, dk, dv, dpd, x, dx1, mod, g_pre, w_in, seq, tm):
    t_all, d = x.shape
    nt = seq // tm
    p = dq.shape[1]

    def body(dq_ref, dk_ref, dv_ref, dpd_ref, halo_ref, x_ref, dx1_ref, mod_ref, g_ref, w_ref,
             gx_ref, du_ref, accb_ref, accg_ref):
        i = pl.program_id(0)
        it = i % nt

        @pl.when(i == 0)
        def _():
            accg_ref[...] = jnp.zeros_like(accg_ref)

        @pl.when(it == 0)
        def _():
            accb_ref[...] = jnp.zeros_like(accb_ref)

        dpd = dpd_ref[...]
        pos = it * tm + lax.broadcasted_iota(jnp.int32, (tm, 1), 0)
        cnts = _window_counts(pos)
        halo = jnp.where(it == nt - 1, 0.0, halo_ref[...])
        scaled = []
        halos = []
        for g, win in enumerate(POOL_WINDOWS):
            cols = slice(g * POOL_GROUP, (g + 1) * POOL_GROUP)
            scaled.append(dpd[:, cols] / cnts[g])
            halos.append(halo[:, cols] / float(win))
        ext = jnp.concatenate([jnp.concatenate(scaled, axis=1), jnp.concatenate(halos, axis=1)], axis=0)
        sums = _window_sums(ext, tm, 0, True)
        du = (jnp.concatenate(sums, axis=1) - dpd).astype(BF16)
        du_ref[...] = du
        g1 = g_ref[...]
        scale1 = 1.0 + mod_ref[0, 1:2, :]
        for c in range(ROW_CHUNKS):
            rows = slice(c * (tm // ROW_CHUNKS), (c + 1) * (tm // ROW_CHUNKS))
            dh1 = (_dot_nt(dq_ref[rows, :], w_ref[0]) + _dot_nt(dk_ref[rows, :], w_ref[1])
                   + _dot_nt(dv_ref[rows, :], w_ref[2]) + _dot_nt(du_ref[rows, :], w_ref[3]))
            xf = x_ref[rows, :]
            r1 = _rms(xf)
            n1 = xf * r1
            accb_ref[0, 0:1, :] += _colsum(dh1)
            accb_ref[0, 1:2, :] += _colsum(dh1 * (n1 * g1))
            accg_ref[0:1, :] += _colsum((dh1 * scale1) * n1)
            gx_ref[rows, :] = dx1_ref[rows, :] + _norm_bwd((dh1 * scale1) * g1, n1, r1)

    tok = lambda i: (i, 0)
    const2 = lambda i: (0, 0)
    hb = tm // HALO
    last = t_all // HALO - 1
    return pl.pallas_call(
        body, name="inproj_bwd", grid=(t_all // tm,),
        in_specs=[pl.BlockSpec((tm, p), tok), pl.BlockSpec((tm, p), tok), pl.BlockSpec((tm, p), tok),
                  pl.BlockSpec((tm, p), tok),
                  pl.BlockSpec((HALO, p), lambda i: (jnp.minimum((i + 1) * hb, last), 0)),
                  pl.BlockSpec((tm, d), tok), pl.BlockSpec((tm, d), tok),
                  pl.BlockSpec((1, MOD_ROWS, d), lambda i: (i // nt, 0, 0)),
                  pl.BlockSpec((1, d), const2),
                  pl.BlockSpec((N_CHIPS, d, p), lambda i: (0, 0, 0))],
        out_specs=[pl.BlockSpec((tm, d), tok), pl.BlockSpec((tm, p), tok),
                   pl.BlockSpec((1, 8, d), lambda i: (i // nt, 0, 0)),
                   pl.BlockSpec((8, d), const2)],
        out_shape=[jax.ShapeDtypeStruct((t_all, d), F32), jax.ShapeDtypeStruct((t_all, p), BF16),
                   jax.ShapeDtypeStruct((t_all // seq, 8, d), F32),
                   jax.ShapeDtypeStruct((8, d), F32)],
        compiler_params=_params(("arbitrary",)),
    )(dq, dk, dv, dpd, dpd, x, dx1, mod, g_pre, w_in)


def _tn_matmul(x, ys, nk, bt, name):
    t_all = x.shape[-2]
    m = x.shape[-1]
    ny = len(ys)
    nt = t_all // bt

    def spec(arr):
        if arr.ndim == 3:
            return pl.BlockSpec((1, bt, arr.shape[-1]), lambda k, t: (k, t, 0))
        return pl.BlockSpec((bt, arr.shape[-1]), lambda k, t: (t, 0))

    def tile(ref):
        return ref[0] if len(ref.shape) == 3 else ref[...]

    def body(*refs):
        x_ref, y_refs, o_refs, h_refs = refs[0], refs[1:1 + ny], refs[1 + ny:1 + 2 * ny], refs[1 + 2 * ny:]
        t = pl.program_id(1)
        xt = tile(x_ref)
        for y_ref, o_ref, h_ref in zip(y_refs, o_refs, h_refs):
            part = _dot_tn(xt, tile(y_ref))

            @pl.when(t == 0)
            def _(o_ref=o_ref, part=part):
                o_ref[0] = part

            @pl.when(t > 0)
            def _(o_ref=o_ref, part=part):
                o_ref[0] += part

            @pl.when(t == nt - 1)
            def _(o_ref=o_ref, h_ref=h_ref):
                h_ref[0] = o_ref[0].astype(BF16)

    out_specs = [pl.BlockSpec((1, m, y.shape[-1]), lambda k, t: (k, 0, 0)) for y in ys]
    out = pl.pallas_call(
        body, name=name, grid=(nk, nt),
        in_specs=[spec(x)] + [spec(y) for y in ys],
        out_specs=out_specs * 2,
        out_shape=[jax.ShapeDtypeStruct((nk, m, y.shape[-1]), dt) for dt in (F32, BF16) for y in ys],
        compiler_params=_params(("arbitrary", "arbitrary")),
    )(x, *ys)
    return out[:ny], out[ny:]


def _cond_fwd(c_all, w_q, b_q, bn):
    nrow, d = c_all.shape
    ncol = w_q.shape[1]

    def body(c_ref, w_ref, b_ref, sc_ref, mod_ref):
        cf = c_ref[...]
        sc = cf * _sigmoid(cf)
        sc_ref[...] = sc
        shi, slo = _split(sc)
        whi, wlo = _split(w_ref[...])
        mod_ref[...] = (_dot(shi, whi) + _dot(shi, wlo) + _dot(slo, whi)) + b_ref[...]

    return pl.pallas_call(
        body, name="cond_fwd", grid=(ncol // bn,),
        in_specs=[pl.BlockSpec((nrow, d), lambda n: (0, 0)), pl.BlockSpec((d, bn), lambda n: (0, n)),
                  pl.BlockSpec((1, bn), lambda n: (0, n))],
        out_specs=[pl.BlockSpec((nrow, d), lambda n: (0, 0)), pl.BlockSpec((nrow, bn), lambda n: (0, n))],
        out_shape=[jax.ShapeDtypeStruct((nrow, d), F32), jax.ShapeDtypeStruct((nrow, ncol), F32)],
        compiler_params=_params(("arbitrary",)),
    )(c_all, w_q, b_q)


def _cond_bwd(sc_all, dmod_q, bn):
    nrow, d = sc_all.shape
    ncol = dmod_q.shape[1]

    def body(sc_ref, dm_ref, gw_ref):
        shi, slo = _split(sc_ref[...])
        dhi, dlo = _split(dm_ref[...])
        gw_ref[...] = _dot_tn(shi, dhi) + _dot_tn(shi, dlo) + _dot_tn(slo, dhi)

    return pl.pallas_call(
        body, name="cond_bwd", grid=(ncol // bn,),
        in_specs=[pl.BlockSpec((nrow, d), lambda n: (0, 0)), pl.BlockSpec((nrow, bn), lambda n: (0, n))],
        out_specs=pl.BlockSpec((d, bn), lambda n: (0, n)),
        out_shape=jax.ShapeDtypeStruct((d, ncol), F32),
        compiler_params=_params(("arbitrary",)),
    )(sc_all, dmod_q)


def _row_block(rows, cols, budget=1 << 18):
    best = None
    for br in range(8, rows + 1, 8):
        if rows % br == 0 and br * cols <= budget:
            best = br
    return best if best is not None else rows


def _adamw(w, g, m, v, name):
    rows, cols = w.shape
    br = _row_block(rows, cols)
    c1 = 1.0 - ADAM_B1 ** ADAM_STEP
    c2 = 1.0 - ADAM_B2 ** ADAM_STEP

    def body(w_ref, g_ref, m_ref, v_ref, d_ref, nm_ref, nv_ref):
        gf = g_ref[...]
        m2 = ADAM_B1 * m_ref[...] + (1.0 - ADAM_B1) * gf
        v2 = ADAM_B2 * v_ref[...] + (1.0 - ADAM_B2) * (gf * gf)
        nm_ref[...] = m2
        nv_ref[...] = v2
        d_ref[...] = -ADAM_LR * ((m2 / c1) / (jnp.sqrt(v2 / c2) + ADAM_EPS) + ADAM_WD * w_ref[...])

    blk = pl.BlockSpec((br, cols), lambda i: (i, 0))
    return pl.pallas_call(
        body, name=name, grid=(rows // br,),
        in_specs=[blk] * 4, out_specs=[blk] * 3,
        out_shape=[jax.ShapeDtypeStruct((rows, cols), F32)] * 3,
        compiler_params=_params(("arbitrary",)),
    )(w, g, m, v)


def _all_gather(x_shard, name):
    m_per, n = x_shard.shape

    def body(x_ref, out_ref, send_sems, recv_sems, local_sem):
        x, y, c = _position()
        me, sibling = (x, y, c), (x, y, 1 - c)
        chips = [(1 - x, y), (x, 1 - y), (1 - x, 1 - y)]

        def rows(px, py, pc):
            return out_ref.at[pl.ds((4 * px + 2 * py + pc) * m_per, m_per), :]

        def copy(k, block, to, src=None):
            return pltpu.make_async_remote_copy(
                src_ref=rows(*block) if src is None else src, dst_ref=rows(*block),
                send_sem=send_sems.at[k], recv_sem=recv_sems.at[k], device_id=to, device_id_type=MESH)

        mine = pltpu.make_async_copy(x_ref, rows(*me), local_sem)
        mine.start()
        first = [copy(0, me, sibling, src=x_ref)]
        first += [copy(1 + j, me, (*chip, c), src=x_ref) for j, chip in enumerate(chips)]
        for cp in first:
            cp.start()
        passed = [copy(4 + j, (*chip, c), sibling) for j, chip in enumerate(chips)]
        for j, chip in enumerate(chips):
            copy(1 + j, (*chip, c), me).wait_recv()
            passed[j].start()
        copy(0, sibling, me).wait_recv()
        for j, chip in enumerate(chips):
            copy(4 + j, (*chip, 1 - c), me).wait_recv()
        for cp in first + passed:
            cp.wait_send()
        mine.wait()

    return pl.pallas_call(
        body, name=name,
        out_shape=jax.ShapeDtypeStruct((N_DEV * m_per, n), x_shard.dtype),
        in_specs=[pl.BlockSpec(memory_space=pltpu.VMEM)],
        out_specs=pl.BlockSpec(memory_space=pltpu.VMEM),
        scratch_shapes=[pltpu.SemaphoreType.DMA((7,)), pltpu.SemaphoreType.DMA((7,)), pltpu.SemaphoreType.DMA],
        compiler_params=pltpu.CompilerParams(vmem_limit_bytes=VMEM_LIMIT),
    )(x_shard)


_ANY = pl.BlockSpec(memory_space=pl.ANY)


def _place_quarters(place, quarters):
    steps = 2

    def body(place_ref, *refs):
        n = len(refs) // 2
        for w_ref, o_ref in zip(refs[:n], refs[n:]):
            o_ref[0] = w_ref[...].astype(BF16)

    return pl.pallas_call(
        body, name="place_quarters",
        grid_spec=pltpu.PrefetchScalarGridSpec(
            num_scalar_prefetch=1, grid=(steps,),
            in_specs=[pl.BlockSpec((q.shape[0] // steps, q.shape[1]), lambda r, place_ref: (r, 0)) for q in quarters],
            out_specs=[pl.BlockSpec((1, q.shape[0] // steps, q.shape[1]), lambda r, place_ref: (place_ref[0], r, 0))
                       for q in quarters]),
        out_shape=[jax.ShapeDtypeStruct((N_CHIPS,) + q.shape, BF16) for q in quarters],
        compiler_params=_params(("arbitrary",)),
    )(place, *quarters)


def _gather_weights(placed):
    n = len(placed)
    shapes = [b.shape[1:] for b in placed]

    def body(*refs):
        g_refs = refs[n:2 * n]
        send_sems, recv_sems = refs[2 * n:]
        x, y, c = _position()
        sibling = (x, y, 1 - c)
        chips = [(1 - x, y), (x, 1 - y), (1 - x, 1 - y)]
        mine = 2 * x + y

        def half(a, which):
            hr = shapes[a][0] // 2
            return pl.ds(which * hr, hr)

        def over_ici(a, p, slot):
            ref = g_refs[a].at[slot, half(a, c), :]
            return pltpu.make_async_remote_copy(
                src_ref=ref, dst_ref=ref,
                send_sem=send_sems.at[6 * a + p], recv_sem=recv_sems.at[6 * a + p],
                device_id=(*chips[p], c), device_id_type=MESH)

        def over_d2d(a, p, slot, which):
            ref = g_refs[a].at[slot, half(a, which), :]
            return pltpu.make_async_remote_copy(
                src_ref=ref, dst_ref=ref,
                send_sem=send_sems.at[6 * a + 3 + p], recv_sem=recv_sems.at[6 * a + 3 + p],
                device_id=sibling, device_id_type=MESH)

        sends = []
        for a in range(n):
            for p in range(3):
                cp = over_ici(a, p, mine)
                cp.start()
                sends.append(cp)
        for a in range(n):
            for p, (cx, cy) in enumerate(chips):
                slot = 2 * cx + cy
                over_ici(a, p, slot).wait_recv()
                cp = over_d2d(a, p, slot, c)
                cp.start()
                sends.append(cp)
        for a in range(n):
            for p, (cx, cy) in enumerate(chips):
                over_d2d(a, p, 2 * cx + cy, 1 - c).wait_recv()
        for cp in sends:
            cp.wait_send()

    return pl.pallas_call(
        body, name="gather_weights",
        out_shape=[jax.ShapeDtypeStruct(b.shape, BF16) for b in placed],
        in_specs=[_ANY] * n, out_specs=[_ANY] * n,
        input_output_aliases={a: a for a in range(n)},
        scratch_shapes=[pltpu.SemaphoreType.DMA((6 * n,)), pltpu.SemaphoreType.DMA((6 * n,))],
    )(*placed)


_HBM = pl.BlockSpec(memory_space=pltpu.HBM)
_SEM = pl.BlockSpec(memory_space=pltpu.SEMAPHORE)
_EFFECT = pltpu.SideEffectType.DATAFLOW_SIDE_EFFECTING


def _quarter_halves(shapes, a, which):
    hr = shapes[a][0] // 2
    return pl.ds(which * hr, hr)


def _gather_start(placed, after):
    n = len(placed)
    m = len(after)
    shapes = [b.shape[1:] for b in placed]

    def body(*refs):
        g_refs = refs[:n]
        send_sems, recv_sems = refs[n + m], refs[n + m + 1]
        token = refs[2 * n + m + 2]
        x, y, c = _position()
        chips = [(1 - x, y), (x, 1 - y), (1 - x, 1 - y)]
        mine = 2 * x + y
        for a in range(n):
            ref = g_refs[a].at[mine, _quarter_halves(shapes, a, c), :]
            for p in range(3):
                pltpu.make_async_remote_copy(
                    src_ref=ref, dst_ref=ref, send_sem=send_sems.at[3 * a + p], recv_sem=recv_sems.at[3 * a + p],
                    device_id=(*chips[p], c), device_id_type=MESH).start()
        token[...] = jnp.zeros_like(token)

    out = pl.pallas_call(
        body, name="gather_start",
        out_shape=(pltpu.SemaphoreType.DMA((3 * n,)), pltpu.SemaphoreType.DMA((3 * n,)),
                   *[pltpu.HBM(b.shape, b.dtype) for b in placed], jax.ShapeDtypeStruct((8, LANES), F32)),
        in_specs=[_HBM] * n + [_ANY] * m,
        out_specs=(_SEM, _SEM, *[_HBM] * n, pl.BlockSpec(memory_space=pltpu.VMEM)),
        input_output_aliases={a: 2 + a for a in range(n)},
        compiler_params=pltpu.CompilerParams(has_side_effects=_EFFECT),
    )(*[pltpu.with_memory_space_constraint(b, pltpu.HBM) for b in placed], *after)
    return out[0], out[1], list(out[2:2 + n]), out[2 + n]


def _gather_wait(send_sems, recv_sems, thru, after):
    n = len(thru)
    shapes = [b.shape[1:] for b in thru]

    def body(*refs):
        g_refs = refs[:n]
        send_sems, recv_sems = refs[n], refs[n + 1]
        x, y, c = _position()
        chips = [(1 - x, y), (x, 1 - y), (1 - x, 1 - y)]
        mine = 2 * x + y
        for a in range(n):
            rows = _quarter_halves(shapes, a, c)
            for p, (cx, cy) in enumerate(chips):
                copy = pltpu.make_async_remote_copy(
                    src_ref=g_refs[a].at[mine, rows, :], dst_ref=g_refs[a].at[2 * cx + cy, rows, :],
                    send_sem=send_sems.at[3 * a + p], recv_sem=recv_sems.at[3 * a + p],
                    device_id=(cx, cy, c), device_id_type=MESH)
                copy.wait_send()
                copy.wait_recv()

    return pl.pallas_call(
        body, name="gather_wait",
        out_shape=[pltpu.HBM(b.shape, b.dtype) for b in thru],
        in_specs=[_HBM] * n + [_SEM, _SEM, _ANY], out_specs=[_HBM] * n,
        input_output_aliases={a: a for a in range(n)},
        compiler_params=pltpu.CompilerParams(has_side_effects=_EFFECT),
    )(*thru, send_sems, recv_sems, after)


def _gather_forward(bufs):
    n = len(bufs)
    shapes = [b.shape[1:] for b in bufs]

    def body(*refs):
        g_refs = refs[n:2 * n]
        send_sems, recv_sems = refs[2 * n:]
        x, y, c = _position()
        chips = [(1 - x, y), (x, 1 - y), (1 - x, 1 - y)]

        def over_d2d(a, p, which):
            cx, cy = chips[p]
            ref = g_refs[a].at[2 * cx + cy, _quarter_halves(shapes, a, which), :]
            return pltpu.make_async_remote_copy(
                src_ref=ref, dst_ref=ref, send_sem=send_sems.at[3 * a + p], recv_sem=recv_sems.at[3 * a + p],
                device_id=(x, y, 1 - c), device_id_type=MESH)

        sends = [over_d2d(a, p, c) for a in range(n) for p in range(3)]
        for cp in sends:
            cp.start()
        for a in range(n):
            for p in range(3):
                over_d2d(a, p, 1 - c).wait_recv()
        for cp in sends:
            cp.wait_send()

    return pl.pallas_call(
        body, name="gather_forward",
        out_shape=[jax.ShapeDtypeStruct(b.shape, BF16) for b in bufs],
        in_specs=[_ANY] * n, out_specs=[_ANY] * n,
        input_output_aliases={a: a for a in range(n)},
        scratch_shapes=[pltpu.SemaphoreType.DMA((3 * n,)), pltpu.SemaphoreType.DMA((3 * n,))],
    )(*bufs)


def _sibling_exchange(grads, tag):
    n = len(grads)
    shapes = [g.shape for g in grads]

    def body(*refs):
        g_refs, x_refs = refs[:n], refs[n:2 * n]
        send_sems, recv_sems = refs[2 * n:]
        x, y, c = _position()
        copies = []
        for a in range(n):
            hr = shapes[a][1] // 2
            cp = pltpu.make_async_remote_copy(
                src_ref=g_refs[a].at[:, pl.ds((1 - c) * hr, hr), :], dst_ref=x_refs[a],
                send_sem=send_sems.at[a], recv_sem=recv_sems.at[a],
                device_id=(x, y, 1 - c), device_id_type=MESH)
            cp.start()
            copies.append(cp)
        for cp in copies:
            cp.wait()

    return pl.pallas_call(
        body, name="grad_sibling_exchange_" + tag,
        out_shape=[jax.ShapeDtypeStruct((g.shape[0], g.shape[1] // 2, g.shape[2]), g.dtype) for g in grads],
        in_specs=[_ANY] * n, out_specs=[_ANY] * n,
        scratch_shapes=[pltpu.SemaphoreType.DMA((n,)), pltpu.SemaphoreType.DMA((n,))],
    )(*grads)


def _chip_sums(core, grads, theirs, tag):
    n = len(grads)

    def body(core_ref, *refs):
        g_refs, t_refs, o_refs = refs[:n], refs[n:2 * n], refs[2 * n:]
        for g_ref, t_ref, o_ref in zip(g_refs, t_refs, o_refs):
            o_ref[...] = (g_ref[...] + t_ref[...].astype(F32)).astype(BF16)

    in_specs = [pl.BlockSpec((1, g.shape[1] // 2, g.shape[2]), lambda k, core_ref: (k, core_ref[0], 0)) for g in grads]
    in_specs += [pl.BlockSpec((1,) + t.shape[1:], lambda k, core_ref: (k, 0, 0)) for t in theirs]
    return pl.pallas_call(
        body, name="grad_chip_sums_" + tag,
        grid_spec=pltpu.PrefetchScalarGridSpec(
            num_scalar_prefetch=1, grid=(N_CHIPS,), in_specs=in_specs,
            out_specs=[pl.BlockSpec((1,) + t.shape[1:], lambda k, core_ref: (k, 0, 0)) for t in theirs]),
        out_shape=[jax.ShapeDtypeStruct(t.shape, BF16) for t in theirs],
        compiler_params=_params(("arbitrary",)),
    )(core, *grads, *theirs)


def _chip_exchange_start(sums, after, tag):
    n = len(sums)
    m = len(after)
    lands = [lax.empty((3,) + s.shape[1:], BF16) for s in sums]

    def body(*refs):
        s_refs, y_refs = refs[:n], refs[n:2 * n]
        send_sems, recv_sems = refs[2 * n + m], refs[2 * n + m + 1]
        token = refs[4 * n + m + 2]
        x, y, c = _position()
        chips = [(1 - x, y), (x, 1 - y), (1 - x, 1 - y)]
        for a in range(n):
            for p, (cx, cy) in enumerate(chips):
                pltpu.make_async_remote_copy(
                    src_ref=s_refs[a].at[2 * cx + cy], dst_ref=y_refs[a].at[p],
                    send_sem=send_sems.at[3 * a + p], recv_sem=recv_sems.at[3 * a + p],
                    device_id=(cx, cy, c), device_id_type=MESH).start()
        token[...] = jnp.zeros_like(token)

    both = list(sums) + lands
    out = pl.pallas_call(
        body, name="grad_chip_exchange_start_" + tag,
        out_shape=(pltpu.SemaphoreType.DMA((3 * n,)), pltpu.SemaphoreType.DMA((3 * n,)),
                   *[pltpu.HBM(b.shape, b.dtype) for b in both], jax.ShapeDtypeStruct((8, LANES), F32)),
        in_specs=[_HBM] * (2 * n) + [_ANY] * m,
        out_specs=(_SEM, _SEM, *[_HBM] * (2 * n), pl.BlockSpec(memory_space=pltpu.VMEM)),
        input_output_aliases={a: 2 + a for a in range(2 * n)},
        compiler_params=pltpu.CompilerParams(has_side_effects=_EFFECT),
    )(*[pltpu.with_memory_space_constraint(b, pltpu.HBM) for b in both], *after)
    return out[0], out[1], list(out[2:2 + n]), list(out[2 + n:2 + 2 * n]), out[2 + 2 * n]


def _chip_exchange_wait(send_sems, recv_sems, sums, lands, after, tag):
    n = len(sums)

    def body(*refs):
        s_refs, y_refs = refs[:n], refs[n:2 * n]
        send_sems, recv_sems = refs[2 * n], refs[2 * n + 1]
        x, y, c = _position()
        chips = [(1 - x, y), (x, 1 - y), (1 - x, 1 - y)]
        for a in range(n):
            for p, (cx, cy) in enumerate(chips):
                copy = pltpu.make_async_remote_copy(
                    src_ref=s_refs[a].at[2 * cx + cy], dst_ref=y_refs[a].at[p],
                    send_sem=send_sems.at[3 * a + p], recv_sem=recv_sems.at[3 * a + p],
                    device_id=(cx, cy, c), device_id_type=MESH)
                copy.wait_send()
                copy.wait_recv()

    both = list(sums) + list(lands)
    out = pl.pallas_call(
        body, name="grad_chip_exchange_wait_" + tag,
        out_shape=[pltpu.HBM(b.shape, b.dtype) for b in both],
        in_specs=[_HBM] * (2 * n) + [_SEM, _SEM, _ANY], out_specs=[_HBM] * (2 * n),
        input_output_aliases={a: a for a in range(2 * n)},
        compiler_params=pltpu.CompilerParams(has_side_effects=_EFFECT),
    )(*both, send_sems, recv_sems, after)
    return list(out[:n]), list(out[n:])


def _total_sums(place, sums, parts, after, tag):
    n = len(parts)
    m = len(after)
    steps = 2

    def body(place_ref, *refs):
        for s_ref, y_ref, o_ref in zip(refs[:n], refs[n:2 * n], refs[2 * n + m:]):
            o_ref[0] = ((s_ref[0].astype(F32) + y_ref[0].astype(F32)) + y_ref[1].astype(F32)) + y_ref[2].astype(F32)

    def step_rows(pt):
        return pt.shape[1] // steps

    in_specs = [pl.BlockSpec((1, step_rows(s), s.shape[2]), lambda r, place_ref: (place_ref[0], r, 0)) for s in sums]
    in_specs += [pl.BlockSpec((3, step_rows(pt), pt.shape[2]), lambda r, place_ref: (0, r, 0)) for pt in parts]
    in_specs += [_ANY] * m
    return pl.pallas_call(
        body, name="grad_total_sums_" + tag,
        grid_spec=pltpu.PrefetchScalarGridSpec(
            num_scalar_prefetch=1, grid=(steps,), in_specs=in_specs,
            out_specs=[pl.BlockSpec((1, step_rows(pt), pt.shape[2]), lambda r, place_ref: (place_ref[1], r, 0))
                       for pt in parts]),
        out_shape=[jax.ShapeDtypeStruct((2,) + pt.shape[1:], F32) for pt in parts],
        compiler_params=_params(("arbitrary",)),
    )(place, *sums, *parts, *after)


def _sibling_share(halves, tag):
    n = len(halves)

    def body(*refs):
        f_refs = refs[n:2 * n]
        send_sems, recv_sems = refs[2 * n:]
        x, y, c = _position()
        copies = []
        for a in range(n):
            cp = pltpu.make_async_remote_copy(
                src_ref=f_refs[a].at[c], dst_ref=f_refs[a].at[c], send_sem=send_sems.at[a], recv_sem=recv_sems.at[a],
                device_id=(x, y, 1 - c), device_id_type=MESH)
            cp.start()
            copies.append(cp)
        for a, cp in enumerate(copies):
            cp.wait_send()
            pltpu.make_async_remote_copy(
                src_ref=f_refs[a].at[1 - c], dst_ref=f_refs[a].at[1 - c], send_sem=send_sems.at[a],
                recv_sem=recv_sems.at[a], device_id=(x, y, c), device_id_type=MESH).wait_recv()

    return pl.pallas_call(
        body, name="grad_sibling_share_" + tag,
        out_shape=[jax.ShapeDtypeStruct(h.shape, F32) for h in halves],
        in_specs=[_ANY] * n, out_specs=[_ANY] * n,
        input_output_aliases={a: a for a in range(n)},
        scratch_shapes=[pltpu.SemaphoreType.DMA((n,)), pltpu.SemaphoreType.DMA((n,))],
    )(*halves)


def _group_sum(stacked, nrow, name):
    total, n = stacked.shape
    groups = total // nrow

    def body(g_ref, o_ref):
        acc = g_ref[0:nrow, :]
        for grp in range(1, groups):
            acc = acc + g_ref[grp * nrow:(grp + 1) * nrow, :]
        o_ref[...] = acc

    return pl.pallas_call(
        body, name=name,
        out_shape=jax.ShapeDtypeStruct((nrow, n), F32),
        compiler_params=pltpu.CompilerParams(vmem_limit_bytes=VMEM_LIMIT),
    )(stacked)


def _local_step(xt, tgt, mod, gains, w_pool, pool_scale, w_in, later_weights, on_ffn_grads, seq):
    g_mpre, g_mpost, g_fpre, g_fpost = gains
    d = xt.shape[1]
    tm, tq = min(TOKEN_TILE, seq), min(ATTN_TILE, seq)

    h1, qn, k, v, u, kt, vt = _prenorm_proj(xt, mod, g_mpre, w_in, seq, tm)
    tk = min(ATTN_KEY_TILE, tq // 2)
    o, ltot = _attn_fwd(qn, k, vt, seq, tq, tk)
    w_out, w_g, w_u, w_d = later_weights(o)
    w_out2 = w_out.reshape(d, d)
    pooled, mixin, mix, x1, h2 =_mixer_post(u, o, xt, mod, g_mpost, g_fpre, w_pool, pool_scale, w_out2, seq, tm)
    a, b, fin, dy, df, loss_blk, accb4, accg4 = _ffn_fwd(h2, w_g, w_u, w_d, x1, tgt, mod, g_fpost, seq, tm)
    da, db, dx1, dmix, accb5, accg5 = _ffn_bwd(df, a, b, w_d, w_g, w_u, x1, dy, mix, mod, g_fpre, g_mpost, seq, tm)
    bt = min(GRAD_TOKEN_TILE, xt.shape[0])
    bt_one = min(2 * GRAD_TOKEN_TILE, xt.shape[0])
    (g_g,), (g_g16,) = _tn_matmul(da, [h2], w_g.shape[0], bt_one, "grad_w_gate")
    (g_u,), (g_u16,) = _tn_matmul(db, [h2], w_u.shape[0], bt_one, "grad_w_up")
    (g_d,), (g_d16,) = _tn_matmul(fin, [df], w_d.shape[0], bt_one, "grad_w_down")
    token = on_ffn_grads([g_g, g_u, g_d], [g_g16, g_u16, g_d16])
    do, dpd, dps, dwp = _mixer_bwd(dmix, w_out2, pooled, w_pool, pool_scale + token, seq, tm)
    dq, dk, dv = _attn_bwd(qn, k, kt, v, do, ltot, seq, tq, tk)
    gx, du, accb8, accg8 = _inproj_bwd(dq, dk, dv, dpd, xt, dx1, mod, g_mpre, w_in, seq, tm)

    g_in, g_in16 = [jnp.concatenate(parts, axis=0) for parts in _tn_matmul(h1, [dq, dk, dv, du], 1, bt, "grad_w_in")]
    g_out, g_out16 = [parts[0].reshape(w_out.shape) for parts in _tn_matmul(mixin, [dmix], 1, bt_one, "grad_w_out")]

    dmod = jnp.stack([accb8[:, 0], accb8[:, 1], accb5[:, 2], accb5[:, 0], accb5[:, 1], accb4[:, 0]], axis=1)
    dgain = jnp.stack([accg8[0], accg5[1], accg5[0], accg4[0]], axis=0)
    grads = [g_in, g_out, g_g, g_u, g_d]
    grads16 = [g_in16, g_out16, g_g16, g_u16, g_d16]
    return loss_blk, gx, grads, grads16, dmod, dgain, dps[0:1], dwp


def kernel(x, c, w_cond, b_cond, g_mix_pre, g_mix_post, w_in, w_pool, pool_scale, w_out, g_ffn_pre, g_ffn_post, w_gate, w_up, w_down, loss_target, m_w_cond, m_b_cond, m_g_mix_pre, m_g_mix_post, m_w_in, m_w_pool, m_pool_scale, m_w_out, m_g_ffn_pre, m_g_ffn_post, m_w_gate, m_w_up, m_w_down, v_w_cond, v_b_cond, v_g_mix_pre, v_g_mix_post, v_w_in, v_w_pool, v_pool_scale, v_w_out, v_g_ffn_pre, v_g_ffn_post, v_w_gate, v_w_up, v_w_down):
    xi, yi, ci = _position()
    chip = 2 * xi + yi
    dev = 4 * xi + 2 * yi + ci
    nb, seq, d = x.shape
    t_all = nb * seq
    xt = x.reshape(t_all, d)
    tgt = loss_target.reshape(t_all, d)
    ncol = w_cond.shape[2]
    pw = pool_scale.shape[1]

    c_pad = jnp.concatenate([c, jnp.zeros((8 - nb, d), F32)], axis=0)
    c_all = _all_gather(c_pad, "gather_c").reshape(N_DEV, 8, d)[:, :nb].reshape(N_DEV * nb, d)
    b_q = lax.dynamic_slice(b_cond, (0, chip * ncol), (1, ncol))
    sc_all, mod_q = _cond_fwd(c_all, w_cond[0], b_q, 512)
    mod_parts = _all_gather(mod_q, "gather_mod").reshape(N_DEV, N_DEV * nb, ncol)
    mod_rows = lax.dynamic_slice(mod_parts, (0, dev * nb, 0), (N_DEV, nb, ncol))[0::2]
    mod = jnp.transpose(mod_rows, (1, 0, 2)).reshape(nb, N_MOD, d)
    mod = jnp.concatenate([mod, jnp.zeros((nb, MOD_ROWS - N_MOD, d), F32)], axis=1)

    place = jnp.stack([chip, ci]).astype(jnp.int32)
    turned = lambda t: jnp.swapaxes(t[0], 0, 1)
    placed = _place_quarters(place, [w_in[0], w_out[0], turned(w_gate), turned(w_up), w_down[0]])
    (w_in_all,) = _gather_weights(placed[:1])
    send_sems, recv_sems, in_flight, token = _gather_start(placed[1:], [mod, w_in_all])
    mod = mod + token[0:1, 0:1]

    def later_weights(after):
        return _gather_forward(_gather_wait(send_sems, recv_sems, in_flight, after))

    ffn_split = []

    def on_ffn_grads(ffn_grads, ffn_grads16):
        theirs = _sibling_exchange(ffn_grads16, "ffn")
        ffn_split.extend(_chip_exchange_start(_chip_sums(place[1:], ffn_grads, theirs, "ffn"), [], "ffn"))
        return ffn_split[4][0:1, 0:1]

    gains = (g_mix_pre, g_mix_post, g_ffn_pre, g_ffn_post)
    loss_blk, gx, grads, grads16, dmod, dgain, dps, dwp = _local_step(
        xt, tgt, mod, gains, w_pool[0], pool_scale, w_in_all, later_weights, on_ffn_grads, seq)

    sums_ffn, parts_ffn = _chip_exchange_wait(*ffn_split[:4], gx, "ffn")

    wp_rows = dwp.size // d
    loss_row = 2 * N_MOD + 4 + 1
    pad_rows = 24 - (loss_row + 1)
    payload = jnp.concatenate([
        dmod.reshape(nb * N_MOD, d), dgain,
        jnp.concatenate([dps, jnp.zeros((1, d - pw), F32)], axis=1),
        jnp.concatenate([loss_blk[0:1], jnp.zeros((1, d - LANES), F32)], axis=1),
        jnp.zeros((pad_rows, d), F32), dwp.reshape(wp_rows, d)], axis=0)
    prow = payload.shape[0]
    gathered = _all_gather(payload, "gather_small")
    summed = _group_sum(gathered, prow, "small_device_sum")
    loss = summed[loss_row, 0]
    dmod_all = gathered.reshape(N_DEV, prow, d)[:, :nb * N_MOD].reshape(N_DEV * nb, N_MOD * d)
    g_b_cond = _group_sum(dmod_all, 1, "grad_b_cond")
    dmod_q = lax.dynamic_slice(dmod_all, (0, chip * ncol), (N_DEV * nb, ncol))
    g_w_cond = _cond_bwd(sc_all, dmod_q, 512)
    first_gain = 2 * N_MOD
    g_gains = [summed[first_gain + r:first_gain + r + 1] for r in range(4)]
    g_pool_scale = summed[first_gain + 4:first_gain + 5, :pw]
    g_w_pool = summed[24:24 + wp_rows].reshape(w_pool.shape[1] * w_pool.shape[2], w_pool.shape[3])

    theirs = _sibling_exchange(grads16[:2], "mix")
    mix_split = _chip_exchange_start(_chip_sums(place[1:], grads[:2], theirs, "mix"), [gathered], "mix")
    unfold = lambda halves: [g.reshape(2 * g.shape[1], g.shape[2]) for g in halves]
    g_ffn = unfold(_sibling_share(_total_sums(place, sums_ffn, parts_ffn, [mix_split[4]], "ffn"), "ffn"))

    flat_pool = lambda t: t.reshape(g_w_pool.shape)
    results = {}

    def update(name, w2, g2, m2, v2, shape):
        delta, new_m, new_v = _adamw(w2, g2, m2, v2, "adamw_" + name)
        back = (lambda t: jnp.swapaxes(t, 0, 1)[None]) if shape is None else (lambda t: t.reshape(shape))
        results[name] = [back(t) for t in (g2, delta, new_m, new_v)]
        return delta

    done = [update("w_gate", turned(w_gate), g_ffn[0], turned(m_w_gate), turned(v_w_gate), None),
            update("w_up", turned(w_up), g_ffn[1], turned(m_w_up), turned(v_w_up), None),
            update("w_down", w_down[0], g_ffn[2], m_w_down[0], v_w_down[0], w_down.shape),
            update("w_cond", w_cond[0], g_w_cond, m_w_cond[0], v_w_cond[0], w_cond.shape)]
    update("b_cond", b_cond, g_b_cond, m_b_cond, v_b_cond, b_cond.shape)
    update("g_mix_pre", g_mix_pre, g_gains[0], m_g_mix_pre, v_g_mix_pre, g_mix_pre.shape)
    update("g_mix_post", g_mix_post, g_gains[1], m_g_mix_post, v_g_mix_post, g_mix_post.shape)
    update("w_pool", flat_pool(w_pool), g_w_pool, flat_pool(m_w_pool), flat_pool(v_w_pool), w_pool.shape)
    update("pool_scale", pool_scale, g_pool_scale, m_pool_scale, v_pool_scale, pool_scale.shape)
    update("g_ffn_pre", g_ffn_pre, g_gains[2], m_g_ffn_pre, v_g_ffn_pre, g_ffn_pre.shape)
    update("g_ffn_post", g_ffn_post, g_gains[3], m_g_ffn_post, v_g_ffn_post, g_ffn_post.shape)

    sums_mix, parts_mix = _chip_exchange_wait(*mix_split[:4], done[-1], "mix")
    g_mix = unfold(_sibling_share(_total_sums(place, sums_mix, parts_mix, done[:3], "mix"), "mix"))
    update("w_in", w_in[0], g_mix[0], m_w_in[0], v_w_in[0], w_in.shape)
    update("w_out", w_out[0], g_mix[1], m_w_out[0], v_w_out[0], w_out.shape)

    names = ("w_cond", "b_cond", "g_mix_pre", "g_mix_post", "w_in", "w_pool", "pool_scale", "w_out",
             "g_ffn_pre", "g_ffn_post", "w_gate", "w_up", "w_down")
    outs = [results[name][part] for part in range(4) for name in names]
    return (loss, gx.reshape(x.shape), *outs)
```

```python
import jax
import jax.numpy as jnp
from jax import lax
from jax.experimental import pallas as pl
from jax.experimental.pallas import tpu as pltpu

F32 = jnp.float32
BF16 = jnp.bfloat16
MESH = pl.DeviceIdType.MESH

EPS = 1e-6
HEAD_DIM = 64
HEADS_PER_BLOCK = 2
LANES = 128
NEG_QK_SCALE = -0.125
POOL_WINDOWS = (2, 4, 8, 16)
POOL_GROUP = 128
HALO = 16
N_MOD = 6
MOD_ROWS = 8
N_CHIPS = 4
N_DEV = 8
VMEM_LIMIT = 56 * 1024 * 1024

ADAM_LR = 0.001
ADAM_B1 = 0.9
ADAM_B2 = 0.999
ADAM_EPS = 1e-08
ADAM_WD = 0.01
ADAM_STEP = 10

TOKEN_TILE = 512
GRAD_TOKEN_TILE = 2048
FFN_ROW_CHUNKS = 2
ROW_CHUNKS = 2
ATTN_TILE = 512
ATTN_KEY_TILE = 256
ATTN_ROW_CHUNK = 32
LOG_SUM_PASSES = 1


def _dot(a, b):
    return jnp.dot(a, b, preferred_element_type=F32)


def _dot_nt(a, b):
    return lax.dot_general(a, b, (((1,), (1,)), ((), ())), preferred_element_type=F32)


def _dot_tn(a, b):
    return lax.dot_general(a, b, (((0,), (0,)), ((), ())), preferred_element_type=F32)


def _split(v):
    hi = v.astype(BF16)
    lo = (v - hi.astype(F32)).astype(BF16)
    return hi, lo


def _rms(v):
    return lax.rsqrt(jnp.mean(v * v, axis=-1, keepdims=True) + EPS)


def _norm_bwd(dn, n, r):
    return r * (dn - n * jnp.mean(dn * n, axis=-1, keepdims=True))


def _sigmoid(v):
    return 0.5 * jnp.tanh(0.5 * v) + 0.5


def _colsum(v):
    return jnp.sum(v, axis=0, keepdims=True)


def _params(sem=None):
    return pltpu.CompilerParams(dimension_semantics=sem, vmem_limit_bytes=VMEM_LIMIT)


def _position():
    return lax.axis_index("x"), lax.axis_index("y"), lax.axis_index("c")


def _prenorm_proj(x, mod, g_pre, w_in, seq, tm):
    t_all, d = x.shape
    nt = seq // tm
    p = w_in.shape[2]

    def body(x_ref, mod_ref, g_ref, w_ref, h_ref, q_ref, k_ref, v_ref, u_ref, kt_ref, vt_ref):
        for c in range(ROW_CHUNKS):
            rows = slice(c * (tm // ROW_CHUNKS), (c + 1) * (tm // ROW_CHUNKS))
            xf = x_ref[rows, :]
            n = xf * _rms(xf)
            h = (n * g_ref[...]) * (1.0 + mod_ref[0, 1:2, :]) + mod_ref[0, 0:1, :]
            hb = h.astype(BF16)
            h_ref[rows, :] = hb
            q_ref[rows, :] = (_dot(hb, w_ref[0]) * NEG_QK_SCALE).astype(BF16)
            kf = _dot(hb, w_ref[1])
            vf = _dot(hb, w_ref[2])
            k_ref[rows, :] = kf.astype(BF16)
            v_ref[rows, :] = vf.astype(BF16)
            kt_ref[:, rows] = kf.T.astype(BF16)
            vt_ref[:, rows] = vf.T.astype(BF16)
            u_ref[rows, :] = _dot(hb, w_ref[3])

    tok = lambda i: (i, 0)
    tok_t = lambda i: (0, i)
    return pl.pallas_call(
        body, name="prenorm_proj", grid=(t_all // tm,),
        in_specs=[pl.BlockSpec((tm, d), tok),
                  pl.BlockSpec((1, MOD_ROWS, d), lambda i: (i // nt, 0, 0)),
                  pl.BlockSpec((1, d), lambda i: (0, 0)),
                  pl.BlockSpec((N_CHIPS, d, p), lambda i: (0, 0, 0))],
        out_specs=[pl.BlockSpec((tm, d), tok)] + [pl.BlockSpec((tm, p), tok)] * 4 + [pl.BlockSpec((p, tm), tok_t)] * 2,
        out_shape=[jax.ShapeDtypeStruct((t_all, d), BF16)] + [jax.ShapeDtypeStruct((t_all, p), BF16)] * 3
        + [jax.ShapeDtypeStruct((t_all, p), F32)] + [jax.ShapeDtypeStruct((p, t_all), BF16)] * 2,
        compiler_params=_params(("arbitrary",)),
    )(x, mod, g_pre, w_in)


def _tri_matrix(tk, kind):
    j = lax.broadcasted_iota(jnp.int32, (2 * tk, tk), 0) % tk
    s = lax.broadcasted_iota(jnp.int32, (2 * tk, tk), 1)
    return {"after": j > s, "upto": j <= s, "before": j < s}[kind].astype(BF16)


def _neg_abs(v):
    bits = lax.bitcast_convert_type(v, jnp.int32) | jnp.int32(-2 ** 31)
    return lax.bitcast_convert_type(bits, F32)


def _row_sums(v):
    return jnp.broadcast_to(jnp.sum(v, axis=-1, keepdims=True), (v.shape[0], LANES))


def _across(v, n):
    return jnp.concatenate([v] * (n // LANES), axis=1)


def _all_masked(c, diag, rc, tk):
    return diag is not None and diag * tk >= (c + 1) * rc - 1


def _some_masked(c, diag, rc, tk):
    return diag is not None and diag * tk + tk - 1 >= c * rc


def _attn_fwd(qn, k, vt, seq, tq, tk):
    t_all, w = qn.shape
    nb, nq, ndiag = t_all // seq, seq // tq, tq // tk
    assert ndiag % 2 == 0, "two key blocks per loop trip"
    rc = ATTN_ROW_CHUNK
    heads = range(HEADS_PER_BLOCK)

    def body(q_ref, k_ref, vt_ref, tri_ref, o_ref, l_ref,
             z_buf, ls_buf, hl_buf, aft_buf, w_buf, tot_buf, acc_t, run_buf):
        i = pl.program_id(2)
        nblk = (i + 1) * ndiag
        lane = lax.broadcasted_iota(jnp.int32, (1, LANES), 1)
        row = lax.broadcasted_iota(jnp.int32, (rc, tk), 0)
        col = lax.broadcasted_iota(jnp.int32, (rc, tk), 1)
        first = lane < HEAD_DIM
        q2 = q_ref[...]
        qs = [jnp.where(first, q2, jnp.zeros_like(q2)), jnp.where(first, jnp.zeros_like(q2), q2)]
        acc_t[...] = jnp.zeros_like(acc_t)
        run_buf[...] = jnp.zeros_like(run_buf)
        w_buf[1] = jnp.zeros((HEADS_PER_BLOCK, tq, tk), BF16)

        def causal(c, diag):
            return (col + diag * tk) < (row + c * rc)

        def scores(blk, slot):
            kj = k_ref[pl.ds(pl.multiple_of(blk * tk, tk), tk), :]
            for h in heads:
                z_buf[slot, h] = _dot_nt(qs[h], kj)

        def values(blk, slot):
            keys = pl.ds(pl.multiple_of(blk * tk, tk), tk)
            for h in heads:
                dims = slice(h * HEAD_DIM, (h + 1) * HEAD_DIM)
                acc_t[dims, :] += _dot_nt(vt_ref[dims, keys], w_buf[slot, h])

        def softplus_stage(h, slot, diag):
            for c in range(tq // rc):
                rows = slice(c * rc, (c + 1) * rc)
                if _all_masked(c, diag, rc, tk):
                    hl_buf[h, rows, :] = jnp.zeros((rc, LOG_SUM_PASSES * tk), BF16)
                    tot_buf[h, rows, :] = jnp.zeros((rc, LANES), F32)
                    continue
                nz = z_buf[slot, h, rows, :]
                l1 = jnp.minimum(nz, 0.0) - jnp.log(1.0 + jnp.exp(_neg_abs(nz)))
                if _some_masked(c, diag, rc, tk):
                    l1 = jnp.where(causal(c, diag), l1, 0.0)
                for s, part in enumerate(_split(l1)[:LOG_SUM_PASSES]):
                    hl_buf[h, rows, s * tk:(s + 1) * tk] = part
                ls_buf[h, rows, :] = l1 - nz
                tot_buf[h, rows, :] = _row_sums(l1)

        def weights_stage(h, slot, diag):
            for c in range(tq // rc):
                rows = slice(c * rc, (c + 1) * rc)
                if _all_masked(c, diag, rc, tk):
                    w_buf[slot, h, rows, :] = jnp.zeros((rc, tk), BF16)
                    continue
                wgt = jnp.exp((ls_buf[h, rows, :] + aft_buf[h, rows, :]) + _across(run_buf[h, rows, :], tk))
                if _some_masked(c, diag, rc, tk):
                    wgt = jnp.where(causal(c, diag), wgt, 0.0)
                w_buf[slot, h, rows, :] = wgt.astype(BF16)
                run_buf[h, rows, :] += tot_buf[h, rows, :]

        def position(blk, slot, diag):
            scores(jnp.maximum(blk - 1, 0), 1 - slot)
            for h in heads:
                softplus_stage(h, slot, diag)
                aft_buf[h] = _dot(hl_buf[h], tri_ref[...])
            values(jnp.minimum(blk + 1, nblk - 1), 1 - slot)
            for h in heads:
                weights_stage(h, slot, diag)

        scores(nblk - 1, 0)
        for p in range(ndiag):
            position(nblk - 1 - p, p % 2, ndiag - 1 - p)

        def trip(jj, carry):
            for u in range(2):
                position(i * ndiag - 1 - 2 * jj - u, u, None)
            return carry

        lax.fori_loop(0, (i * ndiag) // 2, trip, 0)
        values(0, 1)
        o_ref[...] = acc_t[...].T.astype(BF16)
        l_ref[...] = jnp.where(first, run_buf[0], run_buf[1])

    qmap = lambda b, hp, i: (b * nq + i, hp)
    nh = HEADS_PER_BLOCK
    return pl.pallas_call(
        body, name="attn_fwd", grid=(nb, w // LANES, nq),
        in_specs=[pl.BlockSpec((tq, LANES), qmap), pl.BlockSpec((seq, LANES), lambda b, hp, i: (b, hp)),
                  pl.BlockSpec((LANES, seq), lambda b, hp, i: (hp, b)),
                  pl.BlockSpec((LOG_SUM_PASSES * tk, tk), lambda b, hp, i: (0, 0))],
        out_specs=[pl.BlockSpec((tq, LANES), qmap), pl.BlockSpec((tq, LANES), qmap)],
        out_shape=[jax.ShapeDtypeStruct((t_all, w), BF16), jax.ShapeDtypeStruct((t_all, w), F32)],
        scratch_shapes=[pltpu.VMEM((2, nh, tq, tk), F32), pltpu.VMEM((nh, tq, tk), F32),
                        pltpu.VMEM((nh, tq, LOG_SUM_PASSES * tk), BF16), pltpu.VMEM((nh, tq, tk), F32),
                        pltpu.VMEM((2, nh, tq, tk), BF16), pltpu.VMEM((nh, tq, LANES), F32),
                        pltpu.VMEM((LANES, tq), F32), pltpu.VMEM((nh, tq, LANES), F32)],
        compiler_params=_params(("arbitrary", "arbitrary", "arbitrary")),
    )(qn, k, vt, _tri_matrix(tk, "after")[:LOG_SUM_PASSES * tk])


def _window_sums(ext, rows, offset, forward):
    r = lax.broadcasted_iota(jnp.int32, (rows, rows + HALO), 0)
    e = lax.broadcasted_iota(jnp.int32, (rows, rows + HALO), 1)
    hi, lo = _split(ext)
    out = []
    for g, win in enumerate(POOL_WINDOWS):
        if forward:
            band = (e >= r) & (e < r + win)
        else:
            band = (e <= r + offset) & (e > r + offset - win)
        bm = band.astype(BF16)
        cols = slice(g * POOL_GROUP, (g + 1) * POOL_GROUP)
        out.append(_dot(bm, hi[:, cols]) + _dot(bm, lo[:, cols]))
    return out


def _window_counts(pos):
    return [jnp.minimum(pos + 1, win).astype(F32) for win in POOL_WINDOWS]


def _mixer_post(u, o, x, mod, g_post, g_fpre, w_pool, pool_scale, w_out, seq, tm):
    t_all, d = x.shape
    nt = seq // tm
    p = u.shape[1]

    def body(u_ref, halo_ref, o_ref, x_ref, mod_ref, gp_ref, gf_ref, wp_ref, ps_ref, wo_ref,
             pooled_ref, mixin_ref, mix_ref, x1_ref, h2_ref):
        it = pl.program_id(0) % nt
        uf = u_ref[...]
        halo = jnp.where(it == 0, 0.0, halo_ref[...])
        ext = jnp.concatenate([halo, uf], axis=0)
        pos = it * tm + lax.broadcasted_iota(jnp.int32, (tm, 1), 0)
        sums = _window_sums(ext, tm, HALO, False)
        cnts = _window_counts(pos)
        pools = []
        for g in range(len(POOL_WINDOWS)):
            cols = slice(g * POOL_GROUP, (g + 1) * POOL_GROUP)
            pooled = (sums[g] / cnts[g] - uf[:, cols]).astype(BF16)
            pooled_ref[:, cols] = pooled
            yg = _dot(pooled, wp_ref[g].astype(BF16))
            pools.append((yg * ps_ref[:, cols]).astype(BF16))
        mixin_ref[...] = jnp.concatenate([o_ref[...]] + pools, axis=1)
        for c in range(ROW_CHUNKS):
            rows = slice(c * (tm // ROW_CHUNKS), (c + 1) * (tm // ROW_CHUNKS))
            mix = _dot(mixin_ref[rows, :], wo_ref[...])
            mix_ref[rows, :] = mix
            n2 = mix * _rms(mix)
            x1 = x_ref[rows, :] + mod_ref[0, 2:3, :] * (n2 * gp_ref[...])
            x1_ref[rows, :] = x1
            n3 = x1 * _rms(x1)
            h2 = (n3 * gf_ref[...]) * (1.0 + mod_ref[0, 4:5, :]) + mod_ref[0, 3:4, :]
            h2_ref[rows, :] = h2.astype(BF16)

    tok = lambda i: (i, 0)
    const2 = lambda i: (0, 0)
    hb = tm // HALO
    return pl.pallas_call(
        body, name="mixer_post", grid=(t_all // tm,),
        in_specs=[pl.BlockSpec((tm, p), tok),
                  pl.BlockSpec((HALO, p), lambda i: (jnp.maximum(i * hb - 1, 0), 0)),
                  pl.BlockSpec((tm, p), tok),
                  pl.BlockSpec((tm, d), tok),
                  pl.BlockSpec((1, MOD_ROWS, d), lambda i: (i // nt, 0, 0)),
                  pl.BlockSpec((1, d), const2), pl.BlockSpec((1, d), const2),
                  pl.BlockSpec(w_pool.shape, lambda i: (0, 0, 0)),
                  pl.BlockSpec((1, p), const2),
                  pl.BlockSpec((d, d), const2)],
        out_specs=[pl.BlockSpec((tm, p), tok), pl.BlockSpec((tm, d), tok), pl.BlockSpec((tm, d), tok),
                   pl.BlockSpec((tm, d), tok), pl.BlockSpec((tm, d), tok)],
        out_shape=[jax.ShapeDtypeStruct((t_all, p), BF16), jax.ShapeDtypeStruct((t_all, d), BF16),
                   jax.ShapeDtypeStruct((t_all, d), F32), jax.ShapeDtypeStruct((t_all, d), F32),
                   jax.ShapeDtypeStruct((t_all, d), BF16)],
        compiler_params=_params(("arbitrary",)),
    )(u, u, o, x, mod, g_post, g_fpre, w_pool, pool_scale, w_out)


def _ffn_fwd(h2, w_g, w_u, w_d, x1, tgt, mod, g_post, seq, tm):
    t_all, d = x1.shape
    nt = seq // tm
    nk, ff, _ = w_g.shape

    def body(h_ref, wg_ref, wu_ref, wd_ref, x1_ref, t_ref, mod_ref, g_ref,
             a_ref, b_ref, fin_ref, dy_ref, df_ref, loss_ref, accb_ref, accg_ref, facc):
        i, k = pl.program_id(0), pl.program_id(1)

        @pl.when(k == 0)
        def _():
            facc[...] = jnp.zeros_like(facc)

        for c in range(FFN_ROW_CHUNKS):
            rows = slice(c * (tm // FFN_ROW_CHUNKS), (c + 1) * (tm // FFN_ROW_CHUNKS))
            hb = h_ref[rows, :]
            a = _dot_nt(hb, wg_ref[0])
            b = _dot_nt(hb, wu_ref[0])
            a_ref[0, rows, :] = a.astype(BF16)
            b_ref[0, rows, :] = b.astype(BF16)
            fin = ((a * _sigmoid(a)) * b).astype(BF16)
            fin_ref[0, rows, :] = fin
            facc[rows, :] += _dot(fin, wd_ref[0])

        @pl.when(k == nk - 1)
        def _():
            f = facc[...]
            r4 = _rms(f)
            n4 = f * r4
            gate = mod_ref[0, 5:6, :]
            g = g_ref[...]
            err = (x1_ref[...] + gate * (n4 * g)) - t_ref[...]
            dy = err * (1.0 / d)
            dy_ref[...] = dy

            @pl.when(i == 0)
            def _():
                loss_ref[...] = jnp.zeros_like(loss_ref)
                accg_ref[...] = jnp.zeros_like(accg_ref)

            @pl.when(i % nt == 0)
            def _():
                accb_ref[...] = jnp.zeros_like(accb_ref)

            loss_ref[...] += (0.5 / d) * jnp.sum(err * err)
            accb_ref[0, 0:1, :] += _colsum(dy * (n4 * g))
            accg_ref[0:1, :] += _colsum((dy * gate) * n4)
            dn4 = (dy * gate) * g
            df_ref[...] = _norm_bwd(dn4, n4, r4).astype(BF16)

    tok = lambda i, k: (i, 0)
    ktok = lambda i, k: (k, i, 0)
    kw = lambda i, k: (k, 0, 0)
    const2 = lambda i, k: (0, 0)
    return pl.pallas_call(
        body, name="ffn_fwd", grid=(t_all // tm, nk),
        in_specs=[pl.BlockSpec((tm, d), tok),
                  pl.BlockSpec((1, ff, d), kw), pl.BlockSpec((1, ff, d), kw), pl.BlockSpec((1, ff, d), kw),
                  pl.BlockSpec((tm, d), tok), pl.BlockSpec((tm, d), tok),
                  pl.BlockSpec((1, MOD_ROWS, d), lambda i, k: (i // nt, 0, 0)),
                  pl.BlockSpec((1, d), const2)],
        out_specs=[pl.BlockSpec((1, tm, ff), ktok)] * 3
        + [pl.BlockSpec((tm, d), tok), pl.BlockSpec((tm, d), tok),
           pl.BlockSpec((8, LANES), const2),
           pl.BlockSpec((1, 8, d), lambda i, k: (i // nt, 0, 0)),
           pl.BlockSpec((8, d), const2)],
        out_shape=[jax.ShapeDtypeStruct((nk, t_all, ff), BF16)] * 3
        + [jax.ShapeDtypeStruct((t_all, d), F32), jax.ShapeDtypeStruct((t_all, d), BF16),
           jax.ShapeDtypeStruct((8, LANES), F32),
           jax.ShapeDtypeStruct((t_all // seq, 8, d), F32),
           jax.ShapeDtypeStruct((8, d), F32)],
        scratch_shapes=[pltpu.VMEM((tm, d), F32)],
        compiler_params=_params(("arbitrary", "arbitrary")),
    )(h2, w_g, w_u, w_d, x1, tgt, mod, g_post)


def _ffn_bwd(df, a, b, w_d, w_g, w_u, x1, dy, mix, mod, g_fpre, g_mpost, seq, tm):
    t_all, d = x1.shape
    nt = seq // tm
    nk, ff, _ = w_g.shape

    def body(df_ref, a_ref, b_ref, wd_ref, wg_ref, wu_ref, x1_ref, dy_ref, mix_ref, mod_ref, gf_ref, gm_ref,
             da_ref, db_ref, dx1_ref, dmix_ref, accb_ref, accg_ref, hacc):
        i, k = pl.program_id(0), pl.program_id(1)

        @pl.when(k == 0)
        def _():
            hacc[...] = jnp.zeros_like(hacc)

        for c in range(FFN_ROW_CHUNKS):
            rows = slice(c * (tm // FFN_ROW_CHUNKS), (c + 1) * (tm // FFN_ROW_CHUNKS))
            dfin = _dot_nt(df_ref[rows, :], wd_ref[0])
            af = a_ref[0, rows, :].astype(F32)
            bf = b_ref[0, rows, :].astype(F32)
            sig = _sigmoid(af)
            da = ((dfin * bf) * (sig * (1.0 + af * (1.0 - sig)))).astype(BF16)
            db = (dfin * (af * sig)).astype(BF16)
            da_ref[0, rows, :] = da
            db_ref[0, rows, :] = db
            hacc[rows, :] += _dot(da, wg_ref[0]) + _dot(db, wu_ref[0])

        @pl.when(k == nk - 1)
        def _():
            @pl.when(i == 0)
            def _():
                accg_ref[...] = jnp.zeros_like(accg_ref)

            @pl.when(i % nt == 0)
            def _():
                accb_ref[...] = jnp.zeros_like(accb_ref)

            dh2 = hacc[...]
            x1 = x1_ref[...]
            r3 = _rms(x1)
            n3 = x1 * r3
            g3 = gf_ref[...]
            scale1 = 1.0 + mod_ref[0, 4:5, :]
            accb_ref[0, 0:1, :] += _colsum(dh2)
            accb_ref[0, 1:2, :] += _colsum(dh2 * (n3 * g3))
            accg_ref[0:1, :] += _colsum((dh2 * scale1) * n3)
            dx1 = dy_ref[...] + _norm_bwd((dh2 * scale1) * g3, n3, r3)
            dx1_ref[...] = dx1
            mix = mix_ref[...]
            r2 = _rms(mix)
            n2 = mix * r2
            g2 = gm_ref[...]
            gate = mod_ref[0, 2:3, :]
            accb_ref[0, 2:3, :] += _colsum(dx1 * (n2 * g2))
            accg_ref[1:2, :] += _colsum((dx1 * gate) * n2)
            dmix_ref[...] = _norm_bwd((dx1 * gate) * g2, n2, r2).astype(BF16)

    tok = lambda i, k: (i, 0)
    ktok = lambda i, k: (k, i, 0)
    kw = lambda i, k: (k, 0, 0)
    const2 = lambda i, k: (0, 0)
    return pl.pallas_call(
        body, name="ffn_bwd", grid=(t_all // tm, nk),
        in_specs=[pl.BlockSpec((tm, d), tok),
                  pl.BlockSpec((1, tm, ff), ktok), pl.BlockSpec((1, tm, ff), ktok),
                  pl.BlockSpec((1, ff, d), kw), pl.BlockSpec((1, ff, d), kw), pl.BlockSpec((1, ff, d), kw),
                  pl.BlockSpec((tm, d), tok), pl.BlockSpec((tm, d), tok), pl.BlockSpec((tm, d), tok),
                  pl.BlockSpec((1, MOD_ROWS, d), lambda i, k: (i // nt, 0, 0)),
                  pl.BlockSpec((1, d), const2), pl.BlockSpec((1, d), const2)],
        out_specs=[pl.BlockSpec((1, tm, ff), ktok)] * 2
        + [pl.BlockSpec((tm, d), tok), pl.BlockSpec((tm, d), tok),
           pl.BlockSpec((1, 8, d), lambda i, k: (i // nt, 0, 0)),
           pl.BlockSpec((8, d), const2)],
        out_shape=[jax.ShapeDtypeStruct((nk, t_all, ff), BF16)] * 2
        + [jax.ShapeDtypeStruct((t_all, d), F32), jax.ShapeDtypeStruct((t_all, d), BF16),
           jax.ShapeDtypeStruct((t_all // seq, 8, d), F32),
           jax.ShapeDtypeStruct((8, d), F32)],
        scratch_shapes=[pltpu.VMEM((tm, d), F32)],
        compiler_params=_params(("arbitrary", "arbitrary")),
    )(df, a, b, w_d, w_g, w_u, x1, dy, mix, mod, g_fpre, g_mpost)


def _mixer_bwd(dmix, w_out, pooled, w_pool, pool_scale, seq, tm):
    t_all, d = dmix.shape
    p = pooled.shape[1]
    ng = len(POOL_WINDOWS)

    def body(dm_ref, wo_ref, pooled_ref, wp_ref, ps_ref, do_ref, dpd_ref, dps_ref, dwp_ref):
        i = pl.program_id(0)

        @pl.when(i == 0)
        def _():
            dps_ref[...] = jnp.zeros_like(dps_ref)
            dwp_ref[...] = jnp.zeros_like(dwp_ref)

        dmixin = _dot_nt(dm_ref[...], wo_ref[...])
        do_ref[...] = dmixin[:, :p].astype(BF16)
        for g in range(ng):
            cols = slice(g * POOL_GROUP, (g + 1) * POOL_GROUP)
            dpool = dmixin[:, p + g * POOL_GROUP:p + (g + 1) * POOL_GROUP]
            pooled = pooled_ref[:, cols]
            wpg = wp_ref[g].astype(BF16)
            yg = _dot(pooled, wpg)
            dps_ref[0:1, cols] += _colsum(dpool * yg)
            dyg = (dpool * ps_ref[:, cols]).astype(BF16)
            dwp_ref[g] += _dot_tn(pooled, dyg)
            dpd_ref[:, cols] = _dot_nt(dyg, wpg)

    tok = lambda i: (i, 0)
    const2 = lambda i: (0, 0)
    const3 = lambda i: (0, 0, 0)
    return pl.pallas_call(
        body, name="mixer_bwd", grid=(t_all // tm,),
        in_specs=[pl.BlockSpec((tm, d), tok), pl.BlockSpec((d, d), const2), pl.BlockSpec((tm, p), tok),
                  pl.BlockSpec(w_pool.shape, const3), pl.BlockSpec((1, p), const2)],
        out_specs=[pl.BlockSpec((tm, p), tok), pl.BlockSpec((tm, p), tok),
                   pl.BlockSpec((8, p), const2), pl.BlockSpec(w_pool.shape, const3)],
        out_shape=[jax.ShapeDtypeStruct((t_all, p), BF16), jax.ShapeDtypeStruct((t_all, p), F32),
                   jax.ShapeDtypeStruct((8, p), F32), jax.ShapeDtypeStruct(w_pool.shape, F32)],
        compiler_params=_params(("arbitrary",)),
    )(dmix, w_out, pooled, w_pool, pool_scale)


def _attn_bwd(qn, k, kt, v, do, ltot, seq, tq, tk):
    t_all, w = qn.shape
    nb, nq, ndiag, nkb = t_all // seq, seq // tq, tq // tk, seq // tk
    assert ndiag % 2 == 0, "two key blocks per loop trip"
    rc = ATTN_ROW_CHUNK
    nh = HEADS_PER_BLOCK
    heads = range(nh)

    def body(q_ref, k_ref, kt_ref, v_ref, do_ref, l_ref, up_ref, bf_ref, dq_ref, dk_ref, dv_ref,
             z_buf, dw_buf, ls_buf, hl_buf, upto_buf, g_buf, gb_buf, before_buf, w_buf, dz_buf,
             totl_buf, totg_buf, rem_buf, preg_buf, qnt_buf, dot_buf, dq_t, dk_t, dv_t):
        i = pl.program_id(2)
        nblk = (i + 1) * ndiag

        @pl.when(i == 0)
        def _():
            dk_t[...] = jnp.zeros_like(dk_t)
            dv_t[...] = jnp.zeros_like(dv_t)

        lane = lax.broadcasted_iota(jnp.int32, (1, LANES), 1)
        row = lax.broadcasted_iota(jnp.int32, (rc, tk), 0)
        col = lax.broadcasted_iota(jnp.int32, (rc, tk), 1)
        first = lane < HEAD_DIM
        q2 = q_ref[...]
        do2 = do_ref[...]
        l2 = l_ref[...]
        qs = [jnp.where(first, q2, jnp.zeros_like(q2)), jnp.where(first, jnp.zeros_like(q2), q2)]
        dos = [jnp.where(first, do2, jnp.zeros_like(do2)), jnp.where(first, jnp.zeros_like(do2), do2)]
        qnt_buf[...] = q2.astype(F32).T.astype(BF16)
        dot_buf[...] = do2.astype(F32).T.astype(BF16)
        for h in heads:
            rem_buf[h] = jnp.where(first if h == 0 else ~first, l2, pltpu.roll(l2, HEAD_DIM, 1))
        preg_buf[...] = jnp.zeros_like(preg_buf)
        dq_t[...] = jnp.zeros_like(dq_t)
        w_buf[1] = jnp.zeros((nh * tq, tk), BF16)
        dz_buf[1] = jnp.zeros((nh * tq, tk), BF16)

        def causal(c, diag):
            return (col + diag * tk) < (row + c * rc)

        def scores(blk, slot):
            off = pl.multiple_of(blk * tk, tk)
            kj = k_ref[pl.ds(off, tk), :]
            vj = v_ref[pl.ds(off, tk), :]
            for h in heads:
                z_buf[slot, h] = _dot_nt(qs[h], kj)
                dw_buf[slot, h] = _dot_nt(dos[h], vj)

        def gradients(blk, slot):
            keys = pl.ds(pl.multiple_of(blk * tk, tk), tk)
            for h in heads:
                dims = slice(h * HEAD_DIM, (h + 1) * HEAD_DIM)
                queries = slice(h * tq, (h + 1) * tq)
                dq_t[dims, :] += _dot_nt(kt_ref[dims, keys], dz_buf[slot, queries, :])
                dk_t[blk, dims, :] += _dot(qnt_buf[dims, :], dz_buf[slot, queries, :])
                dv_t[blk, dims, :] += _dot(dot_buf[dims, :], w_buf[slot, queries, :])

        def softplus_stage(h, slot, diag):
            for c in range(tq // rc):
                rows = slice(c * rc, (c + 1) * rc)
                if _all_masked(c, diag, rc, tk):
                    hl_buf[h, rows, :] = jnp.zeros((rc, LOG_SUM_PASSES * tk), BF16)
                    continue
                nz = z_buf[slot, h, rows, :]
                l1 = jnp.minimum(nz, 0.0) - jnp.log(1.0 + jnp.exp(_neg_abs(nz)))
                if _some_masked(c, diag, rc, tk):
                    l1 = jnp.where(causal(c, diag), l1, 0.0)
                for s, part in enumerate(_split(l1)[:LOG_SUM_PASSES]):
                    hl_buf[h, rows, s * tk:(s + 1) * tk] = part
                ls_buf[h, rows, :] = l1 - nz
                totl_buf[h, rows, :] = _row_sums(l1)

        def weights_stage(h, slot, diag):
            for c in range(tq // rc):
                rows = slice(c * rc, (c + 1) * rc)
                stacked = slice(h * tq + c * rc, h * tq + (c + 1) * rc)
                if _all_masked(c, diag, rc, tk):
                    w_buf[slot, stacked, :] = jnp.zeros((rc, tk), BF16)
                    gb_buf[h, rows, :] = jnp.zeros((rc, tk), BF16)
                    continue
                wgt = jnp.exp(ls_buf[h, rows, :] + (_across(rem_buf[h, rows, :], tk) - upto_buf[h, rows, :]))
                if _some_masked(c, diag, rc, tk):
                    wgt = jnp.where(causal(c, diag), wgt, 0.0)
                w_buf[slot, stacked, :] = wgt.astype(BF16)
                g = wgt * dw_buf[slot, h, rows, :]
                g_buf[h, rows, :] = g
                gb_buf[h, rows, :] = g.astype(BF16)
                totg_buf[h, rows, :] = _row_sums(g)
                rem_buf[h, rows, :] -= totl_buf[h, rows, :]

        def dscore_stage(h, slot, diag):
            for c in range(tq // rc):
                rows = slice(c * rc, (c + 1) * rc)
                stacked = slice(h * tq + c * rc, h * tq + (c + 1) * rc)
                if _all_masked(c, diag, rc, tk):
                    dz_buf[slot, stacked, :] = jnp.zeros((rc, tk), BF16)
                    continue
                sig = jnp.exp(ls_buf[h, rows, :])
                g = g_buf[h, rows, :]
                dnz = sig * ((before_buf[h, rows, :] + _across(preg_buf[h, rows, :], tk)) + g) - g
                if _some_masked(c, diag, rc, tk):
                    dnz = jnp.where(causal(c, diag), dnz, 0.0)
                dz_buf[slot, stacked, :] = dnz.astype(BF16)
                preg_buf[h, rows, :] += totg_buf[h, rows, :]

        def position(blk, slot, diag, prefetch):
            if prefetch:
                scores(blk + 1, 1 - slot)
            for h in heads:
                softplus_stage(h, slot, diag)
                upto_buf[h] = _dot(hl_buf[h], up_ref[...])
            gradients(jnp.maximum(blk - 1, 0), 1 - slot)
            for h in heads:
                weights_stage(h, slot, diag)
                before_buf[h] = _dot(gb_buf[h], bf_ref[...])
            for h in heads:
                dscore_stage(h, slot, diag)

        scores(0, 0)

        def trip(jj, carry):
            for u in range(2):
                position(2 * jj + u, u, None, True)
            return carry

        lax.fori_loop(0, (i * ndiag) // 2, trip, 0)
        for d in range(ndiag):
            position(i * ndiag + d, d % 2, d, d < ndiag - 1)
        gradients(nblk - 1, 1)
        dq_ref[...] = (dq_t[...].T * NEG_QK_SCALE).astype(BF16)

        @pl.when(i == nq - 1)
        def _():
            for blk in range(nkb):
                dk_ref[blk * tk:(blk + 1) * tk, :] = dk_t[blk].T.astype(BF16)
                dv_ref[blk * tk:(blk + 1) * tk, :] = dv_t[blk].T.astype(BF16)

    qmap = lambda b, hp, i: (b * nq + i, hp)
    kmap = lambda b, hp, i: (b, hp)
    const = lambda b, hp, i: (0, 0)
    return pl.pallas_call(
        body, name="attn_bwd", grid=(nb, w // LANES, nq),
        in_specs=[pl.BlockSpec((tq, LANES), qmap), pl.BlockSpec((seq, LANES), kmap),
                  pl.BlockSpec((LANES, seq), lambda b, hp, i: (hp, b)), pl.BlockSpec((seq, LANES), kmap),
                  pl.BlockSpec((tq, LANES), qmap), pl.BlockSpec((tq, LANES), qmap),
                  pl.BlockSpec((LOG_SUM_PASSES * tk, tk), const), pl.BlockSpec((tk, tk), const)],
        out_specs=[pl.BlockSpec((tq, LANES), qmap), pl.BlockSpec((seq, LANES), kmap), pl.BlockSpec((seq, LANES), kmap)],
        out_shape=[jax.ShapeDtypeStruct((t_all, w), BF16)] * 3,
        scratch_shapes=[pltpu.VMEM((2, nh, tq, tk), F32), pltpu.VMEM((2, nh, tq, tk), F32),
                        pltpu.VMEM((nh, tq, tk), F32), pltpu.VMEM((nh, tq, LOG_SUM_PASSES * tk), BF16),
                        pltpu.VMEM((nh, tq, tk), F32), pltpu.VMEM((nh, tq, tk), F32),
                        pltpu.VMEM((nh, tq, tk), BF16), pltpu.VMEM((nh, tq, tk), F32),
                        pltpu.VMEM((2, nh * tq, tk), BF16), pltpu.VMEM((2, nh * tq, tk), BF16),
                        pltpu.VMEM((nh, tq, LANES), F32), pltpu.VMEM((nh, tq, LANES), F32),
                        pltpu.VMEM((nh, tq, LANES), F32), pltpu.VMEM((nh, tq, LANES), F32),
                        pltpu.VMEM((LANES, tq), BF16), pltpu.VMEM((LANES, tq), BF16),
                        pltpu.VMEM((LANES, tq), F32), pltpu.VMEM((nkb, LANES, tk), F32),
                        pltpu.VMEM((nkb, LANES, tk), F32)],
        compiler_params=_params(("arbitrary", "arbitrary", "arbitrary")),
    )(qn, k, kt, v, do, ltot, _tri_matrix(tk, "upto")[:LOG_SUM_PASSES * tk], _tri_matrix(tk, "before")[:tk])


def _inproj_bwd(dq, dk, dv, dpd, x, dx1, mod, g_pre, w_in, seq, tm):
    t_all, d = x.shape
    nt = seq // tm
    p = dq.shape[1]

    def body(dq_ref, dk_ref, dv_ref, dpd_ref, halo_ref, x_ref, dx1_ref, mod_ref, g_ref, w_ref,
             gx_ref, du_ref, accb_ref, accg_ref):
        i = pl.program_id(0)
        it = i % nt

        @pl.when(i == 0)
        def _():
            accg_ref[...] = jnp.zeros_like(accg_ref)

        @pl.when(it == 0)
        def _():
            accb_ref[...] = jnp.zeros_like(accb_ref)

        dpd = dpd_ref[...]
        pos = it * tm + lax.broadcasted_iota(jnp.int32, (tm, 1), 0)
        cnts = _window_counts(pos)
        halo = jnp.where(it == nt - 1, 0.0, halo_ref[...])
        scaled = []
        halos = []
        for g, win in enumerate(POOL_WINDOWS):
            cols = slice(g * POOL_GROUP, (g + 1) * POOL_GROUP)
            scaled.append(dpd[:, cols] / cnts[g])
            halos.append(halo[:, cols] / float(win))
        ext = jnp.concatenate([jnp.concatenate(scaled, axis=1), jnp.concatenate(halos, axis=1)], axis=0)
        sums = _window_sums(ext, tm, 0, True)
        du = (jnp.concatenate(sums, axis=1) - dpd).astype(BF16)
        du_ref[...] = du
        g1 = g_ref[...]
        scale1 = 1.0 + mod_ref[0, 1:2, :]
        for c in range(ROW_CHUNKS):
            rows = slice(c * (tm // ROW_CHUNKS), (c + 1) * (tm // ROW_CHUNKS))
            dh1 = (_dot_nt(dq_ref[rows, :], w_ref[0]) + _dot_nt(dk_ref[rows, :], w_ref[1])
                   + _dot_nt(dv_ref[rows, :], w_ref[2]) + _dot_nt(du_ref[rows, :], w_ref[3]))
            xf = x_ref[rows, :]
            r1 = _rms(xf)
            n1 = xf * r1
            accb_ref[0, 0:1, :] += _colsum(dh1)
            accb_ref[0, 1:2, :] += _colsum(dh1 * (n1 * g1))
            accg_ref[0:1, :] += _colsum((dh1 * scale1) * n1)
            gx_ref[rows, :] = dx1_ref[rows, :] + _norm_bwd((dh1 * scale1) * g1, n1, r1)

    tok = lambda i: (i, 0)
    const2 = lambda i: (0, 0)
    hb = tm // HALO
    last = t_all // HALO - 1
    return pl.pallas_call(
        body, name="inproj_bwd", grid=(t_all // tm,),
        in_specs=[pl.BlockSpec((tm, p), tok), pl.BlockSpec((tm, p), tok), pl.BlockSpec((tm, p), tok),
                  pl.BlockSpec((tm, p), tok),
                  pl.BlockSpec((HALO, p), lambda i: (jnp.minimum((i + 1) * hb, last), 0)),
                  pl.BlockSpec((tm, d), tok), pl.BlockSpec((tm, d), tok),
                  pl.BlockSpec((1, MOD_ROWS, d), lambda i: (i // nt, 0, 0)),
                  pl.BlockSpec((1, d), const2),
                  pl.BlockSpec((N_CHIPS, d, p), lambda i: (0, 0, 0))],
        out_specs=[pl.BlockSpec((tm, d), tok), pl.BlockSpec((tm, p), tok),
                   pl.BlockSpec((1, 8, d), lambda i: (i // nt, 0, 0)),
                   pl.BlockSpec((8, d), const2)],
        out_shape=[jax.ShapeDtypeStruct((t_all, d), F32), jax.ShapeDtypeStruct((t_all, p), BF16),
                   jax.ShapeDtypeStruct((t_all // seq, 8, d), F32),
                   jax.ShapeDtypeStruct((8, d), F32)],
        compiler_params=_params(("arbitrary",)),
    )(dq, dk, dv, dpd, dpd, x, dx1, mod, g_pre, w_in)


def _tn_matmul(x, ys, nk, bt, name):
    t_all = x.shape[-2]
    m = x.shape[-1]
    ny = len(ys)
    nt = t_all // bt

    def spec(arr):
        if arr.ndim == 3:
            return pl.BlockSpec((1, bt, arr.shape[-1]), lambda k, t: (k, t, 0))
        return pl.BlockSpec((bt, arr.shape[-1]), lambda k, t: (t, 0))

    def tile(ref):
        return ref[0] if len(ref.shape) == 3 else ref[...]

    def body(*refs):
        x_ref, y_refs, o_refs, h_refs = refs[0], refs[1:1 + ny], refs[1 + ny:1 + 2 * ny], refs[1 + 2 * ny:]
        t = pl.program_id(1)
        xt = tile(x_ref)
        for y_ref, o_ref, h_ref in zip(y_refs, o_refs, h_refs):
            part = _dot_tn(xt, tile(y_ref))

            @pl.when(t == 0)
            def _(o_ref=o_ref, part=part):
                o_ref[0] = part

            @pl.when(t > 0)
            def _(o_ref=o_ref, part=part):
                o_ref[0] += part

            @pl.when(t == nt - 1)
            def _(o_ref=o_ref, h_ref=h_ref):
                h_ref[0] = o_ref[0].astype(BF16)

    out_specs = [pl.BlockSpec((1, m, y.shape[-1]), lambda k, t: (k, 0, 0)) for y in ys]
    out = pl.pallas_call(
        body, name=name, grid=(nk, nt),
        in_specs=[spec(x)] + [spec(y) for y in ys],
        out_specs=out_specs * 2,
        out_shape=[jax.ShapeDtypeStruct((nk, m, y.shape[-1]), dt) for dt in (F32, BF16) for y in ys],
        compiler_params=_params(("arbitrary", "arbitrary")),
    )(x, *ys)
    return out[:ny], out[ny:]


def _cond_fwd(c_all, w_q, b_q, bn):
    nrow, d = c_all.shape
    ncol = w_q.shape[1]

    def body(c_ref, w_ref, b_ref, sc_ref, mod_ref):
        cf = c_ref[...]
        sc = cf * _sigmoid(cf)
        sc_ref[...] = sc
        shi, slo = _split(sc)
        whi, wlo = _split(w_ref[...])
        mod_ref[...] = (_dot(shi, whi) + _dot(shi, wlo) + _dot(slo, whi)) + b_ref[...]

    return pl.pallas_call(
        body, name="cond_fwd", grid=(ncol // bn,),
        in_specs=[pl.BlockSpec((nrow, d), lambda n: (0, 0)), pl.BlockSpec((d, bn), lambda n: (0, n)),
                  pl.BlockSpec((1, bn), lambda n: (0, n))],
        out_specs=[pl.BlockSpec((nrow, d), lambda n: (0, 0)), pl.BlockSpec((nrow, bn), lambda n: (0, n))],
        out_shape=[jax.ShapeDtypeStruct((nrow, d), F32), jax.ShapeDtypeStruct((nrow, ncol), F32)],
        compiler_params=_params(("arbitrary",)),
    )(c_all, w_q, b_q)


def _cond_bwd(sc_all, dmod_q, bn):
    nrow, d = sc_all.shape
    ncol = dmod_q.shape[1]

    def body(sc_ref, dm_ref, gw_ref):
        shi, slo = _split(sc_ref[...])
        dhi, dlo = _split(dm_ref[...])
        gw_ref[...] = _dot_tn(shi, dhi) + _dot_tn(shi, dlo) + _dot_tn(slo, dhi)

    return pl.pallas_call(
        body, name="cond_bwd", grid=(ncol // bn,),
        in_specs=[pl.BlockSpec((nrow, d), lambda n: (0, 0)), pl.BlockSpec((nrow, bn), lambda n: (0, n))],
        out_specs=pl.BlockSpec((d, bn), lambda n: (0, n)),
        out_shape=jax.ShapeDtypeStruct((d, ncol), F32),
        compiler_params=_params(("arbitrary",)),
    )(sc_all, dmod_q)


def _row_block(rows, cols, budget=1 << 18):
    best = None
    for br in range(8, rows + 1, 8):
        if rows % br == 0 and br * cols <= budget:
            best = br
    return best if best is not None else rows


def _adamw(w, g, m, v, name):
    rows, cols = w.shape
    br = _row_block(rows, cols)
    c1 = 1.0 - ADAM_B1 ** ADAM_STEP
    c2 = 1.0 - ADAM_B2 ** ADAM_STEP

    def body(w_ref, g_ref, m_ref, v_ref, d_ref, nm_ref, nv_ref):
        gf = g_ref[...]
        m2 = ADAM_B1 * m_ref[...] + (1.0 - ADAM_B1) * gf
        v2 = ADAM_B2 * v_ref[...] + (1.0 - ADAM_B2) * (gf * gf)
        nm_ref[...] = m2
        nv_ref[...] = v2
        d_ref[...] = -ADAM_LR * ((m2 / c1) / (jnp.sqrt(v2 / c2) + ADAM_EPS) + ADAM_WD * w_ref[...])

    blk = pl.BlockSpec((br, cols), lambda i: (i, 0))
    return pl.pallas_call(
        body, name=name, grid=(rows // br,),
        in_specs=[blk] * 4, out_specs=[blk] * 3,
        out_shape=[jax.ShapeDtypeStruct((rows, cols), F32)] * 3,
        compiler_params=_params(("arbitrary",)),
    )(w, g, m, v)


def _all_gather(x_shard, name):
    m_per, n = x_shard.shape

    def body(x_ref, out_ref, send_sems, recv_sems, local_sem):
        x, y, c = _position()
        me, sibling = (x, y, c), (x, y, 1 - c)
        chips = [(1 - x, y), (x, 1 - y), (1 - x, 1 - y)]

        def rows(px, py, pc):
            return out_ref.at[pl.ds((4 * px + 2 * py + pc) * m_per, m_per), :]

        def copy(k, block, to, src=None):
            return pltpu.make_async_remote_copy(
                src_ref=rows(*block) if src is None else src, dst_ref=rows(*block),
                send_sem=send_sems.at[k], recv_sem=recv_sems.at[k], device_id=to, device_id_type=MESH)

        mine = pltpu.make_async_copy(x_ref, rows(*me), local_sem)
        mine.start()
        first = [copy(0, me, sibling, src=x_ref)]
        first += [copy(1 + j, me, (*chip, c), src=x_ref) for j, chip in enumerate(chips)]
        for cp in first:
            cp.start()
        passed = [copy(4 + j, (*chip, c), sibling) for j, chip in enumerate(chips)]
        for j, chip in enumerate(chips):
            copy(1 + j, (*chip, c), me).wait_recv()
            passed[j].start()
        copy(0, sibling, me).wait_recv()
        for j, chip in enumerate(chips):
            copy(4 + j, (*chip, 1 - c), me).wait_recv()
        for cp in first + passed:
            cp.wait_send()
        mine.wait()

    return pl.pallas_call(
        body, name=name,
        out_shape=jax.ShapeDtypeStruct((N_DEV * m_per, n), x_shard.dtype),
        in_specs=[pl.BlockSpec(memory_space=pltpu.VMEM)],
        out_specs=pl.BlockSpec(memory_space=pltpu.VMEM),
        scratch_shapes=[pltpu.SemaphoreType.DMA((7,)), pltpu.SemaphoreType.DMA((7,)), pltpu.SemaphoreType.DMA],
        compiler_params=pltpu.CompilerParams(vmem_limit_bytes=VMEM_LIMIT),
    )(x_shard)


_ANY = pl.BlockSpec(memory_space=pl.ANY)


def _place_quarters(place, quarters):
    steps = 2

    def body(place_ref, *refs):
        n = len(refs) // 2
        for w_ref, o_ref in zip(refs[:n], refs[n:]):
            o_ref[0] = w_ref[...].astype(BF16)

    return pl.pallas_call(
        body, name="place_quarters",
        grid_spec=pltpu.PrefetchScalarGridSpec(
            num_scalar_prefetch=1, grid=(steps,),
            in_specs=[pl.BlockSpec((q.shape[0] // steps, q.shape[1]), lambda r, place_ref: (r, 0)) for q in quarters],
            out_specs=[pl.BlockSpec((1, q.shape[0] // steps, q.shape[1]), lambda r, place_ref: (place_ref[0], r, 0))
                       for q in quarters]),
        out_shape=[jax.ShapeDtypeStruct((N_CHIPS,) + q.shape, BF16) for q in quarters],
        compiler_params=_params(("arbitrary",)),
    )(place, *quarters)


def _gather_weights(placed):
    n = len(placed)
    shapes = [b.shape[1:] for b in placed]

    def body(*refs):
        g_refs = refs[n:2 * n]
        send_sems, recv_sems = refs[2 * n:]
        x, y, c = _position()
        sibling = (x, y, 1 - c)
        chips = [(1 - x, y), (x, 1 - y), (1 - x, 1 - y)]
        mine = 2 * x + y

        def half(a, which):
            hr = shapes[a][0] // 2
            return pl.ds(which * hr, hr)

        def over_ici(a, p, slot):
            ref = g_refs[a].at[slot, half(a, c), :]
            return pltpu.make_async_remote_copy(
                src_ref=ref, dst_ref=ref,
                send_sem=send_sems.at[6 * a + p], recv_sem=recv_sems.at[6 * a + p],
                device_id=(*chips[p], c), device_id_type=MESH)

        def over_d2d(a, p, slot, which):
            ref = g_refs[a].at[slot, half(a, which), :]
            return pltpu.make_async_remote_copy(
                src_ref=ref, dst_ref=ref,
                send_sem=send_sems.at[6 * a + 3 + p], recv_sem=recv_sems.at[6 * a + 3 + p],
                device_id=sibling, device_id_type=MESH)

        sends = []
        for a in range(n):
            for p in range(3):
                cp = over_ici(a, p, mine)
                cp.start()
                sends.append(cp)
        for a in range(n):
            for p, (cx, cy) in enumerate(chips):
                slot = 2 * cx + cy
                over_ici(a, p, slot).wait_recv()
                cp = over_d2d(a, p, slot, c)
                cp.start()
                sends.append(cp)
        for a in range(n):
            for p, (cx, cy) in enumerate(chips):
                over_d2d(a, p, 2 * cx + cy, 1 - c).wait_recv()
        for cp in sends:
            cp.wait_send()

    return pl.pallas_call(
        body, name="gather_weights",
        out_shape=[jax.ShapeDtypeStruct(b.shape, BF16) for b in placed],
        in_specs=[_ANY] * n, out_specs=[_ANY] * n,
        input_output_aliases={a: a for a in range(n)},
        scratch_shapes=[pltpu.SemaphoreType.DMA((6 * n,)), pltpu.SemaphoreType.DMA((6 * n,))],
    )(*placed)


_HBM = pl.BlockSpec(memory_space=pltpu.HBM)
_SEM = pl.BlockSpec(memory_space=pltpu.SEMAPHORE)
_EFFECT = pltpu.SideEffectType.DATAFLOW_SIDE_EFFECTING


def _quarter_halves(shapes, a, which):
    hr = shapes[a][0] // 2
    return pl.ds(which * hr, hr)


def _gather_start(placed, after):
    n = len(placed)
    m = len(after)
    shapes = [b.shape[1:] for b in placed]

    def body(*refs):
        g_refs = refs[:n]
        send_sems, recv_sems = refs[n + m], refs[n + m + 1]
        token = refs[2 * n + m + 2]
        x, y, c = _position()
        chips = [(1 - x, y), (x, 1 - y), (1 - x, 1 - y)]
        mine = 2 * x + y
        for a in range(n):
            ref = g_refs[a].at[mine, _quarter_halves(shapes, a, c), :]
            for p in range(3):
                pltpu.make_async_remote_copy(
                    src_ref=ref, dst_ref=ref, send_sem=send_sems.at[3 * a + p], recv_sem=recv_sems.at[3 * a + p],
                    device_id=(*chips[p], c), device_id_type=MESH).start()
        token[...] = jnp.zeros_like(token)

    out = pl.pallas_call(
        body, name="gather_start",
        out_shape=(pltpu.SemaphoreType.DMA((3 * n,)), pltpu.SemaphoreType.DMA((3 * n,)),
                   *[pltpu.HBM(b.shape, b.dtype) for b in placed], jax.ShapeDtypeStruct((8, LANES), F32)),
        in_specs=[_HBM] * n + [_ANY] * m,
        out_specs=(_SEM, _SEM, *[_HBM] * n, pl.BlockSpec(memory_space=pltpu.VMEM)),
        input_output_aliases={a: 2 + a for a in range(n)},
        compiler_params=pltpu.CompilerParams(has_side_effects=_EFFECT),
    )(*[pltpu.with_memory_space_constraint(b, pltpu.HBM) for b in placed], *after)
    return out[0], out[1], list(out[2:2 + n]), out[2 + n]


def _gather_wait(send_sems, recv_sems, thru, after):
    n = len(thru)
    shapes = [b.shape[1:] for b in thru]

    def body(*refs):
        g_refs = refs[:n]
        send_sems, recv_sems = refs[n], refs[n + 1]
        x, y, c = _position()
        chips = [(1 - x, y), (x, 1 - y), (1 - x, 1 - y)]
        mine = 2 * x + y
        for a in range(n):
            rows = _quarter_halves(shapes, a, c)
            for p, (cx, cy) in enumerate(chips):
                copy = pltpu.make_async_remote_copy(
                    src_ref=g_refs[a].at[mine, rows, :], dst_ref=g_refs[a].at[2 * cx + cy, rows, :],
                    send_sem=send_sems.at[3 * a + p], recv_sem=recv_sems.at[3 * a + p],
                    device_id=(cx, cy, c), device_id_type=MESH)
                copy.wait_send()
                copy.wait_recv()

    return pl.pallas_call(
        body, name="gather_wait",
        out_shape=[pltpu.HBM(b.shape, b.dtype) for b in thru],
        in_specs=[_HBM] * n + [_SEM, _SEM, _ANY], out_specs=[_HBM] * n,
        input_output_aliases={a: a for a in range(n)},
        compiler_params=pltpu.CompilerParams(has_side_effects=_EFFECT),
    )(*thru, send_sems, recv_sems, after)


def _gather_forward(bufs):
    n = len(bufs)
    shapes = [b.shape[1:] for b in bufs]

    def body(*refs):
        g_refs = refs[n:2 * n]
        send_sems, recv_sems = refs[2 * n:]
        x, y, c = _position()
        chips = [(1 - x, y), (x, 1 - y), (1 - x, 1 - y)]

        def over_d2d(a, p, which):
            cx, cy = chips[p]
            ref = g_refs[a].at[2 * cx + cy, _quarter_halves(shapes, a, which), :]
            return pltpu.make_async_remote_copy(
                src_ref=ref, dst_ref=ref, send_sem=send_sems.at[3 * a + p], recv_sem=recv_sems.at[3 * a + p],
                device_id=(x, y, 1 - c), device_id_type=MESH)

        sends = [over_d2d(a, p, c) for a in range(n) for p in range(3)]
        for cp in sends:
            cp.start()
        for a in range(n):
            for p in range(3):
                over_d2d(a, p, 1 - c).wait_recv()
        for cp in sends:
            cp.wait_send()

    return pl.pallas_call(
        body, name="gather_forward",
        out_shape=[jax.ShapeDtypeStruct(b.shape, BF16) for b in bufs],
        in_specs=[_ANY] * n, out_specs=[_ANY] * n,
        input_output_aliases={a: a for a in range(n)},
        scratch_shapes=[pltpu.SemaphoreType.DMA((3 * n,)), pltpu.SemaphoreType.DMA((3 * n,))],
    )(*bufs)


def _sibling_exchange(grads, tag):
    n = len(grads)
    shapes = [g.shape for g in grads]

    def body(*refs):
        g_refs, x_refs = refs[:n], refs[n:2 * n]
        send_sems, recv_sems = refs[2 * n:]
        x, y, c = _position()
        copies = []
        for a in range(n):
            hr = shapes[a][1] // 2
            cp = pltpu.make_async_remote_copy(
                src_ref=g_refs[a].at[:, pl.ds((1 - c) * hr, hr), :], dst_ref=x_refs[a],
                send_sem=send_sems.at[a], recv_sem=recv_sems.at[a],
                device_id=(x, y, 1 - c), device_id_type=MESH)
            cp.start()
            copies.append(cp)
        for cp in copies:
            cp.wait()

    return pl.pallas_call(
        body, name="grad_sibling_exchange_" + tag,
        out_shape=[jax.ShapeDtypeStruct((g.shape[0], g.shape[1] // 2, g.shape[2]), g.dtype) for g in grads],
        in_specs=[_ANY] * n, out_specs=[_ANY] * n,
        scratch_shapes=[pltpu.SemaphoreType.DMA((n,)), pltpu.SemaphoreType.DMA((n,))],
    )(*grads)


def _chip_sums(core, grads, theirs, tag):
    n = len(grads)

    def body(core_ref, *refs):
        g_refs, t_refs, o_refs = refs[:n], refs[n:2 * n], refs[2 * n:]
        for g_ref, t_ref, o_ref in zip(g_refs, t_refs, o_refs):
            o_ref[...] = (g_ref[...] + t_ref[...].astype(F32)).astype(BF16)

    in_specs = [pl.BlockSpec((1, g.shape[1] // 2, g.shape[2]), lambda k, core_ref: (k, core_ref[0], 0)) for g in grads]
    in_specs += [pl.BlockSpec((1,) + t.shape[1:], lambda k, core_ref: (k, 0, 0)) for t in theirs]
    return pl.pallas_call(
        body, name="grad_chip_sums_" + tag,
        grid_spec=pltpu.PrefetchScalarGridSpec(
            num_scalar_prefetch=1, grid=(N_CHIPS,), in_specs=in_specs,
            out_specs=[pl.BlockSpec((1,) + t.shape[1:], lambda k, core_ref: (k, 0, 0)) for t in theirs]),
        out_shape=[jax.ShapeDtypeStruct(t.shape, BF16) for t in theirs],
        compiler_params=_params(("arbitrary",)),
    )(core, *grads, *theirs)


def _chip_exchange_start(sums, after, tag):
    n = len(sums)
    m = len(after)
    lands = [lax.empty((3,) + s.shape[1:], BF16) for s in sums]

    def body(*refs):
        s_refs, y_refs = refs[:n], refs[n:2 * n]
        send_sems, recv_sems = refs[2 * n + m], refs[2 * n + m + 1]
        token = refs[4 * n + m + 2]
        x, y, c = _position()
        chips = [(1 - x, y), (x, 1 - y), (1 - x, 1 - y)]
        for a in range(n):
            for p, (cx, cy) in enumerate(chips):
                pltpu.make_async_remote_copy(
                    src_ref=s_refs[a].at[2 * cx + cy], dst_ref=y_refs[a].at[p],
                    send_sem=send_sems.at[3 * a + p], recv_sem=recv_sems.at[3 * a + p],
                    device_id=(cx, cy, c), device_id_type=MESH).start()
        token[...] = jnp.zeros_like(token)

    both = list(sums) + lands
    out = pl.pallas_call(
        body, name="grad_chip_exchange_start_" + tag,
        out_shape=(pltpu.SemaphoreType.DMA((3 * n,)), pltpu.SemaphoreType.DMA((3 * n,)),
                   *[pltpu.HBM(b.shape, b.dtype) for b in both], jax.ShapeDtypeStruct((8, LANES), F32)),
        in_specs=[_HBM] * (2 * n) + [_ANY] * m,
        out_specs=(_SEM, _SEM, *[_HBM] * (2 * n), pl.BlockSpec(memory_space=pltpu.VMEM)),
        input_output_aliases={a: 2 + a for a in range(2 * n)},
        compiler_params=pltpu.CompilerParams(has_side_effects=_EFFECT),
    )(*[pltpu.with_memory_space_constraint(b, pltpu.HBM) for b in both], *after)
    return out[0], out[1], list(out[2:2 + n]), list(out[2 + n:2 + 2 * n]), out[2 + 2 * n]


def _chip_exchange_wait(send_sems, recv_sems, sums, lands, after, tag):
    n = len(sums)

    def body(*refs):
        s_refs, y_refs = refs[:n], refs[n:2 * n]
        send_sems, recv_sems = refs[2 * n], refs[2 * n + 1]
        x, y, c = _position()
        chips = [(1 - x, y), (x, 1 - y), (1 - x, 1 - y)]
        for a in range(n):
            for p, (cx, cy) in enumerate(chips):
                copy = pltpu.make_async_remote_copy(
                    src_ref=s_refs[a].at[2 * cx + cy], dst_ref=y_refs[a].at[p],
                    send_sem=send_sems.at[3 * a + p], recv_sem=recv_sems.at[3 * a + p],
                    device_id=(cx, cy, c), device_id_type=MESH)
                copy.wait_send()
                copy.wait_recv()

    both = list(sums) + list(lands)
    out = pl.pallas_call(
        body, name="grad_chip_exchange_wait_" + tag,
        out_shape=[pltpu.HBM(b.shape, b.dtype) for b in both],
        in_specs=[_HBM] * (2 * n) + [_SEM, _SEM, _ANY], out_specs=[_HBM] * (2 * n),
        input_output_aliases={a: a for a in range(2 * n)},
        compiler_params=pltpu.CompilerParams(has_side_effects=_EFFECT),
    )(*both, send_sems, recv_sems, after)
    return list(out[:n]), list(out[n:])


def _total_sums(place, sums, parts, after, tag):
    n = len(parts)
    m = len(after)
    steps = 2

    def body(place_ref, *refs):
        for s_ref, y_ref, o_ref in zip(refs[:n], refs[n:2 * n], refs[2 * n + m:]):
            o_ref[0] = ((s_ref[0].astype(F32) + y_ref[0].astype(F32)) + y_ref[1].astype(F32)) + y_ref[2].astype(F32)

    def step_rows(pt):
        return pt.shape[1] // steps

    in_specs = [pl.BlockSpec((1, step_rows(s), s.shape[2]), lambda r, place_ref: (place_ref[0], r, 0)) for s in sums]
    in_specs += [pl.BlockSpec((3, step_rows(pt), pt.shape[2]), lambda r, place_ref: (0, r, 0)) for pt in parts]
    in_specs += [_ANY] * m
    return pl.pallas_call(
        body, name="grad_total_sums_" + tag,
        grid_spec=pltpu.PrefetchScalarGridSpec(
            num_scalar_prefetch=1, grid=(steps,), in_specs=in_specs,
            out_specs=[pl.BlockSpec((1, step_rows(pt), pt.shape[2]), lambda r, place_ref: (place_ref[1], r, 0))
                       for pt in parts]),
        out_shape=[jax.ShapeDtypeStruct((2,) + pt.shape[1:], F32) for pt in parts],
        compiler_params=_params(("arbitrary",)),
    )(place, *sums, *parts, *after)


def _sibling_share(halves, tag):
    n = len(halves)

    def body(*refs):
        f_refs = refs[n:2 * n]
        send_sems, recv_sems = refs[2 * n:]
        x, y, c = _position()
        copies = []
        for a in range(n):
            cp = pltpu.make_async_remote_copy(
                src_ref=f_refs[a].at[c], dst_ref=f_refs[a].at[c], send_sem=send_sems.at[a], recv_sem=recv_sems.at[a],
                device_id=(x, y, 1 - c), device_id_type=MESH)
            cp.start()
            copies.append(cp)
        for a, cp in enumerate(copies):
            cp.wait_send()
            pltpu.make_async_remote_copy(
                src_ref=f_refs[a].at[1 - c], dst_ref=f_refs[a].at[1 - c], send_sem=send_sems.at[a],
                recv_sem=recv_sems.at[a], device_id=(x, y, c), device_id_type=MESH).wait_recv()

    return pl.pallas_call(
        body, name="grad_sibling_share_" + tag,
        out_shape=[jax.ShapeDtypeStruct(h.shape, F32) for h in halves],
        in_specs=[_ANY] * n, out_specs=[_ANY] * n,
        input_output_aliases={a: a for a in range(n)},
        scratch_shapes=[pltpu.SemaphoreType.DMA((n,)), pltpu.SemaphoreType.DMA((n,))],
    )(*halves)


def _group_sum(stacked, nrow, name):
    total, n = stacked.shape
    groups = total // nrow

    def body(g_ref, o_ref):
        acc = g_ref[0:nrow, :]
        for grp in range(1, groups):
            acc = acc + g_ref[grp * nrow:(grp + 1) * nrow, :]
        o_ref[...] = acc

    return pl.pallas_call(
        body, name=name,
        out_shape=jax.ShapeDtypeStruct((nrow, n), F32),
        compiler_params=pltpu.CompilerParams(vmem_limit_bytes=VMEM_LIMIT),
    )(stacked)


def _local_step(xt, tgt, mod, gains, w_pool, pool_scale, w_in, later_weights, on_ffn_grads, seq):
    g_mpre, g_mpost, g_fpre, g_fpost = gains
    d = xt.shape[1]
    tm, tq = min(TOKEN_TILE, seq), min(ATTN_TILE, seq)

    h1, qn, k, v, u, kt, vt = _prenorm_proj(xt, mod, g_mpre, w_in, seq, tm)
    tk = min(ATTN_KEY_TILE, tq // 2)
    o, ltot = _attn_fwd(qn, k, vt, seq, tq, tk)
    w_out, w_g, w_u, w_d = later_weights(o)
    w_out2 = w_out.reshape(d, d)
    pooled, mixin, mix, x1, h2 =_mixer_post(u, o, xt, mod, g_mpost, g_fpre, w_pool, pool_scale, w_out2, seq, tm)
    a, b, fin, dy, df, loss_blk, accb4, accg4 = _ffn_fwd(h2, w_g, w_u, w_d, x1, tgt, mod, g_fpost, seq, tm)
    da, db, dx1, dmix, accb5, accg5 = _ffn_bwd(df, a, b, w_d, w_g, w_u, x1, dy, mix, mod, g_fpre, g_mpost, seq, tm)
    bt = min(GRAD_TOKEN_TILE, xt.shape[0])
    bt_one = min(2 * GRAD_TOKEN_TILE, xt.shape[0])
    (g_g,), (g_g16,) = _tn_matmul(da, [h2], w_g.shape[0], bt_one, "grad_w_gate")
    (g_u,), (g_u16,) = _tn_matmul(db, [h2], w_u.shape[0], bt_one, "grad_w_up")
    (g_d,), (g_d16,) = _tn_matmul(fin, [df], w_d.shape[0], bt_one, "grad_w_down")
    token = on_ffn_grads([g_g, g_u, g_d], [g_g16, g_u16, g_d16])
    do, dpd, dps, dwp = _mixer_bwd(dmix, w_out2, pooled, w_pool, pool_scale + token, seq, tm)
    dq, dk, dv = _attn_bwd(qn, k, kt, v, do, ltot, seq, tq, tk)
    gx, du, accb8, accg8 = _inproj_bwd(dq, dk, dv, dpd, xt, dx1, mod, g_mpre, w_in, seq, tm)

    g_in, g_in16 = [jnp.concatenate(parts, axis=0) for parts in _tn_matmul(h1, [dq, dk, dv, du], 1, bt, "grad_w_in")]
    g_out, g_out16 = [parts[0].reshape(w_out.shape) for parts in _tn_matmul(mixin, [dmix], 1, bt_one, "grad_w_out")]

    dmod = jnp.stack([accb8[:, 0], accb8[:, 1], accb5[:, 2], accb5[:, 0], accb5[:, 1], accb4[:, 0]], axis=1)
    dgain = jnp.stack([accg8[0], accg5[1], accg5[0], accg4[0]], axis=0)
    grads = [g_in, g_out, g_g, g_u, g_d]
    grads16 = [g_in16, g_out16, g_g16, g_u16, g_d16]
    return loss_blk, gx, grads, grads16, dmod, dgain, dps[0:1], dwp


def kernel(x, c, w_cond, b_cond, g_mix_pre, g_mix_post, w_in, w_pool, pool_scale, w_out, g_ffn_pre, g_ffn_post, w_gate, w_up, w_down, loss_target, m_w_cond, m_b_cond, m_g_mix_pre, m_g_mix_post, m_w_in, m_w_pool, m_pool_scale, m_w_out, m_g_ffn_pre, m_g_ffn_post, m_w_gate, m_w_up, m_w_down, v_w_cond, v_b_cond, v_g_mix_pre, v_g_mix_post, v_w_in, v_w_pool, v_pool_scale, v_w_out, v_g_ffn_pre, v_g_ffn_post, v_w_gate, v_w_up, v_w_down):
    xi, yi, ci = _position()
    chip = 2 * xi + yi
    dev = 4 * xi + 2 * yi + ci
    nb, seq, d = x.shape
    t_all = nb * seq
    xt = x.reshape(t_all, d)
    tgt = loss_target.reshape(t_all, d)
    ncol = w_cond.shape[2]
    pw = pool_scale.shape[1]

    c_pad = jnp.concatenate([c, jnp.zeros((8 - nb, d), F32)], axis=0)
    c_all = _all_gather(c_pad, "gather_c").reshape(N_DEV, 8, d)[:, :nb].reshape(N_DEV * nb, d)
    b_q = lax.dynamic_slice(b_cond, (0, chip * ncol), (1, ncol))
    sc_all, mod_q = _cond_fwd(c_all, w_cond[0], b_q, 512)
    mod_parts = _all_gather(mod_q, "gather_mod").reshape(N_DEV, N_DEV * nb, ncol)
    mod_rows = lax.dynamic_slice(mod_parts, (0, dev * nb, 0), (N_DEV, nb, ncol))[0::2]
    mod = jnp.transpose(mod_rows, (1, 0, 2)).reshape(nb, N_MOD, d)
    mod = jnp.concatenate([mod, jnp.zeros((nb, MOD_ROWS - N_MOD, d), F32)], axis=1)

    place = jnp.stack([chip, ci]).astype(jnp.int32)
    turned = lambda t: jnp.swapaxes(t[0], 0, 1)
    placed = _place_quarters(place, [w_in[0], w_out[0], turned(w_gate), turned(w_up), w_down[0]])
    (w_in_all,) = _gather_weights(placed[:1])
    send_sems, recv_sems, in_flight, token = _gather_start(placed[1:], [mod, w_in_all])
    mod = mod + token[0:1, 0:1]

    def later_weights(after):
        return _gather_forward(_gather_wait(send_sems, recv_sems, in_flight, after))

    ffn_split = []

    def on_ffn_grads(ffn_grads, ffn_grads16):
        theirs = _sibling_exchange(ffn_grads16, "ffn")
        ffn_split.extend(_chip_exchange_start(_chip_sums(place[1:], ffn_grads, theirs, "ffn"), [], "ffn"))
        return ffn_split[4][0:1, 0:1]

    gains = (g_mix_pre, g_mix_post, g_ffn_pre, g_ffn_post)
    loss_blk, gx, grads, grads16, dmod, dgain, dps, dwp = _local_step(
        xt, tgt, mod, gains, w_pool[0], pool_scale, w_in_all, later_weights, on_ffn_grads, seq)

    sums_ffn, parts_ffn = _chip_exchange_wait(*ffn_split[:4], gx, "ffn")

    wp_rows = dwp.size // d
    loss_row = 2 * N_MOD + 4 + 1
    pad_rows = 24 - (loss_row + 1)
    payload = jnp.concatenate([
        dmod.reshape(nb * N_MOD, d), dgain,
        jnp.concatenate([dps, jnp.zeros((1, d - pw), F32)], axis=1),
        jnp.concatenate([loss_blk[0:1], jnp.zeros((1, d - LANES), F32)], axis=1),
        jnp.zeros((pad_rows, d), F32), dwp.reshape(wp_rows, d)], axis=0)
    prow = payload.shape[0]
    gathered = _all_gather(payload, "gather_small")
    summed = _group_sum(gathered, prow, "small_device_sum")
    loss = summed[loss_row, 0]
    dmod_all = gathered.reshape(N_DEV, prow, d)[:, :nb * N_MOD].reshape(N_DEV * nb, N_MOD * d)
    g_b_cond = _group_sum(dmod_all, 1, "grad_b_cond")
    dmod_q = lax.dynamic_slice(dmod_all, (0, chip * ncol), (N_DEV * nb, ncol))
    g_w_cond = _cond_bwd(sc_all, dmod_q, 512)
    first_gain = 2 * N_MOD
    g_gains = [summed[first_gain + r:first_gain + r + 1] for r in range(4)]
    g_pool_scale = summed[first_gain + 4:first_gain + 5, :pw]
    g_w_pool = summed[24:24 + wp_rows].reshape(w_pool.shape[1] * w_pool.shape[2], w_pool.shape[3])

    theirs = _sibling_exchange(grads16[:2], "mix")
    mix_split = _chip_exchange_start(_chip_sums(place[1:], grads[:2], theirs, "mix"), [gathered], "mix")
    unfold = lambda halves: [g.reshape(2 * g.shape[1], g.shape[2]) for g in halves]
    g_ffn = unfold(_sibling_share(_total_sums(place, sums_ffn, parts_ffn, [mix_split[4]], "ffn"), "ffn"))

    flat_pool = lambda t: t.reshape(g_w_pool.shape)
    results = {}

    def update(name, w2, g2, m2, v2, shape):
        delta, new_m, new_v = _adamw(w2, g2, m2, v2, "adamw_" + name)
        back = (lambda t: jnp.swapaxes(t, 0, 1)[None]) if shape is None else (lambda t: t.reshape(shape))
        results[name] = [back(t) for t in (g2, delta, new_m, new_v)]
        return delta

    done = [update("w_gate", turned(w_gate), g_ffn[0], turned(m_w_gate), turned(v_w_gate), None),
            update("w_up", turned(w_up), g_ffn[1], turned(m_w_up), turned(v_w_up), None),
            update("w_down", w_down[0], g_ffn[2], m_w_down[0], v_w_down[0], w_down.shape),
            update("w_cond", w_cond[0], g_w_cond, m_w_cond[0], v_w_cond[0], w_cond.shape)]
    update("b_cond", b_cond, g_b_cond, m_b_cond, v_b_cond, b_cond.shape)
    update("g_mix_pre", g_mix_pre, g_gains[0], m_g_mix_pre, v_g_mix_pre, g_mix_pre.shape)
    update("g_mix_post", g_mix_post, g_gains[1], m_g_mix_post, v_g_mix_post, g_mix_post.shape)
    update("w_pool", flat_pool(w_pool), g_w_pool, flat_pool(m_w_pool), flat_pool(v_w_pool), w_pool.shape)
    update("pool_scale", pool_scale, g_pool_scale, m_pool_scale, v_pool_scale, pool_scale.shape)
    update("g_ffn_pre", g_ffn_pre, g_gains[2], m_g_ffn_pre, v_g_ffn_pre, g_ffn_pre.shape)
    update("g_ffn_post", g_ffn_post, g_gains[3], m_g_ffn_post, v_g_ffn_post, g_ffn_post.shape)

    sums_mix, parts_mix = _chip_exchange_wait(*mix_split[:4], done[-1], "mix")
    g_mix = unfold(_sibling_share(_total_sums(place, sums_mix, parts_mix, done[:3], "mix"), "mix"))
    update("w_in", w_in[0], g_mix[0], m_w_in[0], v_w_in[0], w_in.shape)
    update("w_out", w_out[0], g_mix[1], m_w_out[0], v_w_out[0], w_out.shape)

    names = ("w_cond", "b_cond", "g_mix_pre", "g_mix_post", "w_in", "w_pool", "pool_scale", "w_out",
             "g_ffn_pre", "g_ffn_post", "w_gate", "w_up", "w_down")
    outs = [results[name][part] for part in range(4) for name in names]
    return (loss, gx.reshape(x.shape), *outs)
```

```python
import jax
import jax.numpy as jnp
import numpy as np
from jax import lax
from jax.experimental import pallas as pl
from jax.experimental.pallas import tpu as pltpu

F32 = jnp.float32
BF16 = jnp.bfloat16
MESH = pl.DeviceIdType.MESH

EPS = 1e-6
HEAD_DIM = 64
HEADS_PER_BLOCK = 2
LANES = 128
NEG_QK_SCALE = -0.125
POOL_WINDOWS = (2, 4, 8, 16)
POOL_GROUP = 128
HALO = 16
N_MOD = 6
MOD_ROWS = 8
N_CHIPS = 4
N_DEV = 8
VMEM_LIMIT = 56 * 1024 * 1024

ADAM_LR = 0.001
ADAM_B1 = 0.9
ADAM_B2 = 0.999
ADAM_EPS = 1e-08
ADAM_WD = 0.01
ADAM_STEP = 10

TOKEN_TILE = 512
GRAD_TOKEN_TILE = 2048
FFN_ROW_CHUNKS = 2
ROW_CHUNKS = 2
ATTN_TILE = 512
ATTN_KEY_TILE = 256
ATTN_ROW_CHUNK = 32
LOG_SUM_PASSES = 1


def _dot(a, b):
    return jnp.dot(a, b, preferred_element_type=F32)


def _dot_nt(a, b):
    return lax.dot_general(a, b, (((1,), (1,)), ((), ())), preferred_element_type=F32)


def _dot_tn(a, b):
    return lax.dot_general(a, b, (((0,), (0,)), ((), ())), preferred_element_type=F32)


def _split(v):
    hi = v.astype(BF16)
    lo = (v - hi.astype(F32)).astype(BF16)
    return hi, lo


def _rms(v):
    return lax.rsqrt(jnp.mean(v * v, axis=-1, keepdims=True) + EPS)


def _norm_bwd(dn, n, r):
    return r * (dn - n * jnp.mean(dn * n, axis=-1, keepdims=True))


def _sigmoid(v):
    return 0.5 * jnp.tanh(0.5 * v) + 0.5


def _colsum(v):
    return jnp.sum(v, axis=0, keepdims=True)


def _params(sem=None):
    return pltpu.CompilerParams(dimension_semantics=sem, vmem_limit_bytes=VMEM_LIMIT)


def _position():
    return lax.axis_index("x"), lax.axis_index("y"), lax.axis_index("c")


def _prenorm_proj(x, mod, g_pre, w_in, seq, tm):
    t_all, d = x.shape
    nt = seq // tm
    p = w_in.shape[2]

    def body(x_ref, mod_ref, g_ref, w_ref, h_ref, q_ref, k_ref, v_ref, u_ref, kt_ref, vt_ref):
        for c in range(ROW_CHUNKS):
            rows = slice(c * (tm // ROW_CHUNKS), (c + 1) * (tm // ROW_CHUNKS))
            xf = x_ref[rows, :]
            n = xf * _rms(xf)
            h = (n * g_ref[...]) * (1.0 + mod_ref[0, 1:2, :]) + mod_ref[0, 0:1, :]
            hb = h.astype(BF16)
            h_ref[rows, :] = hb
            q_ref[rows, :] = (_dot(hb, w_ref[0]) * NEG_QK_SCALE).astype(BF16)
            kf = _dot(hb, w_ref[1])
            vf = _dot(hb, w_ref[2])
            k_ref[rows, :] = kf.astype(BF16)
            v_ref[rows, :] = vf.astype(BF16)
            kt_ref[:, rows] = kf.T.astype(BF16)
            vt_ref[:, rows] = vf.T.astype(BF16)
            u_ref[rows, :] = _dot(hb, w_ref[3])

    tok = lambda i: (i, 0)
    tok_t = lambda i: (0, i)
    return pl.pallas_call(
        body, name="prenorm_proj", grid=(t_all // tm,),
        in_specs=[pl.BlockSpec((tm, d), tok),
                  pl.BlockSpec((1, MOD_ROWS, d), lambda i: (i // nt, 0, 0)),
                  pl.BlockSpec((1, d), lambda i: (0, 0)),
                  pl.BlockSpec((N_CHIPS, d, p), lambda i: (0, 0, 0))],
        out_specs=[pl.BlockSpec((tm, d), tok)] + [pl.BlockSpec((tm, p), tok)] * 4 + [pl.BlockSpec((p, tm), tok_t)] * 2,
        out_shape=[jax.ShapeDtypeStruct((t_all, d), BF16)] + [jax.ShapeDtypeStruct((t_all, p), BF16)] * 3
        + [jax.ShapeDtypeStruct((t_all, p), F32)] + [jax.ShapeDtypeStruct((p, t_all), BF16)] * 2,
        compiler_params=_params(("arbitrary",)),
    )(x, mod, g_pre, w_in)


def _tri_matrix(tk, kind):
    j = np.arange(2 * tk)[:, None] % tk
    s = np.arange(tk)[None, :]
    return jnp.asarray({"after": j > s, "upto": j <= s, "before": j < s}[kind], dtype=BF16)


def _neg_abs(v):
    bits = lax.bitcast_convert_type(v, jnp.int32) | jnp.int32(-2 ** 31)
    return lax.bitcast_convert_type(bits, F32)


def _row_sums(v):
    return jnp.broadcast_to(jnp.sum(v, axis=-1, keepdims=True), (v.shape[0], LANES))


def _across(v, n):
    return jnp.concatenate([v] * (n // LANES), axis=1)


def _all_masked(c, diag, rc, tk):
    return diag is not None and diag * tk >= (c + 1) * rc - 1


def _some_masked(c, diag, rc, tk):
    return diag is not None and diag * tk + tk - 1 >= c * rc


def _attn_fwd(qn, k, vt, seq, tq, tk):
    t_all, w = qn.shape
    nb, nq, ndiag = t_all // seq, seq // tq, tq // tk
    assert ndiag % 2 == 0, "two key blocks per loop trip"
    rc = ATTN_ROW_CHUNK
    heads = range(HEADS_PER_BLOCK)

    def body(q_ref, k_ref, vt_ref, tri_ref, o_ref, l_ref,
             z_buf, ls_buf, hl_buf, aft_buf, w_buf, tot_buf, acc_t, run_buf):
        i = pl.program_id(2)
        nblk = (i + 1) * ndiag
        lane = lax.broadcasted_iota(jnp.int32, (1, LANES), 1)
        row = lax.broadcasted_iota(jnp.int32, (rc, tk), 0)
        col = lax.broadcasted_iota(jnp.int32, (rc, tk), 1)
        first = lane < HEAD_DIM
        q2 = q_ref[...]
        qs = [jnp.where(first, q2, jnp.zeros_like(q2)), jnp.where(first, jnp.zeros_like(q2), q2)]
        acc_t[...] = jnp.zeros_like(acc_t)
        run_buf[...] = jnp.zeros_like(run_buf)
        w_buf[1] = jnp.zeros((HEADS_PER_BLOCK, tq, tk), BF16)

        def causal(c, diag):
            return (col + diag * tk) < (row + c * rc)

        def scores(blk, slot):
            kj = k_ref[pl.ds(pl.multiple_of(blk * tk, tk), tk), :]
            for h in heads:
                z_buf[slot, h] = _dot_nt(qs[h], kj)

        def values(blk, slot):
            keys = pl.ds(pl.multiple_of(blk * tk, tk), tk)
            for h in heads:
                dims = slice(h * HEAD_DIM, (h + 1) * HEAD_DIM)
                acc_t[dims, :] += _dot_nt(vt_ref[dims, keys], w_buf[slot, h])

        def softplus_stage(h, slot, diag):
            for c in range(tq // rc):
                rows = slice(c * rc, (c + 1) * rc)
                if _all_masked(c, diag, rc, tk):
                    hl_buf[h, rows, :] = jnp.zeros((rc, LOG_SUM_PASSES * tk), BF16)
                    tot_buf[h, rows, :] = jnp.zeros((rc, LANES), F32)
                    continue
                nz = z_buf[slot, h, rows, :]
                l1 = jnp.minimum(nz, 0.0) - jnp.log(1.0 + jnp.exp(_neg_abs(nz)))
                if _some_masked(c, diag, rc, tk):
                    l1 = jnp.where(causal(c, diag), l1, 0.0)
                for s, part in enumerate(_split(l1)[:LOG_SUM_PASSES]):
                    hl_buf[h, rows, s * tk:(s + 1) * tk] = part
                ls_buf[h, rows, :] = l1 - nz
                tot_buf[h, rows, :] = _row_sums(l1)

        def weights_stage(h, slot, diag):
            for c in range(tq // rc):
                rows = slice(c * rc, (c + 1) * rc)
                if _all_masked(c, diag, rc, tk):
                    w_buf[slot, h, rows, :] = jnp.zeros((rc, tk), BF16)
                    continue
                wgt = jnp.exp((ls_buf[h, rows, :] + aft_buf[h, rows, :]) + _across(run_buf[h, rows, :], tk))
                if _some_masked(c, diag, rc, tk):
                    wgt = jnp.where(causal(c, diag), wgt, 0.0)
                w_buf[slot, h, rows, :] = wgt.astype(BF16)
                run_buf[h, rows, :] += tot_buf[h, rows, :]

        def position(blk, slot, diag):
            scores(jnp.maximum(blk - 1, 0), 1 - slot)
            for h in heads:
                softplus_stage(h, slot, diag)
                aft_buf[h] = _dot(hl_buf[h], tri_ref[...])
            values(jnp.minimum(blk + 1, nblk - 1), 1 - slot)
            for h in heads:
                weights_stage(h, slot, diag)

        scores(nblk - 1, 0)
        for p in range(ndiag):
            position(nblk - 1 - p, p % 2, ndiag - 1 - p)

        def trip(jj, carry):
            for u in range(2):
                position(i * ndiag - 1 - 2 * jj - u, u, None)
            return carry

        lax.fori_loop(0, (i * ndiag) // 2, trip, 0)
        values(0, 1)
        o_ref[...] = acc_t[...].T.astype(BF16)
        l_ref[...] = jnp.where(first, run_buf[0], run_buf[1])

    qmap = lambda b, hp, i: (b * nq + i, hp)
    nh = HEADS_PER_BLOCK
    return pl.pallas_call(
        body, name="attn_fwd", grid=(nb, w // LANES, nq),
        in_specs=[pl.BlockSpec((tq, LANES), qmap), pl.BlockSpec((seq, LANES), lambda b, hp, i: (b, hp)),
                  pl.BlockSpec((LANES, seq), lambda b, hp, i: (hp, b)),
                  pl.BlockSpec((LOG_SUM_PASSES * tk, tk), lambda b, hp, i: (0, 0))],
        out_specs=[pl.BlockSpec((tq, LANES), qmap), pl.BlockSpec((tq, LANES), qmap)],
        out_shape=[jax.ShapeDtypeStruct((t_all, w), BF16), jax.ShapeDtypeStruct((t_all, w), F32)],
        scratch_shapes=[pltpu.VMEM((2, nh, tq, tk), F32), pltpu.VMEM((nh, tq, tk), F32),
                        pltpu.VMEM((nh, tq, LOG_SUM_PASSES * tk), BF16), pltpu.VMEM((nh, tq, tk), F32),
                        pltpu.VMEM((2, nh, tq, tk), BF16), pltpu.VMEM((nh, tq, LANES), F32),
                        pltpu.VMEM((LANES, tq), F32), pltpu.VMEM((nh, tq, LANES), F32)],
        compiler_params=_params(("arbitrary", "arbitrary", "arbitrary")),
    )(qn, k, vt, _tri_matrix(tk, "after")[:LOG_SUM_PASSES * tk])


def _window_sums(ext, rows, offset, forward):
    r = lax.broadcasted_iota(jnp.int32, (rows, rows + HALO), 0)
    e = lax.broadcasted_iota(jnp.int32, (rows, rows + HALO), 1)
    hi, lo = _split(ext)
    out = []
    for g, win in enumerate(POOL_WINDOWS):
        if forward:
            band = (e >= r) & (e < r + win)
        else:
            band = (e <= r + offset) & (e > r + offset - win)
        bm = band.astype(BF16)
        cols = slice(g * POOL_GROUP, (g + 1) * POOL_GROUP)
        out.append(_dot(bm, hi[:, cols]) + _dot(bm, lo[:, cols]))
    return out


def _window_counts(pos):
    return [jnp.minimum(pos + 1, win).astype(F32) for win in POOL_WINDOWS]


def _mixer_post(u, o, x, mod, g_post, g_fpre, w_pool, pool_scale, w_out, seq, tm):
    t_all, d = x.shape
    nt = seq // tm
    p = u.shape[1]

    def body(u_ref, halo_ref, o_ref, x_ref, mod_ref, gp_ref, gf_ref, wp_ref, ps_ref, wo_ref,
             pooled_ref, mixin_ref, mix_ref, x1_ref, h2_ref):
        it = pl.program_id(0) % nt
        uf = u_ref[...]
        halo = jnp.where(it == 0, 0.0, halo_ref[...])
        ext = jnp.concatenate([halo, uf], axis=0)
        pos = it * tm + lax.broadcasted_iota(jnp.int32, (tm, 1), 0)
        sums = _window_sums(ext, tm, HALO, False)
        cnts = _window_counts(pos)
        pools = []
        for g in range(len(POOL_WINDOWS)):
            cols = slice(g * POOL_GROUP, (g + 1) * POOL_GROUP)
            pooled = (sums[g] / cnts[g] - uf[:, cols]).astype(BF16)
            pooled_ref[:, cols] = pooled
            yg = _dot(pooled, wp_ref[g].astype(BF16))
            pools.append((yg * ps_ref[:, cols]).astype(BF16))
        mixin_ref[...] = jnp.concatenate([o_ref[...]] + pools, axis=1)
        for c in range(ROW_CHUNKS):
            rows = slice(c * (tm // ROW_CHUNKS), (c + 1) * (tm // ROW_CHUNKS))
            mix = _dot(mixin_ref[rows, :], wo_ref[...])
            mix_ref[rows, :] = mix
            n2 = mix * _rms(mix)
            x1 = x_ref[rows, :] + mod_ref[0, 2:3, :] * (n2 * gp_ref[...])
            x1_ref[rows, :] = x1
            n3 = x1 * _rms(x1)
            h2 = (n3 * gf_ref[...]) * (1.0 + mod_ref[0, 4:5, :]) + mod_ref[0, 3:4, :]
            h2_ref[rows, :] = h2.astype(BF16)

    tok = lambda i: (i, 0)
    const2 = lambda i: (0, 0)
    hb = tm // HALO
    return pl.pallas_call(
        body, name="mixer_post", grid=(t_all // tm,),
        in_specs=[pl.BlockSpec((tm, p), tok),
                  pl.BlockSpec((HALO, p), lambda i: (jnp.maximum(i * hb - 1, 0), 0)),
                  pl.BlockSpec((tm, p), tok),
                  pl.BlockSpec((tm, d), tok),
                  pl.BlockSpec((1, MOD_ROWS, d), lambda i: (i // nt, 0, 0)),
                  pl.BlockSpec((1, d), const2), pl.BlockSpec((1, d), const2),
                  pl.BlockSpec(w_pool.shape, lambda i: (0, 0, 0)),
                  pl.BlockSpec((1, p), const2),
                  pl.BlockSpec((d, d), const2)],
        out_specs=[pl.BlockSpec((tm, p), tok), pl.BlockSpec((tm, d), tok), pl.BlockSpec((tm, d), tok),
                   pl.BlockSpec((tm, d), tok), pl.BlockSpec((tm, d), tok)],
        out_shape=[jax.ShapeDtypeStruct((t_all, p), BF16), jax.ShapeDtypeStruct((t_all, d), BF16),
                   jax.ShapeDtypeStruct((t_all, d), F32), jax.ShapeDtypeStruct((t_all, d), F32),
                   jax.ShapeDtypeStruct((t_all, d), BF16)],
        compiler_params=_params(("arbitrary",)),
    )(u, u, o, x, mod, g_post, g_fpre, w_pool, pool_scale, w_out)


def _ffn_fwd(h2, w_g, w_u, w_d, x1, tgt, mod, g_post, seq, tm):
    t_all, d = x1.shape
    nt = seq // tm
    nk, ff, _ = w_g.shape

    def body(h_ref, wg_ref, wu_ref, wd_ref, x1_ref, t_ref, mod_ref, g_ref,
             a_ref, b_ref, fin_ref, dy_ref, df_ref, loss_ref, accb_ref, accg_ref, facc):
        i, k = pl.program_id(0), pl.program_id(1)

        @pl.when(k == 0)
        def _():
            facc[...] = jnp.zeros_like(facc)

        for c in range(FFN_ROW_CHUNKS):
            rows = slice(c * (tm // FFN_ROW_CHUNKS), (c + 1) * (tm // FFN_ROW_CHUNKS))
            hb = h_ref[rows, :]
            a = _dot_nt(hb, wg_ref[0])
            b = _dot_nt(hb, wu_ref[0])
            a_ref[0, rows, :] = a.astype(BF16)
            b_ref[0, rows, :] = b.astype(BF16)
            fin = ((a * _sigmoid(a)) * b).astype(BF16)
            fin_ref[0, rows, :] = fin
            facc[rows, :] += _dot(fin, wd_ref[0])

        @pl.when(k == nk - 1)
        def _():
            f = facc[...]
            r4 = _rms(f)
            n4 = f * r4
            gate = mod_ref[0, 5:6, :]
            g = g_ref[...]
            err = (x1_ref[...] + gate * (n4 * g)) - t_ref[...]
            dy = err * (1.0 / d)
            dy_ref[...] = dy

            @pl.when(i == 0)
            def _():
                loss_ref[...] = jnp.zeros_like(loss_ref)
                accg_ref[...] = jnp.zeros_like(accg_ref)

            @pl.when(i % nt == 0)
            def _():
                accb_ref[...] = jnp.zeros_like(accb_ref)

            loss_ref[...] += (0.5 / d) * jnp.sum(err * err)
            accb_ref[0, 0:1, :] += _colsum(dy * (n4 * g))
            accg_ref[0:1, :] += _colsum((dy * gate) * n4)
            dn4 = (dy * gate) * g
            df_ref[...] = _norm_bwd(dn4, n4, r4).astype(BF16)

    tok = lambda i, k: (i, 0)
    ktok = lambda i, k: (k, i, 0)
    kw = lambda i, k: (k, 0, 0)
    const2 = lambda i, k: (0, 0)
    return pl.pallas_call(
        body, name="ffn_fwd", grid=(t_all // tm, nk),
        in_specs=[pl.BlockSpec((tm, d), tok),
                  pl.BlockSpec((1, ff, d), kw), pl.BlockSpec((1, ff, d), kw), pl.BlockSpec((1, ff, d), kw),
                  pl.BlockSpec((tm, d), tok), pl.BlockSpec((tm, d), tok),
                  pl.BlockSpec((1, MOD_ROWS, d), lambda i, k: (i // nt, 0, 0)),
                  pl.BlockSpec((1, d), const2)],
        out_specs=[pl.BlockSpec((1, tm, ff), ktok)] * 3
        + [pl.BlockSpec((tm, d), tok), pl.BlockSpec((tm, d), tok),
           pl.BlockSpec((8, LANES), const2),
           pl.BlockSpec((1, 8, d), lambda i, k: (i // nt, 0, 0)),
           pl.BlockSpec((8, d), const2)],
        out_shape=[jax.ShapeDtypeStruct((nk, t_all, ff), BF16)] * 3
        + [jax.ShapeDtypeStruct((t_all, d), F32), jax.ShapeDtypeStruct((t_all, d), BF16),
           jax.ShapeDtypeStruct((8, LANES), F32),
           jax.ShapeDtypeStruct((t_all // seq, 8, d), F32),
           jax.ShapeDtypeStruct((8, d), F32)],
        scratch_shapes=[pltpu.VMEM((tm, d), F32)],
        compiler_params=_params(("arbitrary", "arbitrary")),
    )(h2, w_g, w_u, w_d, x1, tgt, mod, g_post)


def _ffn_bwd(df, a, b, w_d, w_g, w_u, x1, dy, mix, mod, g_fpre, g_mpost, seq, tm):
    t_all, d = x1.shape
    nt = seq // tm
    nk, ff, _ = w_g.shape

    def body(df_ref, a_ref, b_ref, wd_ref, wg_ref, wu_ref, x1_ref, dy_ref, mix_ref, mod_ref, gf_ref, gm_ref,
             da_ref, db_ref, dx1_ref, dmix_ref, accb_ref, accg_ref, hacc):
        i, k = pl.program_id(0), pl.program_id(1)

        @pl.when(k == 0)
        def _():
            hacc[...] = jnp.zeros_like(hacc)

        for c in range(FFN_ROW_CHUNKS):
            rows = slice(c * (tm // FFN_ROW_CHUNKS), (c + 1) * (tm // FFN_ROW_CHUNKS))
            dfin = _dot_nt(df_ref[rows, :], wd_ref[0])
            af = a_ref[0, rows, :].astype(F32)
            bf = b_ref[0, rows, :].astype(F32)
            sig = _sigmoid(af)
            da = ((dfin * bf) * (sig * (1.0 + af * (1.0 - sig)))).astype(BF16)
            db = (dfin * (af * sig)).astype(BF16)
            da_ref[0, rows, :] = da
            db_ref[0, rows, :] = db
            hacc[rows, :] += _dot(da, wg_ref[0]) + _dot(db, wu_ref[0])

        @pl.when(k == nk - 1)
        def _():
            @pl.when(i == 0)
            def _():
                accg_ref[...] = jnp.zeros_like(accg_ref)

            @pl.when(i % nt == 0)
            def _():
                accb_ref[...] = jnp.zeros_like(accb_ref)

            dh2 = hacc[...]
            x1 = x1_ref[...]
            r3 = _rms(x1)
            n3 = x1 * r3
            g3 = gf_ref[...]
            scale1 = 1.0 + mod_ref[0, 4:5, :]
            accb_ref[0, 0:1, :] += _colsum(dh2)
            accb_ref[0, 1:2, :] += _colsum(dh2 * (n3 * g3))
            accg_ref[0:1, :] += _colsum((dh2 * scale1) * n3)
            dx1 = dy_ref[...] + _norm_bwd((dh2 * scale1) * g3, n3, r3)
            dx1_ref[...] = dx1
            mix = mix_ref[...]
            r2 = _rms(mix)
            n2 = mix * r2
            g2 = gm_ref[...]
            gate = mod_ref[0, 2:3, :]
            accb_ref[0, 2:3, :] += _colsum(dx1 * (n2 * g2))
            accg_ref[1:2, :] += _colsum((dx1 * gate) * n2)
            dmix_ref[...] = _norm_bwd((dx1 * gate) * g2, n2, r2).astype(BF16)

    tok = lambda i, k: (i, 0)
    ktok = lambda i, k: (k, i, 0)
    kw = lambda i, k: (k, 0, 0)
    const2 = lambda i, k: (0, 0)
    return pl.pallas_call(
        body, name="ffn_bwd", grid=(t_all // tm, nk),
        in_specs=[pl.BlockSpec((tm, d), tok),
                  pl.BlockSpec((1, tm, ff), ktok), pl.BlockSpec((1, tm, ff), ktok),
                  pl.BlockSpec((1, ff, d), kw), pl.BlockSpec((1, ff, d), kw), pl.BlockSpec((1, ff, d), kw),
                  pl.BlockSpec((tm, d), tok), pl.BlockSpec((tm, d), tok), pl.BlockSpec((tm, d), tok),
                  pl.BlockSpec((1, MOD_ROWS, d), lambda i, k: (i // nt, 0, 0)),
                  pl.BlockSpec((1, d), const2), pl.BlockSpec((1, d), const2)],
        out_specs=[pl.BlockSpec((1, tm, ff), ktok)] * 2
        + [pl.BlockSpec((tm, d), tok), pl.BlockSpec((tm, d), tok),
           pl.BlockSpec((1, 8, d), lambda i, k: (i // nt, 0, 0)),
           pl.BlockSpec((8, d), const2)],
        out_shape=[jax.ShapeDtypeStruct((nk, t_all, ff), BF16)] * 2
        + [jax.ShapeDtypeStruct((t_all, d), F32), jax.ShapeDtypeStruct((t_all, d), BF16),
           jax.ShapeDtypeStruct((t_all // seq, 8, d), F32),
           jax.ShapeDtypeStruct((8, d), F32)],
        scratch_shapes=[pltpu.VMEM((tm, d), F32)],
        compiler_params=_params(("arbitrary", "arbitrary")),
    )(df, a, b, w_d, w_g, w_u, x1, dy, mix, mod, g_fpre, g_mpost)


def _mixer_bwd(dmix, w_out, pooled, w_pool, pool_scale, seq, tm):
    t_all, d = dmix.shape
    p = pooled.shape[1]
    ng = len(POOL_WINDOWS)

    def body(dm_ref, wo_ref, pooled_ref, wp_ref, ps_ref, do_ref, dpd_ref, dps_ref, dwp_ref):
        i = pl.program_id(0)

        @pl.when(i == 0)
        def _():
            dps_ref[...] = jnp.zeros_like(dps_ref)
            dwp_ref[...] = jnp.zeros_like(dwp_ref)

        dmixin = _dot_nt(dm_ref[...], wo_ref[...])
        do_ref[...] = dmixin[:, :p].astype(BF16)
        for g in range(ng):
            cols = slice(g * POOL_GROUP, (g + 1) * POOL_GROUP)
            dpool = dmixin[:, p + g * POOL_GROUP:p + (g + 1) * POOL_GROUP]
            pooled = pooled_ref[:, cols]
            wpg = wp_ref[g].astype(BF16)
            yg = _dot(pooled, wpg)
            dps_ref[0:1, cols] += _colsum(dpool * yg)
            dyg = (dpool * ps_ref[:, cols]).astype(BF16)
            dwp_ref[g] += _dot_tn(pooled, dyg)
            dpd_ref[:, cols] = _dot_nt(dyg, wpg)

    tok = lambda i: (i, 0)
    const2 = lambda i: (0, 0)
    const3 = lambda i: (0, 0, 0)
    return pl.pallas_call(
        body, name="mixer_bwd", grid=(t_all // tm,),
        in_specs=[pl.BlockSpec((tm, d), tok), pl.BlockSpec((d, d), const2), pl.BlockSpec((tm, p), tok),
                  pl.BlockSpec(w_pool.shape, const3), pl.BlockSpec((1, p), const2)],
        out_specs=[pl.BlockSpec((tm, p), tok), pl.BlockSpec((tm, p), tok),
                   pl.BlockSpec((8, p), const2), pl.BlockSpec(w_pool.shape, const3)],
        out_shape=[jax.ShapeDtypeStruct((t_all, p), BF16), jax.ShapeDtypeStruct((t_all, p), F32),
                   jax.ShapeDtypeStruct((8, p), F32), jax.ShapeDtypeStruct(w_pool.shape, F32)],
        compiler_params=_params(("arbitrary",)),
    )(dmix, w_out, pooled, w_pool, pool_scale)


def _attn_bwd(qn, k, kt, v, do, ltot, seq, tq, tk):
    t_all, w = qn.shape
    nb, nq, ndiag, nkb = t_all // seq, seq // tq, tq // tk, seq // tk
    assert ndiag % 2 == 0, "two key blocks per loop trip"
    rc = ATTN_ROW_CHUNK
    nh = HEADS_PER_BLOCK
    heads = range(nh)

    def body(q_ref, k_ref, kt_ref, v_ref, do_ref, l_ref, up_ref, bf_ref, dq_ref, dk_ref, dv_ref,
             z_buf, dw_buf, ls_buf, hl_buf, upto_buf, g_buf, gb_buf, before_buf, w_buf, dz_buf,
             totl_buf, totg_buf, rem_buf, preg_buf, qnt_buf, dot_buf, dq_t, dk_t, dv_t):
        i = pl.program_id(2)
        nblk = (i + 1) * ndiag

        @pl.when(i == 0)
        def _():
            dk_t[...] = jnp.zeros_like(dk_t)
            dv_t[...] = jnp.zeros_like(dv_t)

        lane = lax.broadcasted_iota(jnp.int32, (1, LANES), 1)
        row = lax.broadcasted_iota(jnp.int32, (rc, tk), 0)
        col = lax.broadcasted_iota(jnp.int32, (rc, tk), 1)
        first = lane < HEAD_DIM
        q2 = q_ref[...]
        do2 = do_ref[...]
        l2 = l_ref[...]
        qs = [jnp.where(first, q2, jnp.zeros_like(q2)), jnp.where(first, jnp.zeros_like(q2), q2)]
        dos = [jnp.where(first, do2, jnp.zeros_like(do2)), jnp.where(first, jnp.zeros_like(do2), do2)]
        qnt_buf[...] = q2.astype(F32).T.astype(BF16)
        dot_buf[...] = do2.astype(F32).T.astype(BF16)
        for h in heads:
            rem_buf[h] = jnp.where(first if h == 0 else ~first, l2, pltpu.roll(l2, HEAD_DIM, 1))
        preg_buf[...] = jnp.zeros_like(preg_buf)
        dq_t[...] = jnp.zeros_like(dq_t)
        w_buf[1] = jnp.zeros((nh * tq, tk), BF16)
        dz_buf[1] = jnp.zeros((nh * tq, tk), BF16)

        def causal(c, diag):
            return (col + diag * tk) < (row + c * rc)

        def scores(blk, slot):
            off = pl.multiple_of(blk * tk, tk)
            kj = k_ref[pl.ds(off, tk), :]
            vj = v_ref[pl.ds(off, tk), :]
            for h in heads:
                z_buf[slot, h] = _dot_nt(qs[h], kj)
                dw_buf[slot, h] = _dot_nt(dos[h], vj)

        def gradients(blk, slot):
            keys = pl.ds(pl.multiple_of(blk * tk, tk), tk)
            for h in heads:
                dims = slice(h * HEAD_DIM, (h + 1) * HEAD_DIM)
                queries = slice(h * tq, (h + 1) * tq)
                dq_t[dims, :] += _dot_nt(kt_ref[dims, keys], dz_buf[slot, queries, :])
                dk_t[blk, dims, :] += _dot(qnt_buf[dims, :], dz_buf[slot, queries, :])
                dv_t[blk, dims, :] += _dot(dot_buf[dims, :], w_buf[slot, queries, :])

        def softplus_stage(h, slot, diag):
            for c in range(tq // rc):
                rows = slice(c * rc, (c + 1) * rc)
                if _all_masked(c, diag, rc, tk):
                    hl_buf[h, rows, :] = jnp.zeros((rc, LOG_SUM_PASSES * tk), BF16)
                    continue
                nz = z_buf[slot, h, rows, :]
                l1 = jnp.minimum(nz, 0.0) - jnp.log(1.0 + jnp.exp(_neg_abs(nz)))
                if _some_masked(c, diag, rc, tk):
                    l1 = jnp.where(causal(c, diag), l1, 0.0)
                for s, part in enumerate(_split(l1)[:LOG_SUM_PASSES]):
                    hl_buf[h, rows, s * tk:(s + 1) * tk] = part
                ls_buf[h, rows, :] = l1 - nz
                totl_buf[h, rows, :] = _row_sums(l1)

        def weights_stage(h, slot, diag):
            for c in range(tq // rc):
                rows = slice(c * rc, (c + 1) * rc)
                stacked = slice(h * tq + c * rc, h * tq + (c + 1) * rc)
                if _all_masked(c, diag, rc, tk):
                    w_buf[slot, stacked, :] = jnp.zeros((rc, tk), BF16)
                    gb_buf[h, rows, :] = jnp.zeros((rc, tk), BF16)
                    continue
                wgt = jnp.exp(ls_buf[h, rows, :] + (_across(rem_buf[h, rows, :], tk) - upto_buf[h, rows, :]))
                if _some_masked(c, diag, rc, tk):
                    wgt = jnp.where(causal(c, diag), wgt, 0.0)
                w_buf[slot, stacked, :] = wgt.astype(BF16)
                g = wgt * dw_buf[slot, h, rows, :]
                g_buf[h, rows, :] = g
                gb_buf[h, rows, :] = g.astype(BF16)
                totg_buf[h, rows, :] = _row_sums(g)
                rem_buf[h, rows, :] -= totl_buf[h, rows, :]

        def dscore_stage(h, slot, diag):
            for c in range(tq // rc):
                rows = slice(c * rc, (c + 1) * rc)
                stacked = slice(h * tq + c * rc, h * tq + (c + 1) * rc)
                if _all_masked(c, diag, rc, tk):
                    dz_buf[slot, stacked, :] = jnp.zeros((rc, tk), BF16)
                    continue
                sig = jnp.exp(ls_buf[h, rows, :])
                g = g_buf[h, rows, :]
                dnz = sig * ((before_buf[h, rows, :] + _across(preg_buf[h, rows, :], tk)) + g) - g
                if _some_masked(c, diag, rc, tk):
                    dnz = jnp.where(causal(c, diag), dnz, 0.0)
                dz_buf[slot, stacked, :] = dnz.astype(BF16)
                preg_buf[h, rows, :] += totg_buf[h, rows, :]

        def position(blk, slot, diag, prefetch):
            if prefetch:
                scores(blk + 1, 1 - slot)
            for h in heads:
                softplus_stage(h, slot, diag)
                upto_buf[h] = _dot(hl_buf[h], up_ref[...])
            gradients(jnp.maximum(blk - 1, 0), 1 - slot)
            for h in heads:
                weights_stage(h, slot, diag)
                before_buf[h] = _dot(gb_buf[h], bf_ref[...])
            for h in heads:
                dscore_stage(h, slot, diag)

        scores(0, 0)

        def trip(jj, carry):
            for u in range(2):
                position(2 * jj + u, u, None, True)
            return carry

        lax.fori_loop(0, (i * ndiag) // 2, trip, 0)
        for d in range(ndiag):
            position(i * ndiag + d, d % 2, d, d < ndiag - 1)
        gradients(nblk - 1, 1)
        dq_ref[...] = (dq_t[...].T * NEG_QK_SCALE).astype(BF16)

        @pl.when(i == nq - 1)
        def _():
            for blk in range(nkb):
                dk_ref[blk * tk:(blk + 1) * tk, :] = dk_t[blk].T.astype(BF16)
                dv_ref[blk * tk:(blk + 1) * tk, :] = dv_t[blk].T.astype(BF16)

    qmap = lambda b, hp, i: (b * nq + i, hp)
    kmap = lambda b, hp, i: (b, hp)
    const = lambda b, hp, i: (0, 0)
    return pl.pallas_call(
        body, name="attn_bwd", grid=(nb, w // LANES, nq),
        in_specs=[pl.BlockSpec((tq, LANES), qmap), pl.BlockSpec((seq, LANES), kmap),
                  pl.BlockSpec((LANES, seq), lambda b, hp, i: (hp, b)), pl.BlockSpec((seq, LANES), kmap),
                  pl.BlockSpec((tq, LANES), qmap), pl.BlockSpec((tq, LANES), qmap),
                  pl.BlockSpec((LOG_SUM_PASSES * tk, tk), const), pl.BlockSpec((tk, tk), const)],
        out_specs=[pl.BlockSpec((tq, LANES), qmap), pl.BlockSpec((seq, LANES), kmap), pl.BlockSpec((seq, LANES), kmap)],
        out_shape=[jax.ShapeDtypeStruct((t_all, w), BF16)] * 3,
        scratch_shapes=[pltpu.VMEM((2, nh, tq, tk), F32), pltpu.VMEM((2, nh, tq, tk), F32),
                        pltpu.VMEM((nh, tq, tk), F32), pltpu.VMEM((nh, tq, LOG_SUM_PASSES * tk), BF16),
                        pltpu.VMEM((nh, tq, tk), F32), pltpu.VMEM((nh, tq, tk), F32),
                        pltpu.VMEM((nh, tq, tk), BF16), pltpu.VMEM((nh, tq, tk), F32),
                        pltpu.VMEM((2, nh * tq, tk), BF16), pltpu.VMEM((2, nh * tq, tk), BF16),
                        pltpu.VMEM((nh, tq, LANES), F32), pltpu.VMEM((nh, tq, LANES), F32),
                        pltpu.VMEM((nh, tq, LANES), F32), pltpu.VMEM((nh, tq, LANES), F32),
                        pltpu.VMEM((LANES, tq), BF16), pltpu.VMEM((LANES, tq), BF16),
                        pltpu.VMEM((LANES, tq), F32), pltpu.VMEM((nkb, LANES, tk), F32),
                        pltpu.VMEM((nkb, LANES, tk), F32)],
        compiler_params=_params(("arbitrary", "arbitrary", "arbitrary")),
    )(qn, k, kt, v, do, ltot, _tri_matrix(tk, "upto")[:LOG_SUM_PASSES * tk], _tri_matrix(tk, "before")[:tk])


def _inproj_bwd(dq, dk, dv, dpd, x, dx1, mod, g_pre, w_in, seq, tm):
    t_all, d = x.shape
    nt = seq // tm
    p = dq.shape[1]

    def body(dq_ref, dk_ref, dv_ref, dpd_ref, halo_ref, x_ref, dx1_ref, mod_ref, g_ref, w_ref,
             gx_ref, du_ref, accb_ref, accg_ref):
        i = pl.program_id(0)
        it = i % nt

        @pl.when(i == 0)
        def _():
            accg_ref[...] = jnp.zeros_like(accg_ref)

        @pl.when(it == 0)
        def _():
            accb_ref[...] = jnp.zeros_like(accb_ref)

        dpd = dpd_ref[...]
        pos = it * tm + lax.broadcasted_iota(jnp.int32, (tm, 1), 0)
        cnts = _window_counts(pos)
        halo = jnp.where(it == nt - 1, 0.0, halo_ref[...])
        scaled = []
        halos = []
        for g, win in enumerate(POOL_WINDOWS):
            cols = slice(g * POOL_GROUP, (g + 1) * POOL_GROUP)
            scaled.append(dpd[:, cols] / cnts[g])
            halos.append(halo[:, cols] / float(win))
        ext = jnp.concatenate([jnp.concatenate(scaled, axis=1), jnp.concatenate(halos, axis=1)], axis=0)
        sums = _window_sums(ext, tm, 0, True)
        du = (jnp.concatenate(sums, axis=1) - dpd).astype(BF16)
        du_ref[...] = du
        g1 = g_ref[...]
        scale1 = 1.0 + mod_ref[0, 1:2, :]
        for c in range(ROW_CHUNKS):
            rows = slice(c * (tm // ROW_CHUNKS), (c + 1) * (tm // ROW_CHUNKS))
            dh1 = (_dot_nt(dq_ref[rows, :], w_ref[0]) + _dot_nt(dk_ref[rows, :], w_ref[1])
                   + _dot_nt(dv_ref[rows, :], w_ref[2]) + _dot_nt(du_ref[rows, :], w_ref[3]))
            xf = x_ref[rows, :]
            r1 = _rms(xf)
            n1 = xf * r1
            accb_ref[0, 0:1, :] += _colsum(dh1)
            accb_ref[0, 1:2, :] += _colsum(dh1 * (n1 * g1))
            accg_ref[0:1, :] += _colsum((dh1 * scale1) * n1)
            gx_ref[rows, :] = dx1_ref[rows, :] + _norm_bwd((dh1 * scale1) * g1, n1, r1)

    tok = lambda i: (i, 0)
    const2 = lambda i: (0, 0)
    hb = tm // HALO
    last = t_all // HALO - 1
    return pl.pallas_call(
        body, name="inproj_bwd", grid=(t_all // tm,),
        in_specs=[pl.BlockSpec((tm, p), tok), pl.BlockSpec((tm, p), tok), pl.BlockSpec((tm, p), tok),
                  pl.BlockSpec((tm, p), tok),
                  pl.BlockSpec((HALO, p), lambda i: (jnp.minimum((i + 1) * hb, last), 0)),
                  pl.BlockSpec((tm, d), tok), pl.BlockSpec((tm, d), tok),
                  pl.BlockSpec((1, MOD_ROWS, d), lambda i: (i // nt, 0, 0)),
                  pl.BlockSpec((1, d), const2),
                  pl.BlockSpec((N_CHIPS, d, p), lambda i: (0, 0, 0))],
        out_specs=[pl.BlockSpec((tm, d), tok), pl.BlockSpec((tm, p), tok),
                   pl.BlockSpec((1, 8, d), lambda i: (i // nt, 0, 0)),
                   pl.BlockSpec((8, d), const2)],
        out_shape=[jax.ShapeDtypeStruct((t_all, d), F32), jax.ShapeDtypeStruct((t_all, p), BF16),
                   jax.ShapeDtypeStruct((t_all // seq, 8, d), F32),
                   jax.ShapeDtypeStruct((8, d), F32)],
        compiler_params=_params(("arbitrary",)),
    )(dq, dk, dv, dpd, dpd, x, dx1, mod, g_pre, w_in)


def _tn_matmul(x, ys, nk, bt, name):
    t_all = x.shape[-2]
    m = x.shape[-1]
    ny = len(ys)
    nt = t_all // bt

    def spec(arr):
        if arr.ndim == 3:
            return pl.BlockSpec((1, bt, arr.shape[-1]), lambda k, t: (k, t, 0))
        return pl.BlockSpec((bt, arr.shape[-1]), lambda k, t: (t, 0))

    def tile(ref):
        return ref[0] if len(ref.shape) == 3 else ref[...]

    def body(*refs):
        x_ref, y_refs, o_refs, h_refs = refs[0], refs[1:1 + ny], refs[1 + ny:1 + 2 * ny], refs[1 + 2 * ny:]
        t = pl.program_id(1)
        xt = tile(x_ref)
        for y_ref, o_ref, h_ref in zip(y_refs, o_refs, h_refs):
            part = _dot_tn(xt, tile(y_ref))

            @pl.when(t == 0)
            def _(o_ref=o_ref, part=part):
                o_ref[0] = part

            @pl.when(t > 0)
            def _(o_ref=o_ref, part=part):
                o_ref[0] += part

            @pl.when(t == nt - 1)
            def _(o_ref=o_ref, h_ref=h_ref):
                h_ref[0] = o_ref[0].astype(BF16)

    out_specs = [pl.BlockSpec((1, m, y.shape[-1]), lambda k, t: (k, 0, 0)) for y in ys]
    out = pl.pallas_call(
        body, name=name, grid=(nk, nt),
        in_specs=[spec(x)] + [spec(y) for y in ys],
        out_specs=out_specs * 2,
        out_shape=[jax.ShapeDtypeStruct((nk, m, y.shape[-1]), dt) for dt in (F32, BF16) for y in ys],
        compiler_params=_params(("arbitrary", "arbitrary")),
    )(x, *ys)
    return out[:ny], out[ny:]


def _cond_fwd(c_all, w_q, b_q, bn):
    nrow, d = c_all.shape
    ncol = w_q.shape[1]

    def body(c_ref, w_ref, b_ref, sc_ref, mod_ref):
        cf = c_ref[...]
        sc = cf * _sigmoid(cf)
        sc_ref[...] = sc
        shi, slo = _split(sc)
        whi, wlo = _split(w_ref[...])
        mod_ref[...] = (_dot(shi, whi) + _dot(shi, wlo) + _dot(slo, whi)) + b_ref[...]

    return pl.pallas_call(
        body, name="cond_fwd", grid=(ncol // bn,),
        in_specs=[pl.BlockSpec((nrow, d), lambda n: (0, 0)), pl.BlockSpec((d, bn), lambda n: (0, n)),
                  pl.BlockSpec((1, bn), lambda n: (0, n))],
        out_specs=[pl.BlockSpec((nrow, d), lambda n: (0, 0)), pl.BlockSpec((nrow, bn), lambda n: (0, n))],
        out_shape=[jax.ShapeDtypeStruct((nrow, d), F32), jax.ShapeDtypeStruct((nrow, ncol), F32)],
        compiler_params=_params(("arbitrary",)),
    )(c_all, w_q, b_q)


def _cond_bwd(sc_all, dmod_q, bn):
    nrow, d = sc_all.shape
    ncol = dmod_q.shape[1]

    def body(sc_ref, dm_ref, gw_ref):
        shi, slo = _split(sc_ref[...])
        dhi, dlo = _split(dm_ref[...])
        gw_ref[...] = _dot_tn(shi, dhi) + _dot_tn(shi, dlo) + _dot_tn(slo, dhi)

    return pl.pallas_call(
        body, name="cond_bwd", grid=(ncol // bn,),
        in_specs=[pl.BlockSpec((nrow, d), lambda n: (0, 0)), pl.BlockSpec((nrow, bn), lambda n: (0, n))],
        out_specs=pl.BlockSpec((d, bn), lambda n: (0, n)),
        out_shape=jax.ShapeDtypeStruct((d, ncol), F32),
        compiler_params=_params(("arbitrary",)),
    )(sc_all, dmod_q)


def _row_block(rows, cols, budget=1 << 18):
    best = None
    for br in range(8, rows + 1, 8):
        if rows % br == 0 and br * cols <= budget:
            best = br
    return best if best is not None else rows


def _adam_math(w, g, m, v):
    c1 = 1.0 - ADAM_B1 ** ADAM_STEP
    c2 = 1.0 - ADAM_B2 ** ADAM_STEP
    m2 = ADAM_B1 * m + (1.0 - ADAM_B1) * g
    v2 = ADAM_B2 * v + (1.0 - ADAM_B2) * (g * g)
    return -ADAM_LR * ((m2 / c1) / (jnp.sqrt(v2 / c2) + ADAM_EPS) + ADAM_WD * w), m2, v2


def _small_updates(summed, params):
    n = len(params)

    def body(s_ref, *refs):
        ins, outs = refs[:3 * n], refs[3 * n:]
        for p, (_, _, _, pick) in enumerate(params):
            w_ref, m_ref, v_ref = ins[3 * p:3 * p + 3]
            g = pick(s_ref)
            delta, m2, v2 = _adam_math(w_ref[...], g, m_ref[...], v_ref[...])
            for o_ref, val in zip(outs[4 * p:4 * p + 4], (g, delta, m2, v2)):
                o_ref[...] = val

    out = pl.pallas_call(
        body, name="adamw_small",
        out_shape=[jax.ShapeDtypeStruct(w.shape, F32) for w, _, _, _ in params for _ in range(4)],
        compiler_params=pltpu.CompilerParams(vmem_limit_bytes=VMEM_LIMIT),
    )(summed, *[t for w, m, v, _ in params for t in (w, m, v)])
    return [tuple(out[4 * p:4 * p + 4]) for p in range(n)]


def _adamw(w, g, m, v, name):
    rows, cols = w.shape
    br = _row_block(rows, cols)

    def body(w_ref, g_ref, m_ref, v_ref, d_ref, nm_ref, nv_ref):
        d_ref[...], nm_ref[...], nv_ref[...] = _adam_math(w_ref[...], g_ref[...], m_ref[...], v_ref[...])

    blk = pl.BlockSpec((br, cols), lambda i: (i, 0))
    return pl.pallas_call(
        body, name=name, grid=(rows // br,),
        in_specs=[blk] * 4, out_specs=[blk] * 3,
        out_shape=[jax.ShapeDtypeStruct((rows, cols), F32)] * 3,
        compiler_params=_params(("arbitrary",)),
    )(w, g, m, v)


def _all_gather(x_shard, name):
    m_per, n = x_shard.shape

    def body(x_ref, out_ref, send_sems, recv_sems, local_sem):
        x, y, c = _position()
        me, sibling = (x, y, c), (x, y, 1 - c)
        chips = [(1 - x, y), (x, 1 - y), (1 - x, 1 - y)]

        def rows(px, py, pc):
            return out_ref.at[pl.ds((4 * px + 2 * py + pc) * m_per, m_per), :]

        def copy(k, block, to, src=None):
            return pltpu.make_async_remote_copy(
                src_ref=rows(*block) if src is None else src, dst_ref=rows(*block),
                send_sem=send_sems.at[k], recv_sem=recv_sems.at[k], device_id=to, device_id_type=MESH)

        mine = pltpu.make_async_copy(x_ref, rows(*me), local_sem)
        mine.start()
        first = [copy(0, me, sibling, src=x_ref)]
        first += [copy(1 + j, me, (*chip, c), src=x_ref) for j, chip in enumerate(chips)]
        for cp in first:
            cp.start()
        passed = [copy(4 + j, (*chip, c), sibling) for j, chip in enumerate(chips)]
        for j, chip in enumerate(chips):
            copy(1 + j, (*chip, c), me).wait_recv()
            passed[j].start()
        copy(0, sibling, me).wait_recv()
        for j, chip in enumerate(chips):
            copy(4 + j, (*chip, 1 - c), me).wait_recv()
        for cp in first + passed:
            cp.wait_send()
        mine.wait()

    return pl.pallas_call(
        body, name=name,
        out_shape=jax.ShapeDtypeStruct((N_DEV * m_per, n), x_shard.dtype),
        in_specs=[pl.BlockSpec(memory_space=pltpu.VMEM)],
        out_specs=pl.BlockSpec(memory_space=pltpu.VMEM),
        scratch_shapes=[pltpu.SemaphoreType.DMA((7,)), pltpu.SemaphoreType.DMA((7,)), pltpu.SemaphoreType.DMA],
        compiler_params=pltpu.CompilerParams(vmem_limit_bytes=VMEM_LIMIT),
    )(x_shard)


_ANY = pl.BlockSpec(memory_space=pl.ANY)


def _place_quarters(place, quarters):
    steps = 2

    def body(place_ref, *refs):
        n = len(refs) // 2
        for w_ref, o_ref in zip(refs[:n], refs[n:]):
            o_ref[0] = w_ref[...].astype(BF16)

    return pl.pallas_call(
        body, name="place_quarters",
        grid_spec=pltpu.PrefetchScalarGridSpec(
            num_scalar_prefetch=1, grid=(steps,),
            in_specs=[pl.BlockSpec((q.shape[0] // steps, q.shape[1]), lambda r, place_ref: (r, 0)) for q in quarters],
            out_specs=[pl.BlockSpec((1, q.shape[0] // steps, q.shape[1]), lambda r, place_ref: (place_ref[0], r, 0))
                       for q in quarters]),
        out_shape=[jax.ShapeDtypeStruct((N_CHIPS,) + q.shape, BF16) for q in quarters],
        compiler_params=_params(("arbitrary",)),
    )(place, *quarters)


def _gather_weights(placed):
    n = len(placed)
    shapes = [b.shape[1:] for b in placed]

    def body(*refs):
        g_refs = refs[n:2 * n]
        send_sems, recv_sems = refs[2 * n:]
        x, y, c = _position()
        sibling = (x, y, 1 - c)
        chips = [(1 - x, y), (x, 1 - y), (1 - x, 1 - y)]
        mine = 2 * x + y

        def half(a, which):
            hr = shapes[a][0] // 2
            return pl.ds(which * hr, hr)

        def over_ici(a, p, slot):
            ref = g_refs[a].at[slot, half(a, c), :]
            return pltpu.make_async_remote_copy(
                src_ref=ref, dst_ref=ref,
                send_sem=send_sems.at[6 * a + p], recv_sem=recv_sems.at[6 * a + p],
                device_id=(*chips[p], c), device_id_type=MESH)

        def over_d2d(a, p, slot, which):
            ref = g_refs[a].at[slot, half(a, which), :]
            return pltpu.make_async_remote_copy(
                src_ref=ref, dst_ref=ref,
                send_sem=send_sems.at[6 * a + 3 + p], recv_sem=recv_sems.at[6 * a + 3 + p],
                device_id=sibling, device_id_type=MESH)

        sends = []
        for a in range(n):
            for p in range(3):
                cp = over_ici(a, p, mine)
                cp.start()
                sends.append(cp)
        for a in range(n):
            for p, (cx, cy) in enumerate(chips):
                slot = 2 * cx + cy
                over_ici(a, p, slot).wait_recv()
                cp = over_d2d(a, p, slot, c)
                cp.start()
                sends.append(cp)
        for a in range(n):
            for p, (cx, cy) in enumerate(chips):
                over_d2d(a, p, 2 * cx + cy, 1 - c).wait_recv()
        for cp in sends:
            cp.wait_send()

    return pl.pallas_call(
        body, name="gather_weights",
        out_shape=[jax.ShapeDtypeStruct(b.shape, BF16) for b in placed],
        in_specs=[_ANY] * n, out_specs=[_ANY] * n,
        input_output_aliases={a: a for a in range(n)},
        scratch_shapes=[pltpu.SemaphoreType.DMA((6 * n,)), pltpu.SemaphoreType.DMA((6 * n,))],
    )(*placed)


_HBM = pl.BlockSpec(memory_space=pltpu.HBM)
_SEM = pl.BlockSpec(memory_space=pltpu.SEMAPHORE)
_EFFECT = pltpu.SideEffectType.DATAFLOW_SIDE_EFFECTING


def _quarter_halves(shapes, a, which):
    hr = shapes[a][0] // 2
    return pl.ds(which * hr, hr)


def _gather_start(placed, after):
    n = len(placed)
    m = len(after)
    shapes = [b.shape[1:] for b in placed]

    def body(*refs):
        g_refs = refs[:n]
        send_sems, recv_sems = refs[n + m], refs[n + m + 1]
        token = refs[2 * n + m + 2]
        x, y, c = _position()
        chips = [(1 - x, y), (x, 1 - y), (1 - x, 1 - y)]
        mine = 2 * x + y
        for a in range(n):
            ref = g_refs[a].at[mine, _quarter_halves(shapes, a, c), :]
            for p in range(3):
                pltpu.make_async_remote_copy(
                    src_ref=ref, dst_ref=ref, send_sem=send_sems.at[3 * a + p], recv_sem=recv_sems.at[3 * a + p],
                    device_id=(*chips[p], c), device_id_type=MESH).start()
        token[...] = jnp.zeros_like(token)

    out = pl.pallas_call(
        body, name="gather_start",
        out_shape=(pltpu.SemaphoreType.DMA((3 * n,)), pltpu.SemaphoreType.DMA((3 * n,)),
                   *[pltpu.HBM(b.shape, b.dtype) for b in placed], jax.ShapeDtypeStruct((8, LANES), F32)),
        in_specs=[_HBM] * n + [_ANY] * m,
        out_specs=(_SEM, _SEM, *[_HBM] * n, pl.BlockSpec(memory_space=pltpu.VMEM)),
        input_output_aliases={a: 2 + a for a in range(n)},
        compiler_params=pltpu.CompilerParams(has_side_effects=_EFFECT),
    )(*[pltpu.with_memory_space_constraint(b, pltpu.HBM) for b in placed], *after)
    return out[0], out[1], list(out[2:2 + n]), out[2 + n]


def _gather_wait(send_sems, recv_sems, thru, after):
    n = len(thru)
    shapes = [b.shape[1:] for b in thru]

    def body(*refs):
        g_refs = refs[:n]
        send_sems, recv_sems = refs[n], refs[n + 1]
        x, y, c = _position()
        chips = [(1 - x, y), (x, 1 - y), (1 - x, 1 - y)]
        mine = 2 * x + y
        for a in range(n):
            rows = _quarter_halves(shapes, a, c)
            for p, (cx, cy) in enumerate(chips):
                copy = pltpu.make_async_remote_copy(
                    src_ref=g_refs[a].at[mine, rows, :], dst_ref=g_refs[a].at[2 * cx + cy, rows, :],
                    send_sem=send_sems.at[3 * a + p], recv_sem=recv_sems.at[3 * a + p],
                    device_id=(cx, cy, c), device_id_type=MESH)
                copy.wait_send()
                copy.wait_recv()

    return pl.pallas_call(
        body, name="gather_wait",
        out_shape=[pltpu.HBM(b.shape, b.dtype) for b in thru],
        in_specs=[_HBM] * n + [_SEM, _SEM, _ANY], out_specs=[_HBM] * n,
        input_output_aliases={a: a for a in range(n)},
        compiler_params=pltpu.CompilerParams(has_side_effects=_EFFECT),
    )(*thru, send_sems, recv_sems, after)


def _gather_forward(bufs):
    n = len(bufs)
    shapes = [b.shape[1:] for b in bufs]

    def body(*refs):
        g_refs = refs[n:2 * n]
        send_sems, recv_sems = refs[2 * n:]
        x, y, c = _position()
        chips = [(1 - x, y), (x, 1 - y), (1 - x, 1 - y)]

        def over_d2d(a, p, which):
            cx, cy = chips[p]
            ref = g_refs[a].at[2 * cx + cy, _quarter_halves(shapes, a, which), :]
            return pltpu.make_async_remote_copy(
                src_ref=ref, dst_ref=ref, send_sem=send_sems.at[3 * a + p], recv_sem=recv_sems.at[3 * a + p],
                device_id=(x, y, 1 - c), device_id_type=MESH)

        sends = [over_d2d(a, p, c) for a in range(n) for p in range(3)]
        for cp in sends:
            cp.start()
        for a in range(n):
            for p in range(3):
                over_d2d(a, p, 1 - c).wait_recv()
        for cp in sends:
            cp.wait_send()

    return pl.pallas_call(
        body, name="gather_forward",
        out_shape=[jax.ShapeDtypeStruct(b.shape, BF16) for b in bufs],
        in_specs=[_ANY] * n, out_specs=[_ANY] * n,
        input_output_aliases={a: a for a in range(n)},
        scratch_shapes=[pltpu.SemaphoreType.DMA((3 * n,)), pltpu.SemaphoreType.DMA((3 * n,))],
    )(*bufs)


def _sibling_exchange(grads, tag):
    n = len(grads)
    shapes = [g.shape for g in grads]

    def body(*refs):
        g_refs, x_refs = refs[:n], refs[n:2 * n]
        send_sems, recv_sems = refs[2 * n:]
        x, y, c = _position()
        copies = []
        for a in range(n):
            hr = shapes[a][1] // 2
            cp = pltpu.make_async_remote_copy(
                src_ref=g_refs[a].at[:, pl.ds((1 - c) * hr, hr), :], dst_ref=x_refs[a],
                send_sem=send_sems.at[a], recv_sem=recv_sems.at[a],
                device_id=(x, y, 1 - c), device_id_type=MESH)
            cp.start()
            copies.append(cp)
        for cp in copies:
            cp.wait()

    return pl.pallas_call(
        body, name="grad_sibling_exchange_" + tag,
        out_shape=[jax.ShapeDtypeStruct((g.shape[0], g.shape[1] // 2, g.shape[2]), g.dtype) for g in grads],
        in_specs=[_ANY] * n, out_specs=[_ANY] * n,
        scratch_shapes=[pltpu.SemaphoreType.DMA((n,)), pltpu.SemaphoreType.DMA((n,))],
    )(*grads)


def _chip_sums(core, grads, theirs, tag):
    n = len(grads)

    def body(core_ref, *refs):
        g_refs, t_refs, o_refs = refs[:n], refs[n:2 * n], refs[2 * n:]
        for g_ref, t_ref, o_ref in zip(g_refs, t_refs, o_refs):
            o_ref[...] = (g_ref[...] + t_ref[...].astype(F32)).astype(BF16)

    in_specs = [pl.BlockSpec((1, g.shape[1] // 2, g.shape[2]), lambda k, core_ref: (k, core_ref[0], 0)) for g in grads]
    in_specs += [pl.BlockSpec((1,) + t.shape[1:], lambda k, core_ref: (k, 0, 0)) for t in theirs]
    return pl.pallas_call(
        body, name="grad_chip_sums_" + tag,
        grid_spec=pltpu.PrefetchScalarGridSpec(
            num_scalar_prefetch=1, grid=(N_CHIPS,), in_specs=in_specs,
            out_specs=[pl.BlockSpec((1,) + t.shape[1:], lambda k, core_ref: (k, 0, 0)) for t in theirs]),
        out_shape=[jax.ShapeDtypeStruct(t.shape, BF16) for t in theirs],
        compiler_params=_params(("arbitrary",)),
    )(core, *grads, *theirs)


def _chip_exchange_start(sums, after, tag):
    n = len(sums)
    m = len(after)
    lands = [lax.empty((3,) + s.shape[1:], BF16) for s in sums]

    def body(*refs):
        s_refs, y_refs = refs[:n], refs[n:2 * n]
        send_sems, recv_sems = refs[2 * n + m], refs[2 * n + m + 1]
        token = refs[4 * n + m + 2]
        x, y, c = _position()
        chips = [(1 - x, y), (x, 1 - y), (1 - x, 1 - y)]
        for a in range(n):
            for p, (cx, cy) in enumerate(chips):
                pltpu.make_async_remote_copy(
                    src_ref=s_refs[a].at[2 * cx + cy], dst_ref=y_refs[a].at[p],
                    send_sem=send_sems.at[3 * a + p], recv_sem=recv_sems.at[3 * a + p],
                    device_id=(cx, cy, c), device_id_type=MESH).start()
        token[...] = jnp.zeros_like(token)

    both = list(sums) + lands
    out = pl.pallas_call(
        body, name="grad_chip_exchange_start_" + tag,
        out_shape=(pltpu.SemaphoreType.DMA((3 * n,)), pltpu.SemaphoreType.DMA((3 * n,)),
                   *[pltpu.HBM(b.shape, b.dtype) for b in both], jax.ShapeDtypeStruct((8, LANES), F32)),
        in_specs=[_HBM] * (2 * n) + [_ANY] * m,
        out_specs=(_SEM, _SEM, *[_HBM] * (2 * n), pl.BlockSpec(memory_space=pltpu.VMEM)),
        input_output_aliases={a: 2 + a for a in range(2 * n)},
        compiler_params=pltpu.CompilerParams(has_side_effects=_EFFECT),
    )(*[pltpu.with_memory_space_constraint(b, pltpu.HBM) for b in both], *after)
    return out[0], out[1], list(out[2:2 + n]), list(out[2 + n:2 + 2 * n]), out[2 + 2 * n]


def _chip_exchange_wait(send_sems, recv_sems, sums, lands, after, tag):
    n = len(sums)

    def body(*refs):
        s_refs, y_refs = refs[:n], refs[n:2 * n]
        send_sems, recv_sems = refs[2 * n], refs[2 * n + 1]
        x, y, c = _position()
        chips = [(1 - x, y), (x, 1 - y), (1 - x, 1 - y)]
        for a in range(n):
            for p, (cx, cy) in enumerate(chips):
                copy = pltpu.make_async_remote_copy(
                    src_ref=s_refs[a].at[2 * cx + cy], dst_ref=y_refs[a].at[p],
                    send_sem=send_sems.at[3 * a + p], recv_sem=recv_sems.at[3 * a + p],
                    device_id=(cx, cy, c), device_id_type=MESH)
                copy.wait_send()
                copy.wait_recv()

    both = list(sums) + list(lands)
    out = pl.pallas_call(
        body, name="grad_chip_exchange_wait_" + tag,
        out_shape=[pltpu.HBM(b.shape, b.dtype) for b in both],
        in_specs=[_HBM] * (2 * n) + [_SEM, _SEM, _ANY], out_specs=[_HBM] * (2 * n),
        input_output_aliases={a: a for a in range(2 * n)},
        compiler_params=pltpu.CompilerParams(has_side_effects=_EFFECT),
    )(*both, send_sems, recv_sems, after)
    return list(out[:n]), list(out[n:])


def _total_sums(place, sums, parts, after, tag):
    n = len(parts)
    m = len(after)
    steps = 2

    def body(place_ref, *refs):
        for s_ref, y_ref, o_ref in zip(refs[:n], refs[n:2 * n], refs[2 * n + m:]):
            o_ref[0] = ((s_ref[0].astype(F32) + y_ref[0].astype(F32)) + y_ref[1].astype(F32)) + y_ref[2].astype(F32)

    def step_rows(pt):
        return pt.shape[1] // steps

    in_specs = [pl.BlockSpec((1, step_rows(s), s.shape[2]), lambda r, place_ref: (place_ref[0], r, 0)) for s in sums]
    in_specs += [pl.BlockSpec((3, step_rows(pt), pt.shape[2]), lambda r, place_ref: (0, r, 0)) for pt in parts]
    in_specs += [_ANY] * m
    return pl.pallas_call(
        body, name="grad_total_sums_" + tag,
        grid_spec=pltpu.PrefetchScalarGridSpec(
            num_scalar_prefetch=1, grid=(steps,), in_specs=in_specs,
            out_specs=[pl.BlockSpec((1, step_rows(pt), pt.shape[2]), lambda r, place_ref: (place_ref[1], r, 0))
                       for pt in parts]),
        out_shape=[jax.ShapeDtypeStruct((2,) + pt.shape[1:], F32) for pt in parts],
        compiler_params=_params(("arbitrary",)),
    )(place, *sums, *parts, *after)


def _sibling_share(halves, tag):
    n = len(halves)

    def body(*refs):
        f_refs = refs[n:2 * n]
        send_sems, recv_sems = refs[2 * n:]
        x, y, c = _position()
        copies = []
        for a in range(n):
            cp = pltpu.make_async_remote_copy(
                src_ref=f_refs[a].at[c], dst_ref=f_refs[a].at[c], send_sem=send_sems.at[a], recv_sem=recv_sems.at[a],
                device_id=(x, y, 1 - c), device_id_type=MESH)
            cp.start()
            copies.append(cp)
        for a, cp in enumerate(copies):
            cp.wait_send()
            pltpu.make_async_remote_copy(
                src_ref=f_refs[a].at[1 - c], dst_ref=f_refs[a].at[1 - c], send_sem=send_sems.at[a],
                recv_sem=recv_sems.at[a], device_id=(x, y, c), device_id_type=MESH).wait_recv()

    return pl.pallas_call(
        body, name="grad_sibling_share_" + tag,
        out_shape=[jax.ShapeDtypeStruct(h.shape, F32) for h in halves],
        in_specs=[_ANY] * n, out_specs=[_ANY] * n,
        input_output_aliases={a: a for a in range(n)},
        scratch_shapes=[pltpu.SemaphoreType.DMA((n,)), pltpu.SemaphoreType.DMA((n,))],
    )(*halves)


def _group_sum(stacked, nrow, name):
    total, n = stacked.shape
    groups = total // nrow

    def body(g_ref, o_ref):
        acc = g_ref[0:nrow, :]
        for grp in range(1, groups):
            acc = acc + g_ref[grp * nrow:(grp + 1) * nrow, :]
        o_ref[...] = acc

    return pl.pallas_call(
        body, name=name,
        out_shape=jax.ShapeDtypeStruct((nrow, n), F32),
        compiler_params=pltpu.CompilerParams(vmem_limit_bytes=VMEM_LIMIT),
    )(stacked)


def _local_step(xt, tgt, mod, gains, w_pool, pool_scale, w_in, later_weights, on_ffn_grads, seq):
    g_mpre, g_mpost, g_fpre, g_fpost = gains
    d = xt.shape[1]
    tm, tq = min(TOKEN_TILE, seq), min(ATTN_TILE, seq)

    h1, qn, k, v, u, kt, vt = _prenorm_proj(xt, mod, g_mpre, w_in, seq, tm)
    tk = min(ATTN_KEY_TILE, tq // 2)
    o, ltot = _attn_fwd(qn, k, vt, seq, tq, tk)
    w_out, w_g, w_u, w_d = later_weights(o)
    w_out2 = w_out.reshape(d, d)
    pooled, mixin, mix, x1, h2 =_mixer_post(u, o, xt, mod, g_mpost, g_fpre, w_pool, pool_scale, w_out2, seq, tm)
    a, b, fin, dy, df, loss_blk, accb4, accg4 = _ffn_fwd(h2, w_g, w_u, w_d, x1, tgt, mod, g_fpost, seq, tm)
    da, db, dx1, dmix, accb5, accg5 = _ffn_bwd(df, a, b, w_d, w_g, w_u, x1, dy, mix, mod, g_fpre, g_mpost, seq, tm)
    bt = min(GRAD_TOKEN_TILE, xt.shape[0])
    bt_one = min(2 * GRAD_TOKEN_TILE, xt.shape[0])
    (g_g,), (g_g16,) = _tn_matmul(da, [h2], w_g.shape[0], bt_one, "grad_w_gate")
    (g_u,), (g_u16,) = _tn_matmul(db, [h2], w_u.shape[0], bt_one, "grad_w_up")
    (g_d,), (g_d16,) = _tn_matmul(fin, [df], w_d.shape[0], bt_one, "grad_w_down")
    token = on_ffn_grads([g_g, g_u, g_d], [g_g16, g_u16, g_d16])
    do, dpd, dps, dwp = _mixer_bwd(dmix, w_out2, pooled, w_pool, pool_scale + token, seq, tm)
    dq, dk, dv = _attn_bwd(qn, k, kt, v, do, ltot, seq, tq, tk)
    gx, du, accb8, accg8 = _inproj_bwd(dq, dk, dv, dpd, xt, dx1, mod, g_mpre, w_in, seq, tm)

    g_in, g_in16 = [jnp.concatenate(parts, axis=0) for parts in _tn_matmul(h1, [dq, dk, dv, du], 1, bt, "grad_w_in")]
    g_out, g_out16 = [parts[0].reshape(w_out.shape) for parts in _tn_matmul(mixin, [dmix], 1, bt_one, "grad_w_out")]

    dmod = jnp.stack([accb8[:, 0], accb8[:, 1], accb5[:, 2], accb5[:, 0], accb5[:, 1], accb4[:, 0]], axis=1)
    dgain = jnp.stack([accg8[0], accg5[1], accg5[0], accg4[0]], axis=0)
    grads = [g_in, g_out, g_g, g_u, g_d]
    grads16 = [g_in16, g_out16, g_g16, g_u16, g_d16]
    return loss_blk, gx, grads, grads16, dmod, dgain, dps[0:1], dwp


def kernel(x, c, w_cond, b_cond, g_mix_pre, g_mix_post, w_in, w_pool, pool_scale, w_out, g_ffn_pre, g_ffn_post, w_gate, w_up, w_down, loss_target, m_w_cond, m_b_cond, m_g_mix_pre, m_g_mix_post, m_w_in, m_w_pool, m_pool_scale, m_w_out, m_g_ffn_pre, m_g_ffn_post, m_w_gate, m_w_up, m_w_down, v_w_cond, v_b_cond, v_g_mix_pre, v_g_mix_post, v_w_in, v_w_pool, v_pool_scale, v_w_out, v_g_ffn_pre, v_g_ffn_post, v_w_gate, v_w_up, v_w_down):
    xi, yi, ci = _position()
    chip = 2 * xi + yi
    dev = 4 * xi + 2 * yi + ci
    nb, seq, d = x.shape
    t_all = nb * seq
    xt = x.reshape(t_all, d)
    tgt = loss_target.reshape(t_all, d)
    ncol = w_cond.shape[2]
    pw = pool_scale.shape[1]

    c_pad = jnp.concatenate([c, jnp.zeros((8 - nb, d), F32)], axis=0)
    c_all = _all_gather(c_pad, "gather_c").reshape(N_DEV, 8, d)[:, :nb].reshape(N_DEV * nb, d)
    b_q = lax.dynamic_slice(b_cond, (0, chip * ncol), (1, ncol))
    sc_all, mod_q = _cond_fwd(c_all, w_cond[0], b_q, 512)
    mod_parts = _all_gather(mod_q, "gather_mod").reshape(N_DEV, N_DEV * nb, ncol)
    mod_rows = lax.dynamic_slice(mod_parts, (0, dev * nb, 0), (N_DEV, nb, ncol))[0::2]
    mod = jnp.transpose(mod_rows, (1, 0, 2)).reshape(nb, N_MOD, d)
    mod = jnp.concatenate([mod, jnp.zeros((nb, MOD_ROWS - N_MOD, d), F32)], axis=1)

    place = jnp.stack([chip, ci]).astype(jnp.int32)
    turned = lambda t: jnp.swapaxes(t[0], 0, 1)
    placed = _place_quarters(place, [w_in[0], w_out[0], turned(w_gate), turned(w_up), w_down[0]])
    (w_in_all,) = _gather_weights(placed[:1])
    send_sems, recv_sems, in_flight, token = _gather_start(placed[1:], [mod, w_in_all])
    mod = mod + token[0:1, 0:1]

    def later_weights(after):
        return _gather_forward(_gather_wait(send_sems, recv_sems, in_flight, after))

    ffn_split = []

    def on_ffn_grads(ffn_grads, ffn_grads16):
        theirs = _sibling_exchange(ffn_grads16, "ffn")
        ffn_split.extend(_chip_exchange_start(_chip_sums(place[1:], ffn_grads, theirs, "ffn"), [], "ffn"))
        return ffn_split[4][0:1, 0:1]

    gains = (g_mix_pre, g_mix_post, g_ffn_pre, g_ffn_post)
    loss_blk, gx, grads, grads16, dmod, dgain, dps, dwp = _local_step(
        xt, tgt, mod, gains, w_pool[0], pool_scale, w_in_all, later_weights, on_ffn_grads, seq)

    sums_ffn, parts_ffn = _chip_exchange_wait(*ffn_split[:4], gx, "ffn")

    wp_rows = dwp.size // d
    loss_row = 2 * N_MOD + 4 + 1
    pad_rows = 24 - (loss_row + 1)
    payload = jnp.concatenate([
        dmod.reshape(nb * N_MOD, d), dgain,
        jnp.concatenate([dps, jnp.zeros((1, d - pw), F32)], axis=1),
        jnp.concatenate([loss_blk[0:1], jnp.zeros((1, d - LANES), F32)], axis=1),
        jnp.zeros((pad_rows, d), F32), dwp.reshape(wp_rows, d)], axis=0)
    prow = payload.shape[0]
    gathered = _all_gather(payload, "gather_small")
    summed = _group_sum(gathered, prow, "small_device_sum")
    loss = summed[loss_row, 0]
    dmod_all = gathered.reshape(N_DEV, prow, d)[:, :nb * N_MOD].reshape(N_DEV * nb, N_MOD * d)
    dmod_q = lax.dynamic_slice(dmod_all, (0, chip * ncol), (N_DEV * nb, ncol))
    g_w_cond = _cond_bwd(sc_all, dmod_q, 512)
    first_gain = 2 * N_MOD

    theirs = _sibling_exchange(grads16[:2], "mix")
    mix_split = _chip_exchange_start(_chip_sums(place[1:], grads[:2], theirs, "mix"), [gathered], "mix")
    unfold = lambda halves: [g.reshape(2 * g.shape[1], g.shape[2]) for g in halves]
    g_ffn = unfold(_sibling_share(_total_sums(place, sums_ffn, parts_ffn, [mix_split[4]], "ffn"), "ffn"))

    results = {}

    def update(name, w2, g2, m2, v2, shape):
        delta, new_m, new_v = _adamw(w2, g2, m2, v2, "adamw_" + name)
        back = (lambda t: jnp.swapaxes(t, 0, 1)[None]) if shape is None else (lambda t: t.reshape(shape))
        results[name] = [back(t) for t in (g2, delta, new_m, new_v)]
        return delta

    done = [update("w_gate", turned(w_gate), g_ffn[0], turned(m_w_gate), turned(v_w_gate), None),
            update("w_up", turned(w_up), g_ffn[1], turned(m_w_up), turned(v_w_up), None),
            update("w_down", w_down[0], g_ffn[2], m_w_down[0], v_w_down[0], w_down.shape),
            update("w_cond", w_cond[0], g_w_cond, m_w_cond[0], v_w_cond[0], w_cond.shape)]

    gain_row = lambda r: (lambda s: s[first_gain + r:first_gain + r + 1, :])
    small = [
        ("b_cond", (b_cond, m_b_cond, v_b_cond), (N_MOD, d), lambda s: s[0:N_MOD, :] + s[N_MOD:2 * N_MOD, :]),
        ("g_mix_pre", (g_mix_pre, m_g_mix_pre, v_g_mix_pre), (1, d), gain_row(0)),
        ("g_mix_post", (g_mix_post, m_g_mix_post, v_g_mix_post), (1, d), gain_row(1)),
        ("g_ffn_pre", (g_ffn_pre, m_g_ffn_pre, v_g_ffn_pre), (1, d), gain_row(2)),
        ("g_ffn_post", (g_ffn_post, m_g_ffn_post, v_g_ffn_post), (1, d), gain_row(3)),
        ("pool_scale", (pool_scale, m_pool_scale, v_pool_scale), (1, pw),
         lambda s: s[first_gain + 4:first_gain + 5, 0:pw]),
        ("w_pool", (w_pool, m_w_pool, v_w_pool), (wp_rows, d), lambda s: s[24:24 + wp_rows, :]),
    ]
    updated = _small_updates(summed, [tuple(t.reshape(flat) for t in wmv) + (pick,) for _, wmv, flat, pick in small])
    for (name, wmv, _, _), quad in zip(small, updated):
        results[name] = [t.reshape(wmv[0].shape) for t in quad]

    sums_mix, parts_mix = _chip_exchange_wait(*mix_split[:4], done[-1], "mix")
    g_mix = unfold(_sibling_share(_total_sums(place, sums_mix, parts_mix, done[:3], "mix"), "mix"))
    update("w_in", w_in[0], g_mix[0], m_w_in[0], v_w_in[0], w_in.shape)
    update("w_out", w_out[0], g_mix[1], m_w_out[0], v_w_out[0], w_out.shape)

    names = ("w_cond", "b_cond", "g_mix_pre", "g_mix_post", "w_in", "w_pool", "pool_scale", "w_out",
             "g_ffn_pre", "g_ffn_post", "w_gate", "w_up", "w_down")
    outs = [results[name][part] for part in range(4) for name in names]
    return (loss, gx.reshape(x.shape), *outs)
```

```python
import jax
import jax.numpy as jnp
import numpy as np
from jax import lax
from jax.experimental import pallas as pl
from jax.experimental.pallas import tpu as pltpu

F32 = jnp.float32
BF16 = jnp.bfloat16
MESH = pl.DeviceIdType.MESH

EPS = 1e-6
HEAD_DIM = 64
HEADS_PER_BLOCK = 2
LANES = 128
NEG_QK_SCALE = -0.125
POOL_WINDOWS = (2, 4, 8, 16)
POOL_GROUP = 128
HALO = 16
N_MOD = 6
MOD_ROWS = 8
N_CHIPS = 4
N_DEV = 8
VMEM_LIMIT = 56 * 1024 * 1024

ADAM_LR = 0.001
ADAM_B1 = 0.9
ADAM_B2 = 0.999
ADAM_EPS = 1e-08
ADAM_WD = 0.01
ADAM_STEP = 10

TOKEN_TILE = 512
GRAD_TOKEN_TILE = 2048
FFN_ROW_CHUNKS = 2
ROW_CHUNKS = 2
ATTN_TILE = 512
ATTN_KEY_TILE = 256
ATTN_ROW_CHUNK = 32
LOG_SUM_PASSES = 1


def _dot(a, b):
    return jnp.dot(a, b, preferred_element_type=F32)


def _dot_nt(a, b):
    return lax.dot_general(a, b, (((1,), (1,)), ((), ())), preferred_element_type=F32)


def _dot_tn(a, b):
    return lax.dot_general(a, b, (((0,), (0,)), ((), ())), preferred_element_type=F32)


def _split(v):
    hi = v.astype(BF16)
    lo = (v - hi.astype(F32)).astype(BF16)
    return hi, lo


def _rms(v):
    return lax.rsqrt(jnp.mean(v * v, axis=-1, keepdims=True) + EPS)


def _norm_bwd(dn, n, r):
    return r * (dn - n * jnp.mean(dn * n, axis=-1, keepdims=True))


def _sigmoid(v):
    return 0.5 * jnp.tanh(0.5 * v) + 0.5


def _colsum(v):
    return jnp.sum(v, axis=0, keepdims=True)


def _params(sem=None):
    return pltpu.CompilerParams(dimension_semantics=sem, vmem_limit_bytes=VMEM_LIMIT)


def _position():
    return lax.axis_index("x"), lax.axis_index("y"), lax.axis_index("c")


def _prenorm_proj(x, mod, g_pre, w_in, seq, tm):
    t_all, d = x.shape
    nt = seq // tm
    p = w_in.shape[2]

    def body(x_ref, mod_ref, g_ref, w_ref, h_ref, q_ref, k_ref, v_ref, u_ref, kt_ref, vt_ref):
        for c in range(ROW_CHUNKS):
            rows = slice(c * (tm // ROW_CHUNKS), (c + 1) * (tm // ROW_CHUNKS))
            xf = x_ref[rows, :]
            n = xf * _rms(xf)
            h = (n * g_ref[...]) * (1.0 + mod_ref[0, 1:2, :]) + mod_ref[0, 0:1, :]
            hb = h.astype(BF16)
            h_ref[rows, :] = hb
            q_ref[rows, :] = (_dot(hb, w_ref[0]) * NEG_QK_SCALE).astype(BF16)
            kf = _dot(hb, w_ref[1])
            vf = _dot(hb, w_ref[2])
            k_ref[rows, :] = kf.astype(BF16)
            v_ref[rows, :] = vf.astype(BF16)
            kt_ref[:, rows] = kf.T.astype(BF16)
            vt_ref[:, rows] = vf.T.astype(BF16)
            u_ref[rows, :] = _dot(hb, w_ref[3])

    tok = lambda i: (i, 0)
    tok_t = lambda i: (0, i)
    return pl.pallas_call(
        body, name="prenorm_proj", grid=(t_all // tm,),
        in_specs=[pl.BlockSpec((tm, d), tok),
                  pl.BlockSpec((1, MOD_ROWS, d), lambda i: (i // nt, 0, 0)),
                  pl.BlockSpec((1, d), lambda i: (0, 0)),
                  pl.BlockSpec((N_CHIPS, d, p), lambda i: (0, 0, 0))],
        out_specs=[pl.BlockSpec((tm, d), tok)] + [pl.BlockSpec((tm, p), tok)] * 4 + [pl.BlockSpec((p, tm), tok_t)] * 2,
        out_shape=[jax.ShapeDtypeStruct((t_all, d), BF16)] + [jax.ShapeDtypeStruct((t_all, p), BF16)] * 3
        + [jax.ShapeDtypeStruct((t_all, p), F32)] + [jax.ShapeDtypeStruct((p, t_all), BF16)] * 2,
        compiler_params=_params(("arbitrary",)),
    )(x, mod, g_pre, w_in)


def _tri_matrix(tk, kind):
    j = np.arange(2 * tk)[:, None] % tk
    s = np.arange(tk)[None, :]
    return jnp.asarray({"after": j > s, "upto": j <= s, "before": j < s}[kind], dtype=BF16)


def _neg_abs(v):
    bits = lax.bitcast_convert_type(v, jnp.int32) | jnp.int32(-2 ** 31)
    return lax.bitcast_convert_type(bits, F32)


def _row_sums(v):
    return jnp.broadcast_to(jnp.sum(v, axis=-1, keepdims=True), (v.shape[0], LANES))


def _across(v, n):
    return jnp.concatenate([v] * (n // LANES), axis=1)


def _all_masked(c, diag, rc, tk):
    return diag is not None and diag * tk >= (c + 1) * rc - 1


def _some_masked(c, diag, rc, tk):
    return diag is not None and diag * tk + tk - 1 >= c * rc


def _attn_fwd(qn, k, vt, seq, tq, tk):
    t_all, w = qn.shape
    nb, nq, ndiag = t_all // seq, seq // tq, tq // tk
    assert ndiag % 2 == 0, "two key blocks per loop trip"
    rc = ATTN_ROW_CHUNK
    heads = range(HEADS_PER_BLOCK)

    def body(q_ref, k_ref, vt_ref, tri_ref, o_ref, l_ref,
             z_buf, ls_buf, hl_buf, aft_buf, w_buf, tot_buf, acc_t, run_buf):
        i = pl.program_id(2)
        nblk = (i + 1) * ndiag
        lane = lax.broadcasted_iota(jnp.int32, (1, LANES), 1)
        row = lax.broadcasted_iota(jnp.int32, (rc, tk), 0)
        col = lax.broadcasted_iota(jnp.int32, (rc, tk), 1)
        first = lane < HEAD_DIM
        q2 = q_ref[...]
        qs = [jnp.where(first, q2, jnp.zeros_like(q2)), jnp.where(first, jnp.zeros_like(q2), q2)]
        acc_t[...] = jnp.zeros_like(acc_t)
        run_buf[...] = jnp.zeros_like(run_buf)
        w_buf[1] = jnp.zeros((HEADS_PER_BLOCK, tq, tk), BF16)

        def causal(c, diag):
            return (col + diag * tk) < (row + c * rc)

        def scores(blk, slot):
            kj = k_ref[pl.ds(pl.multiple_of(blk * tk, tk), tk), :]
            for h in heads:
                z_buf[slot, h] = _dot_nt(qs[h], kj)

        def values(blk, slot):
            keys = pl.ds(pl.multiple_of(blk * tk, tk), tk)
            for h in heads:
                dims = slice(h * HEAD_DIM, (h + 1) * HEAD_DIM)
                acc_t[dims, :] += _dot_nt(vt_ref[dims, keys], w_buf[slot, h])

        def softplus_stage(h, slot, diag):
            for c in range(tq // rc):
                rows = slice(c * rc, (c + 1) * rc)
                if _all_masked(c, diag, rc, tk):
                    hl_buf[h, rows, :] = jnp.zeros((rc, LOG_SUM_PASSES * tk), BF16)
                    tot_buf[h, rows, :] = jnp.zeros((rc, LANES), F32)
                    continue
                nz = z_buf[slot, h, rows, :]
                l1 = jnp.minimum(nz, 0.0) - jnp.log(1.0 + jnp.exp(_neg_abs(nz)))
                if _some_masked(c, diag, rc, tk):
                    l1 = jnp.where(causal(c, diag), l1, 0.0)
                for s, part in enumerate(_split(l1)[:LOG_SUM_PASSES]):
                    hl_buf[h, rows, s * tk:(s + 1) * tk] = part
                ls_buf[h, rows, :] = l1 - nz
                tot_buf[h, rows, :] = _row_sums(l1)

        def weights_stage(h, slot, diag):
            for c in range(tq // rc):
                rows = slice(c * rc, (c + 1) * rc)
                if _all_masked(c, diag, rc, tk):
                    w_buf[slot, h, rows, :] = jnp.zeros((rc, tk), BF16)
                    continue
                wgt = jnp.exp((ls_buf[h, rows, :] + aft_buf[h, rows, :]) + _across(run_buf[h, rows, :], tk))
                if _some_masked(c, diag, rc, tk):
                    wgt = jnp.where(causal(c, diag), wgt, 0.0)
                w_buf[slot, h, rows, :] = wgt.astype(BF16)
                run_buf[h, rows, :] += tot_buf[h, rows, :]

        def position(blk, slot, diag):
            scores(jnp.maximum(blk - 1, 0), 1 - slot)
            for h in heads:
                softplus_stage(h, slot, diag)
                aft_buf[h] = _dot(hl_buf[h], tri_ref[...])
            values(jnp.minimum(blk + 1, nblk - 1), 1 - slot)
            for h in heads:
                weights_stage(h, slot, diag)

        scores(nblk - 1, 0)
        for p in range(ndiag):
            position(nblk - 1 - p, p % 2, ndiag - 1 - p)

        def trip(jj, carry):
            for u in range(2):
                position(i * ndiag - 1 - 2 * jj - u, u, None)
            return carry

        lax.fori_loop(0, (i * ndiag) // 2, trip, 0)
        values(0, 1)
        o_ref[...] = acc_t[...].T.astype(BF16)
        l_ref[...] = jnp.where(first, run_buf[0], run_buf[1])

    qmap = lambda b, hp, i: (b * nq + i, hp)
    nh = HEADS_PER_BLOCK
    return pl.pallas_call(
        body, name="attn_fwd", grid=(nb, w // LANES, nq),
        in_specs=[pl.BlockSpec((tq, LANES), qmap), pl.BlockSpec((seq, LANES), lambda b, hp, i: (b, hp)),
                  pl.BlockSpec((LANES, seq), lambda b, hp, i: (hp, b)),
                  pl.BlockSpec((LOG_SUM_PASSES * tk, tk), lambda b, hp, i: (0, 0))],
        out_specs=[pl.BlockSpec((tq, LANES), qmap), pl.BlockSpec((tq, LANES), qmap)],
        out_shape=[jax.ShapeDtypeStruct((t_all, w), BF16), jax.ShapeDtypeStruct((t_all, w), F32)],
        scratch_shapes=[pltpu.VMEM((2, nh, tq, tk), F32), pltpu.VMEM((nh, tq, tk), F32),
                        pltpu.VMEM((nh, tq, LOG_SUM_PASSES * tk), BF16), pltpu.VMEM((nh, tq, tk), F32),
                        pltpu.VMEM((2, nh, tq, tk), BF16), pltpu.VMEM((nh, tq, LANES), F32),
                        pltpu.VMEM((LANES, tq), F32), pltpu.VMEM((nh, tq, LANES), F32)],
        compiler_params=_params(("arbitrary", "arbitrary", "arbitrary")),
    )(qn, k, vt, _tri_matrix(tk, "after")[:LOG_SUM_PASSES * tk])


def _window_sums(ext, rows, offset, forward):
    r = lax.broadcasted_iota(jnp.int32, (rows, rows + HALO), 0)
    e = lax.broadcasted_iota(jnp.int32, (rows, rows + HALO), 1)
    hi, lo = _split(ext)
    out = []
    for g, win in enumerate(POOL_WINDOWS):
        if forward:
            band = (e >= r) & (e < r + win)
        else:
            band = (e <= r + offset) & (e > r + offset - win)
        bm = band.astype(BF16)
        cols = slice(g * POOL_GROUP, (g + 1) * POOL_GROUP)
        out.append(_dot(bm, hi[:, cols]) + _dot(bm, lo[:, cols]))
    return out


def _window_counts(pos):
    return [jnp.minimum(pos + 1, win).astype(F32) for win in POOL_WINDOWS]


def _mixer_post(u, o, x, mod, g_post, g_fpre, w_pool, pool_scale, w_out, seq, tm):
    t_all, d = x.shape
    nt = seq // tm
    p = u.shape[1]

    def body(u_ref, halo_ref, o_ref, x_ref, mod_ref, gp_ref, gf_ref, wp_ref, ps_ref, wo_ref,
             pooled_ref, mixin_ref, mix_ref, x1_ref, h2_ref):
        it = pl.program_id(0) % nt
        uf = u_ref[...]
        halo = jnp.where(it == 0, 0.0, halo_ref[...])
        ext = jnp.concatenate([halo, uf], axis=0)
        pos = it * tm + lax.broadcasted_iota(jnp.int32, (tm, 1), 0)
        sums = _window_sums(ext, tm, HALO, False)
        cnts = _window_counts(pos)
        pools = []
        for g in range(len(POOL_WINDOWS)):
            cols = slice(g * POOL_GROUP, (g + 1) * POOL_GROUP)
            pooled = (sums[g] / cnts[g] - uf[:, cols]).astype(BF16)
            pooled_ref[:, cols] = pooled
            yg = _dot(pooled, wp_ref[g].astype(BF16))
            pools.append((yg * ps_ref[:, cols]).astype(BF16))
        mixin_ref[...] = jnp.concatenate([o_ref[...]] + pools, axis=1)
        for c in range(ROW_CHUNKS):
            rows = slice(c * (tm // ROW_CHUNKS), (c + 1) * (tm // ROW_CHUNKS))
            mix = _dot(mixin_ref[rows, :], wo_ref[...])
            mix_ref[rows, :] = mix
            n2 = mix * _rms(mix)
            x1 = x_ref[rows, :] + mod_ref[0, 2:3, :] * (n2 * gp_ref[...])
            x1_ref[rows, :] = x1
            n3 = x1 * _rms(x1)
            h2 = (n3 * gf_ref[...]) * (1.0 + mod_ref[0, 4:5, :]) + mod_ref[0, 3:4, :]
            h2_ref[rows, :] = h2.astype(BF16)

    tok = lambda i: (i, 0)
    const2 = lambda i: (0, 0)
    hb = tm // HALO
    return pl.pallas_call(
        body, name="mixer_post", grid=(t_all // tm,),
        in_specs=[pl.BlockSpec((tm, p), tok),
                  pl.BlockSpec((HALO, p), lambda i: (jnp.maximum(i * hb - 1, 0), 0)),
                  pl.BlockSpec((tm, p), tok),
                  pl.BlockSpec((tm, d), tok),
                  pl.BlockSpec((1, MOD_ROWS, d), lambda i: (i // nt, 0, 0)),
                  pl.BlockSpec((1, d), const2), pl.BlockSpec((1, d), const2),
                  pl.BlockSpec(w_pool.shape, lambda i: (0, 0, 0)),
                  pl.BlockSpec((1, p), const2),
                  pl.BlockSpec((d, d), const2)],
        out_specs=[pl.BlockSpec((tm, p), tok), pl.BlockSpec((tm, d), tok), pl.BlockSpec((tm, d), tok),
                   pl.BlockSpec((tm, d), tok), pl.BlockSpec((tm, d), tok)],
        out_shape=[jax.ShapeDtypeStruct((t_all, p), BF16), jax.ShapeDtypeStruct((t_all, d), BF16),
                   jax.ShapeDtypeStruct((t_all, d), F32), jax.ShapeDtypeStruct((t_all, d), F32),
                   jax.ShapeDtypeStruct((t_all, d), BF16)],
        compiler_params=_params(("arbitrary",)),
    )(u, u, o, x, mod, g_post, g_fpre, w_pool, pool_scale, w_out)


def _ffn_fwd(h2, w_g, w_u, w_d, x1, tgt, mod, g_post, seq, tm):
    t_all, d = x1.shape
    nt = seq // tm
    nk, ff, _ = w_g.shape

    def body(h_ref, wg_ref, wu_ref, wd_ref, x1_ref, t_ref, mod_ref, g_ref,
             a_ref, b_ref, fin_ref, dy_ref, df_ref, loss_ref, accb_ref, accg_ref, facc):
        i, k = pl.program_id(0), pl.program_id(1)

        @pl.when(k == 0)
        def _():
            facc[...] = jnp.zeros_like(facc)

        for c in range(FFN_ROW_CHUNKS):
            rows = slice(c * (tm // FFN_ROW_CHUNKS), (c + 1) * (tm // FFN_ROW_CHUNKS))
            hb = h_ref[rows, :]
            a = _dot_nt(hb, wg_ref[0])
            b = _dot_nt(hb, wu_ref[0])
            a_ref[0, rows, :] = a.astype(BF16)
            b_ref[0, rows, :] = b.astype(BF16)
            fin = ((a * _sigmoid(a)) * b).astype(BF16)
            fin_ref[0, rows, :] = fin
            facc[rows, :] += _dot(fin, wd_ref[0])

        @pl.when(k == nk - 1)
        def _():
            f = facc[...]
            r4 = _rms(f)
            n4 = f * r4
            gate = mod_ref[0, 5:6, :]
            g = g_ref[...]
            err = (x1_ref[...] + gate * (n4 * g)) - t_ref[...]
            dy = err * (1.0 / d)
            dy_ref[...] = dy

            @pl.when(i == 0)
            def _():
                loss_ref[...] = jnp.zeros_like(loss_ref)
                accg_ref[...] = jnp.zeros_like(accg_ref)

            @pl.when(i % nt == 0)
            def _():
                accb_ref[...] = jnp.zeros_like(accb_ref)

            loss_ref[...] += (0.5 / d) * jnp.sum(err * err)
            accb_ref[0, 0:1, :] += _colsum(dy * (n4 * g))
            accg_ref[0:1, :] += _colsum((dy * gate) * n4)
            dn4 = (dy * gate) * g
            df_ref[...] = _norm_bwd(dn4, n4, r4).astype(BF16)

    tok = lambda i, k: (i, 0)
    ktok = lambda i, k: (k, i, 0)
    kw = lambda i, k: (k, 0, 0)
    const2 = lambda i, k: (0, 0)
    return pl.pallas_call(
        body, name="ffn_fwd", grid=(t_all // tm, nk),
        in_specs=[pl.BlockSpec((tm, d), tok),
                  pl.BlockSpec((1, ff, d), kw), pl.BlockSpec((1, ff, d), kw), pl.BlockSpec((1, ff, d), kw),
                  pl.BlockSpec((tm, d), tok), pl.BlockSpec((tm, d), tok),
                  pl.BlockSpec((1, MOD_ROWS, d), lambda i, k: (i // nt, 0, 0)),
                  pl.BlockSpec((1, d), const2)],
        out_specs=[pl.BlockSpec((1, tm, ff), ktok)] * 3
        + [pl.BlockSpec((tm, d), tok), pl.BlockSpec((tm, d), tok),
           pl.BlockSpec((8, LANES), const2),
           pl.BlockSpec((1, 8, d), lambda i, k: (i // nt, 0, 0)),
           pl.BlockSpec((8, d), const2)],
        out_shape=[jax.ShapeDtypeStruct((nk, t_all, ff), BF16)] * 3
        + [jax.ShapeDtypeStruct((t_all, d), F32), jax.ShapeDtypeStruct((t_all, d), BF16),
           jax.ShapeDtypeStruct((8, LANES), F32),
           jax.ShapeDtypeStruct((t_all // seq, 8, d), F32),
           jax.ShapeDtypeStruct((8, d), F32)],
        scratch_shapes=[pltpu.VMEM((tm, d), F32)],
        compiler_params=_params(("arbitrary", "arbitrary")),
    )(h2, w_g, w_u, w_d, x1, tgt, mod, g_post)


def _ffn_bwd(df, a, b, w_d, w_g, w_u, x1, dy, mix, mod, g_fpre, g_mpost, seq, tm):
    t_all, d = x1.shape
    nt = seq // tm
    nk, ff, _ = w_g.shape

    def body(df_ref, a_ref, b_ref, wd_ref, wg_ref, wu_ref, x1_ref, dy_ref, mix_ref, mod_ref, gf_ref, gm_ref,
             da_ref, db_ref, dx1_ref, dmix_ref, accb_ref, accg_ref, hacc):
        i, k = pl.program_id(0), pl.program_id(1)

        @pl.when(k == 0)
        def _():
            hacc[...] = jnp.zeros_like(hacc)

        for c in range(FFN_ROW_CHUNKS):
            rows = slice(c * (tm // FFN_ROW_CHUNKS), (c + 1) * (tm // FFN_ROW_CHUNKS))
            dfin = _dot_nt(df_ref[rows, :], wd_ref[0])
            af = a_ref[0, rows, :].astype(F32)
            bf = b_ref[0, rows, :].astype(F32)
            sig = _sigmoid(af)
            da = ((dfin * bf) * (sig * (1.0 + af * (1.0 - sig)))).astype(BF16)
            db = (dfin * (af * sig)).astype(BF16)
            da_ref[0, rows, :] = da
            db_ref[0, rows, :] = db
            hacc[rows, :] += _dot(da, wg_ref[0]) + _dot(db, wu_ref[0])

        @pl.when(k == nk - 1)
        def _():
            @pl.when(i == 0)
            def _():
                accg_ref[...] = jnp.zeros_like(accg_ref)

            @pl.when(i % nt == 0)
            def _():
                accb_ref[...] = jnp.zeros_like(accb_ref)

            dh2 = hacc[...]
            x1 = x1_ref[...]
            r3 = _rms(x1)
            n3 = x1 * r3
            g3 = gf_ref[...]
            scale1 = 1.0 + mod_ref[0, 4:5, :]
            accb_ref[0, 0:1, :] += _colsum(dh2)
            accb_ref[0, 1:2, :] += _colsum(dh2 * (n3 * g3))
            accg_ref[0:1, :] += _colsum((dh2 * scale1) * n3)
            dx1 = dy_ref[...] + _norm_bwd((dh2 * scale1) * g3, n3, r3)
            dx1_ref[...] = dx1
            mix = mix_ref[...]
            r2 = _rms(mix)
            n2 = mix * r2
            g2 = gm_ref[...]
            gate = mod_ref[0, 2:3, :]
            accb_ref[0, 2:3, :] += _colsum(dx1 * (n2 * g2))
            accg_ref[1:2, :] += _colsum((dx1 * gate) * n2)
            dmix_ref[...] = _norm_bwd((dx1 * gate) * g2, n2, r2).astype(BF16)

    tok = lambda i, k: (i, 0)
    ktok = lambda i, k: (k, i, 0)
    kw = lambda i, k: (k, 0, 0)
    const2 = lambda i, k: (0, 0)
    return pl.pallas_call(
        body, name="ffn_bwd", grid=(t_all // tm, nk),
        in_specs=[pl.BlockSpec((tm, d), tok),
                  pl.BlockSpec((1, tm, ff), ktok), pl.BlockSpec((1, tm, ff), ktok),
                  pl.BlockSpec((1, ff, d), kw), pl.BlockSpec((1, ff, d), kw), pl.BlockSpec((1, ff, d), kw),
                  pl.BlockSpec((tm, d), tok), pl.BlockSpec((tm, d), tok), pl.BlockSpec((tm, d), tok),
                  pl.BlockSpec((1, MOD_ROWS, d), lambda i, k: (i // nt, 0, 0)),
                  pl.BlockSpec((1, d), const2), pl.BlockSpec((1, d), const2)],
        out_specs=[pl.BlockSpec((1, tm, ff), ktok)] * 2
        + [pl.BlockSpec((tm, d), tok), pl.BlockSpec((tm, d), tok),
           pl.BlockSpec((1, 8, d), lambda i, k: (i // nt, 0, 0)),
           pl.BlockSpec((8, d), const2)],
        out_shape=[jax.ShapeDtypeStruct((nk, t_all, ff), BF16)] * 2
        + [jax.ShapeDtypeStruct((t_all, d), F32), jax.ShapeDtypeStruct((t_all, d), BF16),
           jax.ShapeDtypeStruct((t_all // seq, 8, d), F32),
           jax.ShapeDtypeStruct((8, d), F32)],
        scratch_shapes=[pltpu.VMEM((tm, d), F32)],
        compiler_params=_params(("arbitrary", "arbitrary")),
    )(df, a, b, w_d, w_g, w_u, x1, dy, mix, mod, g_fpre, g_mpost)


def _mixer_bwd(dmix, w_out, pooled, w_pool, pool_scale, seq, tm):
    t_all, d = dmix.shape
    p = pooled.shape[1]
    ng = len(POOL_WINDOWS)

    def body(dm_ref, wo_ref, pooled_ref, wp_ref, ps_ref, do_ref, dpd_ref, dps_ref, dwp_ref):
        i = pl.program_id(0)

        @pl.when(i == 0)
        def _():
            dps_ref[...] = jnp.zeros_like(dps_ref)
            dwp_ref[...] = jnp.zeros_like(dwp_ref)

        dmixin = _dot_nt(dm_ref[...], wo_ref[...])
        do_ref[...] = dmixin[:, :p].astype(BF16)
        for g in range(ng):
            cols = slice(g * POOL_GROUP, (g + 1) * POOL_GROUP)
            dpool = dmixin[:, p + g * POOL_GROUP:p + (g + 1) * POOL_GROUP]
            pooled = pooled_ref[:, cols]
            wpg = wp_ref[g].astype(BF16)
            yg = _dot(pooled, wpg)
            dps_ref[0:1, cols] += _colsum(dpool * yg)
            dyg = (dpool * ps_ref[:, cols]).astype(BF16)
            dwp_ref[g] += _dot_tn(pooled, dyg)
            dpd_ref[:, cols] = _dot_nt(dyg, wpg)

    tok = lambda i: (i, 0)
    const2 = lambda i: (0, 0)
    const3 = lambda i: (0, 0, 0)
    return pl.pallas_call(
        body, name="mixer_bwd", grid=(t_all // tm,),
        in_specs=[pl.BlockSpec((tm, d), tok), pl.BlockSpec((d, d), const2), pl.BlockSpec((tm, p), tok),
                  pl.BlockSpec(w_pool.shape, const3), pl.BlockSpec((1, p), const2)],
        out_specs=[pl.BlockSpec((tm, p), tok), pl.BlockSpec((tm, p), tok),
                   pl.BlockSpec((8, p), const2), pl.BlockSpec(w_pool.shape, const3)],
        out_shape=[jax.ShapeDtypeStruct((t_all, p), BF16), jax.ShapeDtypeStruct((t_all, p), F32),
                   jax.ShapeDtypeStruct((8, p), F32), jax.ShapeDtypeStruct(w_pool.shape, F32)],
        compiler_params=_params(("arbitrary",)),
    )(dmix, w_out, pooled, w_pool, pool_scale)


def _attn_bwd(qn, k, kt, v, do, ltot, seq, tq, tk):
    t_all, w = qn.shape
    nb, nq, ndiag, nkb = t_all // seq, seq // tq, tq // tk, seq // tk
    assert ndiag % 2 == 0, "two key blocks per loop trip"
    rc = ATTN_ROW_CHUNK
    nh = HEADS_PER_BLOCK
    heads = range(nh)

    def body(q_ref, k_ref, kt_ref, v_ref, do_ref, l_ref, up_ref, bf_ref, dq_ref, dk_ref, dv_ref,
             z_buf, dw_buf, ls_buf, hl_buf, upto_buf, g_buf, gb_buf, before_buf, w_buf, dz_buf,
             totl_buf, totg_buf, rem_buf, preg_buf, qnt_buf, dot_buf, dq_t, dk_t, dv_t):
        i = pl.program_id(2)
        nblk = (i + 1) * ndiag

        @pl.when(i == 0)
        def _():
            dk_t[...] = jnp.zeros_like(dk_t)
            dv_t[...] = jnp.zeros_like(dv_t)

        lane = lax.broadcasted_iota(jnp.int32, (1, LANES), 1)
        row = lax.broadcasted_iota(jnp.int32, (rc, tk), 0)
        col = lax.broadcasted_iota(jnp.int32, (rc, tk), 1)
        first = lane < HEAD_DIM
        q2 = q_ref[...]
        do2 = do_ref[...]
        l2 = l_ref[...]
        qs = [jnp.where(first, q2, jnp.zeros_like(q2)), jnp.where(first, jnp.zeros_like(q2), q2)]
        dos = [jnp.where(first, do2, jnp.zeros_like(do2)), jnp.where(first, jnp.zeros_like(do2), do2)]
        qnt_buf[...] = q2.astype(F32).T.astype(BF16)
        dot_buf[...] = do2.astype(F32).T.astype(BF16)
        for h in heads:
            rem_buf[h] = jnp.where(first if h == 0 else ~first, l2, pltpu.roll(l2, HEAD_DIM, 1))
        preg_buf[...] = jnp.zeros_like(preg_buf)
        dq_t[...] = jnp.zeros_like(dq_t)
        w_buf[1] = jnp.zeros((nh * tq, tk), BF16)
        dz_buf[1] = jnp.zeros((nh * tq, tk), BF16)

        def causal(c, diag):
            return (col + diag * tk) < (row + c * rc)

        def scores(blk, slot):
            off = pl.multiple_of(blk * tk, tk)
            kj = k_ref[pl.ds(off, tk), :]
            vj = v_ref[pl.ds(off, tk), :]
            for h in heads:
                z_buf[slot, h] = _dot_nt(qs[h], kj)
                dw_buf[slot, h] = _dot_nt(dos[h], vj)

        def gradients(blk, slot):
            keys = pl.ds(pl.multiple_of(blk * tk, tk), tk)
            for h in heads:
                dims = slice(h * HEAD_DIM, (h + 1) * HEAD_DIM)
                queries = slice(h * tq, (h + 1) * tq)
                dq_t[dims, :] += _dot_nt(kt_ref[dims, keys], dz_buf[slot, queries, :])
                dk_t[blk, dims, :] += _dot(qnt_buf[dims, :], dz_buf[slot, queries, :])
                dv_t[blk, dims, :] += _dot(dot_buf[dims, :], w_buf[slot, queries, :])

        def softplus_stage(h, slot, diag):
            for c in range(tq // rc):
                rows = slice(c * rc, (c + 1) * rc)
                if _all_masked(c, diag, rc, tk):
                    hl_buf[h, rows, :] = jnp.zeros((rc, LOG_SUM_PASSES * tk), BF16)
                    continue
                nz = z_buf[slot, h, rows, :]
                l1 = jnp.minimum(nz, 0.0) - jnp.log(1.0 + jnp.exp(_neg_abs(nz)))
                if _some_masked(c, diag, rc, tk):
                    l1 = jnp.where(causal(c, diag), l1, 0.0)
                for s, part in enumerate(_split(l1)[:LOG_SUM_PASSES]):
                    hl_buf[h, rows, s * tk:(s + 1) * tk] = part
                ls_buf[h, rows, :] = l1 - nz
                totl_buf[h, rows, :] = _row_sums(l1)

        def weights_stage(h, slot, diag):
            for c in range(tq // rc):
                rows = slice(c * rc, (c + 1) * rc)
                stacked = slice(h * tq + c * rc, h * tq + (c + 1) * rc)
                if _all_masked(c, diag, rc, tk):
                    w_buf[slot, stacked, :] = jnp.zeros((rc, tk), BF16)
                    gb_buf[h, rows, :] = jnp.zeros((rc, tk), BF16)
                    continue
                wgt = jnp.exp(ls_buf[h, rows, :] + (_across(rem_buf[h, rows, :], tk) - upto_buf[h, rows, :]))
                if _some_masked(c, diag, rc, tk):
                    wgt = jnp.where(causal(c, diag), wgt, 0.0)
                w_buf[slot, stacked, :] = wgt.astype(BF16)
                g = wgt * dw_buf[slot, h, rows, :]
                g_buf[h, rows, :] = g
                gb_buf[h, rows, :] = g.astype(BF16)
                totg_buf[h, rows, :] = _row_sums(g)
                rem_buf[h, rows, :] -= totl_buf[h, rows, :]

        def dscore_stage(h, slot, diag):
            for c in range(tq // rc):
                rows = slice(c * rc, (c + 1) * rc)
                stacked = slice(h * tq + c * rc, h * tq + (c + 1) * rc)
                if _all_masked(c, diag, rc, tk):
                    dz_buf[slot, stacked, :] = jnp.zeros((rc, tk), BF16)
                    continue
                sig = jnp.exp(ls_buf[h, rows, :])
                g = g_buf[h, rows, :]
                dnz = sig * ((before_buf[h, rows, :] + _across(preg_buf[h, rows, :], tk)) + g) - g
                if _some_masked(c, diag, rc, tk):
                    dnz = jnp.where(causal(c, diag), dnz, 0.0)
                dz_buf[slot, stacked, :] = dnz.astype(BF16)
                preg_buf[h, rows, :] += totg_buf[h, rows, :]

        def position(blk, slot, diag, prefetch):
            if prefetch:
                scores(blk + 1, 1 - slot)
            for h in heads:
                softplus_stage(h, slot, diag)
                upto_buf[h] = _dot(hl_buf[h], up_ref[...])
            gradients(jnp.maximum(blk - 1, 0), 1 - slot)
            for h in heads:
                weights_stage(h, slot, diag)
                before_buf[h] = _dot(gb_buf[h], bf_ref[...])
            for h in heads:
                dscore_stage(h, slot, diag)

        scores(0, 0)

        def trip(jj, carry):
            for u in range(2):
                position(2 * jj + u, u, None, True)
            return carry

        lax.fori_loop(0, (i * ndiag) // 2, trip, 0)
        for d in range(ndiag):
            position(i * ndiag + d, d % 2, d, d < ndiag - 1)
        gradients(nblk - 1, 1)
        dq_ref[...] = (dq_t[...].T * NEG_QK_SCALE).astype(BF16)

        @pl.when(i == nq - 1)
        def _():
            for blk in range(nkb):
                dk_ref[blk * tk:(blk + 1) * tk, :] = dk_t[blk].T.astype(BF16)
                dv_ref[blk * tk:(blk + 1) * tk, :] = dv_t[blk].T.astype(BF16)

    qmap = lambda b, hp, i: (b * nq + i, hp)
    kmap = lambda b, hp, i: (b, hp)
    const = lambda b, hp, i: (0, 0)
    return pl.pallas_call(
        body, name="attn_bwd", grid=(nb, w // LANES, nq),
        in_specs=[pl.BlockSpec((tq, LANES), qmap), pl.BlockSpec((seq, LANES), kmap),
                  pl.BlockSpec((LANES, seq), lambda b, hp, i: (hp, b)), pl.BlockSpec((seq, LANES), kmap),
                  pl.BlockSpec((tq, LANES), qmap), pl.BlockSpec((tq, LANES), qmap),
                  pl.BlockSpec((LOG_SUM_PASSES * tk, tk), const), pl.BlockSpec((tk, tk), const)],
        out_specs=[pl.BlockSpec((tq, LANES), qmap), pl.BlockSpec((seq, LANES), kmap), pl.BlockSpec((seq, LANES), kmap)],
        out_shape=[jax.ShapeDtypeStruct((t_all, w), BF16)] * 3,
        scratch_shapes=[pltpu.VMEM((2, nh, tq, tk), F32), pltpu.VMEM((2, nh, tq, tk), F32),
                        pltpu.VMEM((nh, tq, tk), F32), pltpu.VMEM((nh, tq, LOG_SUM_PASSES * tk), BF16),
                        pltpu.VMEM((nh, tq, tk), F32), pltpu.VMEM((nh, tq, tk), F32),
                        pltpu.VMEM((nh, tq, tk), BF16), pltpu.VMEM((nh, tq, tk), F32),
                        pltpu.VMEM((2, nh * tq, tk), BF16), pltpu.VMEM((2, nh * tq, tk), BF16),
                        pltpu.VMEM((nh, tq, LANES), F32), pltpu.VMEM((nh, tq, LANES), F32),
                        pltpu.VMEM((nh, tq, LANES), F32), pltpu.VMEM((nh, tq, LANES), F32),
                        pltpu.VMEM((LANES, tq), BF16), pltpu.VMEM((LANES, tq), BF16),
                        pltpu.VMEM((LANES, tq), F32), pltpu.VMEM((nkb, LANES, tk), F32),
                        pltpu.VMEM((nkb, LANES, tk), F32)],
        compiler_params=_params(("arbitrary", "arbitrary", "arbitrary")),
    )(qn, k, kt, v, do, ltot, _tri_matrix(tk, "upto")[:LOG_SUM_PASSES * tk], _tri_matrix(tk, "before")[:tk])


def _inproj_bwd(dq, dk, dv, dpd, x, dx1, mod, g_pre, w_in, seq, tm):
    t_all, d = x.shape
    nt = seq // tm
    p = dq.shape[1]

    def body(dq_ref, dk_ref, dv_ref, dpd_ref, halo_ref, x_ref, dx1_ref, mod_ref, g_ref, w_ref,
             gx_ref, du_ref, accb_ref, accg_ref):
        i = pl.program_id(0)
        it = i % nt

        @pl.when(i == 0)
        def _():
            accg_ref[...] = jnp.zeros_like(accg_ref)

        @pl.when(it == 0)
        def _():
            accb_ref[...] = jnp.zeros_like(accb_ref)

        dpd = dpd_ref[...]
        pos = it * tm + lax.broadcasted_iota(jnp.int32, (tm, 1), 0)
        cnts = _window_counts(pos)
        halo = jnp.where(it == nt - 1, 0.0, halo_ref[...])
        scaled = []
        halos = []
        for g, win in enumerate(POOL_WINDOWS):
            cols = slice(g * POOL_GROUP, (g + 1) * POOL_GROUP)
            scaled.append(dpd[:, cols] / cnts[g])
            halos.append(halo[:, cols] / float(win))
        ext = jnp.concatenate([jnp.concatenate(scaled, axis=1), jnp.concatenate(halos, axis=1)], axis=0)
        sums = _window_sums(ext, tm, 0, True)
        du = (jnp.concatenate(sums, axis=1) - dpd).astype(BF16)
        du_ref[...] = du
        g1 = g_ref[...]
        scale1 = 1.0 + mod_ref[0, 1:2, :]
        for c in range(ROW_CHUNKS):
            rows = slice(c * (tm // ROW_CHUNKS), (c + 1) * (tm // ROW_CHUNKS))
            dh1 = (_dot_nt(dq_ref[rows, :], w_ref[0]) + _dot_nt(dk_ref[rows, :], w_ref[1])
                   + _dot_nt(dv_ref[rows, :], w_ref[2]) + _dot_nt(du_ref[rows, :], w_ref[3]))
            xf = x_ref[rows, :]
            r1 = _rms(xf)
            n1 = xf * r1
            accb_ref[0, 0:1, :] += _colsum(dh1)
            accb_ref[0, 1:2, :] += _colsum(dh1 * (n1 * g1))
            accg_ref[0:1, :] += _colsum((dh1 * scale1) * n1)
            gx_ref[rows, :] = dx1_ref[rows, :] + _norm_bwd((dh1 * scale1) * g1, n1, r1)

    tok = lambda i: (i, 0)
    const2 = lambda i: (0, 0)
    hb = tm // HALO
    last = t_all // HALO - 1
    return pl.pallas_call(
        body, name="inproj_bwd", grid=(t_all // tm,),
        in_specs=[pl.BlockSpec((tm, p), tok), pl.BlockSpec((tm, p), tok), pl.BlockSpec((tm, p), tok),
                  pl.BlockSpec((tm, p), tok),
                  pl.BlockSpec((HALO, p), lambda i: (jnp.minimum((i + 1) * hb, last), 0)),
                  pl.BlockSpec((tm, d), tok), pl.BlockSpec((tm, d), tok),
                  pl.BlockSpec((1, MOD_ROWS, d), lambda i: (i // nt, 0, 0)),
                  pl.BlockSpec((1, d), const2),
                  pl.BlockSpec((N_CHIPS, d, p), lambda i: (0, 0, 0))],
        out_specs=[pl.BlockSpec((tm, d), tok), pl.BlockSpec((tm, p), tok),
                   pl.BlockSpec((1, 8, d), lambda i: (i // nt, 0, 0)),
                   pl.BlockSpec((8, d), const2)],
        out_shape=[jax.ShapeDtypeStruct((t_all, d), F32), jax.ShapeDtypeStruct((t_all, p), BF16),
                   jax.ShapeDtypeStruct((t_all // seq, 8, d), F32),
                   jax.ShapeDtypeStruct((8, d), F32)],
        compiler_params=_params(("arbitrary",)),
    )(dq, dk, dv, dpd, dpd, x, dx1, mod, g_pre, w_in)


def _tn_matmul(x, ys, nk, bt, name):
    t_all = x.shape[-2]
    m = x.shape[-1]
    ny = len(ys)
    nt = t_all // bt

    def spec(arr):
        if arr.ndim == 3:
            return pl.BlockSpec((1, bt, arr.shape[-1]), lambda k, t: (k, t, 0))
        return pl.BlockSpec((bt, arr.shape[-1]), lambda k, t: (t, 0))

    def tile(ref):
        return ref[0] if len(ref.shape) == 3 else ref[...]

    def body(*refs):
        x_ref, y_refs, o_refs, h_refs = refs[0], refs[1:1 + ny], refs[1 + ny:1 + 2 * ny], refs[1 + 2 * ny:]
        t = pl.program_id(1)
        xt = tile(x_ref)
        for y_ref, o_ref, h_ref in zip(y_refs, o_refs, h_refs):
            part = _dot_tn(xt, tile(y_ref))

            @pl.when(t == 0)
            def _(o_ref=o_ref, part=part):
                o_ref[0] = part

            @pl.when(t > 0)
            def _(o_ref=o_ref, part=part):
                o_ref[0] += part

            @pl.when(t == nt - 1)
            def _(o_ref=o_ref, h_ref=h_ref):
                h_ref[0] = o_ref[0].astype(BF16)

    out_specs = [pl.BlockSpec((1, m, y.shape[-1]), lambda k, t: (k, 0, 0)) for y in ys]
    out = pl.pallas_call(
        body, name=name, grid=(nk, nt),
        in_specs=[spec(x)] + [spec(y) for y in ys],
        out_specs=out_specs * 2,
        out_shape=[jax.ShapeDtypeStruct((nk, m, y.shape[-1]), dt) for dt in (F32, BF16) for y in ys],
        compiler_params=_params(("arbitrary", "arbitrary")),
    )(x, *ys)
    return out[:ny], out[ny:]


def _tn_matmul_stacked(x, ys, bt, name):
    t_all, m = x.shape
    n = ys[0].shape[1]
    ny = len(ys)
    nt = t_all // bt

    def body(*refs):
        x_ref, y_refs, o_ref, h_ref = refs[0], refs[1:1 + ny], refs[1 + ny], refs[2 + ny]
        t = pl.program_id(0)
        xt = x_ref[...]

        @pl.when(t == 0)
        def _():
            o_ref[...] = jnp.zeros_like(o_ref)

        for j, y_ref in enumerate(y_refs):
            o_ref[j] += _dot_tn(xt, y_ref[...])

        @pl.when(t == nt - 1)
        def _():
            h_ref[...] = o_ref[...].astype(BF16)

    whole = pl.BlockSpec((ny, m, n), lambda t: (0, 0, 0))
    return pl.pallas_call(
        body, name=name, grid=(nt,),
        in_specs=[pl.BlockSpec((bt, m), lambda t: (t, 0))] + [pl.BlockSpec((bt, n), lambda t: (t, 0))] * ny,
        out_specs=[whole, whole],
        out_shape=[jax.ShapeDtypeStruct((ny, m, n), F32), jax.ShapeDtypeStruct((ny, m, n), BF16)],
        compiler_params=_params(("arbitrary",)),
    )(x, *ys)


def _cond_fwd(c_all, w_q, b_q, bn):
    nrow, d = c_all.shape
    ncol = w_q.shape[1]

    def body(c_ref, w_ref, b_ref, sc_ref, mod_ref):
        cf = c_ref[...]
        sc = cf * _sigmoid(cf)
        sc_ref[...] = sc
        shi, slo = _split(sc)
        whi, wlo = _split(w_ref[...])
        mod_ref[...] = (_dot(shi, whi) + _dot(shi, wlo) + _dot(slo, whi)) + b_ref[...]

    return pl.pallas_call(
        body, name="cond_fwd", grid=(ncol // bn,),
        in_specs=[pl.BlockSpec((nrow, d), lambda n: (0, 0)), pl.BlockSpec((d, bn), lambda n: (0, n)),
                  pl.BlockSpec((1, bn), lambda n: (0, n))],
        out_specs=[pl.BlockSpec((nrow, d), lambda n: (0, 0)), pl.BlockSpec((nrow, bn), lambda n: (0, n))],
        out_shape=[jax.ShapeDtypeStruct((nrow, d), F32), jax.ShapeDtypeStruct((nrow, ncol), F32)],
        compiler_params=_params(("arbitrary",)),
    )(c_all, w_q, b_q)


def _cond_bwd(sc_all, dmod_q, bn):
    nrow, d = sc_all.shape
    ncol = dmod_q.shape[1]

    def body(sc_ref, dm_ref, gw_ref):
        shi, slo = _split(sc_ref[...])
        dhi, dlo = _split(dm_ref[...])
        gw_ref[...] = _dot_tn(shi, dhi) + _dot_tn(shi, dlo) + _dot_tn(slo, dhi)

    return pl.pallas_call(
        body, name="cond_bwd", grid=(ncol // bn,),
        in_specs=[pl.BlockSpec((nrow, d), lambda n: (0, 0)), pl.BlockSpec((nrow, bn), lambda n: (0, n))],
        out_specs=pl.BlockSpec((d, bn), lambda n: (0, n)),
        out_shape=jax.ShapeDtypeStruct((d, ncol), F32),
        compiler_params=_params(("arbitrary",)),
    )(sc_all, dmod_q)


def _row_block(rows, cols, budget=1 << 18):
    best = None
    for br in range(8, rows + 1, 8):
        if rows % br == 0 and br * cols <= budget:
            best = br
    return best if best is not None else rows


def _adam_math(w, g, m, v):
    c1 = 1.0 - ADAM_B1 ** ADAM_STEP
    c2 = 1.0 - ADAM_B2 ** ADAM_STEP
    m2 = ADAM_B1 * m + (1.0 - ADAM_B1) * g
    v2 = ADAM_B2 * v + (1.0 - ADAM_B2) * (g * g)
    return -ADAM_LR * ((m2 / c1) / (jnp.sqrt(v2 / c2) + ADAM_EPS) + ADAM_WD * w), m2, v2


def _small_updates(summed, params):
    n = len(params)

    def body(s_ref, *refs):
        ins, outs = refs[:3 * n], refs[3 * n:]
        for p, (_, _, _, pick) in enumerate(params):
            w_ref, m_ref, v_ref = ins[3 * p:3 * p + 3]
            g = pick(s_ref)
            delta, m2, v2 = _adam_math(w_ref[...], g, m_ref[...], v_ref[...])
            for o_ref, val in zip(outs[4 * p:4 * p + 4], (g, delta, m2, v2)):
                o_ref[...] = val

    out = pl.pallas_call(
        body, name="adamw_small",
        out_shape=[jax.ShapeDtypeStruct(w.shape, F32) for w, _, _, _ in params for _ in range(4)],
        compiler_params=pltpu.CompilerParams(vmem_limit_bytes=VMEM_LIMIT),
    )(summed, *[t for w, m, v, _ in params for t in (w, m, v)])
    return [tuple(out[4 * p:4 * p + 4]) for p in range(n)]


def _adamw(w, g, m, v, name):
    rows, cols = w.shape
    br = _row_block(rows, cols)

    def body(w_ref, g_ref, m_ref, v_ref, d_ref, nm_ref, nv_ref):
        d_ref[...], nm_ref[...], nv_ref[...] = _adam_math(w_ref[...], g_ref[...], m_ref[...], v_ref[...])

    blk = pl.BlockSpec((br, cols), lambda i: (i, 0))
    return pl.pallas_call(
        body, name=name, grid=(rows // br,),
        in_specs=[blk] * 4, out_specs=[blk] * 3,
        out_shape=[jax.ShapeDtypeStruct((rows, cols), F32)] * 3,
        compiler_params=_params(("arbitrary",)),
    )(w, g, m, v)


def _all_gather(x_shard, name):
    m_per, n = x_shard.shape

    def body(x_ref, out_ref, send_sems, recv_sems, local_sem):
        x, y, c = _position()
        me, sibling = (x, y, c), (x, y, 1 - c)
        chips = [(1 - x, y), (x, 1 - y), (1 - x, 1 - y)]

        def rows(px, py, pc):
            return out_ref.at[pl.ds((4 * px + 2 * py + pc) * m_per, m_per), :]

        def copy(k, block, to, src=None):
            return pltpu.make_async_remote_copy(
                src_ref=rows(*block) if src is None else src, dst_ref=rows(*block),
                send_sem=send_sems.at[k], recv_sem=recv_sems.at[k], device_id=to, device_id_type=MESH)

        mine = pltpu.make_async_copy(x_ref, rows(*me), local_sem)
        mine.start()
        first = [copy(0, me, sibling, src=x_ref)]
        first += [copy(1 + j, me, (*chip, c), src=x_ref) for j, chip in enumerate(chips)]
        for cp in first:
            cp.start()
        passed = [copy(4 + j, (*chip, c), sibling) for j, chip in enumerate(chips)]
        for j, chip in enumerate(chips):
            copy(1 + j, (*chip, c), me).wait_recv()
            passed[j].start()
        copy(0, sibling, me).wait_recv()
        for j, chip in enumerate(chips):
            copy(4 + j, (*chip, 1 - c), me).wait_recv()
        for cp in first + passed:
            cp.wait_send()
        mine.wait()

    return pl.pallas_call(
        body, name=name,
        out_shape=jax.ShapeDtypeStruct((N_DEV * m_per, n), x_shard.dtype),
        in_specs=[pl.BlockSpec(memory_space=pltpu.VMEM)],
        out_specs=pl.BlockSpec(memory_space=pltpu.VMEM),
        scratch_shapes=[pltpu.SemaphoreType.DMA((7,)), pltpu.SemaphoreType.DMA((7,)), pltpu.SemaphoreType.DMA],
        compiler_params=pltpu.CompilerParams(vmem_limit_bytes=VMEM_LIMIT),
    )(x_shard)


_ANY = pl.BlockSpec(memory_space=pl.ANY)


def _place_quarters(place, quarters):
    steps = 2

    def body(place_ref, *refs):
        n = len(refs) // 2
        for w_ref, o_ref in zip(refs[:n], refs[n:]):
            o_ref[0] = w_ref[...].astype(BF16)

    return pl.pallas_call(
        body, name="place_quarters",
        grid_spec=pltpu.PrefetchScalarGridSpec(
            num_scalar_prefetch=1, grid=(steps,),
            in_specs=[pl.BlockSpec((q.shape[0] // steps, q.shape[1]), lambda r, place_ref: (r, 0)) for q in quarters],
            out_specs=[pl.BlockSpec((1, q.shape[0] // steps, q.shape[1]), lambda r, place_ref: (place_ref[0], r, 0))
                       for q in quarters]),
        out_shape=[jax.ShapeDtypeStruct((N_CHIPS,) + q.shape, BF16) for q in quarters],
        compiler_params=_params(("arbitrary",)),
    )(place, *quarters)


def _gather_weights(placed):
    n = len(placed)
    shapes = [b.shape[1:] for b in placed]

    def body(*refs):
        g_refs = refs[n:2 * n]
        send_sems, recv_sems = refs[2 * n:]
        x, y, c = _position()
        sibling = (x, y, 1 - c)
        chips = [(1 - x, y), (x, 1 - y), (1 - x, 1 - y)]
        mine = 2 * x + y

        def half(a, which):
            hr = shapes[a][0] // 2
            return pl.ds(which * hr, hr)

        def over_ici(a, p, slot):
            ref = g_refs[a].at[slot, half(a, c), :]
            return pltpu.make_async_remote_copy(
                src_ref=ref, dst_ref=ref,
                send_sem=send_sems.at[6 * a + p], recv_sem=recv_sems.at[6 * a + p],
                device_id=(*chips[p], c), device_id_type=MESH)

        def over_d2d(a, p, slot, which):
            ref = g_refs[a].at[slot, half(a, which), :]
            return pltpu.make_async_remote_copy(
                src_ref=ref, dst_ref=ref,
                send_sem=send_sems.at[6 * a + 3 + p], recv_sem=recv_sems.at[6 * a + 3 + p],
                device_id=sibling, device_id_type=MESH)

        sends = []
        for a in range(n):
            for p in range(3):
                cp = over_ici(a, p, mine)
                cp.start()
                sends.append(cp)
        for a in range(n):
            for p, (cx, cy) in enumerate(chips):
                slot = 2 * cx + cy
                over_ici(a, p, slot).wait_recv()
                cp = over_d2d(a, p, slot, c)
                cp.start()
                sends.append(cp)
        for a in range(n):
            for p, (cx, cy) in enumerate(chips):
                over_d2d(a, p, 2 * cx + cy, 1 - c).wait_recv()
        for cp in sends:
            cp.wait_send()

    return pl.pallas_call(
        body, name="gather_weights",
        out_shape=[jax.ShapeDtypeStruct(b.shape, BF16) for b in placed],
        in_specs=[_ANY] * n, out_specs=[_ANY] * n,
        input_output_aliases={a: a for a in range(n)},
        scratch_shapes=[pltpu.SemaphoreType.DMA((6 * n,)), pltpu.SemaphoreType.DMA((6 * n,))],
    )(*placed)


_HBM = pl.BlockSpec(memory_space=pltpu.HBM)
_SEM = pl.BlockSpec(memory_space=pltpu.SEMAPHORE)
_EFFECT = pltpu.SideEffectType.DATAFLOW_SIDE_EFFECTING


def _quarter_halves(shapes, a, which):
    hr = shapes[a][0] // 2
    return pl.ds(which * hr, hr)


def _gather_start(placed, after):
    n = len(placed)
    m = len(after)
    shapes = [b.shape[1:] for b in placed]

    def body(*refs):
        g_refs = refs[:n]
        send_sems, recv_sems = refs[n + m], refs[n + m + 1]
        token = refs[2 * n + m + 2]
        x, y, c = _position()
        chips = [(1 - x, y), (x, 1 - y), (1 - x, 1 - y)]
        mine = 2 * x + y
        for a in range(n):
            ref = g_refs[a].at[mine, _quarter_halves(shapes, a, c), :]
            for p in range(3):
                pltpu.make_async_remote_copy(
                    src_ref=ref, dst_ref=ref, send_sem=send_sems.at[3 * a + p], recv_sem=recv_sems.at[3 * a + p],
                    device_id=(*chips[p], c), device_id_type=MESH).start()
        token[...] = jnp.zeros_like(token)

    out = pl.pallas_call(
        body, name="gather_start",
        out_shape=(pltpu.SemaphoreType.DMA((3 * n,)), pltpu.SemaphoreType.DMA((3 * n,)),
                   *[pltpu.HBM(b.shape, b.dtype) for b in placed], jax.ShapeDtypeStruct((8, LANES), F32)),
        in_specs=[_HBM] * n + [_ANY] * m,
        out_specs=(_SEM, _SEM, *[_HBM] * n, pl.BlockSpec(memory_space=pltpu.VMEM)),
        input_output_aliases={a: 2 + a for a in range(n)},
        compiler_params=pltpu.CompilerParams(has_side_effects=_EFFECT),
    )(*[pltpu.with_memory_space_constraint(b, pltpu.HBM) for b in placed], *after)
    return out[0], out[1], list(out[2:2 + n]), out[2 + n]


def _gather_wait(send_sems, recv_sems, thru, after):
    n = len(thru)
    shapes = [b.shape[1:] for b in thru]

    def body(*refs):
        g_refs = refs[:n]
        send_sems, recv_sems = refs[n], refs[n + 1]
        x, y, c = _position()
        chips = [(1 - x, y), (x, 1 - y), (1 - x, 1 - y)]
        mine = 2 * x + y
        for a in range(n):
            rows = _quarter_halves(shapes, a, c)
            for p, (cx, cy) in enumerate(chips):
                copy = pltpu.make_async_remote_copy(
                    src_ref=g_refs[a].at[mine, rows, :], dst_ref=g_refs[a].at[2 * cx + cy, rows, :],
                    send_sem=send_sems.at[3 * a + p], recv_sem=recv_sems.at[3 * a + p],
                    device_id=(cx, cy, c), device_id_type=MESH)
                copy.wait_send()
                copy.wait_recv()

    return pl.pallas_call(
        body, name="gather_wait",
        out_shape=[pltpu.HBM(b.shape, b.dtype) for b in thru],
        in_specs=[_HBM] * n + [_SEM, _SEM, _ANY], out_specs=[_HBM] * n,
        input_output_aliases={a: a for a in range(n)},
        compiler_params=pltpu.CompilerParams(has_side_effects=_EFFECT),
    )(*thru, send_sems, recv_sems, after)


def _gather_forward(bufs):
    n = len(bufs)
    shapes = [b.shape[1:] for b in bufs]

    def body(*refs):
        g_refs = refs[n:2 * n]
        send_sems, recv_sems = refs[2 * n:]
        x, y, c = _position()
        chips = [(1 - x, y), (x, 1 - y), (1 - x, 1 - y)]

        def over_d2d(a, p, which):
            cx, cy = chips[p]
            ref = g_refs[a].at[2 * cx + cy, _quarter_halves(shapes, a, which), :]
            return pltpu.make_async_remote_copy(
                src_ref=ref, dst_ref=ref, send_sem=send_sems.at[3 * a + p], recv_sem=recv_sems.at[3 * a + p],
                device_id=(x, y, 1 - c), device_id_type=MESH)

        sends = [over_d2d(a, p, c) for a in range(n) for p in range(3)]
        for cp in sends:
            cp.start()
        for a in range(n):
            for p in range(3):
                over_d2d(a, p, 1 - c).wait_recv()
        for cp in sends:
            cp.wait_send()

    return pl.pallas_call(
        body, name="gather_forward",
        out_shape=[jax.ShapeDtypeStruct(b.shape, BF16) for b in bufs],
        in_specs=[_ANY] * n, out_specs=[_ANY] * n,
        input_output_aliases={a: a for a in range(n)},
        scratch_shapes=[pltpu.SemaphoreType.DMA((3 * n,)), pltpu.SemaphoreType.DMA((3 * n,))],
    )(*bufs)


def _sibling_exchange(grads, tag):
    n = len(grads)
    shapes = [g.shape for g in grads]

    def body(*refs):
        g_refs, x_refs = refs[:n], refs[n:2 * n]
        send_sems, recv_sems = refs[2 * n:]
        x, y, c = _position()
        copies = []
        for a in range(n):
            hr = shapes[a][1] // 2
            cp = pltpu.make_async_remote_copy(
                src_ref=g_refs[a].at[:, pl.ds((1 - c) * hr, hr), :], dst_ref=x_refs[a],
                send_sem=send_sems.at[a], recv_sem=recv_sems.at[a],
                device_id=(x, y, 1 - c), device_id_type=MESH)
            cp.start()
            copies.append(cp)
        for cp in copies:
            cp.wait()

    return pl.pallas_call(
        body, name="grad_sibling_exchange_" + tag,
        out_shape=[jax.ShapeDtypeStruct((g.shape[0], g.shape[1] // 2, g.shape[2]), g.dtype) for g in grads],
        in_specs=[_ANY] * n, out_specs=[_ANY] * n,
        scratch_shapes=[pltpu.SemaphoreType.DMA((n,)), pltpu.SemaphoreType.DMA((n,))],
    )(*grads)


def _chip_sums(core, grads, theirs, tag):
    n = len(grads)

    def body(core_ref, *refs):
        g_refs, t_refs, o_refs = refs[:n], refs[n:2 * n], refs[2 * n:]
        for g_ref, t_ref, o_ref in zip(g_refs, t_refs, o_refs):
            o_ref[...] = (g_ref[...] + t_ref[...].astype(F32)).astype(BF16)

    in_specs = [pl.BlockSpec((1, g.shape[1] // 2, g.shape[2]), lambda k, core_ref: (k, core_ref[0], 0)) for g in grads]
    in_specs += [pl.BlockSpec((1,) + t.shape[1:], lambda k, core_ref: (k, 0, 0)) for t in theirs]
    return pl.pallas_call(
        body, name="grad_chip_sums_" + tag,
        grid_spec=pltpu.PrefetchScalarGridSpec(
            num_scalar_prefetch=1, grid=(N_CHIPS,), in_specs=in_specs,
            out_specs=[pl.BlockSpec((1,) + t.shape[1:], lambda k, core_ref: (k, 0, 0)) for t in theirs]),
        out_shape=[jax.ShapeDtypeStruct(t.shape, BF16) for t in theirs],
        compiler_params=_params(("arbitrary",)),
    )(core, *grads, *theirs)


def _chip_exchange_start(sums, after, tag):
    n = len(sums)
    m = len(after)
    lands = [lax.empty((3,) + s.shape[1:], BF16) for s in sums]

    def body(*refs):
        s_refs, y_refs = refs[:n], refs[n:2 * n]
        send_sems, recv_sems = refs[2 * n + m], refs[2 * n + m + 1]
        token = refs[4 * n + m + 2]
        x, y, c = _position()
        chips = [(1 - x, y), (x, 1 - y), (1 - x, 1 - y)]
        for a in range(n):
            for p, (cx, cy) in enumerate(chips):
                pltpu.make_async_remote_copy(
                    src_ref=s_refs[a].at[2 * cx + cy], dst_ref=y_refs[a].at[p],
                    send_sem=send_sems.at[3 * a + p], recv_sem=recv_sems.at[3 * a + p],
                    device_id=(cx, cy, c), device_id_type=MESH).start()
        token[...] = jnp.zeros_like(token)

    both = list(sums) + lands
    out = pl.pallas_call(
        body, name="grad_chip_exchange_start_" + tag,
        out_shape=(pltpu.SemaphoreType.DMA((3 * n,)), pltpu.SemaphoreType.DMA((3 * n,)),
                   *[pltpu.HBM(b.shape, b.dtype) for b in both], jax.ShapeDtypeStruct((8, LANES), F32)),
        in_specs=[_HBM] * (2 * n) + [_ANY] * m,
        out_specs=(_SEM, _SEM, *[_HBM] * (2 * n), pl.BlockSpec(memory_space=pltpu.VMEM)),
        input_output_aliases={a: 2 + a for a in range(2 * n)},
        compiler_params=pltpu.CompilerParams(has_side_effects=_EFFECT),
    )(*[pltpu.with_memory_space_constraint(b, pltpu.HBM) for b in both], *after)
    return out[0], out[1], list(out[2:2 + n]), list(out[2 + n:2 + 2 * n]), out[2 + 2 * n]


def _chip_exchange_wait(send_sems, recv_sems, sums, lands, after, tag):
    n = len(sums)

    def body(*refs):
        s_refs, y_refs = refs[:n], refs[n:2 * n]
        send_sems, recv_sems = refs[2 * n], refs[2 * n + 1]
        x, y, c = _position()
        chips = [(1 - x, y), (x, 1 - y), (1 - x, 1 - y)]
        for a in range(n):
            for p, (cx, cy) in enumerate(chips):
                copy = pltpu.make_async_remote_copy(
                    src_ref=s_refs[a].at[2 * cx + cy], dst_ref=y_refs[a].at[p],
                    send_sem=send_sems.at[3 * a + p], recv_sem=recv_sems.at[3 * a + p],
                    device_id=(cx, cy, c), device_id_type=MESH)
                copy.wait_send()
                copy.wait_recv()

    both = list(sums) + list(lands)
    out = pl.pallas_call(
        body, name="grad_chip_exchange_wait_" + tag,
        out_shape=[pltpu.HBM(b.shape, b.dtype) for b in both],
        in_specs=[_HBM] * (2 * n) + [_SEM, _SEM, _ANY], out_specs=[_HBM] * (2 * n),
        input_output_aliases={a: a for a in range(2 * n)},
        compiler_params=pltpu.CompilerParams(has_side_effects=_EFFECT),
    )(*both, send_sems, recv_sems, after)
    return list(out[:n]), list(out[n:])


def _total_sums(place, sums, parts, after, tag):
    n = len(parts)
    m = len(after)
    steps = 2

    def body(place_ref, *refs):
        for s_ref, y_ref, o_ref in zip(refs[:n], refs[n:2 * n], refs[2 * n + m:]):
            o_ref[0] = ((s_ref[0].astype(F32) + y_ref[0].astype(F32)) + y_ref[1].astype(F32)) + y_ref[2].astype(F32)

    def step_rows(pt):
        return pt.shape[1] // steps

    in_specs = [pl.BlockSpec((1, step_rows(s), s.shape[2]), lambda r, place_ref: (place_ref[0], r, 0)) for s in sums]
    in_specs += [pl.BlockSpec((3, step_rows(pt), pt.shape[2]), lambda r, place_ref: (0, r, 0)) for pt in parts]
    in_specs += [_ANY] * m
    return pl.pallas_call(
        body, name="grad_total_sums_" + tag,
        grid_spec=pltpu.PrefetchScalarGridSpec(
            num_scalar_prefetch=1, grid=(steps,), in_specs=in_specs,
            out_specs=[pl.BlockSpec((1, step_rows(pt), pt.shape[2]), lambda r, place_ref: (place_ref[1], r, 0))
                       for pt in parts]),
        out_shape=[jax.ShapeDtypeStruct((2,) + pt.shape[1:], F32) for pt in parts],
        compiler_params=_params(("arbitrary",)),
    )(place, *sums, *parts, *after)


def _sibling_share(halves, tag):
    n = len(halves)

    def body(*refs):
        f_refs = refs[n:2 * n]
        send_sems, recv_sems = refs[2 * n:]
        x, y, c = _position()
        copies = []
        for a in range(n):
            cp = pltpu.make_async_remote_copy(
                src_ref=f_refs[a].at[c], dst_ref=f_refs[a].at[c], send_sem=send_sems.at[a], recv_sem=recv_sems.at[a],
                device_id=(x, y, 1 - c), device_id_type=MESH)
            cp.start()
            copies.append(cp)
        for a, cp in enumerate(copies):
            cp.wait_send()
            pltpu.make_async_remote_copy(
                src_ref=f_refs[a].at[1 - c], dst_ref=f_refs[a].at[1 - c], send_sem=send_sems.at[a],
                recv_sem=recv_sems.at[a], device_id=(x, y, c), device_id_type=MESH).wait_recv()

    return pl.pallas_call(
        body, name="grad_sibling_share_" + tag,
        out_shape=[jax.ShapeDtypeStruct(h.shape, F32) for h in halves],
        in_specs=[_ANY] * n, out_specs=[_ANY] * n,
        input_output_aliases={a: a for a in range(n)},
        scratch_shapes=[pltpu.SemaphoreType.DMA((n,)), pltpu.SemaphoreType.DMA((n,))],
    )(*halves)


def _group_sum(stacked, nrow, name):
    total, n = stacked.shape
    groups = total // nrow

    def body(g_ref, o_ref):
        acc = g_ref[0:nrow, :]
        for grp in range(1, groups):
            acc = acc + g_ref[grp * nrow:(grp + 1) * nrow, :]
        o_ref[...] = acc

    return pl.pallas_call(
        body, name=name,
        out_shape=jax.ShapeDtypeStruct((nrow, n), F32),
        compiler_params=pltpu.CompilerParams(vmem_limit_bytes=VMEM_LIMIT),
    )(stacked)


def _local_step(xt, tgt, mod, gains, w_pool, pool_scale, w_in, later_weights, on_ffn_grads, seq):
    g_mpre, g_mpost, g_fpre, g_fpost = gains
    d = xt.shape[1]
    tm, tq = min(TOKEN_TILE, seq), min(ATTN_TILE, seq)

    h1, qn, k, v, u, kt, vt = _prenorm_proj(xt, mod, g_mpre, w_in, seq, tm)
    tk = min(ATTN_KEY_TILE, tq // 2)
    o, ltot = _attn_fwd(qn, k, vt, seq, tq, tk)
    w_out, w_g, w_u, w_d = later_weights(o)
    w_out2 = w_out.reshape(d, d)
    pooled, mixin, mix, x1, h2 =_mixer_post(u, o, xt, mod, g_mpost, g_fpre, w_pool, pool_scale, w_out2, seq, tm)
    a, b, fin, dy, df, loss_blk, accb4, accg4 = _ffn_fwd(h2, w_g, w_u, w_d, x1, tgt, mod, g_fpost, seq, tm)
    da, db, dx1, dmix, accb5, accg5 = _ffn_bwd(df, a, b, w_d, w_g, w_u, x1, dy, mix, mod, g_fpre, g_mpost, seq, tm)
    bt = min(GRAD_TOKEN_TILE, xt.shape[0])
    bt_one = min(2 * GRAD_TOKEN_TILE, xt.shape[0])
    (g_g,), (g_g16,) = _tn_matmul(da, [h2], w_g.shape[0], bt_one, "grad_w_gate")
    (g_u,), (g_u16,) = _tn_matmul(db, [h2], w_u.shape[0], bt_one, "grad_w_up")
    (g_d,), (g_d16,) = _tn_matmul(fin, [df], w_d.shape[0], bt_one, "grad_w_down")
    token = on_ffn_grads([g_g, g_u, g_d], [g_g16, g_u16, g_d16])
    do, dpd, dps, dwp = _mixer_bwd(dmix, w_out2, pooled, w_pool, pool_scale + token, seq, tm)
    dq, dk, dv = _attn_bwd(qn, k, kt, v, do, ltot, seq, tq, tk)
    gx, du, accb8, accg8 = _inproj_bwd(dq, dk, dv, dpd, xt, dx1, mod, g_mpre, w_in, seq, tm)

    g_in, g_in16 = _tn_matmul_stacked(h1, [dq, dk, dv, du], bt, "grad_w_in")
    g_out, g_out16 = [parts[0].reshape(w_out.shape) for parts in _tn_matmul(mixin, [dmix], 1, bt_one, "grad_w_out")]

    dmod = jnp.stack([accb8[:, 0], accb8[:, 1], accb5[:, 2], accb5[:, 0], accb5[:, 1], accb4[:, 0]], axis=1)
    dgain = jnp.stack([accg8[0], accg5[1], accg5[0], accg4[0]], axis=0)
    grads = [g_in, g_out, g_g, g_u, g_d]
    grads16 = [g_in16, g_out16, g_g16, g_u16, g_d16]
    return loss_blk, gx, grads, grads16, dmod, dgain, dps[0:1], dwp


def kernel(x, c, w_cond, b_cond, g_mix_pre, g_mix_post, w_in, w_pool, pool_scale, w_out, g_ffn_pre, g_ffn_post, w_gate, w_up, w_down, loss_target, m_w_cond, m_b_cond, m_g_mix_pre, m_g_mix_post, m_w_in, m_w_pool, m_pool_scale, m_w_out, m_g_ffn_pre, m_g_ffn_post, m_w_gate, m_w_up, m_w_down, v_w_cond, v_b_cond, v_g_mix_pre, v_g_mix_post, v_w_in, v_w_pool, v_pool_scale, v_w_out, v_g_ffn_pre, v_g_ffn_post, v_w_gate, v_w_up, v_w_down):
    xi, yi, ci = _position()
    chip = 2 * xi + yi
    dev = 4 * xi + 2 * yi + ci
    nb, seq, d = x.shape
    t_all = nb * seq
    xt = x.reshape(t_all, d)
    tgt = loss_target.reshape(t_all, d)
    ncol = w_cond.shape[2]
    pw = pool_scale.shape[1]

    c_pad = jnp.concatenate([c, jnp.zeros((8 - nb, d), F32)], axis=0)
    c_all = _all_gather(c_pad, "gather_c").reshape(N_DEV, 8, d)[:, :nb].reshape(N_DEV * nb, d)
    b_q = lax.dynamic_slice(b_cond, (0, chip * ncol), (1, ncol))
    sc_all, mod_q = _cond_fwd(c_all, w_cond[0], b_q, 512)
    mod_parts = _all_gather(mod_q, "gather_mod").reshape(N_DEV, N_DEV * nb, ncol)
    mod_rows = lax.dynamic_slice(mod_parts, (0, dev * nb, 0), (N_DEV, nb, ncol))[0::2]
    mod = jnp.transpose(mod_rows, (1, 0, 2)).reshape(nb, N_MOD, d)
    mod = jnp.concatenate([mod, jnp.zeros((nb, MOD_ROWS - N_MOD, d), F32)], axis=1)

    place = jnp.stack([chip, ci]).astype(jnp.int32)
    turned = lambda t: jnp.swapaxes(t[0], 0, 1)
    placed = _place_quarters(place, [w_in[0], w_out[0], turned(w_gate), turned(w_up), w_down[0]])
    (w_in_all,) = _gather_weights(placed[:1])
    send_sems, recv_sems, in_flight, token = _gather_start(placed[1:], [mod, w_in_all])
    mod = mod + token[0:1, 0:1]

    def later_weights(after):
        return _gather_forward(_gather_wait(send_sems, recv_sems, in_flight, after))

    ffn_split = []

    def on_ffn_grads(ffn_grads, ffn_grads16):
        theirs = _sibling_exchange(ffn_grads16, "ffn")
        ffn_split.extend(_chip_exchange_start(_chip_sums(place[1:], ffn_grads, theirs, "ffn"), [], "ffn"))
        return ffn_split[4][0:1, 0:1]

    gains = (g_mix_pre, g_mix_post, g_ffn_pre, g_ffn_post)
    loss_blk, gx, grads, grads16, dmod, dgain, dps, dwp = _local_step(
        xt, tgt, mod, gains, w_pool[0], pool_scale, w_in_all, later_weights, on_ffn_grads, seq)

    sums_ffn, parts_ffn = _chip_exchange_wait(*ffn_split[:4], gx, "ffn")

    wp_rows = dwp.size // d
    loss_row = 2 * N_MOD + 4 + 1
    pad_rows = 24 - (loss_row + 1)
    payload = jnp.concatenate([
        dmod.reshape(nb * N_MOD, d), dgain,
        jnp.concatenate([dps, jnp.zeros((1, d - pw), F32)], axis=1),
        jnp.concatenate([loss_blk[0:1], jnp.zeros((1, d - LANES), F32)], axis=1),
        jnp.zeros((pad_rows, d), F32),
        jnp.concatenate(jnp.split(dwp.reshape(-1, dwp.shape[-1]), d // dwp.shape[-1], axis=0), axis=1)], axis=0)
    prow = payload.shape[0]
    gathered = _all_gather(payload, "gather_small")
    summed = _group_sum(gathered, prow, "small_device_sum")
    loss = summed[loss_row, 0]
    dmod_all = gathered.reshape(N_DEV, prow, d)[:, :nb * N_MOD].reshape(N_DEV * nb, N_MOD * d)
    dmod_q = lax.dynamic_slice(dmod_all, (0, chip * ncol), (N_DEV * nb, ncol))
    g_w_cond = _cond_bwd(sc_all, dmod_q, 512)
    first_gain = 2 * N_MOD

    theirs = _sibling_exchange(grads16[:2], "mix")
    mix_split = _chip_exchange_start(_chip_sums(place[1:], grads[:2], theirs, "mix"), [gathered], "mix")
    unfold = lambda halves: [g.reshape(2 * g.shape[1], g.shape[2]) for g in halves]
    g_ffn = unfold(_sibling_share(_total_sums(place, sums_ffn, parts_ffn, [mix_split[4]], "ffn"), "ffn"))

    results = {}

    def update(name, w2, g2, m2, v2, shape):
        delta, new_m, new_v = _adamw(w2, g2, m2, v2, "adamw_" + name)
        back = (lambda t: jnp.swapaxes(t, 0, 1)[None]) if shape is None else (lambda t: t.reshape(shape))
        results[name] = [back(t) for t in (g2, delta, new_m, new_v)]
        return delta

    done = [update("w_gate", turned(w_gate), g_ffn[0], turned(m_w_gate), turned(v_w_gate), None),
            update("w_up", turned(w_up), g_ffn[1], turned(m_w_up), turned(v_w_up), None),
            update("w_down", w_down[0], g_ffn[2], m_w_down[0], v_w_down[0], w_down.shape),
            update("w_cond", w_cond[0], g_w_cond, m_w_cond[0], v_w_cond[0], w_cond.shape)]

    gain_row = lambda r: (lambda s: s[first_gain + r:first_gain + r + 1, :])
    small = [
        ("b_cond", (b_cond, m_b_cond, v_b_cond), (N_MOD, d), lambda s: s[0:N_MOD, :] + s[N_MOD:2 * N_MOD, :]),
        ("g_mix_pre", (g_mix_pre, m_g_mix_pre, v_g_mix_pre), (1, d), gain_row(0)),
        ("g_mix_post", (g_mix_post, m_g_mix_post, v_g_mix_post), (1, d), gain_row(1)),
        ("g_ffn_pre", (g_ffn_pre, m_g_ffn_pre, v_g_ffn_pre), (1, d), gain_row(2)),
        ("g_ffn_post", (g_ffn_post, m_g_ffn_post, v_g_ffn_post), (1, d), gain_row(3)),
        ("pool_scale", (pool_scale, m_pool_scale, v_pool_scale), (1, pw),
         lambda s: s[first_gain + 4:first_gain + 5, 0:pw]),
        ("w_pool", (w_pool, m_w_pool, v_w_pool), (wp_rows * d // w_pool.shape[-1], w_pool.shape[-1]),
         lambda s: jnp.concatenate([s[24:24 + wp_rows, j * w_pool.shape[-1]:(j + 1) * w_pool.shape[-1]]
                                    for j in range(d // w_pool.shape[-1])], axis=0)),
    ]
    updated = _small_updates(summed, [tuple(t.reshape(flat) for t in wmv) + (pick,) for _, wmv, flat, pick in small])
    for (name, wmv, _, _), quad in zip(small, updated):
        results[name] = [t.reshape(wmv[0].shape) for t in quad]

    sums_mix, parts_mix = _chip_exchange_wait(*mix_split[:4], done[-1], "mix")
    g_mix = unfold(_sibling_share(_total_sums(place, sums_mix, parts_mix, done[:3], "mix"), "mix"))
    update("w_in", w_in[0], g_mix[0], m_w_in[0], v_w_in[0], w_in.shape)
    update("w_out", w_out[0], g_mix[1], m_w_out[0], v_w_out[0], w_out.shape)

    names = ("w_cond", "b_cond", "g_mix_pre", "g_mix_post", "w_in", "w_pool", "pool_scale", "w_out",
             "g_ffn_pre", "g_ffn_post", "w_gate", "w_up", "w_down")
    outs = [results[name][part] for part in range(4) for name in names]
    return (loss, gx.reshape(x.shape), *outs)
```

```python
import jax
import jax.numpy as jnp
import numpy as np
from jax import lax
from jax.experimental import pallas as pl
from jax.experimental.pallas import tpu as pltpu

F32 = jnp.float32
BF16 = jnp.bfloat16
MESH = pl.DeviceIdType.MESH

EPS = 1e-6
HEAD_DIM = 64
HEADS_PER_BLOCK = 2
LANES = 128
NEG_QK_SCALE = -0.125
POOL_WINDOWS = (2, 4, 8, 16)
POOL_GROUP = 128
HALO = 16
N_MOD = 6
MOD_ROWS = 8
N_CHIPS = 4
N_DEV = 8
VMEM_LIMIT = 56 * 1024 * 1024

ADAM_LR = 0.001
ADAM_B1 = 0.9
ADAM_B2 = 0.999
ADAM_EPS = 1e-08
ADAM_WD = 0.01
ADAM_STEP = 10

TOKEN_TILE = 512
GRAD_TOKEN_TILE = 2048
FFN_ROW_CHUNKS = 2
ROW_CHUNKS = 2
ATTN_TILE = 512
ATTN_KEY_TILE = 256
ATTN_ROW_CHUNK = 32
LOG_SUM_PASSES = 1


def _dot(a, b):
    return jnp.dot(a, b, preferred_element_type=F32)


def _dot_nt(a, b):
    return lax.dot_general(a, b, (((1,), (1,)), ((), ())), preferred_element_type=F32)


def _dot_tn(a, b):
    return lax.dot_general(a, b, (((0,), (0,)), ((), ())), preferred_element_type=F32)


def _split(v):
    hi = v.astype(BF16)
    lo = (v - hi.astype(F32)).astype(BF16)
    return hi, lo


def _rms(v):
    return lax.rsqrt(jnp.mean(v * v, axis=-1, keepdims=True) + EPS)


def _norm_bwd(dn, n, r):
    return r * (dn - n * jnp.mean(dn * n, axis=-1, keepdims=True))


def _sigmoid(v):
    return 0.5 * jnp.tanh(0.5 * v) + 0.5


def _colsum(v):
    return jnp.sum(v, axis=0, keepdims=True)


def _params(sem=None):
    return pltpu.CompilerParams(dimension_semantics=sem, vmem_limit_bytes=VMEM_LIMIT)


def _position():
    return lax.axis_index("x"), lax.axis_index("y"), lax.axis_index("c")


def _prenorm_proj(x, mod, g_pre, w_in, seq, tm):
    t_all, d = x.shape
    nt = seq // tm
    p = w_in.shape[2]

    def body(x_ref, mod_ref, g_ref, w_ref, h_ref, q_ref, k_ref, v_ref, u_ref, kt_ref, vt_ref):
        for c in range(ROW_CHUNKS):
            rows = slice(c * (tm // ROW_CHUNKS), (c + 1) * (tm // ROW_CHUNKS))
            xf = x_ref[rows, :]
            n = xf * _rms(xf)
            h = (n * g_ref[...]) * (1.0 + mod_ref[0, 1:2, :]) + mod_ref[0, 0:1, :]
            hb = h.astype(BF16)
            h_ref[rows, :] = hb
            q_ref[rows, :] = (_dot(hb, w_ref[0]) * NEG_QK_SCALE).astype(BF16)
            kf = _dot(hb, w_ref[1])
            vf = _dot(hb, w_ref[2])
            k_ref[rows, :] = kf.astype(BF16)
            v_ref[rows, :] = vf.astype(BF16)
            kt_ref[:, rows] = kf.T.astype(BF16)
            vt_ref[:, rows] = vf.T.astype(BF16)
            u_ref[rows, :] = _dot(hb, w_ref[3])

    tok = lambda i: (i, 0)
    tok_t = lambda i: (0, i)
    return pl.pallas_call(
        body, name="prenorm_proj", grid=(t_all // tm,),
        in_specs=[pl.BlockSpec((tm, d), tok),
                  pl.BlockSpec((1, MOD_ROWS, d), lambda i: (i // nt, 0, 0)),
                  pl.BlockSpec((1, d), lambda i: (0, 0)),
                  pl.BlockSpec((N_CHIPS, d, p), lambda i: (0, 0, 0))],
        out_specs=[pl.BlockSpec((tm, d), tok)] + [pl.BlockSpec((tm, p), tok)] * 4 + [pl.BlockSpec((p, tm), tok_t)] * 2,
        out_shape=[jax.ShapeDtypeStruct((t_all, d), BF16)] + [jax.ShapeDtypeStruct((t_all, p), BF16)] * 3
        + [jax.ShapeDtypeStruct((t_all, p), F32)] + [jax.ShapeDtypeStruct((p, t_all), BF16)] * 2,
        compiler_params=_params(("arbitrary",)),
    )(x, mod, g_pre, w_in)


def _tri_matrix(tk, kind):
    j = np.arange(2 * tk)[:, None] % tk
    s = np.arange(tk)[None, :]
    return jnp.asarray({"after": j > s, "upto": j <= s, "before": j < s}[kind], dtype=BF16)


def _neg_abs(v):
    bits = lax.bitcast_convert_type(v, jnp.int32) | jnp.int32(-2 ** 31)
    return lax.bitcast_convert_type(bits, F32)


def _row_sums(v):
    return jnp.broadcast_to(jnp.sum(v, axis=-1, keepdims=True), (v.shape[0], LANES))


def _across(v, n):
    return jnp.concatenate([v] * (n // LANES), axis=1)


def _all_masked(c, diag, rc, tk):
    return diag is not None and diag * tk >= (c + 1) * rc - 1


def _some_masked(c, diag, rc, tk):
    return diag is not None and diag * tk + tk - 1 >= c * rc


def _attn_fwd(qn, k, vt, seq, tq, tk):
    t_all, w = qn.shape
    nb, nq, ndiag = t_all // seq, seq // tq, tq // tk
    assert ndiag % 2 == 0, "two key blocks per loop trip"
    rc = ATTN_ROW_CHUNK
    heads = range(HEADS_PER_BLOCK)

    def body(q_ref, k_ref, vt_ref, tri_ref, o_ref, l_ref,
             z_buf, ls_buf, hl_buf, aft_buf, w_buf, tot_buf, acc_t, run_buf):
        i = pl.program_id(2)
        nblk = (i + 1) * ndiag
        lane = lax.broadcasted_iota(jnp.int32, (1, LANES), 1)
        row = lax.broadcasted_iota(jnp.int32, (rc, tk), 0)
        col = lax.broadcasted_iota(jnp.int32, (rc, tk), 1)
        first = lane < HEAD_DIM
        q2 = q_ref[...]
        qs = [jnp.where(first, q2, jnp.zeros_like(q2)), jnp.where(first, jnp.zeros_like(q2), q2)]
        acc_t[...] = jnp.zeros_like(acc_t)
        run_buf[...] = jnp.zeros_like(run_buf)
        w_buf[1] = jnp.zeros((HEADS_PER_BLOCK, tq, tk), BF16)

        def causal(c, diag):
            return (col + diag * tk) < (row + c * rc)

        def scores(blk, slot):
            kj = k_ref[pl.ds(pl.multiple_of(blk * tk, tk), tk), :]
            for h in heads:
                z_buf[slot, h] = _dot_nt(qs[h], kj)

        def values(blk, slot):
            keys = pl.ds(pl.multiple_of(blk * tk, tk), tk)
            for h in heads:
                dims = slice(h * HEAD_DIM, (h + 1) * HEAD_DIM)
                acc_t[dims, :] += _dot_nt(vt_ref[dims, keys], w_buf[slot, h])

        def softplus_stage(h, slot, diag):
            for c in range(tq // rc):
                rows = slice(c * rc, (c + 1) * rc)
                if _all_masked(c, diag, rc, tk):
                    hl_buf[h, rows, :] = jnp.zeros((rc, LOG_SUM_PASSES * tk), BF16)
                    tot_buf[h, rows, :] = jnp.zeros((rc, LANES), F32)
                    continue
                nz = z_buf[slot, h, rows, :]
                l1 = jnp.minimum(nz, 0.0) - jnp.log(1.0 + jnp.exp(_neg_abs(nz)))
                if _some_masked(c, diag, rc, tk):
                    l1 = jnp.where(causal(c, diag), l1, 0.0)
                for s, part in enumerate(_split(l1)[:LOG_SUM_PASSES]):
                    hl_buf[h, rows, s * tk:(s + 1) * tk] = part
                ls_buf[h, rows, :] = l1 - nz
                tot_buf[h, rows, :] = _row_sums(l1)

        def weights_stage(h, slot, diag):
            for c in range(tq // rc):
                rows = slice(c * rc, (c + 1) * rc)
                if _all_masked(c, diag, rc, tk):
                    w_buf[slot, h, rows, :] = jnp.zeros((rc, tk), BF16)
                    continue
                wgt = jnp.exp((ls_buf[h, rows, :] + aft_buf[h, rows, :]) + _across(run_buf[h, rows, :], tk))
                if _some_masked(c, diag, rc, tk):
                    wgt = jnp.where(causal(c, diag), wgt, 0.0)
                w_buf[slot, h, rows, :] = wgt.astype(BF16)
                run_buf[h, rows, :] += tot_buf[h, rows, :]

        def position(blk, slot, diag):
            scores(jnp.maximum(blk - 1, 0), 1 - slot)
            for h in heads:
                softplus_stage(h, slot, diag)
                aft_buf[h] = _dot(hl_buf[h], tri_ref[...])
            values(jnp.minimum(blk + 1, nblk - 1), 1 - slot)
            for h in heads:
                weights_stage(h, slot, diag)

        scores(nblk - 1, 0)
        for p in range(ndiag):
            position(nblk - 1 - p, p % 2, ndiag - 1 - p)

        def trip(jj, carry):
            for u in range(2):
                position(i * ndiag - 1 - 2 * jj - u, u, None)
            return carry

        lax.fori_loop(0, (i * ndiag) // 2, trip, 0)
        values(0, 1)
        o_ref[...] = acc_t[...].T.astype(BF16)
        l_ref[...] = jnp.where(first, run_buf[0], run_buf[1])

    qmap = lambda b, hp, i: (b * nq + i, hp)
    nh = HEADS_PER_BLOCK
    return pl.pallas_call(
        body, name="attn_fwd", grid=(nb, w // LANES, nq),
        in_specs=[pl.BlockSpec((tq, LANES), qmap), pl.BlockSpec((seq, LANES), lambda b, hp, i: (b, hp)),
                  pl.BlockSpec((LANES, seq), lambda b, hp, i: (hp, b)),
                  pl.BlockSpec((LOG_SUM_PASSES * tk, tk), lambda b, hp, i: (0, 0))],
        out_specs=[pl.BlockSpec((tq, LANES), qmap), pl.BlockSpec((tq, LANES), qmap)],
        out_shape=[jax.ShapeDtypeStruct((t_all, w), BF16), jax.ShapeDtypeStruct((t_all, w), F32)],
        scratch_shapes=[pltpu.VMEM((2, nh, tq, tk), F32), pltpu.VMEM((nh, tq, tk), F32),
                        pltpu.VMEM((nh, tq, LOG_SUM_PASSES * tk), BF16), pltpu.VMEM((nh, tq, tk), F32),
                        pltpu.VMEM((2, nh, tq, tk), BF16), pltpu.VMEM((nh, tq, LANES), F32),
                        pltpu.VMEM((LANES, tq), F32), pltpu.VMEM((nh, tq, LANES), F32)],
        compiler_params=_params(("arbitrary", "arbitrary", "arbitrary")),
    )(qn, k, vt, _tri_matrix(tk, "after")[:LOG_SUM_PASSES * tk])


def _window_sums(ext, rows, offset, forward):
    r = lax.broadcasted_iota(jnp.int32, (rows, rows + HALO), 0)
    e = lax.broadcasted_iota(jnp.int32, (rows, rows + HALO), 1)
    hi, lo = _split(ext)
    out = []
    for g, win in enumerate(POOL_WINDOWS):
        if forward:
            band = (e >= r) & (e < r + win)
        else:
            band = (e <= r + offset) & (e > r + offset - win)
        bm = band.astype(BF16)
        cols = slice(g * POOL_GROUP, (g + 1) * POOL_GROUP)
        out.append(_dot(bm, hi[:, cols]) + _dot(bm, lo[:, cols]))
    return out


def _window_counts(pos):
    return [jnp.minimum(pos + 1, win).astype(F32) for win in POOL_WINDOWS]


def _mixer_post(u, o, x, mod, g_post, g_fpre, w_pool, pool_scale, w_out, seq, tm):
    t_all, d = x.shape
    nt = seq // tm
    p = u.shape[1]

    def body(u_ref, halo_ref, o_ref, x_ref, mod_ref, gp_ref, gf_ref, wp_ref, ps_ref, wo_ref,
             pooled_ref, mixin_ref, mix_ref, x1_ref, h2_ref):
        it = pl.program_id(0) % nt
        uf = u_ref[...]
        halo = jnp.where(it == 0, 0.0, halo_ref[...])
        ext = jnp.concatenate([halo, uf], axis=0)
        pos = it * tm + lax.broadcasted_iota(jnp.int32, (tm, 1), 0)
        sums = _window_sums(ext, tm, HALO, False)
        cnts = _window_counts(pos)
        pools = []
        for g in range(len(POOL_WINDOWS)):
            cols = slice(g * POOL_GROUP, (g + 1) * POOL_GROUP)
            pooled = (sums[g] / cnts[g] - uf[:, cols]).astype(BF16)
            pooled_ref[:, cols] = pooled
            yg = _dot(pooled, wp_ref[g].astype(BF16))
            pools.append((yg * ps_ref[:, cols]).astype(BF16))
        mixin_ref[...] = jnp.concatenate([o_ref[...]] + pools, axis=1)
        for c in range(ROW_CHUNKS):
            rows = slice(c * (tm // ROW_CHUNKS), (c + 1) * (tm // ROW_CHUNKS))
            mix = _dot(mixin_ref[rows, :], wo_ref[...])
            mix_ref[rows, :] = mix
            n2 = mix * _rms(mix)
            x1 = x_ref[rows, :] + mod_ref[0, 2:3, :] * (n2 * gp_ref[...])
            x1_ref[rows, :] = x1
            n3 = x1 * _rms(x1)
            h2 = (n3 * gf_ref[...]) * (1.0 + mod_ref[0, 4:5, :]) + mod_ref[0, 3:4, :]
            h2_ref[rows, :] = h2.astype(BF16)

    tok = lambda i: (i, 0)
    const2 = lambda i: (0, 0)
    hb = tm // HALO
    return pl.pallas_call(
        body, name="mixer_post", grid=(t_all // tm,),
        in_specs=[pl.BlockSpec((tm, p), tok),
                  pl.BlockSpec((HALO, p), lambda i: (jnp.maximum(i * hb - 1, 0), 0)),
                  pl.BlockSpec((tm, p), tok),
                  pl.BlockSpec((tm, d), tok),
                  pl.BlockSpec((1, MOD_ROWS, d), lambda i: (i // nt, 0, 0)),
                  pl.BlockSpec((1, d), const2), pl.BlockSpec((1, d), const2),
                  pl.BlockSpec(w_pool.shape, lambda i: (0, 0, 0)),
                  pl.BlockSpec((1, p), const2),
                  pl.BlockSpec((d, d), const2)],
        out_specs=[pl.BlockSpec((tm, p), tok), pl.BlockSpec((tm, d), tok), pl.BlockSpec((tm, d), tok),
                   pl.BlockSpec((tm, d), tok), pl.BlockSpec((tm, d), tok)],
        out_shape=[jax.ShapeDtypeStruct((t_all, p), BF16), jax.ShapeDtypeStruct((t_all, d), BF16),
                   jax.ShapeDtypeStruct((t_all, d), F32), jax.ShapeDtypeStruct((t_all, d), F32),
                   jax.ShapeDtypeStruct((t_all, d), BF16)],
        compiler_params=_params(("arbitrary",)),
    )(u, u, o, x, mod, g_post, g_fpre, w_pool, pool_scale, w_out)


def _ffn_fwd(h2, w_g, w_u, w_d, x1, tgt, mod, g_post, seq, tm):
    t_all, d = x1.shape
    nt = seq // tm
    nk, ff, _ = w_g.shape

    def body(h_ref, wg_ref, wu_ref, wd_ref, x1_ref, t_ref, mod_ref, g_ref,
             a_ref, b_ref, fin_ref, dy_ref, df_ref, loss_ref, accb_ref, accg_ref, facc):
        i, k = pl.program_id(0), pl.program_id(1)

        @pl.when(k == 0)
        def _():
            facc[...] = jnp.zeros_like(facc)

        for c in range(FFN_ROW_CHUNKS):
            rows = slice(c * (tm // FFN_ROW_CHUNKS), (c + 1) * (tm // FFN_ROW_CHUNKS))
            hb = h_ref[rows, :]
            a = _dot_nt(hb, wg_ref[0])
            b = _dot_nt(hb, wu_ref[0])
            a_ref[0, rows, :] = a.astype(BF16)
            b_ref[0, rows, :] = b.astype(BF16)
            fin = ((a * _sigmoid(a)) * b).astype(BF16)
            fin_ref[0, rows, :] = fin
            facc[rows, :] += _dot(fin, wd_ref[0])

        @pl.when(k == nk - 1)
        def _():
            f = facc[...]
            r4 = _rms(f)
            n4 = f * r4
            gate = mod_ref[0, 5:6, :]
            g = g_ref[...]
            err = (x1_ref[...] + gate * (n4 * g)) - t_ref[...]
            dy = err * (1.0 / d)
            dy_ref[...] = dy

            @pl.when(i == 0)
            def _():
                loss_ref[...] = jnp.zeros_like(loss_ref)
                accg_ref[...] = jnp.zeros_like(accg_ref)

            @pl.when(i % nt == 0)
            def _():
                accb_ref[...] = jnp.zeros_like(accb_ref)

            loss_ref[...] += (0.5 / d) * jnp.sum(err * err)
            accb_ref[0, 0:1, :] += _colsum(dy * (n4 * g))
            accg_ref[0:1, :] += _colsum((dy * gate) * n4)
            dn4 = (dy * gate) * g
            df_ref[...] = _norm_bwd(dn4, n4, r4).astype(BF16)

    tok = lambda i, k: (i, 0)
    ktok = lambda i, k: (k, i, 0)
    kw = lambda i, k: (k, 0, 0)
    const2 = lambda i, k: (0, 0)
    return pl.pallas_call(
        body, name="ffn_fwd", grid=(t_all // tm, nk),
        in_specs=[pl.BlockSpec((tm, d), tok),
                  pl.BlockSpec((1, ff, d), kw), pl.BlockSpec((1, ff, d), kw), pl.BlockSpec((1, ff, d), kw),
                  pl.BlockSpec((tm, d), tok), pl.BlockSpec((tm, d), tok),
                  pl.BlockSpec((1, MOD_ROWS, d), lambda i, k: (i // nt, 0, 0)),
                  pl.BlockSpec((1, d), const2)],
        out_specs=[pl.BlockSpec((1, tm, ff), ktok)] * 3
        + [pl.BlockSpec((tm, d), tok), pl.BlockSpec((tm, d), tok),
           pl.BlockSpec((8, LANES), const2),
           pl.BlockSpec((1, 8, d), lambda i, k: (i // nt, 0, 0)),
           pl.BlockSpec((8, d), const2)],
        out_shape=[jax.ShapeDtypeStruct((nk, t_all, ff), BF16)] * 3
        + [jax.ShapeDtypeStruct((t_all, d), F32), jax.ShapeDtypeStruct((t_all, d), BF16),
           jax.ShapeDtypeStruct((8, LANES), F32),
           jax.ShapeDtypeStruct((t_all // seq, 8, d), F32),
           jax.ShapeDtypeStruct((8, d), F32)],
        scratch_shapes=[pltpu.VMEM((tm, d), F32)],
        compiler_params=_params(("arbitrary", "arbitrary")),
    )(h2, w_g, w_u, w_d, x1, tgt, mod, g_post)


def _ffn_bwd(df, a, b, w_d, w_g, w_u, x1, dy, mix, mod, g_fpre, g_mpost, seq, tm):
    t_all, d = x1.shape
    nt = seq // tm
    nk, ff, _ = w_g.shape

    def body(df_ref, a_ref, b_ref, wd_ref, wg_ref, wu_ref, x1_ref, dy_ref, mix_ref, mod_ref, gf_ref, gm_ref,
             da_ref, db_ref, dx1_ref, dmix_ref, accb_ref, accg_ref, hacc):
        i, k = pl.program_id(0), pl.program_id(1)

        @pl.when(k == 0)
        def _():
            hacc[...] = jnp.zeros_like(hacc)

        for c in range(FFN_ROW_CHUNKS):
            rows = slice(c * (tm // FFN_ROW_CHUNKS), (c + 1) * (tm // FFN_ROW_CHUNKS))
            dfin = _dot_nt(df_ref[rows, :], wd_ref[0])
            af = a_ref[0, rows, :].astype(F32)
            bf = b_ref[0, rows, :].astype(F32)
            sig = _sigmoid(af)
            da = ((dfin * bf) * (sig * (1.0 + af * (1.0 - sig)))).astype(BF16)
            db = (dfin * (af * sig)).astype(BF16)
            da_ref[0, rows, :] = da
            db_ref[0, rows, :] = db
            hacc[rows, :] += _dot(da, wg_ref[0]) + _dot(db, wu_ref[0])

        @pl.when(k == nk - 1)
        def _():
            @pl.when(i == 0)
            def _():
                accg_ref[...] = jnp.zeros_like(accg_ref)

            @pl.when(i % nt == 0)
            def _():
                accb_ref[...] = jnp.zeros_like(accb_ref)

            dh2 = hacc[...]
            x1 = x1_ref[...]
            r3 = _rms(x1)
            n3 = x1 * r3
            g3 = gf_ref[...]
            scale1 = 1.0 + mod_ref[0, 4:5, :]
            accb_ref[0, 0:1, :] += _colsum(dh2)
            accb_ref[0, 1:2, :] += _colsum(dh2 * (n3 * g3))
            accg_ref[0:1, :] += _colsum((dh2 * scale1) * n3)
            dx1 = dy_ref[...] + _norm_bwd((dh2 * scale1) * g3, n3, r3)
            dx1_ref[...] = dx1
            mix = mix_ref[...]
            r2 = _rms(mix)
            n2 = mix * r2
            g2 = gm_ref[...]
            gate = mod_ref[0, 2:3, :]
            accb_ref[0, 2:3, :] += _colsum(dx1 * (n2 * g2))
            accg_ref[1:2, :] += _colsum((dx1 * gate) * n2)
            dmix_ref[...] = _norm_bwd((dx1 * gate) * g2, n2, r2).astype(BF16)

    tok = lambda i, k: (i, 0)
    ktok = lambda i, k: (k, i, 0)
    kw = lambda i, k: (k, 0, 0)
    const2 = lambda i, k: (0, 0)
    return pl.pallas_call(
        body, name="ffn_bwd", grid=(t_all // tm, nk),
        in_specs=[pl.BlockSpec((tm, d), tok),
                  pl.BlockSpec((1, tm, ff), ktok), pl.BlockSpec((1, tm, ff), ktok),
                  pl.BlockSpec((1, ff, d), kw), pl.BlockSpec((1, ff, d), kw), pl.BlockSpec((1, ff, d), kw),
                  pl.BlockSpec((tm, d), tok), pl.BlockSpec((tm, d), tok), pl.BlockSpec((tm, d), tok),
                  pl.BlockSpec((1, MOD_ROWS, d), lambda i, k: (i // nt, 0, 0)),
                  pl.BlockSpec((1, d), const2), pl.BlockSpec((1, d), const2)],
        out_specs=[pl.BlockSpec((1, tm, ff), ktok)] * 2
        + [pl.BlockSpec((tm, d), tok), pl.BlockSpec((tm, d), tok),
           pl.BlockSpec((1, 8, d), lambda i, k: (i // nt, 0, 0)),
           pl.BlockSpec((8, d), const2)],
        out_shape=[jax.ShapeDtypeStruct((nk, t_all, ff), BF16)] * 2
        + [jax.ShapeDtypeStruct((t_all, d), F32), jax.ShapeDtypeStruct((t_all, d), BF16),
           jax.ShapeDtypeStruct((t_all // seq, 8, d), F32),
           jax.ShapeDtypeStruct((8, d), F32)],
        scratch_shapes=[pltpu.VMEM((tm, d), F32)],
        compiler_params=_params(("arbitrary", "arbitrary")),
    )(df, a, b, w_d, w_g, w_u, x1, dy, mix, mod, g_fpre, g_mpost)


def _mixer_bwd(dmix, w_out, pooled, w_pool, pool_scale, seq, tm):
    t_all, d = dmix.shape
    p = pooled.shape[1]
    ng = len(POOL_WINDOWS)

    def body(dm_ref, wo_ref, pooled_ref, wp_ref, ps_ref, do_ref, dpd_ref, dps_ref, dwp_ref):
        i = pl.program_id(0)

        @pl.when(i == 0)
        def _():
            dps_ref[...] = jnp.zeros_like(dps_ref)
            dwp_ref[...] = jnp.zeros_like(dwp_ref)

        dmixin = _dot_nt(dm_ref[...], wo_ref[...])
        do_ref[...] = dmixin[:, :p].astype(BF16)
        for g in range(ng):
            cols = slice(g * POOL_GROUP, (g + 1) * POOL_GROUP)
            dpool = dmixin[:, p + g * POOL_GROUP:p + (g + 1) * POOL_GROUP]
            pooled = pooled_ref[:, cols]
            wpg = wp_ref[g].astype(BF16)
            yg = _dot(pooled, wpg)
            dps_ref[0:1, cols] += _colsum(dpool * yg)
            dyg = (dpool * ps_ref[:, cols]).astype(BF16)
            dwp_ref[g] += _dot_tn(pooled, dyg)
            dpd_ref[:, cols] = _dot_nt(dyg, wpg)

    tok = lambda i: (i, 0)
    const2 = lambda i: (0, 0)
    const3 = lambda i: (0, 0, 0)
    return pl.pallas_call(
        body, name="mixer_bwd", grid=(t_all // tm,),
        in_specs=[pl.BlockSpec((tm, d), tok), pl.BlockSpec((d, d), const2), pl.BlockSpec((tm, p), tok),
                  pl.BlockSpec(w_pool.shape, const3), pl.BlockSpec((1, p), const2)],
        out_specs=[pl.BlockSpec((tm, p), tok), pl.BlockSpec((tm, p), tok),
                   pl.BlockSpec((8, p), const2), pl.BlockSpec(w_pool.shape, const3)],
        out_shape=[jax.ShapeDtypeStruct((t_all, p), BF16), jax.ShapeDtypeStruct((t_all, p), F32),
                   jax.ShapeDtypeStruct((8, p), F32), jax.ShapeDtypeStruct(w_pool.shape, F32)],
        compiler_params=_params(("arbitrary",)),
    )(dmix, w_out, pooled, w_pool, pool_scale)


def _attn_bwd(qn, k, kt, v, do, ltot, seq, tq, tk):
    t_all, w = qn.shape
    nb, nq, ndiag, nkb = t_all // seq, seq // tq, tq // tk, seq // tk
    assert ndiag % 2 == 0, "two key blocks per loop trip"
    rc = ATTN_ROW_CHUNK
    nh = HEADS_PER_BLOCK
    heads = range(nh)

    def body(q_ref, k_ref, kt_ref, v_ref, do_ref, l_ref, up_ref, bf_ref, dq_ref, dk_ref, dv_ref,
             z_buf, dw_buf, ls_buf, hl_buf, upto_buf, g_buf, gb_buf, before_buf, w_buf, dz_buf,
             totl_buf, totg_buf, rem_buf, preg_buf, qnt_buf, dot_buf, dq_t, dk_t, dv_t):
        i = pl.program_id(2)
        nblk = (i + 1) * ndiag

        @pl.when(i == 0)
        def _():
            dk_t[...] = jnp.zeros_like(dk_t)
            dv_t[...] = jnp.zeros_like(dv_t)

        lane = lax.broadcasted_iota(jnp.int32, (1, LANES), 1)
        row = lax.broadcasted_iota(jnp.int32, (rc, tk), 0)
        col = lax.broadcasted_iota(jnp.int32, (rc, tk), 1)
        first = lane < HEAD_DIM
        q2 = q_ref[...]
        do2 = do_ref[...]
        l2 = l_ref[...]
        qs = [jnp.where(first, q2, jnp.zeros_like(q2)), jnp.where(first, jnp.zeros_like(q2), q2)]
        dos = [jnp.where(first, do2, jnp.zeros_like(do2)), jnp.where(first, jnp.zeros_like(do2), do2)]
        qnt_buf[...] = q2.astype(F32).T.astype(BF16)
        dot_buf[...] = do2.astype(F32).T.astype(BF16)
        for h in heads:
            rem_buf[h] = jnp.where(first if h == 0 else ~first, l2, pltpu.roll(l2, HEAD_DIM, 1))
        preg_buf[...] = jnp.zeros_like(preg_buf)
        dq_t[...] = jnp.zeros_like(dq_t)
        w_buf[1] = jnp.zeros((nh * tq, tk), BF16)
        dz_buf[1] = jnp.zeros((nh * tq, tk), BF16)

        def causal(c, diag):
            return (col + diag * tk) < (row + c * rc)

        def scores(blk, slot):
            off = pl.multiple_of(blk * tk, tk)
            kj = k_ref[pl.ds(off, tk), :]
            vj = v_ref[pl.ds(off, tk), :]
            for h in heads:
                z_buf[slot, h] = _dot_nt(qs[h], kj)
                dw_buf[slot, h] = _dot_nt(dos[h], vj)

        def gradients(blk, slot):
            keys = pl.ds(pl.multiple_of(blk * tk, tk), tk)
            for h in heads:
                dims = slice(h * HEAD_DIM, (h + 1) * HEAD_DIM)
                queries = slice(h * tq, (h + 1) * tq)
                dq_t[dims, :] += _dot_nt(kt_ref[dims, keys], dz_buf[slot, queries, :])
                dk_t[blk, dims, :] += _dot(qnt_buf[dims, :], dz_buf[slot, queries, :])
                dv_t[blk, dims, :] += _dot(dot_buf[dims, :], w_buf[slot, queries, :])

        def softplus_stage(h, slot, diag):
            for c in range(tq // rc):
                rows = slice(c * rc, (c + 1) * rc)
                if _all_masked(c, diag, rc, tk):
                    hl_buf[h, rows, :] = jnp.zeros((rc, LOG_SUM_PASSES * tk), BF16)
                    continue
                nz = z_buf[slot, h, rows, :]
                l1 = jnp.minimum(nz, 0.0) - jnp.log(1.0 + jnp.exp(_neg_abs(nz)))
                if _some_masked(c, diag, rc, tk):
                    l1 = jnp.where(causal(c, diag), l1, 0.0)
                for s, part in enumerate(_split(l1)[:LOG_SUM_PASSES]):
                    hl_buf[h, rows, s * tk:(s + 1) * tk] = part
                ls_buf[h, rows, :] = l1 - nz
                totl_buf[h, rows, :] = _row_sums(l1)

        def weights_stage(h, slot, diag):
            for c in range(tq // rc):
                rows = slice(c * rc, (c + 1) * rc)
                stacked = slice(h * tq + c * rc, h * tq + (c + 1) * rc)
                if _all_masked(c, diag, rc, tk):
                    w_buf[slot, stacked, :] = jnp.zeros((rc, tk), BF16)
                    gb_buf[h, rows, :] = jnp.zeros((rc, tk), BF16)
                    continue
                wgt = jnp.exp(ls_buf[h, rows, :] + (_across(rem_buf[h, rows, :], tk) - upto_buf[h, rows, :]))
                if _some_masked(c, diag, rc, tk):
                    wgt = jnp.where(causal(c, diag), wgt, 0.0)
                w_buf[slot, stacked, :] = wgt.astype(BF16)
                g = wgt * dw_buf[slot, h, rows, :]
                g_buf[h, rows, :] = g
                gb_buf[h, rows, :] = g.astype(BF16)
                totg_buf[h, rows, :] = _row_sums(g)
                rem_buf[h, rows, :] -= totl_buf[h, rows, :]

        def dscore_stage(h, slot, diag):
            for c in range(tq // rc):
                rows = slice(c * rc, (c + 1) * rc)
                stacked = slice(h * tq + c * rc, h * tq + (c + 1) * rc)
                if _all_masked(c, diag, rc, tk):
                    dz_buf[slot, stacked, :] = jnp.zeros((rc, tk), BF16)
                    continue
                sig = jnp.exp(ls_buf[h, rows, :])
                g = g_buf[h, rows, :]
                dnz = sig * ((before_buf[h, rows, :] + _across(preg_buf[h, rows, :], tk)) + g) - g
                if _some_masked(c, diag, rc, tk):
                    dnz = jnp.where(causal(c, diag), dnz, 0.0)
                dz_buf[slot, stacked, :] = dnz.astype(BF16)
                preg_buf[h, rows, :] += totg_buf[h, rows, :]

        def position(blk, slot, diag, prefetch):
            if prefetch:
                scores(blk + 1, 1 - slot)
            for h in heads:
                softplus_stage(h, slot, diag)
                upto_buf[h] = _dot(hl_buf[h], up_ref[...])
            gradients(jnp.maximum(blk - 1, 0), 1 - slot)
            for h in heads:
                weights_stage(h, slot, diag)
                before_buf[h] = _dot(gb_buf[h], bf_ref[...])
            for h in heads:
                dscore_stage(h, slot, diag)

        scores(0, 0)

        def trip(jj, carry):
            for u in range(2):
                position(2 * jj + u, u, None, True)
            return carry

        lax.fori_loop(0, (i * ndiag) // 2, trip, 0)
        for d in range(ndiag):
            position(i * ndiag + d, d % 2, d, d < ndiag - 1)
        gradients(nblk - 1, 1)
        dq_ref[...] = (dq_t[...].T * NEG_QK_SCALE).astype(BF16)

        @pl.when(i == nq - 1)
        def _():
            for blk in range(nkb):
                dk_ref[blk * tk:(blk + 1) * tk, :] = dk_t[blk].T.astype(BF16)
                dv_ref[blk * tk:(blk + 1) * tk, :] = dv_t[blk].T.astype(BF16)

    qmap = lambda b, hp, i: (b * nq + i, hp)
    kmap = lambda b, hp, i: (b, hp)
    const = lambda b, hp, i: (0, 0)
    return pl.pallas_call(
        body, name="attn_bwd", grid=(nb, w // LANES, nq),
        in_specs=[pl.BlockSpec((tq, LANES), qmap), pl.BlockSpec((seq, LANES), kmap),
                  pl.BlockSpec((LANES, seq), lambda b, hp, i: (hp, b)), pl.BlockSpec((seq, LANES), kmap),
                  pl.BlockSpec((tq, LANES), qmap), pl.BlockSpec((tq, LANES), qmap),
                  pl.BlockSpec((LOG_SUM_PASSES * tk, tk), const), pl.BlockSpec((tk, tk), const)],
        out_specs=[pl.BlockSpec((tq, LANES), qmap), pl.BlockSpec((seq, LANES), kmap), pl.BlockSpec((seq, LANES), kmap)],
        out_shape=[jax.ShapeDtypeStruct((t_all, w), BF16)] * 3,
        scratch_shapes=[pltpu.VMEM((2, nh, tq, tk), F32), pltpu.VMEM((2, nh, tq, tk), F32),
                        pltpu.VMEM((nh, tq, tk), F32), pltpu.VMEM((nh, tq, LOG_SUM_PASSES * tk), BF16),
                        pltpu.VMEM((nh, tq, tk), F32), pltpu.VMEM((nh, tq, tk), F32),
                        pltpu.VMEM((nh, tq, tk), BF16), pltpu.VMEM((nh, tq, tk), F32),
                        pltpu.VMEM((2, nh * tq, tk), BF16), pltpu.VMEM((2, nh * tq, tk), BF16),
                        pltpu.VMEM((nh, tq, LANES), F32), pltpu.VMEM((nh, tq, LANES), F32),
                        pltpu.VMEM((nh, tq, LANES), F32), pltpu.VMEM((nh, tq, LANES), F32),
                        pltpu.VMEM((LANES, tq), BF16), pltpu.VMEM((LANES, tq), BF16),
                        pltpu.VMEM((LANES, tq), F32), pltpu.VMEM((nkb, LANES, tk), F32),
                        pltpu.VMEM((nkb, LANES, tk), F32)],
        compiler_params=_params(("arbitrary", "arbitrary", "arbitrary")),
    )(qn, k, kt, v, do, ltot, _tri_matrix(tk, "upto")[:LOG_SUM_PASSES * tk], _tri_matrix(tk, "before")[:tk])


def _inproj_bwd(dq, dk, dv, dpd, x, dx1, mod, g_pre, w_in, seq, tm):
    t_all, d = x.shape
    nt = seq // tm
    p = dq.shape[1]

    def body(dq_ref, dk_ref, dv_ref, dpd_ref, halo_ref, x_ref, dx1_ref, mod_ref, g_ref, w_ref,
             gx_ref, du_ref, accb_ref, accg_ref):
        i = pl.program_id(0)
        it = i % nt

        @pl.when(i == 0)
        def _():
            accg_ref[...] = jnp.zeros_like(accg_ref)

        @pl.when(it == 0)
        def _():
            accb_ref[...] = jnp.zeros_like(accb_ref)

        dpd = dpd_ref[...]
        pos = it * tm + lax.broadcasted_iota(jnp.int32, (tm, 1), 0)
        cnts = _window_counts(pos)
        halo = jnp.where(it == nt - 1, 0.0, halo_ref[...])
        scaled = []
        halos = []
        for g, win in enumerate(POOL_WINDOWS):
            cols = slice(g * POOL_GROUP, (g + 1) * POOL_GROUP)
            scaled.append(dpd[:, cols] / cnts[g])
            halos.append(halo[:, cols] / float(win))
        ext = jnp.concatenate([jnp.concatenate(scaled, axis=1), jnp.concatenate(halos, axis=1)], axis=0)
        sums = _window_sums(ext, tm, 0, True)
        du = (jnp.concatenate(sums, axis=1) - dpd).astype(BF16)
        du_ref[...] = du
        g1 = g_ref[...]
        scale1 = 1.0 + mod_ref[0, 1:2, :]
        for c in range(ROW_CHUNKS):
            rows = slice(c * (tm // ROW_CHUNKS), (c + 1) * (tm // ROW_CHUNKS))
            dh1 = (_dot_nt(dq_ref[rows, :], w_ref[0]) + _dot_nt(dk_ref[rows, :], w_ref[1])
                   + _dot_nt(dv_ref[rows, :], w_ref[2]) + _dot_nt(du_ref[rows, :], w_ref[3]))
            xf = x_ref[rows, :]
            r1 = _rms(xf)
            n1 = xf * r1
            accb_ref[0, 0:1, :] += _colsum(dh1)
            accb_ref[0, 1:2, :] += _colsum(dh1 * (n1 * g1))
            accg_ref[0:1, :] += _colsum((dh1 * scale1) * n1)
            gx_ref[rows, :] = dx1_ref[rows, :] + _norm_bwd((dh1 * scale1) * g1, n1, r1)

    tok = lambda i: (i, 0)
    const2 = lambda i: (0, 0)
    hb = tm // HALO
    last = t_all // HALO - 1
    return pl.pallas_call(
        body, name="inproj_bwd", grid=(t_all // tm,),
        in_specs=[pl.BlockSpec((tm, p), tok), pl.BlockSpec((tm, p), tok), pl.BlockSpec((tm, p), tok),
                  pl.BlockSpec((tm, p), tok),
                  pl.BlockSpec((HALO, p), lambda i: (jnp.minimum((i + 1) * hb, last), 0)),
                  pl.BlockSpec((tm, d), tok), pl.BlockSpec((tm, d), tok),
                  pl.BlockSpec((1, MOD_ROWS, d), lambda i: (i // nt, 0, 0)),
                  pl.BlockSpec((1, d), const2),
                  pl.BlockSpec((N_CHIPS, d, p), lambda i: (0, 0, 0))],
        out_specs=[pl.BlockSpec((tm, d), tok), pl.BlockSpec((tm, p), tok),
                   pl.BlockSpec((1, 8, d), lambda i: (i // nt, 0, 0)),
                   pl.BlockSpec((8, d), const2)],
        out_shape=[jax.ShapeDtypeStruct((t_all, d), F32), jax.ShapeDtypeStruct((t_all, p), BF16),
                   jax.ShapeDtypeStruct((t_all // seq, 8, d), F32),
                   jax.ShapeDtypeStruct((8, d), F32)],
        compiler_params=_params(("arbitrary",)),
    )(dq, dk, dv, dpd, dpd, x, dx1, mod, g_pre, w_in)


def _tn_matmul(x, ys, nk, bt, name):
    t_all = x.shape[-2]
    m = x.shape[-1]
    ny = len(ys)
    nt = t_all // bt

    def spec(arr):
        if arr.ndim == 3:
            return pl.BlockSpec((1, bt, arr.shape[-1]), lambda k, t: (k, t, 0))
        return pl.BlockSpec((bt, arr.shape[-1]), lambda k, t: (t, 0))

    def tile(ref):
        return ref[0] if len(ref.shape) == 3 else ref[...]

    def body(*refs):
        x_ref, y_refs, o_refs, h_refs = refs[0], refs[1:1 + ny], refs[1 + ny:1 + 2 * ny], refs[1 + 2 * ny:]
        t = pl.program_id(1)
        xt = tile(x_ref)
        for y_ref, o_ref, h_ref in zip(y_refs, o_refs, h_refs):
            part = _dot_tn(xt, tile(y_ref))

            @pl.when(t == 0)
            def _(o_ref=o_ref, part=part):
                o_ref[0] = part

            @pl.when(t > 0)
            def _(o_ref=o_ref, part=part):
                o_ref[0] += part

            @pl.when(t == nt - 1)
            def _(o_ref=o_ref, h_ref=h_ref):
                h_ref[0] = o_ref[0].astype(BF16)

    out_specs = [pl.BlockSpec((1, m, y.shape[-1]), lambda k, t: (k, 0, 0)) for y in ys]
    out = pl.pallas_call(
        body, name=name, grid=(nk, nt),
        in_specs=[spec(x)] + [spec(y) for y in ys],
        out_specs=out_specs * 2,
        out_shape=[jax.ShapeDtypeStruct((nk, m, y.shape[-1]), dt) for dt in (F32, BF16) for y in ys],
        compiler_params=_params(("arbitrary", "arbitrary")),
    )(x, *ys)
    return out[:ny], out[ny:]


def _tn_matmul_stacked(x, ys, bt, name):
    t_all, m = x.shape
    n = ys[0].shape[1]
    ny = len(ys)
    nt = t_all // bt

    def body(*refs):
        x_ref, y_refs, o_ref, h_ref = refs[0], refs[1:1 + ny], refs[1 + ny], refs[2 + ny]
        t = pl.program_id(0)
        xt = x_ref[...]

        @pl.when(t == 0)
        def _():
            o_ref[...] = jnp.zeros_like(o_ref)

        for j, y_ref in enumerate(y_refs):
            o_ref[j] += _dot_tn(xt, y_ref[...])

        @pl.when(t == nt - 1)
        def _():
            h_ref[...] = o_ref[...].astype(BF16)

    whole = pl.BlockSpec((ny, m, n), lambda t: (0, 0, 0))
    return pl.pallas_call(
        body, name=name, grid=(nt,),
        in_specs=[pl.BlockSpec((bt, m), lambda t: (t, 0))] + [pl.BlockSpec((bt, n), lambda t: (t, 0))] * ny,
        out_specs=[whole, whole],
        out_shape=[jax.ShapeDtypeStruct((ny, m, n), F32), jax.ShapeDtypeStruct((ny, m, n), BF16)],
        compiler_params=_params(("arbitrary",)),
    )(x, *ys)


def _cond_fwd(c_all, w_q, b_q, bn):
    nrow, d = c_all.shape
    ncol = w_q.shape[1]

    def body(c_ref, w_ref, b_ref, sc_ref, mod_ref):
        cf = c_ref[...]
        sc = cf * _sigmoid(cf)
        sc_ref[...] = sc
        shi, slo = _split(sc)
        whi, wlo = _split(w_ref[...])
        mod_ref[...] = (_dot(shi, whi) + _dot(shi, wlo) + _dot(slo, whi)) + b_ref[...]

    return pl.pallas_call(
        body, name="cond_fwd", grid=(ncol // bn,),
        in_specs=[pl.BlockSpec((nrow, d), lambda n: (0, 0)), pl.BlockSpec((d, bn), lambda n: (0, n)),
                  pl.BlockSpec((1, bn), lambda n: (0, n))],
        out_specs=[pl.BlockSpec((nrow, d), lambda n: (0, 0)), pl.BlockSpec((nrow, bn), lambda n: (0, n))],
        out_shape=[jax.ShapeDtypeStruct((nrow, d), F32), jax.ShapeDtypeStruct((nrow, ncol), F32)],
        compiler_params=_params(("arbitrary",)),
    )(c_all, w_q, b_q)


def _cond_bwd(sc_all, dmod_q, bn):
    nrow, d = sc_all.shape
    ncol = dmod_q.shape[1]

    def body(sc_ref, dm_ref, gw_ref):
        shi, slo = _split(sc_ref[...])
        dhi, dlo = _split(dm_ref[...])
        gw_ref[...] = _dot_tn(shi, dhi) + _dot_tn(shi, dlo) + _dot_tn(slo, dhi)

    return pl.pallas_call(
        body, name="cond_bwd", grid=(ncol // bn,),
        in_specs=[pl.BlockSpec((nrow, d), lambda n: (0, 0)), pl.BlockSpec((nrow, bn), lambda n: (0, n))],
        out_specs=pl.BlockSpec((d, bn), lambda n: (0, n)),
        out_shape=jax.ShapeDtypeStruct((d, ncol), F32),
        compiler_params=_params(("arbitrary",)),
    )(sc_all, dmod_q)


def _row_block(rows, cols, budget=1 << 18):
    best = None
    for br in range(8, rows + 1, 8):
        if rows % br == 0 and br * cols <= budget:
            best = br
    return best if best is not None else rows


def _adam_math(w, g, m, v):
    c1 = 1.0 - ADAM_B1 ** ADAM_STEP
    c2 = 1.0 - ADAM_B2 ** ADAM_STEP
    m2 = ADAM_B1 * m + (1.0 - ADAM_B1) * g
    v2 = ADAM_B2 * v + (1.0 - ADAM_B2) * (g * g)
    return -ADAM_LR * ((m2 / c1) / (jnp.sqrt(v2 / c2) + ADAM_EPS) + ADAM_WD * w), m2, v2


def _small_updates(summed, params):
    n = len(params)

    def body(s_ref, *refs):
        ins, outs = refs[:3 * n], refs[3 * n:]
        for p, (_, _, _, pick) in enumerate(params):
            w_ref, m_ref, v_ref = ins[3 * p:3 * p + 3]
            g = pick(s_ref)
            delta, m2, v2 = _adam_math(w_ref[...], g, m_ref[...], v_ref[...])
            for o_ref, val in zip(outs[4 * p:4 * p + 4], (g, delta, m2, v2)):
                o_ref[...] = val

    out = pl.pallas_call(
        body, name="adamw_small",
        out_shape=[jax.ShapeDtypeStruct(w.shape, F32) for w, _, _, _ in params for _ in range(4)],
        compiler_params=pltpu.CompilerParams(vmem_limit_bytes=VMEM_LIMIT),
    )(summed, *[t for w, m, v, _ in params for t in (w, m, v)])
    return [tuple(out[4 * p:4 * p + 4]) for p in range(n)]


def _adamw(w, g, m, v, name):
    rows, cols = w.shape
    br = _row_block(rows, cols)

    def body(w_ref, g_ref, m_ref, v_ref, d_ref, nm_ref, nv_ref):
        d_ref[...], nm_ref[...], nv_ref[...] = _adam_math(w_ref[...], g_ref[...], m_ref[...], v_ref[...])

    blk = pl.BlockSpec((br, cols), lambda i: (i, 0))
    return pl.pallas_call(
        body, name=name, grid=(rows // br,),
        in_specs=[blk] * 4, out_specs=[blk] * 3,
        out_shape=[jax.ShapeDtypeStruct((rows, cols), F32)] * 3,
        compiler_params=_params(("arbitrary",)),
    )(w, g, m, v)


def _all_gather(x_shard, name):
    m_per, n = x_shard.shape

    def body(x_ref, out_ref, send_sems, recv_sems, local_sem):
        x, y, c = _position()
        me, sibling = (x, y, c), (x, y, 1 - c)
        chips = [(1 - x, y), (x, 1 - y), (1 - x, 1 - y)]

        def rows(px, py, pc):
            return out_ref.at[pl.ds((4 * px + 2 * py + pc) * m_per, m_per), :]

        def copy(k, block, to, src=None):
            return pltpu.make_async_remote_copy(
                src_ref=rows(*block) if src is None else src, dst_ref=rows(*block),
                send_sem=send_sems.at[k], recv_sem=recv_sems.at[k], device_id=to, device_id_type=MESH)

        mine = pltpu.make_async_copy(x_ref, rows(*me), local_sem)
        mine.start()
        first = [copy(0, me, sibling, src=x_ref)]
        first += [copy(1 + j, me, (*chip, c), src=x_ref) for j, chip in enumerate(chips)]
        for cp in first:
            cp.start()
        passed = [copy(4 + j, (*chip, c), sibling) for j, chip in enumerate(chips)]
        for j, chip in enumerate(chips):
            copy(1 + j, (*chip, c), me).wait_recv()
            passed[j].start()
        copy(0, sibling, me).wait_recv()
        for j, chip in enumerate(chips):
            copy(4 + j, (*chip, 1 - c), me).wait_recv()
        for cp in first + passed:
            cp.wait_send()
        mine.wait()

    return pl.pallas_call(
        body, name=name,
        out_shape=jax.ShapeDtypeStruct((N_DEV * m_per, n), x_shard.dtype),
        in_specs=[pl.BlockSpec(memory_space=pltpu.VMEM)],
        out_specs=pl.BlockSpec(memory_space=pltpu.VMEM),
        scratch_shapes=[pltpu.SemaphoreType.DMA((7,)), pltpu.SemaphoreType.DMA((7,)), pltpu.SemaphoreType.DMA],
        compiler_params=pltpu.CompilerParams(vmem_limit_bytes=VMEM_LIMIT),
    )(x_shard)


_ANY = pl.BlockSpec(memory_space=pl.ANY)


def _place_quarters(place, quarters):
    steps = 2

    def body(place_ref, *refs):
        n = len(refs) // 2
        for w_ref, o_ref in zip(refs[:n], refs[n:]):
            o_ref[0] = w_ref[...].astype(BF16)

    return pl.pallas_call(
        body, name="place_quarters",
        grid_spec=pltpu.PrefetchScalarGridSpec(
            num_scalar_prefetch=1, grid=(steps,),
            in_specs=[pl.BlockSpec((q.shape[0] // steps, q.shape[1]), lambda r, place_ref: (r, 0)) for q in quarters],
            out_specs=[pl.BlockSpec((1, q.shape[0] // steps, q.shape[1]), lambda r, place_ref: (place_ref[0], r, 0))
                       for q in quarters]),
        out_shape=[jax.ShapeDtypeStruct((N_CHIPS,) + q.shape, BF16) for q in quarters],
        compiler_params=_params(("arbitrary",)),
    )(place, *quarters)


_HBM = pl.BlockSpec(memory_space=pltpu.HBM)
_SEM = pl.BlockSpec(memory_space=pltpu.SEMAPHORE)
_EFFECT = pltpu.SideEffectType.DATAFLOW_SIDE_EFFECTING


def _quarter_halves(shapes, a, which):
    hr = shapes[a][0] // 2
    return pl.ds(which * hr, hr)


def _gather_start(placed, after, tag):
    n = len(placed)
    m = len(after)
    shapes = [b.shape[1:] for b in placed]

    def body(*refs):
        g_refs = refs[:n]
        send_sems, recv_sems = refs[n + m], refs[n + m + 1]
        token = refs[2 * n + m + 2]
        x, y, c = _position()
        chips = [(1 - x, y), (x, 1 - y), (1 - x, 1 - y)]
        mine = 2 * x + y
        for a in range(n):
            ref = g_refs[a].at[mine, _quarter_halves(shapes, a, c), :]
            for p in range(3):
                pltpu.make_async_remote_copy(
                    src_ref=ref, dst_ref=ref, send_sem=send_sems.at[3 * a + p], recv_sem=recv_sems.at[3 * a + p],
                    device_id=(*chips[p], c), device_id_type=MESH).start()
        token[...] = jnp.zeros_like(token)

    out = pl.pallas_call(
        body, name="gather_start_" + tag,
        out_shape=(pltpu.SemaphoreType.DMA((3 * n,)), pltpu.SemaphoreType.DMA((3 * n,)),
                   *[pltpu.HBM(b.shape, b.dtype) for b in placed], jax.ShapeDtypeStruct((8, LANES), F32)),
        in_specs=[_HBM] * n + [_ANY] * m,
        out_specs=(_SEM, _SEM, *[_HBM] * n, pl.BlockSpec(memory_space=pltpu.VMEM)),
        input_output_aliases={a: 2 + a for a in range(n)},
        compiler_params=pltpu.CompilerParams(has_side_effects=_EFFECT),
    )(*[pltpu.with_memory_space_constraint(b, pltpu.HBM) for b in placed], *after)
    return out[0], out[1], list(out[2:2 + n]), out[2 + n]


def _gather_wait(send_sems, recv_sems, thru, after, tag):
    n = len(thru)
    shapes = [b.shape[1:] for b in thru]

    def body(*refs):
        g_refs = refs[:n]
        send_sems, recv_sems = refs[n], refs[n + 1]
        x, y, c = _position()
        chips = [(1 - x, y), (x, 1 - y), (1 - x, 1 - y)]
        mine = 2 * x + y
        for a in range(n):
            rows = _quarter_halves(shapes, a, c)
            for p, (cx, cy) in enumerate(chips):
                copy = pltpu.make_async_remote_copy(
                    src_ref=g_refs[a].at[mine, rows, :], dst_ref=g_refs[a].at[2 * cx + cy, rows, :],
                    send_sem=send_sems.at[3 * a + p], recv_sem=recv_sems.at[3 * a + p],
                    device_id=(cx, cy, c), device_id_type=MESH)
                copy.wait_send()
                copy.wait_recv()

    return pl.pallas_call(
        body, name="gather_wait_" + tag,
        out_shape=[pltpu.HBM(b.shape, b.dtype) for b in thru],
        in_specs=[_HBM] * n + [_SEM, _SEM, _ANY], out_specs=[_HBM] * n,
        input_output_aliases={a: a for a in range(n)},
        compiler_params=pltpu.CompilerParams(has_side_effects=_EFFECT),
    )(*thru, send_sems, recv_sems, after)


def _gather_forward(bufs, tag):
    n = len(bufs)
    shapes = [b.shape[1:] for b in bufs]

    def body(*refs):
        g_refs = refs[n:2 * n]
        send_sems, recv_sems = refs[2 * n:]
        x, y, c = _position()
        chips = [(1 - x, y), (x, 1 - y), (1 - x, 1 - y)]

        def over_d2d(a, p, which):
            cx, cy = chips[p]
            ref = g_refs[a].at[2 * cx + cy, _quarter_halves(shapes, a, which), :]
            return pltpu.make_async_remote_copy(
                src_ref=ref, dst_ref=ref, send_sem=send_sems.at[3 * a + p], recv_sem=recv_sems.at[3 * a + p],
                device_id=(x, y, 1 - c), device_id_type=MESH)

        sends = [over_d2d(a, p, c) for a in range(n) for p in range(3)]
        for cp in sends:
            cp.start()
        for a in range(n):
            for p in range(3):
                over_d2d(a, p, 1 - c).wait_recv()
        for cp in sends:
            cp.wait_send()

    return pl.pallas_call(
        body, name="gather_forward_" + tag,
        out_shape=[jax.ShapeDtypeStruct(b.shape, BF16) for b in bufs],
        in_specs=[_ANY] * n, out_specs=[_ANY] * n,
        input_output_aliases={a: a for a in range(n)},
        scratch_shapes=[pltpu.SemaphoreType.DMA((3 * n,)), pltpu.SemaphoreType.DMA((3 * n,))],
    )(*bufs)


def _sibling_exchange(grads, tag):
    n = len(grads)
    shapes = [g.shape for g in grads]

    def body(*refs):
        g_refs, x_refs = refs[:n], refs[n:2 * n]
        send_sems, recv_sems = refs[2 * n:]
        x, y, c = _position()
        copies = []
        for a in range(n):
            hr = shapes[a][1] // 2
            cp = pltpu.make_async_remote_copy(
                src_ref=g_refs[a].at[:, pl.ds((1 - c) * hr, hr), :], dst_ref=x_refs[a],
                send_sem=send_sems.at[a], recv_sem=recv_sems.at[a],
                device_id=(x, y, 1 - c), device_id_type=MESH)
            cp.start()
            copies.append(cp)
        for cp in copies:
            cp.wait()

    return pl.pallas_call(
        body, name="grad_sibling_exchange_" + tag,
        out_shape=[jax.ShapeDtypeStruct((g.shape[0], g.shape[1] // 2, g.shape[2]), g.dtype) for g in grads],
        in_specs=[_ANY] * n, out_specs=[_ANY] * n,
        scratch_shapes=[pltpu.SemaphoreType.DMA((n,)), pltpu.SemaphoreType.DMA((n,))],
    )(*grads)


def _chip_sums(core, grads, theirs, tag):
    n = len(grads)

    def body(core_ref, *refs):
        g_refs, t_refs, o_refs = refs[:n], refs[n:2 * n], refs[2 * n:]
        for g_ref, t_ref, o_ref in zip(g_refs, t_refs, o_refs):
            o_ref[...] = (g_ref[...] + t_ref[...].astype(F32)).astype(BF16)

    in_specs = [pl.BlockSpec((1, g.shape[1] // 2, g.shape[2]), lambda k, core_ref: (k, core_ref[0], 0)) for g in grads]
    in_specs += [pl.BlockSpec((1,) + t.shape[1:], lambda k, core_ref: (k, 0, 0)) for t in theirs]
    return pl.pallas_call(
        body, name="grad_chip_sums_" + tag,
        grid_spec=pltpu.PrefetchScalarGridSpec(
            num_scalar_prefetch=1, grid=(N_CHIPS,), in_specs=in_specs,
            out_specs=[pl.BlockSpec((1,) + t.shape[1:], lambda k, core_ref: (k, 0, 0)) for t in theirs]),
        out_shape=[jax.ShapeDtypeStruct(t.shape, BF16) for t in theirs],
        compiler_params=_params(("arbitrary",)),
    )(core, *grads, *theirs)


def _chip_exchange_start(sums, after, tag):
    n = len(sums)
    m = len(after)
    lands = [lax.empty((3,) + s.shape[1:], BF16) for s in sums]

    def body(*refs):
        s_refs, y_refs = refs[:n], refs[n:2 * n]
        send_sems, recv_sems = refs[2 * n + m], refs[2 * n + m + 1]
        token = refs[4 * n + m + 2]
        x, y, c = _position()
        chips = [(1 - x, y), (x, 1 - y), (1 - x, 1 - y)]
        for a in range(n):
            for p, (cx, cy) in enumerate(chips):
                pltpu.make_async_remote_copy(
                    src_ref=s_refs[a].at[2 * cx + cy], dst_ref=y_refs[a].at[p],
                    send_sem=send_sems.at[3 * a + p], recv_sem=recv_sems.at[3 * a + p],
                    device_id=(cx, cy, c), device_id_type=MESH).start()
        token[...] = jnp.zeros_like(token)

    both = list(sums) + lands
    out = pl.pallas_call(
        body, name="grad_chip_exchange_start_" + tag,
        out_shape=(pltpu.SemaphoreType.DMA((3 * n,)), pltpu.SemaphoreType.DMA((3 * n,)),
                   *[pltpu.HBM(b.shape, b.dtype) for b in both], jax.ShapeDtypeStruct((8, LANES), F32)),
        in_specs=[_HBM] * (2 * n) + [_ANY] * m,
        out_specs=(_SEM, _SEM, *[_HBM] * (2 * n), pl.BlockSpec(memory_space=pltpu.VMEM)),
        input_output_aliases={a: 2 + a for a in range(2 * n)},
        compiler_params=pltpu.CompilerParams(has_side_effects=_EFFECT),
    )(*[pltpu.with_memory_space_constraint(b, pltpu.HBM) for b in both], *after)
    return out[0], out[1], list(out[2:2 + n]), list(out[2 + n:2 + 2 * n]), out[2 + 2 * n]


def _chip_exchange_wait(send_sems, recv_sems, sums, lands, after, tag):
    n = len(sums)

    def body(*refs):
        s_refs, y_refs = refs[:n], refs[n:2 * n]
        send_sems, recv_sems = refs[2 * n], refs[2 * n + 1]
        x, y, c = _position()
        chips = [(1 - x, y), (x, 1 - y), (1 - x, 1 - y)]
        for a in range(n):
            for p, (cx, cy) in enumerate(chips):
                copy = pltpu.make_async_remote_copy(
                    src_ref=s_refs[a].at[2 * cx + cy], dst_ref=y_refs[a].at[p],
                    send_sem=send_sems.at[3 * a + p], recv_sem=recv_sems.at[3 * a + p],
                    device_id=(cx, cy, c), device_id_type=MESH)
                copy.wait_send()
                copy.wait_recv()

    both = list(sums) + list(lands)
    out = pl.pallas_call(
        body, name="grad_chip_exchange_wait_" + tag,
        out_shape=[pltpu.HBM(b.shape, b.dtype) for b in both],
        in_specs=[_HBM] * (2 * n) + [_SEM, _SEM, _ANY], out_specs=[_HBM] * (2 * n),
        input_output_aliases={a: a for a in range(2 * n)},
        compiler_params=pltpu.CompilerParams(has_side_effects=_EFFECT),
    )(*both, send_sems, recv_sems, after)
    return list(out[:n]), list(out[n:])


def _total_sums(place, sums, parts, after, tag):
    n = len(parts)
    m = len(after)
    steps = 2

    def body(place_ref, *refs):
        for s_ref, y_ref, o_ref in zip(refs[:n], refs[n:2 * n], refs[2 * n + m:]):
            o_ref[0] = ((s_ref[0].astype(F32) + y_ref[0].astype(F32)) + y_ref[1].astype(F32)) + y_ref[2].astype(F32)

    def step_rows(pt):
        return pt.shape[1] // steps

    in_specs = [pl.BlockSpec((1, step_rows(s), s.shape[2]), lambda r, place_ref: (place_ref[0], r, 0)) for s in sums]
    in_specs += [pl.BlockSpec((3, step_rows(pt), pt.shape[2]), lambda r, place_ref: (0, r, 0)) for pt in parts]
    in_specs += [_ANY] * m
    return pl.pallas_call(
        body, name="grad_total_sums_" + tag,
        grid_spec=pltpu.PrefetchScalarGridSpec(
            num_scalar_prefetch=1, grid=(steps,), in_specs=in_specs,
            out_specs=[pl.BlockSpec((1, step_rows(pt), pt.shape[2]), lambda r, place_ref: (place_ref[1], r, 0))
                       for pt in parts]),
        out_shape=[jax.ShapeDtypeStruct((2,) + pt.shape[1:], F32) for pt in parts],
        compiler_params=_params(("arbitrary",)),
    )(place, *sums, *parts, *after)


def _sibling_share(halves, tag):
    n = len(halves)

    def body(*refs):
        f_refs = refs[n:2 * n]
        send_sems, recv_sems = refs[2 * n:]
        x, y, c = _position()
        copies = []
        for a in range(n):
            cp = pltpu.make_async_remote_copy(
                src_ref=f_refs[a].at[c], dst_ref=f_refs[a].at[c], send_sem=send_sems.at[a], recv_sem=recv_sems.at[a],
                device_id=(x, y, 1 - c), device_id_type=MESH)
            cp.start()
            copies.append(cp)
        for a, cp in enumerate(copies):
            cp.wait_send()
            pltpu.make_async_remote_copy(
                src_ref=f_refs[a].at[1 - c], dst_ref=f_refs[a].at[1 - c], send_sem=send_sems.at[a],
                recv_sem=recv_sems.at[a], device_id=(x, y, c), device_id_type=MESH).wait_recv()

    return pl.pallas_call(
        body, name="grad_sibling_share_" + tag,
        out_shape=[jax.ShapeDtypeStruct(h.shape, F32) for h in halves],
        in_specs=[_ANY] * n, out_specs=[_ANY] * n,
        input_output_aliases={a: a for a in range(n)},
        scratch_shapes=[pltpu.SemaphoreType.DMA((n,)), pltpu.SemaphoreType.DMA((n,))],
    )(*halves)


def _group_sum(stacked, nrow, name):
    total, n = stacked.shape
    groups = total // nrow

    def body(g_ref, o_ref):
        acc = g_ref[0:nrow, :]
        for grp in range(1, groups):
            acc = acc + g_ref[grp * nrow:(grp + 1) * nrow, :]
        o_ref[...] = acc

    return pl.pallas_call(
        body, name=name,
        out_shape=jax.ShapeDtypeStruct((nrow, n), F32),
        compiler_params=pltpu.CompilerParams(vmem_limit_bytes=VMEM_LIMIT),
    )(stacked)


def _local_step(xt, tgt, mod, gains, w_pool, pool_scale, w_in, later_weights, on_ffn_grads, seq):
    g_mpre, g_mpost, g_fpre, g_fpost = gains
    d = xt.shape[1]
    tm, tq = min(TOKEN_TILE, seq), min(ATTN_TILE, seq)

    h1, qn, k, v, u, kt, vt = _prenorm_proj(xt, mod, g_mpre, w_in, seq, tm)
    tk = min(ATTN_KEY_TILE, tq // 2)
    o, ltot = _attn_fwd(qn, k, vt, seq, tq, tk)
    w_out, w_g, w_u, w_d = later_weights(o)
    w_out2 = w_out.reshape(d, d)
    pooled, mixin, mix, x1, h2 =_mixer_post(u, o, xt, mod, g_mpost, g_fpre, w_pool, pool_scale, w_out2, seq, tm)
    a, b, fin, dy, df, loss_blk, accb4, accg4 = _ffn_fwd(h2, w_g, w_u, w_d, x1, tgt, mod, g_fpost, seq, tm)
    da, db, dx1, dmix, accb5, accg5 = _ffn_bwd(df, a, b, w_d, w_g, w_u, x1, dy, mix, mod, g_fpre, g_mpost, seq, tm)
    bt = min(GRAD_TOKEN_TILE, xt.shape[0])
    bt_one = min(2 * GRAD_TOKEN_TILE, xt.shape[0])
    (g_g,), (g_g16,) = _tn_matmul(da, [h2], w_g.shape[0], bt_one, "grad_w_gate")
    (g_u,), (g_u16,) = _tn_matmul(db, [h2], w_u.shape[0], bt_one, "grad_w_up")
    (g_d,), (g_d16,) = _tn_matmul(fin, [df], w_d.shape[0], bt_one, "grad_w_down")
    token = on_ffn_grads([g_g, g_u, g_d], [g_g16, g_u16, g_d16])
    do, dpd, dps, dwp = _mixer_bwd(dmix, w_out2, pooled, w_pool, pool_scale + token, seq, tm)
    dq, dk, dv = _attn_bwd(qn, k, kt, v, do, ltot, seq, tq, tk)
    gx, du, accb8, accg8 = _inproj_bwd(dq, dk, dv, dpd, xt, dx1, mod, g_mpre, w_in, seq, tm)

    g_in, g_in16 = _tn_matmul_stacked(h1, [dq, dk, dv, du], bt, "grad_w_in")
    g_out, g_out16 = [parts[0].reshape(w_out.shape) for parts in _tn_matmul(mixin, [dmix], 1, bt_one, "grad_w_out")]

    dmod = jnp.stack([accb8[:, 0], accb8[:, 1], accb5[:, 2], accb5[:, 0], accb5[:, 1], accb4[:, 0]], axis=1)
    dgain = jnp.stack([accg8[0], accg5[1], accg5[0], accg4[0]], axis=0)
    grads = [g_in, g_out, g_g, g_u, g_d]
    grads16 = [g_in16, g_out16, g_g16, g_u16, g_d16]
    return loss_blk, gx, grads, grads16, dmod, dgain, dps[0:1], dwp


def kernel(x, c, w_cond, b_cond, g_mix_pre, g_mix_post, w_in, w_pool, pool_scale, w_out, g_ffn_pre, g_ffn_post, w_gate, w_up, w_down, loss_target, m_w_cond, m_b_cond, m_g_mix_pre, m_g_mix_post, m_w_in, m_w_pool, m_pool_scale, m_w_out, m_g_ffn_pre, m_g_ffn_post, m_w_gate, m_w_up, m_w_down, v_w_cond, v_b_cond, v_g_mix_pre, v_g_mix_post, v_w_in, v_w_pool, v_pool_scale, v_w_out, v_g_ffn_pre, v_g_ffn_post, v_w_gate, v_w_up, v_w_down):
    xi, yi, ci = _position()
    chip = 2 * xi + yi
    dev = 4 * xi + 2 * yi + ci
    nb, seq, d = x.shape
    t_all = nb * seq
    xt = x.reshape(t_all, d)
    tgt = loss_target.reshape(t_all, d)
    ncol = w_cond.shape[2]
    pw = pool_scale.shape[1]

    place = jnp.stack([chip, ci]).astype(jnp.int32)
    turned = lambda t: jnp.swapaxes(t[0], 0, 1)
    placed = _place_quarters(place, [w_in[0], w_out[0], turned(w_gate), turned(w_up), w_down[0]])
    in_sems = _gather_start(placed[:1], [], "in")

    c_pad = jnp.concatenate([c, jnp.zeros((8 - nb, d), F32)], axis=0) + in_sems[3][0:1, 0:1]
    c_all = _all_gather(c_pad, "gather_c").reshape(N_DEV, 8, d)[:, :nb].reshape(N_DEV * nb, d)
    b_q = lax.dynamic_slice(b_cond, (0, chip * ncol), (1, ncol))
    sc_all, mod_q = _cond_fwd(c_all, w_cond[0], b_q, 512)
    mod_parts = _all_gather(mod_q, "gather_mod").reshape(N_DEV, N_DEV * nb, ncol)
    mod_rows = lax.dynamic_slice(mod_parts, (0, dev * nb, 0), (N_DEV, nb, ncol))[0::2]
    mod = jnp.transpose(mod_rows, (1, 0, 2)).reshape(nb, N_MOD, d)
    mod = jnp.concatenate([mod, jnp.zeros((nb, MOD_ROWS - N_MOD, d), F32)], axis=1)

    (w_in_all,) = _gather_forward(_gather_wait(*in_sems[:3], mod, "in"), "in")
    send_sems, recv_sems, in_flight, token = _gather_start(placed[1:], [mod, w_in_all], "rest")
    mod = mod + token[0:1, 0:1]

    def later_weights(after):
        return _gather_forward(_gather_wait(send_sems, recv_sems, in_flight, after, "rest"), "rest")

    ffn_split = []

    def on_ffn_grads(ffn_grads, ffn_grads16):
        theirs = _sibling_exchange(ffn_grads16, "ffn")
        ffn_split.extend(_chip_exchange_start(_chip_sums(place[1:], ffn_grads, theirs, "ffn"), [], "ffn"))
        return ffn_split[4][0:1, 0:1]

    gains = (g_mix_pre, g_mix_post, g_ffn_pre, g_ffn_post)
    loss_blk, gx, grads, grads16, dmod, dgain, dps, dwp = _local_step(
        xt, tgt, mod, gains, w_pool[0], pool_scale, w_in_all, later_weights, on_ffn_grads, seq)

    sums_ffn, parts_ffn = _chip_exchange_wait(*ffn_split[:4], gx, "ffn")

    wp_rows = dwp.size // d
    loss_row = 2 * N_MOD + 4 + 1
    pad_rows = 24 - (loss_row + 1)
    payload = jnp.concatenate([
        dmod.reshape(nb * N_MOD, d), dgain,
        jnp.concatenate([dps, jnp.zeros((1, d - pw), F32)], axis=1),
        jnp.concatenate([loss_blk[0:1], jnp.zeros((1, d - LANES), F32)], axis=1),
        jnp.zeros((pad_rows, d), F32),
        jnp.concatenate(jnp.split(dwp.reshape(-1, dwp.shape[-1]), d // dwp.shape[-1], axis=0), axis=1)], axis=0)
    prow = payload.shape[0]
    gathered = _all_gather(payload, "gather_small")
    summed = _group_sum(gathered, prow, "small_device_sum")
    loss = summed[loss_row, 0]
    dmod_all = gathered.reshape(N_DEV, prow, d)[:, :nb * N_MOD].reshape(N_DEV * nb, N_MOD * d)
    dmod_q = lax.dynamic_slice(dmod_all, (0, chip * ncol), (N_DEV * nb, ncol))
    g_w_cond = _cond_bwd(sc_all, dmod_q, 512)
    first_gain = 2 * N_MOD

    theirs = _sibling_exchange(grads16[:2], "mix")
    mix_split = _chip_exchange_start(_chip_sums(place[1:], grads[:2], theirs, "mix"), [gathered], "mix")
    unfold = lambda halves: [g.reshape(2 * g.shape[1], g.shape[2]) for g in halves]
    g_ffn = unfold(_sibling_share(_total_sums(place, sums_ffn, parts_ffn, [mix_split[4]], "ffn"), "ffn"))

    results = {}

    def update(name, w2, g2, m2, v2, shape):
        delta, new_m, new_v = _adamw(w2, g2, m2, v2, "adamw_" + name)
        back = (lambda t: jnp.swapaxes(t, 0, 1)[None]) if shape is None else (lambda t: t.reshape(shape))
        results[name] = [back(t) for t in (g2, delta, new_m, new_v)]
        return delta

    done = [update("w_gate", turned(w_gate), g_ffn[0], turned(m_w_gate), turned(v_w_gate), None),
            update("w_up", turned(w_up), g_ffn[1], turned(m_w_up), turned(v_w_up), None),
            update("w_down", w_down[0], g_ffn[2], m_w_down[0], v_w_down[0], w_down.shape),
            update("w_cond", w_cond[0], g_w_cond, m_w_cond[0], v_w_cond[0], w_cond.shape)]

    gain_row = lambda r: (lambda s: s[first_gain + r:first_gain + r + 1, :])
    small = [
        ("b_cond", (b_cond, m_b_cond, v_b_cond), (N_MOD, d), lambda s: s[0:N_MOD, :] + s[N_MOD:2 * N_MOD, :]),
        ("g_mix_pre", (g_mix_pre, m_g_mix_pre, v_g_mix_pre), (1, d), gain_row(0)),
        ("g_mix_post", (g_mix_post, m_g_mix_post, v_g_mix_post), (1, d), gain_row(1)),
        ("g_ffn_pre", (g_ffn_pre, m_g_ffn_pre, v_g_ffn_pre), (1, d), gain_row(2)),
        ("g_ffn_post", (g_ffn_post, m_g_ffn_post, v_g_ffn_post), (1, d), gain_row(3)),
        ("pool_scale", (pool_scale, m_pool_scale, v_pool_scale), (1, pw),
         lambda s: s[first_gain + 4:first_gain + 5, 0:pw]),
        ("w_pool", (w_pool, m_w_pool, v_w_pool), (wp_rows * d // w_pool.shape[-1], w_pool.shape[-1]),
         lambda s: jnp.concatenate([s[24:24 + wp_rows, j * w_pool.shape[-1]:(j + 1) * w_pool.shape[-1]]
                                    for j in range(d // w_pool.shape[-1])], axis=0)),
    ]
    updated = _small_updates(summed, [tuple(t.reshape(flat) for t in wmv) + (pick,) for _, wmv, flat, pick in small])
    for (name, wmv, _, _), quad in zip(small, updated):
        results[name] = [t.reshape(wmv[0].shape) for t in quad]

    sums_mix, parts_mix = _chip_exchange_wait(*mix_split[:4], done[-1], "mix")
    g_mix = unfold(_sibling_share(_total_sums(place, sums_mix, parts_mix, done[:3], "mix"), "mix"))
    update("w_in", w_in[0], g_mix[0], m_w_in[0], v_w_in[0], w_in.shape)
    update("w_out", w_out[0], g_mix[1], m_w_out[0], v_w_out[0], w_out.shape)

    names = ("w_cond", "b_cond", "g_mix_pre", "g_mix_post", "w_in", "w_pool", "pool_scale", "w_out",
             "g_ffn_pre", "g_ffn_post", "w_gate", "w_up", "w_down")
    outs = [results[name][part] for part in range(4) for name in names]
    return (loss, gx.reshape(x.shape), *outs)
```

```python
import jax
import jax.numpy as jnp
import numpy as np
from jax import lax
from jax.experimental import pallas as pl
from jax.experimental.pallas import tpu as pltpu

F32 = jnp.float32
BF16 = jnp.bfloat16
MESH = pl.DeviceIdType.MESH

EPS = 1e-6
HEAD_DIM = 64
HEADS_PER_BLOCK = 2
LANES = 128
NEG_QK_SCALE = -0.125
POOL_WINDOWS = (2, 4, 8, 16)
POOL_GROUP = 128
HALO = 16
N_MOD = 6
MOD_ROWS = 8
N_CHIPS = 4
N_DEV = 8
VMEM_LIMIT = 56 * 1024 * 1024

ADAM_LR = 0.001
ADAM_B1 = 0.9
ADAM_B2 = 0.999
ADAM_EPS = 1e-08
ADAM_WD = 0.01
ADAM_STEP = 10

TOKEN_TILE = 512
GRAD_TOKEN_TILE = 2048
FFN_ROW_CHUNKS = 2
ROW_CHUNKS = 2
ATTN_TILE = 512
ATTN_KEY_TILE = 256
ATTN_ROW_CHUNK = 32
LOG_SUM_PASSES = 1


def _dot(a, b):
    return jnp.dot(a, b, preferred_element_type=F32)


def _dot_nt(a, b):
    return lax.dot_general(a, b, (((1,), (1,)), ((), ())), preferred_element_type=F32)


def _dot_tn(a, b):
    return lax.dot_general(a, b, (((0,), (0,)), ((), ())), preferred_element_type=F32)


def _split(v):
    hi = v.astype(BF16)
    lo = (v - hi.astype(F32)).astype(BF16)
    return hi, lo


def _rms(v):
    return lax.rsqrt(jnp.mean(v * v, axis=-1, keepdims=True) + EPS)


def _norm_bwd(dn, n, r):
    return r * (dn - n * jnp.mean(dn * n, axis=-1, keepdims=True))


def _sigmoid(v):
    return 0.5 * jnp.tanh(0.5 * v) + 0.5


def _colsum(v):
    return jnp.sum(v, axis=0, keepdims=True)


def _params(sem=None):
    return pltpu.CompilerParams(dimension_semantics=sem, vmem_limit_bytes=VMEM_LIMIT)


def _position():
    return lax.axis_index("x"), lax.axis_index("y"), lax.axis_index("c")


def _prenorm_proj(x, mod, g_pre, w_in, seq, tm):
    t_all, d = x.shape
    nt = seq // tm
    p = w_in.shape[2]

    def body(x_ref, mod_ref, g_ref, w_ref, h_ref, q_ref, k_ref, v_ref, u_ref, kt_ref, vt_ref):
        for c in range(ROW_CHUNKS):
            rows = slice(c * (tm // ROW_CHUNKS), (c + 1) * (tm // ROW_CHUNKS))
            xf = x_ref[rows, :]
            n = xf * _rms(xf)
            h = (n * g_ref[...]) * (1.0 + mod_ref[0, 1:2, :]) + mod_ref[0, 0:1, :]
            hb = h.astype(BF16)
            h_ref[rows, :] = hb
            q_ref[rows, :] = (_dot(hb, w_ref[0]) * NEG_QK_SCALE).astype(BF16)
            kf = _dot(hb, w_ref[1])
            vf = _dot(hb, w_ref[2])
            k_ref[rows, :] = kf.astype(BF16)
            v_ref[rows, :] = vf.astype(BF16)
            kt_ref[:, rows] = kf.T.astype(BF16)
            vt_ref[:, rows] = vf.T.astype(BF16)
            u_ref[rows, :] = _dot(hb, w_ref[3])

    tok = lambda i: (i, 0)
    tok_t = lambda i: (0, i)
    return pl.pallas_call(
        body, name="prenorm_proj", grid=(t_all // tm,),
        in_specs=[pl.BlockSpec((tm, d), tok),
                  pl.BlockSpec((1, MOD_ROWS, d), lambda i: (i // nt, 0, 0)),
                  pl.BlockSpec((1, d), lambda i: (0, 0)),
                  pl.BlockSpec((N_CHIPS, d, p), lambda i: (0, 0, 0))],
        out_specs=[pl.BlockSpec((tm, d), tok)] + [pl.BlockSpec((tm, p), tok)] * 4 + [pl.BlockSpec((p, tm), tok_t)] * 2,
        out_shape=[jax.ShapeDtypeStruct((t_all, d), BF16)] + [jax.ShapeDtypeStruct((t_all, p), BF16)] * 3
        + [jax.ShapeDtypeStruct((t_all, p), F32)] + [jax.ShapeDtypeStruct((p, t_all), BF16)] * 2,
        compiler_params=_params(("arbitrary",)),
    )(x, mod, g_pre, w_in)


def _tri_matrix(tk, kind):
    j = np.arange(2 * tk)[:, None] % tk
    s = np.arange(tk)[None, :]
    return jnp.asarray({"after": j > s, "upto": j <= s, "before": j < s}[kind], dtype=BF16)


def _neg_abs(v):
    bits = lax.bitcast_convert_type(v, jnp.int32) | jnp.int32(-2 ** 31)
    return lax.bitcast_convert_type(bits, F32)


def _row_sums(v):
    return jnp.broadcast_to(jnp.sum(v, axis=-1, keepdims=True), (v.shape[0], LANES))


def _across(v, n):
    return jnp.concatenate([v] * (n // LANES), axis=1)


def _all_masked(c, diag, rc, tk):
    return diag is not None and diag * tk >= (c + 1) * rc - 1


def _some_masked(c, diag, rc, tk):
    return diag is not None and diag * tk + tk - 1 >= c * rc


def _attn_fwd(qn, k, vt, seq, tq, tk):
    t_all, w = qn.shape
    nb, nq, ndiag = t_all // seq, seq // tq, tq // tk
    assert ndiag % 2 == 0, "two key blocks per loop trip"
    rc = ATTN_ROW_CHUNK
    heads = range(HEADS_PER_BLOCK)

    def body(q_ref, k_ref, vt_ref, tri_ref, o_ref, l_ref,
             z_buf, ls_buf, hl_buf, aft_buf, w_buf, tot_buf, acc_t, run_buf):
        i = pl.program_id(2)
        nblk = (i + 1) * ndiag
        lane = lax.broadcasted_iota(jnp.int32, (1, LANES), 1)
        row = lax.broadcasted_iota(jnp.int32, (rc, tk), 0)
        col = lax.broadcasted_iota(jnp.int32, (rc, tk), 1)
        first = lane < HEAD_DIM
        q2 = q_ref[...]
        qs = [jnp.where(first, q2, jnp.zeros_like(q2)), jnp.where(first, jnp.zeros_like(q2), q2)]
        acc_t[...] = jnp.zeros_like(acc_t)
        run_buf[...] = jnp.zeros_like(run_buf)
        w_buf[1] = jnp.zeros((HEADS_PER_BLOCK, tq, tk), BF16)

        def causal(c, diag):
            return (col + diag * tk) < (row + c * rc)

        def scores(blk, slot):
            kj = k_ref[pl.ds(pl.multiple_of(blk * tk, tk), tk), :]
            for h in heads:
                z_buf[slot, h] = _dot_nt(qs[h], kj)

        def values(blk, slot):
            keys = pl.ds(pl.multiple_of(blk * tk, tk), tk)
            for h in heads:
                dims = slice(h * HEAD_DIM, (h + 1) * HEAD_DIM)
                acc_t[dims, :] += _dot_nt(vt_ref[dims, keys], w_buf[slot, h])

        def softplus_stage(h, slot, diag):
            for c in range(tq // rc):
                rows = slice(c * rc, (c + 1) * rc)
                if _all_masked(c, diag, rc, tk):
                    hl_buf[h, rows, :] = jnp.zeros((rc, LOG_SUM_PASSES * tk), BF16)
                    tot_buf[h, rows, :] = jnp.zeros((rc, LANES), F32)
                    continue
                nz = z_buf[slot, h, rows, :]
                l1 = jnp.minimum(nz, 0.0) - jnp.log(1.0 + jnp.exp(_neg_abs(nz)))
                if _some_masked(c, diag, rc, tk):
                    l1 = jnp.where(causal(c, diag), l1, 0.0)
                for s, part in enumerate(_split(l1)[:LOG_SUM_PASSES]):
                    hl_buf[h, rows, s * tk:(s + 1) * tk] = part
                ls_buf[h, rows, :] = l1 - nz
                tot_buf[h, rows, :] = _row_sums(l1)

        def weights_stage(h, slot, diag):
            for c in range(tq // rc):
                rows = slice(c * rc, (c + 1) * rc)
                if _all_masked(c, diag, rc, tk):
                    w_buf[slot, h, rows, :] = jnp.zeros((rc, tk), BF16)
                    continue
                wgt = jnp.exp((ls_buf[h, rows, :] + aft_buf[h, rows, :]) + _across(run_buf[h, rows, :], tk))
                if _some_masked(c, diag, rc, tk):
                    wgt = jnp.where(causal(c, diag), wgt, 0.0)
                w_buf[slot, h, rows, :] = wgt.astype(BF16)
                run_buf[h, rows, :] += tot_buf[h, rows, :]

        def position(blk, slot, diag):
            scores(jnp.maximum(blk - 1, 0), 1 - slot)
            for h in heads:
                softplus_stage(h, slot, diag)
                aft_buf[h] = _dot(hl_buf[h], tri_ref[...])
            values(jnp.minimum(blk + 1, nblk - 1), 1 - slot)
            for h in heads:
                weights_stage(h, slot, diag)

        scores(nblk - 1, 0)
        for p in range(ndiag):
            position(nblk - 1 - p, p % 2, ndiag - 1 - p)

        def trip(jj, carry):
            for u in range(2):
                position(i * ndiag - 1 - 2 * jj - u, u, None)
            return carry

        lax.fori_loop(0, (i * ndiag) // 2, trip, 0)
        values(0, 1)
        o_ref[...] = acc_t[...].T.astype(BF16)
        l_ref[...] = jnp.where(first, run_buf[0], run_buf[1])

    qmap = lambda b, hp, i: (b * nq + i, hp)
    nh = HEADS_PER_BLOCK
    return pl.pallas_call(
        body, name="attn_fwd", grid=(nb, w // LANES, nq),
        in_specs=[pl.BlockSpec((tq, LANES), qmap), pl.BlockSpec((seq, LANES), lambda b, hp, i: (b, hp)),
                  pl.BlockSpec((LANES, seq), lambda b, hp, i: (hp, b)),
                  pl.BlockSpec((LOG_SUM_PASSES * tk, tk), lambda b, hp, i: (0, 0))],
        out_specs=[pl.BlockSpec((tq, LANES), qmap), pl.BlockSpec((tq, LANES), qmap)],
        out_shape=[jax.ShapeDtypeStruct((t_all, w), BF16), jax.ShapeDtypeStruct((t_all, w), F32)],
        scratch_shapes=[pltpu.VMEM((2, nh, tq, tk), F32), pltpu.VMEM((nh, tq, tk), F32),
                        pltpu.VMEM((nh, tq, LOG_SUM_PASSES * tk), BF16), pltpu.VMEM((nh, tq, tk), F32),
                        pltpu.VMEM((2, nh, tq, tk), BF16), pltpu.VMEM((nh, tq, LANES), F32),
                        pltpu.VMEM((LANES, tq), F32), pltpu.VMEM((nh, tq, LANES), F32)],
        compiler_params=_params(("arbitrary", "arbitrary", "arbitrary")),
    )(qn, k, vt, _tri_matrix(tk, "after")[:LOG_SUM_PASSES * tk])


def _window_sums(ext, rows, offset, forward):
    r = lax.broadcasted_iota(jnp.int32, (rows, rows + HALO), 0)
    e = lax.broadcasted_iota(jnp.int32, (rows, rows + HALO), 1)
    hi, lo = _split(ext)
    out = []
    for g, win in enumerate(POOL_WINDOWS):
        if forward:
            band = (e >= r) & (e < r + win)
        else:
            band = (e <= r + offset) & (e > r + offset - win)
        bm = band.astype(BF16)
        cols = slice(g * POOL_GROUP, (g + 1) * POOL_GROUP)
        out.append(_dot(bm, hi[:, cols]) + _dot(bm, lo[:, cols]))
    return out


def _window_counts(pos):
    return [jnp.minimum(pos + 1, win).astype(F32) for win in POOL_WINDOWS]


def _mixer_post(u, o, x, mod, g_post, g_fpre, w_pool, pool_scale, w_out, seq, tm):
    t_all, d = x.shape
    nt = seq // tm
    p = u.shape[1]

    def body(u_ref, halo_ref, o_ref, x_ref, mod_ref, gp_ref, gf_ref, wp_ref, ps_ref, wo_ref,
             pooled_ref, mixin_ref, mix_ref, x1_ref, h2_ref):
        it = pl.program_id(0) % nt
        uf = u_ref[...]
        halo = jnp.where(it == 0, 0.0, halo_ref[...])
        ext = jnp.concatenate([halo, uf], axis=0)
        pos = it * tm + lax.broadcasted_iota(jnp.int32, (tm, 1), 0)
        sums = _window_sums(ext, tm, HALO, False)
        cnts = _window_counts(pos)
        pools = []
        for g in range(len(POOL_WINDOWS)):
            cols = slice(g * POOL_GROUP, (g + 1) * POOL_GROUP)
            pooled = (sums[g] / cnts[g] - uf[:, cols]).astype(BF16)
            pooled_ref[:, cols] = pooled
            yg = _dot(pooled, wp_ref[g].astype(BF16))
            pools.append((yg * ps_ref[:, cols]).astype(BF16))
        mixin_ref[...] = jnp.concatenate([o_ref[...]] + pools, axis=1)
        for c in range(ROW_CHUNKS):
            rows = slice(c * (tm // ROW_CHUNKS), (c + 1) * (tm // ROW_CHUNKS))
            mix = _dot(mixin_ref[rows, :], wo_ref[...])
            mix_ref[rows, :] = mix
            n2 = mix * _rms(mix)
            x1 = x_ref[rows, :] + mod_ref[0, 2:3, :] * (n2 * gp_ref[...])
            x1_ref[rows, :] = x1
            n3 = x1 * _rms(x1)
            h2 = (n3 * gf_ref[...]) * (1.0 + mod_ref[0, 4:5, :]) + mod_ref[0, 3:4, :]
            h2_ref[rows, :] = h2.astype(BF16)

    tok = lambda i: (i, 0)
    const2 = lambda i: (0, 0)
    hb = tm // HALO
    return pl.pallas_call(
        body, name="mixer_post", grid=(t_all // tm,),
        in_specs=[pl.BlockSpec((tm, p), tok),
                  pl.BlockSpec((HALO, p), lambda i: (jnp.maximum(i * hb - 1, 0), 0)),
                  pl.BlockSpec((tm, p), tok),
                  pl.BlockSpec((tm, d), tok),
                  pl.BlockSpec((1, MOD_ROWS, d), lambda i: (i // nt, 0, 0)),
                  pl.BlockSpec((1, d), const2), pl.BlockSpec((1, d), const2),
                  pl.BlockSpec(w_pool.shape, lambda i: (0, 0, 0)),
                  pl.BlockSpec((1, p), const2),
                  pl.BlockSpec((d, d), const2)],
        out_specs=[pl.BlockSpec((tm, p), tok), pl.BlockSpec((tm, d), tok), pl.BlockSpec((tm, d), tok),
                   pl.BlockSpec((tm, d), tok), pl.BlockSpec((tm, d), tok)],
        out_shape=[jax.ShapeDtypeStruct((t_all, p), BF16), jax.ShapeDtypeStruct((t_all, d), BF16),
                   jax.ShapeDtypeStruct((t_all, d), F32), jax.ShapeDtypeStruct((t_all, d), F32),
                   jax.ShapeDtypeStruct((t_all, d), BF16)],
        compiler_params=_params(("arbitrary",)),
    )(u, u, o, x, mod, g_post, g_fpre, w_pool, pool_scale, w_out)


def _ffn_fwd(h2, w_g, w_u, w_d, x1, tgt, mod, g_post, seq, tm):
    t_all, d = x1.shape
    nt = seq // tm
    nk, ff, _ = w_g.shape

    def body(h_ref, wg_ref, wu_ref, wd_ref, x1_ref, t_ref, mod_ref, g_ref,
             a_ref, b_ref, fin_ref, dy_ref, df_ref, loss_ref, accb_ref, accg_ref, facc):
        i, k = pl.program_id(0), pl.program_id(1)

        @pl.when(k == 0)
        def _():
            facc[...] = jnp.zeros_like(facc)

        for c in range(FFN_ROW_CHUNKS):
            rows = slice(c * (tm // FFN_ROW_CHUNKS), (c + 1) * (tm // FFN_ROW_CHUNKS))
            hb = h_ref[rows, :]
            a = _dot_nt(hb, wg_ref[0])
            b = _dot_nt(hb, wu_ref[0])
            a_ref[0, rows, :] = a.astype(BF16)
            b_ref[0, rows, :] = b.astype(BF16)
            fin = ((a * _sigmoid(a)) * b).astype(BF16)
            fin_ref[0, rows, :] = fin
            facc[rows, :] += _dot(fin, wd_ref[0])

        @pl.when(k == nk - 1)
        def _():
            f = facc[...]
            r4 = _rms(f)
            n4 = f * r4
            gate = mod_ref[0, 5:6, :]
            g = g_ref[...]
            err = (x1_ref[...] + gate * (n4 * g)) - t_ref[...]
            dy = err * (1.0 / d)
            dy_ref[...] = dy

            @pl.when(i == 0)
            def _():
                loss_ref[...] = jnp.zeros_like(loss_ref)
                accg_ref[...] = jnp.zeros_like(accg_ref)

            @pl.when(i % nt == 0)
            def _():
                accb_ref[...] = jnp.zeros_like(accb_ref)

            loss_ref[...] += (0.5 / d) * jnp.sum(err * err)
            accb_ref[0, 0:1, :] += _colsum(dy * (n4 * g))
            accg_ref[0:1, :] += _colsum((dy * gate) * n4)
            dn4 = (dy * gate) * g
            df_ref[...] = _norm_bwd(dn4, n4, r4).astype(BF16)

    tok = lambda i, k: (i, 0)
    ktok = lambda i, k: (k, i, 0)
    kw = lambda i, k: (k, 0, 0)
    const2 = lambda i, k: (0, 0)
    return pl.pallas_call(
        body, name="ffn_fwd", grid=(t_all // tm, nk),
        in_specs=[pl.BlockSpec((tm, d), tok),
                  pl.BlockSpec((1, ff, d), kw), pl.BlockSpec((1, ff, d), kw), pl.BlockSpec((1, ff, d), kw),
                  pl.BlockSpec((tm, d), tok), pl.BlockSpec((tm, d), tok),
                  pl.BlockSpec((1, MOD_ROWS, d), lambda i, k: (i // nt, 0, 0)),
                  pl.BlockSpec((1, d), const2)],
        out_specs=[pl.BlockSpec((1, tm, ff), ktok)] * 3
        + [pl.BlockSpec((tm, d), tok), pl.BlockSpec((tm, d), tok),
           pl.BlockSpec((8, LANES), const2),
           pl.BlockSpec((1, 8, d), lambda i, k: (i // nt, 0, 0)),
           pl.BlockSpec((8, d), const2)],
        out_shape=[jax.ShapeDtypeStruct((nk, t_all, ff), BF16)] * 3
        + [jax.ShapeDtypeStruct((t_all, d), F32), jax.ShapeDtypeStruct((t_all, d), BF16),
           jax.ShapeDtypeStruct((8, LANES), F32),
           jax.ShapeDtypeStruct((t_all // seq, 8, d), F32),
           jax.ShapeDtypeStruct((8, d), F32)],
        scratch_shapes=[pltpu.VMEM((tm, d), F32)],
        compiler_params=_params(("arbitrary", "arbitrary")),
    )(h2, w_g, w_u, w_d, x1, tgt, mod, g_post)


def _ffn_bwd(df, a, b, w_d, w_g, w_u, x1, dy, mix, mod, g_fpre, g_mpost, seq, tm):
    t_all, d = x1.shape
    nt = seq // tm
    nk, ff, _ = w_g.shape

    def body(df_ref, a_ref, b_ref, wd_ref, wg_ref, wu_ref, x1_ref, dy_ref, mix_ref, mod_ref, gf_ref, gm_ref,
             da_ref, db_ref, dx1_ref, dmix_ref, accb_ref, accg_ref, hacc):
        i, k = pl.program_id(0), pl.program_id(1)

        @pl.when(k == 0)
        def _():
            hacc[...] = jnp.zeros_like(hacc)

        for c in range(FFN_ROW_CHUNKS):
            rows = slice(c * (tm // FFN_ROW_CHUNKS), (c + 1) * (tm // FFN_ROW_CHUNKS))
            dfin = _dot_nt(df_ref[rows, :], wd_ref[0])
            af = a_ref[0, rows, :].astype(F32)
            bf = b_ref[0, rows, :].astype(F32)
            sig = _sigmoid(af)
            da = ((dfin * bf) * (sig * (1.0 + af * (1.0 - sig)))).astype(BF16)
            db = (dfin * (af * sig)).astype(BF16)
            da_ref[0, rows, :] = da
            db_ref[0, rows, :] = db
            hacc[rows, :] += _dot(da, wg_ref[0]) + _dot(db, wu_ref[0])

        @pl.when(k == nk - 1)
        def _():
            @pl.when(i == 0)
            def _():
                accg_ref[...] = jnp.zeros_like(accg_ref)

            @pl.when(i % nt == 0)
            def _():
                accb_ref[...] = jnp.zeros_like(accb_ref)

            dh2 = hacc[...]
            x1 = x1_ref[...]
            r3 = _rms(x1)
            n3 = x1 * r3
            g3 = gf_ref[...]
            scale1 = 1.0 + mod_ref[0, 4:5, :]
            accb_ref[0, 0:1, :] += _colsum(dh2)
            accb_ref[0, 1:2, :] += _colsum(dh2 * (n3 * g3))
            accg_ref[0:1, :] += _colsum((dh2 * scale1) * n3)
            dx1 = dy_ref[...] + _norm_bwd((dh2 * scale1) * g3, n3, r3)
            dx1_ref[...] = dx1
            mix = mix_ref[...]
            r2 = _rms(mix)
            n2 = mix * r2
            g2 = gm_ref[...]
            gate = mod_ref[0, 2:3, :]
            accb_ref[0, 2:3, :] += _colsum(dx1 * (n2 * g2))
            accg_ref[1:2, :] += _colsum((dx1 * gate) * n2)
            dmix_ref[...] = _norm_bwd((dx1 * gate) * g2, n2, r2).astype(BF16)

    tok = lambda i, k: (i, 0)
    ktok = lambda i, k: (k, i, 0)
    kw = lambda i, k: (k, 0, 0)
    const2 = lambda i, k: (0, 0)
    return pl.pallas_call(
        body, name="ffn_bwd", grid=(t_all // tm, nk),
        in_specs=[pl.BlockSpec((tm, d), tok),
                  pl.BlockSpec((1, tm, ff), ktok), pl.BlockSpec((1, tm, ff), ktok),
                  pl.BlockSpec((1, ff, d), kw), pl.BlockSpec((1, ff, d), kw), pl.BlockSpec((1, ff, d), kw),
                  pl.BlockSpec((tm, d), tok), pl.BlockSpec((tm, d), tok), pl.BlockSpec((tm, d), tok),
                  pl.BlockSpec((1, MOD_ROWS, d), lambda i, k: (i // nt, 0, 0)),
                  pl.BlockSpec((1, d), const2), pl.BlockSpec((1, d), const2)],
        out_specs=[pl.BlockSpec((1, tm, ff), ktok)] * 2
        + [pl.BlockSpec((tm, d), tok), pl.BlockSpec((tm, d), tok),
           pl.BlockSpec((1, 8, d), lambda i, k: (i // nt, 0, 0)),
           pl.BlockSpec((8, d), const2)],
        out_shape=[jax.ShapeDtypeStruct((nk, t_all, ff), BF16)] * 2
        + [jax.ShapeDtypeStruct((t_all, d), F32), jax.ShapeDtypeStruct((t_all, d), BF16),
           jax.ShapeDtypeStruct((t_all // seq, 8, d), F32),
           jax.ShapeDtypeStruct((8, d), F32)],
        scratch_shapes=[pltpu.VMEM((tm, d), F32)],
        compiler_params=_params(("arbitrary", "arbitrary")),
    )(df, a, b, w_d, w_g, w_u, x1, dy, mix, mod, g_fpre, g_mpost)


def _mixer_bwd(dmix, w_out, pooled, w_pool, pool_scale, seq, tm):
    t_all, d = dmix.shape
    p = pooled.shape[1]
    ng = len(POOL_WINDOWS)

    def body(dm_ref, wo_ref, pooled_ref, wp_ref, ps_ref, do_ref, dpd_ref, dps_ref, dwp_ref):
        i = pl.program_id(0)

        @pl.when(i == 0)
        def _():
            dps_ref[...] = jnp.zeros_like(dps_ref)
            dwp_ref[...] = jnp.zeros_like(dwp_ref)

        dmixin = _dot_nt(dm_ref[...], wo_ref[...])
        do_ref[...] = dmixin[:, :p].astype(BF16)
        for g in range(ng):
            cols = slice(g * POOL_GROUP, (g + 1) * POOL_GROUP)
            dpool = dmixin[:, p + g * POOL_GROUP:p + (g + 1) * POOL_GROUP]
            pooled = pooled_ref[:, cols]
            wpg = wp_ref[g].astype(BF16)
            yg = _dot(pooled, wpg)
            dps_ref[0:1, cols] += _colsum(dpool * yg)
            dyg = (dpool * ps_ref[:, cols]).astype(BF16)
            dwp_ref[g] += _dot_tn(pooled, dyg)
            dpd_ref[:, cols] = _dot_nt(dyg, wpg)

    tok = lambda i: (i, 0)
    const2 = lambda i: (0, 0)
    const3 = lambda i: (0, 0, 0)
    return pl.pallas_call(
        body, name="mixer_bwd", grid=(t_all // tm,),
        in_specs=[pl.BlockSpec((tm, d), tok), pl.BlockSpec((d, d), const2), pl.BlockSpec((tm, p), tok),
                  pl.BlockSpec(w_pool.shape, const3), pl.BlockSpec((1, p), const2)],
        out_specs=[pl.BlockSpec((tm, p), tok), pl.BlockSpec((tm, p), tok),
                   pl.BlockSpec((8, p), const2), pl.BlockSpec(w_pool.shape, const3)],
        out_shape=[jax.ShapeDtypeStruct((t_all, p), BF16), jax.ShapeDtypeStruct((t_all, p), F32),
                   jax.ShapeDtypeStruct((8, p), F32), jax.ShapeDtypeStruct(w_pool.shape, F32)],
        compiler_params=_params(("arbitrary",)),
    )(dmix, w_out, pooled, w_pool, pool_scale)


def _attn_bwd(qn, k, kt, v, do, ltot, seq, tq, tk):
    t_all, w = qn.shape
    nb, nq, ndiag, nkb = t_all // seq, seq // tq, tq // tk, seq // tk
    assert ndiag % 2 == 0, "two key blocks per loop trip"
    rc = ATTN_ROW_CHUNK
    nh = HEADS_PER_BLOCK
    heads = range(nh)

    def body(q_ref, k_ref, kt_ref, v_ref, do_ref, l_ref, up_ref, bf_ref, dq_ref, dk_ref, dv_ref,
             z_buf, dw_buf, ls_buf, hl_buf, upto_buf, g_buf, gb_buf, before_buf, w_buf, dz_buf,
             totl_buf, totg_buf, rem_buf, preg_buf, qnt_buf, dot_buf, dq_t, dk_t, dv_t):
        i = pl.program_id(2)
        nblk = (i + 1) * ndiag

        @pl.when(i == 0)
        def _():
            dk_t[...] = jnp.zeros_like(dk_t)
            dv_t[...] = jnp.zeros_like(dv_t)

        lane = lax.broadcasted_iota(jnp.int32, (1, LANES), 1)
        row = lax.broadcasted_iota(jnp.int32, (rc, tk), 0)
        col = lax.broadcasted_iota(jnp.int32, (rc, tk), 1)
        first = lane < HEAD_DIM
        q2 = q_ref[...]
        do2 = do_ref[...]
        l2 = l_ref[...]
        qs = [jnp.where(first, q2, jnp.zeros_like(q2)), jnp.where(first, jnp.zeros_like(q2), q2)]
        dos = [jnp.where(first, do2, jnp.zeros_like(do2)), jnp.where(first, jnp.zeros_like(do2), do2)]
        qnt_buf[...] = q2.astype(F32).T.astype(BF16)
        dot_buf[...] = do2.astype(F32).T.astype(BF16)
        for h in heads:
            rem_buf[h] = jnp.where(first if h == 0 else ~first, l2, pltpu.roll(l2, HEAD_DIM, 1))
        preg_buf[...] = jnp.zeros_like(preg_buf)
        dq_t[...] = jnp.zeros_like(dq_t)
        w_buf[1] = jnp.zeros((nh * tq, tk), BF16)
        dz_buf[1] = jnp.zeros((nh * tq, tk), BF16)

        def causal(c, diag):
            return (col + diag * tk) < (row + c * rc)

        def scores(blk, slot):
            off = pl.multiple_of(blk * tk, tk)
            kj = k_ref[pl.ds(off, tk), :]
            vj = v_ref[pl.ds(off, tk), :]
            for h in heads:
                z_buf[slot, h] = _dot_nt(qs[h], kj)
                dw_buf[slot, h] = _dot_nt(dos[h], vj)

        def gradients(blk, slot):
            keys = pl.ds(pl.multiple_of(blk * tk, tk), tk)
            for h in heads:
                dims = slice(h * HEAD_DIM, (h + 1) * HEAD_DIM)
                queries = slice(h * tq, (h + 1) * tq)
                dq_t[dims, :] += _dot_nt(kt_ref[dims, keys], dz_buf[slot, queries, :])
                dk_t[blk, dims, :] += _dot(qnt_buf[dims, :], dz_buf[slot, queries, :])
                dv_t[blk, dims, :] += _dot(dot_buf[dims, :], w_buf[slot, queries, :])

        def softplus_stage(h, slot, diag):
            for c in range(tq // rc):
                rows = slice(c * rc, (c + 1) * rc)
                if _all_masked(c, diag, rc, tk):
                    hl_buf[h, rows, :] = jnp.zeros((rc, LOG_SUM_PASSES * tk), BF16)
                    continue
                nz = z_buf[slot, h, rows, :]
                l1 = jnp.minimum(nz, 0.0) - jnp.log(1.0 + jnp.exp(_neg_abs(nz)))
                if _some_masked(c, diag, rc, tk):
                    l1 = jnp.where(causal(c, diag), l1, 0.0)
                for s, part in enumerate(_split(l1)[:LOG_SUM_PASSES]):
                    hl_buf[h, rows, s * tk:(s + 1) * tk] = part
                ls_buf[h, rows, :] = l1 - nz
                totl_buf[h, rows, :] = _row_sums(l1)

        def weights_stage(h, slot, diag):
            for c in range(tq // rc):
                rows = slice(c * rc, (c + 1) * rc)
                stacked = slice(h * tq + c * rc, h * tq + (c + 1) * rc)
                if _all_masked(c, diag, rc, tk):
                    w_buf[slot, stacked, :] = jnp.zeros((rc, tk), BF16)
                    gb_buf[h, rows, :] = jnp.zeros((rc, tk), BF16)
                    continue
                wgt = jnp.exp(ls_buf[h, rows, :] + (_across(rem_buf[h, rows, :], tk) - upto_buf[h, rows, :]))
                if _some_masked(c, diag, rc, tk):
                    wgt = jnp.where(causal(c, diag), wgt, 0.0)
                w_buf[slot, stacked, :] = wgt.astype(BF16)
                g = wgt * dw_buf[slot, h, rows, :]
                g_buf[h, rows, :] = g
                gb_buf[h, rows, :] = g.astype(BF16)
                totg_buf[h, rows, :] = _row_sums(g)
                rem_buf[h, rows, :] -= totl_buf[h, rows, :]

        def dscore_stage(h, slot, diag):
            for c in range(tq // rc):
                rows = slice(c * rc, (c + 1) * rc)
                stacked = slice(h * tq + c * rc, h * tq + (c + 1) * rc)
                if _all_masked(c, diag, rc, tk):
                    dz_buf[slot, stacked, :] = jnp.zeros((rc, tk), BF16)
                    continue
                sig = jnp.exp(ls_buf[h, rows, :])
                g = g_buf[h, rows, :]
                dnz = sig * ((before_buf[h, rows, :] + _across(preg_buf[h, rows, :], tk)) + g) - g
                if _some_masked(c, diag, rc, tk):
                    dnz = jnp.where(causal(c, diag), dnz, 0.0)
                dz_buf[slot, stacked, :] = dnz.astype(BF16)
                preg_buf[h, rows, :] += totg_buf[h, rows, :]

        def position(blk, slot, diag, prefetch):
            if prefetch:
                scores(blk + 1, 1 - slot)
            for h in heads:
                softplus_stage(h, slot, diag)
                upto_buf[h] = _dot(hl_buf[h], up_ref[...])
            gradients(jnp.maximum(blk - 1, 0), 1 - slot)
            for h in heads:
                weights_stage(h, slot, diag)
                before_buf[h] = _dot(gb_buf[h], bf_ref[...])
            for h in heads:
                dscore_stage(h, slot, diag)

        scores(0, 0)

        def trip(jj, carry):
            for u in range(2):
                position(2 * jj + u, u, None, True)
            return carry

        lax.fori_loop(0, (i * ndiag) // 2, trip, 0)
        for d in range(ndiag):
            position(i * ndiag + d, d % 2, d, d < ndiag - 1)
        gradients(nblk - 1, 1)
        dq_ref[...] = (dq_t[...].T * NEG_QK_SCALE).astype(BF16)

        @pl.when(i == nq - 1)
        def _():
            for blk in range(nkb):
                dk_ref[blk * tk:(blk + 1) * tk, :] = dk_t[blk].T.astype(BF16)
                dv_ref[blk * tk:(blk + 1) * tk, :] = dv_t[blk].T.astype(BF16)

    qmap = lambda b, hp, i: (b * nq + i, hp)
    kmap = lambda b, hp, i: (b, hp)
    const = lambda b, hp, i: (0, 0)
    return pl.pallas_call(
        body, name="attn_bwd", grid=(nb, w // LANES, nq),
        in_specs=[pl.BlockSpec((tq, LANES), qmap), pl.BlockSpec((seq, LANES), kmap),
                  pl.BlockSpec((LANES, seq), lambda b, hp, i: (hp, b)), pl.BlockSpec((seq, LANES), kmap),
                  pl.BlockSpec((tq, LANES), qmap), pl.BlockSpec((tq, LANES), qmap),
                  pl.BlockSpec((LOG_SUM_PASSES * tk, tk), const), pl.BlockSpec((tk, tk), const)],
        out_specs=[pl.BlockSpec((tq, LANES), qmap), pl.BlockSpec((seq, LANES), kmap), pl.BlockSpec((seq, LANES), kmap)],
        out_shape=[jax.ShapeDtypeStruct((t_all, w), BF16)] * 3,
        scratch_shapes=[pltpu.VMEM((2, nh, tq, tk), F32), pltpu.VMEM((2, nh, tq, tk), F32),
                        pltpu.VMEM((nh, tq, tk), F32), pltpu.VMEM((nh, tq, LOG_SUM_PASSES * tk), BF16),
                        pltpu.VMEM((nh, tq, tk), F32), pltpu.VMEM((nh, tq, tk), F32),
                        pltpu.VMEM((nh, tq, tk), BF16), pltpu.VMEM((nh, tq, tk), F32),
                        pltpu.VMEM((2, nh * tq, tk), BF16), pltpu.VMEM((2, nh * tq, tk), BF16),
                        pltpu.VMEM((nh, tq, LANES), F32), pltpu.VMEM((nh, tq, LANES), F32),
                        pltpu.VMEM((nh, tq, LANES), F32), pltpu.VMEM((nh, tq, LANES), F32),
                        pltpu.VMEM((LANES, tq), BF16), pltpu.VMEM((LANES, tq), BF16),
                        pltpu.VMEM((LANES, tq), F32), pltpu.VMEM((nkb, LANES, tk), F32),
                        pltpu.VMEM((nkb, LANES, tk), F32)],
        compiler_params=_params(("arbitrary", "arbitrary", "arbitrary")),
    )(qn, k, kt, v, do, ltot, _tri_matrix(tk, "upto")[:LOG_SUM_PASSES * tk], _tri_matrix(tk, "before")[:tk])


def _inproj_bwd(dq, dk, dv, dpd, x, dx1, mod, g_pre, w_in, seq, tm):
    t_all, d = x.shape
    nt = seq // tm
    p = dq.shape[1]

    def body(dq_ref, dk_ref, dv_ref, dpd_ref, halo_ref, x_ref, dx1_ref, mod_ref, g_ref, w_ref,
             gx_ref, du_ref, accb_ref, accg_ref):
        i = pl.program_id(0)
        it = i % nt

        @pl.when(i == 0)
        def _():
            accg_ref[...] = jnp.zeros_like(accg_ref)

        @pl.when(it == 0)
        def _():
            accb_ref[...] = jnp.zeros_like(accb_ref)

        dpd = dpd_ref[...]
        pos = it * tm + lax.broadcasted_iota(jnp.int32, (tm, 1), 0)
        cnts = _window_counts(pos)
        halo = jnp.where(it == nt - 1, 0.0, halo_ref[...])
        scaled = []
        halos = []
        for g, win in enumerate(POOL_WINDOWS):
            cols = slice(g * POOL_GROUP, (g + 1) * POOL_GROUP)
            scaled.append(dpd[:, cols] / cnts[g])
            halos.append(halo[:, cols] / float(win))
        ext = jnp.concatenate([jnp.concatenate(scaled, axis=1), jnp.concatenate(halos, axis=1)], axis=0)
        sums = _window_sums(ext, tm, 0, True)
        du = (jnp.concatenate(sums, axis=1) - dpd).astype(BF16)
        du_ref[...] = du
        g1 = g_ref[...]
        scale1 = 1.0 + mod_ref[0, 1:2, :]
        for c in range(ROW_CHUNKS):
            rows = slice(c * (tm // ROW_CHUNKS), (c + 1) * (tm // ROW_CHUNKS))
            dh1 = (_dot_nt(dq_ref[rows, :], w_ref[0]) + _dot_nt(dk_ref[rows, :], w_ref[1])
                   + _dot_nt(dv_ref[rows, :], w_ref[2]) + _dot_nt(du_ref[rows, :], w_ref[3]))
            xf = x_ref[rows, :]
            r1 = _rms(xf)
            n1 = xf * r1
            accb_ref[0, 0:1, :] += _colsum(dh1)
            accb_ref[0, 1:2, :] += _colsum(dh1 * (n1 * g1))
            accg_ref[0:1, :] += _colsum((dh1 * scale1) * n1)
            gx_ref[rows, :] = dx1_ref[rows, :] + _norm_bwd((dh1 * scale1) * g1, n1, r1)

    tok = lambda i: (i, 0)
    const2 = lambda i: (0, 0)
    hb = tm // HALO
    last = t_all // HALO - 1
    return pl.pallas_call(
        body, name="inproj_bwd", grid=(t_all // tm,),
        in_specs=[pl.BlockSpec((tm, p), tok), pl.BlockSpec((tm, p), tok), pl.BlockSpec((tm, p), tok),
                  pl.BlockSpec((tm, p), tok),
                  pl.BlockSpec((HALO, p), lambda i: (jnp.minimum((i + 1) * hb, last), 0)),
                  pl.BlockSpec((tm, d), tok), pl.BlockSpec((tm, d), tok),
                  pl.BlockSpec((1, MOD_ROWS, d), lambda i: (i // nt, 0, 0)),
                  pl.BlockSpec((1, d), const2),
                  pl.BlockSpec((N_CHIPS, d, p), lambda i: (0, 0, 0))],
        out_specs=[pl.BlockSpec((tm, d), tok), pl.BlockSpec((tm, p), tok),
                   pl.BlockSpec((1, 8, d), lambda i: (i // nt, 0, 0)),
                   pl.BlockSpec((8, d), const2)],
        out_shape=[jax.ShapeDtypeStruct((t_all, d), F32), jax.ShapeDtypeStruct((t_all, p), BF16),
                   jax.ShapeDtypeStruct((t_all // seq, 8, d), F32),
                   jax.ShapeDtypeStruct((8, d), F32)],
        compiler_params=_params(("arbitrary",)),
    )(dq, dk, dv, dpd, dpd, x, dx1, mod, g_pre, w_in)


def _tn_matmul(x, ys, nk, bt, name, after=()):
    t_all = x.shape[-2]
    m = x.shape[-1]
    ny = len(ys)
    nt = t_all // bt

    def spec(arr):
        if arr.ndim == 3:
            return pl.BlockSpec((1, bt, arr.shape[-1]), lambda k, t: (k, t, 0))
        return pl.BlockSpec((bt, arr.shape[-1]), lambda k, t: (t, 0))

    def tile(ref):
        return ref[0] if len(ref.shape) == 3 else ref[...]

    def body(*refs):
        outs = refs[1 + ny + len(after):]
        x_ref, y_refs, o_refs, h_refs = refs[0], refs[1:1 + ny], outs[:ny], outs[ny:]
        t = pl.program_id(1)
        xt = tile(x_ref)
        for y_ref, o_ref, h_ref in zip(y_refs, o_refs, h_refs):
            part = _dot_tn(xt, tile(y_ref))

            @pl.when(t == 0)
            def _(o_ref=o_ref, part=part):
                o_ref[0] = part

            @pl.when(t > 0)
            def _(o_ref=o_ref, part=part):
                o_ref[0] += part

            @pl.when(t == nt - 1)
            def _(o_ref=o_ref, h_ref=h_ref):
                h_ref[0] = o_ref[0].astype(BF16)

    out_specs = [pl.BlockSpec((1, m, y.shape[-1]), lambda k, t: (k, 0, 0)) for y in ys]
    out = pl.pallas_call(
        body, name=name, grid=(nk, nt),
        in_specs=[spec(x)] + [spec(y) for y in ys] + [_ANY] * len(after),
        out_specs=out_specs * 2,
        out_shape=[jax.ShapeDtypeStruct((nk, m, y.shape[-1]), dt) for dt in (F32, BF16) for y in ys],
        compiler_params=_params(("arbitrary", "arbitrary")),
    )(x, *ys, *after)
    return out[:ny], out[ny:]


def _tn_matmul_stacked(x, ys, bt, name, after=()):
    t_all, m = x.shape
    n = ys[0].shape[1]
    ny = len(ys)
    nt = t_all // bt

    def body(*refs):
        x_ref, y_refs, (o_ref, h_ref) = refs[0], refs[1:1 + ny], refs[1 + ny + len(after):]
        t = pl.program_id(0)
        xt = x_ref[...]

        @pl.when(t == 0)
        def _():
            o_ref[...] = jnp.zeros_like(o_ref)

        for j, y_ref in enumerate(y_refs):
            o_ref[j] += _dot_tn(xt, y_ref[...])

        @pl.when(t == nt - 1)
        def _():
            h_ref[...] = o_ref[...].astype(BF16)

    whole = pl.BlockSpec((ny, m, n), lambda t: (0, 0, 0))
    return pl.pallas_call(
        body, name=name, grid=(nt,),
        in_specs=[pl.BlockSpec((bt, m), lambda t: (t, 0))] + [pl.BlockSpec((bt, n), lambda t: (t, 0))] * ny
        + [_ANY] * len(after),
        out_specs=[whole, whole],
        out_shape=[jax.ShapeDtypeStruct((ny, m, n), F32), jax.ShapeDtypeStruct((ny, m, n), BF16)],
        compiler_params=_params(("arbitrary",)),
    )(x, *ys, *after)


def _cond_fwd(c_all, w_q, b_q, bn):
    nrow, d = c_all.shape
    ncol = w_q.shape[1]

    def body(c_ref, w_ref, b_ref, sc_ref, mod_ref):
        cf = c_ref[...]
        sc = cf * _sigmoid(cf)
        sc_ref[...] = sc
        shi, slo = _split(sc)
        whi, wlo = _split(w_ref[...])
        mod_ref[...] = (_dot(shi, whi) + _dot(shi, wlo) + _dot(slo, whi)) + b_ref[...]

    return pl.pallas_call(
        body, name="cond_fwd", grid=(ncol // bn,),
        in_specs=[pl.BlockSpec((nrow, d), lambda n: (0, 0)), pl.BlockSpec((d, bn), lambda n: (0, n)),
                  pl.BlockSpec((1, bn), lambda n: (0, n))],
        out_specs=[pl.BlockSpec((nrow, d), lambda n: (0, 0)), pl.BlockSpec((nrow, bn), lambda n: (0, n))],
        out_shape=[jax.ShapeDtypeStruct((nrow, d), F32), jax.ShapeDtypeStruct((nrow, ncol), F32)],
        compiler_params=_params(("arbitrary",)),
    )(c_all, w_q, b_q)


def _cond_bwd(sc_all, dmod_q, bn):
    nrow, d = sc_all.shape
    ncol = dmod_q.shape[1]

    def body(sc_ref, dm_ref, gw_ref):
        shi, slo = _split(sc_ref[...])
        dhi, dlo = _split(dm_ref[...])
        gw_ref[...] = _dot_tn(shi, dhi) + _dot_tn(shi, dlo) + _dot_tn(slo, dhi)

    return pl.pallas_call(
        body, name="cond_bwd", grid=(ncol // bn,),
        in_specs=[pl.BlockSpec((nrow, d), lambda n: (0, 0)), pl.BlockSpec((nrow, bn), lambda n: (0, n))],
        out_specs=pl.BlockSpec((d, bn), lambda n: (0, n)),
        out_shape=jax.ShapeDtypeStruct((d, ncol), F32),
        compiler_params=_params(("arbitrary",)),
    )(sc_all, dmod_q)


def _row_block(rows, cols, budget=1 << 18):
    best = None
    for br in range(8, rows + 1, 8):
        if rows % br == 0 and br * cols <= budget:
            best = br
    return best if best is not None else rows


def _adam_math(w, g, m, v):
    c1 = 1.0 - ADAM_B1 ** ADAM_STEP
    c2 = 1.0 - ADAM_B2 ** ADAM_STEP
    m2 = ADAM_B1 * m + (1.0 - ADAM_B1) * g
    v2 = ADAM_B2 * v + (1.0 - ADAM_B2) * (g * g)
    return -ADAM_LR * ((m2 / c1) / (jnp.sqrt(v2 / c2) + ADAM_EPS) + ADAM_WD * w), m2, v2


def _small_updates(summed, params):
    n = len(params)

    def body(s_ref, *refs):
        ins, outs = refs[:3 * n], refs[3 * n:]
        for p, (_, _, _, pick) in enumerate(params):
            w_ref, m_ref, v_ref = ins[3 * p:3 * p + 3]
            g = pick(s_ref)
            delta, m2, v2 = _adam_math(w_ref[...], g, m_ref[...], v_ref[...])
            for o_ref, val in zip(outs[4 * p:4 * p + 4], (g, delta, m2, v2)):
                o_ref[...] = val

    out = pl.pallas_call(
        body, name="adamw_small",
        out_shape=[jax.ShapeDtypeStruct(w.shape, F32) for w, _, _, _ in params for _ in range(4)],
        compiler_params=pltpu.CompilerParams(vmem_limit_bytes=VMEM_LIMIT),
    )(summed, *[t for w, m, v, _ in params for t in (w, m, v)])
    return [tuple(out[4 * p:4 * p + 4]) for p in range(n)]


def _adamw(w, g, m, v, name):
    rows, cols = w.shape
    br = _row_block(rows, cols)

    def body(w_ref, g_ref, m_ref, v_ref, d_ref, nm_ref, nv_ref):
        d_ref[...], nm_ref[...], nv_ref[...] = _adam_math(w_ref[...], g_ref[...], m_ref[...], v_ref[...])

    blk = pl.BlockSpec((br, cols), lambda i: (i, 0))
    return pl.pallas_call(
        body, name=name, grid=(rows // br,),
        in_specs=[blk] * 4, out_specs=[blk] * 3,
        out_shape=[jax.ShapeDtypeStruct((rows, cols), F32)] * 3,
        compiler_params=_params(("arbitrary",)),
    )(w, g, m, v)


def _all_gather(x_shard, name):
    m_per, n = x_shard.shape

    def body(x_ref, out_ref, send_sems, recv_sems, local_sem):
        x, y, c = _position()
        me, sibling = (x, y, c), (x, y, 1 - c)
        chips = [(1 - x, y), (x, 1 - y), (1 - x, 1 - y)]

        def rows(px, py, pc):
            return out_ref.at[pl.ds((4 * px + 2 * py + pc) * m_per, m_per), :]

        def copy(k, block, to, src=None):
            return pltpu.make_async_remote_copy(
                src_ref=rows(*block) if src is None else src, dst_ref=rows(*block),
                send_sem=send_sems.at[k], recv_sem=recv_sems.at[k], device_id=to, device_id_type=MESH)

        mine = pltpu.make_async_copy(x_ref, rows(*me), local_sem)
        mine.start()
        first = [copy(0, me, sibling, src=x_ref)]
        first += [copy(1 + j, me, (*chip, c), src=x_ref) for j, chip in enumerate(chips)]
        for cp in first:
            cp.start()
        passed = [copy(4 + j, (*chip, c), sibling) for j, chip in enumerate(chips)]
        for j, chip in enumerate(chips):
            copy(1 + j, (*chip, c), me).wait_recv()
            passed[j].start()
        copy(0, sibling, me).wait_recv()
        for j, chip in enumerate(chips):
            copy(4 + j, (*chip, 1 - c), me).wait_recv()
        for cp in first + passed:
            cp.wait_send()
        mine.wait()

    return pl.pallas_call(
        body, name=name,
        out_shape=jax.ShapeDtypeStruct((N_DEV * m_per, n), x_shard.dtype),
        in_specs=[pl.BlockSpec(memory_space=pltpu.VMEM)],
        out_specs=pl.BlockSpec(memory_space=pltpu.VMEM),
        scratch_shapes=[pltpu.SemaphoreType.DMA((7,)), pltpu.SemaphoreType.DMA((7,)), pltpu.SemaphoreType.DMA],
        compiler_params=pltpu.CompilerParams(vmem_limit_bytes=VMEM_LIMIT),
    )(x_shard)


_ANY = pl.BlockSpec(memory_space=pl.ANY)


def _place_quarters(place, quarters):
    steps = 2

    def body(place_ref, *refs):
        n = len(refs) // 2
        for w_ref, o_ref in zip(refs[:n], refs[n:]):
            o_ref[0] = w_ref[...].astype(BF16)

    return pl.pallas_call(
        body, name="place_quarters",
        grid_spec=pltpu.PrefetchScalarGridSpec(
            num_scalar_prefetch=1, grid=(steps,),
            in_specs=[pl.BlockSpec((q.shape[0] // steps, q.shape[1]), lambda r, place_ref: (r, 0)) for q in quarters],
            out_specs=[pl.BlockSpec((1, q.shape[0] // steps, q.shape[1]), lambda r, place_ref: (place_ref[0], r, 0))
                       for q in quarters]),
        out_shape=[jax.ShapeDtypeStruct((N_CHIPS,) + q.shape, BF16) for q in quarters],
        compiler_params=_params(("arbitrary",)),
    )(place, *quarters)


_HBM = pl.BlockSpec(memory_space=pltpu.HBM)
_SEM = pl.BlockSpec(memory_space=pltpu.SEMAPHORE)
_EFFECT = pltpu.SideEffectType.DATAFLOW_SIDE_EFFECTING


def _quarter_halves(shapes, a, which):
    hr = shapes[a][0] // 2
    return pl.ds(which * hr, hr)


def _gather_start(placed, after, tag):
    n = len(placed)
    m = len(after)
    shapes = [b.shape[1:] for b in placed]

    def body(*refs):
        g_refs = refs[:n]
        send_sems, recv_sems = refs[n + m], refs[n + m + 1]
        token = refs[2 * n + m + 2]
        x, y, c = _position()
        chips = [(1 - x, y), (x, 1 - y), (1 - x, 1 - y)]
        mine = 2 * x + y
        for a in range(n):
            ref = g_refs[a].at[mine, _quarter_halves(shapes, a, c), :]
            for p in range(3):
                pltpu.make_async_remote_copy(
                    src_ref=ref, dst_ref=ref, send_sem=send_sems.at[3 * a + p], recv_sem=recv_sems.at[3 * a + p],
                    device_id=(*chips[p], c), device_id_type=MESH).start()
        token[...] = jnp.zeros_like(token)

    out = pl.pallas_call(
        body, name="gather_start_" + tag,
        out_shape=(pltpu.SemaphoreType.DMA((3 * n,)), pltpu.SemaphoreType.DMA((3 * n,)),
                   *[pltpu.HBM(b.shape, b.dtype) for b in placed], jax.ShapeDtypeStruct((8, LANES), F32)),
        in_specs=[_HBM] * n + [_ANY] * m,
        out_specs=(_SEM, _SEM, *[_HBM] * n, pl.BlockSpec(memory_space=pltpu.VMEM)),
        input_output_aliases={a: 2 + a for a in range(n)},
        compiler_params=pltpu.CompilerParams(has_side_effects=_EFFECT),
    )(*[pltpu.with_memory_space_constraint(b, pltpu.HBM) for b in placed], *after)
    return out[0], out[1], list(out[2:2 + n]), out[2 + n]


def _gather_wait(send_sems, recv_sems, thru, after, tag):
    n = len(thru)
    shapes = [b.shape[1:] for b in thru]

    def body(*refs):
        g_refs = refs[:n]
        send_sems, recv_sems = refs[n], refs[n + 1]
        x, y, c = _position()
        chips = [(1 - x, y), (x, 1 - y), (1 - x, 1 - y)]
        mine = 2 * x + y
        for a in range(n):
            rows = _quarter_halves(shapes, a, c)
            for p, (cx, cy) in enumerate(chips):
                copy = pltpu.make_async_remote_copy(
                    src_ref=g_refs[a].at[mine, rows, :], dst_ref=g_refs[a].at[2 * cx + cy, rows, :],
                    send_sem=send_sems.at[3 * a + p], recv_sem=recv_sems.at[3 * a + p],
                    device_id=(cx, cy, c), device_id_type=MESH)
                copy.wait_send()
                copy.wait_recv()

    return pl.pallas_call(
        body, name="gather_wait_" + tag,
        out_shape=[pltpu.HBM(b.shape, b.dtype) for b in thru],
        in_specs=[_HBM] * n + [_SEM, _SEM, _ANY], out_specs=[_HBM] * n,
        input_output_aliases={a: a for a in range(n)},
        compiler_params=pltpu.CompilerParams(has_side_effects=_EFFECT),
    )(*thru, send_sems, recv_sems, after)


def _gather_forward(bufs, tag):
    n = len(bufs)
    shapes = [b.shape[1:] for b in bufs]

    def body(*refs):
        g_refs = refs[n:2 * n]
        send_sems, recv_sems = refs[2 * n:]
        x, y, c = _position()
        chips = [(1 - x, y), (x, 1 - y), (1 - x, 1 - y)]

        def over_d2d(a, p, which):
            cx, cy = chips[p]
            ref = g_refs[a].at[2 * cx + cy, _quarter_halves(shapes, a, which), :]
            return pltpu.make_async_remote_copy(
                src_ref=ref, dst_ref=ref, send_sem=send_sems.at[3 * a + p], recv_sem=recv_sems.at[3 * a + p],
                device_id=(x, y, 1 - c), device_id_type=MESH)

        sends = [over_d2d(a, p, c) for a in range(n) for p in range(3)]
        for cp in sends:
            cp.start()
        for a in range(n):
            for p in range(3):
                over_d2d(a, p, 1 - c).wait_recv()
        for cp in sends:
            cp.wait_send()

    return pl.pallas_call(
        body, name="gather_forward_" + tag,
        out_shape=[jax.ShapeDtypeStruct(b.shape, BF16) for b in bufs],
        in_specs=[_ANY] * n, out_specs=[_ANY] * n,
        input_output_aliases={a: a for a in range(n)},
        scratch_shapes=[pltpu.SemaphoreType.DMA((3 * n,)), pltpu.SemaphoreType.DMA((3 * n,))],
    )(*bufs)


_FLIPS = [(fx, fy, fc) for fx in (0, 1) for fy in (0, 1) for fc in (0, 1)][1:]


def _flipped(pos, flip):
    return tuple(1 - p if f else p for p, f in zip(pos, flip))


def _direct_gather_start(slots):
    def body(s_ref, send_sems, recv_sems, thru, token):
        me = _position()
        mine = s_ref.at[4 * me[0] + 2 * me[1] + me[2]]
        for r, flip in enumerate(_FLIPS):
            pltpu.make_async_remote_copy(
                src_ref=mine, dst_ref=mine, send_sem=send_sems.at[r], recv_sem=recv_sems.at[r],
                device_id=_flipped(me, flip), device_id_type=MESH).start()
        token[...] = jnp.zeros_like(token)

    return pl.pallas_call(
        body, name="small_gather_start",
        out_shape=(pltpu.SemaphoreType.DMA((len(_FLIPS),)), pltpu.SemaphoreType.DMA((len(_FLIPS),)),
                   pltpu.HBM(slots.shape, slots.dtype), jax.ShapeDtypeStruct((8, LANES), F32)),
        in_specs=[_HBM], out_specs=(_SEM, _SEM, _HBM, pl.BlockSpec(memory_space=pltpu.VMEM)),
        input_output_aliases={0: 2},
        compiler_params=pltpu.CompilerParams(has_side_effects=_EFFECT),
    )(pltpu.with_memory_space_constraint(slots, pltpu.HBM))


def _direct_gather_wait(send_sems, recv_sems, slots, after):
    def body(s_ref, send_sems, recv_sems, after_ref, out_ref):
        me = _position()
        mine = s_ref.at[4 * me[0] + 2 * me[1] + me[2]]
        for r, flip in enumerate(_FLIPS):
            peer = _flipped(me, flip)
            copy = pltpu.make_async_remote_copy(
                src_ref=mine, dst_ref=s_ref.at[4 * peer[0] + 2 * peer[1] + peer[2]],
                send_sem=send_sems.at[r], recv_sem=recv_sems.at[r], device_id=peer, device_id_type=MESH)
            copy.wait_send()
            copy.wait_recv()

    return pl.pallas_call(
        body, name="small_gather_wait",
        out_shape=pltpu.HBM(slots.shape, slots.dtype),
        in_specs=[_HBM, _SEM, _SEM, _ANY], out_specs=_HBM,
        input_output_aliases={0: 0},
        compiler_params=pltpu.CompilerParams(has_side_effects=_EFFECT),
    )(slots, send_sems, recv_sems, after)


def _sibling_exchange(grads, tag):
    n = len(grads)
    shapes = [g.shape for g in grads]

    def body(*refs):
        g_refs, x_refs = refs[:n], refs[n:2 * n]
        send_sems, recv_sems = refs[2 * n:]
        x, y, c = _position()
        copies = []
        for a in range(n):
            hr = shapes[a][1] // 2
            cp = pltpu.make_async_remote_copy(
                src_ref=g_refs[a].at[:, pl.ds((1 - c) * hr, hr), :], dst_ref=x_refs[a],
                send_sem=send_sems.at[a], recv_sem=recv_sems.at[a],
                device_id=(x, y, 1 - c), device_id_type=MESH)
            cp.start()
            copies.append(cp)
        for cp in copies:
            cp.wait()

    return pl.pallas_call(
        body, name="grad_sibling_exchange_" + tag,
        out_shape=[jax.ShapeDtypeStruct((g.shape[0], g.shape[1] // 2, g.shape[2]), g.dtype) for g in grads],
        in_specs=[_ANY] * n, out_specs=[_ANY] * n,
        scratch_shapes=[pltpu.SemaphoreType.DMA((n,)), pltpu.SemaphoreType.DMA((n,))],
    )(*grads)


def _chip_sums(core, grads, theirs, tag):
    n = len(grads)

    def body(core_ref, *refs):
        g_refs, t_refs, o_refs = refs[:n], refs[n:2 * n], refs[2 * n:]
        for g_ref, t_ref, o_ref in zip(g_refs, t_refs, o_refs):
            o_ref[...] = (g_ref[...] + t_ref[...].astype(F32)).astype(BF16)

    in_specs = [pl.BlockSpec((1, g.shape[1] // 2, g.shape[2]), lambda k, core_ref: (k, core_ref[0], 0)) for g in grads]
    in_specs += [pl.BlockSpec((1,) + t.shape[1:], lambda k, core_ref: (k, 0, 0)) for t in theirs]
    return pl.pallas_call(
        body, name="grad_chip_sums_" + tag,
        grid_spec=pltpu.PrefetchScalarGridSpec(
            num_scalar_prefetch=1, grid=(N_CHIPS,), in_specs=in_specs,
            out_specs=[pl.BlockSpec((1,) + t.shape[1:], lambda k, core_ref: (k, 0, 0)) for t in theirs]),
        out_shape=[jax.ShapeDtypeStruct(t.shape, BF16) for t in theirs],
        compiler_params=_params(("arbitrary",)),
    )(core, *grads, *theirs)


def _chip_exchange_start(sums, after, tag):
    n = len(sums)
    m = len(after)
    lands = [lax.empty((3,) + s.shape[1:], BF16) for s in sums]

    def body(*refs):
        s_refs, y_refs = refs[:n], refs[n:2 * n]
        send_sems, recv_sems = refs[2 * n + m], refs[2 * n + m + 1]
        token = refs[4 * n + m + 2]
        x, y, c = _position()
        chips = [(1 - x, y), (x, 1 - y), (1 - x, 1 - y)]
        for a in range(n):
            for p, (cx, cy) in enumerate(chips):
                pltpu.make_async_remote_copy(
                    src_ref=s_refs[a].at[2 * cx + cy], dst_ref=y_refs[a].at[p],
                    send_sem=send_sems.at[3 * a + p], recv_sem=recv_sems.at[3 * a + p],
                    device_id=(cx, cy, c), device_id_type=MESH).start()
        token[...] = jnp.zeros_like(token)

    both = list(sums) + lands
    out = pl.pallas_call(
        body, name="grad_chip_exchange_start_" + tag,
        out_shape=(pltpu.SemaphoreType.DMA((3 * n,)), pltpu.SemaphoreType.DMA((3 * n,)),
                   *[pltpu.HBM(b.shape, b.dtype) for b in both], jax.ShapeDtypeStruct((8, LANES), F32)),
        in_specs=[_HBM] * (2 * n) + [_ANY] * m,
        out_specs=(_SEM, _SEM, *[_HBM] * (2 * n), pl.BlockSpec(memory_space=pltpu.VMEM)),
        input_output_aliases={a: 2 + a for a in range(2 * n)},
        compiler_params=pltpu.CompilerParams(has_side_effects=_EFFECT),
    )(*[pltpu.with_memory_space_constraint(b, pltpu.HBM) for b in both], *after)
    return out[0], out[1], list(out[2:2 + n]), list(out[2 + n:2 + 2 * n]), out[2 + 2 * n]


def _chip_exchange_wait(send_sems, recv_sems, sums, lands, after, tag):
    n = len(sums)

    def body(*refs):
        s_refs, y_refs = refs[:n], refs[n:2 * n]
        send_sems, recv_sems = refs[2 * n], refs[2 * n + 1]
        x, y, c = _position()
        chips = [(1 - x, y), (x, 1 - y), (1 - x, 1 - y)]
        for a in range(n):
            for p, (cx, cy) in enumerate(chips):
                copy = pltpu.make_async_remote_copy(
                    src_ref=s_refs[a].at[2 * cx + cy], dst_ref=y_refs[a].at[p],
                    send_sem=send_sems.at[3 * a + p], recv_sem=recv_sems.at[3 * a + p],
                    device_id=(cx, cy, c), device_id_type=MESH)
                copy.wait_send()
                copy.wait_recv()

    both = list(sums) + list(lands)
    out = pl.pallas_call(
        body, name="grad_chip_exchange_wait_" + tag,
        out_shape=[pltpu.HBM(b.shape, b.dtype) for b in both],
        in_specs=[_HBM] * (2 * n) + [_SEM, _SEM, _ANY], out_specs=[_HBM] * (2 * n),
        input_output_aliases={a: a for a in range(2 * n)},
        compiler_params=pltpu.CompilerParams(has_side_effects=_EFFECT),
    )(*both, send_sems, recv_sems, after)
    return list(out[:n]), list(out[n:])


def _total_sums(place, sums, parts, after, tag):
    n = len(parts)
    m = len(after)
    steps = 2

    def body(place_ref, *refs):
        for s_ref, y_ref, o_ref in zip(refs[:n], refs[n:2 * n], refs[2 * n + m:]):
            o_ref[0] = ((s_ref[0].astype(F32) + y_ref[0].astype(F32)) + y_ref[1].astype(F32)) + y_ref[2].astype(F32)

    def step_rows(pt):
        return pt.shape[1] // steps

    in_specs = [pl.BlockSpec((1, step_rows(s), s.shape[2]), lambda r, place_ref: (place_ref[0], r, 0)) for s in sums]
    in_specs += [pl.BlockSpec((3, step_rows(pt), pt.shape[2]), lambda r, place_ref: (0, r, 0)) for pt in parts]
    in_specs += [_ANY] * m
    return pl.pallas_call(
        body, name="grad_total_sums_" + tag,
        grid_spec=pltpu.PrefetchScalarGridSpec(
            num_scalar_prefetch=1, grid=(steps,), in_specs=in_specs,
            out_specs=[pl.BlockSpec((1, step_rows(pt), pt.shape[2]), lambda r, place_ref: (place_ref[1], r, 0))
                       for pt in parts]),
        out_shape=[jax.ShapeDtypeStruct((2,) + pt.shape[1:], F32) for pt in parts],
        compiler_params=_params(("arbitrary",)),
    )(place, *sums, *parts, *after)


def _sibling_share(halves, tag):
    n = len(halves)

    def body(*refs):
        f_refs = refs[n:2 * n]
        send_sems, recv_sems = refs[2 * n:]
        x, y, c = _position()
        copies = []
        for a in range(n):
            cp = pltpu.make_async_remote_copy(
                src_ref=f_refs[a].at[c], dst_ref=f_refs[a].at[c], send_sem=send_sems.at[a], recv_sem=recv_sems.at[a],
                device_id=(x, y, 1 - c), device_id_type=MESH)
            cp.start()
            copies.append(cp)
        for a, cp in enumerate(copies):
            cp.wait_send()
            pltpu.make_async_remote_copy(
                src_ref=f_refs[a].at[1 - c], dst_ref=f_refs[a].at[1 - c], send_sem=send_sems.at[a],
                recv_sem=recv_sems.at[a], device_id=(x, y, c), device_id_type=MESH).wait_recv()

    return pl.pallas_call(
        body, name="grad_sibling_share_" + tag,
        out_shape=[jax.ShapeDtypeStruct(h.shape, F32) for h in halves],
        in_specs=[_ANY] * n, out_specs=[_ANY] * n,
        input_output_aliases={a: a for a in range(n)},
        scratch_shapes=[pltpu.SemaphoreType.DMA((n,)), pltpu.SemaphoreType.DMA((n,))],
    )(*halves)


def _group_sum(stacked, nrow, name):
    total, n = stacked.shape
    groups = total // nrow

    def body(g_ref, o_ref):
        acc = g_ref[0:nrow, :]
        for grp in range(1, groups):
            acc = acc + g_ref[grp * nrow:(grp + 1) * nrow, :]
        o_ref[...] = acc

    return pl.pallas_call(
        body, name=name,
        out_shape=jax.ShapeDtypeStruct((nrow, n), F32),
        compiler_params=pltpu.CompilerParams(vmem_limit_bytes=VMEM_LIMIT),
    )(stacked)


def _local_step(xt, tgt, mod, gains, w_pool, pool_scale, w_in, later_weights, on_ffn_grads, on_small_grads, seq):
    g_mpre, g_mpost, g_fpre, g_fpost = gains
    d = xt.shape[1]
    tm, tq = min(TOKEN_TILE, seq), min(ATTN_TILE, seq)

    h1, qn, k, v, u, kt, vt = _prenorm_proj(xt, mod, g_mpre, w_in, seq, tm)
    tk = min(ATTN_KEY_TILE, tq // 2)
    o, ltot = _attn_fwd(qn, k, vt, seq, tq, tk)
    w_out, w_g, w_u, w_d = later_weights(o)
    w_out2 = w_out.reshape(d, d)
    pooled, mixin, mix, x1, h2 =_mixer_post(u, o, xt, mod, g_mpost, g_fpre, w_pool, pool_scale, w_out2, seq, tm)
    a, b, fin, dy, df, loss_blk, accb4, accg4 = _ffn_fwd(h2, w_g, w_u, w_d, x1, tgt, mod, g_fpost, seq, tm)
    da, db, dx1, dmix, accb5, accg5 = _ffn_bwd(df, a, b, w_d, w_g, w_u, x1, dy, mix, mod, g_fpre, g_mpost, seq, tm)
    bt = min(GRAD_TOKEN_TILE, xt.shape[0])
    bt_one = min(2 * GRAD_TOKEN_TILE, xt.shape[0])
    (g_g,), (g_g16,) = _tn_matmul(da, [h2], w_g.shape[0], bt_one, "grad_w_gate")
    (g_u,), (g_u16,) = _tn_matmul(db, [h2], w_u.shape[0], bt_one, "grad_w_up")
    (g_d,), (g_d16,) = _tn_matmul(fin, [df], w_d.shape[0], bt_one, "grad_w_down")
    token = on_ffn_grads([g_g, g_u, g_d], [g_g16, g_u16, g_d16])
    do, dpd, dps, dwp = _mixer_bwd(dmix, w_out2, pooled, w_pool, pool_scale + token, seq, tm)
    dq, dk, dv = _attn_bwd(qn, k, kt, v, do, ltot, seq, tq, tk)
    gx, du, accb8, accg8 = _inproj_bwd(dq, dk, dv, dpd, xt, dx1, mod, g_mpre, w_in, seq, tm)

    dmod = jnp.stack([accb8[:, 0], accb8[:, 1], accb5[:, 2], accb5[:, 0], accb5[:, 1], accb4[:, 0]], axis=1)
    dgain = jnp.stack([accg8[0], accg5[1], accg5[0], accg4[0]], axis=0)
    behind = on_small_grads(loss_blk, dmod, dgain, dps[0:1], dwp)
    g_in, g_in16 = _tn_matmul_stacked(h1, [dq, dk, dv, du], bt, "grad_w_in", behind)
    g_out, g_out16 = [parts[0].reshape(w_out.shape)
                      for parts in _tn_matmul(mixin, [dmix], 1, bt_one, "grad_w_out", behind)]
    grads = [g_in, g_out, g_g, g_u, g_d]
    grads16 = [g_in16, g_out16, g_g16, g_u16, g_d16]
    return gx, grads, grads16


def kernel(x, c, w_cond, b_cond, g_mix_pre, g_mix_post, w_in, w_pool, pool_scale, w_out, g_ffn_pre, g_ffn_post, w_gate, w_up, w_down, loss_target, m_w_cond, m_b_cond, m_g_mix_pre, m_g_mix_post, m_w_in, m_w_pool, m_pool_scale, m_w_out, m_g_ffn_pre, m_g_ffn_post, m_w_gate, m_w_up, m_w_down, v_w_cond, v_b_cond, v_g_mix_pre, v_g_mix_post, v_w_in, v_w_pool, v_pool_scale, v_w_out, v_g_ffn_pre, v_g_ffn_post, v_w_gate, v_w_up, v_w_down):
    xi, yi, ci = _position()
    chip = 2 * xi + yi
    dev = 4 * xi + 2 * yi + ci
    nb, seq, d = x.shape
    t_all = nb * seq
    xt = x.reshape(t_all, d)
    tgt = loss_target.reshape(t_all, d)
    ncol = w_cond.shape[2]
    pw = pool_scale.shape[1]

    place = jnp.stack([chip, ci]).astype(jnp.int32)
    turned = lambda t: jnp.swapaxes(t[0], 0, 1)
    placed = _place_quarters(place, [w_in[0], w_out[0], turned(w_gate), turned(w_up), w_down[0]])
    in_sems = _gather_start(placed[:1], [], "in")

    c_pad = jnp.concatenate([c, jnp.zeros((8 - nb, d), F32)], axis=0) + in_sems[3][0:1, 0:1]
    c_all = _all_gather(c_pad, "gather_c").reshape(N_DEV, 8, d)[:, :nb].reshape(N_DEV * nb, d)
    b_q = lax.dynamic_slice(b_cond, (0, chip * ncol), (1, ncol))
    sc_all, mod_q = _cond_fwd(c_all, w_cond[0], b_q, 512)
    mod_parts = _all_gather(mod_q, "gather_mod").reshape(N_DEV, N_DEV * nb, ncol)
    mod_rows = lax.dynamic_slice(mod_parts, (0, dev * nb, 0), (N_DEV, nb, ncol))[0::2]
    mod = jnp.transpose(mod_rows, (1, 0, 2)).reshape(nb, N_MOD, d)
    mod = jnp.concatenate([mod, jnp.zeros((nb, MOD_ROWS - N_MOD, d), F32)], axis=1)

    (w_in_all,) = _gather_forward(_gather_wait(*in_sems[:3], mod, "in"), "in")
    send_sems, recv_sems, in_flight, token = _gather_start(placed[1:], [mod, w_in_all], "rest")
    mod = mod + token[0:1, 0:1]

    def later_weights(after):
        return _gather_forward(_gather_wait(send_sems, recv_sems, in_flight, after, "rest"), "rest")

    ffn_split = []

    def on_ffn_grads(ffn_grads, ffn_grads16):
        theirs = _sibling_exchange(ffn_grads16, "ffn")
        ffn_split.extend(_chip_exchange_start(_chip_sums(place[1:], ffn_grads, theirs, "ffn"), [], "ffn"))
        return ffn_split[4][0:1, 0:1]

    wp_rows = w_pool[0].size // d
    loss_row = 2 * N_MOD + 4 + 1
    pad_rows = 24 - (loss_row + 1)
    prow = 24 + wp_rows
    small_split = []

    def on_small_grads(loss_blk, dmod, dgain, dps, dwp):
        payload = jnp.concatenate([
            dmod.reshape(nb * N_MOD, d), dgain,
            jnp.concatenate([dps, jnp.zeros((1, d - pw), F32)], axis=1),
            jnp.concatenate([loss_blk[0:1], jnp.zeros((1, d - LANES), F32)], axis=1),
            jnp.zeros((pad_rows, d), F32),
            jnp.concatenate(jnp.split(dwp.reshape(-1, dwp.shape[-1]), d // dwp.shape[-1], axis=0), axis=1)], axis=0)
        slots = lax.dynamic_update_slice(lax.empty((N_DEV, prow, d), F32), payload[None], (dev, 0, 0))
        small_split.extend(_direct_gather_start(slots))
        return [small_split[3]]

    gains = (g_mix_pre, g_mix_post, g_ffn_pre, g_ffn_post)
    gx, grads, grads16 = _local_step(
        xt, tgt, mod, gains, w_pool[0], pool_scale, w_in_all, later_weights, on_ffn_grads, on_small_grads, seq)

    sums_ffn, parts_ffn = _chip_exchange_wait(*ffn_split[:4], gx, "ffn")
    gathered = _direct_gather_wait(*small_split[:3], grads16[1]).reshape(N_DEV * prow, d)
    summed = _group_sum(gathered, prow, "small_device_sum")
    loss = summed[loss_row, 0]
    dmod_all = gathered.reshape(N_DEV, prow, d)[:, :nb * N_MOD].reshape(N_DEV * nb, N_MOD * d)
    dmod_q = lax.dynamic_slice(dmod_all, (0, chip * ncol), (N_DEV * nb, ncol))
    g_w_cond = _cond_bwd(sc_all, dmod_q, 512)
    first_gain = 2 * N_MOD

    theirs = _sibling_exchange(grads16[:2], "mix")
    mix_split = _chip_exchange_start(_chip_sums(place[1:], grads[:2], theirs, "mix"), [gathered], "mix")
    unfold = lambda halves: [g.reshape(2 * g.shape[1], g.shape[2]) for g in halves]
    g_ffn = unfold(_sibling_share(_total_sums(place, sums_ffn, parts_ffn, [mix_split[4]], "ffn"), "ffn"))

    results = {}

    def update(name, w2, g2, m2, v2, shape):
        delta, new_m, new_v = _adamw(w2, g2, m2, v2, "adamw_" + name)
        back = (lambda t: jnp.swapaxes(t, 0, 1)[None]) if shape is None else (lambda t: t.reshape(shape))
        results[name] = [back(t) for t in (g2, delta, new_m, new_v)]
        return delta

    done = [update("w_gate", turned(w_gate), g_ffn[0], turned(m_w_gate), turned(v_w_gate), None),
            update("w_up", turned(w_up), g_ffn[1], turned(m_w_up), turned(v_w_up), None),
            update("w_down", w_down[0], g_ffn[2], m_w_down[0], v_w_down[0], w_down.shape),
            update("w_cond", w_cond[0], g_w_cond, m_w_cond[0], v_w_cond[0], w_cond.shape)]

    gain_row = lambda r: (lambda s: s[first_gain + r:first_gain + r + 1, :])
    small = [
        ("b_cond", (b_cond, m_b_cond, v_b_cond), (N_MOD, d), lambda s: s[0:N_MOD, :] + s[N_MOD:2 * N_MOD, :]),
        ("g_mix_pre", (g_mix_pre, m_g_mix_pre, v_g_mix_pre), (1, d), gain_row(0)),
        ("g_mix_post", (g_mix_post, m_g_mix_post, v_g_mix_post), (1, d), gain_row(1)),
        ("g_ffn_pre", (g_ffn_pre, m_g_ffn_pre, v_g_ffn_pre), (1, d), gain_row(2)),
        ("g_ffn_post", (g_ffn_post, m_g_ffn_post, v_g_ffn_post), (1, d), gain_row(3)),
        ("pool_scale", (pool_scale, m_pool_scale, v_pool_scale), (1, pw),
         lambda s: s[first_gain + 4:first_gain + 5, 0:pw]),
        ("w_pool", (w_pool, m_w_pool, v_w_pool), (wp_rows * d // w_pool.shape[-1], w_pool.shape[-1]),
         lambda s: jnp.concatenate([s[24:24 + wp_rows, j * w_pool.shape[-1]:(j + 1) * w_pool.shape[-1]]
                                    for j in range(d // w_pool.shape[-1])], axis=0)),
    ]
    updated = _small_updates(summed, [tuple(t.reshape(flat) for t in wmv) + (pick,) for _, wmv, flat, pick in small])
    for (name, wmv, _, _), quad in zip(small, updated):
        results[name] = [t.reshape(wmv[0].shape) for t in quad]

    sums_mix, parts_mix = _chip_exchange_wait(*mix_split[:4], done[-1], "mix")
    g_mix = unfold(_sibling_share(_total_sums(place, sums_mix, parts_mix, done[:3], "mix"), "mix"))
    update("w_in", w_in[0], g_mix[0], m_w_in[0], v_w_in[0], w_in.shape)
    update("w_out", w_out[0], g_mix[1], m_w_out[0], v_w_out[0], w_out.shape)

    names = ("w_cond", "b_cond", "g_mix_pre", "g_mix_post", "w_in", "w_pool", "pool_scale", "w_out",
             "g_ffn_pre", "g_ffn_post", "w_gate", "w_up", "w_down")
    outs = [results[name][part] for part in range(4) for name in names]
    return (loss, gx.reshape(x.shape), *outs)
```

```python
import jax
import jax.numpy as jnp
import numpy as np
from jax import lax
from jax.experimental import pallas as pl
from jax.experimental.pallas import tpu as pltpu

F32 = jnp.float32
BF16 = jnp.bfloat16
MESH = pl.DeviceIdType.MESH

EPS = 1e-6
HEAD_DIM = 64
HEADS_PER_BLOCK = 2
LANES = 128
NEG_QK_SCALE = -0.125
POOL_WINDOWS = (2, 4, 8, 16)
POOL_GROUP = 128
HALO = 16
N_MOD = 6
MOD_ROWS = 8
N_CHIPS = 4
N_DEV = 8
VMEM_LIMIT = 56 * 1024 * 1024

ADAM_LR = 0.001
ADAM_B1 = 0.9
ADAM_B2 = 0.999
ADAM_EPS = 1e-08
ADAM_WD = 0.01
ADAM_STEP = 10

TOKEN_TILE = 512
GRAD_TOKEN_TILE = 2048
FFN_ROW_CHUNKS = 2
ROW_CHUNKS = 2
ATTN_TILE = 512
ATTN_KEY_TILE = 256
ATTN_ROW_CHUNK = 32
LOG_SUM_PASSES = 1


def _dot(a, b):
    return jnp.dot(a, b, preferred_element_type=F32)


def _dot_nt(a, b):
    return lax.dot_general(a, b, (((1,), (1,)), ((), ())), preferred_element_type=F32)


def _dot_tn(a, b):
    return lax.dot_general(a, b, (((0,), (0,)), ((), ())), preferred_element_type=F32)


def _split(v):
    hi = v.astype(BF16)
    lo = (v - hi.astype(F32)).astype(BF16)
    return hi, lo


def _rms(v):
    return lax.rsqrt(jnp.mean(v * v, axis=-1, keepdims=True) + EPS)


def _norm_bwd(dn, n, r):
    return r * (dn - n * jnp.mean(dn * n, axis=-1, keepdims=True))


def _sigmoid(v):
    return 0.5 * jnp.tanh(0.5 * v) + 0.5


def _colsum(v):
    return jnp.sum(v, axis=0, keepdims=True)


def _params(sem=None):
    return pltpu.CompilerParams(dimension_semantics=sem, vmem_limit_bytes=VMEM_LIMIT)


def _position():
    return lax.axis_index("x"), lax.axis_index("y"), lax.axis_index("c")


def _prenorm_proj(x, mod, g_pre, w_in, seq, tm):
    t_all, d = x.shape
    nt = seq // tm
    p = w_in.shape[2]

    def body(x_ref, mod_ref, g_ref, w_ref, h_ref, q_ref, k_ref, v_ref, u_ref, kt_ref, vt_ref):
        for c in range(ROW_CHUNKS):
            rows = slice(c * (tm // ROW_CHUNKS), (c + 1) * (tm // ROW_CHUNKS))
            xf = x_ref[rows, :]
            n = xf * _rms(xf)
            h = (n * g_ref[...]) * (1.0 + mod_ref[0, 1:2, :]) + mod_ref[0, 0:1, :]
            hb = h.astype(BF16)
            h_ref[rows, :] = hb
            q_ref[rows, :] = (_dot(hb, w_ref[0]) * NEG_QK_SCALE).astype(BF16)
            kf = _dot(hb, w_ref[1])
            vf = _dot(hb, w_ref[2])
            k_ref[rows, :] = kf.astype(BF16)
            v_ref[rows, :] = vf.astype(BF16)
            kt_ref[:, rows] = kf.T.astype(BF16)
            vt_ref[:, rows] = vf.T.astype(BF16)
            u_ref[rows, :] = _dot(hb, w_ref[3])

    tok = lambda i: (i, 0)
    tok_t = lambda i: (0, i)
    return pl.pallas_call(
        body, name="prenorm_proj", grid=(t_all // tm,),
        in_specs=[pl.BlockSpec((tm, d), tok),
                  pl.BlockSpec((1, MOD_ROWS, d), lambda i: (i // nt, 0, 0)),
                  pl.BlockSpec((1, d), lambda i: (0, 0)),
                  pl.BlockSpec((N_CHIPS, d, p), lambda i: (0, 0, 0))],
        out_specs=[pl.BlockSpec((tm, d), tok)] + [pl.BlockSpec((tm, p), tok)] * 4 + [pl.BlockSpec((p, tm), tok_t)] * 2,
        out_shape=[jax.ShapeDtypeStruct((t_all, d), BF16)] + [jax.ShapeDtypeStruct((t_all, p), BF16)] * 3
        + [jax.ShapeDtypeStruct((t_all, p), F32)] + [jax.ShapeDtypeStruct((p, t_all), BF16)] * 2,
        compiler_params=_params(("arbitrary",)),
    )(x, mod, g_pre, w_in)


def _tri_matrix(tk, kind):
    j = np.arange(2 * tk)[:, None] % tk
    s = np.arange(tk)[None, :]
    return jnp.asarray({"after": j > s, "upto": j <= s, "before": j < s}[kind], dtype=BF16)


def _neg_abs(v):
    bits = lax.bitcast_convert_type(v, jnp.int32) | jnp.int32(-2 ** 31)
    return lax.bitcast_convert_type(bits, F32)


def _row_sums(v):
    return jnp.broadcast_to(jnp.sum(v, axis=-1, keepdims=True), (v.shape[0], LANES))


def _across(v, n):
    return jnp.concatenate([v] * (n // LANES), axis=1)


def _all_masked(c, diag, rc, tk):
    return diag is not None and diag * tk >= (c + 1) * rc - 1


def _some_masked(c, diag, rc, tk):
    return diag is not None and diag * tk + tk - 1 >= c * rc


def _attn_fwd(qn, k, vt, seq, tq, tk):
    t_all, w = qn.shape
    nb, nq, ndiag = t_all // seq, seq // tq, tq // tk
    assert ndiag % 2 == 0, "two key blocks per loop trip"
    rc = ATTN_ROW_CHUNK
    heads = range(HEADS_PER_BLOCK)

    def body(q_ref, k_ref, vt_ref, tri_ref, o_ref, l_ref,
             z_buf, ls_buf, hl_buf, aft_buf, w_buf, tot_buf, acc_t, run_buf):
        i = pl.program_id(2)
        nblk = (i + 1) * ndiag
        lane = lax.broadcasted_iota(jnp.int32, (1, LANES), 1)
        row = lax.broadcasted_iota(jnp.int32, (rc, tk), 0)
        col = lax.broadcasted_iota(jnp.int32, (rc, tk), 1)
        first = lane < HEAD_DIM
        q2 = q_ref[...]
        qs = [jnp.where(first, q2, jnp.zeros_like(q2)), jnp.where(first, jnp.zeros_like(q2), q2)]
        acc_t[...] = jnp.zeros_like(acc_t)
        run_buf[...] = jnp.zeros_like(run_buf)
        w_buf[1] = jnp.zeros((HEADS_PER_BLOCK, tq, tk), BF16)

        def causal(c, diag):
            return (col + diag * tk) < (row + c * rc)

        def scores(blk, slot):
            kj = k_ref[pl.ds(pl.multiple_of(blk * tk, tk), tk), :]
            for h in heads:
                z_buf[slot, h] = _dot_nt(qs[h], kj)

        def values(blk, slot):
            keys = pl.ds(pl.multiple_of(blk * tk, tk), tk)
            for h in heads:
                dims = slice(h * HEAD_DIM, (h + 1) * HEAD_DIM)
                acc_t[dims, :] += _dot_nt(vt_ref[dims, keys], w_buf[slot, h])

        def softplus_stage(h, slot, diag):
            for c in range(tq // rc):
                rows = slice(c * rc, (c + 1) * rc)
                if _all_masked(c, diag, rc, tk):
                    hl_buf[h, rows, :] = jnp.zeros((rc, LOG_SUM_PASSES * tk), BF16)
                    tot_buf[h, rows, :] = jnp.zeros((rc, LANES), F32)
                    continue
                nz = z_buf[slot, h, rows, :]
                l1 = jnp.minimum(nz, 0.0) - jnp.log(1.0 + jnp.exp(_neg_abs(nz)))
                if _some_masked(c, diag, rc, tk):
                    l1 = jnp.where(causal(c, diag), l1, 0.0)
                for s, part in enumerate(_split(l1)[:LOG_SUM_PASSES]):
                    hl_buf[h, rows, s * tk:(s + 1) * tk] = part
                ls_buf[h, rows, :] = l1 - nz
                tot_buf[h, rows, :] = _row_sums(l1)

        def weights_stage(h, slot, diag):
            for c in range(tq // rc):
                rows = slice(c * rc, (c + 1) * rc)
                if _all_masked(c, diag, rc, tk):
                    w_buf[slot, h, rows, :] = jnp.zeros((rc, tk), BF16)
                    continue
                wgt = jnp.exp((ls_buf[h, rows, :] + aft_buf[h, rows, :]) + _across(run_buf[h, rows, :], tk))
                if _some_masked(c, diag, rc, tk):
                    wgt = jnp.where(causal(c, diag), wgt, 0.0)
                w_buf[slot, h, rows, :] = wgt.astype(BF16)
                run_buf[h, rows, :] += tot_buf[h, rows, :]

        def position(blk, slot, diag):
            scores(jnp.maximum(blk - 1, 0), 1 - slot)
            for h in heads:
                softplus_stage(h, slot, diag)
                aft_buf[h] = _dot(hl_buf[h], tri_ref[...])
            values(jnp.minimum(blk + 1, nblk - 1), 1 - slot)
            for h in heads:
                weights_stage(h, slot, diag)

        scores(nblk - 1, 0)
        for p in range(ndiag):
            position(nblk - 1 - p, p % 2, ndiag - 1 - p)

        def trip(jj, carry):
            for u in range(2):
                position(i * ndiag - 1 - 2 * jj - u, u, None)
            return carry

        lax.fori_loop(0, (i * ndiag) // 2, trip, 0)
        values(0, 1)
        o_ref[...] = acc_t[...].T.astype(BF16)
        l_ref[...] = jnp.where(first, run_buf[0], run_buf[1])

    qmap = lambda b, hp, i: (b * nq + i, hp)
    nh = HEADS_PER_BLOCK
    return pl.pallas_call(
        body, name="attn_fwd", grid=(nb, w // LANES, nq),
        in_specs=[pl.BlockSpec((tq, LANES), qmap), pl.BlockSpec((seq, LANES), lambda b, hp, i: (b, hp)),
                  pl.BlockSpec((LANES, seq), lambda b, hp, i: (hp, b)),
                  pl.BlockSpec((LOG_SUM_PASSES * tk, tk), lambda b, hp, i: (0, 0))],
        out_specs=[pl.BlockSpec((tq, LANES), qmap), pl.BlockSpec((tq, LANES), qmap)],
        out_shape=[jax.ShapeDtypeStruct((t_all, w), BF16), jax.ShapeDtypeStruct((t_all, w), F32)],
        scratch_shapes=[pltpu.VMEM((2, nh, tq, tk), F32), pltpu.VMEM((nh, tq, tk), F32),
                        pltpu.VMEM((nh, tq, LOG_SUM_PASSES * tk), BF16), pltpu.VMEM((nh, tq, tk), F32),
                        pltpu.VMEM((2, nh, tq, tk), BF16), pltpu.VMEM((nh, tq, LANES), F32),
                        pltpu.VMEM((LANES, tq), F32), pltpu.VMEM((nh, tq, LANES), F32)],
        compiler_params=_params(("arbitrary", "arbitrary", "arbitrary")),
    )(qn, k, vt, _tri_matrix(tk, "after")[:LOG_SUM_PASSES * tk])


def _window_sums(ext, rows, offset, forward):
    r = lax.broadcasted_iota(jnp.int32, (rows, rows + HALO), 0)
    e = lax.broadcasted_iota(jnp.int32, (rows, rows + HALO), 1)
    hi, lo = _split(ext)
    out = []
    for g, win in enumerate(POOL_WINDOWS):
        if forward:
            band = (e >= r) & (e < r + win)
        else:
            band = (e <= r + offset) & (e > r + offset - win)
        bm = band.astype(BF16)
        cols = slice(g * POOL_GROUP, (g + 1) * POOL_GROUP)
        out.append(_dot(bm, hi[:, cols]) + _dot(bm, lo[:, cols]))
    return out


def _window_counts(pos):
    return [jnp.minimum(pos + 1, win).astype(F32) for win in POOL_WINDOWS]


def _mixer_post(u, o, x, mod, g_post, g_fpre, w_pool, pool_scale, w_out, seq, tm):
    t_all, d = x.shape
    nt = seq // tm
    p = u.shape[1]

    def body(u_ref, halo_ref, o_ref, x_ref, mod_ref, gp_ref, gf_ref, wp_ref, ps_ref, wo_ref,
             pooled_ref, mixin_ref, mix_ref, x1_ref, h2_ref):
        it = pl.program_id(0) % nt
        uf = u_ref[...]
        halo = jnp.where(it == 0, 0.0, halo_ref[...])
        ext = jnp.concatenate([halo, uf], axis=0)
        pos = it * tm + lax.broadcasted_iota(jnp.int32, (tm, 1), 0)
        sums = _window_sums(ext, tm, HALO, False)
        cnts = _window_counts(pos)
        pools = []
        for g in range(len(POOL_WINDOWS)):
            cols = slice(g * POOL_GROUP, (g + 1) * POOL_GROUP)
            pooled = (sums[g] / cnts[g] - uf[:, cols]).astype(BF16)
            pooled_ref[:, cols] = pooled
            yg = _dot(pooled, wp_ref[g].astype(BF16))
            pools.append((yg * ps_ref[:, cols]).astype(BF16))
        mixin_ref[...] = jnp.concatenate([o_ref[...]] + pools, axis=1)
        for c in range(ROW_CHUNKS):
            rows = slice(c * (tm // ROW_CHUNKS), (c + 1) * (tm // ROW_CHUNKS))
            mix = _dot(mixin_ref[rows, :], wo_ref[...])
            mix_ref[rows, :] = mix
            n2 = mix * _rms(mix)
            x1 = x_ref[rows, :] + mod_ref[0, 2:3, :] * (n2 * gp_ref[...])
            x1_ref[rows, :] = x1
            n3 = x1 * _rms(x1)
            h2 = (n3 * gf_ref[...]) * (1.0 + mod_ref[0, 4:5, :]) + mod_ref[0, 3:4, :]
            h2_ref[rows, :] = h2.astype(BF16)

    tok = lambda i: (i, 0)
    const2 = lambda i: (0, 0)
    hb = tm // HALO
    return pl.pallas_call(
        body, name="mixer_post", grid=(t_all // tm,),
        in_specs=[pl.BlockSpec((tm, p), tok),
                  pl.BlockSpec((HALO, p), lambda i: (jnp.maximum(i * hb - 1, 0), 0)),
                  pl.BlockSpec((tm, p), tok),
                  pl.BlockSpec((tm, d), tok),
                  pl.BlockSpec((1, MOD_ROWS, d), lambda i: (i // nt, 0, 0)),
                  pl.BlockSpec((1, d), const2), pl.BlockSpec((1, d), const2),
                  pl.BlockSpec(w_pool.shape, lambda i: (0, 0, 0)),
                  pl.BlockSpec((1, p), const2),
                  pl.BlockSpec((d, d), const2)],
        out_specs=[pl.BlockSpec((tm, p), tok), pl.BlockSpec((tm, d), tok), pl.BlockSpec((tm, d), tok),
                   pl.BlockSpec((tm, d), tok), pl.BlockSpec((tm, d), tok)],
        out_shape=[jax.ShapeDtypeStruct((t_all, p), BF16), jax.ShapeDtypeStruct((t_all, d), BF16),
                   jax.ShapeDtypeStruct((t_all, d), F32), jax.ShapeDtypeStruct((t_all, d), F32),
                   jax.ShapeDtypeStruct((t_all, d), BF16)],
        compiler_params=_params(("arbitrary",)),
    )(u, u, o, x, mod, g_post, g_fpre, w_pool, pool_scale, w_out)


def _ffn_fwd(h2, w_g, w_u, w_d, x1, tgt, mod, g_post, seq, tm):
    t_all, d = x1.shape
    nt = seq // tm
    nk, ff, _ = w_g.shape

    def body(h_ref, wg_ref, wu_ref, wd_ref, x1_ref, t_ref, mod_ref, g_ref,
             a_ref, b_ref, fin_ref, dy_ref, df_ref, loss_ref, accb_ref, accg_ref, facc):
        i, k = pl.program_id(0), pl.program_id(1)

        @pl.when(k == 0)
        def _():
            facc[...] = jnp.zeros_like(facc)

        for c in range(FFN_ROW_CHUNKS):
            rows = slice(c * (tm // FFN_ROW_CHUNKS), (c + 1) * (tm // FFN_ROW_CHUNKS))
            hb = h_ref[rows, :]
            a = _dot_nt(hb, wg_ref[0])
            b = _dot_nt(hb, wu_ref[0])
            a_ref[0, rows, :] = a.astype(BF16)
            b_ref[0, rows, :] = b.astype(BF16)
            fin = ((a * _sigmoid(a)) * b).astype(BF16)
            fin_ref[0, rows, :] = fin
            facc[rows, :] += _dot(fin, wd_ref[0])

        @pl.when(k == nk - 1)
        def _():
            f = facc[...]
            r4 = _rms(f)
            n4 = f * r4
            gate = mod_ref[0, 5:6, :]
            g = g_ref[...]
            err = (x1_ref[...] + gate * (n4 * g)) - t_ref[...]
            dy = err * (1.0 / d)
            dy_ref[...] = dy

            @pl.when(i == 0)
            def _():
                loss_ref[...] = jnp.zeros_like(loss_ref)
                accg_ref[...] = jnp.zeros_like(accg_ref)

            @pl.when(i % nt == 0)
            def _():
                accb_ref[...] = jnp.zeros_like(accb_ref)

            loss_ref[...] += (0.5 / d) * jnp.sum(err * err)
            accb_ref[0, 0:1, :] += _colsum(dy * (n4 * g))
            accg_ref[0:1, :] += _colsum((dy * gate) * n4)
            dn4 = (dy * gate) * g
            df_ref[...] = _norm_bwd(dn4, n4, r4).astype(BF16)

    tok = lambda i, k: (i, 0)
    ktok = lambda i, k: (k, i, 0)
    kw = lambda i, k: (k, 0, 0)
    const2 = lambda i, k: (0, 0)
    return pl.pallas_call(
        body, name="ffn_fwd", grid=(t_all // tm, nk),
        in_specs=[pl.BlockSpec((tm, d), tok),
                  pl.BlockSpec((1, ff, d), kw), pl.BlockSpec((1, ff, d), kw), pl.BlockSpec((1, ff, d), kw),
                  pl.BlockSpec((tm, d), tok), pl.BlockSpec((tm, d), tok),
                  pl.BlockSpec((1, MOD_ROWS, d), lambda i, k: (i // nt, 0, 0)),
                  pl.BlockSpec((1, d), const2)],
        out_specs=[pl.BlockSpec((1, tm, ff), ktok)] * 3
        + [pl.BlockSpec((tm, d), tok), pl.BlockSpec((tm, d), tok),
           pl.BlockSpec((8, LANES), const2),
           pl.BlockSpec((1, 8, d), lambda i, k: (i // nt, 0, 0)),
           pl.BlockSpec((8, d), const2)],
        out_shape=[jax.ShapeDtypeStruct((nk, t_all, ff), BF16)] * 3
        + [jax.ShapeDtypeStruct((t_all, d), F32), jax.ShapeDtypeStruct((t_all, d), BF16),
           jax.ShapeDtypeStruct((8, LANES), F32),
           jax.ShapeDtypeStruct((t_all // seq, 8, d), F32),
           jax.ShapeDtypeStruct((8, d), F32)],
        scratch_shapes=[pltpu.VMEM((tm, d), F32)],
        compiler_params=_params(("arbitrary", "arbitrary")),
    )(h2, w_g, w_u, w_d, x1, tgt, mod, g_post)


def _ffn_bwd(df, a, b, w_d, w_g, w_u, x1, dy, mix, mod, g_fpre, g_mpost, seq, tm):
    t_all, d = x1.shape
    nt = seq // tm
    nk, ff, _ = w_g.shape

    def body(df_ref, a_ref, b_ref, wd_ref, wg_ref, wu_ref, x1_ref, dy_ref, mix_ref, mod_ref, gf_ref, gm_ref,
             da_ref, db_ref, dx1_ref, dmix_ref, accb_ref, accg_ref, hacc):
        i, k = pl.program_id(0), pl.program_id(1)

        @pl.when(k == 0)
        def _():
            hacc[...] = jnp.zeros_like(hacc)

        for c in range(FFN_ROW_CHUNKS):
            rows = slice(c * (tm // FFN_ROW_CHUNKS), (c + 1) * (tm // FFN_ROW_CHUNKS))
            dfin = _dot_nt(df_ref[rows, :], wd_ref[0])
            af = a_ref[0, rows, :].astype(F32)
            bf = b_ref[0, rows, :].astype(F32)
            sig = _sigmoid(af)
            da = ((dfin * bf) * (sig * (1.0 + af * (1.0 - sig)))).astype(BF16)
            db = (dfin * (af * sig)).astype(BF16)
            da_ref[0, rows, :] = da
            db_ref[0, rows, :] = db
            hacc[rows, :] += _dot(da, wg_ref[0]) + _dot(db, wu_ref[0])

        @pl.when(k == nk - 1)
        def _():
            @pl.when(i == 0)
            def _():
                accg_ref[...] = jnp.zeros_like(accg_ref)

            @pl.when(i % nt == 0)
            def _():
                accb_ref[...] = jnp.zeros_like(accb_ref)

            dh2 = hacc[...]
            x1 = x1_ref[...]
            r3 = _rms(x1)
            n3 = x1 * r3
            g3 = gf_ref[...]
            scale1 = 1.0 + mod_ref[0, 4:5, :]
            accb_ref[0, 0:1, :] += _colsum(dh2)
            accb_ref[0, 1:2, :] += _colsum(dh2 * (n3 * g3))
            accg_ref[0:1, :] += _colsum((dh2 * scale1) * n3)
            dx1 = dy_ref[...] + _norm_bwd((dh2 * scale1) * g3, n3, r3)
            dx1_ref[...] = dx1
            mix = mix_ref[...]
            r2 = _rms(mix)
            n2 = mix * r2
            g2 = gm_ref[...]
            gate = mod_ref[0, 2:3, :]
            accb_ref[0, 2:3, :] += _colsum(dx1 * (n2 * g2))
            accg_ref[1:2, :] += _colsum((dx1 * gate) * n2)
            dmix_ref[...] = _norm_bwd((dx1 * gate) * g2, n2, r2).astype(BF16)

    tok = lambda i, k: (i, 0)
    ktok = lambda i, k: (k, i, 0)
    kw = lambda i, k: (k, 0, 0)
    const2 = lambda i, k: (0, 0)
    return pl.pallas_call(
        body, name="ffn_bwd", grid=(t_all // tm, nk),
        in_specs=[pl.BlockSpec((tm, d), tok),
                  pl.BlockSpec((1, tm, ff), ktok), pl.BlockSpec((1, tm, ff), ktok),
                  pl.BlockSpec((1, ff, d), kw), pl.BlockSpec((1, ff, d), kw), pl.BlockSpec((1, ff, d), kw),
                  pl.BlockSpec((tm, d), tok), pl.BlockSpec((tm, d), tok), pl.BlockSpec((tm, d), tok),
                  pl.BlockSpec((1, MOD_ROWS, d), lambda i, k: (i // nt, 0, 0)),
                  pl.BlockSpec((1, d), const2), pl.BlockSpec((1, d), const2)],
        out_specs=[pl.BlockSpec((1, tm, ff), ktok)] * 2
        + [pl.BlockSpec((tm, d), tok), pl.BlockSpec((tm, d), tok),
           pl.BlockSpec((1, 8, d), lambda i, k: (i // nt, 0, 0)),
           pl.BlockSpec((8, d), const2)],
        out_shape=[jax.ShapeDtypeStruct((nk, t_all, ff), BF16)] * 2
        + [jax.ShapeDtypeStruct((t_all, d), F32), jax.ShapeDtypeStruct((t_all, d), BF16),
           jax.ShapeDtypeStruct((t_all // seq, 8, d), F32),
           jax.ShapeDtypeStruct((8, d), F32)],
        scratch_shapes=[pltpu.VMEM((tm, d), F32)],
        compiler_params=_params(("arbitrary", "arbitrary")),
    )(df, a, b, w_d, w_g, w_u, x1, dy, mix, mod, g_fpre, g_mpost)


def _mixer_bwd(dmix, w_out, pooled, w_pool, pool_scale, seq, tm):
    t_all, d = dmix.shape
    p = pooled.shape[1]
    ng = len(POOL_WINDOWS)

    def body(dm_ref, wo_ref, pooled_ref, wp_ref, ps_ref, do_ref, dpd_ref, dps_ref, dwp_ref):
        i = pl.program_id(0)

        @pl.when(i == 0)
        def _():
            dps_ref[...] = jnp.zeros_like(dps_ref)
            dwp_ref[...] = jnp.zeros_like(dwp_ref)

        dmixin = _dot_nt(dm_ref[...], wo_ref[...])
        do_ref[...] = dmixin[:, :p].astype(BF16)
        for g in range(ng):
            cols = slice(g * POOL_GROUP, (g + 1) * POOL_GROUP)
            dpool = dmixin[:, p + g * POOL_GROUP:p + (g + 1) * POOL_GROUP]
            pooled = pooled_ref[:, cols]
            wpg = wp_ref[g].astype(BF16)
            yg = _dot(pooled, wpg)
            dps_ref[0:1, cols] += _colsum(dpool * yg)
            dyg = (dpool * ps_ref[:, cols]).astype(BF16)
            dwp_ref[g] += _dot_tn(pooled, dyg)
            dpd_ref[:, cols] = _dot_nt(dyg, wpg)

    tok = lambda i: (i, 0)
    const2 = lambda i: (0, 0)
    const3 = lambda i: (0, 0, 0)
    return pl.pallas_call(
        body, name="mixer_bwd", grid=(t_all // tm,),
        in_specs=[pl.BlockSpec((tm, d), tok), pl.BlockSpec((d, d), const2), pl.BlockSpec((tm, p), tok),
                  pl.BlockSpec(w_pool.shape, const3), pl.BlockSpec((1, p), const2)],
        out_specs=[pl.BlockSpec((tm, p), tok), pl.BlockSpec((tm, p), tok),
                   pl.BlockSpec((8, p), const2), pl.BlockSpec(w_pool.shape, const3)],
        out_shape=[jax.ShapeDtypeStruct((t_all, p), BF16), jax.ShapeDtypeStruct((t_all, p), F32),
                   jax.ShapeDtypeStruct((8, p), F32), jax.ShapeDtypeStruct(w_pool.shape, F32)],
        compiler_params=_params(("arbitrary",)),
    )(dmix, w_out, pooled, w_pool, pool_scale)


def _attn_bwd(qn, k, kt, v, do, ltot, seq, tq, tk, order):
    t_all, w = qn.shape
    nb, nq, ndiag, nkb = t_all // seq, seq // tq, tq // tk, seq // tk
    assert ndiag % 2 == 0, "two key blocks per loop trip"
    rc = ATTN_ROW_CHUNK
    nh = HEADS_PER_BLOCK
    heads = range(nh)

    def body(q_ref, k_ref, kt_ref, v_ref, do_ref, l_ref, up_ref, bf_ref, dq_ref, dk_ref, dv_ref,
             z_buf, dw_buf, ls_buf, hl_buf, upto_buf, g_buf, gb_buf, before_buf, w_buf, dz_buf,
             totl_buf, totg_buf, rem_buf, preg_buf, qnt_buf, dot_buf, dq_t, dk_t, dv_t):
        i = pl.program_id(2)
        nblk = (i + 1) * ndiag

        @pl.when(i == 0)
        def _():
            dk_t[...] = jnp.zeros_like(dk_t)
            dv_t[...] = jnp.zeros_like(dv_t)

        lane = lax.broadcasted_iota(jnp.int32, (1, LANES), 1)
        row = lax.broadcasted_iota(jnp.int32, (rc, tk), 0)
        col = lax.broadcasted_iota(jnp.int32, (rc, tk), 1)
        first = lane < HEAD_DIM
        q2 = q_ref[...]
        do2 = do_ref[...]
        l2 = l_ref[...]
        qs = [jnp.where(first, q2, jnp.zeros_like(q2)), jnp.where(first, jnp.zeros_like(q2), q2)]
        dos = [jnp.where(first, do2, jnp.zeros_like(do2)), jnp.where(first, jnp.zeros_like(do2), do2)]
        qnt_buf[...] = q2.astype(F32).T.astype(BF16)
        dot_buf[...] = do2.astype(F32).T.astype(BF16)
        for h in heads:
            rem_buf[h] = jnp.where(first if h == 0 else ~first, l2, pltpu.roll(l2, HEAD_DIM, 1))
        preg_buf[...] = jnp.zeros_like(preg_buf)
        dq_t[...] = jnp.zeros_like(dq_t)
        w_buf[1] = jnp.zeros((nh * tq, tk), BF16)
        dz_buf[1] = jnp.zeros((nh * tq, tk), BF16)

        def causal(c, diag):
            return (col + diag * tk) < (row + c * rc)

        def scores(blk, slot):
            off = pl.multiple_of(blk * tk, tk)
            kj = k_ref[pl.ds(off, tk), :]
            vj = v_ref[pl.ds(off, tk), :]
            for h in heads:
                z_buf[slot, h] = _dot_nt(qs[h], kj)
                dw_buf[slot, h] = _dot_nt(dos[h], vj)

        def gradients(blk, slot):
            keys = pl.ds(pl.multiple_of(blk * tk, tk), tk)
            for h in heads:
                dims = slice(h * HEAD_DIM, (h + 1) * HEAD_DIM)
                queries = slice(h * tq, (h + 1) * tq)
                dq_t[dims, :] += _dot_nt(kt_ref[dims, keys], dz_buf[slot, queries, :])
                dk_t[blk, dims, :] += _dot(qnt_buf[dims, :], dz_buf[slot, queries, :])
                dv_t[blk, dims, :] += _dot(dot_buf[dims, :], w_buf[slot, queries, :])

        def softplus_stage(h, slot, diag):
            for c in range(tq // rc):
                rows = slice(c * rc, (c + 1) * rc)
                if _all_masked(c, diag, rc, tk):
                    hl_buf[h, rows, :] = jnp.zeros((rc, LOG_SUM_PASSES * tk), BF16)
                    continue
                nz = z_buf[slot, h, rows, :]
                l1 = jnp.minimum(nz, 0.0) - jnp.log(1.0 + jnp.exp(_neg_abs(nz)))
                if _some_masked(c, diag, rc, tk):
                    l1 = jnp.where(causal(c, diag), l1, 0.0)
                for s, part in enumerate(_split(l1)[:LOG_SUM_PASSES]):
                    hl_buf[h, rows, s * tk:(s + 1) * tk] = part
                ls_buf[h, rows, :] = l1 - nz
                totl_buf[h, rows, :] = _row_sums(l1)

        def weights_stage(h, slot, diag):
            for c in range(tq // rc):
                rows = slice(c * rc, (c + 1) * rc)
                stacked = slice(h * tq + c * rc, h * tq + (c + 1) * rc)
                if _all_masked(c, diag, rc, tk):
                    w_buf[slot, stacked, :] = jnp.zeros((rc, tk), BF16)
                    gb_buf[h, rows, :] = jnp.zeros((rc, tk), BF16)
                    continue
                wgt = jnp.exp(ls_buf[h, rows, :] + (_across(rem_buf[h, rows, :], tk) - upto_buf[h, rows, :]))
                if _some_masked(c, diag, rc, tk):
                    wgt = jnp.where(causal(c, diag), wgt, 0.0)
                w_buf[slot, stacked, :] = wgt.astype(BF16)
                g = wgt * dw_buf[slot, h, rows, :]
                g_buf[h, rows, :] = g
                gb_buf[h, rows, :] = g.astype(BF16)
                totg_buf[h, rows, :] = _row_sums(g)
                rem_buf[h, rows, :] -= totl_buf[h, rows, :]

        def dscore_stage(h, slot, diag):
            for c in range(tq // rc):
                rows = slice(c * rc, (c + 1) * rc)
                stacked = slice(h * tq + c * rc, h * tq + (c + 1) * rc)
                if _all_masked(c, diag, rc, tk):
                    dz_buf[slot, stacked, :] = jnp.zeros((rc, tk), BF16)
                    continue
                sig = jnp.exp(ls_buf[h, rows, :])
                g = g_buf[h, rows, :]
                dnz = sig * ((before_buf[h, rows, :] + _across(preg_buf[h, rows, :], tk)) + g) - g
                if _some_masked(c, diag, rc, tk):
                    dnz = jnp.where(causal(c, diag), dnz, 0.0)
                dz_buf[slot, stacked, :] = dnz.astype(BF16)
                preg_buf[h, rows, :] += totg_buf[h, rows, :]

        def position(blk, slot, diag, prefetch):
            if prefetch:
                scores(blk + 1, 1 - slot)
            for h in heads:
                softplus_stage(h, slot, diag)
                upto_buf[h] = _dot(hl_buf[h], up_ref[...])
            gradients(jnp.maximum(blk - 1, 0), 1 - slot)
            for h in heads:
                weights_stage(h, slot, diag)
                before_buf[h] = _dot(gb_buf[h], bf_ref[...])
            for h in heads:
                dscore_stage(h, slot, diag)

        scores(0, 0)

        def trip(jj, carry):
            for u in range(2):
                position(2 * jj + u, u, None, True)
            return carry

        lax.fori_loop(0, (i * ndiag) // 2, trip, 0)
        for d in range(ndiag):
            position(i * ndiag + d, d % 2, d, d < ndiag - 1)
        gradients(nblk - 1, 1)
        dq_ref[...] = (dq_t[...].T * NEG_QK_SCALE).astype(BF16)

        @pl.when(i == nq - 1)
        def _():
            for blk in range(nkb):
                dk_ref[blk * tk:(blk + 1) * tk, :] = dk_t[blk].T.astype(BF16)
                dv_ref[blk * tk:(blk + 1) * tk, :] = dv_t[blk].T.astype(BF16)

    qmap = lambda b, hp, i: (b * nq + i, hp)
    kmap = lambda b, hp, i: (b, hp)
    const = lambda b, hp, i: (0, 0)
    return pl.pallas_call(
        body, name="attn_bwd", grid=(nb, w // LANES, nq),
        in_specs=[pl.BlockSpec((tq, LANES), qmap), pl.BlockSpec((seq, LANES), kmap),
                  pl.BlockSpec((LANES, seq), lambda b, hp, i: (hp, b)), pl.BlockSpec((seq, LANES), kmap),
                  pl.BlockSpec((tq, LANES), qmap), pl.BlockSpec((tq, LANES), qmap),
                  pl.BlockSpec((LOG_SUM_PASSES * tk, tk), const), pl.BlockSpec((tk, tk), const)],
        out_specs=[pl.BlockSpec((tq, LANES), qmap), pl.BlockSpec((seq, LANES), kmap), pl.BlockSpec((seq, LANES), kmap)],
        out_shape=[jax.ShapeDtypeStruct((t_all, w), BF16)] * 3,
        scratch_shapes=[pltpu.VMEM((2, nh, tq, tk), F32), pltpu.VMEM((2, nh, tq, tk), F32),
                        pltpu.VMEM((nh, tq, tk), F32), pltpu.VMEM((nh, tq, LOG_SUM_PASSES * tk), BF16),
                        pltpu.VMEM((nh, tq, tk), F32), pltpu.VMEM((nh, tq, tk), F32),
                        pltpu.VMEM((nh, tq, tk), BF16), pltpu.VMEM((nh, tq, tk), F32),
                        pltpu.VMEM((2, nh * tq, tk), BF16), pltpu.VMEM((2, nh * tq, tk), BF16),
                        pltpu.VMEM((nh, tq, LANES), F32), pltpu.VMEM((nh, tq, LANES), F32),
                        pltpu.VMEM((nh, tq, LANES), F32), pltpu.VMEM((nh, tq, LANES), F32),
                        pltpu.VMEM((LANES, tq), BF16), pltpu.VMEM((LANES, tq), BF16),
                        pltpu.VMEM((LANES, tq), F32), pltpu.VMEM((nkb, LANES, tk), F32),
                        pltpu.VMEM((nkb, LANES, tk), F32)],
        compiler_params=_params(("arbitrary", "arbitrary", "arbitrary")),
    )(qn, k, kt, v, do, ltot, _tri_matrix(tk, "upto")[:LOG_SUM_PASSES * tk] + order.astype(BF16),
      _tri_matrix(tk, "before")[:tk])


def _inproj_bwd(dq, dk, dv, dpd, x, dx1, mod, g_pre, w_in, seq, tm):
    t_all, d = x.shape
    nt = seq // tm
    p = dq.shape[1]

    def body(dq_ref, dk_ref, dv_ref, dpd_ref, halo_ref, x_ref, dx1_ref, mod_ref, g_ref, w_ref,
             gx_ref, du_ref, accb_ref, accg_ref):
        i = pl.program_id(0)
        it = i % nt

        @pl.when(i == 0)
        def _():
            accg_ref[...] = jnp.zeros_like(accg_ref)

        @pl.when(it == 0)
        def _():
            accb_ref[...] = jnp.zeros_like(accb_ref)

        dpd = dpd_ref[...]
        pos = it * tm + lax.broadcasted_iota(jnp.int32, (tm, 1), 0)
        cnts = _window_counts(pos)
        halo = jnp.where(it == nt - 1, 0.0, halo_ref[...])
        scaled = []
        halos = []
        for g, win in enumerate(POOL_WINDOWS):
            cols = slice(g * POOL_GROUP, (g + 1) * POOL_GROUP)
            scaled.append(dpd[:, cols] / cnts[g])
            halos.append(halo[:, cols] / float(win))
        ext = jnp.concatenate([jnp.concatenate(scaled, axis=1), jnp.concatenate(halos, axis=1)], axis=0)
        sums = _window_sums(ext, tm, 0, True)
        du = (jnp.concatenate(sums, axis=1) - dpd).astype(BF16)
        du_ref[...] = du
        g1 = g_ref[...]
        scale1 = 1.0 + mod_ref[0, 1:2, :]
        for c in range(ROW_CHUNKS):
            rows = slice(c * (tm // ROW_CHUNKS), (c + 1) * (tm // ROW_CHUNKS))
            dh1 = (_dot_nt(dq_ref[rows, :], w_ref[0]) + _dot_nt(dk_ref[rows, :], w_ref[1])
                   + _dot_nt(dv_ref[rows, :], w_ref[2]) + _dot_nt(du_ref[rows, :], w_ref[3]))
            xf = x_ref[rows, :]
            r1 = _rms(xf)
            n1 = xf * r1
            accb_ref[0, 0:1, :] += _colsum(dh1)
            accb_ref[0, 1:2, :] += _colsum(dh1 * (n1 * g1))
            accg_ref[0:1, :] += _colsum((dh1 * scale1) * n1)
            gx_ref[rows, :] = dx1_ref[rows, :] + _norm_bwd((dh1 * scale1) * g1, n1, r1)

    tok = lambda i: (i, 0)
    const2 = lambda i: (0, 0)
    hb = tm // HALO
    last = t_all // HALO - 1
    return pl.pallas_call(
        body, name="inproj_bwd", grid=(t_all // tm,),
        in_specs=[pl.BlockSpec((tm, p), tok), pl.BlockSpec((tm, p), tok), pl.BlockSpec((tm, p), tok),
                  pl.BlockSpec((tm, p), tok),
                  pl.BlockSpec((HALO, p), lambda i: (jnp.minimum((i + 1) * hb, last), 0)),
                  pl.BlockSpec((tm, d), tok), pl.BlockSpec((tm, d), tok),
                  pl.BlockSpec((1, MOD_ROWS, d), lambda i: (i // nt, 0, 0)),
                  pl.BlockSpec((1, d), const2),
                  pl.BlockSpec((N_CHIPS, d, p), lambda i: (0, 0, 0))],
        out_specs=[pl.BlockSpec((tm, d), tok), pl.BlockSpec((tm, p), tok),
                   pl.BlockSpec((1, 8, d), lambda i: (i // nt, 0, 0)),
                   pl.BlockSpec((8, d), const2)],
        out_shape=[jax.ShapeDtypeStruct((t_all, d), F32), jax.ShapeDtypeStruct((t_all, p), BF16),
                   jax.ShapeDtypeStruct((t_all // seq, 8, d), F32),
                   jax.ShapeDtypeStruct((8, d), F32)],
        compiler_params=_params(("arbitrary",)),
    )(dq, dk, dv, dpd, dpd, x, dx1, mod, g_pre, w_in)


def _tn_matmul(x, ys, nk, bt, name, after=()):
    t_all = x.shape[-2]
    m = x.shape[-1]
    ny = len(ys)
    nt = t_all // bt

    def spec(arr):
        if arr.ndim == 3:
            return pl.BlockSpec((1, bt, arr.shape[-1]), lambda k, t: (k, t, 0))
        return pl.BlockSpec((bt, arr.shape[-1]), lambda k, t: (t, 0))

    def tile(ref):
        return ref[0] if len(ref.shape) == 3 else ref[...]

    def body(*refs):
        outs = refs[1 + ny + len(after):]
        x_ref, y_refs, o_refs, h_refs = refs[0], refs[1:1 + ny], outs[:ny], outs[ny:]
        t = pl.program_id(1)
        xt = tile(x_ref)
        for y_ref, o_ref, h_ref in zip(y_refs, o_refs, h_refs):
            part = _dot_tn(xt, tile(y_ref))

            @pl.when(t == 0)
            def _(o_ref=o_ref, part=part):
                o_ref[0] = part

            @pl.when(t > 0)
            def _(o_ref=o_ref, part=part):
                o_ref[0] += part

            @pl.when(t == nt - 1)
            def _(o_ref=o_ref, h_ref=h_ref):
                h_ref[0] = o_ref[0].astype(BF16)

    out_specs = [pl.BlockSpec((1, m, y.shape[-1]), lambda k, t: (k, 0, 0)) for y in ys]
    out = pl.pallas_call(
        body, name=name, grid=(nk, nt),
        in_specs=[spec(x)] + [spec(y) for y in ys] + [_ANY] * len(after),
        out_specs=out_specs * 2,
        out_shape=[jax.ShapeDtypeStruct((nk, m, y.shape[-1]), dt) for dt in (F32, BF16) for y in ys],
        compiler_params=_params(("arbitrary", "arbitrary")),
    )(x, *ys, *after)
    return out[:ny], out[ny:]


def _tn_matmul_stacked(x, ys, bt, name, after=()):
    t_all, m = x.shape
    n = ys[0].shape[1]
    ny = len(ys)
    nt = t_all // bt

    def body(*refs):
        x_ref, y_refs, (o_ref, h_ref) = refs[0], refs[1:1 + ny], refs[1 + ny + len(after):]
        t = pl.program_id(0)
        xt = x_ref[...]

        @pl.when(t == 0)
        def _():
            o_ref[...] = jnp.zeros_like(o_ref)

        for j, y_ref in enumerate(y_refs):
            o_ref[j] += _dot_tn(xt, y_ref[...])

        @pl.when(t == nt - 1)
        def _():
            h_ref[...] = o_ref[...].astype(BF16)

    whole = pl.BlockSpec((ny, m, n), lambda t: (0, 0, 0))
    return pl.pallas_call(
        body, name=name, grid=(nt,),
        in_specs=[pl.BlockSpec((bt, m), lambda t: (t, 0))] + [pl.BlockSpec((bt, n), lambda t: (t, 0))] * ny
        + [_ANY] * len(after),
        out_specs=[whole, whole],
        out_shape=[jax.ShapeDtypeStruct((ny, m, n), F32), jax.ShapeDtypeStruct((ny, m, n), BF16)],
        compiler_params=_params(("arbitrary",)),
    )(x, *ys, *after)


def _cond_fwd(c_all, w_q, b_q, bn):
    nrow, d = c_all.shape
    ncol = w_q.shape[1]

    def body(c_ref, w_ref, b_ref, sc_ref, mod_ref):
        cf = c_ref[...]
        sc = cf * _sigmoid(cf)
        sc_ref[...] = sc
        shi, slo = _split(sc)
        whi, wlo = _split(w_ref[...])
        mod_ref[...] = (_dot(shi, whi) + _dot(shi, wlo) + _dot(slo, whi)) + b_ref[...]

    return pl.pallas_call(
        body, name="cond_fwd", grid=(ncol // bn,),
        in_specs=[pl.BlockSpec((nrow, d), lambda n: (0, 0)), pl.BlockSpec((d, bn), lambda n: (0, n)),
                  pl.BlockSpec((1, bn), lambda n: (0, n))],
        out_specs=[pl.BlockSpec((nrow, d), lambda n: (0, 0)), pl.BlockSpec((nrow, bn), lambda n: (0, n))],
        out_shape=[jax.ShapeDtypeStruct((nrow, d), F32), jax.ShapeDtypeStruct((nrow, ncol), F32)],
        compiler_params=_params(("arbitrary",)),
    )(c_all, w_q, b_q)


def _cond_bwd(sc_all, dmod_q, bn):
    nrow, d = sc_all.shape
    ncol = dmod_q.shape[1]

    def body(sc_ref, dm_ref, gw_ref):
        shi, slo = _split(sc_ref[...])
        dhi, dlo = _split(dm_ref[...])
        gw_ref[...] = _dot_tn(shi, dhi) + _dot_tn(shi, dlo) + _dot_tn(slo, dhi)

    return pl.pallas_call(
        body, name="cond_bwd", grid=(ncol // bn,),
        in_specs=[pl.BlockSpec((nrow, d), lambda n: (0, 0)), pl.BlockSpec((nrow, bn), lambda n: (0, n))],
        out_specs=pl.BlockSpec((d, bn), lambda n: (0, n)),
        out_shape=jax.ShapeDtypeStruct((d, ncol), F32),
        compiler_params=_params(("arbitrary",)),
    )(sc_all, dmod_q)


def _row_block(rows, cols, budget=1 << 18):
    best = None
    for br in range(8, rows + 1, 8):
        if rows % br == 0 and br * cols <= budget:
            best = br
    return best if best is not None else rows


def _adam_math(w, g, m, v):
    c1 = 1.0 - ADAM_B1 ** ADAM_STEP
    c2 = 1.0 - ADAM_B2 ** ADAM_STEP
    m2 = ADAM_B1 * m + (1.0 - ADAM_B1) * g
    v2 = ADAM_B2 * v + (1.0 - ADAM_B2) * (g * g)
    return -ADAM_LR * ((m2 / c1) / (jnp.sqrt(v2 / c2) + ADAM_EPS) + ADAM_WD * w), m2, v2


def _small_updates(summed, params):
    n = len(params)

    def body(s_ref, *refs):
        ins, outs = refs[:3 * n], refs[3 * n:]
        for p, (_, _, _, pick) in enumerate(params):
            w_ref, m_ref, v_ref = ins[3 * p:3 * p + 3]
            g = pick(s_ref)
            delta, m2, v2 = _adam_math(w_ref[...], g, m_ref[...], v_ref[...])
            for o_ref, val in zip(outs[4 * p:4 * p + 4], (g, delta, m2, v2)):
                o_ref[...] = val

    out = pl.pallas_call(
        body, name="adamw_small",
        out_shape=[jax.ShapeDtypeStruct(w.shape, F32) for w, _, _, _ in params for _ in range(4)],
        compiler_params=pltpu.CompilerParams(vmem_limit_bytes=VMEM_LIMIT),
    )(summed, *[t for w, m, v, _ in params for t in (w, m, v)])
    return [tuple(out[4 * p:4 * p + 4]) for p in range(n)]


def _adamw(w, g, m, v, name):
    rows, cols = w.shape
    br = _row_block(rows, cols)

    def body(w_ref, g_ref, m_ref, v_ref, d_ref, nm_ref, nv_ref):
        d_ref[...], nm_ref[...], nv_ref[...] = _adam_math(w_ref[...], g_ref[...], m_ref[...], v_ref[...])

    blk = pl.BlockSpec((br, cols), lambda i: (i, 0))
    return pl.pallas_call(
        body, name=name, grid=(rows // br,),
        in_specs=[blk] * 4, out_specs=[blk] * 3,
        out_shape=[jax.ShapeDtypeStruct((rows, cols), F32)] * 3,
        compiler_params=_params(("arbitrary",)),
    )(w, g, m, v)


def _all_gather(x_shard, name):
    m_per, n = x_shard.shape

    def body(x_ref, out_ref, send_sems, recv_sems, local_sem):
        x, y, c = _position()
        me, sibling = (x, y, c), (x, y, 1 - c)
        chips = [(1 - x, y), (x, 1 - y), (1 - x, 1 - y)]

        def rows(px, py, pc):
            return out_ref.at[pl.ds((4 * px + 2 * py + pc) * m_per, m_per), :]

        def copy(k, block, to, src=None):
            return pltpu.make_async_remote_copy(
                src_ref=rows(*block) if src is None else src, dst_ref=rows(*block),
                send_sem=send_sems.at[k], recv_sem=recv_sems.at[k], device_id=to, device_id_type=MESH)

        mine = pltpu.make_async_copy(x_ref, rows(*me), local_sem)
        mine.start()
        first = [copy(0, me, sibling, src=x_ref)]
        first += [copy(1 + j, me, (*chip, c), src=x_ref) for j, chip in enumerate(chips)]
        for cp in first:
            cp.start()
        passed = [copy(4 + j, (*chip, c), sibling) for j, chip in enumerate(chips)]
        for j, chip in enumerate(chips):
            copy(1 + j, (*chip, c), me).wait_recv()
            passed[j].start()
        copy(0, sibling, me).wait_recv()
        for j, chip in enumerate(chips):
            copy(4 + j, (*chip, 1 - c), me).wait_recv()
        for cp in first + passed:
            cp.wait_send()
        mine.wait()

    return pl.pallas_call(
        body, name=name,
        out_shape=jax.ShapeDtypeStruct((N_DEV * m_per, n), x_shard.dtype),
        in_specs=[pl.BlockSpec(memory_space=pltpu.VMEM)],
        out_specs=pl.BlockSpec(memory_space=pltpu.VMEM),
        scratch_shapes=[pltpu.SemaphoreType.DMA((7,)), pltpu.SemaphoreType.DMA((7,)), pltpu.SemaphoreType.DMA],
        compiler_params=pltpu.CompilerParams(vmem_limit_bytes=VMEM_LIMIT),
    )(x_shard)


_ANY = pl.BlockSpec(memory_space=pl.ANY)


def _place_quarters(place, quarters):
    steps = 2

    def body(place_ref, *refs):
        n = len(refs) // 2
        for w_ref, o_ref in zip(refs[:n], refs[n:]):
            o_ref[0] = w_ref[...].astype(BF16)

    return pl.pallas_call(
        body, name="place_quarters",
        grid_spec=pltpu.PrefetchScalarGridSpec(
            num_scalar_prefetch=1, grid=(steps,),
            in_specs=[pl.BlockSpec((q.shape[0] // steps, q.shape[1]), lambda r, place_ref: (r, 0)) for q in quarters],
            out_specs=[pl.BlockSpec((1, q.shape[0] // steps, q.shape[1]), lambda r, place_ref: (place_ref[0], r, 0))
                       for q in quarters]),
        out_shape=[jax.ShapeDtypeStruct((N_CHIPS,) + q.shape, BF16) for q in quarters],
        compiler_params=_params(("arbitrary",)),
    )(place, *quarters)


_HBM = pl.BlockSpec(memory_space=pltpu.HBM)
_SEM = pl.BlockSpec(memory_space=pltpu.SEMAPHORE)
_EFFECT = pltpu.SideEffectType.DATAFLOW_SIDE_EFFECTING


def _quarter_halves(shapes, a, which):
    hr = shapes[a][0] // 2
    return pl.ds(which * hr, hr)


def _gather_start(placed, after, tag):
    n = len(placed)
    m = len(after)
    shapes = [b.shape[1:] for b in placed]

    def body(*refs):
        g_refs = refs[:n]
        send_sems, recv_sems = refs[n + m], refs[n + m + 1]
        token = refs[2 * n + m + 2]
        x, y, c = _position()
        chips = [(1 - x, y), (x, 1 - y), (1 - x, 1 - y)]
        mine = 2 * x + y
        for a in range(n):
            ref = g_refs[a].at[mine, _quarter_halves(shapes, a, c), :]
            for p in range(3):
                pltpu.make_async_remote_copy(
                    src_ref=ref, dst_ref=ref, send_sem=send_sems.at[3 * a + p], recv_sem=recv_sems.at[3 * a + p],
                    device_id=(*chips[p], c), device_id_type=MESH).start()
        token[...] = jnp.zeros_like(token)

    out = pl.pallas_call(
        body, name="gather_start_" + tag,
        out_shape=(pltpu.SemaphoreType.DMA((3 * n,)), pltpu.SemaphoreType.DMA((3 * n,)),
                   *[pltpu.HBM(b.shape, b.dtype) for b in placed], jax.ShapeDtypeStruct((8, LANES), F32)),
        in_specs=[_HBM] * n + [_ANY] * m,
        out_specs=(_SEM, _SEM, *[_HBM] * n, pl.BlockSpec(memory_space=pltpu.VMEM)),
        input_output_aliases={a: 2 + a for a in range(n)},
        compiler_params=pltpu.CompilerParams(has_side_effects=_EFFECT),
    )(*[pltpu.with_memory_space_constraint(b, pltpu.HBM) for b in placed], *after)
    return out[0], out[1], list(out[2:2 + n]), out[2 + n]


def _gather_wait(send_sems, recv_sems, thru, after, tag):
    n = len(thru)
    shapes = [b.shape[1:] for b in thru]

    def body(*refs):
        g_refs = refs[:n]
        send_sems, recv_sems = refs[n], refs[n + 1]
        x, y, c = _position()
        chips = [(1 - x, y), (x, 1 - y), (1 - x, 1 - y)]
        mine = 2 * x + y
        for a in range(n):
            rows = _quarter_halves(shapes, a, c)
            for p, (cx, cy) in enumerate(chips):
                copy = pltpu.make_async_remote_copy(
                    src_ref=g_refs[a].at[mine, rows, :], dst_ref=g_refs[a].at[2 * cx + cy, rows, :],
                    send_sem=send_sems.at[3 * a + p], recv_sem=recv_sems.at[3 * a + p],
                    device_id=(cx, cy, c), device_id_type=MESH)
                copy.wait_send()
                copy.wait_recv()

    return pl.pallas_call(
        body, name="gather_wait_" + tag,
        out_shape=[pltpu.HBM(b.shape, b.dtype) for b in thru],
        in_specs=[_HBM] * n + [_SEM, _SEM, _ANY], out_specs=[_HBM] * n,
        input_output_aliases={a: a for a in range(n)},
        compiler_params=pltpu.CompilerParams(has_side_effects=_EFFECT),
    )(*thru, send_sems, recv_sems, after)


def _gather_forward(bufs, tag):
    n = len(bufs)
    shapes = [b.shape[1:] for b in bufs]

    def body(*refs):
        g_refs = refs[n:2 * n]
        send_sems, recv_sems = refs[2 * n:]
        x, y, c = _position()
        chips = [(1 - x, y), (x, 1 - y), (1 - x, 1 - y)]

        def over_d2d(a, p, which):
            cx, cy = chips[p]
            ref = g_refs[a].at[2 * cx + cy, _quarter_halves(shapes, a, which), :]
            return pltpu.make_async_remote_copy(
                src_ref=ref, dst_ref=ref, send_sem=send_sems.at[3 * a + p], recv_sem=recv_sems.at[3 * a + p],
                device_id=(x, y, 1 - c), device_id_type=MESH)

        sends = [over_d2d(a, p, c) for a in range(n) for p in range(3)]
        for cp in sends:
            cp.start()
        for a in range(n):
            for p in range(3):
                over_d2d(a, p, 1 - c).wait_recv()
        for cp in sends:
            cp.wait_send()

    return pl.pallas_call(
        body, name="gather_forward_" + tag,
        out_shape=[jax.ShapeDtypeStruct(b.shape, BF16) for b in bufs],
        in_specs=[_ANY] * n, out_specs=[_ANY] * n,
        input_output_aliases={a: a for a in range(n)},
        scratch_shapes=[pltpu.SemaphoreType.DMA((3 * n,)), pltpu.SemaphoreType.DMA((3 * n,))],
    )(*bufs)


_FLIPS = [(fx, fy, fc) for fx in (0, 1) for fy in (0, 1) for fc in (0, 1)][1:]


def _flipped(pos, flip):
    return tuple(1 - p if f else p for p, f in zip(pos, flip))


def _direct_gather_start(slots):
    def body(s_ref, send_sems, recv_sems, thru, token):
        me = _position()
        mine = s_ref.at[4 * me[0] + 2 * me[1] + me[2]]
        for r, flip in enumerate(_FLIPS):
            pltpu.make_async_remote_copy(
                src_ref=mine, dst_ref=mine, send_sem=send_sems.at[r], recv_sem=recv_sems.at[r],
                device_id=_flipped(me, flip), device_id_type=MESH).start()
        token[...] = jnp.zeros_like(token)

    return pl.pallas_call(
        body, name="small_gather_start",
        out_shape=(pltpu.SemaphoreType.DMA((len(_FLIPS),)), pltpu.SemaphoreType.DMA((len(_FLIPS),)),
                   pltpu.HBM(slots.shape, slots.dtype), jax.ShapeDtypeStruct((8, LANES), F32)),
        in_specs=[_HBM], out_specs=(_SEM, _SEM, _HBM, pl.BlockSpec(memory_space=pltpu.VMEM)),
        input_output_aliases={0: 2},
        compiler_params=pltpu.CompilerParams(has_side_effects=_EFFECT),
    )(pltpu.with_memory_space_constraint(slots, pltpu.HBM))


def _direct_gather_wait(send_sems, recv_sems, slots, after):
    def body(s_ref, send_sems, recv_sems, after_ref, out_ref):
        me = _position()
        mine = s_ref.at[4 * me[0] + 2 * me[1] + me[2]]
        for r, flip in enumerate(_FLIPS):
            peer = _flipped(me, flip)
            copy = pltpu.make_async_remote_copy(
                src_ref=mine, dst_ref=s_ref.at[4 * peer[0] + 2 * peer[1] + peer[2]],
                send_sem=send_sems.at[r], recv_sem=recv_sems.at[r], device_id=peer, device_id_type=MESH)
            copy.wait_send()
            copy.wait_recv()

    return pl.pallas_call(
        body, name="small_gather_wait",
        out_shape=pltpu.HBM(slots.shape, slots.dtype),
        in_specs=[_HBM, _SEM, _SEM, _ANY], out_specs=_HBM,
        input_output_aliases={0: 0},
        compiler_params=pltpu.CompilerParams(has_side_effects=_EFFECT),
    )(slots, send_sems, recv_sems, after)


def _sibling_exchange(grads, tag):
    n = len(grads)
    shapes = [g.shape for g in grads]

    def body(*refs):
        g_refs, x_refs = refs[:n], refs[n:2 * n]
        send_sems, recv_sems = refs[2 * n:]
        x, y, c = _position()
        copies = []
        for a in range(n):
            hr = shapes[a][1] // 2
            cp = pltpu.make_async_remote_copy(
                src_ref=g_refs[a].at[:, pl.ds((1 - c) * hr, hr), :], dst_ref=x_refs[a],
                send_sem=send_sems.at[a], recv_sem=recv_sems.at[a],
                device_id=(x, y, 1 - c), device_id_type=MESH)
            cp.start()
            copies.append(cp)
        for cp in copies:
            cp.wait()

    return pl.pallas_call(
        body, name="grad_sibling_exchange_" + tag,
        out_shape=[jax.ShapeDtypeStruct((g.shape[0], g.shape[1] // 2, g.shape[2]), g.dtype) for g in grads],
        in_specs=[_ANY] * n, out_specs=[_ANY] * n,
        scratch_shapes=[pltpu.SemaphoreType.DMA((n,)), pltpu.SemaphoreType.DMA((n,))],
    )(*grads)


def _sibling_exchange_start(grads, tag):
    n = len(grads)
    lands = [lax.empty((g.shape[0], g.shape[1] // 2, g.shape[2]), g.dtype) for g in grads]

    def body(*refs):
        g_refs, x_refs = refs[:n], refs[n:2 * n]
        send_sems, recv_sems = refs[2 * n], refs[2 * n + 1]
        token = refs[4 * n + 2]
        x, y, c = _position()
        for a in range(n):
            hr = grads[a].shape[1] // 2
            pltpu.make_async_remote_copy(
                src_ref=g_refs[a].at[:, pl.ds((1 - c) * hr, hr), :], dst_ref=x_refs[a],
                send_sem=send_sems.at[a], recv_sem=recv_sems.at[a],
                device_id=(x, y, 1 - c), device_id_type=MESH).start()
        token[...] = jnp.zeros_like(token)

    both = list(grads) + lands
    out = pl.pallas_call(
        body, name="grad_sibling_exchange_start_" + tag,
        out_shape=(pltpu.SemaphoreType.DMA((n,)), pltpu.SemaphoreType.DMA((n,)),
                   *[pltpu.HBM(b.shape, b.dtype) for b in both], jax.ShapeDtypeStruct((8, LANES), F32)),
        in_specs=[_HBM] * (2 * n),
        out_specs=(_SEM, _SEM, *[_HBM] * (2 * n), pl.BlockSpec(memory_space=pltpu.VMEM)),
        input_output_aliases={a: 2 + a for a in range(2 * n)},
        compiler_params=pltpu.CompilerParams(has_side_effects=_EFFECT),
    )(*[pltpu.with_memory_space_constraint(b, pltpu.HBM) for b in both])
    return out[0], out[1], list(out[2:2 + n]), list(out[2 + n:2 + 2 * n]), out[2 + 2 * n]


def _sibling_exchange_wait(send_sems, recv_sems, grads, lands, after, tag):
    n = len(grads)

    def body(*refs):
        g_refs, x_refs = refs[:n], refs[n:2 * n]
        send_sems, recv_sems = refs[2 * n], refs[2 * n + 1]
        x, y, c = _position()
        for a in range(n):
            hr = grads[a].shape[1] // 2
            copy = pltpu.make_async_remote_copy(
                src_ref=g_refs[a].at[:, pl.ds((1 - c) * hr, hr), :], dst_ref=x_refs[a],
                send_sem=send_sems.at[a], recv_sem=recv_sems.at[a],
                device_id=(x, y, 1 - c), device_id_type=MESH)
            copy.wait_send()
            copy.wait_recv()

    both = list(grads) + list(lands)
    out = pl.pallas_call(
        body, name="grad_sibling_exchange_wait_" + tag,
        out_shape=[pltpu.HBM(b.shape, b.dtype) for b in both],
        in_specs=[_HBM] * (2 * n) + [_SEM, _SEM, _ANY], out_specs=[_HBM] * (2 * n),
        input_output_aliases={a: a for a in range(2 * n)},
        compiler_params=pltpu.CompilerParams(has_side_effects=_EFFECT),
    )(*both, send_sems, recv_sems, after)
    return list(out[n:])


def _chip_sums(core, grads, theirs, tag):
    n = len(grads)

    def body(core_ref, *refs):
        g_refs, t_refs, o_refs = refs[:n], refs[n:2 * n], refs[2 * n:]
        for g_ref, t_ref, o_ref in zip(g_refs, t_refs, o_refs):
            o_ref[...] = (g_ref[...] + t_ref[...].astype(F32)).astype(BF16)

    in_specs = [pl.BlockSpec((1, g.shape[1] // 2, g.shape[2]), lambda k, core_ref: (k, core_ref[0], 0)) for g in grads]
    in_specs += [pl.BlockSpec((1,) + t.shape[1:], lambda k, core_ref: (k, 0, 0)) for t in theirs]
    return pl.pallas_call(
        body, name="grad_chip_sums_" + tag,
        grid_spec=pltpu.PrefetchScalarGridSpec(
            num_scalar_prefetch=1, grid=(N_CHIPS,), in_specs=in_specs,
            out_specs=[pl.BlockSpec((1,) + t.shape[1:], lambda k, core_ref: (k, 0, 0)) for t in theirs]),
        out_shape=[jax.ShapeDtypeStruct(t.shape, BF16) for t in theirs],
        compiler_params=_params(("arbitrary",)),
    )(core, *grads, *theirs)


def _chip_exchange_start(sums, after, tag):
    n = len(sums)
    m = len(after)
    lands = [lax.empty((3,) + s.shape[1:], BF16) for s in sums]

    def body(*refs):
        s_refs, y_refs = refs[:n], refs[n:2 * n]
        send_sems, recv_sems = refs[2 * n + m], refs[2 * n + m + 1]
        token = refs[4 * n + m + 2]
        x, y, c = _position()
        chips = [(1 - x, y), (x, 1 - y), (1 - x, 1 - y)]
        for a in range(n):
            for p, (cx, cy) in enumerate(chips):
                pltpu.make_async_remote_copy(
                    src_ref=s_refs[a].at[2 * cx + cy], dst_ref=y_refs[a].at[p],
                    send_sem=send_sems.at[3 * a + p], recv_sem=recv_sems.at[3 * a + p],
                    device_id=(cx, cy, c), device_id_type=MESH).start()
        token[...] = jnp.zeros_like(token)

    both = list(sums) + lands
    out = pl.pallas_call(
        body, name="grad_chip_exchange_start_" + tag,
        out_shape=(pltpu.SemaphoreType.DMA((3 * n,)), pltpu.SemaphoreType.DMA((3 * n,)),
                   *[pltpu.HBM(b.shape, b.dtype) for b in both], jax.ShapeDtypeStruct((8, LANES), F32)),
        in_specs=[_HBM] * (2 * n) + [_ANY] * m,
        out_specs=(_SEM, _SEM, *[_HBM] * (2 * n), pl.BlockSpec(memory_space=pltpu.VMEM)),
        input_output_aliases={a: 2 + a for a in range(2 * n)},
        compiler_params=pltpu.CompilerParams(has_side_effects=_EFFECT),
    )(*[pltpu.with_memory_space_constraint(b, pltpu.HBM) for b in both], *after)
    return out[0], out[1], list(out[2:2 + n]), list(out[2 + n:2 + 2 * n]), out[2 + 2 * n]


def _chip_exchange_wait(send_sems, recv_sems, sums, lands, after, tag):
    n = len(sums)

    def body(*refs):
        s_refs, y_refs = refs[:n], refs[n:2 * n]
        send_sems, recv_sems = refs[2 * n], refs[2 * n + 1]
        x, y, c = _position()
        chips = [(1 - x, y), (x, 1 - y), (1 - x, 1 - y)]
        for a in range(n):
            for p, (cx, cy) in enumerate(chips):
                copy = pltpu.make_async_remote_copy(
                    src_ref=s_refs[a].at[2 * cx + cy], dst_ref=y_refs[a].at[p],
                    send_sem=send_sems.at[3 * a + p], recv_sem=recv_sems.at[3 * a + p],
                    device_id=(cx, cy, c), device_id_type=MESH)
                copy.wait_send()
                copy.wait_recv()

    both = list(sums) + list(lands)
    out = pl.pallas_call(
        body, name="grad_chip_exchange_wait_" + tag,
        out_shape=[pltpu.HBM(b.shape, b.dtype) for b in both],
        in_specs=[_HBM] * (2 * n) + [_SEM, _SEM, _ANY], out_specs=[_HBM] * (2 * n),
        input_output_aliases={a: a for a in range(2 * n)},
        compiler_params=pltpu.CompilerParams(has_side_effects=_EFFECT),
    )(*both, send_sems, recv_sems, after)
    return list(out[:n]), list(out[n:])


def _total_sums(place, sums, parts, after, tag):
    n = len(parts)
    m = len(after)
    steps = 2

    def body(place_ref, *refs):
        for s_ref, y_ref, o_ref in zip(refs[:n], refs[n:2 * n], refs[2 * n + m:]):
            o_ref[0] = ((s_ref[0].astype(F32) + y_ref[0].astype(F32)) + y_ref[1].astype(F32)) + y_ref[2].astype(F32)

    def step_rows(pt):
        return pt.shape[1] // steps

    in_specs = [pl.BlockSpec((1, step_rows(s), s.shape[2]), lambda r, place_ref: (place_ref[0], r, 0)) for s in sums]
    in_specs += [pl.BlockSpec((3, step_rows(pt), pt.shape[2]), lambda r, place_ref: (0, r, 0)) for pt in parts]
    in_specs += [_ANY] * m
    return pl.pallas_call(
        body, name="grad_total_sums_" + tag,
        grid_spec=pltpu.PrefetchScalarGridSpec(
            num_scalar_prefetch=1, grid=(steps,), in_specs=in_specs,
            out_specs=[pl.BlockSpec((1, step_rows(pt), pt.shape[2]), lambda r, place_ref: (place_ref[1], r, 0))
                       for pt in parts]),
        out_shape=[jax.ShapeDtypeStruct((2,) + pt.shape[1:], F32) for pt in parts],
        compiler_params=_params(("arbitrary",)),
    )(place, *sums, *parts, *after)


def _sibling_share(halves, tag):
    n = len(halves)

    def body(*refs):
        f_refs = refs[n:2 * n]
        send_sems, recv_sems = refs[2 * n:]
        x, y, c = _position()
        copies = []
        for a in range(n):
            cp = pltpu.make_async_remote_copy(
                src_ref=f_refs[a].at[c], dst_ref=f_refs[a].at[c], send_sem=send_sems.at[a], recv_sem=recv_sems.at[a],
                device_id=(x, y, 1 - c), device_id_type=MESH)
            cp.start()
            copies.append(cp)
        for a, cp in enumerate(copies):
            cp.wait_send()
            pltpu.make_async_remote_copy(
                src_ref=f_refs[a].at[1 - c], dst_ref=f_refs[a].at[1 - c], send_sem=send_sems.at[a],
                recv_sem=recv_sems.at[a], device_id=(x, y, c), device_id_type=MESH).wait_recv()

    return pl.pallas_call(
        body, name="grad_sibling_share_" + tag,
        out_shape=[jax.ShapeDtypeStruct(h.shape, F32) for h in halves],
        in_specs=[_ANY] * n, out_specs=[_ANY] * n,
        input_output_aliases={a: a for a in range(n)},
        scratch_shapes=[pltpu.SemaphoreType.DMA((n,)), pltpu.SemaphoreType.DMA((n,))],
    )(*halves)


def _group_sum(stacked, nrow, name):
    total, n = stacked.shape
    groups = total // nrow

    def body(g_ref, o_ref):
        acc = g_ref[0:nrow, :]
        for grp in range(1, groups):
            acc = acc + g_ref[grp * nrow:(grp + 1) * nrow, :]
        o_ref[...] = acc

    return pl.pallas_call(
        body, name=name,
        out_shape=jax.ShapeDtypeStruct((nrow, n), F32),
        compiler_params=pltpu.CompilerParams(vmem_limit_bytes=VMEM_LIMIT),
    )(stacked)


def _local_step(xt, tgt, mod, gains, w_pool, pool_scale, w_in, later_weights, on_ffn_grads, after_mixer_bwd,
                on_small_grads, seq):
    g_mpre, g_mpost, g_fpre, g_fpost = gains
    d = xt.shape[1]
    tm, tq = min(TOKEN_TILE, seq), min(ATTN_TILE, seq)

    h1, qn, k, v, u, kt, vt = _prenorm_proj(xt, mod, g_mpre, w_in, seq, tm)
    tk = min(ATTN_KEY_TILE, tq // 2)
    o, ltot = _attn_fwd(qn, k, vt, seq, tq, tk)
    w_out, w_g, w_u, w_d = later_weights(o)
    w_out2 = w_out.reshape(d, d)
    pooled, mixin, mix, x1, h2 =_mixer_post(u, o, xt, mod, g_mpost, g_fpre, w_pool, pool_scale, w_out2, seq, tm)
    a, b, fin, dy, df, loss_blk, accb4, accg4 = _ffn_fwd(h2, w_g, w_u, w_d, x1, tgt, mod, g_fpost, seq, tm)
    da, db, dx1, dmix, accb5, accg5 = _ffn_bwd(df, a, b, w_d, w_g, w_u, x1, dy, mix, mod, g_fpre, g_mpost, seq, tm)
    bt = min(GRAD_TOKEN_TILE, xt.shape[0])
    bt_one = min(2 * GRAD_TOKEN_TILE, xt.shape[0])
    (g_g,), (g_g16,) = _tn_matmul(da, [h2], w_g.shape[0], bt_one, "grad_w_gate")
    (g_u,), (g_u16,) = _tn_matmul(db, [h2], w_u.shape[0], bt_one, "grad_w_up")
    (g_d,), (g_d16,) = _tn_matmul(fin, [df], w_d.shape[0], bt_one, "grad_w_down")
    token = on_ffn_grads([g_g, g_u, g_d], [g_g16, g_u16, g_d16])
    do, dpd, dps, dwp = _mixer_bwd(dmix, w_out2, pooled, w_pool, pool_scale + token, seq, tm)
    order = after_mixer_bwd(do)
    dq, dk, dv = _attn_bwd(qn, k, kt, v, do, ltot, seq, tq, tk, order)
    gx, du, accb8, accg8 = _inproj_bwd(dq, dk, dv, dpd, xt, dx1, mod, g_mpre, w_in, seq, tm)

    dmod = jnp.stack([accb8[:, 0], accb8[:, 1], accb5[:, 2], accb5[:, 0], accb5[:, 1], accb4[:, 0]], axis=1)
    dgain = jnp.stack([accg8[0], accg5[1], accg5[0], accg4[0]], axis=0)
    behind = on_small_grads(loss_blk, dmod, dgain, dps[0:1], dwp)
    g_in, g_in16 = _tn_matmul_stacked(h1, [dq, dk, dv, du], bt, "grad_w_in", behind)
    g_out, g_out16 = [parts[0].reshape(w_out.shape)
                      for parts in _tn_matmul(mixin, [dmix], 1, bt_one, "grad_w_out", behind)]
    grads = [g_in, g_out, g_g, g_u, g_d]
    grads16 = [g_in16, g_out16, g_g16, g_u16, g_d16]
    return gx, grads, grads16


def kernel(x, c, w_cond, b_cond, g_mix_pre, g_mix_post, w_in, w_pool, pool_scale, w_out, g_ffn_pre, g_ffn_post, w_gate, w_up, w_down, loss_target, m_w_cond, m_b_cond, m_g_mix_pre, m_g_mix_post, m_w_in, m_w_pool, m_pool_scale, m_w_out, m_g_ffn_pre, m_g_ffn_post, m_w_gate, m_w_up, m_w_down, v_w_cond, v_b_cond, v_g_mix_pre, v_g_mix_post, v_w_in, v_w_pool, v_pool_scale, v_w_out, v_g_ffn_pre, v_g_ffn_post, v_w_gate, v_w_up, v_w_down):
    xi, yi, ci = _position()
    chip = 2 * xi + yi
    dev = 4 * xi + 2 * yi + ci
    nb, seq, d = x.shape
    t_all = nb * seq
    xt = x.reshape(t_all, d)
    tgt = loss_target.reshape(t_all, d)
    ncol = w_cond.shape[2]
    pw = pool_scale.shape[1]

    place = jnp.stack([chip, ci]).astype(jnp.int32)
    turned = lambda t: jnp.swapaxes(t[0], 0, 1)
    placed = _place_quarters(place, [w_in[0], w_out[0], turned(w_gate), turned(w_up), w_down[0]])
    in_sems = _gather_start(placed[:1], [], "in")

    c_pad = jnp.concatenate([c, jnp.zeros((8 - nb, d), F32)], axis=0) + in_sems[3][0:1, 0:1]
    c_all = _all_gather(c_pad, "gather_c").reshape(N_DEV, 8, d)[:, :nb].reshape(N_DEV * nb, d)
    b_q = lax.dynamic_slice(b_cond, (0, chip * ncol), (1, ncol))
    sc_all, mod_q = _cond_fwd(c_all, w_cond[0], b_q, 512)
    mod_parts = _all_gather(mod_q, "gather_mod").reshape(N_DEV, N_DEV * nb, ncol)
    mod_rows = lax.dynamic_slice(mod_parts, (0, dev * nb, 0), (N_DEV, nb, ncol))[0::2]
    mod = jnp.transpose(mod_rows, (1, 0, 2)).reshape(nb, N_MOD, d)
    mod = jnp.concatenate([mod, jnp.zeros((nb, MOD_ROWS - N_MOD, d), F32)], axis=1)

    (w_in_all,) = _gather_forward(_gather_wait(*in_sems[:3], mod, "in"), "in")
    send_sems, recv_sems, in_flight, token = _gather_start(placed[1:], [mod, w_in_all], "rest")
    mod = mod + token[0:1, 0:1]

    def later_weights(after):
        return _gather_forward(_gather_wait(send_sems, recv_sems, in_flight, after, "rest"), "rest")

    ffn_split = []

    ffn_sibling = []

    def on_ffn_grads(ffn_grads, ffn_grads16):
        ffn_sibling.extend(_sibling_exchange_start(ffn_grads16, "ffn"))
        ffn_sibling.append(ffn_grads)
        return ffn_sibling[4][0:1, 0:1]

    def after_mixer_bwd(do):
        theirs = _sibling_exchange_wait(*ffn_sibling[:4], do, "ffn")
        ffn_split.extend(_chip_exchange_start(_chip_sums(place[1:], ffn_sibling[5], theirs, "ffn"), [], "ffn"))
        return ffn_split[4][0:1, 0:1]

    wp_rows = w_pool[0].size // d
    loss_row = 2 * N_MOD + 4 + 1
    pad_rows = 24 - (loss_row + 1)
    prow = 24 + wp_rows
    small_split = []

    def on_small_grads(loss_blk, dmod, dgain, dps, dwp):
        payload = jnp.concatenate([
            dmod.reshape(nb * N_MOD, d), dgain,
            jnp.concatenate([dps, jnp.zeros((1, d - pw), F32)], axis=1),
            jnp.concatenate([loss_blk[0:1], jnp.zeros((1, d - LANES), F32)], axis=1),
            jnp.zeros((pad_rows, d), F32),
            jnp.concatenate(jnp.split(dwp.reshape(-1, dwp.shape[-1]), d // dwp.shape[-1], axis=0), axis=1)], axis=0)
        slots = lax.dynamic_update_slice(lax.empty((N_DEV, prow, d), F32), payload[None], (dev, 0, 0))
        small_split.extend(_direct_gather_start(slots))
        return [small_split[3]]

    gains = (g_mix_pre, g_mix_post, g_ffn_pre, g_ffn_post)
    gx, grads, grads16 = _local_step(
        xt, tgt, mod, gains, w_pool[0], pool_scale, w_in_all, later_weights, on_ffn_grads, after_mixer_bwd,
        on_small_grads, seq)

    sums_ffn, parts_ffn = _chip_exchange_wait(*ffn_split[:4], gx, "ffn")
    gathered = _direct_gather_wait(*small_split[:3], grads16[1]).reshape(N_DEV * prow, d)
    summed = _group_sum(gathered, prow, "small_device_sum")
    loss = summed[loss_row, 0]
    dmod_all = gathered.reshape(N_DEV, prow, d)[:, :nb * N_MOD].reshape(N_DEV * nb, N_MOD * d)
    dmod_q = lax.dynamic_slice(dmod_all, (0, chip * ncol), (N_DEV * nb, ncol))
    g_w_cond = _cond_bwd(sc_all, dmod_q, 512)
    first_gain = 2 * N_MOD

    theirs = _sibling_exchange(grads16[:2], "mix")
    mix_split = _chip_exchange_start(_chip_sums(place[1:], grads[:2], theirs, "mix"), [gathered], "mix")
    unfold = lambda halves: [g.reshape(2 * g.shape[1], g.shape[2]) for g in halves]
    g_ffn = unfold(_sibling_share(_total_sums(place, sums_ffn, parts_ffn, [mix_split[4]], "ffn"), "ffn"))

    results = {}

    def update(name, w2, g2, m2, v2, shape):
        delta, new_m, new_v = _adamw(w2, g2, m2, v2, "adamw_" + name)
        back = (lambda t: jnp.swapaxes(t, 0, 1)[None]) if shape is None else (lambda t: t.reshape(shape))
        results[name] = [back(t) for t in (g2, delta, new_m, new_v)]
        return delta

    done = [update("w_gate", turned(w_gate), g_ffn[0], turned(m_w_gate), turned(v_w_gate), None),
            update("w_up", turned(w_up), g_ffn[1], turned(m_w_up), turned(v_w_up), None),
            update("w_down", w_down[0], g_ffn[2], m_w_down[0], v_w_down[0], w_down.shape),
            update("w_cond", w_cond[0], g_w_cond, m_w_cond[0], v_w_cond[0], w_cond.shape)]

    gain_row = lambda r: (lambda s: s[first_gain + r:first_gain + r + 1, :])
    small = [
        ("b_cond", (b_cond, m_b_cond, v_b_cond), (N_MOD, d), lambda s: s[0:N_MOD, :] + s[N_MOD:2 * N_MOD, :]),
        ("g_mix_pre", (g_mix_pre, m_g_mix_pre, v_g_mix_pre), (1, d), gain_row(0)),
        ("g_mix_post", (g_mix_post, m_g_mix_post, v_g_mix_post), (1, d), gain_row(1)),
        ("g_ffn_pre", (g_ffn_pre, m_g_ffn_pre, v_g_ffn_pre), (1, d), gain_row(2)),
        ("g_ffn_post", (g_ffn_post, m_g_ffn_post, v_g_ffn_post), (1, d), gain_row(3)),
        ("pool_scale", (pool_scale, m_pool_scale, v_pool_scale), (1, pw),
         lambda s: s[first_gain + 4:first_gain + 5, 0:pw]),
        ("w_pool", (w_pool, m_w_pool, v_w_pool), (wp_rows * d // w_pool.shape[-1], w_pool.shape[-1]),
         lambda s: jnp.concatenate([s[24:24 + wp_rows, j * w_pool.shape[-1]:(j + 1) * w_pool.shape[-1]]
                                    for j in range(d // w_pool.shape[-1])], axis=0)),
    ]
    updated = _small_updates(summed, [tuple(t.reshape(flat) for t in wmv) + (pick,) for _, wmv, flat, pick in small])
    for (name, wmv, _, _), quad in zip(small, updated):
        results[name] = [t.reshape(wmv[0].shape) for t in quad]

    sums_mix, parts_mix = _chip_exchange_wait(*mix_split[:4], done[-1], "mix")
    g_mix = unfold(_sibling_share(_total_sums(place, sums_mix, parts_mix, done[:3], "mix"), "mix"))
    update("w_in", w_in[0], g_mix[0], m_w_in[0], v_w_in[0], w_in.shape)
    update("w_out", w_out[0], g_mix[1], m_w_out[0], v_w_out[0], w_out.shape)

    names = ("w_cond", "b_cond", "g_mix_pre", "g_mix_post", "w_in", "w_pool", "pool_scale", "w_out",
             "g_ffn_pre", "g_ffn_post", "w_gate", "w_up", "w_down")
    outs = [results[name][part] for part in range(4) for name in names]
    return (loss, gx.reshape(x.shape), *outs)
```

```python
import jax
import jax.numpy as jnp
import numpy as np
from jax import lax
from jax.experimental import pallas as pl
from jax.experimental.pallas import tpu as pltpu

F32 = jnp.float32
BF16 = jnp.bfloat16
MESH = pl.DeviceIdType.MESH

EPS = 1e-6
HEAD_DIM = 64
HEADS_PER_BLOCK = 2
LANES = 128
NEG_QK_SCALE = -0.125
POOL_WINDOWS = (2, 4, 8, 16)
POOL_GROUP = 128
HALO = 16
N_MOD = 6
MOD_ROWS = 8
N_CHIPS = 4
N_DEV = 8
VMEM_LIMIT = 56 * 1024 * 1024

ADAM_LR = 0.001
ADAM_B1 = 0.9
ADAM_B2 = 0.999
ADAM_EPS = 1e-08
ADAM_WD = 0.01
ADAM_STEP = 10

TOKEN_TILE = 512
GRAD_TOKEN_TILE = 2048
FFN_ROW_CHUNKS = 2
ROW_CHUNKS = 2
ATTN_TILE = 512
ATTN_KEY_TILE = 256
ATTN_ROW_CHUNK = 32
LOG_SUM_PASSES = 1


def _dot(a, b):
    return jnp.dot(a, b, preferred_element_type=F32)


def _dot_nt(a, b):
    return lax.dot_general(a, b, (((1,), (1,)), ((), ())), preferred_element_type=F32)


def _dot_tn(a, b):
    return lax.dot_general(a, b, (((0,), (0,)), ((), ())), preferred_element_type=F32)


def _split(v):
    hi = v.astype(BF16)
    lo = (v - hi.astype(F32)).astype(BF16)
    return hi, lo


def _rms(v):
    return lax.rsqrt(jnp.mean(v * v, axis=-1, keepdims=True) + EPS)


def _norm_bwd(dn, n, r):
    return r * (dn - n * jnp.mean(dn * n, axis=-1, keepdims=True))


def _sigmoid(v):
    return 0.5 * jnp.tanh(0.5 * v) + 0.5


def _colsum(v):
    return jnp.sum(v, axis=0, keepdims=True)


def _params(sem=None):
    return pltpu.CompilerParams(dimension_semantics=sem, vmem_limit_bytes=VMEM_LIMIT)


def _position():
    return lax.axis_index("x"), lax.axis_index("y"), lax.axis_index("c")


def _prenorm_proj(x, mod, g_pre, w_in, seq, tm):
    t_all, d = x.shape
    nt = seq // tm
    p = w_in.shape[2]

    def body(x_ref, mod_ref, g_ref, w_ref, h_ref, q_ref, k_ref, v_ref, u_ref, kt_ref, vt_ref):
        for c in range(ROW_CHUNKS):
            rows = slice(c * (tm // ROW_CHUNKS), (c + 1) * (tm // ROW_CHUNKS))
            xf = x_ref[rows, :]
            n = xf * _rms(xf)
            h = (n * g_ref[...]) * (1.0 + mod_ref[0, 1:2, :]) + mod_ref[0, 0:1, :]
            hb = h.astype(BF16)
            h_ref[rows, :] = hb
            q_ref[rows, :] = (_dot(hb, w_ref[0]) * NEG_QK_SCALE).astype(BF16)
            kf = _dot(hb, w_ref[1])
            vf = _dot(hb, w_ref[2])
            k_ref[rows, :] = kf.astype(BF16)
            v_ref[rows, :] = vf.astype(BF16)
            kt_ref[:, rows] = kf.T.astype(BF16)
            vt_ref[:, rows] = vf.T.astype(BF16)
            u_ref[rows, :] = _dot(hb, w_ref[3])

    tok = lambda i: (i, 0)
    tok_t = lambda i: (0, i)
    return pl.pallas_call(
        body, name="prenorm_proj", grid=(t_all // tm,),
        in_specs=[pl.BlockSpec((tm, d), tok),
                  pl.BlockSpec((1, MOD_ROWS, d), lambda i: (i // nt, 0, 0)),
                  pl.BlockSpec((1, d), lambda i: (0, 0)),
                  pl.BlockSpec((N_CHIPS, d, p), lambda i: (0, 0, 0))],
        out_specs=[pl.BlockSpec((tm, d), tok)] + [pl.BlockSpec((tm, p), tok)] * 4 + [pl.BlockSpec((p, tm), tok_t)] * 2,
        out_shape=[jax.ShapeDtypeStruct((t_all, d), BF16)] + [jax.ShapeDtypeStruct((t_all, p), BF16)] * 3
        + [jax.ShapeDtypeStruct((t_all, p), F32)] + [jax.ShapeDtypeStruct((p, t_all), BF16)] * 2,
        compiler_params=_params(("arbitrary",)),
    )(x, mod, g_pre, w_in)


def _tri_matrix(tk, kind):
    j = np.arange(2 * tk)[:, None] % tk
    s = np.arange(tk)[None, :]
    return jnp.asarray({"after": j > s, "upto": j <= s, "before": j < s}[kind], dtype=BF16)


def _neg_abs(v):
    bits = lax.bitcast_convert_type(v, jnp.int32) | jnp.int32(-2 ** 31)
    return lax.bitcast_convert_type(bits, F32)


def _row_sums(v):
    return jnp.broadcast_to(jnp.sum(v, axis=-1, keepdims=True), (v.shape[0], LANES))


def _across(v, n):
    return jnp.concatenate([v] * (n // LANES), axis=1)


def _all_masked(c, diag, rc, tk):
    return diag is not None and diag * tk >= (c + 1) * rc - 1


def _some_masked(c, diag, rc, tk):
    return diag is not None and diag * tk + tk - 1 >= c * rc


def _attn_fwd(qn, k, vt, seq, tq, tk):
    t_all, w = qn.shape
    nb, nq, ndiag = t_all // seq, seq // tq, tq // tk
    assert ndiag % 2 == 0, "two key blocks per loop trip"
    rc = ATTN_ROW_CHUNK
    heads = range(HEADS_PER_BLOCK)

    def body(q_ref, k_ref, vt_ref, tri_ref, o_ref, l_ref,
             z_buf, ls_buf, hl_buf, aft_buf, w_buf, tot_buf, acc_t, run_buf):
        i = pl.program_id(2)
        nblk = (i + 1) * ndiag
        lane = lax.broadcasted_iota(jnp.int32, (1, LANES), 1)
        row = lax.broadcasted_iota(jnp.int32, (rc, tk), 0)
        col = lax.broadcasted_iota(jnp.int32, (rc, tk), 1)
        first = lane < HEAD_DIM
        q2 = q_ref[...]
        qs = [jnp.where(first, q2, jnp.zeros_like(q2)), jnp.where(first, jnp.zeros_like(q2), q2)]
        acc_t[...] = jnp.zeros_like(acc_t)
        run_buf[...] = jnp.zeros_like(run_buf)
        w_buf[1] = jnp.zeros((HEADS_PER_BLOCK, tq, tk), BF16)

        def causal(c, diag):
            return (col + diag * tk) < (row + c * rc)

        def scores(blk, slot):
            kj = k_ref[pl.ds(pl.multiple_of(blk * tk, tk), tk), :]
            for h in heads:
                z_buf[slot, h] = _dot_nt(qs[h], kj)

        def values(blk, slot):
            keys = pl.ds(pl.multiple_of(blk * tk, tk), tk)
            for h in heads:
                dims = slice(h * HEAD_DIM, (h + 1) * HEAD_DIM)
                acc_t[dims, :] += _dot_nt(vt_ref[dims, keys], w_buf[slot, h])

        def softplus_stage(h, slot, diag):
            for c in range(tq // rc):
                rows = slice(c * rc, (c + 1) * rc)
                if _all_masked(c, diag, rc, tk):
                    hl_buf[h, rows, :] = jnp.zeros((rc, LOG_SUM_PASSES * tk), BF16)
                    tot_buf[h, rows, :] = jnp.zeros((rc, LANES), F32)
                    continue
                nz = z_buf[slot, h, rows, :]
                l1 = jnp.minimum(nz, 0.0) - jnp.log(1.0 + jnp.exp(_neg_abs(nz)))
                if _some_masked(c, diag, rc, tk):
                    l1 = jnp.where(causal(c, diag), l1, 0.0)
                for s, part in enumerate(_split(l1)[:LOG_SUM_PASSES]):
                    hl_buf[h, rows, s * tk:(s + 1) * tk] = part
                ls_buf[h, rows, :] = l1 - nz
                tot_buf[h, rows, :] = _row_sums(l1)

        def weights_stage(h, slot, diag):
            for c in range(tq // rc):
                rows = slice(c * rc, (c + 1) * rc)
                if _all_masked(c, diag, rc, tk):
                    w_buf[slot, h, rows, :] = jnp.zeros((rc, tk), BF16)
                    continue
                wgt = jnp.exp((ls_buf[h, rows, :] + aft_buf[h, rows, :]) + _across(run_buf[h, rows, :], tk))
                if _some_masked(c, diag, rc, tk):
                    wgt = jnp.where(causal(c, diag), wgt, 0.0)
                w_buf[slot, h, rows, :] = wgt.astype(BF16)
                run_buf[h, rows, :] += tot_buf[h, rows, :]

        def position(blk, slot, diag):
            scores(jnp.maximum(blk - 1, 0), 1 - slot)
            for h in heads:
                softplus_stage(h, slot, diag)
                aft_buf[h] = _dot(hl_buf[h], tri_ref[...])
            values(jnp.minimum(blk + 1, nblk - 1), 1 - slot)
            for h in heads:
                weights_stage(h, slot, diag)

        scores(nblk - 1, 0)
        for p in range(ndiag):
            position(nblk - 1 - p, p % 2, ndiag - 1 - p)

        def trip(jj, carry):
            for u in range(2):
                position(i * ndiag - 1 - 2 * jj - u, u, None)
            return carry

        lax.fori_loop(0, (i * ndiag) // 2, trip, 0)
        values(0, 1)
        o_ref[...] = acc_t[...].T.astype(BF16)
        l_ref[...] = jnp.where(first, run_buf[0], run_buf[1])

    qmap = lambda b, hp, i: (b * nq + i, hp)
    nh = HEADS_PER_BLOCK
    return pl.pallas_call(
        body, name="attn_fwd", grid=(nb, w // LANES, nq),
        in_specs=[pl.BlockSpec((tq, LANES), qmap), pl.BlockSpec((seq, LANES), lambda b, hp, i: (b, hp)),
                  pl.BlockSpec((LANES, seq), lambda b, hp, i: (hp, b)),
                  pl.BlockSpec((LOG_SUM_PASSES * tk, tk), lambda b, hp, i: (0, 0))],
        out_specs=[pl.BlockSpec((tq, LANES), qmap), pl.BlockSpec((tq, LANES), qmap)],
        out_shape=[jax.ShapeDtypeStruct((t_all, w), BF16), jax.ShapeDtypeStruct((t_all, w), F32)],
        scratch_shapes=[pltpu.VMEM((2, nh, tq, tk), F32), pltpu.VMEM((nh, tq, tk), F32),
                        pltpu.VMEM((nh, tq, LOG_SUM_PASSES * tk), BF16), pltpu.VMEM((nh, tq, tk), F32),
                        pltpu.VMEM((2, nh, tq, tk), BF16), pltpu.VMEM((nh, tq, LANES), F32),
                        pltpu.VMEM((LANES, tq), F32), pltpu.VMEM((nh, tq, LANES), F32)],
        compiler_params=_params(("arbitrary", "arbitrary", "arbitrary")),
    )(qn, k, vt, _tri_matrix(tk, "after")[:LOG_SUM_PASSES * tk])


def _window_sums(ext, rows, offset, forward):
    r = lax.broadcasted_iota(jnp.int32, (rows, rows + HALO), 0)
    e = lax.broadcasted_iota(jnp.int32, (rows, rows + HALO), 1)
    hi, lo = _split(ext)
    out = []
    for g, win in enumerate(POOL_WINDOWS):
        if forward:
            band = (e >= r) & (e < r + win)
        else:
            band = (e <= r + offset) & (e > r + offset - win)
        bm = band.astype(BF16)
        cols = slice(g * POOL_GROUP, (g + 1) * POOL_GROUP)
        out.append(_dot(bm, hi[:, cols]) + _dot(bm, lo[:, cols]))
    return out


def _window_counts(pos):
    return [jnp.minimum(pos + 1, win).astype(F32) for win in POOL_WINDOWS]


def _mixer_post(u, o, x, mod, g_post, g_fpre, w_pool, pool_scale, w_out, seq, tm):
    t_all, d = x.shape
    nt = seq // tm
    p = u.shape[1]

    def body(u_ref, halo_ref, o_ref, x_ref, mod_ref, gp_ref, gf_ref, wp_ref, ps_ref, wo_ref,
             pooled_ref, mixin_ref, mix_ref, x1_ref, h2_ref):
        it = pl.program_id(0) % nt
        uf = u_ref[...]
        halo = jnp.where(it == 0, 0.0, halo_ref[...])
        ext = jnp.concatenate([halo, uf], axis=0)
        pos = it * tm + lax.broadcasted_iota(jnp.int32, (tm, 1), 0)
        sums = _window_sums(ext, tm, HALO, False)
        cnts = _window_counts(pos)
        pools = []
        for g in range(len(POOL_WINDOWS)):
            cols = slice(g * POOL_GROUP, (g + 1) * POOL_GROUP)
            pooled = (sums[g] / cnts[g] - uf[:, cols]).astype(BF16)
            pooled_ref[:, cols] = pooled
            yg = _dot(pooled, wp_ref[g].astype(BF16))
            pools.append((yg * ps_ref[:, cols]).astype(BF16))
        mixin_ref[...] = jnp.concatenate([o_ref[...]] + pools, axis=1)
        for c in range(ROW_CHUNKS):
            rows = slice(c * (tm // ROW_CHUNKS), (c + 1) * (tm // ROW_CHUNKS))
            mix = _dot(mixin_ref[rows, :], wo_ref[...])
            mix_ref[rows, :] = mix
            n2 = mix * _rms(mix)
            x1 = x_ref[rows, :] + mod_ref[0, 2:3, :] * (n2 * gp_ref[...])
            x1_ref[rows, :] = x1
            n3 = x1 * _rms(x1)
            h2 = (n3 * gf_ref[...]) * (1.0 + mod_ref[0, 4:5, :]) + mod_ref[0, 3:4, :]
            h2_ref[rows, :] = h2.astype(BF16)

    tok = lambda i: (i, 0)
    const2 = lambda i: (0, 0)
    hb = tm // HALO
    return pl.pallas_call(
        body, name="mixer_post", grid=(t_all // tm,),
        in_specs=[pl.BlockSpec((tm, p), tok),
                  pl.BlockSpec((HALO, p), lambda i: (jnp.maximum(i * hb - 1, 0), 0)),
                  pl.BlockSpec((tm, p), tok),
                  pl.BlockSpec((tm, d), tok),
                  pl.BlockSpec((1, MOD_ROWS, d), lambda i: (i // nt, 0, 0)),
                  pl.BlockSpec((1, d), const2), pl.BlockSpec((1, d), const2),
                  pl.BlockSpec(w_pool.shape, lambda i: (0, 0, 0)),
                  pl.BlockSpec((1, p), const2),
                  pl.BlockSpec((d, d), const2)],
        out_specs=[pl.BlockSpec((tm, p), tok), pl.BlockSpec((tm, d), tok), pl.BlockSpec((tm, d), tok),
                   pl.BlockSpec((tm, d), tok), pl.BlockSpec((tm, d), tok)],
        out_shape=[jax.ShapeDtypeStruct((t_all, p), BF16), jax.ShapeDtypeStruct((t_all, d), BF16),
                   jax.ShapeDtypeStruct((t_all, d), F32), jax.ShapeDtypeStruct((t_all, d), F32),
                   jax.ShapeDtypeStruct((t_all, d), BF16)],
        compiler_params=_params(("arbitrary",)),
    )(u, u, o, x, mod, g_post, g_fpre, w_pool, pool_scale, w_out)


def _ffn_fwd(h2, w_g, w_u, w_d, x1, tgt, mod, g_post, seq, tm):
    t_all, d = x1.shape
    nt = seq // tm
    nk, ff, _ = w_g.shape

    def body(h_ref, wg_ref, wu_ref, wd_ref, x1_ref, t_ref, mod_ref, g_ref,
             a_ref, b_ref, fin_ref, dy_ref, df_ref, loss_ref, accb_ref, accg_ref, facc):
        i, k = pl.program_id(0), pl.program_id(1)

        @pl.when(k == 0)
        def _():
            facc[...] = jnp.zeros_like(facc)

        for c in range(FFN_ROW_CHUNKS):
            rows = slice(c * (tm // FFN_ROW_CHUNKS), (c + 1) * (tm // FFN_ROW_CHUNKS))
            hb = h_ref[rows, :]
            a = _dot_nt(hb, wg_ref[0])
            b = _dot_nt(hb, wu_ref[0])
            a_ref[0, rows, :] = a.astype(BF16)
            b_ref[0, rows, :] = b.astype(BF16)
            fin = ((a * _sigmoid(a)) * b).astype(BF16)
            fin_ref[0, rows, :] = fin
            facc[rows, :] += _dot(fin, wd_ref[0])

        @pl.when(k == nk - 1)
        def _():
            f = facc[...]
            r4 = _rms(f)
            n4 = f * r4
            gate = mod_ref[0, 5:6, :]
            g = g_ref[...]
            err = (x1_ref[...] + gate * (n4 * g)) - t_ref[...]
            dy = err * (1.0 / d)
            dy_ref[...] = dy

            @pl.when(i == 0)
            def _():
                loss_ref[...] = jnp.zeros_like(loss_ref)
                accg_ref[...] = jnp.zeros_like(accg_ref)

            @pl.when(i % nt == 0)
            def _():
                accb_ref[...] = jnp.zeros_like(accb_ref)

            loss_ref[...] += (0.5 / d) * jnp.sum(err * err)
            accb_ref[0, 0:1, :] += _colsum(dy * (n4 * g))
            accg_ref[0:1, :] += _colsum((dy * gate) * n4)
            dn4 = (dy * gate) * g
            df_ref[...] = _norm_bwd(dn4, n4, r4).astype(BF16)

    tok = lambda i, k: (i, 0)
    ktok = lambda i, k: (k, i, 0)
    kw = lambda i, k: (k, 0, 0)
    const2 = lambda i, k: (0, 0)
    return pl.pallas_call(
        body, name="ffn_fwd", grid=(t_all // tm, nk),
        in_specs=[pl.BlockSpec((tm, d), tok),
                  pl.BlockSpec((1, ff, d), kw), pl.BlockSpec((1, ff, d), kw), pl.BlockSpec((1, ff, d), kw),
                  pl.BlockSpec((tm, d), tok), pl.BlockSpec((tm, d), tok),
                  pl.BlockSpec((1, MOD_ROWS, d), lambda i, k: (i // nt, 0, 0)),
                  pl.BlockSpec((1, d), const2)],
        out_specs=[pl.BlockSpec((1, tm, ff), ktok)] * 3
        + [pl.BlockSpec((tm, d), tok), pl.BlockSpec((tm, d), tok),
           pl.BlockSpec((8, LANES), const2),
           pl.BlockSpec((1, 8, d), lambda i, k: (i // nt, 0, 0)),
           pl.BlockSpec((8, d), const2)],
        out_shape=[jax.ShapeDtypeStruct((nk, t_all, ff), BF16)] * 3
        + [jax.ShapeDtypeStruct((t_all, d), F32), jax.ShapeDtypeStruct((t_all, d), BF16),
           jax.ShapeDtypeStruct((8, LANES), F32),
           jax.ShapeDtypeStruct((t_all // seq, 8, d), F32),
           jax.ShapeDtypeStruct((8, d), F32)],
        scratch_shapes=[pltpu.VMEM((tm, d), F32)],
        compiler_params=_params(("arbitrary", "arbitrary")),
    )(h2, w_g, w_u, w_d, x1, tgt, mod, g_post)


def _ffn_bwd(df, a, b, w_d, w_g, w_u, x1, dy, mix, mod, g_fpre, g_mpost, seq, tm):
    t_all, d = x1.shape
    nt = seq // tm
    nk, ff, _ = w_g.shape

    def body(df_ref, a_ref, b_ref, wd_ref, wg_ref, wu_ref, x1_ref, dy_ref, mix_ref, mod_ref, gf_ref, gm_ref,
             da_ref, db_ref, dx1_ref, dmix_ref, accb_ref, accg_ref, hacc):
        i, k = pl.program_id(0), pl.program_id(1)

        @pl.when(k == 0)
        def _():
            hacc[...] = jnp.zeros_like(hacc)

        for c in range(FFN_ROW_CHUNKS):
            rows = slice(c * (tm // FFN_ROW_CHUNKS), (c + 1) * (tm // FFN_ROW_CHUNKS))
            dfin = _dot_nt(df_ref[rows, :], wd_ref[0])
            af = a_ref[0, rows, :].astype(F32)
            bf = b_ref[0, rows, :].astype(F32)
            sig = _sigmoid(af)
            da = ((dfin * bf) * (sig * (1.0 + af * (1.0 - sig)))).astype(BF16)
            db = (dfin * (af * sig)).astype(BF16)
            da_ref[0, rows, :] = da
            db_ref[0, rows, :] = db
            hacc[rows, :] += _dot(da, wg_ref[0]) + _dot(db, wu_ref[0])

        @pl.when(k == nk - 1)
        def _():
            @pl.when(i == 0)
            def _():
                accg_ref[...] = jnp.zeros_like(accg_ref)

            @pl.when(i % nt == 0)
            def _():
                accb_ref[...] = jnp.zeros_like(accb_ref)

            dh2 = hacc[...]
            x1 = x1_ref[...]
            r3 = _rms(x1)
            n3 = x1 * r3
            g3 = gf_ref[...]
            scale1 = 1.0 + mod_ref[0, 4:5, :]
            accb_ref[0, 0:1, :] += _colsum(dh2)
            accb_ref[0, 1:2, :] += _colsum(dh2 * (n3 * g3))
            accg_ref[0:1, :] += _colsum((dh2 * scale1) * n3)
            dx1 = dy_ref[...] + _norm_bwd((dh2 * scale1) * g3, n3, r3)
            dx1_ref[...] = dx1
            mix = mix_ref[...]
            r2 = _rms(mix)
            n2 = mix * r2
            g2 = gm_ref[...]
            gate = mod_ref[0, 2:3, :]
            accb_ref[0, 2:3, :] += _colsum(dx1 * (n2 * g2))
            accg_ref[1:2, :] += _colsum((dx1 * gate) * n2)
            dmix_ref[...] = _norm_bwd((dx1 * gate) * g2, n2, r2).astype(BF16)

    tok = lambda i, k: (i, 0)
    ktok = lambda i, k: (k, i, 0)
    kw = lambda i, k: (k, 0, 0)
    const2 = lambda i, k: (0, 0)
    return pl.pallas_call(
        body, name="ffn_bwd", grid=(t_all // tm, nk),
        in_specs=[pl.BlockSpec((tm, d), tok),
                  pl.BlockSpec((1, tm, ff), ktok), pl.BlockSpec((1, tm, ff), ktok),
                  pl.BlockSpec((1, ff, d), kw), pl.BlockSpec((1, ff, d), kw), pl.BlockSpec((1, ff, d), kw),
                  pl.BlockSpec((tm, d), tok), pl.BlockSpec((tm, d), tok), pl.BlockSpec((tm, d), tok),
                  pl.BlockSpec((1, MOD_ROWS, d), lambda i, k: (i // nt, 0, 0)),
                  pl.BlockSpec((1, d), const2), pl.BlockSpec((1, d), const2)],
        out_specs=[pl.BlockSpec((1, tm, ff), ktok)] * 2
        + [pl.BlockSpec((tm, d), tok), pl.BlockSpec((tm, d), tok),
           pl.BlockSpec((1, 8, d), lambda i, k: (i // nt, 0, 0)),
           pl.BlockSpec((8, d), const2)],
        out_shape=[jax.ShapeDtypeStruct((nk, t_all, ff), BF16)] * 2
        + [jax.ShapeDtypeStruct((t_all, d), F32), jax.ShapeDtypeStruct((t_all, d), BF16),
           jax.ShapeDtypeStruct((t_all // seq, 8, d), F32),
           jax.ShapeDtypeStruct((8, d), F32)],
        scratch_shapes=[pltpu.VMEM((tm, d), F32)],
        compiler_params=_params(("arbitrary", "arbitrary")),
    )(df, a, b, w_d, w_g, w_u, x1, dy, mix, mod, g_fpre, g_mpost)


def _mixer_bwd(dmix, w_out, pooled, w_pool, pool_scale, seq, tm):
    t_all, d = dmix.shape
    p = pooled.shape[1]
    ng = len(POOL_WINDOWS)

    def body(dm_ref, wo_ref, pooled_ref, wp_ref, ps_ref, do_ref, dpd_ref, dps_ref, dwp_ref):
        i = pl.program_id(0)

        @pl.when(i == 0)
        def _():
            dps_ref[...] = jnp.zeros_like(dps_ref)
            dwp_ref[...] = jnp.zeros_like(dwp_ref)

        dmixin = _dot_nt(dm_ref[...], wo_ref[...])
        do_ref[...] = dmixin[:, :p].astype(BF16)
        for g in range(ng):
            cols = slice(g * POOL_GROUP, (g + 1) * POOL_GROUP)
            dpool = dmixin[:, p + g * POOL_GROUP:p + (g + 1) * POOL_GROUP]
            pooled = pooled_ref[:, cols]
            wpg = wp_ref[g].astype(BF16)
            yg = _dot(pooled, wpg)
            dps_ref[0:1, cols] += _colsum(dpool * yg)
            dyg = (dpool * ps_ref[:, cols]).astype(BF16)
            dwp_ref[g] += _dot_tn(pooled, dyg)
            dpd_ref[:, cols] = _dot_nt(dyg, wpg)

    tok = lambda i: (i, 0)
    const2 = lambda i: (0, 0)
    const3 = lambda i: (0, 0, 0)
    return pl.pallas_call(
        body, name="mixer_bwd", grid=(t_all // tm,),
        in_specs=[pl.BlockSpec((tm, d), tok), pl.BlockSpec((d, d), const2), pl.BlockSpec((tm, p), tok),
                  pl.BlockSpec(w_pool.shape, const3), pl.BlockSpec((1, p), const2)],
        out_specs=[pl.BlockSpec((tm, p), tok), pl.BlockSpec((tm, p), tok),
                   pl.BlockSpec((8, p), const2), pl.BlockSpec(w_pool.shape, const3)],
        out_shape=[jax.ShapeDtypeStruct((t_all, p), BF16), jax.ShapeDtypeStruct((t_all, p), F32),
                   jax.ShapeDtypeStruct((8, p), F32), jax.ShapeDtypeStruct(w_pool.shape, F32)],
        compiler_params=_params(("arbitrary",)),
    )(dmix, w_out, pooled, w_pool, pool_scale)


def _attn_bwd(qn, k, kt, v, do, ltot, seq, tq, tk, order):
    t_all, w = qn.shape
    nb, nq, ndiag, nkb = t_all // seq, seq // tq, tq // tk, seq // tk
    assert ndiag % 2 == 0, "two key blocks per loop trip"
    rc = ATTN_ROW_CHUNK
    nh = HEADS_PER_BLOCK
    heads = range(nh)

    def body(q_ref, k_ref, kt_ref, v_ref, do_ref, l_ref, up_ref, bf_ref, dq_ref, dk_ref, dv_ref,
             z_buf, dw_buf, ls_buf, hl_buf, upto_buf, g_buf, gb_buf, before_buf, w_buf, dz_buf,
             totl_buf, totg_buf, rem_buf, preg_buf, qnt_buf, dot_buf, dq_t, dk_t, dv_t):
        i = pl.program_id(2)
        nblk = (i + 1) * ndiag

        @pl.when(i == 0)
        def _():
            dk_t[...] = jnp.zeros_like(dk_t)
            dv_t[...] = jnp.zeros_like(dv_t)

        lane = lax.broadcasted_iota(jnp.int32, (1, LANES), 1)
        row = lax.broadcasted_iota(jnp.int32, (rc, tk), 0)
        col = lax.broadcasted_iota(jnp.int32, (rc, tk), 1)
        first = lane < HEAD_DIM
        q2 = q_ref[...]
        do2 = do_ref[...]
        l2 = l_ref[...]
        qs = [jnp.where(first, q2, jnp.zeros_like(q2)), jnp.where(first, jnp.zeros_like(q2), q2)]
        dos = [jnp.where(first, do2, jnp.zeros_like(do2)), jnp.where(first, jnp.zeros_like(do2), do2)]
        qnt_buf[...] = q2.astype(F32).T.astype(BF16)
        dot_buf[...] = do2.astype(F32).T.astype(BF16)
        for h in heads:
            rem_buf[h] = jnp.where(first if h == 0 else ~first, l2, pltpu.roll(l2, HEAD_DIM, 1))
        preg_buf[...] = jnp.zeros_like(preg_buf)
        dq_t[...] = jnp.zeros_like(dq_t)
        w_buf[1] = jnp.zeros((nh * tq, tk), BF16)
        dz_buf[1] = jnp.zeros((nh * tq, tk), BF16)

        def causal(c, diag):
            return (col + diag * tk) < (row + c * rc)

        def scores(blk, slot):
            off = pl.multiple_of(blk * tk, tk)
            kj = k_ref[pl.ds(off, tk), :]
            vj = v_ref[pl.ds(off, tk), :]
            for h in heads:
                z_buf[slot, h] = _dot_nt(qs[h], kj)
                dw_buf[slot, h] = _dot_nt(dos[h], vj)

        def gradients(blk, slot):
            keys = pl.ds(pl.multiple_of(blk * tk, tk), tk)
            for h in heads:
                dims = slice(h * HEAD_DIM, (h + 1) * HEAD_DIM)
                queries = slice(h * tq, (h + 1) * tq)
                dq_t[dims, :] += _dot_nt(kt_ref[dims, keys], dz_buf[slot, queries, :])
                dk_t[blk, dims, :] += _dot(qnt_buf[dims, :], dz_buf[slot, queries, :])
                dv_t[blk, dims, :] += _dot(dot_buf[dims, :], w_buf[slot, queries, :])

        def softplus_stage(h, slot, diag):
            for c in range(tq // rc):
                rows = slice(c * rc, (c + 1) * rc)
                if _all_masked(c, diag, rc, tk):
                    hl_buf[h, rows, :] = jnp.zeros((rc, LOG_SUM_PASSES * tk), BF16)
                    continue
                nz = z_buf[slot, h, rows, :]
                l1 = jnp.minimum(nz, 0.0) - jnp.log(1.0 + jnp.exp(_neg_abs(nz)))
                if _some_masked(c, diag, rc, tk):
                    l1 = jnp.where(causal(c, diag), l1, 0.0)
                for s, part in enumerate(_split(l1)[:LOG_SUM_PASSES]):
                    hl_buf[h, rows, s * tk:(s + 1) * tk] = part
                ls_buf[h, rows, :] = l1 - nz
                totl_buf[h, rows, :] = _row_sums(l1)

        def weights_stage(h, slot, diag):
            for c in range(tq // rc):
                rows = slice(c * rc, (c + 1) * rc)
                stacked = slice(h * tq + c * rc, h * tq + (c + 1) * rc)
                if _all_masked(c, diag, rc, tk):
                    w_buf[slot, stacked, :] = jnp.zeros((rc, tk), BF16)
                    gb_buf[h, rows, :] = jnp.zeros((rc, tk), BF16)
                    continue
                wgt = jnp.exp(ls_buf[h, rows, :] + (_across(rem_buf[h, rows, :], tk) - upto_buf[h, rows, :]))
                if _some_masked(c, diag, rc, tk):
                    wgt = jnp.where(causal(c, diag), wgt, 0.0)
                w_buf[slot, stacked, :] = wgt.astype(BF16)
                g = wgt * dw_buf[slot, h, rows, :]
                g_buf[h, rows, :] = g
                gb_buf[h, rows, :] = g.astype(BF16)
                totg_buf[h, rows, :] = _row_sums(g)
                rem_buf[h, rows, :] -= totl_buf[h, rows, :]

        def dscore_stage(h, slot, diag):
            for c in range(tq // rc):
                rows = slice(c * rc, (c + 1) * rc)
                stacked = slice(h * tq + c * rc, h * tq + (c + 1) * rc)
                if _all_masked(c, diag, rc, tk):
                    dz_buf[slot, stacked, :] = jnp.zeros((rc, tk), BF16)
                    continue
                sig = jnp.exp(ls_buf[h, rows, :])
                g = g_buf[h, rows, :]
                dnz = sig * ((before_buf[h, rows, :] + _across(preg_buf[h, rows, :], tk)) + g) - g
                if _some_masked(c, diag, rc, tk):
                    dnz = jnp.where(causal(c, diag), dnz, 0.0)
                dz_buf[slot, stacked, :] = dnz.astype(BF16)
                preg_buf[h, rows, :] += totg_buf[h, rows, :]

        def position(blk, slot, diag, prefetch):
            if prefetch:
                scores(blk + 1, 1 - slot)
            for h in heads:
                softplus_stage(h, slot, diag)
                upto_buf[h] = _dot(hl_buf[h], up_ref[...])
            gradients(jnp.maximum(blk - 1, 0), 1 - slot)
            for h in heads:
                weights_stage(h, slot, diag)
                before_buf[h] = _dot(gb_buf[h], bf_ref[...])
            for h in heads:
                dscore_stage(h, slot, diag)

        scores(0, 0)

        def trip(jj, carry):
            for u in range(2):
                position(2 * jj + u, u, None, True)
            return carry

        lax.fori_loop(0, (i * ndiag) // 2, trip, 0)
        for d in range(ndiag):
            position(i * ndiag + d, d % 2, d, d < ndiag - 1)
        gradients(nblk - 1, 1)
        dq_ref[...] = (dq_t[...].T * NEG_QK_SCALE).astype(BF16)

        @pl.when(i == nq - 1)
        def _():
            for blk in range(nkb):
                dk_ref[blk * tk:(blk + 1) * tk, :] = dk_t[blk].T.astype(BF16)
                dv_ref[blk * tk:(blk + 1) * tk, :] = dv_t[blk].T.astype(BF16)

    qmap = lambda b, hp, i: (b * nq + i, hp)
    kmap = lambda b, hp, i: (b, hp)
    const = lambda b, hp, i: (0, 0)
    return pl.pallas_call(
        body, name="attn_bwd", grid=(nb, w // LANES, nq),
        in_specs=[pl.BlockSpec((tq, LANES), qmap), pl.BlockSpec((seq, LANES), kmap),
                  pl.BlockSpec((LANES, seq), lambda b, hp, i: (hp, b)), pl.BlockSpec((seq, LANES), kmap),
                  pl.BlockSpec((tq, LANES), qmap), pl.BlockSpec((tq, LANES), qmap),
                  pl.BlockSpec((LOG_SUM_PASSES * tk, tk), const), pl.BlockSpec((tk, tk), const)],
        out_specs=[pl.BlockSpec((tq, LANES), qmap), pl.BlockSpec((seq, LANES), kmap), pl.BlockSpec((seq, LANES), kmap)],
        out_shape=[jax.ShapeDtypeStruct((t_all, w), BF16)] * 3,
        scratch_shapes=[pltpu.VMEM((2, nh, tq, tk), F32), pltpu.VMEM((2, nh, tq, tk), F32),
                        pltpu.VMEM((nh, tq, tk), F32), pltpu.VMEM((nh, tq, LOG_SUM_PASSES * tk), BF16),
                        pltpu.VMEM((nh, tq, tk), F32), pltpu.VMEM((nh, tq, tk), F32),
                        pltpu.VMEM((nh, tq, tk), BF16), pltpu.VMEM((nh, tq, tk), F32),
                        pltpu.VMEM((2, nh * tq, tk), BF16), pltpu.VMEM((2, nh * tq, tk), BF16),
                        pltpu.VMEM((nh, tq, LANES), F32), pltpu.VMEM((nh, tq, LANES), F32),
                        pltpu.VMEM((nh, tq, LANES), F32), pltpu.VMEM((nh, tq, LANES), F32),
                        pltpu.VMEM((LANES, tq), BF16), pltpu.VMEM((LANES, tq), BF16),
                        pltpu.VMEM((LANES, tq), F32), pltpu.VMEM((nkb, LANES, tk), F32),
                        pltpu.VMEM((nkb, LANES, tk), F32)],
        compiler_params=_params(("arbitrary", "arbitrary", "arbitrary")),
    )(qn, k, kt, v, do, ltot, _tri_matrix(tk, "upto")[:LOG_SUM_PASSES * tk] + order.astype(BF16),
      _tri_matrix(tk, "before")[:tk])


def _inproj_bwd(dq, dk, dv, dpd, x, dx1, mod, g_pre, w_in, seq, tm):
    t_all, d = x.shape
    nt = seq // tm
    p = dq.shape[1]

    def body(dq_ref, dk_ref, dv_ref, dpd_ref, halo_ref, x_ref, dx1_ref, mod_ref, g_ref, w_ref,
             gx_ref, du_ref, accb_ref, accg_ref):
        i = pl.program_id(0)
        it = i % nt

        @pl.when(i == 0)
        def _():
            accg_ref[...] = jnp.zeros_like(accg_ref)

        @pl.when(it == 0)
        def _():
            accb_ref[...] = jnp.zeros_like(accb_ref)

        dpd = dpd_ref[...]
        pos = it * tm + lax.broadcasted_iota(jnp.int32, (tm, 1), 0)
        cnts = _window_counts(pos)
        halo = jnp.where(it == nt - 1, 0.0, halo_ref[...])
        scaled = []
        halos = []
        for g, win in enumerate(POOL_WINDOWS):
            cols = slice(g * POOL_GROUP, (g + 1) * POOL_GROUP)
            scaled.append(dpd[:, cols] / cnts[g])
            halos.append(halo[:, cols] / float(win))
        ext = jnp.concatenate([jnp.concatenate(scaled, axis=1), jnp.concatenate(halos, axis=1)], axis=0)
        sums = _window_sums(ext, tm, 0, True)
        du = (jnp.concatenate(sums, axis=1) - dpd).astype(BF16)
        du_ref[...] = du
        g1 = g_ref[...]
        scale1 = 1.0 + mod_ref[0, 1:2, :]
        for c in range(ROW_CHUNKS):
            rows = slice(c * (tm // ROW_CHUNKS), (c + 1) * (tm // ROW_CHUNKS))
            dh1 = (_dot_nt(dq_ref[rows, :], w_ref[0]) + _dot_nt(dk_ref[rows, :], w_ref[1])
                   + _dot_nt(dv_ref[rows, :], w_ref[2]) + _dot_nt(du_ref[rows, :], w_ref[3]))
            xf = x_ref[rows, :]
            r1 = _rms(xf)
            n1 = xf * r1
            accb_ref[0, 0:1, :] += _colsum(dh1)
            accb_ref[0, 1:2, :] += _colsum(dh1 * (n1 * g1))
            accg_ref[0:1, :] += _colsum((dh1 * scale1) * n1)
            gx_ref[rows, :] = dx1_ref[rows, :] + _norm_bwd((dh1 * scale1) * g1, n1, r1)

    tok = lambda i: (i, 0)
    const2 = lambda i: (0, 0)
    hb = tm // HALO
    last = t_all // HALO - 1
    return pl.pallas_call(
        body, name="inproj_bwd", grid=(t_all // tm,),
        in_specs=[pl.BlockSpec((tm, p), tok), pl.BlockSpec((tm, p), tok), pl.BlockSpec((tm, p), tok),
                  pl.BlockSpec((tm, p), tok),
                  pl.BlockSpec((HALO, p), lambda i: (jnp.minimum((i + 1) * hb, last), 0)),
                  pl.BlockSpec((tm, d), tok), pl.BlockSpec((tm, d), tok),
                  pl.BlockSpec((1, MOD_ROWS, d), lambda i: (i // nt, 0, 0)),
                  pl.BlockSpec((1, d), const2),
                  pl.BlockSpec((N_CHIPS, d, p), lambda i: (0, 0, 0))],
        out_specs=[pl.BlockSpec((tm, d), tok), pl.BlockSpec((tm, p), tok),
                   pl.BlockSpec((1, 8, d), lambda i: (i // nt, 0, 0)),
                   pl.BlockSpec((8, d), const2)],
        out_shape=[jax.ShapeDtypeStruct((t_all, d), F32), jax.ShapeDtypeStruct((t_all, p), BF16),
                   jax.ShapeDtypeStruct((t_all // seq, 8, d), F32),
                   jax.ShapeDtypeStruct((8, d), F32)],
        compiler_params=_params(("arbitrary",)),
    )(dq, dk, dv, dpd, dpd, x, dx1, mod, g_pre, w_in)


def _tn_matmul(x, ys, nk, bt, name, after=()):
    t_all = x.shape[-2]
    m = x.shape[-1]
    ny = len(ys)
    nt = t_all // bt

    def spec(arr):
        if arr.ndim == 3:
            return pl.BlockSpec((1, bt, arr.shape[-1]), lambda k, t: (k, t, 0))
        return pl.BlockSpec((bt, arr.shape[-1]), lambda k, t: (t, 0))

    def tile(ref):
        return ref[0] if len(ref.shape) == 3 else ref[...]

    def body(*refs):
        outs = refs[1 + ny + len(after):]
        x_ref, y_refs, o_refs, h_refs = refs[0], refs[1:1 + ny], outs[:ny], outs[ny:]
        t = pl.program_id(1)
        xt = tile(x_ref)
        for y_ref, o_ref, h_ref in zip(y_refs, o_refs, h_refs):
            part = _dot_tn(xt, tile(y_ref))

            @pl.when(t == 0)
            def _(o_ref=o_ref, part=part):
                o_ref[0] = part

            @pl.when(t > 0)
            def _(o_ref=o_ref, part=part):
                o_ref[0] += part

            @pl.when(t == nt - 1)
            def _(o_ref=o_ref, h_ref=h_ref):
                h_ref[0] = o_ref[0].astype(BF16)

    out_specs = [pl.BlockSpec((1, m, y.shape[-1]), lambda k, t: (k, 0, 0)) for y in ys]
    out = pl.pallas_call(
        body, name=name, grid=(nk, nt),
        in_specs=[spec(x)] + [spec(y) for y in ys] + [_ANY] * len(after),
        out_specs=out_specs * 2,
        out_shape=[jax.ShapeDtypeStruct((nk, m, y.shape[-1]), dt) for dt in (F32, BF16) for y in ys],
        compiler_params=_params(("arbitrary", "arbitrary")),
    )(x, *ys, *after)
    return out[:ny], out[ny:]


def _tn_matmul_stacked(x, ys, bt, name, after=()):
    t_all, m = x.shape
    n = ys[0].shape[1]
    ny = len(ys)
    nt = t_all // bt

    def body(*refs):
        x_ref, y_refs, (o_ref, h_ref) = refs[0], refs[1:1 + ny], refs[1 + ny + len(after):]
        t = pl.program_id(0)
        xt = x_ref[...]

        @pl.when(t == 0)
        def _():
            o_ref[...] = jnp.zeros_like(o_ref)

        for j, y_ref in enumerate(y_refs):
            o_ref[j] += _dot_tn(xt, y_ref[...])

        @pl.when(t == nt - 1)
        def _():
            h_ref[...] = o_ref[...].astype(BF16)

    whole = pl.BlockSpec((ny, m, n), lambda t: (0, 0, 0))
    return pl.pallas_call(
        body, name=name, grid=(nt,),
        in_specs=[pl.BlockSpec((bt, m), lambda t: (t, 0))] + [pl.BlockSpec((bt, n), lambda t: (t, 0))] * ny
        + [_ANY] * len(after),
        out_specs=[whole, whole],
        out_shape=[jax.ShapeDtypeStruct((ny, m, n), F32), jax.ShapeDtypeStruct((ny, m, n), BF16)],
        compiler_params=_params(("arbitrary",)),
    )(x, *ys, *after)


def _cond_fwd(c_all, w_q, b_q, bn):
    nrow, d = c_all.shape
    ncol = w_q.shape[1]

    def body(c_ref, w_ref, b_ref, sc_ref, mod_ref):
        cf = c_ref[...]
        sc = cf * _sigmoid(cf)
        sc_ref[...] = sc
        shi, slo = _split(sc)
        whi, wlo = _split(w_ref[...])
        mod_ref[...] = (_dot(shi, whi) + _dot(shi, wlo) + _dot(slo, whi)) + b_ref[...]

    return pl.pallas_call(
        body, name="cond_fwd", grid=(ncol // bn,),
        in_specs=[pl.BlockSpec((nrow, d), lambda n: (0, 0)), pl.BlockSpec((d, bn), lambda n: (0, n)),
                  pl.BlockSpec((1, bn), lambda n: (0, n))],
        out_specs=[pl.BlockSpec((nrow, d), lambda n: (0, 0)), pl.BlockSpec((nrow, bn), lambda n: (0, n))],
        out_shape=[jax.ShapeDtypeStruct((nrow, d), F32), jax.ShapeDtypeStruct((nrow, ncol), F32)],
        compiler_params=_params(("arbitrary",)),
    )(c_all, w_q, b_q)


def _cond_bwd(sc_all, dmod_q, bn):
    nrow, d = sc_all.shape
    ncol = dmod_q.shape[1]

    def body(sc_ref, dm_ref, gw_ref):
        shi, slo = _split(sc_ref[...])
        dhi, dlo = _split(dm_ref[...])
        gw_ref[...] = _dot_tn(shi, dhi) + _dot_tn(shi, dlo) + _dot_tn(slo, dhi)

    return pl.pallas_call(
        body, name="cond_bwd", grid=(ncol // bn,),
        in_specs=[pl.BlockSpec((nrow, d), lambda n: (0, 0)), pl.BlockSpec((nrow, bn), lambda n: (0, n))],
        out_specs=pl.BlockSpec((d, bn), lambda n: (0, n)),
        out_shape=jax.ShapeDtypeStruct((d, ncol), F32),
        compiler_params=_params(("arbitrary",)),
    )(sc_all, dmod_q)


def _row_block(rows, cols, budget=1 << 18):
    best = None
    for br in range(8, rows + 1, 8):
        if rows % br == 0 and br * cols <= budget:
            best = br
    return best if best is not None else rows


def _adam_math(w, g, m, v):
    c1 = 1.0 - ADAM_B1 ** ADAM_STEP
    c2 = 1.0 - ADAM_B2 ** ADAM_STEP
    m2 = ADAM_B1 * m + (1.0 - ADAM_B1) * g
    v2 = ADAM_B2 * v + (1.0 - ADAM_B2) * (g * g)
    return -ADAM_LR * ((m2 / c1) / (jnp.sqrt(v2 / c2) + ADAM_EPS) + ADAM_WD * w), m2, v2


def _small_updates(summed, params):
    n = len(params)

    def body(s_ref, *refs):
        ins, outs = refs[:3 * n], refs[3 * n:]
        for p, (_, _, _, pick) in enumerate(params):
            w_ref, m_ref, v_ref = ins[3 * p:3 * p + 3]
            g = pick(s_ref)
            delta, m2, v2 = _adam_math(w_ref[...], g, m_ref[...], v_ref[...])
            for o_ref, val in zip(outs[4 * p:4 * p + 4], (g, delta, m2, v2)):
                o_ref[...] = val

    out = pl.pallas_call(
        body, name="adamw_small",
        out_shape=[jax.ShapeDtypeStruct(w.shape, F32) for w, _, _, _ in params for _ in range(4)],
        compiler_params=pltpu.CompilerParams(vmem_limit_bytes=VMEM_LIMIT),
    )(summed, *[t for w, m, v, _ in params for t in (w, m, v)])
    return [tuple(out[4 * p:4 * p + 4]) for p in range(n)]


def _adamw(w, g, m, v, name, after=()):
    rows, cols = w.shape
    br = _row_block(rows, cols)

    def body(*refs):
        w_ref, g_ref, m_ref, v_ref = refs[:4]
        d_ref, nm_ref, nv_ref = refs[4 + len(after):]
        d_ref[...], nm_ref[...], nv_ref[...] = _adam_math(w_ref[...], g_ref[...], m_ref[...], v_ref[...])

    blk = pl.BlockSpec((br, cols), lambda i: (i, 0))
    return pl.pallas_call(
        body, name=name, grid=(rows // br,),
        in_specs=[blk] * 4 + [_ANY] * len(after), out_specs=[blk] * 3,
        out_shape=[jax.ShapeDtypeStruct((rows, cols), F32)] * 3,
        compiler_params=_params(("arbitrary",)),
    )(w, g, m, v, *after)


def _all_gather(x_shard, name):
    m_per, n = x_shard.shape

    def body(x_ref, out_ref, send_sems, recv_sems, local_sem):
        x, y, c = _position()
        me, sibling = (x, y, c), (x, y, 1 - c)
        chips = [(1 - x, y), (x, 1 - y), (1 - x, 1 - y)]

        def rows(px, py, pc):
            return out_ref.at[pl.ds((4 * px + 2 * py + pc) * m_per, m_per), :]

        def copy(k, block, to, src=None):
            return pltpu.make_async_remote_copy(
                src_ref=rows(*block) if src is None else src, dst_ref=rows(*block),
                send_sem=send_sems.at[k], recv_sem=recv_sems.at[k], device_id=to, device_id_type=MESH)

        mine = pltpu.make_async_copy(x_ref, rows(*me), local_sem)
        mine.start()
        first = [copy(0, me, sibling, src=x_ref)]
        first += [copy(1 + j, me, (*chip, c), src=x_ref) for j, chip in enumerate(chips)]
        for cp in first:
            cp.start()
        passed = [copy(4 + j, (*chip, c), sibling) for j, chip in enumerate(chips)]
        for j, chip in enumerate(chips):
            copy(1 + j, (*chip, c), me).wait_recv()
            passed[j].start()
        copy(0, sibling, me).wait_recv()
        for j, chip in enumerate(chips):
            copy(4 + j, (*chip, 1 - c), me).wait_recv()
        for cp in first + passed:
            cp.wait_send()
        mine.wait()

    return pl.pallas_call(
        body, name=name,
        out_shape=jax.ShapeDtypeStruct((N_DEV * m_per, n), x_shard.dtype),
        in_specs=[pl.BlockSpec(memory_space=pltpu.VMEM)],
        out_specs=pl.BlockSpec(memory_space=pltpu.VMEM),
        scratch_shapes=[pltpu.SemaphoreType.DMA((7,)), pltpu.SemaphoreType.DMA((7,)), pltpu.SemaphoreType.DMA],
        compiler_params=pltpu.CompilerParams(vmem_limit_bytes=VMEM_LIMIT),
    )(x_shard)


_ANY = pl.BlockSpec(memory_space=pl.ANY)


def _place_quarters(place, quarters):
    steps = 2

    def body(place_ref, *refs):
        n = len(refs) // 2
        for w_ref, o_ref in zip(refs[:n], refs[n:]):
            o_ref[0] = w_ref[...].astype(BF16)

    return pl.pallas_call(
        body, name="place_quarters",
        grid_spec=pltpu.PrefetchScalarGridSpec(
            num_scalar_prefetch=1, grid=(steps,),
            in_specs=[pl.BlockSpec((q.shape[0] // steps, q.shape[1]), lambda r, place_ref: (r, 0)) for q in quarters],
            out_specs=[pl.BlockSpec((1, q.shape[0] // steps, q.shape[1]), lambda r, place_ref: (place_ref[0], r, 0))
                       for q in quarters]),
        out_shape=[jax.ShapeDtypeStruct((N_CHIPS,) + q.shape, BF16) for q in quarters],
        compiler_params=_params(("arbitrary",)),
    )(place, *quarters)


_HBM = pl.BlockSpec(memory_space=pltpu.HBM)
_SEM = pl.BlockSpec(memory_space=pltpu.SEMAPHORE)
_EFFECT = pltpu.SideEffectType.DATAFLOW_SIDE_EFFECTING


def _quarter_halves(shapes, a, which):
    hr = shapes[a][0] // 2
    return pl.ds(which * hr, hr)


def _gather_start(placed, after, tag):
    n = len(placed)
    m = len(after)
    shapes = [b.shape[1:] for b in placed]

    def body(*refs):
        g_refs = refs[:n]
        send_sems, recv_sems = refs[n + m], refs[n + m + 1]
        token = refs[2 * n + m + 2]
        x, y, c = _position()
        chips = [(1 - x, y), (x, 1 - y), (1 - x, 1 - y)]
        mine = 2 * x + y
        for a in range(n):
            ref = g_refs[a].at[mine, _quarter_halves(shapes, a, c), :]
            for p in range(3):
                pltpu.make_async_remote_copy(
                    src_ref=ref, dst_ref=ref, send_sem=send_sems.at[3 * a + p], recv_sem=recv_sems.at[3 * a + p],
                    device_id=(*chips[p], c), device_id_type=MESH).start()
        token[...] = jnp.zeros_like(token)

    out = pl.pallas_call(
        body, name="gather_start_" + tag,
        out_shape=(pltpu.SemaphoreType.DMA((3 * n,)), pltpu.SemaphoreType.DMA((3 * n,)),
                   *[pltpu.HBM(b.shape, b.dtype) for b in placed], jax.ShapeDtypeStruct((8, LANES), F32)),
        in_specs=[_HBM] * n + [_ANY] * m,
        out_specs=(_SEM, _SEM, *[_HBM] * n, pl.BlockSpec(memory_space=pltpu.VMEM)),
        input_output_aliases={a: 2 + a for a in range(n)},
        compiler_params=pltpu.CompilerParams(has_side_effects=_EFFECT),
    )(*[pltpu.with_memory_space_constraint(b, pltpu.HBM) for b in placed], *after)
    return out[0], out[1], list(out[2:2 + n]), out[2 + n]


def _gather_wait(send_sems, recv_sems, thru, after, tag):
    n = len(thru)
    shapes = [b.shape[1:] for b in thru]

    def body(*refs):
        g_refs = refs[:n]
        send_sems, recv_sems = refs[n], refs[n + 1]
        x, y, c = _position()
        chips = [(1 - x, y), (x, 1 - y), (1 - x, 1 - y)]
        mine = 2 * x + y
        for a in range(n):
            rows = _quarter_halves(shapes, a, c)
            for p, (cx, cy) in enumerate(chips):
                copy = pltpu.make_async_remote_copy(
                    src_ref=g_refs[a].at[mine, rows, :], dst_ref=g_refs[a].at[2 * cx + cy, rows, :],
                    send_sem=send_sems.at[3 * a + p], recv_sem=recv_sems.at[3 * a + p],
                    device_id=(cx, cy, c), device_id_type=MESH)
                copy.wait_send()
                copy.wait_recv()

    return pl.pallas_call(
        body, name="gather_wait_" + tag,
        out_shape=[pltpu.HBM(b.shape, b.dtype) for b in thru],
        in_specs=[_HBM] * n + [_SEM, _SEM, _ANY], out_specs=[_HBM] * n,
        input_output_aliases={a: a for a in range(n)},
        compiler_params=pltpu.CompilerParams(has_side_effects=_EFFECT),
    )(*thru, send_sems, recv_sems, after)


def _gather_forward(bufs, tag):
    n = len(bufs)
    shapes = [b.shape[1:] for b in bufs]

    def body(*refs):
        g_refs = refs[n:2 * n]
        send_sems, recv_sems = refs[2 * n:]
        x, y, c = _position()
        chips = [(1 - x, y), (x, 1 - y), (1 - x, 1 - y)]

        def over_d2d(a, p, which):
            cx, cy = chips[p]
            ref = g_refs[a].at[2 * cx + cy, _quarter_halves(shapes, a, which), :]
            return pltpu.make_async_remote_copy(
                src_ref=ref, dst_ref=ref, send_sem=send_sems.at[3 * a + p], recv_sem=recv_sems.at[3 * a + p],
                device_id=(x, y, 1 - c), device_id_type=MESH)

        sends = [over_d2d(a, p, c) for a in range(n) for p in range(3)]
        for cp in sends:
            cp.start()
        for a in range(n):
            for p in range(3):
                over_d2d(a, p, 1 - c).wait_recv()
        for cp in sends:
            cp.wait_send()

    return pl.pallas_call(
        body, name="gather_forward_" + tag,
        out_shape=[jax.ShapeDtypeStruct(b.shape, BF16) for b in bufs],
        in_specs=[_ANY] * n, out_specs=[_ANY] * n,
        input_output_aliases={a: a for a in range(n)},
        scratch_shapes=[pltpu.SemaphoreType.DMA((3 * n,)), pltpu.SemaphoreType.DMA((3 * n,))],
    )(*bufs)


_FLIPS = [(fx, fy, fc) for fx in (0, 1) for fy in (0, 1) for fc in (0, 1)][1:]


def _flipped(pos, flip):
    return tuple(1 - p if f else p for p, f in zip(pos, flip))


def _direct_gather_start(slots):
    def body(s_ref, send_sems, recv_sems, thru, token):
        me = _position()
        mine = s_ref.at[4 * me[0] + 2 * me[1] + me[2]]
        for r, flip in enumerate(_FLIPS):
            pltpu.make_async_remote_copy(
                src_ref=mine, dst_ref=mine, send_sem=send_sems.at[r], recv_sem=recv_sems.at[r],
                device_id=_flipped(me, flip), device_id_type=MESH).start()
        token[...] = jnp.zeros_like(token)

    return pl.pallas_call(
        body, name="small_gather_start",
        out_shape=(pltpu.SemaphoreType.DMA((len(_FLIPS),)), pltpu.SemaphoreType.DMA((len(_FLIPS),)),
                   pltpu.HBM(slots.shape, slots.dtype), jax.ShapeDtypeStruct((8, LANES), F32)),
        in_specs=[_HBM], out_specs=(_SEM, _SEM, _HBM, pl.BlockSpec(memory_space=pltpu.VMEM)),
        input_output_aliases={0: 2},
        compiler_params=pltpu.CompilerParams(has_side_effects=_EFFECT),
    )(pltpu.with_memory_space_constraint(slots, pltpu.HBM))


def _direct_gather_wait(send_sems, recv_sems, slots, after):
    def body(s_ref, send_sems, recv_sems, after_ref, out_ref):
        me = _position()
        mine = s_ref.at[4 * me[0] + 2 * me[1] + me[2]]
        for r, flip in enumerate(_FLIPS):
            peer = _flipped(me, flip)
            copy = pltpu.make_async_remote_copy(
                src_ref=mine, dst_ref=s_ref.at[4 * peer[0] + 2 * peer[1] + peer[2]],
                send_sem=send_sems.at[r], recv_sem=recv_sems.at[r], device_id=peer, device_id_type=MESH)
            copy.wait_send()
            copy.wait_recv()

    return pl.pallas_call(
        body, name="small_gather_wait",
        out_shape=pltpu.HBM(slots.shape, slots.dtype),
        in_specs=[_HBM, _SEM, _SEM, _ANY], out_specs=_HBM,
        input_output_aliases={0: 0},
        compiler_params=pltpu.CompilerParams(has_side_effects=_EFFECT),
    )(slots, send_sems, recv_sems, after)


def _sibling_split_start(bufs, parts, nparts, after, tag):
    n, m = len(bufs), len(after)

    def body(*refs):
        b_refs = refs[:n]
        send_sems, recv_sems = refs[n + m], refs[n + m + 1]
        token = refs[2 * n + m + 2]
        x, y, c = _position()
        for r, ref in enumerate(parts(b_refs, x, y, c)):
            pltpu.make_async_remote_copy(
                src_ref=ref, dst_ref=ref, send_sem=send_sems.at[r], recv_sem=recv_sems.at[r],
                device_id=(x, y, 1 - c), device_id_type=MESH).start()
        token[...] = jnp.zeros_like(token)

    out = pl.pallas_call(
        body, name="sibling_start_" + tag,
        out_shape=(pltpu.SemaphoreType.DMA((nparts,)), pltpu.SemaphoreType.DMA((nparts,)),
                   *[pltpu.HBM(b.shape, b.dtype) for b in bufs], jax.ShapeDtypeStruct((8, LANES), F32)),
        in_specs=[_HBM] * n + [_ANY] * m,
        out_specs=(_SEM, _SEM, *[_HBM] * n, pl.BlockSpec(memory_space=pltpu.VMEM)),
        input_output_aliases={a: 2 + a for a in range(n)},
        compiler_params=pltpu.CompilerParams(has_side_effects=_EFFECT),
    )(*[pltpu.with_memory_space_constraint(b, pltpu.HBM) for b in bufs], *after)
    return out[0], out[1], list(out[2:2 + n]), out[2 + n]


def _sibling_split_wait(send_sems, recv_sems, bufs, parts, after, tag):
    n = len(bufs)

    def body(*refs):
        b_refs = refs[:n]
        send_sems, recv_sems = refs[n], refs[n + 1]
        x, y, c = _position()
        mine, theirs = parts(b_refs, x, y, c), parts(b_refs, x, y, 1 - c)
        for r, (src, dst) in enumerate(zip(mine, theirs)):
            copy = pltpu.make_async_remote_copy(
                src_ref=src, dst_ref=dst, send_sem=send_sems.at[r], recv_sem=recv_sems.at[r],
                device_id=(x, y, 1 - c), device_id_type=MESH)
            copy.wait_send()
            copy.wait_recv()

    return pl.pallas_call(
        body, name="sibling_wait_" + tag,
        out_shape=[pltpu.HBM(b.shape, b.dtype) for b in bufs],
        in_specs=[_HBM] * n + [_SEM, _SEM, _ANY], out_specs=[_HBM] * n,
        input_output_aliases={a: a for a in range(n)},
        compiler_params=pltpu.CompilerParams(has_side_effects=_EFFECT),
    )(*bufs, send_sems, recv_sems, after)


def _sibling_exchange(grads, tag):
    n = len(grads)
    shapes = [g.shape for g in grads]

    def body(*refs):
        g_refs, x_refs = refs[:n], refs[n:2 * n]
        send_sems, recv_sems = refs[2 * n:]
        x, y, c = _position()
        copies = []
        for a in range(n):
            hr = shapes[a][1] // 2
            cp = pltpu.make_async_remote_copy(
                src_ref=g_refs[a].at[:, pl.ds((1 - c) * hr, hr), :], dst_ref=x_refs[a],
                send_sem=send_sems.at[a], recv_sem=recv_sems.at[a],
                device_id=(x, y, 1 - c), device_id_type=MESH)
            cp.start()
            copies.append(cp)
        for cp in copies:
            cp.wait()

    return pl.pallas_call(
        body, name="grad_sibling_exchange_" + tag,
        out_shape=[jax.ShapeDtypeStruct((g.shape[0], g.shape[1] // 2, g.shape[2]), g.dtype) for g in grads],
        in_specs=[_ANY] * n, out_specs=[_ANY] * n,
        scratch_shapes=[pltpu.SemaphoreType.DMA((n,)), pltpu.SemaphoreType.DMA((n,))],
    )(*grads)


def _sibling_exchange_start(grads, tag):
    n = len(grads)
    lands = [lax.empty((g.shape[0], g.shape[1] // 2, g.shape[2]), g.dtype) for g in grads]

    def body(*refs):
        g_refs, x_refs = refs[:n], refs[n:2 * n]
        send_sems, recv_sems = refs[2 * n], refs[2 * n + 1]
        token = refs[4 * n + 2]
        x, y, c = _position()
        for a in range(n):
            hr = grads[a].shape[1] // 2
            pltpu.make_async_remote_copy(
                src_ref=g_refs[a].at[:, pl.ds((1 - c) * hr, hr), :], dst_ref=x_refs[a],
                send_sem=send_sems.at[a], recv_sem=recv_sems.at[a],
                device_id=(x, y, 1 - c), device_id_type=MESH).start()
        token[...] = jnp.zeros_like(token)

    both = list(grads) + lands
    out = pl.pallas_call(
        body, name="grad_sibling_exchange_start_" + tag,
        out_shape=(pltpu.SemaphoreType.DMA((n,)), pltpu.SemaphoreType.DMA((n,)),
                   *[pltpu.HBM(b.shape, b.dtype) for b in both], jax.ShapeDtypeStruct((8, LANES), F32)),
        in_specs=[_HBM] * (2 * n),
        out_specs=(_SEM, _SEM, *[_HBM] * (2 * n), pl.BlockSpec(memory_space=pltpu.VMEM)),
        input_output_aliases={a: 2 + a for a in range(2 * n)},
        compiler_params=pltpu.CompilerParams(has_side_effects=_EFFECT),
    )(*[pltpu.with_memory_space_constraint(b, pltpu.HBM) for b in both])
    return out[0], out[1], list(out[2:2 + n]), list(out[2 + n:2 + 2 * n]), out[2 + 2 * n]


def _sibling_exchange_wait(send_sems, recv_sems, grads, lands, after, tag):
    n = len(grads)

    def body(*refs):
        g_refs, x_refs = refs[:n], refs[n:2 * n]
        send_sems, recv_sems = refs[2 * n], refs[2 * n + 1]
        x, y, c = _position()
        for a in range(n):
            hr = grads[a].shape[1] // 2
            copy = pltpu.make_async_remote_copy(
                src_ref=g_refs[a].at[:, pl.ds((1 - c) * hr, hr), :], dst_ref=x_refs[a],
                send_sem=send_sems.at[a], recv_sem=recv_sems.at[a],
                device_id=(x, y, 1 - c), device_id_type=MESH)
            copy.wait_send()
            copy.wait_recv()

    both = list(grads) + list(lands)
    out = pl.pallas_call(
        body, name="grad_sibling_exchange_wait_" + tag,
        out_shape=[pltpu.HBM(b.shape, b.dtype) for b in both],
        in_specs=[_HBM] * (2 * n) + [_SEM, _SEM, _ANY], out_specs=[_HBM] * (2 * n),
        input_output_aliases={a: a for a in range(2 * n)},
        compiler_params=pltpu.CompilerParams(has_side_effects=_EFFECT),
    )(*both, send_sems, recv_sems, after)
    return list(out[n:])


def _chip_sums(core, grads, theirs, tag):
    n = len(grads)

    def body(core_ref, *refs):
        g_refs, t_refs, o_refs = refs[:n], refs[n:2 * n], refs[2 * n:]
        for g_ref, t_ref, o_ref in zip(g_refs, t_refs, o_refs):
            o_ref[...] = (g_ref[...] + t_ref[...].astype(F32)).astype(BF16)

    in_specs = [pl.BlockSpec((1, g.shape[1] // 2, g.shape[2]), lambda k, core_ref: (k, core_ref[0], 0)) for g in grads]
    in_specs += [pl.BlockSpec((1,) + t.shape[1:], lambda k, core_ref: (k, 0, 0)) for t in theirs]
    return pl.pallas_call(
        body, name="grad_chip_sums_" + tag,
        grid_spec=pltpu.PrefetchScalarGridSpec(
            num_scalar_prefetch=1, grid=(N_CHIPS,), in_specs=in_specs,
            out_specs=[pl.BlockSpec((1,) + t.shape[1:], lambda k, core_ref: (k, 0, 0)) for t in theirs]),
        out_shape=[jax.ShapeDtypeStruct(t.shape, BF16) for t in theirs],
        compiler_params=_params(("arbitrary",)),
    )(core, *grads, *theirs)


def _chip_exchange_start(sums, after, tag):
    n = len(sums)
    m = len(after)
    lands = [lax.empty((3,) + s.shape[1:], BF16) for s in sums]

    def body(*refs):
        s_refs, y_refs = refs[:n], refs[n:2 * n]
        send_sems, recv_sems = refs[2 * n + m], refs[2 * n + m + 1]
        token = refs[4 * n + m + 2]
        x, y, c = _position()
        chips = [(1 - x, y), (x, 1 - y), (1 - x, 1 - y)]
        for a in range(n):
            for p, (cx, cy) in enumerate(chips):
                pltpu.make_async_remote_copy(
                    src_ref=s_refs[a].at[2 * cx + cy], dst_ref=y_refs[a].at[p],
                    send_sem=send_sems.at[3 * a + p], recv_sem=recv_sems.at[3 * a + p],
                    device_id=(cx, cy, c), device_id_type=MESH).start()
        token[...] = jnp.zeros_like(token)

    both = list(sums) + lands
    out = pl.pallas_call(
        body, name="grad_chip_exchange_start_" + tag,
        out_shape=(pltpu.SemaphoreType.DMA((3 * n,)), pltpu.SemaphoreType.DMA((3 * n,)),
                   *[pltpu.HBM(b.shape, b.dtype) for b in both], jax.ShapeDtypeStruct((8, LANES), F32)),
        in_specs=[_HBM] * (2 * n) + [_ANY] * m,
        out_specs=(_SEM, _SEM, *[_HBM] * (2 * n), pl.BlockSpec(memory_space=pltpu.VMEM)),
        input_output_aliases={a: 2 + a for a in range(2 * n)},
        compiler_params=pltpu.CompilerParams(has_side_effects=_EFFECT),
    )(*[pltpu.with_memory_space_constraint(b, pltpu.HBM) for b in both], *after)
    return out[0], out[1], list(out[2:2 + n]), list(out[2 + n:2 + 2 * n]), out[2 + 2 * n]


def _chip_exchange_wait(send_sems, recv_sems, sums, lands, after, tag):
    n = len(sums)

    def body(*refs):
        s_refs, y_refs = refs[:n], refs[n:2 * n]
        send_sems, recv_sems = refs[2 * n], refs[2 * n + 1]
        x, y, c = _position()
        chips = [(1 - x, y), (x, 1 - y), (1 - x, 1 - y)]
        for a in range(n):
            for p, (cx, cy) in enumerate(chips):
                copy = pltpu.make_async_remote_copy(
                    src_ref=s_refs[a].at[2 * cx + cy], dst_ref=y_refs[a].at[p],
                    send_sem=send_sems.at[3 * a + p], recv_sem=recv_sems.at[3 * a + p],
                    device_id=(cx, cy, c), device_id_type=MESH)
                copy.wait_send()
                copy.wait_recv()

    both = list(sums) + list(lands)
    out = pl.pallas_call(
        body, name="grad_chip_exchange_wait_" + tag,
        out_shape=[pltpu.HBM(b.shape, b.dtype) for b in both],
        in_specs=[_HBM] * (2 * n) + [_SEM, _SEM, _ANY], out_specs=[_HBM] * (2 * n),
        input_output_aliases={a: a for a in range(2 * n)},
        compiler_params=pltpu.CompilerParams(has_side_effects=_EFFECT),
    )(*both, send_sems, recv_sems, after)
    return list(out[:n]), list(out[n:])


def _total_sums(place, sums, parts, after, tag):
    n = len(parts)
    m = len(after)
    steps = 2

    def body(place_ref, *refs):
        for s_ref, y_ref, o_ref in zip(refs[:n], refs[n:2 * n], refs[2 * n + m:]):
            o_ref[0] = ((s_ref[0].astype(F32) + y_ref[0].astype(F32)) + y_ref[1].astype(F32)) + y_ref[2].astype(F32)

    def step_rows(pt):
        return pt.shape[1] // steps

    in_specs = [pl.BlockSpec((1, step_rows(s), s.shape[2]), lambda r, place_ref: (place_ref[0], r, 0)) for s in sums]
    in_specs += [pl.BlockSpec((3, step_rows(pt), pt.shape[2]), lambda r, place_ref: (0, r, 0)) for pt in parts]
    in_specs += [_ANY] * m
    return pl.pallas_call(
        body, name="grad_total_sums_" + tag,
        grid_spec=pltpu.PrefetchScalarGridSpec(
            num_scalar_prefetch=1, grid=(steps,), in_specs=in_specs,
            out_specs=[pl.BlockSpec((1, step_rows(pt), pt.shape[2]), lambda r, place_ref: (place_ref[1], r, 0))
                       for pt in parts]),
        out_shape=[jax.ShapeDtypeStruct((2,) + pt.shape[1:], F32) for pt in parts],
        compiler_params=_params(("arbitrary",)),
    )(place, *sums, *parts, *after)


def _sibling_share(halves, tag):
    n = len(halves)

    def body(*refs):
        f_refs = refs[n:2 * n]
        send_sems, recv_sems = refs[2 * n:]
        x, y, c = _position()
        copies = []
        for a in range(n):
            cp = pltpu.make_async_remote_copy(
                src_ref=f_refs[a].at[c], dst_ref=f_refs[a].at[c], send_sem=send_sems.at[a], recv_sem=recv_sems.at[a],
                device_id=(x, y, 1 - c), device_id_type=MESH)
            cp.start()
            copies.append(cp)
        for a, cp in enumerate(copies):
            cp.wait_send()
            pltpu.make_async_remote_copy(
                src_ref=f_refs[a].at[1 - c], dst_ref=f_refs[a].at[1 - c], send_sem=send_sems.at[a],
                recv_sem=recv_sems.at[a], device_id=(x, y, c), device_id_type=MESH).wait_recv()

    return pl.pallas_call(
        body, name="grad_sibling_share_" + tag,
        out_shape=[jax.ShapeDtypeStruct(h.shape, F32) for h in halves],
        in_specs=[_ANY] * n, out_specs=[_ANY] * n,
        input_output_aliases={a: a for a in range(n)},
        scratch_shapes=[pltpu.SemaphoreType.DMA((n,)), pltpu.SemaphoreType.DMA((n,))],
    )(*halves)


def _group_sum(stacked, nrow, name):
    total, n = stacked.shape
    groups = total // nrow

    def body(g_ref, o_ref):
        acc = g_ref[0:nrow, :]
        for grp in range(1, groups):
            acc = acc + g_ref[grp * nrow:(grp + 1) * nrow, :]
        o_ref[...] = acc

    return pl.pallas_call(
        body, name=name,
        out_shape=jax.ShapeDtypeStruct((nrow, n), F32),
        compiler_params=pltpu.CompilerParams(vmem_limit_bytes=VMEM_LIMIT),
    )(stacked)


def _local_step(xt, tgt, mod, gains, w_pool, pool_scale, w_in, later_weights, on_ffn_grads, after_mixer_bwd,
                on_small_grads, seq):
    g_mpre, g_mpost, g_fpre, g_fpost = gains
    d = xt.shape[1]
    tm, tq = min(TOKEN_TILE, seq), min(ATTN_TILE, seq)

    h1, qn, k, v, u, kt, vt = _prenorm_proj(xt, mod, g_mpre, w_in, seq, tm)
    tk = min(ATTN_KEY_TILE, tq // 2)
    o, ltot = _attn_fwd(qn, k, vt, seq, tq, tk)
    w_out, order, ffn_weights = later_weights(o)
    w_out2 = w_out.reshape(d, d)
    pooled, mixin, mix, x1, h2 =_mixer_post(u, o, xt, mod, g_mpost, g_fpre + order, w_pool, pool_scale, w_out2, seq, tm)
    w_g, w_u, w_d = ffn_weights(h2)
    a, b, fin, dy, df, loss_blk, accb4, accg4 = _ffn_fwd(h2, w_g, w_u, w_d, x1, tgt, mod, g_fpost, seq, tm)
    da, db, dx1, dmix, accb5, accg5 = _ffn_bwd(df, a, b, w_d, w_g, w_u, x1, dy, mix, mod, g_fpre, g_mpost, seq, tm)
    bt = min(GRAD_TOKEN_TILE, xt.shape[0])
    bt_one = min(2 * GRAD_TOKEN_TILE, xt.shape[0])
    (g_g,), (g_g16,) = _tn_matmul(da, [h2], w_g.shape[0], bt_one, "grad_w_gate")
    (g_u,), (g_u16,) = _tn_matmul(db, [h2], w_u.shape[0], bt_one, "grad_w_up")
    (g_d,), (g_d16,) = _tn_matmul(fin, [df], w_d.shape[0], bt_one, "grad_w_down")
    token = on_ffn_grads([g_g, g_u, g_d], [g_g16, g_u16, g_d16])
    do, dpd, dps, dwp = _mixer_bwd(dmix, w_out2, pooled, w_pool, pool_scale + token, seq, tm)
    order = after_mixer_bwd(do)
    dq, dk, dv = _attn_bwd(qn, k, kt, v, do, ltot, seq, tq, tk, order)
    gx, du, accb8, accg8 = _inproj_bwd(dq, dk, dv, dpd, xt, dx1, mod, g_mpre, w_in, seq, tm)

    dmod = jnp.stack([accb8[:, 0], accb8[:, 1], accb5[:, 2], accb5[:, 0], accb5[:, 1], accb4[:, 0]], axis=1)
    dgain = jnp.stack([accg8[0], accg5[1], accg5[0], accg4[0]], axis=0)
    behind = on_small_grads(loss_blk, dmod, dgain, dps[0:1], dwp)
    g_in, g_in16 = _tn_matmul_stacked(h1, [dq, dk, dv, du], bt, "grad_w_in", behind)
    g_out, g_out16 = [parts[0].reshape(w_out.shape)
                      for parts in _tn_matmul(mixin, [dmix], 1, bt_one, "grad_w_out", behind)]
    grads = [g_in, g_out, g_g, g_u, g_d]
    grads16 = [g_in16, g_out16, g_g16, g_u16, g_d16]
    return gx, grads, grads16


def kernel(x, c, w_cond, b_cond, g_mix_pre, g_mix_post, w_in, w_pool, pool_scale, w_out, g_ffn_pre, g_ffn_post, w_gate, w_up, w_down, loss_target, m_w_cond, m_b_cond, m_g_mix_pre, m_g_mix_post, m_w_in, m_w_pool, m_pool_scale, m_w_out, m_g_ffn_pre, m_g_ffn_post, m_w_gate, m_w_up, m_w_down, v_w_cond, v_b_cond, v_g_mix_pre, v_g_mix_post, v_w_in, v_w_pool, v_pool_scale, v_w_out, v_g_ffn_pre, v_g_ffn_post, v_w_gate, v_w_up, v_w_down):
    xi, yi, ci = _position()
    chip = 2 * xi + yi
    dev = 4 * xi + 2 * yi + ci
    nb, seq, d = x.shape
    t_all = nb * seq
    xt = x.reshape(t_all, d)
    tgt = loss_target.reshape(t_all, d)
    ncol = w_cond.shape[2]
    pw = pool_scale.shape[1]

    place = jnp.stack([chip, ci]).astype(jnp.int32)
    turned = lambda t: jnp.swapaxes(t[0], 0, 1)
    placed = _place_quarters(place, [w_in[0], w_out[0], turned(w_gate), turned(w_up), w_down[0]])
    in_sems = _gather_start(placed[:1], [], "in")

    c_pad = jnp.concatenate([c, jnp.zeros((8 - nb, d), F32)], axis=0) + in_sems[3][0:1, 0:1]
    c_all = _all_gather(c_pad, "gather_c").reshape(N_DEV, 8, d)[:, :nb].reshape(N_DEV * nb, d)
    b_q = lax.dynamic_slice(b_cond, (0, chip * ncol), (1, ncol))
    sc_all, mod_q = _cond_fwd(c_all, w_cond[0], b_q, 512)
    mod_parts = _all_gather(mod_q, "gather_mod").reshape(N_DEV, N_DEV * nb, ncol)
    mod_rows = lax.dynamic_slice(mod_parts, (0, dev * nb, 0), (N_DEV, nb, ncol))[0::2]
    mod = jnp.transpose(mod_rows, (1, 0, 2)).reshape(nb, N_MOD, d)
    mod = jnp.concatenate([mod, jnp.zeros((nb, MOD_ROWS - N_MOD, d), F32)], axis=1)

    (w_in_all,) = _gather_forward(_gather_wait(*in_sems[:3], mod, "in"), "in")
    send_sems, recv_sems, in_flight, token = _gather_start(placed[1:], [mod, w_in_all], "rest")
    mod = mod + token[0:1, 0:1]

    def later_weights(after):
        waited = _gather_wait(send_sems, recv_sems, in_flight, after, "rest")
        (w_out_all,) = _gather_forward(waited[:1], "out")
        shapes = [b.shape[1:] for b in waited[1:]]

        def parts(refs, px, py, which):
            return [refs[a].at[2 * cx + cy, _quarter_halves(shapes, a, which), :]
                    for a in range(len(refs)) for cx, cy in [(1 - px, py), (px, 1 - py), (1 - px, 1 - py)]]

        forward = _sibling_split_start(waited[1:], parts, 3 * len(shapes), [w_out_all], "ffn_weights")
        finish = lambda after2: _sibling_split_wait(*forward[:3], parts, after2, "ffn_weights")
        return w_out_all, forward[3][0:1, 0:1], finish

    ffn_split = []

    ffn_sibling = []

    def on_ffn_grads(ffn_grads, ffn_grads16):
        ffn_sibling.extend(_sibling_exchange_start(ffn_grads16, "ffn"))
        ffn_sibling.append(ffn_grads)
        return ffn_sibling[4][0:1, 0:1]

    def after_mixer_bwd(do):
        theirs = _sibling_exchange_wait(*ffn_sibling[:4], do, "ffn")
        ffn_split.extend(_chip_exchange_start(_chip_sums(place[1:], ffn_sibling[5], theirs, "ffn"), [], "ffn"))
        return ffn_split[4][0:1, 0:1]

    wp_rows = w_pool[0].size // d
    loss_row = 2 * N_MOD + 4 + 1
    pad_rows = 24 - (loss_row + 1)
    prow = 24 + wp_rows
    small_split = []

    def on_small_grads(loss_blk, dmod, dgain, dps, dwp):
        payload = jnp.concatenate([
            dmod.reshape(nb * N_MOD, d), dgain,
            jnp.concatenate([dps, jnp.zeros((1, d - pw), F32)], axis=1),
            jnp.concatenate([loss_blk[0:1], jnp.zeros((1, d - LANES), F32)], axis=1),
            jnp.zeros((pad_rows, d), F32),
            jnp.concatenate(jnp.split(dwp.reshape(-1, dwp.shape[-1]), d // dwp.shape[-1], axis=0), axis=1)], axis=0)
        slots = lax.dynamic_update_slice(lax.empty((N_DEV, prow, d), F32), payload[None], (dev, 0, 0))
        small_split.extend(_direct_gather_start(slots))
        return [small_split[3]]

    gains = (g_mix_pre, g_mix_post, g_ffn_pre, g_ffn_post)
    gx, grads, grads16 = _local_step(
        xt, tgt, mod, gains, w_pool[0], pool_scale, w_in_all, later_weights, on_ffn_grads, after_mixer_bwd,
        on_small_grads, seq)

    sums_ffn, parts_ffn = _chip_exchange_wait(*ffn_split[:4], gx, "ffn")
    gathered = _direct_gather_wait(*small_split[:3], grads16[1]).reshape(N_DEV * prow, d)
    summed = _group_sum(gathered, prow, "small_device_sum")
    loss = summed[loss_row, 0]
    dmod_all = gathered.reshape(N_DEV, prow, d)[:, :nb * N_MOD].reshape(N_DEV * nb, N_MOD * d)
    dmod_q = lax.dynamic_slice(dmod_all, (0, chip * ncol), (N_DEV * nb, ncol))
    g_w_cond = _cond_bwd(sc_all, dmod_q, 512)
    first_gain = 2 * N_MOD

    theirs = _sibling_exchange(grads16[:2], "mix")
    mix_split = _chip_exchange_start(_chip_sums(place[1:], grads[:2], theirs, "mix"), [gathered], "mix")
    unfold = lambda halves: [g.reshape(2 * g.shape[1], g.shape[2]) for g in halves]
    share_parts = lambda refs, px, py, which: [r.at[which] for r in refs]
    halves_ffn = _total_sums(place, sums_ffn, parts_ffn, [mix_split[4]], "ffn")
    share = _sibling_split_start(halves_ffn, share_parts, len(halves_ffn), [], "share_ffn")

    results = {}

    def update(name, w2, g2, m2, v2, shape, after=()):
        delta, new_m, new_v = _adamw(w2, g2, m2, v2, "adamw_" + name, after)
        back = (lambda t: jnp.swapaxes(t, 0, 1)[None]) if shape is None else (lambda t: t.reshape(shape))
        results[name] = [back(t) for t in (g2, delta, new_m, new_v)]
        return delta

    done_cond = update("w_cond", w_cond[0], g_w_cond, m_w_cond[0], v_w_cond[0], w_cond.shape, [share[3]])
    g_ffn = unfold(_sibling_split_wait(*share[:3], share_parts, done_cond, "share_ffn"))
    done = [update("w_gate", turned(w_gate), g_ffn[0], turned(m_w_gate), turned(v_w_gate), None),
            update("w_up", turned(w_up), g_ffn[1], turned(m_w_up), turned(v_w_up), None),
            update("w_down", w_down[0], g_ffn[2], m_w_down[0], v_w_down[0], w_down.shape)]

    gain_row = lambda r: (lambda s: s[first_gain + r:first_gain + r + 1, :])
    small = [
        ("b_cond", (b_cond, m_b_cond, v_b_cond), (N_MOD, d), lambda s: s[0:N_MOD, :] + s[N_MOD:2 * N_MOD, :]),
        ("g_mix_pre", (g_mix_pre, m_g_mix_pre, v_g_mix_pre), (1, d), gain_row(0)),
        ("g_mix_post", (g_mix_post, m_g_mix_post, v_g_mix_post), (1, d), gain_row(1)),
        ("g_ffn_pre", (g_ffn_pre, m_g_ffn_pre, v_g_ffn_pre), (1, d), gain_row(2)),
        ("g_ffn_post", (g_ffn_post, m_g_ffn_post, v_g_ffn_post), (1, d), gain_row(3)),
        ("pool_scale", (pool_scale, m_pool_scale, v_pool_scale), (1, pw),
         lambda s: s[first_gain + 4:first_gain + 5, 0:pw]),
        ("w_pool", (w_pool, m_w_pool, v_w_pool), (wp_rows * d // w_pool.shape[-1], w_pool.shape[-1]),
         lambda s: jnp.concatenate([s[24:24 + wp_rows, j * w_pool.shape[-1]:(j + 1) * w_pool.shape[-1]]
                                    for j in range(d // w_pool.shape[-1])], axis=0)),
    ]
    updated = _small_updates(summed, [tuple(t.reshape(flat) for t in wmv) + (pick,) for _, wmv, flat, pick in small])
    for (name, wmv, _, _), quad in zip(small, updated):
        results[name] = [t.reshape(wmv[0].shape) for t in quad]

    sums_mix, parts_mix = _chip_exchange_wait(*mix_split[:4], done[-1], "mix")
    g_mix = unfold(_sibling_share(_total_sums(place, sums_mix, parts_mix, done, "mix"), "mix"))
    update("w_in", w_in[0], g_mix[0], m_w_in[0], v_w_in[0], w_in.shape)
    update("w_out", w_out[0], g_mix[1], m_w_out[0], v_w_out[0], w_out.shape)

    names = ("w_cond", "b_cond", "g_mix_pre", "g_mix_post", "w_in", "w_pool", "pool_scale", "w_out",
             "g_ffn_pre", "g_ffn_post", "w_gate", "w_up", "w_down")
    outs = [results[name][part] for part in range(4) for name in names]
    return (loss, gx.reshape(x.shape), *outs)
```

```python
import jax
import jax.numpy as jnp
import numpy as np
from jax import lax
from jax.experimental import pallas as pl
from jax.experimental.pallas import tpu as pltpu

F32 = jnp.float32
BF16 = jnp.bfloat16
MESH = pl.DeviceIdType.MESH

EPS = 1e-6
HEAD_DIM = 64
HEADS_PER_BLOCK = 2
LANES = 128
NEG_QK_SCALE = -0.125
POOL_WINDOWS = (2, 4, 8, 16)
POOL_GROUP = 128
HALO = 16
N_MOD = 6
MOD_ROWS = 8
N_CHIPS = 4
N_DEV = 8
VMEM_LIMIT = 56 * 1024 * 1024

ADAM_LR = 0.001
ADAM_B1 = 0.9
ADAM_B2 = 0.999
ADAM_EPS = 1e-08
ADAM_WD = 0.01
ADAM_STEP = 10

TOKEN_TILE = 512
GRAD_TOKEN_TILE = 2048
FFN_ROW_CHUNKS = 2
ROW_CHUNKS = 2
ATTN_TILE = 512
ATTN_KEY_TILE = 256
ATTN_ROW_CHUNK = 32
LOG_SUM_PASSES = 1


def _dot(a, b):
    return jnp.dot(a, b, preferred_element_type=F32)


def _dot_nt(a, b):
    return lax.dot_general(a, b, (((1,), (1,)), ((), ())), preferred_element_type=F32)


def _dot_tn(a, b):
    return lax.dot_general(a, b, (((0,), (0,)), ((), ())), preferred_element_type=F32)


def _split(v):
    hi = v.astype(BF16)
    lo = (v - hi.astype(F32)).astype(BF16)
    return hi, lo


def _rms(v):
    return lax.rsqrt(jnp.mean(v * v, axis=-1, keepdims=True) + EPS)


def _norm_bwd(dn, n, r):
    return r * (dn - n * jnp.mean(dn * n, axis=-1, keepdims=True))


def _sigmoid(v):
    return 0.5 * jnp.tanh(0.5 * v) + 0.5


def _colsum(v):
    return jnp.sum(v, axis=0, keepdims=True)


def _params(sem=None):
    return pltpu.CompilerParams(dimension_semantics=sem, vmem_limit_bytes=VMEM_LIMIT)


def _position():
    return lax.axis_index("x"), lax.axis_index("y"), lax.axis_index("c")


def _prenorm_proj(x, mod, g_pre, w_in, seq, tm):
    t_all, d = x.shape
    nt = seq // tm
    p = w_in.shape[2]

    def body(x_ref, mod_ref, g_ref, w_ref, h_ref, q_ref, k_ref, v_ref, u_ref, kt_ref, vt_ref):
        for c in range(ROW_CHUNKS):
            rows = slice(c * (tm // ROW_CHUNKS), (c + 1) * (tm // ROW_CHUNKS))
            xf = x_ref[rows, :]
            n = xf * _rms(xf)
            h = (n * g_ref[...]) * (1.0 + mod_ref[0, 1:2, :]) + mod_ref[0, 0:1, :]
            hb = h.astype(BF16)
            h_ref[rows, :] = hb
            q_ref[rows, :] = (_dot(hb, w_ref[0]) * NEG_QK_SCALE).astype(BF16)
            kf = _dot(hb, w_ref[1])
            vf = _dot(hb, w_ref[2])
            k_ref[rows, :] = kf.astype(BF16)
            v_ref[rows, :] = vf.astype(BF16)
            kt_ref[:, rows] = kf.T.astype(BF16)
            vt_ref[:, rows] = vf.T.astype(BF16)
            u_ref[rows, :] = _dot(hb, w_ref[3])

    tok = lambda i: (i, 0)
    tok_t = lambda i: (0, i)
    return pl.pallas_call(
        body, name="prenorm_proj", grid=(t_all // tm,),
        in_specs=[pl.BlockSpec((tm, d), tok),
                  pl.BlockSpec((1, MOD_ROWS, d), lambda i: (i // nt, 0, 0)),
                  pl.BlockSpec((1, d), lambda i: (0, 0)),
                  pl.BlockSpec((N_CHIPS, d, p), lambda i: (0, 0, 0))],
        out_specs=[pl.BlockSpec((tm, d), tok)] + [pl.BlockSpec((tm, p), tok)] * 4 + [pl.BlockSpec((p, tm), tok_t)] * 2,
        out_shape=[jax.ShapeDtypeStruct((t_all, d), BF16)] + [jax.ShapeDtypeStruct((t_all, p), BF16)] * 3
        + [jax.ShapeDtypeStruct((t_all, p), F32)] + [jax.ShapeDtypeStruct((p, t_all), BF16)] * 2,
        compiler_params=_params(("arbitrary",)),
    )(x, mod, g_pre, w_in)


def _tri_matrix(tk, kind):
    j = np.arange(2 * tk)[:, None] % tk
    s = np.arange(tk)[None, :]
    return jnp.asarray({"after": j > s, "upto": j <= s, "before": j < s}[kind], dtype=BF16)


def _neg_abs(v):
    bits = lax.bitcast_convert_type(v, jnp.int32) | jnp.int32(-2 ** 31)
    return lax.bitcast_convert_type(bits, F32)


def _row_sums(v):
    return jnp.broadcast_to(jnp.sum(v, axis=-1, keepdims=True), (v.shape[0], LANES))


def _across(v, n):
    return jnp.concatenate([v] * (n // LANES), axis=1)


def _all_masked(c, diag, rc, tk):
    return diag is not None and diag * tk >= (c + 1) * rc - 1


def _some_masked(c, diag, rc, tk):
    return diag is not None and diag * tk + tk - 1 >= c * rc


def _attn_fwd(qn, k, vt, seq, tq, tk):
    t_all, w = qn.shape
    nb, nq, ndiag = t_all // seq, seq // tq, tq // tk
    assert ndiag % 2 == 0, "two key blocks per loop trip"
    rc = ATTN_ROW_CHUNK
    heads = range(HEADS_PER_BLOCK)

    def body(q_ref, k_ref, vt_ref, tri_ref, o_ref, l_ref,
             z_buf, ls_buf, hl_buf, aft_buf, w_buf, tot_buf, acc_t, run_buf):
        i = pl.program_id(2)
        nblk = (i + 1) * ndiag
        lane = lax.broadcasted_iota(jnp.int32, (1, LANES), 1)
        row = lax.broadcasted_iota(jnp.int32, (rc, tk), 0)
        col = lax.broadcasted_iota(jnp.int32, (rc, tk), 1)
        first = lane < HEAD_DIM
        q2 = q_ref[...]
        qs = [jnp.where(first, q2, jnp.zeros_like(q2)), jnp.where(first, jnp.zeros_like(q2), q2)]
        acc_t[...] = jnp.zeros_like(acc_t)
        run_buf[...] = jnp.zeros_like(run_buf)
        w_buf[1] = jnp.zeros((HEADS_PER_BLOCK, tq, tk), BF16)

        def causal(c, diag):
            return (col + diag * tk) < (row + c * rc)

        def scores(blk, slot):
            kj = k_ref[pl.ds(pl.multiple_of(blk * tk, tk), tk), :]
            for h in heads:
                z_buf[slot, h] = _dot_nt(qs[h], kj)

        def values(blk, slot):
            keys = pl.ds(pl.multiple_of(blk * tk, tk), tk)
            for h in heads:
                dims = slice(h * HEAD_DIM, (h + 1) * HEAD_DIM)
                acc_t[dims, :] += _dot_nt(vt_ref[dims, keys], w_buf[slot, h])

        def softplus_stage(h, slot, diag):
            for c in range(tq // rc):
                rows = slice(c * rc, (c + 1) * rc)
                if _all_masked(c, diag, rc, tk):
                    hl_buf[h, rows, :] = jnp.zeros((rc, LOG_SUM_PASSES * tk), BF16)
                    tot_buf[h, rows, :] = jnp.zeros((rc, LANES), F32)
                    continue
                nz = z_buf[slot, h, rows, :]
                l1 = jnp.minimum(nz, 0.0) - jnp.log(1.0 + jnp.exp(_neg_abs(nz)))
                if _some_masked(c, diag, rc, tk):
                    l1 = jnp.where(causal(c, diag), l1, 0.0)
                for s, part in enumerate(_split(l1)[:LOG_SUM_PASSES]):
                    hl_buf[h, rows, s * tk:(s + 1) * tk] = part
                ls_buf[h, rows, :] = l1 - nz
                tot_buf[h, rows, :] = _row_sums(l1)

        def weights_stage(h, slot, diag):
            for c in range(tq // rc):
                rows = slice(c * rc, (c + 1) * rc)
                if _all_masked(c, diag, rc, tk):
                    w_buf[slot, h, rows, :] = jnp.zeros((rc, tk), BF16)
                    continue
                wgt = jnp.exp((ls_buf[h, rows, :] + aft_buf[h, rows, :]) + _across(run_buf[h, rows, :], tk))
                if _some_masked(c, diag, rc, tk):
                    wgt = jnp.where(causal(c, diag), wgt, 0.0)
                w_buf[slot, h, rows, :] = wgt.astype(BF16)
                run_buf[h, rows, :] += tot_buf[h, rows, :]

        def position(blk, slot, diag):
            scores(jnp.maximum(blk - 1, 0), 1 - slot)
            for h in heads:
                softplus_stage(h, slot, diag)
                aft_buf[h] = _dot(hl_buf[h], tri_ref[...])
            values(jnp.minimum(blk + 1, nblk - 1), 1 - slot)
            for h in heads:
                weights_stage(h, slot, diag)

        scores(nblk - 1, 0)
        for p in range(ndiag):
            position(nblk - 1 - p, p % 2, ndiag - 1 - p)

        def trip(jj, carry):
            for u in range(2):
                position(i * ndiag - 1 - 2 * jj - u, u, None)
            return carry

        lax.fori_loop(0, (i * ndiag) // 2, trip, 0)
        values(0, 1)
        o_ref[...] = acc_t[...].T.astype(BF16)
        l_ref[...] = jnp.where(first, run_buf[0], run_buf[1])

    qmap = lambda b, hp, i: (b * nq + i, hp)
    nh = HEADS_PER_BLOCK
    return pl.pallas_call(
        body, name="attn_fwd", grid=(nb, w // LANES, nq),
        in_specs=[pl.BlockSpec((tq, LANES), qmap), pl.BlockSpec((seq, LANES), lambda b, hp, i: (b, hp)),
                  pl.BlockSpec((LANES, seq), lambda b, hp, i: (hp, b)),
                  pl.BlockSpec((LOG_SUM_PASSES * tk, tk), lambda b, hp, i: (0, 0))],
        out_specs=[pl.BlockSpec((tq, LANES), qmap), pl.BlockSpec((tq, LANES), qmap)],
        out_shape=[jax.ShapeDtypeStruct((t_all, w), BF16), jax.ShapeDtypeStruct((t_all, w), F32)],
        scratch_shapes=[pltpu.VMEM((2, nh, tq, tk), F32), pltpu.VMEM((nh, tq, tk), F32),
                        pltpu.VMEM((nh, tq, LOG_SUM_PASSES * tk), BF16), pltpu.VMEM((nh, tq, tk), F32),
                        pltpu.VMEM((2, nh, tq, tk), BF16), pltpu.VMEM((nh, tq, LANES), F32),
                        pltpu.VMEM((LANES, tq), F32), pltpu.VMEM((nh, tq, LANES), F32)],
        compiler_params=_params(("arbitrary", "arbitrary", "arbitrary")),
    )(qn, k, vt, _tri_matrix(tk, "after")[:LOG_SUM_PASSES * tk])


def _window_sums(ext, rows, offset, forward):
    r = lax.broadcasted_iota(jnp.int32, (rows, rows + HALO), 0)
    e = lax.broadcasted_iota(jnp.int32, (rows, rows + HALO), 1)
    hi, lo = _split(ext)
    out = []
    for g, win in enumerate(POOL_WINDOWS):
        if forward:
            band = (e >= r) & (e < r + win)
        else:
            band = (e <= r + offset) & (e > r + offset - win)
        bm = band.astype(BF16)
        cols = slice(g * POOL_GROUP, (g + 1) * POOL_GROUP)
        out.append(_dot(bm, hi[:, cols]) + _dot(bm, lo[:, cols]))
    return out


def _window_counts(pos):
    return [jnp.minimum(pos + 1, win).astype(F32) for win in POOL_WINDOWS]


def _mixer_post(u, o, x, mod, g_post, g_fpre, w_pool, pool_scale, w_out, seq, tm):
    t_all, d = x.shape
    nt = seq // tm
    p = u.shape[1]

    def body(u_ref, halo_ref, o_ref, x_ref, mod_ref, gp_ref, gf_ref, wp_ref, ps_ref, wo_ref,
             pooled_ref, mixin_ref, mix_ref, x1_ref, h2_ref):
        it = pl.program_id(0) % nt
        uf = u_ref[...]
        halo = jnp.where(it == 0, 0.0, halo_ref[...])
        ext = jnp.concatenate([halo, uf], axis=0)
        pos = it * tm + lax.broadcasted_iota(jnp.int32, (tm, 1), 0)
        sums = _window_sums(ext, tm, HALO, False)
        cnts = _window_counts(pos)
        pools = []
        for g in range(len(POOL_WINDOWS)):
            cols = slice(g * POOL_GROUP, (g + 1) * POOL_GROUP)
            pooled = (sums[g] / cnts[g] - uf[:, cols]).astype(BF16)
            pooled_ref[:, cols] = pooled
            yg = _dot(pooled, wp_ref[g].astype(BF16))
            pools.append((yg * ps_ref[:, cols]).astype(BF16))
        mixin_ref[...] = jnp.concatenate([o_ref[...]] + pools, axis=1)
        for c in range(ROW_CHUNKS):
            rows = slice(c * (tm // ROW_CHUNKS), (c + 1) * (tm // ROW_CHUNKS))
            mix = _dot(mixin_ref[rows, :], wo_ref[...])
            mix_ref[rows, :] = mix
            n2 = mix * _rms(mix)
            x1 = x_ref[rows, :] + mod_ref[0, 2:3, :] * (n2 * gp_ref[...])
            x1_ref[rows, :] = x1
            n3 = x1 * _rms(x1)
            h2 = (n3 * gf_ref[...]) * (1.0 + mod_ref[0, 4:5, :]) + mod_ref[0, 3:4, :]
            h2_ref[rows, :] = h2.astype(BF16)

    tok = lambda i: (i, 0)
    const2 = lambda i: (0, 0)
    hb = tm // HALO
    return pl.pallas_call(
        body, name="mixer_post", grid=(t_all // tm,),
        in_specs=[pl.BlockSpec((tm, p), tok),
                  pl.BlockSpec((HALO, p), lambda i: (jnp.maximum(i * hb - 1, 0), 0)),
                  pl.BlockSpec((tm, p), tok),
                  pl.BlockSpec((tm, d), tok),
                  pl.BlockSpec((1, MOD_ROWS, d), lambda i: (i // nt, 0, 0)),
                  pl.BlockSpec((1, d), const2), pl.BlockSpec((1, d), const2),
                  pl.BlockSpec(w_pool.shape, lambda i: (0, 0, 0)),
                  pl.BlockSpec((1, p), const2),
                  pl.BlockSpec((d, d), const2)],
        out_specs=[pl.BlockSpec((tm, p), tok), pl.BlockSpec((tm, d), tok), pl.BlockSpec((tm, d), tok),
                   pl.BlockSpec((tm, d), tok), pl.BlockSpec((tm, d), tok)],
        out_shape=[jax.ShapeDtypeStruct((t_all, p), BF16), jax.ShapeDtypeStruct((t_all, d), BF16),
                   jax.ShapeDtypeStruct((t_all, d), F32), jax.ShapeDtypeStruct((t_all, d), F32),
                   jax.ShapeDtypeStruct((t_all, d), BF16)],
        compiler_params=_params(("arbitrary",)),
    )(u, u, o, x, mod, g_post, g_fpre, w_pool, pool_scale, w_out)


def _ffn_fwd(h2, w_g, w_u, w_d, x1, tgt, mod, g_post, seq, tm):
    t_all, d = x1.shape
    nt = seq // tm
    nk, ff, _ = w_g.shape

    def body(h_ref, wg_ref, wu_ref, wd_ref, x1_ref, t_ref, mod_ref, g_ref,
             a_ref, b_ref, fin_ref, dy_ref, df_ref, loss_ref, accb_ref, accg_ref, facc):
        i, k = pl.program_id(0), pl.program_id(1)

        @pl.when(k == 0)
        def _():
            facc[...] = jnp.zeros_like(facc)

        for c in range(FFN_ROW_CHUNKS):
            rows = slice(c * (tm // FFN_ROW_CHUNKS), (c + 1) * (tm // FFN_ROW_CHUNKS))
            hb = h_ref[rows, :]
            a = _dot_nt(hb, wg_ref[0])
            b = _dot_nt(hb, wu_ref[0])
            a_ref[0, rows, :] = a.astype(BF16)
            b_ref[0, rows, :] = b.astype(BF16)
            fin = ((a * _sigmoid(a)) * b).astype(BF16)
            fin_ref[0, rows, :] = fin
            facc[rows, :] += _dot(fin, wd_ref[0])

        @pl.when(k == nk - 1)
        def _():
            f = facc[...]
            r4 = _rms(f)
            n4 = f * r4
            gate = mod_ref[0, 5:6, :]
            g = g_ref[...]
            err = (x1_ref[...] + gate * (n4 * g)) - t_ref[...]
            dy = err * (1.0 / d)
            dy_ref[...] = dy

            @pl.when(i == 0)
            def _():
                loss_ref[...] = jnp.zeros_like(loss_ref)
                accg_ref[...] = jnp.zeros_like(accg_ref)

            @pl.when(i % nt == 0)
            def _():
                accb_ref[...] = jnp.zeros_like(accb_ref)

            loss_ref[...] += (0.5 / d) * jnp.sum(err * err)
            accb_ref[0, 0:1, :] += _colsum(dy * (n4 * g))
            accg_ref[0:1, :] += _colsum((dy * gate) * n4)
            dn4 = (dy * gate) * g
            df_ref[...] = _norm_bwd(dn4, n4, r4).astype(BF16)

    tok = lambda i, k: (i, 0)
    ktok = lambda i, k: (k, i, 0)
    kw = lambda i, k: (k, 0, 0)
    const2 = lambda i, k: (0, 0)
    return pl.pallas_call(
        body, name="ffn_fwd", grid=(t_all // tm, nk),
        in_specs=[pl.BlockSpec((tm, d), tok),
                  pl.BlockSpec((1, ff, d), kw), pl.BlockSpec((1, ff, d), kw), pl.BlockSpec((1, ff, d), kw),
                  pl.BlockSpec((tm, d), tok), pl.BlockSpec((tm, d), tok),
                  pl.BlockSpec((1, MOD_ROWS, d), lambda i, k: (i // nt, 0, 0)),
                  pl.BlockSpec((1, d), const2)],
        out_specs=[pl.BlockSpec((1, tm, ff), ktok)] * 3
        + [pl.BlockSpec((tm, d), tok), pl.BlockSpec((tm, d), tok),
           pl.BlockSpec((8, LANES), const2),
           pl.BlockSpec((1, 8, d), lambda i, k: (i // nt, 0, 0)),
           pl.BlockSpec((8, d), const2)],
        out_shape=[jax.ShapeDtypeStruct((nk, t_all, ff), BF16)] * 3
        + [jax.ShapeDtypeStruct((t_all, d), F32), jax.ShapeDtypeStruct((t_all, d), BF16),
           jax.ShapeDtypeStruct((8, LANES), F32),
           jax.ShapeDtypeStruct((t_all // seq, 8, d), F32),
           jax.ShapeDtypeStruct((8, d), F32)],
        scratch_shapes=[pltpu.VMEM((tm, d), F32)],
        compiler_params=_params(("arbitrary", "arbitrary")),
    )(h2, w_g, w_u, w_d, x1, tgt, mod, g_post)


def _ffn_bwd(df, a, b, w_d, w_g, w_u, x1, dy, mix, mod, g_fpre, g_mpost, seq, tm):
    t_all, d = x1.shape
    nt = seq // tm
    nk, ff, _ = w_g.shape

    def body(df_ref, a_ref, b_ref, wd_ref, wg_ref, wu_ref, x1_ref, dy_ref, mix_ref, mod_ref, gf_ref, gm_ref,
             da_ref, db_ref, dx1_ref, dmix_ref, accb_ref, accg_ref, hacc):
        i, k = pl.program_id(0), pl.program_id(1)

        @pl.when(k == 0)
        def _():
            hacc[...] = jnp.zeros_like(hacc)

        for c in range(FFN_ROW_CHUNKS):
            rows = slice(c * (tm // FFN_ROW_CHUNKS), (c + 1) * (tm // FFN_ROW_CHUNKS))
            dfin = _dot_nt(df_ref[rows, :], wd_ref[0])
            af = a_ref[0, rows, :].astype(F32)
            bf = b_ref[0, rows, :].astype(F32)
            sig = _sigmoid(af)
            da = ((dfin * bf) * (sig * (1.0 + af * (1.0 - sig)))).astype(BF16)
            db = (dfin * (af * sig)).astype(BF16)
            da_ref[0, rows, :] = da
            db_ref[0, rows, :] = db
            hacc[rows, :] += _dot(da, wg_ref[0]) + _dot(db, wu_ref[0])

        @pl.when(k == nk - 1)
        def _():
            @pl.when(i == 0)
            def _():
                accg_ref[...] = jnp.zeros_like(accg_ref)

            @pl.when(i % nt == 0)
            def _():
                accb_ref[...] = jnp.zeros_like(accb_ref)

            dh2 = hacc[...]
            x1 = x1_ref[...]
            r3 = _rms(x1)
            n3 = x1 * r3
            g3 = gf_ref[...]
            scale1 = 1.0 + mod_ref[0, 4:5, :]
            accb_ref[0, 0:1, :] += _colsum(dh2)
            accb_ref[0, 1:2, :] += _colsum(dh2 * (n3 * g3))
            accg_ref[0:1, :] += _colsum((dh2 * scale1) * n3)
            dx1 = dy_ref[...] + _norm_bwd((dh2 * scale1) * g3, n3, r3)
            dx1_ref[...] = dx1
            mix = mix_ref[...]
            r2 = _rms(mix)
            n2 = mix * r2
            g2 = gm_ref[...]
            gate = mod_ref[0, 2:3, :]
            accb_ref[0, 2:3, :] += _colsum(dx1 * (n2 * g2))
            accg_ref[1:2, :] += _colsum((dx1 * gate) * n2)
            dmix_ref[...] = _norm_bwd((dx1 * gate) * g2, n2, r2).astype(BF16)

    tok = lambda i, k: (i, 0)
    ktok = lambda i, k: (k, i, 0)
    kw = lambda i, k: (k, 0, 0)
    const2 = lambda i, k: (0, 0)
    return pl.pallas_call(
        body, name="ffn_bwd", grid=(t_all // tm, nk),
        in_specs=[pl.BlockSpec((tm, d), tok),
                  pl.BlockSpec((1, tm, ff), ktok), pl.BlockSpec((1, tm, ff), ktok),
                  pl.BlockSpec((1, ff, d), kw), pl.BlockSpec((1, ff, d), kw), pl.BlockSpec((1, ff, d), kw),
                  pl.BlockSpec((tm, d), tok), pl.BlockSpec((tm, d), tok), pl.BlockSpec((tm, d), tok),
                  pl.BlockSpec((1, MOD_ROWS, d), lambda i, k: (i // nt, 0, 0)),
                  pl.BlockSpec((1, d), const2), pl.BlockSpec((1, d), const2)],
        out_specs=[pl.BlockSpec((1, tm, ff), ktok)] * 2
        + [pl.BlockSpec((tm, d), tok), pl.BlockSpec((tm, d), tok),
           pl.BlockSpec((1, 8, d), lambda i, k: (i // nt, 0, 0)),
           pl.BlockSpec((8, d), const2)],
        out_shape=[jax.ShapeDtypeStruct((nk, t_all, ff), BF16)] * 2
        + [jax.ShapeDtypeStruct((t_all, d), F32), jax.ShapeDtypeStruct((t_all, d), BF16),
           jax.ShapeDtypeStruct((t_all // seq, 8, d), F32),
           jax.ShapeDtypeStruct((8, d), F32)],
        scratch_shapes=[pltpu.VMEM((tm, d), F32)],
        compiler_params=_params(("arbitrary", "arbitrary")),
    )(df, a, b, w_d, w_g, w_u, x1, dy, mix, mod, g_fpre, g_mpost)


def _mixer_bwd(dmix, w_out, pooled, w_pool, pool_scale, seq, tm):
    t_all, d = dmix.shape
    p = pooled.shape[1]
    ng = len(POOL_WINDOWS)

    def body(dm_ref, wo_ref, pooled_ref, wp_ref, ps_ref, do_ref, dpd_ref, dps_ref, dwp_ref):
        i = pl.program_id(0)

        @pl.when(i == 0)
        def _():
            dps_ref[...] = jnp.zeros_like(dps_ref)
            dwp_ref[...] = jnp.zeros_like(dwp_ref)

        dmixin = _dot_nt(dm_ref[...], wo_ref[...])
        do_ref[...] = dmixin[:, :p].astype(BF16)
        for g in range(ng):
            cols = slice(g * POOL_GROUP, (g + 1) * POOL_GROUP)
            dpool = dmixin[:, p + g * POOL_GROUP:p + (g + 1) * POOL_GROUP]
            pooled = pooled_ref[:, cols]
            wpg = wp_ref[g].astype(BF16)
            yg = _dot(pooled, wpg)
            dps_ref[0:1, cols] += _colsum(dpool * yg)
            dyg = (dpool * ps_ref[:, cols]).astype(BF16)
            dwp_ref[g] += _dot_tn(pooled, dyg)
            dpd_ref[:, cols] = _dot_nt(dyg, wpg)

    tok = lambda i: (i, 0)
    const2 = lambda i: (0, 0)
    const3 = lambda i: (0, 0, 0)
    return pl.pallas_call(
        body, name="mixer_bwd", grid=(t_all // tm,),
        in_specs=[pl.BlockSpec((tm, d), tok), pl.BlockSpec((d, d), const2), pl.BlockSpec((tm, p), tok),
                  pl.BlockSpec(w_pool.shape, const3), pl.BlockSpec((1, p), const2)],
        out_specs=[pl.BlockSpec((tm, p), tok), pl.BlockSpec((tm, p), tok),
                   pl.BlockSpec((8, p), const2), pl.BlockSpec(w_pool.shape, const3)],
        out_shape=[jax.ShapeDtypeStruct((t_all, p), BF16), jax.ShapeDtypeStruct((t_all, p), F32),
                   jax.ShapeDtypeStruct((8, p), F32), jax.ShapeDtypeStruct(w_pool.shape, F32)],
        compiler_params=_params(("arbitrary",)),
    )(dmix, w_out, pooled, w_pool, pool_scale)


def _attn_bwd(qn, k, kt, v, do, ltot, seq, tq, tk, order):
    t_all, w = qn.shape
    nb, nq, ndiag, nkb = t_all // seq, seq // tq, tq // tk, seq // tk
    assert ndiag % 2 == 0, "two key blocks per loop trip"
    rc = ATTN_ROW_CHUNK
    nh = HEADS_PER_BLOCK
    heads = range(nh)

    def body(q_ref, k_ref, kt_ref, v_ref, do_ref, l_ref, up_ref, bf_ref, dq_ref, dk_ref, dv_ref,
             z_buf, dw_buf, ls_buf, hl_buf, upto_buf, g_buf, gb_buf, before_buf, w_buf, dz_buf,
             totl_buf, totg_buf, rem_buf, preg_buf, qnt_buf, dot_buf, dq_t, dk_t, dv_t):
        i = pl.program_id(2)
        nblk = (i + 1) * ndiag

        @pl.when(i == 0)
        def _():
            dk_t[...] = jnp.zeros_like(dk_t)
            dv_t[...] = jnp.zeros_like(dv_t)

        lane = lax.broadcasted_iota(jnp.int32, (1, LANES), 1)
        row = lax.broadcasted_iota(jnp.int32, (rc, tk), 0)
        col = lax.broadcasted_iota(jnp.int32, (rc, tk), 1)
        first = lane < HEAD_DIM
        q2 = q_ref[...]
        do2 = do_ref[...]
        l2 = l_ref[...]
        qs = [jnp.where(first, q2, jnp.zeros_like(q2)), jnp.where(first, jnp.zeros_like(q2), q2)]
        dos = [jnp.where(first, do2, jnp.zeros_like(do2)), jnp.where(first, jnp.zeros_like(do2), do2)]
        qnt_buf[...] = q2.astype(F32).T.astype(BF16)
        dot_buf[...] = do2.astype(F32).T.astype(BF16)
        for h in heads:
            rem_buf[h] = jnp.where(first if h == 0 else ~first, l2, pltpu.roll(l2, HEAD_DIM, 1))
        preg_buf[...] = jnp.zeros_like(preg_buf)
        dq_t[...] = jnp.zeros_like(dq_t)
        w_buf[1] = jnp.zeros((nh * tq, tk), BF16)
        dz_buf[1] = jnp.zeros((nh * tq, tk), BF16)

        def causal(c, diag):
            return (col + diag * tk) < (row + c * rc)

        def scores(blk, slot):
            off = pl.multiple_of(blk * tk, tk)
            kj = k_ref[pl.ds(off, tk), :]
            vj = v_ref[pl.ds(off, tk), :]
            for h in heads:
                z_buf[slot, h] = _dot_nt(qs[h], kj)
                dw_buf[slot, h] = _dot_nt(dos[h], vj)

        def gradients(blk, slot):
            keys = pl.ds(pl.multiple_of(blk * tk, tk), tk)
            for h in heads:
                dims = slice(h * HEAD_DIM, (h + 1) * HEAD_DIM)
                queries = slice(h * tq, (h + 1) * tq)
                dq_t[dims, :] += _dot_nt(kt_ref[dims, keys], dz_buf[slot, queries, :])
                dk_t[blk, dims, :] += _dot(qnt_buf[dims, :], dz_buf[slot, queries, :])
                dv_t[blk, dims, :] += _dot(dot_buf[dims, :], w_buf[slot, queries, :])

        def softplus_stage(h, slot, diag):
            for c in range(tq // rc):
                rows = slice(c * rc, (c + 1) * rc)
                if _all_masked(c, diag, rc, tk):
                    hl_buf[h, rows, :] = jnp.zeros((rc, LOG_SUM_PASSES * tk), BF16)
                    continue
                nz = z_buf[slot, h, rows, :]
                l1 = jnp.minimum(nz, 0.0) - jnp.log(1.0 + jnp.exp(_neg_abs(nz)))
                if _some_masked(c, diag, rc, tk):
                    l1 = jnp.where(causal(c, diag), l1, 0.0)
                for s, part in enumerate(_split(l1)[:LOG_SUM_PASSES]):
                    hl_buf[h, rows, s * tk:(s + 1) * tk] = part
                ls_buf[h, rows, :] = l1 - nz
                totl_buf[h, rows, :] = _row_sums(l1)

        def weights_stage(h, slot, diag):
            for c in range(tq // rc):
                rows = slice(c * rc, (c + 1) * rc)
                stacked = slice(h * tq + c * rc, h * tq + (c + 1) * rc)
                if _all_masked(c, diag, rc, tk):
                    w_buf[slot, stacked, :] = jnp.zeros((rc, tk), BF16)
                    gb_buf[h, rows, :] = jnp.zeros((rc, tk), BF16)
                    continue
                wgt = jnp.exp(ls_buf[h, rows, :] + (_across(rem_buf[h, rows, :], tk) - upto_buf[h, rows, :]))
                if _some_masked(c, diag, rc, tk):
                    wgt = jnp.where(causal(c, diag), wgt, 0.0)
                w_buf[slot, stacked, :] = wgt.astype(BF16)
                g = wgt * dw_buf[slot, h, rows, :]
                g_buf[h, rows, :] = g
                gb_buf[h, rows, :] = g.astype(BF16)
                totg_buf[h, rows, :] = _row_sums(g)
                rem_buf[h, rows, :] -= totl_buf[h, rows, :]

        def dscore_stage(h, slot, diag):
            for c in range(tq // rc):
                rows = slice(c * rc, (c + 1) * rc)
                stacked = slice(h * tq + c * rc, h * tq + (c + 1) * rc)
                if _all_masked(c, diag, rc, tk):
                    dz_buf[slot, stacked, :] = jnp.zeros((rc, tk), BF16)
                    continue
                sig = jnp.exp(ls_buf[h, rows, :])
                g = g_buf[h, rows, :]
                dnz = sig * ((before_buf[h, rows, :] + _across(preg_buf[h, rows, :], tk)) + g) - g
                if _some_masked(c, diag, rc, tk):
                    dnz = jnp.where(causal(c, diag), dnz, 0.0)
                dz_buf[slot, stacked, :] = dnz.astype(BF16)
                preg_buf[h, rows, :] += totg_buf[h, rows, :]

        def position(blk, slot, diag, prefetch):
            if prefetch:
                scores(blk + 1, 1 - slot)
            for h in heads:
                softplus_stage(h, slot, diag)
                upto_buf[h] = _dot(hl_buf[h], up_ref[...])
            gradients(jnp.maximum(blk - 1, 0), 1 - slot)
            for h in heads:
                weights_stage(h, slot, diag)
                before_buf[h] = _dot(gb_buf[h], bf_ref[...])
            for h in heads:
                dscore_stage(h, slot, diag)

        scores(0, 0)

        def trip(jj, carry):
            for u in range(2):
                position(2 * jj + u, u, None, True)
            return carry

        lax.fori_loop(0, (i * ndiag) // 2, trip, 0)
        for d in range(ndiag):
            position(i * ndiag + d, d % 2, d, d < ndiag - 1)
        gradients(nblk - 1, 1)
        dq_ref[...] = (dq_t[...].T * NEG_QK_SCALE).astype(BF16)

        @pl.when(i == nq - 1)
        def _():
            for blk in range(nkb):
                dk_ref[blk * tk:(blk + 1) * tk, :] = dk_t[blk].T.astype(BF16)
                dv_ref[blk * tk:(blk + 1) * tk, :] = dv_t[blk].T.astype(BF16)

    qmap = lambda b, hp, i: (b * nq + i, hp)
    kmap = lambda b, hp, i: (b, hp)
    const = lambda b, hp, i: (0, 0)
    return pl.pallas_call(
        body, name="attn_bwd", grid=(nb, w // LANES, nq),
        in_specs=[pl.BlockSpec((tq, LANES), qmap), pl.BlockSpec((seq, LANES), kmap),
                  pl.BlockSpec((LANES, seq), lambda b, hp, i: (hp, b)), pl.BlockSpec((seq, LANES), kmap),
                  pl.BlockSpec((tq, LANES), qmap), pl.BlockSpec((tq, LANES), qmap),
                  pl.BlockSpec((LOG_SUM_PASSES * tk, tk), const), pl.BlockSpec((tk, tk), const)],
        out_specs=[pl.BlockSpec((tq, LANES), qmap), pl.BlockSpec((seq, LANES), kmap), pl.BlockSpec((seq, LANES), kmap)],
        out_shape=[jax.ShapeDtypeStruct((t_all, w), BF16)] * 3,
        scratch_shapes=[pltpu.VMEM((2, nh, tq, tk), F32), pltpu.VMEM((2, nh, tq, tk), F32),
                        pltpu.VMEM((nh, tq, tk), F32), pltpu.VMEM((nh, tq, LOG_SUM_PASSES * tk), BF16),
                        pltpu.VMEM((nh, tq, tk), F32), pltpu.VMEM((nh, tq, tk), F32),
                        pltpu.VMEM((nh, tq, tk), BF16), pltpu.VMEM((nh, tq, tk), F32),
                        pltpu.VMEM((2, nh * tq, tk), BF16), pltpu.VMEM((2, nh * tq, tk), BF16),
                        pltpu.VMEM((nh, tq, LANES), F32), pltpu.VMEM((nh, tq, LANES), F32),
                        pltpu.VMEM((nh, tq, LANES), F32), pltpu.VMEM((nh, tq, LANES), F32),
                        pltpu.VMEM((LANES, tq), BF16), pltpu.VMEM((LANES, tq), BF16),
                        pltpu.VMEM((LANES, tq), F32), pltpu.VMEM((nkb, LANES, tk), F32),
                        pltpu.VMEM((nkb, LANES, tk), F32)],
        compiler_params=_params(("arbitrary", "arbitrary", "arbitrary")),
    )(qn, k, kt, v, do, ltot, _tri_matrix(tk, "upto")[:LOG_SUM_PASSES * tk] + order.astype(BF16),
      _tri_matrix(tk, "before")[:tk])


def _inproj_bwd(dq, dk, dv, dpd, x, dx1, mod, g_pre, w_in, seq, tm):
    t_all, d = x.shape
    nt = seq // tm
    p = dq.shape[1]

    def body(dq_ref, dk_ref, dv_ref, dpd_ref, halo_ref, x_ref, dx1_ref, mod_ref, g_ref, w_ref,
             gx_ref, du_ref, accb_ref, accg_ref):
        i = pl.program_id(0)
        it = i % nt

        @pl.when(i == 0)
        def _():
            accg_ref[...] = jnp.zeros_like(accg_ref)

        @pl.when(it == 0)
        def _():
            accb_ref[...] = jnp.zeros_like(accb_ref)

        dpd = dpd_ref[...]
        pos = it * tm + lax.broadcasted_iota(jnp.int32, (tm, 1), 0)
        cnts = _window_counts(pos)
        halo = jnp.where(it == nt - 1, 0.0, halo_ref[...])
        scaled = []
        halos = []
        for g, win in enumerate(POOL_WINDOWS):
            cols = slice(g * POOL_GROUP, (g + 1) * POOL_GROUP)
            scaled.append(dpd[:, cols] / cnts[g])
            halos.append(halo[:, cols] / float(win))
        ext = jnp.concatenate([jnp.concatenate(scaled, axis=1), jnp.concatenate(halos, axis=1)], axis=0)
        sums = _window_sums(ext, tm, 0, True)
        du = (jnp.concatenate(sums, axis=1) - dpd).astype(BF16)
        du_ref[...] = du
        g1 = g_ref[...]
        scale1 = 1.0 + mod_ref[0, 1:2, :]
        for c in range(ROW_CHUNKS):
            rows = slice(c * (tm // ROW_CHUNKS), (c + 1) * (tm // ROW_CHUNKS))
            dh1 = (_dot_nt(dq_ref[rows, :], w_ref[0]) + _dot_nt(dk_ref[rows, :], w_ref[1])
                   + _dot_nt(dv_ref[rows, :], w_ref[2]) + _dot_nt(du_ref[rows, :], w_ref[3]))
            xf = x_ref[rows, :]
            r1 = _rms(xf)
            n1 = xf * r1
            accb_ref[0, 0:1, :] += _colsum(dh1)
            accb_ref[0, 1:2, :] += _colsum(dh1 * (n1 * g1))
            accg_ref[0:1, :] += _colsum((dh1 * scale1) * n1)
            gx_ref[rows, :] = dx1_ref[rows, :] + _norm_bwd((dh1 * scale1) * g1, n1, r1)

    tok = lambda i: (i, 0)
    const2 = lambda i: (0, 0)
    hb = tm // HALO
    last = t_all // HALO - 1
    return pl.pallas_call(
        body, name="inproj_bwd", grid=(t_all // tm,),
        in_specs=[pl.BlockSpec((tm, p), tok), pl.BlockSpec((tm, p), tok), pl.BlockSpec((tm, p), tok),
                  pl.BlockSpec((tm, p), tok),
                  pl.BlockSpec((HALO, p), lambda i: (jnp.minimum((i + 1) * hb, last), 0)),
                  pl.BlockSpec((tm, d), tok), pl.BlockSpec((tm, d), tok),
                  pl.BlockSpec((1, MOD_ROWS, d), lambda i: (i // nt, 0, 0)),
                  pl.BlockSpec((1, d), const2),
                  pl.BlockSpec((N_CHIPS, d, p), lambda i: (0, 0, 0))],
        out_specs=[pl.BlockSpec((tm, d), tok), pl.BlockSpec((tm, p), tok),
                   pl.BlockSpec((1, 8, d), lambda i: (i // nt, 0, 0)),
                   pl.BlockSpec((8, d), const2)],
        out_shape=[jax.ShapeDtypeStruct((t_all, d), F32), jax.ShapeDtypeStruct((t_all, p), BF16),
                   jax.ShapeDtypeStruct((t_all // seq, 8, d), F32),
                   jax.ShapeDtypeStruct((8, d), F32)],
        compiler_params=_params(("arbitrary",)),
    )(dq, dk, dv, dpd, dpd, x, dx1, mod, g_pre, w_in)


def _tn_matmul(x, ys, nk, bt, name, after=()):
    t_all = x.shape[-2]
    m = x.shape[-1]
    ny = len(ys)
    nt = t_all // bt

    def spec(arr):
        if arr.ndim == 3:
            return pl.BlockSpec((1, bt, arr.shape[-1]), lambda k, t: (k, t, 0))
        return pl.BlockSpec((bt, arr.shape[-1]), lambda k, t: (t, 0))

    def tile(ref):
        return ref[0] if len(ref.shape) == 3 else ref[...]

    def body(*refs):
        outs = refs[1 + ny + len(after):]
        x_ref, y_refs, o_refs, h_refs = refs[0], refs[1:1 + ny], outs[:ny], outs[ny:]
        t = pl.program_id(1)
        xt = tile(x_ref)
        for y_ref, o_ref, h_ref in zip(y_refs, o_refs, h_refs):
            part = _dot_tn(xt, tile(y_ref))

            @pl.when(t == 0)
            def _(o_ref=o_ref, part=part):
                o_ref[0] = part

            @pl.when(t > 0)
            def _(o_ref=o_ref, part=part):
                o_ref[0] += part

            @pl.when(t == nt - 1)
            def _(o_ref=o_ref, h_ref=h_ref):
                h_ref[0] = o_ref[0].astype(BF16)

    out_specs = [pl.BlockSpec((1, m, y.shape[-1]), lambda k, t: (k, 0, 0)) for y in ys]
    out = pl.pallas_call(
        body, name=name, grid=(nk, nt),
        in_specs=[spec(x)] + [spec(y) for y in ys] + [_ANY] * len(after),
        out_specs=out_specs * 2,
        out_shape=[jax.ShapeDtypeStruct((nk, m, y.shape[-1]), dt) for dt in (F32, BF16) for y in ys],
        compiler_params=_params(("arbitrary", "arbitrary")),
    )(x, *ys, *after)
    return out[:ny], out[ny:]


def _tn_matmul_stacked(x, ys, bt, name, after=()):
    t_all, m = x.shape
    n = ys[0].shape[1]
    ny = len(ys)
    nt = t_all // bt

    def body(*refs):
        x_ref, y_refs, (o_ref, h_ref) = refs[0], refs[1:1 + ny], refs[1 + ny + len(after):]
        t = pl.program_id(0)
        xt = x_ref[...]

        @pl.when(t == 0)
        def _():
            o_ref[...] = jnp.zeros_like(o_ref)

        for j, y_ref in enumerate(y_refs):
            o_ref[j] += _dot_tn(xt, y_ref[...])

        @pl.when(t == nt - 1)
        def _():
            h_ref[...] = o_ref[...].astype(BF16)

    whole = pl.BlockSpec((ny, m, n), lambda t: (0, 0, 0))
    return pl.pallas_call(
        body, name=name, grid=(nt,),
        in_specs=[pl.BlockSpec((bt, m), lambda t: (t, 0))] + [pl.BlockSpec((bt, n), lambda t: (t, 0))] * ny
        + [_ANY] * len(after),
        out_specs=[whole, whole],
        out_shape=[jax.ShapeDtypeStruct((ny, m, n), F32), jax.ShapeDtypeStruct((ny, m, n), BF16)],
        compiler_params=_params(("arbitrary",)),
    )(x, *ys, *after)


def _cond_fwd(c_all, w_q, b_q, bn):
    nrow, d = c_all.shape
    ncol = w_q.shape[1]

    def body(c_ref, w_ref, b_ref, sc_ref, mod_ref):
        cf = c_ref[...]
        sc = cf * _sigmoid(cf)
        sc_ref[...] = sc
        shi, slo = _split(sc)
        whi, wlo = _split(w_ref[...])
        mod_ref[...] = (_dot(shi, whi) + _dot(shi, wlo) + _dot(slo, whi)) + b_ref[...]

    return pl.pallas_call(
        body, name="cond_fwd", grid=(ncol // bn,),
        in_specs=[pl.BlockSpec((nrow, d), lambda n: (0, 0)), pl.BlockSpec((d, bn), lambda n: (0, n)),
                  pl.BlockSpec((1, bn), lambda n: (0, n))],
        out_specs=[pl.BlockSpec((nrow, d), lambda n: (0, 0)), pl.BlockSpec((nrow, bn), lambda n: (0, n))],
        out_shape=[jax.ShapeDtypeStruct((nrow, d), F32), jax.ShapeDtypeStruct((nrow, ncol), F32)],
        compiler_params=_params(("arbitrary",)),
    )(c_all, w_q, b_q)


def _cond_bwd(sc_all, dmod_q, bn):
    nrow, d = sc_all.shape
    ncol = dmod_q.shape[1]

    def body(sc_ref, dm_ref, gw_ref):
        shi, slo = _split(sc_ref[...])
        dhi, dlo = _split(dm_ref[...])
        gw_ref[...] = _dot_tn(shi, dhi) + _dot_tn(shi, dlo) + _dot_tn(slo, dhi)

    return pl.pallas_call(
        body, name="cond_bwd", grid=(ncol // bn,),
        in_specs=[pl.BlockSpec((nrow, d), lambda n: (0, 0)), pl.BlockSpec((nrow, bn), lambda n: (0, n))],
        out_specs=pl.BlockSpec((d, bn), lambda n: (0, n)),
        out_shape=jax.ShapeDtypeStruct((d, ncol), F32),
        compiler_params=_params(("arbitrary",)),
    )(sc_all, dmod_q)


def _row_block(rows, cols, budget=1 << 18):
    best = None
    for br in range(8, rows + 1, 8):
        if rows % br == 0 and br * cols <= budget:
            best = br
    return best if best is not None else rows


def _adam_math(w, g, m, v):
    c1 = 1.0 - ADAM_B1 ** ADAM_STEP
    c2 = 1.0 - ADAM_B2 ** ADAM_STEP
    m2 = ADAM_B1 * m + (1.0 - ADAM_B1) * g
    v2 = ADAM_B2 * v + (1.0 - ADAM_B2) * (g * g)
    return -ADAM_LR * ((m2 / c1) / (jnp.sqrt(v2 / c2) + ADAM_EPS) + ADAM_WD * w), m2, v2


def _small_updates(summed, params):
    n = len(params)

    def body(s_ref, *refs):
        ins, outs = refs[:3 * n], refs[3 * n:]
        for p, (_, _, _, pick) in enumerate(params):
            w_ref, m_ref, v_ref = ins[3 * p:3 * p + 3]
            g = pick(s_ref)
            delta, m2, v2 = _adam_math(w_ref[...], g, m_ref[...], v_ref[...])
            for o_ref, val in zip(outs[4 * p:4 * p + 4], (g, delta, m2, v2)):
                o_ref[...] = val

    out = pl.pallas_call(
        body, name="adamw_small",
        out_shape=[jax.ShapeDtypeStruct(w.shape, F32) for w, _, _, _ in params for _ in range(4)],
        compiler_params=pltpu.CompilerParams(vmem_limit_bytes=VMEM_LIMIT),
    )(summed, *[t for w, m, v, _ in params for t in (w, m, v)])
    return [tuple(out[4 * p:4 * p + 4]) for p in range(n)]


def _adamw(w, g, m, v, name, after=()):
    rows, cols = w.shape
    br = _row_block(rows, cols)

    def body(*refs):
        w_ref, g_ref, m_ref, v_ref = refs[:4]
        d_ref, nm_ref, nv_ref = refs[4 + len(after):]
        d_ref[...], nm_ref[...], nv_ref[...] = _adam_math(w_ref[...], g_ref[...], m_ref[...], v_ref[...])

    blk = pl.BlockSpec((br, cols), lambda i: (i, 0))
    return pl.pallas_call(
        body, name=name, grid=(rows // br,),
        in_specs=[blk] * 4 + [_ANY] * len(after), out_specs=[blk] * 3,
        out_shape=[jax.ShapeDtypeStruct((rows, cols), F32)] * 3,
        compiler_params=_params(("arbitrary",)),
    )(w, g, m, v, *after)


def _all_gather(x_shard, name):
    m_per, n = x_shard.shape

    def body(x_ref, out_ref, send_sems, recv_sems, local_sem):
        me = _position()

        def rows(pos):
            return out_ref.at[pl.ds((4 * pos[0] + 2 * pos[1] + pos[2]) * m_per, m_per), :]

        mine = pltpu.make_async_copy(x_ref, rows(me), local_sem)
        mine.start()
        sends = [pltpu.make_async_remote_copy(
            src_ref=x_ref, dst_ref=rows(me), send_sem=send_sems.at[r], recv_sem=recv_sems.at[r],
            device_id=_flipped(me, flip), device_id_type=MESH) for r, flip in enumerate(_FLIPS)]
        for cp in sends:
            cp.start()
        for r, flip in enumerate(_FLIPS):
            peer = _flipped(me, flip)
            pltpu.make_async_remote_copy(
                src_ref=rows(peer), dst_ref=rows(peer), send_sem=send_sems.at[r], recv_sem=recv_sems.at[r],
                device_id=peer, device_id_type=MESH).wait_recv()
        for cp in sends:
            cp.wait_send()
        mine.wait()

    return pl.pallas_call(
        body, name=name,
        out_shape=jax.ShapeDtypeStruct((N_DEV * m_per, n), x_shard.dtype),
        in_specs=[pl.BlockSpec(memory_space=pltpu.VMEM)],
        out_specs=pl.BlockSpec(memory_space=pltpu.VMEM),
        scratch_shapes=[pltpu.SemaphoreType.DMA((7,)), pltpu.SemaphoreType.DMA((7,)), pltpu.SemaphoreType.DMA],
        compiler_params=pltpu.CompilerParams(vmem_limit_bytes=VMEM_LIMIT),
    )(x_shard)


_ANY = pl.BlockSpec(memory_space=pl.ANY)


def _place_quarters(place, quarters):
    steps = 2

    def body(place_ref, *refs):
        n = len(refs) // 2
        for w_ref, o_ref in zip(refs[:n], refs[n:]):
            o_ref[0] = w_ref[...].astype(BF16)

    return pl.pallas_call(
        body, name="place_quarters",
        grid_spec=pltpu.PrefetchScalarGridSpec(
            num_scalar_prefetch=1, grid=(steps,),
            in_specs=[pl.BlockSpec((q.shape[0] // steps, q.shape[1]), lambda r, place_ref: (r, 0)) for q in quarters],
            out_specs=[pl.BlockSpec((1, q.shape[0] // steps, q.shape[1]), lambda r, place_ref: (place_ref[0], r, 0))
                       for q in quarters]),
        out_shape=[jax.ShapeDtypeStruct((N_CHIPS,) + q.shape, BF16) for q in quarters],
        compiler_params=_params(("arbitrary",)),
    )(place, *quarters)


_HBM = pl.BlockSpec(memory_space=pltpu.HBM)
_SEM = pl.BlockSpec(memory_space=pltpu.SEMAPHORE)
_EFFECT = pltpu.SideEffectType.DATAFLOW_SIDE_EFFECTING


def _quarter_halves(shapes, a, which):
    hr = shapes[a][0] // 2
    return pl.ds(which * hr, hr)


def _gather_start(placed, after, tag):
    n = len(placed)
    m = len(after)
    shapes = [b.shape[1:] for b in placed]

    def body(*refs):
        g_refs = refs[:n]
        send_sems, recv_sems = refs[n + m], refs[n + m + 1]
        token = refs[2 * n + m + 2]
        x, y, c = _position()
        chips = [(1 - x, y), (x, 1 - y), (1 - x, 1 - y)]
        mine = 2 * x + y
        for a in range(n):
            ref = g_refs[a].at[mine, _quarter_halves(shapes, a, c), :]
            for p in range(3):
                pltpu.make_async_remote_copy(
                    src_ref=ref, dst_ref=ref, send_sem=send_sems.at[3 * a + p], recv_sem=recv_sems.at[3 * a + p],
                    device_id=(*chips[p], c), device_id_type=MESH).start()
        token[...] = jnp.zeros_like(token)

    out = pl.pallas_call(
        body, name="gather_start_" + tag,
        out_shape=(pltpu.SemaphoreType.DMA((3 * n,)), pltpu.SemaphoreType.DMA((3 * n,)),
                   *[pltpu.HBM(b.shape, b.dtype) for b in placed], jax.ShapeDtypeStruct((8, LANES), F32)),
        in_specs=[_HBM] * n + [_ANY] * m,
        out_specs=(_SEM, _SEM, *[_HBM] * n, pl.BlockSpec(memory_space=pltpu.VMEM)),
        input_output_aliases={a: 2 + a for a in range(n)},
        compiler_params=pltpu.CompilerParams(has_side_effects=_EFFECT),
    )(*[pltpu.with_memory_space_constraint(b, pltpu.HBM) for b in placed], *after)
    return out[0], out[1], list(out[2:2 + n]), out[2 + n]


def _gather_wait(send_sems, recv_sems, thru, after, tag):
    n = len(thru)
    shapes = [b.shape[1:] for b in thru]

    def body(*refs):
        g_refs = refs[:n]
        send_sems, recv_sems = refs[n], refs[n + 1]
        x, y, c = _position()
        chips = [(1 - x, y), (x, 1 - y), (1 - x, 1 - y)]
        mine = 2 * x + y
        for a in range(n):
            rows = _quarter_halves(shapes, a, c)
            for p, (cx, cy) in enumerate(chips):
                copy = pltpu.make_async_remote_copy(
                    src_ref=g_refs[a].at[mine, rows, :], dst_ref=g_refs[a].at[2 * cx + cy, rows, :],
                    send_sem=send_sems.at[3 * a + p], recv_sem=recv_sems.at[3 * a + p],
                    device_id=(cx, cy, c), device_id_type=MESH)
                copy.wait_send()
                copy.wait_recv()

    return pl.pallas_call(
        body, name="gather_wait_" + tag,
        out_shape=[pltpu.HBM(b.shape, b.dtype) for b in thru],
        in_specs=[_HBM] * n + [_SEM, _SEM, _ANY], out_specs=[_HBM] * n,
        input_output_aliases={a: a for a in range(n)},
        compiler_params=pltpu.CompilerParams(has_side_effects=_EFFECT),
    )(*thru, send_sems, recv_sems, after)


def _gather_forward(bufs, tag):
    n = len(bufs)
    shapes = [b.shape[1:] for b in bufs]

    def body(*refs):
        g_refs = refs[n:2 * n]
        send_sems, recv_sems = refs[2 * n:]
        x, y, c = _position()
        chips = [(1 - x, y), (x, 1 - y), (1 - x, 1 - y)]

        def over_d2d(a, p, which):
            cx, cy = chips[p]
            ref = g_refs[a].at[2 * cx + cy, _quarter_halves(shapes, a, which), :]
            return pltpu.make_async_remote_copy(
                src_ref=ref, dst_ref=ref, send_sem=send_sems.at[3 * a + p], recv_sem=recv_sems.at[3 * a + p],
                device_id=(x, y, 1 - c), device_id_type=MESH)

        sends = [over_d2d(a, p, c) for a in range(n) for p in range(3)]
        for cp in sends:
            cp.start()
        for a in range(n):
            for p in range(3):
                over_d2d(a, p, 1 - c).wait_recv()
        for cp in sends:
            cp.wait_send()

    return pl.pallas_call(
        body, name="gather_forward_" + tag,
        out_shape=[jax.ShapeDtypeStruct(b.shape, BF16) for b in bufs],
        in_specs=[_ANY] * n, out_specs=[_ANY] * n,
        input_output_aliases={a: a for a in range(n)},
        scratch_shapes=[pltpu.SemaphoreType.DMA((3 * n,)), pltpu.SemaphoreType.DMA((3 * n,))],
    )(*bufs)


_FLIPS = [(fx, fy, fc) for fx in (0, 1) for fy in (0, 1) for fc in (0, 1)][1:]


def _flipped(pos, flip):
    return tuple(1 - p if f else p for p, f in zip(pos, flip))


def _direct_gather_start(slots):
    def body(s_ref, send_sems, recv_sems, thru, token):
        me = _position()
        mine = s_ref.at[4 * me[0] + 2 * me[1] + me[2]]
        for r, flip in enumerate(_FLIPS):
            pltpu.make_async_remote_copy(
                src_ref=mine, dst_ref=mine, send_sem=send_sems.at[r], recv_sem=recv_sems.at[r],
                device_id=_flipped(me, flip), device_id_type=MESH).start()
        token[...] = jnp.zeros_like(token)

    return pl.pallas_call(
        body, name="small_gather_start",
        out_shape=(pltpu.SemaphoreType.DMA((len(_FLIPS),)), pltpu.SemaphoreType.DMA((len(_FLIPS),)),
                   pltpu.HBM(slots.shape, slots.dtype), jax.ShapeDtypeStruct((8, LANES), F32)),
        in_specs=[_HBM], out_specs=(_SEM, _SEM, _HBM, pl.BlockSpec(memory_space=pltpu.VMEM)),
        input_output_aliases={0: 2},
        compiler_params=pltpu.CompilerParams(has_side_effects=_EFFECT),
    )(pltpu.with_memory_space_constraint(slots, pltpu.HBM))


def _direct_gather_wait(send_sems, recv_sems, slots, after):
    def body(s_ref, send_sems, recv_sems, after_ref, out_ref):
        me = _position()
        mine = s_ref.at[4 * me[0] + 2 * me[1] + me[2]]
        for r, flip in enumerate(_FLIPS):
            peer = _flipped(me, flip)
            copy = pltpu.make_async_remote_copy(
                src_ref=mine, dst_ref=s_ref.at[4 * peer[0] + 2 * peer[1] + peer[2]],
                send_sem=send_sems.at[r], recv_sem=recv_sems.at[r], device_id=peer, device_id_type=MESH)
            copy.wait_send()
            copy.wait_recv()

    return pl.pallas_call(
        body, name="small_gather_wait",
        out_shape=pltpu.HBM(slots.shape, slots.dtype),
        in_specs=[_HBM, _SEM, _SEM, _ANY], out_specs=_HBM,
        input_output_aliases={0: 0},
        compiler_params=pltpu.CompilerParams(has_side_effects=_EFFECT),
    )(slots, send_sems, recv_sems, after)


def _sibling_split_start(bufs, parts, nparts, after, tag):
    n, m = len(bufs), len(after)

    def body(*refs):
        b_refs = refs[:n]
        send_sems, recv_sems = refs[n + m], refs[n + m + 1]
        token = refs[2 * n + m + 2]
        x, y, c = _position()
        for r, ref in enumerate(parts(b_refs, x, y, c)):
            pltpu.make_async_remote_copy(
                src_ref=ref, dst_ref=ref, send_sem=send_sems.at[r], recv_sem=recv_sems.at[r],
                device_id=(x, y, 1 - c), device_id_type=MESH).start()
        token[...] = jnp.zeros_like(token)

    out = pl.pallas_call(
        body, name="sibling_start_" + tag,
        out_shape=(pltpu.SemaphoreType.DMA((nparts,)), pltpu.SemaphoreType.DMA((nparts,)),
                   *[pltpu.HBM(b.shape, b.dtype) for b in bufs], jax.ShapeDtypeStruct((8, LANES), F32)),
        in_specs=[_HBM] * n + [_ANY] * m,
        out_specs=(_SEM, _SEM, *[_HBM] * n, pl.BlockSpec(memory_space=pltpu.VMEM)),
        input_output_aliases={a: 2 + a for a in range(n)},
        compiler_params=pltpu.CompilerParams(has_side_effects=_EFFECT),
    )(*[pltpu.with_memory_space_constraint(b, pltpu.HBM) for b in bufs], *after)
    return out[0], out[1], list(out[2:2 + n]), out[2 + n]


def _sibling_split_wait(send_sems, recv_sems, bufs, parts, after, tag):
    n = len(bufs)

    def body(*refs):
        b_refs = refs[:n]
        send_sems, recv_sems = refs[n], refs[n + 1]
        x, y, c = _position()
        mine, theirs = parts(b_refs, x, y, c), parts(b_refs, x, y, 1 - c)
        for r, (src, dst) in enumerate(zip(mine, theirs)):
            copy = pltpu.make_async_remote_copy(
                src_ref=src, dst_ref=dst, send_sem=send_sems.at[r], recv_sem=recv_sems.at[r],
                device_id=(x, y, 1 - c), device_id_type=MESH)
            copy.wait_send()
            copy.wait_recv()

    return pl.pallas_call(
        body, name="sibling_wait_" + tag,
        out_shape=[pltpu.HBM(b.shape, b.dtype) for b in bufs],
        in_specs=[_HBM] * n + [_SEM, _SEM, _ANY], out_specs=[_HBM] * n,
        input_output_aliases={a: a for a in range(n)},
        compiler_params=pltpu.CompilerParams(has_side_effects=_EFFECT),
    )(*bufs, send_sems, recv_sems, after)


def _sibling_exchange(grads, tag):
    n = len(grads)
    shapes = [g.shape for g in grads]

    def body(*refs):
        g_refs, x_refs = refs[:n], refs[n:2 * n]
        send_sems, recv_sems = refs[2 * n:]
        x, y, c = _position()
        copies = []
        for a in range(n):
            hr = shapes[a][1] // 2
            cp = pltpu.make_async_remote_copy(
                src_ref=g_refs[a].at[:, pl.ds((1 - c) * hr, hr), :], dst_ref=x_refs[a],
                send_sem=send_sems.at[a], recv_sem=recv_sems.at[a],
                device_id=(x, y, 1 - c), device_id_type=MESH)
            cp.start()
            copies.append(cp)
        for cp in copies:
            cp.wait()

    return pl.pallas_call(
        body, name="grad_sibling_exchange_" + tag,
        out_shape=[jax.ShapeDtypeStruct((g.shape[0], g.shape[1] // 2, g.shape[2]), g.dtype) for g in grads],
        in_specs=[_ANY] * n, out_specs=[_ANY] * n,
        scratch_shapes=[pltpu.SemaphoreType.DMA((n,)), pltpu.SemaphoreType.DMA((n,))],
    )(*grads)


def _sibling_exchange_start(grads, tag):
    n = len(grads)
    lands = [lax.empty((g.shape[0], g.shape[1] // 2, g.shape[2]), g.dtype) for g in grads]

    def body(*refs):
        g_refs, x_refs = refs[:n], refs[n:2 * n]
        send_sems, recv_sems = refs[2 * n], refs[2 * n + 1]
        token = refs[4 * n + 2]
        x, y, c = _position()
        for a in range(n):
            hr = grads[a].shape[1] // 2
            pltpu.make_async_remote_copy(
                src_ref=g_refs[a].at[:, pl.ds((1 - c) * hr, hr), :], dst_ref=x_refs[a],
                send_sem=send_sems.at[a], recv_sem=recv_sems.at[a],
                device_id=(x, y, 1 - c), device_id_type=MESH).start()
        token[...] = jnp.zeros_like(token)

    both = list(grads) + lands
    out = pl.pallas_call(
        body, name="grad_sibling_exchange_start_" + tag,
        out_shape=(pltpu.SemaphoreType.DMA((n,)), pltpu.SemaphoreType.DMA((n,)),
                   *[pltpu.HBM(b.shape, b.dtype) for b in both], jax.ShapeDtypeStruct((8, LANES), F32)),
        in_specs=[_HBM] * (2 * n),
        out_specs=(_SEM, _SEM, *[_HBM] * (2 * n), pl.BlockSpec(memory_space=pltpu.VMEM)),
        input_output_aliases={a: 2 + a for a in range(2 * n)},
        compiler_params=pltpu.CompilerParams(has_side_effects=_EFFECT),
    )(*[pltpu.with_memory_space_constraint(b, pltpu.HBM) for b in both])
    return out[0], out[1], list(out[2:2 + n]), list(out[2 + n:2 + 2 * n]), out[2 + 2 * n]


def _sibling_exchange_wait(send_sems, recv_sems, grads, lands, after, tag):
    n = len(grads)

    def body(*refs):
        g_refs, x_refs = refs[:n], refs[n:2 * n]
        send_sems, recv_sems = refs[2 * n], refs[2 * n + 1]
        x, y, c = _position()
        for a in range(n):
            hr = grads[a].shape[1] // 2
            copy = pltpu.make_async_remote_copy(
                src_ref=g_refs[a].at[:, pl.ds((1 - c) * hr, hr), :], dst_ref=x_refs[a],
                send_sem=send_sems.at[a], recv_sem=recv_sems.at[a],
                device_id=(x, y, 1 - c), device_id_type=MESH)
            copy.wait_send()
            copy.wait_recv()

    both = list(grads) + list(lands)
    out = pl.pallas_call(
        body, name="grad_sibling_exchange_wait_" + tag,
        out_shape=[pltpu.HBM(b.shape, b.dtype) for b in both],
        in_specs=[_HBM] * (2 * n) + [_SEM, _SEM, _ANY], out_specs=[_HBM] * (2 * n),
        input_output_aliases={a: a for a in range(2 * n)},
        compiler_params=pltpu.CompilerParams(has_side_effects=_EFFECT),
    )(*both, send_sems, recv_sems, after)
    return list(out[n:])


def _chip_sums(core, grads, theirs, tag):
    n = len(grads)

    def body(core_ref, *refs):
        g_refs, t_refs, o_refs = refs[:n], refs[n:2 * n], refs[2 * n:]
        for g_ref, t_ref, o_ref in zip(g_refs, t_refs, o_refs):
            o_ref[...] = (g_ref[...] + t_ref[...].astype(F32)).astype(BF16)

    in_specs = [pl.BlockSpec((1, g.shape[1] // 2, g.shape[2]), lambda k, core_ref: (k, core_ref[0], 0)) for g in grads]
    in_specs += [pl.BlockSpec((1,) + t.shape[1:], lambda k, core_ref: (k, 0, 0)) for t in theirs]
    return pl.pallas_call(
        body, name="grad_chip_sums_" + tag,
        grid_spec=pltpu.PrefetchScalarGridSpec(
            num_scalar_prefetch=1, grid=(N_CHIPS,), in_specs=in_specs,
            out_specs=[pl.BlockSpec((1,) + t.shape[1:], lambda k, core_ref: (k, 0, 0)) for t in theirs]),
        out_shape=[jax.ShapeDtypeStruct(t.shape, BF16) for t in theirs],
        compiler_params=_params(("arbitrary",)),
    )(core, *grads, *theirs)


def _chip_exchange_start(sums, after, tag):
    n = len(sums)
    m = len(after)
    lands = [lax.empty((3,) + s.shape[1:], BF16) for s in sums]

    def body(*refs):
        s_refs, y_refs = refs[:n], refs[n:2 * n]
        send_sems, recv_sems = refs[2 * n + m], refs[2 * n + m + 1]
        token = refs[4 * n + m + 2]
        x, y, c = _position()
        chips = [(1 - x, y), (x, 1 - y), (1 - x, 1 - y)]
        for a in range(n):
            for p, (cx, cy) in enumerate(chips):
                pltpu.make_async_remote_copy(
                    src_ref=s_refs[a].at[2 * cx + cy], dst_ref=y_refs[a].at[p],
                    send_sem=send_sems.at[3 * a + p], recv_sem=recv_sems.at[3 * a + p],
                    device_id=(cx, cy, c), device_id_type=MESH).start()
        token[...] = jnp.zeros_like(token)

    both = list(sums) + lands
    out = pl.pallas_call(
        body, name="grad_chip_exchange_start_" + tag,
        out_shape=(pltpu.SemaphoreType.DMA((3 * n,)), pltpu.SemaphoreType.DMA((3 * n,)),
                   *[pltpu.HBM(b.shape, b.dtype) for b in both], jax.ShapeDtypeStruct((8, LANES), F32)),
        in_specs=[_HBM] * (2 * n) + [_ANY] * m,
        out_specs=(_SEM, _SEM, *[_HBM] * (2 * n), pl.BlockSpec(memory_space=pltpu.VMEM)),
        input_output_aliases={a: 2 + a for a in range(2 * n)},
        compiler_params=pltpu.CompilerParams(has_side_effects=_EFFECT),
    )(*[pltpu.with_memory_space_constraint(b, pltpu.HBM) for b in both], *after)
    return out[0], out[1], list(out[2:2 + n]), list(out[2 + n:2 + 2 * n]), out[2 + 2 * n]


def _chip_exchange_wait(send_sems, recv_sems, sums, lands, after, tag):
    n = len(sums)

    def body(*refs):
        s_refs, y_refs = refs[:n], refs[n:2 * n]
        send_sems, recv_sems = refs[2 * n], refs[2 * n + 1]
        x, y, c = _position()
        chips = [(1 - x, y), (x, 1 - y), (1 - x, 1 - y)]
        for a in range(n):
            for p, (cx, cy) in enumerate(chips):
                copy = pltpu.make_async_remote_copy(
                    src_ref=s_refs[a].at[2 * cx + cy], dst_ref=y_refs[a].at[p],
                    send_sem=send_sems.at[3 * a + p], recv_sem=recv_sems.at[3 * a + p],
                    device_id=(cx, cy, c), device_id_type=MESH)
                copy.wait_send()
                copy.wait_recv()

    both = list(sums) + list(lands)
    out = pl.pallas_call(
        body, name="grad_chip_exchange_wait_" + tag,
        out_shape=[pltpu.HBM(b.shape, b.dtype) for b in both],
        in_specs=[_HBM] * (2 * n) + [_SEM, _SEM, _ANY], out_specs=[_HBM] * (2 * n),
        input_output_aliases={a: a for a in range(2 * n)},
        compiler_params=pltpu.CompilerParams(has_side_effects=_EFFECT),
    )(*both, send_sems, recv_sems, after)
    return list(out[:n]), list(out[n:])


def _total_sums(place, sums, parts, after, tag):
    n = len(parts)
    m = len(after)
    steps = 2

    def body(place_ref, *refs):
        for s_ref, y_ref, o_ref in zip(refs[:n], refs[n:2 * n], refs[2 * n + m:]):
            o_ref[0] = ((s_ref[0].astype(F32) + y_ref[0].astype(F32)) + y_ref[1].astype(F32)) + y_ref[2].astype(F32)

    def step_rows(pt):
        return pt.shape[1] // steps

    in_specs = [pl.BlockSpec((1, step_rows(s), s.shape[2]), lambda r, place_ref: (place_ref[0], r, 0)) for s in sums]
    in_specs += [pl.BlockSpec((3, step_rows(pt), pt.shape[2]), lambda r, place_ref: (0, r, 0)) for pt in parts]
    in_specs += [_ANY] * m
    return pl.pallas_call(
        body, name="grad_total_sums_" + tag,
        grid_spec=pltpu.PrefetchScalarGridSpec(
            num_scalar_prefetch=1, grid=(steps,), in_specs=in_specs,
            out_specs=[pl.BlockSpec((1, step_rows(pt), pt.shape[2]), lambda r, place_ref: (place_ref[1], r, 0))
                       for pt in parts]),
        out_shape=[jax.ShapeDtypeStruct((2,) + pt.shape[1:], F32) for pt in parts],
        compiler_params=_params(("arbitrary",)),
    )(place, *sums, *parts, *after)


def _sibling_share(halves, tag):
    n = len(halves)

    def body(*refs):
        f_refs = refs[n:2 * n]
        send_sems, recv_sems = refs[2 * n:]
        x, y, c = _position()
        copies = []
        for a in range(n):
            cp = pltpu.make_async_remote_copy(
                src_ref=f_refs[a].at[c], dst_ref=f_refs[a].at[c], send_sem=send_sems.at[a], recv_sem=recv_sems.at[a],
                device_id=(x, y, 1 - c), device_id_type=MESH)
            cp.start()
            copies.append(cp)
        for a, cp in enumerate(copies):
            cp.wait_send()
            pltpu.make_async_remote_copy(
                src_ref=f_refs[a].at[1 - c], dst_ref=f_refs[a].at[1 - c], send_sem=send_sems.at[a],
                recv_sem=recv_sems.at[a], device_id=(x, y, c), device_id_type=MESH).wait_recv()

    return pl.pallas_call(
        body, name="grad_sibling_share_" + tag,
        out_shape=[jax.ShapeDtypeStruct(h.shape, F32) for h in halves],
        in_specs=[_ANY] * n, out_specs=[_ANY] * n,
        input_output_aliases={a: a for a in range(n)},
        scratch_shapes=[pltpu.SemaphoreType.DMA((n,)), pltpu.SemaphoreType.DMA((n,))],
    )(*halves)


def _group_sum(stacked, nrow, name):
    total, n = stacked.shape
    groups = total // nrow

    def body(g_ref, o_ref):
        acc = g_ref[0:nrow, :]
        for grp in range(1, groups):
            acc = acc + g_ref[grp * nrow:(grp + 1) * nrow, :]
        o_ref[...] = acc

    return pl.pallas_call(
        body, name=name,
        out_shape=jax.ShapeDtypeStruct((nrow, n), F32),
        compiler_params=pltpu.CompilerParams(vmem_limit_bytes=VMEM_LIMIT),
    )(stacked)


def _local_step(xt, tgt, mod, gains, w_pool, pool_scale, w_in, later_weights, on_ffn_grads, after_mixer_bwd,
                on_small_grads, seq):
    g_mpre, g_mpost, g_fpre, g_fpost = gains
    d = xt.shape[1]
    tm, tq = min(TOKEN_TILE, seq), min(ATTN_TILE, seq)

    h1, qn, k, v, u, kt, vt = _prenorm_proj(xt, mod, g_mpre, w_in, seq, tm)
    tk = min(ATTN_KEY_TILE, tq // 2)
    o, ltot = _attn_fwd(qn, k, vt, seq, tq, tk)
    w_out, order, ffn_weights = later_weights(o)
    w_out2 = w_out.reshape(d, d)
    pooled, mixin, mix, x1, h2 =_mixer_post(u, o, xt, mod, g_mpost, g_fpre + order, w_pool, pool_scale, w_out2, seq, tm)
    w_g, w_u, w_d = ffn_weights(h2)
    a, b, fin, dy, df, loss_blk, accb4, accg4 = _ffn_fwd(h2, w_g, w_u, w_d, x1, tgt, mod, g_fpost, seq, tm)
    da, db, dx1, dmix, accb5, accg5 = _ffn_bwd(df, a, b, w_d, w_g, w_u, x1, dy, mix, mod, g_fpre, g_mpost, seq, tm)
    bt = min(GRAD_TOKEN_TILE, xt.shape[0])
    bt_one = min(2 * GRAD_TOKEN_TILE, xt.shape[0])
    (g_g,), (g_g16,) = _tn_matmul(da, [h2], w_g.shape[0], bt_one, "grad_w_gate")
    (g_u,), (g_u16,) = _tn_matmul(db, [h2], w_u.shape[0], bt_one, "grad_w_up")
    (g_d,), (g_d16,) = _tn_matmul(fin, [df], w_d.shape[0], bt_one, "grad_w_down")
    token = on_ffn_grads([g_g, g_u, g_d], [g_g16, g_u16, g_d16])
    do, dpd, dps, dwp = _mixer_bwd(dmix, w_out2, pooled, w_pool, pool_scale + token, seq, tm)
    order = after_mixer_bwd(do)
    dq, dk, dv = _attn_bwd(qn, k, kt, v, do, ltot, seq, tq, tk, order)
    gx, du, accb8, accg8 = _inproj_bwd(dq, dk, dv, dpd, xt, dx1, mod, g_mpre, w_in, seq, tm)

    dmod = jnp.stack([accb8[:, 0], accb8[:, 1], accb5[:, 2], accb5[:, 0], accb5[:, 1], accb4[:, 0]], axis=1)
    dgain = jnp.stack([accg8[0], accg5[1], accg5[0], accg4[0]], axis=0)
    behind = on_small_grads(loss_blk, dmod, dgain, dps[0:1], dwp)
    g_in, g_in16 = _tn_matmul_stacked(h1, [dq, dk, dv, du], bt, "grad_w_in", behind)
    g_out, g_out16 = [parts[0].reshape(w_out.shape)
                      for parts in _tn_matmul(mixin, [dmix], 1, bt_one, "grad_w_out", behind)]
    grads = [g_in, g_out, g_g, g_u, g_d]
    grads16 = [g_in16, g_out16, g_g16, g_u16, g_d16]
    return gx, grads, grads16


def kernel(x, c, w_cond, b_cond, g_mix_pre, g_mix_post, w_in, w_pool, pool_scale, w_out, g_ffn_pre, g_ffn_post, w_gate, w_up, w_down, loss_target, m_w_cond, m_b_cond, m_g_mix_pre, m_g_mix_post, m_w_in, m_w_pool, m_pool_scale, m_w_out, m_g_ffn_pre, m_g_ffn_post, m_w_gate, m_w_up, m_w_down, v_w_cond, v_b_cond, v_g_mix_pre, v_g_mix_post, v_w_in, v_w_pool, v_pool_scale, v_w_out, v_g_ffn_pre, v_g_ffn_post, v_w_gate, v_w_up, v_w_down):
    xi, yi, ci = _position()
    chip = 2 * xi + yi
    dev = 4 * xi + 2 * yi + ci
    nb, seq, d = x.shape
    t_all = nb * seq
    xt = x.reshape(t_all, d)
    tgt = loss_target.reshape(t_all, d)
    ncol = w_cond.shape[2]
    pw = pool_scale.shape[1]

    place = jnp.stack([chip, ci]).astype(jnp.int32)
    turned = lambda t: jnp.swapaxes(t[0], 0, 1)
    placed = _place_quarters(place, [w_in[0], w_out[0], turned(w_gate), turned(w_up), w_down[0]])
    in_sems = _gather_start(placed[:1], [], "in")

    c_pad = jnp.concatenate([c, jnp.zeros((8 - nb, d), F32)], axis=0) + in_sems[3][0:1, 0:1]
    c_all = _all_gather(c_pad, "gather_c").reshape(N_DEV, 8, d)[:, :nb].reshape(N_DEV * nb, d)
    b_q = lax.dynamic_slice(b_cond, (0, chip * ncol), (1, ncol))
    sc_all, mod_q = _cond_fwd(c_all, w_cond[0], b_q, 512)
    mod_parts = _all_gather(mod_q, "gather_mod").reshape(N_DEV, N_DEV * nb, ncol)
    mod_rows = lax.dynamic_slice(mod_parts, (0, dev * nb, 0), (N_DEV, nb, ncol))[0::2]
    mod = jnp.transpose(mod_rows, (1, 0, 2)).reshape(nb, N_MOD, d)
    mod = jnp.concatenate([mod, jnp.zeros((nb, MOD_ROWS - N_MOD, d), F32)], axis=1)

    (w_in_all,) = _gather_forward(_gather_wait(*in_sems[:3], mod, "in"), "in")
    send_sems, recv_sems, in_flight, token = _gather_start(placed[1:], [mod, w_in_all], "rest")
    mod = mod + token[0:1, 0:1]

    def later_weights(after):
        waited = _gather_wait(send_sems, recv_sems, in_flight, after, "rest")
        (w_out_all,) = _gather_forward(waited[:1], "out")
        shapes = [b.shape[1:] for b in waited[1:]]

        def parts(refs, px, py, which):
            return [refs[a].at[2 * cx + cy, _quarter_halves(shapes, a, which), :]
                    for a in range(len(refs)) for cx, cy in [(1 - px, py), (px, 1 - py), (1 - px, 1 - py)]]

        forward = _sibling_split_start(waited[1:], parts, 3 * len(shapes), [w_out_all], "ffn_weights")
        finish = lambda after2: _sibling_split_wait(*forward[:3], parts, after2, "ffn_weights")
        return w_out_all, forward[3][0:1, 0:1], finish

    ffn_split = []

    ffn_sibling = []

    def on_ffn_grads(ffn_grads, ffn_grads16):
        ffn_sibling.extend(_sibling_exchange_start(ffn_grads16, "ffn"))
        ffn_sibling.append(ffn_grads)
        return ffn_sibling[4][0:1, 0:1]

    def after_mixer_bwd(do):
        theirs = _sibling_exchange_wait(*ffn_sibling[:4], do, "ffn")
        ffn_split.extend(_chip_exchange_start(_chip_sums(place[1:], ffn_sibling[5], theirs, "ffn"), [], "ffn"))
        return ffn_split[4][0:1, 0:1]

    wp_rows = w_pool[0].size // d
    loss_row = 2 * N_MOD + 4 + 1
    pad_rows = 24 - (loss_row + 1)
    prow = 24 + wp_rows
    small_split = []

    def on_small_grads(loss_blk, dmod, dgain, dps, dwp):
        payload = jnp.concatenate([
            dmod.reshape(nb * N_MOD, d), dgain,
            jnp.concatenate([dps, jnp.zeros((1, d - pw), F32)], axis=1),
            jnp.concatenate([loss_blk[0:1], jnp.zeros((1, d - LANES), F32)], axis=1),
            jnp.zeros((pad_rows, d), F32),
            jnp.concatenate(jnp.split(dwp.reshape(-1, dwp.shape[-1]), d // dwp.shape[-1], axis=0), axis=1)], axis=0)
        slots = lax.dynamic_update_slice(lax.empty((N_DEV, prow, d), F32), payload[None], (dev, 0, 0))
        small_split.extend(_direct_gather_start(slots))
        return [small_split[3]]

    gains = (g_mix_pre, g_mix_post, g_ffn_pre, g_ffn_post)
    gx, grads, grads16 = _local_step(
        xt, tgt, mod, gains, w_pool[0], pool_scale, w_in_all, later_weights, on_ffn_grads, after_mixer_bwd,
        on_small_grads, seq)

    sums_ffn, parts_ffn = _chip_exchange_wait(*ffn_split[:4], gx, "ffn")
    gathered = _direct_gather_wait(*small_split[:3], grads16[1]).reshape(N_DEV * prow, d)
    summed = _group_sum(gathered, prow, "small_device_sum")
    loss = summed[loss_row, 0]
    dmod_all = gathered.reshape(N_DEV, prow, d)[:, :nb * N_MOD].reshape(N_DEV * nb, N_MOD * d)
    dmod_q = lax.dynamic_slice(dmod_all, (0, chip * ncol), (N_DEV * nb, ncol))
    g_w_cond = _cond_bwd(sc_all, dmod_q, 512)
    first_gain = 2 * N_MOD

    theirs = _sibling_exchange(grads16[:2], "mix")
    mix_split = _chip_exchange_start(_chip_sums(place[1:], grads[:2], theirs, "mix"), [gathered], "mix")
    unfold = lambda halves: [g.reshape(2 * g.shape[1], g.shape[2]) for g in halves]
    share_parts = lambda refs, px, py, which: [r.at[which] for r in refs]
    halves_ffn = _total_sums(place, sums_ffn, parts_ffn, [mix_split[4]], "ffn")
    share = _sibling_split_start(halves_ffn, share_parts, len(halves_ffn), [], "share_ffn")

    results = {}

    def update(name, w2, g2, m2, v2, shape, after=()):
        delta, new_m, new_v = _adamw(w2, g2, m2, v2, "adamw_" + name, after)
        back = (lambda t: jnp.swapaxes(t, 0, 1)[None]) if shape is None else (lambda t: t.reshape(shape))
        results[name] = [back(t) for t in (g2, delta, new_m, new_v)]
        return delta

    done_cond = update("w_cond", w_cond[0], g_w_cond, m_w_cond[0], v_w_cond[0], w_cond.shape, [share[3]])
    g_ffn = unfold(_sibling_split_wait(*share[:3], share_parts, done_cond, "share_ffn"))
    done = [update("w_gate", turned(w_gate), g_ffn[0], turned(m_w_gate), turned(v_w_gate), None),
            update("w_up", turned(w_up), g_ffn[1], turned(m_w_up), turned(v_w_up), None),
            update("w_down", w_down[0], g_ffn[2], m_w_down[0], v_w_down[0], w_down.shape)]

    gain_row = lambda r: (lambda s: s[first_gain + r:first_gain + r + 1, :])
    small = [
        ("b_cond", (b_cond, m_b_cond, v_b_cond), (N_MOD, d), lambda s: s[0:N_MOD, :] + s[N_MOD:2 * N_MOD, :]),
        ("g_mix_pre", (g_mix_pre, m_g_mix_pre, v_g_mix_pre), (1, d), gain_row(0)),
        ("g_mix_post", (g_mix_post, m_g_mix_post, v_g_mix_post), (1, d), gain_row(1)),
        ("g_ffn_pre", (g_ffn_pre, m_g_ffn_pre, v_g_ffn_pre), (1, d), gain_row(2)),
        ("g_ffn_post", (g_ffn_post, m_g_ffn_post, v_g_ffn_post), (1, d), gain_row(3)),
        ("pool_scale", (pool_scale, m_pool_scale, v_pool_scale), (1, pw),
         lambda s: s[first_gain + 4:first_gain + 5, 0:pw]),
        ("w_pool", (w_pool, m_w_pool, v_w_pool), (wp_rows * d // w_pool.shape[-1], w_pool.shape[-1]),
         lambda s: jnp.concatenate([s[24:24 + wp_rows, j * w_pool.shape[-1]:(j + 1) * w_pool.shape[-1]]
                                    for j in range(d // w_pool.shape[-1])], axis=0)),
    ]
    updated = _small_updates(summed, [tuple(t.reshape(flat) for t in wmv) + (pick,) for _, wmv, flat, pick in small])
    for (name, wmv, _, _), quad in zip(small, updated):
        results[name] = [t.reshape(wmv[0].shape) for t in quad]

    sums_mix, parts_mix = _chip_exchange_wait(*mix_split[:4], done[-1], "mix")
    g_mix = unfold(_sibling_share(_total_sums(place, sums_mix, parts_mix, done, "mix"), "mix"))
    update("w_in", w_in[0], g_mix[0], m_w_in[0], v_w_in[0], w_in.shape)
    update("w_out", w_out[0], g_mix[1], m_w_out[0], v_w_out[0], w_out.shape)

    names = ("w_cond", "b_cond", "g_mix_pre", "g_mix_post", "w_in", "w_pool", "pool_scale", "w_out",
             "g_ffn_pre", "g_ffn_post", "w_gate", "w_up", "w_down")
    outs = [results[name][part] for part in range(4) for name in names]
    return (loss, gx.reshape(x.shape), *outs)
```

```python
import jax
import jax.numpy as jnp
import numpy as np
from jax import lax
from jax.experimental import pallas as pl
from jax.experimental.pallas import tpu as pltpu

F32 = jnp.float32
BF16 = jnp.bfloat16
MESH = pl.DeviceIdType.MESH

EPS = 1e-6
HEAD_DIM = 64
HEADS_PER_BLOCK = 2
LANES = 128
NEG_QK_SCALE = -0.125
POOL_WINDOWS = (2, 4, 8, 16)
POOL_GROUP = 128
HALO = 16
N_MOD = 6
MOD_ROWS = 8
N_CHIPS = 4
N_DEV = 8
VMEM_LIMIT = 56 * 1024 * 1024

ADAM_LR = 0.001
ADAM_B1 = 0.9
ADAM_B2 = 0.999
ADAM_EPS = 1e-08
ADAM_WD = 0.01
ADAM_STEP = 10

TOKEN_TILE = 512
GRAD_TOKEN_TILE = 2048
FFN_ROW_CHUNKS = 2
ROW_CHUNKS = 2
ATTN_TILE = 512
ATTN_KEY_TILE = 256
ATTN_ROW_CHUNK = 32
LOG_SUM_PASSES = 1


def _dot(a, b):
    return jnp.dot(a, b, preferred_element_type=F32)


def _dot_nt(a, b):
    return lax.dot_general(a, b, (((1,), (1,)), ((), ())), preferred_element_type=F32)


def _dot_tn(a, b):
    return lax.dot_general(a, b, (((0,), (0,)), ((), ())), preferred_element_type=F32)


def _split(v):
    hi = v.astype(BF16)
    lo = (v - hi.astype(F32)).astype(BF16)
    return hi, lo


def _rms(v):
    return lax.rsqrt(jnp.mean(v * v, axis=-1, keepdims=True) + EPS)


def _norm_bwd(dn, n, r):
    return r * (dn - n * jnp.mean(dn * n, axis=-1, keepdims=True))


def _sigmoid(v):
    return 0.5 * jnp.tanh(0.5 * v) + 0.5


def _colsum(v):
    return jnp.sum(v, axis=0, keepdims=True)


def _params(sem=None):
    return pltpu.CompilerParams(dimension_semantics=sem, vmem_limit_bytes=VMEM_LIMIT)


def _position():
    return lax.axis_index("x"), lax.axis_index("y"), lax.axis_index("c")


def _prenorm_proj(x, mod, g_pre, w_in, seq, tm):
    t_all, d = x.shape
    nt = seq // tm
    p = w_in.shape[2]

    def body(x_ref, mod_ref, g_ref, w_ref, h_ref, q_ref, k_ref, v_ref, u_ref, kt_ref, vt_ref):
        for c in range(ROW_CHUNKS):
            rows = slice(c * (tm // ROW_CHUNKS), (c + 1) * (tm // ROW_CHUNKS))
            xf = x_ref[rows, :]
            n = xf * _rms(xf)
            h = (n * g_ref[...]) * (1.0 + mod_ref[0, 1:2, :]) + mod_ref[0, 0:1, :]
            hb = h.astype(BF16)
            h_ref[rows, :] = hb
            q_ref[rows, :] = (_dot(hb, w_ref[0]) * NEG_QK_SCALE).astype(BF16)
            kf = _dot(hb, w_ref[1])
            vf = _dot(hb, w_ref[2])
            k_ref[rows, :] = kf.astype(BF16)
            v_ref[rows, :] = vf.astype(BF16)
            kt_ref[:, rows] = kf.T.astype(BF16)
            vt_ref[:, rows] = vf.T.astype(BF16)
            u_ref[rows, :] = _dot(hb, w_ref[3])

    tok = lambda i: (i, 0)
    tok_t = lambda i: (0, i)
    return pl.pallas_call(
        body, name="prenorm_proj", grid=(t_all // tm,),
        in_specs=[pl.BlockSpec((tm, d), tok),
                  pl.BlockSpec((1, MOD_ROWS, d), lambda i: (i // nt, 0, 0)),
                  pl.BlockSpec((1, d), lambda i: (0, 0)),
                  pl.BlockSpec((N_CHIPS, d, p), lambda i: (0, 0, 0))],
        out_specs=[pl.BlockSpec((tm, d), tok)] + [pl.BlockSpec((tm, p), tok)] * 4 + [pl.BlockSpec((p, tm), tok_t)] * 2,
        out_shape=[jax.ShapeDtypeStruct((t_all, d), BF16)] + [jax.ShapeDtypeStruct((t_all, p), BF16)] * 3
        + [jax.ShapeDtypeStruct((t_all, p), F32)] + [jax.ShapeDtypeStruct((p, t_all), BF16)] * 2,
        compiler_params=_params(("arbitrary",)),
    )(x, mod, g_pre, w_in)


def _tri_matrix(tk, kind):
    j = np.arange(2 * tk)[:, None] % tk
    s = np.arange(tk)[None, :]
    return jnp.asarray({"after": j > s, "upto": j <= s, "before": j < s}[kind], dtype=BF16)


def _neg_abs(v):
    bits = lax.bitcast_convert_type(v, jnp.int32) | jnp.int32(-2 ** 31)
    return lax.bitcast_convert_type(bits, F32)


def _row_sums(v):
    return jnp.broadcast_to(jnp.sum(v, axis=-1, keepdims=True), (v.shape[0], LANES))


def _across(v, n):
    return jnp.concatenate([v] * (n // LANES), axis=1)


def _all_masked(c, diag, rc, tk):
    return diag is not None and diag * tk >= (c + 1) * rc - 1


def _some_masked(c, diag, rc, tk):
    return diag is not None and diag * tk + tk - 1 >= c * rc


def _attn_fwd(qn, k, vt, seq, tq, tk):
    t_all, w = qn.shape
    nb, nq, ndiag = t_all // seq, seq // tq, tq // tk
    assert ndiag % 2 == 0, "two key blocks per loop trip"
    rc = ATTN_ROW_CHUNK
    heads = range(HEADS_PER_BLOCK)

    def body(q_ref, k_ref, vt_ref, tri_ref, o_ref, l_ref,
             z_buf, ls_buf, hl_buf, aft_buf, w_buf, tot_buf, acc_t, run_buf):
        i = pl.program_id(2)
        nblk = (i + 1) * ndiag
        lane = lax.broadcasted_iota(jnp.int32, (1, LANES), 1)
        row = lax.broadcasted_iota(jnp.int32, (rc, tk), 0)
        col = lax.broadcasted_iota(jnp.int32, (rc, tk), 1)
        first = lane < HEAD_DIM
        q2 = q_ref[...]
        qs = [jnp.where(first, q2, jnp.zeros_like(q2)), jnp.where(first, jnp.zeros_like(q2), q2)]
        acc_t[...] = jnp.zeros_like(acc_t)
        run_buf[...] = jnp.zeros_like(run_buf)
        w_buf[1] = jnp.zeros((HEADS_PER_BLOCK, tq, tk), BF16)

        def causal(c, diag):
            return (col + diag * tk) < (row + c * rc)

        def scores(blk, slot):
            kj = k_ref[pl.ds(pl.multiple_of(blk * tk, tk), tk), :]
            for h in heads:
                z_buf[slot, h] = _dot_nt(qs[h], kj)

        def values(blk, slot):
            keys = pl.ds(pl.multiple_of(blk * tk, tk), tk)
            for h in heads:
                dims = slice(h * HEAD_DIM, (h + 1) * HEAD_DIM)
                acc_t[dims, :] += _dot_nt(vt_ref[dims, keys], w_buf[slot, h])

        def softplus_stage(h, slot, diag):
            for c in range(tq // rc):
                rows = slice(c * rc, (c + 1) * rc)
                if _all_masked(c, diag, rc, tk):
                    hl_buf[h, rows, :] = jnp.zeros((rc, LOG_SUM_PASSES * tk), BF16)
                    tot_buf[h, rows, :] = jnp.zeros((rc, LANES), F32)
                    continue
                nz = z_buf[slot, h, rows, :]
                l1 = jnp.minimum(nz, 0.0) - jnp.log(1.0 + jnp.exp(_neg_abs(nz)))
                if _some_masked(c, diag, rc, tk):
                    l1 = jnp.where(causal(c, diag), l1, 0.0)
                for s, part in enumerate(_split(l1)[:LOG_SUM_PASSES]):
                    hl_buf[h, rows, s * tk:(s + 1) * tk] = part
                ls_buf[h, rows, :] = l1 - nz
                tot_buf[h, rows, :] = _row_sums(l1)

        def weights_stage(h, slot, diag):
            for c in range(tq // rc):
                rows = slice(c * rc, (c + 1) * rc)
                if _all_masked(c, diag, rc, tk):
                    w_buf[slot, h, rows, :] = jnp.zeros((rc, tk), BF16)
                    continue
                wgt = jnp.exp((ls_buf[h, rows, :] + aft_buf[h, rows, :]) + _across(run_buf[h, rows, :], tk))
                if _some_masked(c, diag, rc, tk):
                    wgt = jnp.where(causal(c, diag), wgt, 0.0)
                w_buf[slot, h, rows, :] = wgt.astype(BF16)
                run_buf[h, rows, :] += tot_buf[h, rows, :]

        def position(blk, slot, diag):
            scores(jnp.maximum(blk - 1, 0), 1 - slot)
            for h in heads:
                softplus_stage(h, slot, diag)
                aft_buf[h] = _dot(hl_buf[h], tri_ref[...])
            values(jnp.minimum(blk + 1, nblk - 1), 1 - slot)
            for h in heads:
                weights_stage(h, slot, diag)

        scores(nblk - 1, 0)
        for p in range(ndiag):
            position(nblk - 1 - p, p % 2, ndiag - 1 - p)

        def trip(jj, carry):
            for u in range(2):
                position(i * ndiag - 1 - 2 * jj - u, u, None)
            return carry

        lax.fori_loop(0, (i * ndiag) // 2, trip, 0)
        values(0, 1)
        o_ref[...] = acc_t[...].T.astype(BF16)
        l_ref[...] = jnp.where(first, run_buf[0], run_buf[1])

    qmap = lambda b, hp, i: (b * nq + i, hp)
    nh = HEADS_PER_BLOCK
    return pl.pallas_call(
        body, name="attn_fwd", grid=(nb, w // LANES, nq),
        in_specs=[pl.BlockSpec((tq, LANES), qmap), pl.BlockSpec((seq, LANES), lambda b, hp, i: (b, hp)),
                  pl.BlockSpec((LANES, seq), lambda b, hp, i: (hp, b)),
                  pl.BlockSpec((LOG_SUM_PASSES * tk, tk), lambda b, hp, i: (0, 0))],
        out_specs=[pl.BlockSpec((tq, LANES), qmap), pl.BlockSpec((tq, LANES), qmap)],
        out_shape=[jax.ShapeDtypeStruct((t_all, w), BF16), jax.ShapeDtypeStruct((t_all, w), F32)],
        scratch_shapes=[pltpu.VMEM((2, nh, tq, tk), F32), pltpu.VMEM((nh, tq, tk), F32),
                        pltpu.VMEM((nh, tq, LOG_SUM_PASSES * tk), BF16), pltpu.VMEM((nh, tq, tk), F32),
                        pltpu.VMEM((2, nh, tq, tk), BF16), pltpu.VMEM((nh, tq, LANES), F32),
                        pltpu.VMEM((LANES, tq), F32), pltpu.VMEM((nh, tq, LANES), F32)],
        compiler_params=_params(("arbitrary", "arbitrary", "arbitrary")),
    )(qn, k, vt, _tri_matrix(tk, "after")[:LOG_SUM_PASSES * tk])


def _window_sums(ext, rows, offset, forward):
    r = lax.broadcasted_iota(jnp.int32, (rows, rows + HALO), 0)
    e = lax.broadcasted_iota(jnp.int32, (rows, rows + HALO), 1)
    hi, lo = _split(ext)
    out = []
    for g, win in enumerate(POOL_WINDOWS):
        if forward:
            band = (e >= r) & (e < r + win)
        else:
            band = (e <= r + offset) & (e > r + offset - win)
        bm = band.astype(BF16)
        cols = slice(g * POOL_GROUP, (g + 1) * POOL_GROUP)
        out.append(_dot(bm, hi[:, cols]) + _dot(bm, lo[:, cols]))
    return out


def _window_counts(pos):
    return [jnp.minimum(pos + 1, win).astype(F32) for win in POOL_WINDOWS]


def _mixer_post(u, o, x, mod, g_post, g_fpre, w_pool, pool_scale, w_out, seq, tm):
    t_all, d = x.shape
    nt = seq // tm
    p = u.shape[1]

    def body(u_ref, halo_ref, o_ref, x_ref, mod_ref, gp_ref, gf_ref, wp_ref, ps_ref, wo_ref,
             pooled_ref, mixin_ref, mix_ref, x1_ref, h2_ref):
        it = pl.program_id(0) % nt
        uf = u_ref[...]
        halo = jnp.where(it == 0, 0.0, halo_ref[...])
        ext = jnp.concatenate([halo, uf], axis=0)
        pos = it * tm + lax.broadcasted_iota(jnp.int32, (tm, 1), 0)
        sums = _window_sums(ext, tm, HALO, False)
        cnts = _window_counts(pos)
        pools = []
        for g in range(len(POOL_WINDOWS)):
            cols = slice(g * POOL_GROUP, (g + 1) * POOL_GROUP)
            pooled = (sums[g] / cnts[g] - uf[:, cols]).astype(BF16)
            pooled_ref[:, cols] = pooled
            yg = _dot(pooled, wp_ref[g].astype(BF16))
            pools.append((yg * ps_ref[:, cols]).astype(BF16))
        mixin_ref[...] = jnp.concatenate([o_ref[...]] + pools, axis=1)
        for c in range(ROW_CHUNKS):
            rows = slice(c * (tm // ROW_CHUNKS), (c + 1) * (tm // ROW_CHUNKS))
            mix = _dot(mixin_ref[rows, :], wo_ref[...])
            mix_ref[rows, :] = mix
            n2 = mix * _rms(mix)
            x1 = x_ref[rows, :] + mod_ref[0, 2:3, :] * (n2 * gp_ref[...])
            x1_ref[rows, :] = x1
            n3 = x1 * _rms(x1)
            h2 = (n3 * gf_ref[...]) * (1.0 + mod_ref[0, 4:5, :]) + mod_ref[0, 3:4, :]
            h2_ref[rows, :] = h2.astype(BF16)

    tok = lambda i: (i, 0)
    const2 = lambda i: (0, 0)
    hb = tm // HALO
    return pl.pallas_call(
        body, name="mixer_post", grid=(t_all // tm,),
        in_specs=[pl.BlockSpec((tm, p), tok),
                  pl.BlockSpec((HALO, p), lambda i: (jnp.maximum(i * hb - 1, 0), 0)),
                  pl.BlockSpec((tm, p), tok),
                  pl.BlockSpec((tm, d), tok),
                  pl.BlockSpec((1, MOD_ROWS, d), lambda i: (i // nt, 0, 0)),
                  pl.BlockSpec((1, d), const2), pl.BlockSpec((1, d), const2),
                  pl.BlockSpec(w_pool.shape, lambda i: (0, 0, 0)),
                  pl.BlockSpec((1, p), const2),
                  pl.BlockSpec((d, d), const2)],
        out_specs=[pl.BlockSpec((tm, p), tok), pl.BlockSpec((tm, d), tok), pl.BlockSpec((tm, d), tok),
                   pl.BlockSpec((tm, d), tok), pl.BlockSpec((tm, d), tok)],
        out_shape=[jax.ShapeDtypeStruct((t_all, p), BF16), jax.ShapeDtypeStruct((t_all, d), BF16),
                   jax.ShapeDtypeStruct((t_all, d), F32), jax.ShapeDtypeStruct((t_all, d), F32),
                   jax.ShapeDtypeStruct((t_all, d), BF16)],
        compiler_params=_params(("arbitrary",)),
    )(u, u, o, x, mod, g_post, g_fpre, w_pool, pool_scale, w_out)


def _ffn_fwd(h2, w_g, w_u, w_d, x1, tgt, mod, g_post, seq, tm):
    t_all, d = x1.shape
    nt = seq // tm
    nk, ff, _ = w_g.shape

    def body(h_ref, wg_ref, wu_ref, wd_ref, x1_ref, t_ref, mod_ref, g_ref,
             a_ref, b_ref, fin_ref, dy_ref, df_ref, loss_ref, accb_ref, accg_ref, facc):
        i, k = pl.program_id(0), pl.program_id(1)

        @pl.when(k == 0)
        def _():
            facc[...] = jnp.zeros_like(facc)

        for c in range(FFN_ROW_CHUNKS):
            rows = slice(c * (tm // FFN_ROW_CHUNKS), (c + 1) * (tm // FFN_ROW_CHUNKS))
            hb = h_ref[rows, :]
            a = _dot_nt(hb, wg_ref[0])
            b = _dot_nt(hb, wu_ref[0])
            a_ref[0, rows, :] = a.astype(BF16)
            b_ref[0, rows, :] = b.astype(BF16)
            fin = ((a * _sigmoid(a)) * b).astype(BF16)
            fin_ref[0, rows, :] = fin
            facc[rows, :] += _dot(fin, wd_ref[0])

        @pl.when(k == nk - 1)
        def _():
            f = facc[...]
            r4 = _rms(f)
            n4 = f * r4
            gate = mod_ref[0, 5:6, :]
            g = g_ref[...]
            err = (x1_ref[...] + gate * (n4 * g)) - t_ref[...]
            dy = err * (1.0 / d)
            dy_ref[...] = dy

            @pl.when(i == 0)
            def _():
                loss_ref[...] = jnp.zeros_like(loss_ref)
                accg_ref[...] = jnp.zeros_like(accg_ref)

            @pl.when(i % nt == 0)
            def _():
                accb_ref[...] = jnp.zeros_like(accb_ref)

            loss_ref[...] += (0.5 / d) * jnp.sum(err * err)
            accb_ref[0, 0:1, :] += _colsum(dy * (n4 * g))
            accg_ref[0:1, :] += _colsum((dy * gate) * n4)
            dn4 = (dy * gate) * g
            df_ref[...] = _norm_bwd(dn4, n4, r4).astype(BF16)

    tok = lambda i, k: (i, 0)
    ktok = lambda i, k: (k, i, 0)
    kw = lambda i, k: (k, 0, 0)
    const2 = lambda i, k: (0, 0)
    return pl.pallas_call(
        body, name="ffn_fwd", grid=(t_all // tm, nk),
        in_specs=[pl.BlockSpec((tm, d), tok),
                  pl.BlockSpec((1, ff, d), kw), pl.BlockSpec((1, ff, d), kw), pl.BlockSpec((1, ff, d), kw),
                  pl.BlockSpec((tm, d), tok), pl.BlockSpec((tm, d), tok),
                  pl.BlockSpec((1, MOD_ROWS, d), lambda i, k: (i // nt, 0, 0)),
                  pl.BlockSpec((1, d), const2)],
        out_specs=[pl.BlockSpec((1, tm, ff), ktok)] * 3
        + [pl.BlockSpec((tm, d), tok), pl.BlockSpec((tm, d), tok),
           pl.BlockSpec((8, LANES), const2),
           pl.BlockSpec((1, 8, d), lambda i, k: (i // nt, 0, 0)),
           pl.BlockSpec((8, d), const2)],
        out_shape=[jax.ShapeDtypeStruct((nk, t_all, ff), BF16)] * 3
        + [jax.ShapeDtypeStruct((t_all, d), F32), jax.ShapeDtypeStruct((t_all, d), BF16),
           jax.ShapeDtypeStruct((8, LANES), F32),
           jax.ShapeDtypeStruct((t_all // seq, 8, d), F32),
           jax.ShapeDtypeStruct((8, d), F32)],
        scratch_shapes=[pltpu.VMEM((tm, d), F32)],
        compiler_params=_params(("arbitrary", "arbitrary")),
    )(h2, w_g, w_u, w_d, x1, tgt, mod, g_post)


def _ffn_bwd(df, a, b, w_d, w_g, w_u, x1, dy, mix, mod, g_fpre, g_mpost, seq, tm):
    t_all, d = x1.shape
    nt = seq // tm
    nk, ff, _ = w_g.shape

    def body(df_ref, a_ref, b_ref, wd_ref, wg_ref, wu_ref, x1_ref, dy_ref, mix_ref, mod_ref, gf_ref, gm_ref,
             da_ref, db_ref, dx1_ref, dmix_ref, accb_ref, accg_ref, hacc):
        i, k = pl.program_id(0), pl.program_id(1)

        @pl.when(k == 0)
        def _():
            hacc[...] = jnp.zeros_like(hacc)

        for c in range(FFN_ROW_CHUNKS):
            rows = slice(c * (tm // FFN_ROW_CHUNKS), (c + 1) * (tm // FFN_ROW_CHUNKS))
            dfin = _dot_nt(df_ref[rows, :], wd_ref[0])
            af = a_ref[0, rows, :].astype(F32)
            bf = b_ref[0, rows, :].astype(F32)
            sig = _sigmoid(af)
            da = ((dfin * bf) * (sig * (1.0 + af * (1.0 - sig)))).astype(BF16)
            db = (dfin * (af * sig)).astype(BF16)
            da_ref[0, rows, :] = da
            db_ref[0, rows, :] = db
            hacc[rows, :] += _dot(da, wg_ref[0]) + _dot(db, wu_ref[0])

        @pl.when(k == nk - 1)
        def _():
            @pl.when(i == 0)
            def _():
                accg_ref[...] = jnp.zeros_like(accg_ref)

            @pl.when(i % nt == 0)
            def _():
                accb_ref[...] = jnp.zeros_like(accb_ref)

            dh2 = hacc[...]
            x1 = x1_ref[...]
            r3 = _rms(x1)
            n3 = x1 * r3
            g3 = gf_ref[...]
            scale1 = 1.0 + mod_ref[0, 4:5, :]
            accb_ref[0, 0:1, :] += _colsum(dh2)
            accb_ref[0, 1:2, :] += _colsum(dh2 * (n3 * g3))
            accg_ref[0:1, :] += _colsum((dh2 * scale1) * n3)
            dx1 = dy_ref[...] + _norm_bwd((dh2 * scale1) * g3, n3, r3)
            dx1_ref[...] = dx1
            mix = mix_ref[...]
            r2 = _rms(mix)
            n2 = mix * r2
            g2 = gm_ref[...]
            gate = mod_ref[0, 2:3, :]
            accb_ref[0, 2:3, :] += _colsum(dx1 * (n2 * g2))
            accg_ref[1:2, :] += _colsum((dx1 * gate) * n2)
            dmix_ref[...] = _norm_bwd((dx1 * gate) * g2, n2, r2).astype(BF16)

    tok = lambda i, k: (i, 0)
    ktok = lambda i, k: (k, i, 0)
    kw = lambda i, k: (k, 0, 0)
    const2 = lambda i, k: (0, 0)
    return pl.pallas_call(
        body, name="ffn_bwd", grid=(t_all // tm, nk),
        in_specs=[pl.BlockSpec((tm, d), tok),
                  pl.BlockSpec((1, tm, ff), ktok), pl.BlockSpec((1, tm, ff), ktok),
                  pl.BlockSpec((1, ff, d), kw), pl.BlockSpec((1, ff, d), kw), pl.BlockSpec((1, ff, d), kw),
                  pl.BlockSpec((tm, d), tok), pl.BlockSpec((tm, d), tok), pl.BlockSpec((tm, d), tok),
                  pl.BlockSpec((1, MOD_ROWS, d), lambda i, k: (i // nt, 0, 0)),
                  pl.BlockSpec((1, d), const2), pl.BlockSpec((1, d), const2)],
        out_specs=[pl.BlockSpec((1, tm, ff), ktok)] * 2
        + [pl.BlockSpec((tm, d), tok), pl.BlockSpec((tm, d), tok),
           pl.BlockSpec((1, 8, d), lambda i, k: (i // nt, 0, 0)),
           pl.BlockSpec((8, d), const2)],
        out_shape=[jax.ShapeDtypeStruct((nk, t_all, ff), BF16)] * 2
        + [jax.ShapeDtypeStruct((t_all, d), F32), jax.ShapeDtypeStruct((t_all, d), BF16),
           jax.ShapeDtypeStruct((t_all // seq, 8, d), F32),
           jax.ShapeDtypeStruct((8, d), F32)],
        scratch_shapes=[pltpu.VMEM((tm, d), F32)],
        compiler_params=_params(("arbitrary", "arbitrary")),
    )(df, a, b, w_d, w_g, w_u, x1, dy, mix, mod, g_fpre, g_mpost)


def _mixer_bwd(dmix, w_out, pooled, w_pool, pool_scale, seq, tm):
    t_all, d = dmix.shape
    p = pooled.shape[1]
    ng = len(POOL_WINDOWS)

    def body(dm_ref, wo_ref, pooled_ref, wp_ref, ps_ref, do_ref, dpd_ref, dps_ref, dwp_ref):
        i = pl.program_id(0)

        @pl.when(i == 0)
        def _():
            dps_ref[...] = jnp.zeros_like(dps_ref)
            dwp_ref[...] = jnp.zeros_like(dwp_ref)

        dmixin = _dot_nt(dm_ref[...], wo_ref[...])
        do_ref[...] = dmixin[:, :p].astype(BF16)
        for g in range(ng):
            cols = slice(g * POOL_GROUP, (g + 1) * POOL_GROUP)
            dpool = dmixin[:, p + g * POOL_GROUP:p + (g + 1) * POOL_GROUP]
            pooled = pooled_ref[:, cols]
            wpg = wp_ref[g].astype(BF16)
            yg = _dot(pooled, wpg)
            dps_ref[0:1, cols] += _colsum(dpool * yg)
            dyg = (dpool * ps_ref[:, cols]).astype(BF16)
            dwp_ref[g] += _dot_tn(pooled, dyg)
            dpd_ref[:, cols] = _dot_nt(dyg, wpg)

    tok = lambda i: (i, 0)
    const2 = lambda i: (0, 0)
    const3 = lambda i: (0, 0, 0)
    return pl.pallas_call(
        body, name="mixer_bwd", grid=(t_all // tm,),
        in_specs=[pl.BlockSpec((tm, d), tok), pl.BlockSpec((d, d), const2), pl.BlockSpec((tm, p), tok),
                  pl.BlockSpec(w_pool.shape, const3), pl.BlockSpec((1, p), const2)],
        out_specs=[pl.BlockSpec((tm, p), tok), pl.BlockSpec((tm, p), tok),
                   pl.BlockSpec((8, p), const2), pl.BlockSpec(w_pool.shape, const3)],
        out_shape=[jax.ShapeDtypeStruct((t_all, p), BF16), jax.ShapeDtypeStruct((t_all, p), F32),
                   jax.ShapeDtypeStruct((8, p), F32), jax.ShapeDtypeStruct(w_pool.shape, F32)],
        compiler_params=_params(("arbitrary",)),
    )(dmix, w_out, pooled, w_pool, pool_scale)


def _attn_bwd(qn, k, kt, v, do, ltot, seq, tq, tk, order):
    t_all, w = qn.shape
    nb, nq, ndiag, nkb = t_all // seq, seq // tq, tq // tk, seq // tk
    assert ndiag % 2 == 0, "two key blocks per loop trip"
    rc = ATTN_ROW_CHUNK
    nh = HEADS_PER_BLOCK
    heads = range(nh)

    def body(q_ref, k_ref, kt_ref, v_ref, do_ref, l_ref, up_ref, bf_ref, dq_ref, dk_ref, dv_ref,
             z_buf, dw_buf, ls_buf, hl_buf, upto_buf, g_buf, gb_buf, before_buf, w_buf, dz_buf,
             totl_buf, totg_buf, rem_buf, preg_buf, qnt_buf, dot_buf, dq_t, dk_t, dv_t):
        i = pl.program_id(2)
        nblk = (i + 1) * ndiag

        @pl.when(i == 0)
        def _():
            dk_t[...] = jnp.zeros_like(dk_t)
            dv_t[...] = jnp.zeros_like(dv_t)

        lane = lax.broadcasted_iota(jnp.int32, (1, LANES), 1)
        row = lax.broadcasted_iota(jnp.int32, (rc, tk), 0)
        col = lax.broadcasted_iota(jnp.int32, (rc, tk), 1)
        first = lane < HEAD_DIM
        q2 = q_ref[...]
        do2 = do_ref[...]
        l2 = l_ref[...]
        qs = [jnp.where(first, q2, jnp.zeros_like(q2)), jnp.where(first, jnp.zeros_like(q2), q2)]
        dos = [jnp.where(first, do2, jnp.zeros_like(do2)), jnp.where(first, jnp.zeros_like(do2), do2)]
        qnt_buf[...] = q2.astype(F32).T.astype(BF16)
        dot_buf[...] = do2.astype(F32).T.astype(BF16)
        for h in heads:
            rem_buf[h] = jnp.where(first if h == 0 else ~first, l2, pltpu.roll(l2, HEAD_DIM, 1))
        preg_buf[...] = jnp.zeros_like(preg_buf)
        dq_t[...] = jnp.zeros_like(dq_t)
        w_buf[1] = jnp.zeros((nh * tq, tk), BF16)
        dz_buf[1] = jnp.zeros((nh * tq, tk), BF16)

        def causal(c, diag):
            return (col + diag * tk) < (row + c * rc)

        def scores(blk, slot):
            off = pl.multiple_of(blk * tk, tk)
            kj = k_ref[pl.ds(off, tk), :]
            vj = v_ref[pl.ds(off, tk), :]
            for h in heads:
                z_buf[slot, h] = _dot_nt(qs[h], kj)
                dw_buf[slot, h] = _dot_nt(dos[h], vj)

        def gradients(blk, slot):
            keys = pl.ds(pl.multiple_of(blk * tk, tk), tk)
            for h in heads:
                dims = slice(h * HEAD_DIM, (h + 1) * HEAD_DIM)
                queries = slice(h * tq, (h + 1) * tq)
                dq_t[dims, :] += _dot_nt(kt_ref[dims, keys], dz_buf[slot, queries, :])
                dk_t[blk, dims, :] += _dot(qnt_buf[dims, :], dz_buf[slot, queries, :])
                dv_t[blk, dims, :] += _dot(dot_buf[dims, :], w_buf[slot, queries, :])

        def softplus_stage(h, slot, diag):
            for c in range(tq // rc):
                rows = slice(c * rc, (c + 1) * rc)
                if _all_masked(c, diag, rc, tk):
                    hl_buf[h, rows, :] = jnp.zeros((rc, LOG_SUM_PASSES * tk), BF16)
                    continue
                nz = z_buf[slot, h, rows, :]
                l1 = jnp.minimum(nz, 0.0) - jnp.log(1.0 + jnp.exp(_neg_abs(nz)))
                if _some_masked(c, diag, rc, tk):
                    l1 = jnp.where(causal(c, diag), l1, 0.0)
                for s, part in enumerate(_split(l1)[:LOG_SUM_PASSES]):
                    hl_buf[h, rows, s * tk:(s + 1) * tk] = part
                ls_buf[h, rows, :] = l1 - nz
                totl_buf[h, rows, :] = _row_sums(l1)

        def weights_stage(h, slot, diag):
            for c in range(tq // rc):
                rows = slice(c * rc, (c + 1) * rc)
                stacked = slice(h * tq + c * rc, h * tq + (c + 1) * rc)
                if _all_masked(c, diag, rc, tk):
                    w_buf[slot, stacked, :] = jnp.zeros((rc, tk), BF16)
                    gb_buf[h, rows, :] = jnp.zeros((rc, tk), BF16)
                    continue
                wgt = jnp.exp(ls_buf[h, rows, :] + (_across(rem_buf[h, rows, :], tk) - upto_buf[h, rows, :]))
                if _some_masked(c, diag, rc, tk):
                    wgt = jnp.where(causal(c, diag), wgt, 0.0)
                w_buf[slot, stacked, :] = wgt.astype(BF16)
                g = wgt * dw_buf[slot, h, rows, :]
                g_buf[h, rows, :] = g
                gb_buf[h, rows, :] = g.astype(BF16)
                totg_buf[h, rows, :] = _row_sums(g)
                rem_buf[h, rows, :] -= totl_buf[h, rows, :]

        def dscore_stage(h, slot, diag):
            for c in range(tq // rc):
                rows = slice(c * rc, (c + 1) * rc)
                stacked = slice(h * tq + c * rc, h * tq + (c + 1) * rc)
                if _all_masked(c, diag, rc, tk):
                    dz_buf[slot, stacked, :] = jnp.zeros((rc, tk), BF16)
                    continue
                sig = jnp.exp(ls_buf[h, rows, :])
                g = g_buf[h, rows, :]
                dnz = sig * ((before_buf[h, rows, :] + _across(preg_buf[h, rows, :], tk)) + g) - g
                if _some_masked(c, diag, rc, tk):
                    dnz = jnp.where(causal(c, diag), dnz, 0.0)
                dz_buf[slot, stacked, :] = dnz.astype(BF16)
                preg_buf[h, rows, :] += totg_buf[h, rows, :]

        def position(blk, slot, diag, prefetch):
            if prefetch:
                scores(blk + 1, 1 - slot)
            for h in heads:
                softplus_stage(h, slot, diag)
                upto_buf[h] = _dot(hl_buf[h], up_ref[...])
            gradients(jnp.maximum(blk - 1, 0), 1 - slot)
            for h in heads:
                weights_stage(h, slot, diag)
                before_buf[h] = _dot(gb_buf[h], bf_ref[...])
            for h in heads:
                dscore_stage(h, slot, diag)

        scores(0, 0)

        def trip(jj, carry):
            for u in range(2):
                position(2 * jj + u, u, None, True)
            return carry

        lax.fori_loop(0, (i * ndiag) // 2, trip, 0)
        for d in range(ndiag):
            position(i * ndiag + d, d % 2, d, d < ndiag - 1)
        gradients(nblk - 1, 1)
        dq_ref[...] = (dq_t[...].T * NEG_QK_SCALE).astype(BF16)

        @pl.when(i == nq - 1)
        def _():
            for blk in range(nkb):
                dk_ref[blk * tk:(blk + 1) * tk, :] = dk_t[blk].T.astype(BF16)
                dv_ref[blk * tk:(blk + 1) * tk, :] = dv_t[blk].T.astype(BF16)

    qmap = lambda b, hp, i: (b * nq + i, hp)
    kmap = lambda b, hp, i: (b, hp)
    const = lambda b, hp, i: (0, 0)
    return pl.pallas_call(
        body, name="attn_bwd", grid=(nb, w // LANES, nq),
        in_specs=[pl.BlockSpec((tq, LANES), qmap), pl.BlockSpec((seq, LANES), kmap),
                  pl.BlockSpec((LANES, seq), lambda b, hp, i: (hp, b)), pl.BlockSpec((seq, LANES), kmap),
                  pl.BlockSpec((tq, LANES), qmap), pl.BlockSpec((tq, LANES), qmap),
                  pl.BlockSpec((LOG_SUM_PASSES * tk, tk), const), pl.BlockSpec((tk, tk), const)],
        out_specs=[pl.BlockSpec((tq, LANES), qmap), pl.BlockSpec((seq, LANES), kmap), pl.BlockSpec((seq, LANES), kmap)],
        out_shape=[jax.ShapeDtypeStruct((t_all, w), BF16)] * 3,
        scratch_shapes=[pltpu.VMEM((2, nh, tq, tk), F32), pltpu.VMEM((2, nh, tq, tk), F32),
                        pltpu.VMEM((nh, tq, tk), F32), pltpu.VMEM((nh, tq, LOG_SUM_PASSES * tk), BF16),
                        pltpu.VMEM((nh, tq, tk), F32), pltpu.VMEM((nh, tq, tk), F32),
                        pltpu.VMEM((nh, tq, tk), BF16), pltpu.VMEM((nh, tq, tk), F32),
                        pltpu.VMEM((2, nh * tq, tk), BF16), pltpu.VMEM((2, nh * tq, tk), BF16),
                        pltpu.VMEM((nh, tq, LANES), F32), pltpu.VMEM((nh, tq, LANES), F32),
                        pltpu.VMEM((nh, tq, LANES), F32), pltpu.VMEM((nh, tq, LANES), F32),
                        pltpu.VMEM((LANES, tq), BF16), pltpu.VMEM((LANES, tq), BF16),
                        pltpu.VMEM((LANES, tq), F32), pltpu.VMEM((nkb, LANES, tk), F32),
                        pltpu.VMEM((nkb, LANES, tk), F32)],
        compiler_params=_params(("arbitrary", "arbitrary", "arbitrary")),
    )(qn, k, kt, v, do, ltot, _tri_matrix(tk, "upto")[:LOG_SUM_PASSES * tk] + order.astype(BF16),
      _tri_matrix(tk, "before")[:tk])


def _inproj_bwd(dq, dk, dv, dpd, x, dx1, mod, g_pre, w_in, seq, tm):
    t_all, d = x.shape
    nt = seq // tm
    p = dq.shape[1]

    def body(dq_ref, dk_ref, dv_ref, dpd_ref, halo_ref, x_ref, dx1_ref, mod_ref, g_ref, w_ref,
             gx_ref, du_ref, accb_ref, accg_ref):
        i = pl.program_id(0)
        it = i % nt

        @pl.when(i == 0)
        def _():
            accg_ref[...] = jnp.zeros_like(accg_ref)

        @pl.when(it == 0)
        def _():
            accb_ref[...] = jnp.zeros_like(accb_ref)

        dpd = dpd_ref[...]
        pos = it * tm + lax.broadcasted_iota(jnp.int32, (tm, 1), 0)
        cnts = _window_counts(pos)
        halo = jnp.where(it == nt - 1, 0.0, halo_ref[...])
        scaled = []
        halos = []
        for g, win in enumerate(POOL_WINDOWS):
            cols = slice(g * POOL_GROUP, (g + 1) * POOL_GROUP)
            scaled.append(dpd[:, cols] / cnts[g])
            halos.append(halo[:, cols] / float(win))
        ext = jnp.concatenate([jnp.concatenate(scaled, axis=1), jnp.concatenate(halos, axis=1)], axis=0)
        sums = _window_sums(ext, tm, 0, True)
        du = (jnp.concatenate(sums, axis=1) - dpd).astype(BF16)
        du_ref[...] = du
        g1 = g_ref[...]
        scale1 = 1.0 + mod_ref[0, 1:2, :]
        for c in range(ROW_CHUNKS):
            rows = slice(c * (tm // ROW_CHUNKS), (c + 1) * (tm // ROW_CHUNKS))
            dh1 = (_dot_nt(dq_ref[rows, :], w_ref[0]) + _dot_nt(dk_ref[rows, :], w_ref[1])
                   + _dot_nt(dv_ref[rows, :], w_ref[2]) + _dot_nt(du_ref[rows, :], w_ref[3]))
            xf = x_ref[rows, :]
            r1 = _rms(xf)
            n1 = xf * r1
            accb_ref[0, 0:1, :] += _colsum(dh1)
            accb_ref[0, 1:2, :] += _colsum(dh1 * (n1 * g1))
            accg_ref[0:1, :] += _colsum((dh1 * scale1) * n1)
            gx_ref[rows, :] = dx1_ref[rows, :] + _norm_bwd((dh1 * scale1) * g1, n1, r1)

    tok = lambda i: (i, 0)
    const2 = lambda i: (0, 0)
    hb = tm // HALO
    last = t_all // HALO - 1
    return pl.pallas_call(
        body, name="inproj_bwd", grid=(t_all // tm,),
        in_specs=[pl.BlockSpec((tm, p), tok), pl.BlockSpec((tm, p), tok), pl.BlockSpec((tm, p), tok),
                  pl.BlockSpec((tm, p), tok),
                  pl.BlockSpec((HALO, p), lambda i: (jnp.minimum((i + 1) * hb, last), 0)),
                  pl.BlockSpec((tm, d), tok), pl.BlockSpec((tm, d), tok),
                  pl.BlockSpec((1, MOD_ROWS, d), lambda i: (i // nt, 0, 0)),
                  pl.BlockSpec((1, d), const2),
                  pl.BlockSpec((N_CHIPS, d, p), lambda i: (0, 0, 0))],
        out_specs=[pl.BlockSpec((tm, d), tok), pl.BlockSpec((tm, p), tok),
                   pl.BlockSpec((1, 8, d), lambda i: (i // nt, 0, 0)),
                   pl.BlockSpec((8, d), const2)],
        out_shape=[jax.ShapeDtypeStruct((t_all, d), F32), jax.ShapeDtypeStruct((t_all, p), BF16),
                   jax.ShapeDtypeStruct((t_all // seq, 8, d), F32),
                   jax.ShapeDtypeStruct((8, d), F32)],
        compiler_params=_params(("arbitrary",)),
    )(dq, dk, dv, dpd, dpd, x, dx1, mod, g_pre, w_in)


def _tn_matmul(x, ys, nk, bt, name, after=()):
    t_all = x.shape[-2]
    m = x.shape[-1]
    ny = len(ys)
    nt = t_all // bt

    def spec(arr):
        if arr.ndim == 3:
            return pl.BlockSpec((1, bt, arr.shape[-1]), lambda k, t: (k, t, 0))
        return pl.BlockSpec((bt, arr.shape[-1]), lambda k, t: (t, 0))

    def tile(ref):
        return ref[0] if len(ref.shape) == 3 else ref[...]

    def body(*refs):
        outs = refs[1 + ny + len(after):]
        x_ref, y_refs, o_refs, h_refs = refs[0], refs[1:1 + ny], outs[:ny], outs[ny:]
        t = pl.program_id(1)
        xt = tile(x_ref)
        for y_ref, o_ref, h_ref in zip(y_refs, o_refs, h_refs):
            part = _dot_tn(xt, tile(y_ref))

            @pl.when(t == 0)
            def _(o_ref=o_ref, part=part):
                o_ref[0] = part

            @pl.when(t > 0)
            def _(o_ref=o_ref, part=part):
                o_ref[0] += part

            @pl.when(t == nt - 1)
            def _(o_ref=o_ref, h_ref=h_ref):
                h_ref[0] = o_ref[0].astype(BF16)

    out_specs = [pl.BlockSpec((1, m, y.shape[-1]), lambda k, t: (k, 0, 0)) for y in ys]
    out = pl.pallas_call(
        body, name=name, grid=(nk, nt),
        in_specs=[spec(x)] + [spec(y) for y in ys] + [_ANY] * len(after),
        out_specs=out_specs * 2,
        out_shape=[jax.ShapeDtypeStruct((nk, m, y.shape[-1]), dt) for dt in (F32, BF16) for y in ys],
        compiler_params=_params(("arbitrary", "arbitrary")),
    )(x, *ys, *after)
    return out[:ny], out[ny:]


def _tn_matmul_stacked(x, ys, bt, name, after=()):
    t_all, m = x.shape
    n = ys[0].shape[1]
    ny = len(ys)
    nt = t_all // bt

    def body(*refs):
        x_ref, y_refs, (o_ref, h_ref) = refs[0], refs[1:1 + ny], refs[1 + ny + len(after):]
        t = pl.program_id(0)
        xt = x_ref[...]

        @pl.when(t == 0)
        def _():
            o_ref[...] = jnp.zeros_like(o_ref)

        for j, y_ref in enumerate(y_refs):
            o_ref[j] += _dot_tn(xt, y_ref[...])

        @pl.when(t == nt - 1)
        def _():
            h_ref[...] = o_ref[...].astype(BF16)

    whole = pl.BlockSpec((ny, m, n), lambda t: (0, 0, 0))
    return pl.pallas_call(
        body, name=name, grid=(nt,),
        in_specs=[pl.BlockSpec((bt, m), lambda t: (t, 0))] + [pl.BlockSpec((bt, n), lambda t: (t, 0))] * ny
        + [_ANY] * len(after),
        out_specs=[whole, whole],
        out_shape=[jax.ShapeDtypeStruct((ny, m, n), F32), jax.ShapeDtypeStruct((ny, m, n), BF16)],
        compiler_params=_params(("arbitrary",)),
    )(x, *ys, *after)


def _cond_fwd(c_all, w_q, b_q, bn):
    nrow, d = c_all.shape
    ncol = w_q.shape[1]

    def body(c_ref, w_ref, b_ref, sc_ref, mod_ref):
        cf = c_ref[...]
        sc = cf * _sigmoid(cf)
        sc_ref[...] = sc
        shi, slo = _split(sc)
        whi, wlo = _split(w_ref[...])
        mod_ref[...] = (_dot(shi, whi) + _dot(shi, wlo) + _dot(slo, whi)) + b_ref[...]

    return pl.pallas_call(
        body, name="cond_fwd", grid=(ncol // bn,),
        in_specs=[pl.BlockSpec((nrow, d), lambda n: (0, 0)), pl.BlockSpec((d, bn), lambda n: (0, n)),
                  pl.BlockSpec((1, bn), lambda n: (0, n))],
        out_specs=[pl.BlockSpec((nrow, d), lambda n: (0, 0)), pl.BlockSpec((nrow, bn), lambda n: (0, n))],
        out_shape=[jax.ShapeDtypeStruct((nrow, d), F32), jax.ShapeDtypeStruct((nrow, ncol), F32)],
        compiler_params=_params(("arbitrary",)),
    )(c_all, w_q, b_q)


def _cond_bwd(sc_all, dmod_q, bn):
    nrow, d = sc_all.shape
    ncol = dmod_q.shape[1]

    def body(sc_ref, dm_ref, gw_ref):
        shi, slo = _split(sc_ref[...])
        dhi, dlo = _split(dm_ref[...])
        gw_ref[...] = _dot_tn(shi, dhi) + _dot_tn(shi, dlo) + _dot_tn(slo, dhi)

    return pl.pallas_call(
        body, name="cond_bwd", grid=(ncol // bn,),
        in_specs=[pl.BlockSpec((nrow, d), lambda n: (0, 0)), pl.BlockSpec((nrow, bn), lambda n: (0, n))],
        out_specs=pl.BlockSpec((d, bn), lambda n: (0, n)),
        out_shape=jax.ShapeDtypeStruct((d, ncol), F32),
        compiler_params=_params(("arbitrary",)),
    )(sc_all, dmod_q)


def _row_block(rows, cols, budget=1 << 18):
    best = None
    for br in range(8, rows + 1, 8):
        if rows % br == 0 and br * cols <= budget:
            best = br
    return best if best is not None else rows


def _adam_math(w, g, m, v):
    c1 = 1.0 - ADAM_B1 ** ADAM_STEP
    c2 = 1.0 - ADAM_B2 ** ADAM_STEP
    m2 = ADAM_B1 * m + (1.0 - ADAM_B1) * g
    v2 = ADAM_B2 * v + (1.0 - ADAM_B2) * (g * g)
    return -ADAM_LR * ((m2 / c1) / (jnp.sqrt(v2 / c2) + ADAM_EPS) + ADAM_WD * w), m2, v2


def _small_updates(summed, params):
    n = len(params)

    def body(s_ref, *refs):
        ins, outs = refs[:3 * n], refs[3 * n:]
        for p, (_, _, _, pick) in enumerate(params):
            w_ref, m_ref, v_ref = ins[3 * p:3 * p + 3]
            g = pick(s_ref)
            delta, m2, v2 = _adam_math(w_ref[...], g, m_ref[...], v_ref[...])
            for o_ref, val in zip(outs[4 * p:4 * p + 4], (g, delta, m2, v2)):
                o_ref[...] = val

    out = pl.pallas_call(
        body, name="adamw_small",
        out_shape=[jax.ShapeDtypeStruct(w.shape, F32) for w, _, _, _ in params for _ in range(4)],
        compiler_params=pltpu.CompilerParams(vmem_limit_bytes=VMEM_LIMIT),
    )(summed, *[t for w, m, v, _ in params for t in (w, m, v)])
    return [tuple(out[4 * p:4 * p + 4]) for p in range(n)]


def _adamw(w, g, m, v, name, after=()):
    rows, cols = w.shape
    br = _row_block(rows, cols)

    def body(*refs):
        w_ref, g_ref, m_ref, v_ref = refs[:4]
        d_ref, nm_ref, nv_ref = refs[4 + len(after):]
        d_ref[...], nm_ref[...], nv_ref[...] = _adam_math(w_ref[...], g_ref[...], m_ref[...], v_ref[...])

    blk = pl.BlockSpec((br, cols), lambda i: (i, 0))
    return pl.pallas_call(
        body, name=name, grid=(rows // br,),
        in_specs=[blk] * 4 + [_ANY] * len(after), out_specs=[blk] * 3,
        out_shape=[jax.ShapeDtypeStruct((rows, cols), F32)] * 3,
        compiler_params=_params(("arbitrary",)),
    )(w, g, m, v, *after)


def _all_gather(x_shard, name, after=()):
    m_per, n = x_shard.shape

    def body(x_ref, *refs):
        out_ref, send_sems, recv_sems, local_sem = refs[len(after):]
        x, y, c = _position()
        me, sibling = (x, y, c), (x, y, 1 - c)
        chips = [(1 - x, y), (x, 1 - y), (1 - x, 1 - y)]

        def rows(px, py, pc):
            return out_ref.at[pl.ds((4 * px + 2 * py + pc) * m_per, m_per), :]

        def copy(k, block, to, src=None):
            return pltpu.make_async_remote_copy(
                src_ref=rows(*block) if src is None else src, dst_ref=rows(*block),
                send_sem=send_sems.at[k], recv_sem=recv_sems.at[k], device_id=to, device_id_type=MESH)

        mine = pltpu.make_async_copy(x_ref, rows(*me), local_sem)
        mine.start()
        first = [copy(0, me, sibling, src=x_ref)]
        first += [copy(1 + j, me, (*chip, c), src=x_ref) for j, chip in enumerate(chips)]
        for cp in first:
            cp.start()
        passed = [copy(4 + j, (*chip, c), sibling) for j, chip in enumerate(chips)]
        for j, chip in enumerate(chips):
            copy(1 + j, (*chip, c), me).wait_recv()
            passed[j].start()
        copy(0, sibling, me).wait_recv()
        for j, chip in enumerate(chips):
            copy(4 + j, (*chip, 1 - c), me).wait_recv()
        for cp in first + passed:
            cp.wait_send()
        mine.wait()

    return pl.pallas_call(
        body, name=name,
        out_shape=jax.ShapeDtypeStruct((N_DEV * m_per, n), x_shard.dtype),
        in_specs=[pl.BlockSpec(memory_space=pltpu.VMEM)] + [_ANY] * len(after),
        out_specs=pl.BlockSpec(memory_space=pltpu.VMEM),
        scratch_shapes=[pltpu.SemaphoreType.DMA((7,)), pltpu.SemaphoreType.DMA((7,)), pltpu.SemaphoreType.DMA],
        compiler_params=pltpu.CompilerParams(vmem_limit_bytes=VMEM_LIMIT),
    )(x_shard, *after)


_ANY = pl.BlockSpec(memory_space=pl.ANY)


def _place_quarters(place, quarters, tag, after=()):
    steps = 2
    n = len(quarters)

    def body(place_ref, *refs):
        for w_ref, o_ref in zip(refs[:n], refs[n + len(after):]):
            o_ref[0] = w_ref[...].astype(BF16)

    return pl.pallas_call(
        body, name="place_quarters_" + tag,
        grid_spec=pltpu.PrefetchScalarGridSpec(
            num_scalar_prefetch=1, grid=(steps,),
            in_specs=[pl.BlockSpec((q.shape[0] // steps, q.shape[1]), lambda r, place_ref: (r, 0)) for q in quarters]
            + [_ANY] * len(after),
            out_specs=[pl.BlockSpec((1, q.shape[0] // steps, q.shape[1]), lambda r, place_ref: (place_ref[0], r, 0))
                       for q in quarters]),
        out_shape=[jax.ShapeDtypeStruct((N_CHIPS,) + q.shape, BF16) for q in quarters],
        compiler_params=_params(("arbitrary",)),
    )(place, *quarters, *after)


_HBM = pl.BlockSpec(memory_space=pltpu.HBM)
_SEM = pl.BlockSpec(memory_space=pltpu.SEMAPHORE)
_EFFECT = pltpu.SideEffectType.DATAFLOW_SIDE_EFFECTING


def _quarter_halves(shapes, a, which):
    hr = shapes[a][0] // 2
    return pl.ds(which * hr, hr)


def _gather_start(placed, after, tag):
    n = len(placed)
    m = len(after)
    shapes = [b.shape[1:] for b in placed]

    def body(*refs):
        g_refs = refs[:n]
        send_sems, recv_sems = refs[n + m], refs[n + m + 1]
        token = refs[2 * n + m + 2]
        x, y, c = _position()
        chips = [(1 - x, y), (x, 1 - y), (1 - x, 1 - y)]
        mine = 2 * x + y
        for a in range(n):
            ref = g_refs[a].at[mine, _quarter_halves(shapes, a, c), :]
            for p in range(3):
                pltpu.make_async_remote_copy(
                    src_ref=ref, dst_ref=ref, send_sem=send_sems.at[3 * a + p], recv_sem=recv_sems.at[3 * a + p],
                    device_id=(*chips[p], c), device_id_type=MESH).start()
        token[...] = jnp.zeros_like(token)

    out = pl.pallas_call(
        body, name="gather_start_" + tag,
        out_shape=(pltpu.SemaphoreType.DMA((3 * n,)), pltpu.SemaphoreType.DMA((3 * n,)),
                   *[pltpu.HBM(b.shape, b.dtype) for b in placed], jax.ShapeDtypeStruct((8, LANES), F32)),
        in_specs=[_HBM] * n + [_ANY] * m,
        out_specs=(_SEM, _SEM, *[_HBM] * n, pl.BlockSpec(memory_space=pltpu.VMEM)),
        input_output_aliases={a: 2 + a for a in range(n)},
        compiler_params=pltpu.CompilerParams(has_side_effects=_EFFECT),
    )(*[pltpu.with_memory_space_constraint(b, pltpu.HBM) for b in placed], *after)
    return out[0], out[1], list(out[2:2 + n]), out[2 + n]


def _gather_wait(send_sems, recv_sems, thru, after, tag):
    n = len(thru)
    shapes = [b.shape[1:] for b in thru]

    def body(*refs):
        g_refs = refs[:n]
        send_sems, recv_sems = refs[n], refs[n + 1]
        x, y, c = _position()
        chips = [(1 - x, y), (x, 1 - y), (1 - x, 1 - y)]
        mine = 2 * x + y
        for a in range(n):
            rows = _quarter_halves(shapes, a, c)
            for p, (cx, cy) in enumerate(chips):
                copy = pltpu.make_async_remote_copy(
                    src_ref=g_refs[a].at[mine, rows, :], dst_ref=g_refs[a].at[2 * cx + cy, rows, :],
                    send_sem=send_sems.at[3 * a + p], recv_sem=recv_sems.at[3 * a + p],
                    device_id=(cx, cy, c), device_id_type=MESH)
                copy.wait_send()
                copy.wait_recv()

    return pl.pallas_call(
        body, name="gather_wait_" + tag,
        out_shape=[pltpu.HBM(b.shape, b.dtype) for b in thru],
        in_specs=[_HBM] * n + [_SEM, _SEM, _ANY], out_specs=[_HBM] * n,
        input_output_aliases={a: a for a in range(n)},
        compiler_params=pltpu.CompilerParams(has_side_effects=_EFFECT),
    )(*thru, send_sems, recv_sems, after)


def _gather_forward(bufs, tag):
    n = len(bufs)
    shapes = [b.shape[1:] for b in bufs]

    def body(*refs):
        g_refs = refs[n:2 * n]
        send_sems, recv_sems = refs[2 * n:]
        x, y, c = _position()
        chips = [(1 - x, y), (x, 1 - y), (1 - x, 1 - y)]

        def over_d2d(a, p, which):
            cx, cy = chips[p]
            ref = g_refs[a].at[2 * cx + cy, _quarter_halves(shapes, a, which), :]
            return pltpu.make_async_remote_copy(
                src_ref=ref, dst_ref=ref, send_sem=send_sems.at[3 * a + p], recv_sem=recv_sems.at[3 * a + p],
                device_id=(x, y, 1 - c), device_id_type=MESH)

        sends = [over_d2d(a, p, c) for a in range(n) for p in range(3)]
        for cp in sends:
            cp.start()
        for a in range(n):
            for p in range(3):
                over_d2d(a, p, 1 - c).wait_recv()
        for cp in sends:
            cp.wait_send()

    return pl.pallas_call(
        body, name="gather_forward_" + tag,
        out_shape=[jax.ShapeDtypeStruct(b.shape, BF16) for b in bufs],
        in_specs=[_ANY] * n, out_specs=[_ANY] * n,
        input_output_aliases={a: a for a in range(n)},
        scratch_shapes=[pltpu.SemaphoreType.DMA((3 * n,)), pltpu.SemaphoreType.DMA((3 * n,))],
    )(*bufs)


_FLIPS = [(fx, fy, fc) for fx in (0, 1) for fy in (0, 1) for fc in (0, 1)][1:]


def _flipped(pos, flip):
    return tuple(1 - p if f else p for p, f in zip(pos, flip))


def _direct_gather_start(slots):
    def body(s_ref, send_sems, recv_sems, thru, token):
        me = _position()
        mine = s_ref.at[4 * me[0] + 2 * me[1] + me[2]]
        for r, flip in enumerate(_FLIPS):
            pltpu.make_async_remote_copy(
                src_ref=mine, dst_ref=mine, send_sem=send_sems.at[r], recv_sem=recv_sems.at[r],
                device_id=_flipped(me, flip), device_id_type=MESH).start()
        token[...] = jnp.zeros_like(token)

    return pl.pallas_call(
        body, name="small_gather_start",
        out_shape=(pltpu.SemaphoreType.DMA((len(_FLIPS),)), pltpu.SemaphoreType.DMA((len(_FLIPS),)),
                   pltpu.HBM(slots.shape, slots.dtype), jax.ShapeDtypeStruct((8, LANES), F32)),
        in_specs=[_HBM], out_specs=(_SEM, _SEM, _HBM, pl.BlockSpec(memory_space=pltpu.VMEM)),
        input_output_aliases={0: 2},
        compiler_params=pltpu.CompilerParams(has_side_effects=_EFFECT),
    )(pltpu.with_memory_space_constraint(slots, pltpu.HBM))


def _direct_gather_wait(send_sems, recv_sems, slots, after):
    def body(s_ref, send_sems, recv_sems, after_ref, out_ref):
        me = _position()
        mine = s_ref.at[4 * me[0] + 2 * me[1] + me[2]]
        for r, flip in enumerate(_FLIPS):
            peer = _flipped(me, flip)
            copy = pltpu.make_async_remote_copy(
                src_ref=mine, dst_ref=s_ref.at[4 * peer[0] + 2 * peer[1] + peer[2]],
                send_sem=send_sems.at[r], recv_sem=recv_sems.at[r], device_id=peer, device_id_type=MESH)
            copy.wait_send()
            copy.wait_recv()

    return pl.pallas_call(
        body, name="small_gather_wait",
        out_shape=pltpu.HBM(slots.shape, slots.dtype),
        in_specs=[_HBM, _SEM, _SEM, _ANY], out_specs=_HBM,
        input_output_aliases={0: 0},
        compiler_params=pltpu.CompilerParams(has_side_effects=_EFFECT),
    )(slots, send_sems, recv_sems, after)


def _sibling_split_start(bufs, parts, nparts, after, tag):
    n, m = len(bufs), len(after)

    def body(*refs):
        b_refs = refs[:n]
        send_sems, recv_sems = refs[n + m], refs[n + m + 1]
        token = refs[2 * n + m + 2]
        x, y, c = _position()
        for r, ref in enumerate(parts(b_refs, x, y, c)):
            pltpu.make_async_remote_copy(
                src_ref=ref, dst_ref=ref, send_sem=send_sems.at[r], recv_sem=recv_sems.at[r],
                device_id=(x, y, 1 - c), device_id_type=MESH).start()
        token[...] = jnp.zeros_like(token)

    out = pl.pallas_call(
        body, name="sibling_start_" + tag,
        out_shape=(pltpu.SemaphoreType.DMA((nparts,)), pltpu.SemaphoreType.DMA((nparts,)),
                   *[pltpu.HBM(b.shape, b.dtype) for b in bufs], jax.ShapeDtypeStruct((8, LANES), F32)),
        in_specs=[_HBM] * n + [_ANY] * m,
        out_specs=(_SEM, _SEM, *[_HBM] * n, pl.BlockSpec(memory_space=pltpu.VMEM)),
        input_output_aliases={a: 2 + a for a in range(n)},
        compiler_params=pltpu.CompilerParams(has_side_effects=_EFFECT),
    )(*[pltpu.with_memory_space_constraint(b, pltpu.HBM) for b in bufs], *after)
    return out[0], out[1], list(out[2:2 + n]), out[2 + n]


def _sibling_split_wait(send_sems, recv_sems, bufs, parts, after, tag):
    n = len(bufs)

    def body(*refs):
        b_refs = refs[:n]
        send_sems, recv_sems = refs[n], refs[n + 1]
        x, y, c = _position()
        mine, theirs = parts(b_refs, x, y, c), parts(b_refs, x, y, 1 - c)
        for r, (src, dst) in enumerate(zip(mine, theirs)):
            copy = pltpu.make_async_remote_copy(
                src_ref=src, dst_ref=dst, send_sem=send_sems.at[r], recv_sem=recv_sems.at[r],
                device_id=(x, y, 1 - c), device_id_type=MESH)
            copy.wait_send()
            copy.wait_recv()

    return pl.pallas_call(
        body, name="sibling_wait_" + tag,
        out_shape=[pltpu.HBM(b.shape, b.dtype) for b in bufs],
        in_specs=[_HBM] * n + [_SEM, _SEM, _ANY], out_specs=[_HBM] * n,
        input_output_aliases={a: a for a in range(n)},
        compiler_params=pltpu.CompilerParams(has_side_effects=_EFFECT),
    )(*bufs, send_sems, recv_sems, after)


def _sibling_exchange(grads, tag):
    n = len(grads)
    shapes = [g.shape for g in grads]

    def body(*refs):
        g_refs, x_refs = refs[:n], refs[n:2 * n]
        send_sems, recv_sems = refs[2 * n:]
        x, y, c = _position()
        copies = []
        for a in range(n):
            hr = shapes[a][1] // 2
            cp = pltpu.make_async_remote_copy(
                src_ref=g_refs[a].at[:, pl.ds((1 - c) * hr, hr), :], dst_ref=x_refs[a],
                send_sem=send_sems.at[a], recv_sem=recv_sems.at[a],
                device_id=(x, y, 1 - c), device_id_type=MESH)
            cp.start()
            copies.append(cp)
        for cp in copies:
            cp.wait()

    return pl.pallas_call(
        body, name="grad_sibling_exchange_" + tag,
        out_shape=[jax.ShapeDtypeStruct((g.shape[0], g.shape[1] // 2, g.shape[2]), g.dtype) for g in grads],
        in_specs=[_ANY] * n, out_specs=[_ANY] * n,
        scratch_shapes=[pltpu.SemaphoreType.DMA((n,)), pltpu.SemaphoreType.DMA((n,))],
    )(*grads)


def _sibling_exchange_start(grads, tag):
    n = len(grads)
    lands = [lax.empty((g.shape[0], g.shape[1] // 2, g.shape[2]), g.dtype) for g in grads]

    def body(*refs):
        g_refs, x_refs = refs[:n], refs[n:2 * n]
        send_sems, recv_sems = refs[2 * n], refs[2 * n + 1]
        token = refs[4 * n + 2]
        x, y, c = _position()
        for a in range(n):
            hr = grads[a].shape[1] // 2
            pltpu.make_async_remote_copy(
                src_ref=g_refs[a].at[:, pl.ds((1 - c) * hr, hr), :], dst_ref=x_refs[a],
                send_sem=send_sems.at[a], recv_sem=recv_sems.at[a],
                device_id=(x, y, 1 - c), device_id_type=MESH).start()
        token[...] = jnp.zeros_like(token)

    both = list(grads) + lands
    out = pl.pallas_call(
        body, name="grad_sibling_exchange_start_" + tag,
        out_shape=(pltpu.SemaphoreType.DMA((n,)), pltpu.SemaphoreType.DMA((n,)),
                   *[pltpu.HBM(b.shape, b.dtype) for b in both], jax.ShapeDtypeStruct((8, LANES), F32)),
        in_specs=[_HBM] * (2 * n),
        out_specs=(_SEM, _SEM, *[_HBM] * (2 * n), pl.BlockSpec(memory_space=pltpu.VMEM)),
        input_output_aliases={a: 2 + a for a in range(2 * n)},
        compiler_params=pltpu.CompilerParams(has_side_effects=_EFFECT),
    )(*[pltpu.with_memory_space_constraint(b, pltpu.HBM) for b in both])
    return out[0], out[1], list(out[2:2 + n]), list(out[2 + n:2 + 2 * n]), out[2 + 2 * n]


def _sibling_exchange_wait(send_sems, recv_sems, grads, lands, after, tag):
    n = len(grads)

    def body(*refs):
        g_refs, x_refs = refs[:n], refs[n:2 * n]
        send_sems, recv_sems = refs[2 * n], refs[2 * n + 1]
        x, y, c = _position()
        for a in range(n):
            hr = grads[a].shape[1] // 2
            copy = pltpu.make_async_remote_copy(
                src_ref=g_refs[a].at[:, pl.ds((1 - c) * hr, hr), :], dst_ref=x_refs[a],
                send_sem=send_sems.at[a], recv_sem=recv_sems.at[a],
                device_id=(x, y, 1 - c), device_id_type=MESH)
            copy.wait_send()
            copy.wait_recv()

    both = list(grads) + list(lands)
    out = pl.pallas_call(
        body, name="grad_sibling_exchange_wait_" + tag,
        out_shape=[pltpu.HBM(b.shape, b.dtype) for b in both],
        in_specs=[_HBM] * (2 * n) + [_SEM, _SEM, _ANY], out_specs=[_HBM] * (2 * n),
        input_output_aliases={a: a for a in range(2 * n)},
        compiler_params=pltpu.CompilerParams(has_side_effects=_EFFECT),
    )(*both, send_sems, recv_sems, after)
    return list(out[n:])


def _chip_sums(core, grads, theirs, tag):
    n = len(grads)

    def body(core_ref, *refs):
        g_refs, t_refs, o_refs = refs[:n], refs[n:2 * n], refs[2 * n:]
        for g_ref, t_ref, o_ref in zip(g_refs, t_refs, o_refs):
            o_ref[...] = (g_ref[...] + t_ref[...].astype(F32)).astype(BF16)

    in_specs = [pl.BlockSpec((1, g.shape[1] // 2, g.shape[2]), lambda k, core_ref: (k, core_ref[0], 0)) for g in grads]
    in_specs += [pl.BlockSpec((1,) + t.shape[1:], lambda k, core_ref: (k, 0, 0)) for t in theirs]
    return pl.pallas_call(
        body, name="grad_chip_sums_" + tag,
        grid_spec=pltpu.PrefetchScalarGridSpec(
            num_scalar_prefetch=1, grid=(N_CHIPS,), in_specs=in_specs,
            out_specs=[pl.BlockSpec((1,) + t.shape[1:], lambda k, core_ref: (k, 0, 0)) for t in theirs]),
        out_shape=[jax.ShapeDtypeStruct(t.shape, BF16) for t in theirs],
        compiler_params=_params(("arbitrary",)),
    )(core, *grads, *theirs)


def _chip_exchange_start(sums, after, tag):
    n = len(sums)
    m = len(after)
    lands = [lax.empty((3,) + s.shape[1:], BF16) for s in sums]

    def body(*refs):
        s_refs, y_refs = refs[:n], refs[n:2 * n]
        send_sems, recv_sems = refs[2 * n + m], refs[2 * n + m + 1]
        token = refs[4 * n + m + 2]
        x, y, c = _position()
        chips = [(1 - x, y), (x, 1 - y), (1 - x, 1 - y)]
        for a in range(n):
            for p, (cx, cy) in enumerate(chips):
                pltpu.make_async_remote_copy(
                    src_ref=s_refs[a].at[2 * cx + cy], dst_ref=y_refs[a].at[p],
                    send_sem=send_sems.at[3 * a + p], recv_sem=recv_sems.at[3 * a + p],
                    device_id=(cx, cy, c), device_id_type=MESH).start()
        token[...] = jnp.zeros_like(token)

    both = list(sums) + lands
    out = pl.pallas_call(
        body, name="grad_chip_exchange_start_" + tag,
        out_shape=(pltpu.SemaphoreType.DMA((3 * n,)), pltpu.SemaphoreType.DMA((3 * n,)),
                   *[pltpu.HBM(b.shape, b.dtype) for b in both], jax.ShapeDtypeStruct((8, LANES), F32)),
        in_specs=[_HBM] * (2 * n) + [_ANY] * m,
        out_specs=(_SEM, _SEM, *[_HBM] * (2 * n), pl.BlockSpec(memory_space=pltpu.VMEM)),
        input_output_aliases={a: 2 + a for a in range(2 * n)},
        compiler_params=pltpu.CompilerParams(has_side_effects=_EFFECT),
    )(*[pltpu.with_memory_space_constraint(b, pltpu.HBM) for b in both], *after)
    return out[0], out[1], list(out[2:2 + n]), list(out[2 + n:2 + 2 * n]), out[2 + 2 * n]


def _chip_exchange_wait(send_sems, recv_sems, sums, lands, after, tag):
    n = len(sums)

    def body(*refs):
        s_refs, y_refs = refs[:n], refs[n:2 * n]
        send_sems, recv_sems = refs[2 * n], refs[2 * n + 1]
        x, y, c = _position()
        chips = [(1 - x, y), (x, 1 - y), (1 - x, 1 - y)]
        for a in range(n):
            for p, (cx, cy) in enumerate(chips):
                copy = pltpu.make_async_remote_copy(
                    src_ref=s_refs[a].at[2 * cx + cy], dst_ref=y_refs[a].at[p],
                    send_sem=send_sems.at[3 * a + p], recv_sem=recv_sems.at[3 * a + p],
                    device_id=(cx, cy, c), device_id_type=MESH)
                copy.wait_send()
                copy.wait_recv()

    both = list(sums) + list(lands)
    out = pl.pallas_call(
        body, name="grad_chip_exchange_wait_" + tag,
        out_shape=[pltpu.HBM(b.shape, b.dtype) for b in both],
        in_specs=[_HBM] * (2 * n) + [_SEM, _SEM, _ANY], out_specs=[_HBM] * (2 * n),
        input_output_aliases={a: a for a in range(2 * n)},
        compiler_params=pltpu.CompilerParams(has_side_effects=_EFFECT),
    )(*both, send_sems, recv_sems, after)
    return list(out[:n]), list(out[n:])


def _total_sums(place, sums, parts, after, tag):
    n = len(parts)
    m = len(after)
    steps = 2

    def body(place_ref, *refs):
        for s_ref, y_ref, o_ref in zip(refs[:n], refs[n:2 * n], refs[2 * n + m:]):
            o_ref[0] = ((s_ref[0].astype(F32) + y_ref[0].astype(F32)) + y_ref[1].astype(F32)) + y_ref[2].astype(F32)

    def step_rows(pt):
        return pt.shape[1] // steps

    in_specs = [pl.BlockSpec((1, step_rows(s), s.shape[2]), lambda r, place_ref: (place_ref[0], r, 0)) for s in sums]
    in_specs += [pl.BlockSpec((3, step_rows(pt), pt.shape[2]), lambda r, place_ref: (0, r, 0)) for pt in parts]
    in_specs += [_ANY] * m
    return pl.pallas_call(
        body, name="grad_total_sums_" + tag,
        grid_spec=pltpu.PrefetchScalarGridSpec(
            num_scalar_prefetch=1, grid=(steps,), in_specs=in_specs,
            out_specs=[pl.BlockSpec((1, step_rows(pt), pt.shape[2]), lambda r, place_ref: (place_ref[1], r, 0))
                       for pt in parts]),
        out_shape=[jax.ShapeDtypeStruct((2,) + pt.shape[1:], F32) for pt in parts],
        compiler_params=_params(("arbitrary",)),
    )(place, *sums, *parts, *after)


def _sibling_share(halves, tag):
    n = len(halves)

    def body(*refs):
        f_refs = refs[n:2 * n]
        send_sems, recv_sems = refs[2 * n:]
        x, y, c = _position()
        copies = []
        for a in range(n):
            cp = pltpu.make_async_remote_copy(
                src_ref=f_refs[a].at[c], dst_ref=f_refs[a].at[c], send_sem=send_sems.at[a], recv_sem=recv_sems.at[a],
                device_id=(x, y, 1 - c), device_id_type=MESH)
            cp.start()
            copies.append(cp)
        for a, cp in enumerate(copies):
            cp.wait_send()
            pltpu.make_async_remote_copy(
                src_ref=f_refs[a].at[1 - c], dst_ref=f_refs[a].at[1 - c], send_sem=send_sems.at[a],
                recv_sem=recv_sems.at[a], device_id=(x, y, c), device_id_type=MESH).wait_recv()

    return pl.pallas_call(
        body, name="grad_sibling_share_" + tag,
        out_shape=[jax.ShapeDtypeStruct(h.shape, F32) for h in halves],
        in_specs=[_ANY] * n, out_specs=[_ANY] * n,
        input_output_aliases={a: a for a in range(n)},
        scratch_shapes=[pltpu.SemaphoreType.DMA((n,)), pltpu.SemaphoreType.DMA((n,))],
    )(*halves)


def _group_sum(stacked, nrow, name):
    total, n = stacked.shape
    groups = total // nrow

    def body(g_ref, o_ref):
        acc = g_ref[0:nrow, :]
        for grp in range(1, groups):
            acc = acc + g_ref[grp * nrow:(grp + 1) * nrow, :]
        o_ref[...] = acc

    return pl.pallas_call(
        body, name=name,
        out_shape=jax.ShapeDtypeStruct((nrow, n), F32),
        compiler_params=pltpu.CompilerParams(vmem_limit_bytes=VMEM_LIMIT),
    )(stacked)


def _local_step(xt, tgt, mod, gains, w_pool, pool_scale, w_in, later_weights, on_ffn_grads, after_mixer_bwd,
                on_small_grads, seq):
    g_mpre, g_mpost, g_fpre, g_fpost = gains
    d = xt.shape[1]
    tm, tq = min(TOKEN_TILE, seq), min(ATTN_TILE, seq)

    h1, qn, k, v, u, kt, vt = _prenorm_proj(xt, mod, g_mpre, w_in, seq, tm)
    tk = min(ATTN_KEY_TILE, tq // 2)
    o, ltot = _attn_fwd(qn, k, vt, seq, tq, tk)
    w_out, order, ffn_weights = later_weights(o)
    w_out2 = w_out.reshape(d, d)
    pooled, mixin, mix, x1, h2 =_mixer_post(u, o, xt, mod, g_mpost, g_fpre + order, w_pool, pool_scale, w_out2, seq, tm)
    w_g, w_u, w_d = ffn_weights(h2)
    a, b, fin, dy, df, loss_blk, accb4, accg4 = _ffn_fwd(h2, w_g, w_u, w_d, x1, tgt, mod, g_fpost, seq, tm)
    da, db, dx1, dmix, accb5, accg5 = _ffn_bwd(df, a, b, w_d, w_g, w_u, x1, dy, mix, mod, g_fpre, g_mpost, seq, tm)
    bt = min(GRAD_TOKEN_TILE, xt.shape[0])
    bt_one = min(2 * GRAD_TOKEN_TILE, xt.shape[0])
    (g_g,), (g_g16,) = _tn_matmul(da, [h2], w_g.shape[0], bt_one, "grad_w_gate")
    (g_u,), (g_u16,) = _tn_matmul(db, [h2], w_u.shape[0], bt_one, "grad_w_up")
    (g_d,), (g_d16,) = _tn_matmul(fin, [df], w_d.shape[0], bt_one, "grad_w_down")
    token = on_ffn_grads([g_g, g_u, g_d], [g_g16, g_u16, g_d16])
    do, dpd, dps, dwp = _mixer_bwd(dmix, w_out2, pooled, w_pool, pool_scale + token, seq, tm)
    order = after_mixer_bwd(do)
    dq, dk, dv = _attn_bwd(qn, k, kt, v, do, ltot, seq, tq, tk, order)
    gx, du, accb8, accg8 = _inproj_bwd(dq, dk, dv, dpd, xt, dx1, mod, g_mpre, w_in, seq, tm)

    dmod = jnp.stack([accb8[:, 0], accb8[:, 1], accb5[:, 2], accb5[:, 0], accb5[:, 1], accb4[:, 0]], axis=1)
    dgain = jnp.stack([accg8[0], accg5[1], accg5[0], accg4[0]], axis=0)
    behind = on_small_grads(loss_blk, dmod, dgain, dps[0:1], dwp)
    g_in, g_in16 = _tn_matmul_stacked(h1, [dq, dk, dv, du], bt, "grad_w_in", behind)
    g_out, g_out16 = [parts[0].reshape(w_out.shape)
                      for parts in _tn_matmul(mixin, [dmix], 1, bt_one, "grad_w_out", behind)]
    grads = [g_in, g_out, g_g, g_u, g_d]
    grads16 = [g_in16, g_out16, g_g16, g_u16, g_d16]
    return gx, grads, grads16


def kernel(x, c, w_cond, b_cond, g_mix_pre, g_mix_post, w_in, w_pool, pool_scale, w_out, g_ffn_pre, g_ffn_post, w_gate, w_up, w_down, loss_target, m_w_cond, m_b_cond, m_g_mix_pre, m_g_mix_post, m_w_in, m_w_pool, m_pool_scale, m_w_out, m_g_ffn_pre, m_g_ffn_post, m_w_gate, m_w_up, m_w_down, v_w_cond, v_b_cond, v_g_mix_pre, v_g_mix_post, v_w_in, v_w_pool, v_pool_scale, v_w_out, v_g_ffn_pre, v_g_ffn_post, v_w_gate, v_w_up, v_w_down):
    xi, yi, ci = _position()
    chip = 2 * xi + yi
    dev = 4 * xi + 2 * yi + ci
    nb, seq, d = x.shape
    t_all = nb * seq
    xt = x.reshape(t_all, d)
    tgt = loss_target.reshape(t_all, d)
    ncol = w_cond.shape[2]
    pw = pool_scale.shape[1]

    place = jnp.stack([chip, ci]).astype(jnp.int32)
    turned = lambda t: jnp.swapaxes(t[0], 0, 1)
    in_sems = _gather_start(_place_quarters(place, [w_in[0]], "in"), [], "in")
    placed_rest = _place_quarters(place, [w_out[0], turned(w_gate), turned(w_up), w_down[0]], "rest", [in_sems[3]])

    c_pad = jnp.concatenate([c, jnp.zeros((8 - nb, d), F32)], axis=0)
    c_all = _all_gather(c_pad, "gather_c", placed_rest[:1]).reshape(N_DEV, 8, d)[:, :nb].reshape(N_DEV * nb, d)
    b_q = lax.dynamic_slice(b_cond, (0, chip * ncol), (1, ncol))
    sc_all, mod_q = _cond_fwd(c_all, w_cond[0], b_q, 512)
    mod_parts = _all_gather(mod_q, "gather_mod").reshape(N_DEV, N_DEV * nb, ncol)
    mod_rows = lax.dynamic_slice(mod_parts, (0, dev * nb, 0), (N_DEV, nb, ncol))[0::2]
    mod = jnp.transpose(mod_rows, (1, 0, 2)).reshape(nb, N_MOD, d)
    mod = jnp.concatenate([mod, jnp.zeros((nb, MOD_ROWS - N_MOD, d), F32)], axis=1)

    (w_in_all,) = _gather_forward(_gather_wait(*in_sems[:3], mod, "in"), "in")
    send_sems, recv_sems, in_flight, token = _gather_start(placed_rest, [mod, w_in_all], "rest")
    mod = mod + token[0:1, 0:1]

    def later_weights(after):
        waited = _gather_wait(send_sems, recv_sems, in_flight, after, "rest")
        (w_out_all,) = _gather_forward(waited[:1], "out")
        shapes = [b.shape[1:] for b in waited[1:]]

        def parts(refs, px, py, which):
            return [refs[a].at[2 * cx + cy, _quarter_halves(shapes, a, which), :]
                    for a in range(len(refs)) for cx, cy in [(1 - px, py), (px, 1 - py), (1 - px, 1 - py)]]

        forward = _sibling_split_start(waited[1:], parts, 3 * len(shapes), [w_out_all], "ffn_weights")
        finish = lambda after2: _sibling_split_wait(*forward[:3], parts, after2, "ffn_weights")
        return w_out_all, forward[3][0:1, 0:1], finish

    ffn_split = []

    ffn_sibling = []

    def on_ffn_grads(ffn_grads, ffn_grads16):
        ffn_sibling.extend(_sibling_exchange_start(ffn_grads16, "ffn"))
        ffn_sibling.append(ffn_grads)
        return ffn_sibling[4][0:1, 0:1]

    def after_mixer_bwd(do):
        theirs = _sibling_exchange_wait(*ffn_sibling[:4], do, "ffn")
        ffn_split.extend(_chip_exchange_start(_chip_sums(place[1:], ffn_sibling[5], theirs, "ffn"), [], "ffn"))
        return ffn_split[4][0:1, 0:1]

    wp_rows = w_pool[0].size // d
    loss_row = 2 * N_MOD + 4 + 1
    pad_rows = 24 - (loss_row + 1)
    prow = 24 + wp_rows
    small_split = []

    def on_small_grads(loss_blk, dmod, dgain, dps, dwp):
        payload = jnp.concatenate([
            dmod.reshape(nb * N_MOD, d), dgain,
            jnp.concatenate([dps, jnp.zeros((1, d - pw), F32)], axis=1),
            jnp.concatenate([loss_blk[0:1], jnp.zeros((1, d - LANES), F32)], axis=1),
            jnp.zeros((pad_rows, d), F32),
            jnp.concatenate(jnp.split(dwp.reshape(-1, dwp.shape[-1]), d // dwp.shape[-1], axis=0), axis=1)], axis=0)
        slots = lax.dynamic_update_slice(lax.empty((N_DEV, prow, d), F32), payload[None], (dev, 0, 0))
        small_split.extend(_direct_gather_start(slots))
        return [small_split[3]]

    gains = (g_mix_pre, g_mix_post, g_ffn_pre, g_ffn_post)
    gx, grads, grads16 = _local_step(
        xt, tgt, mod, gains, w_pool[0], pool_scale, w_in_all, later_weights, on_ffn_grads, after_mixer_bwd,
        on_small_grads, seq)

    sums_ffn, parts_ffn = _chip_exchange_wait(*ffn_split[:4], gx, "ffn")
    gathered = _direct_gather_wait(*small_split[:3], grads16[1]).reshape(N_DEV * prow, d)
    summed = _group_sum(gathered, prow, "small_device_sum")
    loss = summed[loss_row, 0]
    dmod_all = gathered.reshape(N_DEV, prow, d)[:, :nb * N_MOD].reshape(N_DEV * nb, N_MOD * d)
    dmod_q = lax.dynamic_slice(dmod_all, (0, chip * ncol), (N_DEV * nb, ncol))
    g_w_cond = _cond_bwd(sc_all, dmod_q, 512)
    first_gain = 2 * N_MOD

    theirs = _sibling_exchange(grads16[:2], "mix")
    mix_split = _chip_exchange_start(_chip_sums(place[1:], grads[:2], theirs, "mix"), [gathered], "mix")
    unfold = lambda halves: [g.reshape(2 * g.shape[1], g.shape[2]) for g in halves]
    share_parts = lambda refs, px, py, which: [r.at[which] for r in refs]
    halves_ffn = _total_sums(place, sums_ffn, parts_ffn, [mix_split[4]], "ffn")
    share = _sibling_split_start(halves_ffn, share_parts, len(halves_ffn), [], "share_ffn")

    results = {}

    def update(name, w2, g2, m2, v2, shape, after=()):
        delta, new_m, new_v = _adamw(w2, g2, m2, v2, "adamw_" + name, after)
        back = (lambda t: jnp.swapaxes(t, 0, 1)[None]) if shape is None else (lambda t: t.reshape(shape))
        results[name] = [back(t) for t in (g2, delta, new_m, new_v)]
        return delta

    done_cond = update("w_cond", w_cond[0], g_w_cond, m_w_cond[0], v_w_cond[0], w_cond.shape, [share[3]])
    g_ffn = unfold(_sibling_split_wait(*share[:3], share_parts, done_cond, "share_ffn"))
    done = [update("w_gate", turned(w_gate), g_ffn[0], turned(m_w_gate), turned(v_w_gate), None),
            update("w_up", turned(w_up), g_ffn[1], turned(m_w_up), turned(v_w_up), None),
            update("w_down", w_down[0], g_ffn[2], m_w_down[0], v_w_down[0], w_down.shape)]

    gain_row = lambda r: (lambda s: s[first_gain + r:first_gain + r + 1, :])
    small = [
        ("b_cond", (b_cond, m_b_cond, v_b_cond), (N_MOD, d), lambda s: s[0:N_MOD, :] + s[N_MOD:2 * N_MOD, :]),
        ("g_mix_pre", (g_mix_pre, m_g_mix_pre, v_g_mix_pre), (1, d), gain_row(0)),
        ("g_mix_post", (g_mix_post, m_g_mix_post, v_g_mix_post), (1, d), gain_row(1)),
        ("g_ffn_pre", (g_ffn_pre, m_g_ffn_pre, v_g_ffn_pre), (1, d), gain_row(2)),
        ("g_ffn_post", (g_ffn_post, m_g_ffn_post, v_g_ffn_post), (1, d), gain_row(3)),
        ("pool_scale", (pool_scale, m_pool_scale, v_pool_scale), (1, pw),
         lambda s: s[first_gain + 4:first_gain + 5, 0:pw]),
        ("w_pool", (w_pool, m_w_pool, v_w_pool), (wp_rows * d // w_pool.shape[-1], w_pool.shape[-1]),
         lambda s: jnp.concatenate([s[24:24 + wp_rows, j * w_pool.shape[-1]:(j + 1) * w_pool.shape[-1]]
                                    for j in range(d // w_pool.shape[-1])], axis=0)),
    ]
    updated = _small_updates(summed, [tuple(t.reshape(flat) for t in wmv) + (pick,) for _, wmv, flat, pick in small])
    for (name, wmv, _, _), quad in zip(small, updated):
        results[name] = [t.reshape(wmv[0].shape) for t in quad]

    sums_mix, parts_mix = _chip_exchange_wait(*mix_split[:4], done[-1], "mix")
    g_mix = unfold(_sibling_share(_total_sums(place, sums_mix, parts_mix, done, "mix"), "mix"))
    update("w_in", w_in[0], g_mix[0], m_w_in[0], v_w_in[0], w_in.shape)
    update("w_out", w_out[0], g_mix[1], m_w_out[0], v_w_out[0], w_out.shape)

    names = ("w_cond", "b_cond", "g_mix_pre", "g_mix_post", "w_in", "w_pool", "pool_scale", "w_out",
             "g_ffn_pre", "g_ffn_post", "w_gate", "w_up", "w_down")
    outs = [results[name][part] for part in range(4) for name in names]
    return (loss, gx.reshape(x.shape), *outs)
```

```python
import jax
import jax.numpy as jnp
import numpy as np
from jax import lax
from jax.experimental import pallas as pl
from jax.experimental.pallas import tpu as pltpu

F32 = jnp.float32
BF16 = jnp.bfloat16
MESH = pl.DeviceIdType.MESH

EPS = 1e-6
HEAD_DIM = 64
HEADS_PER_BLOCK = 2
LANES = 128
NEG_QK_SCALE = -0.125
POOL_WINDOWS = (2, 4, 8, 16)
POOL_GROUP = 128
HALO = 16
N_MOD = 6
MOD_ROWS = 8
N_CHIPS = 4
N_DEV = 8
VMEM_LIMIT = 56 * 1024 * 1024

ADAM_LR = 0.001
ADAM_B1 = 0.9
ADAM_B2 = 0.999
ADAM_EPS = 1e-08
ADAM_WD = 0.01
ADAM_STEP = 10

TOKEN_TILE = 512
GRAD_TOKEN_TILE = 2048
FFN_ROW_CHUNKS = 2
ROW_CHUNKS = 2
ATTN_TILE = 512
ATTN_KEY_TILE = 128
ATTN_ROW_CHUNK = 32
LOG_SUM_PASSES = 1


def _dot(a, b):
    return jnp.dot(a, b, preferred_element_type=F32)


def _dot_nt(a, b):
    return lax.dot_general(a, b, (((1,), (1,)), ((), ())), preferred_element_type=F32)


def _dot_tn(a, b):
    return lax.dot_general(a, b, (((0,), (0,)), ((), ())), preferred_element_type=F32)


def _split(v):
    hi = v.astype(BF16)
    lo = (v - hi.astype(F32)).astype(BF16)
    return hi, lo


def _rms(v):
    return lax.rsqrt(jnp.mean(v * v, axis=-1, keepdims=True) + EPS)


def _norm_bwd(dn, n, r):
    return r * (dn - n * jnp.mean(dn * n, axis=-1, keepdims=True))


def _sigmoid(v):
    return 0.5 * jnp.tanh(0.5 * v) + 0.5


def _colsum(v):
    return jnp.sum(v, axis=0, keepdims=True)


def _params(sem=None):
    return pltpu.CompilerParams(dimension_semantics=sem, vmem_limit_bytes=VMEM_LIMIT)


def _position():
    return lax.axis_index("x"), lax.axis_index("y"), lax.axis_index("c")


def _prenorm_proj(x, mod, g_pre, w_in, seq, tm):
    t_all, d = x.shape
    nt = seq // tm
    p = w_in.shape[2]

    def body(x_ref, mod_ref, g_ref, w_ref, h_ref, q_ref, k_ref, v_ref, u_ref, kt_ref, vt_ref):
        for c in range(ROW_CHUNKS):
            rows = slice(c * (tm // ROW_CHUNKS), (c + 1) * (tm // ROW_CHUNKS))
            xf = x_ref[rows, :]
            n = xf * _rms(xf)
            h = (n * g_ref[...]) * (1.0 + mod_ref[0, 1:2, :]) + mod_ref[0, 0:1, :]
            hb = h.astype(BF16)
            h_ref[rows, :] = hb
            q_ref[rows, :] = (_dot(hb, w_ref[0]) * NEG_QK_SCALE).astype(BF16)
            kf = _dot(hb, w_ref[1])
            vf = _dot(hb, w_ref[2])
            k_ref[rows, :] = kf.astype(BF16)
            v_ref[rows, :] = vf.astype(BF16)
            kt_ref[:, rows] = kf.T.astype(BF16)
            vt_ref[:, rows] = vf.T.astype(BF16)
            u_ref[rows, :] = _dot(hb, w_ref[3])

    tok = lambda i: (i, 0)
    tok_t = lambda i: (0, i)
    return pl.pallas_call(
        body, name="prenorm_proj", grid=(t_all // tm,),
        in_specs=[pl.BlockSpec((tm, d), tok),
                  pl.BlockSpec((1, MOD_ROWS, d), lambda i: (i // nt, 0, 0)),
                  pl.BlockSpec((1, d), lambda i: (0, 0)),
                  pl.BlockSpec((N_CHIPS, d, p), lambda i: (0, 0, 0))],
        out_specs=[pl.BlockSpec((tm, d), tok)] + [pl.BlockSpec((tm, p), tok)] * 4 + [pl.BlockSpec((p, tm), tok_t)] * 2,
        out_shape=[jax.ShapeDtypeStruct((t_all, d), BF16)] + [jax.ShapeDtypeStruct((t_all, p), BF16)] * 3
        + [jax.ShapeDtypeStruct((t_all, p), F32)] + [jax.ShapeDtypeStruct((p, t_all), BF16)] * 2,
        compiler_params=_params(("arbitrary",)),
    )(x, mod, g_pre, w_in)


def _tri_matrix(tk, kind):
    j = np.arange(2 * tk)[:, None] % tk
    s = np.arange(tk)[None, :]
    return jnp.asarray({"after": j > s, "upto": j <= s, "before": j < s}[kind], dtype=BF16)


def _neg_abs(v):
    bits = lax.bitcast_convert_type(v, jnp.int32) | jnp.int32(-2 ** 31)
    return lax.bitcast_convert_type(bits, F32)


def _row_sums(v):
    return jnp.broadcast_to(jnp.sum(v, axis=-1, keepdims=True), (v.shape[0], LANES))


def _across(v, n):
    return jnp.concatenate([v] * (n // LANES), axis=1)


def _all_masked(c, diag, rc, tk):
    return diag is not None and diag * tk >= (c + 1) * rc - 1


def _some_masked(c, diag, rc, tk):
    return diag is not None and diag * tk + tk - 1 >= c * rc


def _attn_fwd(qn, k, vt, seq, tq, tk):
    t_all, w = qn.shape
    nb, nq, ndiag = t_all // seq, seq // tq, tq // tk
    assert ndiag % 2 == 0, "two key blocks per loop trip"
    rc = ATTN_ROW_CHUNK
    heads = range(HEADS_PER_BLOCK)

    def body(q_ref, k_ref, vt_ref, tri_ref, o_ref, l_ref,
             z_buf, ls_buf, hl_buf, aft_buf, w_buf, tot_buf, acc_t, run_buf):
        i = pl.program_id(2)
        nblk = (i + 1) * ndiag
        lane = lax.broadcasted_iota(jnp.int32, (1, LANES), 1)
        row = lax.broadcasted_iota(jnp.int32, (rc, tk), 0)
        col = lax.broadcasted_iota(jnp.int32, (rc, tk), 1)
        first = lane < HEAD_DIM
        q2 = q_ref[...]
        qs = [jnp.where(first, q2, jnp.zeros_like(q2)), jnp.where(first, jnp.zeros_like(q2), q2)]
        acc_t[...] = jnp.zeros_like(acc_t)
        run_buf[...] = jnp.zeros_like(run_buf)
        w_buf[1] = jnp.zeros((HEADS_PER_BLOCK, tq, tk), BF16)

        def causal(c, diag):
            return (col + diag * tk) < (row + c * rc)

        def scores(blk, slot):
            kj = k_ref[pl.ds(pl.multiple_of(blk * tk, tk), tk), :]
            for h in heads:
                z_buf[slot, h] = _dot_nt(qs[h], kj)

        def values(blk, slot):
            keys = pl.ds(pl.multiple_of(blk * tk, tk), tk)
            for h in heads:
                dims = slice(h * HEAD_DIM, (h + 1) * HEAD_DIM)
                acc_t[dims, :] += _dot_nt(vt_ref[dims, keys], w_buf[slot, h])

        def softplus_stage(h, slot, diag):
            for c in range(tq // rc):
                rows = slice(c * rc, (c + 1) * rc)
                if _all_masked(c, diag, rc, tk):
                    hl_buf[h, rows, :] = jnp.zeros((rc, LOG_SUM_PASSES * tk), BF16)
                    tot_buf[h, rows, :] = jnp.zeros((rc, LANES), F32)
                    continue
                nz = z_buf[slot, h, rows, :]
                l1 = jnp.minimum(nz, 0.0) - jnp.log(1.0 + jnp.exp(_neg_abs(nz)))
                if _some_masked(c, diag, rc, tk):
                    l1 = jnp.where(causal(c, diag), l1, 0.0)
                for s, part in enumerate(_split(l1)[:LOG_SUM_PASSES]):
                    hl_buf[h, rows, s * tk:(s + 1) * tk] = part
                ls_buf[h, rows, :] = l1 - nz
                tot_buf[h, rows, :] = _row_sums(l1)

        def weights_stage(h, slot, diag):
            for c in range(tq // rc):
                rows = slice(c * rc, (c + 1) * rc)
                if _all_masked(c, diag, rc, tk):
                    w_buf[slot, h, rows, :] = jnp.zeros((rc, tk), BF16)
                    continue
                wgt = jnp.exp((ls_buf[h, rows, :] + aft_buf[h, rows, :]) + _across(run_buf[h, rows, :], tk))
                if _some_masked(c, diag, rc, tk):
                    wgt = jnp.where(causal(c, diag), wgt, 0.0)
                w_buf[slot, h, rows, :] = wgt.astype(BF16)
                run_buf[h, rows, :] += tot_buf[h, rows, :]

        def position(blk, slot, diag):
            scores(jnp.maximum(blk - 1, 0), 1 - slot)
            for h in heads:
                softplus_stage(h, slot, diag)
                aft_buf[h] = _dot(hl_buf[h], tri_ref[...])
            values(jnp.minimum(blk + 1, nblk - 1), 1 - slot)
            for h in heads:
                weights_stage(h, slot, diag)

        scores(nblk - 1, 0)
        for p in range(ndiag):
            position(nblk - 1 - p, p % 2, ndiag - 1 - p)

        def trip(jj, carry):
            for u in range(2):
                position(i * ndiag - 1 - 2 * jj - u, u, None)
            return carry

        lax.fori_loop(0, (i * ndiag) // 2, trip, 0)
        values(0, 1)
        o_ref[...] = acc_t[...].T.astype(BF16)
        l_ref[...] = jnp.where(first, run_buf[0], run_buf[1])

    qmap = lambda b, hp, i: (b * nq + i, hp)
    nh = HEADS_PER_BLOCK
    return pl.pallas_call(
        body, name="attn_fwd", grid=(nb, w // LANES, nq),
        in_specs=[pl.BlockSpec((tq, LANES), qmap), pl.BlockSpec((seq, LANES), lambda b, hp, i: (b, hp)),
                  pl.BlockSpec((LANES, seq), lambda b, hp, i: (hp, b)),
                  pl.BlockSpec((LOG_SUM_PASSES * tk, tk), lambda b, hp, i: (0, 0))],
        out_specs=[pl.BlockSpec((tq, LANES), qmap), pl.BlockSpec((tq, LANES), qmap)],
        out_shape=[jax.ShapeDtypeStruct((t_all, w), BF16), jax.ShapeDtypeStruct((t_all, w), F32)],
        scratch_shapes=[pltpu.VMEM((2, nh, tq, tk), F32), pltpu.VMEM((nh, tq, tk), F32),
                        pltpu.VMEM((nh, tq, LOG_SUM_PASSES * tk), BF16), pltpu.VMEM((nh, tq, tk), F32),
                        pltpu.VMEM((2, nh, tq, tk), BF16), pltpu.VMEM((nh, tq, LANES), F32),
                        pltpu.VMEM((LANES, tq), F32), pltpu.VMEM((nh, tq, LANES), F32)],
        compiler_params=_params(("arbitrary", "arbitrary", "arbitrary")),
    )(qn, k, vt, _tri_matrix(tk, "after")[:LOG_SUM_PASSES * tk])


def _window_sums(ext, rows, offset, forward):
    r = lax.broadcasted_iota(jnp.int32, (rows, rows + HALO), 0)
    e = lax.broadcasted_iota(jnp.int32, (rows, rows + HALO), 1)
    hi, lo = _split(ext)
    out = []
    for g, win in enumerate(POOL_WINDOWS):
        if forward:
            band = (e >= r) & (e < r + win)
        else:
            band = (e <= r + offset) & (e > r + offset - win)
        bm = band.astype(BF16)
        cols = slice(g * POOL_GROUP, (g + 1) * POOL_GROUP)
        out.append(_dot(bm, hi[:, cols]) + _dot(bm, lo[:, cols]))
    return out


def _window_counts(pos):
    return [jnp.minimum(pos + 1, win).astype(F32) for win in POOL_WINDOWS]


def _mixer_post(u, o, x, mod, g_post, g_fpre, w_pool, pool_scale, w_out, seq, tm):
    t_all, d = x.shape
    nt = seq // tm
    p = u.shape[1]

    def body(u_ref, halo_ref, o_ref, x_ref, mod_ref, gp_ref, gf_ref, wp_ref, ps_ref, wo_ref,
             pooled_ref, mixin_ref, mix_ref, x1_ref, h2_ref):
        it = pl.program_id(0) % nt
        uf = u_ref[...]
        halo = jnp.where(it == 0, 0.0, halo_ref[...])
        ext = jnp.concatenate([halo, uf], axis=0)
        pos = it * tm + lax.broadcasted_iota(jnp.int32, (tm, 1), 0)
        sums = _window_sums(ext, tm, HALO, False)
        cnts = _window_counts(pos)
        pools = []
        for g in range(len(POOL_WINDOWS)):
            cols = slice(g * POOL_GROUP, (g + 1) * POOL_GROUP)
            pooled = (sums[g] / cnts[g] - uf[:, cols]).astype(BF16)
            pooled_ref[:, cols] = pooled
            yg = _dot(pooled, wp_ref[g].astype(BF16))
            pools.append((yg * ps_ref[:, cols]).astype(BF16))
        mixin_ref[...] = jnp.concatenate([o_ref[...]] + pools, axis=1)
        for c in range(ROW_CHUNKS):
            rows = slice(c * (tm // ROW_CHUNKS), (c + 1) * (tm // ROW_CHUNKS))
            mix = _dot(mixin_ref[rows, :], wo_ref[...])
            mix_ref[rows, :] = mix
            n2 = mix * _rms(mix)
            x1 = x_ref[rows, :] + mod_ref[0, 2:3, :] * (n2 * gp_ref[...])
            x1_ref[rows, :] = x1
            n3 = x1 * _rms(x1)
            h2 = (n3 * gf_ref[...]) * (1.0 + mod_ref[0, 4:5, :]) + mod_ref[0, 3:4, :]
            h2_ref[rows, :] = h2.astype(BF16)

    tok = lambda i: (i, 0)
    const2 = lambda i: (0, 0)
    hb = tm // HALO
    return pl.pallas_call(
        body, name="mixer_post", grid=(t_all // tm,),
        in_specs=[pl.BlockSpec((tm, p), tok),
                  pl.BlockSpec((HALO, p), lambda i: (jnp.maximum(i * hb - 1, 0), 0)),
                  pl.BlockSpec((tm, p), tok),
                  pl.BlockSpec((tm, d), tok),
                  pl.BlockSpec((1, MOD_ROWS, d), lambda i: (i // nt, 0, 0)),
                  pl.BlockSpec((1, d), const2), pl.BlockSpec((1, d), const2),
                  pl.BlockSpec(w_pool.shape, lambda i: (0, 0, 0)),
                  pl.BlockSpec((1, p), const2),
                  pl.BlockSpec((d, d), const2)],
        out_specs=[pl.BlockSpec((tm, p), tok), pl.BlockSpec((tm, d), tok), pl.BlockSpec((tm, d), tok),
                   pl.BlockSpec((tm, d), tok), pl.BlockSpec((tm, d), tok)],
        out_shape=[jax.ShapeDtypeStruct((t_all, p), BF16), jax.ShapeDtypeStruct((t_all, d), BF16),
                   jax.ShapeDtypeStruct((t_all, d), F32), jax.ShapeDtypeStruct((t_all, d), F32),
                   jax.ShapeDtypeStruct((t_all, d), BF16)],
        compiler_params=_params(("arbitrary",)),
    )(u, u, o, x, mod, g_post, g_fpre, w_pool, pool_scale, w_out)


def _ffn_fwd(h2, w_g, w_u, w_d, x1, tgt, mod, g_post, seq, tm):
    t_all, d = x1.shape
    nt = seq // tm
    nk, ff, _ = w_g.shape

    def body(h_ref, wg_ref, wu_ref, wd_ref, x1_ref, t_ref, mod_ref, g_ref,
             a_ref, b_ref, fin_ref, dy_ref, df_ref, loss_ref, accb_ref, accg_ref, facc):
        i, k = pl.program_id(0), pl.program_id(1)

        @pl.when(k == 0)
        def _():
            facc[...] = jnp.zeros_like(facc)

        for c in range(FFN_ROW_CHUNKS):
            rows = slice(c * (tm // FFN_ROW_CHUNKS), (c + 1) * (tm // FFN_ROW_CHUNKS))
            hb = h_ref[rows, :]
            a = _dot_nt(hb, wg_ref[0])
            b = _dot_nt(hb, wu_ref[0])
            a_ref[0, rows, :] = a.astype(BF16)
            b_ref[0, rows, :] = b.astype(BF16)
            fin = ((a * _sigmoid(a)) * b).astype(BF16)
            fin_ref[0, rows, :] = fin
            facc[rows, :] += _dot(fin, wd_ref[0])

        @pl.when(k == nk - 1)
        def _():
            f = facc[...]
            r4 = _rms(f)
            n4 = f * r4
            gate = mod_ref[0, 5:6, :]
            g = g_ref[...]
            err = (x1_ref[...] + gate * (n4 * g)) - t_ref[...]
            dy = err * (1.0 / d)
            dy_ref[...] = dy

            @pl.when(i == 0)
            def _():
                loss_ref[...] = jnp.zeros_like(loss_ref)
                accg_ref[...] = jnp.zeros_like(accg_ref)

            @pl.when(i % nt == 0)
            def _():
                accb_ref[...] = jnp.zeros_like(accb_ref)

            loss_ref[...] += (0.5 / d) * jnp.sum(err * err)
            accb_ref[0, 0:1, :] += _colsum(dy * (n4 * g))
            accg_ref[0:1, :] += _colsum((dy * gate) * n4)
            dn4 = (dy * gate) * g
            df_ref[...] = _norm_bwd(dn4, n4, r4).astype(BF16)

    tok = lambda i, k: (i, 0)
    ktok = lambda i, k: (k, i, 0)
    kw = lambda i, k: (k, 0, 0)
    const2 = lambda i, k: (0, 0)
    return pl.pallas_call(
        body, name="ffn_fwd", grid=(t_all // tm, nk),
        in_specs=[pl.BlockSpec((tm, d), tok),
                  pl.BlockSpec((1, ff, d), kw), pl.BlockSpec((1, ff, d), kw), pl.BlockSpec((1, ff, d), kw),
                  pl.BlockSpec((tm, d), tok), pl.BlockSpec((tm, d), tok),
                  pl.BlockSpec((1, MOD_ROWS, d), lambda i, k: (i // nt, 0, 0)),
                  pl.BlockSpec((1, d), const2)],
        out_specs=[pl.BlockSpec((1, tm, ff), ktok)] * 3
        + [pl.BlockSpec((tm, d), tok), pl.BlockSpec((tm, d), tok),
           pl.BlockSpec((8, LANES), const2),
           pl.BlockSpec((1, 8, d), lambda i, k: (i // nt, 0, 0)),
           pl.BlockSpec((8, d), const2)],
        out_shape=[jax.ShapeDtypeStruct((nk, t_all, ff), BF16)] * 3
        + [jax.ShapeDtypeStruct((t_all, d), F32), jax.ShapeDtypeStruct((t_all, d), BF16),
           jax.ShapeDtypeStruct((8, LANES), F32),
           jax.ShapeDtypeStruct((t_all // seq, 8, d), F32),
           jax.ShapeDtypeStruct((8, d), F32)],
        scratch_shapes=[pltpu.VMEM((tm, d), F32)],
        compiler_params=_params(("arbitrary", "arbitrary")),
    )(h2, w_g, w_u, w_d, x1, tgt, mod, g_post)


def _ffn_bwd(df, a, b, w_d, w_g, w_u, x1, dy, mix, mod, g_fpre, g_mpost, seq, tm):
    t_all, d = x1.shape
    nt = seq // tm
    nk, ff, _ = w_g.shape

    def body(df_ref, a_ref, b_ref, wd_ref, wg_ref, wu_ref, x1_ref, dy_ref, mix_ref, mod_ref, gf_ref, gm_ref,
             da_ref, db_ref, dx1_ref, dmix_ref, accb_ref, accg_ref, hacc):
        i, k = pl.program_id(0), pl.program_id(1)

        @pl.when(k == 0)
        def _():
            hacc[...] = jnp.zeros_like(hacc)

        for c in range(FFN_ROW_CHUNKS):
            rows = slice(c * (tm // FFN_ROW_CHUNKS), (c + 1) * (tm // FFN_ROW_CHUNKS))
            dfin = _dot_nt(df_ref[rows, :], wd_ref[0])
            af = a_ref[0, rows, :].astype(F32)
            bf = b_ref[0, rows, :].astype(F32)
            sig = _sigmoid(af)
            da = ((dfin * bf) * (sig * (1.0 + af * (1.0 - sig)))).astype(BF16)
            db = (dfin * (af * sig)).astype(BF16)
            da_ref[0, rows, :] = da
            db_ref[0, rows, :] = db
            hacc[rows, :] += _dot(da, wg_ref[0]) + _dot(db, wu_ref[0])

        @pl.when(k == nk - 1)
        def _():
            @pl.when(i == 0)
            def _():
                accg_ref[...] = jnp.zeros_like(accg_ref)

            @pl.when(i % nt == 0)
            def _():
                accb_ref[...] = jnp.zeros_like(accb_ref)

            dh2 = hacc[...]
            x1 = x1_ref[...]
            r3 = _rms(x1)
            n3 = x1 * r3
            g3 = gf_ref[...]
            scale1 = 1.0 + mod_ref[0, 4:5, :]
            accb_ref[0, 0:1, :] += _colsum(dh2)
            accb_ref[0, 1:2, :] += _colsum(dh2 * (n3 * g3))
            accg_ref[0:1, :] += _colsum((dh2 * scale1) * n3)
            dx1 = dy_ref[...] + _norm_bwd((dh2 * scale1) * g3, n3, r3)
            dx1_ref[...] = dx1
            mix = mix_ref[...]
            r2 = _rms(mix)
            n2 = mix * r2
            g2 = gm_ref[...]
            gate = mod_ref[0, 2:3, :]
            accb_ref[0, 2:3, :] += _colsum(dx1 * (n2 * g2))
            accg_ref[1:2, :] += _colsum((dx1 * gate) * n2)
            dmix_ref[...] = _norm_bwd((dx1 * gate) * g2, n2, r2).astype(BF16)

    tok = lambda i, k: (i, 0)
    ktok = lambda i, k: (k, i, 0)
    kw = lambda i, k: (k, 0, 0)
    const2 = lambda i, k: (0, 0)
    return pl.pallas_call(
        body, name="ffn_bwd", grid=(t_all // tm, nk),
        in_specs=[pl.BlockSpec((tm, d), tok),
                  pl.BlockSpec((1, tm, ff), ktok), pl.BlockSpec((1, tm, ff), ktok),
                  pl.BlockSpec((1, ff, d), kw), pl.BlockSpec((1, ff, d), kw), pl.BlockSpec((1, ff, d), kw),
                  pl.BlockSpec((tm, d), tok), pl.BlockSpec((tm, d), tok), pl.BlockSpec((tm, d), tok),
                  pl.BlockSpec((1, MOD_ROWS, d), lambda i, k: (i // nt, 0, 0)),
                  pl.BlockSpec((1, d), const2), pl.BlockSpec((1, d), const2)],
        out_specs=[pl.BlockSpec((1, tm, ff), ktok)] * 2
        + [pl.BlockSpec((tm, d), tok), pl.BlockSpec((tm, d), tok),
           pl.BlockSpec((1, 8, d), lambda i, k: (i // nt, 0, 0)),
           pl.BlockSpec((8, d), const2)],
        out_shape=[jax.ShapeDtypeStruct((nk, t_all, ff), BF16)] * 2
        + [jax.ShapeDtypeStruct((t_all, d), F32), jax.ShapeDtypeStruct((t_all, d), BF16),
           jax.ShapeDtypeStruct((t_all // seq, 8, d), F32),
           jax.ShapeDtypeStruct((8, d), F32)],
        scratch_shapes=[pltpu.VMEM((tm, d), F32)],
        compiler_params=_params(("arbitrary", "arbitrary")),
    )(df, a, b, w_d, w_g, w_u, x1, dy, mix, mod, g_fpre, g_mpost)


def _mixer_bwd(dmix, w_out, pooled, w_pool, pool_scale, seq, tm):
    t_all, d = dmix.shape
    p = pooled.shape[1]
    ng = len(POOL_WINDOWS)

    def body(dm_ref, wo_ref, pooled_ref, wp_ref, ps_ref, do_ref, dpd_ref, dps_ref, dwp_ref):
        i = pl.program_id(0)

        @pl.when(i == 0)
        def _():
            dps_ref[...] = jnp.zeros_like(dps_ref)
            dwp_ref[...] = jnp.zeros_like(dwp_ref)

        dmixin = _dot_nt(dm_ref[...], wo_ref[...])
        do_ref[...] = dmixin[:, :p].astype(BF16)
        for g in range(ng):
            cols = slice(g * POOL_GROUP, (g + 1) * POOL_GROUP)
            dpool = dmixin[:, p + g * POOL_GROUP:p + (g + 1) * POOL_GROUP]
            pooled = pooled_ref[:, cols]
            wpg = wp_ref[g].astype(BF16)
            yg = _dot(pooled, wpg)
            dps_ref[0:1, cols] += _colsum(dpool * yg)
            dyg = (dpool * ps_ref[:, cols]).astype(BF16)
            dwp_ref[g] += _dot_tn(pooled, dyg)
            dpd_ref[:, cols] = _dot_nt(dyg, wpg)

    tok = lambda i: (i, 0)
    const2 = lambda i: (0, 0)
    const3 = lambda i: (0, 0, 0)
    return pl.pallas_call(
        body, name="mixer_bwd", grid=(t_all // tm,),
        in_specs=[pl.BlockSpec((tm, d), tok), pl.BlockSpec((d, d), const2), pl.BlockSpec((tm, p), tok),
                  pl.BlockSpec(w_pool.shape, const3), pl.BlockSpec((1, p), const2)],
        out_specs=[pl.BlockSpec((tm, p), tok), pl.BlockSpec((tm, p), tok),
                   pl.BlockSpec((8, p), const2), pl.BlockSpec(w_pool.shape, const3)],
        out_shape=[jax.ShapeDtypeStruct((t_all, p), BF16), jax.ShapeDtypeStruct((t_all, p), F32),
                   jax.ShapeDtypeStruct((8, p), F32), jax.ShapeDtypeStruct(w_pool.shape, F32)],
        compiler_params=_params(("arbitrary",)),
    )(dmix, w_out, pooled, w_pool, pool_scale)


def _attn_bwd(qn, k, kt, v, do, ltot, seq, tq, tk, order):
    t_all, w = qn.shape
    nb, nq, ndiag, nkb = t_all // seq, seq // tq, tq // tk, seq // tk
    assert ndiag % 2 == 0, "two key blocks per loop trip"
    rc = ATTN_ROW_CHUNK
    nh = HEADS_PER_BLOCK
    heads = range(nh)

    def body(q_ref, k_ref, kt_ref, v_ref, do_ref, l_ref, up_ref, bf_ref, dq_ref, dk_ref, dv_ref,
             z_buf, dw_buf, ls_buf, hl_buf, upto_buf, g_buf, gb_buf, before_buf, w_buf, dz_buf,
             totl_buf, totg_buf, rem_buf, preg_buf, qnt_buf, dot_buf, dq_t, dk_t, dv_t):
        i = pl.program_id(2)
        nblk = (i + 1) * ndiag

        @pl.when(i == 0)
        def _():
            dk_t[...] = jnp.zeros_like(dk_t)
            dv_t[...] = jnp.zeros_like(dv_t)

        lane = lax.broadcasted_iota(jnp.int32, (1, LANES), 1)
        row = lax.broadcasted_iota(jnp.int32, (rc, tk), 0)
        col = lax.broadcasted_iota(jnp.int32, (rc, tk), 1)
        first = lane < HEAD_DIM
        q2 = q_ref[...]
        do2 = do_ref[...]
        l2 = l_ref[...]
        qs = [jnp.where(first, q2, jnp.zeros_like(q2)), jnp.where(first, jnp.zeros_like(q2), q2)]
        dos = [jnp.where(first, do2, jnp.zeros_like(do2)), jnp.where(first, jnp.zeros_like(do2), do2)]
        qnt_buf[...] = q2.astype(F32).T.astype(BF16)
        dot_buf[...] = do2.astype(F32).T.astype(BF16)
        for h in heads:
            rem_buf[h] = jnp.where(first if h == 0 else ~first, l2, pltpu.roll(l2, HEAD_DIM, 1))
        preg_buf[...] = jnp.zeros_like(preg_buf)
        dq_t[...] = jnp.zeros_like(dq_t)
        w_buf[1] = jnp.zeros((nh * tq, tk), BF16)
        dz_buf[1] = jnp.zeros((nh * tq, tk), BF16)

        def causal(c, diag):
            return (col + diag * tk) < (row + c * rc)

        def scores(blk, slot):
            off = pl.multiple_of(blk * tk, tk)
            kj = k_ref[pl.ds(off, tk), :]
            vj = v_ref[pl.ds(off, tk), :]
            for h in heads:
                z_buf[slot, h] = _dot_nt(qs[h], kj)
                dw_buf[slot, h] = _dot_nt(dos[h], vj)

        def gradients(blk, slot):
            keys = pl.ds(pl.multiple_of(blk * tk, tk), tk)
            for h in heads:
                dims = slice(h * HEAD_DIM, (h + 1) * HEAD_DIM)
                queries = slice(h * tq, (h + 1) * tq)
                dq_t[dims, :] += _dot_nt(kt_ref[dims, keys], dz_buf[slot, queries, :])
                dk_t[blk, dims, :] += _dot(qnt_buf[dims, :], dz_buf[slot, queries, :])
                dv_t[blk, dims, :] += _dot(dot_buf[dims, :], w_buf[slot, queries, :])

        def softplus_stage(h, slot, diag):
            for c in range(tq // rc):
                rows = slice(c * rc, (c + 1) * rc)
                if _all_masked(c, diag, rc, tk):
                    hl_buf[h, rows, :] = jnp.zeros((rc, LOG_SUM_PASSES * tk), BF16)
                    continue
                nz = z_buf[slot, h, rows, :]
                l1 = jnp.minimum(nz, 0.0) - jnp.log(1.0 + jnp.exp(_neg_abs(nz)))
                if _some_masked(c, diag, rc, tk):
                    l1 = jnp.where(causal(c, diag), l1, 0.0)
                for s, part in enumerate(_split(l1)[:LOG_SUM_PASSES]):
                    hl_buf[h, rows, s * tk:(s + 1) * tk] = part
                ls_buf[h, rows, :] = l1 - nz
                totl_buf[h, rows, :] = _row_sums(l1)

        def weights_stage(h, slot, diag):
            for c in range(tq // rc):
                rows = slice(c * rc, (c + 1) * rc)
                stacked = slice(h * tq + c * rc, h * tq + (c + 1) * rc)
                if _all_masked(c, diag, rc, tk):
                    w_buf[slot, stacked, :] = jnp.zeros((rc, tk), BF16)
                    gb_buf[h, rows, :] = jnp.zeros((rc, tk), BF16)
                    continue
                wgt = jnp.exp(ls_buf[h, rows, :] + (_across(rem_buf[h, rows, :], tk) - upto_buf[h, rows, :]))
                if _some_masked(c, diag, rc, tk):
                    wgt = jnp.where(causal(c, diag), wgt, 0.0)
                w_buf[slot, stacked, :] = wgt.astype(BF16)
                g = wgt * dw_buf[slot, h, rows, :]
                g_buf[h, rows, :] = g
                gb_buf[h, rows, :] = g.astype(BF16)
                totg_buf[h, rows, :] = _row_sums(g)
                rem_buf[h, rows, :] -= totl_buf[h, rows, :]

        def dscore_stage(h, slot, diag):
            for c in range(tq // rc):
                rows = slice(c * rc, (c + 1) * rc)
                stacked = slice(h * tq + c * rc, h * tq + (c + 1) * rc)
                if _all_masked(c, diag, rc, tk):
                    dz_buf[slot, stacked, :] = jnp.zeros((rc, tk), BF16)
                    continue
                sig = jnp.exp(ls_buf[h, rows, :])
                g = g_buf[h, rows, :]
                dnz = sig * ((before_buf[h, rows, :] + _across(preg_buf[h, rows, :], tk)) + g) - g
                if _some_masked(c, diag, rc, tk):
                    dnz = jnp.where(causal(c, diag), dnz, 0.0)
                dz_buf[slot, stacked, :] = dnz.astype(BF16)
                preg_buf[h, rows, :] += totg_buf[h, rows, :]

        def position(blk, slot, diag, prefetch):
            if prefetch:
                scores(blk + 1, 1 - slot)
            for h in heads:
                softplus_stage(h, slot, diag)
                upto_buf[h] = _dot(hl_buf[h], up_ref[...])
            gradients(jnp.maximum(blk - 1, 0), 1 - slot)
            for h in heads:
                weights_stage(h, slot, diag)
                before_buf[h] = _dot(gb_buf[h], bf_ref[...])
            for h in heads:
                dscore_stage(h, slot, diag)

        scores(0, 0)

        def trip(jj, carry):
            for u in range(2):
                position(2 * jj + u, u, None, True)
            return carry

        lax.fori_loop(0, (i * ndiag) // 2, trip, 0)
        for d in range(ndiag):
            position(i * ndiag + d, d % 2, d, d < ndiag - 1)
        gradients(nblk - 1, 1)
        dq_ref[...] = (dq_t[...].T * NEG_QK_SCALE).astype(BF16)

        @pl.when(i == nq - 1)
        def _():
            for blk in range(nkb):
                dk_ref[blk * tk:(blk + 1) * tk, :] = dk_t[blk].T.astype(BF16)
                dv_ref[blk * tk:(blk + 1) * tk, :] = dv_t[blk].T.astype(BF16)

    qmap = lambda b, hp, i: (b * nq + i, hp)
    kmap = lambda b, hp, i: (b, hp)
    const = lambda b, hp, i: (0, 0)
    return pl.pallas_call(
        body, name="attn_bwd", grid=(nb, w // LANES, nq),
        in_specs=[pl.BlockSpec((tq, LANES), qmap), pl.BlockSpec((seq, LANES), kmap),
                  pl.BlockSpec((LANES, seq), lambda b, hp, i: (hp, b)), pl.BlockSpec((seq, LANES), kmap),
                  pl.BlockSpec((tq, LANES), qmap), pl.BlockSpec((tq, LANES), qmap),
                  pl.BlockSpec((LOG_SUM_PASSES * tk, tk), const), pl.BlockSpec((tk, tk), const)],
        out_specs=[pl.BlockSpec((tq, LANES), qmap), pl.BlockSpec((seq, LANES), kmap), pl.BlockSpec((seq, LANES), kmap)],
        out_shape=[jax.ShapeDtypeStruct((t_all, w), BF16)] * 3,
        scratch_shapes=[pltpu.VMEM((2, nh, tq, tk), F32), pltpu.VMEM((2, nh, tq, tk), F32),
                        pltpu.VMEM((nh, tq, tk), F32), pltpu.VMEM((nh, tq, LOG_SUM_PASSES * tk), BF16),
                        pltpu.VMEM((nh, tq, tk), F32), pltpu.VMEM((nh, tq, tk), F32),
                        pltpu.VMEM((nh, tq, tk), BF16), pltpu.VMEM((nh, tq, tk), F32),
                        pltpu.VMEM((2, nh * tq, tk), BF16), pltpu.VMEM((2, nh * tq, tk), BF16),
                        pltpu.VMEM((nh, tq, LANES), F32), pltpu.VMEM((nh, tq, LANES), F32),
                        pltpu.VMEM((nh, tq, LANES), F32), pltpu.VMEM((nh, tq, LANES), F32),
                        pltpu.VMEM((LANES, tq), BF16), pltpu.VMEM((LANES, tq), BF16),
                        pltpu.VMEM((LANES, tq), F32), pltpu.VMEM((nkb, LANES, tk), F32),
                        pltpu.VMEM((nkb, LANES, tk), F32)],
        compiler_params=_params(("arbitrary", "arbitrary", "arbitrary")),
    )(qn, k, kt, v, do, ltot, _tri_matrix(tk, "upto")[:LOG_SUM_PASSES * tk] + order.astype(BF16),
      _tri_matrix(tk, "before")[:tk])


def _inproj_bwd(dq, dk, dv, dpd, x, dx1, mod, g_pre, w_in, seq, tm):
    t_all, d = x.shape
    nt = seq // tm
    p = dq.shape[1]

    def body(dq_ref, dk_ref, dv_ref, dpd_ref, halo_ref, x_ref, dx1_ref, mod_ref, g_ref, w_ref,
             gx_ref, du_ref, accb_ref, accg_ref):
        i = pl.program_id(0)
        it = i % nt

        @pl.when(i == 0)
        def _():
            accg_ref[...] = jnp.zeros_like(accg_ref)

        @pl.when(it == 0)
        def _():
            accb_ref[...] = jnp.zeros_like(accb_ref)

        dpd = dpd_ref[...]
        pos = it * tm + lax.broadcasted_iota(jnp.int32, (tm, 1), 0)
        cnts = _window_counts(pos)
        halo = jnp.where(it == nt - 1, 0.0, halo_ref[...])
        scaled = []
        halos = []
        for g, win in enumerate(POOL_WINDOWS):
            cols = slice(g * POOL_GROUP, (g + 1) * POOL_GROUP)
            scaled.append(dpd[:, cols] / cnts[g])
            halos.append(halo[:, cols] / float(win))
        ext = jnp.concatenate([jnp.concatenate(scaled, axis=1), jnp.concatenate(halos, axis=1)], axis=0)
        sums = _window_sums(ext, tm, 0, True)
        du = (jnp.concatenate(sums, axis=1) - dpd).astype(BF16)
        du_ref[...] = du
        g1 = g_ref[...]
        scale1 = 1.0 + mod_ref[0, 1:2, :]
        for c in range(ROW_CHUNKS):
            rows = slice(c * (tm // ROW_CHUNKS), (c + 1) * (tm // ROW_CHUNKS))
            dh1 = (_dot_nt(dq_ref[rows, :], w_ref[0]) + _dot_nt(dk_ref[rows, :], w_ref[1])
                   + _dot_nt(dv_ref[rows, :], w_ref[2]) + _dot_nt(du_ref[rows, :], w_ref[3]))
            xf = x_ref[rows, :]
            r1 = _rms(xf)
            n1 = xf * r1
            accb_ref[0, 0:1, :] += _colsum(dh1)
            accb_ref[0, 1:2, :] += _colsum(dh1 * (n1 * g1))
            accg_ref[0:1, :] += _colsum((dh1 * scale1) * n1)
            gx_ref[rows, :] = dx1_ref[rows, :] + _norm_bwd((dh1 * scale1) * g1, n1, r1)

    tok = lambda i: (i, 0)
    const2 = lambda i: (0, 0)
    hb = tm // HALO
    last = t_all // HALO - 1
    return pl.pallas_call(
        body, name="inproj_bwd", grid=(t_all // tm,),
        in_specs=[pl.BlockSpec((tm, p), tok), pl.BlockSpec((tm, p), tok), pl.BlockSpec((tm, p), tok),
                  pl.BlockSpec((tm, p), tok),
                  pl.BlockSpec((HALO, p), lambda i: (jnp.minimum((i + 1) * hb, last), 0)),
                  pl.BlockSpec((tm, d), tok), pl.BlockSpec((tm, d), tok),
                  pl.BlockSpec((1, MOD_ROWS, d), lambda i: (i // nt, 0, 0)),
                  pl.BlockSpec((1, d), const2),
                  pl.BlockSpec((N_CHIPS, d, p), lambda i: (0, 0, 0))],
        out_specs=[pl.BlockSpec((tm, d), tok), pl.BlockSpec((tm, p), tok),
                   pl.BlockSpec((1, 8, d), lambda i: (i // nt, 0, 0)),
                   pl.BlockSpec((8, d), const2)],
        out_shape=[jax.ShapeDtypeStruct((t_all, d), F32), jax.ShapeDtypeStruct((t_all, p), BF16),
                   jax.ShapeDtypeStruct((t_all // seq, 8, d), F32),
                   jax.ShapeDtypeStruct((8, d), F32)],
        compiler_params=_params(("arbitrary",)),
    )(dq, dk, dv, dpd, dpd, x, dx1, mod, g_pre, w_in)


def _tn_matmul(x, ys, nk, bt, name, after=()):
    t_all = x.shape[-2]
    m = x.shape[-1]
    ny = len(ys)
    nt = t_all // bt

    def spec(arr):
        if arr.ndim == 3:
            return pl.BlockSpec((1, bt, arr.shape[-1]), lambda k, t: (k, t, 0))
        return pl.BlockSpec((bt, arr.shape[-1]), lambda k, t: (t, 0))

    def tile(ref):
        return ref[0] if len(ref.shape) == 3 else ref[...]

    def body(*refs):
        outs = refs[1 + ny + len(after):]
        x_ref, y_refs, o_refs, h_refs = refs[0], refs[1:1 + ny], outs[:ny], outs[ny:]
        t = pl.program_id(1)
        xt = tile(x_ref)
        for y_ref, o_ref, h_ref in zip(y_refs, o_refs, h_refs):
            part = _dot_tn(xt, tile(y_ref))

            @pl.when(t == 0)
            def _(o_ref=o_ref, part=part):
                o_ref[0] = part

            @pl.when(t > 0)
            def _(o_ref=o_ref, part=part):
                o_ref[0] += part

            @pl.when(t == nt - 1)
            def _(o_ref=o_ref, h_ref=h_ref):
                h_ref[0] = o_ref[0].astype(BF16)

    out_specs = [pl.BlockSpec((1, m, y.shape[-1]), lambda k, t: (k, 0, 0)) for y in ys]
    out = pl.pallas_call(
        body, name=name, grid=(nk, nt),
        in_specs=[spec(x)] + [spec(y) for y in ys] + [_ANY] * len(after),
        out_specs=out_specs * 2,
        out_shape=[jax.ShapeDtypeStruct((nk, m, y.shape[-1]), dt) for dt in (F32, BF16) for y in ys],
        compiler_params=_params(("arbitrary", "arbitrary")),
    )(x, *ys, *after)
    return out[:ny], out[ny:]


def _tn_matmul_stacked(x, ys, bt, name, after=()):
    t_all, m = x.shape
    n = ys[0].shape[1]
    ny = len(ys)
    nt = t_all // bt

    def body(*refs):
        x_ref, y_refs, (o_ref, h_ref) = refs[0], refs[1:1 + ny], refs[1 + ny + len(after):]
        t = pl.program_id(0)
        xt = x_ref[...]

        @pl.when(t == 0)
        def _():
            o_ref[...] = jnp.zeros_like(o_ref)

        for j, y_ref in enumerate(y_refs):
            o_ref[j] += _dot_tn(xt, y_ref[...])

        @pl.when(t == nt - 1)
        def _():
            h_ref[...] = o_ref[...].astype(BF16)

    whole = pl.BlockSpec((ny, m, n), lambda t: (0, 0, 0))
    return pl.pallas_call(
        body, name=name, grid=(nt,),
        in_specs=[pl.BlockSpec((bt, m), lambda t: (t, 0))] + [pl.BlockSpec((bt, n), lambda t: (t, 0))] * ny
        + [_ANY] * len(after),
        out_specs=[whole, whole],
        out_shape=[jax.ShapeDtypeStruct((ny, m, n), F32), jax.ShapeDtypeStruct((ny, m, n), BF16)],
        compiler_params=_params(("arbitrary",)),
    )(x, *ys, *after)


def _cond_fwd(c_all, w_q, b_q, bn):
    nrow, d = c_all.shape
    ncol = w_q.shape[1]

    def body(c_ref, w_ref, b_ref, sc_ref, mod_ref):
        cf = c_ref[...]
        sc = cf * _sigmoid(cf)
        sc_ref[...] = sc
        shi, slo = _split(sc)
        whi, wlo = _split(w_ref[...])
        mod_ref[...] = (_dot(shi, whi) + _dot(shi, wlo) + _dot(slo, whi)) + b_ref[...]

    return pl.pallas_call(
        body, name="cond_fwd", grid=(ncol // bn,),
        in_specs=[pl.BlockSpec((nrow, d), lambda n: (0, 0)), pl.BlockSpec((d, bn), lambda n: (0, n)),
                  pl.BlockSpec((1, bn), lambda n: (0, n))],
        out_specs=[pl.BlockSpec((nrow, d), lambda n: (0, 0)), pl.BlockSpec((nrow, bn), lambda n: (0, n))],
        out_shape=[jax.ShapeDtypeStruct((nrow, d), F32), jax.ShapeDtypeStruct((nrow, ncol), F32)],
        compiler_params=_params(("arbitrary",)),
    )(c_all, w_q, b_q)


def _cond_bwd(sc_all, dmod_q, bn):
    nrow, d = sc_all.shape
    ncol = dmod_q.shape[1]

    def body(sc_ref, dm_ref, gw_ref):
        shi, slo = _split(sc_ref[...])
        dhi, dlo = _split(dm_ref[...])
        gw_ref[...] = _dot_tn(shi, dhi) + _dot_tn(shi, dlo) + _dot_tn(slo, dhi)

    return pl.pallas_call(
        body, name="cond_bwd", grid=(ncol // bn,),
        in_specs=[pl.BlockSpec((nrow, d), lambda n: (0, 0)), pl.BlockSpec((nrow, bn), lambda n: (0, n))],
        out_specs=pl.BlockSpec((d, bn), lambda n: (0, n)),
        out_shape=jax.ShapeDtypeStruct((d, ncol), F32),
        compiler_params=_params(("arbitrary",)),
    )(sc_all, dmod_q)


def _row_block(rows, cols, budget=1 << 18):
    best = None
    for br in range(8, rows + 1, 8):
        if rows % br == 0 and br * cols <= budget:
            best = br
    return best if best is not None else rows


def _adam_math(w, g, m, v):
    c1 = 1.0 - ADAM_B1 ** ADAM_STEP
    c2 = 1.0 - ADAM_B2 ** ADAM_STEP
    m2 = ADAM_B1 * m + (1.0 - ADAM_B1) * g
    v2 = ADAM_B2 * v + (1.0 - ADAM_B2) * (g * g)
    return -ADAM_LR * ((m2 / c1) / (jnp.sqrt(v2 / c2) + ADAM_EPS) + ADAM_WD * w), m2, v2


def _small_updates(summed, params):
    n = len(params)

    def body(s_ref, *refs):
        ins, outs = refs[:3 * n], refs[3 * n:]
        for p, (_, _, _, pick) in enumerate(params):
            w_ref, m_ref, v_ref = ins[3 * p:3 * p + 3]
            g = pick(s_ref)
            delta, m2, v2 = _adam_math(w_ref[...], g, m_ref[...], v_ref[...])
            for o_ref, val in zip(outs[4 * p:4 * p + 4], (g, delta, m2, v2)):
                o_ref[...] = val

    out = pl.pallas_call(
        body, name="adamw_small",
        out_shape=[jax.ShapeDtypeStruct(w.shape, F32) for w, _, _, _ in params for _ in range(4)],
        compiler_params=pltpu.CompilerParams(vmem_limit_bytes=VMEM_LIMIT),
    )(summed, *[t for w, m, v, _ in params for t in (w, m, v)])
    return [tuple(out[4 * p:4 * p + 4]) for p in range(n)]


def _adamw(w, g, m, v, name, after=()):
    rows, cols = w.shape
    br = _row_block(rows, cols)

    def body(*refs):
        w_ref, g_ref, m_ref, v_ref = refs[:4]
        d_ref, nm_ref, nv_ref = refs[4 + len(after):]
        d_ref[...], nm_ref[...], nv_ref[...] = _adam_math(w_ref[...], g_ref[...], m_ref[...], v_ref[...])

    blk = pl.BlockSpec((br, cols), lambda i: (i, 0))
    return pl.pallas_call(
        body, name=name, grid=(rows // br,),
        in_specs=[blk] * 4 + [_ANY] * len(after), out_specs=[blk] * 3,
        out_shape=[jax.ShapeDtypeStruct((rows, cols), F32)] * 3,
        compiler_params=_params(("arbitrary",)),
    )(w, g, m, v, *after)


def _all_gather(x_shard, name):
    m_per, n = x_shard.shape

    def body(x_ref, out_ref, send_sems, recv_sems, local_sem):
        x, y, c = _position()
        me, sibling = (x, y, c), (x, y, 1 - c)
        chips = [(1 - x, y), (x, 1 - y), (1 - x, 1 - y)]

        def rows(px, py, pc):
            return out_ref.at[pl.ds((4 * px + 2 * py + pc) * m_per, m_per), :]

        def copy(k, block, to, src=None):
            return pltpu.make_async_remote_copy(
                src_ref=rows(*block) if src is None else src, dst_ref=rows(*block),
                send_sem=send_sems.at[k], recv_sem=recv_sems.at[k], device_id=to, device_id_type=MESH)

        mine = pltpu.make_async_copy(x_ref, rows(*me), local_sem)
        mine.start()
        first = [copy(0, me, sibling, src=x_ref)]
        first += [copy(1 + j, me, (*chip, c), src=x_ref) for j, chip in enumerate(chips)]
        for cp in first:
            cp.start()
        passed = [copy(4 + j, (*chip, c), sibling) for j, chip in enumerate(chips)]
        for j, chip in enumerate(chips):
            copy(1 + j, (*chip, c), me).wait_recv()
            passed[j].start()
        copy(0, sibling, me).wait_recv()
        for j, chip in enumerate(chips):
            copy(4 + j, (*chip, 1 - c), me).wait_recv()
        for cp in first + passed:
            cp.wait_send()
        mine.wait()

    return pl.pallas_call(
        body, name=name,
        out_shape=jax.ShapeDtypeStruct((N_DEV * m_per, n), x_shard.dtype),
        in_specs=[pl.BlockSpec(memory_space=pltpu.VMEM)],
        out_specs=pl.BlockSpec(memory_space=pltpu.VMEM),
        scratch_shapes=[pltpu.SemaphoreType.DMA((7,)), pltpu.SemaphoreType.DMA((7,)), pltpu.SemaphoreType.DMA],
        compiler_params=pltpu.CompilerParams(vmem_limit_bytes=VMEM_LIMIT),
    )(x_shard)


_ANY = pl.BlockSpec(memory_space=pl.ANY)


def _place_quarters(place, quarters):
    steps = 2

    def body(place_ref, *refs):
        n = len(refs) // 2
        for w_ref, o_ref in zip(refs[:n], refs[n:]):
            o_ref[0] = w_ref[...].astype(BF16)

    return pl.pallas_call(
        body, name="place_quarters",
        grid_spec=pltpu.PrefetchScalarGridSpec(
            num_scalar_prefetch=1, grid=(steps,),
            in_specs=[pl.BlockSpec((q.shape[0] // steps, q.shape[1]), lambda r, place_ref: (r, 0)) for q in quarters],
            out_specs=[pl.BlockSpec((1, q.shape[0] // steps, q.shape[1]), lambda r, place_ref: (place_ref[0], r, 0))
                       for q in quarters]),
        out_shape=[jax.ShapeDtypeStruct((N_CHIPS,) + q.shape, BF16) for q in quarters],
        compiler_params=_params(("arbitrary",)),
    )(place, *quarters)


_HBM = pl.BlockSpec(memory_space=pltpu.HBM)
_SEM = pl.BlockSpec(memory_space=pltpu.SEMAPHORE)
_EFFECT = pltpu.SideEffectType.DATAFLOW_SIDE_EFFECTING


def _quarter_halves(shapes, a, which):
    hr = shapes[a][0] // 2
    return pl.ds(which * hr, hr)


def _gather_start(placed, after, tag):
    n = len(placed)
    m = len(after)
    shapes = [b.shape[1:] for b in placed]

    def body(*refs):
        g_refs = refs[:n]
        send_sems, recv_sems = refs[n + m], refs[n + m + 1]
        token = refs[2 * n + m + 2]
        x, y, c = _position()
        chips = [(1 - x, y), (x, 1 - y), (1 - x, 1 - y)]
        mine = 2 * x + y
        for a in range(n):
            ref = g_refs[a].at[mine, _quarter_halves(shapes, a, c), :]
            for p in range(3):
                pltpu.make_async_remote_copy(
                    src_ref=ref, dst_ref=ref, send_sem=send_sems.at[3 * a + p], recv_sem=recv_sems.at[3 * a + p],
                    device_id=(*chips[p], c), device_id_type=MESH).start()
        token[...] = jnp.zeros_like(token)

    out = pl.pallas_call(
        body, name="gather_start_" + tag,
        out_shape=(pltpu.SemaphoreType.DMA((3 * n,)), pltpu.SemaphoreType.DMA((3 * n,)),
                   *[pltpu.HBM(b.shape, b.dtype) for b in placed], jax.ShapeDtypeStruct((8, LANES), F32)),
        in_specs=[_HBM] * n + [_ANY] * m,
        out_specs=(_SEM, _SEM, *[_HBM] * n, pl.BlockSpec(memory_space=pltpu.VMEM)),
        input_output_aliases={a: 2 + a for a in range(n)},
        compiler_params=pltpu.CompilerParams(has_side_effects=_EFFECT),
    )(*[pltpu.with_memory_space_constraint(b, pltpu.HBM) for b in placed], *after)
    return out[0], out[1], list(out[2:2 + n]), out[2 + n]


def _gather_wait(send_sems, recv_sems, thru, after, tag):
    n = len(thru)
    shapes = [b.shape[1:] for b in thru]

    def body(*refs):
        g_refs = refs[:n]
        send_sems, recv_sems = refs[n], refs[n + 1]
        x, y, c = _position()
        chips = [(1 - x, y), (x, 1 - y), (1 - x, 1 - y)]
        mine = 2 * x + y
        for a in range(n):
            rows = _quarter_halves(shapes, a, c)
            for p, (cx, cy) in enumerate(chips):
                copy = pltpu.make_async_remote_copy(
                    src_ref=g_refs[a].at[mine, rows, :], dst_ref=g_refs[a].at[2 * cx + cy, rows, :],
                    send_sem=send_sems.at[3 * a + p], recv_sem=recv_sems.at[3 * a + p],
                    device_id=(cx, cy, c), device_id_type=MESH)
                copy.wait_send()
                copy.wait_recv()

    return pl.pallas_call(
        body, name="gather_wait_" + tag,
        out_shape=[pltpu.HBM(b.shape, b.dtype) for b in thru],
        in_specs=[_HBM] * n + [_SEM, _SEM, _ANY], out_specs=[_HBM] * n,
        input_output_aliases={a: a for a in range(n)},
        compiler_params=pltpu.CompilerParams(has_side_effects=_EFFECT),
    )(*thru, send_sems, recv_sems, after)


def _gather_forward(bufs, tag):
    n = len(bufs)
    shapes = [b.shape[1:] for b in bufs]

    def body(*refs):
        g_refs = refs[n:2 * n]
        send_sems, recv_sems = refs[2 * n:]
        x, y, c = _position()
        chips = [(1 - x, y), (x, 1 - y), (1 - x, 1 - y)]

        def over_d2d(a, p, which):
            cx, cy = chips[p]
            ref = g_refs[a].at[2 * cx + cy, _quarter_halves(shapes, a, which), :]
            return pltpu.make_async_remote_copy(
                src_ref=ref, dst_ref=ref, send_sem=send_sems.at[3 * a + p], recv_sem=recv_sems.at[3 * a + p],
                device_id=(x, y, 1 - c), device_id_type=MESH)

        sends = [over_d2d(a, p, c) for a in range(n) for p in range(3)]
        for cp in sends:
            cp.start()
        for a in range(n):
            for p in range(3):
                over_d2d(a, p, 1 - c).wait_recv()
        for cp in sends:
            cp.wait_send()

    return pl.pallas_call(
        body, name="gather_forward_" + tag,
        out_shape=[jax.ShapeDtypeStruct(b.shape, BF16) for b in bufs],
        in_specs=[_ANY] * n, out_specs=[_ANY] * n,
        input_output_aliases={a: a for a in range(n)},
        scratch_shapes=[pltpu.SemaphoreType.DMA((3 * n,)), pltpu.SemaphoreType.DMA((3 * n,))],
    )(*bufs)


_FLIPS = [(fx, fy, fc) for fx in (0, 1) for fy in (0, 1) for fc in (0, 1)][1:]


def _flipped(pos, flip):
    return tuple(1 - p if f else p for p, f in zip(pos, flip))


def _direct_gather_start(slots):
    def body(s_ref, send_sems, recv_sems, thru, token):
        me = _position()
        mine = s_ref.at[4 * me[0] + 2 * me[1] + me[2]]
        for r, flip in enumerate(_FLIPS):
            pltpu.make_async_remote_copy(
                src_ref=mine, dst_ref=mine, send_sem=send_sems.at[r], recv_sem=recv_sems.at[r],
                device_id=_flipped(me, flip), device_id_type=MESH).start()
        token[...] = jnp.zeros_like(token)

    return pl.pallas_call(
        body, name="small_gather_start",
        out_shape=(pltpu.SemaphoreType.DMA((len(_FLIPS),)), pltpu.SemaphoreType.DMA((len(_FLIPS),)),
                   pltpu.HBM(slots.shape, slots.dtype), jax.ShapeDtypeStruct((8, LANES), F32)),
        in_specs=[_HBM], out_specs=(_SEM, _SEM, _HBM, pl.BlockSpec(memory_space=pltpu.VMEM)),
        input_output_aliases={0: 2},
        compiler_params=pltpu.CompilerParams(has_side_effects=_EFFECT),
    )(pltpu.with_memory_space_constraint(slots, pltpu.HBM))


def _direct_gather_wait(send_sems, recv_sems, slots, after):
    def body(s_ref, send_sems, recv_sems, after_ref, out_ref):
        me = _position()
        mine = s_ref.at[4 * me[0] + 2 * me[1] + me[2]]
        for r, flip in enumerate(_FLIPS):
            peer = _flipped(me, flip)
            copy = pltpu.make_async_remote_copy(
                src_ref=mine, dst_ref=s_ref.at[4 * peer[0] + 2 * peer[1] + peer[2]],
                send_sem=send_sems.at[r], recv_sem=recv_sems.at[r], device_id=peer, device_id_type=MESH)
            copy.wait_send()
            copy.wait_recv()

    return pl.pallas_call(
        body, name="small_gather_wait",
        out_shape=pltpu.HBM(slots.shape, slots.dtype),
        in_specs=[_HBM, _SEM, _SEM, _ANY], out_specs=_HBM,
        input_output_aliases={0: 0},
        compiler_params=pltpu.CompilerParams(has_side_effects=_EFFECT),
    )(slots, send_sems, recv_sems, after)


def _sibling_split_start(bufs, parts, nparts, after, tag):
    n, m = len(bufs), len(after)

    def body(*refs):
        b_refs = refs[:n]
        send_sems, recv_sems = refs[n + m], refs[n + m + 1]
        token = refs[2 * n + m + 2]
        x, y, c = _position()
        for r, ref in enumerate(parts(b_refs, x, y, c)):
            pltpu.make_async_remote_copy(
                src_ref=ref, dst_ref=ref, send_sem=send_sems.at[r], recv_sem=recv_sems.at[r],
                device_id=(x, y, 1 - c), device_id_type=MESH).start()
        token[...] = jnp.zeros_like(token)

    out = pl.pallas_call(
        body, name="sibling_start_" + tag,
        out_shape=(pltpu.SemaphoreType.DMA((nparts,)), pltpu.SemaphoreType.DMA((nparts,)),
                   *[pltpu.HBM(b.shape, b.dtype) for b in bufs], jax.ShapeDtypeStruct((8, LANES), F32)),
        in_specs=[_HBM] * n + [_ANY] * m,
        out_specs=(_SEM, _SEM, *[_HBM] * n, pl.BlockSpec(memory_space=pltpu.VMEM)),
        input_output_aliases={a: 2 + a for a in range(n)},
        compiler_params=pltpu.CompilerParams(has_side_effects=_EFFECT),
    )(*[pltpu.with_memory_space_constraint(b, pltpu.HBM) for b in bufs], *after)
    return out[0], out[1], list(out[2:2 + n]), out[2 + n]


def _sibling_split_wait(send_sems, recv_sems, bufs, parts, after, tag):
    n = len(bufs)

    def body(*refs):
        b_refs = refs[:n]
        send_sems, recv_sems = refs[n], refs[n + 1]
        x, y, c = _position()
        mine, theirs = parts(b_refs, x, y, c), parts(b_refs, x, y, 1 - c)
        for r, (src, dst) in enumerate(zip(mine, theirs)):
            copy = pltpu.make_async_remote_copy(
                src_ref=src, dst_ref=dst, send_sem=send_sems.at[r], recv_sem=recv_sems.at[r],
                device_id=(x, y, 1 - c), device_id_type=MESH)
            copy.wait_send()
            copy.wait_recv()

    return pl.pallas_call(
        body, name="sibling_wait_" + tag,
        out_shape=[pltpu.HBM(b.shape, b.dtype) for b in bufs],
        in_specs=[_HBM] * n + [_SEM, _SEM, _ANY], out_specs=[_HBM] * n,
        input_output_aliases={a: a for a in range(n)},
        compiler_params=pltpu.CompilerParams(has_side_effects=_EFFECT),
    )(*bufs, send_sems, recv_sems, after)


def _sibling_exchange(grads, tag):
    n = len(grads)
    shapes = [g.shape for g in grads]

    def body(*refs):
        g_refs, x_refs = refs[:n], refs[n:2 * n]
        send_sems, recv_sems = refs[2 * n:]
        x, y, c = _position()
        copies = []
        for a in range(n):
            hr = shapes[a][1] // 2
            cp = pltpu.make_async_remote_copy(
                src_ref=g_refs[a].at[:, pl.ds((1 - c) * hr, hr), :], dst_ref=x_refs[a],
                send_sem=send_sems.at[a], recv_sem=recv_sems.at[a],
                device_id=(x, y, 1 - c), device_id_type=MESH)
            cp.start()
            copies.append(cp)
        for cp in copies:
            cp.wait()

    return pl.pallas_call(
        body, name="grad_sibling_exchange_" + tag,
        out_shape=[jax.ShapeDtypeStruct((g.shape[0], g.shape[1] // 2, g.shape[2]), g.dtype) for g in grads],
        in_specs=[_ANY] * n, out_specs=[_ANY] * n,
        scratch_shapes=[pltpu.SemaphoreType.DMA((n,)), pltpu.SemaphoreType.DMA((n,))],
    )(*grads)


def _sibling_exchange_start(grads, tag):
    n = len(grads)
    lands = [lax.empty((g.shape[0], g.shape[1] // 2, g.shape[2]), g.dtype) for g in grads]

    def body(*refs):
        g_refs, x_refs = refs[:n], refs[n:2 * n]
        send_sems, recv_sems = refs[2 * n], refs[2 * n + 1]
        token = refs[4 * n + 2]
        x, y, c = _position()
        for a in range(n):
            hr = grads[a].shape[1] // 2
            pltpu.make_async_remote_copy(
                src_ref=g_refs[a].at[:, pl.ds((1 - c) * hr, hr), :], dst_ref=x_refs[a],
                send_sem=send_sems.at[a], recv_sem=recv_sems.at[a],
                device_id=(x, y, 1 - c), device_id_type=MESH).start()
        token[...] = jnp.zeros_like(token)

    both = list(grads) + lands
    out = pl.pallas_call(
        body, name="grad_sibling_exchange_start_" + tag,
        out_shape=(pltpu.SemaphoreType.DMA((n,)), pltpu.SemaphoreType.DMA((n,)),
                   *[pltpu.HBM(b.shape, b.dtype) for b in both], jax.ShapeDtypeStruct((8, LANES), F32)),
        in_specs=[_HBM] * (2 * n),
        out_specs=(_SEM, _SEM, *[_HBM] * (2 * n), pl.BlockSpec(memory_space=pltpu.VMEM)),
        input_output_aliases={a: 2 + a for a in range(2 * n)},
        compiler_params=pltpu.CompilerParams(has_side_effects=_EFFECT),
    )(*[pltpu.with_memory_space_constraint(b, pltpu.HBM) for b in both])
    return out[0], out[1], list(out[2:2 + n]), list(out[2 + n:2 + 2 * n]), out[2 + 2 * n]


def _sibling_exchange_wait(send_sems, recv_sems, grads, lands, after, tag):
    n = len(grads)

    def body(*refs):
        g_refs, x_refs = refs[:n], refs[n:2 * n]
        send_sems, recv_sems = refs[2 * n], refs[2 * n + 1]
        x, y, c = _position()
        for a in range(n):
            hr = grads[a].shape[1] // 2
            copy = pltpu.make_async_remote_copy(
                src_ref=g_refs[a].at[:, pl.ds((1 - c) * hr, hr), :], dst_ref=x_refs[a],
                send_sem=send_sems.at[a], recv_sem=recv_sems.at[a],
                device_id=(x, y, 1 - c), device_id_type=MESH)
            copy.wait_send()
            copy.wait_recv()

    both = list(grads) + list(lands)
    out = pl.pallas_call(
        body, name="grad_sibling_exchange_wait_" + tag,
        out_shape=[pltpu.HBM(b.shape, b.dtype) for b in both],
        in_specs=[_HBM] * (2 * n) + [_SEM, _SEM, _ANY], out_specs=[_HBM] * (2 * n),
        input_output_aliases={a: a for a in range(2 * n)},
        compiler_params=pltpu.CompilerParams(has_side_effects=_EFFECT),
    )(*both, send_sems, recv_sems, after)
    return list(out[n:])


def _chip_sums(core, grads, theirs, tag):
    n = len(grads)

    def body(core_ref, *refs):
        g_refs, t_refs, o_refs = refs[:n], refs[n:2 * n], refs[2 * n:]
        for g_ref, t_ref, o_ref in zip(g_refs, t_refs, o_refs):
            o_ref[...] = (g_ref[...] + t_ref[...].astype(F32)).astype(BF16)

    in_specs = [pl.BlockSpec((1, g.shape[1] // 2, g.shape[2]), lambda k, core_ref: (k, core_ref[0], 0)) for g in grads]
    in_specs += [pl.BlockSpec((1,) + t.shape[1:], lambda k, core_ref: (k, 0, 0)) for t in theirs]
    return pl.pallas_call(
        body, name="grad_chip_sums_" + tag,
        grid_spec=pltpu.PrefetchScalarGridSpec(
            num_scalar_prefetch=1, grid=(N_CHIPS,), in_specs=in_specs,
            out_specs=[pl.BlockSpec((1,) + t.shape[1:], lambda k, core_ref: (k, 0, 0)) for t in theirs]),
        out_shape=[jax.ShapeDtypeStruct(t.shape, BF16) for t in theirs],
        compiler_params=_params(("arbitrary",)),
    )(core, *grads, *theirs)


def _chip_exchange_start(sums, after, tag):
    n = len(sums)
    m = len(after)
    lands = [lax.empty((3,) + s.shape[1:], BF16) for s in sums]

    def body(*refs):
        s_refs, y_refs = refs[:n], refs[n:2 * n]
        send_sems, recv_sems = refs[2 * n + m], refs[2 * n + m + 1]
        token = refs[4 * n + m + 2]
        x, y, c = _position()
        chips = [(1 - x, y), (x, 1 - y), (1 - x, 1 - y)]
        for a in range(n):
            for p, (cx, cy) in enumerate(chips):
                pltpu.make_async_remote_copy(
                    src_ref=s_refs[a].at[2 * cx + cy], dst_ref=y_refs[a].at[p],
                    send_sem=send_sems.at[3 * a + p], recv_sem=recv_sems.at[3 * a + p],
                    device_id=(cx, cy, c), device_id_type=MESH).start()
        token[...] = jnp.zeros_like(token)

    both = list(sums) + lands
    out = pl.pallas_call(
        body, name="grad_chip_exchange_start_" + tag,
        out_shape=(pltpu.SemaphoreType.DMA((3 * n,)), pltpu.SemaphoreType.DMA((3 * n,)),
                   *[pltpu.HBM(b.shape, b.dtype) for b in both], jax.ShapeDtypeStruct((8, LANES), F32)),
        in_specs=[_HBM] * (2 * n) + [_ANY] * m,
        out_specs=(_SEM, _SEM, *[_HBM] * (2 * n), pl.BlockSpec(memory_space=pltpu.VMEM)),
        input_output_aliases={a: 2 + a for a in range(2 * n)},
        compiler_params=pltpu.CompilerParams(has_side_effects=_EFFECT),
    )(*[pltpu.with_memory_space_constraint(b, pltpu.HBM) for b in both], *after)
    return out[0], out[1], list(out[2:2 + n]), list(out[2 + n:2 + 2 * n]), out[2 + 2 * n]


def _chip_exchange_wait(send_sems, recv_sems, sums, lands, after, tag):
    n = len(sums)

    def body(*refs):
        s_refs, y_refs = refs[:n], refs[n:2 * n]
        send_sems, recv_sems = refs[2 * n], refs[2 * n + 1]
        x, y, c = _position()
        chips = [(1 - x, y), (x, 1 - y), (1 - x, 1 - y)]
        for a in range(n):
            for p, (cx, cy) in enumerate(chips):
                copy = pltpu.make_async_remote_copy(
                    src_ref=s_refs[a].at[2 * cx + cy], dst_ref=y_refs[a].at[p],
                    send_sem=send_sems.at[3 * a + p], recv_sem=recv_sems.at[3 * a + p],
                    device_id=(cx, cy, c), device_id_type=MESH)
                copy.wait_send()
                copy.wait_recv()

    both = list(sums) + list(lands)
    out = pl.pallas_call(
        body, name="grad_chip_exchange_wait_" + tag,
        out_shape=[pltpu.HBM(b.shape, b.dtype) for b in both],
        in_specs=[_HBM] * (2 * n) + [_SEM, _SEM, _ANY], out_specs=[_HBM] * (2 * n),
        input_output_aliases={a: a for a in range(2 * n)},
        compiler_params=pltpu.CompilerParams(has_side_effects=_EFFECT),
    )(*both, send_sems, recv_sems, after)
    return list(out[:n]), list(out[n:])


def _total_sums(place, sums, parts, after, tag):
    n = len(parts)
    m = len(after)
    steps = 2

    def body(place_ref, *refs):
        for s_ref, y_ref, o_ref in zip(refs[:n], refs[n:2 * n], refs[2 * n + m:]):
            o_ref[0] = ((s_ref[0].astype(F32) + y_ref[0].astype(F32)) + y_ref[1].astype(F32)) + y_ref[2].astype(F32)

    def step_rows(pt):
        return pt.shape[1] // steps

    in_specs = [pl.BlockSpec((1, step_rows(s), s.shape[2]), lambda r, place_ref: (place_ref[0], r, 0)) for s in sums]
    in_specs += [pl.BlockSpec((3, step_rows(pt), pt.shape[2]), lambda r, place_ref: (0, r, 0)) for pt in parts]
    in_specs += [_ANY] * m
    return pl.pallas_call(
        body, name="grad_total_sums_" + tag,
        grid_spec=pltpu.PrefetchScalarGridSpec(
            num_scalar_prefetch=1, grid=(steps,), in_specs=in_specs,
            out_specs=[pl.BlockSpec((1, step_rows(pt), pt.shape[2]), lambda r, place_ref: (place_ref[1], r, 0))
                       for pt in parts]),
        out_shape=[jax.ShapeDtypeStruct((2,) + pt.shape[1:], F32) for pt in parts],
        compiler_params=_params(("arbitrary",)),
    )(place, *sums, *parts, *after)


def _sibling_share(halves, tag):
    n = len(halves)

    def body(*refs):
        f_refs = refs[n:2 * n]
        send_sems, recv_sems = refs[2 * n:]
        x, y, c = _position()
        copies = []
        for a in range(n):
            cp = pltpu.make_async_remote_copy(
                src_ref=f_refs[a].at[c], dst_ref=f_refs[a].at[c], send_sem=send_sems.at[a], recv_sem=recv_sems.at[a],
                device_id=(x, y, 1 - c), device_id_type=MESH)
            cp.start()
            copies.append(cp)
        for a, cp in enumerate(copies):
            cp.wait_send()
            pltpu.make_async_remote_copy(
                src_ref=f_refs[a].at[1 - c], dst_ref=f_refs[a].at[1 - c], send_sem=send_sems.at[a],
                recv_sem=recv_sems.at[a], device_id=(x, y, c), device_id_type=MESH).wait_recv()

    return pl.pallas_call(
        body, name="grad_sibling_share_" + tag,
        out_shape=[jax.ShapeDtypeStruct(h.shape, F32) for h in halves],
        in_specs=[_ANY] * n, out_specs=[_ANY] * n,
        input_output_aliases={a: a for a in range(n)},
        scratch_shapes=[pltpu.SemaphoreType.DMA((n,)), pltpu.SemaphoreType.DMA((n,))],
    )(*halves)


def _group_sum(stacked, nrow, name):
    total, n = stacked.shape
    groups = total // nrow

    def body(g_ref, o_ref):
        acc = g_ref[0:nrow, :]
        for grp in range(1, groups):
            acc = acc + g_ref[grp * nrow:(grp + 1) * nrow, :]
        o_ref[...] = acc

    return pl.pallas_call(
        body, name=name,
        out_shape=jax.ShapeDtypeStruct((nrow, n), F32),
        compiler_params=pltpu.CompilerParams(vmem_limit_bytes=VMEM_LIMIT),
    )(stacked)


def _local_step(xt, tgt, mod, gains, w_pool, pool_scale, w_in, later_weights, on_ffn_grads, after_mixer_bwd,
                on_small_grads, seq):
    g_mpre, g_mpost, g_fpre, g_fpost = gains
    d = xt.shape[1]
    tm, tq = min(TOKEN_TILE, seq), min(ATTN_TILE, seq)

    h1, qn, k, v, u, kt, vt = _prenorm_proj(xt, mod, g_mpre, w_in, seq, tm)
    tk = min(ATTN_KEY_TILE, tq // 2)
    o, ltot = _attn_fwd(qn, k, vt, seq, tq, tk)
    w_out, order, ffn_weights = later_weights(o)
    w_out2 = w_out.reshape(d, d)
    pooled, mixin, mix, x1, h2 =_mixer_post(u, o, xt, mod, g_mpost, g_fpre + order, w_pool, pool_scale, w_out2, seq, tm)
    w_g, w_u, w_d = ffn_weights(h2)
    a, b, fin, dy, df, loss_blk, accb4, accg4 = _ffn_fwd(h2, w_g, w_u, w_d, x1, tgt, mod, g_fpost, seq, tm)
    da, db, dx1, dmix, accb5, accg5 = _ffn_bwd(df, a, b, w_d, w_g, w_u, x1, dy, mix, mod, g_fpre, g_mpost, seq, tm)
    bt = min(GRAD_TOKEN_TILE, xt.shape[0])
    bt_one = min(2 * GRAD_TOKEN_TILE, xt.shape[0])
    (g_g,), (g_g16,) = _tn_matmul(da, [h2], w_g.shape[0], bt_one, "grad_w_gate")
    (g_u,), (g_u16,) = _tn_matmul(db, [h2], w_u.shape[0], bt_one, "grad_w_up")
    (g_d,), (g_d16,) = _tn_matmul(fin, [df], w_d.shape[0], bt_one, "grad_w_down")
    token = on_ffn_grads([g_g, g_u, g_d], [g_g16, g_u16, g_d16])
    do, dpd, dps, dwp = _mixer_bwd(dmix, w_out2, pooled, w_pool, pool_scale + token, seq, tm)
    order = after_mixer_bwd(do)
    dq, dk, dv = _attn_bwd(qn, k, kt, v, do, ltot, seq, tq, tk, order)
    gx, du, accb8, accg8 = _inproj_bwd(dq, dk, dv, dpd, xt, dx1, mod, g_mpre, w_in, seq, tm)

    dmod = jnp.stack([accb8[:, 0], accb8[:, 1], accb5[:, 2], accb5[:, 0], accb5[:, 1], accb4[:, 0]], axis=1)
    dgain = jnp.stack([accg8[0], accg5[1], accg5[0], accg4[0]], axis=0)
    behind = on_small_grads(loss_blk, dmod, dgain, dps[0:1], dwp)
    g_in, g_in16 = _tn_matmul_stacked(h1, [dq, dk, dv, du], bt, "grad_w_in", behind)
    g_out, g_out16 = [parts[0].reshape(w_out.shape)
                      for parts in _tn_matmul(mixin, [dmix], 1, bt_one, "grad_w_out", behind)]
    grads = [g_in, g_out, g_g, g_u, g_d]
    grads16 = [g_in16, g_out16, g_g16, g_u16, g_d16]
    return gx, grads, grads16


def kernel(x, c, w_cond, b_cond, g_mix_pre, g_mix_post, w_in, w_pool, pool_scale, w_out, g_ffn_pre, g_ffn_post, w_gate, w_up, w_down, loss_target, m_w_cond, m_b_cond, m_g_mix_pre, m_g_mix_post, m_w_in, m_w_pool, m_pool_scale, m_w_out, m_g_ffn_pre, m_g_ffn_post, m_w_gate, m_w_up, m_w_down, v_w_cond, v_b_cond, v_g_mix_pre, v_g_mix_post, v_w_in, v_w_pool, v_pool_scale, v_w_out, v_g_ffn_pre, v_g_ffn_post, v_w_gate, v_w_up, v_w_down):
    xi, yi, ci = _position()
    chip = 2 * xi + yi
    dev = 4 * xi + 2 * yi + ci
    nb, seq, d = x.shape
    t_all = nb * seq
    xt = x.reshape(t_all, d)
    tgt = loss_target.reshape(t_all, d)
    ncol = w_cond.shape[2]
    pw = pool_scale.shape[1]

    place = jnp.stack([chip, ci]).astype(jnp.int32)
    turned = lambda t: jnp.swapaxes(t[0], 0, 1)
    placed = _place_quarters(place, [w_in[0], w_out[0], turned(w_gate), turned(w_up), w_down[0]])
    in_sems = _gather_start(placed[:1], [], "in")

    c_pad = jnp.concatenate([c, jnp.zeros((8 - nb, d), F32)], axis=0) + in_sems[3][0:1, 0:1]
    c_all = _all_gather(c_pad, "gather_c").reshape(N_DEV, 8, d)[:, :nb].reshape(N_DEV * nb, d)
    b_q = lax.dynamic_slice(b_cond, (0, chip * ncol), (1, ncol))
    sc_all, mod_q = _cond_fwd(c_all, w_cond[0], b_q, 512)
    mod_parts = _all_gather(mod_q, "gather_mod").reshape(N_DEV, N_DEV * nb, ncol)
    mod_rows = lax.dynamic_slice(mod_parts, (0, dev * nb, 0), (N_DEV, nb, ncol))[0::2]
    mod = jnp.transpose(mod_rows, (1, 0, 2)).reshape(nb, N_MOD, d)
    mod = jnp.concatenate([mod, jnp.zeros((nb, MOD_ROWS - N_MOD, d), F32)], axis=1)

    (w_in_all,) = _gather_forward(_gather_wait(*in_sems[:3], mod, "in"), "in")
    send_sems, recv_sems, in_flight, token = _gather_start(placed[1:], [mod, w_in_all], "rest")
    mod = mod + token[0:1, 0:1]

    def later_weights(after):
        waited = _gather_wait(send_sems, recv_sems, in_flight, after, "rest")
        (w_out_all,) = _gather_forward(waited[:1], "out")
        shapes = [b.shape[1:] for b in waited[1:]]

        def parts(refs, px, py, which):
            return [refs[a].at[2 * cx + cy, _quarter_halves(shapes, a, which), :]
                    for a in range(len(refs)) for cx, cy in [(1 - px, py), (px, 1 - py), (1 - px, 1 - py)]]

        forward = _sibling_split_start(waited[1:], parts, 3 * len(shapes), [w_out_all], "ffn_weights")
        finish = lambda after2: _sibling_split_wait(*forward[:3], parts, after2, "ffn_weights")
        return w_out_all, forward[3][0:1, 0:1], finish

    ffn_split = []

    ffn_sibling = []

    def on_ffn_grads(ffn_grads, ffn_grads16):
        ffn_sibling.extend(_sibling_exchange_start(ffn_grads16, "ffn"))
        ffn_sibling.append(ffn_grads)
        return ffn_sibling[4][0:1, 0:1]

    def after_mixer_bwd(do):
        theirs = _sibling_exchange_wait(*ffn_sibling[:4], do, "ffn")
        ffn_split.extend(_chip_exchange_start(_chip_sums(place[1:], ffn_sibling[5], theirs, "ffn"), [], "ffn"))
        return ffn_split[4][0:1, 0:1]

    wp_rows = w_pool[0].size // d
    loss_row = 2 * N_MOD + 4 + 1
    pad_rows = 24 - (loss_row + 1)
    prow = 24 + wp_rows
    small_split = []

    def on_small_grads(loss_blk, dmod, dgain, dps, dwp):
        payload = jnp.concatenate([
            dmod.reshape(nb * N_MOD, d), dgain,
            jnp.concatenate([dps, jnp.zeros((1, d - pw), F32)], axis=1),
            jnp.concatenate([loss_blk[0:1], jnp.zeros((1, d - LANES), F32)], axis=1),
            jnp.zeros((pad_rows, d), F32),
            jnp.concatenate(jnp.split(dwp.reshape(-1, dwp.shape[-1]), d // dwp.shape[-1], axis=0), axis=1)], axis=0)
        slots = lax.dynamic_update_slice(lax.empty((N_DEV, prow, d), F32), payload[None], (dev, 0, 0))
        small_split.extend(_direct_gather_start(slots))
        return [small_split[3]]

    gains = (g_mix_pre, g_mix_post, g_ffn_pre, g_ffn_post)
    gx, grads, grads16 = _local_step(
        xt, tgt, mod, gains, w_pool[0], pool_scale, w_in_all, later_weights, on_ffn_grads, after_mixer_bwd,
        on_small_grads, seq)

    sums_ffn, parts_ffn = _chip_exchange_wait(*ffn_split[:4], gx, "ffn")
    gathered = _direct_gather_wait(*small_split[:3], grads16[1]).reshape(N_DEV * prow, d)
    summed = _group_sum(gathered, prow, "small_device_sum")
    loss = summed[loss_row, 0]
    dmod_all = gathered.reshape(N_DEV, prow, d)[:, :nb * N_MOD].reshape(N_DEV * nb, N_MOD * d)
    dmod_q = lax.dynamic_slice(dmod_all, (0, chip * ncol), (N_DEV * nb, ncol))
    g_w_cond = _cond_bwd(sc_all, dmod_q, 512)
    first_gain = 2 * N_MOD

    theirs = _sibling_exchange(grads16[:2], "mix")
    mix_split = _chip_exchange_start(_chip_sums(place[1:], grads[:2], theirs, "mix"), [gathered], "mix")
    unfold = lambda halves: [g.reshape(2 * g.shape[1], g.shape[2]) for g in halves]
    share_parts = lambda refs, px, py, which: [r.at[which] for r in refs]
    halves_ffn = _total_sums(place, sums_ffn, parts_ffn, [mix_split[4]], "ffn")
    share = _sibling_split_start(halves_ffn, share_parts, len(halves_ffn), [], "share_ffn")

    results = {}

    def update(name, w2, g2, m2, v2, shape, after=()):
        delta, new_m, new_v = _adamw(w2, g2, m2, v2, "adamw_" + name, after)
        back = (lambda t: jnp.swapaxes(t, 0, 1)[None]) if shape is None else (lambda t: t.reshape(shape))
        results[name] = [back(t) for t in (g2, delta, new_m, new_v)]
        return delta

    done_cond = update("w_cond", w_cond[0], g_w_cond, m_w_cond[0], v_w_cond[0], w_cond.shape, [share[3]])
    g_ffn = unfold(_sibling_split_wait(*share[:3], share_parts, done_cond, "share_ffn"))
    done = [update("w_gate", turned(w_gate), g_ffn[0], turned(m_w_gate), turned(v_w_gate), None),
            update("w_up", turned(w_up), g_ffn[1], turned(m_w_up), turned(v_w_up), None),
            update("w_down", w_down[0], g_ffn[2], m_w_down[0], v_w_down[0], w_down.shape)]

    gain_row = lambda r: (lambda s: s[first_gain + r:first_gain + r + 1, :])
    small = [
        ("b_cond", (b_cond, m_b_cond, v_b_cond), (N_MOD, d), lambda s: s[0:N_MOD, :] + s[N_MOD:2 * N_MOD, :]),
        ("g_mix_pre", (g_mix_pre, m_g_mix_pre, v_g_mix_pre), (1, d), gain_row(0)),
        ("g_mix_post", (g_mix_post, m_g_mix_post, v_g_mix_post), (1, d), gain_row(1)),
        ("g_ffn_pre", (g_ffn_pre, m_g_ffn_pre, v_g_ffn_pre), (1, d), gain_row(2)),
        ("g_ffn_post", (g_ffn_post, m_g_ffn_post, v_g_ffn_post), (1, d), gain_row(3)),
        ("pool_scale", (pool_scale, m_pool_scale, v_pool_scale), (1, pw),
         lambda s: s[first_gain + 4:first_gain + 5, 0:pw]),
        ("w_pool", (w_pool, m_w_pool, v_w_pool), (wp_rows * d // w_pool.shape[-1], w_pool.shape[-1]),
         lambda s: jnp.concatenate([s[24:24 + wp_rows, j * w_pool.shape[-1]:(j + 1) * w_pool.shape[-1]]
                                    for j in range(d // w_pool.shape[-1])], axis=0)),
    ]
    updated = _small_updates(summed, [tuple(t.reshape(flat) for t in wmv) + (pick,) for _, wmv, flat, pick in small])
    for (name, wmv, _, _), quad in zip(small, updated):
        results[name] = [t.reshape(wmv[0].shape) for t in quad]

    sums_mix, parts_mix = _chip_exchange_wait(*mix_split[:4], done[-1], "mix")
    g_mix = unfold(_sibling_share(_total_sums(place, sums_mix, parts_mix, done, "mix"), "mix"))
    update("w_in", w_in[0], g_mix[0], m_w_in[0], v_w_in[0], w_in.shape)
    update("w_out", w_out[0], g_mix[1], m_w_out[0], v_w_out[0], w_out.shape)

    names = ("w_cond", "b_cond", "g_mix_pre", "g_mix_post", "w_in", "w_pool", "pool_scale", "w_out",
             "g_ffn_pre", "g_ffn_post", "w_gate", "w_up", "w_down")
    outs = [results[name][part] for part in range(4) for name in names]
    return (loss, gx.reshape(x.shape), *outs)
```

```python
import jax
import jax.numpy as jnp
import numpy as np
from jax import lax
from jax.experimental import pallas as pl
from jax.experimental.pallas import tpu as pltpu

F32 = jnp.float32
BF16 = jnp.bfloat16
MESH = pl.DeviceIdType.MESH

EPS = 1e-6
HEAD_DIM = 64
HEADS_PER_BLOCK = 2
LANES = 128
NEG_QK_SCALE = -0.125
POOL_WINDOWS = (2, 4, 8, 16)
POOL_GROUP = 128
HALO = 16
N_MOD = 6
MOD_ROWS = 8
N_CHIPS = 4
N_DEV = 8
VMEM_LIMIT = 56 * 1024 * 1024

ADAM_LR = 0.001
ADAM_B1 = 0.9
ADAM_B2 = 0.999
ADAM_EPS = 1e-08
ADAM_WD = 0.01
ADAM_STEP = 10

TOKEN_TILE = 512
GRAD_TOKEN_TILE = 2048
FFN_ROW_CHUNKS = 2
ROW_CHUNKS = 2
ATTN_TILE = 512
ATTN_KEY_TILE = 128
ATTN_ROW_CHUNK = 32
LOG_SUM_PASSES = 1


def _dot(a, b):
    return jnp.dot(a, b, preferred_element_type=F32)


def _dot_nt(a, b):
    return lax.dot_general(a, b, (((1,), (1,)), ((), ())), preferred_element_type=F32)


def _dot_tn(a, b):
    return lax.dot_general(a, b, (((0,), (0,)), ((), ())), preferred_element_type=F32)


def _split(v):
    hi = v.astype(BF16)
    lo = (v - hi.astype(F32)).astype(BF16)
    return hi, lo


def _rms(v):
    return lax.rsqrt(jnp.mean(v * v, axis=-1, keepdims=True) + EPS)


def _norm_bwd(dn, n, r):
    return r * (dn - n * jnp.mean(dn * n, axis=-1, keepdims=True))


def _sigmoid(v):
    return 0.5 * jnp.tanh(0.5 * v) + 0.5


def _colsum(v):
    return jnp.sum(v, axis=0, keepdims=True)


def _params(sem=None):
    return pltpu.CompilerParams(dimension_semantics=sem, vmem_limit_bytes=VMEM_LIMIT)


def _position():
    return lax.axis_index("x"), lax.axis_index("y"), lax.axis_index("c")


def _prenorm_proj(x, mod, g_pre, w_in, seq, tm):
    t_all, d = x.shape
    nt = seq // tm
    p = w_in.shape[2]

    def body(x_ref, mod_ref, g_ref, w_ref, h_ref, q_ref, k_ref, v_ref, u_ref, kt_ref, vt_ref):
        for c in range(ROW_CHUNKS):
            rows = slice(c * (tm // ROW_CHUNKS), (c + 1) * (tm // ROW_CHUNKS))
            xf = x_ref[rows, :]
            n = xf * _rms(xf)
            h = (n * g_ref[...]) * (1.0 + mod_ref[0, 1:2, :]) + mod_ref[0, 0:1, :]
            hb = h.astype(BF16)
            h_ref[rows, :] = hb
            q_ref[rows, :] = (_dot(hb, w_ref[0]) * NEG_QK_SCALE).astype(BF16)
            kf = _dot(hb, w_ref[1])
            vf = _dot(hb, w_ref[2])
            k_ref[rows, :] = kf.astype(BF16)
            v_ref[rows, :] = vf.astype(BF16)
            kt_ref[:, rows] = kf.T.astype(BF16)
            vt_ref[:, rows] = vf.T.astype(BF16)
            u_ref[rows, :] = _dot(hb, w_ref[3])

    tok = lambda i: (i, 0)
    tok_t = lambda i: (0, i)
    return pl.pallas_call(
        body, name="prenorm_proj", grid=(t_all // tm,),
        in_specs=[pl.BlockSpec((tm, d), tok),
                  pl.BlockSpec((1, MOD_ROWS, d), lambda i: (i // nt, 0, 0)),
                  pl.BlockSpec((1, d), lambda i: (0, 0)),
                  pl.BlockSpec((N_CHIPS, d, p), lambda i: (0, 0, 0))],
        out_specs=[pl.BlockSpec((tm, d), tok)] + [pl.BlockSpec((tm, p), tok)] * 4 + [pl.BlockSpec((p, tm), tok_t)] * 2,
        out_shape=[jax.ShapeDtypeStruct((t_all, d), BF16)] + [jax.ShapeDtypeStruct((t_all, p), BF16)] * 3
        + [jax.ShapeDtypeStruct((t_all, p), F32)] + [jax.ShapeDtypeStruct((p, t_all), BF16)] * 2,
        compiler_params=_params(("arbitrary",)),
    )(x, mod, g_pre, w_in)


def _tri_matrix(tk, kind):
    j = np.arange(2 * tk)[:, None] % tk
    s = np.arange(tk)[None, :]
    return jnp.asarray({"after": j > s, "upto": j <= s, "before": j < s}[kind], dtype=BF16)


def _neg_abs(v):
    bits = lax.bitcast_convert_type(v, jnp.int32) | jnp.int32(-2 ** 31)
    return lax.bitcast_convert_type(bits, F32)


def _row_sums(v):
    return jnp.broadcast_to(jnp.sum(v, axis=-1, keepdims=True), (v.shape[0], LANES))


def _across(v, n):
    return jnp.concatenate([v] * (n // LANES), axis=1)


def _all_masked(c, diag, rc, tk):
    return diag is not None and diag * tk >= (c + 1) * rc - 1


def _some_masked(c, diag, rc, tk):
    return diag is not None and diag * tk + tk - 1 >= c * rc


def _attn_fwd(qn, k, vt, seq, tq, tk):
    t_all, w = qn.shape
    nb, nq, ndiag = t_all // seq, seq // tq, tq // tk
    assert ndiag % 2 == 0, "two key blocks per loop trip"
    rc = ATTN_ROW_CHUNK
    heads = range(HEADS_PER_BLOCK)

    def body(q_ref, k_ref, vt_ref, tri_ref, o_ref, l_ref,
             z_buf, ls_buf, hl_buf, aft_buf, w_buf, tot_buf, acc_t, run_buf):
        i = pl.program_id(2)
        nblk = (i + 1) * ndiag
        lane = lax.broadcasted_iota(jnp.int32, (1, LANES), 1)
        row = lax.broadcasted_iota(jnp.int32, (rc, tk), 0)
        col = lax.broadcasted_iota(jnp.int32, (rc, tk), 1)
        first = lane < HEAD_DIM
        q2 = q_ref[...]
        qs = [jnp.where(first, q2, jnp.zeros_like(q2)), jnp.where(first, jnp.zeros_like(q2), q2)]
        acc_t[...] = jnp.zeros_like(acc_t)
        run_buf[...] = jnp.zeros_like(run_buf)
        w_buf[1] = jnp.zeros((HEADS_PER_BLOCK, tq, tk), BF16)

        def causal(c, diag):
            return (col + diag * tk) < (row + c * rc)

        def scores(blk, slot):
            kj = k_ref[pl.ds(pl.multiple_of(blk * tk, tk), tk), :]
            for h in heads:
                z_buf[slot, h] = _dot_nt(qs[h], kj)

        def values(blk, slot):
            keys = pl.ds(pl.multiple_of(blk * tk, tk), tk)
            for h in heads:
                dims = slice(h * HEAD_DIM, (h + 1) * HEAD_DIM)
                acc_t[dims, :] += _dot_nt(vt_ref[dims, keys], w_buf[slot, h])

        def softplus_stage(h, slot, diag):
            for c in range(tq // rc):
                rows = slice(c * rc, (c + 1) * rc)
                if _all_masked(c, diag, rc, tk):
                    hl_buf[h, rows, :] = jnp.zeros((rc, LOG_SUM_PASSES * tk), BF16)
                    tot_buf[h, rows, :] = jnp.zeros((rc, LANES), F32)
                    continue
                nz = z_buf[slot, h, rows, :]
                l1 = jnp.minimum(nz, 0.0) - jnp.log(1.0 + jnp.exp(_neg_abs(nz)))
                if _some_masked(c, diag, rc, tk):
                    l1 = jnp.where(causal(c, diag), l1, 0.0)
                for s, part in enumerate(_split(l1)[:LOG_SUM_PASSES]):
                    hl_buf[h, rows, s * tk:(s + 1) * tk] = part
                ls_buf[h, rows, :] = l1 - nz
                tot_buf[h, rows, :] = _row_sums(l1)

        def weights_stage(h, slot, diag):
            for c in range(tq // rc):
                rows = slice(c * rc, (c + 1) * rc)
                if _all_masked(c, diag, rc, tk):
                    w_buf[slot, h, rows, :] = jnp.zeros((rc, tk), BF16)
                    continue
                wgt = jnp.exp((ls_buf[h, rows, :] + aft_buf[h, rows, :]) + _across(run_buf[h, rows, :], tk))
                if _some_masked(c, diag, rc, tk):
                    wgt = jnp.where(causal(c, diag), wgt, 0.0)
                w_buf[slot, h, rows, :] = wgt.astype(BF16)
                run_buf[h, rows, :] += tot_buf[h, rows, :]

        def position(blk, slot, diag):
            scores(jnp.maximum(blk - 1, 0), 1 - slot)
            for h in heads:
                softplus_stage(h, slot, diag)
                aft_buf[h] = _dot(hl_buf[h], tri_ref[...])
            values(jnp.minimum(blk + 1, nblk - 1), 1 - slot)
            for h in heads:
                weights_stage(h, slot, diag)

        scores(nblk - 1, 0)
        for p in range(ndiag):
            position(nblk - 1 - p, p % 2, ndiag - 1 - p)

        def trip(jj, carry):
            for u in range(2):
                position(i * ndiag - 1 - 2 * jj - u, u, None)
            return carry

        lax.fori_loop(0, (i * ndiag) // 2, trip, 0)
        values(0, 1)
        o_ref[...] = acc_t[...].T.astype(BF16)
        l_ref[...] = jnp.where(first, run_buf[0], run_buf[1])

    qmap = lambda b, hp, i: (b * nq + i, hp)
    nh = HEADS_PER_BLOCK
    return pl.pallas_call(
        body, name="attn_fwd", grid=(nb, w // LANES, nq),
        in_specs=[pl.BlockSpec((tq, LANES), qmap), pl.BlockSpec((seq, LANES), lambda b, hp, i: (b, hp)),
                  pl.BlockSpec((LANES, seq), lambda b, hp, i: (hp, b)),
                  pl.BlockSpec((LOG_SUM_PASSES * tk, tk), lambda b, hp, i: (0, 0))],
        out_specs=[pl.BlockSpec((tq, LANES), qmap), pl.BlockSpec((tq, LANES), qmap)],
        out_shape=[jax.ShapeDtypeStruct((t_all, w), BF16), jax.ShapeDtypeStruct((t_all, w), F32)],
        scratch_shapes=[pltpu.VMEM((2, nh, tq, tk), F32), pltpu.VMEM((nh, tq, tk), F32),
                        pltpu.VMEM((nh, tq, LOG_SUM_PASSES * tk), BF16), pltpu.VMEM((nh, tq, tk), F32),
                        pltpu.VMEM((2, nh, tq, tk), BF16), pltpu.VMEM((nh, tq, LANES), F32),
                        pltpu.VMEM((LANES, tq), F32), pltpu.VMEM((nh, tq, LANES), F32)],
        compiler_params=_params(("arbitrary", "arbitrary", "arbitrary")),
    )(qn, k, vt, _tri_matrix(tk, "after")[:LOG_SUM_PASSES * tk])


def _window_sums(ext, rows, offset, forward):
    r = lax.broadcasted_iota(jnp.int32, (rows, rows + HALO), 0)
    e = lax.broadcasted_iota(jnp.int32, (rows, rows + HALO), 1)
    hi, lo = _split(ext)
    out = []
    for g, win in enumerate(POOL_WINDOWS):
        if forward:
            band = (e >= r) & (e < r + win)
        else:
            band = (e <= r + offset) & (e > r + offset - win)
        bm = band.astype(BF16)
        cols = slice(g * POOL_GROUP, (g + 1) * POOL_GROUP)
        out.append(_dot(bm, hi[:, cols]) + _dot(bm, lo[:, cols]))
    return out


def _window_counts(pos):
    return [jnp.minimum(pos + 1, win).astype(F32) for win in POOL_WINDOWS]


def _mixer_post(u, o, x, mod, g_post, g_fpre, w_pool, pool_scale, w_out, seq, tm):
    t_all, d = x.shape
    nt = seq // tm
    p = u.shape[1]

    def body(u_ref, halo_ref, o_ref, x_ref, mod_ref, gp_ref, gf_ref, wp_ref, ps_ref, wo_ref,
             pooled_ref, mixin_ref, mix_ref, x1_ref, h2_ref):
        it = pl.program_id(0) % nt
        uf = u_ref[...]
        halo = jnp.where(it == 0, 0.0, halo_ref[...])
        ext = jnp.concatenate([halo, uf], axis=0)
        pos = it * tm + lax.broadcasted_iota(jnp.int32, (tm, 1), 0)
        sums = _window_sums(ext, tm, HALO, False)
        cnts = _window_counts(pos)
        pools = []
        for g in range(len(POOL_WINDOWS)):
            cols = slice(g * POOL_GROUP, (g + 1) * POOL_GROUP)
            pooled = (sums[g] / cnts[g] - uf[:, cols]).astype(BF16)
            pooled_ref[:, cols] = pooled
            yg = _dot(pooled, wp_ref[g].astype(BF16))
            pools.append((yg * ps_ref[:, cols]).astype(BF16))
        mixin_ref[...] = jnp.concatenate([o_ref[...]] + pools, axis=1)
        for c in range(ROW_CHUNKS):
            rows = slice(c * (tm // ROW_CHUNKS), (c + 1) * (tm // ROW_CHUNKS))
            mix = _dot(mixin_ref[rows, :], wo_ref[...])
            mix_ref[rows, :] = mix
            n2 = mix * _rms(mix)
            x1 = x_ref[rows, :] + mod_ref[0, 2:3, :] * (n2 * gp_ref[...])
            x1_ref[rows, :] = x1
            n3 = x1 * _rms(x1)
            h2 = (n3 * gf_ref[...]) * (1.0 + mod_ref[0, 4:5, :]) + mod_ref[0, 3:4, :]
            h2_ref[rows, :] = h2.astype(BF16)

    tok = lambda i: (i, 0)
    const2 = lambda i: (0, 0)
    hb = tm // HALO
    return pl.pallas_call(
        body, name="mixer_post", grid=(t_all // tm,),
        in_specs=[pl.BlockSpec((tm, p), tok),
                  pl.BlockSpec((HALO, p), lambda i: (jnp.maximum(i * hb - 1, 0), 0)),
                  pl.BlockSpec((tm, p), tok),
                  pl.BlockSpec((tm, d), tok),
                  pl.BlockSpec((1, MOD_ROWS, d), lambda i: (i // nt, 0, 0)),
                  pl.BlockSpec((1, d), const2), pl.BlockSpec((1, d), const2),
                  pl.BlockSpec(w_pool.shape, lambda i: (0, 0, 0)),
                  pl.BlockSpec((1, p), const2),
                  pl.BlockSpec((d, d), const2)],
        out_specs=[pl.BlockSpec((tm, p), tok), pl.BlockSpec((tm, d), tok), pl.BlockSpec((tm, d), tok),
                   pl.BlockSpec((tm, d), tok), pl.BlockSpec((tm, d), tok)],
        out_shape=[jax.ShapeDtypeStruct((t_all, p), BF16), jax.ShapeDtypeStruct((t_all, d), BF16),
                   jax.ShapeDtypeStruct((t_all, d), F32), jax.ShapeDtypeStruct((t_all, d), F32),
                   jax.ShapeDtypeStruct((t_all, d), BF16)],
        compiler_params=_params(("arbitrary",)),
    )(u, u, o, x, mod, g_post, g_fpre, w_pool, pool_scale, w_out)


def _ffn_fwd(h2, w_g, w_u, w_d, x1, tgt, mod, g_post, seq, tm):
    t_all, d = x1.shape
    nt = seq // tm
    nk, ff, _ = w_g.shape

    def body(h_ref, wg_ref, wu_ref, wd_ref, x1_ref, t_ref, mod_ref, g_ref,
             a_ref, b_ref, fin_ref, dy_ref, df_ref, loss_ref, accb_ref, accg_ref, facc):
        k, i = pl.program_id(0), pl.program_id(1)
        chunk = tm // FFN_ROW_CHUNKS

        @pl.when(k == 0)
        def _():
            facc[pl.ds(pl.multiple_of(i * tm, tm), tm), :] = jnp.zeros((tm, d), F32)

        for c in range(FFN_ROW_CHUNKS):
            rows = slice(c * chunk, (c + 1) * chunk)
            acc_rows = pl.ds(pl.multiple_of(i * tm + c * chunk, chunk), chunk)
            hb = h_ref[rows, :]
            a = _dot_nt(hb, wg_ref[0])
            b = _dot_nt(hb, wu_ref[0])
            a_ref[0, rows, :] = a.astype(BF16)
            b_ref[0, rows, :] = b.astype(BF16)
            fin = ((a * _sigmoid(a)) * b).astype(BF16)
            fin_ref[0, rows, :] = fin
            facc[acc_rows, :] += _dot(fin, wd_ref[0])

        @pl.when(k == nk - 1)
        def _():
            f = facc[pl.ds(pl.multiple_of(i * tm, tm), tm), :]
            r4 = _rms(f)
            n4 = f * r4
            gate = mod_ref[0, 5:6, :]
            g = g_ref[...]
            err = (x1_ref[...] + gate * (n4 * g)) - t_ref[...]
            dy = err * (1.0 / d)
            dy_ref[...] = dy

            @pl.when(i == 0)
            def _():
                loss_ref[...] = jnp.zeros_like(loss_ref)
                accg_ref[...] = jnp.zeros_like(accg_ref)

            @pl.when(i % nt == 0)
            def _():
                accb_ref[...] = jnp.zeros_like(accb_ref)

            loss_ref[...] += (0.5 / d) * jnp.sum(err * err)
            accb_ref[0, 0:1, :] += _colsum(dy * (n4 * g))
            accg_ref[0:1, :] += _colsum((dy * gate) * n4)
            dn4 = (dy * gate) * g
            df_ref[...] = _norm_bwd(dn4, n4, r4).astype(BF16)

    last = lambda k, i: jnp.where(k == nk - 1, i, 0)
    tok = lambda k, i: (i, 0)
    late = lambda k, i: (last(k, i), 0)
    ktok = lambda k, i: (k, i, 0)
    kw = lambda k, i: (k, 0, 0)
    const2 = lambda k, i: (0, 0)
    return pl.pallas_call(
        body, name="ffn_fwd", grid=(nk, t_all // tm),
        in_specs=[pl.BlockSpec((tm, d), tok),
                  pl.BlockSpec((1, ff, d), kw), pl.BlockSpec((1, ff, d), kw), pl.BlockSpec((1, ff, d), kw),
                  pl.BlockSpec((tm, d), late), pl.BlockSpec((tm, d), late),
                  pl.BlockSpec((1, MOD_ROWS, d), lambda k, i: (last(k, i) // nt, 0, 0)),
                  pl.BlockSpec((1, d), const2)],
        out_specs=[pl.BlockSpec((1, tm, ff), ktok)] * 3
        + [pl.BlockSpec((tm, d), late), pl.BlockSpec((tm, d), late),
           pl.BlockSpec((8, LANES), const2),
           pl.BlockSpec((1, 8, d), lambda k, i: (last(k, i) // nt, 0, 0)),
           pl.BlockSpec((8, d), const2)],
        out_shape=[jax.ShapeDtypeStruct((nk, t_all, ff), BF16)] * 3
        + [jax.ShapeDtypeStruct((t_all, d), F32), jax.ShapeDtypeStruct((t_all, d), BF16),
           jax.ShapeDtypeStruct((8, LANES), F32),
           jax.ShapeDtypeStruct((t_all // seq, 8, d), F32),
           jax.ShapeDtypeStruct((8, d), F32)],
        scratch_shapes=[pltpu.VMEM((t_all, d), F32)],
        compiler_params=_params(("arbitrary", "arbitrary")),
    )(h2, w_g, w_u, w_d, x1, tgt, mod, g_post)


def _ffn_bwd(df, a, b, w_d, w_g, w_u, x1, dy, mix, mod, g_fpre, g_mpost, seq, tm):
    t_all, d = x1.shape
    nt = seq // tm
    nk, ff, _ = w_g.shape

    def body(df_ref, a_ref, b_ref, wd_ref, wg_ref, wu_ref, x1_ref, dy_ref, mix_ref, mod_ref, gf_ref, gm_ref,
             da_ref, db_ref, dx1_ref, dmix_ref, accb_ref, accg_ref, hacc):
        i, k = pl.program_id(0), pl.program_id(1)

        @pl.when(k == 0)
        def _():
            hacc[...] = jnp.zeros_like(hacc)

        for c in range(FFN_ROW_CHUNKS):
            rows = slice(c * (tm // FFN_ROW_CHUNKS), (c + 1) * (tm // FFN_ROW_CHUNKS))
            dfin = _dot_nt(df_ref[rows, :], wd_ref[0])
            af = a_ref[0, rows, :].astype(F32)
            bf = b_ref[0, rows, :].astype(F32)
            sig = _sigmoid(af)
            da = ((dfin * bf) * (sig * (1.0 + af * (1.0 - sig)))).astype(BF16)
            db = (dfin * (af * sig)).astype(BF16)
            da_ref[0, rows, :] = da
            db_ref[0, rows, :] = db
            hacc[rows, :] += _dot(da, wg_ref[0]) + _dot(db, wu_ref[0])

        @pl.when(k == nk - 1)
        def _():
            @pl.when(i == 0)
            def _():
                accg_ref[...] = jnp.zeros_like(accg_ref)

            @pl.when(i % nt == 0)
            def _():
                accb_ref[...] = jnp.zeros_like(accb_ref)

            dh2 = hacc[...]
            x1 = x1_ref[...]
            r3 = _rms(x1)
            n3 = x1 * r3
            g3 = gf_ref[...]
            scale1 = 1.0 + mod_ref[0, 4:5, :]
            accb_ref[0, 0:1, :] += _colsum(dh2)
            accb_ref[0, 1:2, :] += _colsum(dh2 * (n3 * g3))
            accg_ref[0:1, :] += _colsum((dh2 * scale1) * n3)
            dx1 = dy_ref[...] + _norm_bwd((dh2 * scale1) * g3, n3, r3)
            dx1_ref[...] = dx1
            mix = mix_ref[...]
            r2 = _rms(mix)
            n2 = mix * r2
            g2 = gm_ref[...]
            gate = mod_ref[0, 2:3, :]
            accb_ref[0, 2:3, :] += _colsum(dx1 * (n2 * g2))
            accg_ref[1:2, :] += _colsum((dx1 * gate) * n2)
            dmix_ref[...] = _norm_bwd((dx1 * gate) * g2, n2, r2).astype(BF16)

    tok = lambda i, k: (i, 0)
    ktok = lambda i, k: (k, i, 0)
    kw = lambda i, k: (k, 0, 0)
    const2 = lambda i, k: (0, 0)
    return pl.pallas_call(
        body, name="ffn_bwd", grid=(t_all // tm, nk),
        in_specs=[pl.BlockSpec((tm, d), tok),
                  pl.BlockSpec((1, tm, ff), ktok), pl.BlockSpec((1, tm, ff), ktok),
                  pl.BlockSpec((1, ff, d), kw), pl.BlockSpec((1, ff, d), kw), pl.BlockSpec((1, ff, d), kw),
                  pl.BlockSpec((tm, d), tok), pl.BlockSpec((tm, d), tok), pl.BlockSpec((tm, d), tok),
                  pl.BlockSpec((1, MOD_ROWS, d), lambda i, k: (i // nt, 0, 0)),
                  pl.BlockSpec((1, d), const2), pl.BlockSpec((1, d), const2)],
        out_specs=[pl.BlockSpec((1, tm, ff), ktok)] * 2
        + [pl.BlockSpec((tm, d), tok), pl.BlockSpec((tm, d), tok),
           pl.BlockSpec((1, 8, d), lambda i, k: (i // nt, 0, 0)),
           pl.BlockSpec((8, d), const2)],
        out_shape=[jax.ShapeDtypeStruct((nk, t_all, ff), BF16)] * 2
        + [jax.ShapeDtypeStruct((t_all, d), F32), jax.ShapeDtypeStruct((t_all, d), BF16),
           jax.ShapeDtypeStruct((t_all // seq, 8, d), F32),
           jax.ShapeDtypeStruct((8, d), F32)],
        scratch_shapes=[pltpu.VMEM((tm, d), F32)],
        compiler_params=_params(("arbitrary", "arbitrary")),
    )(df, a, b, w_d, w_g, w_u, x1, dy, mix, mod, g_fpre, g_mpost)


def _mixer_bwd(dmix, w_out, pooled, w_pool, pool_scale, seq, tm):
    t_all, d = dmix.shape
    p = pooled.shape[1]
    ng = len(POOL_WINDOWS)

    def body(dm_ref, wo_ref, pooled_ref, wp_ref, ps_ref, do_ref, dpd_ref, dps_ref, dwp_ref):
        i = pl.program_id(0)

        @pl.when(i == 0)
        def _():
            dps_ref[...] = jnp.zeros_like(dps_ref)
            dwp_ref[...] = jnp.zeros_like(dwp_ref)

        dmixin = _dot_nt(dm_ref[...], wo_ref[...])
        do_ref[...] = dmixin[:, :p].astype(BF16)
        for g in range(ng):
            cols = slice(g * POOL_GROUP, (g + 1) * POOL_GROUP)
            dpool = dmixin[:, p + g * POOL_GROUP:p + (g + 1) * POOL_GROUP]
            pooled = pooled_ref[:, cols]
            wpg = wp_ref[g].astype(BF16)
            yg = _dot(pooled, wpg)
            dps_ref[0:1, cols] += _colsum(dpool * yg)
            dyg = (dpool * ps_ref[:, cols]).astype(BF16)
            dwp_ref[g] += _dot_tn(pooled, dyg)
            dpd_ref[:, cols] = _dot_nt(dyg, wpg)

    tok = lambda i: (i, 0)
    const2 = lambda i: (0, 0)
    const3 = lambda i: (0, 0, 0)
    return pl.pallas_call(
        body, name="mixer_bwd", grid=(t_all // tm,),
        in_specs=[pl.BlockSpec((tm, d), tok), pl.BlockSpec((d, d), const2), pl.BlockSpec((tm, p), tok),
                  pl.BlockSpec(w_pool.shape, const3), pl.BlockSpec((1, p), const2)],
        out_specs=[pl.BlockSpec((tm, p), tok), pl.BlockSpec((tm, p), tok),
                   pl.BlockSpec((8, p), const2), pl.BlockSpec(w_pool.shape, const3)],
        out_shape=[jax.ShapeDtypeStruct((t_all, p), BF16), jax.ShapeDtypeStruct((t_all, p), F32),
                   jax.ShapeDtypeStruct((8, p), F32), jax.ShapeDtypeStruct(w_pool.shape, F32)],
        compiler_params=_params(("arbitrary",)),
    )(dmix, w_out, pooled, w_pool, pool_scale)


def _attn_bwd(qn, k, kt, v, do, ltot, seq, tq, tk, order):
    t_all, w = qn.shape
    nb, nq, ndiag, nkb = t_all // seq, seq // tq, tq // tk, seq // tk
    assert ndiag % 2 == 0, "two key blocks per loop trip"
    rc = ATTN_ROW_CHUNK
    nh = HEADS_PER_BLOCK
    heads = range(nh)

    def body(q_ref, k_ref, kt_ref, v_ref, do_ref, l_ref, up_ref, bf_ref, dq_ref, dk_ref, dv_ref,
             z_buf, dw_buf, ls_buf, hl_buf, upto_buf, g_buf, gb_buf, before_buf, w_buf, dz_buf,
             totl_buf, totg_buf, rem_buf, preg_buf, qnt_buf, dot_buf, dq_t, dk_t, dv_t):
        i = pl.program_id(2)
        nblk = (i + 1) * ndiag

        @pl.when(i == 0)
        def _():
            dk_t[...] = jnp.zeros_like(dk_t)
            dv_t[...] = jnp.zeros_like(dv_t)

        lane = lax.broadcasted_iota(jnp.int32, (1, LANES), 1)
        row = lax.broadcasted_iota(jnp.int32, (rc, tk), 0)
        col = lax.broadcasted_iota(jnp.int32, (rc, tk), 1)
        first = lane < HEAD_DIM
        q2 = q_ref[...]
        do2 = do_ref[...]
        l2 = l_ref[...]
        qs = [jnp.where(first, q2, jnp.zeros_like(q2)), jnp.where(first, jnp.zeros_like(q2), q2)]
        dos = [jnp.where(first, do2, jnp.zeros_like(do2)), jnp.where(first, jnp.zeros_like(do2), do2)]
        qnt_buf[...] = q2.astype(F32).T.astype(BF16)
        dot_buf[...] = do2.astype(F32).T.astype(BF16)
        for h in heads:
            rem_buf[h] = jnp.where(first if h == 0 else ~first, l2, pltpu.roll(l2, HEAD_DIM, 1))
        preg_buf[...] = jnp.zeros_like(preg_buf)
        dq_t[...] = jnp.zeros_like(dq_t)
        w_buf[1] = jnp.zeros((nh * tq, tk), BF16)
        dz_buf[1] = jnp.zeros((nh * tq, tk), BF16)

        def causal(c, diag):
            return (col + diag * tk) < (row + c * rc)

        def scores(blk, slot):
            off = pl.multiple_of(blk * tk, tk)
            kj = k_ref[pl.ds(off, tk), :]
            vj = v_ref[pl.ds(off, tk), :]
            for h in heads:
                z_buf[slot, h] = _dot_nt(qs[h], kj)
                dw_buf[slot, h] = _dot_nt(dos[h], vj)

        def gradients(blk, slot):
            keys = pl.ds(pl.multiple_of(blk * tk, tk), tk)
            for h in heads:
                dims = slice(h * HEAD_DIM, (h + 1) * HEAD_DIM)
                queries = slice(h * tq, (h + 1) * tq)
                dq_t[dims, :] += _dot_nt(kt_ref[dims, keys], dz_buf[slot, queries, :])
                dk_t[blk, dims, :] += _dot(qnt_buf[dims, :], dz_buf[slot, queries, :])
                dv_t[blk, dims, :] += _dot(dot_buf[dims, :], w_buf[slot, queries, :])

        def softplus_stage(h, slot, diag):
            for c in range(tq // rc):
                rows = slice(c * rc, (c + 1) * rc)
                if _all_masked(c, diag, rc, tk):
                    hl_buf[h, rows, :] = jnp.zeros((rc, LOG_SUM_PASSES * tk), BF16)
                    continue
                nz = z_buf[slot, h, rows, :]
                l1 = jnp.minimum(nz, 0.0) - jnp.log(1.0 + jnp.exp(_neg_abs(nz)))
                if _some_masked(c, diag, rc, tk):
                    l1 = jnp.where(causal(c, diag), l1, 0.0)
                for s, part in enumerate(_split(l1)[:LOG_SUM_PASSES]):
                    hl_buf[h, rows, s * tk:(s + 1) * tk] = part
                ls_buf[h, rows, :] = l1 - nz
                totl_buf[h, rows, :] = _row_sums(l1)

        def weights_stage(h, slot, diag):
            for c in range(tq // rc):
                rows = slice(c * rc, (c + 1) * rc)
                stacked = slice(h * tq + c * rc, h * tq + (c + 1) * rc)
                if _all_masked(c, diag, rc, tk):
                    w_buf[slot, stacked, :] = jnp.zeros((rc, tk), BF16)
                    gb_buf[h, rows, :] = jnp.zeros((rc, tk), BF16)
                    continue
                wgt = jnp.exp(ls_buf[h, rows, :] + (_across(rem_buf[h, rows, :], tk) - upto_buf[h, rows, :]))
                if _some_masked(c, diag, rc, tk):
                    wgt = jnp.where(causal(c, diag), wgt, 0.0)
                w_buf[slot, stacked, :] = wgt.astype(BF16)
                g = wgt * dw_buf[slot, h, rows, :]
                g_buf[h, rows, :] = g
                gb_buf[h, rows, :] = g.astype(BF16)
                totg_buf[h, rows, :] = _row_sums(g)
                rem_buf[h, rows, :] -= totl_buf[h, rows, :]

        def dscore_stage(h, slot, diag):
            for c in range(tq // rc):
                rows = slice(c * rc, (c + 1) * rc)
                stacked = slice(h * tq + c * rc, h * tq + (c + 1) * rc)
                if _all_masked(c, diag, rc, tk):
                    dz_buf[slot, stacked, :] = jnp.zeros((rc, tk), BF16)
                    continue
                sig = jnp.exp(ls_buf[h, rows, :])
                g = g_buf[h, rows, :]
                dnz = sig * ((before_buf[h, rows, :] + _across(preg_buf[h, rows, :], tk)) + g) - g
                if _some_masked(c, diag, rc, tk):
                    dnz = jnp.where(causal(c, diag), dnz, 0.0)
                dz_buf[slot, stacked, :] = dnz.astype(BF16)
                preg_buf[h, rows, :] += totg_buf[h, rows, :]

        def position(blk, slot, diag, prefetch):
            if prefetch:
                scores(blk + 1, 1 - slot)
            for h in heads:
                softplus_stage(h, slot, diag)
                upto_buf[h] = _dot(hl_buf[h], up_ref[...])
            gradients(jnp.maximum(blk - 1, 0), 1 - slot)
            for h in heads:
                weights_stage(h, slot, diag)
                before_buf[h] = _dot(gb_buf[h], bf_ref[...])
            for h in heads:
                dscore_stage(h, slot, diag)

        scores(0, 0)

        def trip(jj, carry):
            for u in range(2):
                position(2 * jj + u, u, None, True)
            return carry

        lax.fori_loop(0, (i * ndiag) // 2, trip, 0)
        for d in range(ndiag):
            position(i * ndiag + d, d % 2, d, d < ndiag - 1)
        gradients(nblk - 1, 1)
        dq_ref[...] = (dq_t[...].T * NEG_QK_SCALE).astype(BF16)

        @pl.when(i == nq - 1)
        def _():
            for blk in range(nkb):
                dk_ref[blk * tk:(blk + 1) * tk, :] = dk_t[blk].T.astype(BF16)
                dv_ref[blk * tk:(blk + 1) * tk, :] = dv_t[blk].T.astype(BF16)

    qmap = lambda b, hp, i: (b * nq + i, hp)
    kmap = lambda b, hp, i: (b, hp)
    const = lambda b, hp, i: (0, 0)
    return pl.pallas_call(
        body, name="attn_bwd", grid=(nb, w // LANES, nq),
        in_specs=[pl.BlockSpec((tq, LANES), qmap), pl.BlockSpec((seq, LANES), kmap),
                  pl.BlockSpec((LANES, seq), lambda b, hp, i: (hp, b)), pl.BlockSpec((seq, LANES), kmap),
                  pl.BlockSpec((tq, LANES), qmap), pl.BlockSpec((tq, LANES), qmap),
                  pl.BlockSpec((LOG_SUM_PASSES * tk, tk), const), pl.BlockSpec((tk, tk), const)],
        out_specs=[pl.BlockSpec((tq, LANES), qmap), pl.BlockSpec((seq, LANES), kmap), pl.BlockSpec((seq, LANES), kmap)],
        out_shape=[jax.ShapeDtypeStruct((t_all, w), BF16)] * 3,
        scratch_shapes=[pltpu.VMEM((2, nh, tq, tk), F32), pltpu.VMEM((2, nh, tq, tk), F32),
                        pltpu.VMEM((nh, tq, tk), F32), pltpu.VMEM((nh, tq, LOG_SUM_PASSES * tk), BF16),
                        pltpu.VMEM((nh, tq, tk), F32), pltpu.VMEM((nh, tq, tk), F32),
                        pltpu.VMEM((nh, tq, tk), BF16), pltpu.VMEM((nh, tq, tk), F32),
                        pltpu.VMEM((2, nh * tq, tk), BF16), pltpu.VMEM((2, nh * tq, tk), BF16),
                        pltpu.VMEM((nh, tq, LANES), F32), pltpu.VMEM((nh, tq, LANES), F32),
                        pltpu.VMEM((nh, tq, LANES), F32), pltpu.VMEM((nh, tq, LANES), F32),
                        pltpu.VMEM((LANES, tq), BF16), pltpu.VMEM((LANES, tq), BF16),
                        pltpu.VMEM((LANES, tq), F32), pltpu.VMEM((nkb, LANES, tk), F32),
                        pltpu.VMEM((nkb, LANES, tk), F32)],
        compiler_params=_params(("arbitrary", "arbitrary", "arbitrary")),
    )(qn, k, kt, v, do, ltot, _tri_matrix(tk, "upto")[:LOG_SUM_PASSES * tk] + order.astype(BF16),
      _tri_matrix(tk, "before")[:tk])


def _inproj_bwd(dq, dk, dv, dpd, x, dx1, mod, g_pre, w_in, seq, tm):
    t_all, d = x.shape
    nt = seq // tm
    p = dq.shape[1]

    def body(dq_ref, dk_ref, dv_ref, dpd_ref, halo_ref, x_ref, dx1_ref, mod_ref, g_ref, w_ref,
             gx_ref, du_ref, accb_ref, accg_ref):
        i = pl.program_id(0)
        it = i % nt

        @pl.when(i == 0)
        def _():
            accg_ref[...] = jnp.zeros_like(accg_ref)

        @pl.when(it == 0)
        def _():
            accb_ref[...] = jnp.zeros_like(accb_ref)

        dpd = dpd_ref[...]
        pos = it * tm + lax.broadcasted_iota(jnp.int32, (tm, 1), 0)
        cnts = _window_counts(pos)
        halo = jnp.where(it == nt - 1, 0.0, halo_ref[...])
        scaled = []
        halos = []
        for g, win in enumerate(POOL_WINDOWS):
            cols = slice(g * POOL_GROUP, (g + 1) * POOL_GROUP)
            scaled.append(dpd[:, cols] / cnts[g])
            halos.append(halo[:, cols] / float(win))
        ext = jnp.concatenate([jnp.concatenate(scaled, axis=1), jnp.concatenate(halos, axis=1)], axis=0)
        sums = _window_sums(ext, tm, 0, True)
        du = (jnp.concatenate(sums, axis=1) - dpd).astype(BF16)
        du_ref[...] = du
        g1 = g_ref[...]
        scale1 = 1.0 + mod_ref[0, 1:2, :]
        for c in range(ROW_CHUNKS):
            rows = slice(c * (tm // ROW_CHUNKS), (c + 1) * (tm // ROW_CHUNKS))
            dh1 = (_dot_nt(dq_ref[rows, :], w_ref[0]) + _dot_nt(dk_ref[rows, :], w_ref[1])
                   + _dot_nt(dv_ref[rows, :], w_ref[2]) + _dot_nt(du_ref[rows, :], w_ref[3]))
            xf = x_ref[rows, :]
            r1 = _rms(xf)
            n1 = xf * r1
            accb_ref[0, 0:1, :] += _colsum(dh1)
            accb_ref[0, 1:2, :] += _colsum(dh1 * (n1 * g1))
            accg_ref[0:1, :] += _colsum((dh1 * scale1) * n1)
            gx_ref[rows, :] = dx1_ref[rows, :] + _norm_bwd((dh1 * scale1) * g1, n1, r1)

    tok = lambda i: (i, 0)
    const2 = lambda i: (0, 0)
    hb = tm // HALO
    last = t_all // HALO - 1
    return pl.pallas_call(
        body, name="inproj_bwd", grid=(t_all // tm,),
        in_specs=[pl.BlockSpec((tm, p), tok), pl.BlockSpec((tm, p), tok), pl.BlockSpec((tm, p), tok),
                  pl.BlockSpec((tm, p), tok),
                  pl.BlockSpec((HALO, p), lambda i: (jnp.minimum((i + 1) * hb, last), 0)),
                  pl.BlockSpec((tm, d), tok), pl.BlockSpec((tm, d), tok),
                  pl.BlockSpec((1, MOD_ROWS, d), lambda i: (i // nt, 0, 0)),
                  pl.BlockSpec((1, d), const2),
                  pl.BlockSpec((N_CHIPS, d, p), lambda i: (0, 0, 0))],
        out_specs=[pl.BlockSpec((tm, d), tok), pl.BlockSpec((tm, p), tok),
                   pl.BlockSpec((1, 8, d), lambda i: (i // nt, 0, 0)),
                   pl.BlockSpec((8, d), const2)],
        out_shape=[jax.ShapeDtypeStruct((t_all, d), F32), jax.ShapeDtypeStruct((t_all, p), BF16),
                   jax.ShapeDtypeStruct((t_all // seq, 8, d), F32),
                   jax.ShapeDtypeStruct((8, d), F32)],
        compiler_params=_params(("arbitrary",)),
    )(dq, dk, dv, dpd, dpd, x, dx1, mod, g_pre, w_in)


def _tn_matmul(x, ys, nk, bt, name, after=()):
    t_all = x.shape[-2]
    m = x.shape[-1]
    ny = len(ys)
    nt = t_all // bt

    def spec(arr):
        if arr.ndim == 3:
            return pl.BlockSpec((1, bt, arr.shape[-1]), lambda k, t: (k, t, 0))
        return pl.BlockSpec((bt, arr.shape[-1]), lambda k, t: (t, 0))

    def tile(ref):
        return ref[0] if len(ref.shape) == 3 else ref[...]

    def body(*refs):
        outs = refs[1 + ny + len(after):]
        x_ref, y_refs, o_refs, h_refs = refs[0], refs[1:1 + ny], outs[:ny], outs[ny:]
        t = pl.program_id(1)
        xt = tile(x_ref)
        for y_ref, o_ref, h_ref in zip(y_refs, o_refs, h_refs):
            part = _dot_tn(xt, tile(y_ref))

            @pl.when(t == 0)
            def _(o_ref=o_ref, part=part):
                o_ref[0] = part

            @pl.when(t > 0)
            def _(o_ref=o_ref, part=part):
                o_ref[0] += part

            @pl.when(t == nt - 1)
            def _(o_ref=o_ref, h_ref=h_ref):
                h_ref[0] = o_ref[0].astype(BF16)

    out_specs = [pl.BlockSpec((1, m, y.shape[-1]), lambda k, t: (k, 0, 0)) for y in ys]
    out = pl.pallas_call(
        body, name=name, grid=(nk, nt),
        in_specs=[spec(x)] + [spec(y) for y in ys] + [_ANY] * len(after),
        out_specs=out_specs * 2,
        out_shape=[jax.ShapeDtypeStruct((nk, m, y.shape[-1]), dt) for dt in (F32, BF16) for y in ys],
        compiler_params=_params(("arbitrary", "arbitrary")),
    )(x, *ys, *after)
    return out[:ny], out[ny:]


def _tn_matmul_stacked(x, ys, bt, name, after=()):
    t_all, m = x.shape
    n = ys[0].shape[1]
    ny = len(ys)
    nt = t_all // bt

    def body(*refs):
        x_ref, y_refs, (o_ref, h_ref) = refs[0], refs[1:1 + ny], refs[1 + ny + len(after):]
        t = pl.program_id(0)
        xt = x_ref[...]

        @pl.when(t == 0)
        def _():
            o_ref[...] = jnp.zeros_like(o_ref)

        for j, y_ref in enumerate(y_refs):
            o_ref[j] += _dot_tn(xt, y_ref[...])

        @pl.when(t == nt - 1)
        def _():
            h_ref[...] = o_ref[...].astype(BF16)

    whole = pl.BlockSpec((ny, m, n), lambda t: (0, 0, 0))
    return pl.pallas_call(
        body, name=name, grid=(nt,),
        in_specs=[pl.BlockSpec((bt, m), lambda t: (t, 0))] + [pl.BlockSpec((bt, n), lambda t: (t, 0))] * ny
        + [_ANY] * len(after),
        out_specs=[whole, whole],
        out_shape=[jax.ShapeDtypeStruct((ny, m, n), F32), jax.ShapeDtypeStruct((ny, m, n), BF16)],
        compiler_params=_params(("arbitrary",)),
    )(x, *ys, *after)


def _cond_fwd(c_all, w_q, b_q, bn):
    nrow, d = c_all.shape
    ncol = w_q.shape[1]

    def body(c_ref, w_ref, b_ref, sc_ref, mod_ref):
        cf = c_ref[...]
        sc = cf * _sigmoid(cf)
        sc_ref[...] = sc
        shi, slo = _split(sc)
        whi, wlo = _split(w_ref[...])
        mod_ref[...] = (_dot(shi, whi) + _dot(shi, wlo) + _dot(slo, whi)) + b_ref[...]

    return pl.pallas_call(
        body, name="cond_fwd", grid=(ncol // bn,),
        in_specs=[pl.BlockSpec((nrow, d), lambda n: (0, 0)), pl.BlockSpec((d, bn), lambda n: (0, n)),
                  pl.BlockSpec((1, bn), lambda n: (0, n))],
        out_specs=[pl.BlockSpec((nrow, d), lambda n: (0, 0)), pl.BlockSpec((nrow, bn), lambda n: (0, n))],
        out_shape=[jax.ShapeDtypeStruct((nrow, d), F32), jax.ShapeDtypeStruct((nrow, ncol), F32)],
        compiler_params=_params(("arbitrary",)),
    )(c_all, w_q, b_q)


def _cond_bwd(sc_all, dmod_q, bn):
    nrow, d = sc_all.shape
    ncol = dmod_q.shape[1]

    def body(sc_ref, dm_ref, gw_ref):
        shi, slo = _split(sc_ref[...])
        dhi, dlo = _split(dm_ref[...])
        gw_ref[...] = _dot_tn(shi, dhi) + _dot_tn(shi, dlo) + _dot_tn(slo, dhi)

    return pl.pallas_call(
        body, name="cond_bwd", grid=(ncol // bn,),
        in_specs=[pl.BlockSpec((nrow, d), lambda n: (0, 0)), pl.BlockSpec((nrow, bn), lambda n: (0, n))],
        out_specs=pl.BlockSpec((d, bn), lambda n: (0, n)),
        out_shape=jax.ShapeDtypeStruct((d, ncol), F32),
        compiler_params=_params(("arbitrary",)),
    )(sc_all, dmod_q)


def _row_block(rows, cols, budget=1 << 18):
    best = None
    for br in range(8, rows + 1, 8):
        if rows % br == 0 and br * cols <= budget:
            best = br
    return best if best is not None else rows


def _adam_math(w, g, m, v):
    c1 = 1.0 - ADAM_B1 ** ADAM_STEP
    c2 = 1.0 - ADAM_B2 ** ADAM_STEP
    m2 = ADAM_B1 * m + (1.0 - ADAM_B1) * g
    v2 = ADAM_B2 * v + (1.0 - ADAM_B2) * (g * g)
    return -ADAM_LR * ((m2 / c1) / (jnp.sqrt(v2 / c2) + ADAM_EPS) + ADAM_WD * w), m2, v2


def _small_updates(summed, params):
    n = len(params)

    def body(s_ref, *refs):
        ins, outs = refs[:3 * n], refs[3 * n:]
        for p, (_, _, _, pick) in enumerate(params):
            w_ref, m_ref, v_ref = ins[3 * p:3 * p + 3]
            g = pick(s_ref)
            delta, m2, v2 = _adam_math(w_ref[...], g, m_ref[...], v_ref[...])
            for o_ref, val in zip(outs[4 * p:4 * p + 4], (g, delta, m2, v2)):
                o_ref[...] = val

    out = pl.pallas_call(
        body, name="adamw_small",
        out_shape=[jax.ShapeDtypeStruct(w.shape, F32) for w, _, _, _ in params for _ in range(4)],
        compiler_params=pltpu.CompilerParams(vmem_limit_bytes=VMEM_LIMIT),
    )(summed, *[t for w, m, v, _ in params for t in (w, m, v)])
    return [tuple(out[4 * p:4 * p + 4]) for p in range(n)]


def _adamw(w, g, m, v, name, after=()):
    rows, cols = w.shape
    br = _row_block(rows, cols)

    def body(*refs):
        w_ref, g_ref, m_ref, v_ref = refs[:4]
        d_ref, nm_ref, nv_ref = refs[4 + len(after):]
        d_ref[...], nm_ref[...], nv_ref[...] = _adam_math(w_ref[...], g_ref[...], m_ref[...], v_ref[...])

    blk = pl.BlockSpec((br, cols), lambda i: (i, 0))
    return pl.pallas_call(
        body, name=name, grid=(rows // br,),
        in_specs=[blk] * 4 + [_ANY] * len(after), out_specs=[blk] * 3,
        out_shape=[jax.ShapeDtypeStruct((rows, cols), F32)] * 3,
        compiler_params=_params(("arbitrary",)),
    )(w, g, m, v, *after)


def _all_gather(x_shard, name):
    m_per, n = x_shard.shape

    def body(x_ref, out_ref, send_sems, recv_sems, local_sem):
        x, y, c = _position()
        me, sibling = (x, y, c), (x, y, 1 - c)
        chips = [(1 - x, y), (x, 1 - y), (1 - x, 1 - y)]

        def rows(px, py, pc):
            return out_ref.at[pl.ds((4 * px + 2 * py + pc) * m_per, m_per), :]

        def copy(k, block, to, src=None):
            return pltpu.make_async_remote_copy(
                src_ref=rows(*block) if src is None else src, dst_ref=rows(*block),
                send_sem=send_sems.at[k], recv_sem=recv_sems.at[k], device_id=to, device_id_type=MESH)

        mine = pltpu.make_async_copy(x_ref, rows(*me), local_sem)
        mine.start()
        first = [copy(0, me, sibling, src=x_ref)]
        first += [copy(1 + j, me, (*chip, c), src=x_ref) for j, chip in enumerate(chips)]
        for cp in first:
            cp.start()
        passed = [copy(4 + j, (*chip, c), sibling) for j, chip in enumerate(chips)]
        for j, chip in enumerate(chips):
            copy(1 + j, (*chip, c), me).wait_recv()
            passed[j].start()
        copy(0, sibling, me).wait_recv()
        for j, chip in enumerate(chips):
            copy(4 + j, (*chip, 1 - c), me).wait_recv()
        for cp in first + passed:
            cp.wait_send()
        mine.wait()

    return pl.pallas_call(
        body, name=name,
        out_shape=jax.ShapeDtypeStruct((N_DEV * m_per, n), x_shard.dtype),
        in_specs=[pl.BlockSpec(memory_space=pltpu.VMEM)],
        out_specs=pl.BlockSpec(memory_space=pltpu.VMEM),
        scratch_shapes=[pltpu.SemaphoreType.DMA((7,)), pltpu.SemaphoreType.DMA((7,)), pltpu.SemaphoreType.DMA],
        compiler_params=pltpu.CompilerParams(vmem_limit_bytes=VMEM_LIMIT),
    )(x_shard)


_ANY = pl.BlockSpec(memory_space=pl.ANY)


def _place_quarters(place, quarters):
    steps = 2

    def body(place_ref, *refs):
        n = len(refs) // 2
        for w_ref, o_ref in zip(refs[:n], refs[n:]):
            o_ref[0] = w_ref[...].astype(BF16)

    return pl.pallas_call(
        body, name="place_quarters",
        grid_spec=pltpu.PrefetchScalarGridSpec(
            num_scalar_prefetch=1, grid=(steps,),
            in_specs=[pl.BlockSpec((q.shape[0] // steps, q.shape[1]), lambda r, place_ref: (r, 0)) for q in quarters],
            out_specs=[pl.BlockSpec((1, q.shape[0] // steps, q.shape[1]), lambda r, place_ref: (place_ref[0], r, 0))
                       for q in quarters]),
        out_shape=[jax.ShapeDtypeStruct((N_CHIPS,) + q.shape, BF16) for q in quarters],
        compiler_params=_params(("arbitrary",)),
    )(place, *quarters)


_HBM = pl.BlockSpec(memory_space=pltpu.HBM)
_SEM = pl.BlockSpec(memory_space=pltpu.SEMAPHORE)
_EFFECT = pltpu.SideEffectType.DATAFLOW_SIDE_EFFECTING


def _quarter_halves(shapes, a, which):
    hr = shapes[a][0] // 2
    return pl.ds(which * hr, hr)


def _gather_start(placed, after, tag):
    n = len(placed)
    m = len(after)
    shapes = [b.shape[1:] for b in placed]

    def body(*refs):
        g_refs = refs[:n]
        send_sems, recv_sems = refs[n + m], refs[n + m + 1]
        token = refs[2 * n + m + 2]
        x, y, c = _position()
        chips = [(1 - x, y), (x, 1 - y), (1 - x, 1 - y)]
        mine = 2 * x + y
        for a in range(n):
            ref = g_refs[a].at[mine, _quarter_halves(shapes, a, c), :]
            for p in range(3):
                pltpu.make_async_remote_copy(
                    src_ref=ref, dst_ref=ref, send_sem=send_sems.at[3 * a + p], recv_sem=recv_sems.at[3 * a + p],
                    device_id=(*chips[p], c), device_id_type=MESH).start()
        token[...] = jnp.zeros_like(token)

    out = pl.pallas_call(
        body, name="gather_start_" + tag,
        out_shape=(pltpu.SemaphoreType.DMA((3 * n,)), pltpu.SemaphoreType.DMA((3 * n,)),
                   *[pltpu.HBM(b.shape, b.dtype) for b in placed], jax.ShapeDtypeStruct((8, LANES), F32)),
        in_specs=[_HBM] * n + [_ANY] * m,
        out_specs=(_SEM, _SEM, *[_HBM] * n, pl.BlockSpec(memory_space=pltpu.VMEM)),
        input_output_aliases={a: 2 + a for a in range(n)},
        compiler_params=pltpu.CompilerParams(has_side_effects=_EFFECT),
    )(*[pltpu.with_memory_space_constraint(b, pltpu.HBM) for b in placed], *after)
    return out[0], out[1], list(out[2:2 + n]), out[2 + n]


def _gather_wait(send_sems, recv_sems, thru, after, tag):
    n = len(thru)
    shapes = [b.shape[1:] for b in thru]

    def body(*refs):
        g_refs = refs[:n]
        send_sems, recv_sems = refs[n], refs[n + 1]
        x, y, c = _position()
        chips = [(1 - x, y), (x, 1 - y), (1 - x, 1 - y)]
        mine = 2 * x + y
        for a in range(n):
            rows = _quarter_halves(shapes, a, c)
            for p, (cx, cy) in enumerate(chips):
                copy = pltpu.make_async_remote_copy(
                    src_ref=g_refs[a].at[mine, rows, :], dst_ref=g_refs[a].at[2 * cx + cy, rows, :],
                    send_sem=send_sems.at[3 * a + p], recv_sem=recv_sems.at[3 * a + p],
                    device_id=(cx, cy, c), device_id_type=MESH)
                copy.wait_send()
                copy.wait_recv()

    return pl.pallas_call(
        body, name="gather_wait_" + tag,
        out_shape=[pltpu.HBM(b.shape, b.dtype) for b in thru],
        in_specs=[_HBM] * n + [_SEM, _SEM, _ANY], out_specs=[_HBM] * n,
        input_output_aliases={a: a for a in range(n)},
        compiler_params=pltpu.CompilerParams(has_side_effects=_EFFECT),
    )(*thru, send_sems, recv_sems, after)


def _gather_forward(bufs, tag):
    n = len(bufs)
    shapes = [b.shape[1:] for b in bufs]

    def body(*refs):
        g_refs = refs[n:2 * n]
        send_sems, recv_sems = refs[2 * n:]
        x, y, c = _position()
        chips = [(1 - x, y), (x, 1 - y), (1 - x, 1 - y)]

        def over_d2d(a, p, which):
            cx, cy = chips[p]
            ref = g_refs[a].at[2 * cx + cy, _quarter_halves(shapes, a, which), :]
            return pltpu.make_async_remote_copy(
                src_ref=ref, dst_ref=ref, send_sem=send_sems.at[3 * a + p], recv_sem=recv_sems.at[3 * a + p],
                device_id=(x, y, 1 - c), device_id_type=MESH)

        sends = [over_d2d(a, p, c) for a in range(n) for p in range(3)]
        for cp in sends:
            cp.start()
        for a in range(n):
            for p in range(3):
                over_d2d(a, p, 1 - c).wait_recv()
        for cp in sends:
            cp.wait_send()

    return pl.pallas_call(
        body, name="gather_forward_" + tag,
        out_shape=[jax.ShapeDtypeStruct(b.shape, BF16) for b in bufs],
        in_specs=[_ANY] * n, out_specs=[_ANY] * n,
        input_output_aliases={a: a for a in range(n)},
        scratch_shapes=[pltpu.SemaphoreType.DMA((3 * n,)), pltpu.SemaphoreType.DMA((3 * n,))],
    )(*bufs)


_FLIPS = [(fx, fy, fc) for fx in (0, 1) for fy in (0, 1) for fc in (0, 1)][1:]


def _flipped(pos, flip):
    return tuple(1 - p if f else p for p, f in zip(pos, flip))


def _direct_gather_start(slots):
    def body(s_ref, send_sems, recv_sems, thru, token):
        me = _position()
        mine = s_ref.at[4 * me[0] + 2 * me[1] + me[2]]
        for r, flip in enumerate(_FLIPS):
            pltpu.make_async_remote_copy(
                src_ref=mine, dst_ref=mine, send_sem=send_sems.at[r], recv_sem=recv_sems.at[r],
                device_id=_flipped(me, flip), device_id_type=MESH).start()
        token[...] = jnp.zeros_like(token)

    return pl.pallas_call(
        body, name="small_gather_start",
        out_shape=(pltpu.SemaphoreType.DMA((len(_FLIPS),)), pltpu.SemaphoreType.DMA((len(_FLIPS),)),
                   pltpu.HBM(slots.shape, slots.dtype), jax.ShapeDtypeStruct((8, LANES), F32)),
        in_specs=[_HBM], out_specs=(_SEM, _SEM, _HBM, pl.BlockSpec(memory_space=pltpu.VMEM)),
        input_output_aliases={0: 2},
        compiler_params=pltpu.CompilerParams(has_side_effects=_EFFECT),
    )(pltpu.with_memory_space_constraint(slots, pltpu.HBM))


def _direct_gather_wait(send_sems, recv_sems, slots, after):
    def body(s_ref, send_sems, recv_sems, after_ref, out_ref):
        me = _position()
        mine = s_ref.at[4 * me[0] + 2 * me[1] + me[2]]
        for r, flip in enumerate(_FLIPS):
            peer = _flipped(me, flip)
            copy = pltpu.make_async_remote_copy(
                src_ref=mine, dst_ref=s_ref.at[4 * peer[0] + 2 * peer[1] + peer[2]],
                send_sem=send_sems.at[r], recv_sem=recv_sems.at[r], device_id=peer, device_id_type=MESH)
            copy.wait_send()
            copy.wait_recv()

    return pl.pallas_call(
        body, name="small_gather_wait",
        out_shape=pltpu.HBM(slots.shape, slots.dtype),
        in_specs=[_HBM, _SEM, _SEM, _ANY], out_specs=_HBM,
        input_output_aliases={0: 0},
        compiler_params=pltpu.CompilerParams(has_side_effects=_EFFECT),
    )(slots, send_sems, recv_sems, after)


def _sibling_split_start(bufs, parts, nparts, after, tag):
    n, m = len(bufs), len(after)

    def body(*refs):
        b_refs = refs[:n]
        send_sems, recv_sems = refs[n + m], refs[n + m + 1]
        token = refs[2 * n + m + 2]
        x, y, c = _position()
        for r, ref in enumerate(parts(b_refs, x, y, c)):
            pltpu.make_async_remote_copy(
                src_ref=ref, dst_ref=ref, send_sem=send_sems.at[r], recv_sem=recv_sems.at[r],
                device_id=(x, y, 1 - c), device_id_type=MESH).start()
        token[...] = jnp.zeros_like(token)

    out = pl.pallas_call(
        body, name="sibling_start_" + tag,
        out_shape=(pltpu.SemaphoreType.DMA((nparts,)), pltpu.SemaphoreType.DMA((nparts,)),
                   *[pltpu.HBM(b.shape, b.dtype) for b in bufs], jax.ShapeDtypeStruct((8, LANES), F32)),
        in_specs=[_HBM] * n + [_ANY] * m,
        out_specs=(_SEM, _SEM, *[_HBM] * n, pl.BlockSpec(memory_space=pltpu.VMEM)),
        input_output_aliases={a: 2 + a for a in range(n)},
        compiler_params=pltpu.CompilerParams(has_side_effects=_EFFECT),
    )(*[pltpu.with_memory_space_constraint(b, pltpu.HBM) for b in bufs], *after)
    return out[0], out[1], list(out[2:2 + n]), out[2 + n]


def _sibling_split_wait(send_sems, recv_sems, bufs, parts, after, tag):
    n = len(bufs)

    def body(*refs):
        b_refs = refs[:n]
        send_sems, recv_sems = refs[n], refs[n + 1]
        x, y, c = _position()
        mine, theirs = parts(b_refs, x, y, c), parts(b_refs, x, y, 1 - c)
        for r, (src, dst) in enumerate(zip(mine, theirs)):
            copy = pltpu.make_async_remote_copy(
                src_ref=src, dst_ref=dst, send_sem=send_sems.at[r], recv_sem=recv_sems.at[r],
                device_id=(x, y, 1 - c), device_id_type=MESH)
            copy.wait_send()
            copy.wait_recv()

    return pl.pallas_call(
        body, name="sibling_wait_" + tag,
        out_shape=[pltpu.HBM(b.shape, b.dtype) for b in bufs],
        in_specs=[_HBM] * n + [_SEM, _SEM, _ANY], out_specs=[_HBM] * n,
        input_output_aliases={a: a for a in range(n)},
        compiler_params=pltpu.CompilerParams(has_side_effects=_EFFECT),
    )(*bufs, send_sems, recv_sems, after)


def _sibling_exchange(grads, tag):
    n = len(grads)
    shapes = [g.shape for g in grads]

    def body(*refs):
        g_refs, x_refs = refs[:n], refs[n:2 * n]
        send_sems, recv_sems = refs[2 * n:]
        x, y, c = _position()
        copies = []
        for a in range(n):
            hr = shapes[a][1] // 2
            cp = pltpu.make_async_remote_copy(
                src_ref=g_refs[a].at[:, pl.ds((1 - c) * hr, hr), :], dst_ref=x_refs[a],
                send_sem=send_sems.at[a], recv_sem=recv_sems.at[a],
                device_id=(x, y, 1 - c), device_id_type=MESH)
            cp.start()
            copies.append(cp)
        for cp in copies:
            cp.wait()

    return pl.pallas_call(
        body, name="grad_sibling_exchange_" + tag,
        out_shape=[jax.ShapeDtypeStruct((g.shape[0], g.shape[1] // 2, g.shape[2]), g.dtype) for g in grads],
        in_specs=[_ANY] * n, out_specs=[_ANY] * n,
        scratch_shapes=[pltpu.SemaphoreType.DMA((n,)), pltpu.SemaphoreType.DMA((n,))],
    )(*grads)


def _sibling_exchange_start(grads, tag):
    n = len(grads)
    lands = [lax.empty((g.shape[0], g.shape[1] // 2, g.shape[2]), g.dtype) for g in grads]

    def body(*refs):
        g_refs, x_refs = refs[:n], refs[n:2 * n]
        send_sems, recv_sems = refs[2 * n], refs[2 * n + 1]
        token = refs[4 * n + 2]
        x, y, c = _position()
        for a in range(n):
            hr = grads[a].shape[1] // 2
            pltpu.make_async_remote_copy(
                src_ref=g_refs[a].at[:, pl.ds((1 - c) * hr, hr), :], dst_ref=x_refs[a],
                send_sem=send_sems.at[a], recv_sem=recv_sems.at[a],
                device_id=(x, y, 1 - c), device_id_type=MESH).start()
        token[...] = jnp.zeros_like(token)

    both = list(grads) + lands
    out = pl.pallas_call(
        body, name="grad_sibling_exchange_start_" + tag,
        out_shape=(pltpu.SemaphoreType.DMA((n,)), pltpu.SemaphoreType.DMA((n,)),
                   *[pltpu.HBM(b.shape, b.dtype) for b in both], jax.ShapeDtypeStruct((8, LANES), F32)),
        in_specs=[_HBM] * (2 * n),
        out_specs=(_SEM, _SEM, *[_HBM] * (2 * n), pl.BlockSpec(memory_space=pltpu.VMEM)),
        input_output_aliases={a: 2 + a for a in range(2 * n)},
        compiler_params=pltpu.CompilerParams(has_side_effects=_EFFECT),
    )(*[pltpu.with_memory_space_constraint(b, pltpu.HBM) for b in both])
    return out[0], out[1], list(out[2:2 + n]), list(out[2 + n:2 + 2 * n]), out[2 + 2 * n]


def _sibling_exchange_wait(send_sems, recv_sems, grads, lands, after, tag):
    n = len(grads)

    def body(*refs):
        g_refs, x_refs = refs[:n], refs[n:2 * n]
        send_sems, recv_sems = refs[2 * n], refs[2 * n + 1]
        x, y, c = _position()
        for a in range(n):
            hr = grads[a].shape[1] // 2
            copy = pltpu.make_async_remote_copy(
                src_ref=g_refs[a].at[:, pl.ds((1 - c) * hr, hr), :], dst_ref=x_refs[a],
                send_sem=send_sems.at[a], recv_sem=recv_sems.at[a],
                device_id=(x, y, 1 - c), device_id_type=MESH)
            copy.wait_send()
            copy.wait_recv()

    both = list(grads) + list(lands)
    out = pl.pallas_call(
        body, name="grad_sibling_exchange_wait_" + tag,
        out_shape=[pltpu.HBM(b.shape, b.dtype) for b in both],
        in_specs=[_HBM] * (2 * n) + [_SEM, _SEM, _ANY], out_specs=[_HBM] * (2 * n),
        input_output_aliases={a: a for a in range(2 * n)},
        compiler_params=pltpu.CompilerParams(has_side_effects=_EFFECT),
    )(*both, send_sems, recv_sems, after)
    return list(out[n:])


def _chip_sums(core, grads, theirs, tag):
    n = len(grads)

    def body(core_ref, *refs):
        g_refs, t_refs, o_refs = refs[:n], refs[n:2 * n], refs[2 * n:]
        for g_ref, t_ref, o_ref in zip(g_refs, t_refs, o_refs):
            o_ref[...] = (g_ref[...] + t_ref[...].astype(F32)).astype(BF16)

    in_specs = [pl.BlockSpec((1, g.shape[1] // 2, g.shape[2]), lambda k, core_ref: (k, core_ref[0], 0)) for g in grads]
    in_specs += [pl.BlockSpec((1,) + t.shape[1:], lambda k, core_ref: (k, 0, 0)) for t in theirs]
    return pl.pallas_call(
        body, name="grad_chip_sums_" + tag,
        grid_spec=pltpu.PrefetchScalarGridSpec(
            num_scalar_prefetch=1, grid=(N_CHIPS,), in_specs=in_specs,
            out_specs=[pl.BlockSpec((1,) + t.shape[1:], lambda k, core_ref: (k, 0, 0)) for t in theirs]),
        out_shape=[jax.ShapeDtypeStruct(t.shape, BF16) for t in theirs],
        compiler_params=_params(("arbitrary",)),
    )(core, *grads, *theirs)


def _chip_exchange_start(sums, after, tag):
    n = len(sums)
    m = len(after)
    lands = [lax.empty((3,) + s.shape[1:], BF16) for s in sums]

    def body(*refs):
        s_refs, y_refs = refs[:n], refs[n:2 * n]
        send_sems, recv_sems = refs[2 * n + m], refs[2 * n + m + 1]
        token = refs[4 * n + m + 2]
        x, y, c = _position()
        chips = [(1 - x, y), (x, 1 - y), (1 - x, 1 - y)]
        for a in range(n):
            for p, (cx, cy) in enumerate(chips):
                pltpu.make_async_remote_copy(
                    src_ref=s_refs[a].at[2 * cx + cy], dst_ref=y_refs[a].at[p],
                    send_sem=send_sems.at[3 * a + p], recv_sem=recv_sems.at[3 * a + p],
                    device_id=(cx, cy, c), device_id_type=MESH).start()
        token[...] = jnp.zeros_like(token)

    both = list(sums) + lands
    out = pl.pallas_call(
        body, name="grad_chip_exchange_start_" + tag,
        out_shape=(pltpu.SemaphoreType.DMA((3 * n,)), pltpu.SemaphoreType.DMA((3 * n,)),
                   *[pltpu.HBM(b.shape, b.dtype) for b in both], jax.ShapeDtypeStruct((8, LANES), F32)),
        in_specs=[_HBM] * (2 * n) + [_ANY] * m,
        out_specs=(_SEM, _SEM, *[_HBM] * (2 * n), pl.BlockSpec(memory_space=pltpu.VMEM)),
        input_output_aliases={a: 2 + a for a in range(2 * n)},
        compiler_params=pltpu.CompilerParams(has_side_effects=_EFFECT),
    )(*[pltpu.with_memory_space_constraint(b, pltpu.HBM) for b in both], *after)
    return out[0], out[1], list(out[2:2 + n]), list(out[2 + n:2 + 2 * n]), out[2 + 2 * n]


def _chip_exchange_wait(send_sems, recv_sems, sums, lands, after, tag):
    n = len(sums)

    def body(*refs):
        s_refs, y_refs = refs[:n], refs[n:2 * n]
        send_sems, recv_sems = refs[2 * n], refs[2 * n + 1]
        x, y, c = _position()
        chips = [(1 - x, y), (x, 1 - y), (1 - x, 1 - y)]
        for a in range(n):
            for p, (cx, cy) in enumerate(chips):
                copy = pltpu.make_async_remote_copy(
                    src_ref=s_refs[a].at[2 * cx + cy], dst_ref=y_refs[a].at[p],
                    send_sem=send_sems.at[3 * a + p], recv_sem=recv_sems.at[3 * a + p],
                    device_id=(cx, cy, c), device_id_type=MESH)
                copy.wait_send()
                copy.wait_recv()

    both = list(sums) + list(lands)
    out = pl.pallas_call(
        body, name="grad_chip_exchange_wait_" + tag,
        out_shape=[pltpu.HBM(b.shape, b.dtype) for b in both],
        in_specs=[_HBM] * (2 * n) + [_SEM, _SEM, _ANY], out_specs=[_HBM] * (2 * n),
        input_output_aliases={a: a for a in range(2 * n)},
        compiler_params=pltpu.CompilerParams(has_side_effects=_EFFECT),
    )(*both, send_sems, recv_sems, after)
    return list(out[:n]), list(out[n:])


def _total_sums(place, sums, parts, after, tag):
    n = len(parts)
    m = len(after)
    steps = 2

    def body(place_ref, *refs):
        for s_ref, y_ref, o_ref in zip(refs[:n], refs[n:2 * n], refs[2 * n + m:]):
            o_ref[0] = ((s_ref[0].astype(F32) + y_ref[0].astype(F32)) + y_ref[1].astype(F32)) + y_ref[2].astype(F32)

    def step_rows(pt):
        return pt.shape[1] // steps

    in_specs = [pl.BlockSpec((1, step_rows(s), s.shape[2]), lambda r, place_ref: (place_ref[0], r, 0)) for s in sums]
    in_specs += [pl.BlockSpec((3, step_rows(pt), pt.shape[2]), lambda r, place_ref: (0, r, 0)) for pt in parts]
    in_specs += [_ANY] * m
    return pl.pallas_call(
        body, name="grad_total_sums_" + tag,
        grid_spec=pltpu.PrefetchScalarGridSpec(
            num_scalar_prefetch=1, grid=(steps,), in_specs=in_specs,
            out_specs=[pl.BlockSpec((1, step_rows(pt), pt.shape[2]), lambda r, place_ref: (place_ref[1], r, 0))
                       for pt in parts]),
        out_shape=[jax.ShapeDtypeStruct((2,) + pt.shape[1:], F32) for pt in parts],
        compiler_params=_params(("arbitrary",)),
    )(place, *sums, *parts, *after)


def _sibling_share(halves, tag):
    n = len(halves)

    def body(*refs):
        f_refs = refs[n:2 * n]
        send_sems, recv_sems = refs[2 * n:]
        x, y, c = _position()
        copies = []
        for a in range(n):
            cp = pltpu.make_async_remote_copy(
                src_ref=f_refs[a].at[c], dst_ref=f_refs[a].at[c], send_sem=send_sems.at[a], recv_sem=recv_sems.at[a],
                device_id=(x, y, 1 - c), device_id_type=MESH)
            cp.start()
            copies.append(cp)
        for a, cp in enumerate(copies):
            cp.wait_send()
            pltpu.make_async_remote_copy(
                src_ref=f_refs[a].at[1 - c], dst_ref=f_refs[a].at[1 - c], send_sem=send_sems.at[a],
                recv_sem=recv_sems.at[a], device_id=(x, y, c), device_id_type=MESH).wait_recv()

    return pl.pallas_call(
        body, name="grad_sibling_share_" + tag,
        out_shape=[jax.ShapeDtypeStruct(h.shape, F32) for h in halves],
        in_specs=[_ANY] * n, out_specs=[_ANY] * n,
        input_output_aliases={a: a for a in range(n)},
        scratch_shapes=[pltpu.SemaphoreType.DMA((n,)), pltpu.SemaphoreType.DMA((n,))],
    )(*halves)


def _group_sum(stacked, nrow, name):
    total, n = stacked.shape
    groups = total // nrow

    def body(g_ref, o_ref):
        acc = g_ref[0:nrow, :]
        for grp in range(1, groups):
            acc = acc + g_ref[grp * nrow:(grp + 1) * nrow, :]
        o_ref[...] = acc

    return pl.pallas_call(
        body, name=name,
        out_shape=jax.ShapeDtypeStruct((nrow, n), F32),
        compiler_params=pltpu.CompilerParams(vmem_limit_bytes=VMEM_LIMIT),
    )(stacked)


def _local_step(xt, tgt, mod, gains, w_pool, pool_scale, w_in, later_weights, on_ffn_grads, after_mixer_bwd,
                on_small_grads, seq):
    g_mpre, g_mpost, g_fpre, g_fpost = gains
    d = xt.shape[1]
    tm, tq = min(TOKEN_TILE, seq), min(ATTN_TILE, seq)

    h1, qn, k, v, u, kt, vt = _prenorm_proj(xt, mod, g_mpre, w_in, seq, tm)
    tk = min(ATTN_KEY_TILE, tq // 2)
    o, ltot = _attn_fwd(qn, k, vt, seq, tq, tk)
    w_out, order, ffn_weights = later_weights(o)
    w_out2 = w_out.reshape(d, d)
    pooled, mixin, mix, x1, h2 =_mixer_post(u, o, xt, mod, g_mpost, g_fpre + order, w_pool, pool_scale, w_out2, seq, tm)
    w_g, w_u, w_d = ffn_weights(h2)
    a, b, fin, dy, df, loss_blk, accb4, accg4 = _ffn_fwd(h2, w_g, w_u, w_d, x1, tgt, mod, g_fpost, seq, tm)
    da, db, dx1, dmix, accb5, accg5 = _ffn_bwd(df, a, b, w_d, w_g, w_u, x1, dy, mix, mod, g_fpre, g_mpost, seq, tm)
    bt = min(GRAD_TOKEN_TILE, xt.shape[0])
    bt_one = min(2 * GRAD_TOKEN_TILE, xt.shape[0])
    (g_g,), (g_g16,) = _tn_matmul(da, [h2], w_g.shape[0], bt_one, "grad_w_gate")
    (g_u,), (g_u16,) = _tn_matmul(db, [h2], w_u.shape[0], bt_one, "grad_w_up")
    (g_d,), (g_d16,) = _tn_matmul(fin, [df], w_d.shape[0], bt_one, "grad_w_down")
    token = on_ffn_grads([g_g, g_u, g_d], [g_g16, g_u16, g_d16])
    do, dpd, dps, dwp = _mixer_bwd(dmix, w_out2, pooled, w_pool, pool_scale + token, seq, tm)
    order = after_mixer_bwd(do)
    dq, dk, dv = _attn_bwd(qn, k, kt, v, do, ltot, seq, tq, tk, order)
    gx, du, accb8, accg8 = _inproj_bwd(dq, dk, dv, dpd, xt, dx1, mod, g_mpre, w_in, seq, tm)

    dmod = jnp.stack([accb8[:, 0], accb8[:, 1], accb5[:, 2], accb5[:, 0], accb5[:, 1], accb4[:, 0]], axis=1)
    dgain = jnp.stack([accg8[0], accg5[1], accg5[0], accg4[0]], axis=0)
    behind = on_small_grads(loss_blk, dmod, dgain, dps[0:1], dwp)
    g_in, g_in16 = _tn_matmul_stacked(h1, [dq, dk, dv, du], bt, "grad_w_in", behind)
    g_out, g_out16 = [parts[0].reshape(w_out.shape)
                      for parts in _tn_matmul(mixin, [dmix], 1, bt_one, "grad_w_out", behind)]
    grads = [g_in, g_out, g_g, g_u, g_d]
    grads16 = [g_in16, g_out16, g_g16, g_u16, g_d16]
    return gx, grads, grads16


def kernel(x, c, w_cond, b_cond, g_mix_pre, g_mix_post, w_in, w_pool, pool_scale, w_out, g_ffn_pre, g_ffn_post, w_gate, w_up, w_down, loss_target, m_w_cond, m_b_cond, m_g_mix_pre, m_g_mix_post, m_w_in, m_w_pool, m_pool_scale, m_w_out, m_g_ffn_pre, m_g_ffn_post, m_w_gate, m_w_up, m_w_down, v_w_cond, v_b_cond, v_g_mix_pre, v_g_mix_post, v_w_in, v_w_pool, v_pool_scale, v_w_out, v_g_ffn_pre, v_g_ffn_post, v_w_gate, v_w_up, v_w_down):
    xi, yi, ci = _position()
    chip = 2 * xi + yi
    dev = 4 * xi + 2 * yi + ci
    nb, seq, d = x.shape
    t_all = nb * seq
    xt = x.reshape(t_all, d)
    tgt = loss_target.reshape(t_all, d)
    ncol = w_cond.shape[2]
    pw = pool_scale.shape[1]

    place = jnp.stack([chip, ci]).astype(jnp.int32)
    turned = lambda t: jnp.swapaxes(t[0], 0, 1)
    placed = _place_quarters(place, [w_in[0], w_out[0], turned(w_gate), turned(w_up), w_down[0]])
    in_sems = _gather_start(placed[:1], [], "in")

    c_pad = jnp.concatenate([c, jnp.zeros((8 - nb, d), F32)], axis=0) + in_sems[3][0:1, 0:1]
    c_all = _all_gather(c_pad, "gather_c").reshape(N_DEV, 8, d)[:, :nb].reshape(N_DEV * nb, d)
    b_q = lax.dynamic_slice(b_cond, (0, chip * ncol), (1, ncol))
    sc_all, mod_q = _cond_fwd(c_all, w_cond[0], b_q, 512)
    mod_parts = _all_gather(mod_q, "gather_mod").reshape(N_DEV, N_DEV * nb, ncol)
    mod_rows = lax.dynamic_slice(mod_parts, (0, dev * nb, 0), (N_DEV, nb, ncol))[0::2]
    mod = jnp.transpose(mod_rows, (1, 0, 2)).reshape(nb, N_MOD, d)
    mod = jnp.concatenate([mod, jnp.zeros((nb, MOD_ROWS - N_MOD, d), F32)], axis=1)

    (w_in_all,) = _gather_forward(_gather_wait(*in_sems[:3], mod, "in"), "in")
    send_sems, recv_sems, in_flight, token = _gather_start(placed[1:], [mod, w_in_all], "rest")
    mod = mod + token[0:1, 0:1]

    def later_weights(after):
        waited = _gather_wait(send_sems, recv_sems, in_flight, after, "rest")
        (w_out_all,) = _gather_forward(waited[:1], "out")
        shapes = [b.shape[1:] for b in waited[1:]]

        def parts(refs, px, py, which):
            return [refs[a].at[2 * cx + cy, _quarter_halves(shapes, a, which), :]
                    for a in range(len(refs)) for cx, cy in [(1 - px, py), (px, 1 - py), (1 - px, 1 - py)]]

        forward = _sibling_split_start(waited[1:], parts, 3 * len(shapes), [w_out_all], "ffn_weights")
        finish = lambda after2: _sibling_split_wait(*forward[:3], parts, after2, "ffn_weights")
        return w_out_all, forward[3][0:1, 0:1], finish

    ffn_split = []

    ffn_sibling = []

    def on_ffn_grads(ffn_grads, ffn_grads16):
        ffn_sibling.extend(_sibling_exchange_start(ffn_grads16, "ffn"))
        ffn_sibling.append(ffn_grads)
        return ffn_sibling[4][0:1, 0:1]

    def after_mixer_bwd(do):
        theirs = _sibling_exchange_wait(*ffn_sibling[:4], do, "ffn")
        ffn_split.extend(_chip_exchange_start(_chip_sums(place[1:], ffn_sibling[5], theirs, "ffn"), [], "ffn"))
        return ffn_split[4][0:1, 0:1]

    wp_rows = w_pool[0].size // d
    loss_row = 2 * N_MOD + 4 + 1
    pad_rows = 24 - (loss_row + 1)
    prow = 24 + wp_rows
    small_split = []

    def on_small_grads(loss_blk, dmod, dgain, dps, dwp):
        payload = jnp.concatenate([
            dmod.reshape(nb * N_MOD, d), dgain,
            jnp.concatenate([dps, jnp.zeros((1, d - pw), F32)], axis=1),
            jnp.concatenate([loss_blk[0:1], jnp.zeros((1, d - LANES), F32)], axis=1),
            jnp.zeros((pad_rows, d), F32),
            jnp.concatenate(jnp.split(dwp.reshape(-1, dwp.shape[-1]), d // dwp.shape[-1], axis=0), axis=1)], axis=0)
        slots = lax.dynamic_update_slice(lax.empty((N_DEV, prow, d), F32), payload[None], (dev, 0, 0))
        small_split.extend(_direct_gather_start(slots))
        return [small_split[3]]

    gains = (g_mix_pre, g_mix_post, g_ffn_pre, g_ffn_post)
    gx, grads, grads16 = _local_step(
        xt, tgt, mod, gains, w_pool[0], pool_scale, w_in_all, later_weights, on_ffn_grads, after_mixer_bwd,
        on_small_grads, seq)

    sums_ffn, parts_ffn = _chip_exchange_wait(*ffn_split[:4], gx, "ffn")
    gathered = _direct_gather_wait(*small_split[:3], grads16[1]).reshape(N_DEV * prow, d)
    summed = _group_sum(gathered, prow, "small_device_sum")
    loss = summed[loss_row, 0]
    dmod_all = gathered.reshape(N_DEV, prow, d)[:, :nb * N_MOD].reshape(N_DEV * nb, N_MOD * d)
    dmod_q = lax.dynamic_slice(dmod_all, (0, chip * ncol), (N_DEV * nb, ncol))
    g_w_cond = _cond_bwd(sc_all, dmod_q, 512)
    first_gain = 2 * N_MOD

    theirs = _sibling_exchange(grads16[:2], "mix")
    mix_split = _chip_exchange_start(_chip_sums(place[1:], grads[:2], theirs, "mix"), [gathered], "mix")
    unfold = lambda halves: [g.reshape(2 * g.shape[1], g.shape[2]) for g in halves]
    share_parts = lambda refs, px, py, which: [r.at[which] for r in refs]
    halves_ffn = _total_sums(place, sums_ffn, parts_ffn, [mix_split[4]], "ffn")
    share = _sibling_split_start(halves_ffn, share_parts, len(halves_ffn), [], "share_ffn")

    results = {}

    def update(name, w2, g2, m2, v2, shape, after=()):
        delta, new_m, new_v = _adamw(w2, g2, m2, v2, "adamw_" + name, after)
        back = (lambda t: jnp.swapaxes(t, 0, 1)[None]) if shape is None else (lambda t: t.reshape(shape))
        results[name] = [back(t) for t in (g2, delta, new_m, new_v)]
        return delta

    done_cond = update("w_cond", w_cond[0], g_w_cond, m_w_cond[0], v_w_cond[0], w_cond.shape, [share[3]])
    g_ffn = unfold(_sibling_split_wait(*share[:3], share_parts, done_cond, "share_ffn"))
    done = [update("w_gate", turned(w_gate), g_ffn[0], turned(m_w_gate), turned(v_w_gate), None),
            update("w_up", turned(w_up), g_ffn[1], turned(m_w_up), turned(v_w_up), None),
            update("w_down", w_down[0], g_ffn[2], m_w_down[0], v_w_down[0], w_down.shape)]

    gain_row = lambda r: (lambda s: s[first_gain + r:first_gain + r + 1, :])
    small = [
        ("b_cond", (b_cond, m_b_cond, v_b_cond), (N_MOD, d), lambda s: s[0:N_MOD, :] + s[N_MOD:2 * N_MOD, :]),
        ("g_mix_pre", (g_mix_pre, m_g_mix_pre, v_g_mix_pre), (1, d), gain_row(0)),
        ("g_mix_post", (g_mix_post, m_g_mix_post, v_g_mix_post), (1, d), gain_row(1)),
        ("g_ffn_pre", (g_ffn_pre, m_g_ffn_pre, v_g_ffn_pre), (1, d), gain_row(2)),
        ("g_ffn_post", (g_ffn_post, m_g_ffn_post, v_g_ffn_post), (1, d), gain_row(3)),
        ("pool_scale", (pool_scale, m_pool_scale, v_pool_scale), (1, pw),
         lambda s: s[first_gain + 4:first_gain + 5, 0:pw]),
        ("w_pool", (w_pool, m_w_pool, v_w_pool), (wp_rows * d // w_pool.shape[-1], w_pool.shape[-1]),
         lambda s: jnp.concatenate([s[24:24 + wp_rows, j * w_pool.shape[-1]:(j + 1) * w_pool.shape[-1]]
                                    for j in range(d // w_pool.shape[-1])], axis=0)),
    ]
    updated = _small_updates(summed, [tuple(t.reshape(flat) for t in wmv) + (pick,) for _, wmv, flat, pick in small])
    for (name, wmv, _, _), quad in zip(small, updated):
        results[name] = [t.reshape(wmv[0].shape) for t in quad]

    sums_mix, parts_mix = _chip_exchange_wait(*mix_split[:4], done[-1], "mix")
    g_mix = unfold(_sibling_share(_total_sums(place, sums_mix, parts_mix, done, "mix"), "mix"))
    update("w_in", w_in[0], g_mix[0], m_w_in[0], v_w_in[0], w_in.shape)
    update("w_out", w_out[0], g_mix[1], m_w_out[0], v_w_out[0], w_out.shape)

    names = ("w_cond", "b_cond", "g_mix_pre", "g_mix_post", "w_in", "w_pool", "pool_scale", "w_out",
             "g_ffn_pre", "g_ffn_post", "w_gate", "w_up", "w_down")
    outs = [results[name][part] for part in range(4) for name in names]
    return (loss, gx.reshape(x.shape), *outs)
```
